```python
import math
import jax, jax.numpy as jnp
from jax import lax
import numpy as np

D_MODEL = 1024
BATCH = 8
SEQ = 8192
DEPTH = 1

MEM_LEN = 256
EPS = 1e-6
CONV_K = 4
GDN_HEADS = 8
GDN_DK = 128
GDN_DV = 128
GDN_DIM = GDN_HEADS * GDN_DV
GDN_CHUNK = 64
GDN_COLS = 4 * GDN_DIM + 2 * GDN_HEADS
SSM_DIM = D_MODEL
SSM_HEADDIM = 64
SSM_HEADS = SSM_DIM // SSM_HEADDIM
SSM_GROUPS = 2
SSM_HPG = SSM_HEADS // SSM_GROUPS
SSM_STATE = 128
SSM_CHUNK = 128
SSM_BC = SSM_GROUPS * SSM_STATE
SSM_COLS = 2 * SSM_DIM + 2 * SSM_BC + SSM_HEADS
MIX_DIM = GDN_DIM + SSM_DIM
IN_COLS = GDN_COLS + SSM_COLS
MEM_HEADS = 4
MEM_HEADDIM = D_MODEL // MEM_HEADS
D_FF = 4 * D_MODEL

kernel_name = "hybrid_gdn_ssd_parallel_heads_memxattn"


def rmsnorm(x, w):
    xf = x.astype(jnp.float32)
    y = xf * lax.rsqrt(jnp.mean(xf * xf, axis=-1, keepdims=True) + EPS)
    return (y * w.astype(jnp.float32)).astype(x.dtype)


def l2norm(x):
    return x * lax.rsqrt(jnp.sum(x * x, axis=-1, keepdims=True) + EPS)


def causal_depthwise_conv(x, w, b=None):
    K, C = w.shape
    xp = jnp.pad(x, ((0, 0), (K - 1, 0), (0, 0)))
    y = lax.conv_general_dilated(xp, w[:, None, :].astype(x.dtype), window_strides=(1,),
                                 padding='VALID', dimension_numbers=('NWC', 'WIO', 'NWC'),
                                 feature_group_count=C)
    if b is not None:
        y = y + b.astype(x.dtype)
    return y


def gated_delta_rule_chunked(q, k, v, g, beta):
    Bsz, T, H, Dk = q.shape
    Dv = v.shape[-1]
    C = GDN_CHUNK
    N = T // C
    q = l2norm(q) * (Dk ** -0.5)
    k = l2norm(k)

    def chunks(t):
        return t.reshape(Bsz, N, C, H, -1).transpose(0, 3, 1, 2, 4)

    q, k, v = chunks(q), chunks(k), chunks(v)
    g = g.reshape(Bsz, N, C, H).transpose(0, 3, 1, 2)
    beta = beta.reshape(Bsz, N, C, H).transpose(0, 3, 1, 2)
    g_cs = jnp.cumsum(g, axis=-1)
    causal = jnp.tril(jnp.ones((C, C), dtype=bool))
    strict = jnp.tril(jnp.ones((C, C), dtype=bool), -1)
    decay = jnp.exp(jnp.where(causal, g_cs[..., :, None] - g_cs[..., None, :], -jnp.inf))
    k_beta = k * beta[..., None]
    v_beta = v * beta[..., None]
    Lmat = jnp.where(strict, jnp.einsum('bhncd,bhnsd->bhncs', k_beta, k) * decay, 0.0)
    Amat = Lmat + jnp.eye(C, dtype=jnp.float32)
    rhs = jnp.concatenate([v_beta, k_beta * jnp.exp(g_cs)[..., None]], axis=-1)
    sol = lax.linalg.triangular_solve(Amat, rhs, left_side=True, lower=True, unit_diagonal=True)
    u = sol[..., :Dv]
    w = sol[..., Dv:]
    attn_intra = jnp.einsum('bhncd,bhnsd->bhncs', q, k) * decay
    q_dec = q * jnp.exp(g_cs)[..., None]
    k_dec = k * jnp.exp(g_cs[..., -1:] - g_cs)[..., None]
    chunk_decay = jnp.exp(g_cs[..., -1])

    def step(S, inp):
        qd, kd, u_c, w_c, a_c, cd = inp
        v_new = u_c - jnp.einsum('bhcd,bhde->bhce', w_c, S)
        o = jnp.einsum('bhcd,bhde->bhce', qd, S) + jnp.einsum('bhcs,bhse->bhce', a_c, v_new)
        S = S * cd[..., None, None] + jnp.einsum('bhcd,bhce->bhde', kd, v_new)
        return S, o

    mv = lambda t: jnp.moveaxis(t, 2, 0)
    S0 = jnp.zeros((Bsz, H, Dk, Dv), jnp.float32)
    _, o = lax.scan(step, S0, (mv(q_dec), mv(k_dec), mv(u), mv(w), mv(attn_intra), mv(chunk_decay)))
    return o.transpose(1, 0, 3, 2, 4).reshape(Bsz, T, H, Dv)


def gdn_mixer(p, conv_w, a_log, dt_bias, norm_w):
    Bsz, T, _ = p.shape
    qkv = jax.nn.silu(causal_depthwise_conv(p[..., :3 * GDN_DIM], conv_w))
    z = p[..., 3 * GDN_DIM:4 * GDN_DIM]
    b_raw = p[..., 4 * GDN_DIM:4 * GDN_DIM + GDN_HEADS].astype(jnp.float32)
    a_raw = p[..., 4 * GDN_DIM + GDN_HEADS:].astype(jnp.float32)
    qkv = qkv.astype(jnp.float32)
    q = qkv[..., :GDN_DIM].reshape(Bsz, T, GDN_HEADS, GDN_DK)
    k = qkv[..., GDN_DIM:2 * GDN_DIM].reshape(Bsz, T, GDN_HEADS, GDN_DK)
    v = qkv[..., 2 * GDN_DIM:].reshape(Bsz, T, GDN_HEADS, GDN_DV)
    beta = jax.nn.sigmoid(b_raw)
    g = -jnp.exp(a_log.astype(jnp.float32)) * jax.nn.softplus(a_raw + dt_bias.astype(jnp.float32))
    o = gated_delta_rule_chunked(q, k, v, g, beta)
    zf = z.astype(jnp.float32).reshape(Bsz, T, GDN_HEADS, GDN_DV)
    o = rmsnorm(o, norm_w) * jax.nn.silu(zf)
    return o.reshape(Bsz, T, GDN_DIM).astype(p.dtype)


def ssd_chunked(x, dt, A, Bm, Cm):
    Bsz, T, H, P = x.shape
    L = SSD_L = SSM_CHUNK
    Nc = T // L
    G, J, N = SSM_GROUPS, SSM_HPG, SSM_STATE
    xdt = (x * dt[..., None]).reshape(Bsz, Nc, L, G, J, P)
    a = (dt * A).reshape(Bsz, Nc, L, H).transpose(0, 3, 1, 2)
    Bc = Bm.reshape(Bsz, Nc, L, G, N)
    Cc = Cm.reshape(Bsz, Nc, L, G, N)
    a_cs = jnp.cumsum(a, axis=-1)
    causal = jnp.tril(jnp.ones((L, L), dtype=bool))
    seg = jnp.exp(jnp.where(causal, a_cs[..., :, None] - a_cs[..., None, :], -jnp.inf))
    CB = jnp.einsum('bclgn,bcsgn->bgcls', Cc, Bc)
    scores = seg.reshape(Bsz, G, J, Nc, L, L) * CB[:, :, None]
    y_diag = jnp.einsum('bgjcls,bcsgjp->bclgjp', scores, xdt)
    decay_states = jnp.exp(a_cs[..., -1:] - a_cs).reshape(Bsz, G, J, Nc, L)
    states = jnp.einsum('bclgn,bgjcl,bclgjp->bcgjpn', Bc, decay_states, xdt)
    chunk_decay = jnp.exp(a_cs[..., -1]).reshape(Bsz, G, J, Nc)

    def step(h, inp):
        st, dec = inp
        return h * dec[..., None, None] + st, h

    h0 = jnp.zeros((Bsz, G, J, P, N), jnp.float32)
    _, prev = lax.scan(step, h0, (jnp.moveaxis(states, 1, 0), jnp.moveaxis(chunk_decay, 3, 0)))
    prev = jnp.moveaxis(prev, 0, 1)
    y_off = jnp.einsum('bclgn,bcgjpn,bgjcl->bclgjp', Cc, prev,
                       jnp.exp(a_cs).reshape(Bsz, G, J, Nc, L))
    return (y_diag + y_off).reshape(Bsz, T, H, P)


def ssd_mixer(p, conv_w, conv_b, a_log, dt_bias, d_skip, norm_w):
    Bsz, T, _ = p.shape
    z = p[..., :SSM_DIM]
    xbc = jax.nn.silu(causal_depthwise_conv(p[..., SSM_DIM:2 * SSM_DIM + 2 * SSM_BC], conv_w, conv_b))
    dt_raw = p[..., 2 * SSM_DIM + 2 * SSM_BC:].astype(jnp.float32)
    xbc = xbc.astype(jnp.float32)
    xs = xbc[..., :SSM_DIM].reshape(Bsz, T, SSM_HEADS, SSM_HEADDIM)
    Bm = xbc[..., SSM_DIM:SSM_DIM + SSM_BC].reshape(Bsz, T, SSM_GROUPS, SSM_STATE)
    Cm = xbc[..., SSM_DIM + SSM_BC:].reshape(Bsz, T, SSM_GROUPS, SSM_STATE)
    dt = jax.nn.softplus(dt_raw + dt_bias.astype(jnp.float32))
    A = -jnp.exp(a_log.astype(jnp.float32))
    y = ssd_chunked(xs, dt, A, Bm, Cm) + xs * d_skip.astype(jnp.float32)[:, None]
    yg = (y.reshape(Bsz, T, SSM_DIM) * jax.nn.silu(z.astype(jnp.float32)))
    yg = yg.reshape(Bsz, T, SSM_GROUPS, SSM_DIM // SSM_GROUPS)
    yg = yg * lax.rsqrt(jnp.mean(yg * yg, axis=-1, keepdims=True) + EPS)
    yg = yg.reshape(Bsz, T, SSM_DIM) * norm_w.astype(jnp.float32)
    return yg.astype(p.dtype)


def memory_cross_attention(h, m, wq, wk, wv, wo):
    Bsz, T, _ = h.shape
    M = m.shape[1]
    q = (h @ wq).reshape(Bsz, T, MEM_HEADS, MEM_HEADDIM)
    k = (m @ wk).reshape(Bsz, M, MEM_HEADS, MEM_HEADDIM)
    v = (m @ wv).reshape(Bsz, M, MEM_HEADS, MEM_HEADDIM)
    s = jnp.einsum('bthd,bmhd->bhtm', q, k).astype(jnp.float32) * (MEM_HEADDIM ** -0.5)
    pr = jax.nn.softmax(s, axis=-1).astype(v.dtype)
    o = jnp.einsum('bhtm,bmhd->bthd', pr, v).reshape(Bsz, T, D_MODEL)
    return o @ wo


def _inv_softplus_dt(key, n):
    dt = jnp.exp(jax.random.uniform(key, (n,), minval=math.log(1e-3), maxval=math.log(1e-1)))
    return dt + jnp.log(-jnp.expm1(-dt))


def _fwd_setup_inputs(seed: int = 0) -> dict:
    key = jax.random.key(seed)
    ks = jax.random.split(key, 26)
    nrm = lambda k, shape, s: jax.random.normal(k, shape, jnp.float32) * s
    gain = lambda k, n: 1.0 + 0.02 * jax.random.normal(k, (n,), jnp.float32)
    return {
        "x": nrm(ks[0], (BATCH, SEQ, D_MODEL), 1.0),
        "mem": nrm(ks[1], (BATCH, MEM_LEN, D_MODEL), 1.0),
        "norm1_w": gain(ks[2], D_MODEL),
        "w_in": nrm(ks[3], (D_MODEL, IN_COLS), D_MODEL ** -0.5),
        "gdn_conv_w": nrm(ks[4], (CONV_K, 3 * GDN_DIM), CONV_K ** -0.5),
        "gdn_a_log": jnp.log(jax.random.uniform(ks[5], (GDN_HEADS,), minval=1.0, maxval=16.0)),
        "gdn_dt_bias": _inv_softplus_dt(ks[6], GDN_HEADS),
        "gdn_norm_w": gain(ks[7], GDN_DV),
        "ssm_conv_w": nrm(ks[8], (CONV_K, SSM_DIM + 2 * SSM_BC), CONV_K ** -0.5),
        "ssm_conv_b": nrm(ks[9], (SSM_DIM + 2 * SSM_BC,), 0.02),
        "ssm_a_log": jnp.log(jax.random.uniform(ks[10], (SSM_HEADS,), minval=1.0, maxval=16.0)),
        "ssm_dt_bias": _inv_softplus_dt(ks[11], SSM_HEADS),
        "ssm_d": gain(ks[12], SSM_HEADS),
        "ssm_norm_w": gain(ks[13], SSM_DIM),
        "w_out": nrm(ks[14], (MIX_DIM, D_MODEL), MIX_DIM ** -0.5),
        "norm2_w": gain(ks[15], D_MODEL),
        "mem_norm_w": gain(ks[16], D_MODEL),
        "wq_mem": nrm(ks[17], (D_MODEL, D_MODEL), D_MODEL ** -0.5),
        "wk_mem": nrm(ks[18], (D_MODEL, D_MODEL), D_MODEL ** -0.5),
        "wv_mem": nrm(ks[19], (D_MODEL, D_MODEL), D_MODEL ** -0.5),
        "wo_mem": nrm(ks[20], (D_MODEL, D_MODEL), D_MODEL ** -0.5),
        "norm3_w": gain(ks[21], D_MODEL),
        "w_up": nrm(ks[22], (D_MODEL, D_FF), D_MODEL ** -0.5),
        "w_down": nrm(ks[23], (D_FF, D_MODEL), D_FF ** -0.5),
        "final_norm_w": gain(ks[24], D_MODEL),
    }


def _fwd_reference(x, mem, norm1_w, w_in, gdn_conv_w, gdn_a_log, gdn_dt_bias, gdn_norm_w,
              ssm_conv_w, ssm_conv_b, ssm_a_log, ssm_dt_bias, ssm_d, ssm_norm_w, w_out,
              norm2_w, mem_norm_w, wq_mem, wk_mem, wv_mem, wo_mem, norm3_w, w_up, w_down,
              final_norm_w):
    m = rmsnorm(mem, mem_norm_w)
    for _ in range(DEPTH):
        h = rmsnorm(x, norm1_w)
        p = h @ w_in
        o_a = gdn_mixer(p[..., :GDN_COLS], gdn_conv_w, gdn_a_log, gdn_dt_bias, gdn_norm_w)
        o_b = ssd_mixer(p[..., GDN_COLS:], ssm_conv_w, ssm_conv_b, ssm_a_log, ssm_dt_bias,
                        ssm_d, ssm_norm_w)
        x = x + jnp.concatenate([o_a, o_b], axis=-1) @ w_out
        x = x + memory_cross_attention(rmsnorm(x, norm2_w), m, wq_mem, wk_mem, wv_mem, wo_mem)
        u = jax.nn.relu(rmsnorm(x, norm3_w) @ w_up)
        x = x + (u * u) @ w_down
    return rmsnorm(x, final_norm_w)


import jax as _jax
import jax.numpy as _jnp

TWIN_FORMAT = 'train_step'
FWD_PARAMS = ['x', 'mem', 'norm1_w', 'w_in', 'gdn_conv_w', 'gdn_a_log', 'gdn_dt_bias', 'gdn_norm_w', 'ssm_conv_w', 'ssm_conv_b', 'ssm_a_log', 'ssm_dt_bias', 'ssm_d', 'ssm_norm_w', 'w_out', 'norm2_w', 'mem_norm_w', 'wq_mem', 'wk_mem', 'wv_mem', 'wo_mem', 'norm3_w', 'w_up', 'w_down', 'final_norm_w']
TWIN_WEIGHTS = ['norm1_w', 'w_in', 'gdn_conv_w', 'gdn_a_log', 'gdn_dt_bias', 'gdn_norm_w', 'ssm_conv_w', 'ssm_conv_b', 'ssm_a_log', 'ssm_dt_bias', 'ssm_d', 'ssm_norm_w', 'w_out', 'norm2_w', 'mem_norm_w', 'wq_mem', 'wk_mem', 'wv_mem', 'wo_mem', 'norm3_w', 'w_up', 'w_down', 'final_norm_w']
TWIN_DIFF_INPUT = 'x'
TWIN_INPUTS = ['x', 'mem', 'norm1_w', 'w_in', 'gdn_conv_w', 'gdn_a_log', 'gdn_dt_bias', 'gdn_norm_w', 'ssm_conv_w', 'ssm_conv_b', 'ssm_a_log', 'ssm_dt_bias', 'ssm_d', 'ssm_norm_w', 'w_out', 'norm2_w', 'mem_norm_w', 'wq_mem', 'wk_mem', 'wv_mem', 'wo_mem', 'norm3_w', 'w_up', 'w_down', 'final_norm_w', 'loss_target', 'm_norm1_w', 'm_w_in', 'm_gdn_conv_w', 'm_gdn_a_log', 'm_gdn_dt_bias', 'm_gdn_norm_w', 'm_ssm_conv_w', 'm_ssm_conv_b', 'm_ssm_a_log', 'm_ssm_dt_bias', 'm_ssm_d', 'm_ssm_norm_w', 'm_w_out', 'm_norm2_w', 'm_mem_norm_w', 'm_wq_mem', 'm_wk_mem', 'm_wv_mem', 'm_wo_mem', 'm_norm3_w', 'm_w_up', 'm_w_down', 'm_final_norm_w', 'v_norm1_w', 'v_w_in', 'v_gdn_conv_w', 'v_gdn_a_log', 'v_gdn_dt_bias', 'v_gdn_norm_w', 'v_ssm_conv_w', 'v_ssm_conv_b', 'v_ssm_a_log', 'v_ssm_dt_bias', 'v_ssm_d', 'v_ssm_norm_w', 'v_w_out', 'v_norm2_w', 'v_mem_norm_w', 'v_wq_mem', 'v_wk_mem', 'v_wv_mem', 'v_wo_mem', 'v_norm3_w', 'v_w_up', 'v_w_down', 'v_final_norm_w']
TWIN_OUTPUTS = ['loss', 'grad_x', 'grad_norm1_w', 'grad_w_in', 'grad_gdn_conv_w', 'grad_gdn_a_log', 'grad_gdn_dt_bias', 'grad_gdn_norm_w', 'grad_ssm_conv_w', 'grad_ssm_conv_b', 'grad_ssm_a_log', 'grad_ssm_dt_bias', 'grad_ssm_d', 'grad_ssm_norm_w', 'grad_w_out', 'grad_norm2_w', 'grad_mem_norm_w', 'grad_wq_mem', 'grad_wk_mem', 'grad_wv_mem', 'grad_wo_mem', 'grad_norm3_w', 'grad_w_up', 'grad_w_down', 'grad_final_norm_w', 'delta_norm1_w', 'delta_w_in', 'delta_gdn_conv_w', 'delta_gdn_a_log', 'delta_gdn_dt_bias', 'delta_gdn_norm_w', 'delta_ssm_conv_w', 'delta_ssm_conv_b', 'delta_ssm_a_log', 'delta_ssm_dt_bias', 'delta_ssm_d', 'delta_ssm_norm_w', 'delta_w_out', 'delta_norm2_w', 'delta_mem_norm_w', 'delta_wq_mem', 'delta_wk_mem', 'delta_wv_mem', 'delta_wo_mem', 'delta_norm3_w', 'delta_w_up', 'delta_w_down', 'delta_final_norm_w', 'new_m_norm1_w', 'new_m_w_in', 'new_m_gdn_conv_w', 'new_m_gdn_a_log', 'new_m_gdn_dt_bias', 'new_m_gdn_norm_w', 'new_m_ssm_conv_w', 'new_m_ssm_conv_b', 'new_m_ssm_a_log', 'new_m_ssm_dt_bias', 'new_m_ssm_d', 'new_m_ssm_norm_w', 'new_m_w_out', 'new_m_norm2_w', 'new_m_mem_norm_w', 'new_m_wq_mem', 'new_m_wk_mem', 'new_m_wv_mem', 'new_m_wo_mem', 'new_m_norm3_w', 'new_m_w_up', 'new_m_w_down', 'new_m_final_norm_w', 'new_v_norm1_w', 'new_v_w_in', 'new_v_gdn_conv_w', 'new_v_gdn_a_log', 'new_v_gdn_dt_bias', 'new_v_gdn_norm_w', 'new_v_ssm_conv_w', 'new_v_ssm_conv_b', 'new_v_ssm_a_log', 'new_v_ssm_dt_bias', 'new_v_ssm_d', 'new_v_ssm_norm_w', 'new_v_w_out', 'new_v_norm2_w', 'new_v_mem_norm_w', 'new_v_wq_mem', 'new_v_wk_mem', 'new_v_wv_mem', 'new_v_wo_mem', 'new_v_norm3_w', 'new_v_w_up', 'new_v_w_down', 'new_v_final_norm_w']
TWIN_LEAF_KINDS = {'loss': 'loss', 'grad_x': 'grad_x', 'grad_norm1_w': 'grad_w', 'grad_w_in': 'grad_w', 'grad_gdn_conv_w': 'grad_w', 'grad_gdn_a_log': 'grad_w', 'grad_gdn_dt_bias': 'grad_w', 'grad_gdn_norm_w': 'grad_w', 'grad_ssm_conv_w': 'grad_w', 'grad_ssm_conv_b': 'grad_w', 'grad_ssm_a_log': 'grad_w', 'grad_ssm_dt_bias': 'grad_w', 'grad_ssm_d': 'grad_w', 'grad_ssm_norm_w': 'grad_w', 'grad_w_out': 'grad_w', 'grad_norm2_w': 'grad_w', 'grad_mem_norm_w': 'grad_w', 'grad_wq_mem': 'grad_w', 'grad_wk_mem': 'grad_w', 'grad_wv_mem': 'grad_w', 'grad_wo_mem': 'grad_w', 'grad_norm3_w': 'grad_w', 'grad_w_up': 'grad_w', 'grad_w_down': 'grad_w', 'grad_final_norm_w': 'grad_w', 'delta_norm1_w': 'delta_w', 'delta_w_in': 'delta_w', 'delta_gdn_conv_w': 'delta_w', 'delta_gdn_a_log': 'delta_w', 'delta_gdn_dt_bias': 'delta_w', 'delta_gdn_norm_w': 'delta_w', 'delta_ssm_conv_w': 'delta_w', 'delta_ssm_conv_b': 'delta_w', 'delta_ssm_a_log': 'delta_w', 'delta_ssm_dt_bias': 'delta_w', 'delta_ssm_d': 'delta_w', 'delta_ssm_norm_w': 'delta_w', 'delta_w_out': 'delta_w', 'delta_norm2_w': 'delta_w', 'delta_mem_norm_w': 'delta_w', 'delta_wq_mem': 'delta_w', 'delta_wk_mem': 'delta_w', 'delta_wv_mem': 'delta_w', 'delta_wo_mem': 'delta_w', 'delta_norm3_w': 'delta_w', 'delta_w_up': 'delta_w', 'delta_w_down': 'delta_w', 'delta_final_norm_w': 'delta_w', 'new_m_norm1_w': 'new_m', 'new_m_w_in': 'new_m', 'new_m_gdn_conv_w': 'new_m', 'new_m_gdn_a_log': 'new_m', 'new_m_gdn_dt_bias': 'new_m', 'new_m_gdn_norm_w': 'new_m', 'new_m_ssm_conv_w': 'new_m', 'new_m_ssm_conv_b': 'new_m', 'new_m_ssm_a_log': 'new_m', 'new_m_ssm_dt_bias': 'new_m', 'new_m_ssm_d': 'new_m', 'new_m_ssm_norm_w': 'new_m', 'new_m_w_out': 'new_m', 'new_m_norm2_w': 'new_m', 'new_m_mem_norm_w': 'new_m', 'new_m_wq_mem': 'new_m', 'new_m_wk_mem': 'new_m', 'new_m_wv_mem': 'new_m', 'new_m_wo_mem': 'new_m', 'new_m_norm3_w': 'new_m', 'new_m_w_up': 'new_m', 'new_m_w_down': 'new_m', 'new_m_final_norm_w': 'new_m', 'new_v_norm1_w': 'new_v', 'new_v_w_in': 'new_v', 'new_v_gdn_conv_w': 'new_v', 'new_v_gdn_a_log': 'new_v', 'new_v_gdn_dt_bias': 'new_v', 'new_v_gdn_norm_w': 'new_v', 'new_v_ssm_conv_w': 'new_v', 'new_v_ssm_conv_b': 'new_v', 'new_v_ssm_a_log': 'new_v', 'new_v_ssm_dt_bias': 'new_v', 'new_v_ssm_d': 'new_v', 'new_v_ssm_norm_w': 'new_v', 'new_v_w_out': 'new_v', 'new_v_norm2_w': 'new_v', 'new_v_mem_norm_w': 'new_v', 'new_v_wq_mem': 'new_v', 'new_v_wk_mem': 'new_v', 'new_v_wv_mem': 'new_v', 'new_v_wo_mem': 'new_v', 'new_v_norm3_w': 'new_v', 'new_v_w_up': 'new_v', 'new_v_w_down': 'new_v', 'new_v_final_norm_w': 'new_v'}


def _forward(args):
    return _fwd_reference(*[args[k] for k in FWD_PARAMS])


def _output_shape():
    def fwd():
        inp = _fwd_setup_inputs(0)
        return _fwd_reference(*[inp[k] for k in FWD_PARAMS])
    out = _jax.eval_shape(fwd)
    return out.shape, out.dtype

N_MICROBATCH = 1
ADAM_LR = 0.001
ADAM_B1 = 0.9
ADAM_B2 = 0.999
ADAM_EPS = 1e-08
ADAM_WD = 0.01
ADAM_STEP = 10
PER_EXAMPLE_BATCH_AXIS = {'x': 0, 'mem': 0, 'loss_target': 0}
SHARED_INPUTS = []
_WEIGHT_DTYPES = {'norm1_w': _jnp.float32, 'w_in': _jnp.float32, 'gdn_conv_w': _jnp.float32, 'gdn_a_log': _jnp.float32, 'gdn_dt_bias': _jnp.float32, 'gdn_norm_w': _jnp.float32, 'ssm_conv_w': _jnp.float32, 'ssm_conv_b': _jnp.float32, 'ssm_a_log': _jnp.float32, 'ssm_dt_bias': _jnp.float32, 'ssm_d': _jnp.float32, 'ssm_norm_w': _jnp.float32, 'w_out': _jnp.float32, 'norm2_w': _jnp.float32, 'mem_norm_w': _jnp.float32, 'wq_mem': _jnp.float32, 'wk_mem': _jnp.float32, 'wv_mem': _jnp.float32, 'wo_mem': _jnp.float32, 'norm3_w': _jnp.float32, 'w_up': _jnp.float32, 'w_down': _jnp.float32, 'final_norm_w': _jnp.float32}
MOMENT_SCALE = {'norm1_w': 2.751816e-01, 'w_in': 1.048180e-01, 'gdn_conv_w': 6.775262e-02, 'gdn_a_log': 2.906481e-01, 'gdn_dt_bias': 2.695705e-01, 'gdn_norm_w': 2.633383e-01, 'ssm_conv_w': 1.320642e-01, 'ssm_conv_b': 1.707116e-01, 'ssm_a_log': 3.899130e-01, 'ssm_dt_bias': 4.447649e-01, 'ssm_d': 8.277188e-01, 'ssm_norm_w': 1.520551e-01, 'w_out': 1.756489e-01, 'norm2_w': 2.204272e-02, 'mem_norm_w': 3.292713e-02, 'wq_mem': 2.222308e-02, 'wk_mem': 2.213462e-02, 'wv_mem': 2.273970e-02, 'wo_mem': 2.292846e-02, 'norm3_w': 2.143544e-01, 'w_up': 1.006638e-01, 'w_down': 2.036066e-01, 'final_norm_w': 6.443864e+01}


def _to_microbatches(a, axis):
    t = _jnp.moveaxis(a, axis, 0)
    t = t.reshape((N_MICROBATCH, t.shape[0] // N_MICROBATCH) + t.shape[1:])
    return _jnp.moveaxis(t, 1, axis + 1)


def setup_inputs(seed: int = 0) -> dict:
    inp = _fwd_setup_inputs(seed)
    key = _jax.random.fold_in(_jax.random.key(seed), 7919)
    shape, _ = _output_shape()
    out = dict(inp)
    out["loss_target"] = _jax.random.normal(_jax.random.fold_in(key, 0), shape, _jnp.float32)
    for i, name in enumerate(TWIN_WEIGHTS):
        w = inp[name].astype(_jnp.float32)
        if MOMENT_SCALE is None:
            s = _jnp.sqrt(_jnp.mean(_jnp.square(w)) + 1e-30)
        else:
            s = MOMENT_SCALE[name]
        km, kv = _jax.random.split(_jax.random.fold_in(key, i + 1))
        out[name] = w
        out["m_" + name] = s * _jax.random.normal(km, w.shape, _jnp.float32)
        out["v_" + name] = (s * s) * _jax.random.uniform(kv, w.shape, _jnp.float32, 0.5, 1.5)
    if N_MICROBATCH > 1:
        for name, axis in PER_EXAMPLE_BATCH_AXIS.items():
            out[name] = _to_microbatches(out[name], axis)
    return {'x': out['x'], 'mem': out['mem'], 'norm1_w': out['norm1_w'], 'w_in': out['w_in'], 'gdn_conv_w': out['gdn_conv_w'], 'gdn_a_log': out['gdn_a_log'], 'gdn_dt_bias': out['gdn_dt_bias'], 'gdn_norm_w': out['gdn_norm_w'], 'ssm_conv_w': out['ssm_conv_w'], 'ssm_conv_b': out['ssm_conv_b'], 'ssm_a_log': out['ssm_a_log'], 'ssm_dt_bias': out['ssm_dt_bias'], 'ssm_d': out['ssm_d'], 'ssm_norm_w': out['ssm_norm_w'], 'w_out': out['w_out'], 'norm2_w': out['norm2_w'], 'mem_norm_w': out['mem_norm_w'], 'wq_mem': out['wq_mem'], 'wk_mem': out['wk_mem'], 'wv_mem': out['wv_mem'], 'wo_mem': out['wo_mem'], 'norm3_w': out['norm3_w'], 'w_up': out['w_up'], 'w_down': out['w_down'], 'final_norm_w': out['final_norm_w'], 'loss_target': out['loss_target'], 'm_norm1_w': out['m_norm1_w'], 'm_w_in': out['m_w_in'], 'm_gdn_conv_w': out['m_gdn_conv_w'], 'm_gdn_a_log': out['m_gdn_a_log'], 'm_gdn_dt_bias': out['m_gdn_dt_bias'], 'm_gdn_norm_w': out['m_gdn_norm_w'], 'm_ssm_conv_w': out['m_ssm_conv_w'], 'm_ssm_conv_b': out['m_ssm_conv_b'], 'm_ssm_a_log': out['m_ssm_a_log'], 'm_ssm_dt_bias': out['m_ssm_dt_bias'], 'm_ssm_d': out['m_ssm_d'], 'm_ssm_norm_w': out['m_ssm_norm_w'], 'm_w_out': out['m_w_out'], 'm_norm2_w': out['m_norm2_w'], 'm_mem_norm_w': out['m_mem_norm_w'], 'm_wq_mem': out['m_wq_mem'], 'm_wk_mem': out['m_wk_mem'], 'm_wv_mem': out['m_wv_mem'], 'm_wo_mem': out['m_wo_mem'], 'm_norm3_w': out['m_norm3_w'], 'm_w_up': out['m_w_up'], 'm_w_down': out['m_w_down'], 'm_final_norm_w': out['m_final_norm_w'], 'v_norm1_w': out['v_norm1_w'], 'v_w_in': out['v_w_in'], 'v_gdn_conv_w': out['v_gdn_conv_w'], 'v_gdn_a_log': out['v_gdn_a_log'], 'v_gdn_dt_bias': out['v_gdn_dt_bias'], 'v_gdn_norm_w': out['v_gdn_norm_w'], 'v_ssm_conv_w': out['v_ssm_conv_w'], 'v_ssm_conv_b': out['v_ssm_conv_b'], 'v_ssm_a_log': out['v_ssm_a_log'], 'v_ssm_dt_bias': out['v_ssm_dt_bias'], 'v_ssm_d': out['v_ssm_d'], 'v_ssm_norm_w': out['v_ssm_norm_w'], 'v_w_out': out['v_w_out'], 'v_norm2_w': out['v_norm2_w'], 'v_mem_norm_w': out['v_mem_norm_w'], 'v_wq_mem': out['v_wq_mem'], 'v_wk_mem': out['v_wk_mem'], 'v_wv_mem': out['v_wv_mem'], 'v_wo_mem': out['v_wo_mem'], 'v_norm3_w': out['v_norm3_w'], 'v_w_up': out['v_w_up'], 'v_w_down': out['v_w_down'], 'v_final_norm_w': out['v_final_norm_w']}


def _loss(weights, diff, rest, loss_target):
    with _jax.named_scope("forward"):
        args = {**rest, TWIN_DIFF_INPUT: diff, **{k: w.astype(_WEIGHT_DTYPES[k]) for k, w in weights.items()}}
        y = _forward(args)
    with _jax.named_scope("loss_head"):
        err = _jnp.square(y.astype(_jnp.float32) - loss_target)
        return 0.5 * _jnp.sum(_jnp.mean(err, axis=-1)) if err.ndim else 0.5 * err


def _adamw(w, g, m, v):
    m = ADAM_B1 * m + (1.0 - ADAM_B1) * g
    v = ADAM_B2 * v + (1.0 - ADAM_B2) * _jnp.square(g)
    m_hat = m / (1.0 - ADAM_B1 ** ADAM_STEP)
    v_hat = v / (1.0 - ADAM_B2 ** ADAM_STEP)
    delta = -ADAM_LR * (m_hat / (_jnp.sqrt(v_hat) + ADAM_EPS) + ADAM_WD * w)
    return delta, m, v


def reference(x, mem, norm1_w, w_in, gdn_conv_w, gdn_a_log, gdn_dt_bias, gdn_norm_w, ssm_conv_w, ssm_conv_b, ssm_a_log, ssm_dt_bias, ssm_d, ssm_norm_w, w_out, norm2_w, mem_norm_w, wq_mem, wk_mem, wv_mem, wo_mem, norm3_w, w_up, w_down, final_norm_w, loss_target, m_norm1_w, m_w_in, m_gdn_conv_w, m_gdn_a_log, m_gdn_dt_bias, m_gdn_norm_w, m_ssm_conv_w, m_ssm_conv_b, m_ssm_a_log, m_ssm_dt_bias, m_ssm_d, m_ssm_norm_w, m_w_out, m_norm2_w, m_mem_norm_w, m_wq_mem, m_wk_mem, m_wv_mem, m_wo_mem, m_norm3_w, m_w_up, m_w_down, m_final_norm_w, v_norm1_w, v_w_in, v_gdn_conv_w, v_gdn_a_log, v_gdn_dt_bias, v_gdn_norm_w, v_ssm_conv_w, v_ssm_conv_b, v_ssm_a_log, v_ssm_dt_bias, v_ssm_d, v_ssm_norm_w, v_w_out, v_norm2_w, v_mem_norm_w, v_wq_mem, v_wk_mem, v_wv_mem, v_wo_mem, v_norm3_w, v_w_up, v_w_down, v_final_norm_w):
    given = dict(x=x, mem=mem, norm1_w=norm1_w, w_in=w_in, gdn_conv_w=gdn_conv_w, gdn_a_log=gdn_a_log, gdn_dt_bias=gdn_dt_bias, gdn_norm_w=gdn_norm_w, ssm_conv_w=ssm_conv_w, ssm_conv_b=ssm_conv_b, ssm_a_log=ssm_a_log, ssm_dt_bias=ssm_dt_bias, ssm_d=ssm_d, ssm_norm_w=ssm_norm_w, w_out=w_out, norm2_w=norm2_w, mem_norm_w=mem_norm_w, wq_mem=wq_mem, wk_mem=wk_mem, wv_mem=wv_mem, wo_mem=wo_mem, norm3_w=norm3_w, w_up=w_up, w_down=w_down, final_norm_w=final_norm_w, loss_target=loss_target, m_norm1_w=m_norm1_w, m_w_in=m_w_in, m_gdn_conv_w=m_gdn_conv_w, m_gdn_a_log=m_gdn_a_log, m_gdn_dt_bias=m_gdn_dt_bias, m_gdn_norm_w=m_gdn_norm_w, m_ssm_conv_w=m_ssm_conv_w, m_ssm_conv_b=m_ssm_conv_b, m_ssm_a_log=m_ssm_a_log, m_ssm_dt_bias=m_ssm_dt_bias, m_ssm_d=m_ssm_d, m_ssm_norm_w=m_ssm_norm_w, m_w_out=m_w_out, m_norm2_w=m_norm2_w, m_mem_norm_w=m_mem_norm_w, m_wq_mem=m_wq_mem, m_wk_mem=m_wk_mem, m_wv_mem=m_wv_mem, m_wo_mem=m_wo_mem, m_norm3_w=m_norm3_w, m_w_up=m_w_up, m_w_down=m_w_down, m_final_norm_w=m_final_norm_w, v_norm1_w=v_norm1_w, v_w_in=v_w_in, v_gdn_conv_w=v_gdn_conv_w, v_gdn_a_log=v_gdn_a_log, v_gdn_dt_bias=v_gdn_dt_bias, v_gdn_norm_w=v_gdn_norm_w, v_ssm_conv_w=v_ssm_conv_w, v_ssm_conv_b=v_ssm_conv_b, v_ssm_a_log=v_ssm_a_log, v_ssm_dt_bias=v_ssm_dt_bias, v_ssm_d=v_ssm_d, v_ssm_norm_w=v_ssm_norm_w, v_w_out=v_w_out, v_norm2_w=v_norm2_w, v_mem_norm_w=v_mem_norm_w, v_wq_mem=v_wq_mem, v_wk_mem=v_wk_mem, v_wv_mem=v_wv_mem, v_wo_mem=v_wo_mem, v_norm3_w=v_norm3_w, v_w_up=v_w_up, v_w_down=v_w_down, v_final_norm_w=v_final_norm_w)
    weights = {n: given[n] for n in TWIN_WEIGHTS}
    shared = {n: given[n] for n in SHARED_INPUTS}
    per_example = {n: given[n] for n in ['x', 'mem']}
    grad_fn = _jax.value_and_grad(_loss, argnums=(0, 1))

    def one_microbatch(ex, loss_target):
        ex = dict(ex)
        diff = ex.pop(TWIN_DIFF_INPUT)
        return grad_fn(weights, diff, {**shared, **ex}, loss_target)

    if N_MICROBATCH == 1:
        loss, (grad_w, grad_x) = one_microbatch(per_example, given["loss_target"])
    else:
        def body(carry, xs):
            loss_sum, grad_sum = carry
            l_k, (gw_k, gx_k) = one_microbatch(xs[0], xs[1])
            with _jax.named_scope("update"):
                return (loss_sum + l_k, _jax.tree.map(_jnp.add, grad_sum, gw_k)), gx_k

        init = (_jnp.zeros((), _jnp.float32), _jax.tree.map(_jnp.zeros_like, weights))
        (loss, grad_w), grad_x = _jax.lax.scan(body, init, (per_example, given["loss_target"]))
    with _jax.named_scope("update"):
        delta_w, new_m, new_v = {}, {}, {}
        for n in TWIN_WEIGHTS:
            delta_w[n], new_m[n], new_v[n] = _adamw(weights[n], grad_w[n], given["m_" + n], given["v_" + n])
    return (loss, grad_x, *[grad_w[n] for n in TWIN_WEIGHTS], *[delta_w[n] for n in TWIN_WEIGHTS],
            *[new_m[n] for n in TWIN_WEIGHTS], *[new_v[n] for n in TWIN_WEIGHTS])
```

```python
import functools

import numpy as np
import jax
import jax.numpy as jnp
from jax import lax
from jax.experimental import pallas as pl
from jax.experimental.pallas import tpu as pltpu

F32, BF16 = jnp.float32, jnp.bfloat16
HI = lax.Precision.HIGHEST
MESH = pl.DeviceIdType.MESH
ANY = pl.BlockSpec(memory_space=pl.ANY)

EPS = 1e-6
D = 1024
GDN_H, GDN_DK, GDN_C = 8, 128, 64
SSM_H, SSM_P, SSM_N, SSM_L = 16, 64, 128, 128
MEM_H, MEM_DH = 4, 256
D_FF = 4096
IN_COLS = 6688
PAD_COLS = 6912
CB_QKV, CB_Z, CB_ZS, CB_XS, CB_BC, CB_BA, CB_DT = 0, 3, 4, 5, 12, 52, 53
VMEM_LIMIT = 56 * 1024 * 1024

ADAM_LR, ADAM_B1, ADAM_B2, ADAM_EPS, ADAM_WD, ADAM_STEP = 0.001, 0.9, 0.999, 1e-08, 0.01, 10


def _dg(a, b, ca, cb, prec=None):
    return lax.dot_general(a, b, (((ca,), (cb,)), ((), ())), precision=prec,
                           preferred_element_type=F32)


def _bf(x):
    return x.astype(BF16)


def mm(a, b):
    return _dg(_bf(a), _bf(b), 1, 0)


def mm_nt(a, b):
    return _dg(_bf(a), _bf(b), 1, 1)


def mm_tn(a, b):
    return _dg(_bf(a), _bf(b), 0, 0)


def mh(a, b):
    return _dg(a, b, 1, 0, HI)


def mh_nt(a, b):
    return _dg(a, b, 1, 1, HI)


def mh_tn(a, b):
    return _dg(a, b, 0, 0, HI)


def _iota(shape, dim):
    return lax.broadcasted_iota(jnp.int32, shape, dim)


def _sig(x):
    return 1.0 / (1.0 + jnp.exp(-x))


def _softplus(x):
    return jnp.maximum(x, 0.0) + jnp.log(1.0 + jnp.exp(-jnp.abs(x)))


def _rows(v):
    return jnp.sum(v, axis=0, keepdims=True)


def _sum_all(v):
    return jnp.sum(jnp.sum(v, axis=1, keepdims=True), axis=0, keepdims=True)


def _cparams(sem):
    return pltpu.CompilerParams(dimension_semantics=sem, vmem_limit_bytes=VMEM_LIMIT)


def rowwise(fn, name, T, tb, row_ins, full_ins, row_outs, acc_outs=()):
    nblk = T // tb
    assert nblk * tb == T
    in_specs, args = [], []
    for arr, w, cb, halo, off in row_ins:
        in_specs.append(pl.BlockSpec((tb, w), lambda i, cb=cb, off=off: (i + off, cb)))
        args.append(arr)
        if halo == "prev":
            r = tb // 8
            in_specs.append(pl.BlockSpec((8, w), lambda i, cb=cb, r=r: (jnp.maximum(i * r - 1, 0), cb)))
            args.append(arr)
        elif halo == "next":
            r, last = tb // 8, T // 8 - 1
            in_specs.append(pl.BlockSpec((8, w), lambda i, cb=cb, r=r, last=last:
                                         (jnp.minimum((i + 1) * r, last), cb)))
            args.append(arr)
    for arr in full_ins:
        in_specs.append(pl.BlockSpec(arr.shape, lambda i, nd=arr.ndim: (0,) * nd))
        args.append(arr)
    n_in, n_ro = len(args), len(row_outs)
    out_shape = [jax.ShapeDtypeStruct((T, w), dt) for w, dt in row_outs]
    out_specs = [pl.BlockSpec((tb, w), lambda i: (i, 0)) for w, _ in row_outs]
    for shp in acc_outs:
        out_shape.append(jax.ShapeDtypeStruct(shp, F32))
        out_specs.append(pl.BlockSpec(shp, lambda i, nd=len(shp): (0,) * nd))

    def body(*refs):
        i = pl.program_id(0)
        vals = fn(i, nblk, *[r[...] for r in refs[:n_in]])
        outs = refs[n_in:]
        for ref, val in zip(outs[:n_ro], vals[:n_ro]):
            ref[...] = val.astype(ref.dtype)
        for ref, val in zip(outs[n_ro:], vals[n_ro:]):
            @pl.when(i == 0)
            def _(ref=ref, val=val):
                ref[...] = val

            @pl.when(i > 0)
            def _(ref=ref, val=val):
                ref[...] += val

    res = pl.pallas_call(
        body, name=name, grid=(nblk,), in_specs=in_specs, out_specs=out_specs, out_shape=out_shape,
        compiler_params=_cparams(("arbitrary",) if acc_outs else ("parallel",)),
    )(*args)
    return res


def R(arr, w=None, cb=0, halo=None, off=0):
    return (arr, arr.shape[1] if w is None else w, cb, halo, off)


def matmul(name, a, b, form, tm, tn, tk, out_dtypes, epi=None, extras=()):
    if form == "nn":
        (M, K), N = a.shape, b.shape[1]
    elif form == "nt":
        (M, K), N = a.shape, b.shape[0]
    else:
        (K, M), N = a.shape, b.shape[1]
    tm, tn, tk = min(tm, M), min(tn, N), min(tk, K)
    assert M % tm == 0 and N % tn == 0 and K % tk == 0, (name, M, N, K, tm, tn, tk)
    if form == "nn":
        a_spec = pl.BlockSpec((tm, tk), lambda i, j, k: (i, k))
        b_spec = pl.BlockSpec((tk, tn), lambda i, j, k: (k, j))
        ca, cb = 1, 0
    elif form == "nt":
        a_spec = pl.BlockSpec((tm, tk), lambda i, j, k: (i, k))
        b_spec = pl.BlockSpec((tn, tk), lambda i, j, k: (j, k))
        ca, cb = 1, 1
    else:
        a_spec = pl.BlockSpec((tk, tm), lambda i, j, k: (k, i))
        b_spec = pl.BlockSpec((tk, tn), lambda i, j, k: (k, j))
        ca, cb = 0, 0
    nk, ne, no = K // tk, len(extras), len(out_dtypes)
    if epi is None:
        epi = lambda acc: (acc,)

    def body(a_ref, b_ref, *rest):
        ex, outs, acc = rest[:ne], rest[ne:ne + no], rest[ne + no]
        k = pl.program_id(2)

        @pl.when(k == 0)
        def _():
            acc[...] = jnp.zeros_like(acc)

        acc[...] += _dg(_bf(a_ref[...]), _bf(b_ref[...]), ca, cb)

        @pl.when(k == nk - 1)
        def _():
            vals = epi(acc[...], *[e[...] for e in ex])
            for r, v in zip(outs, vals):
                r[...] = v.astype(r.dtype)

    mn = pl.BlockSpec((tm, tn), lambda i, j, k: (i, j))
    res = pl.pallas_call(
        body, name=name, grid=(M // tm, N // tn, nk),
        in_specs=[a_spec, b_spec] + [mn] * ne, out_specs=[mn] * no,
        out_shape=[jax.ShapeDtypeStruct((M, N), dt) for dt in out_dtypes],
        scratch_shapes=[pltpu.VMEM((tm, tn), F32)],
        compiler_params=_cparams(("parallel", "parallel", "arbitrary")),
    )(a, b, *extras)
    return res


def _epi_res(acc, res):
    return (res + acc,)


def _epi_relu2(acc):
    u = jnp.maximum(acc, 0.0)
    return (u, u * u)


def _epi_dup(acc, u):
    return (acc * 2.0 * u.astype(F32),)


def _conv(x, halo, w, i):
    halo = jnp.where(i == 0, 0.0, halo)
    xt = jnp.concatenate([halo, x], axis=0)
    shifted = [pltpu.roll(xt, 3 - k, 0)[8:, :] for k in range(3)] + [x]
    y = shifted[3] * w[3:4, :]
    for k in range(3):
        y = y + shifted[k] * w[k:k + 1, :]
    return y, shifted


def _l2n(x, scale):
    outs = []
    for h in range(x.shape[1] // 128):
        xh = x[:, 128 * h:128 * h + 128]
        outs.append(xh * (lax.rsqrt(jnp.sum(xh * xh, axis=-1, keepdims=True) + EPS) * scale))
    return jnp.concatenate(outs, axis=1)


def _l2n_bwd(x, dy, scale):
    outs = []
    for h in range(x.shape[1] // 128):
        xh, dh = x[:, 128 * h:128 * h + 128], dy[:, 128 * h:128 * h + 128] * scale
        r = lax.rsqrt(jnp.sum(xh * xh, axis=-1, keepdims=True) + EPS)
        outs.append(r * dh - xh * (r * r * r) * jnp.sum(xh * dh, axis=-1, keepdims=True))
    return jnp.concatenate(outs, axis=1)


def rms_fwd_fn(i, n, x, w):
    r = lax.rsqrt(jnp.mean(x * x, axis=-1, keepdims=True) + EPS)
    return (x * r * w,)


def rms_bwd_fn(i, n, x, dh, dres, w):
    r = lax.rsqrt(jnp.mean(x * x, axis=-1, keepdims=True) + EPS)
    g = dh * w
    dx = dres + r * g - x * (r * r * r) * jnp.mean(x * g, axis=-1, keepdims=True)
    return dx, dx, _rows(dh * x * r)


def rms_bwd_w_fn(i, n, x, dh, w):
    r = lax.rsqrt(jnp.mean(x * x, axis=-1, keepdims=True) + EPS)
    return (_rows(dh * x * r),)


def final_fn(i, n, x, tgt, w):
    r = lax.rsqrt(jnp.mean(x * x, axis=-1, keepdims=True) + EPS)
    xn = x * r
    e = xn * w - tgt
    dy = e * (1.0 / D)
    g = dy * w
    dx = r * g - x * (r * r * r) * jnp.mean(x * g, axis=-1, keepdims=True)
    return dx, dx, _rows(e * e), _rows(dy * xn)


def _gdn_gates(ba, alog_x, dtb_x, eb, ea):
    b_x, a_x = mh(ba, eb), mh(ba, ea)
    z = a_x + dtb_x
    return _sig(b_x), z, jnp.exp(alog_x)


def gdn_prep_fn(i, n, qkv, halo, ba, cw, alog_x, dtb_x, eb, ea):
    yc, _ = _conv(qkv, halo, cw, i)
    act = yc * _sig(yc)
    qn = _l2n(act[:, :D], GDN_DK ** -0.5)
    kn = _l2n(act[:, D:2 * D], 1.0)
    beta, z, ea_ = _gdn_gates(ba, alog_x, dtb_x, eb, ea)
    return qn, kn, act[:, 2 * D:], -ea_ * _softplus(z), beta


def gdn_prep_bwd_fn(i, n, qkv, halo, ba, dqn, dkn, dv, dg_x, dbeta_x, cw, alog_x, dtb_x, eb, ea, pb, pa):
    yc, shifted = _conv(qkv, halo, cw, i)
    sg = _sig(yc)
    act = yc * sg
    dq = _l2n_bwd(act[:, :D], dqn, GDN_DK ** -0.5)
    dk = _l2n_bwd(act[:, D:2 * D], dkn, 1.0)
    dyc = jnp.concatenate([dq, dk, dv], axis=1) * (sg * (1.0 + yc * (1.0 - sg)))
    dws = [_rows(dyc * shifted[k]) for k in range(4)]
    beta, z, ea_ = _gdn_gates(ba, alog_x, dtb_x, eb, ea)
    g = -ea_ * _softplus(z)
    dbraw = dbeta_x * beta * (1.0 - beta)
    draw = dg_x * (-ea_) * _sig(z)
    dba = mh(dbraw, pb) + mh(draw, pa)
    return (dyc, dba, dws[0], dws[1], dws[2], dws[3], _rows(dg_x * g), _rows(draw))


def conv_bwd_fn(i, n, dyc, halo, w):
    halo = jnp.where(i == n - 1, 0.0, halo)
    tb = dyc.shape[0]
    xt = jnp.concatenate([dyc, halo], axis=0)
    dx = dyc * w[3:4, :]
    for k in range(3):
        dx = dx + pltpu.roll(xt, tb + 8 - (3 - k), 0)[:tb, :] * w[k:k + 1, :]
    return (dx,)


def gdn_post_fn(i, n, o, z, w):
    outs = []
    for h in range(GDN_H):
        oh, zh = o[:, 128 * h:128 * h + 128], z[:, 128 * h:128 * h + 128]
        r = lax.rsqrt(jnp.mean(oh * oh, axis=-1, keepdims=True) + EPS)
        outs.append(oh * r * w * (zh * _sig(zh)))
    return (jnp.concatenate(outs, axis=1),)


def gdn_post_bwd_fn(i, n, o, z, doa, w):
    dos, dzs, dw = [], [], None
    for h in range(GDN_H):
        sl = slice(128 * h, 128 * h + 128)
        oh, zh, dh = o[:, sl], z[:, sl], doa[:, sl]
        r = lax.rsqrt(jnp.mean(oh * oh, axis=-1, keepdims=True) + EPS)
        s = _sig(zh)
        dn = dh * (zh * s)
        dzs.append(dh * (oh * r * w) * (s * (1.0 + zh * (1.0 - s))))
        t = _rows(dn * oh * r)
        dw = t if dw is None else dw + t
        g = dn * w
        dos.append(r * g - oh * (r * r * r) * jnp.mean(oh * g, axis=-1, keepdims=True))
    return jnp.concatenate(dos, axis=1), jnp.concatenate(dzs, axis=1), dw


def _ssd_dt(dtblk, dtb_x, e16):
    return mh(dtblk, e16) + dtb_x


def ssd_prep_fn(i, n, xp, hx, bcp, hbc, dtblk, cwx, cwbc, cbx, cbbc, dtb_x, e16):
    yx, _ = _conv(xp, hx, cwx, i)
    yx = yx + cbx
    ybc, _ = _conv(bcp, hbc, cwbc, i)
    ybc = ybc + cbbc
    return yx * _sig(yx), ybc * _sig(ybc), _softplus(_ssd_dt(dtblk, dtb_x, e16))


def ssd_prep_bwd_fn(i, n, xp, hx, bcp, hbc, dtblk, dxs, dbc, ddt_x, cwx, cwbc, cbx, cbbc, dtb_x, e16, p16):
    yx, shx = _conv(xp, hx, cwx, i)
    yx = yx + cbx
    ybc, shbc = _conv(bcp, hbc, cwbc, i)
    ybc = ybc + cbbc
    sx, sbc = _sig(yx), _sig(ybc)
    dyx = dxs * (sx * (1.0 + yx * (1.0 - sx)))
    dybc = dbc * (sbc * (1.0 + ybc * (1.0 - sbc)))
    dwx = [_rows(dyx * shx[k]) for k in range(4)]
    dwbc = [_rows(dybc * shbc[k]) for k in range(4)]
    draw = ddt_x * _sig(_ssd_dt(dtblk, dtb_x, e16))
    return (dyx, dybc, mh(draw, p16), *dwx, *dwbc, _rows(dyx), _rows(dybc), _rows(draw))


def _ssd_gate(y, xs, zs, d_x):
    y2 = y + xs * d_x
    s = _sig(zs)
    return y2, s, y2 * (zs * s)


def ssd_post_fn(i, n, y, xs, zs, d_x, nw):
    _, _, yg = _ssd_gate(y, xs, zs, d_x)
    outs = []
    for g in range(2):
        v = yg[:, 512 * g:512 * g + 512]
        outs.append(v * lax.rsqrt(jnp.mean(v * v, axis=-1, keepdims=True) + EPS))
    return (jnp.concatenate(outs, axis=1) * nw,)


def ssd_post_bwd_fn(i, n, y, xs, zs, dob, d_x, nw):
    y2, s, yg = _ssd_gate(y, xs, zs, d_x)
    gfull = dob * nw
    dygs, dnw = [], []
    for g in range(2):
        sl = slice(512 * g, 512 * g + 512)
        v, gg = yg[:, sl], gfull[:, sl]
        r = lax.rsqrt(jnp.mean(v * v, axis=-1, keepdims=True) + EPS)
        dygs.append(r * gg - v * (r * r * r) * jnp.mean(v * gg, axis=-1, keepdims=True))
        dnw.append(_rows(dob[:, sl] * v * r))
    dyg = jnp.concatenate(dygs, axis=1)
    dy2 = dyg * (zs * s)
    dzs = dyg * y2 * (s * (1.0 + zs * (1.0 - s)))
    return dy2, dy2 * d_x, dzs, jnp.concatenate(dnw, axis=1), _rows(dy2 * xs)


def attn_fn(i, n, q, k, v):
    outs = []
    for h in range(MEM_H):
        sl = slice(MEM_DH * h, MEM_DH * h + MEM_DH)
        s = mm_nt(q[:, sl], k[:, sl]) * (MEM_DH ** -0.5)
        p = jnp.exp(s - jnp.max(s, axis=-1, keepdims=True))
        p = p / jnp.sum(p, axis=-1, keepdims=True)
        outs.append(mm(p, v[:, sl]))
    return (jnp.concatenate(outs, axis=1),)


def attn_bwd_fn(i, n, q, do, k, v):
    dqs, dks, dvs = [], [], []
    for h in range(MEM_H):
        sl = slice(MEM_DH * h, MEM_DH * h + MEM_DH)
        s = mm_nt(q[:, sl], k[:, sl]) * (MEM_DH ** -0.5)
        p = jnp.exp(s - jnp.max(s, axis=-1, keepdims=True))
        p = p / jnp.sum(p, axis=-1, keepdims=True)
        dvs.append(mm_tn(p, do[:, sl]))
        dp = mm_nt(do[:, sl], v[:, sl])
        ds = p * (dp - jnp.sum(dp * p, axis=-1, keepdims=True)) * (MEM_DH ** -0.5)
        dqs.append(mm(ds, k[:, sl]))
        dks.append(mm_tn(ds, q[:, sl]))
    return jnp.concatenate(dqs, axis=1), jnp.concatenate(dks, axis=1), jnp.concatenate(dvs, axis=1)


def add2_fn(i, n, a, b):
    return (a + b,)


def add3_fn(i, n, a, b, c):
    return (a + b + c,)


def sum4_fn(i, n, a, b, c, d):
    return (((a + b) + c) + d,)


def adamw_fn(i, n, w, g, m, v):
    m = ADAM_B1 * m + (1.0 - ADAM_B1) * g
    v = ADAM_B2 * v + (1.0 - ADAM_B2) * (g * g)
    m_hat = m / (1.0 - ADAM_B1 ** ADAM_STEP)
    v_hat = v / (1.0 - ADAM_B2 ** ADAM_STEP)
    delta = -ADAM_LR * (m_hat / (jnp.sqrt(v_hat) + ADAM_EPS) + ADAM_WD * w)
    return delta, m, v


def _gdn_local(q, k, v, gb, bb):
    C = GDN_C
    row, col = _iota((C, C), 0), _iota((C, C), 1)
    incl, strict = row >= col, row > col
    gcs = mh(incl.astype(F32), gb)
    sel = (_iota((C, 128), 1) == 0).astype(F32)
    diff = gcs[:, :C] - mh_nt(sel, gcs)
    dmat = jnp.where(incl, jnp.exp(jnp.minimum(diff, 0.0)), 0.0)
    gam = jnp.exp(gcs)
    gl = gcs[C - 1:C, :]
    kb, vb = k * bb, v * bb
    kg = kb * gam
    lmat = jnp.where(strict, mm_nt(kb, k) * dmat, 0.0)
    x = -lmat
    eye = (row == col).astype(F32)
    tinv = eye + x
    for _ in range(5):
        x = mh(x, x)
        tinv = tinv + mh(tinv, x)
    u, w = mm(tinv, vb), mm(tinv, kg)
    pmat = jnp.where(incl, mm_nt(q, k) * dmat, 0.0)
    return dict(incl=incl, strict=strict, gcs=gcs, dmat=dmat, gam=gam, gl=gl, kb=kb, vb=vb, kg=kg,
                lmat=lmat, tinv=tinv, u=u, w=w, pmat=pmat, qd=q * gam, kd=k * jnp.exp(gl - gcs),
                cd=jnp.exp(gl))


def gdn_fwd(qn, kn, v, g_x, beta_x, tb):
    T = qn.shape[0]
    nb, ncb, nc = T // tb, tb // GDN_C, T // GDN_C

    def body(q_ref, k_ref, v_ref, g_ref, b_ref, o_ref, st_ref, s_scr):
        @pl.when(pl.program_id(1) == 0)
        def _():
            s_scr[...] = jnp.zeros_like(s_scr)

        for c in range(ncb):
            sl = slice(GDN_C * c, GDN_C * (c + 1))
            lc = _gdn_local(q_ref[sl, :], k_ref[sl, :], v_ref[sl, :], g_ref[sl, :], b_ref[sl, :])
            s = s_scr[...]
            st_ref[c] = s
            vn = lc["u"] - mm(lc["w"], s)
            o_ref[sl, :] = mm(lc["qd"], s) + mm(lc["pmat"], vn)
            s_scr[...] = s * lc["cd"] + mm_tn(lc["kd"], vn)

    blk = pl.BlockSpec((tb, 128), lambda h, i: (i, h))
    return pl.pallas_call(
        body, name="gdn_fwd", grid=(GDN_H, nb), in_specs=[blk] * 5,
        out_specs=[blk, pl.BlockSpec((None, ncb, 128, 128), lambda h, i: (h, i, 0, 0))],
        out_shape=[jax.ShapeDtypeStruct((T, D), F32), jax.ShapeDtypeStruct((GDN_H, nc, 128, 128), F32)],
        scratch_shapes=[pltpu.VMEM((128, 128), F32)],
        compiler_params=_cparams(("parallel", "arbitrary")),
    )(qn, kn, v, g_x, beta_x)


def gdn_bwd(qn, kn, v, g_x, beta_x, do, states, tb):
    T = qn.shape[0]
    nb, ncb = T // tb, tb // GDN_C
    C = GDN_C

    def body(q_ref, k_ref, v_ref, g_ref, b_ref, do_ref, st_ref, dq_ref, dk_ref, dv_ref, dg_ref, db_ref, ds_scr):
        @pl.when(pl.program_id(1) == 0)
        def _():
            ds_scr[...] = jnp.zeros_like(ds_scr)

        ones = jnp.ones((C, 128), F32)
        row = _iota((C, C), 0)
        utri = (row <= _iota((C, C), 1)).astype(F32)
        lastrow = _iota((C, 128), 0) == C - 1
        for c in reversed(range(ncb)):
            sl = slice(C * c, C * (c + 1))
            q, k, vv, bb = q_ref[sl, :], k_ref[sl, :], v_ref[sl, :], b_ref[sl, :]
            lc = _gdn_local(q, k, vv, g_ref[sl, :], bb)
            do_c, s, dsn = do_ref[sl, :], st_ref[c], ds_scr[...]
            gam, cd, kd, qd, pm, u, w, tinv = (lc[n] for n in ("gam", "cd", "kd", "qd", "pmat", "u", "w", "tinv"))
            vn = u - mm(w, s)
            dvn = mm_tn(pm, do_c) + mm(kd, dsn)
            dqd = mm_nt(do_c, s)
            dp = jnp.where(lc["incl"], mm_nt(do_c, vn), 0.0)
            dkd = mm_nt(vn, dsn)
            dcd = _sum_all(s * dsn)
            ds_scr[...] = mm_tn(qd, do_c) + cd * dsn - mm_tn(w, dvn)
            dw = -mm_nt(dvn, s)
            dvb, dkg = mm_tn(tinv, dvn), mm_tn(tinv, dw)
            da = -jnp.where(lc["strict"], mm_nt(dvb, u) + mm_nt(dkg, w), 0.0)
            dm = da * lc["dmat"]
            dn = dp * lc["dmat"]
            dkb = mm(dm, k)
            e = da * lc["lmat"] + dp * pm
            dgcs = mh(e, ones) - mh_tn(e, ones)
            t_kd = jnp.sum(dkd * kd, axis=-1, keepdims=True)
            dgcs = dgcs + jnp.sum(dqd * qd, axis=-1, keepdims=True) - t_kd \
                + jnp.sum(dkg * lc["kg"], axis=-1, keepdims=True)
            dgl = _sum_all(t_kd) + dcd * cd[:, :1]
            dgcs = dgcs + jnp.where(lastrow, dgl, 0.0)
            dq_ref[sl, :] = mm(dn, k) + gam * dqd
            dk_ref[sl, :] = (mm_tn(dm, lc["kb"]) + mm_tn(dn, q) + jnp.exp(lc["gl"] - lc["gcs"]) * dkd
                             + bb * gam * dkg + bb * dkb)
            dv_ref[sl, :] = bb * dvb
            dbeta = (jnp.sum(dkg * gam * k, axis=-1, keepdims=True) + jnp.sum(dvb * vv, axis=-1, keepdims=True)
                     + jnp.sum(dkb * k, axis=-1, keepdims=True))
            db_ref[sl, :] = jnp.broadcast_to(dbeta, (C, 128))
            dg_ref[sl, :] = mh(utri, dgcs)

    blk = pl.BlockSpec((tb, 128), lambda h, i: (nb - 1 - i, h))
    return pl.pallas_call(
        body, name="gdn_bwd", grid=(GDN_H, nb),
        in_specs=[blk] * 6 + [pl.BlockSpec((None, ncb, 128, 128), lambda h, i: (h, nb - 1 - i, 0, 0))],
        out_specs=[blk] * 5, out_shape=[jax.ShapeDtypeStruct((T, D), F32)] * 5,
        scratch_shapes=[pltpu.VMEM((128, 128), F32)],
        compiler_params=_cparams(("parallel", "arbitrary")),
    )(qn, kn, v, g_x, beta_x, do, states)


def _ssd_pair(x2, dt2, alog2, ltri):
    a2n = -jnp.exp(alog2)
    acs2 = mh(ltri, dt2 * a2n)
    last = acs2[SSM_L - 1:SSM_L, :]
    return a2n, acs2, jnp.exp(acs2), jnp.exp(last - acs2), x2 * dt2


def _ssd_head(hh, acs2, dec2, cbm, bm, incl, col):
    lmask = (col >= 64 * hh) & (col < 64 * hh + 64)
    selh = (col == 64 * hh).astype(F32)
    rmat = mh_nt(selh, acs2)
    sg = jnp.where(incl, jnp.exp(jnp.minimum(acs2[:, 64 * hh:64 * hh + 1] - rmat, 0.0)), 0.0)
    dec_col = dec2[:, 64 * hh:64 * hh + 1]
    return lmask, sg, sg * cbm, dec_col, bm * dec_col


def _lam_rows(lam2, row):
    return mh_tn(jnp.broadcast_to(lam2[SSM_L - 1:SSM_L, :], (128, 128)), (row == 0).astype(F32))


def ssd_fwd(xs, bc, dt_x, alog_x):
    T = xs.shape[0]
    nc = T // SSM_L

    def body(x_ref, b_ref, c_ref, dt_ref, al_ref, y_ref, hst_ref, h_scr):
        @pl.when(pl.program_id(1) == 0)
        def _():
            h_scr[...] = jnp.zeros_like(h_scr)

        bm, cm = b_ref[...], c_ref[...]
        cbm = mm_nt(cm, bm)
        row, col = _iota((128, 128), 0), _iota((128, 128), 1)
        incl = row >= col
        ltri = incl.astype(F32)
        for pr in range(4):
            sl = slice(128 * pr, 128 * pr + 128)
            x2, dt2 = x_ref[:, sl], dt_ref[:, sl]
            _, acs2, lam2, dec2, xd2 = _ssd_pair(x2, dt2, al_ref[:, sl], ltri)
            y2, st = None, []
            for hh in range(2):
                lmask, _, mmat, _, bd = _ssd_head(hh, acs2, dec2, cbm, bm, incl, col)
                t = mm(mmat, jnp.where(lmask, xd2, 0.0))
                y2 = t if y2 is None else y2 + t
                st.append(mm_tn(xd2, bd))
            hprev = h_scr[pr]
            hst_ref[pr] = hprev
            y_ref[:, sl] = y2 + lam2 * mm_nt(cm, hprev)
            h_scr[pr] = _lam_rows(lam2, row) * hprev + jnp.where(row < 64, st[0], st[1])

    return pl.pallas_call(
        body, name="ssd_fwd", grid=(2, nc),
        in_specs=[pl.BlockSpec((SSM_L, 512), lambda g, c: (c, g)),
                  pl.BlockSpec((SSM_L, 128), lambda g, c: (c, g)),
                  pl.BlockSpec((SSM_L, 128), lambda g, c: (c, 2 + g)),
                  pl.BlockSpec((SSM_L, 512), lambda g, c: (c, g)),
                  pl.BlockSpec((1, 512), lambda g, c: (0, g))],
        out_specs=[pl.BlockSpec((SSM_L, 512), lambda g, c: (c, g)),
                   pl.BlockSpec((None, None, 4, 128, 128), lambda g, c: (g, c, 0, 0, 0))],
        out_shape=[jax.ShapeDtypeStruct((T, D), F32), jax.ShapeDtypeStruct((2, nc, 4, 128, 128), F32)],
        scratch_shapes=[pltpu.VMEM((4, 128, 128), F32)],
        compiler_params=_cparams(("parallel", "arbitrary")),
    )(xs, bc, bc, dt_x, alog_x)


def ssd_bwd(xs, bc, dt_x, alog_x, dy, hstates):
    T = xs.shape[0]
    nc = T // SSM_L
    L = SSM_L

    def body(x_ref, b_ref, c_ref, dt_ref, al_ref, dy_ref, hst_ref,
             dx_ref, db_ref, dc_ref, ddt_ref, dal_ref, dh_scr):
        first = pl.program_id(1) == 0

        @pl.when(first)
        def _():
            dh_scr[...] = jnp.zeros_like(dh_scr)

        bm, cm = b_ref[...], c_ref[...]
        cbm = mm_nt(cm, bm)
        row, col = _iota((L, L), 0), _iota((L, L), 1)
        incl = row >= col
        ltri, utri = incl.astype(F32), (row <= col).astype(F32)
        bd64 = ((row // 64) == (col // 64)).astype(F32)
        ones = jnp.ones((L, L), F32)
        dcb = jnp.zeros((L, L), F32)
        dbm = jnp.zeros((L, SSM_N), F32)
        dcm = jnp.zeros((L, SSM_N), F32)
        for pr in range(4):
            sl = slice(128 * pr, 128 * pr + 128)
            x2, dt2, dy2 = x_ref[:, sl], dt_ref[:, sl], dy_ref[:, sl]
            a2n, acs2, lam2, dec2, xd2 = _ssd_pair(x2, dt2, al_ref[:, sl], ltri)
            hprev, dhn = hst_ref[pr], dh_scr[pr]
            zmat = mm_nt(cm, hprev)
            dz = lam2 * dy2
            dacs2 = mh(dy2 * lam2 * zmat, bd64)
            dcm = dcm + mm(dz, hprev)
            lam_rows = _lam_rows(lam2, row)
            q_rows = jnp.sum(dhn * hprev * lam_rows, axis=-1, keepdims=True)
            dlam_b = mh_tn(jnp.broadcast_to(q_rows, (L, L)), bd64)
            dxd2 = jnp.zeros((L, 128), F32)
            for hh in range(2):
                lmask, sg, mmat, dec_col, bd = _ssd_head(hh, acs2, dec2, cbm, bm, incl, col)
                dm = jnp.where(incl, mm_nt(jnp.where(lmask, dy2, 0.0), xd2), 0.0)
                dcb = dcb + dm * sg
                e = dm * mmat
                dxd2 = dxd2 + jnp.where(lmask, mm_tn(mmat, dy2) + mm_nt(bd, dhn), 0.0)
                dbd = mm(jnp.where(lmask, xd2, 0.0), dhn)
                dbm = dbm + dec_col * dbd
                t = jnp.sum(dbd * bd, axis=-1, keepdims=True)
                add_last = _sum_all(t) + dlam_b[0:1, 64 * hh:64 * hh + 1]
                dac = mh(e, ones) - mh_tn(e, ones) - t + jnp.where(row == L - 1, add_last, 0.0)
                dacs2 = dacs2 + jnp.where(lmask, dac, 0.0)
            dh_scr[pr] = mm_tn(dz, cm) + lam_rows * dhn
            da2 = mh(utri, dacs2)
            dx_ref[:, sl] = dt2 * dxd2
            ddt_ref[:, sl] = mh(dxd2 * x2, bd64) + da2 * a2n
            dal = _rows(da2 * dt2) * a2n

            @pl.when(first)
            def _(dal=dal, sl=sl):
                dal_ref[:, sl] = dal

            @pl.when(jnp.logical_not(first))
            def _(dal=dal, sl=sl):
                dal_ref[:, sl] += dal

        db_ref[...] = dbm + mm_tn(dcb, cm)
        dc_ref[...] = dcm + mm(dcb, bm)

    rv = lambda g, c: (nc - 1 - c, g)
    return pl.pallas_call(
        body, name="ssd_bwd", grid=(2, nc),
        in_specs=[pl.BlockSpec((L, 512), rv),
                  pl.BlockSpec((L, 128), rv),
                  pl.BlockSpec((L, 128), lambda g, c: (nc - 1 - c, 2 + g)),
                  pl.BlockSpec((L, 512), rv),
                  pl.BlockSpec((1, 512), lambda g, c: (0, g)),
                  pl.BlockSpec((L, 512), rv),
                  pl.BlockSpec((None, None, 4, 128, 128), lambda g, c: (g, nc - 1 - c, 0, 0, 0))],
        out_specs=[pl.BlockSpec((L, 512), rv), pl.BlockSpec((L, 128), rv), pl.BlockSpec((L, 128), rv),
                   pl.BlockSpec((L, 512), rv), pl.BlockSpec((1, 512), lambda g, c: (0, g))],
        out_shape=[jax.ShapeDtypeStruct((T, D), F32), jax.ShapeDtypeStruct((T, 256), F32),
                   jax.ShapeDtypeStruct((T, 256), F32), jax.ShapeDtypeStruct((T, D), F32),
                   jax.ShapeDtypeStruct((1, D), F32)],
        scratch_shapes=[pltpu.VMEM((4, 128, 128), F32)],
        compiler_params=_cparams(("parallel", "arbitrary")),
    )(xs, bc, bc, dt_x, alog_x, dy, hstates)


def _pos():
    return lax.axis_index("x"), lax.axis_index("y"), lax.axis_index("c")


def _other_chips(x, y):
    return [(1 - x, y), (x, 1 - y), (1 - x, 1 - y)]


def _rcopy(src, dst, ssem, rsem, dev):
    return pltpu.make_async_remote_copy(src_ref=src, dst_ref=dst, send_sem=ssem, recv_sem=rsem,
                                        device_id=dev, device_id_type=MESH)


def _comm_call(body, name, out_shape, n_in, scratch):
    return pl.pallas_call(
        body, name=name, out_shape=out_shape, in_specs=[ANY] * n_in,
        out_specs=[ANY] * len(out_shape) if isinstance(out_shape, (list, tuple)) else ANY,
        scratch_shapes=scratch,
        compiler_params=pltpu.CompilerParams(has_side_effects=True),
    )


def ag_chips(name, shard):
    rr, cc = shard.shape
    h = rr // 2

    def body(x_ref, out_ref, ssem, rsem, lsem):
        x, y, c = _pos()
        me_s = 2 * x + y
        mine = pl.ds(pl.multiple_of(c * h, 8), h)
        theirs = pl.ds(pl.multiple_of((1 - c) * h, 8), h)
        chips = _other_chips(x, y)
        local = pltpu.make_async_copy(x_ref, out_ref.at[me_s], lsem)
        local.start()
        first = [_rcopy(x_ref.at[mine], out_ref.at[me_s, mine], ssem.at[j], rsem.at[j], (cx, cy, c))
                 for j, (cx, cy) in enumerate(chips)]
        for cp in first:
            cp.start()
        passed = []
        for j, (cx, cy) in enumerate(chips):
            blk = out_ref.at[2 * cx + cy, mine]
            _rcopy(blk, blk, ssem.at[j], rsem.at[j], (cx, cy, c)).wait_recv()
            cp = _rcopy(blk, blk, ssem.at[3 + j], rsem.at[3 + j], (x, y, 1 - c))
            cp.start()
            passed.append(cp)
        for j, (cx, cy) in enumerate(chips):
            blk = out_ref.at[2 * cx + cy, theirs]
            _rcopy(blk, blk, ssem.at[3 + j], rsem.at[3 + j], (x, y, 1 - c)).wait_recv()
        for cp in first + passed:
            cp.wait_send()
        local.wait()

    return _comm_call(body, name, jax.ShapeDtypeStruct((4, rr, cc), shard.dtype), 1,
                      [pltpu.SemaphoreType.DMA((6,)), pltpu.SemaphoreType.DMA((6,)), pltpu.SemaphoreType.DMA])(shard)


def rs_pair(name, g):
    _, rr, cc = g.shape
    h = rr // 2

    def body(g_ref, mine_ref, recv_ref, ssem, rsem, lsem):
        x, y, c = _pos()
        mine = pl.ds(pl.multiple_of(c * h, 8), h)
        theirs = pl.ds(pl.multiple_of((1 - c) * h, 8), h)
        local = pltpu.make_async_copy(g_ref.at[:, mine, :], mine_ref, lsem)
        local.start()
        cp = _rcopy(g_ref.at[:, theirs, :], recv_ref, ssem, rsem, (x, y, 1 - c))
        cp.start()
        cp.wait()
        local.wait()

    shp = jax.ShapeDtypeStruct((4, h, cc), g.dtype)
    return _comm_call(body, name, [shp, shp], 1,
                      [pltpu.SemaphoreType.DMA, pltpu.SemaphoreType.DMA, pltpu.SemaphoreType.DMA])(g)


def rs_chips(name, p):
    _, h, cc = p.shape

    def body(p_ref, buf_ref, ssem, rsem, lsem):
        x, y, c = _pos()
        me_s = 2 * x + y
        chips = _other_chips(x, y)
        local = pltpu.make_async_copy(p_ref.at[me_s], buf_ref.at[0], lsem)
        local.start()
        sends = [_rcopy(p_ref.at[2 * cx + cy], buf_ref.at[1 + j], ssem.at[j], rsem.at[j], (cx, cy, c))
                 for j, (cx, cy) in enumerate(chips)]
        for cp in sends:
            cp.start()
        for cp in sends:
            cp.wait()
        local.wait()

    return _comm_call(body, name, jax.ShapeDtypeStruct((4, h, cc), p.dtype), 1,
                      [pltpu.SemaphoreType.DMA((3,)), pltpu.SemaphoreType.DMA((3,)), pltpu.SemaphoreType.DMA])(p)


def rs_join(name, half):
    h, cc = half.shape

    def body(h_ref, out_ref, ssem, rsem, lsem):
        x, y, c = _pos()
        mine = pl.ds(pl.multiple_of(c * h, 8), h)
        local = pltpu.make_async_copy(h_ref, out_ref.at[mine], lsem)
        local.start()
        cp = _rcopy(h_ref, out_ref.at[mine], ssem, rsem, (x, y, 1 - c))
        cp.start()
        cp.wait()
        local.wait()

    return _comm_call(body, name, jax.ShapeDtypeStruct((2 * h, cc), half.dtype), 1,
                      [pltpu.SemaphoreType.DMA, pltpu.SemaphoreType.DMA, pltpu.SemaphoreType.DMA])(half)


def reduce_scatter(tag, g, tb):
    _, rr, cc = g.shape
    h = rr // 2
    mine, recv = rs_pair(tag + "_pair", g)
    part = rowwise(add2_fn, tag + "_add", 4 * h, tb, [R(mine.reshape(4 * h, cc)), R(recv.reshape(4 * h, cc))],
                   [], [(cc, F32)])[0]
    buf = rs_chips(tag + "_chips", part.reshape(4, h, cc)).reshape(4 * h, cc)
    nb = h // tb
    red = rowwise(sum4_fn, tag + "_sum", h, tb, [R(buf, off=k * nb) for k in range(4)], [], [(cc, F32)])[0]
    return rs_join(tag + "_join", red)


def all_reduce_small(name, v):
    rows = v.shape[0]

    def body(v_ref, out_ref, buf, ssem, rsem):
        x, y, c = _pos()
        me = 4 * x + 2 * y + c
        buf[me] = v_ref[...]
        cps = []
        for k in range(1, 8):
            dev = (x ^ (k >> 2), y ^ ((k >> 1) & 1), c ^ (k & 1))
            cp = _rcopy(v_ref, buf.at[me], ssem.at[k - 1], rsem.at[k - 1], dev)
            cp.start()
            cps.append(cp)
        for cp in cps:
            cp.wait()
        acc = buf[0]
        for d in range(1, 8):
            acc = acc + buf[d]
        out_ref[...] = acc

    return pl.pallas_call(
        body, name=name, out_shape=jax.ShapeDtypeStruct((rows, 128), F32),
        in_specs=[pl.BlockSpec(memory_space=pltpu.VMEM)], out_specs=pl.BlockSpec(memory_space=pltpu.VMEM),
        scratch_shapes=[pltpu.VMEM((8, rows, 128), F32), pltpu.SemaphoreType.DMA((7,)), pltpu.SemaphoreType.DMA((7,))],
        compiler_params=pltpu.CompilerParams(has_side_effects=True),
    )(v)


def _pack(vs):
    parts, offs, r = [], [], 0
    for v in vs:
        n = v.size
        nr = -(-n // 128)
        parts.append(jnp.pad(v.reshape(-1).astype(F32), (0, nr * 128 - n)))
        offs.append((r, n))
        r += nr
    pad = (-r) % 8
    if pad:
        parts.append(jnp.zeros((pad * 128,), F32))
    return jnp.concatenate(parts).reshape(r + pad, 128), offs


def _unpack(buf, offs, shapes):
    flat = buf.reshape(-1)
    return [flat[r * 128:r * 128 + n].reshape(s) for (r, n), s in zip(offs, shapes)]


def _sel(rows, cols, pairs):
    m = np.zeros((rows, cols), np.float32)
    for r, c in pairs:
        m[r, c] = 1.0
    return jnp.asarray(m)


def _pad_win(w):
    z = jnp.zeros((w.shape[0], 112), w.dtype)
    return jnp.concatenate([w[:, :4096], w[:, 4112:6672], w[:, 4096:4112], z, w[:, 6672:6688], z], axis=1)


def _unpad_win(wp):
    return jnp.concatenate([wp[:, :4096], wp[:, 6656:6672], wp[:, 4096:6656], wp[:, 6784:6800]], axis=1)


def kernel(x, mem, norm1_w, w_in, gdn_conv_w, gdn_a_log, gdn_dt_bias, gdn_norm_w, ssm_conv_w, ssm_conv_b, ssm_a_log, ssm_dt_bias, ssm_d, ssm_norm_w, w_out, norm2_w, mem_norm_w, wq_mem, wk_mem, wv_mem, wo_mem, norm3_w, w_up, w_down, final_norm_w, loss_target, m_norm1_w, m_w_in, m_gdn_conv_w, m_gdn_a_log, m_gdn_dt_bias, m_gdn_norm_w, m_ssm_conv_w, m_ssm_conv_b, m_ssm_a_log, m_ssm_dt_bias, m_ssm_d, m_ssm_norm_w, m_w_out, m_norm2_w, m_mem_norm_w, m_wq_mem, m_wk_mem, m_wv_mem, m_wo_mem, m_norm3_w, m_w_up, m_w_down, m_final_norm_w, v_norm1_w, v_w_in, v_gdn_conv_w, v_gdn_a_log, v_gdn_dt_bias, v_gdn_norm_w, v_ssm_conv_w, v_ssm_conv_b, v_ssm_a_log, v_ssm_dt_bias, v_ssm_d, v_ssm_norm_w, v_w_out, v_norm2_w, v_mem_norm_w, v_wq_mem, v_wk_mem, v_wv_mem, v_wo_mem, v_norm3_w, v_w_up, v_w_down, v_final_norm_w):
    T, M = x.shape[1], mem.shape[1]
    xi, yi, ci = _pos()
    s_me = 2 * xi + yi
    x0, mem0, tgt = x[0], mem[0], loss_target[0]
    tb = min(256, T)
    row = lambda v: v.reshape(1, -1)

    win_g = ag_chips("ag_win", w_in.astype(BF16))
    rest_g = ag_chips("ag_rest", jnp.concatenate([w_out, wq_mem, wk_mem, wv_mem, wo_mem, w_up, w_down],
                                                 axis=0).astype(BF16))
    w_in_p = _pad_win(win_g.transpose(1, 0, 2).reshape(D, IN_COLS))
    wout_f = rest_g[:, 0:512].reshape(2 * D, D)
    wq_f, wk_f, wv_f, wo_f = (rest_g[:, 512 + 256 * k:768 + 256 * k].reshape(D, D) for k in range(4))
    wup_f = rest_g[:, 1536:2560].transpose(1, 0, 2).reshape(D, D_FF)
    wdown_f = rest_g[:, 2560:3584].reshape(D_FF, D)
    keep = (ci == 0).astype(F32)
    gcw_z = lax.dynamic_update_slice(jnp.zeros((4, 3 * D), F32), gdn_conv_w * keep, (0, s_me * 768))
    scw_z = lax.dynamic_update_slice(jnp.zeros((4, 1536), F32), ssm_conv_w * keep, (0, s_me * 384))
    cbuf, coffs = _pack([gcw_z, scw_z])
    gcw, scw = _unpack(all_reduce_small("ar_convw", cbuf), coffs, [(4, 3 * D), (4, 1536)])
    scw_x, scw_bc = scw[:, :D], scw[:, D:]
    scb_x, scb_bc = row(ssm_conv_b[:D]), row(ssm_conv_b[D:])

    galog_x, gdtb_x = row(jnp.repeat(gdn_a_log, 128)), row(jnp.repeat(gdn_dt_bias, 128))
    salog_x, sdtb_x, sd_x = (row(jnp.repeat(v, 64)) for v in (ssm_a_log, ssm_dt_bias, ssm_d))
    eb = _sel(128, D, [(h, 128 * h + l) for h in range(8) for l in range(128)])
    ea = _sel(128, D, [(8 + h, 128 * h + l) for h in range(8) for l in range(128)])
    e16 = _sel(128, D, [(h, 64 * h + l) for h in range(16) for l in range(64)])
    pb = _sel(D, 128, [(128 * h, h) for h in range(8)])
    pa = _sel(D, 128, [(128 * h, 8 + h) for h in range(8)])
    p16 = _sel(D, 128, [(64 * h, h) for h in range(16)])

    h1 = rowwise(rms_fwd_fn, "rms1", T, tb, [R(x0)], [row(norm1_w)], [(D, BF16)])[0]
    p = matmul("mm_in", h1, w_in_p, "nn", 1024, 768, 1024, [F32])[0]
    gp_ins = [R(p, 3 * D, CB_QKV, "prev"), R(p, 128, CB_BA)]
    gp_full = [gcw, galog_x, gdtb_x, eb, ea]
    qn, kn, vv, g_x, beta_x = rowwise(gdn_prep_fn, "gdn_prep", T, tb, gp_ins, gp_full, [(D, F32)] * 5)
    gtb = min(256, T)
    o_gdn, s_states = gdn_fwd(qn, kn, vv, g_x, beta_x, gtb)
    gnw = row(gdn_norm_w)
    oa = rowwise(gdn_post_fn, "gdn_post", T, tb, [R(o_gdn), R(p, D, CB_Z)], [gnw], [(D, BF16)])[0]
    sp_ins = [R(p, D, CB_XS, "prev"), R(p, 512, CB_BC, "prev"), R(p, 128, CB_DT)]
    sp_full = [scw_x, scw_bc, scb_x, scb_bc, sdtb_x, e16]
    xs, bc, dt_x = rowwise(ssd_prep_fn, "ssd_prep", T, tb, sp_ins, sp_full, [(D, F32), (512, F32), (D, F32)])
    y_ssd, h_states = ssd_fwd(xs, bc, dt_x, salog_x)
    snw = row(ssm_norm_w)
    ob = rowwise(ssd_post_fn, "ssd_post", T, tb, [R(y_ssd), R(xs), R(p, D, CB_ZS)], [sd_x, snw], [(D, BF16)])[0]
    x1a = matmul("mm_out_a", oa, wout_f[:D], "nn", 1024, 1024, 1024, [F32], _epi_res, [x0])[0]
    x1 = matmul("mm_out_b", ob, wout_f[D:], "nn", 1024, 1024, 1024, [F32], _epi_res, [x1a])[0]

    mn = rowwise(rms_fwd_fn, "rms_mem", M, M, [R(mem0)], [row(mem_norm_w)], [(D, BF16)])[0]
    km = matmul("mm_k", mn, wk_f, "nn", 256, 1024, 1024, [BF16])[0]
    vm = matmul("mm_v", mn, wv_f, "nn", 256, 1024, 1024, [BF16])[0]
    h2 = rowwise(rms_fwd_fn, "rms2", T, tb, [R(x1)], [row(norm2_w)], [(D, BF16)])[0]
    qm = matmul("mm_q", h2, wq_f, "nn", 1024, 1024, 1024, [BF16])[0]
    ao = rowwise(attn_fn, "attn", T, tb, [R(qm)], [km, vm], [(D, BF16)])[0]
    x2 = matmul("mm_o", ao, wo_f, "nn", 1024, 1024, 1024, [F32], _epi_res, [x1])[0]

    h3 = rowwise(rms_fwd_fn, "rms3", T, tb, [R(x2)], [row(norm3_w)], [(D, BF16)])[0]
    u, act = matmul("mm_up", h3, wup_f, "nn", 1024, 1024, 1024, [BF16, BF16], _epi_relu2)
    x3 = matmul("mm_down", act, wdown_f, "nn", 1024, 1024, 1024, [F32], _epi_res, [x2])[0]

    dx3, dx3b, loss_lane, g_final = rowwise(final_fn, "final", T, tb, [R(x3), R(tgt)], [row(final_norm_w)],
                                            [(D, F32), (D, BF16)], [(1, D), (1, D)])
    loss = lax.psum(0.5 / D * jnp.sum(loss_lane), ("x", "y", "c"))

    dup = matmul("mm_dact", dx3b, wdown_f, "nt", 1024, 1024, 1024, [BF16], _epi_dup, [u])[0]
    g_wdown = matmul("mm_gdown", act, dx3b, "tn", 1024, 1024, 1024, [F32])[0]
    dh3 = matmul("mm_dh3", dup, wup_f, "nt", 1024, 1024, 1024, [F32])[0]
    g_wup = matmul("mm_gup", h3, dup, "tn", 1024, 1024, 1024, [F32])[0]
    dx2, dx2b, g_n3 = rowwise(rms_bwd_fn, "rms3_bwd", T, tb, [R(x2), R(dh3), R(dx3)], [row(norm3_w)],
                              [(D, F32), (D, BF16)], [(1, D)])
    dao = matmul("mm_dao", dx2b, wo_f, "nt", 1024, 1024, 1024, [F32])[0]
    g_wo = matmul("mm_gwo", ao, dx2b, "tn", 1024, 1024, 1024, [F32])[0]
    dqm, dkm, dvm = rowwise(attn_bwd_fn, "attn_bwd", T, tb, [R(qm), R(dao)], [km, vm], [(D, BF16)],
                            [(M, D), (M, D)])
    dh2 = matmul("mm_dh2", dqm, wq_f, "nt", 1024, 1024, 1024, [F32])[0]
    g_wq = matmul("mm_gwq", h2, dqm, "tn", 1024, 1024, 1024, [F32])[0]
    g_wk = matmul("mm_gwk", mn, dkm, "tn", 1024, 1024, 256, [F32])[0]
    g_wv = matmul("mm_gwv", mn, dvm, "tn", 1024, 1024, 256, [F32])[0]
    dmn_k = matmul("mm_dmk", dkm, wk_f, "nt", 256, 1024, 1024, [F32])[0]
    dmn = matmul("mm_dmv", dvm, wv_f, "nt", 256, 1024, 1024, [F32], _epi_res, [dmn_k])[0]
    g_nmem = rowwise(rms_bwd_w_fn, "rmsmem_bwd", M, M, [R(mem0), R(dmn)], [row(mem_norm_w)], [], [(1, D)])[0]
    dx1, dx1b, g_n2 = rowwise(rms_bwd_fn, "rms2_bwd", T, tb, [R(x1), R(dh2), R(dx2)], [row(norm2_w)],
                              [(D, F32), (D, BF16)], [(1, D)])
    doa = matmul("mm_doa", dx1b, wout_f[:D], "nt", 1024, 1024, 1024, [F32])[0]
    dob = matmul("mm_dob", dx1b, wout_f[D:], "nt", 1024, 1024, 1024, [F32])[0]
    g_wout_a = matmul("mm_gwout_a", oa, dx1b, "tn", 1024, 1024, 1024, [F32])[0]
    g_wout_b = matmul("mm_gwout_b", ob, dx1b, "tn", 1024, 1024, 1024, [F32])[0]

    dy_ssd, dxs_dir, dzs, g_snw, g_sd_lane = rowwise(
        ssd_post_bwd_fn, "ssd_post_bwd", T, tb, [R(y_ssd), R(xs), R(p, D, CB_ZS), R(dob)], [sd_x, snw],
        [(D, F32), (D, F32), (D, BF16)], [(1, D), (1, D)])
    dxs_scan, db_s, dc_s, ddt_x, g_salog_lane = ssd_bwd(xs, bc, dt_x, salog_x, dy_ssd, h_states)
    dbc = jnp.concatenate([db_s, dc_s], axis=1)
    dxs = rowwise(add2_fn, "ssd_dxs", T, tb, [R(dxs_scan), R(dxs_dir)], [], [(D, F32)])[0]
    spb = rowwise(ssd_prep_bwd_fn, "ssd_prep_bwd", T, tb, sp_ins + [R(dxs), R(dbc), R(ddt_x)], sp_full + [p16],
                  [(D, F32), (512, F32), (128, BF16)], [(1, D)] * 4 + [(1, 512)] * 4 + [(1, D), (1, 512), (1, D)])
    dyc_x, dyc_bc, ddt_blk = spb[:3]
    g_scw = jnp.concatenate([jnp.concatenate(spb[3:7], axis=0), jnp.concatenate(spb[7:11], axis=0)], axis=1)
    g_scb = jnp.concatenate([spb[11], spb[12]], axis=1).reshape(-1)
    g_sdtb_lane = spb[13]
    dp_xs = rowwise(conv_bwd_fn, "conv_bwd_x", T, tb, [R(dyc_x, halo="next")], [scw_x], [(D, BF16)])[0]
    dp_bc = rowwise(conv_bwd_fn, "conv_bwd_bc", T, tb, [R(dyc_bc, halo="next")], [scw_bc], [(512, BF16)])[0]

    do_gdn, dz, g_gnw = rowwise(gdn_post_bwd_fn, "gdn_post_bwd", T, tb, [R(o_gdn), R(p, D, CB_Z), R(doa)], [gnw],
                                [(D, F32), (D, BF16)], [(1, 128)])
    dqn, dkn, dvv, dg_x, dbeta_x = gdn_bwd(qn, kn, vv, g_x, beta_x, do_gdn, s_states, gtb)
    gpb = rowwise(gdn_prep_bwd_fn, "gdn_prep_bwd", T, tb,
                  gp_ins + [R(dqn), R(dkn), R(dvv), R(dg_x), R(dbeta_x)], gp_full + [pb, pa],
                  [(3 * D, F32), (128, BF16)], [(1, 3 * D)] * 4 + [(1, D), (1, D)])
    dyc_qkv, dba = gpb[:2]
    g_gcw = jnp.concatenate(gpb[2:6], axis=0)
    g_galog_lane, g_gdtb_lane = gpb[6], gpb[7]
    dp_qkv = rowwise(conv_bwd_fn, "conv_bwd_qkv", T, tb, [R(dyc_qkv, halo="next")], [gcw], [(3 * D, BF16)])[0]

    dp = jnp.concatenate([dp_qkv, dz, dzs, dp_xs, dp_bc, dba, ddt_blk], axis=1)
    dh1 = matmul("mm_dh1", dp, w_in_p, "nt", 1024, 1024, 768, [F32])[0]
    g_win_p = matmul("mm_gwin", h1, dp, "tn", 1024, 768, 1024, [F32])[0]
    grad_x, _, g_n1 = rowwise(rms_bwd_fn, "rms1_bwd", T, tb, [R(x0), R(dh1), R(dx1)], [row(norm1_w)],
                              [(D, F32), (D, BF16)], [(1, D)])

    small = [g_n1, g_galog_lane[0, ::128], g_gdtb_lane[0, ::128], g_gnw, g_scb, g_salog_lane[0, ::64],
             g_sdtb_lane[0, ::64], g_sd_lane.reshape(16, 64).sum(axis=1), g_snw, g_n2, g_nmem, g_n3, g_final,
             g_gcw, g_scw]
    sshapes = [(D,), (8,), (8,), (128,), (1536,), (16,), (16,), (16,), (D,), (D,), (D,), (D,), (D,),
               (4, 3 * D), (4, 1536)]
    sbuf, soffs = _pack(small)
    sg = _unpack(all_reduce_small("ar_grads", sbuf), soffs, sshapes)
    (gr_n1, gr_galog, gr_gdtb, gr_gnw, gr_scb, gr_salog, gr_sdtb, gr_sd, gr_snw, gr_n2, gr_nmem, gr_n3,
     gr_final, gr_gcw_full, gr_scw_full) = sg
    gr_gcw = lax.dynamic_slice(gr_gcw_full, (0, s_me * 768), (4, 768))
    gr_scw = lax.dynamic_slice(gr_scw_full, (0, s_me * 384), (4, 384))

    g_win = _unpad_win(g_win_p).reshape(D, 4, IN_COLS // 4).transpose(1, 0, 2)
    gr_win = reduce_scatter("rs_win", g_win, 256)
    g_rest = jnp.concatenate([
        jnp.concatenate([g_wout_a, g_wout_b], axis=0).reshape(4, 512, D),
        g_wq.reshape(4, 256, D), g_wk.reshape(4, 256, D), g_wv.reshape(4, 256, D), g_wo.reshape(4, 256, D),
        g_wup.reshape(D, 4, D).transpose(1, 0, 2), g_wdown.reshape(4, D, D)], axis=1)
    gr_rest = reduce_scatter("rs_rest", g_rest, 256)
    gr_wout, gr_wq, gr_wk, gr_wv, gr_wo = (gr_rest[0:512], gr_rest[512:768], gr_rest[768:1024],
                                           gr_rest[1024:1280], gr_rest[1280:1536])
    gr_wup, gr_wdown = gr_rest[1536:2560], gr_rest[2560:3584]

    def adam_big(name, w, g, m, v, tbr):
        return rowwise(adamw_fn, name, w.shape[0], tbr, [R(w), R(g), R(m), R(v)], [], [(w.shape[1], F32)] * 3)

    big = {
        "w_in": adam_big("adam_win", w_in, gr_win, m_w_in, v_w_in, 256),
        "w_out": adam_big("adam_wout", w_out, gr_wout, m_w_out, v_w_out, 256),
        "wq_mem": adam_big("adam_wq", wq_mem, gr_wq, m_wq_mem, v_wq_mem, 256),
        "wk_mem": adam_big("adam_wk", wk_mem, gr_wk, m_wk_mem, v_wk_mem, 256),
        "wv_mem": adam_big("adam_wv", wv_mem, gr_wv, m_wv_mem, v_wv_mem, 256),
        "wo_mem": adam_big("adam_wo", wo_mem, gr_wo, m_wo_mem, v_wo_mem, 256),
        "w_up": adam_big("adam_wup", w_up, gr_wup, m_w_up, v_w_up, 256),
        "w_down": adam_big("adam_wdown", w_down, gr_wdown, m_w_down, v_w_down, 256),
    }
    names_s = ["norm1_w", "gdn_conv_w", "gdn_a_log", "gdn_dt_bias", "gdn_norm_w", "ssm_conv_w", "ssm_conv_b",
               "ssm_a_log", "ssm_dt_bias", "ssm_d", "ssm_norm_w", "norm2_w", "mem_norm_w", "norm3_w", "final_norm_w"]
    w_s = [norm1_w, gdn_conv_w, gdn_a_log, gdn_dt_bias, gdn_norm_w, ssm_conv_w, ssm_conv_b, ssm_a_log, ssm_dt_bias,
           ssm_d, ssm_norm_w, norm2_w, mem_norm_w, norm3_w, final_norm_w]
    g_s = [gr_n1, gr_gcw, gr_galog, gr_gdtb, gr_gnw, gr_scw, gr_scb, gr_salog, gr_sdtb, gr_sd, gr_snw, gr_n2,
           gr_nmem, gr_n3, gr_final]
    m_s = [m_norm1_w, m_gdn_conv_w, m_gdn_a_log, m_gdn_dt_bias, m_gdn_norm_w, m_ssm_conv_w, m_ssm_conv_b, m_ssm_a_log,
           m_ssm_dt_bias, m_ssm_d, m_ssm_norm_w, m_norm2_w, m_mem_norm_w, m_norm3_w, m_final_norm_w]
    v_s = [v_norm1_w, v_gdn_conv_w, v_gdn_a_log, v_gdn_dt_bias, v_gdn_norm_w, v_ssm_conv_w, v_ssm_conv_b, v_ssm_a_log,
           v_ssm_dt_bias, v_ssm_d, v_ssm_norm_w, v_norm2_w, v_mem_norm_w, v_norm3_w, v_final_norm_w]
    shp_s = [w.shape for w in w_s]
    wb, aoffs = _pack(w_s)
    gb_, _ = _pack(g_s)
    mb, _ = _pack(m_s)
    vb_, _ = _pack(v_s)
    nrows = wb.shape[0]
    d_b, m_b, v_b = rowwise(adamw_fn, "adam_small", nrows, nrows, [R(wb), R(gb_), R(mb), R(vb_)], [], [(128, F32)] * 3)
    d_l, m_l, v_l = (_unpack(b, aoffs, shp_s) for b in (d_b, m_b, v_b))

    grads = {"w_in": gr_win, "w_out": gr_wout, "wq_mem": gr_wq, "wk_mem": gr_wk, "wv_mem": gr_wv, "wo_mem": gr_wo,
             "w_up": gr_wup, "w_down": gr_wdown}
    deltas, new_m, new_v = {}, {}, {}
    for n, (dd, mm_, vv_) in big.items():
        deltas[n], new_m[n], new_v[n] = dd, mm_, vv_
    for k, n in enumerate(names_s):
        grads[n] = g_s[k].reshape(shp_s[k])
        deltas[n], new_m[n], new_v[n] = d_l[k], m_l[k], v_l[k]
    order = ["norm1_w", "w_in", "gdn_conv_w", "gdn_a_log", "gdn_dt_bias", "gdn_norm_w", "ssm_conv_w", "ssm_conv_b",
             "ssm_a_log", "ssm_dt_bias", "ssm_d", "ssm_norm_w", "w_out", "norm2_w", "mem_norm_w", "wq_mem", "wk_mem",
             "wv_mem", "wo_mem", "norm3_w", "w_up", "w_down", "final_norm_w"]
    return (loss, grad_x[None], *[grads[n] for n in order], *[deltas[n] for n in order],
            *[new_m[n] for n in order], *[new_v[n] for n in order])
```

```python
import numpy as np
import jax
import jax.numpy as jnp
from jax import lax
from jax.experimental import pallas as pl
from jax.experimental.pallas import tpu as pltpu

F32, BF16 = jnp.float32, jnp.bfloat16
MESH = pl.DeviceIdType.MESH
ANY = pl.BlockSpec(memory_space=pl.ANY)

EPS = 1e-6
D = 1024
GDN_H, GDN_DK, GDN_C = 8, 128, 64
SSM_H, SSM_P, SSM_N, SSM_L = 16, 64, 128, 128
MEM_H, MEM_DH = 4, 256
D_FF = 4096
IN_COLS = 6688
CB_QKV, CB_Z, CB_ZS, CB_XS, CB_BC, CB_BA, CB_DT = 0, 3, 4, 5, 12, 52, 53
VMEM_LIMIT = 56 * 1024 * 1024
D2D_CHUNKS = 8
ICI_CHUNKS = 4

ADAM_LR, ADAM_B1, ADAM_B2, ADAM_EPS, ADAM_WD, ADAM_STEP = 0.001, 0.9, 0.999, 1e-08, 0.01, 10


def _dg(a, b, ca, cb):
    return lax.dot_general(a, b, (((ca,), (cb,)), ((), ())), preferred_element_type=F32)


def _bf(x):
    return x.astype(BF16)


def mm(a, b):
    return _dg(_bf(a), _bf(b), 1, 0)


def mm_nt(a, b):
    return _dg(_bf(a), _bf(b), 1, 1)


def mm_tn(a, b):
    return _dg(_bf(a), _bf(b), 0, 0)


def mm_sel(a, sel):
    hi = a.astype(BF16)
    r1 = a - hi.astype(F32)
    mid = r1.astype(BF16)
    lo = (r1 - mid.astype(F32)).astype(BF16)
    s = sel.astype(BF16)
    return _dg(hi, s, 1, 0) + (_dg(mid, s, 1, 0) + _dg(lo, s, 1, 0))


def mm3(a, b):
    ah, bh = a.astype(BF16), b.astype(BF16)
    al, bl = (a - ah.astype(F32)).astype(BF16), (b - bh.astype(F32)).astype(BF16)
    return _dg(ah, bh, 1, 0) + (_dg(ah, bl, 1, 0) + _dg(al, bh, 1, 0))


def _iota(shape, dim):
    return lax.broadcasted_iota(jnp.int32, shape, dim)


def _chunk_cumsum(x, c):
    pos = _iota(x.shape, 0) & (c - 1)
    s = 1
    while s < c:
        x = x + jnp.where(pos >= s, pltpu.roll(x, s, 0), 0.0)
        s *= 2
    return x


def _chunk_revcumsum(x, c):
    n = x.shape[0]
    pos = _iota(x.shape, 0) & (c - 1)
    s = 1
    while s < c:
        x = x + jnp.where(pos < c - s, pltpu.roll(x, n - s, 0), 0.0)
        s *= 2
    return x


def _sig(x):
    return 1.0 / (1.0 + jnp.exp(-x))


def _softplus(x):
    return jnp.maximum(x, 0.0) + jnp.log(1.0 + jnp.exp(-jnp.abs(x)))


def _rows(v):
    return jnp.sum(v, axis=0, keepdims=True)


def _lanes(v):
    return jnp.sum(v, axis=1, keepdims=True)


def _sum_all(v):
    return _rows(_lanes(v))


def _cparams(sem):
    return pltpu.CompilerParams(dimension_semantics=sem, vmem_limit_bytes=VMEM_LIMIT)


def rowwise(fn, name, T, tb, row_ins, full_ins, row_outs, acc_outs=()):
    nblk = T // tb
    assert nblk * tb == T
    in_specs, args = [], []
    for arr, w, cb, halo, off in row_ins:
        if halo == "col":
            in_specs.append(pl.BlockSpec((w, tb), lambda i: (0, i)))
            args.append(arr)
            continue
        in_specs.append(pl.BlockSpec((tb, w), lambda i, cb=cb, off=off: (i + off, cb)))
        args.append(arr)
        if halo == "prev":
            r = tb // 8
            in_specs.append(pl.BlockSpec((8, w), lambda i, cb=cb, r=r: (jnp.maximum(i * r - 1, 0), cb)))
            args.append(arr)
        elif halo == "next":
            r, last = tb // 8, T // 8 - 1
            in_specs.append(pl.BlockSpec((8, w), lambda i, cb=cb, r=r, last=last:
                                         (jnp.minimum((i + 1) * r, last), cb)))
            args.append(arr)
    for arr in full_ins:
        in_specs.append(pl.BlockSpec(arr.shape, lambda i, nd=arr.ndim: (0,) * nd))
        args.append(arr)
    n_in, n_ro = len(args), len(row_outs)
    out_shape, out_specs = [], []
    for w, dt in row_outs:
        if w < 0:
            out_shape.append(jax.ShapeDtypeStruct((-w, T), dt))
            out_specs.append(pl.BlockSpec((-w, tb), lambda i: (0, i)))
        else:
            out_shape.append(jax.ShapeDtypeStruct((T, w), dt))
            out_specs.append(pl.BlockSpec((tb, w), lambda i: (i, 0)))
    for shp in acc_outs:
        out_shape.append(jax.ShapeDtypeStruct(shp, F32))
        out_specs.append(pl.BlockSpec(shp, lambda i, nd=len(shp): (0,) * nd))

    def body(*refs):
        i = pl.program_id(0)
        vals = fn(i, nblk, *[r[...] for r in refs[:n_in]])
        outs = refs[n_in:]
        for ref, val in zip(outs[:n_ro], vals[:n_ro]):
            ref[...] = val.astype(ref.dtype)
        for ref, val in zip(outs[n_ro:], vals[n_ro:]):
            @pl.when(i == 0)
            def _(ref=ref, val=val):
                ref[...] = val

            @pl.when(i > 0)
            def _(ref=ref, val=val):
                ref[...] += val

    return pl.pallas_call(
        body, name=name, grid=(nblk,), in_specs=in_specs, out_specs=out_specs, out_shape=out_shape,
        compiler_params=_cparams(("arbitrary",) if acc_outs else ("parallel",)),
    )(*args)


def R(arr, w=None, cb=0, halo=None, off=0):
    return (arr, arr.shape[1] if w is None else w, cb, halo, off)


def RC(arr):
    return (arr, arr.shape[0], 0, "col", 0)


def matmul(name, a, b, form, tm, tn, tk, out_dtypes, epi=None, extras=()):
    if form == "nn":
        (M, K), N = a.shape, b.shape[1]
    elif form == "nt":
        (M, K), N = a.shape, b.shape[0]
    else:
        (K, M), N = a.shape, b.shape[1]
    tm, tn, tk = min(tm, M), min(tn, N), min(tk, K)
    assert M % tm == 0 and N % tn == 0 and K % tk == 0, (name, M, N, K, tm, tn, tk)
    if form == "nn":
        a_spec = pl.BlockSpec((tm, tk), lambda i, j, k: (i, k))
        b_spec = pl.BlockSpec((tk, tn), lambda i, j, k: (k, j))
        ca, cb = 1, 0
    elif form == "nt":
        a_spec = pl.BlockSpec((tm, tk), lambda i, j, k: (i, k))
        b_spec = pl.BlockSpec((tn, tk), lambda i, j, k: (j, k))
        ca, cb = 1, 1
    else:
        a_spec = pl.BlockSpec((tk, tm), lambda i, j, k: (k, i))
        b_spec = pl.BlockSpec((tk, tn), lambda i, j, k: (k, j))
        ca, cb = 0, 0
    nk, ne, no = K // tk, len(extras), len(out_dtypes)
    if epi is None:
        epi = lambda acc: (acc,)

    def body(a_ref, b_ref, *rest):
        ex, outs, acc = rest[:ne], rest[ne:ne + no], rest[ne + no]
        k = pl.program_id(2)

        @pl.when(k == 0)
        def _():
            acc[...] = jnp.zeros_like(acc)

        acc[...] += _dg(_bf(a_ref[...]), _bf(b_ref[...]), ca, cb)

        @pl.when(k == nk - 1)
        def _():
            vals = epi(acc[...], *[e[...] for e in ex])
            for r, v in zip(outs, vals):
                r[...] = v.astype(r.dtype)

    mn = pl.BlockSpec((tm, tn), lambda i, j, k: (i, j))
    return pl.pallas_call(
        body, name=name, grid=(M // tm, N // tn, nk),
        in_specs=[a_spec, b_spec] + [mn] * ne, out_specs=[mn] * no,
        out_shape=[jax.ShapeDtypeStruct((M, N), dt) for dt in out_dtypes],
        scratch_shapes=[pltpu.VMEM((tm, tn), F32)],
        compiler_params=_cparams(("parallel", "parallel", "arbitrary")),
    )(a, b, *extras)


def _epi_res(acc, res):
    return (res + acc,)


def _epi_relu2(acc):
    u = jnp.maximum(acc, 0.0)
    return (u, u * u)


def _epi_dup(acc, u):
    return (acc * 2.0 * u.astype(F32),)


def _conv(x, halo, w, i):
    halo = jnp.where(i == 0, 0.0, halo)
    xt = jnp.concatenate([halo, x], axis=0)
    shifted = [pltpu.roll(xt, 3 - k, 0)[8:, :] for k in range(3)] + [x]
    y = shifted[3] * w[3:4, :]
    for k in range(3):
        y = y + shifted[k] * w[k:k + 1, :]
    return y, shifted


def _l2n(x, scale):
    outs = []
    for h in range(x.shape[1] // 128):
        xh = x[:, 128 * h:128 * h + 128]
        outs.append(xh * (lax.rsqrt(jnp.sum(xh * xh, axis=-1, keepdims=True) + EPS) * scale))
    return jnp.concatenate(outs, axis=1)


def _l2n_bwd(x, dy, scale):
    outs = []
    for h in range(x.shape[1] // 128):
        xh, dh = x[:, 128 * h:128 * h + 128], dy[:, 128 * h:128 * h + 128] * scale
        r = lax.rsqrt(jnp.sum(xh * xh, axis=-1, keepdims=True) + EPS)
        outs.append(r * dh - xh * (r * r * r) * jnp.sum(xh * dh, axis=-1, keepdims=True))
    return jnp.concatenate(outs, axis=1)


def rms_fwd_fn(i, n, x, w):
    r = lax.rsqrt(jnp.mean(x * x, axis=-1, keepdims=True) + EPS)
    return (x * r * w,)


def rms_bwd_fn(i, n, x, dh, dres, w):
    r = lax.rsqrt(jnp.mean(x * x, axis=-1, keepdims=True) + EPS)
    g = dh * w
    dx = dres + r * g - x * (r * r * r) * jnp.mean(x * g, axis=-1, keepdims=True)
    return dx, dx, _rows(dh * x * r)


def rms_bwd_w_fn(i, n, x, dh, w):
    r = lax.rsqrt(jnp.mean(x * x, axis=-1, keepdims=True) + EPS)
    return (_rows(dh * x * r),)


def final_fn(i, n, x, tgt, w):
    r = lax.rsqrt(jnp.mean(x * x, axis=-1, keepdims=True) + EPS)
    xn = x * r
    e = xn * w - tgt
    dy = e * (1.0 / D)
    g = dy * w
    dx = r * g - x * (r * r * r) * jnp.mean(x * g, axis=-1, keepdims=True)
    return dx, dx, _rows(e * e), _rows(dy * xn)


def _gdn_gates(ba, alog_c, dtb_c):
    col = _iota(ba.shape, 1)
    amask = (col >= 8) & (col < 16)
    beta = jnp.where(col < 8, _sig(ba), 0.0)
    z = ba + dtb_c
    ea_ = jnp.exp(alog_c)
    return beta, z, ea_, jnp.where(amask, -ea_ * _softplus(z), 0.0), amask


def gdn_prep_fn(i, n, qkv, halo, ba, cw, alog_c, dtb_c, eb, ea):
    yc, _ = _conv(qkv, halo, cw, i)
    act = yc * _sig(yc)
    qn = _l2n(act[:, :D], GDN_DK ** -0.5)
    kn = _l2n(act[:, D:2 * D], 1.0)
    beta, _, _, g, _ = _gdn_gates(ba, alog_c, dtb_c)
    gcs = _chunk_cumsum(g, GDN_C)
    return qn, kn, act[:, 2 * D:], mm_sel(gcs, ea), mm_sel(beta, eb), jnp.transpose(gcs)[8:16, :]


def gdn_prep_bwd_fn(i, n, qkv, halo, ba, dqn, dkn, dv, dgcs_x, dbeta_x, dgcs_t, cw, alog_c, dtb_c, pb, pa):
    yc, shifted = _conv(qkv, halo, cw, i)
    sg = _sig(yc)
    act = yc * sg
    dq = _l2n_bwd(act[:, :D], dqn, GDN_DK ** -0.5)
    dk = _l2n_bwd(act[:, D:2 * D], dkn, 1.0)
    dyc = jnp.concatenate([dq, dk, dv], axis=1) * (sg * (1.0 + yc * (1.0 - sg)))
    dws = [_rows(dyc * shifted[k]) for k in range(4)]
    beta, z, ea_, g, amask = _gdn_gates(ba, alog_c, dtb_c)
    tbn = ba.shape[0]
    rowpart = jnp.transpose(jnp.concatenate([jnp.zeros((8, tbn), F32), dgcs_t, jnp.zeros((112, tbn), F32)], axis=0))
    dg = _chunk_revcumsum(mm_sel(dgcs_x, pa) - rowpart, GDN_C)
    draw = jnp.where(amask, dg * (-ea_) * _sig(z), 0.0)
    dba = draw + mm_sel(dbeta_x, pb) * beta * (1.0 - beta)
    return (dyc, dba, dws[0], dws[1], dws[2], dws[3], _rows(dg * g), _rows(draw))


def conv_bwd_fn(i, n, dyc, halo, w):
    halo = jnp.where(i == n - 1, 0.0, halo)
    tb = dyc.shape[0]
    xt = jnp.concatenate([dyc, halo], axis=0)
    dx = dyc * w[3:4, :]
    for k in range(3):
        dx = dx + pltpu.roll(xt, tb + 8 - (3 - k), 0)[:tb, :] * w[k:k + 1, :]
    return (dx,)


def gdn_post_fn(i, n, o, z, w):
    outs = []
    for h in range(GDN_H):
        oh, zh = o[:, 128 * h:128 * h + 128], z[:, 128 * h:128 * h + 128]
        r = lax.rsqrt(jnp.mean(oh * oh, axis=-1, keepdims=True) + EPS)
        outs.append(oh * r * w * (zh * _sig(zh)))
    return (jnp.concatenate(outs, axis=1),)


def gdn_post_bwd_fn(i, n, o, z, doa, w):
    dos, dzs, dw = [], [], None
    for h in range(GDN_H):
        sl = slice(128 * h, 128 * h + 128)
        oh, zh, dh = o[:, sl], z[:, sl], doa[:, sl]
        r = lax.rsqrt(jnp.mean(oh * oh, axis=-1, keepdims=True) + EPS)
        s = _sig(zh)
        dn = dh * (zh * s)
        dzs.append(dh * (oh * r * w) * (s * (1.0 + zh * (1.0 - s))))
        t = _rows(dn * oh * r)
        dw = t if dw is None else dw + t
        g = dn * w
        dos.append(r * g - oh * (r * r * r) * jnp.mean(oh * g, axis=-1, keepdims=True))
    return jnp.concatenate(dos, axis=1), jnp.concatenate(dzs, axis=1), dw


def _ssd_gates(dtblk, alog_c, dtb_c):
    hmask = _iota(dtblk.shape, 1) < SSM_H
    z = dtblk + dtb_c
    return jnp.where(hmask, _softplus(z), 0.0), -jnp.exp(alog_c), z, hmask


def ssd_prep_fn(i, n, xp, hx, bcp, hbc, dtblk, cwx, cwbc, cbx, cbbc, alog_c, dtb_c, e16):
    yx, _ = _conv(xp, hx, cwx, i)
    yx = yx + cbx
    ybc, _ = _conv(bcp, hbc, cwbc, i)
    ybc = ybc + cbbc
    dt, a_neg, _, _ = _ssd_gates(dtblk, alog_c, dtb_c)
    acs = _chunk_cumsum(dt * a_neg, SSM_L)
    return (yx * _sig(yx), ybc * _sig(ybc), mm_sel(dt, e16), mm_sel(acs, e16), jnp.transpose(acs)[0:SSM_H, :])


def ssd_prep_bwd_fn(i, n, xp, hx, bcp, hbc, dtblk, dxs, dbc, dgate, dacs_t, cwx, cwbc, cbx, cbbc, alog_c, dtb_c):
    yx, shx = _conv(xp, hx, cwx, i)
    yx = yx + cbx
    ybc, shbc = _conv(bcp, hbc, cwbc, i)
    ybc = ybc + cbbc
    sx, sbc = _sig(yx), _sig(ybc)
    dyx = dxs * (sx * (1.0 + yx * (1.0 - sx)))
    dybc = dbc * (sbc * (1.0 + ybc * (1.0 - sbc)))
    dwx = [_rows(dyx * shx[k]) for k in range(4)]
    dwbc = [_rows(dybc * shbc[k]) for k in range(4)]
    dt, a_neg, z, hmask = _ssd_gates(dtblk, alog_c, dtb_c)
    g0, g1 = dgate[:, :128], dgate[:, 128:]
    col = _iota(g0.shape, 1)
    lo, mid = col < 8, (col >= 8) & (col < 16)
    dacs_col = jnp.where(lo, g0, 0.0) + pltpu.roll(jnp.where(lo, g1, 0.0), 8, 1)
    ddt_dir = pltpu.roll(jnp.where(mid, g0, 0.0), 120, 1) + jnp.where(mid, g1, 0.0)
    tbn = dtblk.shape[0]
    rowpart = jnp.transpose(jnp.concatenate([dacs_t, jnp.zeros((128 - SSM_H, tbn), F32)], axis=0))
    da = _chunk_revcumsum(dacs_col - rowpart, SSM_L)
    draw = jnp.where(hmask, (ddt_dir + da * a_neg) * _sig(z), 0.0)
    return (dyx, dybc, draw, *dwx, *dwbc, _rows(dyx), _rows(dybc), _rows(da * dt * a_neg), _rows(draw))


def _ssd_gate(y, xs, zs, d_x):
    y2 = y + xs * d_x
    s = _sig(zs)
    return y2, s, y2 * (zs * s)


def ssd_post_fn(i, n, y, xs, zs, d_x, nw):
    _, _, yg = _ssd_gate(y, xs, zs, d_x)
    outs = []
    for g in range(2):
        v = yg[:, 512 * g:512 * g + 512]
        outs.append(v * lax.rsqrt(jnp.mean(v * v, axis=-1, keepdims=True) + EPS))
    return (jnp.concatenate(outs, axis=1) * nw,)


def ssd_post_bwd_fn(i, n, y, xs, zs, dob, d_x, nw):
    y2, s, yg = _ssd_gate(y, xs, zs, d_x)
    gfull = dob * nw
    dygs, dnw = [], []
    for g in range(2):
        sl = slice(512 * g, 512 * g + 512)
        v, gg = yg[:, sl], gfull[:, sl]
        r = lax.rsqrt(jnp.mean(v * v, axis=-1, keepdims=True) + EPS)
        dygs.append(r * gg - v * (r * r * r) * jnp.mean(v * gg, axis=-1, keepdims=True))
        dnw.append(_rows(dob[:, sl] * v * r))
    dyg = jnp.concatenate(dygs, axis=1)
    dy2 = dyg * (zs * s)
    dzs = dyg * y2 * (s * (1.0 + zs * (1.0 - s)))
    return dy2, dy2 * d_x, dzs, jnp.concatenate(dnw, axis=1), _rows(dy2 * xs)


def attn_fn(i, n, q, k, v):
    outs = []
    for h in range(MEM_H):
        sl = slice(MEM_DH * h, MEM_DH * h + MEM_DH)
        s = mm_nt(q[:, sl], k[:, sl]) * (MEM_DH ** -0.5)
        p = jnp.exp(s - jnp.max(s, axis=-1, keepdims=True))
        p = p / jnp.sum(p, axis=-1, keepdims=True)
        outs.append(mm(p, v[:, sl]))
    return (jnp.concatenate(outs, axis=1),)


def attn_bwd_fn(i, n, q, do, k, v):
    dqs, dks, dvs = [], [], []
    for h in range(MEM_H):
        sl = slice(MEM_DH * h, MEM_DH * h + MEM_DH)
        s = mm_nt(q[:, sl], k[:, sl]) * (MEM_DH ** -0.5)
        p = jnp.exp(s - jnp.max(s, axis=-1, keepdims=True))
        p = p / jnp.sum(p, axis=-1, keepdims=True)
        dvs.append(mm_tn(p, do[:, sl]))
        dp = mm_nt(do[:, sl], v[:, sl])
        ds = p * (dp - jnp.sum(dp * p, axis=-1, keepdims=True)) * (MEM_DH ** -0.5)
        dqs.append(mm(ds, k[:, sl]))
        dks.append(mm_tn(ds, q[:, sl]))
    return jnp.concatenate(dqs, axis=1), jnp.concatenate(dks, axis=1), jnp.concatenate(dvs, axis=1)


def add2_fn(i, n, a, b):
    return (a + b,)


def sum4_fn(i, n, a, b, c, d):
    return (((a.astype(F32) + b.astype(F32)) + c.astype(F32)) + d.astype(F32),)


def adamw_fn(i, n, w, g, m, v):
    m = ADAM_B1 * m + (1.0 - ADAM_B1) * g
    v = ADAM_B2 * v + (1.0 - ADAM_B2) * (g * g)
    m_hat = m / (1.0 - ADAM_B1 ** ADAM_STEP)
    v_hat = v / (1.0 - ADAM_B2 ** ADAM_STEP)
    delta = -ADAM_LR * (m_hat / (jnp.sqrt(v_hat) + ADAM_EPS) + ADAM_WD * w)
    return delta, m, v


def _gdn_stage1(q, k, v, gcs, grow, bb):
    C = GDN_C
    row, col = _iota((C, C), 0), _iota((C, C), 1)
    incl, strict = row >= col, row > col
    dmat = jnp.where(incl, jnp.exp(jnp.minimum(gcs[:, :C] - grow, 0.0)), 0.0)
    gam = jnp.exp(gcs)
    gl = gcs[C - 1:C, :]
    kb, vb = k * bb, v * bb
    kg = kb * gam
    lmat = jnp.where(strict, mm_nt(kb, k) * dmat, 0.0)
    pmat = jnp.where(incl, mm_nt(q, k) * dmat, 0.0)
    return dict(q=q, k=k, v=v, bb=bb, incl=incl, strict=strict, dmat=dmat, gam=gam, kb=kb, vb=vb, kg=kg,
                lmat=lmat, pmat=pmat, qd=q * gam, kdec=jnp.exp(gl - gcs), cd=jnp.exp(gl))


def _gdn_inverse(lmats):
    C = GDN_C
    eye = (_iota((C, C), 0) == _iota((C, C), 1)).astype(F32)
    xs = [-l for l in lmats]
    ts = [eye + x for x in xs]
    for _ in range(5):
        xs = [mm(x, x) for x in xs]
        ts = [t + mm(t, x) for t, x in zip(ts, xs)]
    res = [eye - mm3(eye + l, t) for l, t in zip(lmats, ts)]
    return [t + mm(t, r) for t, r in zip(ts, res)]


def gdn_fwd(qn, kn, v, gcs_x, beta_x, gcs_t, tb):
    T = qn.shape[0]
    nb, ncb, nc, C = T // tb, tb // GDN_C, T // GDN_C, GDN_C

    def body(q_ref, k_ref, v_ref, g_ref, b_ref, gt_ref, o_ref, st_ref, ti_ref, s_scr):
        @pl.when(pl.program_id(1) == 0)
        def _():
            s_scr[...] = jnp.zeros_like(s_scr)

        grow = gt_ref[...]
        st1 = []
        for c in range(ncb):
            sl = slice(C * c, C * (c + 1))
            st1.append(_gdn_stage1(q_ref[sl, :], k_ref[sl, :], v_ref[sl, :], g_ref[sl, :], grow[:, sl], b_ref[sl, :]))
        tinvs = _gdn_inverse([s["lmat"] for s in st1])
        us = [mm(t, s["vb"]) for t, s in zip(tinvs, st1)]
        ws = [mm(t, s["kg"]) for t, s in zip(tinvs, st1)]
        for c in range(ncb):
            sl = slice(C * c, C * (c + 1))
            lc = st1[c]
            ti_ref[sl, :] = tinvs[c]
            s = s_scr[...]
            st_ref[c] = s
            vn = us[c] - mm(ws[c], s)
            o_ref[sl, :] = mm(lc["qd"], s) + mm(lc["pmat"], vn)
            s_scr[...] = s * lc["cd"] + mm_tn(lc["k"] * lc["kdec"], vn)

    blk = pl.BlockSpec((tb, 128), lambda h, i: (i, h))
    return pl.pallas_call(
        body, name="gdn_fwd", grid=(GDN_H, nb),
        in_specs=[blk] * 5 + [pl.BlockSpec((None, 1, tb), lambda h, i: (h, 0, i))],
        out_specs=[blk, pl.BlockSpec((None, ncb, 128, 128), lambda h, i: (h, i, 0, 0)),
                   pl.BlockSpec((None, tb, C), lambda h, i: (h, i, 0))],
        out_shape=[jax.ShapeDtypeStruct((T, D), F32), jax.ShapeDtypeStruct((GDN_H, nc, 128, 128), F32),
                   jax.ShapeDtypeStruct((GDN_H, T, C), F32)],
        scratch_shapes=[pltpu.VMEM((128, 128), F32)],
        compiler_params=_cparams(("parallel", "arbitrary")),
    )(qn, kn, v, gcs_x, beta_x, gcs_t)


def gdn_bwd(qn, kn, v, gcs_x, beta_x, gcs_t, do, states, tinv, tb):
    T = qn.shape[0]
    nb, ncb, C = T // tb, tb // GDN_C, GDN_C

    def body(q_ref, k_ref, v_ref, g_ref, b_ref, gt_ref, do_ref, st_ref, ti_ref,
             dq_ref, dk_ref, dv_ref, dgc_ref, db_ref, dgr_ref, ds_scr):
        @pl.when(pl.program_id(1) == 0)
        def _():
            ds_scr[...] = jnp.zeros_like(ds_scr)

        grow = gt_ref[...]
        lastrow = _iota((C, 1), 0) == C - 1
        pre = []
        for c in range(ncb):
            sl = slice(C * c, C * (c + 1))
            lc = _gdn_stage1(q_ref[sl, :], k_ref[sl, :], v_ref[sl, :], g_ref[sl, :], grow[:, sl], b_ref[sl, :])
            tinv_c, s, do_c = ti_ref[sl, :], st_ref[c], do_ref[sl, :]
            u, w = mm(tinv_c, lc["vb"]), mm(tinv_c, lc["kg"])
            vn = u - mm(w, s)
            lc.update(tinv=tinv_c, s=s, u=u, w=w, vn=vn, dqd=mm_nt(do_c, s),
                      dp=jnp.where(lc["incl"], mm_nt(do_c, vn), 0.0), ds_q=mm_tn(lc["qd"], do_c),
                      dvn_p=mm_tn(lc["pmat"], do_c), kd=lc["k"] * lc["kdec"])
            pre.append(lc)
        rows = [None] * ncb
        for c in reversed(range(ncb)):
            sl = slice(C * c, C * (c + 1))
            lc = pre[c]
            q, k, vv, bb, gam, kd, u, w, s, tinv_c = (lc[n] for n in ("q", "k", "v", "bb", "gam", "kd", "u", "w", "s", "tinv"))
            dsn = ds_scr[...]
            dvn = lc["dvn_p"] + mm(kd, dsn)
            dkd = mm_nt(lc["vn"], dsn)
            dcd = _sum_all(s * dsn)
            ds_scr[...] = lc["ds_q"] + lc["cd"] * dsn - mm_tn(w, dvn)
            dw = -mm_nt(dvn, s)
            dvb, dkg = mm_tn(tinv_c, dvn), mm_tn(tinv_c, dw)
            da = -jnp.where(lc["strict"], mm_nt(dvb, u) + mm_nt(dkg, w), 0.0)
            dm = da * lc["dmat"]
            dn = lc["dp"] * lc["dmat"]
            dkb = mm(dm, k)
            e = da * lc["lmat"] + lc["dp"] * lc["pmat"]
            t_kd = _lanes(dkd * kd)
            dgl = _sum_all(t_kd) + dcd * lc["cd"][:, :1]
            dgcs = (_lanes(e) + _lanes(lc["dqd"] * lc["qd"]) - t_kd + _lanes(dkg * lc["kg"])
                    + jnp.where(lastrow, dgl, 0.0))
            rows[c] = _rows(e)
            dq_ref[sl, :] = mm(dn, k) + gam * lc["dqd"]
            dk_ref[sl, :] = (mm_tn(dm, lc["kb"]) + mm_tn(dn, q) + lc["kdec"] * dkd + bb * gam * dkg + bb * dkb)
            dv_ref[sl, :] = bb * dvb
            dbeta = _lanes(dkg * gam * k) + _lanes(dvb * vv) + _lanes(dkb * k)
            db_ref[sl, :] = jnp.broadcast_to(dbeta, (C, 128))
            dgc_ref[sl, :] = jnp.broadcast_to(dgcs, (C, 128))
        dgr_ref[...] = jnp.concatenate(rows, axis=1)

    blk = pl.BlockSpec((tb, 128), lambda h, i: (nb - 1 - i, h))
    rowspec = pl.BlockSpec((None, 1, tb), lambda h, i: (h, 0, nb - 1 - i))
    return pl.pallas_call(
        body, name="gdn_bwd", grid=(GDN_H, nb),
        in_specs=[blk] * 5 + [rowspec, blk,
                              pl.BlockSpec((None, ncb, 128, 128), lambda h, i: (h, nb - 1 - i, 0, 0)),
                              pl.BlockSpec((None, tb, C), lambda h, i: (h, nb - 1 - i, 0))],
        out_specs=[blk] * 5 + [rowspec],
        out_shape=[jax.ShapeDtypeStruct((T, D), F32)] * 5 + [jax.ShapeDtypeStruct((GDN_H, 1, T), F32)],
        scratch_shapes=[pltpu.VMEM((128, 128), F32)],
        compiler_params=_cparams(("parallel", "arbitrary")),
    )(qn, kn, v, gcs_x, beta_x, gcs_t, do, states, tinv)


def _ssd_pair(x2, dt2, acs2):
    last = acs2[SSM_L - 1:SSM_L, :]
    return jnp.exp(acs2), jnp.exp(last - acs2), x2 * dt2


def _ssd_head(hh, acs2, arow, dec2, cbm, bm, incl, col):
    lmask = (col >= 64 * hh) & (col < 64 * hh + 64)
    sg = jnp.where(incl, jnp.exp(jnp.minimum(acs2[:, 64 * hh:64 * hh + 1] - arow, 0.0)), 0.0)
    dec_col = dec2[:, 64 * hh:64 * hh + 1]
    return lmask, sg, sg * cbm, dec_col, bm * dec_col


def ssd_fwd(xs, bc, dt_x, acs_x, acs_t):
    T = xs.shape[0]
    nc, L = T // SSM_L, SSM_L

    def body(x_ref, b_ref, c_ref, dt_ref, ac_ref, at_ref, y_ref, hst_ref, h_scr):
        @pl.when(pl.program_id(1) == 0)
        def _():
            h_scr[...] = jnp.zeros_like(h_scr)

        bm, cm = b_ref[...], c_ref[...]
        cbm = mm_nt(cm, bm)
        row, col = _iota((L, L), 0), _iota((L, L), 1)
        incl = row >= col
        for pr in range(4):
            sl = slice(128 * pr, 128 * pr + 128)
            acs2 = ac_ref[:, sl]
            lam2, dec2, xd2 = _ssd_pair(x_ref[:, sl], dt_ref[:, sl], acs2)
            y2, st = None, []
            for hh in range(2):
                lmask, _, mmat, _, bd = _ssd_head(hh, acs2, at_ref[2 * pr + hh], dec2, cbm, bm, incl, col)
                t = mm(mmat, jnp.where(lmask, xd2, 0.0))
                y2 = t if y2 is None else y2 + t
                st.append(mm_tn(xd2, bd))
            hprev = h_scr[pr]
            hst_ref[pr] = hprev
            y_ref[:, sl] = y2 + lam2 * mm_nt(cm, hprev)
            lam_rows = jnp.where(row < 64, lam2[L - 1:L, 0:1], lam2[L - 1:L, 64:65])
            h_scr[pr] = lam_rows * hprev + jnp.where(row < 64, st[0], st[1])

    return pl.pallas_call(
        body, name="ssd_fwd", grid=(2, nc),
        in_specs=[pl.BlockSpec((L, 512), lambda g, c: (c, g)),
                  pl.BlockSpec((L, 128), lambda g, c: (c, g)),
                  pl.BlockSpec((L, 128), lambda g, c: (c, 2 + g)),
                  pl.BlockSpec((L, 512), lambda g, c: (c, g)),
                  pl.BlockSpec((L, 512), lambda g, c: (c, g)),
                  pl.BlockSpec((8, 1, L), lambda g, c: (g, 0, c))],
        out_specs=[pl.BlockSpec((L, 512), lambda g, c: (c, g)),
                   pl.BlockSpec((None, None, 4, 128, 128), lambda g, c: (g, c, 0, 0, 0))],
        out_shape=[jax.ShapeDtypeStruct((T, D), F32), jax.ShapeDtypeStruct((2, nc, 4, 128, 128), F32)],
        scratch_shapes=[pltpu.VMEM((4, 128, 128), F32)],
        compiler_params=_cparams(("parallel", "arbitrary")),
    )(xs, bc, bc, dt_x, acs_x, acs_t)


def ssd_bwd(xs, bc, dt_x, acs_x, acs_t, dy, hstates):
    T = xs.shape[0]
    nc, L = T // SSM_L, SSM_L

    def body(x_ref, b_ref, c_ref, dt_ref, ac_ref, at_ref, dy_ref, hst_ref,
             dx_ref, db_ref, dc_ref, dgate_ref, dar_ref, dh_scr):
        @pl.when(pl.program_id(1) == 0)
        def _():
            dh_scr[...] = jnp.zeros_like(dh_scr)

        bm, cm = b_ref[...], c_ref[...]
        cbm = mm_nt(cm, bm)
        row, col = _iota((L, L), 0), _iota((L, L), 1)
        rowc = _iota((L, 1), 0)
        incl = row >= col
        dcb = jnp.zeros((L, L), F32)
        dbm = jnp.zeros((L, SSM_N), F32)
        dcm = jnp.zeros((L, SSM_N), F32)
        comp = jnp.zeros((L, 128), F32)
        for pr in range(4):
            sl = slice(128 * pr, 128 * pr + 128)
            x2, dt2, dy2, acs2 = x_ref[:, sl], dt_ref[:, sl], dy_ref[:, sl], ac_ref[:, sl]
            lam2, dec2, xd2 = _ssd_pair(x2, dt2, acs2)
            hprev, dhn = hst_ref[pr], dh_scr[pr]
            dz = lam2 * dy2
            yoff = dz * mm_nt(cm, hprev)
            dcm = dcm + mm(dz, hprev)
            q_rows = _lanes(dhn * hprev)
            dxd2 = jnp.zeros((L, 128), F32)
            for hh in range(2):
                lmask, sg, mmat, dec_col, bd = _ssd_head(hh, acs2, at_ref[2 * pr + hh], dec2, cbm, bm, incl, col)
                dm = jnp.where(incl, mm_nt(jnp.where(lmask, dy2, 0.0), xd2), 0.0)
                dcb = dcb + dm * sg
                e = dm * mmat
                dxd_h = jnp.where(lmask, mm_tn(mmat, dy2) + mm_nt(bd, dhn), 0.0)
                dxd2 = dxd2 + dxd_h
                dbd = mm(jnp.where(lmask, xd2, 0.0), dhn)
                dbm = dbm + dec_col * dbd
                t = _lanes(dbd * bd)
                lam_h = lam2[L - 1:L, 64 * hh:64 * hh + 1]
                in_head = (rowc >= 64 * hh) & (rowc < 64 * hh + 64)
                add_last = _sum_all(t) + _sum_all(jnp.where(in_head, q_rows, 0.0)) * lam_h
                dacs_col = (_lanes(jnp.where(lmask, yoff, 0.0)) + _lanes(e) - t
                            + jnp.where(rowc == L - 1, add_last, 0.0))
                ddt_col = _lanes(dxd_h * x2)
                j = 2 * pr + hh
                dar_ref[j] = _rows(e)
                comp = comp + jnp.where(col == j, dacs_col, 0.0) + jnp.where(col == 8 + j, ddt_col, 0.0)
            lam_rows = jnp.where(row < 64, lam2[L - 1:L, 0:1], lam2[L - 1:L, 64:65])
            dh_scr[pr] = mm_tn(dz, cm) + lam_rows * dhn
            dx_ref[:, sl] = dt2 * dxd2
        db_ref[...] = dbm + mm_tn(dcb, cm)
        dc_ref[...] = dcm + mm(dcb, bm)
        dgate_ref[...] = comp

    rv = lambda g, c: (nc - 1 - c, g)
    rowspec = pl.BlockSpec((8, 1, L), lambda g, c: (g, 0, nc - 1 - c))
    return pl.pallas_call(
        body, name="ssd_bwd", grid=(2, nc),
        in_specs=[pl.BlockSpec((L, 512), rv),
                  pl.BlockSpec((L, 128), rv),
                  pl.BlockSpec((L, 128), lambda g, c: (nc - 1 - c, 2 + g)),
                  pl.BlockSpec((L, 512), rv),
                  pl.BlockSpec((L, 512), rv),
                  rowspec,
                  pl.BlockSpec((L, 512), rv),
                  pl.BlockSpec((None, None, 4, 128, 128), lambda g, c: (g, nc - 1 - c, 0, 0, 0))],
        out_specs=[pl.BlockSpec((L, 512), rv), pl.BlockSpec((L, 128), rv), pl.BlockSpec((L, 128), rv),
                   pl.BlockSpec((L, 128), rv), rowspec],
        out_shape=[jax.ShapeDtypeStruct((T, D), F32), jax.ShapeDtypeStruct((T, 256), F32),
                   jax.ShapeDtypeStruct((T, 256), F32), jax.ShapeDtypeStruct((T, 256), F32),
                   jax.ShapeDtypeStruct((SSM_H, 1, T), F32)],
        scratch_shapes=[pltpu.VMEM((4, 128, 128), F32)],
        compiler_params=_cparams(("parallel", "arbitrary")),
    )(xs, bc, bc, dt_x, acs_x, acs_t, dy, hstates)


def _pos():
    return lax.axis_index("x"), lax.axis_index("y"), lax.axis_index("c")


def _other_chips(x, y):
    return [(1 - x, y), (x, 1 - y), (1 - x, 1 - y)]


def _rcopy(src, dst, ssem, rsem, dev):
    return pltpu.make_async_remote_copy(src_ref=src, dst_ref=dst, send_sem=ssem, recv_sem=rsem,
                                        device_id=dev, device_id_type=MESH)


def _rows_at(start, n):
    return pl.ds(pl.multiple_of(start, 8), n)


def _comm_call(body, name, out_shape, n_in, scratch):
    return pl.pallas_call(
        body, name=name, out_shape=out_shape, in_specs=[ANY] * n_in,
        out_specs=[ANY] * len(out_shape) if isinstance(out_shape, (list, tuple)) else ANY,
        scratch_shapes=scratch,
        compiler_params=pltpu.CompilerParams(has_side_effects=True),
    )


def _dma_sems(n):
    return pltpu.SemaphoreType.DMA((n,))


def ag_chips(name, shard):
    rr, cc = shard.shape
    h, nq = rr // 2, ICI_CHUNKS
    hq = h // nq

    def body(x_ref, out_ref, ssem, rsem, lsem):
        x, y, c = _pos()
        me_s = 2 * x + y
        chips = _other_chips(x, y)
        local = pltpu.make_async_copy(x_ref, out_ref.at[me_s], lsem)
        local.start()
        started = []
        for q in range(nq):
            rows = _rows_at(c * h + q * hq, hq)
            for j, (cx, cy) in enumerate(chips):
                cp = _rcopy(x_ref.at[rows], out_ref.at[me_s, rows], ssem.at[j * nq + q], rsem.at[j * nq + q], (cx, cy, c))
                cp.start()
                started.append(cp)
        for q in range(nq):
            rows = _rows_at(c * h + q * hq, hq)
            for j, (cx, cy) in enumerate(chips):
                blk = out_ref.at[2 * cx + cy, rows]
                _rcopy(blk, blk, ssem.at[j * nq + q], rsem.at[j * nq + q], (cx, cy, c)).wait_recv()
                k = 3 * nq + j * nq + q
                cp = _rcopy(blk, blk, ssem.at[k], rsem.at[k], (x, y, 1 - c))
                cp.start()
                started.append(cp)
        for q in range(nq):
            rows = _rows_at((1 - c) * h + q * hq, hq)
            for j, (cx, cy) in enumerate(chips):
                blk = out_ref.at[2 * cx + cy, rows]
                k = 3 * nq + j * nq + q
                _rcopy(blk, blk, ssem.at[k], rsem.at[k], (x, y, 1 - c)).wait_recv()
        for cp in started:
            cp.wait_send()
        local.wait()

    return _comm_call(body, name, jax.ShapeDtypeStruct((4, rr, cc), shard.dtype), 1,
                      [_dma_sems(6 * nq), _dma_sems(6 * nq), pltpu.SemaphoreType.DMA])(shard)


def rs_pair(name, g):
    _, rr, cc = g.shape
    h, nq = rr // 2, D2D_CHUNKS
    hq = h // nq

    def body(g_ref, mine_ref, recv_ref, ssem, rsem, lsem):
        x, y, c = _pos()
        cps = []
        for q in range(nq):
            dst = pl.ds(q * hq, hq)
            loc = pltpu.make_async_copy(g_ref.at[:, _rows_at(c * h + q * hq, hq), :], mine_ref.at[:, dst, :], lsem.at[q])
            loc.start()
            cp = _rcopy(g_ref.at[:, _rows_at((1 - c) * h + q * hq, hq), :], recv_ref.at[:, dst, :],
                        ssem.at[q], rsem.at[q], (x, y, 1 - c))
            cp.start()
            cps.append((loc, cp))
        for loc, cp in cps:
            cp.wait()
            loc.wait()

    shp = jax.ShapeDtypeStruct((4, h, cc), g.dtype)
    return _comm_call(body, name, [shp, shp], 1, [_dma_sems(nq), _dma_sems(nq), _dma_sems(nq)])(g)


def rs_chips(name, p):
    _, h, cc = p.shape
    nq = ICI_CHUNKS
    hq = h // nq

    def body(p_ref, buf_ref, ssem, rsem, lsem):
        x, y, c = _pos()
        me_s = 2 * x + y
        local = pltpu.make_async_copy(p_ref.at[me_s], buf_ref.at[0], lsem)
        local.start()
        sends = []
        for q in range(nq):
            rows = pl.ds(q * hq, hq)
            for j, (cx, cy) in enumerate(_other_chips(x, y)):
                cp = _rcopy(p_ref.at[2 * cx + cy, rows], buf_ref.at[1 + j, rows], ssem.at[j * nq + q],
                            rsem.at[j * nq + q], (cx, cy, c))
                cp.start()
                sends.append(cp)
        for cp in sends:
            cp.wait()
        local.wait()

    return _comm_call(body, name, jax.ShapeDtypeStruct((4, h, cc), p.dtype), 1,
                      [_dma_sems(3 * nq), _dma_sems(3 * nq), pltpu.SemaphoreType.DMA])(p)


def rs_join(name, half):
    h, cc = half.shape
    nq = D2D_CHUNKS
    hq = h // nq

    def body(h_ref, out_ref, ssem, rsem, lsem):
        x, y, c = _pos()
        cps = []
        for q in range(nq):
            src, dst = h_ref.at[pl.ds(q * hq, hq)], out_ref.at[_rows_at(c * h + q * hq, hq)]
            loc = pltpu.make_async_copy(src, dst, lsem.at[q])
            loc.start()
            cp = _rcopy(src, dst, ssem.at[q], rsem.at[q], (x, y, 1 - c))
            cp.start()
            cps.append((loc, cp))
        for loc, cp in cps:
            cp.wait()
            loc.wait()

    return _comm_call(body, name, jax.ShapeDtypeStruct((2 * h, cc), half.dtype), 1,
                      [_dma_sems(nq), _dma_sems(nq), _dma_sems(nq)])(half)


def reduce_scatter(tag, g, tb):
    _, rr, cc = g.shape
    h = rr // 2
    mine, recv = rs_pair(tag + "_pair", g)
    part = rowwise(add2_fn, tag + "_add", 4 * h, tb, [R(mine.reshape(4 * h, cc)), R(recv.reshape(4 * h, cc))],
                   [], [(cc, BF16)])[0]
    buf = rs_chips(tag + "_chips", part.reshape(4, h, cc)).reshape(4 * h, cc)
    nb = h // tb
    red = rowwise(sum4_fn, tag + "_sum", h, tb, [R(buf, off=k * nb) for k in range(4)], [], [(cc, F32)])[0]
    return rs_join(tag + "_join", red)


def all_reduce_small(name, v):
    rows = v.shape[0]

    def body(v_ref, out_ref, buf, ssem, rsem):
        x, y, c = _pos()
        me = 4 * x + 2 * y + c
        buf[me] = v_ref[...]
        cps = []
        for k in range(1, 8):
            dev = (x ^ (k >> 2), y ^ ((k >> 1) & 1), c ^ (k & 1))
            cp = _rcopy(v_ref, buf.at[me], ssem.at[k - 1], rsem.at[k - 1], dev)
            cp.start()
            cps.append(cp)
        for cp in cps:
            cp.wait()
        acc = buf[0]
        for d in range(1, 8):
            acc = acc + buf[d]
        out_ref[...] = acc

    return pl.pallas_call(
        body, name=name, out_shape=jax.ShapeDtypeStruct((rows, 128), F32),
        in_specs=[pl.BlockSpec(memory_space=pltpu.VMEM)], out_specs=pl.BlockSpec(memory_space=pltpu.VMEM),
        scratch_shapes=[pltpu.VMEM((8, rows, 128), F32), _dma_sems(7), _dma_sems(7)],
        compiler_params=pltpu.CompilerParams(has_side_effects=True),
    )(v)


def _pack(vs):
    parts, offs, r = [], [], 0
    for v in vs:
        n = v.size
        nr = -(-n // 128)
        parts.append(jnp.pad(v.reshape(-1).astype(F32), (0, nr * 128 - n)))
        offs.append((r, n))
        r += nr
    pad = (-r) % 8
    if pad:
        parts.append(jnp.zeros((pad * 128,), F32))
    return jnp.concatenate(parts).reshape(r + pad, 128), offs


def _unpack(buf, offs, shapes):
    flat = buf.reshape(-1)
    return [flat[r * 128:r * 128 + n].reshape(s) for (r, n), s in zip(offs, shapes)]


def _sel(rows, cols, pairs):
    m = np.zeros((rows, cols), np.float32)
    for r, c in pairs:
        m[r, c] = 1.0
    return jnp.asarray(m)


def _pad_win(w):
    z = jnp.zeros((w.shape[0], 112), w.dtype)
    return jnp.concatenate([w[:, :4096], w[:, 4112:6672], w[:, 4096:4112], z, w[:, 6672:6688], z], axis=1)


def _unpad_win(wp):
    return jnp.concatenate([wp[:, :4096], wp[:, 6656:6672], wp[:, 4096:6656], wp[:, 6784:6800]], axis=1)


def kernel(x, mem, norm1_w, w_in, gdn_conv_w, gdn_a_log, gdn_dt_bias, gdn_norm_w, ssm_conv_w, ssm_conv_b, ssm_a_log, ssm_dt_bias, ssm_d, ssm_norm_w, w_out, norm2_w, mem_norm_w, wq_mem, wk_mem, wv_mem, wo_mem, norm3_w, w_up, w_down, final_norm_w, loss_target, m_norm1_w, m_w_in, m_gdn_conv_w, m_gdn_a_log, m_gdn_dt_bias, m_gdn_norm_w, m_ssm_conv_w, m_ssm_conv_b, m_ssm_a_log, m_ssm_dt_bias, m_ssm_d, m_ssm_norm_w, m_w_out, m_norm2_w, m_mem_norm_w, m_wq_mem, m_wk_mem, m_wv_mem, m_wo_mem, m_norm3_w, m_w_up, m_w_down, m_final_norm_w, v_norm1_w, v_w_in, v_gdn_conv_w, v_gdn_a_log, v_gdn_dt_bias, v_gdn_norm_w, v_ssm_conv_w, v_ssm_conv_b, v_ssm_a_log, v_ssm_dt_bias, v_ssm_d, v_ssm_norm_w, v_w_out, v_norm2_w, v_mem_norm_w, v_wq_mem, v_wk_mem, v_wv_mem, v_wo_mem, v_norm3_w, v_w_up, v_w_down, v_final_norm_w):
    T, M = x.shape[1], mem.shape[1]
    xi, yi, ci = _pos()
    s_me = 2 * xi + yi
    x0, mem0, tgt = x[0], mem[0], loss_target[0]
    tb = min(256, T)
    row = lambda v: v.reshape(1, -1)

    win_g = ag_chips("ag_win", w_in.astype(BF16))
    rest_g = ag_chips("ag_rest", jnp.concatenate([w_out, wq_mem, wk_mem, wv_mem, wo_mem, w_up, w_down],
                                                 axis=0).astype(BF16))
    w_in_p = _pad_win(win_g.transpose(1, 0, 2).reshape(D, IN_COLS))
    wout_f = rest_g[:, 0:512].reshape(2 * D, D)
    wq_f, wk_f, wv_f, wo_f = (rest_g[:, 512 + 256 * k:768 + 256 * k].reshape(D, D) for k in range(4))
    wup_f = rest_g[:, 1536:2560].transpose(1, 0, 2).reshape(D, D_FF)
    wdown_f = rest_g[:, 2560:3584].reshape(D_FF, D)
    keep = (ci == 0).astype(F32)
    gcw_z = lax.dynamic_update_slice(jnp.zeros((4, 3 * D), F32), gdn_conv_w * keep, (0, s_me * 768))
    scw_z = lax.dynamic_update_slice(jnp.zeros((4, 1536), F32), ssm_conv_w * keep, (0, s_me * 384))
    cbuf, coffs = _pack([gcw_z, scw_z])
    gcw, scw = _unpack(all_reduce_small("ar_convw", cbuf), coffs, [(4, 3 * D), (4, 1536)])
    scw_x, scw_bc = scw[:, :D], scw[:, D:]
    scb_x, scb_bc = row(ssm_conv_b[:D]), row(ssm_conv_b[D:])

    galog_c, gdtb_c = row(jnp.pad(gdn_a_log, (8, 112))), row(jnp.pad(gdn_dt_bias, (8, 112)))
    salog_c, sdtb_c = row(jnp.pad(ssm_a_log, (0, 112))), row(jnp.pad(ssm_dt_bias, (0, 112)))
    sd_x = row(jnp.repeat(ssm_d, 64))
    eb = _sel(128, D, [(h, 128 * h + l) for h in range(8) for l in range(128)])
    ea = _sel(128, D, [(8 + h, 128 * h + l) for h in range(8) for l in range(128)])
    e16 = _sel(128, D, [(h, 64 * h + l) for h in range(16) for l in range(64)])
    pb = _sel(D, 128, [(128 * h, h) for h in range(8)])
    pa = _sel(D, 128, [(128 * h, 8 + h) for h in range(8)])

    h1 = rowwise(rms_fwd_fn, "rms1", T, tb, [R(x0)], [row(norm1_w)], [(D, BF16)])[0]
    p = matmul("mm_in", h1, w_in_p, "nn", 1024, 768, 1024, [F32])[0]
    gp_ins = [R(p, 3 * D, CB_QKV, "prev"), R(p, 128, CB_BA)]
    qn, kn, vv, gcs_x, beta_x, gcs_t = rowwise(gdn_prep_fn, "gdn_prep", T, tb, gp_ins,
                                               [gcw, galog_c, gdtb_c, eb, ea], [(D, F32)] * 5 + [(-8, F32)])
    gcs_t = gcs_t.reshape(GDN_H, 1, T)
    gtb = min(256, T)
    o_gdn, s_states, tinv = gdn_fwd(qn, kn, vv, gcs_x, beta_x, gcs_t, gtb)
    gnw = row(gdn_norm_w)
    oa = rowwise(gdn_post_fn, "gdn_post", T, tb, [R(o_gdn), R(p, D, CB_Z)], [gnw], [(D, BF16)])[0]
    sp_ins = [R(p, D, CB_XS, "prev"), R(p, 512, CB_BC, "prev"), R(p, 128, CB_DT)]
    sp_full = [scw_x, scw_bc, scb_x, scb_bc, salog_c, sdtb_c]
    xs, bc, dt_x, acs_x, acs_t = rowwise(ssd_prep_fn, "ssd_prep", T, tb, sp_ins, sp_full + [e16],
                                         [(D, F32), (512, F32), (D, F32), (D, F32), (-SSM_H, F32)])
    acs_t = acs_t.reshape(SSM_H, 1, T)
    y_ssd, h_states = ssd_fwd(xs, bc, dt_x, acs_x, acs_t)
    snw = row(ssm_norm_w)
    ob = rowwise(ssd_post_fn, "ssd_post", T, tb, [R(y_ssd), R(xs), R(p, D, CB_ZS)], [sd_x, snw], [(D, BF16)])[0]
    x1a = matmul("mm_out_a", oa, wout_f[:D], "nn", 1024, 1024, 1024, [F32], _epi_res, [x0])[0]
    x1 = matmul("mm_out_b", ob, wout_f[D:], "nn", 1024, 1024, 1024, [F32], _epi_res, [x1a])[0]

    mn = rowwise(rms_fwd_fn, "rms_mem", M, M, [R(mem0)], [row(mem_norm_w)], [(D, BF16)])[0]
    km = matmul("mm_k", mn, wk_f, "nn", 256, 1024, 1024, [BF16])[0]
    vm = matmul("mm_v", mn, wv_f, "nn", 256, 1024, 1024, [BF16])[0]
    h2 = rowwise(rms_fwd_fn, "rms2", T, tb, [R(x1)], [row(norm2_w)], [(D, BF16)])[0]
    qm = matmul("mm_q", h2, wq_f, "nn", 1024, 1024, 1024, [BF16])[0]
    ao = rowwise(attn_fn, "attn", T, tb, [R(qm)], [km, vm], [(D, BF16)])[0]
    x2 = matmul("mm_o", ao, wo_f, "nn", 1024, 1024, 1024, [F32], _epi_res, [x1])[0]

    h3 = rowwise(rms_fwd_fn, "rms3", T, tb, [R(x2)], [row(norm3_w)], [(D, BF16)])[0]
    u, act = matmul("mm_up", h3, wup_f, "nn", 1024, 1024, 1024, [BF16, BF16], _epi_relu2)
    x3 = matmul("mm_down", act, wdown_f, "nn", 1024, 1024, 1024, [F32], _epi_res, [x2])[0]

    dx3, dx3b, loss_lane, g_final = rowwise(final_fn, "final", T, tb, [R(x3), R(tgt)], [row(final_norm_w)],
                                            [(D, F32), (D, BF16)], [(1, D), (1, D)])
    loss = lax.psum(0.5 / D * jnp.sum(loss_lane), ("x", "y", "c"))

    dup = matmul("mm_dact", dx3b, wdown_f, "nt", 1024, 1024, 1024, [BF16], _epi_dup, [u])[0]
    g_wdown = matmul("mm_gdown", act, dx3b, "tn", 1024, 1024, 1024, [F32])[0]
    dh3 = matmul("mm_dh3", dup, wup_f, "nt", 1024, 1024, 1024, [F32])[0]
    g_wup = matmul("mm_gup", h3, dup, "tn", 1024, 1024, 1024, [F32])[0]
    dx2, dx2b, g_n3 = rowwise(rms_bwd_fn, "rms3_bwd", T, tb, [R(x2), R(dh3), R(dx3)], [row(norm3_w)],
                              [(D, F32), (D, BF16)], [(1, D)])
    dao = matmul("mm_dao", dx2b, wo_f, "nt", 1024, 1024, 1024, [F32])[0]
    g_wo = matmul("mm_gwo", ao, dx2b, "tn", 1024, 1024, 1024, [F32])[0]
    dqm, dkm, dvm = rowwise(attn_bwd_fn, "attn_bwd", T, tb, [R(qm), R(dao)], [km, vm], [(D, BF16)],
                            [(M, D), (M, D)])
    dh2 = matmul("mm_dh2", dqm, wq_f, "nt", 1024, 1024, 1024, [F32])[0]
    g_wq = matmul("mm_gwq", h2, dqm, "tn", 1024, 1024, 1024, [F32])[0]
    g_wk = matmul("mm_gwk", mn, dkm, "tn", 1024, 1024, 256, [F32])[0]
    g_wv = matmul("mm_gwv", mn, dvm, "tn", 1024, 1024, 256, [F32])[0]
    dmn_k = matmul("mm_dmk", dkm, wk_f, "nt", 256, 1024, 1024, [F32])[0]
    dmn = matmul("mm_dmv", dvm, wv_f, "nt", 256, 1024, 1024, [F32], _epi_res, [dmn_k])[0]
    g_nmem = rowwise(rms_bwd_w_fn, "rmsmem_bwd", M, M, [R(mem0), R(dmn)], [row(mem_norm_w)], [], [(1, D)])[0]
    dx1, dx1b, g_n2 = rowwise(rms_bwd_fn, "rms2_bwd", T, tb, [R(x1), R(dh2), R(dx2)], [row(norm2_w)],
                              [(D, F32), (D, BF16)], [(1, D)])
    doa = matmul("mm_doa", dx1b, wout_f[:D], "nt", 1024, 1024, 1024, [F32])[0]
    dob = matmul("mm_dob", dx1b, wout_f[D:], "nt", 1024, 1024, 1024, [F32])[0]
    g_wout_a = matmul("mm_gwout_a", oa, dx1b, "tn", 1024, 1024, 1024, [F32])[0]
    g_wout_b = matmul("mm_gwout_b", ob, dx1b, "tn", 1024, 1024, 1024, [F32])[0]

    dy_ssd, dxs_dir, dzs, g_snw, g_sd_lane = rowwise(
        ssd_post_bwd_fn, "ssd_post_bwd", T, tb, [R(y_ssd), R(xs), R(p, D, CB_ZS), R(dob)], [sd_x, snw],
        [(D, F32), (D, F32), (D, BF16)], [(1, D), (1, D)])
    dxs_scan, db_s, dc_s, dgate, dacs_t = ssd_bwd(xs, bc, dt_x, acs_x, acs_t, dy_ssd, h_states)
    dbc = jnp.concatenate([db_s, dc_s], axis=1)
    dxs = rowwise(add2_fn, "ssd_dxs", T, tb, [R(dxs_scan), R(dxs_dir)], [], [(D, F32)])[0]
    spb = rowwise(ssd_prep_bwd_fn, "ssd_prep_bwd", T, tb,
                  sp_ins + [R(dxs), R(dbc), R(dgate), RC(dacs_t.reshape(SSM_H, T))], sp_full,
                  [(D, F32), (512, F32), (128, BF16)],
                  [(1, D)] * 4 + [(1, 512)] * 4 + [(1, D), (1, 512), (1, 128), (1, 128)])
    dyc_x, dyc_bc, ddt_blk = spb[:3]
    g_scw = jnp.concatenate([jnp.concatenate(spb[3:7], axis=0), jnp.concatenate(spb[7:11], axis=0)], axis=1)
    g_scb = jnp.concatenate([spb[11], spb[12]], axis=1).reshape(-1)
    g_salog, g_sdtb = spb[13][0, :SSM_H], spb[14][0, :SSM_H]
    dp_xs = rowwise(conv_bwd_fn, "conv_bwd_x", T, tb, [R(dyc_x, halo="next")], [scw_x], [(D, BF16)])[0]
    dp_bc = rowwise(conv_bwd_fn, "conv_bwd_bc", T, tb, [R(dyc_bc, halo="next")], [scw_bc], [(512, BF16)])[0]

    do_gdn, dz, g_gnw = rowwise(gdn_post_bwd_fn, "gdn_post_bwd", T, tb, [R(o_gdn), R(p, D, CB_Z), R(doa)], [gnw],
                                [(D, F32), (D, BF16)], [(1, 128)])
    dqn, dkn, dvv, dgcs_x, dbeta_x, dgcs_t = gdn_bwd(qn, kn, vv, gcs_x, beta_x, gcs_t, do_gdn, s_states, tinv, gtb)
    gpb = rowwise(gdn_prep_bwd_fn, "gdn_prep_bwd", T, tb,
                  gp_ins + [R(dqn), R(dkn), R(dvv), R(dgcs_x), R(dbeta_x), RC(dgcs_t.reshape(GDN_H, T))],
                  [gcw, galog_c, gdtb_c, pb, pa],
                  [(3 * D, F32), (128, BF16)], [(1, 3 * D)] * 4 + [(1, 128), (1, 128)])
    dyc_qkv, dba = gpb[:2]
    g_gcw = jnp.concatenate(gpb[2:6], axis=0)
    g_galog, g_gdtb = gpb[6][0, 8:16], gpb[7][0, 8:16]
    dp_qkv = rowwise(conv_bwd_fn, "conv_bwd_qkv", T, tb, [R(dyc_qkv, halo="next")], [gcw], [(3 * D, BF16)])[0]

    dp = jnp.concatenate([dp_qkv, dz, dzs, dp_xs, dp_bc, dba, ddt_blk], axis=1)
    dh1 = matmul("mm_dh1", dp, w_in_p, "nt", 1024, 1024, 768, [F32])[0]
    g_win_p = matmul("mm_gwin", h1, dp, "tn", 1024, 768, 1024, [F32])[0]
    grad_x, _, g_n1 = rowwise(rms_bwd_fn, "rms1_bwd", T, tb, [R(x0), R(dh1), R(dx1)], [row(norm1_w)],
                              [(D, F32), (D, BF16)], [(1, D)])

    small = [g_n1, g_galog, g_gdtb, g_gnw, g_scb, g_salog, g_sdtb, g_sd_lane.reshape(16, 64).sum(axis=1), g_snw,
             g_n2, g_nmem, g_n3, g_final, g_gcw, g_scw]
    sshapes = [(D,), (8,), (8,), (128,), (1536,), (16,), (16,), (16,), (D,), (D,), (D,), (D,), (D,),
               (4, 3 * D), (4, 1536)]
    sbuf, soffs = _pack(small)
    sg = _unpack(all_reduce_small("ar_grads", sbuf), soffs, sshapes)
    (gr_n1, gr_galog, gr_gdtb, gr_gnw, gr_scb, gr_salog, gr_sdtb, gr_sd, gr_snw, gr_n2, gr_nmem, gr_n3,
     gr_final, gr_gcw_full, gr_scw_full) = sg
    gr_gcw = lax.dynamic_slice(gr_gcw_full, (0, s_me * 768), (4, 768))
    gr_scw = lax.dynamic_slice(gr_scw_full, (0, s_me * 384), (4, 384))

    g_win = _unpad_win(g_win_p).reshape(D, 4, IN_COLS // 4).transpose(1, 0, 2)
    gr_win = reduce_scatter("rs_win", g_win, 256)
    g_rest = jnp.concatenate([
        jnp.concatenate([g_wout_a, g_wout_b], axis=0).reshape(4, 512, D),
        g_wq.reshape(4, 256, D), g_wk.reshape(4, 256, D), g_wv.reshape(4, 256, D), g_wo.reshape(4, 256, D),
        g_wup.reshape(D, 4, D).transpose(1, 0, 2), g_wdown.reshape(4, D, D)], axis=1)
    gr_rest = reduce_scatter("rs_rest", g_rest, 256)
    gr_wout, gr_wq, gr_wk, gr_wv, gr_wo = (gr_rest[0:512], gr_rest[512:768], gr_rest[768:1024],
                                           gr_rest[1024:1280], gr_rest[1280:1536])
    gr_wup, gr_wdown = gr_rest[1536:2560], gr_rest[2560:3584]

    def adam_big(name, w, g, m, v, tbr):
        return rowwise(adamw_fn, name, w.shape[0], tbr, [R(w), R(g), R(m), R(v)], [], [(w.shape[1], F32)] * 3)

    big = {
        "w_in": adam_big("adam_win", w_in, gr_win, m_w_in, v_w_in, 256),
        "w_out": adam_big("adam_wout", w_out, gr_wout, m_w_out, v_w_out, 256),
        "wq_mem": adam_big("adam_wq", wq_mem, gr_wq, m_wq_mem, v_wq_mem, 256),
        "wk_mem": adam_big("adam_wk", wk_mem, gr_wk, m_wk_mem, v_wk_mem, 256),
        "wv_mem": adam_big("adam_wv", wv_mem, gr_wv, m_wv_mem, v_wv_mem, 256),
        "wo_mem": adam_big("adam_wo", wo_mem, gr_wo, m_wo_mem, v_wo_mem, 256),
        "w_up": adam_big("adam_wup", w_up, gr_wup, m_w_up, v_w_up, 256),
        "w_down": adam_big("adam_wdown", w_down, gr_wdown, m_w_down, v_w_down, 256),
    }
    names_s = ["norm1_w", "gdn_conv_w", "gdn_a_log", "gdn_dt_bias", "gdn_norm_w", "ssm_conv_w", "ssm_conv_b",
               "ssm_a_log", "ssm_dt_bias", "ssm_d", "ssm_norm_w", "norm2_w", "mem_norm_w", "norm3_w", "final_norm_w"]
    w_s = [norm1_w, gdn_conv_w, gdn_a_log, gdn_dt_bias, gdn_norm_w, ssm_conv_w, ssm_conv_b, ssm_a_log, ssm_dt_bias,
           ssm_d, ssm_norm_w, norm2_w, mem_norm_w, norm3_w, final_norm_w]
    g_s = [gr_n1, gr_gcw, gr_galog, gr_gdtb, gr_gnw, gr_scw, gr_scb, gr_salog, gr_sdtb, gr_sd, gr_snw, gr_n2,
           gr_nmem, gr_n3, gr_final]
    m_s = [m_norm1_w, m_gdn_conv_w, m_gdn_a_log, m_gdn_dt_bias, m_gdn_norm_w, m_ssm_conv_w, m_ssm_conv_b, m_ssm_a_log,
           m_ssm_dt_bias, m_ssm_d, m_ssm_norm_w, m_norm2_w, m_mem_norm_w, m_norm3_w, m_final_norm_w]
    v_s = [v_norm1_w, v_gdn_conv_w, v_gdn_a_log, v_gdn_dt_bias, v_gdn_norm_w, v_ssm_conv_w, v_ssm_conv_b, v_ssm_a_log,
           v_ssm_dt_bias, v_ssm_d, v_ssm_norm_w, v_norm2_w, v_mem_norm_w, v_norm3_w, v_final_norm_w]
    shp_s = [w.shape for w in w_s]
    wb, aoffs = _pack(w_s)
    gb_, _ = _pack(g_s)
    mb, _ = _pack(m_s)
    vb_, _ = _pack(v_s)
    nrows = wb.shape[0]
    d_b, m_b, v_b = rowwise(adamw_fn, "adam_small", nrows, nrows, [R(wb), R(gb_), R(mb), R(vb_)], [], [(128, F32)] * 3)
    d_l, m_l, v_l = (_unpack(b, aoffs, shp_s) for b in (d_b, m_b, v_b))

    grads = {"w_in": gr_win, "w_out": gr_wout, "wq_mem": gr_wq, "wk_mem": gr_wk, "wv_mem": gr_wv, "wo_mem": gr_wo,
             "w_up": gr_wup, "w_down": gr_wdown}
    deltas, new_m, new_v = {}, {}, {}
    for n, (dd, mm_, vv_) in big.items():
        deltas[n], new_m[n], new_v[n] = dd, mm_, vv_
    for k, n in enumerate(names_s):
        grads[n] = g_s[k].reshape(shp_s[k])
        deltas[n], new_m[n], new_v[n] = d_l[k], m_l[k], v_l[k]
    order = ["norm1_w", "w_in", "gdn_conv_w", "gdn_a_log", "gdn_dt_bias", "gdn_norm_w", "ssm_conv_w", "ssm_conv_b",
             "ssm_a_log", "ssm_dt_bias", "ssm_d", "ssm_norm_w", "w_out", "norm2_w", "mem_norm_w", "wq_mem", "wk_mem",
             "wv_mem", "wo_mem", "norm3_w", "w_up", "w_down", "final_norm_w"]
    return (loss, grad_x[None], *[grads[n] for n in order], *[deltas[n] for n in order],
            *[new_m[n] for n in order], *[new_v[n] for n in order])
```

```python
import numpy as np
import jax
import jax.numpy as jnp
from jax import lax
from jax.experimental import pallas as pl
from jax.experimental.pallas import tpu as pltpu

F32, BF16 = jnp.float32, jnp.bfloat16
MESH = pl.DeviceIdType.MESH
ANY = pl.BlockSpec(memory_space=pl.ANY)

EPS = 1e-6
D = 1024
GDN_H, GDN_DK, GDN_C = 8, 128, 64
SSM_H, SSM_P, SSM_N, SSM_L = 16, 64, 128, 128
MEM_H, MEM_DH = 4, 256
D_FF = 4096
IN_COLS = 6688
CB_QKV, CB_Z, CB_ZS, CB_XS, CB_BC, CB_BA, CB_DT = 0, 3, 4, 5, 12, 52, 53
VMEM_LIMIT = 56 * 1024 * 1024
D2D_CHUNKS = 8
ICI_CHUNKS = 4

ADAM_LR, ADAM_B1, ADAM_B2, ADAM_EPS, ADAM_WD, ADAM_STEP = 0.001, 0.9, 0.999, 1e-08, 0.01, 10


def _dg(a, b, ca, cb):
    return lax.dot_general(a, b, (((ca,), (cb,)), ((), ())), preferred_element_type=F32)


def _bf(x):
    return x.astype(BF16)


def mm(a, b):
    return _dg(_bf(a), _bf(b), 1, 0)


def mm_nt(a, b):
    return _dg(_bf(a), _bf(b), 1, 1)


def mm_tn(a, b):
    return _dg(_bf(a), _bf(b), 0, 0)


def mm_sel(a, sel):
    hi = a.astype(BF16)
    r1 = a - hi.astype(F32)
    mid = r1.astype(BF16)
    lo = (r1 - mid.astype(F32)).astype(BF16)
    s = sel.astype(BF16)
    return _dg(hi, s, 1, 0) + (_dg(mid, s, 1, 0) + _dg(lo, s, 1, 0))


def mm3(a, b):
    ah, bh = a.astype(BF16), b.astype(BF16)
    al, bl = (a - ah.astype(F32)).astype(BF16), (b - bh.astype(F32)).astype(BF16)
    return _dg(ah, bh, 1, 0) + (_dg(ah, bl, 1, 0) + _dg(al, bh, 1, 0))


def _iota(shape, dim):
    return lax.broadcasted_iota(jnp.int32, shape, dim)


def _chunk_cumsum(x, c):
    pos = _iota(x.shape, 0) & (c - 1)
    s = 1
    while s < c:
        x = x + jnp.where(pos >= s, pltpu.roll(x, s, 0), 0.0)
        s *= 2
    return x


def _chunk_revcumsum(x, c):
    n = x.shape[0]
    pos = _iota(x.shape, 0) & (c - 1)
    s = 1
    while s < c:
        x = x + jnp.where(pos < c - s, pltpu.roll(x, n - s, 0), 0.0)
        s *= 2
    return x


def _sig(x):
    return 1.0 / (1.0 + jnp.exp(-x))


def _softplus(x):
    return jnp.maximum(x, 0.0) + jnp.log(1.0 + jnp.exp(-jnp.abs(x)))


def _rows(v):
    return jnp.sum(v, axis=0, keepdims=True)


def _lanes(v):
    return jnp.sum(v, axis=1, keepdims=True)


def _sum_all(v):
    return _rows(_lanes(v))


def _cparams(sem):
    return pltpu.CompilerParams(dimension_semantics=sem, vmem_limit_bytes=VMEM_LIMIT)


def rowwise(fn, name, T, tb, row_ins, full_ins, row_outs, acc_outs=()):
    nblk = T // tb
    assert nblk * tb == T
    in_specs, args = [], []
    for arr, w, cb, halo, off in row_ins:
        if halo == "col":
            in_specs.append(pl.BlockSpec((w, tb), lambda i: (0, i)))
            args.append(arr)
            continue
        in_specs.append(pl.BlockSpec((tb, w), lambda i, cb=cb, off=off: (i + off, cb)))
        args.append(arr)
        if halo == "prev":
            r = tb // 8
            in_specs.append(pl.BlockSpec((8, w), lambda i, cb=cb, r=r: (jnp.maximum(i * r - 1, 0), cb)))
            args.append(arr)
        elif halo == "next":
            r, last = tb // 8, T // 8 - 1
            in_specs.append(pl.BlockSpec((8, w), lambda i, cb=cb, r=r, last=last:
                                         (jnp.minimum((i + 1) * r, last), cb)))
            args.append(arr)
    for arr in full_ins:
        in_specs.append(pl.BlockSpec(arr.shape, lambda i, nd=arr.ndim: (0,) * nd))
        args.append(arr)
    n_in, n_ro = len(args), len(row_outs)
    out_shape, out_specs = [], []
    for w, dt in row_outs:
        if w < 0:
            out_shape.append(jax.ShapeDtypeStruct((-w, T), dt))
            out_specs.append(pl.BlockSpec((-w, tb), lambda i: (0, i)))
        else:
            out_shape.append(jax.ShapeDtypeStruct((T, w), dt))
            out_specs.append(pl.BlockSpec((tb, w), lambda i: (i, 0)))
    for shp in acc_outs:
        out_shape.append(jax.ShapeDtypeStruct(shp, F32))
        out_specs.append(pl.BlockSpec(shp, lambda i, nd=len(shp): (0,) * nd))

    def body(*refs):
        i = pl.program_id(0)
        vals = fn(i, nblk, *[r[...] for r in refs[:n_in]])
        outs = refs[n_in:]
        for ref, val in zip(outs[:n_ro], vals[:n_ro]):
            ref[...] = val.astype(ref.dtype)
        for ref, val in zip(outs[n_ro:], vals[n_ro:]):
            @pl.when(i == 0)
            def _(ref=ref, val=val):
                ref[...] = val

            @pl.when(i > 0)
            def _(ref=ref, val=val):
                ref[...] += val

    return pl.pallas_call(
        body, name=name, grid=(nblk,), in_specs=in_specs, out_specs=out_specs, out_shape=out_shape,
        compiler_params=_cparams(("arbitrary",) if acc_outs else ("parallel",)),
    )(*args)


def R(arr, w=None, cb=0, halo=None, off=0):
    return (arr, arr.shape[1] if w is None else w, cb, halo, off)


def RC(arr):
    return (arr, arr.shape[0], 0, "col", 0)


def matmul(name, a, b, form, tm, tn, tk, out_dtypes, epi=None, extras=(), rows=()):
    if form == "nn":
        (M, K), N = a.shape, b.shape[1]
    elif form == "nt":
        (M, K), N = a.shape, b.shape[0]
    else:
        (K, M), N = a.shape, b.shape[1]
    tm, tn, tk = min(tm, M), min(tn, N), min(tk, K)
    assert M % tm == 0 and N % tn == 0 and K % tk == 0, (name, M, N, K, tm, tn, tk)
    if form == "nn":
        a_spec = pl.BlockSpec((tm, tk), lambda i, j, k: (i, k))
        b_spec = pl.BlockSpec((tk, tn), lambda i, j, k: (k, j))
        ca, cb = 1, 0
    elif form == "nt":
        a_spec = pl.BlockSpec((tm, tk), lambda i, j, k: (i, k))
        b_spec = pl.BlockSpec((tn, tk), lambda i, j, k: (j, k))
        ca, cb = 1, 1
    else:
        a_spec = pl.BlockSpec((tk, tm), lambda i, j, k: (k, i))
        b_spec = pl.BlockSpec((tk, tn), lambda i, j, k: (k, j))
        ca, cb = 0, 0
    nk, ne, no = K // tk, len(extras) + len(rows), len(out_dtypes)
    if epi is None:
        epi = lambda acc: (acc,)

    def body(a_ref, b_ref, *rest):
        ex, outs, acc = rest[:ne], rest[ne:ne + no], rest[ne + no]
        k = pl.program_id(2)

        @pl.when(k == 0)
        def _():
            acc[...] = jnp.zeros_like(acc)

        acc[...] += _dg(_bf(a_ref[...]), _bf(b_ref[...]), ca, cb)

        @pl.when(k == nk - 1)
        def _():
            vals = epi(acc[...], *[e[...] for e in ex])
            for r, v in zip(outs, vals):
                r[...] = v.astype(r.dtype)

    mn = pl.BlockSpec((tm, tn), lambda i, j, k: (i, j))
    rw = pl.BlockSpec((1, tn), lambda i, j, k: (0, j))
    return pl.pallas_call(
        body, name=name, grid=(M // tm, N // tn, nk),
        in_specs=[a_spec, b_spec] + [mn] * len(extras) + [rw] * len(rows), out_specs=[mn] * no,
        out_shape=[jax.ShapeDtypeStruct((M, N), dt) for dt in out_dtypes],
        scratch_shapes=[pltpu.VMEM((tm, tn), F32)],
        compiler_params=_cparams(("parallel", "parallel", "arbitrary")),
    )(a, b, *extras, *rows)


def _epi_res(acc, res):
    return (res + acc,)


def _epi_res_rms(acc, res, w):
    x = res + acc
    return (x, x * lax.rsqrt(jnp.mean(x * x, axis=-1, keepdims=True) + EPS) * w)


def _epi_relu2(acc):
    u = jnp.maximum(acc, 0.0)
    return (u, u * u)


def _epi_dup(acc, u):
    return (acc * 2.0 * u.astype(F32),)


def _conv(x, halo, w, i):
    halo = jnp.where(i == 0, 0.0, halo)
    xt = jnp.concatenate([halo, x], axis=0)
    shifted = [pltpu.roll(xt, 3 - k, 0)[8:, :] for k in range(3)] + [x]
    y = shifted[3] * w[3:4, :]
    for k in range(3):
        y = y + shifted[k] * w[k:k + 1, :]
    return y, shifted


def _l2n(x, scale):
    outs = []
    for h in range(x.shape[1] // 128):
        xh = x[:, 128 * h:128 * h + 128]
        outs.append(xh * (lax.rsqrt(jnp.sum(xh * xh, axis=-1, keepdims=True) + EPS) * scale))
    return jnp.concatenate(outs, axis=1)


def _l2n_bwd(x, dy, scale):
    outs = []
    for h in range(x.shape[1] // 128):
        xh, dh = x[:, 128 * h:128 * h + 128], dy[:, 128 * h:128 * h + 128] * scale
        r = lax.rsqrt(jnp.sum(xh * xh, axis=-1, keepdims=True) + EPS)
        outs.append(r * dh - xh * (r * r * r) * jnp.sum(xh * dh, axis=-1, keepdims=True))
    return jnp.concatenate(outs, axis=1)


def rms_fwd_fn(i, n, x, w):
    r = lax.rsqrt(jnp.mean(x * x, axis=-1, keepdims=True) + EPS)
    return (x * r * w,)


def rms_bwd_fn(i, n, x, dh, dres, w):
    r = lax.rsqrt(jnp.mean(x * x, axis=-1, keepdims=True) + EPS)
    g = dh * w
    dx = dres + r * g - x * (r * r * r) * jnp.mean(x * g, axis=-1, keepdims=True)
    return dx, dx, _rows(dh * x * r)


def rms_bwd_w_fn(i, n, x, dh, w):
    r = lax.rsqrt(jnp.mean(x * x, axis=-1, keepdims=True) + EPS)
    return (_rows(dh * x * r),)


def final_fn(i, n, x, tgt, w):
    r = lax.rsqrt(jnp.mean(x * x, axis=-1, keepdims=True) + EPS)
    xn = x * r
    e = xn * w - tgt
    dy = e * (1.0 / D)
    g = dy * w
    dx = r * g - x * (r * r * r) * jnp.mean(x * g, axis=-1, keepdims=True)
    return dx, dx, _rows(e * e), _rows(dy * xn)


def _gdn_gates(ba, alog_c, dtb_c):
    col = _iota(ba.shape, 1)
    amask = (col >= 8) & (col < 16)
    beta = jnp.where(col < 8, _sig(ba), 0.0)
    z = ba + dtb_c
    ea_ = jnp.exp(alog_c)
    return beta, z, ea_, jnp.where(amask, -ea_ * _softplus(z), 0.0), amask


def gdn_prep_fn(i, n, qkv, halo, ba, cw, alog_c, dtb_c, eb, ea):
    yc, _ = _conv(qkv, halo, cw, i)
    act = yc * _sig(yc)
    qn = _l2n(act[:, :D], GDN_DK ** -0.5)
    kn = _l2n(act[:, D:2 * D], 1.0)
    beta, _, _, g, _ = _gdn_gates(ba, alog_c, dtb_c)
    gcs = _chunk_cumsum(g, GDN_C)
    return qn, kn, act[:, 2 * D:], mm_sel(gcs, ea), mm_sel(beta, eb), jnp.transpose(gcs)[8:16, :]


def gdn_prep_bwd_fn(i, n, qkv, halo, ba, dqn, dkn, dv, dgcs_x, dbeta_x, dgcs_t, cw, alog_c, dtb_c, pb, pa):
    yc, shifted = _conv(qkv, halo, cw, i)
    sg = _sig(yc)
    act = yc * sg
    dq = _l2n_bwd(act[:, :D], dqn, GDN_DK ** -0.5)
    dk = _l2n_bwd(act[:, D:2 * D], dkn, 1.0)
    dyc = jnp.concatenate([dq, dk, dv], axis=1) * (sg * (1.0 + yc * (1.0 - sg)))
    dws = [_rows(dyc * shifted[k]) for k in range(4)]
    beta, z, ea_, g, amask = _gdn_gates(ba, alog_c, dtb_c)
    tbn = ba.shape[0]
    rowpart = jnp.transpose(jnp.concatenate([jnp.zeros((8, tbn), F32), dgcs_t, jnp.zeros((112, tbn), F32)], axis=0))
    dg = _chunk_revcumsum(mm_sel(dgcs_x, pa) - rowpart, GDN_C)
    draw = jnp.where(amask, dg * (-ea_) * _sig(z), 0.0)
    dba = draw + mm_sel(dbeta_x, pb) * beta * (1.0 - beta)
    return (dyc, dba, dws[0], dws[1], dws[2], dws[3], _rows(dg * g), _rows(draw))


def conv_bwd_fn(i, n, dyc, halo, w):
    halo = jnp.where(i == n - 1, 0.0, halo)
    tb = dyc.shape[0]
    xt = jnp.concatenate([dyc, halo], axis=0)
    dx = dyc * w[3:4, :]
    for k in range(3):
        dx = dx + pltpu.roll(xt, tb + 8 - (3 - k), 0)[:tb, :] * w[k:k + 1, :]
    return (dx,)


def gdn_post_fn(i, n, o, z, w):
    outs = []
    for h in range(GDN_H):
        oh, zh = o[:, 128 * h:128 * h + 128], z[:, 128 * h:128 * h + 128]
        r = lax.rsqrt(jnp.mean(oh * oh, axis=-1, keepdims=True) + EPS)
        outs.append(oh * r * w * (zh * _sig(zh)))
    return (jnp.concatenate(outs, axis=1),)


def gdn_post_bwd_fn(i, n, o, z, doa, w):
    dos, dzs, dw = [], [], None
    for h in range(GDN_H):
        sl = slice(128 * h, 128 * h + 128)
        oh, zh, dh = o[:, sl], z[:, sl], doa[:, sl]
        r = lax.rsqrt(jnp.mean(oh * oh, axis=-1, keepdims=True) + EPS)
        s = _sig(zh)
        dn = dh * (zh * s)
        dzs.append(dh * (oh * r * w) * (s * (1.0 + zh * (1.0 - s))))
        t = _rows(dn * oh * r)
        dw = t if dw is None else dw + t
        g = dn * w
        dos.append(r * g - oh * (r * r * r) * jnp.mean(oh * g, axis=-1, keepdims=True))
    return jnp.concatenate(dos, axis=1), jnp.concatenate(dzs, axis=1), dw


def _ssd_gates(dtblk, alog_c, dtb_c):
    hmask = _iota(dtblk.shape, 1) < SSM_H
    z = dtblk + dtb_c
    return jnp.where(hmask, _softplus(z), 0.0), -jnp.exp(alog_c), z, hmask


def ssd_prep_fn(i, n, xp, hx, bcp, hbc, dtblk, cwx, cwbc, cbx, cbbc, alog_c, dtb_c, e16):
    yx, _ = _conv(xp, hx, cwx, i)
    yx = yx + cbx
    ybc, _ = _conv(bcp, hbc, cwbc, i)
    ybc = ybc + cbbc
    dt, a_neg, _, _ = _ssd_gates(dtblk, alog_c, dtb_c)
    acs = _chunk_cumsum(dt * a_neg, SSM_L)
    return (yx * _sig(yx), ybc * _sig(ybc), mm_sel(dt, e16), mm_sel(acs, e16), jnp.transpose(acs)[0:SSM_H, :])


def ssd_prep_bwd_fn(i, n, xp, hx, bcp, hbc, dtblk, dxs_a, dxs_b, dbc, dgate, dacs_t, cwx, cwbc, cbx, cbbc, alog_c, dtb_c):
    yx, shx = _conv(xp, hx, cwx, i)
    yx = yx + cbx
    ybc, shbc = _conv(bcp, hbc, cwbc, i)
    ybc = ybc + cbbc
    sx, sbc = _sig(yx), _sig(ybc)
    dyx = (dxs_a + dxs_b) * (sx * (1.0 + yx * (1.0 - sx)))
    dybc = dbc * (sbc * (1.0 + ybc * (1.0 - sbc)))
    dwx = [_rows(dyx * shx[k]) for k in range(4)]
    dwbc = [_rows(dybc * shbc[k]) for k in range(4)]
    dt, a_neg, z, hmask = _ssd_gates(dtblk, alog_c, dtb_c)
    g0, g1 = dgate[:, :128], dgate[:, 128:]
    col = _iota(g0.shape, 1)
    lo, mid = col < 8, (col >= 8) & (col < 16)
    dacs_col = jnp.where(lo, g0, 0.0) + pltpu.roll(jnp.where(lo, g1, 0.0), 8, 1)
    ddt_dir = pltpu.roll(jnp.where(mid, g0, 0.0), 120, 1) + jnp.where(mid, g1, 0.0)
    tbn = dtblk.shape[0]
    rowpart = jnp.transpose(jnp.concatenate([dacs_t, jnp.zeros((128 - SSM_H, tbn), F32)], axis=0))
    da = _chunk_revcumsum(dacs_col - rowpart, SSM_L)
    draw = jnp.where(hmask, (ddt_dir + da * a_neg) * _sig(z), 0.0)
    return (dyx, dybc, draw, *dwx, *dwbc, _rows(dyx), _rows(dybc), _rows(da * dt * a_neg), _rows(draw))


def _ssd_gate(y, xs, zs, d_x):
    y2 = y + xs * d_x
    s = _sig(zs)
    return y2, s, y2 * (zs * s)


def ssd_post_fn(i, n, y, xs, zs, d_x, nw):
    _, _, yg = _ssd_gate(y, xs, zs, d_x)
    outs = []
    for g in range(2):
        v = yg[:, 512 * g:512 * g + 512]
        outs.append(v * lax.rsqrt(jnp.mean(v * v, axis=-1, keepdims=True) + EPS))
    return (jnp.concatenate(outs, axis=1) * nw,)


def ssd_post_bwd_fn(i, n, y, xs, zs, dob, d_x, nw):
    y2, s, yg = _ssd_gate(y, xs, zs, d_x)
    gfull = dob * nw
    dygs, dnw = [], []
    for g in range(2):
        sl = slice(512 * g, 512 * g + 512)
        v, gg = yg[:, sl], gfull[:, sl]
        r = lax.rsqrt(jnp.mean(v * v, axis=-1, keepdims=True) + EPS)
        dygs.append(r * gg - v * (r * r * r) * jnp.mean(v * gg, axis=-1, keepdims=True))
        dnw.append(_rows(dob[:, sl] * v * r))
    dyg = jnp.concatenate(dygs, axis=1)
    dy2 = dyg * (zs * s)
    dzs = dyg * y2 * (s * (1.0 + zs * (1.0 - s)))
    return dy2, dy2 * d_x, dzs, jnp.concatenate(dnw, axis=1), _rows(dy2 * xs)


def attn_fn(i, n, q, k, v):
    outs = []
    for h in range(MEM_H):
        sl = slice(MEM_DH * h, MEM_DH * h + MEM_DH)
        s = mm_nt(q[:, sl], k[:, sl]) * (MEM_DH ** -0.5)
        p = jnp.exp(s - jnp.max(s, axis=-1, keepdims=True))
        p = p / jnp.sum(p, axis=-1, keepdims=True)
        outs.append(mm(p, v[:, sl]))
    return (jnp.concatenate(outs, axis=1),)


def attn_bwd_fn(i, n, q, do, k, v):
    dqs, dks, dvs = [], [], []
    for h in range(MEM_H):
        sl = slice(MEM_DH * h, MEM_DH * h + MEM_DH)
        s = mm_nt(q[:, sl], k[:, sl]) * (MEM_DH ** -0.5)
        p = jnp.exp(s - jnp.max(s, axis=-1, keepdims=True))
        p = p / jnp.sum(p, axis=-1, keepdims=True)
        dvs.append(mm_tn(p, do[:, sl]))
        dp = mm_nt(do[:, sl], v[:, sl])
        ds = p * (dp - jnp.sum(dp * p, axis=-1, keepdims=True)) * (MEM_DH ** -0.5)
        dqs.append(mm(ds, k[:, sl]))
        dks.append(mm_tn(ds, q[:, sl]))
    return jnp.concatenate(dqs, axis=1), jnp.concatenate(dks, axis=1), jnp.concatenate(dvs, axis=1)


def add2_fn(i, n, a, b):
    return (a + b,)


def sum4_fn(i, n, a, b, c, d):
    return (((a.astype(F32) + b.astype(F32)) + c.astype(F32)) + d.astype(F32),)


def adamw_fn(i, n, w, g, m, v):
    m = ADAM_B1 * m + (1.0 - ADAM_B1) * g
    v = ADAM_B2 * v + (1.0 - ADAM_B2) * (g * g)
    m_hat = m / (1.0 - ADAM_B1 ** ADAM_STEP)
    v_hat = v / (1.0 - ADAM_B2 ** ADAM_STEP)
    delta = -ADAM_LR * (m_hat / (jnp.sqrt(v_hat) + ADAM_EPS) + ADAM_WD * w)
    return delta, m, v


def _gdn_stage1(q, k, v, gcs, grow, bb):
    C = GDN_C
    row, col = _iota((C, C), 0), _iota((C, C), 1)
    incl, strict = row >= col, row > col
    dmat = jnp.where(incl, jnp.exp(jnp.minimum(gcs[:, :C] - grow, 0.0)), 0.0)
    gam = jnp.exp(gcs)
    gl = gcs[C - 1:C, :]
    kb, vb = k * bb, v * bb
    kg = kb * gam
    lmat = jnp.where(strict, mm_nt(kb, k) * dmat, 0.0)
    pmat = jnp.where(incl, mm_nt(q, k) * dmat, 0.0)
    return dict(q=q, k=k, v=v, bb=bb, incl=incl, strict=strict, dmat=dmat, gam=gam, kb=kb, vb=vb, kg=kg,
                lmat=lmat, pmat=pmat, qd=q * gam, kdec=jnp.exp(gl - gcs), cd=jnp.exp(gl))


def _gdn_inverse(lmats):
    C = GDN_C
    eye = (_iota((C, C), 0) == _iota((C, C), 1)).astype(F32)
    xs = [-l for l in lmats]
    ts = [eye + x for x in xs]
    for _ in range(5):
        xs = [mm(x, x) for x in xs]
        ts = [t + mm(t, x) for t, x in zip(ts, xs)]
    res = [eye - mm3(eye + l, t) for l, t in zip(lmats, ts)]
    return [t + mm(t, r) for t, r in zip(ts, res)]


def gdn_fwd(qn, kn, v, gcs_x, beta_x, gcs_t, tb):
    T = qn.shape[0]
    nb, ncb, nc, C = T // tb, tb // GDN_C, T // GDN_C, GDN_C

    def body(q_ref, k_ref, v_ref, g_ref, b_ref, gt_ref, o_ref, st_ref, ti_ref, s_scr):
        @pl.when(pl.program_id(1) == 0)
        def _():
            s_scr[...] = jnp.zeros_like(s_scr)

        grow = gt_ref[...]
        st1 = []
        for c in range(ncb):
            sl = slice(C * c, C * (c + 1))
            st1.append(_gdn_stage1(q_ref[sl, :], k_ref[sl, :], v_ref[sl, :], g_ref[sl, :], grow[:, sl], b_ref[sl, :]))
        tinvs = _gdn_inverse([s["lmat"] for s in st1])
        us = [mm(t, s["vb"]) for t, s in zip(tinvs, st1)]
        ws = [mm(t, s["kg"]) for t, s in zip(tinvs, st1)]
        for c in range(ncb):
            sl = slice(C * c, C * (c + 1))
            lc = st1[c]
            ti_ref[sl, :] = tinvs[c]
            s = s_scr[...]
            st_ref[c] = s
            vn = us[c] - mm(ws[c], s)
            o_ref[sl, :] = mm(lc["qd"], s) + mm(lc["pmat"], vn)
            s_scr[...] = s * lc["cd"] + mm_tn(lc["k"] * lc["kdec"], vn)

    blk = pl.BlockSpec((tb, 128), lambda h, i: (i, h))
    return pl.pallas_call(
        body, name="gdn_fwd", grid=(GDN_H, nb),
        in_specs=[blk] * 5 + [pl.BlockSpec((None, 1, tb), lambda h, i: (h, 0, i))],
        out_specs=[blk, pl.BlockSpec((None, ncb, 128, 128), lambda h, i: (h, i, 0, 0)),
                   pl.BlockSpec((None, tb, C), lambda h, i: (h, i, 0))],
        out_shape=[jax.ShapeDtypeStruct((T, D), F32), jax.ShapeDtypeStruct((GDN_H, nc, 128, 128), F32),
                   jax.ShapeDtypeStruct((GDN_H, T, C), F32)],
        scratch_shapes=[pltpu.VMEM((128, 128), F32)],
        compiler_params=_cparams(("parallel", "arbitrary")),
    )(qn, kn, v, gcs_x, beta_x, gcs_t)


def gdn_bwd(qn, kn, v, gcs_x, beta_x, gcs_t, do, states, tinv, tb):
    T = qn.shape[0]
    nb, ncb, C = T // tb, tb // GDN_C, GDN_C

    def body(q_ref, k_ref, v_ref, g_ref, b_ref, gt_ref, do_ref, st_ref, ti_ref,
             dq_ref, dk_ref, dv_ref, dgc_ref, db_ref, dgr_ref, ds_scr):
        @pl.when(pl.program_id(1) == 0)
        def _():
            ds_scr[...] = jnp.zeros_like(ds_scr)

        grow = gt_ref[...]
        lastrow = _iota((C, 1), 0) == C - 1
        pre = []
        for c in range(ncb):
            sl = slice(C * c, C * (c + 1))
            lc = _gdn_stage1(q_ref[sl, :], k_ref[sl, :], v_ref[sl, :], g_ref[sl, :], grow[:, sl], b_ref[sl, :])
            tinv_c, s, do_c = ti_ref[sl, :], st_ref[c], do_ref[sl, :]
            u, w = mm(tinv_c, lc["vb"]), mm(tinv_c, lc["kg"])
            vn = u - mm(w, s)
            lc.update(tinv=tinv_c, s=s, u=u, w=w, vn=vn, dqd=mm_nt(do_c, s),
                      dp=jnp.where(lc["incl"], mm_nt(do_c, vn), 0.0), ds_q=mm_tn(lc["qd"], do_c),
                      dvn_p=mm_tn(lc["pmat"], do_c), kd=lc["k"] * lc["kdec"])
            pre.append(lc)
        rows = [None] * ncb
        for c in reversed(range(ncb)):
            sl = slice(C * c, C * (c + 1))
            lc = pre[c]
            q, k, vv, bb, gam, kd, u, w, s, tinv_c = (lc[n] for n in ("q", "k", "v", "bb", "gam", "kd", "u", "w", "s", "tinv"))
            dsn = ds_scr[...]
            dvn = lc["dvn_p"] + mm(kd, dsn)
            dkd = mm_nt(lc["vn"], dsn)
            dcd = _sum_all(s * dsn)
            ds_scr[...] = lc["ds_q"] + lc["cd"] * dsn - mm_tn(w, dvn)
            dw = -mm_nt(dvn, s)
            dvb, dkg = mm_tn(tinv_c, dvn), mm_tn(tinv_c, dw)
            da = -jnp.where(lc["strict"], mm_nt(dvb, u) + mm_nt(dkg, w), 0.0)
            dm = da * lc["dmat"]
            dn = lc["dp"] * lc["dmat"]
            dkb = mm(dm, k)
            e = da * lc["lmat"] + lc["dp"] * lc["pmat"]
            t_kd = _lanes(dkd * kd)
            dgl = _sum_all(t_kd) + dcd * lc["cd"][:, :1]
            dgcs = (_lanes(e) + _lanes(lc["dqd"] * lc["qd"]) - t_kd + _lanes(dkg * lc["kg"])
                    + jnp.where(lastrow, dgl, 0.0))
            rows[c] = _rows(e)
            dq_ref[sl, :] = mm(dn, k) + gam * lc["dqd"]
            dk_ref[sl, :] = (mm_tn(dm, lc["kb"]) + mm_tn(dn, q) + lc["kdec"] * dkd + bb * gam * dkg + bb * dkb)
            dv_ref[sl, :] = bb * dvb
            dbeta = _lanes(dkg * gam * k) + _lanes(dvb * vv) + _lanes(dkb * k)
            db_ref[sl, :] = jnp.broadcast_to(dbeta, (C, 128))
            dgc_ref[sl, :] = jnp.broadcast_to(dgcs, (C, 128))
        dgr_ref[...] = jnp.concatenate(rows, axis=1)

    blk = pl.BlockSpec((tb, 128), lambda h, i: (nb - 1 - i, h))
    rowspec = pl.BlockSpec((None, 1, tb), lambda h, i: (h, 0, nb - 1 - i))
    return pl.pallas_call(
        body, name="gdn_bwd", grid=(GDN_H, nb),
        in_specs=[blk] * 5 + [rowspec, blk,
                              pl.BlockSpec((None, ncb, 128, 128), lambda h, i: (h, nb - 1 - i, 0, 0)),
                              pl.BlockSpec((None, tb, C), lambda h, i: (h, nb - 1 - i, 0))],
        out_specs=[blk] * 5 + [rowspec],
        out_shape=[jax.ShapeDtypeStruct((T, D), F32)] * 5 + [jax.ShapeDtypeStruct((GDN_H, 1, T), F32)],
        scratch_shapes=[pltpu.VMEM((128, 128), F32)],
        compiler_params=_cparams(("parallel", "arbitrary")),
    )(qn, kn, v, gcs_x, beta_x, gcs_t, do, states, tinv)


def _ssd_pair(x2, dt2, acs2):
    last = acs2[SSM_L - 1:SSM_L, :]
    return jnp.exp(acs2), jnp.exp(last - acs2), x2 * dt2


def _ssd_head(hh, acs2, arow, dec2, cbm, bm, incl, col):
    lmask = (col >= 64 * hh) & (col < 64 * hh + 64)
    sg = jnp.where(incl, jnp.exp(jnp.minimum(acs2[:, 64 * hh:64 * hh + 1] - arow, 0.0)), 0.0)
    dec_col = dec2[:, 64 * hh:64 * hh + 1]
    return lmask, sg, sg * cbm, dec_col, bm * dec_col


def ssd_fwd(xs, bc, dt_x, acs_x, acs_t):
    T = xs.shape[0]
    nc, L = T // SSM_L, SSM_L

    def body(x_ref, b_ref, c_ref, dt_ref, ac_ref, at_ref, y_ref, hst_ref, h_scr):
        @pl.when(pl.program_id(1) == 0)
        def _():
            h_scr[...] = jnp.zeros_like(h_scr)

        bm, cm = b_ref[...], c_ref[...]
        cbm = mm_nt(cm, bm)
        row, col = _iota((L, L), 0), _iota((L, L), 1)
        incl = row >= col
        for pr in range(4):
            sl = slice(128 * pr, 128 * pr + 128)
            acs2 = ac_ref[:, sl]
            lam2, dec2, xd2 = _ssd_pair(x_ref[:, sl], dt_ref[:, sl], acs2)
            y2, st = None, []
            for hh in range(2):
                lmask, _, mmat, _, bd = _ssd_head(hh, acs2, at_ref[2 * pr + hh], dec2, cbm, bm, incl, col)
                t = mm(mmat, jnp.where(lmask, xd2, 0.0))
                y2 = t if y2 is None else y2 + t
                st.append(mm_tn(xd2, bd))
            hprev = h_scr[pr]
            hst_ref[pr] = hprev
            y_ref[:, sl] = y2 + lam2 * mm_nt(cm, hprev)
            lam_rows = jnp.where(row < 64, lam2[L - 1:L, 0:1], lam2[L - 1:L, 64:65])
            h_scr[pr] = lam_rows * hprev + jnp.where(row < 64, st[0], st[1])

    return pl.pallas_call(
        body, name="ssd_fwd", grid=(2, nc),
        in_specs=[pl.BlockSpec((L, 512), lambda g, c: (c, g)),
                  pl.BlockSpec((L, 128), lambda g, c: (c, g)),
                  pl.BlockSpec((L, 128), lambda g, c: (c, 2 + g)),
                  pl.BlockSpec((L, 512), lambda g, c: (c, g)),
                  pl.BlockSpec((L, 512), lambda g, c: (c, g)),
                  pl.BlockSpec((8, 1, L), lambda g, c: (g, 0, c))],
        out_specs=[pl.BlockSpec((L, 512), lambda g, c: (c, g)),
                   pl.BlockSpec((None, None, 4, 128, 128), lambda g, c: (g, c, 0, 0, 0))],
        out_shape=[jax.ShapeDtypeStruct((T, D), F32), jax.ShapeDtypeStruct((2, nc, 4, 128, 128), F32)],
        scratch_shapes=[pltpu.VMEM((4, 128, 128), F32)],
        compiler_params=_cparams(("parallel", "arbitrary")),
    )(xs, bc, bc, dt_x, acs_x, acs_t)


def ssd_bwd(xs, bc, dt_x, acs_x, acs_t, dy, hstates):
    T = xs.shape[0]
    nc, L = T // SSM_L, SSM_L

    def body(x_ref, b_ref, c_ref, dt_ref, ac_ref, at_ref, dy_ref, hst_ref,
             dx_ref, db_ref, dc_ref, dgate_ref, dar_ref, dh_scr):
        @pl.when(pl.program_id(1) == 0)
        def _():
            dh_scr[...] = jnp.zeros_like(dh_scr)

        bm, cm = b_ref[...], c_ref[...]
        cbm = mm_nt(cm, bm)
        row, col = _iota((L, L), 0), _iota((L, L), 1)
        rowc = _iota((L, 1), 0)
        incl = row >= col
        dcb = jnp.zeros((L, L), F32)
        dbm = jnp.zeros((L, SSM_N), F32)
        dcm = jnp.zeros((L, SSM_N), F32)
        comp = jnp.zeros((L, 128), F32)
        for pr in range(4):
            sl = slice(128 * pr, 128 * pr + 128)
            x2, dt2, dy2, acs2 = x_ref[:, sl], dt_ref[:, sl], dy_ref[:, sl], ac_ref[:, sl]
            lam2, dec2, xd2 = _ssd_pair(x2, dt2, acs2)
            hprev, dhn = hst_ref[pr], dh_scr[pr]
            dz = lam2 * dy2
            yoff = dz * mm_nt(cm, hprev)
            dcm = dcm + mm(dz, hprev)
            q_rows = _lanes(dhn * hprev)
            dxd2 = jnp.zeros((L, 128), F32)
            for hh in range(2):
                lmask, sg, mmat, dec_col, bd = _ssd_head(hh, acs2, at_ref[2 * pr + hh], dec2, cbm, bm, incl, col)
                dm = jnp.where(incl, mm_nt(jnp.where(lmask, dy2, 0.0), xd2), 0.0)
                dcb = dcb + dm * sg
                e = dm * mmat
                dxd_h = jnp.where(lmask, mm_tn(mmat, dy2) + mm_nt(bd, dhn), 0.0)
                dxd2 = dxd2 + dxd_h
                dbd = mm(jnp.where(lmask, xd2, 0.0), dhn)
                dbm = dbm + dec_col * dbd
                t = _lanes(dbd * bd)
                lam_h = lam2[L - 1:L, 64 * hh:64 * hh + 1]
                in_head = (rowc >= 64 * hh) & (rowc < 64 * hh + 64)
                add_last = _sum_all(t) + _sum_all(jnp.where(in_head, q_rows, 0.0)) * lam_h
                dacs_col = (_lanes(jnp.where(lmask, yoff, 0.0)) + _lanes(e) - t
                            + jnp.where(rowc == L - 1, add_last, 0.0))
                ddt_col = _lanes(dxd_h * x2)
                j = 2 * pr + hh
                dar_ref[j] = _rows(e)
                comp = comp + jnp.where(col == j, dacs_col, 0.0) + jnp.where(col == 8 + j, ddt_col, 0.0)
            lam_rows = jnp.where(row < 64, lam2[L - 1:L, 0:1], lam2[L - 1:L, 64:65])
            dh_scr[pr] = mm_tn(dz, cm) + lam_rows * dhn
            dx_ref[:, sl] = dt2 * dxd2
        db_ref[...] = dbm + mm_tn(dcb, cm)
        dc_ref[...] = dcm + mm(dcb, bm)
        dgate_ref[...] = comp

    rv = lambda g, c: (nc - 1 - c, g)
    rowspec = pl.BlockSpec((8, 1, L), lambda g, c: (g, 0, nc - 1 - c))
    return pl.pallas_call(
        body, name="ssd_bwd", grid=(2, nc),
        in_specs=[pl.BlockSpec((L, 512), rv),
                  pl.BlockSpec((L, 128), rv),
                  pl.BlockSpec((L, 128), lambda g, c: (nc - 1 - c, 2 + g)),
                  pl.BlockSpec((L, 512), rv),
                  pl.BlockSpec((L, 512), rv),
                  rowspec,
                  pl.BlockSpec((L, 512), rv),
                  pl.BlockSpec((None, None, 4, 128, 128), lambda g, c: (g, nc - 1 - c, 0, 0, 0))],
        out_specs=[pl.BlockSpec((L, 512), rv), pl.BlockSpec((L, 128), rv), pl.BlockSpec((L, 128), rv),
                   pl.BlockSpec((L, 128), rv), rowspec],
        out_shape=[jax.ShapeDtypeStruct((T, D), F32), jax.ShapeDtypeStruct((T, 256), F32),
                   jax.ShapeDtypeStruct((T, 256), F32), jax.ShapeDtypeStruct((T, 256), F32),
                   jax.ShapeDtypeStruct((SSM_H, 1, T), F32)],
        scratch_shapes=[pltpu.VMEM((4, 128, 128), F32)],
        compiler_params=_cparams(("parallel", "arbitrary")),
    )(xs, bc, bc, dt_x, acs_x, acs_t, dy, hstates)


def _pos():
    return lax.axis_index("x"), lax.axis_index("y"), lax.axis_index("c")


def _other_chips(x, y):
    return [(1 - x, y), (x, 1 - y), (1 - x, 1 - y)]


def _rcopy(src, dst, ssem, rsem, dev):
    return pltpu.make_async_remote_copy(src_ref=src, dst_ref=dst, send_sem=ssem, recv_sem=rsem,
                                        device_id=dev, device_id_type=MESH)


def _rows_at(start, n):
    return pl.ds(pl.multiple_of(start, 8), n)


def _comm_call(body, name, out_shape, n_in, scratch):
    return pl.pallas_call(
        body, name=name, out_shape=out_shape, in_specs=[ANY] * n_in,
        out_specs=[ANY] * len(out_shape) if isinstance(out_shape, (list, tuple)) else ANY,
        scratch_shapes=scratch,
        compiler_params=pltpu.CompilerParams(has_side_effects=True),
    )


def _dma_sems(n):
    return pltpu.SemaphoreType.DMA((n,))


def ag_chips(name, shard):
    rr, cc = shard.shape
    h, nq = rr // 2, ICI_CHUNKS
    hq = h // nq

    def body(x_ref, out_ref, ssem, rsem):
        x, y, c = _pos()
        chips = _other_chips(x, y)
        started = []
        for q in range(nq):
            rows = _rows_at(c * h + q * hq, hq)
            for j, (cx, cy) in enumerate(chips):
                cp = _rcopy(x_ref.at[rows], out_ref.at[j, rows], ssem.at[j * nq + q], rsem.at[j * nq + q], (cx, cy, c))
                cp.start()
                started.append(cp)
        for q in range(nq):
            rows = _rows_at(c * h + q * hq, hq)
            for j, (cx, cy) in enumerate(chips):
                blk = out_ref.at[j, rows]
                _rcopy(blk, blk, ssem.at[j * nq + q], rsem.at[j * nq + q], (cx, cy, c)).wait_recv()
                k = 3 * nq + j * nq + q
                cp = _rcopy(blk, blk, ssem.at[k], rsem.at[k], (x, y, 1 - c))
                cp.start()
                started.append(cp)
        for q in range(nq):
            rows = _rows_at((1 - c) * h + q * hq, hq)
            for j in range(3):
                blk = out_ref.at[j, rows]
                k = 3 * nq + j * nq + q
                _rcopy(blk, blk, ssem.at[k], rsem.at[k], (x, y, 1 - c)).wait_recv()
        for cp in started:
            cp.wait_send()

    return _comm_call(body, name, jax.ShapeDtypeStruct((3, rr, cc), shard.dtype), 1,
                      [_dma_sems(6 * nq), _dma_sems(6 * nq)])(shard)


def all_gather_chips(name, shard, s_me):
    got = ag_chips(name, shard)
    by_rel = jnp.stack([shard, got[1], got[0], got[2]])
    return jnp.take(by_rel, jnp.arange(4) ^ s_me, axis=0)


def rs_pair(name, g):
    _, rr, cc = g.shape
    h, nq = rr // 2, D2D_CHUNKS
    hq = h // nq

    def body(g_ref, recv_ref, ssem, rsem):
        x, y, c = _pos()
        cps = []
        for q in range(nq):
            cp = _rcopy(g_ref.at[:, _rows_at((1 - c) * h + q * hq, hq), :], recv_ref.at[:, pl.ds(q * hq, hq), :],
                        ssem.at[q], rsem.at[q], (x, y, 1 - c))
            cp.start()
            cps.append(cp)
        for cp in cps:
            cp.wait()

    return _comm_call(body, name, jax.ShapeDtypeStruct((4, h, cc), g.dtype), 1, [_dma_sems(nq), _dma_sems(nq)])(g)


def rs_chips(name, p):
    _, h, cc = p.shape
    nq = ICI_CHUNKS
    hq = h // nq

    def body(p_ref, buf_ref, ssem, rsem):
        x, y, c = _pos()
        sends = []
        for q in range(nq):
            rows = pl.ds(q * hq, hq)
            for j, (cx, cy) in enumerate(_other_chips(x, y)):
                cp = _rcopy(p_ref.at[2 * cx + cy, rows], buf_ref.at[j, rows], ssem.at[j * nq + q],
                            rsem.at[j * nq + q], (cx, cy, c))
                cp.start()
                sends.append(cp)
        for cp in sends:
            cp.wait()

    return _comm_call(body, name, jax.ShapeDtypeStruct((3, h, cc), p.dtype), 1,
                      [_dma_sems(3 * nq), _dma_sems(3 * nq)])(p)


def rs_join(name, half):
    h, cc = half.shape
    nq = D2D_CHUNKS
    hq = h // nq

    def body(h_ref, out_ref, ssem, rsem):
        x, y, c = _pos()
        cps = []
        for q in range(nq):
            rows = pl.ds(q * hq, hq)
            cp = _rcopy(h_ref.at[rows], out_ref.at[rows], ssem.at[q], rsem.at[q], (x, y, 1 - c))
            cp.start()
            cps.append(cp)
        for cp in cps:
            cp.wait()

    return _comm_call(body, name, jax.ShapeDtypeStruct((h, cc), half.dtype), 1, [_dma_sems(nq), _dma_sems(nq)])(half)


def reduce_scatter(tag, g, tb, s_me, ci):
    _, rr, cc = g.shape
    h = rr // 2
    recv = rs_pair(tag + "_pair", g)
    mine = lax.dynamic_slice_in_dim(g, ci * h, h, axis=1)
    part = rowwise(add2_fn, tag + "_add", 4 * h, tb, [R(mine.reshape(4 * h, cc)), R(recv.reshape(4 * h, cc))],
                   [], [(cc, BF16)])[0].reshape(4, h, cc)
    buf = rs_chips(tag + "_chips", part).reshape(3 * h, cc)
    own = lax.dynamic_index_in_dim(part, s_me, 0, keepdims=False)
    nb = h // tb
    red = rowwise(sum4_fn, tag + "_sum", h, tb, [R(own)] + [R(buf, off=k * nb) for k in range(3)], [], [(cc, F32)])[0]
    other = rs_join(tag + "_join", red)
    return lax.dynamic_update_slice_in_dim(jnp.concatenate([other, other], axis=0), red, ci * h, 0)


def all_reduce_small(name, v):
    rows = v.shape[0]

    def body(v_ref, out_ref, buf, ssem, rsem):
        x, y, c = _pos()
        me = 4 * x + 2 * y + c
        buf[me] = v_ref[...]
        cps = []
        for k in range(1, 8):
            dev = (x ^ (k >> 2), y ^ ((k >> 1) & 1), c ^ (k & 1))
            cp = _rcopy(v_ref, buf.at[me], ssem.at[k - 1], rsem.at[k - 1], dev)
            cp.start()
            cps.append(cp)
        for cp in cps:
            cp.wait()
        acc = buf[0]
        for d in range(1, 8):
            acc = acc + buf[d]
        out_ref[...] = acc

    return pl.pallas_call(
        body, name=name, out_shape=jax.ShapeDtypeStruct((rows, 128), F32),
        in_specs=[pl.BlockSpec(memory_space=pltpu.VMEM)], out_specs=pl.BlockSpec(memory_space=pltpu.VMEM),
        scratch_shapes=[pltpu.VMEM((8, rows, 128), F32), _dma_sems(7), _dma_sems(7)],
        compiler_params=pltpu.CompilerParams(has_side_effects=True),
    )(v)


def _pack(vs):
    parts, offs, r = [], [], 0
    for v in vs:
        n = v.size
        nr = -(-n // 128)
        parts.append(jnp.pad(v.reshape(-1).astype(F32), (0, nr * 128 - n)))
        offs.append((r, n))
        r += nr
    pad = (-r) % 8
    if pad:
        parts.append(jnp.zeros((pad * 128,), F32))
    return jnp.concatenate(parts).reshape(r + pad, 128), offs


def _unpack(buf, offs, shapes):
    flat = buf.reshape(-1)
    return [flat[r * 128:r * 128 + n].reshape(s) for (r, n), s in zip(offs, shapes)]


def _sel(rows, cols, pairs):
    m = np.zeros((rows, cols), np.float32)
    for r, c in pairs:
        m[r, c] = 1.0
    return jnp.asarray(m)


def _pad_win(w):
    z = jnp.zeros((w.shape[0], 112), w.dtype)
    return jnp.concatenate([w[:, :4096], w[:, 4112:6672], w[:, 4096:4112], z, w[:, 6672:6688], z], axis=1)


def _unpad_win(wp):
    return jnp.concatenate([wp[:, :4096], wp[:, 6656:6672], wp[:, 4096:6656], wp[:, 6784:6800]], axis=1)


def kernel(x, mem, norm1_w, w_in, gdn_conv_w, gdn_a_log, gdn_dt_bias, gdn_norm_w, ssm_conv_w, ssm_conv_b, ssm_a_log, ssm_dt_bias, ssm_d, ssm_norm_w, w_out, norm2_w, mem_norm_w, wq_mem, wk_mem, wv_mem, wo_mem, norm3_w, w_up, w_down, final_norm_w, loss_target, m_norm1_w, m_w_in, m_gdn_conv_w, m_gdn_a_log, m_gdn_dt_bias, m_gdn_norm_w, m_ssm_conv_w, m_ssm_conv_b, m_ssm_a_log, m_ssm_dt_bias, m_ssm_d, m_ssm_norm_w, m_w_out, m_norm2_w, m_mem_norm_w, m_wq_mem, m_wk_mem, m_wv_mem, m_wo_mem, m_norm3_w, m_w_up, m_w_down, m_final_norm_w, v_norm1_w, v_w_in, v_gdn_conv_w, v_gdn_a_log, v_gdn_dt_bias, v_gdn_norm_w, v_ssm_conv_w, v_ssm_conv_b, v_ssm_a_log, v_ssm_dt_bias, v_ssm_d, v_ssm_norm_w, v_w_out, v_norm2_w, v_mem_norm_w, v_wq_mem, v_wk_mem, v_wv_mem, v_wo_mem, v_norm3_w, v_w_up, v_w_down, v_final_norm_w):
    T, M = x.shape[1], mem.shape[1]
    xi, yi, ci = _pos()
    s_me = 2 * xi + yi
    x0, mem0, tgt = x[0], mem[0], loss_target[0]
    tb = min(256, T)
    row = lambda v: v.reshape(1, -1)

    win_g = all_gather_chips("ag_win", w_in.astype(BF16), s_me)
    rest_g = all_gather_chips("ag_rest", jnp.concatenate([w_out, wq_mem, wk_mem, wv_mem, wo_mem, w_up, w_down],
                                                         axis=0).astype(BF16), s_me)
    w_in_p = _pad_win(win_g.transpose(1, 0, 2).reshape(D, IN_COLS))
    wout_f = rest_g[:, 0:512].reshape(2 * D, D)
    wq_f, wk_f, wv_f, wo_f = (rest_g[:, 512 + 256 * k:768 + 256 * k].reshape(D, D) for k in range(4))
    wup_f = rest_g[:, 1536:2560].transpose(1, 0, 2).reshape(D, D_FF)
    wdown_f = rest_g[:, 2560:3584].reshape(D_FF, D)
    keep = (ci == 0).astype(F32)
    gcw_z = lax.dynamic_update_slice(jnp.zeros((4, 3 * D), F32), gdn_conv_w * keep, (0, s_me * 768))
    scw_z = lax.dynamic_update_slice(jnp.zeros((4, 1536), F32), ssm_conv_w * keep, (0, s_me * 384))
    cbuf, coffs = _pack([gcw_z, scw_z])
    gcw, scw = _unpack(all_reduce_small("ar_convw", cbuf), coffs, [(4, 3 * D), (4, 1536)])
    scw_x, scw_bc = scw[:, :D], scw[:, D:]
    scb_x, scb_bc = row(ssm_conv_b[:D]), row(ssm_conv_b[D:])

    galog_c, gdtb_c = row(jnp.pad(gdn_a_log, (8, 112))), row(jnp.pad(gdn_dt_bias, (8, 112)))
    salog_c, sdtb_c = row(jnp.pad(ssm_a_log, (0, 112))), row(jnp.pad(ssm_dt_bias, (0, 112)))
    sd_x = row(jnp.repeat(ssm_d, 64))
    eb = _sel(128, D, [(h, 128 * h + l) for h in range(8) for l in range(128)])
    ea = _sel(128, D, [(8 + h, 128 * h + l) for h in range(8) for l in range(128)])
    e16 = _sel(128, D, [(h, 64 * h + l) for h in range(16) for l in range(64)])
    pb = _sel(D, 128, [(128 * h, h) for h in range(8)])
    pa = _sel(D, 128, [(128 * h, 8 + h) for h in range(8)])

    h1 = rowwise(rms_fwd_fn, "rms1", T, tb, [R(x0)], [row(norm1_w)], [(D, BF16)])[0]
    p = matmul("mm_in", h1, w_in_p, "nn", 1024, 768, 1024, [F32])[0]
    gp_ins = [R(p, 3 * D, CB_QKV, "prev"), R(p, 128, CB_BA)]
    qn, kn, vv, gcs_x, beta_x, gcs_t = rowwise(gdn_prep_fn, "gdn_prep", T, tb, gp_ins,
                                               [gcw, galog_c, gdtb_c, eb, ea], [(D, F32)] * 5 + [(-8, F32)])
    gcs_t = gcs_t.reshape(GDN_H, 1, T)
    gtb = min(512, T)
    o_gdn, s_states, tinv = gdn_fwd(qn, kn, vv, gcs_x, beta_x, gcs_t, gtb)
    gnw = row(gdn_norm_w)
    oa = rowwise(gdn_post_fn, "gdn_post", T, tb, [R(o_gdn), R(p, D, CB_Z)], [gnw], [(D, BF16)])[0]
    sp_ins = [R(p, D, CB_XS, "prev"), R(p, 512, CB_BC, "prev"), R(p, 128, CB_DT)]
    sp_full = [scw_x, scw_bc, scb_x, scb_bc, salog_c, sdtb_c]
    xs, bc, dt_x, acs_x, acs_t = rowwise(ssd_prep_fn, "ssd_prep", T, tb, sp_ins, sp_full + [e16],
                                         [(D, F32), (512, F32), (D, F32), (D, F32), (-SSM_H, F32)])
    acs_t = acs_t.reshape(SSM_H, 1, T)
    y_ssd, h_states = ssd_fwd(xs, bc, dt_x, acs_x, acs_t)
    snw = row(ssm_norm_w)
    ob = rowwise(ssd_post_fn, "ssd_post", T, tb, [R(y_ssd), R(xs), R(p, D, CB_ZS)], [sd_x, snw], [(D, BF16)])[0]
    x1a = matmul("mm_out_a", oa, wout_f[:D], "nn", 1024, 1024, 1024, [F32], _epi_res, [x0])[0]
    assert D == 1024
    x1, h2 = matmul("mm_out_b", ob, wout_f[D:], "nn", 1024, 1024, 1024, [F32, BF16], _epi_res_rms, [x1a],
                    [row(norm2_w)])

    mn = rowwise(rms_fwd_fn, "rms_mem", M, M, [R(mem0)], [row(mem_norm_w)], [(D, BF16)])[0]
    km = matmul("mm_k", mn, wk_f, "nn", 256, 1024, 1024, [BF16])[0]
    vm = matmul("mm_v", mn, wv_f, "nn", 256, 1024, 1024, [BF16])[0]
    qm = matmul("mm_q", h2, wq_f, "nn", 1024, 1024, 1024, [BF16])[0]
    ao = rowwise(attn_fn, "attn", T, tb, [R(qm)], [km, vm], [(D, BF16)])[0]
    x2, h3 = matmul("mm_o", ao, wo_f, "nn", 1024, 1024, 1024, [F32, BF16], _epi_res_rms, [x1], [row(norm3_w)])
    u, act = matmul("mm_up", h3, wup_f, "nn", 1024, 1024, 1024, [BF16, BF16], _epi_relu2)
    x3 = matmul("mm_down", act, wdown_f, "nn", 1024, 1024, 1024, [F32], _epi_res, [x2])[0]

    dx3, dx3b, loss_lane, g_final = rowwise(final_fn, "final", T, tb, [R(x3), R(tgt)], [row(final_norm_w)],
                                            [(D, F32), (D, BF16)], [(1, D), (1, D)])
    loss = lax.psum(0.5 / D * jnp.sum(loss_lane), ("x", "y", "c"))

    dup = matmul("mm_dact", dx3b, wdown_f, "nt", 1024, 1024, 1024, [BF16], _epi_dup, [u])[0]
    g_wdown = matmul("mm_gdown", act, dx3b, "tn", 1024, 1024, 1024, [F32])[0]
    dh3 = matmul("mm_dh3", dup, wup_f, "nt", 1024, 1024, 1024, [F32])[0]
    g_wup = matmul("mm_gup", h3, dup, "tn", 1024, 1024, 1024, [F32])[0]
    dx2, dx2b, g_n3 = rowwise(rms_bwd_fn, "rms3_bwd", T, tb, [R(x2), R(dh3), R(dx3)], [row(norm3_w)],
                              [(D, F32), (D, BF16)], [(1, D)])
    dao = matmul("mm_dao", dx2b, wo_f, "nt", 1024, 1024, 1024, [F32])[0]
    g_wo = matmul("mm_gwo", ao, dx2b, "tn", 1024, 1024, 1024, [F32])[0]
    dqm, dkm, dvm = rowwise(attn_bwd_fn, "attn_bwd", T, tb, [R(qm), R(dao)], [km, vm], [(D, BF16)],
                            [(M, D), (M, D)])
    dh2 = matmul("mm_dh2", dqm, wq_f, "nt", 1024, 1024, 1024, [F32])[0]
    g_wq = matmul("mm_gwq", h2, dqm, "tn", 1024, 1024, 1024, [F32])[0]
    g_wk = matmul("mm_gwk", mn, dkm, "tn", 1024, 1024, 256, [F32])[0]
    g_wv = matmul("mm_gwv", mn, dvm, "tn", 1024, 1024, 256, [F32])[0]
    dmn_k = matmul("mm_dmk", dkm, wk_f, "nt", 256, 1024, 1024, [F32])[0]
    dmn = matmul("mm_dmv", dvm, wv_f, "nt", 256, 1024, 1024, [F32], _epi_res, [dmn_k])[0]
    g_nmem = rowwise(rms_bwd_w_fn, "rmsmem_bwd", M, M, [R(mem0), R(dmn)], [row(mem_norm_w)], [], [(1, D)])[0]
    dx1, dx1b, g_n2 = rowwise(rms_bwd_fn, "rms2_bwd", T, tb, [R(x1), R(dh2), R(dx2)], [row(norm2_w)],
                              [(D, F32), (D, BF16)], [(1, D)])
    doa = matmul("mm_doa", dx1b, wout_f[:D], "nt", 1024, 1024, 1024, [F32])[0]
    dob = matmul("mm_dob", dx1b, wout_f[D:], "nt", 1024, 1024, 1024, [F32])[0]
    g_wout_a = matmul("mm_gwout_a", oa, dx1b, "tn", 1024, 1024, 1024, [F32])[0]
    g_wout_b = matmul("mm_gwout_b", ob, dx1b, "tn", 1024, 1024, 1024, [F32])[0]

    dy_ssd, dxs_dir, dzs, g_snw, g_sd_lane = rowwise(
        ssd_post_bwd_fn, "ssd_post_bwd", T, tb, [R(y_ssd), R(xs), R(p, D, CB_ZS), R(dob)], [sd_x, snw],
        [(D, F32), (D, F32), (D, BF16)], [(1, D), (1, D)])
    dxs_scan, db_s, dc_s, dgate, dacs_t = ssd_bwd(xs, bc, dt_x, acs_x, acs_t, dy_ssd, h_states)
    dbc = jnp.concatenate([db_s, dc_s], axis=1)
    spb = rowwise(ssd_prep_bwd_fn, "ssd_prep_bwd", T, tb,
                  sp_ins + [R(dxs_scan), R(dxs_dir), R(dbc), R(dgate), RC(dacs_t.reshape(SSM_H, T))], sp_full,
                  [(D, F32), (512, F32), (128, BF16)],
                  [(1, D)] * 4 + [(1, 512)] * 4 + [(1, D), (1, 512), (1, 128), (1, 128)])
    dyc_x, dyc_bc, ddt_blk = spb[:3]
    g_scw = jnp.concatenate([jnp.concatenate(spb[3:7], axis=0), jnp.concatenate(spb[7:11], axis=0)], axis=1)
    g_scb = jnp.concatenate([spb[11], spb[12]], axis=1).reshape(-1)
    g_salog, g_sdtb = spb[13][0, :SSM_H], spb[14][0, :SSM_H]
    dp_xs = rowwise(conv_bwd_fn, "conv_bwd_x", T, tb, [R(dyc_x, halo="next")], [scw_x], [(D, BF16)])[0]
    dp_bc = rowwise(conv_bwd_fn, "conv_bwd_bc", T, tb, [R(dyc_bc, halo="next")], [scw_bc], [(512, BF16)])[0]

    do_gdn, dz, g_gnw = rowwise(gdn_post_bwd_fn, "gdn_post_bwd", T, tb, [R(o_gdn), R(p, D, CB_Z), R(doa)], [gnw],
                                [(D, F32), (D, BF16)], [(1, 128)])
    dqn, dkn, dvv, dgcs_x, dbeta_x, dgcs_t = gdn_bwd(qn, kn, vv, gcs_x, beta_x, gcs_t, do_gdn, s_states, tinv, gtb)
    gpb = rowwise(gdn_prep_bwd_fn, "gdn_prep_bwd", T, tb,
                  gp_ins + [R(dqn), R(dkn), R(dvv), R(dgcs_x), R(dbeta_x), RC(dgcs_t.reshape(GDN_H, T))],
                  [gcw, galog_c, gdtb_c, pb, pa],
                  [(3 * D, F32), (128, BF16)], [(1, 3 * D)] * 4 + [(1, 128), (1, 128)])
    dyc_qkv, dba = gpb[:2]
    g_gcw = jnp.concatenate(gpb[2:6], axis=0)
    g_galog, g_gdtb = gpb[6][0, 8:16], gpb[7][0, 8:16]
    dp_qkv = rowwise(conv_bwd_fn, "conv_bwd_qkv", T, tb, [R(dyc_qkv, halo="next")], [gcw], [(3 * D, BF16)])[0]

    dp = jnp.concatenate([dp_qkv, dz, dzs, dp_xs, dp_bc, dba, ddt_blk], axis=1)
    dh1 = matmul("mm_dh1", dp, w_in_p, "nt", 1024, 1024, 768, [F32])[0]
    g_win_p = matmul("mm_gwin", h1, dp, "tn", 1024, 768, 1024, [F32])[0]
    grad_x, _, g_n1 = rowwise(rms_bwd_fn, "rms1_bwd", T, tb, [R(x0), R(dh1), R(dx1)], [row(norm1_w)],
                              [(D, F32), (D, BF16)], [(1, D)])

    small = [g_n1, g_galog, g_gdtb, g_gnw, g_scb, g_salog, g_sdtb, g_sd_lane.reshape(16, 64).sum(axis=1), g_snw,
             g_n2, g_nmem, g_n3, g_final, g_gcw, g_scw]
    sshapes = [(D,), (8,), (8,), (128,), (1536,), (16,), (16,), (16,), (D,), (D,), (D,), (D,), (D,),
               (4, 3 * D), (4, 1536)]
    sbuf, soffs = _pack(small)
    sg = _unpack(all_reduce_small("ar_grads", sbuf), soffs, sshapes)
    (gr_n1, gr_galog, gr_gdtb, gr_gnw, gr_scb, gr_salog, gr_sdtb, gr_sd, gr_snw, gr_n2, gr_nmem, gr_n3,
     gr_final, gr_gcw_full, gr_scw_full) = sg
    gr_gcw = lax.dynamic_slice(gr_gcw_full, (0, s_me * 768), (4, 768))
    gr_scw = lax.dynamic_slice(gr_scw_full, (0, s_me * 384), (4, 384))

    g_win = _unpad_win(g_win_p).reshape(D, 4, IN_COLS // 4).transpose(1, 0, 2)
    gr_win = reduce_scatter("rs_win", g_win, 256, s_me, ci)
    g_rest = jnp.concatenate([
        jnp.concatenate([g_wout_a, g_wout_b], axis=0).reshape(4, 512, D),
        g_wq.reshape(4, 256, D), g_wk.reshape(4, 256, D), g_wv.reshape(4, 256, D), g_wo.reshape(4, 256, D),
        g_wup.reshape(D, 4, D).transpose(1, 0, 2), g_wdown.reshape(4, D, D)], axis=1)
    gr_rest = reduce_scatter("rs_rest", g_rest, 256, s_me, ci)
    gr_wout, gr_wq, gr_wk, gr_wv, gr_wo = (gr_rest[0:512], gr_rest[512:768], gr_rest[768:1024],
                                           gr_rest[1024:1280], gr_rest[1280:1536])
    gr_wup, gr_wdown = gr_rest[1536:2560], gr_rest[2560:3584]

    def adam_big(name, w, g, m, v, tbr):
        return rowwise(adamw_fn, name, w.shape[0], tbr, [R(w), R(g), R(m), R(v)], [], [(w.shape[1], F32)] * 3)

    big = {
        "w_in": adam_big("adam_win", w_in, gr_win, m_w_in, v_w_in, 256),
        "w_out": adam_big("adam_wout", w_out, gr_wout, m_w_out, v_w_out, 256),
        "wq_mem": adam_big("adam_wq", wq_mem, gr_wq, m_wq_mem, v_wq_mem, 256),
        "wk_mem": adam_big("adam_wk", wk_mem, gr_wk, m_wk_mem, v_wk_mem, 256),
        "wv_mem": adam_big("adam_wv", wv_mem, gr_wv, m_wv_mem, v_wv_mem, 256),
        "wo_mem": adam_big("adam_wo", wo_mem, gr_wo, m_wo_mem, v_wo_mem, 256),
        "w_up": adam_big("adam_wup", w_up, gr_wup, m_w_up, v_w_up, 256),
        "w_down": adam_big("adam_wdown", w_down, gr_wdown, m_w_down, v_w_down, 256),
    }
    names_s = ["norm1_w", "gdn_conv_w", "gdn_a_log", "gdn_dt_bias", "gdn_norm_w", "ssm_conv_w", "ssm_conv_b",
               "ssm_a_log", "ssm_dt_bias", "ssm_d", "ssm_norm_w", "norm2_w", "mem_norm_w", "norm3_w", "final_norm_w"]
    w_s = [norm1_w, gdn_conv_w, gdn_a_log, gdn_dt_bias, gdn_norm_w, ssm_conv_w, ssm_conv_b, ssm_a_log, ssm_dt_bias,
           ssm_d, ssm_norm_w, norm2_w, mem_norm_w, norm3_w, final_norm_w]
    g_s = [gr_n1, gr_gcw, gr_galog, gr_gdtb, gr_gnw, gr_scw, gr_scb, gr_salog, gr_sdtb, gr_sd, gr_snw, gr_n2,
           gr_nmem, gr_n3, gr_final]
    m_s = [m_norm1_w, m_gdn_conv_w, m_gdn_a_log, m_gdn_dt_bias, m_gdn_norm_w, m_ssm_conv_w, m_ssm_conv_b, m_ssm_a_log,
           m_ssm_dt_bias, m_ssm_d, m_ssm_norm_w, m_norm2_w, m_mem_norm_w, m_norm3_w, m_final_norm_w]
    v_s = [v_norm1_w, v_gdn_conv_w, v_gdn_a_log, v_gdn_dt_bias, v_gdn_norm_w, v_ssm_conv_w, v_ssm_conv_b, v_ssm_a_log,
           v_ssm_dt_bias, v_ssm_d, v_ssm_norm_w, v_norm2_w, v_mem_norm_w, v_norm3_w, v_final_norm_w]
    shp_s = [w.shape for w in w_s]
    wb, aoffs = _pack(w_s)
    gb_, _ = _pack(g_s)
    mb, _ = _pack(m_s)
    vb_, _ = _pack(v_s)
    nrows = wb.shape[0]
    d_b, m_b, v_b = rowwise(adamw_fn, "adam_small", nrows, nrows, [R(wb), R(gb_), R(mb), R(vb_)], [], [(128, F32)] * 3)
    d_l, m_l, v_l = (_unpack(b, aoffs, shp_s) for b in (d_b, m_b, v_b))

    grads = {"w_in": gr_win, "w_out": gr_wout, "wq_mem": gr_wq, "wk_mem": gr_wk, "wv_mem": gr_wv, "wo_mem": gr_wo,
             "w_up": gr_wup, "w_down": gr_wdown}
    deltas, new_m, new_v = {}, {}, {}
    for n, (dd, mm_, vv_) in big.items():
        deltas[n], new_m[n], new_v[n] = dd, mm_, vv_
    for k, n in enumerate(names_s):
        grads[n] = g_s[k].reshape(shp_s[k])
        deltas[n], new_m[n], new_v[n] = d_l[k], m_l[k], v_l[k]
    order = ["norm1_w", "w_in", "gdn_conv_w", "gdn_a_log", "gdn_dt_bias", "gdn_norm_w", "ssm_conv_w", "ssm_conv_b",
             "ssm_a_log", "ssm_dt_bias", "ssm_d", "ssm_norm_w", "w_out", "norm2_w", "mem_norm_w", "wq_mem", "wk_mem",
             "wv_mem", "wo_mem", "norm3_w", "w_up", "w_down", "final_norm_w"]
    return (loss, grad_x[None], *[grads[n] for n in order], *[deltas[n] for n in order],
            *[new_m[n] for n in order], *[new_v[n] for n in order])
```

```python
import numpy as np
import jax
import jax.numpy as jnp
from jax import lax
from jax.experimental import pallas as pl
from jax.experimental.pallas import tpu as pltpu

F32, BF16 = jnp.float32, jnp.bfloat16
MESH = pl.DeviceIdType.MESH
ANY = pl.BlockSpec(memory_space=pl.ANY)

EPS = 1e-6
D = 1024
GDN_H, GDN_DK, GDN_C = 8, 128, 64
SSM_H, SSM_P, SSM_N, SSM_L = 16, 64, 128, 128
MEM_H, MEM_DH = 4, 256
D_FF = 4096
IN_COLS = 6688
CB_QKV, CB_Z, CB_ZS, CB_XS, CB_BC, CB_BA, CB_DT = 0, 3, 4, 5, 12, 52, 53
VMEM_LIMIT = 56 * 1024 * 1024
D2D_CHUNKS = 8
ICI_CHUNKS = 4

ADAM_LR, ADAM_B1, ADAM_B2, ADAM_EPS, ADAM_WD, ADAM_STEP = 0.001, 0.9, 0.999, 1e-08, 0.01, 10


def _dg(a, b, ca, cb):
    return lax.dot_general(a, b, (((ca,), (cb,)), ((), ())), preferred_element_type=F32)


def _bf(x):
    return x.astype(BF16)


def mm(a, b):
    return _dg(_bf(a), _bf(b), 1, 0)


def mm_nt(a, b):
    return _dg(_bf(a), _bf(b), 1, 1)


def mm_tn(a, b):
    return _dg(_bf(a), _bf(b), 0, 0)


def mm_sel(a, sel):
    hi = a.astype(BF16)
    r1 = a - hi.astype(F32)
    mid = r1.astype(BF16)
    lo = (r1 - mid.astype(F32)).astype(BF16)
    s = sel.astype(BF16)
    return _dg(hi, s, 1, 0) + (_dg(mid, s, 1, 0) + _dg(lo, s, 1, 0))


def mm3(a, b):
    ah, bh = a.astype(BF16), b.astype(BF16)
    al, bl = (a - ah.astype(F32)).astype(BF16), (b - bh.astype(F32)).astype(BF16)
    return _dg(ah, bh, 1, 0) + (_dg(ah, bl, 1, 0) + _dg(al, bh, 1, 0))


def _iota(shape, dim):
    return lax.broadcasted_iota(jnp.int32, shape, dim)


def _chunk_cumsum(x, c):
    pos = _iota(x.shape, 0) & (c - 1)
    s = 1
    while s < c:
        x = x + jnp.where(pos >= s, pltpu.roll(x, s, 0), 0.0)
        s *= 2
    return x


def _chunk_revcumsum(x, c):
    n = x.shape[0]
    pos = _iota(x.shape, 0) & (c - 1)
    s = 1
    while s < c:
        x = x + jnp.where(pos < c - s, pltpu.roll(x, n - s, 0), 0.0)
        s *= 2
    return x


def _sig(x):
    return 1.0 / (1.0 + jnp.exp(-x))


def _softplus(x):
    return jnp.maximum(x, 0.0) + jnp.log(1.0 + jnp.exp(-jnp.abs(x)))


def _rows(v):
    return jnp.sum(v, axis=0, keepdims=True)


def _lanes(v):
    return jnp.sum(v, axis=1, keepdims=True)


def _sum_all(v):
    return _rows(_lanes(v))


def _cparams(sem):
    return pltpu.CompilerParams(dimension_semantics=sem, vmem_limit_bytes=VMEM_LIMIT)


def rowwise(fn, name, T, tb, row_ins, full_ins, row_outs, acc_outs=(), sp=None):
    nblk = T // tb
    assert nblk * tb == T
    has_sp = sp is not None

    def imap(f):
        return (lambda i, s: f(i, s)) if has_sp else (lambda i: f(i, None))

    in_specs, args = [], []
    for arr, w, cb, halo, off in row_ins:
        if halo == "col":
            in_specs.append(pl.BlockSpec((w, tb), imap(lambda i, s: (0, i))))
            args.append(arr)
            continue
        rowf = off if callable(off) else (lambda i, s, off=off: i + off)
        in_specs.append(pl.BlockSpec((tb, w), imap(lambda i, s, cb=cb, rowf=rowf: (rowf(i, s), cb))))
        args.append(arr)
        if halo == "prev":
            r = tb // 8
            in_specs.append(pl.BlockSpec((8, w), imap(lambda i, s, cb=cb, r=r: (jnp.maximum(i * r - 1, 0), cb))))
            args.append(arr)
        elif halo == "next":
            r, last = tb // 8, T // 8 - 1
            in_specs.append(pl.BlockSpec((8, w), imap(lambda i, s, cb=cb, r=r, last=last:
                                                      (jnp.minimum((i + 1) * r, last), cb))))
            args.append(arr)
    for arr in full_ins:
        in_specs.append(pl.BlockSpec(arr.shape, imap(lambda i, s, nd=arr.ndim: (0,) * nd)))
        args.append(arr)
    n_in, n_ro = len(args), len(row_outs)
    out_shape, out_specs = [], []
    for w, dt in row_outs:
        if w < 0:
            out_shape.append(jax.ShapeDtypeStruct((-w, T), dt))
            out_specs.append(pl.BlockSpec((-w, tb), imap(lambda i, s: (0, i))))
        else:
            out_shape.append(jax.ShapeDtypeStruct((T, w), dt))
            out_specs.append(pl.BlockSpec((tb, w), imap(lambda i, s: (i, 0))))
    for shp in acc_outs:
        out_shape.append(jax.ShapeDtypeStruct(shp, F32))
        out_specs.append(pl.BlockSpec(shp, imap(lambda i, s, nd=len(shp): (0,) * nd)))

    def body(*refs):
        i = pl.program_id(0)
        if has_sp:
            sp_ref, refs = refs[0], refs[1:]
            vals = fn(i, nblk, sp_ref, *[r[...] for r in refs[:n_in]])
        else:
            vals = fn(i, nblk, *[r[...] for r in refs[:n_in]])
        outs = refs[n_in:]
        for ref, val in zip(outs[:n_ro], vals[:n_ro]):
            ref[...] = val.astype(ref.dtype)
        for ref, val in zip(outs[n_ro:], vals[n_ro:]):
            @pl.when(i == 0)
            def _(ref=ref, val=val):
                ref[...] = val

            @pl.when(i > 0)
            def _(ref=ref, val=val):
                ref[...] += val

    cparams = _cparams(("arbitrary",) if acc_outs else ("parallel",))
    if has_sp:
        return pl.pallas_call(
            body, name=name, out_shape=out_shape, compiler_params=cparams,
            grid_spec=pltpu.PrefetchScalarGridSpec(num_scalar_prefetch=1, grid=(nblk,), in_specs=in_specs,
                                                   out_specs=out_specs),
        )(sp, *args)
    return pl.pallas_call(
        body, name=name, grid=(nblk,), in_specs=in_specs, out_specs=out_specs, out_shape=out_shape,
        compiler_params=cparams,
    )(*args)


def R(arr, w=None, cb=0, halo=None, off=0):
    return (arr, arr.shape[1] if w is None else w, cb, halo, off)


def RC(arr):
    return (arr, arr.shape[0], 0, "col", 0)


def matmul(name, a, b, form, tm, tn, tk, out_dtypes, epi=None, extras=(), rows=(), into=None):
    if form == "nn":
        (M, K), N = a.shape, b.shape[1]
    elif form == "nt":
        (M, K), N = a.shape, b.shape[0]
    else:
        (K, M), N = a.shape, b.shape[1]
    tm, tn, tk = min(tm, M), min(tn, N), min(tk, K)
    assert M % tm == 0 and N % tn == 0 and K % tk == 0, (name, M, N, K, tm, tn, tk)
    if form == "nn":
        a_spec = pl.BlockSpec((tm, tk), lambda i, j, k: (i, k))
        b_spec = pl.BlockSpec((tk, tn), lambda i, j, k: (k, j))
        ca, cb = 1, 0
    elif form == "nt":
        a_spec = pl.BlockSpec((tm, tk), lambda i, j, k: (i, k))
        b_spec = pl.BlockSpec((tn, tk), lambda i, j, k: (j, k))
        ca, cb = 1, 1
    else:
        a_spec = pl.BlockSpec((tk, tm), lambda i, j, k: (k, i))
        b_spec = pl.BlockSpec((tk, tn), lambda i, j, k: (k, j))
        ca, cb = 0, 0
    nk, ne, no = K // tk, len(extras) + len(rows), len(out_dtypes)
    if epi is None:
        epi = lambda acc: (acc,)

    def body(a_ref, b_ref, *rest):
        ex, outs, acc = rest[:ne], rest[ne:ne + no], rest[ne + no]
        k = pl.program_id(2)

        @pl.when(k == 0)
        def _():
            acc[...] = jnp.zeros_like(acc)

        acc[...] += _dg(_bf(a_ref[...]), _bf(b_ref[...]), ca, cb)

        @pl.when(k == nk - 1)
        def _():
            vals = epi(acc[...], *[e[...] for e in ex])
            for r, v in zip(outs, vals):
                r[...] = v.astype(r.dtype).reshape(r.shape)

    mn = pl.BlockSpec((tm, tn), lambda i, j, k: (i, j))
    rw = pl.BlockSpec((1, tn), lambda i, j, k: (0, j))
    if into is not None:
        buf, blk, bmap = into
        assert ne == 0 and no == 1
        aliased = not isinstance(buf, jax.ShapeDtypeStruct)

        def body_into(a_ref, b_ref, *rest):
            body(a_ref, b_ref, *rest[-2:])

        return pl.pallas_call(
            body_into, name=name, grid=(M // tm, N // tn, nk),
            in_specs=[a_spec, b_spec] + ([ANY] if aliased else []), out_specs=pl.BlockSpec(blk, bmap),
            out_shape=jax.ShapeDtypeStruct(buf.shape, buf.dtype),
            scratch_shapes=[pltpu.VMEM((tm, tn), F32)],
            input_output_aliases={2: 0} if aliased else {},
            compiler_params=_cparams(("parallel", "parallel", "arbitrary")),
        )(a, b, *([buf] if aliased else []))
    return pl.pallas_call(
        body, name=name, grid=(M // tm, N // tn, nk),
        in_specs=[a_spec, b_spec] + [mn] * len(extras) + [rw] * len(rows), out_specs=[mn] * no,
        out_shape=[jax.ShapeDtypeStruct((M, N), dt) for dt in out_dtypes],
        scratch_shapes=[pltpu.VMEM((tm, tn), F32)],
        compiler_params=_cparams(("parallel", "parallel", "arbitrary")),
    )(a, b, *extras, *rows)


def _epi_res(acc, res):
    return (res + acc,)


def _epi_res_rms(acc, res, w):
    x = res + acc
    return (x, x * lax.rsqrt(jnp.mean(x * x, axis=-1, keepdims=True) + EPS) * w)


def _epi_relu2(acc):
    u = jnp.maximum(acc, 0.0)
    return (u, u * u)


def _epi_dup(acc, u):
    return (acc * 2.0 * u.astype(F32),)


def _conv(x, halo, w, i):
    halo = jnp.where(i == 0, 0.0, halo)
    xt = jnp.concatenate([halo, x], axis=0)
    shifted = [pltpu.roll(xt, 3 - k, 0)[8:, :] for k in range(3)] + [x]
    y = shifted[3] * w[3:4, :]
    for k in range(3):
        y = y + shifted[k] * w[k:k + 1, :]
    return y, shifted


def _l2n(x, scale):
    outs = []
    for h in range(x.shape[1] // 128):
        xh = x[:, 128 * h:128 * h + 128]
        outs.append(xh * (lax.rsqrt(jnp.sum(xh * xh, axis=-1, keepdims=True) + EPS) * scale))
    return jnp.concatenate(outs, axis=1)


def _l2n_bwd(x, dy, scale):
    outs = []
    for h in range(x.shape[1] // 128):
        xh, dh = x[:, 128 * h:128 * h + 128], dy[:, 128 * h:128 * h + 128] * scale
        r = lax.rsqrt(jnp.sum(xh * xh, axis=-1, keepdims=True) + EPS)
        outs.append(r * dh - xh * (r * r * r) * jnp.sum(xh * dh, axis=-1, keepdims=True))
    return jnp.concatenate(outs, axis=1)


def rms_fwd_fn(i, n, x, w):
    r = lax.rsqrt(jnp.mean(x * x, axis=-1, keepdims=True) + EPS)
    return (x * r * w,)


def rms_bwd_fn(i, n, x, dh, dres, w):
    r = lax.rsqrt(jnp.mean(x * x, axis=-1, keepdims=True) + EPS)
    g = dh * w
    dx = dres + r * g - x * (r * r * r) * jnp.mean(x * g, axis=-1, keepdims=True)
    return dx, dx, _rows(dh * x * r)


def rms_bwd_w_fn(i, n, x, dh, w):
    r = lax.rsqrt(jnp.mean(x * x, axis=-1, keepdims=True) + EPS)
    return (_rows(dh * x * r),)


def final_fn(i, n, x, tgt, w):
    r = lax.rsqrt(jnp.mean(x * x, axis=-1, keepdims=True) + EPS)
    xn = x * r
    e = xn * w - tgt
    dy = e * (1.0 / D)
    g = dy * w
    dx = r * g - x * (r * r * r) * jnp.mean(x * g, axis=-1, keepdims=True)
    return dx, dx, _rows(e * e), _rows(dy * xn)


def _gdn_gates(ba, alog_c, dtb_c):
    col = _iota(ba.shape, 1)
    amask = (col >= 8) & (col < 16)
    beta = jnp.where(col < 8, _sig(ba), 0.0)
    z = ba + dtb_c
    ea_ = jnp.exp(alog_c)
    return beta, z, ea_, jnp.where(amask, -ea_ * _softplus(z), 0.0), amask


def gdn_prep_fn(i, n, qkv, halo, ba, cw, alog_c, dtb_c, eb, ea):
    yc, _ = _conv(qkv, halo, cw, i)
    act = yc * _sig(yc)
    qn = _l2n(act[:, :D], GDN_DK ** -0.5)
    kn = _l2n(act[:, D:2 * D], 1.0)
    beta, _, _, g, _ = _gdn_gates(ba, alog_c, dtb_c)
    gcs = _chunk_cumsum(g, GDN_C)
    return qn, kn, act[:, 2 * D:], mm_sel(gcs, ea), mm_sel(beta, eb), jnp.transpose(gcs)[8:16, :]


def gdn_prep_bwd_fn(i, n, qkv, halo, ba, dqn, dkn, dv, dgcs_x, dbeta_x, dgcs_t, cw, alog_c, dtb_c, pb, pa):
    yc, shifted = _conv(qkv, halo, cw, i)
    sg = _sig(yc)
    act = yc * sg
    dq = _l2n_bwd(act[:, :D], dqn, GDN_DK ** -0.5)
    dk = _l2n_bwd(act[:, D:2 * D], dkn, 1.0)
    dyc = jnp.concatenate([dq, dk, dv], axis=1) * (sg * (1.0 + yc * (1.0 - sg)))
    dws = [_rows(dyc * shifted[k]) for k in range(4)]
    beta, z, ea_, g, amask = _gdn_gates(ba, alog_c, dtb_c)
    tbn = ba.shape[0]
    rowpart = jnp.transpose(jnp.concatenate([jnp.zeros((8, tbn), F32), dgcs_t, jnp.zeros((112, tbn), F32)], axis=0))
    dg = _chunk_revcumsum(mm_sel(dgcs_x, pa) - rowpart, GDN_C)
    draw = jnp.where(amask, dg * (-ea_) * _sig(z), 0.0)
    dba = draw + mm_sel(dbeta_x, pb) * beta * (1.0 - beta)
    return (dyc, dba, dws[0], dws[1], dws[2], dws[3], _rows(dg * g), _rows(draw))


def conv_bwd_fn(i, n, dyc, halo, w):
    halo = jnp.where(i == n - 1, 0.0, halo)
    tb = dyc.shape[0]
    xt = jnp.concatenate([dyc, halo], axis=0)
    dx = dyc * w[3:4, :]
    for k in range(3):
        dx = dx + pltpu.roll(xt, tb + 8 - (3 - k), 0)[:tb, :] * w[k:k + 1, :]
    return (dx,)


def gdn_post_fn(i, n, o, z, w):
    outs = []
    for h in range(GDN_H):
        oh, zh = o[:, 128 * h:128 * h + 128], z[:, 128 * h:128 * h + 128]
        r = lax.rsqrt(jnp.mean(oh * oh, axis=-1, keepdims=True) + EPS)
        outs.append(oh * r * w * (zh * _sig(zh)))
    return (jnp.concatenate(outs, axis=1),)


def gdn_post_bwd_fn(i, n, o, z, doa, w):
    dos, dzs, dw = [], [], None
    for h in range(GDN_H):
        sl = slice(128 * h, 128 * h + 128)
        oh, zh, dh = o[:, sl], z[:, sl], doa[:, sl]
        r = lax.rsqrt(jnp.mean(oh * oh, axis=-1, keepdims=True) + EPS)
        s = _sig(zh)
        dn = dh * (zh * s)
        dzs.append(dh * (oh * r * w) * (s * (1.0 + zh * (1.0 - s))))
        t = _rows(dn * oh * r)
        dw = t if dw is None else dw + t
        g = dn * w
        dos.append(r * g - oh * (r * r * r) * jnp.mean(oh * g, axis=-1, keepdims=True))
    return jnp.concatenate(dos, axis=1), jnp.concatenate(dzs, axis=1), dw


def _ssd_gates(dtblk, alog_c, dtb_c):
    hmask = _iota(dtblk.shape, 1) < SSM_H
    z = dtblk + dtb_c
    return jnp.where(hmask, _softplus(z), 0.0), -jnp.exp(alog_c), z, hmask


def ssd_prep_fn(i, n, xp, hx, bcp, hbc, dtblk, cwx, cwbc, cbx, cbbc, alog_c, dtb_c, e16):
    yx, _ = _conv(xp, hx, cwx, i)
    yx = yx + cbx
    ybc, _ = _conv(bcp, hbc, cwbc, i)
    ybc = ybc + cbbc
    dt, a_neg, _, _ = _ssd_gates(dtblk, alog_c, dtb_c)
    acs = _chunk_cumsum(dt * a_neg, SSM_L)
    return (yx * _sig(yx), ybc * _sig(ybc), mm_sel(dt, e16), mm_sel(acs, e16), jnp.transpose(acs)[0:SSM_H, :])


def ssd_prep_bwd_fn(i, n, xp, hx, bcp, hbc, dtblk, dxs_a, dxs_b, dbc, dgate, dacs_t, cwx, cwbc, cbx, cbbc, alog_c, dtb_c):
    yx, shx = _conv(xp, hx, cwx, i)
    yx = yx + cbx
    ybc, shbc = _conv(bcp, hbc, cwbc, i)
    ybc = ybc + cbbc
    sx, sbc = _sig(yx), _sig(ybc)
    dyx = (dxs_a + dxs_b) * (sx * (1.0 + yx * (1.0 - sx)))
    dybc = dbc * (sbc * (1.0 + ybc * (1.0 - sbc)))
    dwx = [_rows(dyx * shx[k]) for k in range(4)]
    dwbc = [_rows(dybc * shbc[k]) for k in range(4)]
    dt, a_neg, z, hmask = _ssd_gates(dtblk, alog_c, dtb_c)
    g0, g1 = dgate[:, :128], dgate[:, 128:]
    col = _iota(g0.shape, 1)
    lo, mid = col < 8, (col >= 8) & (col < 16)
    dacs_col = jnp.where(lo, g0, 0.0) + pltpu.roll(jnp.where(lo, g1, 0.0), 8, 1)
    ddt_dir = pltpu.roll(jnp.where(mid, g0, 0.0), 120, 1) + jnp.where(mid, g1, 0.0)
    tbn = dtblk.shape[0]
    rowpart = jnp.transpose(jnp.concatenate([dacs_t, jnp.zeros((128 - SSM_H, tbn), F32)], axis=0))
    da = _chunk_revcumsum(dacs_col - rowpart, SSM_L)
    draw = jnp.where(hmask, (ddt_dir + da * a_neg) * _sig(z), 0.0)
    return (dyx, dybc, draw, *dwx, *dwbc, _rows(dyx), _rows(dybc), _rows(da * dt * a_neg), _rows(draw))


def _ssd_gate(y, xs, zs, d_x):
    y2 = y + xs * d_x
    s = _sig(zs)
    return y2, s, y2 * (zs * s)


def ssd_post_fn(i, n, y, xs, zs, d_x, nw):
    _, _, yg = _ssd_gate(y, xs, zs, d_x)
    outs = []
    for g in range(2):
        v = yg[:, 512 * g:512 * g + 512]
        outs.append(v * lax.rsqrt(jnp.mean(v * v, axis=-1, keepdims=True) + EPS))
    return (jnp.concatenate(outs, axis=1) * nw,)


def ssd_post_bwd_fn(i, n, y, xs, zs, dob, d_x, nw):
    y2, s, yg = _ssd_gate(y, xs, zs, d_x)
    gfull = dob * nw
    dygs, dnw = [], []
    for g in range(2):
        sl = slice(512 * g, 512 * g + 512)
        v, gg = yg[:, sl], gfull[:, sl]
        r = lax.rsqrt(jnp.mean(v * v, axis=-1, keepdims=True) + EPS)
        dygs.append(r * gg - v * (r * r * r) * jnp.mean(v * gg, axis=-1, keepdims=True))
        dnw.append(_rows(dob[:, sl] * v * r))
    dyg = jnp.concatenate(dygs, axis=1)
    dy2 = dyg * (zs * s)
    dzs = dyg * y2 * (s * (1.0 + zs * (1.0 - s)))
    return dy2, dy2 * d_x, dzs, jnp.concatenate(dnw, axis=1), _rows(dy2 * xs)


def attn_fn(i, n, q, k, v):
    outs = []
    for h in range(MEM_H):
        sl = slice(MEM_DH * h, MEM_DH * h + MEM_DH)
        s = mm_nt(q[:, sl], k[:, sl]) * (MEM_DH ** -0.5)
        p = jnp.exp(s - jnp.max(s, axis=-1, keepdims=True))
        p = p / jnp.sum(p, axis=-1, keepdims=True)
        outs.append(mm(p, v[:, sl]))
    return (jnp.concatenate(outs, axis=1),)


def attn_bwd_fn(i, n, q, do, k, v):
    dqs, dks, dvs = [], [], []
    for h in range(MEM_H):
        sl = slice(MEM_DH * h, MEM_DH * h + MEM_DH)
        s = mm_nt(q[:, sl], k[:, sl]) * (MEM_DH ** -0.5)
        p = jnp.exp(s - jnp.max(s, axis=-1, keepdims=True))
        p = p / jnp.sum(p, axis=-1, keepdims=True)
        dvs.append(mm_tn(p, do[:, sl]))
        dp = mm_nt(do[:, sl], v[:, sl])
        ds = p * (dp - jnp.sum(dp * p, axis=-1, keepdims=True)) * (MEM_DH ** -0.5)
        dqs.append(mm(ds, k[:, sl]))
        dks.append(mm_tn(ds, q[:, sl]))
    return jnp.concatenate(dqs, axis=1), jnp.concatenate(dks, axis=1), jnp.concatenate(dvs, axis=1)


def add2_fn(i, n, sp, a, b):
    return (a + b,)


def sum4_fn(i, n, sp, a, b, c, d):
    return (((a.astype(F32) + b.astype(F32)) + c.astype(F32)) + d.astype(F32),)


def _adamw(w, g, m, v):
    m = ADAM_B1 * m + (1.0 - ADAM_B1) * g
    v = ADAM_B2 * v + (1.0 - ADAM_B2) * (g * g)
    m_hat = m / (1.0 - ADAM_B1 ** ADAM_STEP)
    v_hat = v / (1.0 - ADAM_B2 ** ADAM_STEP)
    delta = -ADAM_LR * (m_hat / (jnp.sqrt(v_hat) + ADAM_EPS) + ADAM_WD * w)
    return delta, m, v


def _gdn_stage1(q, k, v, gcs, grow, bb):
    C = GDN_C
    row, col = _iota((C, C), 0), _iota((C, C), 1)
    incl, strict = row >= col, row > col
    dmat = jnp.where(incl, jnp.exp(jnp.minimum(gcs[:, :C] - grow, 0.0)), 0.0)
    gam = jnp.exp(gcs)
    gl = gcs[C - 1:C, :]
    kb, vb = k * bb, v * bb
    kg = kb * gam
    lmat = jnp.where(strict, mm_nt(kb, k) * dmat, 0.0)
    pmat = jnp.where(incl, mm_nt(q, k) * dmat, 0.0)
    return dict(q=q, k=k, v=v, bb=bb, incl=incl, strict=strict, dmat=dmat, gam=gam, kb=kb, vb=vb, kg=kg,
                lmat=lmat, pmat=pmat, qd=q * gam, kdec=jnp.exp(gl - gcs), cd=jnp.exp(gl))


def _gdn_inverse(lmats):
    C = GDN_C
    eye = (_iota((C, C), 0) == _iota((C, C), 1)).astype(F32)
    xs = [-l for l in lmats]
    ts = [eye + x for x in xs]
    for _ in range(5):
        xs = [mm(x, x) for x in xs]
        ts = [t + mm(t, x) for t, x in zip(ts, xs)]
    res = [eye - mm3(eye + l, t) for l, t in zip(lmats, ts)]
    return [t + mm(t, r) for t, r in zip(ts, res)]


def gdn_fwd(qn, kn, v, gcs_x, beta_x, gcs_t, tb):
    T = qn.shape[0]
    nb, ncb, nc, C = T // tb, tb // GDN_C, T // GDN_C, GDN_C

    def body(q_ref, k_ref, v_ref, g_ref, b_ref, gt_ref, o_ref, st_ref, ti_ref, s_scr):
        @pl.when(pl.program_id(1) == 0)
        def _():
            s_scr[...] = jnp.zeros_like(s_scr)

        grow = gt_ref[...]
        st1 = []
        for c in range(ncb):
            sl = slice(C * c, C * (c + 1))
            st1.append(_gdn_stage1(q_ref[sl, :], k_ref[sl, :], v_ref[sl, :], g_ref[sl, :], grow[:, sl], b_ref[sl, :]))
        tinvs = _gdn_inverse([s["lmat"] for s in st1])
        us = [mm(t, s["vb"]) for t, s in zip(tinvs, st1)]
        ws = [mm(t, s["kg"]) for t, s in zip(tinvs, st1)]
        for c in range(ncb):
            sl = slice(C * c, C * (c + 1))
            lc = st1[c]
            ti_ref[sl, :] = tinvs[c]
            s = s_scr[...]
            st_ref[c] = s
            vn = us[c] - mm(ws[c], s)
            o_ref[sl, :] = mm(lc["qd"], s) + mm(lc["pmat"], vn)
            s_scr[...] = s * lc["cd"] + mm_tn(lc["k"] * lc["kdec"], vn)

    blk = pl.BlockSpec((tb, 128), lambda h, i: (i, h))
    return pl.pallas_call(
        body, name="gdn_fwd", grid=(GDN_H, nb),
        in_specs=[blk] * 5 + [pl.BlockSpec((None, 1, tb), lambda h, i: (h, 0, i))],
        out_specs=[blk, pl.BlockSpec((None, ncb, 128, 128), lambda h, i: (h, i, 0, 0)),
                   pl.BlockSpec((None, tb, C), lambda h, i: (h, i, 0))],
        out_shape=[jax.ShapeDtypeStruct((T, D), F32), jax.ShapeDtypeStruct((GDN_H, nc, 128, 128), F32),
                   jax.ShapeDtypeStruct((GDN_H, T, C), F32)],
        scratch_shapes=[pltpu.VMEM((128, 128), F32)],
        compiler_params=_cparams(("parallel", "arbitrary")),
    )(qn, kn, v, gcs_x, beta_x, gcs_t)


def gdn_bwd(qn, kn, v, gcs_x, beta_x, gcs_t, do, states, tinv, tb):
    T = qn.shape[0]
    nb, ncb, C = T // tb, tb // GDN_C, GDN_C

    def body(q_ref, k_ref, v_ref, g_ref, b_ref, gt_ref, do_ref, st_ref, ti_ref,
             dq_ref, dk_ref, dv_ref, dgc_ref, db_ref, dgr_ref, ds_scr):
        @pl.when(pl.program_id(1) == 0)
        def _():
            ds_scr[...] = jnp.zeros_like(ds_scr)

        grow = gt_ref[...]
        lastrow = _iota((C, 1), 0) == C - 1
        pre = []
        for c in range(ncb):
            sl = slice(C * c, C * (c + 1))
            lc = _gdn_stage1(q_ref[sl, :], k_ref[sl, :], v_ref[sl, :], g_ref[sl, :], grow[:, sl], b_ref[sl, :])
            tinv_c, s, do_c = ti_ref[sl, :], st_ref[c], do_ref[sl, :]
            u, w = mm(tinv_c, lc["vb"]), mm(tinv_c, lc["kg"])
            vn = u - mm(w, s)
            lc.update(tinv=tinv_c, s=s, u=u, w=w, vn=vn, dqd=mm_nt(do_c, s),
                      dp=jnp.where(lc["incl"], mm_nt(do_c, vn), 0.0), ds_q=mm_tn(lc["qd"], do_c),
                      dvn_p=mm_tn(lc["pmat"], do_c), kd=lc["k"] * lc["kdec"])
            pre.append(lc)
        rows = [None] * ncb
        for c in reversed(range(ncb)):
            sl = slice(C * c, C * (c + 1))
            lc = pre[c]
            q, k, vv, bb, gam, kd, u, w, s, tinv_c = (lc[n] for n in ("q", "k", "v", "bb", "gam", "kd", "u", "w", "s", "tinv"))
            dsn = ds_scr[...]
            dvn = lc["dvn_p"] + mm(kd, dsn)
            dkd = mm_nt(lc["vn"], dsn)
            dcd = _sum_all(s * dsn)
            ds_scr[...] = lc["ds_q"] + lc["cd"] * dsn - mm_tn(w, dvn)
            dw = -mm_nt(dvn, s)
            dvb, dkg = mm_tn(tinv_c, dvn), mm_tn(tinv_c, dw)
            da = -jnp.where(lc["strict"], mm_nt(dvb, u) + mm_nt(dkg, w), 0.0)
            dm = da * lc["dmat"]
            dn = lc["dp"] * lc["dmat"]
            dkb = mm(dm, k)
            e = da * lc["lmat"] + lc["dp"] * lc["pmat"]
            t_kd = _lanes(dkd * kd)
            dgl = _sum_all(t_kd) + dcd * lc["cd"][:, :1]
            dgcs = (_lanes(e) + _lanes(lc["dqd"] * lc["qd"]) - t_kd + _lanes(dkg * lc["kg"])
                    + jnp.where(lastrow, dgl, 0.0))
            rows[c] = _rows(e)
            dq_ref[sl, :] = mm(dn, k) + gam * lc["dqd"]
            dk_ref[sl, :] = (mm_tn(dm, lc["kb"]) + mm_tn(dn, q) + lc["kdec"] * dkd + bb * gam * dkg + bb * dkb)
            dv_ref[sl, :] = bb * dvb
            dbeta = _lanes(dkg * gam * k) + _lanes(dvb * vv) + _lanes(dkb * k)
            db_ref[sl, :] = jnp.broadcast_to(dbeta, (C, 128))
            dgc_ref[sl, :] = jnp.broadcast_to(dgcs, (C, 128))
        dgr_ref[...] = jnp.concatenate(rows, axis=1)

    blk = pl.BlockSpec((tb, 128), lambda h, i: (nb - 1 - i, h))
    rowspec = pl.BlockSpec((None, 1, tb), lambda h, i: (h, 0, nb - 1 - i))
    return pl.pallas_call(
        body, name="gdn_bwd", grid=(GDN_H, nb),
        in_specs=[blk] * 5 + [rowspec, blk,
                              pl.BlockSpec((None, ncb, 128, 128), lambda h, i: (h, nb - 1 - i, 0, 0)),
                              pl.BlockSpec((None, tb, C), lambda h, i: (h, nb - 1 - i, 0))],
        out_specs=[blk] * 5 + [rowspec],
        out_shape=[jax.ShapeDtypeStruct((T, D), F32)] * 5 + [jax.ShapeDtypeStruct((GDN_H, 1, T), F32)],
        scratch_shapes=[pltpu.VMEM((128, 128), F32)],
        compiler_params=_cparams(("parallel", "arbitrary")),
    )(qn, kn, v, gcs_x, beta_x, gcs_t, do, states, tinv)


def _ssd_pair(x2, dt2, acs2):
    last = acs2[SSM_L - 1:SSM_L, :]
    return jnp.exp(acs2), jnp.exp(last - acs2), x2 * dt2


def _ssd_head(hh, acs2, arow, dec2, cbm, bm, incl, col):
    lmask = (col >= 64 * hh) & (col < 64 * hh + 64)
    sg = jnp.where(incl, jnp.exp(jnp.minimum(acs2[:, 64 * hh:64 * hh + 1] - arow, 0.0)), 0.0)
    dec_col = dec2[:, 64 * hh:64 * hh + 1]
    return lmask, sg, sg * cbm, dec_col, bm * dec_col


def ssd_fwd(xs, bc, dt_x, acs_x, acs_t):
    T = xs.shape[0]
    nc, L = T // SSM_L, SSM_L

    def body(x_ref, b_ref, c_ref, dt_ref, ac_ref, at_ref, y_ref, hst_ref, h_scr):
        @pl.when(pl.program_id(1) == 0)
        def _():
            h_scr[...] = jnp.zeros_like(h_scr)

        bm, cm = b_ref[...], c_ref[...]
        cbm = mm_nt(cm, bm)
        row, col = _iota((L, L), 0), _iota((L, L), 1)
        incl = row >= col
        for pr in range(4):
            sl = slice(128 * pr, 128 * pr + 128)
            acs2 = ac_ref[:, sl]
            lam2, dec2, xd2 = _ssd_pair(x_ref[:, sl], dt_ref[:, sl], acs2)
            y2, st = None, []
            for hh in range(2):
                lmask, _, mmat, _, bd = _ssd_head(hh, acs2, at_ref[2 * pr + hh], dec2, cbm, bm, incl, col)
                t = mm(mmat, jnp.where(lmask, xd2, 0.0))
                y2 = t if y2 is None else y2 + t
                st.append(mm_tn(xd2, bd))
            hprev = h_scr[pr]
            hst_ref[pr] = hprev
            y_ref[:, sl] = y2 + lam2 * mm_nt(cm, hprev)
            lam_rows = jnp.where(row < 64, lam2[L - 1:L, 0:1], lam2[L - 1:L, 64:65])
            h_scr[pr] = lam_rows * hprev + jnp.where(row < 64, st[0], st[1])

    return pl.pallas_call(
        body, name="ssd_fwd", grid=(2, nc),
        in_specs=[pl.BlockSpec((L, 512), lambda g, c: (c, g)),
                  pl.BlockSpec((L, 128), lambda g, c: (c, g)),
                  pl.BlockSpec((L, 128), lambda g, c: (c, 2 + g)),
                  pl.BlockSpec((L, 512), lambda g, c: (c, g)),
                  pl.BlockSpec((L, 512), lambda g, c: (c, g)),
                  pl.BlockSpec((8, 1, L), lambda g, c: (g, 0, c))],
        out_specs=[pl.BlockSpec((L, 512), lambda g, c: (c, g)),
                   pl.BlockSpec((None, None, 4, 128, 128), lambda g, c: (g, c, 0, 0, 0))],
        out_shape=[jax.ShapeDtypeStruct((T, D), F32), jax.ShapeDtypeStruct((2, nc, 4, 128, 128), F32)],
        scratch_shapes=[pltpu.VMEM((4, 128, 128), F32)],
        compiler_params=_cparams(("parallel", "arbitrary")),
    )(xs, bc, bc, dt_x, acs_x, acs_t)


def ssd_bwd(xs, bc, dt_x, acs_x, acs_t, dy, hstates):
    T = xs.shape[0]
    nc, L = T // SSM_L, SSM_L

    def body(x_ref, b_ref, c_ref, dt_ref, ac_ref, at_ref, dy_ref, hst_ref,
             dx_ref, db_ref, dc_ref, dgate_ref, dar_ref, dh_scr):
        @pl.when(pl.program_id(1) == 0)
        def _():
            dh_scr[...] = jnp.zeros_like(dh_scr)

        bm, cm = b_ref[...], c_ref[...]
        cbm = mm_nt(cm, bm)
        row, col = _iota((L, L), 0), _iota((L, L), 1)
        rowc = _iota((L, 1), 0)
        incl = row >= col
        dcb = jnp.zeros((L, L), F32)
        dbm = jnp.zeros((L, SSM_N), F32)
        dcm = jnp.zeros((L, SSM_N), F32)
        comp = jnp.zeros((L, 128), F32)
        for pr in range(4):
            sl = slice(128 * pr, 128 * pr + 128)
            x2, dt2, dy2, acs2 = x_ref[:, sl], dt_ref[:, sl], dy_ref[:, sl], ac_ref[:, sl]
            lam2, dec2, xd2 = _ssd_pair(x2, dt2, acs2)
            hprev, dhn = hst_ref[pr], dh_scr[pr]
            dz = lam2 * dy2
            yoff = dz * mm_nt(cm, hprev)
            dcm = dcm + mm(dz, hprev)
            q_rows = _lanes(dhn * hprev)
            dxd2 = jnp.zeros((L, 128), F32)
            for hh in range(2):
                lmask, sg, mmat, dec_col, bd = _ssd_head(hh, acs2, at_ref[2 * pr + hh], dec2, cbm, bm, incl, col)
                dm = jnp.where(incl, mm_nt(jnp.where(lmask, dy2, 0.0), xd2), 0.0)
                dcb = dcb + dm * sg
                e = dm * mmat
                dxd_h = jnp.where(lmask, mm_tn(mmat, dy2) + mm_nt(bd, dhn), 0.0)
                dxd2 = dxd2 + dxd_h
                dbd = mm(jnp.where(lmask, xd2, 0.0), dhn)
                dbm = dbm + dec_col * dbd
                t = _lanes(dbd * bd)
                lam_h = lam2[L - 1:L, 64 * hh:64 * hh + 1]
                in_head = (rowc >= 64 * hh) & (rowc < 64 * hh + 64)
                add_last = _sum_all(t) + _sum_all(jnp.where(in_head, q_rows, 0.0)) * lam_h
                dacs_col = (_lanes(jnp.where(lmask, yoff, 0.0)) + _lanes(e) - t
                            + jnp.where(rowc == L - 1, add_last, 0.0))
                ddt_col = _lanes(dxd_h * x2)
                j = 2 * pr + hh
                dar_ref[j] = _rows(e)
                comp = comp + jnp.where(col == j, dacs_col, 0.0) + jnp.where(col == 8 + j, ddt_col, 0.0)
            lam_rows = jnp.where(row < 64, lam2[L - 1:L, 0:1], lam2[L - 1:L, 64:65])
            dh_scr[pr] = mm_tn(dz, cm) + lam_rows * dhn
            dx_ref[:, sl] = dt2 * dxd2
        db_ref[...] = dbm + mm_tn(dcb, cm)
        dc_ref[...] = dcm + mm(dcb, bm)
        dgate_ref[...] = comp

    rv = lambda g, c: (nc - 1 - c, g)
    rowspec = pl.BlockSpec((8, 1, L), lambda g, c: (g, 0, nc - 1 - c))
    return pl.pallas_call(
        body, name="ssd_bwd", grid=(2, nc),
        in_specs=[pl.BlockSpec((L, 512), rv),
                  pl.BlockSpec((L, 128), rv),
                  pl.BlockSpec((L, 128), lambda g, c: (nc - 1 - c, 2 + g)),
                  pl.BlockSpec((L, 512), rv),
                  pl.BlockSpec((L, 512), rv),
                  rowspec,
                  pl.BlockSpec((L, 512), rv),
                  pl.BlockSpec((None, None, 4, 128, 128), lambda g, c: (g, nc - 1 - c, 0, 0, 0))],
        out_specs=[pl.BlockSpec((L, 512), rv), pl.BlockSpec((L, 128), rv), pl.BlockSpec((L, 128), rv),
                   pl.BlockSpec((L, 128), rv), rowspec],
        out_shape=[jax.ShapeDtypeStruct((T, D), F32), jax.ShapeDtypeStruct((T, 256), F32),
                   jax.ShapeDtypeStruct((T, 256), F32), jax.ShapeDtypeStruct((T, 256), F32),
                   jax.ShapeDtypeStruct((SSM_H, 1, T), F32)],
        scratch_shapes=[pltpu.VMEM((4, 128, 128), F32)],
        compiler_params=_cparams(("parallel", "arbitrary")),
    )(xs, bc, bc, dt_x, acs_x, acs_t, dy, hstates)


def _pos():
    return lax.axis_index("x"), lax.axis_index("y"), lax.axis_index("c")


def _other_chips(x, y):
    return [(1 - x, y), (x, 1 - y), (1 - x, 1 - y)]


def _rcopy(src, dst, ssem, rsem, dev):
    return pltpu.make_async_remote_copy(src_ref=src, dst_ref=dst, send_sem=ssem, recv_sem=rsem,
                                        device_id=dev, device_id_type=MESH)


def _rows_at(start, n):
    return pl.ds(pl.multiple_of(start, 8), n)


def _comm_call(body, name, out_shape, n_in, scratch):
    return pl.pallas_call(
        body, name=name, out_shape=out_shape, in_specs=[ANY] * n_in,
        out_specs=[ANY] * len(out_shape) if isinstance(out_shape, (list, tuple)) else ANY,
        scratch_shapes=scratch,
        compiler_params=pltpu.CompilerParams(has_side_effects=True),
    )


def _dma_sems(n):
    return pltpu.SemaphoreType.DMA((n,))


def ag_chips(name, shard):
    rr, cc = shard.shape
    h, nq = rr // 2, ICI_CHUNKS
    hq = h // nq

    def body(x_ref, out_ref, ssem, rsem):
        x, y, c = _pos()
        chips = _other_chips(x, y)
        started = []
        for q in range(nq):
            rows = _rows_at(c * h + q * hq, hq)
            for j, (cx, cy) in enumerate(chips):
                cp = _rcopy(x_ref.at[rows], out_ref.at[j, rows], ssem.at[j * nq + q], rsem.at[j * nq + q], (cx, cy, c))
                cp.start()
                started.append(cp)
        for q in range(nq):
            rows = _rows_at(c * h + q * hq, hq)
            for j, (cx, cy) in enumerate(chips):
                blk = out_ref.at[j, rows]
                _rcopy(blk, blk, ssem.at[j * nq + q], rsem.at[j * nq + q], (cx, cy, c)).wait_recv()
                k = 3 * nq + j * nq + q
                cp = _rcopy(blk, blk, ssem.at[k], rsem.at[k], (x, y, 1 - c))
                cp.start()
                started.append(cp)
        for q in range(nq):
            rows = _rows_at((1 - c) * h + q * hq, hq)
            for j in range(3):
                blk = out_ref.at[j, rows]
                k = 3 * nq + j * nq + q
                _rcopy(blk, blk, ssem.at[k], rsem.at[k], (x, y, 1 - c)).wait_recv()
        for cp in started:
            cp.wait_send()

    return _comm_call(body, name, jax.ShapeDtypeStruct((3, rr, cc), shard.dtype), 1,
                      [_dma_sems(6 * nq), _dma_sems(6 * nq)])(shard)


def all_gather_chips(name, shard, s_me):
    got = ag_chips(name, shard)
    by_rel = jnp.stack([shard, got[1], got[0], got[2]])
    return jnp.take(by_rel, jnp.arange(4) ^ s_me, axis=0)


def rs_pair(name, g):
    _, rr, cc = g.shape
    h, nq = rr // 2, D2D_CHUNKS
    hq = h // nq

    def body(g_ref, recv_ref, ssem, rsem):
        x, y, c = _pos()
        cps = []
        for q in range(nq):
            cp = _rcopy(g_ref.at[:, _rows_at((1 - c) * h + q * hq, hq), :], recv_ref.at[:, pl.ds(q * hq, hq), :],
                        ssem.at[q], rsem.at[q], (x, y, 1 - c))
            cp.start()
            cps.append(cp)
        for cp in cps:
            cp.wait()

    return _comm_call(body, name, jax.ShapeDtypeStruct((4, h, cc), g.dtype), 1, [_dma_sems(nq), _dma_sems(nq)])(g)


def rs_chips(name, p):
    _, h, cc = p.shape
    nq = ICI_CHUNKS
    hq = h // nq

    def body(p_ref, buf_ref, ssem, rsem):
        x, y, c = _pos()
        sends = []
        for q in range(nq):
            rows = pl.ds(q * hq, hq)
            for j, (cx, cy) in enumerate(_other_chips(x, y)):
                cp = _rcopy(p_ref.at[2 * cx + cy, rows], buf_ref.at[j, rows], ssem.at[j * nq + q],
                            rsem.at[j * nq + q], (cx, cy, c))
                cp.start()
                sends.append(cp)
        for cp in sends:
            cp.wait()

    return _comm_call(body, name, jax.ShapeDtypeStruct((3, h, cc), p.dtype), 1,
                      [_dma_sems(3 * nq), _dma_sems(3 * nq)])(p)


def rs_join(name, half):
    h, cc = half.shape
    nq = D2D_CHUNKS
    hq = h // nq

    def body(h_ref, out_ref, ssem, rsem):
        x, y, c = _pos()
        cps = []
        for q in range(nq):
            rows = pl.ds(q * hq, hq)
            cp = _rcopy(h_ref.at[rows], out_ref.at[rows], ssem.at[q], rsem.at[q], (x, y, 1 - c))
            cp.start()
            cps.append(cp)
        for cp in cps:
            cp.wait()

    return _comm_call(body, name, jax.ShapeDtypeStruct((h, cc), half.dtype), 1, [_dma_sems(nq), _dma_sems(nq)])(half)


def reduce_scatter(tag, g, tb, sp):
    _, rr, cc = g.shape
    h = rr // 2
    nbh = h // tb
    recv = rs_pair(tag + "_pair", g)
    mine_rows = lambda i, s: (i // nbh) * (2 * nbh) + s[0] * nbh + i % nbh
    part = rowwise(add2_fn, tag + "_add", 4 * h, tb, [R(g.reshape(4 * rr, cc), off=mine_rows), R(recv.reshape(4 * h, cc))],
                   [], [(cc, BF16)], sp=sp)[0]
    buf = rs_chips(tag + "_chips", part.reshape(4, h, cc)).reshape(3 * h, cc)
    red = rowwise(sum4_fn, tag + "_sum", h, tb,
                  [R(part, off=lambda i, s: s[1] * nbh + i)] + [R(buf, off=k * nbh) for k in range(3)],
                  [], [(cc, F32)], sp=sp)[0]
    return red, rs_join(tag + "_join", red)


def adam_halves(name, w, m, v, red, other, tb, blk0, sp):
    nbh = red.shape[0] // tb

    def fn(i, n, s, w_, m_, v_, r_, o_):
        g = jnp.where((blk0 + i) // nbh == s[0], r_, o_)
        return (g,) + _adamw(w_, g, m_, v_)

    half_rows = lambda i, s: (blk0 + i) % nbh
    return rowwise(fn, name, w.shape[0], tb, [R(w), R(m), R(v), R(red, off=half_rows), R(other, off=half_rows)],
                   [], [(w.shape[1], F32)] * 4, sp=sp)


SMALL_LANES = 3 * D


def all_reduce_items(name, items):
    flat = [a for it in items for a in it]
    shapes = [(sum(a.shape[0] for a in it), it[0].shape[1]) for it in items]
    nrows = -(-sum(s[0] for s in shapes) // 8) * 8

    def body(*refs):
        ins, outs = refs[:len(flat)], refs[len(flat):len(flat) + len(items)]
        mine, buf, ssem, rsem = refs[len(flat) + len(items):]
        x, y, c = _pos()
        me = 4 * x + 2 * y + c
        mine[...] = jnp.zeros_like(mine)
        r = 0
        for ref in ins:
            mine[r:r + ref.shape[0], 0:ref.shape[1]] = ref[...]
            r += ref.shape[0]
        buf[me] = mine[...]
        cps = []
        for k in range(1, 8):
            dev = (x ^ (k >> 2), y ^ ((k >> 1) & 1), c ^ (k & 1))
            cp = _rcopy(mine, buf.at[me], ssem.at[k - 1], rsem.at[k - 1], dev)
            cp.start()
            cps.append(cp)
        for cp in cps:
            cp.wait()
        r = 0
        for (nr, n), out in zip(shapes, outs):
            acc = buf[0, r:r + nr, 0:n]
            for d in range(1, 8):
                acc = acc + buf[d, r:r + nr, 0:n]
            out[...] = acc
            r += nr

    vm = pl.BlockSpec(memory_space=pltpu.VMEM)
    return pl.pallas_call(
        body, name=name, out_shape=[jax.ShapeDtypeStruct(s, F32) for s in shapes],
        in_specs=[vm] * len(flat), out_specs=[vm] * len(items),
        scratch_shapes=[pltpu.VMEM((nrows, SMALL_LANES), F32), pltpu.VMEM((8, nrows, SMALL_LANES), F32),
                        _dma_sems(7), _dma_sems(7)],
        compiler_params=pltpu.CompilerParams(has_side_effects=True),
    )(*flat)


def adam_small(ws, gs, ms, vs):
    n = len(ws)

    def body(*refs):
        for k in range(n):
            w, g, m, v = (refs[j * n + k][...] for j in range(4))
            for j, val in enumerate(_adamw(w, g, m, v)):
                refs[(4 + j) * n + k][...] = val

    vm = pl.BlockSpec(memory_space=pltpu.VMEM)
    res = pl.pallas_call(
        body, name="adam_small", out_shape=[jax.ShapeDtypeStruct(w.shape, F32) for w in ws] * 3,
        in_specs=[vm] * (4 * n), out_specs=[vm] * (3 * n),
    )(*ws, *gs, *ms, *vs)
    return res[:n], res[n:2 * n], res[2 * n:]


def _sel(rows, cols, pairs):
    m = np.zeros((rows, cols), np.float32)
    for r, c in pairs:
        m[r, c] = 1.0
    return jnp.asarray(m)


def _pad_win(w):
    z = jnp.zeros((w.shape[0], 112), w.dtype)
    return jnp.concatenate([w[:, :4096], w[:, 4112:6672], w[:, 4096:4112], z, w[:, 6672:6688], z], axis=1)


def _unpad_win(wp):
    return jnp.concatenate([wp[:, :4096], wp[:, 6656:6672], wp[:, 4096:6656], wp[:, 6784:6800]], axis=1)


def kernel(x, mem, norm1_w, w_in, gdn_conv_w, gdn_a_log, gdn_dt_bias, gdn_norm_w, ssm_conv_w, ssm_conv_b, ssm_a_log, ssm_dt_bias, ssm_d, ssm_norm_w, w_out, norm2_w, mem_norm_w, wq_mem, wk_mem, wv_mem, wo_mem, norm3_w, w_up, w_down, final_norm_w, loss_target, m_norm1_w, m_w_in, m_gdn_conv_w, m_gdn_a_log, m_gdn_dt_bias, m_gdn_norm_w, m_ssm_conv_w, m_ssm_conv_b, m_ssm_a_log, m_ssm_dt_bias, m_ssm_d, m_ssm_norm_w, m_w_out, m_norm2_w, m_mem_norm_w, m_wq_mem, m_wk_mem, m_wv_mem, m_wo_mem, m_norm3_w, m_w_up, m_w_down, m_final_norm_w, v_norm1_w, v_w_in, v_gdn_conv_w, v_gdn_a_log, v_gdn_dt_bias, v_gdn_norm_w, v_ssm_conv_w, v_ssm_conv_b, v_ssm_a_log, v_ssm_dt_bias, v_ssm_d, v_ssm_norm_w, v_w_out, v_norm2_w, v_mem_norm_w, v_wq_mem, v_wk_mem, v_wv_mem, v_wo_mem, v_norm3_w, v_w_up, v_w_down, v_final_norm_w):
    T, M = x.shape[1], mem.shape[1]
    xi, yi, ci = _pos()
    s_me = 2 * xi + yi
    x0, mem0, tgt = x[0], mem[0], loss_target[0]
    tb = min(256, T)
    row = lambda v: v.reshape(1, -1)

    win_g = all_gather_chips("ag_win", w_in.astype(BF16), s_me)
    rest_g = all_gather_chips("ag_rest", jnp.concatenate([w_up, w_down, w_out, wq_mem, wk_mem, wv_mem, wo_mem],
                                                         axis=0).astype(BF16), s_me)
    w_in_p = _pad_win(win_g.transpose(1, 0, 2).reshape(D, IN_COLS))
    wup_f = rest_g[:, 0:1024].transpose(1, 0, 2).reshape(D, D_FF)
    wdown_f = rest_g[:, 1024:2048].reshape(D_FF, D)
    wout_f = rest_g[:, 2048:2560].reshape(2 * D, D)
    wq_f, wk_f, wv_f, wo_f = (rest_g[:, 2560 + 256 * k:2816 + 256 * k].reshape(D, D) for k in range(4))
    keep = (ci == 0).astype(F32)
    gcw_z = lax.dynamic_update_slice(jnp.zeros((4, 3 * D), F32), gdn_conv_w * keep, (0, s_me * 768))
    scw_z = lax.dynamic_update_slice(jnp.zeros((4, 1536), F32), ssm_conv_w * keep, (0, s_me * 384))
    gcw, scw = all_reduce_items("ar_convw", [[gcw_z], [scw_z]])
    scw_x, scw_bc = scw[:, :D], scw[:, D:]
    sp = jnp.stack([ci, s_me]).astype(jnp.int32)
    scb_x, scb_bc = row(ssm_conv_b[:D]), row(ssm_conv_b[D:])

    galog_c, gdtb_c = row(jnp.pad(gdn_a_log, (8, 112))), row(jnp.pad(gdn_dt_bias, (8, 112)))
    salog_c, sdtb_c = row(jnp.pad(ssm_a_log, (0, 112))), row(jnp.pad(ssm_dt_bias, (0, 112)))
    sd_x = row(jnp.repeat(ssm_d, 64))
    eb = _sel(128, D, [(h, 128 * h + l) for h in range(8) for l in range(128)])
    ea = _sel(128, D, [(8 + h, 128 * h + l) for h in range(8) for l in range(128)])
    e16 = _sel(128, D, [(h, 64 * h + l) for h in range(16) for l in range(64)])
    pb = _sel(D, 128, [(128 * h, h) for h in range(8)])
    pa = _sel(D, 128, [(128 * h, 8 + h) for h in range(8)])

    h1 = rowwise(rms_fwd_fn, "rms1", T, tb, [R(x0)], [row(norm1_w)], [(D, BF16)])[0]
    p = matmul("mm_in", h1, w_in_p, "nn", 1024, 768, 1024, [F32])[0]
    gp_ins = [R(p, 3 * D, CB_QKV, "prev"), R(p, 128, CB_BA)]
    qn, kn, vv, gcs_x, beta_x, gcs_t = rowwise(gdn_prep_fn, "gdn_prep", T, tb, gp_ins,
                                               [gcw, galog_c, gdtb_c, eb, ea], [(D, F32)] * 5 + [(-8, F32)])
    gcs_t = gcs_t.reshape(GDN_H, 1, T)
    gtb = min(512, T)
    o_gdn, s_states, tinv = gdn_fwd(qn, kn, vv, gcs_x, beta_x, gcs_t, gtb)
    gnw = row(gdn_norm_w)
    oa = rowwise(gdn_post_fn, "gdn_post", T, tb, [R(o_gdn), R(p, D, CB_Z)], [gnw], [(D, BF16)])[0]
    sp_ins = [R(p, D, CB_XS, "prev"), R(p, 512, CB_BC, "prev"), R(p, 128, CB_DT)]
    sp_full = [scw_x, scw_bc, scb_x, scb_bc, salog_c, sdtb_c]
    xs, bc, dt_x, acs_x, acs_t = rowwise(ssd_prep_fn, "ssd_prep", T, tb, sp_ins, sp_full + [e16],
                                         [(D, F32), (512, F32), (D, F32), (D, F32), (-SSM_H, F32)])
    acs_t = acs_t.reshape(SSM_H, 1, T)
    y_ssd, h_states = ssd_fwd(xs, bc, dt_x, acs_x, acs_t)
    snw = row(ssm_norm_w)
    ob = rowwise(ssd_post_fn, "ssd_post", T, tb, [R(y_ssd), R(xs), R(p, D, CB_ZS)], [sd_x, snw], [(D, BF16)])[0]
    x1a = matmul("mm_out_a", oa, wout_f[:D], "nn", 1024, 1024, 1024, [F32], _epi_res, [x0])[0]
    assert D == 1024
    x1, h2 = matmul("mm_out_b", ob, wout_f[D:], "nn", 1024, 1024, 1024, [F32, BF16], _epi_res_rms, [x1a],
                    [row(norm2_w)])

    mn = rowwise(rms_fwd_fn, "rms_mem", M, M, [R(mem0)], [row(mem_norm_w)], [(D, BF16)])[0]
    km = matmul("mm_k", mn, wk_f, "nn", 256, 1024, 1024, [BF16])[0]
    vm = matmul("mm_v", mn, wv_f, "nn", 256, 1024, 1024, [BF16])[0]
    qm = matmul("mm_q", h2, wq_f, "nn", 1024, 1024, 1024, [BF16])[0]
    ao = rowwise(attn_fn, "attn", T, tb, [R(qm)], [km, vm], [(D, BF16)])[0]
    x2, h3 = matmul("mm_o", ao, wo_f, "nn", 1024, 1024, 1024, [F32, BF16], _epi_res_rms, [x1], [row(norm3_w)])
    u, act = matmul("mm_up", h3, wup_f, "nn", 1024, 1024, 1024, [BF16, BF16], _epi_relu2)
    x3 = matmul("mm_down", act, wdown_f, "nn", 1024, 1024, 1024, [F32], _epi_res, [x2])[0]

    dx3, dx3b, loss_lane, g_final = rowwise(final_fn, "final", T, tb, [R(x3), R(tgt)], [row(final_norm_w)],
                                            [(D, F32), (D, BF16)], [(1, D), (1, D)])
    loss = lax.psum(0.5 / D * jnp.sum(loss_lane), ("x", "y", "c"))

    dup = matmul("mm_dact", dx3b, wdown_f, "nt", 1024, 1024, 1024, [BF16], _epi_dup, [u])[0]
    def g_into(buf, blk, at):
        return dict(into=(buf, blk, lambda i, j, k, at=at: at(i, j)))

    grest = jax.ShapeDtypeStruct((4, 3584, D), F32)
    grest = matmul("mm_gdown", act, dx3b, "tn", 1024, 1024, 1024, [F32],
                   **g_into(grest, (None, 1024, D), lambda i, j: (i, 1, 0)))
    dh3 = matmul("mm_dh3", dup, wup_f, "nt", 1024, 1024, 1024, [F32])[0]
    grest = matmul("mm_gup", h3, dup, "tn", 1024, 1024, 1024, [F32],
                   **g_into(grest, (None, 1024, D), lambda i, j: (j, 0, 0)))
    dx2, dx2b, g_n3 = rowwise(rms_bwd_fn, "rms3_bwd", T, tb, [R(x2), R(dh3), R(dx3)], [row(norm3_w)],
                              [(D, F32), (D, BF16)], [(1, D)])
    dao = matmul("mm_dao", dx2b, wo_f, "nt", 1024, 1024, 1024, [F32])[0]
    grest = matmul("mm_gwo", ao, dx2b, "tn", 1024, 1024, 1024, [F32],
                   **g_into(grest, (4, 256, D), lambda i, j: (0, 13, 0)))
    dqm, dkm, dvm = rowwise(attn_bwd_fn, "attn_bwd", T, tb, [R(qm), R(dao)], [km, vm], [(D, BF16)],
                            [(M, D), (M, D)])
    dh2 = matmul("mm_dh2", dqm, wq_f, "nt", 1024, 1024, 1024, [F32])[0]
    grest = matmul("mm_gwq", h2, dqm, "tn", 1024, 1024, 1024, [F32],
                   **g_into(grest, (4, 256, D), lambda i, j: (0, 10, 0)))
    grest = matmul("mm_gwk", mn, dkm, "tn", 1024, 1024, 256, [F32],
                   **g_into(grest, (4, 256, D), lambda i, j: (0, 11, 0)))
    grest = matmul("mm_gwv", mn, dvm, "tn", 1024, 1024, 256, [F32],
                   **g_into(grest, (4, 256, D), lambda i, j: (0, 12, 0)))
    dmn_k = matmul("mm_dmk", dkm, wk_f, "nt", 256, 1024, 1024, [F32])[0]
    dmn = matmul("mm_dmv", dvm, wv_f, "nt", 256, 1024, 1024, [F32], _epi_res, [dmn_k])[0]
    g_nmem = rowwise(rms_bwd_w_fn, "rmsmem_bwd", M, M, [R(mem0), R(dmn)], [row(mem_norm_w)], [], [(1, D)])[0]
    dx1, dx1b, g_n2 = rowwise(rms_bwd_fn, "rms2_bwd", T, tb, [R(x1), R(dh2), R(dx2)], [row(norm2_w)],
                              [(D, F32), (D, BF16)], [(1, D)])
    doa = matmul("mm_doa", dx1b, wout_f[:D], "nt", 1024, 1024, 1024, [F32])[0]
    dob = matmul("mm_dob", dx1b, wout_f[D:], "nt", 1024, 1024, 1024, [F32])[0]
    grest = matmul("mm_gwout_a", oa, dx1b, "tn", 1024, 1024, 1024, [F32],
                   **g_into(grest, (2, 512, D), lambda i, j: (0, 4, 0)))
    grest = matmul("mm_gwout_b", ob, dx1b, "tn", 1024, 1024, 1024, [F32],
                   **g_into(grest, (2, 512, D), lambda i, j: (1, 4, 0)))

    dy_ssd, dxs_dir, dzs, g_snw, g_sd_lane = rowwise(
        ssd_post_bwd_fn, "ssd_post_bwd", T, tb, [R(y_ssd), R(xs), R(p, D, CB_ZS), R(dob)], [sd_x, snw],
        [(D, F32), (D, F32), (D, BF16)], [(1, D), (1, D)])
    dxs_scan, db_s, dc_s, dgate, dacs_t = ssd_bwd(xs, bc, dt_x, acs_x, acs_t, dy_ssd, h_states)
    dbc = jnp.concatenate([db_s, dc_s], axis=1)
    spb = rowwise(ssd_prep_bwd_fn, "ssd_prep_bwd", T, tb,
                  sp_ins + [R(dxs_scan), R(dxs_dir), R(dbc), R(dgate), RC(dacs_t.reshape(SSM_H, T))], sp_full,
                  [(D, F32), (512, F32), (128, BF16)],
                  [(1, D)] * 4 + [(1, 512)] * 4 + [(1, D), (1, 512), (1, 128), (1, 128)])
    dyc_x, dyc_bc, ddt_blk = spb[:3]
    dp_xs = rowwise(conv_bwd_fn, "conv_bwd_x", T, tb, [R(dyc_x, halo="next")], [scw_x], [(D, BF16)])[0]
    dp_bc = rowwise(conv_bwd_fn, "conv_bwd_bc", T, tb, [R(dyc_bc, halo="next")], [scw_bc], [(512, BF16)])[0]

    do_gdn, dz, g_gnw = rowwise(gdn_post_bwd_fn, "gdn_post_bwd", T, tb, [R(o_gdn), R(p, D, CB_Z), R(doa)], [gnw],
                                [(D, F32), (D, BF16)], [(1, 128)])
    dqn, dkn, dvv, dgcs_x, dbeta_x, dgcs_t = gdn_bwd(qn, kn, vv, gcs_x, beta_x, gcs_t, do_gdn, s_states, tinv, gtb)
    gpb = rowwise(gdn_prep_bwd_fn, "gdn_prep_bwd", T, tb,
                  gp_ins + [R(dqn), R(dkn), R(dvv), R(dgcs_x), R(dbeta_x), RC(dgcs_t.reshape(GDN_H, T))],
                  [gcw, galog_c, gdtb_c, pb, pa],
                  [(3 * D, F32), (128, BF16)], [(1, 3 * D)] * 4 + [(1, 128), (1, 128)])
    dyc_qkv, dba = gpb[:2]
    dp_qkv = rowwise(conv_bwd_fn, "conv_bwd_qkv", T, tb, [R(dyc_qkv, halo="next")], [gcw], [(3 * D, BF16)])[0]

    dp = jnp.concatenate([dp_qkv, dz, dzs, dp_xs, dp_bc, dba, ddt_blk], axis=1)
    dh1 = matmul("mm_dh1", dp, w_in_p, "nt", 1024, 1024, 768, [F32])[0]
    g_win_p = matmul("mm_gwin", h1, dp, "tn", 1024, 768, 1024, [F32])[0]
    grad_x, _, g_n1 = rowwise(rms_bwd_fn, "rms1_bwd", T, tb, [R(x0), R(dh1), R(dx1)], [row(norm1_w)],
                              [(D, F32), (D, BF16)], [(1, D)])

    items = [[g_n1], [gpb[6]], [gpb[7]], [g_gnw], [spb[11]], [spb[12]], [spb[13]], [spb[14]], [g_sd_lane], [g_snw],
             [g_n2], [g_nmem], [g_n3], [g_final], list(gpb[2:6]), list(spb[3:7]), list(spb[7:11])]
    (gr_n1, r_galog, r_gdtb, gr_gnw, r_scb_x, r_scb_bc, r_salog, r_sdtb, r_sd, gr_snw, gr_n2, gr_nmem, gr_n3,
     gr_final, r_gcw, r_scw_x, r_scw_bc) = all_reduce_items("ar_grads", items)
    gr_galog, gr_gdtb = r_galog[:, 8:16], r_gdtb[:, 8:16]
    gr_salog, gr_sdtb = r_salog[:, :SSM_H], r_sdtb[:, :SSM_H]
    gr_sd = r_sd.reshape(SSM_H, SSM_P).sum(axis=1).reshape(1, SSM_H)
    gr_scb = jnp.concatenate([r_scb_x, r_scb_bc], axis=1)
    gr_gcw = lax.dynamic_slice(r_gcw, (0, s_me * 768), (4, 768))
    gr_scw = lax.dynamic_slice(jnp.concatenate([r_scw_x, r_scw_bc], axis=1), (0, s_me * 384), (4, 384))

    g_win = _unpad_win(g_win_p).reshape(D, 4, IN_COLS // 4).transpose(1, 0, 2)
    red_w, oth_w = reduce_scatter("rs_win", g_win, 256, sp)
    red_r, oth_r = reduce_scatter("rs_rest", grest, 256, sp)

    big = {"w_in": adam_halves("adam_win", w_in, m_w_in, v_w_in, red_w, oth_w, 256, 0, sp)}
    for n, w, m, v, blk0 in (("w_up", w_up, m_w_up, v_w_up, 0), ("w_down", w_down, m_w_down, v_w_down, 4),
                             ("w_out", w_out, m_w_out, v_w_out, 8), ("wq_mem", wq_mem, m_wq_mem, v_wq_mem, 10),
                             ("wk_mem", wk_mem, m_wk_mem, v_wk_mem, 11), ("wv_mem", wv_mem, m_wv_mem, v_wv_mem, 12),
                             ("wo_mem", wo_mem, m_wo_mem, v_wo_mem, 13)):
        big[n] = adam_halves("adam_" + n, w, m, v, red_r, oth_r, 256, blk0, sp)
    names_s = ["norm1_w", "gdn_conv_w", "gdn_a_log", "gdn_dt_bias", "gdn_norm_w", "ssm_conv_w", "ssm_conv_b",
               "ssm_a_log", "ssm_dt_bias", "ssm_d", "ssm_norm_w", "norm2_w", "mem_norm_w", "norm3_w", "final_norm_w"]
    w_s = [norm1_w, gdn_conv_w, gdn_a_log, gdn_dt_bias, gdn_norm_w, ssm_conv_w, ssm_conv_b, ssm_a_log, ssm_dt_bias,
           ssm_d, ssm_norm_w, norm2_w, mem_norm_w, norm3_w, final_norm_w]
    g_s = [gr_n1, gr_gcw, gr_galog, gr_gdtb, gr_gnw, gr_scw, gr_scb, gr_salog, gr_sdtb, gr_sd, gr_snw, gr_n2,
           gr_nmem, gr_n3, gr_final]
    m_s = [m_norm1_w, m_gdn_conv_w, m_gdn_a_log, m_gdn_dt_bias, m_gdn_norm_w, m_ssm_conv_w, m_ssm_conv_b, m_ssm_a_log,
           m_ssm_dt_bias, m_ssm_d, m_ssm_norm_w, m_norm2_w, m_mem_norm_w, m_norm3_w, m_final_norm_w]
    v_s = [v_norm1_w, v_gdn_conv_w, v_gdn_a_log, v_gdn_dt_bias, v_gdn_norm_w, v_ssm_conv_w, v_ssm_conv_b, v_ssm_a_log,
           v_ssm_dt_bias, v_ssm_d, v_ssm_norm_w, v_norm2_w, v_mem_norm_w, v_norm3_w, v_final_norm_w]
    shp_s = [w.shape for w in w_s]
    as2d = lambda a: a if a.ndim == 2 else a.reshape(1, -1)
    d_l, m_l, v_l = adam_small([as2d(a) for a in w_s], [as2d(a) for a in g_s], [as2d(a) for a in m_s],
                               [as2d(a) for a in v_s])

    grads, deltas, new_m, new_v = {}, {}, {}, {}
    for n, (gg, dd, mm_, vv_) in big.items():
        grads[n], deltas[n], new_m[n], new_v[n] = gg, dd, mm_, vv_
    for k, n in enumerate(names_s):
        grads[n] = g_s[k].reshape(shp_s[k])
        deltas[n], new_m[n], new_v[n] = (a[k].reshape(shp_s[k]) for a in (d_l, m_l, v_l))
    order = ["norm1_w", "w_in", "gdn_conv_w", "gdn_a_log", "gdn_dt_bias", "gdn_norm_w", "ssm_conv_w", "ssm_conv_b",
             "ssm_a_log", "ssm_dt_bias", "ssm_d", "ssm_norm_w", "w_out", "norm2_w", "mem_norm_w", "wq_mem", "wk_mem",
             "wv_mem", "wo_mem", "norm3_w", "w_up", "w_down", "final_norm_w"]
    return (loss, grad_x[None], *[grads[n] for n in order], *[deltas[n] for n in order],
            *[new_m[n] for n in order], *[new_v[n] for n in order])
```

```python
import numpy as np
import jax
import jax.numpy as jnp
from jax import lax
from jax.experimental import pallas as pl
from jax.experimental.pallas import tpu as pltpu

F32, BF16 = jnp.float32, jnp.bfloat16
MESH = pl.DeviceIdType.MESH
ANY = pl.BlockSpec(memory_space=pl.ANY)

EPS = 1e-6
D = 1024
GDN_H, GDN_DK, GDN_C = 8, 128, 64
SSM_H, SSM_P, SSM_N, SSM_L = 16, 64, 128, 128
MEM_H, MEM_DH = 4, 256
D_FF = 4096
IN_COLS = 6688
CB_QKV, CB_Z, CB_ZS, CB_XS, CB_BC, CB_BA, CB_DT = 0, 3, 4, 5, 12, 52, 53
VMEM_LIMIT = 56 * 1024 * 1024
D2D_CHUNKS = 8
ICI_CHUNKS = 4

ADAM_LR, ADAM_B1, ADAM_B2, ADAM_EPS, ADAM_WD, ADAM_STEP = 0.001, 0.9, 0.999, 1e-08, 0.01, 10


def _dg(a, b, ca, cb):
    return lax.dot_general(a, b, (((ca,), (cb,)), ((), ())), preferred_element_type=F32)


def _bf(x):
    return x.astype(BF16)


def mm(a, b):
    return _dg(_bf(a), _bf(b), 1, 0)


def mm_nt(a, b):
    return _dg(_bf(a), _bf(b), 1, 1)


def mm_tn(a, b):
    return _dg(_bf(a), _bf(b), 0, 0)


def mm_sel(a, sel):
    hi = a.astype(BF16)
    r1 = a - hi.astype(F32)
    mid = r1.astype(BF16)
    lo = (r1 - mid.astype(F32)).astype(BF16)
    s = sel.astype(BF16)
    return _dg(hi, s, 1, 0) + (_dg(mid, s, 1, 0) + _dg(lo, s, 1, 0))


def mm3(a, b):
    ah, bh = a.astype(BF16), b.astype(BF16)
    al, bl = (a - ah.astype(F32)).astype(BF16), (b - bh.astype(F32)).astype(BF16)
    return _dg(ah, bh, 1, 0) + (_dg(ah, bl, 1, 0) + _dg(al, bh, 1, 0))


def _iota(shape, dim):
    return lax.broadcasted_iota(jnp.int32, shape, dim)


def _chunk_cumsum(x, c):
    pos = _iota(x.shape, 0) & (c - 1)
    s = 1
    while s < c:
        x = x + jnp.where(pos >= s, pltpu.roll(x, s, 0), 0.0)
        s *= 2
    return x


def _chunk_revcumsum(x, c):
    n = x.shape[0]
    pos = _iota(x.shape, 0) & (c - 1)
    s = 1
    while s < c:
        x = x + jnp.where(pos < c - s, pltpu.roll(x, n - s, 0), 0.0)
        s *= 2
    return x


def _sig(x):
    return 1.0 / (1.0 + jnp.exp(-x))


def _softplus(x):
    return jnp.maximum(x, 0.0) + jnp.log(1.0 + jnp.exp(-jnp.abs(x)))


def _rows(v):
    return jnp.sum(v, axis=0, keepdims=True)


def _lanes(v):
    return jnp.sum(v, axis=1, keepdims=True)


def _sum_all(v):
    return _rows(_lanes(v))


def _cparams(sem):
    return pltpu.CompilerParams(dimension_semantics=sem, vmem_limit_bytes=VMEM_LIMIT)


def rowwise(fn, name, T, tb, row_ins, full_ins, row_outs, acc_outs=(), sp=None):
    nblk = T // tb
    assert nblk * tb == T
    has_sp = sp is not None

    def imap(f):
        return (lambda i, s: f(i, s)) if has_sp else (lambda i: f(i, None))

    in_specs, args = [], []
    for arr, w, cb, halo, off in row_ins:
        if halo == "col":
            in_specs.append(pl.BlockSpec((w, tb), imap(lambda i, s: (0, i))))
            args.append(arr)
            continue
        rowf = off if callable(off) else (lambda i, s, off=off: i + off)
        in_specs.append(pl.BlockSpec((tb, w), imap(lambda i, s, cb=cb, rowf=rowf: (rowf(i, s), cb))))
        args.append(arr)
        if halo == "prev":
            r = tb // 8
            in_specs.append(pl.BlockSpec((8, w), imap(lambda i, s, cb=cb, r=r: (jnp.maximum(i * r - 1, 0), cb))))
            args.append(arr)
        elif halo == "next":
            r, last = tb // 8, T // 8 - 1
            in_specs.append(pl.BlockSpec((8, w), imap(lambda i, s, cb=cb, r=r, last=last:
                                                      (jnp.minimum((i + 1) * r, last), cb))))
            args.append(arr)
    for arr in full_ins:
        in_specs.append(pl.BlockSpec(arr.shape, imap(lambda i, s, nd=arr.ndim: (0,) * nd)))
        args.append(arr)
    n_in, n_ro = len(args), len(row_outs)
    out_shape, out_specs, aliases = [], [], {}
    for k, (w, dt, *dest) in enumerate(row_outs):
        if dest:
            buf, cb = dest
            out_shape.append(jax.ShapeDtypeStruct(buf.shape, buf.dtype))
            out_specs.append(pl.BlockSpec((tb, w), imap(lambda i, s, cb=cb: (i, cb))))
            if not isinstance(buf, jax.ShapeDtypeStruct):
                aliases[len(args) + int(has_sp)] = k
                in_specs.append(ANY)
                args.append(buf)
        elif w < 0:
            out_shape.append(jax.ShapeDtypeStruct((-w, T), dt))
            out_specs.append(pl.BlockSpec((-w, tb), imap(lambda i, s: (0, i))))
        else:
            out_shape.append(jax.ShapeDtypeStruct((T, w), dt))
            out_specs.append(pl.BlockSpec((tb, w), imap(lambda i, s: (i, 0))))
    for shp in acc_outs:
        out_shape.append(jax.ShapeDtypeStruct(shp, F32))
        out_specs.append(pl.BlockSpec(shp, imap(lambda i, s, nd=len(shp): (0,) * nd)))

    def body(*refs):
        i = pl.program_id(0)
        if has_sp:
            sp_ref, refs = refs[0], refs[1:]
            vals = fn(i, nblk, sp_ref, *[r[...] for r in refs[:n_in]])
        else:
            vals = fn(i, nblk, *[r[...] for r in refs[:n_in]])
        outs = refs[n_in + len(aliases):]
        for ref, val in zip(outs[:n_ro], vals[:n_ro]):
            ref[...] = val.astype(ref.dtype)
        for ref, val in zip(outs[n_ro:], vals[n_ro:]):
            @pl.when(i == 0)
            def _(ref=ref, val=val):
                ref[...] = val

            @pl.when(i > 0)
            def _(ref=ref, val=val):
                ref[...] += val

    cparams = _cparams(("arbitrary",) if acc_outs else ("parallel",))
    if has_sp:
        return pl.pallas_call(
            body, name=name, out_shape=out_shape, compiler_params=cparams, input_output_aliases=aliases,
            grid_spec=pltpu.PrefetchScalarGridSpec(num_scalar_prefetch=1, grid=(nblk,), in_specs=in_specs,
                                                   out_specs=out_specs),
        )(sp, *args)
    return pl.pallas_call(
        body, name=name, grid=(nblk,), in_specs=in_specs, out_specs=out_specs, out_shape=out_shape,
        compiler_params=cparams, input_output_aliases=aliases,
    )(*args)


def R(arr, w=None, cb=0, halo=None, off=0):
    return (arr, arr.shape[1] if w is None else w, cb, halo, off)


def RC(arr):
    return (arr, arr.shape[0], 0, "col", 0)


def matmul(name, a, b, form, tm, tn, tk, out_dtypes, epi=None, extras=(), rows=(), into=None, n_acc=0):
    if form == "nn":
        (M, K), N = a.shape, b.shape[1]
    elif form == "nt":
        (M, K), N = a.shape, b.shape[0]
    else:
        (K, M), N = a.shape, b.shape[1]
    tm, tn, tk = min(tm, M), min(tn, N), min(tk, K)
    assert M % tm == 0 and N % tn == 0 and K % tk == 0, (name, M, N, K, tm, tn, tk)
    if form == "nn":
        a_spec = pl.BlockSpec((tm, tk), lambda i, j, k: (i, k))
        b_spec = pl.BlockSpec((tk, tn), lambda i, j, k: (k, j))
        ca, cb = 1, 0
    elif form == "nt":
        a_spec = pl.BlockSpec((tm, tk), lambda i, j, k: (i, k))
        b_spec = pl.BlockSpec((tn, tk), lambda i, j, k: (j, k))
        ca, cb = 1, 1
    else:
        a_spec = pl.BlockSpec((tk, tm), lambda i, j, k: (k, i))
        b_spec = pl.BlockSpec((tk, tn), lambda i, j, k: (k, j))
        ca, cb = 0, 0
    nk, ne, no = K // tk, len(extras) + len(rows), len(out_dtypes)
    if epi is None:
        epi = lambda acc: (acc,)

    assert n_acc == 0 or tn == N

    def body(a_ref, b_ref, *rest):
        ex, outs, accs, acc = rest[:ne], rest[ne:ne + no], rest[ne + no:ne + no + n_acc], rest[ne + no + n_acc]
        i, k = pl.program_id(0), pl.program_id(2)

        @pl.when(k == 0)
        def _():
            acc[...] = jnp.zeros_like(acc)

        acc[...] += _dg(_bf(a_ref[...]), _bf(b_ref[...]), ca, cb)

        @pl.when(k == nk - 1)
        def _():
            vals = epi(acc[...], *[e[...] for e in ex])
            for r, v in zip(outs, vals[:no]):
                r[...] = v.astype(r.dtype).reshape(r.shape)
            for r, v in zip(accs, vals[no:]):
                @pl.when(i == 0)
                def _(r=r, v=v):
                    r[...] = v

                @pl.when(i > 0)
                def _(r=r, v=v):
                    r[...] += v

    mn = pl.BlockSpec((tm, tn), lambda i, j, k: (i, j))
    rw = pl.BlockSpec((1, tn), lambda i, j, k: (0, j))
    if into is not None:
        buf, blk, bmap = into
        assert ne == 0 and no == 1
        aliased = not isinstance(buf, jax.ShapeDtypeStruct)

        def body_into(a_ref, b_ref, *rest):
            body(a_ref, b_ref, *rest[-2:])

        return pl.pallas_call(
            body_into, name=name, grid=(M // tm, N // tn, nk),
            in_specs=[a_spec, b_spec] + ([ANY] if aliased else []), out_specs=pl.BlockSpec(blk, bmap),
            out_shape=jax.ShapeDtypeStruct(buf.shape, buf.dtype),
            scratch_shapes=[pltpu.VMEM((tm, tn), F32)],
            input_output_aliases={2: 0} if aliased else {},
            compiler_params=_cparams(("parallel", "parallel", "arbitrary")),
        )(a, b, *([buf] if aliased else []))
    return pl.pallas_call(
        body, name=name, grid=(M // tm, N // tn, nk),
        in_specs=[a_spec, b_spec] + [mn] * len(extras) + [rw] * len(rows), out_specs=[mn] * no + [rw] * n_acc,
        out_shape=[jax.ShapeDtypeStruct((M, N), dt) for dt in out_dtypes] + [jax.ShapeDtypeStruct((1, N), F32)] * n_acc,
        scratch_shapes=[pltpu.VMEM((tm, tn), F32)],
        compiler_params=_cparams(("arbitrary",) * 3 if n_acc else ("parallel", "parallel", "arbitrary")),
    )(a, b, *extras, *rows)


def _epi_res(acc, res):
    return (res + acc,)


def _epi_rms_bwd(acc, x, dres, w):
    return rms_bwd_fn(0, 0, x, acc, dres, w)


def _epi_rms_bwd1(acc, x, dres, w):
    dx, _, gw = rms_bwd_fn(0, 0, x, acc, dres, w)
    return dx, gw


def _epi_final(acc, res, tgt, w):
    return final_fn(0, 0, res + acc, tgt, w)


def _epi_res_rms(acc, res, w):
    x = res + acc
    return (x, x * lax.rsqrt(jnp.mean(x * x, axis=-1, keepdims=True) + EPS) * w)


def _epi_relu2(acc):
    u = jnp.maximum(acc, 0.0)
    return (u, u * u)


def _epi_dup(acc, u):
    return (acc * 2.0 * u.astype(F32),)


def _conv(x, halo, w, i):
    halo = jnp.where(i == 0, 0.0, halo)
    xt = jnp.concatenate([halo, x], axis=0)
    shifted = [pltpu.roll(xt, 3 - k, 0)[8:, :] for k in range(3)] + [x]
    y = shifted[3] * w[3:4, :]
    for k in range(3):
        y = y + shifted[k] * w[k:k + 1, :]
    return y, shifted


def _l2n(x, scale):
    outs = []
    for h in range(x.shape[1] // 128):
        xh = x[:, 128 * h:128 * h + 128]
        outs.append(xh * (lax.rsqrt(jnp.sum(xh * xh, axis=-1, keepdims=True) + EPS) * scale))
    return jnp.concatenate(outs, axis=1)


def _l2n_bwd(x, dy, scale):
    outs = []
    for h in range(x.shape[1] // 128):
        xh, dh = x[:, 128 * h:128 * h + 128], dy[:, 128 * h:128 * h + 128] * scale
        r = lax.rsqrt(jnp.sum(xh * xh, axis=-1, keepdims=True) + EPS)
        outs.append(r * dh - xh * (r * r * r) * jnp.sum(xh * dh, axis=-1, keepdims=True))
    return jnp.concatenate(outs, axis=1)


def rms_fwd_fn(i, n, x, w):
    r = lax.rsqrt(jnp.mean(x * x, axis=-1, keepdims=True) + EPS)
    return (x * r * w,)


def rms_bwd_fn(i, n, x, dh, dres, w):
    r = lax.rsqrt(jnp.mean(x * x, axis=-1, keepdims=True) + EPS)
    g = dh * w
    dx = dres + r * g - x * (r * r * r) * jnp.mean(x * g, axis=-1, keepdims=True)
    return dx, dx, _rows(dh * x * r)


def rms_bwd_w_fn(i, n, x, dh, w):
    r = lax.rsqrt(jnp.mean(x * x, axis=-1, keepdims=True) + EPS)
    return (_rows(dh * x * r),)


def final_fn(i, n, x, tgt, w):
    r = lax.rsqrt(jnp.mean(x * x, axis=-1, keepdims=True) + EPS)
    xn = x * r
    e = xn * w - tgt
    dy = e * (1.0 / D)
    g = dy * w
    dx = r * g - x * (r * r * r) * jnp.mean(x * g, axis=-1, keepdims=True)
    return dx, dx, _rows(e * e), _rows(dy * xn)


def _gdn_gates(ba, alog_c, dtb_c):
    col = _iota(ba.shape, 1)
    amask = (col >= 8) & (col < 16)
    beta = jnp.where(col < 8, _sig(ba), 0.0)
    z = ba + dtb_c
    ea_ = jnp.exp(alog_c)
    return beta, z, ea_, jnp.where(amask, -ea_ * _softplus(z), 0.0), amask


def gdn_prep_fn(i, n, qkv, halo, ba, cw, alog_c, dtb_c, eb, ea):
    yc, _ = _conv(qkv, halo, cw, i)
    act = yc * _sig(yc)
    qn = _l2n(act[:, :D], GDN_DK ** -0.5)
    kn = _l2n(act[:, D:2 * D], 1.0)
    beta, _, _, g, _ = _gdn_gates(ba, alog_c, dtb_c)
    gcs = _chunk_cumsum(g, GDN_C)
    return qn, kn, act[:, 2 * D:], mm_sel(gcs, ea), mm_sel(beta, eb), jnp.transpose(gcs)[8:16, :]


def gdn_prep_bwd_fn(i, n, qkv, halo, ba, dqn, dkn, dv, dgcs_x, dbeta_x, dgcs_t, cw, alog_c, dtb_c, pb, pa):
    yc, shifted = _conv(qkv, halo, cw, i)
    sg = _sig(yc)
    act = yc * sg
    dq = _l2n_bwd(act[:, :D], dqn, GDN_DK ** -0.5)
    dk = _l2n_bwd(act[:, D:2 * D], dkn, 1.0)
    dyc = jnp.concatenate([dq, dk, dv], axis=1) * (sg * (1.0 + yc * (1.0 - sg)))
    dws = [_rows(dyc * shifted[k]) for k in range(4)]
    beta, z, ea_, g, amask = _gdn_gates(ba, alog_c, dtb_c)
    tbn = ba.shape[0]
    rowpart = jnp.transpose(jnp.concatenate([jnp.zeros((8, tbn), F32), dgcs_t, jnp.zeros((112, tbn), F32)], axis=0))
    dg = _chunk_revcumsum(mm_sel(dgcs_x, pa) - rowpart, GDN_C)
    draw = jnp.where(amask, dg * (-ea_) * _sig(z), 0.0)
    dba = draw + mm_sel(dbeta_x, pb) * beta * (1.0 - beta)
    return (dyc, dba, dws[0], dws[1], dws[2], dws[3], _rows(dg * g), _rows(draw))


def conv_bwd_fn(i, n, dyc, halo, w):
    halo = jnp.where(i == n - 1, 0.0, halo)
    tb = dyc.shape[0]
    xt = jnp.concatenate([dyc, halo], axis=0)
    dx = dyc * w[3:4, :]
    for k in range(3):
        dx = dx + pltpu.roll(xt, tb + 8 - (3 - k), 0)[:tb, :] * w[k:k + 1, :]
    return (dx,)


def gdn_post_fn(i, n, o, z, w):
    outs = []
    for h in range(GDN_H):
        oh, zh = o[:, 128 * h:128 * h + 128], z[:, 128 * h:128 * h + 128]
        r = lax.rsqrt(jnp.mean(oh * oh, axis=-1, keepdims=True) + EPS)
        outs.append(oh * r * w * (zh * _sig(zh)))
    return (jnp.concatenate(outs, axis=1),)


def gdn_post_bwd_fn(i, n, o, z, doa, w):
    dos, dzs, dw = [], [], None
    for h in range(GDN_H):
        sl = slice(128 * h, 128 * h + 128)
        oh, zh, dh = o[:, sl], z[:, sl], doa[:, sl]
        r = lax.rsqrt(jnp.mean(oh * oh, axis=-1, keepdims=True) + EPS)
        s = _sig(zh)
        dn = dh * (zh * s)
        dzs.append(dh * (oh * r * w) * (s * (1.0 + zh * (1.0 - s))))
        t = _rows(dn * oh * r)
        dw = t if dw is None else dw + t
        g = dn * w
        dos.append(r * g - oh * (r * r * r) * jnp.mean(oh * g, axis=-1, keepdims=True))
    return jnp.concatenate(dos, axis=1), jnp.concatenate(dzs, axis=1), dw


def _ssd_gates(dtblk, alog_c, dtb_c):
    hmask = _iota(dtblk.shape, 1) < SSM_H
    z = dtblk + dtb_c
    return jnp.where(hmask, _softplus(z), 0.0), -jnp.exp(alog_c), z, hmask


def ssd_prep_fn(i, n, xp, hx, bcp, hbc, dtblk, cwx, cwbc, cbx, cbbc, alog_c, dtb_c, e16):
    yx, _ = _conv(xp, hx, cwx, i)
    yx = yx + cbx
    ybc, _ = _conv(bcp, hbc, cwbc, i)
    ybc = ybc + cbbc
    dt, a_neg, _, _ = _ssd_gates(dtblk, alog_c, dtb_c)
    acs = _chunk_cumsum(dt * a_neg, SSM_L)
    return (yx * _sig(yx), ybc * _sig(ybc), mm_sel(dt, e16), mm_sel(acs, e16), jnp.transpose(acs)[0:SSM_H, :])


def ssd_prep_bwd_fn(i, n, xp, hx, bcp, hbc, dtblk, dxs_a, dxs_b, db, dc, dgate, dacs_t, cwx, cwbc, cbx, cbbc, alog_c, dtb_c):
    dbc = jnp.concatenate([db, dc], axis=1)
    yx, shx = _conv(xp, hx, cwx, i)
    yx = yx + cbx
    ybc, shbc = _conv(bcp, hbc, cwbc, i)
    ybc = ybc + cbbc
    sx, sbc = _sig(yx), _sig(ybc)
    dyx = (dxs_a + dxs_b) * (sx * (1.0 + yx * (1.0 - sx)))
    dybc = dbc * (sbc * (1.0 + ybc * (1.0 - sbc)))
    dwx = [_rows(dyx * shx[k]) for k in range(4)]
    dwbc = [_rows(dybc * shbc[k]) for k in range(4)]
    dt, a_neg, z, hmask = _ssd_gates(dtblk, alog_c, dtb_c)
    g0, g1 = dgate[:, :128], dgate[:, 128:]
    col = _iota(g0.shape, 1)
    lo, mid = col < 8, (col >= 8) & (col < 16)
    dacs_col = jnp.where(lo, g0, 0.0) + pltpu.roll(jnp.where(lo, g1, 0.0), 8, 1)
    ddt_dir = pltpu.roll(jnp.where(mid, g0, 0.0), 120, 1) + jnp.where(mid, g1, 0.0)
    tbn = dtblk.shape[0]
    rowpart = jnp.transpose(jnp.concatenate([dacs_t, jnp.zeros((128 - SSM_H, tbn), F32)], axis=0))
    da = _chunk_revcumsum(dacs_col - rowpart, SSM_L)
    draw = jnp.where(hmask, (ddt_dir + da * a_neg) * _sig(z), 0.0)
    return (dyx, dybc, draw, *dwx, *dwbc, _rows(dyx), _rows(dybc), _rows(da * dt * a_neg), _rows(draw))


def _ssd_gate(y, xs, zs, d_x):
    y2 = y + xs * d_x
    s = _sig(zs)
    return y2, s, y2 * (zs * s)


def ssd_post_fn(i, n, y, xs, zs, d_x, nw):
    _, _, yg = _ssd_gate(y, xs, zs, d_x)
    outs = []
    for g in range(2):
        v = yg[:, 512 * g:512 * g + 512]
        outs.append(v * lax.rsqrt(jnp.mean(v * v, axis=-1, keepdims=True) + EPS))
    return (jnp.concatenate(outs, axis=1) * nw,)


def ssd_post_bwd_fn(i, n, y, xs, zs, dob, d_x, nw):
    y2, s, yg = _ssd_gate(y, xs, zs, d_x)
    gfull = dob * nw
    dygs, dnw = [], []
    for g in range(2):
        sl = slice(512 * g, 512 * g + 512)
        v, gg = yg[:, sl], gfull[:, sl]
        r = lax.rsqrt(jnp.mean(v * v, axis=-1, keepdims=True) + EPS)
        dygs.append(r * gg - v * (r * r * r) * jnp.mean(v * gg, axis=-1, keepdims=True))
        dnw.append(_rows(dob[:, sl] * v * r))
    dyg = jnp.concatenate(dygs, axis=1)
    dy2 = dyg * (zs * s)
    dzs = dyg * y2 * (s * (1.0 + zs * (1.0 - s)))
    return dy2, dy2 * d_x, dzs, jnp.concatenate(dnw, axis=1), _rows(dy2 * xs)


def attn_fn(i, n, q, k, v):
    outs = []
    for h in range(MEM_H):
        sl = slice(MEM_DH * h, MEM_DH * h + MEM_DH)
        s = mm_nt(q[:, sl], k[:, sl]) * (MEM_DH ** -0.5)
        p = jnp.exp(s - jnp.max(s, axis=-1, keepdims=True))
        p = p / jnp.sum(p, axis=-1, keepdims=True)
        outs.append(mm(p, v[:, sl]))
    return (jnp.concatenate(outs, axis=1),)


def attn_bwd_fn(i, n, q, do, k, v):
    dqs, dks, dvs = [], [], []
    for h in range(MEM_H):
        sl = slice(MEM_DH * h, MEM_DH * h + MEM_DH)
        s = mm_nt(q[:, sl], k[:, sl]) * (MEM_DH ** -0.5)
        p = jnp.exp(s - jnp.max(s, axis=-1, keepdims=True))
        p = p / jnp.sum(p, axis=-1, keepdims=True)
        dvs.append(mm_tn(p, do[:, sl]))
        dp = mm_nt(do[:, sl], v[:, sl])
        ds = p * (dp - jnp.sum(dp * p, axis=-1, keepdims=True)) * (MEM_DH ** -0.5)
        dqs.append(mm(ds, k[:, sl]))
        dks.append(mm_tn(ds, q[:, sl]))
    return jnp.concatenate(dqs, axis=1), jnp.concatenate(dks, axis=1), jnp.concatenate(dvs, axis=1)


def add2_fn(i, n, sp, a, b):
    return (a + b,)


def sum4_fn(i, n, sp, a, b, c, d):
    return (((a.astype(F32) + b.astype(F32)) + c.astype(F32)) + d.astype(F32),)


def _adamw(w, g, m, v):
    m = ADAM_B1 * m + (1.0 - ADAM_B1) * g
    v = ADAM_B2 * v + (1.0 - ADAM_B2) * (g * g)
    m_hat = m / (1.0 - ADAM_B1 ** ADAM_STEP)
    v_hat = v / (1.0 - ADAM_B2 ** ADAM_STEP)
    delta = -ADAM_LR * (m_hat / (jnp.sqrt(v_hat) + ADAM_EPS) + ADAM_WD * w)
    return delta, m, v


def _gdn_stage1(q, k, v, gcs, grow, bb):
    C = GDN_C
    row, col = _iota((C, C), 0), _iota((C, C), 1)
    incl, strict = row >= col, row > col
    dmat = jnp.where(incl, jnp.exp(jnp.minimum(gcs[:, :C] - grow, 0.0)), 0.0)
    gam = jnp.exp(gcs)
    gl = gcs[C - 1:C, :]
    kb, vb = k * bb, v * bb
    kg = kb * gam
    lmat = jnp.where(strict, mm_nt(kb, k) * dmat, 0.0)
    pmat = jnp.where(incl, mm_nt(q, k) * dmat, 0.0)
    return dict(q=q, k=k, v=v, bb=bb, incl=incl, strict=strict, dmat=dmat, gam=gam, kb=kb, vb=vb, kg=kg,
                lmat=lmat, pmat=pmat, qd=q * gam, kdec=jnp.exp(gl - gcs), cd=jnp.exp(gl))


def _gdn_inverse(lmats):
    C = GDN_C
    eye = (_iota((C, C), 0) == _iota((C, C), 1)).astype(F32)
    xs = [-l for l in lmats]
    ts = [eye + x for x in xs]
    for _ in range(5):
        xs = [mm(x, x) for x in xs]
        ts = [t + mm(t, x) for t, x in zip(ts, xs)]
    res = [eye - mm3(eye + l, t) for l, t in zip(lmats, ts)]
    return [t + mm(t, r) for t, r in zip(ts, res)]


def gdn_fwd(qn, kn, v, gcs_x, beta_x, gcs_t, tb):
    T = qn.shape[0]
    nb, ncb, nc, C = T // tb, tb // GDN_C, T // GDN_C, GDN_C

    def body(q_ref, k_ref, v_ref, g_ref, b_ref, gt_ref, o_ref, st_ref, ti_ref, s_scr):
        @pl.when(pl.program_id(1) == 0)
        def _():
            s_scr[...] = jnp.zeros_like(s_scr)

        grow = gt_ref[...]
        st1 = []
        for c in range(ncb):
            sl = slice(C * c, C * (c + 1))
            st1.append(_gdn_stage1(q_ref[sl, :], k_ref[sl, :], v_ref[sl, :], g_ref[sl, :], grow[:, sl], b_ref[sl, :]))
        tinvs = _gdn_inverse([s["lmat"] for s in st1])
        us = [mm(t, s["vb"]) for t, s in zip(tinvs, st1)]
        ws = [mm(t, s["kg"]) for t, s in zip(tinvs, st1)]
        for c in range(ncb):
            sl = slice(C * c, C * (c + 1))
            lc = st1[c]
            ti_ref[sl, :] = tinvs[c]
            s = s_scr[...]
            st_ref[c] = s
            vn = us[c] - mm(ws[c], s)
            o_ref[sl, :] = mm(lc["qd"], s) + mm(lc["pmat"], vn)
            s_scr[...] = s * lc["cd"] + mm_tn(lc["k"] * lc["kdec"], vn)

    blk = pl.BlockSpec((tb, 128), lambda h, i: (i, h))
    return pl.pallas_call(
        body, name="gdn_fwd", grid=(GDN_H, nb),
        in_specs=[blk] * 5 + [pl.BlockSpec((None, 1, tb), lambda h, i: (h, 0, i))],
        out_specs=[blk, pl.BlockSpec((None, ncb, 128, 128), lambda h, i: (h, i, 0, 0)),
                   pl.BlockSpec((None, tb, C), lambda h, i: (h, i, 0))],
        out_shape=[jax.ShapeDtypeStruct((T, D), F32), jax.ShapeDtypeStruct((GDN_H, nc, 128, 128), F32),
                   jax.ShapeDtypeStruct((GDN_H, T, C), F32)],
        scratch_shapes=[pltpu.VMEM((128, 128), F32)],
        compiler_params=_cparams(("parallel", "arbitrary")),
    )(qn, kn, v, gcs_x, beta_x, gcs_t)


def gdn_bwd(qn, kn, v, gcs_x, beta_x, gcs_t, do, states, tinv, tb):
    T = qn.shape[0]
    nb, ncb, C = T // tb, tb // GDN_C, GDN_C

    def body(q_ref, k_ref, v_ref, g_ref, b_ref, gt_ref, do_ref, st_ref, ti_ref,
             dq_ref, dk_ref, dv_ref, dgc_ref, db_ref, dgr_ref, ds_scr):
        @pl.when(pl.program_id(1) == 0)
        def _():
            ds_scr[...] = jnp.zeros_like(ds_scr)

        grow = gt_ref[...]
        lastrow = _iota((C, 1), 0) == C - 1
        pre = []
        for c in range(ncb):
            sl = slice(C * c, C * (c + 1))
            lc = _gdn_stage1(q_ref[sl, :], k_ref[sl, :], v_ref[sl, :], g_ref[sl, :], grow[:, sl], b_ref[sl, :])
            tinv_c, s, do_c = ti_ref[sl, :], st_ref[c], do_ref[sl, :]
            u, w = mm(tinv_c, lc["vb"]), mm(tinv_c, lc["kg"])
            vn = u - mm(w, s)
            lc.update(tinv=tinv_c, s=s, u=u, w=w, vn=vn, dqd=mm_nt(do_c, s),
                      dp=jnp.where(lc["incl"], mm_nt(do_c, vn), 0.0), ds_q=mm_tn(lc["qd"], do_c),
                      dvn_p=mm_tn(lc["pmat"], do_c), kd=lc["k"] * lc["kdec"])
            pre.append(lc)
        rows = [None] * ncb
        for c in reversed(range(ncb)):
            sl = slice(C * c, C * (c + 1))
            lc = pre[c]
            q, k, vv, bb, gam, kd, u, w, s, tinv_c = (lc[n] for n in ("q", "k", "v", "bb", "gam", "kd", "u", "w", "s", "tinv"))
            dsn = ds_scr[...]
            dvn = lc["dvn_p"] + mm(kd, dsn)
            dkd = mm_nt(lc["vn"], dsn)
            dcd = _sum_all(s * dsn)
            ds_scr[...] = lc["ds_q"] + lc["cd"] * dsn - mm_tn(w, dvn)
            dw = -mm_nt(dvn, s)
            dvb, dkg = mm_tn(tinv_c, dvn), mm_tn(tinv_c, dw)
            da = -jnp.where(lc["strict"], mm_nt(dvb, u) + mm_nt(dkg, w), 0.0)
            dm = da * lc["dmat"]
            dn = lc["dp"] * lc["dmat"]
            dkb = mm(dm, k)
            e = da * lc["lmat"] + lc["dp"] * lc["pmat"]
            t_kd = _lanes(dkd * kd)
            dgl = _sum_all(t_kd) + dcd * lc["cd"][:, :1]
            dgcs = (_lanes(e) + _lanes(lc["dqd"] * lc["qd"]) - t_kd + _lanes(dkg * lc["kg"])
                    + jnp.where(lastrow, dgl, 0.0))
            rows[c] = _rows(e)
            dq_ref[sl, :] = mm(dn, k) + gam * lc["dqd"]
            dk_ref[sl, :] = (mm_tn(dm, lc["kb"]) + mm_tn(dn, q) + lc["kdec"] * dkd + bb * gam * dkg + bb * dkb)
            dv_ref[sl, :] = bb * dvb
            dbeta = _lanes(dkg * gam * k) + _lanes(dvb * vv) + _lanes(dkb * k)
            db_ref[sl, :] = jnp.broadcast_to(dbeta, (C, 128))
            dgc_ref[sl, :] = jnp.broadcast_to(dgcs, (C, 128))
        dgr_ref[...] = jnp.concatenate(rows, axis=1)

    blk = pl.BlockSpec((tb, 128), lambda h, i: (nb - 1 - i, h))
    rowspec = pl.BlockSpec((None, 1, tb), lambda h, i: (h, 0, nb - 1 - i))
    return pl.pallas_call(
        body, name="gdn_bwd", grid=(GDN_H, nb),
        in_specs=[blk] * 5 + [rowspec, blk,
                              pl.BlockSpec((None, ncb, 128, 128), lambda h, i: (h, nb - 1 - i, 0, 0)),
                              pl.BlockSpec((None, tb, C), lambda h, i: (h, nb - 1 - i, 0))],
        out_specs=[blk] * 5 + [rowspec],
        out_shape=[jax.ShapeDtypeStruct((T, D), F32)] * 5 + [jax.ShapeDtypeStruct((GDN_H, 1, T), F32)],
        scratch_shapes=[pltpu.VMEM((128, 128), F32)],
        compiler_params=_cparams(("parallel", "arbitrary")),
    )(qn, kn, v, gcs_x, beta_x, gcs_t, do, states, tinv)


def _ssd_pair(x2, dt2, acs2):
    last = acs2[SSM_L - 1:SSM_L, :]
    return jnp.exp(acs2), jnp.exp(last - acs2), x2 * dt2


def _ssd_head(hh, acs2, arow, dec2, cbm, bm, incl, col):
    lmask = (col >= 64 * hh) & (col < 64 * hh + 64)
    sg = jnp.where(incl, jnp.exp(jnp.minimum(acs2[:, 64 * hh:64 * hh + 1] - arow, 0.0)), 0.0)
    dec_col = dec2[:, 64 * hh:64 * hh + 1]
    return lmask, sg, sg * cbm, dec_col, bm * dec_col


def ssd_fwd(xs, bc, dt_x, acs_x, acs_t):
    T = xs.shape[0]
    nc, L = T // SSM_L, SSM_L

    def body(x_ref, b_ref, c_ref, dt_ref, ac_ref, at_ref, y_ref, hst_ref, h_scr):
        @pl.when(pl.program_id(1) == 0)
        def _():
            h_scr[...] = jnp.zeros_like(h_scr)

        bm, cm = b_ref[...], c_ref[...]
        cbm = mm_nt(cm, bm)
        row, col = _iota((L, L), 0), _iota((L, L), 1)
        incl = row >= col
        for pr in range(4):
            sl = slice(128 * pr, 128 * pr + 128)
            acs2 = ac_ref[:, sl]
            lam2, dec2, xd2 = _ssd_pair(x_ref[:, sl], dt_ref[:, sl], acs2)
            y2, st = None, []
            for hh in range(2):
                lmask, _, mmat, _, bd = _ssd_head(hh, acs2, at_ref[2 * pr + hh], dec2, cbm, bm, incl, col)
                t = mm(mmat, jnp.where(lmask, xd2, 0.0))
                y2 = t if y2 is None else y2 + t
                st.append(mm_tn(xd2, bd))
            hprev = h_scr[pr]
            hst_ref[pr] = hprev
            y_ref[:, sl] = y2 + lam2 * mm_nt(cm, hprev)
            lam_rows = jnp.where(row < 64, lam2[L - 1:L, 0:1], lam2[L - 1:L, 64:65])
            h_scr[pr] = lam_rows * hprev + jnp.where(row < 64, st[0], st[1])

    return pl.pallas_call(
        body, name="ssd_fwd", grid=(2, nc),
        in_specs=[pl.BlockSpec((L, 512), lambda g, c: (c, g)),
                  pl.BlockSpec((L, 128), lambda g, c: (c, g)),
                  pl.BlockSpec((L, 128), lambda g, c: (c, 2 + g)),
                  pl.BlockSpec((L, 512), lambda g, c: (c, g)),
                  pl.BlockSpec((L, 512), lambda g, c: (c, g)),
                  pl.BlockSpec((8, 1, L), lambda g, c: (g, 0, c))],
        out_specs=[pl.BlockSpec((L, 512), lambda g, c: (c, g)),
                   pl.BlockSpec((None, None, 4, 128, 128), lambda g, c: (g, c, 0, 0, 0))],
        out_shape=[jax.ShapeDtypeStruct((T, D), F32), jax.ShapeDtypeStruct((2, nc, 4, 128, 128), F32)],
        scratch_shapes=[pltpu.VMEM((4, 128, 128), F32)],
        compiler_params=_cparams(("parallel", "arbitrary")),
    )(xs, bc, bc, dt_x, acs_x, acs_t)


def ssd_bwd(xs, bc, dt_x, acs_x, acs_t, dy, hstates):
    T = xs.shape[0]
    nc, L = T // SSM_L, SSM_L

    def body(x_ref, b_ref, c_ref, dt_ref, ac_ref, at_ref, dy_ref, hst_ref,
             dx_ref, db_ref, dc_ref, dgate_ref, dar_ref, dh_scr):
        @pl.when(pl.program_id(1) == 0)
        def _():
            dh_scr[...] = jnp.zeros_like(dh_scr)

        bm, cm = b_ref[...], c_ref[...]
        cbm = mm_nt(cm, bm)
        row, col = _iota((L, L), 0), _iota((L, L), 1)
        rowc = _iota((L, 1), 0)
        incl = row >= col
        dcb = jnp.zeros((L, L), F32)
        dbm = jnp.zeros((L, SSM_N), F32)
        dcm = jnp.zeros((L, SSM_N), F32)
        comp = jnp.zeros((L, 128), F32)
        for pr in range(4):
            sl = slice(128 * pr, 128 * pr + 128)
            x2, dt2, dy2, acs2 = x_ref[:, sl], dt_ref[:, sl], dy_ref[:, sl], ac_ref[:, sl]
            lam2, dec2, xd2 = _ssd_pair(x2, dt2, acs2)
            hprev, dhn = hst_ref[pr], dh_scr[pr]
            dz = lam2 * dy2
            yoff = dz * mm_nt(cm, hprev)
            dcm = dcm + mm(dz, hprev)
            q_rows = _lanes(dhn * hprev)
            dxd2 = jnp.zeros((L, 128), F32)
            for hh in range(2):
                lmask, sg, mmat, dec_col, bd = _ssd_head(hh, acs2, at_ref[2 * pr + hh], dec2, cbm, bm, incl, col)
                dm = jnp.where(incl, mm_nt(jnp.where(lmask, dy2, 0.0), xd2), 0.0)
                dcb = dcb + dm * sg
                e = dm * mmat
                dxd_h = jnp.where(lmask, mm_tn(mmat, dy2) + mm_nt(bd, dhn), 0.0)
                dxd2 = dxd2 + dxd_h
                dbd = mm(jnp.where(lmask, xd2, 0.0), dhn)
                dbm = dbm + dec_col * dbd
                t = _lanes(dbd * bd)
                lam_h = lam2[L - 1:L, 64 * hh:64 * hh + 1]
                in_head = (rowc >= 64 * hh) & (rowc < 64 * hh + 64)
                add_last = _sum_all(t) + _sum_all(jnp.where(in_head, q_rows, 0.0)) * lam_h
                dacs_col = (_lanes(jnp.where(lmask, yoff, 0.0)) + _lanes(e) - t
                            + jnp.where(rowc == L - 1, add_last, 0.0))
                ddt_col = _lanes(dxd_h * x2)
                j = 2 * pr + hh
                dar_ref[j] = _rows(e)
                comp = comp + jnp.where(col == j, dacs_col, 0.0) + jnp.where(col == 8 + j, ddt_col, 0.0)
            lam_rows = jnp.where(row < 64, lam2[L - 1:L, 0:1], lam2[L - 1:L, 64:65])
            dh_scr[pr] = mm_tn(dz, cm) + lam_rows * dhn
            dx_ref[:, sl] = dt2 * dxd2
        db_ref[...] = dbm + mm_tn(dcb, cm)
        dc_ref[...] = dcm + mm(dcb, bm)
        dgate_ref[...] = comp

    rv = lambda g, c: (nc - 1 - c, g)
    rowspec = pl.BlockSpec((8, 1, L), lambda g, c: (g, 0, nc - 1 - c))
    return pl.pallas_call(
        body, name="ssd_bwd", grid=(2, nc),
        in_specs=[pl.BlockSpec((L, 512), rv),
                  pl.BlockSpec((L, 128), rv),
                  pl.BlockSpec((L, 128), lambda g, c: (nc - 1 - c, 2 + g)),
                  pl.BlockSpec((L, 512), rv),
                  pl.BlockSpec((L, 512), rv),
                  rowspec,
                  pl.BlockSpec((L, 512), rv),
                  pl.BlockSpec((None, None, 4, 128, 128), lambda g, c: (g, nc - 1 - c, 0, 0, 0))],
        out_specs=[pl.BlockSpec((L, 512), rv), pl.BlockSpec((L, 128), rv), pl.BlockSpec((L, 128), rv),
                   pl.BlockSpec((L, 128), rv), rowspec],
        out_shape=[jax.ShapeDtypeStruct((T, D), F32), jax.ShapeDtypeStruct((T, 256), F32),
                   jax.ShapeDtypeStruct((T, 256), F32), jax.ShapeDtypeStruct((T, 256), F32),
                   jax.ShapeDtypeStruct((SSM_H, 1, T), F32)],
        scratch_shapes=[pltpu.VMEM((4, 128, 128), F32)],
        compiler_params=_cparams(("parallel", "arbitrary")),
    )(xs, bc, bc, dt_x, acs_x, acs_t, dy, hstates)


def _pos():
    return lax.axis_index("x"), lax.axis_index("y"), lax.axis_index("c")


def _other_chips(x, y):
    return [(1 - x, y), (x, 1 - y), (1 - x, 1 - y)]


def _rcopy(src, dst, ssem, rsem, dev):
    return pltpu.make_async_remote_copy(src_ref=src, dst_ref=dst, send_sem=ssem, recv_sem=rsem,
                                        device_id=dev, device_id_type=MESH)


def _rows_at(start, n):
    return pl.ds(pl.multiple_of(start, 8), n)


def _comm_call(body, name, out_shape, n_in, scratch):
    return pl.pallas_call(
        body, name=name, out_shape=out_shape, in_specs=[ANY] * n_in,
        out_specs=[ANY] * len(out_shape) if isinstance(out_shape, (list, tuple)) else ANY,
        scratch_shapes=scratch,
        compiler_params=pltpu.CompilerParams(has_side_effects=True),
    )


def _dma_sems(n):
    return pltpu.SemaphoreType.DMA((n,))


def ag_chips(name, shard):
    rr, cc = shard.shape
    h, nq = rr // 2, ICI_CHUNKS
    hq = h // nq

    def body(x_ref, out_ref, ssem, rsem):
        x, y, c = _pos()
        chips = _other_chips(x, y)
        started = []
        for q in range(nq):
            rows = _rows_at(c * h + q * hq, hq)
            for j, (cx, cy) in enumerate(chips):
                cp = _rcopy(x_ref.at[rows], out_ref.at[j, rows], ssem.at[j * nq + q], rsem.at[j * nq + q], (cx, cy, c))
                cp.start()
                started.append(cp)
        for q in range(nq):
            rows = _rows_at(c * h + q * hq, hq)
            for j, (cx, cy) in enumerate(chips):
                blk = out_ref.at[j, rows]
                _rcopy(blk, blk, ssem.at[j * nq + q], rsem.at[j * nq + q], (cx, cy, c)).wait_recv()
                k = 3 * nq + j * nq + q
                cp = _rcopy(blk, blk, ssem.at[k], rsem.at[k], (x, y, 1 - c))
                cp.start()
                started.append(cp)
        for q in range(nq):
            rows = _rows_at((1 - c) * h + q * hq, hq)
            for j in range(3):
                blk = out_ref.at[j, rows]
                k = 3 * nq + j * nq + q
                _rcopy(blk, blk, ssem.at[k], rsem.at[k], (x, y, 1 - c)).wait_recv()
        for cp in started:
            cp.wait_send()

    return _comm_call(body, name, jax.ShapeDtypeStruct((3, rr, cc), shard.dtype), 1,
                      [_dma_sems(6 * nq), _dma_sems(6 * nq)])(shard)


def all_gather_chips(name, shard, s_me):
    got = ag_chips(name, shard)
    by_rel = jnp.stack([shard, got[1], got[0], got[2]])
    return jnp.take(by_rel, jnp.arange(4) ^ s_me, axis=0)


def rs_pair(name, g):
    _, rr, cc = g.shape
    h, nq = rr // 2, D2D_CHUNKS
    hq = h // nq

    def body(g_ref, recv_ref, ssem, rsem):
        x, y, c = _pos()
        cps = []
        for q in range(nq):
            cp = _rcopy(g_ref.at[:, _rows_at((1 - c) * h + q * hq, hq), :], recv_ref.at[:, pl.ds(q * hq, hq), :],
                        ssem.at[q], rsem.at[q], (x, y, 1 - c))
            cp.start()
            cps.append(cp)
        for cp in cps:
            cp.wait()

    return _comm_call(body, name, jax.ShapeDtypeStruct((4, h, cc), g.dtype), 1, [_dma_sems(nq), _dma_sems(nq)])(g)


def rs_chips(name, p):
    _, h, cc = p.shape
    nq = ICI_CHUNKS
    hq = h // nq

    def body(p_ref, buf_ref, ssem, rsem):
        x, y, c = _pos()
        sends = []
        for q in range(nq):
            rows = pl.ds(q * hq, hq)
            for j, (cx, cy) in enumerate(_other_chips(x, y)):
                cp = _rcopy(p_ref.at[2 * cx + cy, rows], buf_ref.at[j, rows], ssem.at[j * nq + q],
                            rsem.at[j * nq + q], (cx, cy, c))
                cp.start()
                sends.append(cp)
        for cp in sends:
            cp.wait()

    return _comm_call(body, name, jax.ShapeDtypeStruct((3, h, cc), p.dtype), 1,
                      [_dma_sems(3 * nq), _dma_sems(3 * nq)])(p)


def rs_join(name, half):
    h, cc = half.shape
    nq = D2D_CHUNKS
    hq = h // nq

    def body(h_ref, out_ref, ssem, rsem):
        x, y, c = _pos()
        cps = []
        for q in range(nq):
            rows = pl.ds(q * hq, hq)
            cp = _rcopy(h_ref.at[rows], out_ref.at[rows], ssem.at[q], rsem.at[q], (x, y, 1 - c))
            cp.start()
            cps.append(cp)
        for cp in cps:
            cp.wait()

    return _comm_call(body, name, jax.ShapeDtypeStruct((h, cc), half.dtype), 1, [_dma_sems(nq), _dma_sems(nq)])(half)


def reduce_scatter(tag, g, tb, sp):
    _, rr, cc = g.shape
    h = rr // 2
    nbh = h // tb
    recv = rs_pair(tag + "_pair", g)
    mine_rows = lambda i, s: (i // nbh) * (2 * nbh) + s[0] * nbh + i % nbh
    part = rowwise(add2_fn, tag + "_add", 4 * h, tb, [R(g.reshape(4 * rr, cc), off=mine_rows), R(recv.reshape(4 * h, cc))],
                   [], [(cc, BF16)], sp=sp)[0]
    buf = rs_chips(tag + "_chips", part.reshape(4, h, cc)).reshape(3 * h, cc)
    red = rowwise(sum4_fn, tag + "_sum", h, tb,
                  [R(part, off=lambda i, s: s[1] * nbh + i)] + [R(buf, off=k * nbh) for k in range(3)],
                  [], [(cc, F32)], sp=sp)[0]
    return red, rs_join(tag + "_join", red)


def adam_halves(name, w, m, v, red, other, tb, blk0, sp):
    nbh = red.shape[0] // tb

    def fn(i, n, s, w_, m_, v_, r_, o_):
        g = jnp.where((blk0 + i) // nbh == s[0], r_, o_)
        return (g,) + _adamw(w_, g, m_, v_)

    half_rows = lambda i, s: (blk0 + i) % nbh
    return rowwise(fn, name, w.shape[0], tb, [R(w), R(m), R(v), R(red, off=half_rows), R(other, off=half_rows)],
                   [], [(w.shape[1], F32)] * 4, sp=sp)


SMALL_LANES = 3 * D


def all_reduce_items(name, items):
    flat = [a for it in items for a in it]
    shapes = [(sum(a.shape[0] for a in it), it[0].shape[1]) for it in items]
    nrows = -(-sum(s[0] for s in shapes) // 8) * 8

    def body(*refs):
        ins, outs = refs[:len(flat)], refs[len(flat):len(flat) + len(items)]
        mine, buf, ssem, rsem = refs[len(flat) + len(items):]
        x, y, c = _pos()
        me = 4 * x + 2 * y + c
        mine[...] = jnp.zeros_like(mine)
        r = 0
        for ref in ins:
            mine[r:r + ref.shape[0], 0:ref.shape[1]] = ref[...]
            r += ref.shape[0]
        buf[me] = mine[...]
        cps = []
        for k in range(1, 8):
            dev = (x ^ (k >> 2), y ^ ((k >> 1) & 1), c ^ (k & 1))
            cp = _rcopy(mine, buf.at[me], ssem.at[k - 1], rsem.at[k - 1], dev)
            cp.start()
            cps.append(cp)
        for cp in cps:
            cp.wait()
        r = 0
        for (nr, n), out in zip(shapes, outs):
            acc = buf[0, r:r + nr, 0:n]
            for d in range(1, 8):
                acc = acc + buf[d, r:r + nr, 0:n]
            out[...] = acc
            r += nr

    vm = pl.BlockSpec(memory_space=pltpu.VMEM)
    return pl.pallas_call(
        body, name=name, out_shape=[jax.ShapeDtypeStruct(s, F32) for s in shapes],
        in_specs=[vm] * len(flat), out_specs=[vm] * len(items),
        scratch_shapes=[pltpu.VMEM((nrows, SMALL_LANES), F32), pltpu.VMEM((8, nrows, SMALL_LANES), F32),
                        _dma_sems(7), _dma_sems(7)],
        compiler_params=pltpu.CompilerParams(has_side_effects=True),
    )(*flat)


def adam_small(ws, gs, ms, vs):
    n = len(ws)

    def body(*refs):
        for k in range(n):
            w, g, m, v = (refs[j * n + k][...] for j in range(4))
            for j, val in enumerate(_adamw(w, g, m, v)):
                refs[(4 + j) * n + k][...] = val

    vm = pl.BlockSpec(memory_space=pltpu.VMEM)
    res = pl.pallas_call(
        body, name="adam_small", out_shape=[jax.ShapeDtypeStruct(w.shape, F32) for w in ws] * 3,
        in_specs=[vm] * (4 * n), out_specs=[vm] * (3 * n),
    )(*ws, *gs, *ms, *vs)
    return res[:n], res[n:2 * n], res[2 * n:]


def _sel(rows, cols, pairs):
    m = np.zeros((rows, cols), np.float32)
    for r, c in pairs:
        m[r, c] = 1.0
    return jnp.asarray(m)


def _pad_win(w):
    z = jnp.zeros((w.shape[0], 112), w.dtype)
    return jnp.concatenate([w[:, :4096], w[:, 4112:6672], w[:, 4096:4112], z, w[:, 6672:6688], z], axis=1)


def _unpad_win(wp):
    return jnp.concatenate([wp[:, :4096], wp[:, 6656:6672], wp[:, 4096:6656], wp[:, 6784:6800]], axis=1)


def kernel(x, mem, norm1_w, w_in, gdn_conv_w, gdn_a_log, gdn_dt_bias, gdn_norm_w, ssm_conv_w, ssm_conv_b, ssm_a_log, ssm_dt_bias, ssm_d, ssm_norm_w, w_out, norm2_w, mem_norm_w, wq_mem, wk_mem, wv_mem, wo_mem, norm3_w, w_up, w_down, final_norm_w, loss_target, m_norm1_w, m_w_in, m_gdn_conv_w, m_gdn_a_log, m_gdn_dt_bias, m_gdn_norm_w, m_ssm_conv_w, m_ssm_conv_b, m_ssm_a_log, m_ssm_dt_bias, m_ssm_d, m_ssm_norm_w, m_w_out, m_norm2_w, m_mem_norm_w, m_wq_mem, m_wk_mem, m_wv_mem, m_wo_mem, m_norm3_w, m_w_up, m_w_down, m_final_norm_w, v_norm1_w, v_w_in, v_gdn_conv_w, v_gdn_a_log, v_gdn_dt_bias, v_gdn_norm_w, v_ssm_conv_w, v_ssm_conv_b, v_ssm_a_log, v_ssm_dt_bias, v_ssm_d, v_ssm_norm_w, v_w_out, v_norm2_w, v_mem_norm_w, v_wq_mem, v_wk_mem, v_wv_mem, v_wo_mem, v_norm3_w, v_w_up, v_w_down, v_final_norm_w):
    T, M = x.shape[1], mem.shape[1]
    xi, yi, ci = _pos()
    s_me = 2 * xi + yi
    x0, mem0, tgt = x[0], mem[0], loss_target[0]
    tb = min(256, T)
    row = lambda v: v.reshape(1, -1)

    win_g = all_gather_chips("ag_win", w_in.astype(BF16), s_me)
    rest_g = all_gather_chips("ag_rest", jnp.concatenate([w_up, w_down, w_out, wq_mem, wk_mem, wv_mem, wo_mem],
                                                         axis=0).astype(BF16), s_me)
    w_in_p = _pad_win(win_g.transpose(1, 0, 2).reshape(D, IN_COLS))
    wup_f = rest_g[:, 0:1024].transpose(1, 0, 2).reshape(D, D_FF)
    wdown_f = rest_g[:, 1024:2048].reshape(D_FF, D)
    wout_f = rest_g[:, 2048:2560].reshape(2 * D, D)
    wq_f, wk_f, wv_f, wo_f = (rest_g[:, 2560 + 256 * k:2816 + 256 * k].reshape(D, D) for k in range(4))
    keep = (ci == 0).astype(F32)
    gcw_z = lax.dynamic_update_slice(jnp.zeros((4, 3 * D), F32), gdn_conv_w * keep, (0, s_me * 768))
    scw_z = lax.dynamic_update_slice(jnp.zeros((4, 1536), F32), ssm_conv_w * keep, (0, s_me * 384))
    gcw, scw = all_reduce_items("ar_convw", [[gcw_z], [scw_z]])
    scw_x, scw_bc = scw[:, :D], scw[:, D:]
    sp = jnp.stack([ci, s_me]).astype(jnp.int32)
    scb_x, scb_bc = row(ssm_conv_b[:D]), row(ssm_conv_b[D:])

    galog_c, gdtb_c = row(jnp.pad(gdn_a_log, (8, 112))), row(jnp.pad(gdn_dt_bias, (8, 112)))
    salog_c, sdtb_c = row(jnp.pad(ssm_a_log, (0, 112))), row(jnp.pad(ssm_dt_bias, (0, 112)))
    sd_x = row(jnp.repeat(ssm_d, 64))
    eb = _sel(128, D, [(h, 128 * h + l) for h in range(8) for l in range(128)])
    ea = _sel(128, D, [(8 + h, 128 * h + l) for h in range(8) for l in range(128)])
    e16 = _sel(128, D, [(h, 64 * h + l) for h in range(16) for l in range(64)])
    pb = _sel(D, 128, [(128 * h, h) for h in range(8)])
    pa = _sel(D, 128, [(128 * h, 8 + h) for h in range(8)])

    h1 = rowwise(rms_fwd_fn, "rms1", T, tb, [R(x0)], [row(norm1_w)], [(D, BF16)])[0]
    p = matmul("mm_in", h1, w_in_p, "nn", 1024, 768, 1024, [F32])[0]
    gp_ins = [R(p, 3 * D, CB_QKV, "prev"), R(p, 128, CB_BA)]
    qn, kn, vv, gcs_x, beta_x, gcs_t = rowwise(gdn_prep_fn, "gdn_prep", T, tb, gp_ins,
                                               [gcw, galog_c, gdtb_c, eb, ea], [(D, F32)] * 5 + [(-8, F32)])
    gcs_t = gcs_t.reshape(GDN_H, 1, T)
    gtb = min(512, T)
    o_gdn, s_states, tinv = gdn_fwd(qn, kn, vv, gcs_x, beta_x, gcs_t, gtb)
    gnw = row(gdn_norm_w)
    oa = rowwise(gdn_post_fn, "gdn_post", T, tb, [R(o_gdn), R(p, D, CB_Z)], [gnw], [(D, BF16)])[0]
    sp_ins = [R(p, D, CB_XS, "prev"), R(p, 512, CB_BC, "prev"), R(p, 128, CB_DT)]
    sp_full = [scw_x, scw_bc, scb_x, scb_bc, salog_c, sdtb_c]
    xs, bc, dt_x, acs_x, acs_t = rowwise(ssd_prep_fn, "ssd_prep", T, tb, sp_ins, sp_full + [e16],
                                         [(D, F32), (512, F32), (D, F32), (D, F32), (-SSM_H, F32)])
    acs_t = acs_t.reshape(SSM_H, 1, T)
    y_ssd, h_states = ssd_fwd(xs, bc, dt_x, acs_x, acs_t)
    snw = row(ssm_norm_w)
    ob = rowwise(ssd_post_fn, "ssd_post", T, tb, [R(y_ssd), R(xs), R(p, D, CB_ZS)], [sd_x, snw], [(D, BF16)])[0]
    x1a = matmul("mm_out_a", oa, wout_f[:D], "nn", 1024, 1024, 1024, [F32], _epi_res, [x0])[0]
    assert D == 1024
    x1, h2 = matmul("mm_out_b", ob, wout_f[D:], "nn", 1024, 1024, 1024, [F32, BF16], _epi_res_rms, [x1a],
                    [row(norm2_w)])

    mn = rowwise(rms_fwd_fn, "rms_mem", M, M, [R(mem0)], [row(mem_norm_w)], [(D, BF16)])[0]
    km = matmul("mm_k", mn, wk_f, "nn", 256, 1024, 1024, [BF16])[0]
    vm = matmul("mm_v", mn, wv_f, "nn", 256, 1024, 1024, [BF16])[0]
    qm = matmul("mm_q", h2, wq_f, "nn", 1024, 1024, 1024, [BF16])[0]
    ao = rowwise(attn_fn, "attn", T, tb, [R(qm)], [km, vm], [(D, BF16)])[0]
    x2, h3 = matmul("mm_o", ao, wo_f, "nn", 1024, 1024, 1024, [F32, BF16], _epi_res_rms, [x1], [row(norm3_w)])
    u, act = matmul("mm_up", h3, wup_f, "nn", 1024, 1024, 1024, [BF16, BF16], _epi_relu2)
    dx3, dx3b, loss_lane, g_final = matmul("mm_down", act, wdown_f, "nn", 512, 1024, 1024, [F32, BF16], _epi_final,
                                           [x2, tgt], [row(final_norm_w)], n_acc=2)
    loss = lax.psum(0.5 / D * jnp.sum(loss_lane), ("x", "y", "c"))

    dup = matmul("mm_dact", dx3b, wdown_f, "nt", 1024, 1024, 1024, [BF16], _epi_dup, [u])[0]
    def g_into(buf, blk, at):
        return dict(into=(buf, blk, lambda i, j, k, at=at: at(i, j)))

    grest = jax.ShapeDtypeStruct((4, 3584, D), F32)
    grest = matmul("mm_gdown", act, dx3b, "tn", 1024, 1024, 1024, [F32],
                   **g_into(grest, (None, 1024, D), lambda i, j: (i, 1, 0)))
    dx2, dx2b, g_n3 = matmul("mm_dh3", dup, wup_f, "nt", 512, 1024, 1024, [F32, BF16], _epi_rms_bwd, [x2, dx3],
                             [row(norm3_w)], n_acc=1)
    grest = matmul("mm_gup", h3, dup, "tn", 1024, 1024, 1024, [F32],
                   **g_into(grest, (None, 1024, D), lambda i, j: (j, 0, 0)))
    dao = matmul("mm_dao", dx2b, wo_f, "nt", 1024, 1024, 1024, [F32])[0]
    grest = matmul("mm_gwo", ao, dx2b, "tn", 1024, 1024, 1024, [F32],
                   **g_into(grest, (4, 256, D), lambda i, j: (0, 13, 0)))
    dqm, dkm, dvm = rowwise(attn_bwd_fn, "attn_bwd", T, tb, [R(qm), R(dao)], [km, vm], [(D, BF16)],
                            [(M, D), (M, D)])
    dx1, dx1b, g_n2 = matmul("mm_dh2", dqm, wq_f, "nt", 512, 1024, 1024, [F32, BF16], _epi_rms_bwd, [x1, dx2],
                             [row(norm2_w)], n_acc=1)
    grest = matmul("mm_gwq", h2, dqm, "tn", 1024, 1024, 1024, [F32],
                   **g_into(grest, (4, 256, D), lambda i, j: (0, 10, 0)))
    grest = matmul("mm_gwk", mn, dkm, "tn", 1024, 1024, 256, [F32],
                   **g_into(grest, (4, 256, D), lambda i, j: (0, 11, 0)))
    grest = matmul("mm_gwv", mn, dvm, "tn", 1024, 1024, 256, [F32],
                   **g_into(grest, (4, 256, D), lambda i, j: (0, 12, 0)))
    dmn_k = matmul("mm_dmk", dkm, wk_f, "nt", 256, 1024, 1024, [F32])[0]
    dmn = matmul("mm_dmv", dvm, wv_f, "nt", 256, 1024, 1024, [F32], _epi_res, [dmn_k])[0]
    g_nmem = rowwise(rms_bwd_w_fn, "rmsmem_bwd", M, M, [R(mem0), R(dmn)], [row(mem_norm_w)], [], [(1, D)])[0]
    doa = matmul("mm_doa", dx1b, wout_f[:D], "nt", 1024, 1024, 1024, [F32])[0]
    dob = matmul("mm_dob", dx1b, wout_f[D:], "nt", 1024, 1024, 1024, [F32])[0]
    grest = matmul("mm_gwout_a", oa, dx1b, "tn", 1024, 1024, 1024, [F32],
                   **g_into(grest, (2, 512, D), lambda i, j: (0, 4, 0)))
    grest = matmul("mm_gwout_b", ob, dx1b, "tn", 1024, 1024, 1024, [F32],
                   **g_into(grest, (2, 512, D), lambda i, j: (1, 4, 0)))

    dp = jax.ShapeDtypeStruct((T, p.shape[1]), BF16)
    dy_ssd, dxs_dir, dp, g_snw, g_sd_lane = rowwise(
        ssd_post_bwd_fn, "ssd_post_bwd", T, tb, [R(y_ssd), R(xs), R(p, D, CB_ZS), R(dob)], [sd_x, snw],
        [(D, F32), (D, F32), (D, BF16, dp, CB_ZS)], [(1, D), (1, D)])
    dxs_scan, db_s, dc_s, dgate, dacs_t = ssd_bwd(xs, bc, dt_x, acs_x, acs_t, dy_ssd, h_states)
    spb = rowwise(ssd_prep_bwd_fn, "ssd_prep_bwd", T, tb,
                  sp_ins + [R(dxs_scan), R(dxs_dir), R(db_s), R(dc_s), R(dgate), RC(dacs_t.reshape(SSM_H, T))], sp_full,
                  [(D, F32), (512, F32), (128, BF16, dp, CB_DT)],
                  [(1, D)] * 4 + [(1, 512)] * 4 + [(1, D), (1, 512), (1, 128), (1, 128)])
    dyc_x, dyc_bc, dp = spb[:3]
    dp = rowwise(conv_bwd_fn, "conv_bwd_x", T, tb, [R(dyc_x, halo="next")], [scw_x], [(D, BF16, dp, CB_XS)])[0]
    dp = rowwise(conv_bwd_fn, "conv_bwd_bc", T, tb, [R(dyc_bc, halo="next")], [scw_bc], [(512, BF16, dp, CB_BC)])[0]

    do_gdn, dp, g_gnw = rowwise(gdn_post_bwd_fn, "gdn_post_bwd", T, tb, [R(o_gdn), R(p, D, CB_Z), R(doa)], [gnw],
                                [(D, F32), (D, BF16, dp, CB_Z)], [(1, 128)])
    dqn, dkn, dvv, dgcs_x, dbeta_x, dgcs_t = gdn_bwd(qn, kn, vv, gcs_x, beta_x, gcs_t, do_gdn, s_states, tinv, gtb)
    gpb = rowwise(gdn_prep_bwd_fn, "gdn_prep_bwd", T, tb,
                  gp_ins + [R(dqn), R(dkn), R(dvv), R(dgcs_x), R(dbeta_x), RC(dgcs_t.reshape(GDN_H, T))],
                  [gcw, galog_c, gdtb_c, pb, pa],
                  [(3 * D, F32), (128, BF16, dp, CB_BA)], [(1, 3 * D)] * 4 + [(1, 128), (1, 128)])
    dyc_qkv, dp = gpb[:2]
    dp = rowwise(conv_bwd_fn, "conv_bwd_qkv", T, tb, [R(dyc_qkv, halo="next")], [gcw], [(3 * D, BF16, dp, CB_QKV)])[0]
    grad_x, g_n1 = matmul("mm_dh1", dp, w_in_p, "nt", 512, 1024, 768, [F32], _epi_rms_bwd1, [x0, dx1],
                          [row(norm1_w)], n_acc=1)
    g_win_p = matmul("mm_gwin", h1, dp, "tn", 1024, 768, 1024, [F32])[0]

    items = [[g_n1], [gpb[6]], [gpb[7]], [g_gnw], [spb[11]], [spb[12]], [spb[13]], [spb[14]], [g_sd_lane], [g_snw],
             [g_n2], [g_nmem], [g_n3], [g_final], list(gpb[2:6]), list(spb[3:7]), list(spb[7:11])]
    (gr_n1, r_galog, r_gdtb, gr_gnw, r_scb_x, r_scb_bc, r_salog, r_sdtb, r_sd, gr_snw, gr_n2, gr_nmem, gr_n3,
     gr_final, r_gcw, r_scw_x, r_scw_bc) = all_reduce_items("ar_grads", items)
    gr_galog, gr_gdtb = r_galog[:, 8:16], r_gdtb[:, 8:16]
    gr_salog, gr_sdtb = r_salog[:, :SSM_H], r_sdtb[:, :SSM_H]
    gr_sd = r_sd.reshape(SSM_H, SSM_P).sum(axis=1).reshape(1, SSM_H)
    gr_scb = jnp.concatenate([r_scb_x, r_scb_bc], axis=1)
    gr_gcw = lax.dynamic_slice(r_gcw, (0, s_me * 768), (4, 768))
    gr_scw = lax.dynamic_slice(jnp.concatenate([r_scw_x, r_scw_bc], axis=1), (0, s_me * 384), (4, 384))

    g_win = _unpad_win(g_win_p).reshape(D, 4, IN_COLS // 4).transpose(1, 0, 2)
    red_w, oth_w = reduce_scatter("rs_win", g_win, 256, sp)
    red_r, oth_r = reduce_scatter("rs_rest", grest, 256, sp)

    big = {"w_in": adam_halves("adam_win", w_in, m_w_in, v_w_in, red_w, oth_w, 256, 0, sp)}
    for n, w, m, v, blk0 in (("w_up", w_up, m_w_up, v_w_up, 0), ("w_down", w_down, m_w_down, v_w_down, 4),
                             ("w_out", w_out, m_w_out, v_w_out, 8), ("wq_mem", wq_mem, m_wq_mem, v_wq_mem, 10),
                             ("wk_mem", wk_mem, m_wk_mem, v_wk_mem, 11), ("wv_mem", wv_mem, m_wv_mem, v_wv_mem, 12),
                             ("wo_mem", wo_mem, m_wo_mem, v_wo_mem, 13)):
        big[n] = adam_halves("adam_" + n, w, m, v, red_r, oth_r, 256, blk0, sp)
    names_s = ["norm1_w", "gdn_conv_w", "gdn_a_log", "gdn_dt_bias", "gdn_norm_w", "ssm_conv_w", "ssm_conv_b",
               "ssm_a_log", "ssm_dt_bias", "ssm_d", "ssm_norm_w", "norm2_w", "mem_norm_w", "norm3_w", "final_norm_w"]
    w_s = [norm1_w, gdn_conv_w, gdn_a_log, gdn_dt_bias, gdn_norm_w, ssm_conv_w, ssm_conv_b, ssm_a_log, ssm_dt_bias,
           ssm_d, ssm_norm_w, norm2_w, mem_norm_w, norm3_w, final_norm_w]
    g_s = [gr_n1, gr_gcw, gr_galog, gr_gdtb, gr_gnw, gr_scw, gr_scb, gr_salog, gr_sdtb, gr_sd, gr_snw, gr_n2,
           gr_nmem, gr_n3, gr_final]
    m_s = [m_norm1_w, m_gdn_conv_w, m_gdn_a_log, m_gdn_dt_bias, m_gdn_norm_w, m_ssm_conv_w, m_ssm_conv_b, m_ssm_a_log,
           m_ssm_dt_bias, m_ssm_d, m_ssm_norm_w, m_norm2_w, m_mem_norm_w, m_norm3_w, m_final_norm_w]
    v_s = [v_norm1_w, v_gdn_conv_w, v_gdn_a_log, v_gdn_dt_bias, v_gdn_norm_w, v_ssm_conv_w, v_ssm_conv_b, v_ssm_a_log,
           v_ssm_dt_bias, v_ssm_d, v_ssm_norm_w, v_norm2_w, v_mem_norm_w, v_norm3_w, v_final_norm_w]
    shp_s = [w.shape for w in w_s]
    as2d = lambda a: a if a.ndim == 2 else a.reshape(1, -1)
    d_l, m_l, v_l = adam_small([as2d(a) for a in w_s], [as2d(a) for a in g_s], [as2d(a) for a in m_s],
                               [as2d(a) for a in v_s])

    grads, deltas, new_m, new_v = {}, {}, {}, {}
    for n, (gg, dd, mm_, vv_) in big.items():
        grads[n], deltas[n], new_m[n], new_v[n] = gg, dd, mm_, vv_
    for k, n in enumerate(names_s):
        grads[n] = g_s[k].reshape(shp_s[k])
        deltas[n], new_m[n], new_v[n] = (a[k].reshape(shp_s[k]) for a in (d_l, m_l, v_l))
    order = ["norm1_w", "w_in", "gdn_conv_w", "gdn_a_log", "gdn_dt_bias", "gdn_norm_w", "ssm_conv_w", "ssm_conv_b",
             "ssm_a_log", "ssm_dt_bias", "ssm_d", "ssm_norm_w", "w_out", "norm2_w", "mem_norm_w", "wq_mem", "wk_mem",
             "wv_mem", "wo_mem", "norm3_w", "w_up", "w_down", "final_norm_w"]
    return (loss, grad_x[None], *[grads[n] for n in order], *[deltas[n] for n in order],
            *[new_m[n] for n in order], *[new_v[n] for n in order])
```

```python
import numpy as np
import jax
import jax.numpy as jnp
from jax import lax
from jax.experimental import pallas as pl
from jax.experimental.pallas import tpu as pltpu

F32, BF16 = jnp.float32, jnp.bfloat16
MESH = pl.DeviceIdType.MESH
ANY = pl.BlockSpec(memory_space=pl.ANY)

EPS = 1e-6
D = 1024
GDN_H, GDN_DK, GDN_C = 8, 128, 64
SSM_H, SSM_P, SSM_N, SSM_L = 16, 64, 128, 128
MEM_H, MEM_DH = 4, 256
D_FF = 4096
IN_COLS = 6688
CB_QKV, CB_Z, CB_ZS, CB_XS, CB_BC, CB_BA, CB_DT = 0, 3, 4, 5, 12, 52, 53
VMEM_LIMIT = 56 * 1024 * 1024
D2D_CHUNKS = 8
ICI_CHUNKS = 4

ADAM_LR, ADAM_B1, ADAM_B2, ADAM_EPS, ADAM_WD, ADAM_STEP = 0.001, 0.9, 0.999, 1e-08, 0.01, 10


def _dg(a, b, ca, cb):
    return lax.dot_general(a, b, (((ca,), (cb,)), ((), ())), preferred_element_type=F32)


def _bf(x):
    return x.astype(BF16)


def mm(a, b):
    return _dg(_bf(a), _bf(b), 1, 0)


def mm_nt(a, b):
    return _dg(_bf(a), _bf(b), 1, 1)


def mm_tn(a, b):
    return _dg(_bf(a), _bf(b), 0, 0)


def mm_sel(a, sel):
    hi = a.astype(BF16)
    r1 = a - hi.astype(F32)
    mid = r1.astype(BF16)
    lo = (r1 - mid.astype(F32)).astype(BF16)
    s = sel.astype(BF16)
    return _dg(hi, s, 1, 0) + (_dg(mid, s, 1, 0) + _dg(lo, s, 1, 0))


def mm3(a, b):
    ah, bh = a.astype(BF16), b.astype(BF16)
    al, bl = (a - ah.astype(F32)).astype(BF16), (b - bh.astype(F32)).astype(BF16)
    return _dg(ah, bh, 1, 0) + (_dg(ah, bl, 1, 0) + _dg(al, bh, 1, 0))


def _iota(shape, dim):
    return lax.broadcasted_iota(jnp.int32, shape, dim)


def _chunk_cumsum(x, c):
    pos = _iota(x.shape, 0) & (c - 1)
    s = 1
    while s < c:
        x = x + jnp.where(pos >= s, pltpu.roll(x, s, 0), 0.0)
        s *= 2
    return x


def _chunk_revcumsum(x, c):
    n = x.shape[0]
    pos = _iota(x.shape, 0) & (c - 1)
    s = 1
    while s < c:
        x = x + jnp.where(pos < c - s, pltpu.roll(x, n - s, 0), 0.0)
        s *= 2
    return x


def _sig(x):
    return 1.0 / (1.0 + jnp.exp(-x))


def _softplus(x):
    return jnp.maximum(x, 0.0) + jnp.log(1.0 + jnp.exp(-jnp.abs(x)))


def _rows(v):
    return jnp.sum(v, axis=0, keepdims=True)


def _lanes(v):
    return jnp.sum(v, axis=1, keepdims=True)


def _sum_all(v):
    return _rows(_lanes(v))


def _cparams(sem):
    return pltpu.CompilerParams(dimension_semantics=sem, vmem_limit_bytes=VMEM_LIMIT)


def rowwise(fn, name, T, tb, row_ins, full_ins, row_outs, acc_outs=(), sp=None):
    nblk = T // tb
    assert nblk * tb == T
    has_sp = sp is not None

    def imap(f):
        return (lambda i, s: f(i, s)) if has_sp else (lambda i: f(i, None))

    in_specs, args = [], []
    for arr, w, cb, halo, off in row_ins:
        if halo == "col":
            in_specs.append(pl.BlockSpec((w, tb), imap(lambda i, s: (0, i))))
            args.append(arr)
            continue
        rowf = off if callable(off) else (lambda i, s, off=off: i + off)
        in_specs.append(pl.BlockSpec((tb, w), imap(lambda i, s, cb=cb, rowf=rowf: (rowf(i, s), cb))))
        args.append(arr)
        if halo == "prev":
            r = tb // 8
            in_specs.append(pl.BlockSpec((8, w), imap(lambda i, s, cb=cb, r=r: (jnp.maximum(i * r - 1, 0), cb))))
            args.append(arr)
        elif halo == "next":
            r, last = tb // 8, T // 8 - 1
            in_specs.append(pl.BlockSpec((8, w), imap(lambda i, s, cb=cb, r=r, last=last:
                                                      (jnp.minimum((i + 1) * r, last), cb))))
            args.append(arr)
    for arr in full_ins:
        in_specs.append(pl.BlockSpec(arr.shape, imap(lambda i, s, nd=arr.ndim: (0,) * nd)))
        args.append(arr)
    n_in, n_ro = len(args), len(row_outs)
    out_shape, out_specs, aliases = [], [], {}
    for k, (w, dt, *dest) in enumerate(row_outs):
        if dest:
            buf, cb = dest
            out_shape.append(jax.ShapeDtypeStruct(buf.shape, buf.dtype))
            out_specs.append(pl.BlockSpec((tb, w), imap(lambda i, s, cb=cb: (i, cb))))
            if not isinstance(buf, jax.ShapeDtypeStruct):
                aliases[len(args) + int(has_sp)] = k
                in_specs.append(ANY)
                args.append(buf)
        elif w < 0:
            out_shape.append(jax.ShapeDtypeStruct((-w, T), dt))
            out_specs.append(pl.BlockSpec((-w, tb), imap(lambda i, s: (0, i))))
        else:
            out_shape.append(jax.ShapeDtypeStruct((T, w), dt))
            out_specs.append(pl.BlockSpec((tb, w), imap(lambda i, s: (i, 0))))
    for shp in acc_outs:
        out_shape.append(jax.ShapeDtypeStruct(shp, F32))
        out_specs.append(pl.BlockSpec(shp, imap(lambda i, s, nd=len(shp): (0,) * nd)))

    def body(*refs):
        i = pl.program_id(0)
        if has_sp:
            sp_ref, refs = refs[0], refs[1:]
            vals = fn(i, nblk, sp_ref, *[r[...] for r in refs[:n_in]])
        else:
            vals = fn(i, nblk, *[r[...] for r in refs[:n_in]])
        outs = refs[n_in + len(aliases):]
        for ref, val in zip(outs[:n_ro], vals[:n_ro]):
            ref[...] = val.astype(ref.dtype)
        for ref, val in zip(outs[n_ro:], vals[n_ro:]):
            @pl.when(i == 0)
            def _(ref=ref, val=val):
                ref[...] = val

            @pl.when(i > 0)
            def _(ref=ref, val=val):
                ref[...] += val

    cparams = _cparams(("arbitrary",) if acc_outs else ("parallel",))
    if has_sp:
        return pl.pallas_call(
            body, name=name, out_shape=out_shape, compiler_params=cparams, input_output_aliases=aliases,
            grid_spec=pltpu.PrefetchScalarGridSpec(num_scalar_prefetch=1, grid=(nblk,), in_specs=in_specs,
                                                   out_specs=out_specs),
        )(sp, *args)
    return pl.pallas_call(
        body, name=name, grid=(nblk,), in_specs=in_specs, out_specs=out_specs, out_shape=out_shape,
        compiler_params=cparams, input_output_aliases=aliases,
    )(*args)


def R(arr, w=None, cb=0, halo=None, off=0):
    return (arr, arr.shape[1] if w is None else w, cb, halo, off)


def RC(arr):
    return (arr, arr.shape[0], 0, "col", 0)


def matmul(name, a, b, form, tm, tn, tk, out_dtypes, epi=None, extras=(), rows=(), into=None, n_acc=0):
    if form == "nn":
        (M, K), N = a.shape, b.shape[1]
    elif form == "nt":
        (M, K), N = a.shape, b.shape[0]
    else:
        (K, M), N = a.shape, b.shape[1]
    tm, tn, tk = min(tm, M), min(tn, N), min(tk, K)
    assert M % tm == 0 and N % tn == 0 and K % tk == 0, (name, M, N, K, tm, tn, tk)
    if form == "nn":
        a_spec = pl.BlockSpec((tm, tk), lambda i, j, k: (i, k))
        b_spec = pl.BlockSpec((tk, tn), lambda i, j, k: (k, j))
        ca, cb = 1, 0
    elif form == "nt":
        a_spec = pl.BlockSpec((tm, tk), lambda i, j, k: (i, k))
        b_spec = pl.BlockSpec((tn, tk), lambda i, j, k: (j, k))
        ca, cb = 1, 1
    else:
        a_spec = pl.BlockSpec((tk, tm), lambda i, j, k: (k, i))
        b_spec = pl.BlockSpec((tk, tn), lambda i, j, k: (k, j))
        ca, cb = 0, 0
    nk, ne, no = K // tk, len(extras) + len(rows), len(out_dtypes)
    if epi is None:
        epi = lambda acc: (acc,)

    assert n_acc == 0 or tn == N

    def body(a_ref, b_ref, *rest):
        ex, outs, accs, acc = rest[:ne], rest[ne:ne + no], rest[ne + no:ne + no + n_acc], rest[ne + no + n_acc]
        i, k = pl.program_id(0), pl.program_id(2)

        @pl.when(k == 0)
        def _():
            acc[...] = jnp.zeros_like(acc)

        acc[...] += _dg(_bf(a_ref[...]), _bf(b_ref[...]), ca, cb)

        @pl.when(k == nk - 1)
        def _():
            vals = epi(acc[...], *[e[...] for e in ex])
            for r, v in zip(outs, vals[:no]):
                r[...] = v.astype(r.dtype).reshape(r.shape)
            for r, v in zip(accs, vals[no:]):
                @pl.when(i == 0)
                def _(r=r, v=v):
                    r[...] = v

                @pl.when(i > 0)
                def _(r=r, v=v):
                    r[...] += v

    mn = pl.BlockSpec((tm, tn), lambda i, j, k: (i, j))
    rw = pl.BlockSpec((1, tn), lambda i, j, k: (0, j))
    if into is not None:
        buf, blk, bmap = into
        assert ne == 0 and no == 1
        aliased = not isinstance(buf, jax.ShapeDtypeStruct)

        def body_into(a_ref, b_ref, *rest):
            body(a_ref, b_ref, *rest[-2:])

        return pl.pallas_call(
            body_into, name=name, grid=(M // tm, N // tn, nk),
            in_specs=[a_spec, b_spec] + ([ANY] if aliased else []), out_specs=pl.BlockSpec(blk, bmap),
            out_shape=jax.ShapeDtypeStruct(buf.shape, buf.dtype),
            scratch_shapes=[pltpu.VMEM((tm, tn), F32)],
            input_output_aliases={2: 0} if aliased else {},
            compiler_params=_cparams(("parallel", "parallel", "arbitrary")),
        )(a, b, *([buf] if aliased else []))
    return pl.pallas_call(
        body, name=name, grid=(M // tm, N // tn, nk),
        in_specs=[a_spec, b_spec] + [mn] * len(extras) + [rw] * len(rows), out_specs=[mn] * no + [rw] * n_acc,
        out_shape=[jax.ShapeDtypeStruct((M, N), dt) for dt in out_dtypes] + [jax.ShapeDtypeStruct((1, N), F32)] * n_acc,
        scratch_shapes=[pltpu.VMEM((tm, tn), F32)],
        compiler_params=_cparams(("arbitrary",) * 3 if n_acc else ("parallel", "parallel", "arbitrary")),
    )(a, b, *extras, *rows)


def _epi_res(acc, res):
    return (res + acc,)


def _epi_rms_bwd(acc, x, dres, w):
    return rms_bwd_fn(0, 0, x, acc, dres, w)


def _epi_rms_bwd1(acc, x, dres, w):
    dx, _, gw = rms_bwd_fn(0, 0, x, acc, dres, w)
    return dx, gw


def _epi_final(acc, res, tgt, w):
    return final_fn(0, 0, res + acc, tgt, w)


def _epi_res_rms(acc, res, w):
    x = res + acc
    return (x, x * lax.rsqrt(jnp.mean(x * x, axis=-1, keepdims=True) + EPS) * w)


def _epi_relu2(acc):
    u = jnp.maximum(acc, 0.0)
    return (u, u * u)


def _epi_dup(acc, u):
    return (acc * 2.0 * u.astype(F32),)


def _conv(x, halo, w, i):
    halo = jnp.where(i == 0, 0.0, halo)
    xt = jnp.concatenate([halo, x], axis=0)
    shifted = [pltpu.roll(xt, 3 - k, 0)[8:, :] for k in range(3)] + [x]
    y = shifted[3] * w[3:4, :]
    for k in range(3):
        y = y + shifted[k] * w[k:k + 1, :]
    return y, shifted


def _l2n(x, scale):
    outs = []
    for h in range(x.shape[1] // 128):
        xh = x[:, 128 * h:128 * h + 128]
        outs.append(xh * (lax.rsqrt(jnp.sum(xh * xh, axis=-1, keepdims=True) + EPS) * scale))
    return jnp.concatenate(outs, axis=1)


def _l2n_bwd(x, dy, scale):
    outs = []
    for h in range(x.shape[1] // 128):
        xh, dh = x[:, 128 * h:128 * h + 128], dy[:, 128 * h:128 * h + 128] * scale
        r = lax.rsqrt(jnp.sum(xh * xh, axis=-1, keepdims=True) + EPS)
        outs.append(r * dh - xh * (r * r * r) * jnp.sum(xh * dh, axis=-1, keepdims=True))
    return jnp.concatenate(outs, axis=1)


def rms_fwd_fn(i, n, x, w):
    r = lax.rsqrt(jnp.mean(x * x, axis=-1, keepdims=True) + EPS)
    return (x * r * w,)


def rms_bwd_fn(i, n, x, dh, dres, w):
    r = lax.rsqrt(jnp.mean(x * x, axis=-1, keepdims=True) + EPS)
    g = dh * w
    dx = dres + r * g - x * (r * r * r) * jnp.mean(x * g, axis=-1, keepdims=True)
    return dx, dx, _rows(dh * x * r)


def rms_bwd_w_fn(i, n, x, dh, w):
    r = lax.rsqrt(jnp.mean(x * x, axis=-1, keepdims=True) + EPS)
    return (_rows(dh * x * r),)


def final_fn(i, n, x, tgt, w):
    r = lax.rsqrt(jnp.mean(x * x, axis=-1, keepdims=True) + EPS)
    xn = x * r
    e = xn * w - tgt
    dy = e * (1.0 / D)
    g = dy * w
    dx = r * g - x * (r * r * r) * jnp.mean(x * g, axis=-1, keepdims=True)
    return dx, dx, _rows(e * e), _rows(dy * xn)


def _gdn_gates(ba, alog_c, dtb_c):
    col = _iota(ba.shape, 1)
    amask = (col >= 8) & (col < 16)
    beta = jnp.where(col < 8, _sig(ba), 0.0)
    z = ba + dtb_c
    ea_ = jnp.exp(alog_c)
    return beta, z, ea_, jnp.where(amask, -ea_ * _softplus(z), 0.0), amask


def gdn_prep_fn(i, n, qkv, halo, ba, cw, alog_c, dtb_c, eb, ea):
    yc, _ = _conv(qkv, halo, cw, i)
    act = yc * _sig(yc)
    qn = _l2n(act[:, :D], GDN_DK ** -0.5)
    kn = _l2n(act[:, D:2 * D], 1.0)
    beta, _, _, g, _ = _gdn_gates(ba, alog_c, dtb_c)
    gcs = _chunk_cumsum(g, GDN_C)
    return qn, kn, act[:, 2 * D:], mm_sel(gcs, ea), mm_sel(beta, eb), jnp.transpose(gcs)[8:16, :]


def gdn_prep_bwd_fn(i, n, qkv, halo, ba, dqn, dkn, dv, dgcs_x, dbeta_x, dgcs_t, cw, alog_c, dtb_c, pb, pa):
    yc, shifted = _conv(qkv, halo, cw, i)
    sg = _sig(yc)
    act = yc * sg
    dq = _l2n_bwd(act[:, :D], dqn, GDN_DK ** -0.5)
    dk = _l2n_bwd(act[:, D:2 * D], dkn, 1.0)
    dyc = jnp.concatenate([dq, dk, dv], axis=1) * (sg * (1.0 + yc * (1.0 - sg)))
    dws = [_rows(dyc * shifted[k]) for k in range(4)]
    beta, z, ea_, g, amask = _gdn_gates(ba, alog_c, dtb_c)
    tbn = ba.shape[0]
    rowpart = jnp.transpose(jnp.concatenate([jnp.zeros((8, tbn), F32), dgcs_t, jnp.zeros((112, tbn), F32)], axis=0))
    dg = _chunk_revcumsum(mm_sel(dgcs_x, pa) - rowpart, GDN_C)
    draw = jnp.where(amask, dg * (-ea_) * _sig(z), 0.0)
    dba = draw + mm_sel(dbeta_x, pb) * beta * (1.0 - beta)
    return (dyc, dba, dws[0], dws[1], dws[2], dws[3], _rows(dg * g), _rows(draw))


def conv_bwd_fn(i, n, dyc, halo, w):
    halo = jnp.where(i == n - 1, 0.0, halo)
    tb = dyc.shape[0]
    xt = jnp.concatenate([dyc, halo], axis=0)
    dx = dyc * w[3:4, :]
    for k in range(3):
        dx = dx + pltpu.roll(xt, tb + 8 - (3 - k), 0)[:tb, :] * w[k:k + 1, :]
    return (dx,)


def gdn_post_fn(i, n, o, z, w):
    outs = []
    for h in range(GDN_H):
        oh, zh = o[:, 128 * h:128 * h + 128], z[:, 128 * h:128 * h + 128]
        r = lax.rsqrt(jnp.mean(oh * oh, axis=-1, keepdims=True) + EPS)
        outs.append(oh * r * w * (zh * _sig(zh)))
    return (jnp.concatenate(outs, axis=1),)


def gdn_post_bwd_fn(i, n, o, z, doa, w):
    dos, dzs, dw = [], [], None
    for h in range(GDN_H):
        sl = slice(128 * h, 128 * h + 128)
        oh, zh, dh = o[:, sl], z[:, sl], doa[:, sl]
        r = lax.rsqrt(jnp.mean(oh * oh, axis=-1, keepdims=True) + EPS)
        s = _sig(zh)
        dn = dh * (zh * s)
        dzs.append(dh * (oh * r * w) * (s * (1.0 + zh * (1.0 - s))))
        t = _rows(dn * oh * r)
        dw = t if dw is None else dw + t
        g = dn * w
        dos.append(r * g - oh * (r * r * r) * jnp.mean(oh * g, axis=-1, keepdims=True))
    return jnp.concatenate(dos, axis=1), jnp.concatenate(dzs, axis=1), dw


def _ssd_gates(dtblk, alog_c, dtb_c):
    hmask = _iota(dtblk.shape, 1) < SSM_H
    z = dtblk + dtb_c
    return jnp.where(hmask, _softplus(z), 0.0), -jnp.exp(alog_c), z, hmask


def ssd_prep_fn(i, n, xp, hx, bcp, hbc, dtblk, cwx, cwbc, cbx, cbbc, alog_c, dtb_c, e16):
    yx, _ = _conv(xp, hx, cwx, i)
    yx = yx + cbx
    ybc, _ = _conv(bcp, hbc, cwbc, i)
    ybc = ybc + cbbc
    dt, a_neg, _, _ = _ssd_gates(dtblk, alog_c, dtb_c)
    acs = _chunk_cumsum(dt * a_neg, SSM_L)
    return (yx * _sig(yx), ybc * _sig(ybc), mm_sel(dt, e16), mm_sel(acs, e16), jnp.transpose(acs)[0:SSM_H, :])


def ssd_prep_bwd_fn(i, n, xp, hx, bcp, hbc, dtblk, dxs_a, dxs_b, db, dc, dgate, dacs_t, cwx, cwbc, cbx, cbbc, alog_c, dtb_c):
    dbc = jnp.concatenate([db, dc], axis=1)
    yx, shx = _conv(xp, hx, cwx, i)
    yx = yx + cbx
    ybc, shbc = _conv(bcp, hbc, cwbc, i)
    ybc = ybc + cbbc
    sx, sbc = _sig(yx), _sig(ybc)
    dyx = (dxs_a + dxs_b) * (sx * (1.0 + yx * (1.0 - sx)))
    dybc = dbc * (sbc * (1.0 + ybc * (1.0 - sbc)))
    dwx = [_rows(dyx * shx[k]) for k in range(4)]
    dwbc = [_rows(dybc * shbc[k]) for k in range(4)]
    dt, a_neg, z, hmask = _ssd_gates(dtblk, alog_c, dtb_c)
    g0, g1 = dgate[:, :128], dgate[:, 128:]
    col = _iota(g0.shape, 1)
    lo, mid = col < 8, (col >= 8) & (col < 16)
    dacs_col = jnp.where(lo, g0, 0.0) + pltpu.roll(jnp.where(lo, g1, 0.0), 8, 1)
    ddt_dir = pltpu.roll(jnp.where(mid, g0, 0.0), 120, 1) + jnp.where(mid, g1, 0.0)
    tbn = dtblk.shape[0]
    rowpart = jnp.transpose(jnp.concatenate([dacs_t, jnp.zeros((128 - SSM_H, tbn), F32)], axis=0))
    da = _chunk_revcumsum(dacs_col - rowpart, SSM_L)
    draw = jnp.where(hmask, (ddt_dir + da * a_neg) * _sig(z), 0.0)
    return (dyx, dybc, draw, *dwx, *dwbc, _rows(dyx), _rows(dybc), _rows(da * dt * a_neg), _rows(draw))


def _ssd_gate(y, xs, zs, d_x):
    y2 = y + xs * d_x
    s = _sig(zs)
    return y2, s, y2 * (zs * s)


def ssd_post_fn(i, n, y, xs, zs, d_x, nw):
    _, _, yg = _ssd_gate(y, xs, zs, d_x)
    outs = []
    for g in range(2):
        v = yg[:, 512 * g:512 * g + 512]
        outs.append(v * lax.rsqrt(jnp.mean(v * v, axis=-1, keepdims=True) + EPS))
    return (jnp.concatenate(outs, axis=1) * nw,)


def ssd_post_bwd_fn(i, n, y, xs, zs, dob, d_x, nw):
    y2, s, yg = _ssd_gate(y, xs, zs, d_x)
    gfull = dob * nw
    dygs, dnw = [], []
    for g in range(2):
        sl = slice(512 * g, 512 * g + 512)
        v, gg = yg[:, sl], gfull[:, sl]
        r = lax.rsqrt(jnp.mean(v * v, axis=-1, keepdims=True) + EPS)
        dygs.append(r * gg - v * (r * r * r) * jnp.mean(v * gg, axis=-1, keepdims=True))
        dnw.append(_rows(dob[:, sl] * v * r))
    dyg = jnp.concatenate(dygs, axis=1)
    dy2 = dyg * (zs * s)
    dzs = dyg * y2 * (s * (1.0 + zs * (1.0 - s)))
    return dy2, dy2 * d_x, dzs, jnp.concatenate(dnw, axis=1), _rows(dy2 * xs)


def attn_fn(i, n, q, k, v):
    outs = []
    for h in range(MEM_H):
        sl = slice(MEM_DH * h, MEM_DH * h + MEM_DH)
        s = mm_nt(q[:, sl], k[:, sl]) * (MEM_DH ** -0.5)
        p = jnp.exp(s - jnp.max(s, axis=-1, keepdims=True))
        p = p / jnp.sum(p, axis=-1, keepdims=True)
        outs.append(mm(p, v[:, sl]))
    return (jnp.concatenate(outs, axis=1),)


def attn_bwd_fn(i, n, q, do, k, v):
    dqs, dks, dvs = [], [], []
    for h in range(MEM_H):
        sl = slice(MEM_DH * h, MEM_DH * h + MEM_DH)
        s = mm_nt(q[:, sl], k[:, sl]) * (MEM_DH ** -0.5)
        p = jnp.exp(s - jnp.max(s, axis=-1, keepdims=True))
        p = p / jnp.sum(p, axis=-1, keepdims=True)
        dvs.append(mm_tn(p, do[:, sl]))
        dp = mm_nt(do[:, sl], v[:, sl])
        ds = p * (dp - jnp.sum(dp * p, axis=-1, keepdims=True)) * (MEM_DH ** -0.5)
        dqs.append(mm(ds, k[:, sl]))
        dks.append(mm_tn(ds, q[:, sl]))
    return jnp.concatenate(dqs, axis=1), jnp.concatenate(dks, axis=1), jnp.concatenate(dvs, axis=1)


def add2_fn(i, n, sp, a, b):
    return (a + b,)


def sum4_fn(i, n, sp, a, b, c, d):
    return (((a.astype(F32) + b.astype(F32)) + c.astype(F32)) + d.astype(F32),)


def _adamw(w, g, m, v):
    m = ADAM_B1 * m + (1.0 - ADAM_B1) * g
    v = ADAM_B2 * v + (1.0 - ADAM_B2) * (g * g)
    m_hat = m / (1.0 - ADAM_B1 ** ADAM_STEP)
    v_hat = v / (1.0 - ADAM_B2 ** ADAM_STEP)
    delta = -ADAM_LR * (m_hat / (jnp.sqrt(v_hat) + ADAM_EPS) + ADAM_WD * w)
    return delta, m, v


def _gdn_stage1(q, k, v, gcs, grow, bb):
    C = GDN_C
    row, col = _iota((C, C), 0), _iota((C, C), 1)
    incl, strict = row >= col, row > col
    dmat = jnp.where(incl, jnp.exp(jnp.minimum(gcs[:, :C] - grow, 0.0)), 0.0)
    gam = jnp.exp(gcs)
    gl = gcs[C - 1:C, :]
    kb, vb = k * bb, v * bb
    kg = kb * gam
    lmat = jnp.where(strict, mm_nt(kb, k) * dmat, 0.0)
    pmat = jnp.where(incl, mm_nt(q, k) * dmat, 0.0)
    return dict(q=q, k=k, v=v, bb=bb, incl=incl, strict=strict, dmat=dmat, gam=gam, kb=kb, vb=vb, kg=kg,
                lmat=lmat, pmat=pmat, qd=q * gam, kdec=jnp.exp(gl - gcs), cd=jnp.exp(gl))


def _gdn_inverse(lmats):
    C = GDN_C
    eye = (_iota((C, C), 0) == _iota((C, C), 1)).astype(F32)
    xs = [-l for l in lmats]
    ts = [eye + x for x in xs]
    for _ in range(5):
        xs = [mm(x, x) for x in xs]
        ts = [t + mm(t, x) for t, x in zip(ts, xs)]
    res = [eye - mm3(eye + l, t) for l, t in zip(lmats, ts)]
    return [t + mm(t, r) for t, r in zip(ts, res)]


def gdn_fwd(qn, kn, v, gcs_x, beta_x, gcs_t, tb, gh):
    T = qn.shape[0]
    nb, ncb, nc, C = T // tb, tb // GDN_C, T // GDN_C, GDN_C
    idx = [(hh, c) for hh in range(gh) for c in range(ncb)]

    def body(q_ref, k_ref, v_ref, g_ref, b_ref, gt_ref, o_ref, st_ref, ti_ref, s_scr):
        @pl.when(pl.program_id(1) == 0)
        def _():
            s_scr[...] = jnp.zeros_like(s_scr)

        grows = [gt_ref[hh] for hh in range(gh)]
        at = lambda hh, c: (slice(C * c, C * (c + 1)), slice(128 * hh, 128 * hh + 128))
        st1 = []
        for hh, c in idx:
            sl, ln = at(hh, c)
            st1.append(_gdn_stage1(q_ref[sl, ln], k_ref[sl, ln], v_ref[sl, ln], g_ref[sl, ln], grows[hh][:, sl],
                                   b_ref[sl, ln]))
        tinvs = _gdn_inverse([s["lmat"] for s in st1])
        us = [mm(t, s["vb"]) for t, s in zip(tinvs, st1)]
        ws = [mm(t, s["kg"]) for t, s in zip(tinvs, st1)]
        kds = [s["k"] * s["kdec"] for s in st1]
        ms = [mm_tn(kd, w) for kd, w in zip(kds, ws)]
        bs = [mm_tn(kd, u) for kd, u in zip(kds, us)]
        gs = [s["qd"] - mm(s["pmat"], w) for s, w in zip(st1, ws)]
        pus = [mm(s["pmat"], u) for s, u in zip(st1, us)]
        ss = [s_scr[hh] for hh in range(gh)]
        for c in range(ncb):
            for hh in range(gh):
                n, (sl, ln) = hh * ncb + c, at(hh, c)
                ti_ref[hh, sl, :] = tinvs[n]
                st_ref[hh, c] = ss[hh]
                o_ref[sl, ln] = mm(gs[n], ss[hh]) + pus[n]
                ss[hh] = st1[n]["cd"] * ss[hh] - mm(ms[n], ss[hh]) + bs[n]
        for hh in range(gh):
            s_scr[hh] = ss[hh]

    blk = pl.BlockSpec((tb, 128 * gh), lambda h, i: (i, h))
    return pl.pallas_call(
        body, name="gdn_fwd", grid=(GDN_H // gh, nb),
        in_specs=[blk] * 5 + [pl.BlockSpec((gh, 1, tb), lambda h, i: (h, 0, i))],
        out_specs=[blk, pl.BlockSpec((gh, ncb, 128, 128), lambda h, i: (h, i, 0, 0)),
                   pl.BlockSpec((gh, tb, C), lambda h, i: (h, i, 0))],
        out_shape=[jax.ShapeDtypeStruct((T, D), F32), jax.ShapeDtypeStruct((GDN_H, nc, 128, 128), F32),
                   jax.ShapeDtypeStruct((GDN_H, T, C), F32)],
        scratch_shapes=[pltpu.VMEM((gh, 128, 128), F32)],
        compiler_params=_cparams(("parallel", "arbitrary")),
    )(qn, kn, v, gcs_x, beta_x, gcs_t)


def gdn_bwd(qn, kn, v, gcs_x, beta_x, gcs_t, do, states, tinv, tb, gh):
    T = qn.shape[0]
    nb, ncb, C = T // tb, tb // GDN_C, GDN_C

    def body(q_ref, k_ref, v_ref, g_ref, b_ref, gt_ref, do_ref, st_ref, ti_ref,
             dq_ref, dk_ref, dv_ref, dgc_ref, db_ref, dgr_ref, ds_scr):
        @pl.when(pl.program_id(1) == 0)
        def _():
            ds_scr[...] = jnp.zeros_like(ds_scr)

        grows = [gt_ref[hh] for hh in range(gh)]
        at = lambda hh, c: (slice(C * c, C * (c + 1)), slice(128 * hh, 128 * hh + 128))
        lastrow = _iota((C, 1), 0) == C - 1
        idx = [(hh, c) for hh in range(gh) for c in range(ncb)]
        P = []
        for hh, c in idx:
            sl, ln = at(hh, c)
            lc = _gdn_stage1(q_ref[sl, ln], k_ref[sl, ln], v_ref[sl, ln], g_ref[sl, ln], grows[hh][:, sl],
                             b_ref[sl, ln])
            lc.update(tinv=ti_ref[hh, sl, :], s=st_ref[hh, c], do=do_ref[sl, ln], kd=lc["k"] * lc["kdec"])
            P.append(lc)
        for l, u, w in zip(P, [mm(l["tinv"], l["vb"]) for l in P], [mm(l["tinv"], l["kg"]) for l in P]):
            l.update(u=u, w=w)
        for l, x in zip(P, [mm(l["w"], l["s"]) for l in P]):
            l["vn"] = l["u"] - x
        for l, a, b, c_, d in zip(P, [mm_nt(l["do"], l["s"]) for l in P], [mm_nt(l["do"], l["vn"]) for l in P],
                                  [mm_tn(l["qd"], l["do"]) for l in P], [mm_tn(l["pmat"], l["do"]) for l in P]):
            l.update(dqd=a, dp=jnp.where(l["incl"], b, 0.0), ds_q=c_, dvn_p=d)
        pre = dict(zip(idx, P))
        rows = {}
        hs = range(gh)
        ds = [ds_scr[hh] for hh in hs]
        for c in reversed(range(ncb)):
            L = [pre[hh, c] for hh in hs]
            dvn = [l["dvn_p"] + mm(l["kd"], d) for l, d in zip(L, ds)]
            dkd = [mm_nt(l["vn"], d) for l, d in zip(L, ds)]
            dcd = [_sum_all(l["s"] * d) for l, d in zip(L, ds)]
            ds = [l["ds_q"] + l["cd"] * d - mm_tn(l["w"], x) for l, d, x in zip(L, ds, dvn)]
            dw = [-mm_nt(x, l["s"]) for l, x in zip(L, dvn)]
            dvb = [mm_tn(l["tinv"], x) for l, x in zip(L, dvn)]
            dkg = [mm_tn(l["tinv"], x) for l, x in zip(L, dw)]
            da = [-jnp.where(l["strict"], mm_nt(a, l["u"]) + mm_nt(b, l["w"]), 0.0) for l, a, b in zip(L, dvb, dkg)]
            dm = [a * l["dmat"] for l, a in zip(L, da)]
            dn = [l["dp"] * l["dmat"] for l in L]
            dkb = [mm(a, l["k"]) for l, a in zip(L, dm)]
            dq = [mm(a, l["k"]) + l["gam"] * l["dqd"] for l, a in zip(L, dn)]
            dk = [mm_tn(a, l["kb"]) + mm_tn(b, l["q"]) for l, a, b in zip(L, dm, dn)]
            for hh in hs:
                sl, ln = at(hh, c)
                l = L[hh]
                e = da[hh] * l["lmat"] + l["dp"] * l["pmat"]
                t_kd = _lanes(dkd[hh] * l["kd"])
                dgl = _sum_all(t_kd) + dcd[hh] * l["cd"][:, :1]
                dgcs = (_lanes(e) + _lanes(l["dqd"] * l["qd"]) - t_kd + _lanes(dkg[hh] * l["kg"])
                        + jnp.where(lastrow, dgl, 0.0))
                rows[hh, c] = _rows(e)
                dq_ref[sl, ln] = dq[hh]
                dk_ref[sl, ln] = (dk[hh] + l["kdec"] * dkd[hh] + l["bb"] * l["gam"] * dkg[hh] + l["bb"] * dkb[hh])
                dv_ref[sl, ln] = l["bb"] * dvb[hh]
                dbeta = _lanes(dkg[hh] * l["gam"] * l["k"]) + _lanes(dvb[hh] * l["v"]) + _lanes(dkb[hh] * l["k"])
                db_ref[sl, ln] = jnp.broadcast_to(dbeta, (C, 128))
                dgc_ref[sl, ln] = jnp.broadcast_to(dgcs, (C, 128))
        for hh in hs:
            ds_scr[hh] = ds[hh]
            dgr_ref[hh] = jnp.concatenate([rows[hh, c] for c in range(ncb)], axis=1)

    blk = pl.BlockSpec((tb, 128 * gh), lambda h, i: (nb - 1 - i, h))
    rowspec = pl.BlockSpec((gh, 1, tb), lambda h, i: (h, 0, nb - 1 - i))
    return pl.pallas_call(
        body, name="gdn_bwd", grid=(GDN_H // gh, nb),
        in_specs=[blk] * 5 + [rowspec, blk,
                              pl.BlockSpec((gh, ncb, 128, 128), lambda h, i: (h, nb - 1 - i, 0, 0)),
                              pl.BlockSpec((gh, tb, C), lambda h, i: (h, nb - 1 - i, 0))],
        out_specs=[blk] * 5 + [rowspec],
        out_shape=[jax.ShapeDtypeStruct((T, D), F32)] * 5 + [jax.ShapeDtypeStruct((GDN_H, 1, T), F32)],
        scratch_shapes=[pltpu.VMEM((gh, 128, 128), F32)],
        compiler_params=_cparams(("parallel", "arbitrary")),
    )(qn, kn, v, gcs_x, beta_x, gcs_t, do, states, tinv)


def _ssd_pair(x2, dt2, acs2):
    last = acs2[SSM_L - 1:SSM_L, :]
    return jnp.exp(acs2), jnp.exp(last - acs2), x2 * dt2


def _ssd_head(hh, acs2, arow, dec2, cbm, bm, incl, col):
    lmask = (col >= 64 * hh) & (col < 64 * hh + 64)
    sg = jnp.where(incl, jnp.exp(jnp.minimum(acs2[:, 64 * hh:64 * hh + 1] - arow, 0.0)), 0.0)
    dec_col = dec2[:, 64 * hh:64 * hh + 1]
    return lmask, sg, sg * cbm, dec_col, bm * dec_col


def ssd_fwd(xs, bc, dt_x, acs_x, acs_t):
    T = xs.shape[0]
    nc, L = T // SSM_L, SSM_L

    def body(x_ref, b_ref, c_ref, dt_ref, ac_ref, at_ref, y_ref, hst_ref, h_scr):
        @pl.when(pl.program_id(1) == 0)
        def _():
            h_scr[...] = jnp.zeros_like(h_scr)

        bm, cm = b_ref[...], c_ref[...]
        cbm = mm_nt(cm, bm)
        row, col = _iota((L, L), 0), _iota((L, L), 1)
        incl = row >= col
        P, H = [], []
        for pr in range(4):
            sl = slice(128 * pr, 128 * pr + 128)
            acs2 = ac_ref[:, sl]
            lam2, dec2, xd2 = _ssd_pair(x_ref[:, sl], dt_ref[:, sl], acs2)
            P.append(dict(sl=sl, lam2=lam2, xd2=xd2, hprev=h_scr[pr]))
            for hh in range(2):
                lmask, _, mmat, _, bd = _ssd_head(hh, acs2, at_ref[2 * pr + hh], dec2, cbm, bm, incl, col)
                H.append(dict(mmat=mmat, bd=bd, xdh=jnp.where(lmask, xd2, 0.0), xd2=xd2))
        ys = [mm(h["mmat"], h["xdh"]) for h in H]
        sts = [mm_tn(h["xd2"], h["bd"]) for h in H]
        zs = [mm_nt(cm, p["hprev"]) for p in P]
        for pr, p in enumerate(P):
            hst_ref[pr] = p["hprev"]
            y_ref[:, p["sl"]] = ys[2 * pr] + ys[2 * pr + 1] + p["lam2"] * zs[pr]
            lam_rows = jnp.where(row < 64, p["lam2"][L - 1:L, 0:1], p["lam2"][L - 1:L, 64:65])
            h_scr[pr] = lam_rows * p["hprev"] + jnp.where(row < 64, sts[2 * pr], sts[2 * pr + 1])

    return pl.pallas_call(
        body, name="ssd_fwd", grid=(2, nc),
        in_specs=[pl.BlockSpec((L, 512), lambda g, c: (c, g)),
                  pl.BlockSpec((L, 128), lambda g, c: (c, g)),
                  pl.BlockSpec((L, 128), lambda g, c: (c, 2 + g)),
                  pl.BlockSpec((L, 512), lambda g, c: (c, g)),
                  pl.BlockSpec((L, 512), lambda g, c: (c, g)),
                  pl.BlockSpec((8, 1, L), lambda g, c: (g, 0, c))],
        out_specs=[pl.BlockSpec((L, 512), lambda g, c: (c, g)),
                   pl.BlockSpec((None, None, 4, 128, 128), lambda g, c: (g, c, 0, 0, 0))],
        out_shape=[jax.ShapeDtypeStruct((T, D), F32), jax.ShapeDtypeStruct((2, nc, 4, 128, 128), F32)],
        scratch_shapes=[pltpu.VMEM((4, 128, 128), F32)],
        compiler_params=_cparams(("parallel", "arbitrary")),
    )(xs, bc, bc, dt_x, acs_x, acs_t)


def ssd_bwd(xs, bc, dt_x, acs_x, acs_t, dy, hstates):
    T = xs.shape[0]
    nc, L = T // SSM_L, SSM_L

    def body(x_ref, b_ref, c_ref, dt_ref, ac_ref, at_ref, dy_ref, hst_ref,
             dx_ref, db_ref, dc_ref, dgate_ref, dar_ref, dh_scr):
        @pl.when(pl.program_id(1) == 0)
        def _():
            dh_scr[...] = jnp.zeros_like(dh_scr)

        bm, cm = b_ref[...], c_ref[...]
        cbm = mm_nt(cm, bm)
        row, col = _iota((L, L), 0), _iota((L, L), 1)
        rowc = _iota((L, 1), 0)
        incl = row >= col
        prs = range(4)
        P = []
        for pr in prs:
            sl = slice(128 * pr, 128 * pr + 128)
            x2, dt2, dy2, acs2 = x_ref[:, sl], dt_ref[:, sl], dy_ref[:, sl], ac_ref[:, sl]
            lam2, dec2, xd2 = _ssd_pair(x2, dt2, acs2)
            P.append(dict(sl=sl, x2=x2, dt2=dt2, dy2=dy2, acs2=acs2, lam2=lam2, dec2=dec2, xd2=xd2,
                          hprev=hst_ref[pr], dhn=dh_scr[pr], dz=lam2 * dy2))
        zs = [mm_nt(cm, p["hprev"]) for p in P]
        dcm_t = [mm(p["dz"], p["hprev"]) for p in P]
        dh_z = [mm_tn(p["dz"], cm) for p in P]
        H = []
        for pr in prs:
            p = P[pr]
            p["yoff"] = p["dz"] * zs[pr]
            p["q_rows"] = _lanes(p["dhn"] * p["hprev"])
            for hh in range(2):
                lmask, sg, mmat, dec_col, bd = _ssd_head(hh, p["acs2"], at_ref[2 * pr + hh], p["dec2"], cbm, bm, incl, col)
                H.append(dict(p=p, hh=hh, j=2 * pr + hh, lmask=lmask, sg=sg, mmat=mmat, dec_col=dec_col, bd=bd))
        dms = [mm_nt(jnp.where(h["lmask"], h["p"]["dy2"], 0.0), h["p"]["xd2"]) for h in H]
        a1s = [mm_tn(h["mmat"], h["p"]["dy2"]) for h in H]
        a2s = [mm_nt(h["bd"], h["p"]["dhn"]) for h in H]
        dbds = [mm(jnp.where(h["lmask"], h["p"]["xd2"], 0.0), h["p"]["dhn"]) for h in H]
        dcb = jnp.zeros((L, L), F32)
        dbm = jnp.zeros((L, SSM_N), F32)
        comp = jnp.zeros((L, 128), F32)
        dxd = [jnp.zeros((L, 128), F32) for _ in prs]
        for h, dm_raw, a1, a2, dbd in zip(H, dms, a1s, a2s, dbds):
            p, hh, j = h["p"], h["hh"], h["j"]
            dm = jnp.where(incl, dm_raw, 0.0)
            dcb = dcb + dm * h["sg"]
            e = dm * h["mmat"]
            dxd_h = jnp.where(h["lmask"], a1 + a2, 0.0)
            dxd[j // 2] = dxd[j // 2] + dxd_h
            dbm = dbm + h["dec_col"] * dbd
            t = _lanes(dbd * h["bd"])
            lam_h = p["lam2"][L - 1:L, 64 * hh:64 * hh + 1]
            in_head = (rowc >= 64 * hh) & (rowc < 64 * hh + 64)
            add_last = _sum_all(t) + _sum_all(jnp.where(in_head, p["q_rows"], 0.0)) * lam_h
            dacs_col = (_lanes(jnp.where(h["lmask"], p["yoff"], 0.0)) + _lanes(e) - t
                        + jnp.where(rowc == L - 1, add_last, 0.0))
            ddt_col = _lanes(dxd_h * p["x2"])
            dar_ref[j] = _rows(e)
            comp = comp + jnp.where(col == j, dacs_col, 0.0) + jnp.where(col == 8 + j, ddt_col, 0.0)
        dcm = dcm_t[0]
        for pr in prs:
            p = P[pr]
            if pr:
                dcm = dcm + dcm_t[pr]
            lam_rows = jnp.where(row < 64, p["lam2"][L - 1:L, 0:1], p["lam2"][L - 1:L, 64:65])
            dh_scr[pr] = dh_z[pr] + lam_rows * p["dhn"]
            dx_ref[:, p["sl"]] = p["dt2"] * dxd[pr]
        db_ref[...] = dbm + mm_tn(dcb, cm)
        dc_ref[...] = dcm + mm(dcb, bm)
        dgate_ref[...] = comp

    rv = lambda g, c: (nc - 1 - c, g)
    rowspec = pl.BlockSpec((8, 1, L), lambda g, c: (g, 0, nc - 1 - c))
    return pl.pallas_call(
        body, name="ssd_bwd", grid=(2, nc),
        in_specs=[pl.BlockSpec((L, 512), rv),
                  pl.BlockSpec((L, 128), rv),
                  pl.BlockSpec((L, 128), lambda g, c: (nc - 1 - c, 2 + g)),
                  pl.BlockSpec((L, 512), rv),
                  pl.BlockSpec((L, 512), rv),
                  rowspec,
                  pl.BlockSpec((L, 512), rv),
                  pl.BlockSpec((None, None, 4, 128, 128), lambda g, c: (g, nc - 1 - c, 0, 0, 0))],
        out_specs=[pl.BlockSpec((L, 512), rv), pl.BlockSpec((L, 128), rv), pl.BlockSpec((L, 128), rv),
                   pl.BlockSpec((L, 128), rv), rowspec],
        out_shape=[jax.ShapeDtypeStruct((T, D), F32), jax.ShapeDtypeStruct((T, 256), F32),
                   jax.ShapeDtypeStruct((T, 256), F32), jax.ShapeDtypeStruct((T, 256), F32),
                   jax.ShapeDtypeStruct((SSM_H, 1, T), F32)],
        scratch_shapes=[pltpu.VMEM((4, 128, 128), F32)],
        compiler_params=_cparams(("parallel", "arbitrary")),
    )(xs, bc, bc, dt_x, acs_x, acs_t, dy, hstates)


def _pos():
    return lax.axis_index("x"), lax.axis_index("y"), lax.axis_index("c")


def _other_chips(x, y):
    return [(1 - x, y), (x, 1 - y), (1 - x, 1 - y)]


def _rcopy(src, dst, ssem, rsem, dev):
    return pltpu.make_async_remote_copy(src_ref=src, dst_ref=dst, send_sem=ssem, recv_sem=rsem,
                                        device_id=dev, device_id_type=MESH)


def _rows_at(start, n):
    return pl.ds(pl.multiple_of(start, 8), n)


def _comm_call(body, name, out_shape, n_in, scratch):
    return pl.pallas_call(
        body, name=name, out_shape=out_shape, in_specs=[ANY] * n_in,
        out_specs=[ANY] * len(out_shape) if isinstance(out_shape, (list, tuple)) else ANY,
        scratch_shapes=scratch,
        compiler_params=pltpu.CompilerParams(has_side_effects=True),
    )


def _dma_sems(n):
    return pltpu.SemaphoreType.DMA((n,))


def ag_chips(name, shard):
    rr, cc = shard.shape
    h, nq = rr // 2, ICI_CHUNKS
    hq = h // nq

    def body(x_ref, out_ref, ssem, rsem):
        x, y, c = _pos()
        chips = _other_chips(x, y)
        started = []
        for q in range(nq):
            rows = _rows_at(c * h + q * hq, hq)
            for j, (cx, cy) in enumerate(chips):
                cp = _rcopy(x_ref.at[rows], out_ref.at[j, rows], ssem.at[j * nq + q], rsem.at[j * nq + q], (cx, cy, c))
                cp.start()
                started.append(cp)
        for q in range(nq):
            rows = _rows_at(c * h + q * hq, hq)
            for j, (cx, cy) in enumerate(chips):
                blk = out_ref.at[j, rows]
                _rcopy(blk, blk, ssem.at[j * nq + q], rsem.at[j * nq + q], (cx, cy, c)).wait_recv()
                k = 3 * nq + j * nq + q
                cp = _rcopy(blk, blk, ssem.at[k], rsem.at[k], (x, y, 1 - c))
                cp.start()
                started.append(cp)
        for q in range(nq):
            rows = _rows_at((1 - c) * h + q * hq, hq)
            for j in range(3):
                blk = out_ref.at[j, rows]
                k = 3 * nq + j * nq + q
                _rcopy(blk, blk, ssem.at[k], rsem.at[k], (x, y, 1 - c)).wait_recv()
        for cp in started:
            cp.wait_send()

    return _comm_call(body, name, jax.ShapeDtypeStruct((3, rr, cc), shard.dtype), 1,
                      [_dma_sems(6 * nq), _dma_sems(6 * nq)])(shard)


def all_gather_chips(name, shard, s_me):
    got = ag_chips(name, shard)
    by_rel = jnp.stack([shard, got[1], got[0], got[2]])
    return jnp.take(by_rel, jnp.arange(4) ^ s_me, axis=0)


def rs_pair(name, g):
    _, rr, cc = g.shape
    h, nq = rr // 2, D2D_CHUNKS
    hq = h // nq

    def body(g_ref, recv_ref, ssem, rsem):
        x, y, c = _pos()
        cps = []
        for q in range(nq):
            cp = _rcopy(g_ref.at[:, _rows_at((1 - c) * h + q * hq, hq), :], recv_ref.at[:, pl.ds(q * hq, hq), :],
                        ssem.at[q], rsem.at[q], (x, y, 1 - c))
            cp.start()
            cps.append(cp)
        for cp in cps:
            cp.wait()

    return _comm_call(body, name, jax.ShapeDtypeStruct((4, h, cc), g.dtype), 1, [_dma_sems(nq), _dma_sems(nq)])(g)


def rs_chips(name, p):
    _, h, cc = p.shape
    nq = ICI_CHUNKS
    hq = h // nq

    def body(p_ref, buf_ref, ssem, rsem):
        x, y, c = _pos()
        sends = []
        for q in range(nq):
            rows = pl.ds(q * hq, hq)
            for j, (cx, cy) in enumerate(_other_chips(x, y)):
                cp = _rcopy(p_ref.at[2 * cx + cy, rows], buf_ref.at[j, rows], ssem.at[j * nq + q],
                            rsem.at[j * nq + q], (cx, cy, c))
                cp.start()
                sends.append(cp)
        for cp in sends:
            cp.wait()

    return _comm_call(body, name, jax.ShapeDtypeStruct((3, h, cc), p.dtype), 1,
                      [_dma_sems(3 * nq), _dma_sems(3 * nq)])(p)


def rs_join(name, half):
    h, cc = half.shape
    nq = D2D_CHUNKS
    hq = h // nq

    def body(h_ref, out_ref, ssem, rsem):
        x, y, c = _pos()
        cps = []
        for q in range(nq):
            rows = pl.ds(q * hq, hq)
            cp = _rcopy(h_ref.at[rows], out_ref.at[rows], ssem.at[q], rsem.at[q], (x, y, 1 - c))
            cp.start()
            cps.append(cp)
        for cp in cps:
            cp.wait()

    return _comm_call(body, name, jax.ShapeDtypeStruct((h, cc), half.dtype), 1, [_dma_sems(nq), _dma_sems(nq)])(half)


def reduce_scatter(tag, g, tb, sp):
    _, rr, cc = g.shape
    h = rr // 2
    nbh = h // tb
    recv = rs_pair(tag + "_pair", g)
    mine_rows = lambda i, s: (i // nbh) * (2 * nbh) + s[0] * nbh + i % nbh
    part = rowwise(add2_fn, tag + "_add", 4 * h, tb, [R(g.reshape(4 * rr, cc), off=mine_rows), R(recv.reshape(4 * h, cc))],
                   [], [(cc, BF16)], sp=sp)[0]
    buf = rs_chips(tag + "_chips", part.reshape(4, h, cc)).reshape(3 * h, cc)
    red = rowwise(sum4_fn, tag + "_sum", h, tb,
                  [R(part, off=lambda i, s: s[1] * nbh + i)] + [R(buf, off=k * nbh) for k in range(3)],
                  [], [(cc, F32)], sp=sp)[0]
    return red, rs_join(tag + "_join", red)


def adam_halves(name, w, m, v, red, other, tb, blk0, sp):
    nbh = red.shape[0] // tb

    def fn(i, n, s, w_, m_, v_, r_, o_):
        g = jnp.where((blk0 + i) // nbh == s[0], r_, o_)
        return (g,) + _adamw(w_, g, m_, v_)

    half_rows = lambda i, s: (blk0 + i) % nbh
    return rowwise(fn, name, w.shape[0], tb, [R(w), R(m), R(v), R(red, off=half_rows), R(other, off=half_rows)],
                   [], [(w.shape[1], F32)] * 4, sp=sp)


SMALL_LANES = 3 * D


def all_reduce_items(name, items):
    flat = [a for it in items for a in it]
    shapes = [(sum(a.shape[0] for a in it), it[0].shape[1]) for it in items]
    nrows = -(-sum(s[0] for s in shapes) // 8) * 8

    def body(*refs):
        ins, outs = refs[:len(flat)], refs[len(flat):len(flat) + len(items)]
        mine, buf, ssem, rsem = refs[len(flat) + len(items):]
        x, y, c = _pos()
        me = 4 * x + 2 * y + c
        mine[...] = jnp.zeros_like(mine)
        r = 0
        for ref in ins:
            mine[r:r + ref.shape[0], 0:ref.shape[1]] = ref[...]
            r += ref.shape[0]
        buf[me] = mine[...]
        cps = []
        for k in range(1, 8):
            dev = (x ^ (k >> 2), y ^ ((k >> 1) & 1), c ^ (k & 1))
            cp = _rcopy(mine, buf.at[me], ssem.at[k - 1], rsem.at[k - 1], dev)
            cp.start()
            cps.append(cp)
        for cp in cps:
            cp.wait()
        r = 0
        for (nr, n), out in zip(shapes, outs):
            acc = buf[0, r:r + nr, 0:n]
            for d in range(1, 8):
                acc = acc + buf[d, r:r + nr, 0:n]
            out[...] = acc
            r += nr

    vm = pl.BlockSpec(memory_space=pltpu.VMEM)
    return pl.pallas_call(
        body, name=name, out_shape=[jax.ShapeDtypeStruct(s, F32) for s in shapes],
        in_specs=[vm] * len(flat), out_specs=[vm] * len(items),
        scratch_shapes=[pltpu.VMEM((nrows, SMALL_LANES), F32), pltpu.VMEM((8, nrows, SMALL_LANES), F32),
                        _dma_sems(7), _dma_sems(7)],
        compiler_params=pltpu.CompilerParams(has_side_effects=True),
    )(*flat)


def adam_small(ws, gs, ms, vs):
    n = len(ws)

    def body(*refs):
        for k in range(n):
            w, g, m, v = (refs[j * n + k][...] for j in range(4))
            for j, val in enumerate(_adamw(w, g, m, v)):
                refs[(4 + j) * n + k][...] = val

    vm = pl.BlockSpec(memory_space=pltpu.VMEM)
    res = pl.pallas_call(
        body, name="adam_small", out_shape=[jax.ShapeDtypeStruct(w.shape, F32) for w in ws] * 3,
        in_specs=[vm] * (4 * n), out_specs=[vm] * (3 * n),
    )(*ws, *gs, *ms, *vs)
    return res[:n], res[n:2 * n], res[2 * n:]


def _sel(rows, cols, pairs):
    m = np.zeros((rows, cols), np.float32)
    for r, c in pairs:
        m[r, c] = 1.0
    return jnp.asarray(m)


def _pad_win(w):
    z = jnp.zeros((w.shape[0], 112), w.dtype)
    return jnp.concatenate([w[:, :4096], w[:, 4112:6672], w[:, 4096:4112], z, w[:, 6672:6688], z], axis=1)


def _unpad_win(wp):
    return jnp.concatenate([wp[:, :4096], wp[:, 6656:6672], wp[:, 4096:6656], wp[:, 6784:6800]], axis=1)


def kernel(x, mem, norm1_w, w_in, gdn_conv_w, gdn_a_log, gdn_dt_bias, gdn_norm_w, ssm_conv_w, ssm_conv_b, ssm_a_log, ssm_dt_bias, ssm_d, ssm_norm_w, w_out, norm2_w, mem_norm_w, wq_mem, wk_mem, wv_mem, wo_mem, norm3_w, w_up, w_down, final_norm_w, loss_target, m_norm1_w, m_w_in, m_gdn_conv_w, m_gdn_a_log, m_gdn_dt_bias, m_gdn_norm_w, m_ssm_conv_w, m_ssm_conv_b, m_ssm_a_log, m_ssm_dt_bias, m_ssm_d, m_ssm_norm_w, m_w_out, m_norm2_w, m_mem_norm_w, m_wq_mem, m_wk_mem, m_wv_mem, m_wo_mem, m_norm3_w, m_w_up, m_w_down, m_final_norm_w, v_norm1_w, v_w_in, v_gdn_conv_w, v_gdn_a_log, v_gdn_dt_bias, v_gdn_norm_w, v_ssm_conv_w, v_ssm_conv_b, v_ssm_a_log, v_ssm_dt_bias, v_ssm_d, v_ssm_norm_w, v_w_out, v_norm2_w, v_mem_norm_w, v_wq_mem, v_wk_mem, v_wv_mem, v_wo_mem, v_norm3_w, v_w_up, v_w_down, v_final_norm_w):
    T, M = x.shape[1], mem.shape[1]
    xi, yi, ci = _pos()
    s_me = 2 * xi + yi
    x0, mem0, tgt = x[0], mem[0], loss_target[0]
    tb = min(256, T)
    row = lambda v: v.reshape(1, -1)

    win_g = all_gather_chips("ag_win", w_in.astype(BF16), s_me)
    rest_g = all_gather_chips("ag_rest", jnp.concatenate([w_up, w_down, w_out, wq_mem, wk_mem, wv_mem, wo_mem],
                                                         axis=0).astype(BF16), s_me)
    w_in_p = _pad_win(win_g.transpose(1, 0, 2).reshape(D, IN_COLS))
    wup_f = rest_g[:, 0:1024].transpose(1, 0, 2).reshape(D, D_FF)
    wdown_f = rest_g[:, 1024:2048].reshape(D_FF, D)
    wout_f = rest_g[:, 2048:2560].reshape(2 * D, D)
    wq_f, wk_f, wv_f, wo_f = (rest_g[:, 2560 + 256 * k:2816 + 256 * k].reshape(D, D) for k in range(4))
    keep = (ci == 0).astype(F32)
    gcw_z = lax.dynamic_update_slice(jnp.zeros((4, 3 * D), F32), gdn_conv_w * keep, (0, s_me * 768))
    scw_z = lax.dynamic_update_slice(jnp.zeros((4, 1536), F32), ssm_conv_w * keep, (0, s_me * 384))
    gcw, scw = all_reduce_items("ar_convw", [[gcw_z], [scw_z]])
    scw_x, scw_bc = scw[:, :D], scw[:, D:]
    sp = jnp.stack([ci, s_me]).astype(jnp.int32)
    scb_x, scb_bc = row(ssm_conv_b[:D]), row(ssm_conv_b[D:])

    galog_c, gdtb_c = row(jnp.pad(gdn_a_log, (8, 112))), row(jnp.pad(gdn_dt_bias, (8, 112)))
    salog_c, sdtb_c = row(jnp.pad(ssm_a_log, (0, 112))), row(jnp.pad(ssm_dt_bias, (0, 112)))
    sd_x = row(jnp.repeat(ssm_d, 64))
    eb = _sel(128, D, [(h, 128 * h + l) for h in range(8) for l in range(128)])
    ea = _sel(128, D, [(8 + h, 128 * h + l) for h in range(8) for l in range(128)])
    e16 = _sel(128, D, [(h, 64 * h + l) for h in range(16) for l in range(64)])
    pb = _sel(D, 128, [(128 * h, h) for h in range(8)])
    pa = _sel(D, 128, [(128 * h, 8 + h) for h in range(8)])

    h1 = rowwise(rms_fwd_fn, "rms1", T, tb, [R(x0)], [row(norm1_w)], [(D, BF16)])[0]
    p = matmul("mm_in", h1, w_in_p, "nn", 1024, 768, 1024, [F32])[0]
    gp_ins = [R(p, 3 * D, CB_QKV, "prev"), R(p, 128, CB_BA)]
    qn, kn, vv, gcs_x, beta_x, gcs_t = rowwise(gdn_prep_fn, "gdn_prep", T, tb, gp_ins,
                                               [gcw, galog_c, gdtb_c, eb, ea], [(D, F32)] * 5 + [(-8, F32)])
    gcs_t = gcs_t.reshape(GDN_H, 1, T)
    gtb, ggh = min(128, T), 4
    o_gdn, s_states, tinv = gdn_fwd(qn, kn, vv, gcs_x, beta_x, gcs_t, gtb, ggh)
    gnw = row(gdn_norm_w)
    oa = rowwise(gdn_post_fn, "gdn_post", T, tb, [R(o_gdn), R(p, D, CB_Z)], [gnw], [(D, BF16)])[0]
    sp_ins = [R(p, D, CB_XS, "prev"), R(p, 512, CB_BC, "prev"), R(p, 128, CB_DT)]
    sp_full = [scw_x, scw_bc, scb_x, scb_bc, salog_c, sdtb_c]
    xs, bc, dt_x, acs_x, acs_t = rowwise(ssd_prep_fn, "ssd_prep", T, tb, sp_ins, sp_full + [e16],
                                         [(D, F32), (512, F32), (D, F32), (D, F32), (-SSM_H, F32)])
    acs_t = acs_t.reshape(SSM_H, 1, T)
    y_ssd, h_states = ssd_fwd(xs, bc, dt_x, acs_x, acs_t)
    snw = row(ssm_norm_w)
    ob = rowwise(ssd_post_fn, "ssd_post", T, tb, [R(y_ssd), R(xs), R(p, D, CB_ZS)], [sd_x, snw], [(D, BF16)])[0]
    x1a = matmul("mm_out_a", oa, wout_f[:D], "nn", 1024, 1024, 1024, [F32], _epi_res, [x0])[0]
    assert D == 1024
    x1, h2 = matmul("mm_out_b", ob, wout_f[D:], "nn", 1024, 1024, 1024, [F32, BF16], _epi_res_rms, [x1a],
                    [row(norm2_w)])

    mn = rowwise(rms_fwd_fn, "rms_mem", M, M, [R(mem0)], [row(mem_norm_w)], [(D, BF16)])[0]
    km = matmul("mm_k", mn, wk_f, "nn", 256, 1024, 1024, [BF16])[0]
    vm = matmul("mm_v", mn, wv_f, "nn", 256, 1024, 1024, [BF16])[0]
    qm = matmul("mm_q", h2, wq_f, "nn", 1024, 1024, 1024, [BF16])[0]
    ao = rowwise(attn_fn, "attn", T, tb, [R(qm)], [km, vm], [(D, BF16)])[0]
    x2, h3 = matmul("mm_o", ao, wo_f, "nn", 1024, 1024, 1024, [F32, BF16], _epi_res_rms, [x1], [row(norm3_w)])
    u, act = matmul("mm_up", h3, wup_f, "nn", 1024, 1024, 1024, [BF16, BF16], _epi_relu2)
    dx3, dx3b, loss_lane, g_final = matmul("mm_down", act, wdown_f, "nn", 512, 1024, 1024, [F32, BF16], _epi_final,
                                           [x2, tgt], [row(final_norm_w)], n_acc=2)
    loss = lax.psum(0.5 / D * jnp.sum(loss_lane), ("x", "y", "c"))

    dup = matmul("mm_dact", dx3b, wdown_f, "nt", 1024, 1024, 1024, [BF16], _epi_dup, [u])[0]
    def g_into(buf, blk, at):
        return dict(into=(buf, blk, lambda i, j, k, at=at: at(i, j)))

    grest = jax.ShapeDtypeStruct((4, 3584, D), F32)
    grest = matmul("mm_gdown", act, dx3b, "tn", 1024, 1024, 1024, [F32],
                   **g_into(grest, (None, 1024, D), lambda i, j: (i, 1, 0)))
    dx2, dx2b, g_n3 = matmul("mm_dh3", dup, wup_f, "nt", 512, 1024, 1024, [F32, BF16], _epi_rms_bwd, [x2, dx3],
                             [row(norm3_w)], n_acc=1)
    grest = matmul("mm_gup", h3, dup, "tn", 1024, 1024, 1024, [F32],
                   **g_into(grest, (None, 1024, D), lambda i, j: (j, 0, 0)))
    dao = matmul("mm_dao", dx2b, wo_f, "nt", 1024, 1024, 1024, [F32])[0]
    grest = matmul("mm_gwo", ao, dx2b, "tn", 1024, 1024, 1024, [F32],
                   **g_into(grest, (4, 256, D), lambda i, j: (0, 13, 0)))
    dqm, dkm, dvm = rowwise(attn_bwd_fn, "attn_bwd", T, tb, [R(qm), R(dao)], [km, vm], [(D, BF16)],
                            [(M, D), (M, D)])
    dx1, dx1b, g_n2 = matmul("mm_dh2", dqm, wq_f, "nt", 512, 1024, 1024, [F32, BF16], _epi_rms_bwd, [x1, dx2],
                             [row(norm2_w)], n_acc=1)
    grest = matmul("mm_gwq", h2, dqm, "tn", 1024, 1024, 1024, [F32],
                   **g_into(grest, (4, 256, D), lambda i, j: (0, 10, 0)))
    grest = matmul("mm_gwk", mn, dkm, "tn", 1024, 1024, 256, [F32],
                   **g_into(grest, (4, 256, D), lambda i, j: (0, 11, 0)))
    grest = matmul("mm_gwv", mn, dvm, "tn", 1024, 1024, 256, [F32],
                   **g_into(grest, (4, 256, D), lambda i, j: (0, 12, 0)))
    dmn_k = matmul("mm_dmk", dkm, wk_f, "nt", 256, 1024, 1024, [F32])[0]
    dmn = matmul("mm_dmv", dvm, wv_f, "nt", 256, 1024, 1024, [F32], _epi_res, [dmn_k])[0]
    g_nmem = rowwise(rms_bwd_w_fn, "rmsmem_bwd", M, M, [R(mem0), R(dmn)], [row(mem_norm_w)], [], [(1, D)])[0]
    doa = matmul("mm_doa", dx1b, wout_f[:D], "nt", 1024, 1024, 1024, [F32])[0]
    dob = matmul("mm_dob", dx1b, wout_f[D:], "nt", 1024, 1024, 1024, [F32])[0]
    grest = matmul("mm_gwout_a", oa, dx1b, "tn", 1024, 1024, 1024, [F32],
                   **g_into(grest, (2, 512, D), lambda i, j: (0, 4, 0)))
    grest = matmul("mm_gwout_b", ob, dx1b, "tn", 1024, 1024, 1024, [F32],
                   **g_into(grest, (2, 512, D), lambda i, j: (1, 4, 0)))

    dp = jax.ShapeDtypeStruct((T, p.shape[1]), BF16)
    dy_ssd, dxs_dir, dp, g_snw, g_sd_lane = rowwise(
        ssd_post_bwd_fn, "ssd_post_bwd", T, tb, [R(y_ssd), R(xs), R(p, D, CB_ZS), R(dob)], [sd_x, snw],
        [(D, F32), (D, F32), (D, BF16, dp, CB_ZS)], [(1, D), (1, D)])
    dxs_scan, db_s, dc_s, dgate, dacs_t = ssd_bwd(xs, bc, dt_x, acs_x, acs_t, dy_ssd, h_states)
    spb = rowwise(ssd_prep_bwd_fn, "ssd_prep_bwd", T, tb,
                  sp_ins + [R(dxs_scan), R(dxs_dir), R(db_s), R(dc_s), R(dgate), RC(dacs_t.reshape(SSM_H, T))], sp_full,
                  [(D, F32), (512, F32), (128, BF16, dp, CB_DT)],
                  [(1, D)] * 4 + [(1, 512)] * 4 + [(1, D), (1, 512), (1, 128), (1, 128)])
    dyc_x, dyc_bc, dp = spb[:3]
    dp = rowwise(conv_bwd_fn, "conv_bwd_x", T, tb, [R(dyc_x, halo="next")], [scw_x], [(D, BF16, dp, CB_XS)])[0]
    dp = rowwise(conv_bwd_fn, "conv_bwd_bc", T, tb, [R(dyc_bc, halo="next")], [scw_bc], [(512, BF16, dp, CB_BC)])[0]

    do_gdn, dp, g_gnw = rowwise(gdn_post_bwd_fn, "gdn_post_bwd", T, tb, [R(o_gdn), R(p, D, CB_Z), R(doa)], [gnw],
                                [(D, F32), (D, BF16, dp, CB_Z)], [(1, 128)])
    dqn, dkn, dvv, dgcs_x, dbeta_x, dgcs_t = gdn_bwd(qn, kn, vv, gcs_x, beta_x, gcs_t, do_gdn, s_states, tinv, gtb, ggh)
    gpb = rowwise(gdn_prep_bwd_fn, "gdn_prep_bwd", T, tb,
                  gp_ins + [R(dqn), R(dkn), R(dvv), R(dgcs_x), R(dbeta_x), RC(dgcs_t.reshape(GDN_H, T))],
                  [gcw, galog_c, gdtb_c, pb, pa],
                  [(3 * D, F32), (128, BF16, dp, CB_BA)], [(1, 3 * D)] * 4 + [(1, 128), (1, 128)])
    dyc_qkv, dp = gpb[:2]
    dp = rowwise(conv_bwd_fn, "conv_bwd_qkv", T, tb, [R(dyc_qkv, halo="next")], [gcw], [(3 * D, BF16, dp, CB_QKV)])[0]
    grad_x, g_n1 = matmul("mm_dh1", dp, w_in_p, "nt", 512, 1024, 768, [F32], _epi_rms_bwd1, [x0, dx1],
                          [row(norm1_w)], n_acc=1)
    g_win_p = matmul("mm_gwin", h1, dp, "tn", 1024, 768, 1024, [F32])[0]

    items = [[g_n1], [gpb[6]], [gpb[7]], [g_gnw], [spb[11]], [spb[12]], [spb[13]], [spb[14]], [g_sd_lane], [g_snw],
             [g_n2], [g_nmem], [g_n3], [g_final], list(gpb[2:6]), list(spb[3:7]), list(spb[7:11])]
    (gr_n1, r_galog, r_gdtb, gr_gnw, r_scb_x, r_scb_bc, r_salog, r_sdtb, r_sd, gr_snw, gr_n2, gr_nmem, gr_n3,
     gr_final, r_gcw, r_scw_x, r_scw_bc) = all_reduce_items("ar_grads", items)
    gr_galog, gr_gdtb = r_galog[:, 8:16], r_gdtb[:, 8:16]
    gr_salog, gr_sdtb = r_salog[:, :SSM_H], r_sdtb[:, :SSM_H]
    gr_sd = r_sd.reshape(SSM_H, SSM_P).sum(axis=1).reshape(1, SSM_H)
    gr_scb = jnp.concatenate([r_scb_x, r_scb_bc], axis=1)
    gr_gcw = lax.dynamic_slice(r_gcw, (0, s_me * 768), (4, 768))
    gr_scw = lax.dynamic_slice(jnp.concatenate([r_scw_x, r_scw_bc], axis=1), (0, s_me * 384), (4, 384))

    g_win = _unpad_win(g_win_p).reshape(D, 4, IN_COLS // 4).transpose(1, 0, 2)
    red_w, oth_w = reduce_scatter("rs_win", g_win, 256, sp)
    red_r, oth_r = reduce_scatter("rs_rest", grest, 256, sp)

    big = {"w_in": adam_halves("adam_win", w_in, m_w_in, v_w_in, red_w, oth_w, 256, 0, sp)}
    for n, w, m, v, blk0 in (("w_up", w_up, m_w_up, v_w_up, 0), ("w_down", w_down, m_w_down, v_w_down, 4),
                             ("w_out", w_out, m_w_out, v_w_out, 8), ("wq_mem", wq_mem, m_wq_mem, v_wq_mem, 10),
                             ("wk_mem", wk_mem, m_wk_mem, v_wk_mem, 11), ("wv_mem", wv_mem, m_wv_mem, v_wv_mem, 12),
                             ("wo_mem", wo_mem, m_wo_mem, v_wo_mem, 13)):
        big[n] = adam_halves("adam_" + n, w, m, v, red_r, oth_r, 256, blk0, sp)
    names_s = ["norm1_w", "gdn_conv_w", "gdn_a_log", "gdn_dt_bias", "gdn_norm_w", "ssm_conv_w", "ssm_conv_b",
               "ssm_a_log", "ssm_dt_bias", "ssm_d", "ssm_norm_w", "norm2_w", "mem_norm_w", "norm3_w", "final_norm_w"]
    w_s = [norm1_w, gdn_conv_w, gdn_a_log, gdn_dt_bias, gdn_norm_w, ssm_conv_w, ssm_conv_b, ssm_a_log, ssm_dt_bias,
           ssm_d, ssm_norm_w, norm2_w, mem_norm_w, norm3_w, final_norm_w]
    g_s = [gr_n1, gr_gcw, gr_galog, gr_gdtb, gr_gnw, gr_scw, gr_scb, gr_salog, gr_sdtb, gr_sd, gr_snw, gr_n2,
           gr_nmem, gr_n3, gr_final]
    m_s = [m_norm1_w, m_gdn_conv_w, m_gdn_a_log, m_gdn_dt_bias, m_gdn_norm_w, m_ssm_conv_w, m_ssm_conv_b, m_ssm_a_log,
           m_ssm_dt_bias, m_ssm_d, m_ssm_norm_w, m_norm2_w, m_mem_norm_w, m_norm3_w, m_final_norm_w]
    v_s = [v_norm1_w, v_gdn_conv_w, v_gdn_a_log, v_gdn_dt_bias, v_gdn_norm_w, v_ssm_conv_w, v_ssm_conv_b, v_ssm_a_log,
           v_ssm_dt_bias, v_ssm_d, v_ssm_norm_w, v_norm2_w, v_mem_norm_w, v_norm3_w, v_final_norm_w]
    shp_s = [w.shape for w in w_s]
    as2d = lambda a: a if a.ndim == 2 else a.reshape(1, -1)
    d_l, m_l, v_l = adam_small([as2d(a) for a in w_s], [as2d(a) for a in g_s], [as2d(a) for a in m_s],
                               [as2d(a) for a in v_s])

    grads, deltas, new_m, new_v = {}, {}, {}, {}
    for n, (gg, dd, mm_, vv_) in big.items():
        grads[n], deltas[n], new_m[n], new_v[n] = gg, dd, mm_, vv_
    for k, n in enumerate(names_s):
        grads[n] = g_s[k].reshape(shp_s[k])
        deltas[n], new_m[n], new_v[n] = (a[k].reshape(shp_s[k]) for a in (d_l, m_l, v_l))
    order = ["norm1_w", "w_in", "gdn_conv_w", "gdn_a_log", "gdn_dt_bias", "gdn_norm_w", "ssm_conv_w", "ssm_conv_b",
             "ssm_a_log", "ssm_dt_bias", "ssm_d", "ssm_norm_w", "w_out", "norm2_w", "mem_norm_w", "wq_mem", "wk_mem",
             "wv_mem", "wo_mem", "norm3_w", "w_up", "w_down", "final_norm_w"]
    return (loss, grad_x[None], *[grads[n] for n in order], *[deltas[n] for n in order],
            *[new_m[n] for n in order], *[new_v[n] for n in order])
```

```python
import numpy as np
import jax
import jax.numpy as jnp
from jax import lax
from jax.experimental import pallas as pl
from jax.experimental.pallas import tpu as pltpu

F32, BF16 = jnp.float32, jnp.bfloat16
MESH = pl.DeviceIdType.MESH
ANY = pl.BlockSpec(memory_space=pl.ANY)

EPS = 1e-6
D = 1024
GDN_H, GDN_DK, GDN_C = 8, 128, 64
SSM_H, SSM_P, SSM_N, SSM_L = 16, 64, 128, 128
MEM_H, MEM_DH = 4, 256
D_FF = 4096
IN_COLS = 6688
CB_QKV, CB_Z, CB_ZS, CB_XS, CB_BC, CB_BA, CB_DT = 0, 3, 4, 5, 12, 52, 53
VMEM_LIMIT = 56 * 1024 * 1024
D2D_CHUNKS = 8
ICI_CHUNKS = 4

ADAM_LR, ADAM_B1, ADAM_B2, ADAM_EPS, ADAM_WD, ADAM_STEP = 0.001, 0.9, 0.999, 1e-08, 0.01, 10


def _dg(a, b, ca, cb):
    return lax.dot_general(a, b, (((ca,), (cb,)), ((), ())), preferred_element_type=F32)


def _bf(x):
    return x.astype(BF16)


def mm(a, b):
    return _dg(_bf(a), _bf(b), 1, 0)


def mm_nt(a, b):
    return _dg(_bf(a), _bf(b), 1, 1)


def mm_tn(a, b):
    return _dg(_bf(a), _bf(b), 0, 0)


def mm_sel(a, sel):
    hi = a.astype(BF16)
    r1 = a - hi.astype(F32)
    mid = r1.astype(BF16)
    lo = (r1 - mid.astype(F32)).astype(BF16)
    s = sel.astype(BF16)
    return _dg(hi, s, 1, 0) + (_dg(mid, s, 1, 0) + _dg(lo, s, 1, 0))


def mm3(a, b):
    ah, bh = a.astype(BF16), b.astype(BF16)
    al, bl = (a - ah.astype(F32)).astype(BF16), (b - bh.astype(F32)).astype(BF16)
    return _dg(ah, bh, 1, 0) + (_dg(ah, bl, 1, 0) + _dg(al, bh, 1, 0))


def _iota(shape, dim):
    return lax.broadcasted_iota(jnp.int32, shape, dim)


def _chunk_cumsum(x, c):
    pos = _iota(x.shape, 0) & (c - 1)
    s = 1
    while s < c:
        x = x + jnp.where(pos >= s, pltpu.roll(x, s, 0), 0.0)
        s *= 2
    return x


def _chunk_revcumsum(x, c):
    n = x.shape[0]
    pos = _iota(x.shape, 0) & (c - 1)
    s = 1
    while s < c:
        x = x + jnp.where(pos < c - s, pltpu.roll(x, n - s, 0), 0.0)
        s *= 2
    return x


def _sig(x):
    return 1.0 / (1.0 + jnp.exp(-x))


def _softplus(x):
    return jnp.maximum(x, 0.0) + jnp.log(1.0 + jnp.exp(-jnp.abs(x)))


def _rows(v):
    return jnp.sum(v, axis=0, keepdims=True)


def _lanes(v):
    return jnp.sum(v, axis=1, keepdims=True)


def _sum_all(v):
    return _rows(_lanes(v))


def _cparams(sem):
    return pltpu.CompilerParams(dimension_semantics=sem, vmem_limit_bytes=VMEM_LIMIT)


def rowwise(fn, name, T, tb, row_ins, full_ins, row_outs, acc_outs=(), sp=None):
    nblk = T // tb
    assert nblk * tb == T
    has_sp = sp is not None

    def imap(f):
        return (lambda i, s: f(i, s)) if has_sp else (lambda i: f(i, None))

    in_specs, args = [], []
    for arr, w, cb, halo, off in row_ins:
        if halo == "col":
            in_specs.append(pl.BlockSpec((w, tb), imap(lambda i, s: (0, i))))
            args.append(arr)
            continue
        rowf = off if callable(off) else (lambda i, s, off=off: i + off)
        in_specs.append(pl.BlockSpec((tb, w), imap(lambda i, s, cb=cb, rowf=rowf: (rowf(i, s), cb))))
        args.append(arr)
        if halo == "prev":
            r = tb // 8
            in_specs.append(pl.BlockSpec((8, w), imap(lambda i, s, cb=cb, r=r: (jnp.maximum(i * r - 1, 0), cb))))
            args.append(arr)
        elif halo == "next":
            r, last = tb // 8, T // 8 - 1
            in_specs.append(pl.BlockSpec((8, w), imap(lambda i, s, cb=cb, r=r, last=last:
                                                      (jnp.minimum((i + 1) * r, last), cb))))
            args.append(arr)
    for arr in full_ins:
        in_specs.append(pl.BlockSpec(arr.shape, imap(lambda i, s, nd=arr.ndim: (0,) * nd)))
        args.append(arr)
    n_in, n_ro = len(args), len(row_outs)
    out_shape, out_specs, aliases = [], [], {}
    for k, (w, dt, *dest) in enumerate(row_outs):
        if dest:
            buf, cb = dest
            out_shape.append(jax.ShapeDtypeStruct(buf.shape, buf.dtype))
            out_specs.append(pl.BlockSpec((tb, w), imap(lambda i, s, cb=cb: (i, cb))))
            if not isinstance(buf, jax.ShapeDtypeStruct):
                aliases[len(args) + int(has_sp)] = k
                in_specs.append(ANY)
                args.append(buf)
        elif w < 0:
            out_shape.append(jax.ShapeDtypeStruct((-w, T), dt))
            out_specs.append(pl.BlockSpec((-w, tb), imap(lambda i, s: (0, i))))
        else:
            out_shape.append(jax.ShapeDtypeStruct((T, w), dt))
            out_specs.append(pl.BlockSpec((tb, w), imap(lambda i, s: (i, 0))))
    for shp in acc_outs:
        out_shape.append(jax.ShapeDtypeStruct(shp, F32))
        out_specs.append(pl.BlockSpec(shp, imap(lambda i, s, nd=len(shp): (0,) * nd)))

    def body(*refs):
        i = pl.program_id(0)
        if has_sp:
            sp_ref, refs = refs[0], refs[1:]
            vals = fn(i, nblk, sp_ref, *[r[...] for r in refs[:n_in]])
        else:
            vals = fn(i, nblk, *[r[...] for r in refs[:n_in]])
        outs = refs[n_in + len(aliases):]
        for ref, val in zip(outs[:n_ro], vals[:n_ro]):
            ref[...] = val.astype(ref.dtype)
        for ref, val in zip(outs[n_ro:], vals[n_ro:]):
            @pl.when(i == 0)
            def _(ref=ref, val=val):
                ref[...] = val

            @pl.when(i > 0)
            def _(ref=ref, val=val):
                ref[...] += val

    cparams = _cparams(("arbitrary",) if acc_outs else ("parallel",))
    if has_sp:
        return pl.pallas_call(
            body, name=name, out_shape=out_shape, compiler_params=cparams, input_output_aliases=aliases,
            grid_spec=pltpu.PrefetchScalarGridSpec(num_scalar_prefetch=1, grid=(nblk,), in_specs=in_specs,
                                                   out_specs=out_specs),
        )(sp, *args)
    return pl.pallas_call(
        body, name=name, grid=(nblk,), in_specs=in_specs, out_specs=out_specs, out_shape=out_shape,
        compiler_params=cparams, input_output_aliases=aliases,
    )(*args)


def R(arr, w=None, cb=0, halo=None, off=0):
    return (arr, arr.shape[1] if w is None else w, cb, halo, off)


def RC(arr):
    return (arr, arr.shape[0], 0, "col", 0)


def matmul(name, a, b, form, tm, tn, tk, out_dtypes, epi=None, extras=(), rows=(), into=None, n_acc=0):
    if form == "nn":
        (M, K), N = a.shape, b.shape[1]
    elif form == "nt":
        (M, K), N = a.shape, b.shape[0]
    else:
        (K, M), N = a.shape, b.shape[1]
    tm, tn, tk = min(tm, M), min(tn, N), min(tk, K)
    assert M % tm == 0 and N % tn == 0 and K % tk == 0, (name, M, N, K, tm, tn, tk)
    if form == "nn":
        a_spec = pl.BlockSpec((tm, tk), lambda i, j, k: (i, k))
        b_spec = pl.BlockSpec((tk, tn), lambda i, j, k: (k, j))
        ca, cb = 1, 0
    elif form == "nt":
        a_spec = pl.BlockSpec((tm, tk), lambda i, j, k: (i, k))
        b_spec = pl.BlockSpec((tn, tk), lambda i, j, k: (j, k))
        ca, cb = 1, 1
    else:
        a_spec = pl.BlockSpec((tk, tm), lambda i, j, k: (k, i))
        b_spec = pl.BlockSpec((tk, tn), lambda i, j, k: (k, j))
        ca, cb = 0, 0
    nk, ne, no = K // tk, len(extras) + len(rows), len(out_dtypes)
    if epi is None:
        epi = lambda acc: (acc,)

    assert n_acc == 0 or tn == N

    def body(a_ref, b_ref, *rest):
        ex, outs, accs, acc = rest[:ne], rest[ne:ne + no], rest[ne + no:ne + no + n_acc], rest[ne + no + n_acc]
        i, k = pl.program_id(0), pl.program_id(2)

        @pl.when(k == 0)
        def _():
            acc[...] = jnp.zeros_like(acc)

        acc[...] += _dg(_bf(a_ref[...]), _bf(b_ref[...]), ca, cb)

        @pl.when(k == nk - 1)
        def _():
            vals = epi(acc[...], *[e[...] for e in ex])
            for r, v in zip(outs, vals[:no]):
                r[...] = v.astype(r.dtype).reshape(r.shape)
            for r, v in zip(accs, vals[no:]):
                @pl.when(i == 0)
                def _(r=r, v=v):
                    r[...] = v

                @pl.when(i > 0)
                def _(r=r, v=v):
                    r[...] += v

    mn = pl.BlockSpec((tm, tn), lambda i, j, k: (i, j))
    rw = pl.BlockSpec((1, tn), lambda i, j, k: (0, j))
    if into is not None:
        buf, blk, bmap = into
        assert ne == 0 and no == 1
        aliased = not isinstance(buf, jax.ShapeDtypeStruct)

        def body_into(a_ref, b_ref, *rest):
            body(a_ref, b_ref, *rest[-2:])

        return pl.pallas_call(
            body_into, name=name, grid=(M // tm, N // tn, nk),
            in_specs=[a_spec, b_spec] + ([ANY] if aliased else []), out_specs=pl.BlockSpec(blk, bmap),
            out_shape=jax.ShapeDtypeStruct(buf.shape, buf.dtype),
            scratch_shapes=[pltpu.VMEM((tm, tn), F32)],
            input_output_aliases={2: 0} if aliased else {},
            compiler_params=_cparams(("parallel", "parallel", "arbitrary")),
        )(a, b, *([buf] if aliased else []))
    return pl.pallas_call(
        body, name=name, grid=(M // tm, N // tn, nk),
        in_specs=[a_spec, b_spec] + [mn] * len(extras) + [rw] * len(rows), out_specs=[mn] * no + [rw] * n_acc,
        out_shape=[jax.ShapeDtypeStruct((M, N), dt) for dt in out_dtypes] + [jax.ShapeDtypeStruct((1, N), F32)] * n_acc,
        scratch_shapes=[pltpu.VMEM((tm, tn), F32)],
        compiler_params=_cparams(("arbitrary",) * 3 if n_acc else ("parallel", "parallel", "arbitrary")),
    )(a, b, *extras, *rows)


def _epi_res(acc, res):
    return (res + acc,)


def _epi_rms_bwd(acc, x, dres, w):
    return rms_bwd_fn(0, 0, x, acc, dres, w)


def _epi_rms_bwd1(acc, x, dres, w):
    dx, _, gw = rms_bwd_fn(0, 0, x, acc, dres, w)
    return dx, gw


def _epi_final(acc, res, tgt, w):
    return final_fn(0, 0, res + acc, tgt, w)


def _epi_res_rms(acc, res, w):
    x = res + acc
    return (x, x * lax.rsqrt(jnp.mean(x * x, axis=-1, keepdims=True) + EPS) * w)


def _epi_relu2(acc):
    u = jnp.maximum(acc, 0.0)
    return (u, u * u)


def _epi_dup(acc, u):
    return (acc * 2.0 * u.astype(F32),)


def _conv(x, halo, w, i):
    halo = jnp.where(i == 0, 0.0, halo)
    xt = jnp.concatenate([halo, x], axis=0)
    shifted = [pltpu.roll(xt, 3 - k, 0)[8:, :] for k in range(3)] + [x]
    y = shifted[3] * w[3:4, :]
    for k in range(3):
        y = y + shifted[k] * w[k:k + 1, :]
    return y, shifted


def _l2n(x, scale):
    outs = []
    for h in range(x.shape[1] // 128):
        xh = x[:, 128 * h:128 * h + 128]
        outs.append(xh * (lax.rsqrt(jnp.sum(xh * xh, axis=-1, keepdims=True) + EPS) * scale))
    return jnp.concatenate(outs, axis=1)


def _l2n_bwd(x, dy, scale):
    outs = []
    for h in range(x.shape[1] // 128):
        xh, dh = x[:, 128 * h:128 * h + 128], dy[:, 128 * h:128 * h + 128] * scale
        r = lax.rsqrt(jnp.sum(xh * xh, axis=-1, keepdims=True) + EPS)
        outs.append(r * dh - xh * (r * r * r) * jnp.sum(xh * dh, axis=-1, keepdims=True))
    return jnp.concatenate(outs, axis=1)


def rms_fwd_fn(i, n, x, w):
    r = lax.rsqrt(jnp.mean(x * x, axis=-1, keepdims=True) + EPS)
    return (x * r * w,)


def rms_bwd_fn(i, n, x, dh, dres, w):
    r = lax.rsqrt(jnp.mean(x * x, axis=-1, keepdims=True) + EPS)
    g = dh * w
    dx = dres + r * g - x * (r * r * r) * jnp.mean(x * g, axis=-1, keepdims=True)
    return dx, dx, _rows(dh * x * r)


def rms_bwd_w_fn(i, n, x, dh, w):
    r = lax.rsqrt(jnp.mean(x * x, axis=-1, keepdims=True) + EPS)
    return (_rows(dh * x * r),)


def final_fn(i, n, x, tgt, w):
    r = lax.rsqrt(jnp.mean(x * x, axis=-1, keepdims=True) + EPS)
    xn = x * r
    e = xn * w - tgt
    dy = e * (1.0 / D)
    g = dy * w
    dx = r * g - x * (r * r * r) * jnp.mean(x * g, axis=-1, keepdims=True)
    return dx, dx, _rows(e * e), _rows(dy * xn)


def _gdn_gates(ba, alog_c, dtb_c):
    col = _iota(ba.shape, 1)
    amask = (col >= 8) & (col < 16)
    beta = jnp.where(col < 8, _sig(ba), 0.0)
    z = ba + dtb_c
    ea_ = jnp.exp(alog_c)
    return beta, z, ea_, jnp.where(amask, -ea_ * _softplus(z), 0.0), amask


def gdn_prep_fn(i, n, qkv, halo, ba, cw, alog_c, dtb_c, eb, ea):
    yc, _ = _conv(qkv, halo, cw, i)
    act = yc * _sig(yc)
    qn = _l2n(act[:, :D], GDN_DK ** -0.5)
    kn = _l2n(act[:, D:2 * D], 1.0)
    beta, _, _, g, _ = _gdn_gates(ba, alog_c, dtb_c)
    gcs = _chunk_cumsum(g, GDN_C)
    return qn, kn, act[:, 2 * D:], mm_sel(gcs, ea), mm_sel(beta, eb), jnp.transpose(gcs)[8:16, :]


def gdn_prep_bwd_fn(i, n, qkv, halo, ba, dqn, dkn, dv, dgcs_x, dbeta_x, dgcs_t, cw, alog_c, dtb_c, pb, pa):
    yc, shifted = _conv(qkv, halo, cw, i)
    sg = _sig(yc)
    act = yc * sg
    dq = _l2n_bwd(act[:, :D], dqn, GDN_DK ** -0.5)
    dk = _l2n_bwd(act[:, D:2 * D], dkn, 1.0)
    dyc = jnp.concatenate([dq, dk, dv], axis=1) * (sg * (1.0 + yc * (1.0 - sg)))
    dws = [_rows(dyc * shifted[k]) for k in range(4)]
    beta, z, ea_, g, amask = _gdn_gates(ba, alog_c, dtb_c)
    tbn = ba.shape[0]
    rowpart = jnp.transpose(jnp.concatenate([jnp.zeros((8, tbn), F32), dgcs_t, jnp.zeros((112, tbn), F32)], axis=0))
    dg = _chunk_revcumsum(mm_sel(dgcs_x, pa) - rowpart, GDN_C)
    draw = jnp.where(amask, dg * (-ea_) * _sig(z), 0.0)
    dba = draw + mm_sel(dbeta_x, pb) * beta * (1.0 - beta)
    return (dyc, dba, dws[0], dws[1], dws[2], dws[3], _rows(dg * g), _rows(draw))


def conv_bwd_fn(i, n, dyc, halo, w):
    halo = jnp.where(i == n - 1, 0.0, halo)
    tb = dyc.shape[0]
    xt = jnp.concatenate([dyc, halo], axis=0)
    dx = dyc * w[3:4, :]
    for k in range(3):
        dx = dx + pltpu.roll(xt, tb + 8 - (3 - k), 0)[:tb, :] * w[k:k + 1, :]
    return (dx,)


def gdn_post_fn(i, n, o, z, w):
    outs = []
    for h in range(GDN_H):
        oh, zh = o[:, 128 * h:128 * h + 128], z[:, 128 * h:128 * h + 128]
        r = lax.rsqrt(jnp.mean(oh * oh, axis=-1, keepdims=True) + EPS)
        outs.append(oh * r * w * (zh * _sig(zh)))
    return (jnp.concatenate(outs, axis=1),)


def gdn_post_bwd_fn(i, n, o, z, doa, w):
    dos, dzs, dw = [], [], None
    for h in range(GDN_H):
        sl = slice(128 * h, 128 * h + 128)
        oh, zh, dh = o[:, sl], z[:, sl], doa[:, sl]
        r = lax.rsqrt(jnp.mean(oh * oh, axis=-1, keepdims=True) + EPS)
        s = _sig(zh)
        dn = dh * (zh * s)
        dzs.append(dh * (oh * r * w) * (s * (1.0 + zh * (1.0 - s))))
        t = _rows(dn * oh * r)
        dw = t if dw is None else dw + t
        g = dn * w
        dos.append(r * g - oh * (r * r * r) * jnp.mean(oh * g, axis=-1, keepdims=True))
    return jnp.concatenate(dos, axis=1), jnp.concatenate(dzs, axis=1), dw


def _ssd_gates(dtblk, alog_c, dtb_c):
    hmask = _iota(dtblk.shape, 1) < SSM_H
    z = dtblk + dtb_c
    return jnp.where(hmask, _softplus(z), 0.0), -jnp.exp(alog_c), z, hmask


def ssd_prep_fn(i, n, xp, hx, bcp, hbc, dtblk, cwx, cwbc, cbx, cbbc, alog_c, dtb_c, e16):
    yx, _ = _conv(xp, hx, cwx, i)
    yx = yx + cbx
    ybc, _ = _conv(bcp, hbc, cwbc, i)
    ybc = ybc + cbbc
    dt, a_neg, _, _ = _ssd_gates(dtblk, alog_c, dtb_c)
    acs = _chunk_cumsum(dt * a_neg, SSM_L)
    return (yx * _sig(yx), ybc * _sig(ybc), mm_sel(dt, e16), mm_sel(acs, e16), jnp.transpose(acs)[0:SSM_H, :])


def ssd_prep_bwd_fn(i, n, xp, hx, bcp, hbc, dtblk, dxs_a, dxs_b, db, dc, dgate, dacs_t, cwx, cwbc, cbx, cbbc, alog_c, dtb_c):
    dbc = jnp.concatenate([db, dc], axis=1)
    yx, shx = _conv(xp, hx, cwx, i)
    yx = yx + cbx
    ybc, shbc = _conv(bcp, hbc, cwbc, i)
    ybc = ybc + cbbc
    sx, sbc = _sig(yx), _sig(ybc)
    dyx = (dxs_a + dxs_b) * (sx * (1.0 + yx * (1.0 - sx)))
    dybc = dbc * (sbc * (1.0 + ybc * (1.0 - sbc)))
    dwx = [_rows(dyx * shx[k]) for k in range(4)]
    dwbc = [_rows(dybc * shbc[k]) for k in range(4)]
    dt, a_neg, z, hmask = _ssd_gates(dtblk, alog_c, dtb_c)
    g0, g1 = dgate[:, :128], dgate[:, 128:]
    col = _iota(g0.shape, 1)
    lo, mid = col < 8, (col >= 8) & (col < 16)
    dacs_col = jnp.where(lo, g0, 0.0) + pltpu.roll(jnp.where(lo, g1, 0.0), 8, 1)
    ddt_dir = pltpu.roll(jnp.where(mid, g0, 0.0), 120, 1) + jnp.where(mid, g1, 0.0)
    tbn = dtblk.shape[0]
    rowpart = jnp.transpose(jnp.concatenate([dacs_t, jnp.zeros((128 - SSM_H, tbn), F32)], axis=0))
    da = _chunk_revcumsum(dacs_col - rowpart, SSM_L)
    draw = jnp.where(hmask, (ddt_dir + da * a_neg) * _sig(z), 0.0)
    return (dyx, dybc, draw, *dwx, *dwbc, _rows(dyx), _rows(dybc), _rows(da * dt * a_neg), _rows(draw))


def _ssd_gate(y, xs, zs, d_x):
    y2 = y + xs * d_x
    s = _sig(zs)
    return y2, s, y2 * (zs * s)


def ssd_post_fn(i, n, y, xs, zs, d_x, nw):
    _, _, yg = _ssd_gate(y, xs, zs, d_x)
    outs = []
    for g in range(2):
        v = yg[:, 512 * g:512 * g + 512]
        outs.append(v * lax.rsqrt(jnp.mean(v * v, axis=-1, keepdims=True) + EPS))
    return (jnp.concatenate(outs, axis=1) * nw,)


def ssd_post_bwd_fn(i, n, y, xs, zs, dob, d_x, nw):
    y2, s, yg = _ssd_gate(y, xs, zs, d_x)
    gfull = dob * nw
    dygs, dnw = [], []
    for g in range(2):
        sl = slice(512 * g, 512 * g + 512)
        v, gg = yg[:, sl], gfull[:, sl]
        r = lax.rsqrt(jnp.mean(v * v, axis=-1, keepdims=True) + EPS)
        dygs.append(r * gg - v * (r * r * r) * jnp.mean(v * gg, axis=-1, keepdims=True))
        dnw.append(_rows(dob[:, sl] * v * r))
    dyg = jnp.concatenate(dygs, axis=1)
    dy2 = dyg * (zs * s)
    dzs = dyg * y2 * (s * (1.0 + zs * (1.0 - s)))
    return dy2, dy2 * d_x, dzs, jnp.concatenate(dnw, axis=1), _rows(dy2 * xs)


def attn_fn(i, n, q, k, v):
    outs = []
    for h in range(MEM_H):
        sl = slice(MEM_DH * h, MEM_DH * h + MEM_DH)
        s = mm_nt(q[:, sl], k[:, sl]) * (MEM_DH ** -0.5)
        p = jnp.exp(s - jnp.max(s, axis=-1, keepdims=True))
        p = p / jnp.sum(p, axis=-1, keepdims=True)
        outs.append(mm(p, v[:, sl]))
    return (jnp.concatenate(outs, axis=1),)


def attn_bwd_fn(i, n, q, do, k, v):
    dqs, dks, dvs = [], [], []
    for h in range(MEM_H):
        sl = slice(MEM_DH * h, MEM_DH * h + MEM_DH)
        s = mm_nt(q[:, sl], k[:, sl]) * (MEM_DH ** -0.5)
        p = jnp.exp(s - jnp.max(s, axis=-1, keepdims=True))
        p = p / jnp.sum(p, axis=-1, keepdims=True)
        dvs.append(mm_tn(p, do[:, sl]))
        dp = mm_nt(do[:, sl], v[:, sl])
        ds = p * (dp - jnp.sum(dp * p, axis=-1, keepdims=True)) * (MEM_DH ** -0.5)
        dqs.append(mm(ds, k[:, sl]))
        dks.append(mm_tn(ds, q[:, sl]))
    return jnp.concatenate(dqs, axis=1), jnp.concatenate(dks, axis=1), jnp.concatenate(dvs, axis=1)


def add2_fn(i, n, sp, a, b):
    return (a + b,)


def sum4_fn(i, n, sp, a, b, c, d):
    return (((a.astype(F32) + b.astype(F32)) + c.astype(F32)) + d.astype(F32),)


def _adamw(w, g, m, v):
    m = ADAM_B1 * m + (1.0 - ADAM_B1) * g
    v = ADAM_B2 * v + (1.0 - ADAM_B2) * (g * g)
    m_hat = m / (1.0 - ADAM_B1 ** ADAM_STEP)
    v_hat = v / (1.0 - ADAM_B2 ** ADAM_STEP)
    delta = -ADAM_LR * (m_hat / (jnp.sqrt(v_hat) + ADAM_EPS) + ADAM_WD * w)
    return delta, m, v


def _gdn_stage1(q, k, v, gcs, grow, bb):
    C = GDN_C
    row, col = _iota((C, C), 0), _iota((C, C), 1)
    incl, strict = row >= col, row > col
    dmat = jnp.where(incl, jnp.exp(jnp.minimum(gcs[:, :C] - grow, 0.0)), 0.0)
    gam = jnp.exp(gcs)
    gl = gcs[C - 1:C, :]
    kb, vb = k * bb, v * bb
    kg = kb * gam
    lmat = jnp.where(strict, mm_nt(kb, k) * dmat, 0.0)
    pmat = jnp.where(incl, mm_nt(q, k) * dmat, 0.0)
    return dict(q=q, k=k, v=v, bb=bb, incl=incl, strict=strict, dmat=dmat, gam=gam, kb=kb, vb=vb, kg=kg,
                lmat=lmat, pmat=pmat, qd=q * gam, kdec=jnp.exp(gl - gcs), cd=jnp.exp(gl))


def _gdn_inverse(lmats):
    C = GDN_C
    eye = (_iota((C, C), 0) == _iota((C, C), 1)).astype(F32)
    xs = [-l for l in lmats]
    ts = [eye + x for x in xs]
    for _ in range(5):
        xs = [mm(x, x) for x in xs]
        ts = [t + mm(t, x) for t, x in zip(ts, xs)]
    res = [eye - mm3(eye + l, t) for l, t in zip(lmats, ts)]
    return [t + mm(t, r) for t, r in zip(ts, res)]


def gdn_fwd(qn, kn, v, gcs_x, beta_x, gcs_t, tb, gh):
    T = qn.shape[0]
    nb, ncb, nc, C = T // tb, tb // GDN_C, T // GDN_C, GDN_C
    idx = [(hh, c) for hh in range(gh) for c in range(ncb)]

    def body(q_ref, k_ref, v_ref, g_ref, b_ref, gt_ref, o_ref, st_ref, ti_ref, s_scr):
        @pl.when(pl.program_id(1) == 0)
        def _():
            s_scr[...] = jnp.zeros_like(s_scr)

        grows = [gt_ref[hh] for hh in range(gh)]
        at = lambda hh, c: (slice(C * c, C * (c + 1)), slice(128 * hh, 128 * hh + 128))
        st1 = []
        for hh, c in idx:
            sl, ln = at(hh, c)
            st1.append(_gdn_stage1(q_ref[sl, ln], k_ref[sl, ln], v_ref[sl, ln], g_ref[sl, ln], grows[hh][:, sl],
                                   b_ref[sl, ln]))
        tinvs = _gdn_inverse([s["lmat"] for s in st1])
        us = [mm(t, s["vb"]) for t, s in zip(tinvs, st1)]
        ws = [mm(t, s["kg"]) for t, s in zip(tinvs, st1)]
        kds = [s["k"] * s["kdec"] for s in st1]
        ms = [mm_tn(kd, w) for kd, w in zip(kds, ws)]
        bs = [mm_tn(kd, u) for kd, u in zip(kds, us)]
        gs = [s["qd"] - mm(s["pmat"], w) for s, w in zip(st1, ws)]
        pus = [mm(s["pmat"], u) for s, u in zip(st1, us)]
        ss = [s_scr[hh] for hh in range(gh)]
        for c in range(ncb):
            for hh in range(gh):
                n, (sl, ln) = hh * ncb + c, at(hh, c)
                ti_ref[hh, sl, :] = tinvs[n]
                st_ref[hh, c] = ss[hh]
                o_ref[sl, ln] = mm(gs[n], ss[hh]) + pus[n]
                ss[hh] = st1[n]["cd"] * ss[hh] - mm(ms[n], ss[hh]) + bs[n]
        for hh in range(gh):
            s_scr[hh] = ss[hh]

    blk = pl.BlockSpec((tb, 128 * gh), lambda h, i: (i, h))
    return pl.pallas_call(
        body, name="gdn_fwd", grid=(GDN_H // gh, nb),
        in_specs=[blk] * 5 + [pl.BlockSpec((gh, 1, tb), lambda h, i: (h, 0, i))],
        out_specs=[blk, pl.BlockSpec((gh, ncb, 128, 128), lambda h, i: (h, i, 0, 0)),
                   pl.BlockSpec((gh, tb, C), lambda h, i: (h, i, 0))],
        out_shape=[jax.ShapeDtypeStruct((T, D), F32), jax.ShapeDtypeStruct((GDN_H, nc, 128, 128), F32),
                   jax.ShapeDtypeStruct((GDN_H, T, C), F32)],
        scratch_shapes=[pltpu.VMEM((gh, 128, 128), F32)],
        compiler_params=_cparams(("parallel", "arbitrary")),
    )(qn, kn, v, gcs_x, beta_x, gcs_t)


def gdn_bwd(qn, kn, v, gcs_x, beta_x, gcs_t, do, states, tinv, tb, gh):
    T = qn.shape[0]
    nb, ncb, C = T // tb, tb // GDN_C, GDN_C

    def body(q_ref, k_ref, v_ref, g_ref, b_ref, gt_ref, do_ref, st_ref, ti_ref,
             dq_ref, dk_ref, dv_ref, dgc_ref, db_ref, dgr_ref, ds_scr):
        @pl.when(pl.program_id(1) == 0)
        def _():
            ds_scr[...] = jnp.zeros_like(ds_scr)

        grows = [gt_ref[hh] for hh in range(gh)]
        at = lambda hh, c: (slice(C * c, C * (c + 1)), slice(128 * hh, 128 * hh + 128))
        lastrow = _iota((C, 1), 0) == C - 1
        idx = [(hh, c) for hh in range(gh) for c in range(ncb)]
        P = []
        for hh, c in idx:
            sl, ln = at(hh, c)
            lc = _gdn_stage1(q_ref[sl, ln], k_ref[sl, ln], v_ref[sl, ln], g_ref[sl, ln], grows[hh][:, sl],
                             b_ref[sl, ln])
            lc.update(tinv=ti_ref[hh, sl, :], s=st_ref[hh, c], do=do_ref[sl, ln], kd=lc["k"] * lc["kdec"])
            P.append(lc)
        for l, u, w in zip(P, [mm(l["tinv"], l["vb"]) for l in P], [mm(l["tinv"], l["kg"]) for l in P]):
            l.update(u=u, w=w)
        for l, x in zip(P, [mm(l["w"], l["s"]) for l in P]):
            l["vn"] = l["u"] - x
        for l, a, b, c_, d in zip(P, [mm_nt(l["do"], l["s"]) for l in P], [mm_nt(l["do"], l["vn"]) for l in P],
                                  [mm_tn(l["qd"], l["do"]) for l in P], [mm_tn(l["pmat"], l["do"]) for l in P]):
            l.update(dqd=a, dp=jnp.where(l["incl"], b, 0.0), ds_q=c_, dvn_p=d)
        pre = dict(zip(idx, P))
        rows = {}
        hs = range(gh)
        ds = [ds_scr[hh] for hh in hs]
        for c in reversed(range(ncb)):
            L = [pre[hh, c] for hh in hs]
            dvn = [l["dvn_p"] + mm(l["kd"], d) for l, d in zip(L, ds)]
            dkd = [mm_nt(l["vn"], d) for l, d in zip(L, ds)]
            dcd = [_sum_all(l["s"] * d) for l, d in zip(L, ds)]
            ds = [l["ds_q"] + l["cd"] * d - mm_tn(l["w"], x) for l, d, x in zip(L, ds, dvn)]
            dw = [-mm_nt(x, l["s"]) for l, x in zip(L, dvn)]
            dvb = [mm_tn(l["tinv"], x) for l, x in zip(L, dvn)]
            dkg = [mm_tn(l["tinv"], x) for l, x in zip(L, dw)]
            da = [-jnp.where(l["strict"], mm_nt(a, l["u"]) + mm_nt(b, l["w"]), 0.0) for l, a, b in zip(L, dvb, dkg)]
            dm = [a * l["dmat"] for l, a in zip(L, da)]
            dn = [l["dp"] * l["dmat"] for l in L]
            dkb = [mm(a, l["k"]) for l, a in zip(L, dm)]
            dq = [mm(a, l["k"]) + l["gam"] * l["dqd"] for l, a in zip(L, dn)]
            dk = [mm_tn(a, l["kb"]) + mm_tn(b, l["q"]) for l, a, b in zip(L, dm, dn)]
            for hh in hs:
                sl, ln = at(hh, c)
                l = L[hh]
                e = da[hh] * l["lmat"] + l["dp"] * l["pmat"]
                t_kd = _lanes(dkd[hh] * l["kd"])
                dgl = _sum_all(t_kd) + dcd[hh] * l["cd"][:, :1]
                dgcs = (_lanes(e) + _lanes(l["dqd"] * l["qd"]) - t_kd + _lanes(dkg[hh] * l["kg"])
                        + jnp.where(lastrow, dgl, 0.0))
                rows[hh, c] = _rows(e)
                dq_ref[sl, ln] = dq[hh]
                dk_ref[sl, ln] = (dk[hh] + l["kdec"] * dkd[hh] + l["bb"] * l["gam"] * dkg[hh] + l["bb"] * dkb[hh])
                dv_ref[sl, ln] = l["bb"] * dvb[hh]
                dbeta = _lanes(dkg[hh] * l["gam"] * l["k"]) + _lanes(dvb[hh] * l["v"]) + _lanes(dkb[hh] * l["k"])
                db_ref[sl, ln] = jnp.broadcast_to(dbeta, (C, 128))
                dgc_ref[sl, ln] = jnp.broadcast_to(dgcs, (C, 128))
        for hh in hs:
            ds_scr[hh] = ds[hh]
            dgr_ref[hh] = jnp.concatenate([rows[hh, c] for c in range(ncb)], axis=1)

    blk = pl.BlockSpec((tb, 128 * gh), lambda h, i: (nb - 1 - i, h))
    rowspec = pl.BlockSpec((gh, 1, tb), lambda h, i: (h, 0, nb - 1 - i))
    return pl.pallas_call(
        body, name="gdn_bwd", grid=(GDN_H // gh, nb),
        in_specs=[blk] * 5 + [rowspec, blk,
                              pl.BlockSpec((gh, ncb, 128, 128), lambda h, i: (h, nb - 1 - i, 0, 0)),
                              pl.BlockSpec((gh, tb, C), lambda h, i: (h, nb - 1 - i, 0))],
        out_specs=[blk] * 5 + [rowspec],
        out_shape=[jax.ShapeDtypeStruct((T, D), F32)] * 5 + [jax.ShapeDtypeStruct((GDN_H, 1, T), F32)],
        scratch_shapes=[pltpu.VMEM((gh, 128, 128), F32)],
        compiler_params=_cparams(("parallel", "arbitrary")),
    )(qn, kn, v, gcs_x, beta_x, gcs_t, do, states, tinv)


def _ssd_pair(x2, dt2, acs2):
    last = acs2[SSM_L - 1:SSM_L, :]
    return jnp.exp(acs2), jnp.exp(last - acs2), x2 * dt2


def _ssd_head(hh, acs2, arow, dec2, cbm, bm, incl, col):
    lmask = (col >= 64 * hh) & (col < 64 * hh + 64)
    sg = jnp.where(incl, jnp.exp(jnp.minimum(acs2[:, 64 * hh:64 * hh + 1] - arow, 0.0)), 0.0)
    dec_col = dec2[:, 64 * hh:64 * hh + 1]
    return lmask, sg, sg * cbm, dec_col, bm * dec_col


def ssd_fwd(xs, bc, dt_x, acs_x, acs_t):
    T = xs.shape[0]
    nc, L = T // SSM_L, SSM_L

    def body(x_ref, b_ref, c_ref, dt_ref, ac_ref, at_ref, y_ref, hst_ref, h_scr):
        @pl.when(pl.program_id(1) == 0)
        def _():
            h_scr[...] = jnp.zeros_like(h_scr)

        bm, cm = b_ref[...], c_ref[...]
        cbm = mm_nt(cm, bm)
        row, col = _iota((L, L), 0), _iota((L, L), 1)
        incl = row >= col
        P, H = [], []
        for pr in range(4):
            sl = slice(128 * pr, 128 * pr + 128)
            acs2 = ac_ref[:, sl]
            lam2, dec2, xd2 = _ssd_pair(x_ref[:, sl], dt_ref[:, sl], acs2)
            P.append(dict(sl=sl, lam2=lam2, xd2=xd2, hprev=h_scr[pr]))
            for hh in range(2):
                lmask, _, mmat, _, bd = _ssd_head(hh, acs2, at_ref[2 * pr + hh], dec2, cbm, bm, incl, col)
                H.append(dict(mmat=mmat, bd=bd, xdh=jnp.where(lmask, xd2, 0.0), xd2=xd2))
        ys = [mm(h["mmat"], h["xdh"]) for h in H]
        sts = [mm_tn(h["xd2"], h["bd"]) for h in H]
        zs = [mm_nt(cm, p["hprev"]) for p in P]
        for pr, p in enumerate(P):
            hst_ref[pr] = p["hprev"]
            y_ref[:, p["sl"]] = ys[2 * pr] + ys[2 * pr + 1] + p["lam2"] * zs[pr]
            lam_rows = jnp.where(row < 64, p["lam2"][L - 1:L, 0:1], p["lam2"][L - 1:L, 64:65])
            h_scr[pr] = lam_rows * p["hprev"] + jnp.where(row < 64, sts[2 * pr], sts[2 * pr + 1])

    return pl.pallas_call(
        body, name="ssd_fwd", grid=(2, nc),
        in_specs=[pl.BlockSpec((L, 512), lambda g, c: (c, g)),
                  pl.BlockSpec((L, 128), lambda g, c: (c, g)),
                  pl.BlockSpec((L, 128), lambda g, c: (c, 2 + g)),
                  pl.BlockSpec((L, 512), lambda g, c: (c, g)),
                  pl.BlockSpec((L, 512), lambda g, c: (c, g)),
                  pl.BlockSpec((8, 1, L), lambda g, c: (g, 0, c))],
        out_specs=[pl.BlockSpec((L, 512), lambda g, c: (c, g)),
                   pl.BlockSpec((None, None, 4, 128, 128), lambda g, c: (g, c, 0, 0, 0))],
        out_shape=[jax.ShapeDtypeStruct((T, D), F32), jax.ShapeDtypeStruct((2, nc, 4, 128, 128), F32)],
        scratch_shapes=[pltpu.VMEM((4, 128, 128), F32)],
        compiler_params=_cparams(("parallel", "arbitrary")),
    )(xs, bc, bc, dt_x, acs_x, acs_t)


def ssd_bwd(xs, bc, dt_x, acs_x, acs_t, dy, hstates):
    T = xs.shape[0]
    nc, L = T // SSM_L, SSM_L

    def body(x_ref, b_ref, c_ref, dt_ref, ac_ref, at_ref, dy_ref, hst_ref,
             dx_ref, db_ref, dc_ref, dgate_ref, dar_ref, dh_scr):
        @pl.when(pl.program_id(1) == 0)
        def _():
            dh_scr[...] = jnp.zeros_like(dh_scr)

        bm, cm = b_ref[...], c_ref[...]
        cbm = mm_nt(cm, bm)
        row, col = _iota((L, L), 0), _iota((L, L), 1)
        rowc = _iota((L, 1), 0)
        incl = row >= col
        prs = range(4)
        P = []
        for pr in prs:
            sl = slice(128 * pr, 128 * pr + 128)
            x2, dt2, dy2, acs2 = x_ref[:, sl], dt_ref[:, sl], dy_ref[:, sl], ac_ref[:, sl]
            lam2, dec2, xd2 = _ssd_pair(x2, dt2, acs2)
            P.append(dict(sl=sl, x2=x2, dt2=dt2, dy2=dy2, acs2=acs2, lam2=lam2, dec2=dec2, xd2=xd2,
                          hprev=hst_ref[pr], dhn=dh_scr[pr], dz=lam2 * dy2))
        zs = [mm_nt(cm, p["hprev"]) for p in P]
        dcm_t = [mm(p["dz"], p["hprev"]) for p in P]
        dh_z = [mm_tn(p["dz"], cm) for p in P]
        H = []
        for pr in prs:
            p = P[pr]
            p["yoff"] = p["dz"] * zs[pr]
            p["q_rows"] = _lanes(p["dhn"] * p["hprev"])
            for hh in range(2):
                lmask, sg, mmat, dec_col, bd = _ssd_head(hh, p["acs2"], at_ref[2 * pr + hh], p["dec2"], cbm, bm, incl, col)
                H.append(dict(p=p, hh=hh, j=2 * pr + hh, lmask=lmask, sg=sg, mmat=mmat, dec_col=dec_col, bd=bd))
        dms = [mm_nt(jnp.where(h["lmask"], h["p"]["dy2"], 0.0), h["p"]["xd2"]) for h in H]
        a1s = [mm_tn(h["mmat"], h["p"]["dy2"]) for h in H]
        a2s = [mm_nt(h["bd"], h["p"]["dhn"]) for h in H]
        dbds = [mm(jnp.where(h["lmask"], h["p"]["xd2"], 0.0), h["p"]["dhn"]) for h in H]
        dcb = jnp.zeros((L, L), F32)
        dbm = jnp.zeros((L, SSM_N), F32)
        comp = jnp.zeros((L, 128), F32)
        dxd = [jnp.zeros((L, 128), F32) for _ in prs]
        for h, dm_raw, a1, a2, dbd in zip(H, dms, a1s, a2s, dbds):
            p, hh, j = h["p"], h["hh"], h["j"]
            dm = jnp.where(incl, dm_raw, 0.0)
            dcb = dcb + dm * h["sg"]
            e = dm * h["mmat"]
            dxd_h = jnp.where(h["lmask"], a1 + a2, 0.0)
            dxd[j // 2] = dxd[j // 2] + dxd_h
            dbm = dbm + h["dec_col"] * dbd
            t = _lanes(dbd * h["bd"])
            lam_h = p["lam2"][L - 1:L, 64 * hh:64 * hh + 1]
            in_head = (rowc >= 64 * hh) & (rowc < 64 * hh + 64)
            add_last = _sum_all(t) + _sum_all(jnp.where(in_head, p["q_rows"], 0.0)) * lam_h
            dacs_col = (_lanes(jnp.where(h["lmask"], p["yoff"], 0.0)) + _lanes(e) - t
                        + jnp.where(rowc == L - 1, add_last, 0.0))
            ddt_col = _lanes(dxd_h * p["x2"])
            dar_ref[j] = _rows(e)
            comp = comp + jnp.where(col == j, dacs_col, 0.0) + jnp.where(col == 8 + j, ddt_col, 0.0)
        dcm = dcm_t[0]
        for pr in prs:
            p = P[pr]
            if pr:
                dcm = dcm + dcm_t[pr]
            lam_rows = jnp.where(row < 64, p["lam2"][L - 1:L, 0:1], p["lam2"][L - 1:L, 64:65])
            dh_scr[pr] = dh_z[pr] + lam_rows * p["dhn"]
            dx_ref[:, p["sl"]] = p["dt2"] * dxd[pr]
        db_ref[...] = dbm + mm_tn(dcb, cm)
        dc_ref[...] = dcm + mm(dcb, bm)
        dgate_ref[...] = comp

    rv = lambda g, c: (nc - 1 - c, g)
    rowspec = pl.BlockSpec((8, 1, L), lambda g, c: (g, 0, nc - 1 - c))
    return pl.pallas_call(
        body, name="ssd_bwd", grid=(2, nc),
        in_specs=[pl.BlockSpec((L, 512), rv),
                  pl.BlockSpec((L, 128), rv),
                  pl.BlockSpec((L, 128), lambda g, c: (nc - 1 - c, 2 + g)),
                  pl.BlockSpec((L, 512), rv),
                  pl.BlockSpec((L, 512), rv),
                  rowspec,
                  pl.BlockSpec((L, 512), rv),
                  pl.BlockSpec((None, None, 4, 128, 128), lambda g, c: (g, nc - 1 - c, 0, 0, 0))],
        out_specs=[pl.BlockSpec((L, 512), rv), pl.BlockSpec((L, 128), rv), pl.BlockSpec((L, 128), rv),
                   pl.BlockSpec((L, 128), rv), rowspec],
        out_shape=[jax.ShapeDtypeStruct((T, D), F32), jax.ShapeDtypeStruct((T, 256), F32),
                   jax.ShapeDtypeStruct((T, 256), F32), jax.ShapeDtypeStruct((T, 256), F32),
                   jax.ShapeDtypeStruct((SSM_H, 1, T), F32)],
        scratch_shapes=[pltpu.VMEM((4, 128, 128), F32)],
        compiler_params=_cparams(("parallel", "arbitrary")),
    )(xs, bc, bc, dt_x, acs_x, acs_t, dy, hstates)


def _pos():
    return lax.axis_index("x"), lax.axis_index("y"), lax.axis_index("c")


def _other_chips(x, y):
    return [(1 - x, y), (x, 1 - y), (1 - x, 1 - y)]


def _rcopy(src, dst, ssem, rsem, dev):
    return pltpu.make_async_remote_copy(src_ref=src, dst_ref=dst, send_sem=ssem, recv_sem=rsem,
                                        device_id=dev, device_id_type=MESH)


def _rows_at(start, n):
    return pl.ds(pl.multiple_of(start, 8), n)


def _comm_call(body, name, out_shape, n_in, scratch):
    return pl.pallas_call(
        body, name=name, out_shape=out_shape, in_specs=[ANY] * n_in,
        out_specs=[ANY] * len(out_shape) if isinstance(out_shape, (list, tuple)) else ANY,
        scratch_shapes=scratch,
        compiler_params=pltpu.CompilerParams(has_side_effects=True),
    )


def _dma_sems(n):
    return pltpu.SemaphoreType.DMA((n,))


def ag_chips(name, shard):
    rr, cc = shard.shape
    h, nq = rr // 2, ICI_CHUNKS
    hq = h // nq

    def body(x_ref, out_ref, ssem, rsem):
        x, y, c = _pos()
        chips = _other_chips(x, y)
        started = []
        for q in range(nq):
            rows = _rows_at(c * h + q * hq, hq)
            for j, (cx, cy) in enumerate(chips):
                cp = _rcopy(x_ref.at[rows], out_ref.at[j, rows], ssem.at[j * nq + q], rsem.at[j * nq + q], (cx, cy, c))
                cp.start()
                started.append(cp)
        for q in range(nq):
            rows = _rows_at(c * h + q * hq, hq)
            for j, (cx, cy) in enumerate(chips):
                blk = out_ref.at[j, rows]
                _rcopy(blk, blk, ssem.at[j * nq + q], rsem.at[j * nq + q], (cx, cy, c)).wait_recv()
                k = 3 * nq + j * nq + q
                cp = _rcopy(blk, blk, ssem.at[k], rsem.at[k], (x, y, 1 - c))
                cp.start()
                started.append(cp)
        for q in range(nq):
            rows = _rows_at((1 - c) * h + q * hq, hq)
            for j in range(3):
                blk = out_ref.at[j, rows]
                k = 3 * nq + j * nq + q
                _rcopy(blk, blk, ssem.at[k], rsem.at[k], (x, y, 1 - c)).wait_recv()
        for cp in started:
            cp.wait_send()

    return _comm_call(body, name, jax.ShapeDtypeStruct((3, rr, cc), shard.dtype), 1,
                      [_dma_sems(6 * nq), _dma_sems(6 * nq)])(shard)


def _by_chip(shard, got, s_me):
    by_rel = jnp.stack([shard, got[1], got[0], got[2]])
    return jnp.take(by_rel, jnp.arange(4) ^ s_me, axis=0)


def all_gather_chips(name, shard, s_me):
    return _by_chip(shard, ag_chips(name, shard), s_me)


HBM_SPEC = pl.BlockSpec(memory_space=pltpu.HBM)
SEM_SPEC = pl.BlockSpec(memory_space=pltpu.SEMAPHORE)
SPLIT_EFFECT = pltpu.SideEffectType.DATAFLOW_SIDE_EFFECTING


def _split_copies(pieces, x_ref, land_ref, sems):
    x, y, c = _pos()
    return [_rcopy(s, d, sems[j], sems[3 + j], dev) for j, (s, d, dev) in enumerate(pieces(x_ref, land_ref, x, y, c))]


def split_copy_start(name, src, land_shape, pieces):
    def body(x_ref, land_ref, *outs):
        for cp in _split_copies(pieces, x_ref, land_ref, outs[:6]):
            cp.start()
        outs[8][...] = jnp.zeros_like(outs[8])

    dma = pltpu.SemaphoreType.DMA(())
    res = pl.pallas_call(
        body, name=name,
        out_shape=(dma,) * 6 + (pltpu.HBM(src.shape, src.dtype), pltpu.HBM(land_shape, src.dtype),
                                jax.ShapeDtypeStruct((8, 128), F32)),
        in_specs=(HBM_SPEC, HBM_SPEC),
        out_specs=(SEM_SPEC,) * 6 + (HBM_SPEC, HBM_SPEC, pl.BlockSpec(memory_space=pltpu.VMEM)),
        input_output_aliases={0: 6, 1: 7},
        compiler_params=pltpu.CompilerParams(has_side_effects=SPLIT_EFFECT),
    )(pltpu.with_memory_space_constraint(src, pltpu.HBM),
      pltpu.with_memory_space_constraint(lax.empty(land_shape, src.dtype), pltpu.HBM))
    return res[:6], res[6], res[7], res[8]


def split_copy_wait(name, sems, src_thru, land_thru, after, pieces):
    def body(x_ref, land_ref, *rest):
        for cp in _split_copies(pieces, x_ref, land_ref, rest[:6]):
            cp.wait_send()
            cp.wait_recv()

    return pl.pallas_call(
        body, name=name,
        out_shape=(pltpu.HBM(src_thru.shape, src_thru.dtype), pltpu.HBM(land_thru.shape, land_thru.dtype)),
        in_specs=(HBM_SPEC, HBM_SPEC) + (SEM_SPEC,) * 6 + (ANY,), out_specs=(HBM_SPEC, HBM_SPEC),
        input_output_aliases={0: 0, 1: 1},
        compiler_params=pltpu.CompilerParams(has_side_effects=SPLIT_EFFECT),
    )(src_thru, land_thru, *sems, after)


def ag_pieces(h):
    def pieces(x_ref, land_ref, x, y, c):
        rows = _rows_at(c * h, h)
        return [(x_ref.at[rows], land_ref.at[j, rows], (cx, cy, c)) for j, (cx, cy) in enumerate(_other_chips(x, y))]
    return pieces


def rs_pieces(x_ref, land_ref, x, y, c):
    return [(x_ref.at[2 * cx + cy], land_ref.at[j], (cx, cy, c)) for j, (cx, cy) in enumerate(_other_chips(x, y))]


def ag_forward(name, got):
    _, rr, cc = got.shape
    h, nq = rr // 2, D2D_CHUNKS
    hq = h // nq

    def body(g_ref, out_ref, ssem, rsem):
        x, y, c = _pos()
        cps = []
        for j in range(3):
            for q in range(nq):
                blk = out_ref.at[j, _rows_at(c * h + q * hq, hq)]
                cp = _rcopy(blk, blk, ssem.at[j * nq + q], rsem.at[j * nq + q], (x, y, 1 - c))
                cp.start()
                cps.append(cp)
        for cp in cps:
            cp.wait_send()
        for j in range(3):
            for q in range(nq):
                blk = out_ref.at[j, _rows_at((1 - c) * h + q * hq, hq)]
                _rcopy(blk, blk, ssem.at[j * nq + q], rsem.at[j * nq + q], (x, y, 1 - c)).wait_recv()

    return pl.pallas_call(
        body, name=name, out_shape=jax.ShapeDtypeStruct(got.shape, got.dtype), in_specs=[ANY], out_specs=ANY,
        scratch_shapes=[_dma_sems(3 * nq), _dma_sems(3 * nq)], input_output_aliases={0: 0},
        compiler_params=pltpu.CompilerParams(has_side_effects=True),
    )(got)


def rs_pair(name, g):
    _, rr, cc = g.shape
    h, nq = rr // 2, D2D_CHUNKS
    hq = h // nq

    def body(g_ref, recv_ref, ssem, rsem):
        x, y, c = _pos()
        cps = []
        for q in range(nq):
            cp = _rcopy(g_ref.at[:, _rows_at((1 - c) * h + q * hq, hq), :], recv_ref.at[:, pl.ds(q * hq, hq), :],
                        ssem.at[q], rsem.at[q], (x, y, 1 - c))
            cp.start()
            cps.append(cp)
        for cp in cps:
            cp.wait()

    return _comm_call(body, name, jax.ShapeDtypeStruct((4, h, cc), g.dtype), 1, [_dma_sems(nq), _dma_sems(nq)])(g)


def rs_chips(name, p):
    _, h, cc = p.shape
    nq = ICI_CHUNKS
    hq = h // nq

    def body(p_ref, buf_ref, ssem, rsem):
        x, y, c = _pos()
        sends = []
        for q in range(nq):
            rows = pl.ds(q * hq, hq)
            for j, (cx, cy) in enumerate(_other_chips(x, y)):
                cp = _rcopy(p_ref.at[2 * cx + cy, rows], buf_ref.at[j, rows], ssem.at[j * nq + q],
                            rsem.at[j * nq + q], (cx, cy, c))
                cp.start()
                sends.append(cp)
        for cp in sends:
            cp.wait()

    return _comm_call(body, name, jax.ShapeDtypeStruct((3, h, cc), p.dtype), 1,
                      [_dma_sems(3 * nq), _dma_sems(3 * nq)])(p)


def rs_join(name, half):
    h, cc = half.shape
    nq = D2D_CHUNKS
    hq = h // nq

    def body(h_ref, out_ref, ssem, rsem):
        x, y, c = _pos()
        cps = []
        for q in range(nq):
            rows = pl.ds(q * hq, hq)
            cp = _rcopy(h_ref.at[rows], out_ref.at[rows], ssem.at[q], rsem.at[q], (x, y, 1 - c))
            cp.start()
            cps.append(cp)
        for cp in cps:
            cp.wait()

    return _comm_call(body, name, jax.ShapeDtypeStruct((h, cc), half.dtype), 1, [_dma_sems(nq), _dma_sems(nq)])(half)


def reduce_scatter(tag, g, tb, sp):
    return rs_end(rs_begin(tag, g, tb, sp, False), None)


def rs_begin(tag, g, tb, sp, split):
    _, rr, cc = g.shape
    h = rr // 2
    nbh = h // tb
    recv = rs_pair(tag + "_pair", g)
    mine_rows = lambda i, s: (i // nbh) * (2 * nbh) + s[0] * nbh + i % nbh
    part = rowwise(add2_fn, tag + "_add", 4 * h, tb, [R(g.reshape(4 * rr, cc), off=mine_rows), R(recv.reshape(4 * h, cc))],
                   [], [(cc, BF16)], sp=sp)[0].reshape(4, h, cc)
    st = dict(tag=tag, tb=tb, sp=sp, split=split, part=part)
    if split:
        st["sems"], st["part"], st["land"], st["token"] = split_copy_start(tag + "_start", part, (3, h, cc), rs_pieces)
    return st


def rs_end(st, after):
    tag, tb, sp, part = st["tag"], st["tb"], st["sp"], st["part"]
    _, h, cc = part.shape
    nbh = h // tb
    if st["split"]:
        part, buf = split_copy_wait(tag + "_wait", st["sems"], part, st["land"], after, rs_pieces)
    else:
        buf = rs_chips(tag + "_chips", part)
    red = rowwise(sum4_fn, tag + "_sum", h, tb,
                  [R(part.reshape(4 * h, cc), off=lambda i, s: s[1] * nbh + i)]
                  + [R(buf.reshape(3 * h, cc), off=k * nbh) for k in range(3)],
                  [], [(cc, F32)], sp=sp)[0]
    return red, rs_join(tag + "_join", red)


def adam_halves(name, w, m, v, red, other, tb, blk0, sp):
    nbh = red.shape[0] // tb

    def fn(i, n, s, w_, m_, v_, r_, o_):
        g = jnp.where((blk0 + i) // nbh == s[0], r_, o_)
        return (g,) + _adamw(w_, g, m_, v_)

    half_rows = lambda i, s: (blk0 + i) % nbh
    return rowwise(fn, name, w.shape[0], tb, [R(w), R(m), R(v), R(red, off=half_rows), R(other, off=half_rows)],
                   [], [(w.shape[1], F32)] * 4, sp=sp)


SMALL_LANES = 3 * D


def all_reduce_items(name, items):
    flat = [a for it in items for a in it]
    shapes = [(sum(a.shape[0] for a in it), it[0].shape[1]) for it in items]
    nrows = -(-sum(s[0] for s in shapes) // 8) * 8

    def body(*refs):
        ins, outs = refs[:len(flat)], refs[len(flat):len(flat) + len(items)]
        mine, buf, ssem, rsem = refs[len(flat) + len(items):]
        x, y, c = _pos()
        me = 4 * x + 2 * y + c
        mine[...] = jnp.zeros_like(mine)
        r = 0
        for ref in ins:
            mine[r:r + ref.shape[0], 0:ref.shape[1]] = ref[...]
            r += ref.shape[0]
        buf[me] = mine[...]
        cps = []
        for k in range(1, 8):
            dev = (x ^ (k >> 2), y ^ ((k >> 1) & 1), c ^ (k & 1))
            cp = _rcopy(mine, buf.at[me], ssem.at[k - 1], rsem.at[k - 1], dev)
            cp.start()
            cps.append(cp)
        for cp in cps:
            cp.wait()
        r = 0
        for (nr, n), out in zip(shapes, outs):
            acc = buf[0, r:r + nr, 0:n]
            for d in range(1, 8):
                acc = acc + buf[d, r:r + nr, 0:n]
            out[...] = acc
            r += nr

    vm = pl.BlockSpec(memory_space=pltpu.VMEM)
    return pl.pallas_call(
        body, name=name, out_shape=[jax.ShapeDtypeStruct(s, F32) for s in shapes],
        in_specs=[vm] * len(flat), out_specs=[vm] * len(items),
        scratch_shapes=[pltpu.VMEM((nrows, SMALL_LANES), F32), pltpu.VMEM((8, nrows, SMALL_LANES), F32),
                        _dma_sems(7), _dma_sems(7)],
        compiler_params=pltpu.CompilerParams(has_side_effects=True),
    )(*flat)


def adam_small(ws, gs, ms, vs):
    n = len(ws)

    def body(*refs):
        for k in range(n):
            w, g, m, v = (refs[j * n + k][...] for j in range(4))
            for j, val in enumerate(_adamw(w, g, m, v)):
                refs[(4 + j) * n + k][...] = val

    vm = pl.BlockSpec(memory_space=pltpu.VMEM)
    res = pl.pallas_call(
        body, name="adam_small", out_shape=[jax.ShapeDtypeStruct(w.shape, F32) for w in ws] * 3,
        in_specs=[vm] * (4 * n), out_specs=[vm] * (3 * n),
    )(*ws, *gs, *ms, *vs)
    return res[:n], res[n:2 * n], res[2 * n:]


def _sel(rows, cols, pairs):
    m = np.zeros((rows, cols), np.float32)
    for r, c in pairs:
        m[r, c] = 1.0
    return jnp.asarray(m)


def _pad_win(w):
    z = jnp.zeros((w.shape[0], 112), w.dtype)
    return jnp.concatenate([w[:, :4096], w[:, 4112:6672], w[:, 4096:4112], z, w[:, 6672:6688], z], axis=1)


def _unpad_win(wp):
    return jnp.concatenate([wp[:, :4096], wp[:, 6656:6672], wp[:, 4096:6656], wp[:, 6784:6800]], axis=1)


def kernel(x, mem, norm1_w, w_in, gdn_conv_w, gdn_a_log, gdn_dt_bias, gdn_norm_w, ssm_conv_w, ssm_conv_b, ssm_a_log, ssm_dt_bias, ssm_d, ssm_norm_w, w_out, norm2_w, mem_norm_w, wq_mem, wk_mem, wv_mem, wo_mem, norm3_w, w_up, w_down, final_norm_w, loss_target, m_norm1_w, m_w_in, m_gdn_conv_w, m_gdn_a_log, m_gdn_dt_bias, m_gdn_norm_w, m_ssm_conv_w, m_ssm_conv_b, m_ssm_a_log, m_ssm_dt_bias, m_ssm_d, m_ssm_norm_w, m_w_out, m_norm2_w, m_mem_norm_w, m_wq_mem, m_wk_mem, m_wv_mem, m_wo_mem, m_norm3_w, m_w_up, m_w_down, m_final_norm_w, v_norm1_w, v_w_in, v_gdn_conv_w, v_gdn_a_log, v_gdn_dt_bias, v_gdn_norm_w, v_ssm_conv_w, v_ssm_conv_b, v_ssm_a_log, v_ssm_dt_bias, v_ssm_d, v_ssm_norm_w, v_w_out, v_norm2_w, v_mem_norm_w, v_wq_mem, v_wk_mem, v_wv_mem, v_wo_mem, v_norm3_w, v_w_up, v_w_down, v_final_norm_w):
    T, M = x.shape[1], mem.shape[1]
    xi, yi, ci = _pos()
    s_me = 2 * xi + yi
    x0, mem0, tgt = x[0], mem[0], loss_target[0]
    tb = min(256, T)
    row = lambda v: v.reshape(1, -1)

    win_g = all_gather_chips("ag_win", w_in.astype(BF16), s_me)
    rest_shard = jnp.concatenate([w_up, w_down, w_out, wq_mem, wk_mem, wv_mem, wo_mem], axis=0).astype(BF16)
    ag_sems, rest_thru, rest_land, ag_token = split_copy_start("ag_rest_start", rest_shard, (3,) + rest_shard.shape,
                                                               ag_pieces(rest_shard.shape[0] // 2))
    w_in_p = _pad_win(win_g.transpose(1, 0, 2).reshape(D, IN_COLS))
    keep = (ci == 0).astype(F32)
    gcw_z = lax.dynamic_update_slice(jnp.zeros((4, 3 * D), F32), gdn_conv_w * keep, (0, s_me * 768))
    scw_z = lax.dynamic_update_slice(jnp.zeros((4, 1536), F32), ssm_conv_w * keep, (0, s_me * 384))
    gcw, scw = all_reduce_items("ar_convw", [[gcw_z], [scw_z]])
    scw_x, scw_bc = scw[:, :D], scw[:, D:]
    sp = jnp.stack([ci, s_me]).astype(jnp.int32)
    scb_x, scb_bc = row(ssm_conv_b[:D]), row(ssm_conv_b[D:])

    galog_c, gdtb_c = row(jnp.pad(gdn_a_log, (8, 112))), row(jnp.pad(gdn_dt_bias, (8, 112)))
    salog_c, sdtb_c = row(jnp.pad(ssm_a_log, (0, 112))), row(jnp.pad(ssm_dt_bias, (0, 112)))
    sd_x = row(jnp.repeat(ssm_d, 64))
    eb = _sel(128, D, [(h, 128 * h + l) for h in range(8) for l in range(128)])
    ea = _sel(128, D, [(8 + h, 128 * h + l) for h in range(8) for l in range(128)])
    e16 = _sel(128, D, [(h, 64 * h + l) for h in range(16) for l in range(64)])
    pb = _sel(D, 128, [(128 * h, h) for h in range(8)])
    pa = _sel(D, 128, [(128 * h, 8 + h) for h in range(8)])

    h1 = rowwise(rms_fwd_fn, "rms1", T, tb, [R(x0)], [row(norm1_w) + ag_token[0:1, 0:1]], [(D, BF16)])[0]
    p = matmul("mm_in", h1, w_in_p, "nn", 1024, 768, 1024, [F32])[0]
    gp_ins = [R(p, 3 * D, CB_QKV, "prev"), R(p, 128, CB_BA)]
    qn, kn, vv, gcs_x, beta_x, gcs_t = rowwise(gdn_prep_fn, "gdn_prep", T, tb, gp_ins,
                                               [gcw, galog_c, gdtb_c, eb, ea], [(D, F32)] * 5 + [(-8, F32)])
    gcs_t = gcs_t.reshape(GDN_H, 1, T)
    gtb, ggh = min(128, T), 4
    o_gdn, s_states, tinv = gdn_fwd(qn, kn, vv, gcs_x, beta_x, gcs_t, gtb, ggh)
    gnw = row(gdn_norm_w)
    oa = rowwise(gdn_post_fn, "gdn_post", T, tb, [R(o_gdn), R(p, D, CB_Z)], [gnw], [(D, BF16)])[0]
    sp_ins = [R(p, D, CB_XS, "prev"), R(p, 512, CB_BC, "prev"), R(p, 128, CB_DT)]
    sp_full = [scw_x, scw_bc, scb_x, scb_bc, salog_c, sdtb_c]
    xs, bc, dt_x, acs_x, acs_t = rowwise(ssd_prep_fn, "ssd_prep", T, tb, sp_ins, sp_full + [e16],
                                         [(D, F32), (512, F32), (D, F32), (D, F32), (-SSM_H, F32)])
    acs_t = acs_t.reshape(SSM_H, 1, T)
    y_ssd, h_states = ssd_fwd(xs, bc, dt_x, acs_x, acs_t)
    snw = row(ssm_norm_w)
    ob = rowwise(ssd_post_fn, "ssd_post", T, tb, [R(y_ssd), R(xs), R(p, D, CB_ZS)], [sd_x, snw], [(D, BF16)])[0]
    rest_thru, rest_land = split_copy_wait("ag_rest_wait", ag_sems, rest_thru, rest_land, ob,
                                           ag_pieces(rest_shard.shape[0] // 2))
    rest_g = _by_chip(rest_thru, ag_forward("ag_rest_fwd", rest_land), s_me)
    wup_f = rest_g[:, 0:1024].transpose(1, 0, 2).reshape(D, D_FF)
    wdown_f = rest_g[:, 1024:2048].reshape(D_FF, D)
    wout_f = rest_g[:, 2048:2560].reshape(2 * D, D)
    wq_f, wk_f, wv_f, wo_f = (rest_g[:, 2560 + 256 * k:2816 + 256 * k].reshape(D, D) for k in range(4))
    x1a = matmul("mm_out_a", oa, wout_f[:D], "nn", 1024, 1024, 1024, [F32], _epi_res, [x0])[0]
    assert D == 1024
    x1, h2 = matmul("mm_out_b", ob, wout_f[D:], "nn", 1024, 1024, 1024, [F32, BF16], _epi_res_rms, [x1a],
                    [row(norm2_w)])

    mn = rowwise(rms_fwd_fn, "rms_mem", M, M, [R(mem0)], [row(mem_norm_w)], [(D, BF16)])[0]
    km = matmul("mm_k", mn, wk_f, "nn", 256, 1024, 1024, [BF16])[0]
    vm = matmul("mm_v", mn, wv_f, "nn", 256, 1024, 1024, [BF16])[0]
    qm = matmul("mm_q", h2, wq_f, "nn", 1024, 1024, 1024, [BF16])[0]
    ao = rowwise(attn_fn, "attn", T, tb, [R(qm)], [km, vm], [(D, BF16)])[0]
    x2, h3 = matmul("mm_o", ao, wo_f, "nn", 1024, 1024, 1024, [F32, BF16], _epi_res_rms, [x1], [row(norm3_w)])
    u, act = matmul("mm_up", h3, wup_f, "nn", 1024, 1024, 1024, [BF16, BF16], _epi_relu2)
    dx3, dx3b, loss_lane, g_final = matmul("mm_down", act, wdown_f, "nn", 512, 1024, 1024, [F32, BF16], _epi_final,
                                           [x2, tgt], [row(final_norm_w)], n_acc=2)
    loss = lax.psum(0.5 / D * jnp.sum(loss_lane), ("x", "y", "c"))

    dup = matmul("mm_dact", dx3b, wdown_f, "nt", 1024, 1024, 1024, [BF16], _epi_dup, [u])[0]
    def g_into(buf, blk, at):
        return dict(into=(buf, blk, lambda i, j, k, at=at: at(i, j)))

    grest = jax.ShapeDtypeStruct((4, 3584, D), F32)
    grest = matmul("mm_gdown", act, dx3b, "tn", 1024, 1024, 1024, [F32],
                   **g_into(grest, (None, 1024, D), lambda i, j: (i, 1, 0)))
    dx2, dx2b, g_n3 = matmul("mm_dh3", dup, wup_f, "nt", 512, 1024, 1024, [F32, BF16], _epi_rms_bwd, [x2, dx3],
                             [row(norm3_w)], n_acc=1)
    grest = matmul("mm_gup", h3, dup, "tn", 1024, 1024, 1024, [F32],
                   **g_into(grest, (None, 1024, D), lambda i, j: (j, 0, 0)))
    dao = matmul("mm_dao", dx2b, wo_f, "nt", 1024, 1024, 1024, [F32])[0]
    grest = matmul("mm_gwo", ao, dx2b, "tn", 1024, 1024, 1024, [F32],
                   **g_into(grest, (4, 256, D), lambda i, j: (0, 13, 0)))
    dqm, dkm, dvm = rowwise(attn_bwd_fn, "attn_bwd", T, tb, [R(qm), R(dao)], [km, vm], [(D, BF16)],
                            [(M, D), (M, D)])
    dx1, dx1b, g_n2 = matmul("mm_dh2", dqm, wq_f, "nt", 512, 1024, 1024, [F32, BF16], _epi_rms_bwd, [x1, dx2],
                             [row(norm2_w)], n_acc=1)
    grest = matmul("mm_gwq", h2, dqm, "tn", 1024, 1024, 1024, [F32],
                   **g_into(grest, (4, 256, D), lambda i, j: (0, 10, 0)))
    grest = matmul("mm_gwk", mn, dkm, "tn", 1024, 1024, 256, [F32],
                   **g_into(grest, (4, 256, D), lambda i, j: (0, 11, 0)))
    grest = matmul("mm_gwv", mn, dvm, "tn", 1024, 1024, 256, [F32],
                   **g_into(grest, (4, 256, D), lambda i, j: (0, 12, 0)))
    dmn_k = matmul("mm_dmk", dkm, wk_f, "nt", 256, 1024, 1024, [F32])[0]
    dmn = matmul("mm_dmv", dvm, wv_f, "nt", 256, 1024, 1024, [F32], _epi_res, [dmn_k])[0]
    g_nmem = rowwise(rms_bwd_w_fn, "rmsmem_bwd", M, M, [R(mem0), R(dmn)], [row(mem_norm_w)], [], [(1, D)])[0]
    doa = matmul("mm_doa", dx1b, wout_f[:D], "nt", 1024, 1024, 1024, [F32])[0]
    dob = matmul("mm_dob", dx1b, wout_f[D:], "nt", 1024, 1024, 1024, [F32])[0]
    grest = matmul("mm_gwout_a", oa, dx1b, "tn", 1024, 1024, 1024, [F32],
                   **g_into(grest, (2, 512, D), lambda i, j: (0, 4, 0)))
    grest = matmul("mm_gwout_b", ob, dx1b, "tn", 1024, 1024, 1024, [F32],
                   **g_into(grest, (2, 512, D), lambda i, j: (1, 4, 0)))

    rs_rest = rs_begin("rs_rest", grest, 256, sp, True)

    dp = jax.ShapeDtypeStruct((T, p.shape[1]), BF16)
    dy_ssd, dxs_dir, dp, g_snw, g_sd_lane = rowwise(
        ssd_post_bwd_fn, "ssd_post_bwd", T, tb, [R(y_ssd), R(xs), R(p, D, CB_ZS), R(dob)],
        [sd_x + rs_rest["token"][0:1, 0:1], snw],
        [(D, F32), (D, F32), (D, BF16, dp, CB_ZS)], [(1, D), (1, D)])
    dxs_scan, db_s, dc_s, dgate, dacs_t = ssd_bwd(xs, bc, dt_x, acs_x, acs_t, dy_ssd, h_states)
    spb = rowwise(ssd_prep_bwd_fn, "ssd_prep_bwd", T, tb,
                  sp_ins + [R(dxs_scan), R(dxs_dir), R(db_s), R(dc_s), R(dgate), RC(dacs_t.reshape(SSM_H, T))], sp_full,
                  [(D, F32), (512, F32), (128, BF16, dp, CB_DT)],
                  [(1, D)] * 4 + [(1, 512)] * 4 + [(1, D), (1, 512), (1, 128), (1, 128)])
    dyc_x, dyc_bc, dp = spb[:3]
    dp = rowwise(conv_bwd_fn, "conv_bwd_x", T, tb, [R(dyc_x, halo="next")], [scw_x], [(D, BF16, dp, CB_XS)])[0]
    dp = rowwise(conv_bwd_fn, "conv_bwd_bc", T, tb, [R(dyc_bc, halo="next")], [scw_bc], [(512, BF16, dp, CB_BC)])[0]

    do_gdn, dp, g_gnw = rowwise(gdn_post_bwd_fn, "gdn_post_bwd", T, tb, [R(o_gdn), R(p, D, CB_Z), R(doa)], [gnw],
                                [(D, F32), (D, BF16, dp, CB_Z)], [(1, 128)])
    dqn, dkn, dvv, dgcs_x, dbeta_x, dgcs_t = gdn_bwd(qn, kn, vv, gcs_x, beta_x, gcs_t, do_gdn, s_states, tinv, gtb, ggh)
    gpb = rowwise(gdn_prep_bwd_fn, "gdn_prep_bwd", T, tb,
                  gp_ins + [R(dqn), R(dkn), R(dvv), R(dgcs_x), R(dbeta_x), RC(dgcs_t.reshape(GDN_H, T))],
                  [gcw, galog_c, gdtb_c, pb, pa],
                  [(3 * D, F32), (128, BF16, dp, CB_BA)], [(1, 3 * D)] * 4 + [(1, 128), (1, 128)])
    dyc_qkv, dp = gpb[:2]
    dp = rowwise(conv_bwd_fn, "conv_bwd_qkv", T, tb, [R(dyc_qkv, halo="next")], [gcw], [(3 * D, BF16, dp, CB_QKV)])[0]
    grad_x, g_n1 = matmul("mm_dh1", dp, w_in_p, "nt", 512, 1024, 768, [F32], _epi_rms_bwd1, [x0, dx1],
                          [row(norm1_w)], n_acc=1)
    g_win_p = matmul("mm_gwin", h1, dp, "tn", 1024, 768, 1024, [F32])[0]

    items = [[g_n1], [gpb[6]], [gpb[7]], [g_gnw], [spb[11]], [spb[12]], [spb[13]], [spb[14]], [g_sd_lane], [g_snw],
             [g_n2], [g_nmem], [g_n3], [g_final], list(gpb[2:6]), list(spb[3:7]), list(spb[7:11])]
    (gr_n1, r_galog, r_gdtb, gr_gnw, r_scb_x, r_scb_bc, r_salog, r_sdtb, r_sd, gr_snw, gr_n2, gr_nmem, gr_n3,
     gr_final, r_gcw, r_scw_x, r_scw_bc) = all_reduce_items("ar_grads", items)
    gr_galog, gr_gdtb = r_galog[:, 8:16], r_gdtb[:, 8:16]
    gr_salog, gr_sdtb = r_salog[:, :SSM_H], r_sdtb[:, :SSM_H]
    gr_sd = r_sd.reshape(SSM_H, SSM_P).sum(axis=1).reshape(1, SSM_H)
    gr_scb = jnp.concatenate([r_scb_x, r_scb_bc], axis=1)
    gr_gcw = lax.dynamic_slice(r_gcw, (0, s_me * 768), (4, 768))
    gr_scw = lax.dynamic_slice(jnp.concatenate([r_scw_x, r_scw_bc], axis=1), (0, s_me * 384), (4, 384))

    g_win = _unpad_win(g_win_p).reshape(D, 4, IN_COLS // 4).transpose(1, 0, 2)
    red_r, oth_r = rs_end(rs_rest, g_win_p)
    red_w, oth_w = reduce_scatter("rs_win", g_win, 256, sp)

    big = {"w_in": adam_halves("adam_win", w_in, m_w_in, v_w_in, red_w, oth_w, 256, 0, sp)}
    for n, w, m, v, blk0 in (("w_up", w_up, m_w_up, v_w_up, 0), ("w_down", w_down, m_w_down, v_w_down, 4),
                             ("w_out", w_out, m_w_out, v_w_out, 8), ("wq_mem", wq_mem, m_wq_mem, v_wq_mem, 10),
                             ("wk_mem", wk_mem, m_wk_mem, v_wk_mem, 11), ("wv_mem", wv_mem, m_wv_mem, v_wv_mem, 12),
                             ("wo_mem", wo_mem, m_wo_mem, v_wo_mem, 13)):
        big[n] = adam_halves("adam_" + n, w, m, v, red_r, oth_r, 256, blk0, sp)
    names_s = ["norm1_w", "gdn_conv_w", "gdn_a_log", "gdn_dt_bias", "gdn_norm_w", "ssm_conv_w", "ssm_conv_b",
               "ssm_a_log", "ssm_dt_bias", "ssm_d", "ssm_norm_w", "norm2_w", "mem_norm_w", "norm3_w", "final_norm_w"]
    w_s = [norm1_w, gdn_conv_w, gdn_a_log, gdn_dt_bias, gdn_norm_w, ssm_conv_w, ssm_conv_b, ssm_a_log, ssm_dt_bias,
           ssm_d, ssm_norm_w, norm2_w, mem_norm_w, norm3_w, final_norm_w]
    g_s = [gr_n1, gr_gcw, gr_galog, gr_gdtb, gr_gnw, gr_scw, gr_scb, gr_salog, gr_sdtb, gr_sd, gr_snw, gr_n2,
           gr_nmem, gr_n3, gr_final]
    m_s = [m_norm1_w, m_gdn_conv_w, m_gdn_a_log, m_gdn_dt_bias, m_gdn_norm_w, m_ssm_conv_w, m_ssm_conv_b, m_ssm_a_log,
           m_ssm_dt_bias, m_ssm_d, m_ssm_norm_w, m_norm2_w, m_mem_norm_w, m_norm3_w, m_final_norm_w]
    v_s = [v_norm1_w, v_gdn_conv_w, v_gdn_a_log, v_gdn_dt_bias, v_gdn_norm_w, v_ssm_conv_w, v_ssm_conv_b, v_ssm_a_log,
           v_ssm_dt_bias, v_ssm_d, v_ssm_norm_w, v_norm2_w, v_mem_norm_w, v_norm3_w, v_final_norm_w]
    shp_s = [w.shape for w in w_s]
    as2d = lambda a: a if a.ndim == 2 else a.reshape(1, -1)
    d_l, m_l, v_l = adam_small([as2d(a) for a in w_s], [as2d(a) for a in g_s], [as2d(a) for a in m_s],
                               [as2d(a) for a in v_s])

    grads, deltas, new_m, new_v = {}, {}, {}, {}
    for n, (gg, dd, mm_, vv_) in big.items():
        grads[n], deltas[n], new_m[n], new_v[n] = gg, dd, mm_, vv_
    for k, n in enumerate(names_s):
        grads[n] = g_s[k].reshape(shp_s[k])
        deltas[n], new_m[n], new_v[n] = (a[k].reshape(shp_s[k]) for a in (d_l, m_l, v_l))
    order = ["norm1_w", "w_in", "gdn_conv_w", "gdn_a_log", "gdn_dt_bias", "gdn_norm_w", "ssm_conv_w", "ssm_conv_b",
             "ssm_a_log", "ssm_dt_bias", "ssm_d", "ssm_norm_w", "w_out", "norm2_w", "mem_norm_w", "wq_mem", "wk_mem",
             "wv_mem", "wo_mem", "norm3_w", "w_up", "w_down", "final_norm_w"]
    return (loss, grad_x[None], *[grads[n] for n in order], *[deltas[n] for n in order],
            *[new_m[n] for n in order], *[new_v[n] for n in order])
```

```python
import numpy as np
import jax
import jax.numpy as jnp
from jax import lax
from jax.experimental import pallas as pl
from jax.experimental.pallas import tpu as pltpu

F32, BF16 = jnp.float32, jnp.bfloat16
MESH = pl.DeviceIdType.MESH
ANY = pl.BlockSpec(memory_space=pl.ANY)

EPS = 1e-6
D = 1024
GDN_H, GDN_DK, GDN_C = 8, 128, 64
SSM_H, SSM_P, SSM_N, SSM_L = 16, 64, 128, 128
MEM_H, MEM_DH = 4, 256
D_FF = 4096
IN_COLS = 6688
CB_QKV, CB_Z, CB_ZS, CB_XS, CB_BC, CB_BA, CB_DT = 0, 3, 4, 5, 12, 52, 53
VMEM_LIMIT = 56 * 1024 * 1024
D2D_CHUNKS = 8
ICI_CHUNKS = 4

ADAM_LR, ADAM_B1, ADAM_B2, ADAM_EPS, ADAM_WD, ADAM_STEP = 0.001, 0.9, 0.999, 1e-08, 0.01, 10


def _dg(a, b, ca, cb):
    return lax.dot_general(a, b, (((ca,), (cb,)), ((), ())), preferred_element_type=F32)


def _bf(x):
    return x.astype(BF16)


def mm(a, b):
    return _dg(_bf(a), _bf(b), 1, 0)


def mm_nt(a, b):
    return _dg(_bf(a), _bf(b), 1, 1)


def mm_tn(a, b):
    return _dg(_bf(a), _bf(b), 0, 0)


def mm_sel(a, sel):
    hi = a.astype(BF16)
    r1 = a - hi.astype(F32)
    mid = r1.astype(BF16)
    lo = (r1 - mid.astype(F32)).astype(BF16)
    s = sel.astype(BF16)
    return _dg(hi, s, 1, 0) + (_dg(mid, s, 1, 0) + _dg(lo, s, 1, 0))


def mm3(a, b):
    ah, bh = a.astype(BF16), b.astype(BF16)
    al, bl = (a - ah.astype(F32)).astype(BF16), (b - bh.astype(F32)).astype(BF16)
    return _dg(ah, bh, 1, 0) + (_dg(ah, bl, 1, 0) + _dg(al, bh, 1, 0))


def _iota(shape, dim):
    return lax.broadcasted_iota(jnp.int32, shape, dim)


def _chunk_cumsum(x, c):
    pos = _iota(x.shape, 0) & (c - 1)
    s = 1
    while s < c:
        x = x + jnp.where(pos >= s, pltpu.roll(x, s, 0), 0.0)
        s *= 2
    return x


def _chunk_revcumsum(x, c):
    n = x.shape[0]
    pos = _iota(x.shape, 0) & (c - 1)
    s = 1
    while s < c:
        x = x + jnp.where(pos < c - s, pltpu.roll(x, n - s, 0), 0.0)
        s *= 2
    return x


def _sig(x):
    return 1.0 / (1.0 + jnp.exp(-x))


def _softplus(x):
    return jnp.maximum(x, 0.0) + jnp.log(1.0 + jnp.exp(-jnp.abs(x)))


def _rows(v):
    return jnp.sum(v, axis=0, keepdims=True)


def _lanes(v):
    return jnp.sum(v, axis=1, keepdims=True)


def _sum_all(v):
    return _rows(_lanes(v))


def _cparams(sem):
    return pltpu.CompilerParams(dimension_semantics=sem, vmem_limit_bytes=VMEM_LIMIT)


def rowwise(fn, name, T, tb, row_ins, full_ins, row_outs, acc_outs=(), sp=None):
    nblk = T // tb
    assert nblk * tb == T
    has_sp = sp is not None

    def imap(f):
        return (lambda i, s: f(i, s)) if has_sp else (lambda i: f(i, None))

    in_specs, args = [], []
    for arr, w, cb, halo, off in row_ins:
        if halo == "col":
            in_specs.append(pl.BlockSpec((w, tb), imap(lambda i, s: (0, i))))
            args.append(arr)
            continue
        rowf = off if callable(off) else (lambda i, s, off=off: i + off)
        in_specs.append(pl.BlockSpec((tb, w), imap(lambda i, s, cb=cb, rowf=rowf: (rowf(i, s), cb))))
        args.append(arr)
        if halo == "prev":
            r = tb // 8
            in_specs.append(pl.BlockSpec((8, w), imap(lambda i, s, cb=cb, r=r: (jnp.maximum(i * r - 1, 0), cb))))
            args.append(arr)
        elif halo == "next":
            r, last = tb // 8, T // 8 - 1
            in_specs.append(pl.BlockSpec((8, w), imap(lambda i, s, cb=cb, r=r, last=last:
                                                      (jnp.minimum((i + 1) * r, last), cb))))
            args.append(arr)
    for arr in full_ins:
        in_specs.append(pl.BlockSpec(arr.shape, imap(lambda i, s, nd=arr.ndim: (0,) * nd)))
        args.append(arr)
    n_in, n_ro = len(args), len(row_outs)
    out_shape, out_specs, aliases = [], [], {}
    for k, (w, dt, *dest) in enumerate(row_outs):
        if dest:
            buf, cb = dest
            out_shape.append(jax.ShapeDtypeStruct(buf.shape, buf.dtype))
            out_specs.append(pl.BlockSpec((tb, w), imap(lambda i, s, cb=cb: (i, cb))))
            if not isinstance(buf, jax.ShapeDtypeStruct):
                aliases[len(args) + int(has_sp)] = k
                in_specs.append(ANY)
                args.append(buf)
        elif w < 0:
            out_shape.append(jax.ShapeDtypeStruct((-w, T), dt))
            out_specs.append(pl.BlockSpec((-w, tb), imap(lambda i, s: (0, i))))
        else:
            out_shape.append(jax.ShapeDtypeStruct((T, w), dt))
            out_specs.append(pl.BlockSpec((tb, w), imap(lambda i, s: (i, 0))))
    for shp in acc_outs:
        out_shape.append(jax.ShapeDtypeStruct(shp, F32))
        out_specs.append(pl.BlockSpec(shp, imap(lambda i, s, nd=len(shp): (0,) * nd)))

    def body(*refs):
        i = pl.program_id(0)
        if has_sp:
            sp_ref, refs = refs[0], refs[1:]
            vals = fn(i, nblk, sp_ref, *[r[...] for r in refs[:n_in]])
        else:
            vals = fn(i, nblk, *[r[...] for r in refs[:n_in]])
        outs = refs[n_in + len(aliases):]
        for ref, val in zip(outs[:n_ro], vals[:n_ro]):
            ref[...] = val.astype(ref.dtype)
        for ref, val in zip(outs[n_ro:], vals[n_ro:]):
            @pl.when(i == 0)
            def _(ref=ref, val=val):
                ref[...] = val

            @pl.when(i > 0)
            def _(ref=ref, val=val):
                ref[...] += val

    cparams = _cparams(("arbitrary",) if acc_outs else ("parallel",))
    if has_sp:
        return pl.pallas_call(
            body, name=name, out_shape=out_shape, compiler_params=cparams, input_output_aliases=aliases,
            grid_spec=pltpu.PrefetchScalarGridSpec(num_scalar_prefetch=1, grid=(nblk,), in_specs=in_specs,
                                                   out_specs=out_specs),
        )(sp, *args)
    return pl.pallas_call(
        body, name=name, grid=(nblk,), in_specs=in_specs, out_specs=out_specs, out_shape=out_shape,
        compiler_params=cparams, input_output_aliases=aliases,
    )(*args)


def R(arr, w=None, cb=0, halo=None, off=0):
    return (arr, arr.shape[1] if w is None else w, cb, halo, off)


def RC(arr):
    return (arr, arr.shape[0], 0, "col", 0)


def matmul(name, a, b, form, tm, tn, tk, out_dtypes, epi=None, extras=(), rows=(), into=None, n_acc=0):
    if form == "nn":
        (M, K), N = a.shape, b.shape[1]
    elif form == "nt":
        (M, K), N = a.shape, b.shape[0]
    else:
        (K, M), N = a.shape, b.shape[1]
    tm, tn, tk = min(tm, M), min(tn, N), min(tk, K)
    assert M % tm == 0 and N % tn == 0 and K % tk == 0, (name, M, N, K, tm, tn, tk)
    if form == "nn":
        a_spec = pl.BlockSpec((tm, tk), lambda i, j, k: (i, k))
        b_spec = pl.BlockSpec((tk, tn), lambda i, j, k: (k, j))
        ca, cb = 1, 0
    elif form == "nt":
        a_spec = pl.BlockSpec((tm, tk), lambda i, j, k: (i, k))
        b_spec = pl.BlockSpec((tn, tk), lambda i, j, k: (j, k))
        ca, cb = 1, 1
    else:
        a_spec = pl.BlockSpec((tk, tm), lambda i, j, k: (k, i))
        b_spec = pl.BlockSpec((tk, tn), lambda i, j, k: (k, j))
        ca, cb = 0, 0
    nk, ne, no = K // tk, len(extras) + len(rows), len(out_dtypes)
    if epi is None:
        epi = lambda acc: (acc,)

    assert n_acc == 0 or tn == N

    def body(a_ref, b_ref, *rest):
        ex, outs, accs, acc = rest[:ne], rest[ne:ne + no], rest[ne + no:ne + no + n_acc], rest[ne + no + n_acc]
        i, k = pl.program_id(0), pl.program_id(2)

        @pl.when(k == 0)
        def _():
            acc[...] = jnp.zeros_like(acc)

        acc[...] += _dg(_bf(a_ref[...]), _bf(b_ref[...]), ca, cb)

        @pl.when(k == nk - 1)
        def _():
            vals = epi(acc[...], *[e[...] for e in ex])
            for r, v in zip(outs, vals[:no]):
                r[...] = v.astype(r.dtype).reshape(r.shape)
            for r, v in zip(accs, vals[no:]):
                @pl.when(i == 0)
                def _(r=r, v=v):
                    r[...] = v

                @pl.when(i > 0)
                def _(r=r, v=v):
                    r[...] += v

    mn = pl.BlockSpec((tm, tn), lambda i, j, k: (i, j))
    rw = pl.BlockSpec((1, tn), lambda i, j, k: (0, j))
    if into is not None:
        buf, blk, bmap = into
        assert ne == 0 and no == 1
        aliased = not isinstance(buf, jax.ShapeDtypeStruct)

        def body_into(a_ref, b_ref, *rest):
            body(a_ref, b_ref, *rest[-2:])

        return pl.pallas_call(
            body_into, name=name, grid=(M // tm, N // tn, nk),
            in_specs=[a_spec, b_spec] + ([ANY] if aliased else []), out_specs=pl.BlockSpec(blk, bmap),
            out_shape=jax.ShapeDtypeStruct(buf.shape, buf.dtype),
            scratch_shapes=[pltpu.VMEM((tm, tn), F32)],
            input_output_aliases={2: 0} if aliased else {},
            compiler_params=_cparams(("parallel", "parallel", "arbitrary")),
        )(a, b, *([buf] if aliased else []))
    return pl.pallas_call(
        body, name=name, grid=(M // tm, N // tn, nk),
        in_specs=[a_spec, b_spec] + [mn] * len(extras) + [rw] * len(rows), out_specs=[mn] * no + [rw] * n_acc,
        out_shape=[jax.ShapeDtypeStruct((M, N), dt) for dt in out_dtypes] + [jax.ShapeDtypeStruct((1, N), F32)] * n_acc,
        scratch_shapes=[pltpu.VMEM((tm, tn), F32)],
        compiler_params=_cparams(("arbitrary",) * 3 if n_acc else ("parallel", "parallel", "arbitrary")),
    )(a, b, *extras, *rows)


def _epi_res(acc, res):
    return (res + acc,)


def _epi_rms_bwd(acc, x, dres, w):
    return rms_bwd_fn(0, 0, x, acc, dres, w)


def _epi_rms_bwd1(acc, x, dres, w):
    dx, _, gw = rms_bwd_fn(0, 0, x, acc, dres, w)
    return dx, gw


def _epi_final(acc, res, tgt, w):
    return final_fn(0, 0, res + acc, tgt, w)


def _epi_res_rms(acc, res, w):
    x = res + acc
    return (x, x * lax.rsqrt(jnp.mean(x * x, axis=-1, keepdims=True) + EPS) * w)


def _epi_relu2(acc):
    u = jnp.maximum(acc, 0.0)
    return (u, u * u)


def _epi_dup(acc, u):
    return (acc * 2.0 * u.astype(F32),)


def _conv(x, halo, w, i):
    halo = jnp.where(i == 0, 0.0, halo)
    xt = jnp.concatenate([halo, x], axis=0)
    shifted = [pltpu.roll(xt, 3 - k, 0)[8:, :] for k in range(3)] + [x]
    y = shifted[3] * w[3:4, :]
    for k in range(3):
        y = y + shifted[k] * w[k:k + 1, :]
    return y, shifted


def _l2n(x, scale):
    outs = []
    for h in range(x.shape[1] // 128):
        xh = x[:, 128 * h:128 * h + 128]
        outs.append(xh * (lax.rsqrt(jnp.sum(xh * xh, axis=-1, keepdims=True) + EPS) * scale))
    return jnp.concatenate(outs, axis=1)


def _l2n_bwd(x, dy, scale):
    outs = []
    for h in range(x.shape[1] // 128):
        xh, dh = x[:, 128 * h:128 * h + 128], dy[:, 128 * h:128 * h + 128] * scale
        r = lax.rsqrt(jnp.sum(xh * xh, axis=-1, keepdims=True) + EPS)
        outs.append(r * dh - xh * (r * r * r) * jnp.sum(xh * dh, axis=-1, keepdims=True))
    return jnp.concatenate(outs, axis=1)


def rms_fwd_fn(i, n, x, w):
    r = lax.rsqrt(jnp.mean(x * x, axis=-1, keepdims=True) + EPS)
    return (x * r * w,)


def rms_bwd_fn(i, n, x, dh, dres, w):
    r = lax.rsqrt(jnp.mean(x * x, axis=-1, keepdims=True) + EPS)
    g = dh * w
    dx = dres + r * g - x * (r * r * r) * jnp.mean(x * g, axis=-1, keepdims=True)
    return dx, dx, _rows(dh * x * r)


def rms_bwd_w_fn(i, n, x, dh, w):
    r = lax.rsqrt(jnp.mean(x * x, axis=-1, keepdims=True) + EPS)
    return (_rows(dh * x * r),)


def final_fn(i, n, x, tgt, w):
    r = lax.rsqrt(jnp.mean(x * x, axis=-1, keepdims=True) + EPS)
    xn = x * r
    e = xn * w - tgt
    dy = e * (1.0 / D)
    g = dy * w
    dx = r * g - x * (r * r * r) * jnp.mean(x * g, axis=-1, keepdims=True)
    return dx, dx, _rows(e * e), _rows(dy * xn)


def _gdn_gates(ba, alog_c, dtb_c):
    col = _iota(ba.shape, 1)
    amask = (col >= 8) & (col < 16)
    beta = jnp.where(col < 8, _sig(ba), 0.0)
    z = ba + dtb_c
    ea_ = jnp.exp(alog_c)
    return beta, z, ea_, jnp.where(amask, -ea_ * _softplus(z), 0.0), amask


def gdn_prep_fn(i, n, qkv, halo, ba, cw, alog_c, dtb_c, eb, ea):
    yc, _ = _conv(qkv, halo, cw, i)
    act = yc * _sig(yc)
    qn = _l2n(act[:, :D], GDN_DK ** -0.5)
    kn = _l2n(act[:, D:2 * D], 1.0)
    beta, _, _, g, _ = _gdn_gates(ba, alog_c, dtb_c)
    gcs = _chunk_cumsum(g, GDN_C)
    return qn, kn, act[:, 2 * D:], mm_sel(gcs, ea), mm_sel(beta, eb), jnp.transpose(gcs)[8:16, :]


def gdn_prep_bwd_fn(i, n, qkv, halo, ba, dqn, dkn, dv, dgcs_x, dbeta_x, dgcs_t, cw, alog_c, dtb_c, pb, pa):
    yc, shifted = _conv(qkv, halo, cw, i)
    sg = _sig(yc)
    act = yc * sg
    dq = _l2n_bwd(act[:, :D], dqn, GDN_DK ** -0.5)
    dk = _l2n_bwd(act[:, D:2 * D], dkn, 1.0)
    dyc = jnp.concatenate([dq, dk, dv], axis=1) * (sg * (1.0 + yc * (1.0 - sg)))
    dws = [_rows(dyc * shifted[k]) for k in range(4)]
    beta, z, ea_, g, amask = _gdn_gates(ba, alog_c, dtb_c)
    tbn = ba.shape[0]
    rowpart = jnp.transpose(jnp.concatenate([jnp.zeros((8, tbn), F32), dgcs_t, jnp.zeros((112, tbn), F32)], axis=0))
    dg = _chunk_revcumsum(mm_sel(dgcs_x, pa) - rowpart, GDN_C)
    draw = jnp.where(amask, dg * (-ea_) * _sig(z), 0.0)
    dba = draw + mm_sel(dbeta_x, pb) * beta * (1.0 - beta)
    return (dyc, dba, dws[0], dws[1], dws[2], dws[3], _rows(dg * g), _rows(draw))


def conv_bwd_fn(i, n, dyc, halo, w):
    halo = jnp.where(i == n - 1, 0.0, halo)
    tb = dyc.shape[0]
    xt = jnp.concatenate([dyc, halo], axis=0)
    dx = dyc * w[3:4, :]
    for k in range(3):
        dx = dx + pltpu.roll(xt, tb + 8 - (3 - k), 0)[:tb, :] * w[k:k + 1, :]
    return (dx,)


def gdn_post_fn(i, n, o, z, w):
    outs = []
    for h in range(GDN_H):
        oh, zh = o[:, 128 * h:128 * h + 128], z[:, 128 * h:128 * h + 128]
        r = lax.rsqrt(jnp.mean(oh * oh, axis=-1, keepdims=True) + EPS)
        outs.append(oh * r * w * (zh * _sig(zh)))
    return (jnp.concatenate(outs, axis=1),)


def gdn_post_bwd_fn(i, n, o, z, doa, w):
    dos, dzs, dw = [], [], None
    for h in range(GDN_H):
        sl = slice(128 * h, 128 * h + 128)
        oh, zh, dh = o[:, sl], z[:, sl], doa[:, sl]
        r = lax.rsqrt(jnp.mean(oh * oh, axis=-1, keepdims=True) + EPS)
        s = _sig(zh)
        dn = dh * (zh * s)
        dzs.append(dh * (oh * r * w) * (s * (1.0 + zh * (1.0 - s))))
        t = _rows(dn * oh * r)
        dw = t if dw is None else dw + t
        g = dn * w
        dos.append(r * g - oh * (r * r * r) * jnp.mean(oh * g, axis=-1, keepdims=True))
    return jnp.concatenate(dos, axis=1), jnp.concatenate(dzs, axis=1), dw


def _ssd_gates(dtblk, alog_c, dtb_c):
    hmask = _iota(dtblk.shape, 1) < SSM_H
    z = dtblk + dtb_c
    return jnp.where(hmask, _softplus(z), 0.0), -jnp.exp(alog_c), z, hmask


def ssd_prep_fn(i, n, xp, hx, bcp, hbc, dtblk, cwx, cwbc, cbx, cbbc, alog_c, dtb_c, e16):
    yx, _ = _conv(xp, hx, cwx, i)
    yx = yx + cbx
    ybc, _ = _conv(bcp, hbc, cwbc, i)
    ybc = ybc + cbbc
    dt, a_neg, _, _ = _ssd_gates(dtblk, alog_c, dtb_c)
    acs = _chunk_cumsum(dt * a_neg, SSM_L)
    return (yx * _sig(yx), ybc * _sig(ybc), mm_sel(dt, e16), mm_sel(acs, e16), jnp.transpose(acs)[0:SSM_H, :])


def ssd_prep_bwd_fn(i, n, xp, hx, bcp, hbc, dtblk, dxs_a, dxs_b, db, dc, dgate, dacs_t, cwx, cwbc, cbx, cbbc, alog_c, dtb_c):
    dbc = jnp.concatenate([db, dc], axis=1)
    yx, shx = _conv(xp, hx, cwx, i)
    yx = yx + cbx
    ybc, shbc = _conv(bcp, hbc, cwbc, i)
    ybc = ybc + cbbc
    sx, sbc = _sig(yx), _sig(ybc)
    dyx = (dxs_a + dxs_b) * (sx * (1.0 + yx * (1.0 - sx)))
    dybc = dbc * (sbc * (1.0 + ybc * (1.0 - sbc)))
    dwx = [_rows(dyx * shx[k]) for k in range(4)]
    dwbc = [_rows(dybc * shbc[k]) for k in range(4)]
    dt, a_neg, z, hmask = _ssd_gates(dtblk, alog_c, dtb_c)
    g0, g1 = dgate[:, :128], dgate[:, 128:]
    col = _iota(g0.shape, 1)
    lo, mid = col < 8, (col >= 8) & (col < 16)
    dacs_col = jnp.where(lo, g0, 0.0) + pltpu.roll(jnp.where(lo, g1, 0.0), 8, 1)
    ddt_dir = pltpu.roll(jnp.where(mid, g0, 0.0), 120, 1) + jnp.where(mid, g1, 0.0)
    tbn = dtblk.shape[0]
    rowpart = jnp.transpose(jnp.concatenate([dacs_t, jnp.zeros((128 - SSM_H, tbn), F32)], axis=0))
    da = _chunk_revcumsum(dacs_col - rowpart, SSM_L)
    draw = jnp.where(hmask, (ddt_dir + da * a_neg) * _sig(z), 0.0)
    return (dyx, dybc, draw, *dwx, *dwbc, _rows(dyx), _rows(dybc), _rows(da * dt * a_neg), _rows(draw))


def _ssd_gate(y, xs, zs, d_x):
    y2 = y + xs * d_x
    s = _sig(zs)
    return y2, s, y2 * (zs * s)


def ssd_post_fn(i, n, y, xs, zs, d_x, nw):
    _, _, yg = _ssd_gate(y, xs, zs, d_x)
    outs = []
    for g in range(2):
        v = yg[:, 512 * g:512 * g + 512]
        outs.append(v * lax.rsqrt(jnp.mean(v * v, axis=-1, keepdims=True) + EPS))
    return (jnp.concatenate(outs, axis=1) * nw,)


def ssd_post_bwd_fn(i, n, y, xs, zs, dob, d_x, nw):
    y2, s, yg = _ssd_gate(y, xs, zs, d_x)
    gfull = dob * nw
    dygs, dnw = [], []
    for g in range(2):
        sl = slice(512 * g, 512 * g + 512)
        v, gg = yg[:, sl], gfull[:, sl]
        r = lax.rsqrt(jnp.mean(v * v, axis=-1, keepdims=True) + EPS)
        dygs.append(r * gg - v * (r * r * r) * jnp.mean(v * gg, axis=-1, keepdims=True))
        dnw.append(_rows(dob[:, sl] * v * r))
    dyg = jnp.concatenate(dygs, axis=1)
    dy2 = dyg * (zs * s)
    dzs = dyg * y2 * (s * (1.0 + zs * (1.0 - s)))
    return dy2, dy2 * d_x, dzs, jnp.concatenate(dnw, axis=1), _rows(dy2 * xs)


def _attn_probs(q, k):
    hs = [slice(MEM_DH * h, MEM_DH * h + MEM_DH) for h in range(MEM_H)]
    ss = [mm_nt(q[:, sl], k[:, sl]) * (MEM_DH ** -0.5) for sl in hs]
    es = [jnp.exp(s - jnp.max(s, axis=-1, keepdims=True)) for s in ss]
    return hs, [e / jnp.sum(e, axis=-1, keepdims=True) for e in es]


def attn_fn(i, n, q, k, v):
    hs, ps = _attn_probs(q, k)
    return (jnp.concatenate([mm(p, v[:, sl]) for p, sl in zip(ps, hs)], axis=1),)


def attn_bwd_fn(i, n, q, do, k, v):
    hs, ps = _attn_probs(q, k)
    dvs = [mm_tn(p, do[:, sl]) for p, sl in zip(ps, hs)]
    dps = [mm_nt(do[:, sl], v[:, sl]) for sl in hs]
    dss = [p * (dp - jnp.sum(dp * p, axis=-1, keepdims=True)) * (MEM_DH ** -0.5) for p, dp in zip(ps, dps)]
    dqs = [mm(ds, k[:, sl]) for ds, sl in zip(dss, hs)]
    dks = [mm_tn(ds, q[:, sl]) for ds, sl in zip(dss, hs)]
    return jnp.concatenate(dqs, axis=1), jnp.concatenate(dks, axis=1), jnp.concatenate(dvs, axis=1)


def add2_fn(i, n, sp, a, b):
    return (a + b,)


def sum4_fn(i, n, sp, a, b, c, d):
    return (((a.astype(F32) + b.astype(F32)) + c.astype(F32)) + d.astype(F32),)


def _adamw(w, g, m, v):
    m = ADAM_B1 * m + (1.0 - ADAM_B1) * g
    v = ADAM_B2 * v + (1.0 - ADAM_B2) * (g * g)
    m_hat = m / (1.0 - ADAM_B1 ** ADAM_STEP)
    v_hat = v / (1.0 - ADAM_B2 ** ADAM_STEP)
    delta = -ADAM_LR * (m_hat / (jnp.sqrt(v_hat) + ADAM_EPS) + ADAM_WD * w)
    return delta, m, v


def _gdn_stage1(q, k, v, gcs, grow, bb):
    C = GDN_C
    row, col = _iota((C, C), 0), _iota((C, C), 1)
    incl, strict = row >= col, row > col
    dmat = jnp.where(incl, jnp.exp(jnp.minimum(gcs[:, :C] - grow, 0.0)), 0.0)
    gam = jnp.exp(gcs)
    gl = gcs[C - 1:C, :]
    kb, vb = k * bb, v * bb
    kg = kb * gam
    lmat = jnp.where(strict, mm_nt(kb, k) * dmat, 0.0)
    pmat = jnp.where(incl, mm_nt(q, k) * dmat, 0.0)
    return dict(q=q, k=k, v=v, bb=bb, incl=incl, strict=strict, dmat=dmat, gam=gam, kb=kb, vb=vb, kg=kg,
                lmat=lmat, pmat=pmat, qd=q * gam, kdec=jnp.exp(gl - gcs), cd=jnp.exp(gl))


def _gdn_inverse(lmats):
    C = GDN_C
    eye = (_iota((C, C), 0) == _iota((C, C), 1)).astype(F32)
    xs = [-l for l in lmats]
    ts = [eye + x for x in xs]
    for _ in range(5):
        xs = [mm(x, x) for x in xs]
        ts = [t + mm(t, x) for t, x in zip(ts, xs)]
    res = [eye - mm3(eye + l, t) for l, t in zip(lmats, ts)]
    return [t + mm(t, r) for t, r in zip(ts, res)]


def gdn_fwd(qn, kn, v, gcs_x, beta_x, gcs_t, tb, gh):
    T = qn.shape[0]
    nb, ncb, nc, C = T // tb, tb // GDN_C, T // GDN_C, GDN_C
    idx = [(hh, c) for hh in range(gh) for c in range(ncb)]

    def body(q_ref, k_ref, v_ref, g_ref, b_ref, gt_ref, o_ref, st_ref, ti_ref, s_scr):
        @pl.when(pl.program_id(1) == 0)
        def _():
            s_scr[...] = jnp.zeros_like(s_scr)

        grows = [gt_ref[hh] for hh in range(gh)]
        at = lambda hh, c: (slice(C * c, C * (c + 1)), slice(128 * hh, 128 * hh + 128))
        st1 = []
        for hh, c in idx:
            sl, ln = at(hh, c)
            st1.append(_gdn_stage1(q_ref[sl, ln], k_ref[sl, ln], v_ref[sl, ln], g_ref[sl, ln], grows[hh][:, sl],
                                   b_ref[sl, ln]))
        tinvs = _gdn_inverse([s["lmat"] for s in st1])
        us = [mm(t, s["vb"]) for t, s in zip(tinvs, st1)]
        ws = [mm(t, s["kg"]) for t, s in zip(tinvs, st1)]
        kds = [s["k"] * s["kdec"] for s in st1]
        ms = [mm_tn(kd, w) for kd, w in zip(kds, ws)]
        bs = [mm_tn(kd, u) for kd, u in zip(kds, us)]
        gs = [s["qd"] - mm(s["pmat"], w) for s, w in zip(st1, ws)]
        pus = [mm(s["pmat"], u) for s, u in zip(st1, us)]
        ss = [s_scr[hh] for hh in range(gh)]
        for c in range(ncb):
            for hh in range(gh):
                n, (sl, ln) = hh * ncb + c, at(hh, c)
                ti_ref[hh, sl, :] = tinvs[n]
                st_ref[hh, c] = ss[hh]
                o_ref[sl, ln] = mm(gs[n], ss[hh]) + pus[n]
                ss[hh] = st1[n]["cd"] * ss[hh] - mm(ms[n], ss[hh]) + bs[n]
        for hh in range(gh):
            s_scr[hh] = ss[hh]

    blk = pl.BlockSpec((tb, 128 * gh), lambda h, i: (i, h))
    return pl.pallas_call(
        body, name="gdn_fwd", grid=(GDN_H // gh, nb),
        in_specs=[blk] * 5 + [pl.BlockSpec((gh, 1, tb), lambda h, i: (h, 0, i))],
        out_specs=[blk, pl.BlockSpec((gh, ncb, 128, 128), lambda h, i: (h, i, 0, 0)),
                   pl.BlockSpec((gh, tb, C), lambda h, i: (h, i, 0))],
        out_shape=[jax.ShapeDtypeStruct((T, D), F32), jax.ShapeDtypeStruct((GDN_H, nc, 128, 128), F32),
                   jax.ShapeDtypeStruct((GDN_H, T, C), F32)],
        scratch_shapes=[pltpu.VMEM((gh, 128, 128), F32)],
        compiler_params=_cparams(("parallel", "arbitrary")),
    )(qn, kn, v, gcs_x, beta_x, gcs_t)


def gdn_bwd(qn, kn, v, gcs_x, beta_x, gcs_t, do, states, tinv, tb, gh):
    T = qn.shape[0]
    nb, ncb, C = T // tb, tb // GDN_C, GDN_C

    def body(q_ref, k_ref, v_ref, g_ref, b_ref, gt_ref, do_ref, st_ref, ti_ref,
             dq_ref, dk_ref, dv_ref, dgc_ref, db_ref, dgr_ref, ds_scr):
        @pl.when(pl.program_id(1) == 0)
        def _():
            ds_scr[...] = jnp.zeros_like(ds_scr)

        grows = [gt_ref[hh] for hh in range(gh)]
        at = lambda hh, c: (slice(C * c, C * (c + 1)), slice(128 * hh, 128 * hh + 128))
        lastrow = _iota((C, 1), 0) == C - 1
        idx = [(hh, c) for hh in range(gh) for c in range(ncb)]
        P = []
        for hh, c in idx:
            sl, ln = at(hh, c)
            lc = _gdn_stage1(q_ref[sl, ln], k_ref[sl, ln], v_ref[sl, ln], g_ref[sl, ln], grows[hh][:, sl],
                             b_ref[sl, ln])
            lc.update(tinv=ti_ref[hh, sl, :], s=st_ref[hh, c], do=do_ref[sl, ln], kd=lc["k"] * lc["kdec"])
            P.append(lc)
        for l, u, w in zip(P, [mm(l["tinv"], l["vb"]) for l in P], [mm(l["tinv"], l["kg"]) for l in P]):
            l.update(u=u, w=w)
        for l, x in zip(P, [mm(l["w"], l["s"]) for l in P]):
            l["vn"] = l["u"] - x
        for l, a, b, c_, d in zip(P, [mm_nt(l["do"], l["s"]) for l in P], [mm_nt(l["do"], l["vn"]) for l in P],
                                  [mm_tn(l["qd"], l["do"]) for l in P], [mm_tn(l["pmat"], l["do"]) for l in P]):
            l.update(dqd=a, dp=jnp.where(l["incl"], b, 0.0), ds_q=c_, dvn_p=d)
        pre = dict(zip(idx, P))
        rows = {}
        hs = range(gh)
        ds = [ds_scr[hh] for hh in hs]
        for c in reversed(range(ncb)):
            L = [pre[hh, c] for hh in hs]
            dvn = [l["dvn_p"] + mm(l["kd"], d) for l, d in zip(L, ds)]
            dkd = [mm_nt(l["vn"], d) for l, d in zip(L, ds)]
            dcd = [_sum_all(l["s"] * d) for l, d in zip(L, ds)]
            ds = [l["ds_q"] + l["cd"] * d - mm_tn(l["w"], x) for l, d, x in zip(L, ds, dvn)]
            dw = [-mm_nt(x, l["s"]) for l, x in zip(L, dvn)]
            dvb = [mm_tn(l["tinv"], x) for l, x in zip(L, dvn)]
            dkg = [mm_tn(l["tinv"], x) for l, x in zip(L, dw)]
            da = [-jnp.where(l["strict"], mm_nt(a, l["u"]) + mm_nt(b, l["w"]), 0.0) for l, a, b in zip(L, dvb, dkg)]
            dm = [a * l["dmat"] for l, a in zip(L, da)]
            dn = [l["dp"] * l["dmat"] for l in L]
            dkb = [mm(a, l["k"]) for l, a in zip(L, dm)]
            dq = [mm(a, l["k"]) + l["gam"] * l["dqd"] for l, a in zip(L, dn)]
            dk = [mm_tn(a, l["kb"]) + mm_tn(b, l["q"]) for l, a, b in zip(L, dm, dn)]
            for hh in hs:
                sl, ln = at(hh, c)
                l = L[hh]
                e = da[hh] * l["lmat"] + l["dp"] * l["pmat"]
                t_kd = _lanes(dkd[hh] * l["kd"])
                dgl = _sum_all(t_kd) + dcd[hh] * l["cd"][:, :1]
                dgcs = (_lanes(e) + _lanes(l["dqd"] * l["qd"]) - t_kd + _lanes(dkg[hh] * l["kg"])
                        + jnp.where(lastrow, dgl, 0.0))
                rows[hh, c] = _rows(e)
                dq_ref[sl, ln] = dq[hh]
                dk_ref[sl, ln] = (dk[hh] + l["kdec"] * dkd[hh] + l["bb"] * l["gam"] * dkg[hh] + l["bb"] * dkb[hh])
                dv_ref[sl, ln] = l["bb"] * dvb[hh]
                dbeta = _lanes(dkg[hh] * l["gam"] * l["k"]) + _lanes(dvb[hh] * l["v"]) + _lanes(dkb[hh] * l["k"])
                db_ref[sl, ln] = jnp.broadcast_to(dbeta, (C, 128))
                dgc_ref[sl, ln] = jnp.broadcast_to(dgcs, (C, 128))
        for hh in hs:
            ds_scr[hh] = ds[hh]
            dgr_ref[hh] = jnp.concatenate([rows[hh, c] for c in range(ncb)], axis=1)

    blk = pl.BlockSpec((tb, 128 * gh), lambda h, i: (nb - 1 - i, h))
    rowspec = pl.BlockSpec((gh, 1, tb), lambda h, i: (h, 0, nb - 1 - i))
    return pl.pallas_call(
        body, name="gdn_bwd", grid=(GDN_H // gh, nb),
        in_specs=[blk] * 5 + [rowspec, blk,
                              pl.BlockSpec((gh, ncb, 128, 128), lambda h, i: (h, nb - 1 - i, 0, 0)),
                              pl.BlockSpec((gh, tb, C), lambda h, i: (h, nb - 1 - i, 0))],
        out_specs=[blk] * 5 + [rowspec],
        out_shape=[jax.ShapeDtypeStruct((T, D), F32)] * 5 + [jax.ShapeDtypeStruct((GDN_H, 1, T), F32)],
        scratch_shapes=[pltpu.VMEM((gh, 128, 128), F32)],
        compiler_params=_cparams(("parallel", "arbitrary")),
    )(qn, kn, v, gcs_x, beta_x, gcs_t, do, states, tinv)


def _ssd_pair(x2, dt2, acs2):
    last = acs2[SSM_L - 1:SSM_L, :]
    return jnp.exp(acs2), jnp.exp(last - acs2), x2 * dt2


def _ssd_head(hh, acs2, arow, dec2, cbm, bm, incl, col):
    lmask = (col >= 64 * hh) & (col < 64 * hh + 64)
    sg = jnp.where(incl, jnp.exp(jnp.minimum(acs2[:, 64 * hh:64 * hh + 1] - arow, 0.0)), 0.0)
    dec_col = dec2[:, 64 * hh:64 * hh + 1]
    return lmask, sg, sg * cbm, dec_col, bm * dec_col


def ssd_fwd(xs, bc, dt_x, acs_x, acs_t):
    T = xs.shape[0]
    nc, L = T // SSM_L, SSM_L

    def body(x_ref, b_ref, c_ref, dt_ref, ac_ref, at_ref, y_ref, hst_ref, h_scr):
        @pl.when(pl.program_id(1) == 0)
        def _():
            h_scr[...] = jnp.zeros_like(h_scr)

        bm, cm = b_ref[...], c_ref[...]
        cbm = mm_nt(cm, bm)
        row, col = _iota((L, L), 0), _iota((L, L), 1)
        incl = row >= col
        P, H = [], []
        for pr in range(4):
            sl = slice(128 * pr, 128 * pr + 128)
            acs2 = ac_ref[:, sl]
            lam2, dec2, xd2 = _ssd_pair(x_ref[:, sl], dt_ref[:, sl], acs2)
            P.append(dict(sl=sl, lam2=lam2, xd2=xd2, hprev=h_scr[pr]))
            for hh in range(2):
                lmask, _, mmat, _, bd = _ssd_head(hh, acs2, at_ref[2 * pr + hh], dec2, cbm, bm, incl, col)
                H.append(dict(mmat=mmat, bd=bd, xdh=jnp.where(lmask, xd2, 0.0), xd2=xd2))
        ys = [mm(h["mmat"], h["xdh"]) for h in H]
        sts = [mm_tn(h["xd2"], h["bd"]) for h in H]
        zs = [mm_nt(cm, p["hprev"]) for p in P]
        for pr, p in enumerate(P):
            hst_ref[pr] = p["hprev"]
            y_ref[:, p["sl"]] = ys[2 * pr] + ys[2 * pr + 1] + p["lam2"] * zs[pr]
            lam_rows = jnp.where(row < 64, p["lam2"][L - 1:L, 0:1], p["lam2"][L - 1:L, 64:65])
            h_scr[pr] = lam_rows * p["hprev"] + jnp.where(row < 64, sts[2 * pr], sts[2 * pr + 1])

    return pl.pallas_call(
        body, name="ssd_fwd", grid=(2, nc),
        in_specs=[pl.BlockSpec((L, 512), lambda g, c: (c, g)),
                  pl.BlockSpec((L, 128), lambda g, c: (c, g)),
                  pl.BlockSpec((L, 128), lambda g, c: (c, 2 + g)),
                  pl.BlockSpec((L, 512), lambda g, c: (c, g)),
                  pl.BlockSpec((L, 512), lambda g, c: (c, g)),
                  pl.BlockSpec((8, 1, L), lambda g, c: (g, 0, c))],
        out_specs=[pl.BlockSpec((L, 512), lambda g, c: (c, g)),
                   pl.BlockSpec((None, None, 4, 128, 128), lambda g, c: (g, c, 0, 0, 0))],
        out_shape=[jax.ShapeDtypeStruct((T, D), F32), jax.ShapeDtypeStruct((2, nc, 4, 128, 128), F32)],
        scratch_shapes=[pltpu.VMEM((4, 128, 128), F32)],
        compiler_params=_cparams(("parallel", "arbitrary")),
    )(xs, bc, bc, dt_x, acs_x, acs_t)


def ssd_bwd(xs, bc, dt_x, acs_x, acs_t, dy, hstates):
    T = xs.shape[0]
    nc, L = T // SSM_L, SSM_L

    def body(x_ref, b_ref, c_ref, dt_ref, ac_ref, at_ref, dy_ref, hst_ref,
             dx_ref, db_ref, dc_ref, dgate_ref, dar_ref, dh_scr):
        @pl.when(pl.program_id(1) == 0)
        def _():
            dh_scr[...] = jnp.zeros_like(dh_scr)

        bm, cm = b_ref[...], c_ref[...]
        cbm = mm_nt(cm, bm)
        row, col = _iota((L, L), 0), _iota((L, L), 1)
        rowc = _iota((L, 1), 0)
        incl = row >= col
        prs = range(4)
        P = []
        for pr in prs:
            sl = slice(128 * pr, 128 * pr + 128)
            x2, dt2, dy2, acs2 = x_ref[:, sl], dt_ref[:, sl], dy_ref[:, sl], ac_ref[:, sl]
            lam2, dec2, xd2 = _ssd_pair(x2, dt2, acs2)
            P.append(dict(sl=sl, x2=x2, dt2=dt2, dy2=dy2, acs2=acs2, lam2=lam2, dec2=dec2, xd2=xd2,
                          hprev=hst_ref[pr], dhn=dh_scr[pr], dz=lam2 * dy2))
        zs = [mm_nt(cm, p["hprev"]) for p in P]
        dcm_t = [mm(p["dz"], p["hprev"]) for p in P]
        dh_z = [mm_tn(p["dz"], cm) for p in P]
        H = []
        for pr in prs:
            p = P[pr]
            p["yoff"] = p["dz"] * zs[pr]
            p["q_rows"] = _lanes(p["dhn"] * p["hprev"])
            for hh in range(2):
                lmask, sg, mmat, dec_col, bd = _ssd_head(hh, p["acs2"], at_ref[2 * pr + hh], p["dec2"], cbm, bm, incl, col)
                H.append(dict(p=p, hh=hh, j=2 * pr + hh, lmask=lmask, sg=sg, mmat=mmat, dec_col=dec_col, bd=bd))
        dms = [mm_nt(jnp.where(h["lmask"], h["p"]["dy2"], 0.0), h["p"]["xd2"]) for h in H]
        a1s = [mm_tn(h["mmat"], h["p"]["dy2"]) for h in H]
        a2s = [mm_nt(h["bd"], h["p"]["dhn"]) for h in H]
        dbds = [mm(jnp.where(h["lmask"], h["p"]["xd2"], 0.0), h["p"]["dhn"]) for h in H]
        dcb = jnp.zeros((L, L), F32)
        dbm = jnp.zeros((L, SSM_N), F32)
        comp = jnp.zeros((L, 128), F32)
        dxd = [jnp.zeros((L, 128), F32) for _ in prs]
        for h, dm_raw, a1, a2, dbd in zip(H, dms, a1s, a2s, dbds):
            p, hh, j = h["p"], h["hh"], h["j"]
            dm = jnp.where(incl, dm_raw, 0.0)
            dcb = dcb + dm * h["sg"]
            e = dm * h["mmat"]
            dxd_h = jnp.where(h["lmask"], a1 + a2, 0.0)
            dxd[j // 2] = dxd[j // 2] + dxd_h
            dbm = dbm + h["dec_col"] * dbd
            t = _lanes(dbd * h["bd"])
            lam_h = p["lam2"][L - 1:L, 64 * hh:64 * hh + 1]
            in_head = (rowc >= 64 * hh) & (rowc < 64 * hh + 64)
            add_last = _sum_all(t) + _sum_all(jnp.where(in_head, p["q_rows"], 0.0)) * lam_h
            dacs_col = (_lanes(jnp.where(h["lmask"], p["yoff"], 0.0)) + _lanes(e) - t
                        + jnp.where(rowc == L - 1, add_last, 0.0))
            ddt_col = _lanes(dxd_h * p["x2"])
            dar_ref[j] = _rows(e)
            comp = comp + jnp.where(col == j, dacs_col, 0.0) + jnp.where(col == 8 + j, ddt_col, 0.0)
        dcm = dcm_t[0]
        for pr in prs:
            p = P[pr]
            if pr:
                dcm = dcm + dcm_t[pr]
            lam_rows = jnp.where(row < 64, p["lam2"][L - 1:L, 0:1], p["lam2"][L - 1:L, 64:65])
            dh_scr[pr] = dh_z[pr] + lam_rows * p["dhn"]
            dx_ref[:, p["sl"]] = p["dt2"] * dxd[pr]
        db_ref[...] = dbm + mm_tn(dcb, cm)
        dc_ref[...] = dcm + mm(dcb, bm)
        dgate_ref[...] = comp

    rv = lambda g, c: (nc - 1 - c, g)
    rowspec = pl.BlockSpec((8, 1, L), lambda g, c: (g, 0, nc - 1 - c))
    return pl.pallas_call(
        body, name="ssd_bwd", grid=(2, nc),
        in_specs=[pl.BlockSpec((L, 512), rv),
                  pl.BlockSpec((L, 128), rv),
                  pl.BlockSpec((L, 128), lambda g, c: (nc - 1 - c, 2 + g)),
                  pl.BlockSpec((L, 512), rv),
                  pl.BlockSpec((L, 512), rv),
                  rowspec,
                  pl.BlockSpec((L, 512), rv),
                  pl.BlockSpec((None, None, 4, 128, 128), lambda g, c: (g, nc - 1 - c, 0, 0, 0))],
        out_specs=[pl.BlockSpec((L, 512), rv), pl.BlockSpec((L, 128), rv), pl.BlockSpec((L, 128), rv),
                   pl.BlockSpec((L, 128), rv), rowspec],
        out_shape=[jax.ShapeDtypeStruct((T, D), F32), jax.ShapeDtypeStruct((T, 256), F32),
                   jax.ShapeDtypeStruct((T, 256), F32), jax.ShapeDtypeStruct((T, 256), F32),
                   jax.ShapeDtypeStruct((SSM_H, 1, T), F32)],
        scratch_shapes=[pltpu.VMEM((4, 128, 128), F32)],
        compiler_params=_cparams(("parallel", "arbitrary")),
    )(xs, bc, bc, dt_x, acs_x, acs_t, dy, hstates)


def _pos():
    return lax.axis_index("x"), lax.axis_index("y"), lax.axis_index("c")


def _other_chips(x, y):
    return [(1 - x, y), (x, 1 - y), (1 - x, 1 - y)]


def _rcopy(src, dst, ssem, rsem, dev):
    return pltpu.make_async_remote_copy(src_ref=src, dst_ref=dst, send_sem=ssem, recv_sem=rsem,
                                        device_id=dev, device_id_type=MESH)


def _rows_at(start, n):
    return pl.ds(pl.multiple_of(start, 8), n)


def _comm_call(body, name, out_shape, n_in, scratch):
    return pl.pallas_call(
        body, name=name, out_shape=out_shape, in_specs=[ANY] * n_in,
        out_specs=[ANY] * len(out_shape) if isinstance(out_shape, (list, tuple)) else ANY,
        scratch_shapes=scratch,
        compiler_params=pltpu.CompilerParams(has_side_effects=True),
    )


def _dma_sems(n):
    return pltpu.SemaphoreType.DMA((n,))


def ag_chips(name, shard):
    rr, cc = shard.shape
    h, nq = rr // 2, ICI_CHUNKS
    hq = h // nq

    def body(x_ref, out_ref, ssem, rsem):
        x, y, c = _pos()
        chips = _other_chips(x, y)
        started = []
        for q in range(nq):
            rows = _rows_at(c * h + q * hq, hq)
            for j, (cx, cy) in enumerate(chips):
                cp = _rcopy(x_ref.at[rows], out_ref.at[j, rows], ssem.at[j * nq + q], rsem.at[j * nq + q], (cx, cy, c))
                cp.start()
                started.append(cp)
        for q in range(nq):
            rows = _rows_at(c * h + q * hq, hq)
            for j, (cx, cy) in enumerate(chips):
                blk = out_ref.at[j, rows]
                _rcopy(blk, blk, ssem.at[j * nq + q], rsem.at[j * nq + q], (cx, cy, c)).wait_recv()
                k = 3 * nq + j * nq + q
                cp = _rcopy(blk, blk, ssem.at[k], rsem.at[k], (x, y, 1 - c))
                cp.start()
                started.append(cp)
        for q in range(nq):
            rows = _rows_at((1 - c) * h + q * hq, hq)
            for j in range(3):
                blk = out_ref.at[j, rows]
                k = 3 * nq + j * nq + q
                _rcopy(blk, blk, ssem.at[k], rsem.at[k], (x, y, 1 - c)).wait_recv()
        for cp in started:
            cp.wait_send()

    return _comm_call(body, name, jax.ShapeDtypeStruct((3, rr, cc), shard.dtype), 1,
                      [_dma_sems(6 * nq), _dma_sems(6 * nq)])(shard)


def _by_chip(shard, got, s_me):
    by_rel = jnp.stack([shard, got[1], got[0], got[2]])
    return jnp.take(by_rel, jnp.arange(4) ^ s_me, axis=0)


def all_gather_chips(name, shard, s_me):
    return _by_chip(shard, ag_chips(name, shard), s_me)


HBM_SPEC = pl.BlockSpec(memory_space=pltpu.HBM)
SEM_SPEC = pl.BlockSpec(memory_space=pltpu.SEMAPHORE)
SPLIT_EFFECT = pltpu.SideEffectType.DATAFLOW_SIDE_EFFECTING


def _split_copies(pieces, x_ref, land_ref, sems):
    x, y, c = _pos()
    return [_rcopy(s, d, sems[j], sems[3 + j], dev) for j, (s, d, dev) in enumerate(pieces(x_ref, land_ref, x, y, c))]


def split_copy_start(name, src, land_shape, pieces, after):
    def body(x_ref, land_ref, after_ref, *outs):
        for cp in _split_copies(pieces, x_ref, land_ref, outs[:6]):
            cp.start()
        outs[8][...] = jnp.zeros_like(outs[8])

    dma = pltpu.SemaphoreType.DMA(())
    res = pl.pallas_call(
        body, name=name,
        out_shape=(dma,) * 6 + (pltpu.HBM(src.shape, src.dtype), pltpu.HBM(land_shape, src.dtype),
                                jax.ShapeDtypeStruct((8, 128), F32)),
        in_specs=(HBM_SPEC, HBM_SPEC, ANY),
        out_specs=(SEM_SPEC,) * 6 + (HBM_SPEC, HBM_SPEC, pl.BlockSpec(memory_space=pltpu.VMEM)),
        input_output_aliases={0: 6, 1: 7},
        compiler_params=pltpu.CompilerParams(has_side_effects=SPLIT_EFFECT),
    )(pltpu.with_memory_space_constraint(src, pltpu.HBM),
      pltpu.with_memory_space_constraint(lax.empty(land_shape, src.dtype), pltpu.HBM), after)
    return res[:6], res[6], res[7], res[8]


def split_copy_wait(name, sems, src_thru, land_thru, after, pieces):
    def body(x_ref, land_ref, *rest):
        for cp in _split_copies(pieces, x_ref, land_ref, rest[:6]):
            cp.wait_send()
            cp.wait_recv()

    return pl.pallas_call(
        body, name=name,
        out_shape=(pltpu.HBM(src_thru.shape, src_thru.dtype), pltpu.HBM(land_thru.shape, land_thru.dtype)),
        in_specs=(HBM_SPEC, HBM_SPEC) + (SEM_SPEC,) * 6 + (ANY,), out_specs=(HBM_SPEC, HBM_SPEC),
        input_output_aliases={0: 0, 1: 1},
        compiler_params=pltpu.CompilerParams(has_side_effects=SPLIT_EFFECT),
    )(src_thru, land_thru, *sems, after)


def ag_pieces(h):
    def pieces(x_ref, land_ref, x, y, c):
        rows = _rows_at(c * h, h)
        return [(x_ref.at[rows], land_ref.at[j, rows], (cx, cy, c)) for j, (cx, cy) in enumerate(_other_chips(x, y))]
    return pieces


def rs_pieces(x_ref, land_ref, x, y, c):
    return [(x_ref.at[2 * cx + cy], land_ref.at[j], (cx, cy, c)) for j, (cx, cy) in enumerate(_other_chips(x, y))]


def ag_forward(name, got):
    _, rr, cc = got.shape
    h, nq = rr // 2, D2D_CHUNKS
    hq = h // nq

    def body(g_ref, out_ref, ssem, rsem):
        x, y, c = _pos()
        cps = []
        for j in range(3):
            for q in range(nq):
                blk = out_ref.at[j, _rows_at(c * h + q * hq, hq)]
                cp = _rcopy(blk, blk, ssem.at[j * nq + q], rsem.at[j * nq + q], (x, y, 1 - c))
                cp.start()
                cps.append(cp)
        for cp in cps:
            cp.wait_send()
        for j in range(3):
            for q in range(nq):
                blk = out_ref.at[j, _rows_at((1 - c) * h + q * hq, hq)]
                _rcopy(blk, blk, ssem.at[j * nq + q], rsem.at[j * nq + q], (x, y, 1 - c)).wait_recv()

    return pl.pallas_call(
        body, name=name, out_shape=jax.ShapeDtypeStruct(got.shape, got.dtype), in_specs=[ANY], out_specs=ANY,
        scratch_shapes=[_dma_sems(3 * nq), _dma_sems(3 * nq)], input_output_aliases={0: 0},
        compiler_params=pltpu.CompilerParams(has_side_effects=True),
    )(got)


def rs_pair(name, g):
    _, rr, cc = g.shape
    h, nq = rr // 2, D2D_CHUNKS
    hq = h // nq

    def body(g_ref, recv_ref, ssem, rsem):
        x, y, c = _pos()
        cps = []
        for q in range(nq):
            cp = _rcopy(g_ref.at[:, _rows_at((1 - c) * h + q * hq, hq), :], recv_ref.at[:, pl.ds(q * hq, hq), :],
                        ssem.at[q], rsem.at[q], (x, y, 1 - c))
            cp.start()
            cps.append(cp)
        for cp in cps:
            cp.wait()

    return _comm_call(body, name, jax.ShapeDtypeStruct((4, h, cc), g.dtype), 1, [_dma_sems(nq), _dma_sems(nq)])(g)


def rs_chips(name, p):
    _, h, cc = p.shape
    nq = ICI_CHUNKS
    hq = h // nq

    def body(p_ref, buf_ref, ssem, rsem):
        x, y, c = _pos()
        sends = []
        for q in range(nq):
            rows = pl.ds(q * hq, hq)
            for j, (cx, cy) in enumerate(_other_chips(x, y)):
                cp = _rcopy(p_ref.at[2 * cx + cy, rows], buf_ref.at[j, rows], ssem.at[j * nq + q],
                            rsem.at[j * nq + q], (cx, cy, c))
                cp.start()
                sends.append(cp)
        for cp in sends:
            cp.wait()

    return _comm_call(body, name, jax.ShapeDtypeStruct((3, h, cc), p.dtype), 1,
                      [_dma_sems(3 * nq), _dma_sems(3 * nq)])(p)


def rs_join(name, half):
    h, cc = half.shape
    nq = D2D_CHUNKS
    hq = h // nq

    def body(h_ref, out_ref, ssem, rsem):
        x, y, c = _pos()
        cps = []
        for q in range(nq):
            rows = pl.ds(q * hq, hq)
            cp = _rcopy(h_ref.at[rows], out_ref.at[rows], ssem.at[q], rsem.at[q], (x, y, 1 - c))
            cp.start()
            cps.append(cp)
        for cp in cps:
            cp.wait()

    return _comm_call(body, name, jax.ShapeDtypeStruct((h, cc), half.dtype), 1, [_dma_sems(nq), _dma_sems(nq)])(half)


def reduce_scatter(tag, g, tb, sp):
    return rs_end(rs_begin(tag, g, tb, sp, False), None)


def rs_begin(tag, g, tb, sp, split, after=None):
    _, rr, cc = g.shape
    h = rr // 2
    nbh = h // tb
    recv = rs_pair(tag + "_pair", g)
    mine_rows = lambda i, s: (i // nbh) * (2 * nbh) + s[0] * nbh + i % nbh
    part = rowwise(add2_fn, tag + "_add", 4 * h, tb, [R(g.reshape(4 * rr, cc), off=mine_rows), R(recv.reshape(4 * h, cc))],
                   [], [(cc, BF16)], sp=sp)[0].reshape(4, h, cc)
    st = dict(tag=tag, tb=tb, sp=sp, split=split, part=part)
    if split:
        st["sems"], st["part"], st["land"], st["token"] = split_copy_start(tag + "_start", part, (3, h, cc), rs_pieces,
                                                                           sp if after is None else after)
    return st


def rs_end(st, after):
    tag, tb, sp, part = st["tag"], st["tb"], st["sp"], st["part"]
    _, h, cc = part.shape
    nbh = h // tb
    if st["split"]:
        part, buf = split_copy_wait(tag + "_wait", st["sems"], part, st["land"], after, rs_pieces)
    else:
        buf = rs_chips(tag + "_chips", part)
    red = rowwise(sum4_fn, tag + "_sum", h, tb,
                  [R(part.reshape(4 * h, cc), off=lambda i, s: s[1] * nbh + i)]
                  + [R(buf.reshape(3 * h, cc), off=k * nbh) for k in range(3)],
                  [], [(cc, F32)], sp=sp)[0]
    return red, rs_join(tag + "_join", red)


def adam_halves(name, w, m, v, red, other, tb, blk0, sp):
    nbh = red.shape[0] // tb

    def fn(i, n, s, w_, m_, v_, r_, o_):
        g = jnp.where((blk0 + i) // nbh == s[0], r_, o_)
        return (g,) + _adamw(w_, g, m_, v_)

    half_rows = lambda i, s: (blk0 + i) % nbh
    return rowwise(fn, name, w.shape[0], tb, [R(w), R(m), R(v), R(red, off=half_rows), R(other, off=half_rows)],
                   [], [(w.shape[1], F32)] * 4, sp=sp)


SMALL_LANES = 3 * D


def all_reduce_items(name, items):
    flat = [a for it in items for a in it]
    shapes = [(sum(a.shape[0] for a in it), it[0].shape[1]) for it in items]
    nrows = -(-sum(s[0] for s in shapes) // 8) * 8

    def body(*refs):
        ins, outs = refs[:len(flat)], refs[len(flat):len(flat) + len(items)]
        mine, buf, ssem, rsem = refs[len(flat) + len(items):]
        x, y, c = _pos()
        me = 4 * x + 2 * y + c
        mine[...] = jnp.zeros_like(mine)
        r = 0
        for ref in ins:
            mine[r:r + ref.shape[0], 0:ref.shape[1]] = ref[...]
            r += ref.shape[0]
        buf[me] = mine[...]
        cps = []
        for k in range(1, 8):
            dev = (x ^ (k >> 2), y ^ ((k >> 1) & 1), c ^ (k & 1))
            cp = _rcopy(mine, buf.at[me], ssem.at[k - 1], rsem.at[k - 1], dev)
            cp.start()
            cps.append(cp)
        for cp in cps:
            cp.wait()
        r = 0
        for (nr, n), out in zip(shapes, outs):
            acc = buf[0, r:r + nr, 0:n]
            for d in range(1, 8):
                acc = acc + buf[d, r:r + nr, 0:n]
            out[...] = acc
            r += nr

    vm = pl.BlockSpec(memory_space=pltpu.VMEM)
    return pl.pallas_call(
        body, name=name, out_shape=[jax.ShapeDtypeStruct(s, F32) for s in shapes],
        in_specs=[vm] * len(flat), out_specs=[vm] * len(items),
        scratch_shapes=[pltpu.VMEM((nrows, SMALL_LANES), F32), pltpu.VMEM((8, nrows, SMALL_LANES), F32),
                        _dma_sems(7), _dma_sems(7)],
        compiler_params=pltpu.CompilerParams(has_side_effects=True),
    )(*flat)


def adam_small(ws, gs, ms, vs):
    n = len(ws)

    def body(*refs):
        for k in range(n):
            w, g, m, v = (refs[j * n + k][...] for j in range(4))
            for j, val in enumerate(_adamw(w, g, m, v)):
                refs[(4 + j) * n + k][...] = val

    vm = pl.BlockSpec(memory_space=pltpu.VMEM)
    res = pl.pallas_call(
        body, name="adam_small", out_shape=[jax.ShapeDtypeStruct(w.shape, F32) for w in ws] * 3,
        in_specs=[vm] * (4 * n), out_specs=[vm] * (3 * n),
    )(*ws, *gs, *ms, *vs)
    return res[:n], res[n:2 * n], res[2 * n:]


def _sel(rows, cols, pairs):
    m = np.zeros((rows, cols), np.float32)
    for r, c in pairs:
        m[r, c] = 1.0
    return jnp.asarray(m)


def _pad_win(w):
    z = jnp.zeros((w.shape[0], 112), w.dtype)
    return jnp.concatenate([w[:, :4096], w[:, 4112:6672], w[:, 4096:4112], z, w[:, 6672:6688], z], axis=1)


def _unpad_win(wp):
    return jnp.concatenate([wp[:, :4096], wp[:, 6656:6672], wp[:, 4096:6656], wp[:, 6784:6800]], axis=1)


def kernel(x, mem, norm1_w, w_in, gdn_conv_w, gdn_a_log, gdn_dt_bias, gdn_norm_w, ssm_conv_w, ssm_conv_b, ssm_a_log, ssm_dt_bias, ssm_d, ssm_norm_w, w_out, norm2_w, mem_norm_w, wq_mem, wk_mem, wv_mem, wo_mem, norm3_w, w_up, w_down, final_norm_w, loss_target, m_norm1_w, m_w_in, m_gdn_conv_w, m_gdn_a_log, m_gdn_dt_bias, m_gdn_norm_w, m_ssm_conv_w, m_ssm_conv_b, m_ssm_a_log, m_ssm_dt_bias, m_ssm_d, m_ssm_norm_w, m_w_out, m_norm2_w, m_mem_norm_w, m_wq_mem, m_wk_mem, m_wv_mem, m_wo_mem, m_norm3_w, m_w_up, m_w_down, m_final_norm_w, v_norm1_w, v_w_in, v_gdn_conv_w, v_gdn_a_log, v_gdn_dt_bias, v_gdn_norm_w, v_ssm_conv_w, v_ssm_conv_b, v_ssm_a_log, v_ssm_dt_bias, v_ssm_d, v_ssm_norm_w, v_w_out, v_norm2_w, v_mem_norm_w, v_wq_mem, v_wk_mem, v_wv_mem, v_wo_mem, v_norm3_w, v_w_up, v_w_down, v_final_norm_w):
    T, M = x.shape[1], mem.shape[1]
    xi, yi, ci = _pos()
    s_me = 2 * xi + yi
    x0, mem0, tgt = x[0], mem[0], loss_target[0]
    tb = min(256, T)
    row = lambda v: v.reshape(1, -1)

    win_g = all_gather_chips("ag_win", w_in.astype(BF16), s_me)
    w_in_p = _pad_win(win_g.transpose(1, 0, 2).reshape(D, IN_COLS))
    keep = (ci == 0).astype(F32)
    gcw_z = lax.dynamic_update_slice(jnp.zeros((4, 3 * D), F32), gdn_conv_w * keep, (0, s_me * 768))
    scw_z = lax.dynamic_update_slice(jnp.zeros((4, 1536), F32), ssm_conv_w * keep, (0, s_me * 384))
    gcw, scw = all_reduce_items("ar_convw", [[gcw_z], [scw_z]])
    scw_x, scw_bc = scw[:, :D], scw[:, D:]
    rest_shard = jnp.concatenate([w_up, w_down, w_out, wq_mem, wk_mem, wv_mem, wo_mem], axis=0).astype(BF16)
    ag_sems, rest_thru, rest_land, ag_token = split_copy_start("ag_rest_start", rest_shard, (3,) + rest_shard.shape,
                                                               ag_pieces(rest_shard.shape[0] // 2), gcw)
    sp = jnp.stack([ci, s_me]).astype(jnp.int32)
    scb_x, scb_bc = row(ssm_conv_b[:D]), row(ssm_conv_b[D:])

    galog_c, gdtb_c = row(jnp.pad(gdn_a_log, (8, 112))), row(jnp.pad(gdn_dt_bias, (8, 112)))
    salog_c, sdtb_c = row(jnp.pad(ssm_a_log, (0, 112))), row(jnp.pad(ssm_dt_bias, (0, 112)))
    sd_x = row(jnp.repeat(ssm_d, 64))
    eb = _sel(128, D, [(h, 128 * h + l) for h in range(8) for l in range(128)])
    ea = _sel(128, D, [(8 + h, 128 * h + l) for h in range(8) for l in range(128)])
    e16 = _sel(128, D, [(h, 64 * h + l) for h in range(16) for l in range(64)])
    pb = _sel(D, 128, [(128 * h, h) for h in range(8)])
    pa = _sel(D, 128, [(128 * h, 8 + h) for h in range(8)])

    h1 = rowwise(rms_fwd_fn, "rms1", T, tb, [R(x0)], [row(norm1_w) + ag_token[0:1, 0:1]], [(D, BF16)])[0]
    p = matmul("mm_in", h1, w_in_p, "nn", 1024, 768, 1024, [F32])[0]
    gp_ins = [R(p, 3 * D, CB_QKV, "prev"), R(p, 128, CB_BA)]
    qn, kn, vv, gcs_x, beta_x, gcs_t = rowwise(gdn_prep_fn, "gdn_prep", T, tb, gp_ins,
                                               [gcw, galog_c, gdtb_c, eb, ea], [(D, F32)] * 5 + [(-8, F32)])
    gcs_t = gcs_t.reshape(GDN_H, 1, T)
    gtb, ggh = min(128, T), 4
    o_gdn, s_states, tinv = gdn_fwd(qn, kn, vv, gcs_x, beta_x, gcs_t, gtb, ggh)
    gnw = row(gdn_norm_w)
    oa = rowwise(gdn_post_fn, "gdn_post", T, tb, [R(o_gdn), R(p, D, CB_Z)], [gnw], [(D, BF16)])[0]
    sp_ins = [R(p, D, CB_XS, "prev"), R(p, 512, CB_BC, "prev"), R(p, 128, CB_DT)]
    sp_full = [scw_x, scw_bc, scb_x, scb_bc, salog_c, sdtb_c]
    xs, bc, dt_x, acs_x, acs_t = rowwise(ssd_prep_fn, "ssd_prep", T, tb, sp_ins, sp_full + [e16],
                                         [(D, F32), (512, F32), (D, F32), (D, F32), (-SSM_H, F32)])
    acs_t = acs_t.reshape(SSM_H, 1, T)
    y_ssd, h_states = ssd_fwd(xs, bc, dt_x, acs_x, acs_t)
    snw = row(ssm_norm_w)
    ob = rowwise(ssd_post_fn, "ssd_post", T, tb, [R(y_ssd), R(xs), R(p, D, CB_ZS)], [sd_x, snw], [(D, BF16)])[0]
    rest_thru, rest_land = split_copy_wait("ag_rest_wait", ag_sems, rest_thru, rest_land, ob,
                                           ag_pieces(rest_shard.shape[0] // 2))
    rest_g = _by_chip(rest_thru, ag_forward("ag_rest_fwd", rest_land), s_me)
    wup_f = rest_g[:, 0:1024].transpose(1, 0, 2).reshape(D, D_FF)
    wdown_f = rest_g[:, 1024:2048].reshape(D_FF, D)
    wout_f = rest_g[:, 2048:2560].reshape(2 * D, D)
    wq_f, wk_f, wv_f, wo_f = (rest_g[:, 2560 + 256 * k:2816 + 256 * k].reshape(D, D) for k in range(4))
    x1a = matmul("mm_out_a", oa, wout_f[:D], "nn", 1024, 1024, 1024, [F32], _epi_res, [x0])[0]
    assert D == 1024
    x1, h2 = matmul("mm_out_b", ob, wout_f[D:], "nn", 1024, 1024, 1024, [F32, BF16], _epi_res_rms, [x1a],
                    [row(norm2_w)])

    mn = rowwise(rms_fwd_fn, "rms_mem", M, M, [R(mem0)], [row(mem_norm_w)], [(D, BF16)])[0]
    km = matmul("mm_k", mn, wk_f, "nn", 256, 1024, 1024, [BF16])[0]
    vm = matmul("mm_v", mn, wv_f, "nn", 256, 1024, 1024, [BF16])[0]
    qm = matmul("mm_q", h2, wq_f, "nn", 1024, 1024, 1024, [BF16])[0]
    ao = rowwise(attn_fn, "attn", T, tb, [R(qm)], [km, vm], [(D, BF16)])[0]
    x2, h3 = matmul("mm_o", ao, wo_f, "nn", 1024, 1024, 1024, [F32, BF16], _epi_res_rms, [x1], [row(norm3_w)])
    u, act = matmul("mm_up", h3, wup_f, "nn", 1024, 1024, 1024, [BF16, BF16], _epi_relu2)
    dx3, dx3b, loss_lane, g_final = matmul("mm_down", act, wdown_f, "nn", 512, 1024, 1024, [F32, BF16], _epi_final,
                                           [x2, tgt], [row(final_norm_w)], n_acc=2)
    loss = lax.psum(0.5 / D * jnp.sum(loss_lane), ("x", "y", "c"))

    dup = matmul("mm_dact", dx3b, wdown_f, "nt", 1024, 1024, 1024, [BF16], _epi_dup, [u])[0]
    def g_into(buf, blk, at):
        return dict(into=(buf, blk, lambda i, j, k, at=at: at(i, j)))

    grest = jax.ShapeDtypeStruct((4, 3584, D), F32)
    grest = matmul("mm_gdown", act, dx3b, "tn", 1024, 1024, 1024, [F32],
                   **g_into(grest, (None, 1024, D), lambda i, j: (i, 1, 0)))
    dx2, dx2b, g_n3 = matmul("mm_dh3", dup, wup_f, "nt", 512, 1024, 1024, [F32, BF16], _epi_rms_bwd, [x2, dx3],
                             [row(norm3_w)], n_acc=1)
    grest = matmul("mm_gup", h3, dup, "tn", 1024, 1024, 1024, [F32],
                   **g_into(grest, (None, 1024, D), lambda i, j: (j, 0, 0)))
    dao = matmul("mm_dao", dx2b, wo_f, "nt", 1024, 1024, 1024, [F32])[0]
    grest = matmul("mm_gwo", ao, dx2b, "tn", 1024, 1024, 1024, [F32],
                   **g_into(grest, (4, 256, D), lambda i, j: (0, 13, 0)))
    dqm, dkm, dvm = rowwise(attn_bwd_fn, "attn_bwd", T, tb, [R(qm), R(dao)], [km, vm], [(D, BF16)],
                            [(M, D), (M, D)])
    dx1, dx1b, g_n2 = matmul("mm_dh2", dqm, wq_f, "nt", 512, 1024, 1024, [F32, BF16], _epi_rms_bwd, [x1, dx2],
                             [row(norm2_w)], n_acc=1)
    grest = matmul("mm_gwq", h2, dqm, "tn", 1024, 1024, 1024, [F32],
                   **g_into(grest, (4, 256, D), lambda i, j: (0, 10, 0)))
    grest = matmul("mm_gwk", mn, dkm, "tn", 1024, 1024, 256, [F32],
                   **g_into(grest, (4, 256, D), lambda i, j: (0, 11, 0)))
    grest = matmul("mm_gwv", mn, dvm, "tn", 1024, 1024, 256, [F32],
                   **g_into(grest, (4, 256, D), lambda i, j: (0, 12, 0)))
    dmn_k = matmul("mm_dmk", dkm, wk_f, "nt", 256, 1024, 1024, [F32])[0]
    dmn = matmul("mm_dmv", dvm, wv_f, "nt", 256, 1024, 1024, [F32], _epi_res, [dmn_k])[0]
    g_nmem = rowwise(rms_bwd_w_fn, "rmsmem_bwd", M, M, [R(mem0), R(dmn)], [row(mem_norm_w)], [], [(1, D)])[0]
    doa = matmul("mm_doa", dx1b, wout_f[:D], "nt", 1024, 1024, 1024, [F32])[0]
    dob = matmul("mm_dob", dx1b, wout_f[D:], "nt", 1024, 1024, 1024, [F32])[0]
    grest = matmul("mm_gwout_a", oa, dx1b, "tn", 1024, 1024, 1024, [F32],
                   **g_into(grest, (2, 512, D), lambda i, j: (0, 4, 0)))
    grest = matmul("mm_gwout_b", ob, dx1b, "tn", 1024, 1024, 1024, [F32],
                   **g_into(grest, (2, 512, D), lambda i, j: (1, 4, 0)))

    rs_rest = rs_begin("rs_rest", grest, 256, sp, True)

    dp = jax.ShapeDtypeStruct((T, p.shape[1]), BF16)
    dy_ssd, dxs_dir, dp, g_snw, g_sd_lane = rowwise(
        ssd_post_bwd_fn, "ssd_post_bwd", T, tb, [R(y_ssd), R(xs), R(p, D, CB_ZS), R(dob)],
        [sd_x + rs_rest["token"][0:1, 0:1], snw],
        [(D, F32), (D, F32), (D, BF16, dp, CB_ZS)], [(1, D), (1, D)])
    dxs_scan, db_s, dc_s, dgate, dacs_t = ssd_bwd(xs, bc, dt_x, acs_x, acs_t, dy_ssd, h_states)
    spb = rowwise(ssd_prep_bwd_fn, "ssd_prep_bwd", T, tb,
                  sp_ins + [R(dxs_scan), R(dxs_dir), R(db_s), R(dc_s), R(dgate), RC(dacs_t.reshape(SSM_H, T))], sp_full,
                  [(D, F32), (512, F32), (128, BF16, dp, CB_DT)],
                  [(1, D)] * 4 + [(1, 512)] * 4 + [(1, D), (1, 512), (1, 128), (1, 128)])
    dyc_x, dyc_bc, dp = spb[:3]
    dp = rowwise(conv_bwd_fn, "conv_bwd_x", T, tb, [R(dyc_x, halo="next")], [scw_x], [(D, BF16, dp, CB_XS)])[0]
    dp = rowwise(conv_bwd_fn, "conv_bwd_bc", T, tb, [R(dyc_bc, halo="next")], [scw_bc], [(512, BF16, dp, CB_BC)])[0]

    do_gdn, dp, g_gnw = rowwise(gdn_post_bwd_fn, "gdn_post_bwd", T, tb, [R(o_gdn), R(p, D, CB_Z), R(doa)], [gnw],
                                [(D, F32), (D, BF16, dp, CB_Z)], [(1, 128)])
    dqn, dkn, dvv, dgcs_x, dbeta_x, dgcs_t = gdn_bwd(qn, kn, vv, gcs_x, beta_x, gcs_t, do_gdn, s_states, tinv, gtb, ggh)
    gpb = rowwise(gdn_prep_bwd_fn, "gdn_prep_bwd", T, tb,
                  gp_ins + [R(dqn), R(dkn), R(dvv), R(dgcs_x), R(dbeta_x), RC(dgcs_t.reshape(GDN_H, T))],
                  [gcw, galog_c, gdtb_c, pb, pa],
                  [(3 * D, F32), (128, BF16, dp, CB_BA)], [(1, 3 * D)] * 4 + [(1, 128), (1, 128)])
    dyc_qkv, dp = gpb[:2]
    dp = rowwise(conv_bwd_fn, "conv_bwd_qkv", T, tb, [R(dyc_qkv, halo="next")], [gcw], [(3 * D, BF16, dp, CB_QKV)])[0]
    grad_x, g_n1 = matmul("mm_dh1", dp, w_in_p, "nt", 512, 1024, 768, [F32], _epi_rms_bwd1, [x0, dx1],
                          [row(norm1_w)], n_acc=1)
    g_win_p = matmul("mm_gwin", h1, dp, "tn", 1024, 768, 1024, [F32])[0]

    items = [[g_n1], [gpb[6]], [gpb[7]], [g_gnw], [spb[11]], [spb[12]], [spb[13]], [spb[14]], [g_sd_lane], [g_snw],
             [g_n2], [g_nmem], [g_n3], [g_final], list(gpb[2:6]), list(spb[3:7]), list(spb[7:11])]
    (gr_n1, r_galog, r_gdtb, gr_gnw, r_scb_x, r_scb_bc, r_salog, r_sdtb, r_sd, gr_snw, gr_n2, gr_nmem, gr_n3,
     gr_final, r_gcw, r_scw_x, r_scw_bc) = all_reduce_items("ar_grads", items)
    gr_galog, gr_gdtb = r_galog[:, 8:16], r_gdtb[:, 8:16]
    gr_salog, gr_sdtb = r_salog[:, :SSM_H], r_sdtb[:, :SSM_H]
    gr_sd = r_sd.reshape(SSM_H, SSM_P).sum(axis=1).reshape(1, SSM_H)
    gr_scb = jnp.concatenate([r_scb_x, r_scb_bc], axis=1)
    gr_gcw = lax.dynamic_slice(r_gcw, (0, s_me * 768), (4, 768))
    gr_scw = lax.dynamic_slice(jnp.concatenate([r_scw_x, r_scw_bc], axis=1), (0, s_me * 384), (4, 384))

    g_win = _unpad_win(g_win_p).reshape(D, 4, IN_COLS // 4).transpose(1, 0, 2)
    rs_win = rs_begin("rs_win", g_win, 256, sp, True, gr_n1)
    red_r, oth_r = rs_end(rs_rest, rs_win["token"])

    big = {}
    for n, w, m, v, blk0 in (("w_up", w_up, m_w_up, v_w_up, 0), ("w_down", w_down, m_w_down, v_w_down, 4),
                             ("w_out", w_out, m_w_out, v_w_out, 8), ("wq_mem", wq_mem, m_wq_mem, v_wq_mem, 10),
                             ("wk_mem", wk_mem, m_wk_mem, v_wk_mem, 11), ("wv_mem", wv_mem, m_wv_mem, v_wv_mem, 12),
                             ("wo_mem", wo_mem, m_wo_mem, v_wo_mem, 13)):
        big[n] = adam_halves("adam_" + n, w, m, v, red_r, oth_r, 256, blk0, sp)
    red_w, oth_w = rs_end(rs_win, big["wo_mem"][1])
    big["w_in"] = adam_halves("adam_win", w_in, m_w_in, v_w_in, red_w, oth_w, 256, 0, sp)
    names_s =["norm1_w", "gdn_conv_w", "gdn_a_log", "gdn_dt_bias", "gdn_norm_w", "ssm_conv_w", "ssm_conv_b",
               "ssm_a_log", "ssm_dt_bias", "ssm_d", "ssm_norm_w", "norm2_w", "mem_norm_w", "norm3_w", "final_norm_w"]
    w_s = [norm1_w, gdn_conv_w, gdn_a_log, gdn_dt_bias, gdn_norm_w, ssm_conv_w, ssm_conv_b, ssm_a_log, ssm_dt_bias,
           ssm_d, ssm_norm_w, norm2_w, mem_norm_w, norm3_w, final_norm_w]
    g_s = [gr_n1, gr_gcw, gr_galog, gr_gdtb, gr_gnw, gr_scw, gr_scb, gr_salog, gr_sdtb, gr_sd, gr_snw, gr_n2,
           gr_nmem, gr_n3, gr_final]
    m_s = [m_norm1_w, m_gdn_conv_w, m_gdn_a_log, m_gdn_dt_bias, m_gdn_norm_w, m_ssm_conv_w, m_ssm_conv_b, m_ssm_a_log,
           m_ssm_dt_bias, m_ssm_d, m_ssm_norm_w, m_norm2_w, m_mem_norm_w, m_norm3_w, m_final_norm_w]
    v_s = [v_norm1_w, v_gdn_conv_w, v_gdn_a_log, v_gdn_dt_bias, v_gdn_norm_w, v_ssm_conv_w, v_ssm_conv_b, v_ssm_a_log,
           v_ssm_dt_bias, v_ssm_d, v_ssm_norm_w, v_norm2_w, v_mem_norm_w, v_norm3_w, v_final_norm_w]
    shp_s = [w.shape for w in w_s]
    as2d = lambda a: a if a.ndim == 2 else a.reshape(1, -1)
    d_l, m_l, v_l = adam_small([as2d(a) for a in w_s], [as2d(a) for a in g_s], [as2d(a) for a in m_s],
                               [as2d(a) for a in v_s])

    grads, deltas, new_m, new_v = {}, {}, {}, {}
    for n, (gg, dd, mm_, vv_) in big.items():
        grads[n], deltas[n], new_m[n], new_v[n] = gg, dd, mm_, vv_
    for k, n in enumerate(names_s):
        grads[n] = g_s[k].reshape(shp_s[k])
        deltas[n], new_m[n], new_v[n] = (a[k].reshape(shp_s[k]) for a in (d_l, m_l, v_l))
    order = ["norm1_w", "w_in", "gdn_conv_w", "gdn_a_log", "gdn_dt_bias", "gdn_norm_w", "ssm_conv_w", "ssm_conv_b",
             "ssm_a_log", "ssm_dt_bias", "ssm_d", "ssm_norm_w", "w_out", "norm2_w", "mem_norm_w", "wq_mem", "wk_mem",
             "wv_mem", "wo_mem", "norm3_w", "w_up", "w_down", "final_norm_w"]
    return (loss, grad_x[None], *[grads[n] for n in order], *[deltas[n] for n in order],
            *[new_m[n] for n in order], *[new_v[n] for n in order])
```

```python
import numpy as np
import jax
import jax.numpy as jnp
from jax import lax
from jax.experimental import pallas as pl
from jax.experimental.pallas import tpu as pltpu

F32, BF16 = jnp.float32, jnp.bfloat16
MESH = pl.DeviceIdType.MESH
ANY = pl.BlockSpec(memory_space=pl.ANY)

EPS = 1e-6
D = 1024
GDN_H, GDN_DK, GDN_C = 8, 128, 64
SSM_H, SSM_P, SSM_N, SSM_L = 16, 64, 128, 128
MEM_H, MEM_DH = 4, 256
D_FF = 4096
IN_COLS = 6688
CB_QKV, CB_Z, CB_ZS, CB_XS, CB_BC, CB_BA, CB_DT = 0, 3, 4, 5, 12, 52, 53
VMEM_LIMIT = 56 * 1024 * 1024
D2D_CHUNKS = 8
ICI_CHUNKS = 4

ADAM_LR, ADAM_B1, ADAM_B2, ADAM_EPS, ADAM_WD, ADAM_STEP = 0.001, 0.9, 0.999, 1e-08, 0.01, 10


def _dg(a, b, ca, cb):
    return lax.dot_general(a, b, (((ca,), (cb,)), ((), ())), preferred_element_type=F32)


def _bf(x):
    return x.astype(BF16)


def mm(a, b):
    return _dg(_bf(a), _bf(b), 1, 0)


def mm_nt(a, b):
    return _dg(_bf(a), _bf(b), 1, 1)


def mm_tn(a, b):
    return _dg(_bf(a), _bf(b), 0, 0)


def mm_sel(a, sel):
    hi = a.astype(BF16)
    r1 = a - hi.astype(F32)
    mid = r1.astype(BF16)
    lo = (r1 - mid.astype(F32)).astype(BF16)
    s = sel.astype(BF16)
    return _dg(hi, s, 1, 0) + (_dg(mid, s, 1, 0) + _dg(lo, s, 1, 0))


def mm3(a, b):
    ah, bh = a.astype(BF16), b.astype(BF16)
    al, bl = (a - ah.astype(F32)).astype(BF16), (b - bh.astype(F32)).astype(BF16)
    return _dg(ah, bh, 1, 0) + (_dg(ah, bl, 1, 0) + _dg(al, bh, 1, 0))


def _iota(shape, dim):
    return lax.broadcasted_iota(jnp.int32, shape, dim)


def _chunk_cumsum(x, c):
    pos = _iota(x.shape, 0) & (c - 1)
    s = 1
    while s < c:
        x = x + jnp.where(pos >= s, pltpu.roll(x, s, 0), 0.0)
        s *= 2
    return x


def _chunk_revcumsum(x, c):
    n = x.shape[0]
    pos = _iota(x.shape, 0) & (c - 1)
    s = 1
    while s < c:
        x = x + jnp.where(pos < c - s, pltpu.roll(x, n - s, 0), 0.0)
        s *= 2
    return x


def _sig(x):
    return 1.0 / (1.0 + jnp.exp(-x))


def _softplus(x):
    return jnp.maximum(x, 0.0) + jnp.log(1.0 + jnp.exp(-jnp.abs(x)))


def _rows(v):
    return jnp.sum(v, axis=0, keepdims=True)


def _lanes(v):
    return jnp.sum(v, axis=1, keepdims=True)


def _sum_all(v):
    return _rows(_lanes(v))


def _cparams(sem):
    return pltpu.CompilerParams(dimension_semantics=sem, vmem_limit_bytes=VMEM_LIMIT)


def rowwise(fn, name, T, tb, row_ins, full_ins, row_outs, acc_outs=(), sp=None):
    nblk = T // tb
    assert nblk * tb == T
    has_sp = sp is not None

    def imap(f):
        return (lambda i, s: f(i, s)) if has_sp else (lambda i: f(i, None))

    in_specs, args = [], []
    for arr, w, cb, halo, off in row_ins:
        if halo == "col":
            in_specs.append(pl.BlockSpec((w, tb), imap(lambda i, s: (0, i))))
            args.append(arr)
            continue
        rowf = off if callable(off) else (lambda i, s, off=off: i + off)
        in_specs.append(pl.BlockSpec((tb, w), imap(lambda i, s, cb=cb, rowf=rowf: (rowf(i, s), cb))))
        args.append(arr)
        if halo == "prev":
            r = tb // 8
            in_specs.append(pl.BlockSpec((8, w), imap(lambda i, s, cb=cb, r=r: (jnp.maximum(i * r - 1, 0), cb))))
            args.append(arr)
        elif halo == "next":
            r, last = tb // 8, T // 8 - 1
            in_specs.append(pl.BlockSpec((8, w), imap(lambda i, s, cb=cb, r=r, last=last:
                                                      (jnp.minimum((i + 1) * r, last), cb))))
            args.append(arr)
    for arr in full_ins:
        in_specs.append(pl.BlockSpec(arr.shape, imap(lambda i, s, nd=arr.ndim: (0,) * nd)))
        args.append(arr)
    n_in, n_ro = len(args), len(row_outs)
    out_shape, out_specs, aliases = [], [], {}
    for k, (w, dt, *dest) in enumerate(row_outs):
        if dest:
            buf, cb = dest
            out_shape.append(jax.ShapeDtypeStruct(buf.shape, buf.dtype))
            out_specs.append(pl.BlockSpec((tb, w), imap(lambda i, s, cb=cb: (i, cb))))
            if not isinstance(buf, jax.ShapeDtypeStruct):
                aliases[len(args) + int(has_sp)] = k
                in_specs.append(ANY)
                args.append(buf)
        elif w < 0:
            out_shape.append(jax.ShapeDtypeStruct((-w, T), dt))
            out_specs.append(pl.BlockSpec((-w, tb), imap(lambda i, s: (0, i))))
        else:
            out_shape.append(jax.ShapeDtypeStruct((T, w), dt))
            out_specs.append(pl.BlockSpec((tb, w), imap(lambda i, s: (i, 0))))
    for shp in acc_outs:
        out_shape.append(jax.ShapeDtypeStruct(shp, F32))
        out_specs.append(pl.BlockSpec(shp, imap(lambda i, s, nd=len(shp): (0,) * nd)))

    def body(*refs):
        i = pl.program_id(0)
        if has_sp:
            sp_ref, refs = refs[0], refs[1:]
            vals = fn(i, nblk, sp_ref, *[r[...] for r in refs[:n_in]])
        else:
            vals = fn(i, nblk, *[r[...] for r in refs[:n_in]])
        outs = refs[n_in + len(aliases):]
        for ref, val in zip(outs[:n_ro], vals[:n_ro]):
            ref[...] = val.astype(ref.dtype)
        for ref, val in zip(outs[n_ro:], vals[n_ro:]):
            @pl.when(i == 0)
            def _(ref=ref, val=val):
                ref[...] = val

            @pl.when(i > 0)
            def _(ref=ref, val=val):
                ref[...] += val

    cparams = _cparams(("arbitrary",) if acc_outs else ("parallel",))
    if has_sp:
        return pl.pallas_call(
            body, name=name, out_shape=out_shape, compiler_params=cparams, input_output_aliases=aliases,
            grid_spec=pltpu.PrefetchScalarGridSpec(num_scalar_prefetch=1, grid=(nblk,), in_specs=in_specs,
                                                   out_specs=out_specs),
        )(sp, *args)
    return pl.pallas_call(
        body, name=name, grid=(nblk,), in_specs=in_specs, out_specs=out_specs, out_shape=out_shape,
        compiler_params=cparams, input_output_aliases=aliases,
    )(*args)


def R(arr, w=None, cb=0, halo=None, off=0):
    return (arr, arr.shape[1] if w is None else w, cb, halo, off)


def RC(arr):
    return (arr, arr.shape[0], 0, "col", 0)


def matmul(name, a, b, form, tm, tn, tk, out_dtypes, epi=None, extras=(), rows=(), into=None, n_acc=0):
    if form == "nn":
        (M, K), N = a.shape, b.shape[1]
    elif form == "nt":
        (M, K), N = a.shape, b.shape[0]
    else:
        (K, M), N = a.shape, b.shape[1]
    tm, tn, tk = min(tm, M), min(tn, N), min(tk, K)
    assert M % tm == 0 and N % tn == 0 and K % tk == 0, (name, M, N, K, tm, tn, tk)
    if form == "nn":
        a_spec = pl.BlockSpec((tm, tk), lambda i, j, k: (i, k))
        b_spec = pl.BlockSpec((tk, tn), lambda i, j, k: (k, j))
        ca, cb = 1, 0
    elif form == "nt":
        a_spec = pl.BlockSpec((tm, tk), lambda i, j, k: (i, k))
        b_spec = pl.BlockSpec((tn, tk), lambda i, j, k: (j, k))
        ca, cb = 1, 1
    else:
        a_spec = pl.BlockSpec((tk, tm), lambda i, j, k: (k, i))
        b_spec = pl.BlockSpec((tk, tn), lambda i, j, k: (k, j))
        ca, cb = 0, 0
    nk, ne, no = K // tk, len(extras) + len(rows), len(out_dtypes)
    if epi is None:
        epi = lambda acc: (acc,)

    assert n_acc == 0 or tn == N

    def body(a_ref, b_ref, *rest):
        ex, outs, accs, acc = rest[:ne], rest[ne:ne + no], rest[ne + no:ne + no + n_acc], rest[ne + no + n_acc]
        i, k = pl.program_id(0), pl.program_id(2)

        @pl.when(k == 0)
        def _():
            acc[...] = jnp.zeros_like(acc)

        acc[...] += _dg(_bf(a_ref[...]), _bf(b_ref[...]), ca, cb)

        @pl.when(k == nk - 1)
        def _():
            vals = epi(acc[...], *[e[...] for e in ex])
            for r, v in zip(outs, vals[:no]):
                r[...] = v.astype(r.dtype).reshape(r.shape)
            for r, v in zip(accs, vals[no:]):
                @pl.when(i == 0)
                def _(r=r, v=v):
                    r[...] = v

                @pl.when(i > 0)
                def _(r=r, v=v):
                    r[...] += v

    mn = pl.BlockSpec((tm, tn), lambda i, j, k: (i, j))
    rw = pl.BlockSpec((1, tn), lambda i, j, k: (0, j))
    if into is not None:
        buf, blk, bmap = into
        assert ne == 0 and no == 1
        aliased = not isinstance(buf, jax.ShapeDtypeStruct)

        def body_into(a_ref, b_ref, *rest):
            body(a_ref, b_ref, *rest[-2:])

        return pl.pallas_call(
            body_into, name=name, grid=(M // tm, N // tn, nk),
            in_specs=[a_spec, b_spec] + ([ANY] if aliased else []), out_specs=pl.BlockSpec(blk, bmap),
            out_shape=jax.ShapeDtypeStruct(buf.shape, buf.dtype),
            scratch_shapes=[pltpu.VMEM((tm, tn), F32)],
            input_output_aliases={2: 0} if aliased else {},
            compiler_params=_cparams(("parallel", "parallel", "arbitrary")),
        )(a, b, *([buf] if aliased else []))
    return pl.pallas_call(
        body, name=name, grid=(M // tm, N // tn, nk),
        in_specs=[a_spec, b_spec] + [mn] * len(extras) + [rw] * len(rows), out_specs=[mn] * no + [rw] * n_acc,
        out_shape=[jax.ShapeDtypeStruct((M, N), dt) for dt in out_dtypes] + [jax.ShapeDtypeStruct((1, N), F32)] * n_acc,
        scratch_shapes=[pltpu.VMEM((tm, tn), F32)],
        compiler_params=_cparams(("arbitrary",) * 3 if n_acc else ("parallel", "parallel", "arbitrary")),
    )(a, b, *extras, *rows)


def _epi_res(acc, res):
    return (res + acc,)


def _epi_rms_bwd(acc, x, dres, w):
    return rms_bwd_fn(0, 0, x, acc, dres, w)


def _epi_rms_bwd1(acc, x, dres, w):
    dx, _, gw = rms_bwd_fn(0, 0, x, acc, dres, w)
    return dx, gw


def _epi_final(acc, res, tgt, w):
    return final_fn(0, 0, res + acc, tgt, w)


def _epi_res_rms(acc, res, w):
    x = res + acc
    return (x, x * lax.rsqrt(jnp.mean(x * x, axis=-1, keepdims=True) + EPS) * w)


def _epi_relu2(acc):
    u = jnp.maximum(acc, 0.0)
    return (u, u * u)


def _epi_dup(acc, u):
    return (acc * 2.0 * u.astype(F32),)


def _conv(x, halo, w, i):
    halo = jnp.where(i == 0, 0.0, halo)
    xt = jnp.concatenate([halo, x], axis=0)
    shifted = [pltpu.roll(xt, 3 - k, 0)[8:, :] for k in range(3)] + [x]
    y = shifted[3] * w[3:4, :]
    for k in range(3):
        y = y + shifted[k] * w[k:k + 1, :]
    return y, shifted


def _l2n(x, scale):
    outs = []
    for h in range(x.shape[1] // 128):
        xh = x[:, 128 * h:128 * h + 128]
        outs.append(xh * (lax.rsqrt(jnp.sum(xh * xh, axis=-1, keepdims=True) + EPS) * scale))
    return jnp.concatenate(outs, axis=1)


def _l2n_bwd(x, dy, scale):
    outs = []
    for h in range(x.shape[1] // 128):
        xh, dh = x[:, 128 * h:128 * h + 128], dy[:, 128 * h:128 * h + 128] * scale
        r = lax.rsqrt(jnp.sum(xh * xh, axis=-1, keepdims=True) + EPS)
        outs.append(r * dh - xh * (r * r * r) * jnp.sum(xh * dh, axis=-1, keepdims=True))
    return jnp.concatenate(outs, axis=1)


def rms_fwd_fn(i, n, x, w):
    r = lax.rsqrt(jnp.mean(x * x, axis=-1, keepdims=True) + EPS)
    return (x * r * w,)


def rms_bwd_fn(i, n, x, dh, dres, w):
    r = lax.rsqrt(jnp.mean(x * x, axis=-1, keepdims=True) + EPS)
    g = dh * w
    dx = dres + r * g - x * (r * r * r) * jnp.mean(x * g, axis=-1, keepdims=True)
    return dx, dx, _rows(dh * x * r)


def rms_bwd_w_fn(i, n, x, dh, w):
    r = lax.rsqrt(jnp.mean(x * x, axis=-1, keepdims=True) + EPS)
    return (_rows(dh * x * r),)


def final_fn(i, n, x, tgt, w):
    r = lax.rsqrt(jnp.mean(x * x, axis=-1, keepdims=True) + EPS)
    xn = x * r
    e = xn * w - tgt
    dy = e * (1.0 / D)
    g = dy * w
    dx = r * g - x * (r * r * r) * jnp.mean(x * g, axis=-1, keepdims=True)
    return dx, dx, _rows(e * e), _rows(dy * xn)


def _gdn_gates(ba, alog_c, dtb_c):
    col = _iota(ba.shape, 1)
    amask = (col >= 8) & (col < 16)
    beta = jnp.where(col < 8, _sig(ba), 0.0)
    z = ba + dtb_c
    ea_ = jnp.exp(alog_c)
    return beta, z, ea_, jnp.where(amask, -ea_ * _softplus(z), 0.0), amask


def gdn_prep_fn(i, n, qkv, halo, ba, cw, alog_c, dtb_c, eb, ea):
    yc, _ = _conv(qkv, halo, cw, i)
    act = yc * _sig(yc)
    qn = _l2n(act[:, :D], GDN_DK ** -0.5)
    kn = _l2n(act[:, D:2 * D], 1.0)
    beta, _, _, g, _ = _gdn_gates(ba, alog_c, dtb_c)
    gcs = _chunk_cumsum(g, GDN_C)
    return qn, kn, act[:, 2 * D:], mm_sel(gcs, ea), mm_sel(beta, eb), jnp.transpose(gcs)[8:16, :]


def gdn_prep_bwd_fn(i, n, qkv, halo, ba, dqn, dkn, dv, dgcs_x, dbeta_x, dgcs_t, cw, alog_c, dtb_c, pb, pa):
    yc, shifted = _conv(qkv, halo, cw, i)
    sg = _sig(yc)
    act = yc * sg
    dq = _l2n_bwd(act[:, :D], dqn, GDN_DK ** -0.5)
    dk = _l2n_bwd(act[:, D:2 * D], dkn, 1.0)
    dyc = jnp.concatenate([dq, dk, dv], axis=1) * (sg * (1.0 + yc * (1.0 - sg)))
    dws = [_rows(dyc * shifted[k]) for k in range(4)]
    beta, z, ea_, g, amask = _gdn_gates(ba, alog_c, dtb_c)
    tbn = ba.shape[0]
    rowpart = jnp.transpose(jnp.concatenate([jnp.zeros((8, tbn), F32), dgcs_t, jnp.zeros((112, tbn), F32)], axis=0))
    dg = _chunk_revcumsum(mm_sel(dgcs_x, pa) - rowpart, GDN_C)
    draw = jnp.where(amask, dg * (-ea_) * _sig(z), 0.0)
    dba = draw + mm_sel(dbeta_x, pb) * beta * (1.0 - beta)
    return (dyc, dba, dws[0], dws[1], dws[2], dws[3], _rows(dg * g), _rows(draw))


def conv_bwd_fn(i, n, dyc, halo, w):
    halo = jnp.where(i == n - 1, 0.0, halo)
    tb = dyc.shape[0]
    xt = jnp.concatenate([dyc, halo], axis=0)
    dx = dyc * w[3:4, :]
    for k in range(3):
        dx = dx + pltpu.roll(xt, tb + 8 - (3 - k), 0)[:tb, :] * w[k:k + 1, :]
    return (dx,)


def gdn_post_fn(i, n, o, z, w):
    outs = []
    for h in range(GDN_H):
        oh, zh = o[:, 128 * h:128 * h + 128], z[:, 128 * h:128 * h + 128]
        r = lax.rsqrt(jnp.mean(oh * oh, axis=-1, keepdims=True) + EPS)
        outs.append(oh * r * w * (zh * _sig(zh)))
    return (jnp.concatenate(outs, axis=1),)


def gdn_post_bwd_fn(i, n, o, z, doa, w):
    dos, dzs, dw = [], [], None
    for h in range(GDN_H):
        sl = slice(128 * h, 128 * h + 128)
        oh, zh, dh = o[:, sl], z[:, sl], doa[:, sl]
        r = lax.rsqrt(jnp.mean(oh * oh, axis=-1, keepdims=True) + EPS)
        s = _sig(zh)
        dn = dh * (zh * s)
        dzs.append(dh * (oh * r * w) * (s * (1.0 + zh * (1.0 - s))))
        t = _rows(dn * oh * r)
        dw = t if dw is None else dw + t
        g = dn * w
        dos.append(r * g - oh * (r * r * r) * jnp.mean(oh * g, axis=-1, keepdims=True))
    return jnp.concatenate(dos, axis=1), jnp.concatenate(dzs, axis=1), dw


def _ssd_gates(dtblk, alog_c, dtb_c):
    hmask = _iota(dtblk.shape, 1) < SSM_H
    z = dtblk + dtb_c
    return jnp.where(hmask, _softplus(z), 0.0), -jnp.exp(alog_c), z, hmask


def ssd_prep_fn(i, n, xp, hx, bcp, hbc, dtblk, cwx, cwbc, cbx, cbbc, alog_c, dtb_c, e16):
    yx, _ = _conv(xp, hx, cwx, i)
    yx = yx + cbx
    ybc, _ = _conv(bcp, hbc, cwbc, i)
    ybc = ybc + cbbc
    dt, a_neg, _, _ = _ssd_gates(dtblk, alog_c, dtb_c)
    acs = _chunk_cumsum(dt * a_neg, SSM_L)
    return (yx * _sig(yx), ybc * _sig(ybc), mm_sel(dt, e16), mm_sel(acs, e16), jnp.transpose(acs)[0:SSM_H, :])


def ssd_prep_bwd_fn(i, n, xp, hx, bcp, hbc, dtblk, dxs_a, dxs_b, db, dc, dgate, dacs_t, cwx, cwbc, cbx, cbbc, alog_c, dtb_c):
    dbc = jnp.concatenate([db, dc], axis=1)
    yx, shx = _conv(xp, hx, cwx, i)
    yx = yx + cbx
    ybc, shbc = _conv(bcp, hbc, cwbc, i)
    ybc = ybc + cbbc
    sx, sbc = _sig(yx), _sig(ybc)
    dyx = (dxs_a + dxs_b) * (sx * (1.0 + yx * (1.0 - sx)))
    dybc = dbc * (sbc * (1.0 + ybc * (1.0 - sbc)))
    dwx = [_rows(dyx * shx[k]) for k in range(4)]
    dwbc = [_rows(dybc * shbc[k]) for k in range(4)]
    dt, a_neg, z, hmask = _ssd_gates(dtblk, alog_c, dtb_c)
    g0, g1 = dgate[:, :128], dgate[:, 128:]
    col = _iota(g0.shape, 1)
    lo, mid = col < 8, (col >= 8) & (col < 16)
    dacs_col = jnp.where(lo, g0, 0.0) + pltpu.roll(jnp.where(lo, g1, 0.0), 8, 1)
    ddt_dir = pltpu.roll(jnp.where(mid, g0, 0.0), 120, 1) + jnp.where(mid, g1, 0.0)
    tbn = dtblk.shape[0]
    rowpart = jnp.transpose(jnp.concatenate([dacs_t, jnp.zeros((128 - SSM_H, tbn), F32)], axis=0))
    da = _chunk_revcumsum(dacs_col - rowpart, SSM_L)
    draw = jnp.where(hmask, (ddt_dir + da * a_neg) * _sig(z), 0.0)
    return (dyx, dybc, draw, *dwx, *dwbc, _rows(dyx), _rows(dybc), _rows(da * dt * a_neg), _rows(draw))


def _ssd_gate(y, xs, zs, d_x):
    y2 = y + xs * d_x
    s = _sig(zs)
    return y2, s, y2 * (zs * s)


def ssd_post_fn(i, n, y, xs, zs, d_x, nw):
    _, _, yg = _ssd_gate(y, xs, zs, d_x)
    outs = []
    for g in range(2):
        v = yg[:, 512 * g:512 * g + 512]
        outs.append(v * lax.rsqrt(jnp.mean(v * v, axis=-1, keepdims=True) + EPS))
    return (jnp.concatenate(outs, axis=1) * nw,)


def ssd_post_bwd_fn(i, n, y, xs, zs, dob, d_x, nw):
    y2, s, yg = _ssd_gate(y, xs, zs, d_x)
    gfull = dob * nw
    dygs, dnw = [], []
    for g in range(2):
        sl = slice(512 * g, 512 * g + 512)
        v, gg = yg[:, sl], gfull[:, sl]
        r = lax.rsqrt(jnp.mean(v * v, axis=-1, keepdims=True) + EPS)
        dygs.append(r * gg - v * (r * r * r) * jnp.mean(v * gg, axis=-1, keepdims=True))
        dnw.append(_rows(dob[:, sl] * v * r))
    dyg = jnp.concatenate(dygs, axis=1)
    dy2 = dyg * (zs * s)
    dzs = dyg * y2 * (s * (1.0 + zs * (1.0 - s)))
    return dy2, dy2 * d_x, dzs, jnp.concatenate(dnw, axis=1), _rows(dy2 * xs)


def _attn_probs(q, k):
    hs = [slice(MEM_DH * h, MEM_DH * h + MEM_DH) for h in range(MEM_H)]
    ss = [mm_nt(q[:, sl], k[:, sl]) * (MEM_DH ** -0.5) for sl in hs]
    es = [jnp.exp(s - jnp.max(s, axis=-1, keepdims=True)) for s in ss]
    return hs, [e / jnp.sum(e, axis=-1, keepdims=True) for e in es]


def attn_fn(i, n, q, k, v):
    hs, ps = _attn_probs(q, k)
    return (jnp.concatenate([mm(p, v[:, sl]) for p, sl in zip(ps, hs)], axis=1),)


def attn_bwd_fn(i, n, q, do, k, v):
    hs, ps = _attn_probs(q, k)
    dvs = [mm_tn(p, do[:, sl]) for p, sl in zip(ps, hs)]
    dps = [mm_nt(do[:, sl], v[:, sl]) for sl in hs]
    dss = [p * (dp - jnp.sum(dp * p, axis=-1, keepdims=True)) * (MEM_DH ** -0.5) for p, dp in zip(ps, dps)]
    dqs = [mm(ds, k[:, sl]) for ds, sl in zip(dss, hs)]
    dks = [mm_tn(ds, q[:, sl]) for ds, sl in zip(dss, hs)]
    return jnp.concatenate(dqs, axis=1), jnp.concatenate(dks, axis=1), jnp.concatenate(dvs, axis=1)


def add2_fn(i, n, sp, a, b):
    return (a + b,)


def sum4_fn(i, n, sp, a, b, c, d):
    return (((a.astype(F32) + b.astype(F32)) + c.astype(F32)) + d.astype(F32),)


def _adamw(w, g, m, v):
    m = ADAM_B1 * m + (1.0 - ADAM_B1) * g
    v = ADAM_B2 * v + (1.0 - ADAM_B2) * (g * g)
    m_hat = m / (1.0 - ADAM_B1 ** ADAM_STEP)
    v_hat = v / (1.0 - ADAM_B2 ** ADAM_STEP)
    delta = -ADAM_LR * (m_hat / (jnp.sqrt(v_hat) + ADAM_EPS) + ADAM_WD * w)
    return delta, m, v


def _gdn_stage1(q, k, v, gcs, grow, bb):
    C = GDN_C
    row, col = _iota((C, C), 0), _iota((C, C), 1)
    incl, strict = row >= col, row > col
    dmat = jnp.where(incl, jnp.exp(jnp.minimum(gcs[:, :C] - grow, 0.0)), 0.0)
    gam = jnp.exp(gcs)
    gl = gcs[C - 1:C, :]
    kb, vb = k * bb, v * bb
    kg = kb * gam
    lmat = jnp.where(strict, mm_nt(kb, k) * dmat, 0.0)
    pmat = jnp.where(incl, mm_nt(q, k) * dmat, 0.0)
    return dict(q=q, k=k, v=v, bb=bb, incl=incl, strict=strict, dmat=dmat, gam=gam, kb=kb, vb=vb, kg=kg,
                lmat=lmat, pmat=pmat, qd=q * gam, kdec=jnp.exp(gl - gcs), cd=jnp.exp(gl))


def _gdn_inverse(lmats):
    C = GDN_C
    eye = (_iota((C, C), 0) == _iota((C, C), 1)).astype(F32)
    xs = [-l for l in lmats]
    ts = [eye + x for x in xs]
    for _ in range(5):
        xs = [mm(x, x) for x in xs]
        ts = [t + mm(t, x) for t, x in zip(ts, xs)]
    res = [eye - mm3(eye + l, t) for l, t in zip(lmats, ts)]
    return [t + mm(t, r) for t, r in zip(ts, res)]


def gdn_fwd(qn, kn, v, gcs_x, beta_x, gcs_t, tb, gh):
    T = qn.shape[0]
    nb, ncb, nc, C = T // tb, tb // GDN_C, T // GDN_C, GDN_C
    idx = [(hh, c) for hh in range(gh) for c in range(ncb)]

    def body(q_ref, k_ref, v_ref, g_ref, b_ref, gt_ref, o_ref, st_ref, ti_ref, s_scr):
        @pl.when(pl.program_id(1) == 0)
        def _():
            s_scr[...] = jnp.zeros_like(s_scr)

        grows = [gt_ref[hh] for hh in range(gh)]
        at = lambda hh, c: (slice(C * c, C * (c + 1)), slice(128 * hh, 128 * hh + 128))
        st1 = []
        for hh, c in idx:
            sl, ln = at(hh, c)
            st1.append(_gdn_stage1(q_ref[sl, ln], k_ref[sl, ln], v_ref[sl, ln], g_ref[sl, ln], grows[hh][:, sl],
                                   b_ref[sl, ln]))
        tinvs = _gdn_inverse([s["lmat"] for s in st1])
        us = [mm(t, s["vb"]) for t, s in zip(tinvs, st1)]
        ws = [mm(t, s["kg"]) for t, s in zip(tinvs, st1)]
        kds = [s["k"] * s["kdec"] for s in st1]
        ms = [mm_tn(kd, w) for kd, w in zip(kds, ws)]
        bs = [mm_tn(kd, u) for kd, u in zip(kds, us)]
        gs = [s["qd"] - mm(s["pmat"], w) for s, w in zip(st1, ws)]
        pus = [mm(s["pmat"], u) for s, u in zip(st1, us)]
        ss = [s_scr[hh] for hh in range(gh)]
        for c in range(ncb):
            for hh in range(gh):
                n, (sl, ln) = hh * ncb + c, at(hh, c)
                ti_ref[hh, sl, :] = tinvs[n]
                st_ref[hh, c] = ss[hh]
                o_ref[sl, ln] = mm(gs[n], ss[hh]) + pus[n]
                ss[hh] = st1[n]["cd"] * ss[hh] - mm(ms[n], ss[hh]) + bs[n]
        for hh in range(gh):
            s_scr[hh] = ss[hh]

    blk = pl.BlockSpec((tb, 128 * gh), lambda h, i: (i, h))
    return pl.pallas_call(
        body, name="gdn_fwd", grid=(GDN_H // gh, nb),
        in_specs=[blk] * 5 + [pl.BlockSpec((gh, 1, tb), lambda h, i: (h, 0, i))],
        out_specs=[blk, pl.BlockSpec((gh, ncb, 128, 128), lambda h, i: (h, i, 0, 0)),
                   pl.BlockSpec((gh, tb, C), lambda h, i: (h, i, 0))],
        out_shape=[jax.ShapeDtypeStruct((T, D), F32), jax.ShapeDtypeStruct((GDN_H, nc, 128, 128), F32),
                   jax.ShapeDtypeStruct((GDN_H, T, C), F32)],
        scratch_shapes=[pltpu.VMEM((gh, 128, 128), F32)],
        compiler_params=_cparams(("parallel", "arbitrary")),
    )(qn, kn, v, gcs_x, beta_x, gcs_t)


def gdn_bwd(qn, kn, v, gcs_x, beta_x, gcs_t, do, states, tinv, tb, gh):
    T = qn.shape[0]
    nb, ncb, C = T // tb, tb // GDN_C, GDN_C

    def body(q_ref, k_ref, v_ref, g_ref, b_ref, gt_ref, do_ref, st_ref, ti_ref,
             dq_ref, dk_ref, dv_ref, dgc_ref, db_ref, dgr_ref, ds_scr):
        @pl.when(pl.program_id(1) == 0)
        def _():
            ds_scr[...] = jnp.zeros_like(ds_scr)

        grows = [gt_ref[hh] for hh in range(gh)]
        at = lambda hh, c: (slice(C * c, C * (c + 1)), slice(128 * hh, 128 * hh + 128))
        lastrow = _iota((C, 1), 0) == C - 1
        idx = [(hh, c) for hh in range(gh) for c in range(ncb)]
        P = []
        for hh, c in idx:
            sl, ln = at(hh, c)
            lc = _gdn_stage1(q_ref[sl, ln], k_ref[sl, ln], v_ref[sl, ln], g_ref[sl, ln], grows[hh][:, sl],
                             b_ref[sl, ln])
            lc.update(tinv=ti_ref[hh, sl, :], s=st_ref[hh, c], do=do_ref[sl, ln], kd=lc["k"] * lc["kdec"])
            P.append(lc)
        for l, u, w in zip(P, [mm(l["tinv"], l["vb"]) for l in P], [mm(l["tinv"], l["kg"]) for l in P]):
            l.update(u=u, w=w)
        for l, x in zip(P, [mm(l["w"], l["s"]) for l in P]):
            l["vn"] = l["u"] - x
        for l, a, b, c_, d in zip(P, [mm_nt(l["do"], l["s"]) for l in P], [mm_nt(l["do"], l["vn"]) for l in P],
                                  [mm_tn(l["qd"], l["do"]) for l in P], [mm_tn(l["pmat"], l["do"]) for l in P]):
            l.update(dqd=a, dp=jnp.where(l["incl"], b, 0.0), ds_q=c_, dvn_p=d)
        pre = dict(zip(idx, P))
        rows = {}
        hs = range(gh)
        ds = [ds_scr[hh] for hh in hs]
        for c in reversed(range(ncb)):
            L = [pre[hh, c] for hh in hs]
            dvn = [l["dvn_p"] + mm(l["kd"], d) for l, d in zip(L, ds)]
            dkd = [mm_nt(l["vn"], d) for l, d in zip(L, ds)]
            dcd = [_sum_all(l["s"] * d) for l, d in zip(L, ds)]
            ds = [l["ds_q"] + l["cd"] * d - mm_tn(l["w"], x) for l, d, x in zip(L, ds, dvn)]
            dw = [-mm_nt(x, l["s"]) for l, x in zip(L, dvn)]
            dvb = [mm_tn(l["tinv"], x) for l, x in zip(L, dvn)]
            dkg = [mm_tn(l["tinv"], x) for l, x in zip(L, dw)]
            da = [-jnp.where(l["strict"], mm_nt(a, l["u"]) + mm_nt(b, l["w"]), 0.0) for l, a, b in zip(L, dvb, dkg)]
            dm = [a * l["dmat"] for l, a in zip(L, da)]
            dn = [l["dp"] * l["dmat"] for l in L]
            dkb = [mm(a, l["k"]) for l, a in zip(L, dm)]
            dq = [mm(a, l["k"]) + l["gam"] * l["dqd"] for l, a in zip(L, dn)]
            dk = [mm_tn(a, l["kb"]) + mm_tn(b, l["q"]) for l, a, b in zip(L, dm, dn)]
            for hh in hs:
                sl, ln = at(hh, c)
                l = L[hh]
                e = da[hh] * l["lmat"] + l["dp"] * l["pmat"]
                t_kd = _lanes(dkd[hh] * l["kd"])
                dgl = _sum_all(t_kd) + dcd[hh] * l["cd"][:, :1]
                dgcs = (_lanes(e) + _lanes(l["dqd"] * l["qd"]) - t_kd + _lanes(dkg[hh] * l["kg"])
                        + jnp.where(lastrow, dgl, 0.0))
                rows[hh, c] = _rows(e)
                dq_ref[sl, ln] = dq[hh]
                dk_ref[sl, ln] = (dk[hh] + l["kdec"] * dkd[hh] + l["bb"] * l["gam"] * dkg[hh] + l["bb"] * dkb[hh])
                dv_ref[sl, ln] = l["bb"] * dvb[hh]
                dbeta = _lanes(dkg[hh] * l["gam"] * l["k"]) + _lanes(dvb[hh] * l["v"]) + _lanes(dkb[hh] * l["k"])
                db_ref[sl, ln] = jnp.broadcast_to(dbeta, (C, 128))
                dgc_ref[sl, ln] = jnp.broadcast_to(dgcs, (C, 128))
        for hh in hs:
            ds_scr[hh] = ds[hh]
            dgr_ref[hh] = jnp.concatenate([rows[hh, c] for c in range(ncb)], axis=1)

    blk = pl.BlockSpec((tb, 128 * gh), lambda h, i: (nb - 1 - i, h))
    rowspec = pl.BlockSpec((gh, 1, tb), lambda h, i: (h, 0, nb - 1 - i))
    return pl.pallas_call(
        body, name="gdn_bwd", grid=(GDN_H // gh, nb),
        in_specs=[blk] * 5 + [rowspec, blk,
                              pl.BlockSpec((gh, ncb, 128, 128), lambda h, i: (h, nb - 1 - i, 0, 0)),
                              pl.BlockSpec((gh, tb, C), lambda h, i: (h, nb - 1 - i, 0))],
        out_specs=[blk] * 5 + [rowspec],
        out_shape=[jax.ShapeDtypeStruct((T, D), F32)] * 5 + [jax.ShapeDtypeStruct((GDN_H, 1, T), F32)],
        scratch_shapes=[pltpu.VMEM((gh, 128, 128), F32)],
        compiler_params=_cparams(("parallel", "arbitrary")),
    )(qn, kn, v, gcs_x, beta_x, gcs_t, do, states, tinv)


def _ssd_pair(x2, dt2, acs2):
    last = acs2[SSM_L - 1:SSM_L, :]
    return jnp.exp(acs2), jnp.exp(last - acs2), x2 * dt2


def _ssd_head(hh, acs2, arow, dec2, cbm, bm, incl, col):
    lmask = (col >= 64 * hh) & (col < 64 * hh + 64)
    sg = jnp.where(incl, jnp.exp(jnp.minimum(acs2[:, 64 * hh:64 * hh + 1] - arow, 0.0)), 0.0)
    dec_col = dec2[:, 64 * hh:64 * hh + 1]
    return lmask, sg, sg * cbm, dec_col, bm * dec_col


def ssd_fwd(xs, bc, dt_x, acs_x, acs_t):
    T = xs.shape[0]
    nc, L = T // SSM_L, SSM_L

    def body(x_ref, b_ref, c_ref, dt_ref, ac_ref, at_ref, y_ref, hst_ref, h_scr):
        @pl.when(pl.program_id(1) == 0)
        def _():
            h_scr[...] = jnp.zeros_like(h_scr)

        bm, cm = b_ref[...], c_ref[...]
        cbm = mm_nt(cm, bm)
        row, col = _iota((L, L), 0), _iota((L, L), 1)
        incl = row >= col
        P, H = [], []
        for pr in range(4):
            sl = slice(128 * pr, 128 * pr + 128)
            acs2 = ac_ref[:, sl]
            lam2, dec2, xd2 = _ssd_pair(x_ref[:, sl], dt_ref[:, sl], acs2)
            P.append(dict(sl=sl, lam2=lam2, xd2=xd2, hprev=h_scr[pr]))
            for hh in range(2):
                lmask, _, mmat, _, bd = _ssd_head(hh, acs2, at_ref[2 * pr + hh], dec2, cbm, bm, incl, col)
                H.append(dict(mmat=mmat, bd=bd, xdh=jnp.where(lmask, xd2, 0.0), xd2=xd2))
        ys = [mm(h["mmat"], h["xdh"]) for h in H]
        sts = [mm_tn(h["xd2"], h["bd"]) for h in H]
        zs = [mm_nt(cm, p["hprev"]) for p in P]
        for pr, p in enumerate(P):
            hst_ref[pr] = p["hprev"]
            y_ref[:, p["sl"]] = ys[2 * pr] + ys[2 * pr + 1] + p["lam2"] * zs[pr]
            lam_rows = jnp.where(row < 64, p["lam2"][L - 1:L, 0:1], p["lam2"][L - 1:L, 64:65])
            h_scr[pr] = lam_rows * p["hprev"] + jnp.where(row < 64, sts[2 * pr], sts[2 * pr + 1])

    return pl.pallas_call(
        body, name="ssd_fwd", grid=(2, nc),
        in_specs=[pl.BlockSpec((L, 512), lambda g, c: (c, g)),
                  pl.BlockSpec((L, 128), lambda g, c: (c, g)),
                  pl.BlockSpec((L, 128), lambda g, c: (c, 2 + g)),
                  pl.BlockSpec((L, 512), lambda g, c: (c, g)),
                  pl.BlockSpec((L, 512), lambda g, c: (c, g)),
                  pl.BlockSpec((8, 1, L), lambda g, c: (g, 0, c))],
        out_specs=[pl.BlockSpec((L, 512), lambda g, c: (c, g)),
                   pl.BlockSpec((None, None, 4, 128, 128), lambda g, c: (g, c, 0, 0, 0))],
        out_shape=[jax.ShapeDtypeStruct((T, D), F32), jax.ShapeDtypeStruct((2, nc, 4, 128, 128), F32)],
        scratch_shapes=[pltpu.VMEM((4, 128, 128), F32)],
        compiler_params=_cparams(("parallel", "arbitrary")),
    )(xs, bc, bc, dt_x, acs_x, acs_t)


def ssd_bwd(xs, bc, dt_x, acs_x, acs_t, dy, hstates):
    T = xs.shape[0]
    nc, L = T // SSM_L, SSM_L

    def body(x_ref, b_ref, c_ref, dt_ref, ac_ref, at_ref, dy_ref, hst_ref,
             dx_ref, db_ref, dc_ref, dgate_ref, dar_ref, dh_scr):
        @pl.when(pl.program_id(1) == 0)
        def _():
            dh_scr[...] = jnp.zeros_like(dh_scr)

        bm, cm = b_ref[...], c_ref[...]
        cbm = mm_nt(cm, bm)
        row, col = _iota((L, L), 0), _iota((L, L), 1)
        rowc = _iota((L, 1), 0)
        incl = row >= col
        prs = range(4)
        P = []
        for pr in prs:
            sl = slice(128 * pr, 128 * pr + 128)
            x2, dt2, dy2, acs2 = x_ref[:, sl], dt_ref[:, sl], dy_ref[:, sl], ac_ref[:, sl]
            lam2, dec2, xd2 = _ssd_pair(x2, dt2, acs2)
            P.append(dict(sl=sl, x2=x2, dt2=dt2, dy2=dy2, acs2=acs2, lam2=lam2, dec2=dec2, xd2=xd2,
                          hprev=hst_ref[pr], dhn=dh_scr[pr], dz=lam2 * dy2))
        zs = [mm_nt(cm, p["hprev"]) for p in P]
        dcm_t = [mm(p["dz"], p["hprev"]) for p in P]
        dh_z = [mm_tn(p["dz"], cm) for p in P]
        H = []
        for pr in prs:
            p = P[pr]
            p["yoff"] = p["dz"] * zs[pr]
            p["q_rows"] = _lanes(p["dhn"] * p["hprev"])
            for hh in range(2):
                lmask, sg, mmat, dec_col, bd = _ssd_head(hh, p["acs2"], at_ref[2 * pr + hh], p["dec2"], cbm, bm, incl, col)
                H.append(dict(p=p, hh=hh, j=2 * pr + hh, lmask=lmask, sg=sg, mmat=mmat, dec_col=dec_col, bd=bd))
        dms = [mm_nt(jnp.where(h["lmask"], h["p"]["dy2"], 0.0), h["p"]["xd2"]) for h in H]
        a1s = [mm_tn(h["mmat"], h["p"]["dy2"]) for h in H]
        a2s = [mm_nt(h["bd"], h["p"]["dhn"]) for h in H]
        dbds = [mm(jnp.where(h["lmask"], h["p"]["xd2"], 0.0), h["p"]["dhn"]) for h in H]
        dcb = jnp.zeros((L, L), F32)
        dbm = jnp.zeros((L, SSM_N), F32)
        comp = jnp.zeros((L, 128), F32)
        dxd = [jnp.zeros((L, 128), F32) for _ in prs]
        for h, dm_raw, a1, a2, dbd in zip(H, dms, a1s, a2s, dbds):
            p, hh, j = h["p"], h["hh"], h["j"]
            dm = jnp.where(incl, dm_raw, 0.0)
            dcb = dcb + dm * h["sg"]
            e = dm * h["mmat"]
            dxd_h = jnp.where(h["lmask"], a1 + a2, 0.0)
            dxd[j // 2] = dxd[j // 2] + dxd_h
            dbm = dbm + h["dec_col"] * dbd
            t = _lanes(dbd * h["bd"])
            lam_h = p["lam2"][L - 1:L, 64 * hh:64 * hh + 1]
            in_head = (rowc >= 64 * hh) & (rowc < 64 * hh + 64)
            add_last = _sum_all(t) + _sum_all(jnp.where(in_head, p["q_rows"], 0.0)) * lam_h
            dacs_col = (_lanes(jnp.where(h["lmask"], p["yoff"], 0.0)) + _lanes(e) - t
                        + jnp.where(rowc == L - 1, add_last, 0.0))
            ddt_col = _lanes(dxd_h * p["x2"])
            dar_ref[j] = _rows(e)
            comp = comp + jnp.where(col == j, dacs_col, 0.0) + jnp.where(col == 8 + j, ddt_col, 0.0)
        dcm = dcm_t[0]
        for pr in prs:
            p = P[pr]
            if pr:
                dcm = dcm + dcm_t[pr]
            lam_rows = jnp.where(row < 64, p["lam2"][L - 1:L, 0:1], p["lam2"][L - 1:L, 64:65])
            dh_scr[pr] = dh_z[pr] + lam_rows * p["dhn"]
            dx_ref[:, p["sl"]] = p["dt2"] * dxd[pr]
        db_ref[...] = dbm + mm_tn(dcb, cm)
        dc_ref[...] = dcm + mm(dcb, bm)
        dgate_ref[...] = comp

    rv = lambda g, c: (nc - 1 - c, g)
    rowspec = pl.BlockSpec((8, 1, L), lambda g, c: (g, 0, nc - 1 - c))
    return pl.pallas_call(
        body, name="ssd_bwd", grid=(2, nc),
        in_specs=[pl.BlockSpec((L, 512), rv),
                  pl.BlockSpec((L, 128), rv),
                  pl.BlockSpec((L, 128), lambda g, c: (nc - 1 - c, 2 + g)),
                  pl.BlockSpec((L, 512), rv),
                  pl.BlockSpec((L, 512), rv),
                  rowspec,
                  pl.BlockSpec((L, 512), rv),
                  pl.BlockSpec((None, None, 4, 128, 128), lambda g, c: (g, nc - 1 - c, 0, 0, 0))],
        out_specs=[pl.BlockSpec((L, 512), rv), pl.BlockSpec((L, 128), rv), pl.BlockSpec((L, 128), rv),
                   pl.BlockSpec((L, 128), rv), rowspec],
        out_shape=[jax.ShapeDtypeStruct((T, D), F32), jax.ShapeDtypeStruct((T, 256), F32),
                   jax.ShapeDtypeStruct((T, 256), F32), jax.ShapeDtypeStruct((T, 256), F32),
                   jax.ShapeDtypeStruct((SSM_H, 1, T), F32)],
        scratch_shapes=[pltpu.VMEM((4, 128, 128), F32)],
        compiler_params=_cparams(("parallel", "arbitrary")),
    )(xs, bc, bc, dt_x, acs_x, acs_t, dy, hstates)


def _pos():
    return lax.axis_index("x"), lax.axis_index("y"), lax.axis_index("c")


def _other_chips(x, y):
    return [(1 - x, y), (x, 1 - y), (1 - x, 1 - y)]


def _rcopy(src, dst, ssem, rsem, dev):
    return pltpu.make_async_remote_copy(src_ref=src, dst_ref=dst, send_sem=ssem, recv_sem=rsem,
                                        device_id=dev, device_id_type=MESH)


def _rows_at(start, n):
    return pl.ds(pl.multiple_of(start, 8), n)


def _comm_call(body, name, out_shape, n_in, scratch):
    return pl.pallas_call(
        body, name=name, out_shape=out_shape, in_specs=[ANY] * n_in,
        out_specs=[ANY] * len(out_shape) if isinstance(out_shape, (list, tuple)) else ANY,
        scratch_shapes=scratch,
        compiler_params=pltpu.CompilerParams(has_side_effects=True),
    )


def _dma_sems(n):
    return pltpu.SemaphoreType.DMA((n,))


def ag_chips(name, shard):
    rr, cc = shard.shape
    h, nq = rr // 2, ICI_CHUNKS
    hq = h // nq

    def body(x_ref, out_ref, ssem, rsem):
        x, y, c = _pos()
        chips = _other_chips(x, y)
        started = []
        for q in range(nq):
            rows = _rows_at(c * h + q * hq, hq)
            for j, (cx, cy) in enumerate(chips):
                cp = _rcopy(x_ref.at[rows], out_ref.at[j, rows], ssem.at[j * nq + q], rsem.at[j * nq + q], (cx, cy, c))
                cp.start()
                started.append(cp)
        for q in range(nq):
            rows = _rows_at(c * h + q * hq, hq)
            for j, (cx, cy) in enumerate(chips):
                blk = out_ref.at[j, rows]
                _rcopy(blk, blk, ssem.at[j * nq + q], rsem.at[j * nq + q], (cx, cy, c)).wait_recv()
                k = 3 * nq + j * nq + q
                cp = _rcopy(blk, blk, ssem.at[k], rsem.at[k], (x, y, 1 - c))
                cp.start()
                started.append(cp)
        for q in range(nq):
            rows = _rows_at((1 - c) * h + q * hq, hq)
            for j in range(3):
                blk = out_ref.at[j, rows]
                k = 3 * nq + j * nq + q
                _rcopy(blk, blk, ssem.at[k], rsem.at[k], (x, y, 1 - c)).wait_recv()
        for cp in started:
            cp.wait_send()

    return _comm_call(body, name, jax.ShapeDtypeStruct((3, rr, cc), shard.dtype), 1,
                      [_dma_sems(6 * nq), _dma_sems(6 * nq)])(shard)


def _by_chip(shard, got, s_me):
    by_rel = jnp.stack([shard, got[1], got[0], got[2]])
    return jnp.take(by_rel, jnp.arange(4) ^ s_me, axis=0)


def all_gather_chips(name, shard, s_me):
    return _by_chip(shard, ag_chips(name, shard), s_me)


HBM_SPEC = pl.BlockSpec(memory_space=pltpu.HBM)
SEM_SPEC = pl.BlockSpec(memory_space=pltpu.SEMAPHORE)
SPLIT_EFFECT = pltpu.SideEffectType.DATAFLOW_SIDE_EFFECTING


def _split_copies(pieces, x_ref, land_ref, sems):
    x, y, c = _pos()
    return [_rcopy(s, d, sems[j], sems[3 + j], dev) for j, (s, d, dev) in enumerate(pieces(x_ref, land_ref, x, y, c))]


def split_copy_start(name, src, land_shape, pieces, after):
    def body(x_ref, land_ref, after_ref, *outs):
        for cp in _split_copies(pieces, x_ref, land_ref, outs[:6]):
            cp.start()
        outs[8][...] = jnp.zeros_like(outs[8])

    dma = pltpu.SemaphoreType.DMA(())
    res = pl.pallas_call(
        body, name=name,
        out_shape=(dma,) * 6 + (pltpu.HBM(src.shape, src.dtype), pltpu.HBM(land_shape, src.dtype),
                                jax.ShapeDtypeStruct((8, 128), F32)),
        in_specs=(HBM_SPEC, HBM_SPEC, ANY),
        out_specs=(SEM_SPEC,) * 6 + (HBM_SPEC, HBM_SPEC, pl.BlockSpec(memory_space=pltpu.VMEM)),
        input_output_aliases={0: 6, 1: 7},
        compiler_params=pltpu.CompilerParams(has_side_effects=SPLIT_EFFECT),
    )(pltpu.with_memory_space_constraint(src, pltpu.HBM),
      pltpu.with_memory_space_constraint(lax.empty(land_shape, src.dtype), pltpu.HBM), after)
    return res[:6], res[6], res[7], res[8]


def split_copy_wait(name, sems, src_thru, land_thru, after, pieces):
    def body(x_ref, land_ref, *rest):
        for cp in _split_copies(pieces, x_ref, land_ref, rest[:6]):
            cp.wait_send()
            cp.wait_recv()

    return pl.pallas_call(
        body, name=name,
        out_shape=(pltpu.HBM(src_thru.shape, src_thru.dtype), pltpu.HBM(land_thru.shape, land_thru.dtype)),
        in_specs=(HBM_SPEC, HBM_SPEC) + (SEM_SPEC,) * 6 + (ANY,), out_specs=(HBM_SPEC, HBM_SPEC),
        input_output_aliases={0: 0, 1: 1},
        compiler_params=pltpu.CompilerParams(has_side_effects=SPLIT_EFFECT),
    )(src_thru, land_thru, *sems, after)


def ag_pieces(h):
    def pieces(x_ref, land_ref, x, y, c):
        rows = _rows_at(c * h, h)
        return [(x_ref.at[rows], land_ref.at[j, rows], (cx, cy, c)) for j, (cx, cy) in enumerate(_other_chips(x, y))]
    return pieces


def rs_pieces(x_ref, land_ref, x, y, c):
    return [(x_ref.at[2 * cx + cy], land_ref.at[j], (cx, cy, c)) for j, (cx, cy) in enumerate(_other_chips(x, y))]


def ag_forward(name, got):
    _, rr, cc = got.shape
    h, nq = rr // 2, D2D_CHUNKS
    hq = h // nq

    def body(g_ref, out_ref, ssem, rsem):
        x, y, c = _pos()
        cps = []
        for j in range(3):
            for q in range(nq):
                blk = out_ref.at[j, _rows_at(c * h + q * hq, hq)]
                cp = _rcopy(blk, blk, ssem.at[j * nq + q], rsem.at[j * nq + q], (x, y, 1 - c))
                cp.start()
                cps.append(cp)
        for cp in cps:
            cp.wait_send()
        for j in range(3):
            for q in range(nq):
                blk = out_ref.at[j, _rows_at((1 - c) * h + q * hq, hq)]
                _rcopy(blk, blk, ssem.at[j * nq + q], rsem.at[j * nq + q], (x, y, 1 - c)).wait_recv()

    return pl.pallas_call(
        body, name=name, out_shape=jax.ShapeDtypeStruct(got.shape, got.dtype), in_specs=[ANY], out_specs=ANY,
        scratch_shapes=[_dma_sems(3 * nq), _dma_sems(3 * nq)], input_output_aliases={0: 0},
        compiler_params=pltpu.CompilerParams(has_side_effects=True),
    )(got)


def rs_pair(name, g):
    _, rr, cc = g.shape
    h, nq = rr // 2, D2D_CHUNKS
    hq = h // nq

    def body(g_ref, recv_ref, ssem, rsem):
        x, y, c = _pos()
        cps = []
        for q in range(nq):
            cp = _rcopy(g_ref.at[:, _rows_at((1 - c) * h + q * hq, hq), :], recv_ref.at[:, pl.ds(q * hq, hq), :],
                        ssem.at[q], rsem.at[q], (x, y, 1 - c))
            cp.start()
            cps.append(cp)
        for cp in cps:
            cp.wait()

    return _comm_call(body, name, jax.ShapeDtypeStruct((4, h, cc), g.dtype), 1, [_dma_sems(nq), _dma_sems(nq)])(g)


def rs_chips(name, p):
    _, h, cc = p.shape
    nq = ICI_CHUNKS
    hq = h // nq

    def body(p_ref, buf_ref, ssem, rsem):
        x, y, c = _pos()
        sends = []
        for q in range(nq):
            rows = pl.ds(q * hq, hq)
            for j, (cx, cy) in enumerate(_other_chips(x, y)):
                cp = _rcopy(p_ref.at[2 * cx + cy, rows], buf_ref.at[j, rows], ssem.at[j * nq + q],
                            rsem.at[j * nq + q], (cx, cy, c))
                cp.start()
                sends.append(cp)
        for cp in sends:
            cp.wait()

    return _comm_call(body, name, jax.ShapeDtypeStruct((3, h, cc), p.dtype), 1,
                      [_dma_sems(3 * nq), _dma_sems(3 * nq)])(p)


def rs_join(name, half):
    h, cc = half.shape
    nq = D2D_CHUNKS
    hq = h // nq

    def body(h_ref, out_ref, ssem, rsem):
        x, y, c = _pos()
        cps = []
        for q in range(nq):
            rows = pl.ds(q * hq, hq)
            cp = _rcopy(h_ref.at[rows], out_ref.at[rows], ssem.at[q], rsem.at[q], (x, y, 1 - c))
            cp.start()
            cps.append(cp)
        for cp in cps:
            cp.wait()

    return _comm_call(body, name, jax.ShapeDtypeStruct((h, cc), half.dtype), 1, [_dma_sems(nq), _dma_sems(nq)])(half)


def reduce_scatter(tag, g, tb, sp):
    return rs_end(rs_begin(tag, g, tb, sp, False), None)


def rs_begin(tag, g, tb, sp, split, after=None):
    _, rr, cc = g.shape
    h = rr // 2
    nbh = h // tb
    recv = rs_pair(tag + "_pair", g)
    mine_rows = lambda i, s: (i // nbh) * (2 * nbh) + s[0] * nbh + i % nbh
    part = rowwise(add2_fn, tag + "_add", 4 * h, tb, [R(g.reshape(4 * rr, cc), off=mine_rows), R(recv.reshape(4 * h, cc))],
                   [], [(cc, BF16)], sp=sp)[0].reshape(4, h, cc)
    st = dict(tag=tag, tb=tb, sp=sp, split=split, part=part)
    if split:
        st["sems"], st["part"], st["land"], st["token"] = split_copy_start(tag + "_start", part, (3, h, cc), rs_pieces,
                                                                           sp if after is None else after)
    return st


def rs_end(st, after):
    tag, tb, sp, part = st["tag"], st["tb"], st["sp"], st["part"]
    _, h, cc = part.shape
    nbh = h // tb
    if st["split"]:
        part, buf = split_copy_wait(tag + "_wait", st["sems"], part, st["land"], after, rs_pieces)
    else:
        buf = rs_chips(tag + "_chips", part)
    red = rowwise(sum4_fn, tag + "_sum", h, tb,
                  [R(part.reshape(4 * h, cc), off=lambda i, s: s[1] * nbh + i)]
                  + [R(buf.reshape(3 * h, cc), off=k * nbh) for k in range(3)],
                  [], [(cc, F32)], sp=sp)[0]
    return red, rs_join(tag + "_join", red)


def adam_halves(name, w, m, v, red, other, tb, blk0, sp):
    nbh = red.shape[0] // tb

    def fn(i, n, s, w_, m_, v_, r_, o_):
        g = jnp.where((blk0 + i) // nbh == s[0], r_, o_)
        return (g,) + _adamw(w_, g, m_, v_)

    half_rows = lambda i, s: (blk0 + i) % nbh
    return rowwise(fn, name, w.shape[0], tb, [R(w), R(m), R(v), R(red, off=half_rows), R(other, off=half_rows)],
                   [], [(w.shape[1], F32)] * 4, sp=sp)


SMALL_LANES = 3 * D


def all_reduce_items(name, items):
    flat = [a for it in items for a in it]
    shapes = [(sum(a.shape[0] for a in it), it[0].shape[1]) for it in items]
    nrows = -(-sum(s[0] for s in shapes) // 8) * 8

    def body(*refs):
        ins, outs = refs[:len(flat)], refs[len(flat):len(flat) + len(items)]
        mine, buf, ssem, rsem = refs[len(flat) + len(items):]
        x, y, c = _pos()
        me = 4 * x + 2 * y + c
        mine[...] = jnp.zeros_like(mine)
        r = 0
        for ref in ins:
            mine[r:r + ref.shape[0], 0:ref.shape[1]] = ref[...]
            r += ref.shape[0]
        buf[me] = mine[...]
        cps = []
        for k in range(1, 8):
            dev = (x ^ (k >> 2), y ^ ((k >> 1) & 1), c ^ (k & 1))
            cp = _rcopy(mine, buf.at[me], ssem.at[k - 1], rsem.at[k - 1], dev)
            cp.start()
            cps.append(cp)
        for cp in cps:
            cp.wait()
        r = 0
        for (nr, n), out in zip(shapes, outs):
            acc = buf[0, r:r + nr, 0:n]
            for d in range(1, 8):
                acc = acc + buf[d, r:r + nr, 0:n]
            out[...] = acc
            r += nr

    vm = pl.BlockSpec(memory_space=pltpu.VMEM)
    return pl.pallas_call(
        body, name=name, out_shape=[jax.ShapeDtypeStruct(s, F32) for s in shapes],
        in_specs=[vm] * len(flat), out_specs=[vm] * len(items),
        scratch_shapes=[pltpu.VMEM((nrows, SMALL_LANES), F32), pltpu.VMEM((8, nrows, SMALL_LANES), F32),
                        _dma_sems(7), _dma_sems(7)],
        compiler_params=pltpu.CompilerParams(has_side_effects=True),
    )(*flat)


def adam_small(ws, gs, ms, vs):
    n = len(ws)

    def body(*refs):
        for k in range(n):
            w, g, m, v = (refs[j * n + k][...] for j in range(4))
            for j, val in enumerate(_adamw(w, g, m, v)):
                refs[(4 + j) * n + k][...] = val

    vm = pl.BlockSpec(memory_space=pltpu.VMEM)
    res = pl.pallas_call(
        body, name="adam_small", out_shape=[jax.ShapeDtypeStruct(w.shape, F32) for w in ws] * 3,
        in_specs=[vm] * (4 * n), out_specs=[vm] * (3 * n),
    )(*ws, *gs, *ms, *vs)
    return res[:n], res[n:2 * n], res[2 * n:]


def _sel(rows, cols, pairs):
    m = np.zeros((rows, cols), np.float32)
    for r, c in pairs:
        m[r, c] = 1.0
    return jnp.asarray(m)


def _pad_win(w):
    z = jnp.zeros((w.shape[0], 112), w.dtype)
    return jnp.concatenate([w[:, :4096], w[:, 4112:6672], w[:, 4096:4112], z, w[:, 6672:6688], z], axis=1)


def _unpad_win(wp):
    return jnp.concatenate([wp[:, :4096], wp[:, 6656:6672], wp[:, 4096:6656], wp[:, 6784:6800]], axis=1)


def kernel(x, mem, norm1_w, w_in, gdn_conv_w, gdn_a_log, gdn_dt_bias, gdn_norm_w, ssm_conv_w, ssm_conv_b, ssm_a_log, ssm_dt_bias, ssm_d, ssm_norm_w, w_out, norm2_w, mem_norm_w, wq_mem, wk_mem, wv_mem, wo_mem, norm3_w, w_up, w_down, final_norm_w, loss_target, m_norm1_w, m_w_in, m_gdn_conv_w, m_gdn_a_log, m_gdn_dt_bias, m_gdn_norm_w, m_ssm_conv_w, m_ssm_conv_b, m_ssm_a_log, m_ssm_dt_bias, m_ssm_d, m_ssm_norm_w, m_w_out, m_norm2_w, m_mem_norm_w, m_wq_mem, m_wk_mem, m_wv_mem, m_wo_mem, m_norm3_w, m_w_up, m_w_down, m_final_norm_w, v_norm1_w, v_w_in, v_gdn_conv_w, v_gdn_a_log, v_gdn_dt_bias, v_gdn_norm_w, v_ssm_conv_w, v_ssm_conv_b, v_ssm_a_log, v_ssm_dt_bias, v_ssm_d, v_ssm_norm_w, v_w_out, v_norm2_w, v_mem_norm_w, v_wq_mem, v_wk_mem, v_wv_mem, v_wo_mem, v_norm3_w, v_w_up, v_w_down, v_final_norm_w):
    T, M = x.shape[1], mem.shape[1]
    xi, yi, ci = _pos()
    s_me = 2 * xi + yi
    x0, mem0, tgt = x[0], mem[0], loss_target[0]
    tb = min(256, T)
    row = lambda v: v.reshape(1, -1)

    win_g = all_gather_chips("ag_win", w_in.astype(BF16), s_me)
    w_in_p = _pad_win(win_g.transpose(1, 0, 2).reshape(D, IN_COLS))
    keep = (ci == 0).astype(F32)
    gcw_z = lax.dynamic_update_slice(jnp.zeros((4, 3 * D), F32), gdn_conv_w * keep, (0, s_me * 768))
    scw_z = lax.dynamic_update_slice(jnp.zeros((4, 1536), F32), ssm_conv_w * keep, (0, s_me * 384))
    gcw, scw = all_reduce_items("ar_convw", [[gcw_z], [scw_z]])
    scw_x, scw_bc = scw[:, :D], scw[:, D:]
    rest_shard = jnp.concatenate([w_up, w_down, w_out, wq_mem, wk_mem, wv_mem, wo_mem], axis=0).astype(BF16)
    ag_sems, rest_thru, rest_land, ag_token = split_copy_start("ag_rest_start", rest_shard, (3,) + rest_shard.shape,
                                                               ag_pieces(rest_shard.shape[0] // 2), gcw)
    sp = jnp.stack([ci, s_me]).astype(jnp.int32)
    scb_x, scb_bc = row(ssm_conv_b[:D]), row(ssm_conv_b[D:])

    galog_c, gdtb_c = row(jnp.pad(gdn_a_log, (8, 112))), row(jnp.pad(gdn_dt_bias, (8, 112)))
    salog_c, sdtb_c = row(jnp.pad(ssm_a_log, (0, 112))), row(jnp.pad(ssm_dt_bias, (0, 112)))
    sd_x = row(jnp.repeat(ssm_d, 64))
    eb = _sel(128, D, [(h, 128 * h + l) for h in range(8) for l in range(128)])
    ea = _sel(128, D, [(8 + h, 128 * h + l) for h in range(8) for l in range(128)])
    e16 = _sel(128, D, [(h, 64 * h + l) for h in range(16) for l in range(64)])
    pb = _sel(D, 128, [(128 * h, h) for h in range(8)])
    pa = _sel(D, 128, [(128 * h, 8 + h) for h in range(8)])

    h1 = rowwise(rms_fwd_fn, "rms1", T, tb, [R(x0)], [row(norm1_w) + ag_token[0:1, 0:1]], [(D, BF16)])[0]
    p = matmul("mm_in", h1, w_in_p, "nn", 1024, 768, 1024, [F32])[0]
    gp_ins = [R(p, 3 * D, CB_QKV, "prev"), R(p, 128, CB_BA)]
    qn, kn, vv, gcs_x, beta_x, gcs_t = rowwise(gdn_prep_fn, "gdn_prep", T, tb, gp_ins,
                                               [gcw, galog_c, gdtb_c, eb, ea], [(D, F32)] * 5 + [(-8, F32)])
    gcs_t = gcs_t.reshape(GDN_H, 1, T)
    gtb, ggh = min(128, T), 8
    o_gdn, s_states, tinv = gdn_fwd(qn, kn, vv, gcs_x, beta_x, gcs_t, gtb, ggh)
    gnw = row(gdn_norm_w)
    oa = rowwise(gdn_post_fn, "gdn_post", T, tb, [R(o_gdn), R(p, D, CB_Z)], [gnw], [(D, BF16)])[0]
    sp_ins = [R(p, D, CB_XS, "prev"), R(p, 512, CB_BC, "prev"), R(p, 128, CB_DT)]
    sp_full = [scw_x, scw_bc, scb_x, scb_bc, salog_c, sdtb_c]
    xs, bc, dt_x, acs_x, acs_t = rowwise(ssd_prep_fn, "ssd_prep", T, tb, sp_ins, sp_full + [e16],
                                         [(D, F32), (512, F32), (D, F32), (D, F32), (-SSM_H, F32)])
    acs_t = acs_t.reshape(SSM_H, 1, T)
    y_ssd, h_states = ssd_fwd(xs, bc, dt_x, acs_x, acs_t)
    snw = row(ssm_norm_w)
    ob = rowwise(ssd_post_fn, "ssd_post", T, tb, [R(y_ssd), R(xs), R(p, D, CB_ZS)], [sd_x, snw], [(D, BF16)])[0]
    rest_thru, rest_land = split_copy_wait("ag_rest_wait", ag_sems, rest_thru, rest_land, ob,
                                           ag_pieces(rest_shard.shape[0] // 2))
    rest_g = _by_chip(rest_thru, ag_forward("ag_rest_fwd", rest_land), s_me)
    wup_f = rest_g[:, 0:1024].transpose(1, 0, 2).reshape(D, D_FF)
    wdown_f = rest_g[:, 1024:2048].reshape(D_FF, D)
    wout_f = rest_g[:, 2048:2560].reshape(2 * D, D)
    wq_f, wk_f, wv_f, wo_f = (rest_g[:, 2560 + 256 * k:2816 + 256 * k].reshape(D, D) for k in range(4))
    x1a = matmul("mm_out_a", oa, wout_f[:D], "nn", 1024, 1024, 1024, [F32], _epi_res, [x0])[0]
    assert D == 1024
    x1, h2 = matmul("mm_out_b", ob, wout_f[D:], "nn", 1024, 1024, 1024, [F32, BF16], _epi_res_rms, [x1a],
                    [row(norm2_w)])

    mn = rowwise(rms_fwd_fn, "rms_mem", M, M, [R(mem0)], [row(mem_norm_w)], [(D, BF16)])[0]
    km = matmul("mm_k", mn, wk_f, "nn", 256, 1024, 1024, [BF16])[0]
    vm = matmul("mm_v", mn, wv_f, "nn", 256, 1024, 1024, [BF16])[0]
    qm = matmul("mm_q", h2, wq_f, "nn", 1024, 1024, 1024, [BF16])[0]
    ao = rowwise(attn_fn, "attn", T, tb, [R(qm)], [km, vm], [(D, BF16)])[0]
    x2, h3 = matmul("mm_o", ao, wo_f, "nn", 1024, 1024, 1024, [F32, BF16], _epi_res_rms, [x1], [row(norm3_w)])
    u, act = matmul("mm_up", h3, wup_f, "nn", 1024, 1024, 1024, [BF16, BF16], _epi_relu2)
    dx3, dx3b, loss_lane, g_final = matmul("mm_down", act, wdown_f, "nn", 512, 1024, 1024, [F32, BF16], _epi_final,
                                           [x2, tgt], [row(final_norm_w)], n_acc=2)
    loss = lax.psum(0.5 / D * jnp.sum(loss_lane), ("x", "y", "c"))

    dup = matmul("mm_dact", dx3b, wdown_f, "nt", 1024, 1024, 1024, [BF16], _epi_dup, [u])[0]
    def g_into(buf, blk, at):
        return dict(into=(buf, blk, lambda i, j, k, at=at: at(i, j)))

    grest = jax.ShapeDtypeStruct((4, 3584, D), F32)
    grest = matmul("mm_gdown", act, dx3b, "tn", 1024, 1024, 1024, [F32],
                   **g_into(grest, (None, 1024, D), lambda i, j: (i, 1, 0)))
    dx2, dx2b, g_n3 = matmul("mm_dh3", dup, wup_f, "nt", 512, 1024, 1024, [F32, BF16], _epi_rms_bwd, [x2, dx3],
                             [row(norm3_w)], n_acc=1)
    grest = matmul("mm_gup", h3, dup, "tn", 1024, 1024, 1024, [F32],
                   **g_into(grest, (None, 1024, D), lambda i, j: (j, 0, 0)))
    dao = matmul("mm_dao", dx2b, wo_f, "nt", 1024, 1024, 1024, [F32])[0]
    grest = matmul("mm_gwo", ao, dx2b, "tn", 1024, 1024, 1024, [F32],
                   **g_into(grest, (4, 256, D), lambda i, j: (0, 13, 0)))
    dqm, dkm, dvm = rowwise(attn_bwd_fn, "attn_bwd", T, tb, [R(qm), R(dao)], [km, vm], [(D, BF16)],
                            [(M, D), (M, D)])
    dx1, dx1b, g_n2 = matmul("mm_dh2", dqm, wq_f, "nt", 512, 1024, 1024, [F32, BF16], _epi_rms_bwd, [x1, dx2],
                             [row(norm2_w)], n_acc=1)
    grest = matmul("mm_gwq", h2, dqm, "tn", 1024, 1024, 1024, [F32],
                   **g_into(grest, (4, 256, D), lambda i, j: (0, 10, 0)))
    grest = matmul("mm_gwk", mn, dkm, "tn", 1024, 1024, 256, [F32],
                   **g_into(grest, (4, 256, D), lambda i, j: (0, 11, 0)))
    grest = matmul("mm_gwv", mn, dvm, "tn", 1024, 1024, 256, [F32],
                   **g_into(grest, (4, 256, D), lambda i, j: (0, 12, 0)))
    dmn_k = matmul("mm_dmk", dkm, wk_f, "nt", 256, 1024, 1024, [F32])[0]
    dmn = matmul("mm_dmv", dvm, wv_f, "nt", 256, 1024, 1024, [F32], _epi_res, [dmn_k])[0]
    g_nmem = rowwise(rms_bwd_w_fn, "rmsmem_bwd", M, M, [R(mem0), R(dmn)], [row(mem_norm_w)], [], [(1, D)])[0]
    doa = matmul("mm_doa", dx1b, wout_f[:D], "nt", 1024, 1024, 1024, [F32])[0]
    dob = matmul("mm_dob", dx1b, wout_f[D:], "nt", 1024, 1024, 1024, [F32])[0]
    grest = matmul("mm_gwout_a", oa, dx1b, "tn", 1024, 1024, 1024, [F32],
                   **g_into(grest, (2, 512, D), lambda i, j: (0, 4, 0)))
    grest = matmul("mm_gwout_b", ob, dx1b, "tn", 1024, 1024, 1024, [F32],
                   **g_into(grest, (2, 512, D), lambda i, j: (1, 4, 0)))

    rs_rest = rs_begin("rs_rest", grest, 256, sp, True)

    dp = jax.ShapeDtypeStruct((T, p.shape[1]), BF16)
    dy_ssd, dxs_dir, dp, g_snw, g_sd_lane = rowwise(
        ssd_post_bwd_fn, "ssd_post_bwd", T, tb, [R(y_ssd), R(xs), R(p, D, CB_ZS), R(dob)],
        [sd_x + rs_rest["token"][0:1, 0:1], snw],
        [(D, F32), (D, F32), (D, BF16, dp, CB_ZS)], [(1, D), (1, D)])
    dxs_scan, db_s, dc_s, dgate, dacs_t = ssd_bwd(xs, bc, dt_x, acs_x, acs_t, dy_ssd, h_states)
    spb = rowwise(ssd_prep_bwd_fn, "ssd_prep_bwd", T, tb,
                  sp_ins + [R(dxs_scan), R(dxs_dir), R(db_s), R(dc_s), R(dgate), RC(dacs_t.reshape(SSM_H, T))], sp_full,
                  [(D, F32), (512, F32), (128, BF16, dp, CB_DT)],
                  [(1, D)] * 4 + [(1, 512)] * 4 + [(1, D), (1, 512), (1, 128), (1, 128)])
    dyc_x, dyc_bc, dp = spb[:3]
    dp = rowwise(conv_bwd_fn, "conv_bwd_x", T, tb, [R(dyc_x, halo="next")], [scw_x], [(D, BF16, dp, CB_XS)])[0]
    dp = rowwise(conv_bwd_fn, "conv_bwd_bc", T, tb, [R(dyc_bc, halo="next")], [scw_bc], [(512, BF16, dp, CB_BC)])[0]

    do_gdn, dp, g_gnw = rowwise(gdn_post_bwd_fn, "gdn_post_bwd", T, tb, [R(o_gdn), R(p, D, CB_Z), R(doa)], [gnw],
                                [(D, F32), (D, BF16, dp, CB_Z)], [(1, 128)])
    dqn, dkn, dvv, dgcs_x, dbeta_x, dgcs_t = gdn_bwd(qn, kn, vv, gcs_x, beta_x, gcs_t, do_gdn, s_states, tinv, gtb, ggh)
    gpb = rowwise(gdn_prep_bwd_fn, "gdn_prep_bwd", T, tb,
                  gp_ins + [R(dqn), R(dkn), R(dvv), R(dgcs_x), R(dbeta_x), RC(dgcs_t.reshape(GDN_H, T))],
                  [gcw, galog_c, gdtb_c, pb, pa],
                  [(3 * D, F32), (128, BF16, dp, CB_BA)], [(1, 3 * D)] * 4 + [(1, 128), (1, 128)])
    dyc_qkv, dp = gpb[:2]
    dp = rowwise(conv_bwd_fn, "conv_bwd_qkv", T, tb, [R(dyc_qkv, halo="next")], [gcw], [(3 * D, BF16, dp, CB_QKV)])[0]
    grad_x, g_n1 = matmul("mm_dh1", dp, w_in_p, "nt", 512, 1024, 768, [F32], _epi_rms_bwd1, [x0, dx1],
                          [row(norm1_w)], n_acc=1)
    g_win_p = matmul("mm_gwin", h1, dp, "tn", 1024, 768, 1024, [F32])[0]

    items = [[g_n1], [gpb[6]], [gpb[7]], [g_gnw], [spb[11]], [spb[12]], [spb[13]], [spb[14]], [g_sd_lane], [g_snw],
             [g_n2], [g_nmem], [g_n3], [g_final], list(gpb[2:6]), list(spb[3:7]), list(spb[7:11])]
    (gr_n1, r_galog, r_gdtb, gr_gnw, r_scb_x, r_scb_bc, r_salog, r_sdtb, r_sd, gr_snw, gr_n2, gr_nmem, gr_n3,
     gr_final, r_gcw, r_scw_x, r_scw_bc) = all_reduce_items("ar_grads", items)
    gr_galog, gr_gdtb = r_galog[:, 8:16], r_gdtb[:, 8:16]
    gr_salog, gr_sdtb = r_salog[:, :SSM_H], r_sdtb[:, :SSM_H]
    gr_sd = r_sd.reshape(SSM_H, SSM_P).sum(axis=1).reshape(1, SSM_H)
    gr_scb = jnp.concatenate([r_scb_x, r_scb_bc], axis=1)
    gr_gcw = lax.dynamic_slice(r_gcw, (0, s_me * 768), (4, 768))
    gr_scw = lax.dynamic_slice(jnp.concatenate([r_scw_x, r_scw_bc], axis=1), (0, s_me * 384), (4, 384))

    g_win = _unpad_win(g_win_p).reshape(D, 4, IN_COLS // 4).transpose(1, 0, 2)
    rs_win = rs_begin("rs_win", g_win, 256, sp, True, gr_n1)
    red_r, oth_r = rs_end(rs_rest, rs_win["token"])

    big = {}
    for n, w, m, v, blk0 in (("w_up", w_up, m_w_up, v_w_up, 0), ("w_down", w_down, m_w_down, v_w_down, 4),
                             ("w_out", w_out, m_w_out, v_w_out, 8), ("wq_mem", wq_mem, m_wq_mem, v_wq_mem, 10),
                             ("wk_mem", wk_mem, m_wk_mem, v_wk_mem, 11), ("wv_mem", wv_mem, m_wv_mem, v_wv_mem, 12),
                             ("wo_mem", wo_mem, m_wo_mem, v_wo_mem, 13)):
        big[n] = adam_halves("adam_" + n, w, m, v, red_r, oth_r, 256, blk0, sp)
    red_w, oth_w = rs_end(rs_win, big["wo_mem"][1])
    big["w_in"] = adam_halves("adam_win", w_in, m_w_in, v_w_in, red_w, oth_w, 256, 0, sp)
    names_s =["norm1_w", "gdn_conv_w", "gdn_a_log", "gdn_dt_bias", "gdn_norm_w", "ssm_conv_w", "ssm_conv_b",
               "ssm_a_log", "ssm_dt_bias", "ssm_d", "ssm_norm_w", "norm2_w", "mem_norm_w", "norm3_w", "final_norm_w"]
    w_s = [norm1_w, gdn_conv_w, gdn_a_log, gdn_dt_bias, gdn_norm_w, ssm_conv_w, ssm_conv_b, ssm_a_log, ssm_dt_bias,
           ssm_d, ssm_norm_w, norm2_w, mem_norm_w, norm3_w, final_norm_w]
    g_s = [gr_n1, gr_gcw, gr_galog, gr_gdtb, gr_gnw, gr_scw, gr_scb, gr_salog, gr_sdtb, gr_sd, gr_snw, gr_n2,
           gr_nmem, gr_n3, gr_final]
    m_s = [m_norm1_w, m_gdn_conv_w, m_gdn_a_log, m_gdn_dt_bias, m_gdn_norm_w, m_ssm_conv_w, m_ssm_conv_b, m_ssm_a_log,
           m_ssm_dt_bias, m_ssm_d, m_ssm_norm_w, m_norm2_w, m_mem_norm_w, m_norm3_w, m_final_norm_w]
    v_s = [v_norm1_w, v_gdn_conv_w, v_gdn_a_log, v_gdn_dt_bias, v_gdn_norm_w, v_ssm_conv_w, v_ssm_conv_b, v_ssm_a_log,
           v_ssm_dt_bias, v_ssm_d, v_ssm_norm_w, v_norm2_w, v_mem_norm_w, v_norm3_w, v_final_norm_w]
    shp_s = [w.shape for w in w_s]
    as2d = lambda a: a if a.ndim == 2 else a.reshape(1, -1)
    d_l, m_l, v_l = adam_small([as2d(a) for a in w_s], [as2d(a) for a in g_s], [as2d(a) for a in m_s],
                               [as2d(a) for a in v_s])

    grads, deltas, new_m, new_v = {}, {}, {}, {}
    for n, (gg, dd, mm_, vv_) in big.items():
        grads[n], deltas[n], new_m[n], new_v[n] = gg, dd, mm_, vv_
    for k, n in enumerate(names_s):
        grads[n] = g_s[k].reshape(shp_s[k])
        deltas[n], new_m[n], new_v[n] = (a[k].reshape(shp_s[k]) for a in (d_l, m_l, v_l))
    order = ["norm1_w", "w_in", "gdn_conv_w", "gdn_a_log", "gdn_dt_bias", "gdn_norm_w", "ssm_conv_w", "ssm_conv_b",
             "ssm_a_log", "ssm_dt_bias", "ssm_d", "ssm_norm_w", "w_out", "norm2_w", "mem_norm_w", "wq_mem", "wk_mem",
             "wv_mem", "wo_mem", "norm3_w", "w_up", "w_down", "final_norm_w"]
    return (loss, grad_x[None], *[grads[n] for n in order], *[deltas[n] for n in order],
            *[new_m[n] for n in order], *[new_v[n] for n in order])
```

```python
import numpy as np
import jax
import jax.numpy as jnp
from jax import lax
from jax.experimental import pallas as pl
from jax.experimental.pallas import tpu as pltpu

F32, BF16 = jnp.float32, jnp.bfloat16
MESH = pl.DeviceIdType.MESH
ANY = pl.BlockSpec(memory_space=pl.ANY)

EPS = 1e-6
D = 1024
GDN_H, GDN_DK, GDN_C = 8, 128, 64
SSM_H, SSM_P, SSM_N, SSM_L = 16, 64, 128, 128
MEM_H, MEM_DH = 4, 256
D_FF = 4096
IN_COLS = 6688
CB_QKV, CB_Z, CB_ZS, CB_XS, CB_BC, CB_BA, CB_DT = 0, 3, 4, 5, 12, 52, 53
VMEM_LIMIT = 56 * 1024 * 1024
D2D_CHUNKS = 8
ICI_CHUNKS = 4

ADAM_LR, ADAM_B1, ADAM_B2, ADAM_EPS, ADAM_WD, ADAM_STEP = 0.001, 0.9, 0.999, 1e-08, 0.01, 10


def _dg(a, b, ca, cb):
    return lax.dot_general(a, b, (((ca,), (cb,)), ((), ())), preferred_element_type=F32)


def _bf(x):
    return x.astype(BF16)


def mm(a, b):
    return _dg(_bf(a), _bf(b), 1, 0)


def mm_nt(a, b):
    return _dg(_bf(a), _bf(b), 1, 1)


def mm_tn(a, b):
    return _dg(_bf(a), _bf(b), 0, 0)


def mm_sel(a, sel):
    hi = a.astype(BF16)
    r1 = a - hi.astype(F32)
    mid = r1.astype(BF16)
    lo = (r1 - mid.astype(F32)).astype(BF16)
    s = sel.astype(BF16)
    return _dg(hi, s, 1, 0) + (_dg(mid, s, 1, 0) + _dg(lo, s, 1, 0))


def mm3(a, b):
    ah, bh = a.astype(BF16), b.astype(BF16)
    al, bl = (a - ah.astype(F32)).astype(BF16), (b - bh.astype(F32)).astype(BF16)
    return _dg(ah, bh, 1, 0) + (_dg(ah, bl, 1, 0) + _dg(al, bh, 1, 0))


def _iota(shape, dim):
    return lax.broadcasted_iota(jnp.int32, shape, dim)


def _chunk_cumsum(x, c):
    pos = _iota(x.shape, 0) & (c - 1)
    s = 1
    while s < c:
        x = x + jnp.where(pos >= s, pltpu.roll(x, s, 0), 0.0)
        s *= 2
    return x


def _chunk_revcumsum(x, c):
    n = x.shape[0]
    pos = _iota(x.shape, 0) & (c - 1)
    s = 1
    while s < c:
        x = x + jnp.where(pos < c - s, pltpu.roll(x, n - s, 0), 0.0)
        s *= 2
    return x


def _sig(x):
    return 1.0 / (1.0 + jnp.exp(-x))


def _softplus(x):
    return jnp.maximum(x, 0.0) + jnp.log(1.0 + jnp.exp(-jnp.abs(x)))


def _rows(v):
    return jnp.sum(v, axis=0, keepdims=True)


def _lanes(v):
    return jnp.sum(v, axis=1, keepdims=True)


def _sum_all(v):
    return _rows(_lanes(v))


def _cparams(sem):
    return pltpu.CompilerParams(dimension_semantics=sem, vmem_limit_bytes=VMEM_LIMIT)


def rowwise(fn, name, T, tb, row_ins, full_ins, row_outs, acc_outs=(), sp=None):
    nblk = T // tb
    assert nblk * tb == T
    has_sp = sp is not None

    def imap(f):
        return (lambda i, s: f(i, s)) if has_sp else (lambda i: f(i, None))

    in_specs, args = [], []
    for arr, w, cb, halo, off in row_ins:
        if halo == "col":
            in_specs.append(pl.BlockSpec((w, tb), imap(lambda i, s: (0, i))))
            args.append(arr)
            continue
        rowf = off if callable(off) else (lambda i, s, off=off: i + off)
        in_specs.append(pl.BlockSpec((tb, w), imap(lambda i, s, cb=cb, rowf=rowf: (rowf(i, s), cb))))
        args.append(arr)
        if halo == "prev":
            r = tb // 8
            in_specs.append(pl.BlockSpec((8, w), imap(lambda i, s, cb=cb, r=r: (jnp.maximum(i * r - 1, 0), cb))))
            args.append(arr)
        elif halo == "next":
            r, last = tb // 8, T // 8 - 1
            in_specs.append(pl.BlockSpec((8, w), imap(lambda i, s, cb=cb, r=r, last=last:
                                                      (jnp.minimum((i + 1) * r, last), cb))))
            args.append(arr)
    for arr in full_ins:
        in_specs.append(pl.BlockSpec(arr.shape, imap(lambda i, s, nd=arr.ndim: (0,) * nd)))
        args.append(arr)
    n_in, n_ro = len(args), len(row_outs)
    out_shape, out_specs, aliases = [], [], {}
    for k, (w, dt, *dest) in enumerate(row_outs):
        if dest:
            buf, cb = dest
            out_shape.append(jax.ShapeDtypeStruct(buf.shape, buf.dtype))
            out_specs.append(pl.BlockSpec((tb, w), imap(lambda i, s, cb=cb: (i, cb))))
            if not isinstance(buf, jax.ShapeDtypeStruct):
                aliases[len(args) + int(has_sp)] = k
                in_specs.append(ANY)
                args.append(buf)
        elif w < 0:
            out_shape.append(jax.ShapeDtypeStruct((-w, T), dt))
            out_specs.append(pl.BlockSpec((-w, tb), imap(lambda i, s: (0, i))))
        else:
            out_shape.append(jax.ShapeDtypeStruct((T, w), dt))
            out_specs.append(pl.BlockSpec((tb, w), imap(lambda i, s: (i, 0))))
    for shp in acc_outs:
        out_shape.append(jax.ShapeDtypeStruct(shp, F32))
        out_specs.append(pl.BlockSpec(shp, imap(lambda i, s, nd=len(shp): (0,) * nd)))

    def body(*refs):
        i = pl.program_id(0)
        if has_sp:
            sp_ref, refs = refs[0], refs[1:]
            vals = fn(i, nblk, sp_ref, *[r[...] for r in refs[:n_in]])
        else:
            vals = fn(i, nblk, *[r[...] for r in refs[:n_in]])
        outs = refs[n_in + len(aliases):]
        for ref, val in zip(outs[:n_ro], vals[:n_ro]):
            ref[...] = val.astype(ref.dtype)
        for ref, val in zip(outs[n_ro:], vals[n_ro:]):
            @pl.when(i == 0)
            def _(ref=ref, val=val):
                ref[...] = val

            @pl.when(i > 0)
            def _(ref=ref, val=val):
                ref[...] += val

    cparams = _cparams(("arbitrary",) if acc_outs else ("parallel",))
    if has_sp:
        return pl.pallas_call(
            body, name=name, out_shape=out_shape, compiler_params=cparams, input_output_aliases=aliases,
            grid_spec=pltpu.PrefetchScalarGridSpec(num_scalar_prefetch=1, grid=(nblk,), in_specs=in_specs,
                                                   out_specs=out_specs),
        )(sp, *args)
    return pl.pallas_call(
        body, name=name, grid=(nblk,), in_specs=in_specs, out_specs=out_specs, out_shape=out_shape,
        compiler_params=cparams, input_output_aliases=aliases,
    )(*args)


def R(arr, w=None, cb=0, halo=None, off=0):
    return (arr, arr.shape[1] if w is None else w, cb, halo, off)


def RC(arr):
    return (arr, arr.shape[0], 0, "col", 0)


def matmul(name, a, b, form, tm, tn, tk, out_dtypes, epi=None, extras=(), rows=(), into=None, n_acc=0, b_sel=None):
    bs = b.shape if b_sel is None else b.shape[1:]
    if form == "nn":
        (M, K), N = a.shape, bs[1]
    elif form == "nt":
        (M, K), N = a.shape, bs[0]
    else:
        (K, M), N = a.shape, bs[1]
    tm, tn, tk = min(tm, M), min(tn, N), min(tk, K)
    assert M % tm == 0 and N % tn == 0 and K % tk == 0, (name, M, N, K, tm, tn, tk)

    def b_spec_of(blk, at):
        if b_sel is None:
            return pl.BlockSpec(blk, lambda i, j, k: at(i, j, k))
        return pl.BlockSpec((None,) + blk, lambda i, j, k: (b_sel,) + at(i, j, k))

    if form == "nn":
        a_spec = pl.BlockSpec((tm, tk), lambda i, j, k: (i, k))
        b_spec = b_spec_of((tk, tn), lambda i, j, k: (k, j))
        ca, cb = 1, 0
    elif form == "nt":
        a_spec = pl.BlockSpec((tm, tk), lambda i, j, k: (i, k))
        b_spec = b_spec_of((tn, tk), lambda i, j, k: (j, k))
        ca, cb = 1, 1
    else:
        a_spec = pl.BlockSpec((tk, tm), lambda i, j, k: (k, i))
        b_spec = b_spec_of((tk, tn), lambda i, j, k: (k, j))
        ca, cb = 0, 0
    nk, ne, no = K // tk, len(extras) + len(rows), len(out_dtypes)
    if epi is None:
        epi = lambda acc: (acc,)

    assert n_acc == 0 or tn == N

    def body(a_ref, b_ref, *rest):
        ex, outs, accs, acc = rest[:ne], rest[ne:ne + no], rest[ne + no:ne + no + n_acc], rest[ne + no + n_acc]
        i, k = pl.program_id(0), pl.program_id(2)

        def finish(total):
            vals = epi(total, *[e[...] for e in ex])
            for r, v in zip(outs, vals[:no]):
                r[...] = v.astype(r.dtype).reshape(r.shape)
            for r, v in zip(accs, vals[no:]):
                @pl.when(i == 0)
                def _(r=r, v=v):
                    r[...] = v

                @pl.when(i > 0)
                def _(r=r, v=v):
                    r[...] += v

        prod = _dg(_bf(a_ref[...]), _bf(b_ref[...]), ca, cb)
        if nk == 1:
            finish(prod)
            return

        @pl.when(k == 0)
        def _():
            acc[...] = prod

        @pl.when(k > 0)
        def _():
            acc[...] += prod

        @pl.when(k == nk - 1)
        def _():
            finish(acc[...])

    mn = pl.BlockSpec((tm, tn), lambda i, j, k: (i, j))
    rw = pl.BlockSpec((1, tn), lambda i, j, k: (0, j))
    acc_scratch = pltpu.VMEM((tm, tn) if nk > 1 else (8, 128), F32)
    if into is not None:
        buf, blk, bmap = into
        assert ne == 0 and no == 1
        aliased = not isinstance(buf, jax.ShapeDtypeStruct)

        def body_into(a_ref, b_ref, *rest):
            body(a_ref, b_ref, *rest[-2:])

        return pl.pallas_call(
            body_into, name=name, grid=(M // tm, N // tn, nk),
            in_specs=[a_spec, b_spec] + ([ANY] if aliased else []), out_specs=pl.BlockSpec(blk, bmap),
            out_shape=jax.ShapeDtypeStruct(buf.shape, buf.dtype),
            scratch_shapes=[acc_scratch],
            input_output_aliases={2: 0} if aliased else {},
            compiler_params=_cparams(("parallel", "parallel", "arbitrary")),
        )(a, b, *([buf] if aliased else []))
    return pl.pallas_call(
        body, name=name, grid=(M // tm, N // tn, nk),
        in_specs=[a_spec, b_spec] + [mn] * len(extras) + [rw] * len(rows), out_specs=[mn] * no + [rw] * n_acc,
        out_shape=[jax.ShapeDtypeStruct((M, N), dt) for dt in out_dtypes] + [jax.ShapeDtypeStruct((1, N), F32)] * n_acc,
        scratch_shapes=[acc_scratch],
        compiler_params=_cparams(("arbitrary",) * 3 if n_acc else ("parallel", "parallel", "arbitrary")),
    )(a, b, *extras, *rows)


def _epi_res(acc, res):
    return (res + acc,)


def _epi_rms_bwd(acc, x, dres, w):
    return rms_bwd_fn(0, 0, x, acc, dres, w)


def rms_bwd1_fn(i, n, x, dh, dres, w):
    dx, _, gw = rms_bwd_fn(i, n, x, dh, dres, w)
    return dx, gw


def _epi_final(acc, res, tgt, w):
    return final_fn(0, 0, res + acc, tgt, w)


def _epi_res_rms(acc, res, w):
    x = res + acc
    return (x, x * lax.rsqrt(jnp.mean(x * x, axis=-1, keepdims=True) + EPS) * w)


def _epi_relu2(acc):
    u = jnp.maximum(acc, 0.0)
    return (u, u * u)


def _epi_dup(acc, u):
    return (acc * 2.0 * u.astype(F32),)


def _conv(x, halo, w, i):
    halo = jnp.where(i == 0, 0.0, halo)
    xt = jnp.concatenate([halo, x], axis=0)
    shifted = [pltpu.roll(xt, 3 - k, 0)[8:, :] for k in range(3)] + [x]
    y = shifted[3] * w[3:4, :]
    for k in range(3):
        y = y + shifted[k] * w[k:k + 1, :]
    return y, shifted


def _l2n(x, scale):
    outs = []
    for h in range(x.shape[1] // 128):
        xh = x[:, 128 * h:128 * h + 128]
        outs.append(xh * (lax.rsqrt(jnp.sum(xh * xh, axis=-1, keepdims=True) + EPS) * scale))
    return jnp.concatenate(outs, axis=1)


def _l2n_bwd(x, dy, scale):
    outs = []
    for h in range(x.shape[1] // 128):
        xh, dh = x[:, 128 * h:128 * h + 128], dy[:, 128 * h:128 * h + 128] * scale
        r = lax.rsqrt(jnp.sum(xh * xh, axis=-1, keepdims=True) + EPS)
        outs.append(r * dh - xh * (r * r * r) * jnp.sum(xh * dh, axis=-1, keepdims=True))
    return jnp.concatenate(outs, axis=1)


def rms_fwd_fn(i, n, x, w):
    r = lax.rsqrt(jnp.mean(x * x, axis=-1, keepdims=True) + EPS)
    return (x * r * w,)


def rms_bwd_fn(i, n, x, dh, dres, w):
    r = lax.rsqrt(jnp.mean(x * x, axis=-1, keepdims=True) + EPS)
    g = dh * w
    dx = dres + r * g - x * (r * r * r) * jnp.mean(x * g, axis=-1, keepdims=True)
    return dx, dx, _rows(dh * x * r)


def rms_bwd_w_fn(i, n, x, dh, w):
    r = lax.rsqrt(jnp.mean(x * x, axis=-1, keepdims=True) + EPS)
    return (_rows(dh * x * r),)


def final_fn(i, n, x, tgt, w):
    r = lax.rsqrt(jnp.mean(x * x, axis=-1, keepdims=True) + EPS)
    xn = x * r
    e = xn * w - tgt
    dy = e * (1.0 / D)
    g = dy * w
    dx = r * g - x * (r * r * r) * jnp.mean(x * g, axis=-1, keepdims=True)
    return dx, dx, _rows(e * e), _rows(dy * xn)


def _gdn_gates(ba, alog_c, dtb_c):
    col = _iota(ba.shape, 1)
    amask = (col >= 8) & (col < 16)
    beta = jnp.where(col < 8, _sig(ba), 0.0)
    z = ba + dtb_c
    ea_ = jnp.exp(alog_c)
    return beta, z, ea_, jnp.where(amask, -ea_ * _softplus(z), 0.0), amask


def gdn_prep_fn(i, n, qkv, halo, ba, cw, alog_c, dtb_c, eb, ea):
    yc, _ = _conv(qkv, halo, cw, i)
    act = yc * _sig(yc)
    qn = _l2n(act[:, :D], GDN_DK ** -0.5)
    kn = _l2n(act[:, D:2 * D], 1.0)
    beta, _, _, g, _ = _gdn_gates(ba, alog_c, dtb_c)
    gcs = _chunk_cumsum(g, GDN_C)
    return qn, kn, act[:, 2 * D:], mm_sel(gcs, ea), mm_sel(beta, eb), jnp.transpose(gcs)[8:16, :]


def gdn_prep_bwd_fn(i, n, qkv, halo, ba, dqn, dkn, dv, dgcs_x, dbeta_x, dgcs_t, cw, alog_c, dtb_c, pb, pa):
    yc, shifted = _conv(qkv, halo, cw, i)
    sg = _sig(yc)
    act = yc * sg
    dq = _l2n_bwd(act[:, :D], dqn, GDN_DK ** -0.5)
    dk = _l2n_bwd(act[:, D:2 * D], dkn, 1.0)
    dyc = jnp.concatenate([dq, dk, dv], axis=1) * (sg * (1.0 + yc * (1.0 - sg)))
    dws = [_rows(dyc * shifted[k]) for k in range(4)]
    beta, z, ea_, g, amask = _gdn_gates(ba, alog_c, dtb_c)
    tbn = ba.shape[0]
    rowpart = jnp.transpose(jnp.concatenate([jnp.zeros((8, tbn), F32), dgcs_t, jnp.zeros((112, tbn), F32)], axis=0))
    dg = _chunk_revcumsum(mm_sel(dgcs_x, pa) - rowpart, GDN_C)
    draw = jnp.where(amask, dg * (-ea_) * _sig(z), 0.0)
    dba = draw + mm_sel(dbeta_x, pb) * beta * (1.0 - beta)
    return (dyc, dba, dws[0], dws[1], dws[2], dws[3], _rows(dg * g), _rows(draw))


def conv_bwd_fn(i, n, dyc, halo, w):
    halo = jnp.where(i == n - 1, 0.0, halo)
    tb = dyc.shape[0]
    xt = jnp.concatenate([dyc, halo], axis=0)
    dx = dyc * w[3:4, :]
    for k in range(3):
        dx = dx + pltpu.roll(xt, tb + 8 - (3 - k), 0)[:tb, :] * w[k:k + 1, :]
    return (dx,)


def gdn_post_fn(i, n, o, z, w):
    outs = []
    for h in range(GDN_H):
        oh, zh = o[:, 128 * h:128 * h + 128], z[:, 128 * h:128 * h + 128]
        r = lax.rsqrt(jnp.mean(oh * oh, axis=-1, keepdims=True) + EPS)
        outs.append(oh * r * w * (zh * _sig(zh)))
    return (jnp.concatenate(outs, axis=1),)


def gdn_post_bwd_fn(i, n, o, z, doa, w):
    dos, dzs, dw = [], [], None
    for h in range(GDN_H):
        sl = slice(128 * h, 128 * h + 128)
        oh, zh, dh = o[:, sl], z[:, sl], doa[:, sl]
        r = lax.rsqrt(jnp.mean(oh * oh, axis=-1, keepdims=True) + EPS)
        s = _sig(zh)
        dn = dh * (zh * s)
        dzs.append(dh * (oh * r * w) * (s * (1.0 + zh * (1.0 - s))))
        t = _rows(dn * oh * r)
        dw = t if dw is None else dw + t
        g = dn * w
        dos.append(r * g - oh * (r * r * r) * jnp.mean(oh * g, axis=-1, keepdims=True))
    return jnp.concatenate(dos, axis=1), jnp.concatenate(dzs, axis=1), dw


def _ssd_gates(dtblk, alog_c, dtb_c):
    hmask = _iota(dtblk.shape, 1) < SSM_H
    z = dtblk + dtb_c
    return jnp.where(hmask, _softplus(z), 0.0), -jnp.exp(alog_c), z, hmask


def ssd_prep_fn(i, n, xp, hx, bcp, hbc, dtblk, cwx, cwbc, cbx, cbbc, alog_c, dtb_c, e16):
    yx, _ = _conv(xp, hx, cwx, i)
    yx = yx + cbx
    ybc, _ = _conv(bcp, hbc, cwbc, i)
    ybc = ybc + cbbc
    dt, a_neg, _, _ = _ssd_gates(dtblk, alog_c, dtb_c)
    acs = _chunk_cumsum(dt * a_neg, SSM_L)
    return (yx * _sig(yx), ybc * _sig(ybc), mm_sel(dt, e16), mm_sel(acs, e16), jnp.transpose(acs)[0:SSM_H, :])


def ssd_prep_bwd_fn(i, n, xp, hx, bcp, hbc, dtblk, dxs_a, dxs_b, db, dc, dgate, dacs_t, cwx, cwbc, cbx, cbbc, alog_c, dtb_c):
    dbc = jnp.concatenate([db, dc], axis=1)
    yx, shx = _conv(xp, hx, cwx, i)
    yx = yx + cbx
    ybc, shbc = _conv(bcp, hbc, cwbc, i)
    ybc = ybc + cbbc
    sx, sbc = _sig(yx), _sig(ybc)
    dyx = (dxs_a + dxs_b) * (sx * (1.0 + yx * (1.0 - sx)))
    dybc = dbc * (sbc * (1.0 + ybc * (1.0 - sbc)))
    dwx = [_rows(dyx * shx[k]) for k in range(4)]
    dwbc = [_rows(dybc * shbc[k]) for k in range(4)]
    dt, a_neg, z, hmask = _ssd_gates(dtblk, alog_c, dtb_c)
    g0, g1 = dgate[:, :128], dgate[:, 128:]
    col = _iota(g0.shape, 1)
    lo, mid = col < 8, (col >= 8) & (col < 16)
    dacs_col = jnp.where(lo, g0, 0.0) + pltpu.roll(jnp.where(lo, g1, 0.0), 8, 1)
    ddt_dir = pltpu.roll(jnp.where(mid, g0, 0.0), 120, 1) + jnp.where(mid, g1, 0.0)
    tbn = dtblk.shape[0]
    rowpart = jnp.transpose(jnp.concatenate([dacs_t, jnp.zeros((128 - SSM_H, tbn), F32)], axis=0))
    da = _chunk_revcumsum(dacs_col - rowpart, SSM_L)
    draw = jnp.where(hmask, (ddt_dir + da * a_neg) * _sig(z), 0.0)
    return (dyx, dybc, draw, *dwx, *dwbc, _rows(dyx), _rows(dybc), _rows(da * dt * a_neg), _rows(draw))


def _ssd_gate(y, xs, zs, d_x):
    y2 = y + xs * d_x
    s = _sig(zs)
    return y2, s, y2 * (zs * s)


def ssd_post_fn(i, n, y, xs, zs, d_x, nw):
    _, _, yg = _ssd_gate(y, xs, zs, d_x)
    outs = []
    for g in range(2):
        v = yg[:, 512 * g:512 * g + 512]
        outs.append(v * lax.rsqrt(jnp.mean(v * v, axis=-1, keepdims=True) + EPS))
    return (jnp.concatenate(outs, axis=1) * nw,)


def ssd_post_bwd_fn(i, n, y, xs, zs, dob, d_x, nw):
    y2, s, yg = _ssd_gate(y, xs, zs, d_x)
    gfull = dob * nw
    dygs, dnw = [], []
    for g in range(2):
        sl = slice(512 * g, 512 * g + 512)
        v, gg = yg[:, sl], gfull[:, sl]
        r = lax.rsqrt(jnp.mean(v * v, axis=-1, keepdims=True) + EPS)
        dygs.append(r * gg - v * (r * r * r) * jnp.mean(v * gg, axis=-1, keepdims=True))
        dnw.append(_rows(dob[:, sl] * v * r))
    dyg = jnp.concatenate(dygs, axis=1)
    dy2 = dyg * (zs * s)
    dzs = dyg * y2 * (s * (1.0 + zs * (1.0 - s)))
    return dy2, dy2 * d_x, dzs, jnp.concatenate(dnw, axis=1), _rows(dy2 * xs)


def _attn_probs(q, k):
    hs = [slice(MEM_DH * h, MEM_DH * h + MEM_DH) for h in range(MEM_H)]
    ss = [mm_nt(q[:, sl], k[:, sl]) * (MEM_DH ** -0.5) for sl in hs]
    es = [jnp.exp(s - jnp.max(s, axis=-1, keepdims=True)) for s in ss]
    return hs, [e / jnp.sum(e, axis=-1, keepdims=True) for e in es]


def attn_fn(i, n, q, k, v):
    hs, ps = _attn_probs(q, k)
    return (jnp.concatenate([mm(p, v[:, sl]) for p, sl in zip(ps, hs)], axis=1),)


def attn_bwd_fn(i, n, q, do, k, v):
    hs, ps = _attn_probs(q, k)
    dvs = [mm_tn(p, do[:, sl]) for p, sl in zip(ps, hs)]
    dps = [mm_nt(do[:, sl], v[:, sl]) for sl in hs]
    dss = [p * (dp - jnp.sum(dp * p, axis=-1, keepdims=True)) * (MEM_DH ** -0.5) for p, dp in zip(ps, dps)]
    dqs = [mm(ds, k[:, sl]) for ds, sl in zip(dss, hs)]
    dks = [mm_tn(ds, q[:, sl]) for ds, sl in zip(dss, hs)]
    return jnp.concatenate(dqs, axis=1), jnp.concatenate(dks, axis=1), jnp.concatenate(dvs, axis=1)


def add2_fn(i, n, sp, a, b):
    return (a + b,)


def sum4_fn(i, n, sp, a, b, c, d):
    return (((a.astype(F32) + b.astype(F32)) + c.astype(F32)) + d.astype(F32),)


def _adamw(w, g, m, v):
    m = ADAM_B1 * m + (1.0 - ADAM_B1) * g
    v = ADAM_B2 * v + (1.0 - ADAM_B2) * (g * g)
    m_hat = m / (1.0 - ADAM_B1 ** ADAM_STEP)
    v_hat = v / (1.0 - ADAM_B2 ** ADAM_STEP)
    delta = -ADAM_LR * (m_hat / (jnp.sqrt(v_hat) + ADAM_EPS) + ADAM_WD * w)
    return delta, m, v


def _gdn_stage1(q, k, v, gcs, grow, bb):
    C = GDN_C
    row, col = _iota((C, C), 0), _iota((C, C), 1)
    incl, strict = row >= col, row > col
    dmat = jnp.where(incl, jnp.exp(jnp.minimum(gcs[:, :C] - grow, 0.0)), 0.0)
    gam = jnp.exp(gcs)
    gl = gcs[C - 1:C, :]
    kb, vb = k * bb, v * bb
    kg = kb * gam
    lmat = jnp.where(strict, mm_nt(kb, k) * dmat, 0.0)
    pmat = jnp.where(incl, mm_nt(q, k) * dmat, 0.0)
    return dict(q=q, k=k, v=v, bb=bb, incl=incl, strict=strict, dmat=dmat, gam=gam, kb=kb, vb=vb, kg=kg,
                lmat=lmat, pmat=pmat, qd=q * gam, kdec=jnp.exp(gl - gcs), cd=jnp.exp(gl))


def _gdn_inverse(lmats):
    C = GDN_C
    eye = (_iota((C, C), 0) == _iota((C, C), 1)).astype(F32)
    xs = [-l for l in lmats]
    ts = [eye + x for x in xs]
    for _ in range(5):
        xs = [mm(x, x) for x in xs]
        ts = [t + mm(t, x) for t, x in zip(ts, xs)]
    res = [eye - mm3(eye + l, t) for l, t in zip(lmats, ts)]
    return [t + mm(t, r) for t, r in zip(ts, res)]


def gdn_fwd(qn, kn, v, gcs_x, beta_x, gcs_t, tb, gh):
    T = qn.shape[0]
    nb, ncb, nc, C = T // tb, tb // GDN_C, T // GDN_C, GDN_C
    idx = [(hh, c) for hh in range(gh) for c in range(ncb)]

    def body(q_ref, k_ref, v_ref, g_ref, b_ref, gt_ref, o_ref, st_ref, ti_ref, s_scr):
        @pl.when(pl.program_id(1) == 0)
        def _():
            s_scr[...] = jnp.zeros_like(s_scr)

        grows = [gt_ref[hh] for hh in range(gh)]
        at = lambda hh, c: (slice(C * c, C * (c + 1)), slice(128 * hh, 128 * hh + 128))
        st1 = []
        for hh, c in idx:
            sl, ln = at(hh, c)
            st1.append(_gdn_stage1(q_ref[sl, ln], k_ref[sl, ln], v_ref[sl, ln], g_ref[sl, ln], grows[hh][:, sl],
                                   b_ref[sl, ln]))
        tinvs = _gdn_inverse([s["lmat"] for s in st1])
        us = [mm(t, s["vb"]) for t, s in zip(tinvs, st1)]
        ws = [mm(t, s["kg"]) for t, s in zip(tinvs, st1)]
        kds = [s["k"] * s["kdec"] for s in st1]
        ms = [mm_tn(kd, w) for kd, w in zip(kds, ws)]
        bs = [mm_tn(kd, u) for kd, u in zip(kds, us)]
        gs = [s["qd"] - mm(s["pmat"], w) for s, w in zip(st1, ws)]
        pus = [mm(s["pmat"], u) for s, u in zip(st1, us)]
        ss = [s_scr[hh] for hh in range(gh)]
        for c in range(ncb):
            for hh in range(gh):
                n, (sl, ln) = hh * ncb + c, at(hh, c)
                ti_ref[hh, sl, :] = tinvs[n]
                st_ref[hh, c] = ss[hh]
                o_ref[sl, ln] = mm(gs[n], ss[hh]) + pus[n]
                ss[hh] = st1[n]["cd"] * ss[hh] - mm(ms[n], ss[hh]) + bs[n]
        for hh in range(gh):
            s_scr[hh] = ss[hh]

    blk = pl.BlockSpec((tb, 128 * gh), lambda h, i: (i, h))
    return pl.pallas_call(
        body, name="gdn_fwd", grid=(GDN_H // gh, nb),
        in_specs=[blk] * 5 + [pl.BlockSpec((gh, 1, tb), lambda h, i: (h, 0, i))],
        out_specs=[blk, pl.BlockSpec((gh, ncb, 128, 128), lambda h, i: (h, i, 0, 0)),
                   pl.BlockSpec((gh, tb, C), lambda h, i: (h, i, 0))],
        out_shape=[jax.ShapeDtypeStruct((T, D), F32), jax.ShapeDtypeStruct((GDN_H, nc, 128, 128), F32),
                   jax.ShapeDtypeStruct((GDN_H, T, C), F32)],
        scratch_shapes=[pltpu.VMEM((gh, 128, 128), F32)],
        compiler_params=_cparams(("parallel", "arbitrary")),
    )(qn, kn, v, gcs_x, beta_x, gcs_t)


def gdn_bwd(qn, kn, v, gcs_x, beta_x, gcs_t, do, states, tinv, tb, gh):
    T = qn.shape[0]
    nb, ncb, C = T // tb, tb // GDN_C, GDN_C

    def body(q_ref, k_ref, v_ref, g_ref, b_ref, gt_ref, do_ref, st_ref, ti_ref,
             dq_ref, dk_ref, dv_ref, dgc_ref, db_ref, dgr_ref, ds_scr):
        @pl.when(pl.program_id(1) == 0)
        def _():
            ds_scr[...] = jnp.zeros_like(ds_scr)

        grows = [gt_ref[hh] for hh in range(gh)]
        at = lambda hh, c: (slice(C * c, C * (c + 1)), slice(128 * hh, 128 * hh + 128))
        lastrow = _iota((C, 1), 0) == C - 1
        idx = [(hh, c) for hh in range(gh) for c in range(ncb)]
        P = []
        for hh, c in idx:
            sl, ln = at(hh, c)
            lc = _gdn_stage1(q_ref[sl, ln], k_ref[sl, ln], v_ref[sl, ln], g_ref[sl, ln], grows[hh][:, sl],
                             b_ref[sl, ln])
            lc.update(tinv=ti_ref[hh, sl, :], s=st_ref[hh, c], do=do_ref[sl, ln], kd=lc["k"] * lc["kdec"])
            P.append(lc)
        for l, u, w in zip(P, [mm(l["tinv"], l["vb"]) for l in P], [mm(l["tinv"], l["kg"]) for l in P]):
            l.update(u=u, w=w)
        for l, x in zip(P, [mm(l["w"], l["s"]) for l in P]):
            l["vn"] = l["u"] - x
        for l, a, b, c_, d in zip(P, [mm_nt(l["do"], l["s"]) for l in P], [mm_nt(l["do"], l["vn"]) for l in P],
                                  [mm_tn(l["qd"], l["do"]) for l in P], [mm_tn(l["pmat"], l["do"]) for l in P]):
            l.update(dqd=a, dp=jnp.where(l["incl"], b, 0.0), ds_q=c_, dvn_p=d)
        pre = dict(zip(idx, P))
        rows = {}
        hs = range(gh)
        ds = [ds_scr[hh] for hh in hs]
        for c in reversed(range(ncb)):
            L = [pre[hh, c] for hh in hs]
            dvn = [l["dvn_p"] + mm(l["kd"], d) for l, d in zip(L, ds)]
            dkd = [mm_nt(l["vn"], d) for l, d in zip(L, ds)]
            dcd = [_sum_all(l["s"] * d) for l, d in zip(L, ds)]
            ds = [l["ds_q"] + l["cd"] * d - mm_tn(l["w"], x) for l, d, x in zip(L, ds, dvn)]
            dw = [-mm_nt(x, l["s"]) for l, x in zip(L, dvn)]
            dvb = [mm_tn(l["tinv"], x) for l, x in zip(L, dvn)]
            dkg = [mm_tn(l["tinv"], x) for l, x in zip(L, dw)]
            da = [-jnp.where(l["strict"], mm_nt(a, l["u"]) + mm_nt(b, l["w"]), 0.0) for l, a, b in zip(L, dvb, dkg)]
            dm = [a * l["dmat"] for l, a in zip(L, da)]
            dn = [l["dp"] * l["dmat"] for l in L]
            dkb = [mm(a, l["k"]) for l, a in zip(L, dm)]
            dq = [mm(a, l["k"]) + l["gam"] * l["dqd"] for l, a in zip(L, dn)]
            dk = [mm_tn(a, l["kb"]) + mm_tn(b, l["q"]) for l, a, b in zip(L, dm, dn)]
            for hh in hs:
                sl, ln = at(hh, c)
                l = L[hh]
                e = da[hh] * l["lmat"] + l["dp"] * l["pmat"]
                t_kd = _lanes(dkd[hh] * l["kd"])
                dgl = _sum_all(t_kd) + dcd[hh] * l["cd"][:, :1]
                dgcs = (_lanes(e) + _lanes(l["dqd"] * l["qd"]) - t_kd + _lanes(dkg[hh] * l["kg"])
                        + jnp.where(lastrow, dgl, 0.0))
                rows[hh, c] = _rows(e)
                dq_ref[sl, ln] = dq[hh]
                dk_ref[sl, ln] = (dk[hh] + l["kdec"] * dkd[hh] + l["bb"] * l["gam"] * dkg[hh] + l["bb"] * dkb[hh])
                dv_ref[sl, ln] = l["bb"] * dvb[hh]
                dbeta = _lanes(dkg[hh] * l["gam"] * l["k"]) + _lanes(dvb[hh] * l["v"]) + _lanes(dkb[hh] * l["k"])
                db_ref[sl, ln] = jnp.broadcast_to(dbeta, (C, 128))
                dgc_ref[sl, ln] = jnp.broadcast_to(dgcs, (C, 128))
        for hh in hs:
            ds_scr[hh] = ds[hh]
            dgr_ref[hh] = jnp.concatenate([rows[hh, c] for c in range(ncb)], axis=1)

    blk = pl.BlockSpec((tb, 128 * gh), lambda h, i: (nb - 1 - i, h))
    rowspec = pl.BlockSpec((gh, 1, tb), lambda h, i: (h, 0, nb - 1 - i))
    return pl.pallas_call(
        body, name="gdn_bwd", grid=(GDN_H // gh, nb),
        in_specs=[blk] * 5 + [rowspec, blk,
                              pl.BlockSpec((gh, ncb, 128, 128), lambda h, i: (h, nb - 1 - i, 0, 0)),
                              pl.BlockSpec((gh, tb, C), lambda h, i: (h, nb - 1 - i, 0))],
        out_specs=[blk] * 5 + [rowspec],
        out_shape=[jax.ShapeDtypeStruct((T, D), F32)] * 5 + [jax.ShapeDtypeStruct((GDN_H, 1, T), F32)],
        scratch_shapes=[pltpu.VMEM((gh, 128, 128), F32)],
        compiler_params=_cparams(("parallel", "arbitrary")),
    )(qn, kn, v, gcs_x, beta_x, gcs_t, do, states, tinv)


def _ssd_pair(x2, dt2, acs2):
    last = acs2[SSM_L - 1:SSM_L, :]
    return jnp.exp(acs2), jnp.exp(last - acs2), x2 * dt2


def _ssd_head(hh, acs2, arow, dec2, cbm, bm, incl, col):
    lmask = (col >= 64 * hh) & (col < 64 * hh + 64)
    sg = jnp.where(incl, jnp.exp(jnp.minimum(acs2[:, 64 * hh:64 * hh + 1] - arow, 0.0)), 0.0)
    dec_col = dec2[:, 64 * hh:64 * hh + 1]
    return lmask, sg, sg * cbm, dec_col, bm * dec_col


def ssd_fwd(xs, bc, dt_x, acs_x, acs_t):
    T = xs.shape[0]
    nc, L = T // SSM_L, SSM_L

    def body(x_ref, b_ref, c_ref, dt_ref, ac_ref, at_ref, y_ref, hst_ref, h_scr):
        @pl.when(pl.program_id(1) == 0)
        def _():
            h_scr[...] = jnp.zeros_like(h_scr)

        bm, cm = b_ref[...], c_ref[...]
        cbm = mm_nt(cm, bm)
        row, col = _iota((L, L), 0), _iota((L, L), 1)
        incl = row >= col
        P, H = [], []
        for pr in range(4):
            sl = slice(128 * pr, 128 * pr + 128)
            acs2 = ac_ref[:, sl]
            lam2, dec2, xd2 = _ssd_pair(x_ref[:, sl], dt_ref[:, sl], acs2)
            P.append(dict(sl=sl, lam2=lam2, xd2=xd2, hprev=h_scr[pr]))
            for hh in range(2):
                lmask, _, mmat, _, bd = _ssd_head(hh, acs2, at_ref[2 * pr + hh], dec2, cbm, bm, incl, col)
                H.append(dict(mmat=mmat, bd=bd, xdh=jnp.where(lmask, xd2, 0.0), xd2=xd2))
        ys = [mm(h["mmat"], h["xdh"]) for h in H]
        sts = [mm_tn(h["xd2"], h["bd"]) for h in H]
        zs = [mm_nt(cm, p["hprev"]) for p in P]
        for pr, p in enumerate(P):
            hst_ref[pr] = p["hprev"]
            y_ref[:, p["sl"]] = ys[2 * pr] + ys[2 * pr + 1] + p["lam2"] * zs[pr]
            lam_rows = jnp.where(row < 64, p["lam2"][L - 1:L, 0:1], p["lam2"][L - 1:L, 64:65])
            h_scr[pr] = lam_rows * p["hprev"] + jnp.where(row < 64, sts[2 * pr], sts[2 * pr + 1])

    return pl.pallas_call(
        body, name="ssd_fwd", grid=(2, nc),
        in_specs=[pl.BlockSpec((L, 512), lambda g, c: (c, g)),
                  pl.BlockSpec((L, 128), lambda g, c: (c, g)),
                  pl.BlockSpec((L, 128), lambda g, c: (c, 2 + g)),
                  pl.BlockSpec((L, 512), lambda g, c: (c, g)),
                  pl.BlockSpec((L, 512), lambda g, c: (c, g)),
                  pl.BlockSpec((8, 1, L), lambda g, c: (g, 0, c))],
        out_specs=[pl.BlockSpec((L, 512), lambda g, c: (c, g)),
                   pl.BlockSpec((None, None, 4, 128, 128), lambda g, c: (g, c, 0, 0, 0))],
        out_shape=[jax.ShapeDtypeStruct((T, D), F32), jax.ShapeDtypeStruct((2, nc, 4, 128, 128), F32)],
        scratch_shapes=[pltpu.VMEM((4, 128, 128), F32)],
        compiler_params=_cparams(("parallel", "arbitrary")),
    )(xs, bc, bc, dt_x, acs_x, acs_t)


def ssd_bwd(xs, bc, dt_x, acs_x, acs_t, dy, hstates):
    T = xs.shape[0]
    nc, L = T // SSM_L, SSM_L

    def body(x_ref, b_ref, c_ref, dt_ref, ac_ref, at_ref, dy_ref, hst_ref,
             dx_ref, db_ref, dc_ref, dgate_ref, dar_ref, dh_scr):
        @pl.when(pl.program_id(1) == 0)
        def _():
            dh_scr[...] = jnp.zeros_like(dh_scr)

        bm, cm = b_ref[...], c_ref[...]
        cbm = mm_nt(cm, bm)
        row, col = _iota((L, L), 0), _iota((L, L), 1)
        rowc = _iota((L, 1), 0)
        incl = row >= col
        prs = range(4)
        P = []
        for pr in prs:
            sl = slice(128 * pr, 128 * pr + 128)
            x2, dt2, dy2, acs2 = x_ref[:, sl], dt_ref[:, sl], dy_ref[:, sl], ac_ref[:, sl]
            lam2, dec2, xd2 = _ssd_pair(x2, dt2, acs2)
            P.append(dict(sl=sl, x2=x2, dt2=dt2, dy2=dy2, acs2=acs2, lam2=lam2, dec2=dec2, xd2=xd2,
                          hprev=hst_ref[pr], dhn=dh_scr[pr], dz=lam2 * dy2))
        zs = [mm_nt(cm, p["hprev"]) for p in P]
        dcm_t = [mm(p["dz"], p["hprev"]) for p in P]
        dh_z = [mm_tn(p["dz"], cm) for p in P]
        H = []
        for pr in prs:
            p = P[pr]
            p["yoff"] = p["dz"] * zs[pr]
            p["q_rows"] = _lanes(p["dhn"] * p["hprev"])
            for hh in range(2):
                lmask, sg, mmat, dec_col, bd = _ssd_head(hh, p["acs2"], at_ref[2 * pr + hh], p["dec2"], cbm, bm, incl, col)
                H.append(dict(p=p, hh=hh, j=2 * pr + hh, lmask=lmask, sg=sg, mmat=mmat, dec_col=dec_col, bd=bd))
        dms = [mm_nt(jnp.where(h["lmask"], h["p"]["dy2"], 0.0), h["p"]["xd2"]) for h in H]
        a1s = [mm_tn(h["mmat"], h["p"]["dy2"]) for h in H]
        a2s = [mm_nt(h["bd"], h["p"]["dhn"]) for h in H]
        dbds = [mm(jnp.where(h["lmask"], h["p"]["xd2"], 0.0), h["p"]["dhn"]) for h in H]
        dcb = jnp.zeros((L, L), F32)
        dbm = jnp.zeros((L, SSM_N), F32)
        comp = jnp.zeros((L, 128), F32)
        dxd = [jnp.zeros((L, 128), F32) for _ in prs]
        for h, dm_raw, a1, a2, dbd in zip(H, dms, a1s, a2s, dbds):
            p, hh, j = h["p"], h["hh"], h["j"]
            dm = jnp.where(incl, dm_raw, 0.0)
            dcb = dcb + dm * h["sg"]
            e = dm * h["mmat"]
            dxd_h = jnp.where(h["lmask"], a1 + a2, 0.0)
            dxd[j // 2] = dxd[j // 2] + dxd_h
            dbm = dbm + h["dec_col"] * dbd
            t = _lanes(dbd * h["bd"])
            lam_h = p["lam2"][L - 1:L, 64 * hh:64 * hh + 1]
            in_head = (rowc >= 64 * hh) & (rowc < 64 * hh + 64)
            add_last = _sum_all(t) + _sum_all(jnp.where(in_head, p["q_rows"], 0.0)) * lam_h
            dacs_col = (_lanes(jnp.where(h["lmask"], p["yoff"], 0.0)) + _lanes(e) - t
                        + jnp.where(rowc == L - 1, add_last, 0.0))
            ddt_col = _lanes(dxd_h * p["x2"])
            dar_ref[j] = _rows(e)
            comp = comp + jnp.where(col == j, dacs_col, 0.0) + jnp.where(col == 8 + j, ddt_col, 0.0)
        dcm = dcm_t[0]
        for pr in prs:
            p = P[pr]
            if pr:
                dcm = dcm + dcm_t[pr]
            lam_rows = jnp.where(row < 64, p["lam2"][L - 1:L, 0:1], p["lam2"][L - 1:L, 64:65])
            dh_scr[pr] = dh_z[pr] + lam_rows * p["dhn"]
            dx_ref[:, p["sl"]] = p["dt2"] * dxd[pr]
        db_ref[...] = dbm + mm_tn(dcb, cm)
        dc_ref[...] = dcm + mm(dcb, bm)
        dgate_ref[...] = comp

    rv = lambda g, c: (nc - 1 - c, g)
    rowspec = pl.BlockSpec((8, 1, L), lambda g, c: (g, 0, nc - 1 - c))
    return pl.pallas_call(
        body, name="ssd_bwd", grid=(2, nc),
        in_specs=[pl.BlockSpec((L, 512), rv),
                  pl.BlockSpec((L, 128), rv),
                  pl.BlockSpec((L, 128), lambda g, c: (nc - 1 - c, 2 + g)),
                  pl.BlockSpec((L, 512), rv),
                  pl.BlockSpec((L, 512), rv),
                  rowspec,
                  pl.BlockSpec((L, 512), rv),
                  pl.BlockSpec((None, None, 4, 128, 128), lambda g, c: (g, nc - 1 - c, 0, 0, 0))],
        out_specs=[pl.BlockSpec((L, 512), rv), pl.BlockSpec((L, 128), rv), pl.BlockSpec((L, 128), rv),
                   pl.BlockSpec((L, 128), rv), rowspec],
        out_shape=[jax.ShapeDtypeStruct((T, D), F32), jax.ShapeDtypeStruct((T, 256), F32),
                   jax.ShapeDtypeStruct((T, 256), F32), jax.ShapeDtypeStruct((T, 256), F32),
                   jax.ShapeDtypeStruct((SSM_H, 1, T), F32)],
        scratch_shapes=[pltpu.VMEM((4, 128, 128), F32)],
        compiler_params=_cparams(("parallel", "arbitrary")),
    )(xs, bc, bc, dt_x, acs_x, acs_t, dy, hstates)


def _pos():
    return lax.axis_index("x"), lax.axis_index("y"), lax.axis_index("c")


def _other_chips(x, y):
    return [(1 - x, y), (x, 1 - y), (1 - x, 1 - y)]


def _rcopy(src, dst, ssem, rsem, dev):
    return pltpu.make_async_remote_copy(src_ref=src, dst_ref=dst, send_sem=ssem, recv_sem=rsem,
                                        device_id=dev, device_id_type=MESH)


def _rows_at(start, n):
    return pl.ds(pl.multiple_of(start, 8), n)


def _comm_call(body, name, out_shape, n_in, scratch):
    return pl.pallas_call(
        body, name=name, out_shape=out_shape, in_specs=[ANY] * n_in,
        out_specs=[ANY] * len(out_shape) if isinstance(out_shape, (list, tuple)) else ANY,
        scratch_shapes=scratch,
        compiler_params=pltpu.CompilerParams(has_side_effects=True),
    )


def _dma_sems(n):
    return pltpu.SemaphoreType.DMA((n,))


def ag_chips(name, shard):
    rr, cc = shard.shape
    h, nq = rr // 2, ICI_CHUNKS
    hq = h // nq

    def body(x_ref, out_ref, ssem, rsem):
        x, y, c = _pos()
        chips = _other_chips(x, y)
        started = []
        for q in range(nq):
            rows = _rows_at(c * h + q * hq, hq)
            for j, (cx, cy) in enumerate(chips):
                cp = _rcopy(x_ref.at[rows], out_ref.at[j, rows], ssem.at[j * nq + q], rsem.at[j * nq + q], (cx, cy, c))
                cp.start()
                started.append(cp)
        for q in range(nq):
            rows = _rows_at(c * h + q * hq, hq)
            for j, (cx, cy) in enumerate(chips):
                blk = out_ref.at[j, rows]
                _rcopy(blk, blk, ssem.at[j * nq + q], rsem.at[j * nq + q], (cx, cy, c)).wait_recv()
                k = 3 * nq + j * nq + q
                cp = _rcopy(blk, blk, ssem.at[k], rsem.at[k], (x, y, 1 - c))
                cp.start()
                started.append(cp)
        for q in range(nq):
            rows = _rows_at((1 - c) * h + q * hq, hq)
            for j in range(3):
                blk = out_ref.at[j, rows]
                k = 3 * nq + j * nq + q
                _rcopy(blk, blk, ssem.at[k], rsem.at[k], (x, y, 1 - c)).wait_recv()
        for cp in started:
            cp.wait_send()

    return _comm_call(body, name, jax.ShapeDtypeStruct((3, rr, cc), shard.dtype), 1,
                      [_dma_sems(6 * nq), _dma_sems(6 * nq)])(shard)


def _by_chip(shard, got, s_me):
    by_rel = jnp.stack([shard, got[1], got[0], got[2]])
    return jnp.take(by_rel, jnp.arange(4) ^ s_me, axis=0)


def all_gather_chips(name, shard, s_me):
    return _by_chip(shard, ag_chips(name, shard), s_me)


HBM_SPEC = pl.BlockSpec(memory_space=pltpu.HBM)
SEM_SPEC = pl.BlockSpec(memory_space=pltpu.SEMAPHORE)
SPLIT_EFFECT = pltpu.SideEffectType.DATAFLOW_SIDE_EFFECTING


def _split_copies(pieces, x_ref, land_ref, sems):
    x, y, c = _pos()
    return [_rcopy(s, d, sems[j], sems[3 + j], dev) for j, (s, d, dev) in enumerate(pieces(x_ref, land_ref, x, y, c))]


def split_copy_start(name, src, land_shape, pieces, after):
    def body(x_ref, land_ref, after_ref, *outs):
        for cp in _split_copies(pieces, x_ref, land_ref, outs[:6]):
            cp.start()
        outs[8][...] = jnp.zeros_like(outs[8])

    dma = pltpu.SemaphoreType.DMA(())
    res = pl.pallas_call(
        body, name=name,
        out_shape=(dma,) * 6 + (pltpu.HBM(src.shape, src.dtype), pltpu.HBM(land_shape, src.dtype),
                                jax.ShapeDtypeStruct((8, 128), F32)),
        in_specs=(HBM_SPEC, HBM_SPEC, ANY),
        out_specs=(SEM_SPEC,) * 6 + (HBM_SPEC, HBM_SPEC, pl.BlockSpec(memory_space=pltpu.VMEM)),
        input_output_aliases={0: 6, 1: 7},
        compiler_params=pltpu.CompilerParams(has_side_effects=SPLIT_EFFECT),
    )(pltpu.with_memory_space_constraint(src, pltpu.HBM),
      pltpu.with_memory_space_constraint(lax.empty(land_shape, src.dtype), pltpu.HBM), after)
    return res[:6], res[6], res[7], res[8]


def split_copy_wait(name, sems, src_thru, land_thru, after, pieces):
    def body(x_ref, land_ref, *rest):
        for cp in _split_copies(pieces, x_ref, land_ref, rest[:6]):
            cp.wait_send()
            cp.wait_recv()

    return pl.pallas_call(
        body, name=name,
        out_shape=(pltpu.HBM(src_thru.shape, src_thru.dtype), pltpu.HBM(land_thru.shape, land_thru.dtype)),
        in_specs=(HBM_SPEC, HBM_SPEC) + (SEM_SPEC,) * 6 + (ANY,), out_specs=(HBM_SPEC, HBM_SPEC),
        input_output_aliases={0: 0, 1: 1},
        compiler_params=pltpu.CompilerParams(has_side_effects=SPLIT_EFFECT),
    )(src_thru, land_thru, *sems, after)


def ag_pieces(h):
    def pieces(x_ref, land_ref, x, y, c):
        rows = _rows_at(c * h, h)
        return [(x_ref.at[rows], land_ref.at[j, rows], (cx, cy, c)) for j, (cx, cy) in enumerate(_other_chips(x, y))]
    return pieces


def rs_pieces(x_ref, land_ref, x, y, c):
    return [(x_ref.at[2 * cx + cy], land_ref.at[j], (cx, cy, c)) for j, (cx, cy) in enumerate(_other_chips(x, y))]


def ag_forward(name, got):
    _, rr, cc = got.shape
    h, nq = rr // 2, D2D_CHUNKS
    hq = h // nq

    def body(g_ref, out_ref, ssem, rsem):
        x, y, c = _pos()
        cps = []
        for j in range(3):
            for q in range(nq):
                blk = out_ref.at[j, _rows_at(c * h + q * hq, hq)]
                cp = _rcopy(blk, blk, ssem.at[j * nq + q], rsem.at[j * nq + q], (x, y, 1 - c))
                cp.start()
                cps.append(cp)
        for cp in cps:
            cp.wait_send()
        for j in range(3):
            for q in range(nq):
                blk = out_ref.at[j, _rows_at((1 - c) * h + q * hq, hq)]
                _rcopy(blk, blk, ssem.at[j * nq + q], rsem.at[j * nq + q], (x, y, 1 - c)).wait_recv()

    return pl.pallas_call(
        body, name=name, out_shape=jax.ShapeDtypeStruct(got.shape, got.dtype), in_specs=[ANY], out_specs=ANY,
        scratch_shapes=[_dma_sems(3 * nq), _dma_sems(3 * nq)], input_output_aliases={0: 0},
        compiler_params=pltpu.CompilerParams(has_side_effects=True),
    )(got)


def rs_pair(name, g):
    _, rr, cc = g.shape
    h, nq = rr // 2, D2D_CHUNKS
    hq = h // nq

    def body(g_ref, recv_ref, ssem, rsem):
        x, y, c = _pos()
        cps = []
        for q in range(nq):
            cp = _rcopy(g_ref.at[:, _rows_at((1 - c) * h + q * hq, hq), :], recv_ref.at[:, pl.ds(q * hq, hq), :],
                        ssem.at[q], rsem.at[q], (x, y, 1 - c))
            cp.start()
            cps.append(cp)
        for cp in cps:
            cp.wait()

    return _comm_call(body, name, jax.ShapeDtypeStruct((4, h, cc), g.dtype), 1, [_dma_sems(nq), _dma_sems(nq)])(g)


def rs_chips(name, p):
    _, h, cc = p.shape
    nq = ICI_CHUNKS
    hq = h // nq

    def body(p_ref, buf_ref, ssem, rsem):
        x, y, c = _pos()
        sends = []
        for q in range(nq):
            rows = pl.ds(q * hq, hq)
            for j, (cx, cy) in enumerate(_other_chips(x, y)):
                cp = _rcopy(p_ref.at[2 * cx + cy, rows], buf_ref.at[j, rows], ssem.at[j * nq + q],
                            rsem.at[j * nq + q], (cx, cy, c))
                cp.start()
                sends.append(cp)
        for cp in sends:
            cp.wait()

    return _comm_call(body, name, jax.ShapeDtypeStruct((3, h, cc), p.dtype), 1,
                      [_dma_sems(3 * nq), _dma_sems(3 * nq)])(p)


def rs_join(name, half):
    h, cc = half.shape
    nq = D2D_CHUNKS
    hq = h // nq

    def body(h_ref, out_ref, ssem, rsem):
        x, y, c = _pos()
        cps = []
        for q in range(nq):
            rows = pl.ds(q * hq, hq)
            cp = _rcopy(h_ref.at[rows], out_ref.at[rows], ssem.at[q], rsem.at[q], (x, y, 1 - c))
            cp.start()
            cps.append(cp)
        for cp in cps:
            cp.wait()

    return _comm_call(body, name, jax.ShapeDtypeStruct((h, cc), half.dtype), 1, [_dma_sems(nq), _dma_sems(nq)])(half)


def reduce_scatter(tag, g, tb, sp):
    return rs_end(rs_begin(tag, g, tb, sp, False), None)


def rs_begin(tag, g, tb, sp, split, after=None):
    _, rr, cc = g.shape
    h = rr // 2
    nbh = h // tb
    recv = rs_pair(tag + "_pair", g)
    mine_rows = lambda i, s: (i // nbh) * (2 * nbh) + s[0] * nbh + i % nbh
    part = rowwise(add2_fn, tag + "_add", 4 * h, tb, [R(g.reshape(4 * rr, cc), off=mine_rows), R(recv.reshape(4 * h, cc))],
                   [], [(cc, BF16)], sp=sp)[0].reshape(4, h, cc)
    st = dict(tag=tag, tb=tb, sp=sp, split=split, part=part)
    if split:
        st["sems"], st["part"], st["land"], st["token"] = split_copy_start(tag + "_start", part, (3, h, cc), rs_pieces,
                                                                           sp if after is None else after)
    return st


def rs_end(st, after):
    tag, tb, sp, part = st["tag"], st["tb"], st["sp"], st["part"]
    _, h, cc = part.shape
    nbh = h // tb
    if st["split"]:
        part, buf = split_copy_wait(tag + "_wait", st["sems"], part, st["land"], after, rs_pieces)
    else:
        buf = rs_chips(tag + "_chips", part)
    red = rowwise(sum4_fn, tag + "_sum", h, tb,
                  [R(part.reshape(4 * h, cc), off=lambda i, s: s[1] * nbh + i)]
                  + [R(buf.reshape(3 * h, cc), off=k * nbh) for k in range(3)],
                  [], [(cc, F32)], sp=sp)[0]
    return red, rs_join(tag + "_join", red)


def adam_halves(name, w, m, v, red, other, tb, blk0, sp):
    nbh = red.shape[0] // tb

    def fn(i, n, s, w_, m_, v_, r_, o_):
        g = jnp.where((blk0 + i) // nbh == s[0], r_, o_)
        return (g,) + _adamw(w_, g, m_, v_)

    half_rows = lambda i, s: (blk0 + i) % nbh
    return rowwise(fn, name, w.shape[0], tb, [R(w), R(m), R(v), R(red, off=half_rows), R(other, off=half_rows)],
                   [], [(w.shape[1], F32)] * 4, sp=sp)


SMALL_LANES = 3 * D


def all_reduce_items(name, items):
    flat = [a for it in items for a in it]
    shapes = [(sum(a.shape[0] for a in it), it[0].shape[1]) for it in items]
    nrows = -(-sum(s[0] for s in shapes) // 8) * 8

    def body(*refs):
        ins, outs = refs[:len(flat)], refs[len(flat):len(flat) + len(items)]
        mine, buf, ssem, rsem = refs[len(flat) + len(items):]
        x, y, c = _pos()
        me = 4 * x + 2 * y + c
        mine[...] = jnp.zeros_like(mine)
        r = 0
        for ref in ins:
            mine[r:r + ref.shape[0], 0:ref.shape[1]] = ref[...]
            r += ref.shape[0]
        buf[me] = mine[...]
        cps = []
        for k in range(1, 8):
            dev = (x ^ (k >> 2), y ^ ((k >> 1) & 1), c ^ (k & 1))
            cp = _rcopy(mine, buf.at[me], ssem.at[k - 1], rsem.at[k - 1], dev)
            cp.start()
            cps.append(cp)
        for cp in cps:
            cp.wait()
        r = 0
        for (nr, n), out in zip(shapes, outs):
            acc = buf[0, r:r + nr, 0:n]
            for d in range(1, 8):
                acc = acc + buf[d, r:r + nr, 0:n]
            out[...] = acc
            r += nr

    vm = pl.BlockSpec(memory_space=pltpu.VMEM)
    return pl.pallas_call(
        body, name=name, out_shape=[jax.ShapeDtypeStruct(s, F32) for s in shapes],
        in_specs=[vm] * len(flat), out_specs=[vm] * len(items),
        scratch_shapes=[pltpu.VMEM((nrows, SMALL_LANES), F32), pltpu.VMEM((8, nrows, SMALL_LANES), F32),
                        _dma_sems(7), _dma_sems(7)],
        compiler_params=pltpu.CompilerParams(has_side_effects=True),
    )(*flat)


def adam_small(ws, gs, ms, vs):
    n = len(ws)

    def body(*refs):
        for k in range(n):
            w, g, m, v = (refs[j * n + k][...] for j in range(4))
            for j, val in enumerate(_adamw(w, g, m, v)):
                refs[(4 + j) * n + k][...] = val

    vm = pl.BlockSpec(memory_space=pltpu.VMEM)
    res = pl.pallas_call(
        body, name="adam_small", out_shape=[jax.ShapeDtypeStruct(w.shape, F32) for w in ws] * 3,
        in_specs=[vm] * (4 * n), out_specs=[vm] * (3 * n),
    )(*ws, *gs, *ms, *vs)
    return res[:n], res[n:2 * n], res[2 * n:]


def _sel(rows, cols, pairs):
    m = np.zeros((rows, cols), np.float32)
    for r, c in pairs:
        m[r, c] = 1.0
    return jnp.asarray(m)


def _pad_win(w):
    z = jnp.zeros((w.shape[0], 112), w.dtype)
    return jnp.concatenate([w[:, :4096], w[:, 4112:6672], w[:, 4096:4112], z, w[:, 6672:6688], z], axis=1)


def _unpad_win(wp):
    return jnp.concatenate([wp[:, :4096], wp[:, 6656:6672], wp[:, 4096:6656], wp[:, 6784:6800]], axis=1)


def kernel(x, mem, norm1_w, w_in, gdn_conv_w, gdn_a_log, gdn_dt_bias, gdn_norm_w, ssm_conv_w, ssm_conv_b, ssm_a_log, ssm_dt_bias, ssm_d, ssm_norm_w, w_out, norm2_w, mem_norm_w, wq_mem, wk_mem, wv_mem, wo_mem, norm3_w, w_up, w_down, final_norm_w, loss_target, m_norm1_w, m_w_in, m_gdn_conv_w, m_gdn_a_log, m_gdn_dt_bias, m_gdn_norm_w, m_ssm_conv_w, m_ssm_conv_b, m_ssm_a_log, m_ssm_dt_bias, m_ssm_d, m_ssm_norm_w, m_w_out, m_norm2_w, m_mem_norm_w, m_wq_mem, m_wk_mem, m_wv_mem, m_wo_mem, m_norm3_w, m_w_up, m_w_down, m_final_norm_w, v_norm1_w, v_w_in, v_gdn_conv_w, v_gdn_a_log, v_gdn_dt_bias, v_gdn_norm_w, v_ssm_conv_w, v_ssm_conv_b, v_ssm_a_log, v_ssm_dt_bias, v_ssm_d, v_ssm_norm_w, v_w_out, v_norm2_w, v_mem_norm_w, v_wq_mem, v_wk_mem, v_wv_mem, v_wo_mem, v_norm3_w, v_w_up, v_w_down, v_final_norm_w):
    T, M = x.shape[1], mem.shape[1]
    xi, yi, ci = _pos()
    s_me = 2 * xi + yi
    x0, mem0, tgt = x[0], mem[0], loss_target[0]
    tb = min(256, T)
    row = lambda v: v.reshape(1, -1)

    win_g = all_gather_chips("ag_win", w_in.astype(BF16), s_me)
    w_in_p = _pad_win(win_g.transpose(1, 0, 2).reshape(D, IN_COLS))
    keep = (ci == 0).astype(F32)
    gcw_z = lax.dynamic_update_slice(jnp.zeros((4, 3 * D), F32), gdn_conv_w * keep, (0, s_me * 768))
    scw_z = lax.dynamic_update_slice(jnp.zeros((4, 1536), F32), ssm_conv_w * keep, (0, s_me * 384))
    gcw, scw = all_reduce_items("ar_convw", [[gcw_z], [scw_z]])
    scw_x, scw_bc = scw[:, :D], scw[:, D:]
    rest_shard = jnp.concatenate([w_up, w_down, w_out, wq_mem, wk_mem, wv_mem, wo_mem], axis=0).astype(BF16)
    ag_sems, rest_thru, rest_land, ag_token = split_copy_start("ag_rest_start", rest_shard, (3,) + rest_shard.shape,
                                                               ag_pieces(rest_shard.shape[0] // 2), gcw)
    sp = jnp.stack([ci, s_me]).astype(jnp.int32)
    scb_x, scb_bc = row(ssm_conv_b[:D]), row(ssm_conv_b[D:])

    galog_c, gdtb_c = row(jnp.pad(gdn_a_log, (8, 112))), row(jnp.pad(gdn_dt_bias, (8, 112)))
    salog_c, sdtb_c = row(jnp.pad(ssm_a_log, (0, 112))), row(jnp.pad(ssm_dt_bias, (0, 112)))
    sd_x = row(jnp.repeat(ssm_d, 64))
    eb = _sel(128, D, [(h, 128 * h + l) for h in range(8) for l in range(128)])
    ea = _sel(128, D, [(8 + h, 128 * h + l) for h in range(8) for l in range(128)])
    e16 = _sel(128, D, [(h, 64 * h + l) for h in range(16) for l in range(64)])
    pb = _sel(D, 128, [(128 * h, h) for h in range(8)])
    pa = _sel(D, 128, [(128 * h, 8 + h) for h in range(8)])

    h1 = rowwise(rms_fwd_fn, "rms1", T, tb, [R(x0)], [row(norm1_w) + ag_token[0:1, 0:1]], [(D, BF16)])[0]
    p = matmul("mm_in", h1, w_in_p, "nn", 2048, 768, 1024, [F32])[0]
    gp_ins = [R(p, 3 * D, CB_QKV, "prev"), R(p, 128, CB_BA)]
    qn, kn, vv, gcs_x, beta_x, gcs_t = rowwise(gdn_prep_fn, "gdn_prep", T, tb, gp_ins,
                                               [gcw, galog_c, gdtb_c, eb, ea], [(D, F32)] * 5 + [(-8, F32)])
    gcs_t = gcs_t.reshape(GDN_H, 1, T)
    gtb, ggh = min(128, T), 8
    o_gdn, s_states, tinv = gdn_fwd(qn, kn, vv, gcs_x, beta_x, gcs_t, gtb, ggh)
    gnw = row(gdn_norm_w)
    oa = rowwise(gdn_post_fn, "gdn_post", T, tb, [R(o_gdn), R(p, D, CB_Z)], [gnw], [(D, BF16)])[0]
    sp_ins = [R(p, D, CB_XS, "prev"), R(p, 512, CB_BC, "prev"), R(p, 128, CB_DT)]
    sp_full = [scw_x, scw_bc, scb_x, scb_bc, salog_c, sdtb_c]
    xs, bc, dt_x, acs_x, acs_t = rowwise(ssd_prep_fn, "ssd_prep", T, tb, sp_ins, sp_full + [e16],
                                         [(D, F32), (512, F32), (D, F32), (D, F32), (-SSM_H, F32)])
    acs_t = acs_t.reshape(SSM_H, 1, T)
    y_ssd, h_states = ssd_fwd(xs, bc, dt_x, acs_x, acs_t)
    snw = row(ssm_norm_w)
    ob = rowwise(ssd_post_fn, "ssd_post", T, tb, [R(y_ssd), R(xs), R(p, D, CB_ZS)], [sd_x, snw], [(D, BF16)])[0]
    rest_thru, rest_land = split_copy_wait("ag_rest_wait", ag_sems, rest_thru, rest_land, ob,
                                           ag_pieces(rest_shard.shape[0] // 2))
    rest_g = _by_chip(rest_thru, ag_forward("ag_rest_fwd", rest_land), s_me)
    wup_f = rest_g[:, 0:1024].transpose(1, 0, 2).reshape(D, D_FF)
    wdown_f = rest_g[:, 1024:2048].reshape(D_FF, D)
    wout_f = rest_g[:, 2048:2560].reshape(2, D, D)
    wq_f, wk_f, wv_f, wo_f = (rest_g[:, 2560 + 256 * k:2816 + 256 * k].reshape(D, D) for k in range(4))
    x1a = matmul("mm_out_a", oa, wout_f, "nn", 1024, 1024, 1024, [F32], _epi_res, [x0], b_sel=0)[0]
    assert D == 1024
    x1, h2 = matmul("mm_out_b", ob, wout_f, "nn", 1024, 1024, 1024, [F32, BF16], _epi_res_rms, [x1a],
                    [row(norm2_w)], b_sel=1)

    mn = rowwise(rms_fwd_fn, "rms_mem", M, M, [R(mem0)], [row(mem_norm_w)], [(D, BF16)])[0]
    km = matmul("mm_k", mn, wk_f, "nn", 256, 1024, 1024, [BF16])[0]
    vm = matmul("mm_v", mn, wv_f, "nn", 256, 1024, 1024, [BF16])[0]
    qm = matmul("mm_q", h2, wq_f, "nn", 1024, 1024, 1024, [BF16])[0]
    ao = rowwise(attn_fn, "attn", T, tb, [R(qm)], [km, vm], [(D, BF16)])[0]
    x2, h3 = matmul("mm_o", ao, wo_f, "nn", 1024, 1024, 1024, [F32, BF16], _epi_res_rms, [x1], [row(norm3_w)])
    u, act = matmul("mm_up", h3, wup_f, "nn", 2048, 1024, 1024, [BF16, BF16], _epi_relu2)
    dx3, dx3b, loss_lane, g_final = matmul("mm_down", act, wdown_f, "nn", 512, 1024, 1024, [F32, BF16], _epi_final,
                                           [x2, tgt], [row(final_norm_w)], n_acc=2)
    loss = lax.psum(0.5 / D * jnp.sum(loss_lane), ("x", "y", "c"))

    dup = matmul("mm_dact", dx3b, wdown_f, "nt", 2048, 1024, 1024, [BF16], _epi_dup, [u])[0]
    def g_into(buf, blk, at):
        return dict(into=(buf, blk, lambda i, j, k, at=at: at(i, j)))

    grest = jax.ShapeDtypeStruct((4, 3584, D), F32)
    grest = matmul("mm_gdown", act, dx3b, "tn", 1024, 1024, 2048, [F32],
                   **g_into(grest, (None, 1024, D), lambda i, j: (i, 1, 0)))
    dx2, dx2b, g_n3 = matmul("mm_dh3", dup, wup_f, "nt", 512, 1024, 1024, [F32, BF16], _epi_rms_bwd, [x2, dx3],
                             [row(norm3_w)], n_acc=1)
    grest = matmul("mm_gup", h3, dup, "tn", 1024, 1024, 2048, [F32],
                   **g_into(grest, (None, 1024, D), lambda i, j: (j, 0, 0)))
    dao = matmul("mm_dao", dx2b, wo_f, "nt", 1024, 1024, 1024, [F32])[0]
    grest = matmul("mm_gwo", ao, dx2b, "tn", 1024, 1024, 2048, [F32],
                   **g_into(grest, (4, 256, D), lambda i, j: (0, 13, 0)))
    dqm, dkm, dvm = rowwise(attn_bwd_fn, "attn_bwd", T, tb, [R(qm), R(dao)], [km, vm], [(D, BF16)],
                            [(M, D), (M, D)])
    dx1, dx1b, g_n2 = matmul("mm_dh2", dqm, wq_f, "nt", 512, 1024, 1024, [F32, BF16], _epi_rms_bwd, [x1, dx2],
                             [row(norm2_w)], n_acc=1)
    grest = matmul("mm_gwq", h2, dqm, "tn", 1024, 1024, 2048, [F32],
                   **g_into(grest, (4, 256, D), lambda i, j: (0, 10, 0)))
    grest = matmul("mm_gwk", mn, dkm, "tn", 1024, 1024, 256, [F32],
                   **g_into(grest, (4, 256, D), lambda i, j: (0, 11, 0)))
    grest = matmul("mm_gwv", mn, dvm, "tn", 1024, 1024, 256, [F32],
                   **g_into(grest, (4, 256, D), lambda i, j: (0, 12, 0)))
    dmn_k = matmul("mm_dmk", dkm, wk_f, "nt", 256, 1024, 1024, [F32])[0]
    dmn = matmul("mm_dmv", dvm, wv_f, "nt", 256, 1024, 1024, [F32], _epi_res, [dmn_k])[0]
    g_nmem = rowwise(rms_bwd_w_fn, "rmsmem_bwd", M, M, [R(mem0), R(dmn)], [row(mem_norm_w)], [], [(1, D)])[0]
    doa = matmul("mm_doa", dx1b, wout_f, "nt", 2048, 1024, 1024, [F32], b_sel=0)[0]
    dob = matmul("mm_dob", dx1b, wout_f, "nt", 2048, 1024, 1024, [F32], b_sel=1)[0]
    grest = matmul("mm_gwout_a", oa, dx1b, "tn", 1024, 1024, 2048, [F32],
                   **g_into(grest, (2, 512, D), lambda i, j: (0, 4, 0)))
    grest = matmul("mm_gwout_b", ob, dx1b, "tn", 1024, 1024, 2048, [F32],
                   **g_into(grest, (2, 512, D), lambda i, j: (1, 4, 0)))

    rs_rest = rs_begin("rs_rest", grest, 256, sp, True)

    dp = jax.ShapeDtypeStruct((T, p.shape[1]), BF16)
    dy_ssd, dxs_dir, dp, g_snw, g_sd_lane = rowwise(
        ssd_post_bwd_fn, "ssd_post_bwd", T, tb, [R(y_ssd), R(xs), R(p, D, CB_ZS), R(dob)],
        [sd_x + rs_rest["token"][0:1, 0:1], snw],
        [(D, F32), (D, F32), (D, BF16, dp, CB_ZS)], [(1, D), (1, D)])
    dxs_scan, db_s, dc_s, dgate, dacs_t = ssd_bwd(xs, bc, dt_x, acs_x, acs_t, dy_ssd, h_states)
    spb = rowwise(ssd_prep_bwd_fn, "ssd_prep_bwd", T, tb,
                  sp_ins + [R(dxs_scan), R(dxs_dir), R(db_s), R(dc_s), R(dgate), RC(dacs_t.reshape(SSM_H, T))], sp_full,
                  [(D, F32), (512, F32), (128, BF16, dp, CB_DT)],
                  [(1, D)] * 4 + [(1, 512)] * 4 + [(1, D), (1, 512), (1, 128), (1, 128)])
    dyc_x, dyc_bc, dp = spb[:3]
    dp = rowwise(conv_bwd_fn, "conv_bwd_x", T, tb, [R(dyc_x, halo="next")], [scw_x], [(D, BF16, dp, CB_XS)])[0]
    dp = rowwise(conv_bwd_fn, "conv_bwd_bc", T, tb, [R(dyc_bc, halo="next")], [scw_bc], [(512, BF16, dp, CB_BC)])[0]

    do_gdn, dp, g_gnw = rowwise(gdn_post_bwd_fn, "gdn_post_bwd", T, tb, [R(o_gdn), R(p, D, CB_Z), R(doa)], [gnw],
                                [(D, F32), (D, BF16, dp, CB_Z)], [(1, 128)])
    dqn, dkn, dvv, dgcs_x, dbeta_x, dgcs_t = gdn_bwd(qn, kn, vv, gcs_x, beta_x, gcs_t, do_gdn, s_states, tinv, gtb, ggh)
    gpb = rowwise(gdn_prep_bwd_fn, "gdn_prep_bwd", T, tb,
                  gp_ins + [R(dqn), R(dkn), R(dvv), R(dgcs_x), R(dbeta_x), RC(dgcs_t.reshape(GDN_H, T))],
                  [gcw, galog_c, gdtb_c, pb, pa],
                  [(3 * D, F32), (128, BF16, dp, CB_BA)], [(1, 3 * D)] * 4 + [(1, 128), (1, 128)])
    dyc_qkv, dp = gpb[:2]
    dp = rowwise(conv_bwd_fn, "conv_bwd_qkv", T, tb, [R(dyc_qkv, halo="next")], [gcw], [(3 * D, BF16, dp, CB_QKV)])[0]
    dh1 = matmul("mm_dh1", dp, w_in_p, "nt", 2048, 1024, 768, [F32])[0]
    grad_x, g_n1 = rowwise(rms_bwd1_fn, "rms1_bwd", T, tb, [R(x0), R(dh1), R(dx1)], [row(norm1_w)], [(D, F32)], [(1, D)])
    g_win_p = matmul("mm_gwin", h1, dp, "tn", 1024, 768, 2048, [F32])[0]

    items = [[g_n1], [gpb[6]], [gpb[7]], [g_gnw], [spb[11]], [spb[12]], [spb[13]], [spb[14]], [g_sd_lane], [g_snw],
             [g_n2], [g_nmem], [g_n3], [g_final], list(gpb[2:6]), list(spb[3:7]), list(spb[7:11])]
    (gr_n1, r_galog, r_gdtb, gr_gnw, r_scb_x, r_scb_bc, r_salog, r_sdtb, r_sd, gr_snw, gr_n2, gr_nmem, gr_n3,
     gr_final, r_gcw, r_scw_x, r_scw_bc) = all_reduce_items("ar_grads", items)
    gr_galog, gr_gdtb = r_galog[:, 8:16], r_gdtb[:, 8:16]
    gr_salog, gr_sdtb = r_salog[:, :SSM_H], r_sdtb[:, :SSM_H]
    gr_sd = r_sd.reshape(SSM_H, SSM_P).sum(axis=1).reshape(1, SSM_H)
    gr_scb = jnp.concatenate([r_scb_x, r_scb_bc], axis=1)
    gr_gcw = lax.dynamic_slice(r_gcw, (0, s_me * 768), (4, 768))
    gr_scw = lax.dynamic_slice(jnp.concatenate([r_scw_x, r_scw_bc], axis=1), (0, s_me * 384), (4, 384))

    g_win = _unpad_win(g_win_p).reshape(D, 4, IN_COLS // 4).transpose(1, 0, 2)
    rs_win = rs_begin("rs_win", g_win, 256, sp, True, gr_n1)
    red_r, oth_r = rs_end(rs_rest, rs_win["token"])

    big = {}
    for n, w, m, v, blk0 in (("w_up", w_up, m_w_up, v_w_up, 0), ("w_down", w_down, m_w_down, v_w_down, 4),
                             ("w_out", w_out, m_w_out, v_w_out, 8), ("wq_mem", wq_mem, m_wq_mem, v_wq_mem, 10),
                             ("wk_mem", wk_mem, m_wk_mem, v_wk_mem, 11), ("wv_mem", wv_mem, m_wv_mem, v_wv_mem, 12),
                             ("wo_mem", wo_mem, m_wo_mem, v_wo_mem, 13)):
        big[n] = adam_halves("adam_" + n, w, m, v, red_r, oth_r, 256, blk0, sp)
    red_w, oth_w = rs_end(rs_win, big["wo_mem"][1])
    big["w_in"] = adam_halves("adam_win", w_in, m_w_in, v_w_in, red_w, oth_w, 256, 0, sp)
    names_s =["norm1_w", "gdn_conv_w", "gdn_a_log", "gdn_dt_bias", "gdn_norm_w", "ssm_conv_w", "ssm_conv_b",
               "ssm_a_log", "ssm_dt_bias", "ssm_d", "ssm_norm_w", "norm2_w", "mem_norm_w", "norm3_w", "final_norm_w"]
    w_s = [norm1_w, gdn_conv_w, gdn_a_log, gdn_dt_bias, gdn_norm_w, ssm_conv_w, ssm_conv_b, ssm_a_log, ssm_dt_bias,
           ssm_d, ssm_norm_w, norm2_w, mem_norm_w, norm3_w, final_norm_w]
    g_s = [gr_n1, gr_gcw, gr_galog, gr_gdtb, gr_gnw, gr_scw, gr_scb, gr_salog, gr_sdtb, gr_sd, gr_snw, gr_n2,
           gr_nmem, gr_n3, gr_final]
    m_s = [m_norm1_w, m_gdn_conv_w, m_gdn_a_log, m_gdn_dt_bias, m_gdn_norm_w, m_ssm_conv_w, m_ssm_conv_b, m_ssm_a_log,
           m_ssm_dt_bias, m_ssm_d, m_ssm_norm_w, m_norm2_w, m_mem_norm_w, m_norm3_w, m_final_norm_w]
    v_s = [v_norm1_w, v_gdn_conv_w, v_gdn_a_log, v_gdn_dt_bias, v_gdn_norm_w, v_ssm_conv_w, v_ssm_conv_b, v_ssm_a_log,
           v_ssm_dt_bias, v_ssm_d, v_ssm_norm_w, v_norm2_w, v_mem_norm_w, v_norm3_w, v_final_norm_w]
    shp_s = [w.shape for w in w_s]
    as2d = lambda a: a if a.ndim == 2 else a.reshape(1, -1)
    d_l, m_l, v_l = adam_small([as2d(a) for a in w_s], [as2d(a) for a in g_s], [as2d(a) for a in m_s],
                               [as2d(a) for a in v_s])

    grads, deltas, new_m, new_v = {}, {}, {}, {}
    for n, (gg, dd, mm_, vv_) in big.items():
        grads[n], deltas[n], new_m[n], new_v[n] = gg, dd, mm_, vv_
    for k, n in enumerate(names_s):
        grads[n] = g_s[k].reshape(shp_s[k])
        deltas[n], new_m[n], new_v[n] = (a[k].reshape(shp_s[k]) for a in (d_l, m_l, v_l))
    order = ["norm1_w", "w_in", "gdn_conv_w", "gdn_a_log", "gdn_dt_bias", "gdn_norm_w", "ssm_conv_w", "ssm_conv_b",
             "ssm_a_log", "ssm_dt_bias", "ssm_d", "ssm_norm_w", "w_out", "norm2_w", "mem_norm_w", "wq_mem", "wk_mem",
             "wv_mem", "wo_mem", "norm3_w", "w_up", "w_down", "final_norm_w"]
    return (loss, grad_x[None], *[grads[n] for n in order], *[deltas[n] for n in order],
            *[new_m[n] for n in order], *[new_v[n] for n in order])
```

```python
import numpy as np
import jax
import jax.numpy as jnp
from jax import lax
from jax.experimental import pallas as pl
from jax.experimental.pallas import tpu as pltpu

F32, BF16 = jnp.float32, jnp.bfloat16
MESH = pl.DeviceIdType.MESH
ANY = pl.BlockSpec(memory_space=pl.ANY)

EPS = 1e-6
D = 1024
GDN_H, GDN_DK, GDN_C = 8, 128, 64
SSM_H, SSM_P, SSM_N, SSM_L = 16, 64, 128, 128
MEM_H, MEM_DH = 4, 256
D_FF = 4096
IN_COLS = 6688
CB_QKV, CB_Z, CB_ZS, CB_XS, CB_BC, CB_BA, CB_DT = 0, 3, 4, 5, 12, 52, 53
VMEM_LIMIT = 56 * 1024 * 1024
D2D_CHUNKS = 8
ICI_CHUNKS = 4

ADAM_LR, ADAM_B1, ADAM_B2, ADAM_EPS, ADAM_WD, ADAM_STEP = 0.001, 0.9, 0.999, 1e-08, 0.01, 10


def _dg(a, b, ca, cb):
    return lax.dot_general(a, b, (((ca,), (cb,)), ((), ())), preferred_element_type=F32)


def _bf(x):
    return x.astype(BF16)


def mm(a, b):
    return _dg(_bf(a), _bf(b), 1, 0)


def mm_nt(a, b):
    return _dg(_bf(a), _bf(b), 1, 1)


def mm_tn(a, b):
    return _dg(_bf(a), _bf(b), 0, 0)


def mm_sel(a, sel):
    hi = a.astype(BF16)
    r1 = a - hi.astype(F32)
    mid = r1.astype(BF16)
    lo = (r1 - mid.astype(F32)).astype(BF16)
    s = sel.astype(BF16)
    return _dg(hi, s, 1, 0) + (_dg(mid, s, 1, 0) + _dg(lo, s, 1, 0))


def mm3(a, b):
    ah, bh = a.astype(BF16), b.astype(BF16)
    al, bl = (a - ah.astype(F32)).astype(BF16), (b - bh.astype(F32)).astype(BF16)
    return _dg(ah, bh, 1, 0) + (_dg(ah, bl, 1, 0) + _dg(al, bh, 1, 0))


def _iota(shape, dim):
    return lax.broadcasted_iota(jnp.int32, shape, dim)


def _chunk_cumsum(x, c):
    pos = _iota(x.shape, 0) & (c - 1)
    s = 1
    while s < c:
        x = x + jnp.where(pos >= s, pltpu.roll(x, s, 0), 0.0)
        s *= 2
    return x


def _chunk_revcumsum(x, c):
    n = x.shape[0]
    pos = _iota(x.shape, 0) & (c - 1)
    s = 1
    while s < c:
        x = x + jnp.where(pos < c - s, pltpu.roll(x, n - s, 0), 0.0)
        s *= 2
    return x


def _sig(x):
    return jax.nn.sigmoid(x)


def _softplus(x):
    return jnp.maximum(x, 0.0) + jnp.log(1.0 + jnp.exp(-jnp.abs(x)))


def _rows(v):
    return jnp.sum(v, axis=0, keepdims=True)


def _lanes(v):
    return jnp.sum(v, axis=1, keepdims=True)


def _sum_all(v):
    return _rows(_lanes(v))


def _cparams(sem):
    return pltpu.CompilerParams(dimension_semantics=sem, vmem_limit_bytes=VMEM_LIMIT)


def rowwise(fn, name, T, tb, row_ins, full_ins, row_outs, acc_outs=(), sp=None):
    nblk = T // tb
    assert nblk * tb == T
    has_sp = sp is not None

    def imap(f):
        return (lambda i, s: f(i, s)) if has_sp else (lambda i: f(i, None))

    in_specs, args = [], []
    for arr, w, cb, halo, off in row_ins:
        if halo == "col":
            in_specs.append(pl.BlockSpec((w, tb), imap(lambda i, s: (0, i))))
            args.append(arr)
            continue
        rowf = off if callable(off) else (lambda i, s, off=off: i + off)
        in_specs.append(pl.BlockSpec((tb, w), imap(lambda i, s, cb=cb, rowf=rowf: (rowf(i, s), cb))))
        args.append(arr)
        if halo == "prev":
            r = tb // 8
            in_specs.append(pl.BlockSpec((8, w), imap(lambda i, s, cb=cb, r=r: (jnp.maximum(i * r - 1, 0), cb))))
            args.append(arr)
        elif halo == "next":
            r, last = tb // 8, T // 8 - 1
            in_specs.append(pl.BlockSpec((8, w), imap(lambda i, s, cb=cb, r=r, last=last:
                                                      (jnp.minimum((i + 1) * r, last), cb))))
            args.append(arr)
    for arr in full_ins:
        in_specs.append(pl.BlockSpec(arr.shape, imap(lambda i, s, nd=arr.ndim: (0,) * nd)))
        args.append(arr)
    n_in, n_ro = len(args), len(row_outs)
    out_shape, out_specs, aliases = [], [], {}
    for k, (w, dt, *dest) in enumerate(row_outs):
        if dest:
            buf, cb = dest
            out_shape.append(jax.ShapeDtypeStruct(buf.shape, buf.dtype))
            out_specs.append(pl.BlockSpec((tb, w), imap(lambda i, s, cb=cb: (i, cb))))
            if not isinstance(buf, jax.ShapeDtypeStruct):
                aliases[len(args) + int(has_sp)] = k
                in_specs.append(ANY)
                args.append(buf)
        elif w < 0:
            out_shape.append(jax.ShapeDtypeStruct((-w, T), dt))
            out_specs.append(pl.BlockSpec((-w, tb), imap(lambda i, s: (0, i))))
        else:
            out_shape.append(jax.ShapeDtypeStruct((T, w), dt))
            out_specs.append(pl.BlockSpec((tb, w), imap(lambda i, s: (i, 0))))
    for shp in acc_outs:
        out_shape.append(jax.ShapeDtypeStruct(shp, F32))
        out_specs.append(pl.BlockSpec(shp, imap(lambda i, s, nd=len(shp): (0,) * nd)))

    def body(*refs):
        i = pl.program_id(0)
        if has_sp:
            sp_ref, refs = refs[0], refs[1:]
            vals = fn(i, nblk, sp_ref, *[r[...] for r in refs[:n_in]])
        else:
            vals = fn(i, nblk, *[r[...] for r in refs[:n_in]])
        outs = refs[n_in + len(aliases):]
        for ref, val in zip(outs[:n_ro], vals[:n_ro]):
            ref[...] = val.astype(ref.dtype)
        for ref, val in zip(outs[n_ro:], vals[n_ro:]):
            @pl.when(i == 0)
            def _(ref=ref, val=val):
                ref[...] = val

            @pl.when(i > 0)
            def _(ref=ref, val=val):
                ref[...] += val

    cparams = _cparams(("arbitrary",) if acc_outs else ("parallel",))
    if has_sp:
        return pl.pallas_call(
            body, name=name, out_shape=out_shape, compiler_params=cparams, input_output_aliases=aliases,
            grid_spec=pltpu.PrefetchScalarGridSpec(num_scalar_prefetch=1, grid=(nblk,), in_specs=in_specs,
                                                   out_specs=out_specs),
        )(sp, *args)
    return pl.pallas_call(
        body, name=name, grid=(nblk,), in_specs=in_specs, out_specs=out_specs, out_shape=out_shape,
        compiler_params=cparams, input_output_aliases=aliases,
    )(*args)


def R(arr, w=None, cb=0, halo=None, off=0):
    return (arr, arr.shape[1] if w is None else w, cb, halo, off)


def RC(arr):
    return (arr, arr.shape[0], 0, "col", 0)


def matmul(name, a, b, form, tm, tn, tk, out_dtypes, epi=None, extras=(), rows=(), into=None, n_acc=0, b_sel=None):
    bs = b.shape if b_sel is None else b.shape[1:]
    if form == "nn":
        (M, K), N = a.shape, bs[1]
    elif form == "nt":
        (M, K), N = a.shape, bs[0]
    else:
        (K, M), N = a.shape, bs[1]
    tm, tn, tk = min(tm, M), min(tn, N), min(tk, K)
    assert M % tm == 0 and N % tn == 0 and K % tk == 0, (name, M, N, K, tm, tn, tk)

    def b_spec_of(blk, at):
        if b_sel is None:
            return pl.BlockSpec(blk, lambda i, j, k: at(i, j, k))
        return pl.BlockSpec((None,) + blk, lambda i, j, k: (b_sel,) + at(i, j, k))

    if form == "nn":
        a_spec = pl.BlockSpec((tm, tk), lambda i, j, k: (i, k))
        b_spec = b_spec_of((tk, tn), lambda i, j, k: (k, j))
        ca, cb = 1, 0
    elif form == "nt":
        a_spec = pl.BlockSpec((tm, tk), lambda i, j, k: (i, k))
        b_spec = b_spec_of((tn, tk), lambda i, j, k: (j, k))
        ca, cb = 1, 1
    else:
        a_spec = pl.BlockSpec((tk, tm), lambda i, j, k: (k, i))
        b_spec = b_spec_of((tk, tn), lambda i, j, k: (k, j))
        ca, cb = 0, 0
    nk, ne, no = K // tk, len(extras) + len(rows), len(out_dtypes)
    if epi is None:
        epi = lambda acc: (acc,)

    assert n_acc == 0 or tn == N

    def body(a_ref, b_ref, *rest):
        ex, outs, accs, acc = rest[:ne], rest[ne:ne + no], rest[ne + no:ne + no + n_acc], rest[ne + no + n_acc]
        i, k = pl.program_id(0), pl.program_id(2)

        def finish(total):
            vals = epi(total, *[e[...] for e in ex])
            for r, v in zip(outs, vals[:no]):
                r[...] = v.astype(r.dtype).reshape(r.shape)
            for r, v in zip(accs, vals[no:]):
                @pl.when(i == 0)
                def _(r=r, v=v):
                    r[...] = v

                @pl.when(i > 0)
                def _(r=r, v=v):
                    r[...] += v

        prod = _dg(_bf(a_ref[...]), _bf(b_ref[...]), ca, cb)
        if nk == 1:
            finish(prod)
            return

        @pl.when(k == 0)
        def _():
            acc[...] = prod

        @pl.when(k > 0)
        def _():
            acc[...] += prod

        @pl.when(k == nk - 1)
        def _():
            finish(acc[...])

    mn = pl.BlockSpec((tm, tn), lambda i, j, k: (i, j))
    rw = pl.BlockSpec((1, tn), lambda i, j, k: (0, j))
    acc_scratch = pltpu.VMEM((tm, tn) if nk > 1 else (8, 128), F32)
    if into is not None:
        buf, blk, bmap = into
        assert ne == 0 and no == 1
        aliased = not isinstance(buf, jax.ShapeDtypeStruct)

        def body_into(a_ref, b_ref, *rest):
            body(a_ref, b_ref, *rest[-2:])

        return pl.pallas_call(
            body_into, name=name, grid=(M // tm, N // tn, nk),
            in_specs=[a_spec, b_spec] + ([ANY] if aliased else []), out_specs=pl.BlockSpec(blk, bmap),
            out_shape=jax.ShapeDtypeStruct(buf.shape, buf.dtype),
            scratch_shapes=[acc_scratch],
            input_output_aliases={2: 0} if aliased else {},
            compiler_params=_cparams(("parallel", "parallel", "arbitrary")),
        )(a, b, *([buf] if aliased else []))
    return pl.pallas_call(
        body, name=name, grid=(M // tm, N // tn, nk),
        in_specs=[a_spec, b_spec] + [mn] * len(extras) + [rw] * len(rows), out_specs=[mn] * no + [rw] * n_acc,
        out_shape=[jax.ShapeDtypeStruct((M, N), dt) for dt in out_dtypes] + [jax.ShapeDtypeStruct((1, N), F32)] * n_acc,
        scratch_shapes=[acc_scratch],
        compiler_params=_cparams(("arbitrary",) * 3 if n_acc else ("parallel", "parallel", "arbitrary")),
    )(a, b, *extras, *rows)


def _epi_res(acc, res):
    return (res + acc,)


def _epi_rms_bwd(acc, x, dres, w):
    return rms_bwd_fn(0, 0, x, acc, dres, w)


def rms_bwd1_fn(i, n, x, dh, dres, w):
    dx, _, gw = rms_bwd_fn(i, n, x, dh, dres, w)
    return dx, gw


def _epi_final(acc, res, tgt, w):
    return final_fn(0, 0, res + acc, tgt, w)


def _epi_res_rms(acc, res, w):
    x = res + acc
    return (x, x * lax.rsqrt(jnp.mean(x * x, axis=-1, keepdims=True) + EPS) * w)


def _epi_relu2(acc):
    u = jnp.maximum(acc, 0.0)
    return (u, u * u)


def _epi_dup(acc, u):
    return (acc * 2.0 * u.astype(F32),)


def _conv(x, halo, w, i):
    halo = jnp.where(i == 0, 0.0, halo)
    xt = jnp.concatenate([halo, x], axis=0)
    shifted = [pltpu.roll(xt, 3 - k, 0)[8:, :] for k in range(3)] + [x]
    y = shifted[3] * w[3:4, :]
    for k in range(3):
        y = y + shifted[k] * w[k:k + 1, :]
    return y, shifted


def _l2n(x, scale):
    outs = []
    for h in range(x.shape[1] // 128):
        xh = x[:, 128 * h:128 * h + 128]
        outs.append(xh * (lax.rsqrt(jnp.sum(xh * xh, axis=-1, keepdims=True) + EPS) * scale))
    return jnp.concatenate(outs, axis=1)


def _l2n_bwd(x, dy, scale):
    outs = []
    for h in range(x.shape[1] // 128):
        xh, dh = x[:, 128 * h:128 * h + 128], dy[:, 128 * h:128 * h + 128] * scale
        r = lax.rsqrt(jnp.sum(xh * xh, axis=-1, keepdims=True) + EPS)
        outs.append(r * dh - xh * (r * r * r) * jnp.sum(xh * dh, axis=-1, keepdims=True))
    return jnp.concatenate(outs, axis=1)


def rms_fwd_fn(i, n, x, w):
    r = lax.rsqrt(jnp.mean(x * x, axis=-1, keepdims=True) + EPS)
    return (x * r * w,)


def rms_bwd_fn(i, n, x, dh, dres, w):
    r = lax.rsqrt(jnp.mean(x * x, axis=-1, keepdims=True) + EPS)
    g = dh * w
    dx = dres + r * g - x * (r * r * r) * jnp.mean(x * g, axis=-1, keepdims=True)
    return dx, dx, _rows(dh * x * r)


def rms_bwd_w_fn(i, n, x, dh, w):
    r = lax.rsqrt(jnp.mean(x * x, axis=-1, keepdims=True) + EPS)
    return (_rows(dh * x * r),)


def final_fn(i, n, x, tgt, w):
    r = lax.rsqrt(jnp.mean(x * x, axis=-1, keepdims=True) + EPS)
    xn = x * r
    e = xn * w - tgt
    dy = e * (1.0 / D)
    g = dy * w
    dx = r * g - x * (r * r * r) * jnp.mean(x * g, axis=-1, keepdims=True)
    return dx, dx, _rows(e * e), _rows(dy * xn)


def _gdn_gates(ba, alog_c, dtb_c):
    col = _iota(ba.shape, 1)
    amask = (col >= 8) & (col < 16)
    beta = jnp.where(col < 8, _sig(ba), 0.0)
    z = ba + dtb_c
    ea_ = jnp.exp(alog_c)
    return beta, z, ea_, jnp.where(amask, -ea_ * _softplus(z), 0.0), amask


def _cols(x, g):
    return x[:, 128 * g:128 * g + 128]


def gdn_prep_fn(i, n, qkv, halo, ba, cw, alog_c, dtb_c, eb, ea):
    outs = [[], [], []]
    for g in range(3 * GDN_H):
        yc, _ = _conv(_cols(qkv, g), _cols(halo, g), _cols(cw, g), i)
        act = yc * _sig(yc)
        if g < 2 * GDN_H:
            act = _l2n(act, GDN_DK ** -0.5 if g < GDN_H else 1.0)
        outs[g // GDN_H].append(act)
    beta, _, _, gg, _ = _gdn_gates(ba, alog_c, dtb_c)
    gcs = _chunk_cumsum(gg, GDN_C)
    return (*[jnp.concatenate(o, axis=1) for o in outs], mm_sel(gcs, ea), mm_sel(beta, eb), jnp.transpose(gcs)[8:16, :])


def gdn_prep_bwd_fn(i, n, qkv, halo, ba, dqn, dkn, dv, dgcs_x, dbeta_x, dgcs_t, cw, alog_c, dtb_c, pb, pa):
    dycs, dwl = [], [[], [], [], []]
    for g in range(3 * GDN_H):
        yc, shifted = _conv(_cols(qkv, g), _cols(halo, g), _cols(cw, g), i)
        sg = _sig(yc)
        act = yc * sg
        if g < GDN_H:
            d = _l2n_bwd(act, _cols(dqn, g), GDN_DK ** -0.5)
        elif g < 2 * GDN_H:
            d = _l2n_bwd(act, _cols(dkn, g - GDN_H), 1.0)
        else:
            d = _cols(dv, g - 2 * GDN_H)
        dyc_g = d * (sg * (1.0 + yc * (1.0 - sg)))
        dycs.append(dyc_g)
        for k in range(4):
            dwl[k].append(_rows(dyc_g * shifted[k]))
    dyc = jnp.concatenate(dycs, axis=1)
    dws = [jnp.concatenate(l, axis=1) for l in dwl]
    beta, z, ea_, g, amask = _gdn_gates(ba, alog_c, dtb_c)
    tbn = ba.shape[0]
    rowpart = jnp.transpose(jnp.concatenate([jnp.zeros((8, tbn), F32), dgcs_t, jnp.zeros((112, tbn), F32)], axis=0))
    dg = _chunk_revcumsum(mm_sel(dgcs_x, pa) - rowpart, GDN_C)
    draw = jnp.where(amask, dg * (-ea_) * _sig(z), 0.0)
    dba = draw + mm_sel(dbeta_x, pb) * beta * (1.0 - beta)
    return (dyc, dba, dws[0], dws[1], dws[2], dws[3], _rows(dg * g), _rows(draw))


def conv_bwd_fn(i, n, dyc, halo, w):
    halo = jnp.where(i == n - 1, 0.0, halo)
    tb = dyc.shape[0]
    outs = []
    for g in range(dyc.shape[1] // 128):
        d, wg = _cols(dyc, g), _cols(w, g)
        xt = jnp.concatenate([d, _cols(halo, g)], axis=0)
        dx = d * wg[3:4, :]
        for k in range(3):
            dx = dx + pltpu.roll(xt, tb + 8 - (3 - k), 0)[:tb, :] * wg[k:k + 1, :]
        outs.append(dx)
    return (jnp.concatenate(outs, axis=1),)


def gdn_post_fn(i, n, o, z, w):
    outs = []
    for h in range(GDN_H):
        oh, zh = o[:, 128 * h:128 * h + 128], z[:, 128 * h:128 * h + 128]
        r = lax.rsqrt(jnp.mean(oh * oh, axis=-1, keepdims=True) + EPS)
        outs.append(oh * r * w * (zh * _sig(zh)))
    return (jnp.concatenate(outs, axis=1),)


def gdn_post_bwd_fn(i, n, o, z, doa, w):
    dos, dzs, dw = [], [], None
    for h in range(GDN_H):
        sl = slice(128 * h, 128 * h + 128)
        oh, zh, dh = o[:, sl], z[:, sl], doa[:, sl]
        r = lax.rsqrt(jnp.mean(oh * oh, axis=-1, keepdims=True) + EPS)
        s = _sig(zh)
        dn = dh * (zh * s)
        dzs.append(dh * (oh * r * w) * (s * (1.0 + zh * (1.0 - s))))
        t = _rows(dn * oh * r)
        dw = t if dw is None else dw + t
        g = dn * w
        dos.append(r * g - oh * (r * r * r) * jnp.mean(oh * g, axis=-1, keepdims=True))
    return jnp.concatenate(dos, axis=1), jnp.concatenate(dzs, axis=1), dw


def _ssd_gates(dtblk, alog_c, dtb_c):
    hmask = _iota(dtblk.shape, 1) < SSM_H
    z = dtblk + dtb_c
    return jnp.where(hmask, _softplus(z), 0.0), -jnp.exp(alog_c), z, hmask


def _silu_conv_cols(x, halo, w, b, i):
    outs = []
    for g in range(x.shape[1] // 128):
        yc, _ = _conv(_cols(x, g), _cols(halo, g), _cols(w, g), i)
        yc = yc + _cols(b, g)
        outs.append(yc * _sig(yc))
    return jnp.concatenate(outs, axis=1)


def _silu_conv_bwd_cols(x, halo, w, b, dout, i):
    dycs, dwl = [], [[], [], [], []]
    for g in range(x.shape[1] // 128):
        yc, shifted = _conv(_cols(x, g), _cols(halo, g), _cols(w, g), i)
        yc = yc + _cols(b, g)
        s = _sig(yc)
        dyc_g = _cols(dout, g) * (s * (1.0 + yc * (1.0 - s)))
        dycs.append(dyc_g)
        for k in range(4):
            dwl[k].append(_rows(dyc_g * shifted[k]))
    dyc = jnp.concatenate(dycs, axis=1)
    return dyc, [jnp.concatenate(l, axis=1) for l in dwl], _rows(dyc)


def ssd_prep_fn(i, n, xp, hx, bcp, hbc, dtblk, cwx, cwbc, cbx, cbbc, alog_c, dtb_c, e16):
    dt, a_neg, _, _ = _ssd_gates(dtblk, alog_c, dtb_c)
    acs = _chunk_cumsum(dt * a_neg, SSM_L)
    return (_silu_conv_cols(xp, hx, cwx, cbx, i), _silu_conv_cols(bcp, hbc, cwbc, cbbc, i), mm_sel(dt, e16),
            mm_sel(acs, e16), jnp.transpose(acs)[0:SSM_H, :])


def ssd_prep_bwd_fn(i, n, xp, hx, bcp, hbc, dtblk, dxs_a, dxs_b, db, dc, dgate, dacs_t, cwx, cwbc, cbx, cbbc, alog_c, dtb_c):
    dyx, dwx, dbx = _silu_conv_bwd_cols(xp, hx, cwx, cbx, dxs_a + dxs_b, i)
    dybc, dwbc, dbbc = _silu_conv_bwd_cols(bcp, hbc, cwbc, cbbc, jnp.concatenate([db, dc], axis=1), i)
    dt, a_neg, z, hmask = _ssd_gates(dtblk, alog_c, dtb_c)
    g0, g1 = dgate[:, :128], dgate[:, 128:]
    col = _iota(g0.shape, 1)
    lo, mid = col < 8, (col >= 8) & (col < 16)
    dacs_col = jnp.where(lo, g0, 0.0) + pltpu.roll(jnp.where(lo, g1, 0.0), 8, 1)
    ddt_dir = pltpu.roll(jnp.where(mid, g0, 0.0), 120, 1) + jnp.where(mid, g1, 0.0)
    tbn = dtblk.shape[0]
    rowpart = jnp.transpose(jnp.concatenate([dacs_t, jnp.zeros((128 - SSM_H, tbn), F32)], axis=0))
    da = _chunk_revcumsum(dacs_col - rowpart, SSM_L)
    draw = jnp.where(hmask, (ddt_dir + da * a_neg) * _sig(z), 0.0)
    return (dyx, dybc, draw, *dwx, *dwbc, dbx, dbbc, _rows(da * dt * a_neg), _rows(draw))


def _ssd_gate(y, xs, zs, d_x):
    y2 = y + xs * d_x
    s = _sig(zs)
    return y2, s, y2 * (zs * s)


def ssd_post_fn(i, n, y, xs, zs, d_x, nw):
    _, _, yg = _ssd_gate(y, xs, zs, d_x)
    outs = []
    for g in range(2):
        v = yg[:, 512 * g:512 * g + 512]
        outs.append(v * lax.rsqrt(jnp.mean(v * v, axis=-1, keepdims=True) + EPS))
    return (jnp.concatenate(outs, axis=1) * nw,)


def ssd_post_bwd_fn(i, n, y, xs, zs, dob, d_x, nw):
    y2, s, yg = _ssd_gate(y, xs, zs, d_x)
    gfull = dob * nw
    dygs, dnw = [], []
    for g in range(2):
        sl = slice(512 * g, 512 * g + 512)
        v, gg = yg[:, sl], gfull[:, sl]
        r = lax.rsqrt(jnp.mean(v * v, axis=-1, keepdims=True) + EPS)
        dygs.append(r * gg - v * (r * r * r) * jnp.mean(v * gg, axis=-1, keepdims=True))
        dnw.append(_rows(dob[:, sl] * v * r))
    dyg = jnp.concatenate(dygs, axis=1)
    dy2 = dyg * (zs * s)
    dzs = dyg * y2 * (s * (1.0 + zs * (1.0 - s)))
    return dy2, dy2 * d_x, dzs, jnp.concatenate(dnw, axis=1), _rows(dy2 * xs)


def _attn_probs(q, k):
    hs = [slice(MEM_DH * h, MEM_DH * h + MEM_DH) for h in range(MEM_H)]
    ss = [mm_nt(q[:, sl], k[:, sl]) * (MEM_DH ** -0.5) for sl in hs]
    es = [jnp.exp(s - jnp.max(s, axis=-1, keepdims=True)) for s in ss]
    return hs, [e / jnp.sum(e, axis=-1, keepdims=True) for e in es]


def attn_fn(i, n, q, k, v):
    hs, ps = _attn_probs(q, k)
    return (jnp.concatenate([mm(p, v[:, sl]) for p, sl in zip(ps, hs)], axis=1),)


def attn_bwd_fn(i, n, q, do, k, v):
    hs, ps = _attn_probs(q, k)
    dvs = [mm_tn(p, do[:, sl]) for p, sl in zip(ps, hs)]
    dps = [mm_nt(do[:, sl], v[:, sl]) for sl in hs]
    dss = [p * (dp - jnp.sum(dp * p, axis=-1, keepdims=True)) * (MEM_DH ** -0.5) for p, dp in zip(ps, dps)]
    dqs = [mm(ds, k[:, sl]) for ds, sl in zip(dss, hs)]
    dks = [mm_tn(ds, q[:, sl]) for ds, sl in zip(dss, hs)]
    return jnp.concatenate(dqs, axis=1), jnp.concatenate(dks, axis=1), jnp.concatenate(dvs, axis=1)


def add2_fn(i, n, sp, a, b):
    return (a + b,)


def sum4_fn(i, n, sp, a, b, c, d):
    return (((a.astype(F32) + b.astype(F32)) + c.astype(F32)) + d.astype(F32),)


def _adamw(w, g, m, v):
    m = ADAM_B1 * m + (1.0 - ADAM_B1) * g
    v = ADAM_B2 * v + (1.0 - ADAM_B2) * (g * g)
    m_hat = m / (1.0 - ADAM_B1 ** ADAM_STEP)
    v_hat = v / (1.0 - ADAM_B2 ** ADAM_STEP)
    delta = -ADAM_LR * (m_hat / (jnp.sqrt(v_hat) + ADAM_EPS) + ADAM_WD * w)
    return delta, m, v


def _gdn_stage1(q, k, v, gcs, grow, bb):
    C = GDN_C
    row, col = _iota((C, C), 0), _iota((C, C), 1)
    incl, strict = row >= col, row > col
    dmat = jnp.where(incl, jnp.exp(jnp.minimum(gcs[:, :C] - grow, 0.0)), 0.0)
    gam = jnp.exp(gcs)
    gl = gcs[C - 1:C, :]
    kb, vb = k * bb, v * bb
    kg = kb * gam
    lmat = jnp.where(strict, mm_nt(kb, k) * dmat, 0.0)
    pmat = jnp.where(incl, mm_nt(q, k) * dmat, 0.0)
    return dict(q=q, k=k, v=v, bb=bb, incl=incl, strict=strict, dmat=dmat, gam=gam, kb=kb, vb=vb, kg=kg,
                lmat=lmat, pmat=pmat, qd=q * gam, kdec=jnp.exp(gl - gcs), cd=jnp.exp(gl))


def _gdn_inverse(lmats):
    C = GDN_C
    eye = (_iota((C, C), 0) == _iota((C, C), 1)).astype(F32)
    xs = [-l for l in lmats]
    ts = [eye + x for x in xs]
    for _ in range(5):
        xs = [mm(x, x) for x in xs]
        ts = [t + mm(t, x) for t, x in zip(ts, xs)]
    res = [eye - mm3(eye + l, t) for l, t in zip(lmats, ts)]
    return [t + mm(t, r) for t, r in zip(ts, res)]


def gdn_fwd(qn, kn, v, gcs_x, beta_x, gcs_t, tb, gh):
    T = qn.shape[0]
    nb, ncb, nc, C = T // tb, tb // GDN_C, T // GDN_C, GDN_C
    idx = [(hh, c) for hh in range(gh) for c in range(ncb)]

    def body(q_ref, k_ref, v_ref, g_ref, b_ref, gt_ref, o_ref, st_ref, ti_ref, s_scr):
        @pl.when(pl.program_id(1) == 0)
        def _():
            s_scr[...] = jnp.zeros_like(s_scr)

        grows = [gt_ref[hh] for hh in range(gh)]
        at = lambda hh, c: (slice(C * c, C * (c + 1)), slice(128 * hh, 128 * hh + 128))
        st1 = []
        for hh, c in idx:
            sl, ln = at(hh, c)
            st1.append(_gdn_stage1(q_ref[sl, ln], k_ref[sl, ln], v_ref[sl, ln], g_ref[sl, ln], grows[hh][:, sl],
                                   b_ref[sl, ln]))
        tinvs = _gdn_inverse([s["lmat"] for s in st1])
        us = [mm(t, s["vb"]) for t, s in zip(tinvs, st1)]
        ws = [mm(t, s["kg"]) for t, s in zip(tinvs, st1)]
        kds = [s["k"] * s["kdec"] for s in st1]
        ms = [mm_tn(kd, w) for kd, w in zip(kds, ws)]
        bs = [mm_tn(kd, u) for kd, u in zip(kds, us)]
        gs = [s["qd"] - mm(s["pmat"], w) for s, w in zip(st1, ws)]
        pus = [mm(s["pmat"], u) for s, u in zip(st1, us)]
        ss = [s_scr[hh] for hh in range(gh)]
        for c in range(ncb):
            for hh in range(gh):
                n, (sl, ln) = hh * ncb + c, at(hh, c)
                ti_ref[hh, sl, :] = tinvs[n]
                st_ref[hh, c] = ss[hh]
                o_ref[sl, ln] = mm(gs[n], ss[hh]) + pus[n]
                ss[hh] = st1[n]["cd"] * ss[hh] - mm(ms[n], ss[hh]) + bs[n]
        for hh in range(gh):
            s_scr[hh] = ss[hh]

    blk = pl.BlockSpec((tb, 128 * gh), lambda h, i: (i, h))
    return pl.pallas_call(
        body, name="gdn_fwd", grid=(GDN_H // gh, nb),
        in_specs=[blk] * 5 + [pl.BlockSpec((gh, 1, tb), lambda h, i: (h, 0, i))],
        out_specs=[blk, pl.BlockSpec((gh, ncb, 128, 128), lambda h, i: (h, i, 0, 0)),
                   pl.BlockSpec((gh, tb, C), lambda h, i: (h, i, 0))],
        out_shape=[jax.ShapeDtypeStruct((T, D), F32), jax.ShapeDtypeStruct((GDN_H, nc, 128, 128), F32),
                   jax.ShapeDtypeStruct((GDN_H, T, C), F32)],
        scratch_shapes=[pltpu.VMEM((gh, 128, 128), F32)],
        compiler_params=_cparams(("parallel", "arbitrary")),
    )(qn, kn, v, gcs_x, beta_x, gcs_t)


def gdn_bwd(qn, kn, v, gcs_x, beta_x, gcs_t, do, states, tinv, tb, gh):
    T = qn.shape[0]
    nb, ncb, C = T // tb, tb // GDN_C, GDN_C

    def body(q_ref, k_ref, v_ref, g_ref, b_ref, gt_ref, do_ref, st_ref, ti_ref,
             dq_ref, dk_ref, dv_ref, dgc_ref, db_ref, dgr_ref, ds_scr):
        @pl.when(pl.program_id(1) == 0)
        def _():
            ds_scr[...] = jnp.zeros_like(ds_scr)

        grows = [gt_ref[hh] for hh in range(gh)]
        at = lambda hh, c: (slice(C * c, C * (c + 1)), slice(128 * hh, 128 * hh + 128))
        lastrow = _iota((C, 1), 0) == C - 1
        idx = [(hh, c) for hh in range(gh) for c in range(ncb)]
        P = []
        for hh, c in idx:
            sl, ln = at(hh, c)
            lc = _gdn_stage1(q_ref[sl, ln], k_ref[sl, ln], v_ref[sl, ln], g_ref[sl, ln], grows[hh][:, sl],
                             b_ref[sl, ln])
            lc.update(tinv=ti_ref[hh, sl, :], s=st_ref[hh, c], do=do_ref[sl, ln], kd=lc["k"] * lc["kdec"])
            P.append(lc)
        for l, u, w in zip(P, [mm(l["tinv"], l["vb"]) for l in P], [mm(l["tinv"], l["kg"]) for l in P]):
            l.update(u=u, w=w)
        for l, x in zip(P, [mm(l["w"], l["s"]) for l in P]):
            l["vn"] = l["u"] - x
        for l, a, b, c_, d in zip(P, [mm_nt(l["do"], l["s"]) for l in P], [mm_nt(l["do"], l["vn"]) for l in P],
                                  [mm_tn(l["qd"], l["do"]) for l in P], [mm_tn(l["pmat"], l["do"]) for l in P]):
            l.update(dqd=a, dp=jnp.where(l["incl"], b, 0.0), ds_q=c_, dvn_p=d)
        pre = dict(zip(idx, P))
        rows = {}
        hs = range(gh)
        ds = [ds_scr[hh] for hh in hs]
        for c in reversed(range(ncb)):
            L = [pre[hh, c] for hh in hs]
            dvn = [l["dvn_p"] + mm(l["kd"], d) for l, d in zip(L, ds)]
            dkd = [mm_nt(l["vn"], d) for l, d in zip(L, ds)]
            dcd = [_sum_all(l["s"] * d) for l, d in zip(L, ds)]
            ds = [l["ds_q"] + l["cd"] * d - mm_tn(l["w"], x) for l, d, x in zip(L, ds, dvn)]
            dw = [-mm_nt(x, l["s"]) for l, x in zip(L, dvn)]
            dvb = [mm_tn(l["tinv"], x) for l, x in zip(L, dvn)]
            dkg = [mm_tn(l["tinv"], x) for l, x in zip(L, dw)]
            da = [-jnp.where(l["strict"], mm_nt(a, l["u"]) + mm_nt(b, l["w"]), 0.0) for l, a, b in zip(L, dvb, dkg)]
            dm = [a * l["dmat"] for l, a in zip(L, da)]
            dn = [l["dp"] * l["dmat"] for l in L]
            dkb = [mm(a, l["k"]) for l, a in zip(L, dm)]
            dq = [mm(a, l["k"]) + l["gam"] * l["dqd"] for l, a in zip(L, dn)]
            dk = [mm_tn(a, l["kb"]) + mm_tn(b, l["q"]) for l, a, b in zip(L, dm, dn)]
            for hh in hs:
                sl, ln = at(hh, c)
                l = L[hh]
                e = da[hh] * l["lmat"] + l["dp"] * l["pmat"]
                t_kd = _lanes(dkd[hh] * l["kd"])
                dgl = _sum_all(t_kd) + dcd[hh] * l["cd"][:, :1]
                dgcs = (_lanes(e) + _lanes(l["dqd"] * l["qd"]) - t_kd + _lanes(dkg[hh] * l["kg"])
                        + jnp.where(lastrow, dgl, 0.0))
                rows[hh, c] = _rows(e)
                dq_ref[sl, ln] = dq[hh]
                dk_ref[sl, ln] = (dk[hh] + l["kdec"] * dkd[hh] + l["bb"] * l["gam"] * dkg[hh] + l["bb"] * dkb[hh])
                dv_ref[sl, ln] = l["bb"] * dvb[hh]
                dbeta = _lanes(dkg[hh] * l["gam"] * l["k"]) + _lanes(dvb[hh] * l["v"]) + _lanes(dkb[hh] * l["k"])
                db_ref[sl, ln] = jnp.broadcast_to(dbeta, (C, 128))
                dgc_ref[sl, ln] = jnp.broadcast_to(dgcs, (C, 128))
        for hh in hs:
            ds_scr[hh] = ds[hh]
            dgr_ref[hh] = jnp.concatenate([rows[hh, c] for c in range(ncb)], axis=1)

    blk = pl.BlockSpec((tb, 128 * gh), lambda h, i: (nb - 1 - i, h))
    rowspec = pl.BlockSpec((gh, 1, tb), lambda h, i: (h, 0, nb - 1 - i))
    return pl.pallas_call(
        body, name="gdn_bwd", grid=(GDN_H // gh, nb),
        in_specs=[blk] * 5 + [rowspec, blk,
                              pl.BlockSpec((gh, ncb, 128, 128), lambda h, i: (h, nb - 1 - i, 0, 0)),
                              pl.BlockSpec((gh, tb, C), lambda h, i: (h, nb - 1 - i, 0))],
        out_specs=[blk] * 5 + [rowspec],
        out_shape=[jax.ShapeDtypeStruct((T, D), F32)] * 5 + [jax.ShapeDtypeStruct((GDN_H, 1, T), F32)],
        scratch_shapes=[pltpu.VMEM((gh, 128, 128), F32)],
        compiler_params=_cparams(("parallel", "arbitrary")),
    )(qn, kn, v, gcs_x, beta_x, gcs_t, do, states, tinv)


def _ssd_pair(x2, dt2, acs2):
    last = acs2[SSM_L - 1:SSM_L, :]
    return jnp.exp(acs2), jnp.exp(last - acs2), x2 * dt2


def _ssd_head(hh, acs2, arow, dec2, cbm, bm, incl, col):
    lmask = (col >= 64 * hh) & (col < 64 * hh + 64)
    sg = jnp.where(incl, jnp.exp(jnp.minimum(acs2[:, 64 * hh:64 * hh + 1] - arow, 0.0)), 0.0)
    dec_col = dec2[:, 64 * hh:64 * hh + 1]
    return lmask, sg, sg * cbm, dec_col, bm * dec_col


def ssd_fwd(xs, bc, dt_x, acs_x, acs_t):
    T = xs.shape[0]
    nc, L = T // SSM_L, SSM_L

    def body(x_ref, b_ref, c_ref, dt_ref, ac_ref, at_ref, y_ref, hst_ref, h_scr):
        @pl.when(pl.program_id(1) == 0)
        def _():
            h_scr[...] = jnp.zeros_like(h_scr)

        bm, cm = b_ref[...], c_ref[...]
        cbm = mm_nt(cm, bm)
        row, col = _iota((L, L), 0), _iota((L, L), 1)
        incl = row >= col
        P, H = [], []
        for pr in range(4):
            sl = slice(128 * pr, 128 * pr + 128)
            acs2 = ac_ref[:, sl]
            lam2, dec2, xd2 = _ssd_pair(x_ref[:, sl], dt_ref[:, sl], acs2)
            P.append(dict(sl=sl, lam2=lam2, xd2=xd2, hprev=h_scr[pr]))
            for hh in range(2):
                lmask, _, mmat, _, bd = _ssd_head(hh, acs2, at_ref[2 * pr + hh], dec2, cbm, bm, incl, col)
                H.append(dict(mmat=mmat, bd=bd, xdh=jnp.where(lmask, xd2, 0.0), xd2=xd2))
        ys = [mm(h["mmat"], h["xdh"]) for h in H]
        sts = [mm_tn(h["xd2"], h["bd"]) for h in H]
        zs = [mm_nt(cm, p["hprev"]) for p in P]
        for pr, p in enumerate(P):
            hst_ref[pr] = p["hprev"]
            y_ref[:, p["sl"]] = ys[2 * pr] + ys[2 * pr + 1] + p["lam2"] * zs[pr]
            lam_rows = jnp.where(row < 64, p["lam2"][L - 1:L, 0:1], p["lam2"][L - 1:L, 64:65])
            h_scr[pr] = lam_rows * p["hprev"] + jnp.where(row < 64, sts[2 * pr], sts[2 * pr + 1])

    return pl.pallas_call(
        body, name="ssd_fwd", grid=(2, nc),
        in_specs=[pl.BlockSpec((L, 512), lambda g, c: (c, g)),
                  pl.BlockSpec((L, 128), lambda g, c: (c, g)),
                  pl.BlockSpec((L, 128), lambda g, c: (c, 2 + g)),
                  pl.BlockSpec((L, 512), lambda g, c: (c, g)),
                  pl.BlockSpec((L, 512), lambda g, c: (c, g)),
                  pl.BlockSpec((8, 1, L), lambda g, c: (g, 0, c))],
        out_specs=[pl.BlockSpec((L, 512), lambda g, c: (c, g)),
                   pl.BlockSpec((None, None, 4, 128, 128), lambda g, c: (g, c, 0, 0, 0))],
        out_shape=[jax.ShapeDtypeStruct((T, D), F32), jax.ShapeDtypeStruct((2, nc, 4, 128, 128), F32)],
        scratch_shapes=[pltpu.VMEM((4, 128, 128), F32)],
        compiler_params=_cparams(("parallel", "arbitrary")),
    )(xs, bc, bc, dt_x, acs_x, acs_t)


def ssd_bwd(xs, bc, dt_x, acs_x, acs_t, dy, hstates):
    T = xs.shape[0]
    nc, L = T // SSM_L, SSM_L

    def body(x_ref, b_ref, c_ref, dt_ref, ac_ref, at_ref, dy_ref, hst_ref,
             dx_ref, db_ref, dc_ref, dgate_ref, dar_ref, dh_scr):
        @pl.when(pl.program_id(1) == 0)
        def _():
            dh_scr[...] = jnp.zeros_like(dh_scr)

        bm, cm = b_ref[...], c_ref[...]
        cbm = mm_nt(cm, bm)
        row, col = _iota((L, L), 0), _iota((L, L), 1)
        rowc = _iota((L, 1), 0)
        incl = row >= col
        prs = range(4)
        P = []
        for pr in prs:
            sl = slice(128 * pr, 128 * pr + 128)
            x2, dt2, dy2, acs2 = x_ref[:, sl], dt_ref[:, sl], dy_ref[:, sl], ac_ref[:, sl]
            lam2, dec2, xd2 = _ssd_pair(x2, dt2, acs2)
            P.append(dict(sl=sl, x2=x2, dt2=dt2, dy2=dy2, acs2=acs2, lam2=lam2, dec2=dec2, xd2=xd2,
                          hprev=hst_ref[pr], dhn=dh_scr[pr], dz=lam2 * dy2))
        zs = [mm_nt(cm, p["hprev"]) for p in P]
        dcm_t = [mm(p["dz"], p["hprev"]) for p in P]
        dh_z = [mm_tn(p["dz"], cm) for p in P]
        H = []
        for pr in prs:
            p = P[pr]
            p["yoff"] = p["dz"] * zs[pr]
            p["q_rows"] = _lanes(p["dhn"] * p["hprev"])
            for hh in range(2):
                lmask, sg, mmat, dec_col, bd = _ssd_head(hh, p["acs2"], at_ref[2 * pr + hh], p["dec2"], cbm, bm, incl, col)
                H.append(dict(p=p, hh=hh, j=2 * pr + hh, lmask=lmask, sg=sg, mmat=mmat, dec_col=dec_col, bd=bd))
        dms = [mm_nt(jnp.where(h["lmask"], h["p"]["dy2"], 0.0), h["p"]["xd2"]) for h in H]
        a1s = [mm_tn(h["mmat"], h["p"]["dy2"]) for h in H]
        a2s = [mm_nt(h["bd"], h["p"]["dhn"]) for h in H]
        dbds = [mm(jnp.where(h["lmask"], h["p"]["xd2"], 0.0), h["p"]["dhn"]) for h in H]
        dcb = jnp.zeros((L, L), F32)
        dbm = jnp.zeros((L, SSM_N), F32)
        comp = jnp.zeros((L, 128), F32)
        dxd = [jnp.zeros((L, 128), F32) for _ in prs]
        for h, dm_raw, a1, a2, dbd in zip(H, dms, a1s, a2s, dbds):
            p, hh, j = h["p"], h["hh"], h["j"]
            dm = jnp.where(incl, dm_raw, 0.0)
            dcb = dcb + dm * h["sg"]
            e = dm * h["mmat"]
            dxd_h = jnp.where(h["lmask"], a1 + a2, 0.0)
            dxd[j // 2] = dxd[j // 2] + dxd_h
            dbm = dbm + h["dec_col"] * dbd
            t = _lanes(dbd * h["bd"])
            lam_h = p["lam2"][L - 1:L, 64 * hh:64 * hh + 1]
            in_head = (rowc >= 64 * hh) & (rowc < 64 * hh + 64)
            add_last = _sum_all(t) + _sum_all(jnp.where(in_head, p["q_rows"], 0.0)) * lam_h
            dacs_col = (_lanes(jnp.where(h["lmask"], p["yoff"], 0.0)) + _lanes(e) - t
                        + jnp.where(rowc == L - 1, add_last, 0.0))
            ddt_col = _lanes(dxd_h * p["x2"])
            dar_ref[j] = _rows(e)
            comp = comp + jnp.where(col == j, dacs_col, 0.0) + jnp.where(col == 8 + j, ddt_col, 0.0)
        dcm = dcm_t[0]
        for pr in prs:
            p = P[pr]
            if pr:
                dcm = dcm + dcm_t[pr]
            lam_rows = jnp.where(row < 64, p["lam2"][L - 1:L, 0:1], p["lam2"][L - 1:L, 64:65])
            dh_scr[pr] = dh_z[pr] + lam_rows * p["dhn"]
            dx_ref[:, p["sl"]] = p["dt2"] * dxd[pr]
        db_ref[...] = dbm + mm_tn(dcb, cm)
        dc_ref[...] = dcm + mm(dcb, bm)
        dgate_ref[...] = comp

    rv = lambda g, c: (nc - 1 - c, g)
    rowspec = pl.BlockSpec((8, 1, L), lambda g, c: (g, 0, nc - 1 - c))
    return pl.pallas_call(
        body, name="ssd_bwd", grid=(2, nc),
        in_specs=[pl.BlockSpec((L, 512), rv),
                  pl.BlockSpec((L, 128), rv),
                  pl.BlockSpec((L, 128), lambda g, c: (nc - 1 - c, 2 + g)),
                  pl.BlockSpec((L, 512), rv),
                  pl.BlockSpec((L, 512), rv),
                  rowspec,
                  pl.BlockSpec((L, 512), rv),
                  pl.BlockSpec((None, None, 4, 128, 128), lambda g, c: (g, nc - 1 - c, 0, 0, 0))],
        out_specs=[pl.BlockSpec((L, 512), rv), pl.BlockSpec((L, 128), rv), pl.BlockSpec((L, 128), rv),
                   pl.BlockSpec((L, 128), rv), rowspec],
        out_shape=[jax.ShapeDtypeStruct((T, D), F32), jax.ShapeDtypeStruct((T, 256), F32),
                   jax.ShapeDtypeStruct((T, 256), F32), jax.ShapeDtypeStruct((T, 256), F32),
                   jax.ShapeDtypeStruct((SSM_H, 1, T), F32)],
        scratch_shapes=[pltpu.VMEM((4, 128, 128), F32)],
        compiler_params=_cparams(("parallel", "arbitrary")),
    )(xs, bc, bc, dt_x, acs_x, acs_t, dy, hstates)


def _pos():
    return lax.axis_index("x"), lax.axis_index("y"), lax.axis_index("c")


def _other_chips(x, y):
    return [(1 - x, y), (x, 1 - y), (1 - x, 1 - y)]


def _rcopy(src, dst, ssem, rsem, dev):
    return pltpu.make_async_remote_copy(src_ref=src, dst_ref=dst, send_sem=ssem, recv_sem=rsem,
                                        device_id=dev, device_id_type=MESH)


def _rows_at(start, n):
    return pl.ds(pl.multiple_of(start, 8), n)


def _comm_call(body, name, out_shape, n_in, scratch):
    return pl.pallas_call(
        body, name=name, out_shape=out_shape, in_specs=[ANY] * n_in,
        out_specs=[ANY] * len(out_shape) if isinstance(out_shape, (list, tuple)) else ANY,
        scratch_shapes=scratch,
        compiler_params=pltpu.CompilerParams(has_side_effects=True),
    )


def _dma_sems(n):
    return pltpu.SemaphoreType.DMA((n,))


def ag_chips(name, shard):
    rr, cc = shard.shape
    h, nq = rr // 2, ICI_CHUNKS
    hq = h // nq

    def body(x_ref, out_ref, ssem, rsem):
        x, y, c = _pos()
        me_s = 2 * x + y
        chips = _other_chips(x, y)
        started = []
        for q in range(nq):
            rows = _rows_at(c * h + q * hq, hq)
            for j, (cx, cy) in enumerate(chips):
                cp = _rcopy(x_ref.at[rows], out_ref.at[me_s, rows], ssem.at[j * nq + q], rsem.at[j * nq + q], (cx, cy, c))
                cp.start()
                started.append(cp)
        for q in range(nq):
            rows = _rows_at(c * h + q * hq, hq)
            for j, (cx, cy) in enumerate(chips):
                blk = out_ref.at[2 * cx + cy, rows]
                _rcopy(blk, blk, ssem.at[j * nq + q], rsem.at[j * nq + q], (cx, cy, c)).wait_recv()
                k = 3 * nq + j * nq + q
                cp = _rcopy(blk, blk, ssem.at[k], rsem.at[k], (x, y, 1 - c))
                cp.start()
                started.append(cp)
        for q in range(nq):
            rows = _rows_at((1 - c) * h + q * hq, hq)
            for j, (cx, cy) in enumerate(chips):
                blk = out_ref.at[2 * cx + cy, rows]
                k = 3 * nq + j * nq + q
                _rcopy(blk, blk, ssem.at[k], rsem.at[k], (x, y, 1 - c)).wait_recv()
        for cp in started:
            cp.wait_send()

    return _comm_call(body, name, jax.ShapeDtypeStruct((4, rr, cc), shard.dtype), 1,
                      [_dma_sems(6 * nq), _dma_sems(6 * nq)])(shard)


def _with_own(shard, got, s_me):
    return lax.dynamic_update_index_in_dim(got, shard, s_me, 0)


def all_gather_chips(name, shard, s_me):
    return _with_own(shard, ag_chips(name, shard), s_me)


HBM_SPEC = pl.BlockSpec(memory_space=pltpu.HBM)
SEM_SPEC = pl.BlockSpec(memory_space=pltpu.SEMAPHORE)
SPLIT_EFFECT = pltpu.SideEffectType.DATAFLOW_SIDE_EFFECTING


def _split_copies(pieces, x_ref, land_ref, sems, arriving):
    x, y, c = _pos()
    return [_rcopy(s, d_in if arriving else d_out, sems[j], sems[3 + j], dev)
            for j, (s, d_out, d_in, dev) in enumerate(pieces(x_ref, land_ref, x, y, c))]


def split_copy_start(name, src, land_shape, pieces, after):
    def body(x_ref, land_ref, after_ref, *outs):
        for cp in _split_copies(pieces, x_ref, land_ref, outs[:6], False):
            cp.start()
        outs[8][...] = jnp.zeros_like(outs[8])

    dma = pltpu.SemaphoreType.DMA(())
    res = pl.pallas_call(
        body, name=name,
        out_shape=(dma,) * 6 + (pltpu.HBM(src.shape, src.dtype), pltpu.HBM(land_shape, src.dtype),
                                jax.ShapeDtypeStruct((8, 128), F32)),
        in_specs=(HBM_SPEC, HBM_SPEC, ANY),
        out_specs=(SEM_SPEC,) * 6 + (HBM_SPEC, HBM_SPEC, pl.BlockSpec(memory_space=pltpu.VMEM)),
        input_output_aliases={0: 6, 1: 7},
        compiler_params=pltpu.CompilerParams(has_side_effects=SPLIT_EFFECT),
    )(pltpu.with_memory_space_constraint(src, pltpu.HBM),
      pltpu.with_memory_space_constraint(lax.empty(land_shape, src.dtype), pltpu.HBM), after)
    return res[:6], res[6], res[7], res[8]


def split_copy_wait(name, sems, src_thru, land_thru, after, pieces):
    def body(x_ref, land_ref, *rest):
        for cp in _split_copies(pieces, x_ref, land_ref, rest[:6], False):
            cp.wait_send()
        for cp in _split_copies(pieces, x_ref, land_ref, rest[:6], True):
            cp.wait_recv()

    return pl.pallas_call(
        body, name=name,
        out_shape=(pltpu.HBM(src_thru.shape, src_thru.dtype), pltpu.HBM(land_thru.shape, land_thru.dtype)),
        in_specs=(HBM_SPEC, HBM_SPEC) + (SEM_SPEC,) * 6 + (ANY,), out_specs=(HBM_SPEC, HBM_SPEC),
        input_output_aliases={0: 0, 1: 1},
        compiler_params=pltpu.CompilerParams(has_side_effects=SPLIT_EFFECT),
    )(src_thru, land_thru, *sems, after)


def ag_pieces(h):
    def pieces(x_ref, land_ref, x, y, c):
        rows = _rows_at(c * h, h)
        return [(x_ref.at[rows], land_ref.at[2 * x + y, rows], land_ref.at[2 * cx + cy, rows], (cx, cy, c))
                for cx, cy in _other_chips(x, y)]
    return pieces


def rs_pieces(x_ref, land_ref, x, y, c):
    return [(x_ref.at[2 * cx + cy], land_ref.at[j], land_ref.at[j], (cx, cy, c))
            for j, (cx, cy) in enumerate(_other_chips(x, y))]


def ag_forward(name, got):
    _, rr, cc = got.shape
    h, nq = rr // 2, D2D_CHUNKS
    hq = h // nq

    def body(g_ref, out_ref, ssem, rsem):
        x, y, c = _pos()
        slots = [2 * cx + cy for cx, cy in _other_chips(x, y)]
        cps = []
        for j, s in enumerate(slots):
            for q in range(nq):
                blk = out_ref.at[s, _rows_at(c * h + q * hq, hq)]
                cp = _rcopy(blk, blk, ssem.at[j * nq + q], rsem.at[j * nq + q], (x, y, 1 - c))
                cp.start()
                cps.append(cp)
        for cp in cps:
            cp.wait_send()
        for j, s in enumerate(slots):
            for q in range(nq):
                blk = out_ref.at[s, _rows_at((1 - c) * h + q * hq, hq)]
                _rcopy(blk, blk, ssem.at[j * nq + q], rsem.at[j * nq + q], (x, y, 1 - c)).wait_recv()

    return pl.pallas_call(
        body, name=name, out_shape=jax.ShapeDtypeStruct(got.shape, got.dtype), in_specs=[ANY], out_specs=ANY,
        scratch_shapes=[_dma_sems(3 * nq), _dma_sems(3 * nq)], input_output_aliases={0: 0},
        compiler_params=pltpu.CompilerParams(has_side_effects=True),
    )(got)


def rs_pair(name, g):
    _, rr, cc = g.shape
    h, nq = rr // 2, D2D_CHUNKS
    hq = h // nq

    def body(g_ref, recv_ref, ssem, rsem):
        x, y, c = _pos()
        cps = []
        for q in range(nq):
            cp = _rcopy(g_ref.at[:, _rows_at((1 - c) * h + q * hq, hq), :], recv_ref.at[:, pl.ds(q * hq, hq), :],
                        ssem.at[q], rsem.at[q], (x, y, 1 - c))
            cp.start()
            cps.append(cp)
        for cp in cps:
            cp.wait()

    return _comm_call(body, name, jax.ShapeDtypeStruct((4, h, cc), g.dtype), 1, [_dma_sems(nq), _dma_sems(nq)])(g)


def rs_chips(name, p):
    _, h, cc = p.shape
    nq = ICI_CHUNKS
    hq = h // nq

    def body(p_ref, buf_ref, ssem, rsem):
        x, y, c = _pos()
        sends = []
        for q in range(nq):
            rows = pl.ds(q * hq, hq)
            for j, (cx, cy) in enumerate(_other_chips(x, y)):
                cp = _rcopy(p_ref.at[2 * cx + cy, rows], buf_ref.at[j, rows], ssem.at[j * nq + q],
                            rsem.at[j * nq + q], (cx, cy, c))
                cp.start()
                sends.append(cp)
        for cp in sends:
            cp.wait()

    return _comm_call(body, name, jax.ShapeDtypeStruct((3, h, cc), p.dtype), 1,
                      [_dma_sems(3 * nq), _dma_sems(3 * nq)])(p)


def rs_join(name, half):
    h, cc = half.shape
    nq = D2D_CHUNKS
    hq = h // nq

    def body(h_ref, out_ref, ssem, rsem):
        x, y, c = _pos()
        cps = []
        for q in range(nq):
            rows = pl.ds(q * hq, hq)
            cp = _rcopy(h_ref.at[rows], out_ref.at[rows], ssem.at[q], rsem.at[q], (x, y, 1 - c))
            cp.start()
            cps.append(cp)
        for cp in cps:
            cp.wait()

    return _comm_call(body, name, jax.ShapeDtypeStruct((h, cc), half.dtype), 1, [_dma_sems(nq), _dma_sems(nq)])(half)


def reduce_scatter(tag, g, tb, sp):
    return rs_end(rs_begin(tag, g, tb, sp, False), None)


def rs_begin(tag, g, tb, sp, split, after=None):
    _, rr, cc = g.shape
    h = rr // 2
    nbh = h // tb
    recv = rs_pair(tag + "_pair", g)
    mine_rows = lambda i, s: (i // nbh) * (2 * nbh) + s[0] * nbh + i % nbh
    part = rowwise(add2_fn, tag + "_add", 4 * h, tb, [R(g.reshape(4 * rr, cc), off=mine_rows), R(recv.reshape(4 * h, cc))],
                   [], [(cc, BF16)], sp=sp)[0].reshape(4, h, cc)
    st = dict(tag=tag, tb=tb, sp=sp, split=split, part=part)
    if split:
        st["sems"], st["part"], st["land"], st["token"] = split_copy_start(tag + "_start", part, (3, h, cc), rs_pieces,
                                                                           sp if after is None else after)
    return st


def rs_end(st, after):
    tag, tb, sp, part = st["tag"], st["tb"], st["sp"], st["part"]
    _, h, cc = part.shape
    nbh = h // tb
    if st["split"]:
        part, buf = split_copy_wait(tag + "_wait", st["sems"], part, st["land"], after, rs_pieces)
    else:
        buf = rs_chips(tag + "_chips", part)
    red = rowwise(sum4_fn, tag + "_sum", h, tb,
                  [R(part.reshape(4 * h, cc), off=lambda i, s: s[1] * nbh + i)]
                  + [R(buf.reshape(3 * h, cc), off=k * nbh) for k in range(3)],
                  [], [(cc, F32)], sp=sp)[0]
    return red, rs_join(tag + "_join", red)


def adam_halves(name, w, m, v, red, other, tb, blk0, sp):
    nbh = red.shape[0] // tb

    def fn(i, n, s, w_, m_, v_, r_, o_):
        g = jnp.where((blk0 + i) // nbh == s[0], r_, o_)
        return (g,) + _adamw(w_, g, m_, v_)

    half_rows = lambda i, s: (blk0 + i) % nbh
    return rowwise(fn, name, w.shape[0], tb, [R(w), R(m), R(v), R(red, off=half_rows), R(other, off=half_rows)],
                   [], [(w.shape[1], F32)] * 4, sp=sp)


SMALL_LANES = 3 * D


def all_reduce_items(name, items):
    flat = [a for it in items for a in it]
    shapes = [(sum(a.shape[0] for a in it), it[0].shape[1]) for it in items]
    nrows = -(-sum(s[0] for s in shapes) // 8) * 8

    def body(*refs):
        ins, outs = refs[:len(flat)], refs[len(flat):len(flat) + len(items)]
        mine, buf, ssem, rsem = refs[len(flat) + len(items):]
        x, y, c = _pos()
        me = 4 * x + 2 * y + c
        mine[...] = jnp.zeros_like(mine)
        r = 0
        for ref in ins:
            mine[r:r + ref.shape[0], 0:ref.shape[1]] = ref[...]
            r += ref.shape[0]
        buf[me] = mine[...]
        cps = []
        for k in range(1, 8):
            dev = (x ^ (k >> 2), y ^ ((k >> 1) & 1), c ^ (k & 1))
            cp = _rcopy(mine, buf.at[me], ssem.at[k - 1], rsem.at[k - 1], dev)
            cp.start()
            cps.append(cp)
        for cp in cps:
            cp.wait()
        r = 0
        for (nr, n), out in zip(shapes, outs):
            acc = buf[0, r:r + nr, 0:n]
            for d in range(1, 8):
                acc = acc + buf[d, r:r + nr, 0:n]
            out[...] = acc
            r += nr

    vm = pl.BlockSpec(memory_space=pltpu.VMEM)
    return pl.pallas_call(
        body, name=name, out_shape=[jax.ShapeDtypeStruct(s, F32) for s in shapes],
        in_specs=[vm] * len(flat), out_specs=[vm] * len(items),
        scratch_shapes=[pltpu.VMEM((nrows, SMALL_LANES), F32), pltpu.VMEM((8, nrows, SMALL_LANES), F32),
                        _dma_sems(7), _dma_sems(7)],
        compiler_params=pltpu.CompilerParams(has_side_effects=True),
    )(*flat)


def adam_small(ws, gs, ms, vs):
    n = len(ws)

    def body(*refs):
        for k in range(n):
            w, g, m, v = (refs[j * n + k][...] for j in range(4))
            for j, val in enumerate(_adamw(w, g, m, v)):
                refs[(4 + j) * n + k][...] = val

    vm = pl.BlockSpec(memory_space=pltpu.VMEM)
    res = pl.pallas_call(
        body, name="adam_small", out_shape=[jax.ShapeDtypeStruct(w.shape, F32) for w in ws] * 3,
        in_specs=[vm] * (4 * n), out_specs=[vm] * (3 * n),
    )(*ws, *gs, *ms, *vs)
    return res[:n], res[n:2 * n], res[2 * n:]


def _sel(rows, cols, pairs):
    m = np.zeros((rows, cols), np.float32)
    for r, c in pairs:
        m[r, c] = 1.0
    return jnp.asarray(m)


def _pad_win(w):
    z = jnp.zeros((w.shape[0], 112), w.dtype)
    return jnp.concatenate([w[:, :4096], w[:, 4112:6672], w[:, 4096:4112], z, w[:, 6672:6688], z], axis=1)


def _unpad_win(wp):
    return jnp.concatenate([wp[:, :4096], wp[:, 6656:6672], wp[:, 4096:6656], wp[:, 6784:6800]], axis=1)


def kernel(x, mem, norm1_w, w_in, gdn_conv_w, gdn_a_log, gdn_dt_bias, gdn_norm_w, ssm_conv_w, ssm_conv_b, ssm_a_log, ssm_dt_bias, ssm_d, ssm_norm_w, w_out, norm2_w, mem_norm_w, wq_mem, wk_mem, wv_mem, wo_mem, norm3_w, w_up, w_down, final_norm_w, loss_target, m_norm1_w, m_w_in, m_gdn_conv_w, m_gdn_a_log, m_gdn_dt_bias, m_gdn_norm_w, m_ssm_conv_w, m_ssm_conv_b, m_ssm_a_log, m_ssm_dt_bias, m_ssm_d, m_ssm_norm_w, m_w_out, m_norm2_w, m_mem_norm_w, m_wq_mem, m_wk_mem, m_wv_mem, m_wo_mem, m_norm3_w, m_w_up, m_w_down, m_final_norm_w, v_norm1_w, v_w_in, v_gdn_conv_w, v_gdn_a_log, v_gdn_dt_bias, v_gdn_norm_w, v_ssm_conv_w, v_ssm_conv_b, v_ssm_a_log, v_ssm_dt_bias, v_ssm_d, v_ssm_norm_w, v_w_out, v_norm2_w, v_mem_norm_w, v_wq_mem, v_wk_mem, v_wv_mem, v_wo_mem, v_norm3_w, v_w_up, v_w_down, v_final_norm_w):
    T, M = x.shape[1], mem.shape[1]
    xi, yi, ci = _pos()
    s_me = 2 * xi + yi
    x0, mem0, tgt = x[0], mem[0], loss_target[0]
    tb = min(256, T)
    tbp = min(256, T)
    row = lambda v: v.reshape(1, -1)

    win_g = all_gather_chips("ag_win", w_in.astype(BF16), s_me)
    w_in_p = _pad_win(win_g.transpose(1, 0, 2).reshape(D, IN_COLS))
    keep = (ci == 0).astype(F32)
    gcw_z = lax.dynamic_update_slice(jnp.zeros((4, 3 * D), F32), gdn_conv_w * keep, (0, s_me * 768))
    scw_z = lax.dynamic_update_slice(jnp.zeros((4, 1536), F32), ssm_conv_w * keep, (0, s_me * 384))
    gcw, scw = all_reduce_items("ar_convw", [[gcw_z], [scw_z]])
    scw_x, scw_bc = scw[:, :D], scw[:, D:]
    rest_shard = jnp.concatenate([w_up, w_down, w_out, wq_mem, wk_mem, wv_mem, wo_mem], axis=0).astype(BF16)
    ag_sems, rest_thru, rest_land, ag_token = split_copy_start("ag_rest_start", rest_shard, (4,) + rest_shard.shape,
                                                               ag_pieces(rest_shard.shape[0] // 2), gcw)
    sp = jnp.stack([ci, s_me]).astype(jnp.int32)
    scb_x, scb_bc = row(ssm_conv_b[:D]), row(ssm_conv_b[D:])

    galog_c, gdtb_c = row(jnp.pad(gdn_a_log, (8, 112))), row(jnp.pad(gdn_dt_bias, (8, 112)))
    salog_c, sdtb_c = row(jnp.pad(ssm_a_log, (0, 112))), row(jnp.pad(ssm_dt_bias, (0, 112)))
    sd_x = row(jnp.repeat(ssm_d, 64))
    eb = _sel(128, D, [(h, 128 * h + l) for h in range(8) for l in range(128)])
    ea = _sel(128, D, [(8 + h, 128 * h + l) for h in range(8) for l in range(128)])
    e16 = _sel(128, D, [(h, 64 * h + l) for h in range(16) for l in range(64)])
    pb = _sel(D, 128, [(128 * h, h) for h in range(8)])
    pa = _sel(D, 128, [(128 * h, 8 + h) for h in range(8)])

    h1 = rowwise(rms_fwd_fn, "rms1", T, tb, [R(x0)], [row(norm1_w) + ag_token[0:1, 0:1]], [(D, BF16)])[0]
    p = matmul("mm_in", h1, w_in_p, "nn", 2048, 768, 1024, [F32])[0]
    gp_ins = [R(p, 3 * D, CB_QKV, "prev"), R(p, 128, CB_BA)]
    qn, kn, vv, gcs_x, beta_x, gcs_t = rowwise(gdn_prep_fn, "gdn_prep", T, tbp, gp_ins,
                                               [gcw, galog_c, gdtb_c, eb, ea], [(D, F32)] * 5 + [(-8, F32)])
    gcs_t = gcs_t.reshape(GDN_H, 1, T)
    gtb, ggh = min(128, T), 8
    o_gdn, s_states, tinv = gdn_fwd(qn, kn, vv, gcs_x, beta_x, gcs_t, gtb, ggh)
    gnw = row(gdn_norm_w)
    oa = rowwise(gdn_post_fn, "gdn_post", T, tb, [R(o_gdn), R(p, D, CB_Z)], [gnw], [(D, BF16)])[0]
    sp_ins = [R(p, D, CB_XS, "prev"), R(p, 512, CB_BC, "prev"), R(p, 128, CB_DT)]
    sp_full = [scw_x, scw_bc, scb_x, scb_bc, salog_c, sdtb_c]
    xs, bc, dt_x, acs_x, acs_t = rowwise(ssd_prep_fn, "ssd_prep", T, tbp, sp_ins, sp_full + [e16],
                                         [(D, F32), (512, F32), (D, F32), (D, F32), (-SSM_H, F32)])
    acs_t = acs_t.reshape(SSM_H, 1, T)
    y_ssd, h_states = ssd_fwd(xs, bc, dt_x, acs_x, acs_t)
    snw = row(ssm_norm_w)
    ob = rowwise(ssd_post_fn, "ssd_post", T, tb, [R(y_ssd), R(xs), R(p, D, CB_ZS)], [sd_x, snw], [(D, BF16)])[0]
    rest_thru, rest_land = split_copy_wait("ag_rest_wait", ag_sems, rest_thru, rest_land, ob,
                                           ag_pieces(rest_shard.shape[0] // 2))
    rest_g = _with_own(rest_thru, ag_forward("ag_rest_fwd", rest_land), s_me)
    wup_f = rest_g[:, 0:1024].transpose(1, 0, 2).reshape(D, D_FF)
    wdown_f = rest_g[:, 1024:2048].reshape(D_FF, D)
    wout_f = rest_g[:, 2048:2560].reshape(2, D, D)
    wq_f, wk_f, wv_f, wo_f = (rest_g[:, 2560 + 256 * k:2816 + 256 * k].reshape(D, D) for k in range(4))
    x1a = matmul("mm_out_a", oa, wout_f, "nn", 1024, 1024, 1024, [F32], _epi_res, [x0], b_sel=0)[0]
    assert D == 1024
    x1, h2 = matmul("mm_out_b", ob, wout_f, "nn", 1024, 1024, 1024, [F32, BF16], _epi_res_rms, [x1a],
                    [row(norm2_w)], b_sel=1)

    mn = rowwise(rms_fwd_fn, "rms_mem", M, M, [R(mem0)], [row(mem_norm_w)], [(D, BF16)])[0]
    km = matmul("mm_k", mn, wk_f, "nn", 256, 1024, 1024, [BF16])[0]
    vm = matmul("mm_v", mn, wv_f, "nn", 256, 1024, 1024, [BF16])[0]
    qm = matmul("mm_q", h2, wq_f, "nn", 1024, 1024, 1024, [BF16])[0]
    ao = rowwise(attn_fn, "attn", T, tb, [R(qm)], [km, vm], [(D, BF16)])[0]
    x2, h3 = matmul("mm_o", ao, wo_f, "nn", 1024, 1024, 1024, [F32, BF16], _epi_res_rms, [x1], [row(norm3_w)])
    u, act = matmul("mm_up", h3, wup_f, "nn", 2048, 1024, 1024, [BF16, BF16], _epi_relu2)
    x3 = matmul("mm_down", act, wdown_f, "nn", 1024, 1024, 1024, [F32], _epi_res, [x2])[0]
    dx3, dx3b, loss_lane, g_final = rowwise(final_fn, "final", T, tb, [R(x3), R(tgt)], [row(final_norm_w)],
                                            [(D, F32), (D, BF16)], [(1, D), (1, D)])
    loss = lax.psum(0.5 / D * jnp.sum(loss_lane), ("x", "y", "c"))

    dup = matmul("mm_dact", dx3b, wdown_f, "nt", 2048, 1024, 1024, [BF16], _epi_dup, [u])[0]
    def g_into(buf, blk, at):
        return dict(into=(buf, blk, lambda i, j, k, at=at: at(i, j)))

    grest = jax.ShapeDtypeStruct((4, 3584, D), F32)
    grest = matmul("mm_gdown", act, dx3b, "tn", 1024, 1024, 2048, [F32],
                   **g_into(grest, (None, 1024, D), lambda i, j: (i, 1, 0)))
    dh3 = matmul("mm_dh3", dup, wup_f, "nt", 2048, 1024, 1024, [F32])[0]
    dx2, dx2b, g_n3 = rowwise(rms_bwd_fn, "rms3_bwd", T, tb, [R(x2), R(dh3), R(dx3)], [row(norm3_w)],
                              [(D, F32), (D, BF16)], [(1, D)])
    grest = matmul("mm_gup", h3, dup, "tn", 1024, 1024, 2048, [F32],
                   **g_into(grest, (None, 1024, D), lambda i, j: (j, 0, 0)))
    dao = matmul("mm_dao", dx2b, wo_f, "nt", 1024, 1024, 1024, [F32])[0]
    grest = matmul("mm_gwo", ao, dx2b, "tn", 1024, 1024, 2048, [F32],
                   **g_into(grest, (4, 256, D), lambda i, j: (0, 13, 0)))
    dqm, dkm, dvm = rowwise(attn_bwd_fn, "attn_bwd", T, tb, [R(qm), R(dao)], [km, vm], [(D, BF16)],
                            [(M, D), (M, D)])
    dx1, dx1b, g_n2 = matmul("mm_dh2", dqm, wq_f, "nt", 512, 1024, 1024, [F32, BF16], _epi_rms_bwd, [x1, dx2],
                             [row(norm2_w)], n_acc=1)
    grest = matmul("mm_gwq", h2, dqm, "tn", 1024, 1024, 2048, [F32],
                   **g_into(grest, (4, 256, D), lambda i, j: (0, 10, 0)))
    grest = matmul("mm_gwk", mn, dkm, "tn", 1024, 1024, 256, [F32],
                   **g_into(grest, (4, 256, D), lambda i, j: (0, 11, 0)))
    grest = matmul("mm_gwv", mn, dvm, "tn", 1024, 1024, 256, [F32],
                   **g_into(grest, (4, 256, D), lambda i, j: (0, 12, 0)))
    dmn_k = matmul("mm_dmk", dkm, wk_f, "nt", 256, 1024, 1024, [F32])[0]
    dmn = matmul("mm_dmv", dvm, wv_f, "nt", 256, 1024, 1024, [F32], _epi_res, [dmn_k])[0]
    g_nmem = rowwise(rms_bwd_w_fn, "rmsmem_bwd", M, M, [R(mem0), R(dmn)], [row(mem_norm_w)], [], [(1, D)])[0]
    doa = matmul("mm_doa", dx1b, wout_f, "nt", 2048, 1024, 1024, [F32], b_sel=0)[0]
    dob = matmul("mm_dob", dx1b, wout_f, "nt", 2048, 1024, 1024, [F32], b_sel=1)[0]
    grest = matmul("mm_gwout_a", oa, dx1b, "tn", 1024, 1024, 2048, [F32],
                   **g_into(grest, (2, 512, D), lambda i, j: (0, 4, 0)))
    grest = matmul("mm_gwout_b", ob, dx1b, "tn", 1024, 1024, 2048, [F32],
                   **g_into(grest, (2, 512, D), lambda i, j: (1, 4, 0)))

    rs_rest = rs_begin("rs_rest", grest, 256, sp, True)

    dp = jax.ShapeDtypeStruct((T, p.shape[1]), BF16)
    dy_ssd, dxs_dir, dp, g_snw, g_sd_lane = rowwise(
        ssd_post_bwd_fn, "ssd_post_bwd", T, tb, [R(y_ssd), R(xs), R(p, D, CB_ZS), R(dob)],
        [sd_x + rs_rest["token"][0:1, 0:1], snw],
        [(D, F32), (D, F32), (D, BF16, dp, CB_ZS)], [(1, D), (1, D)])
    dxs_scan, db_s, dc_s, dgate, dacs_t = ssd_bwd(xs, bc, dt_x, acs_x, acs_t, dy_ssd, h_states)
    spb = rowwise(ssd_prep_bwd_fn, "ssd_prep_bwd", T, tbp,
                  sp_ins + [R(dxs_scan), R(dxs_dir), R(db_s), R(dc_s), R(dgate), RC(dacs_t.reshape(SSM_H, T))], sp_full,
                  [(D, F32), (512, F32), (128, BF16, dp, CB_DT)],
                  [(1, D)] * 4 + [(1, 512)] * 4 + [(1, D), (1, 512), (1, 128), (1, 128)])
    dyc_x, dyc_bc, dp = spb[:3]
    dp = rowwise(conv_bwd_fn, "conv_bwd_x", T, tbp, [R(dyc_x, halo="next")], [scw_x], [(D, BF16, dp, CB_XS)])[0]
    dp = rowwise(conv_bwd_fn, "conv_bwd_bc", T, tbp, [R(dyc_bc, halo="next")], [scw_bc], [(512, BF16, dp, CB_BC)])[0]

    do_gdn, dp, g_gnw = rowwise(gdn_post_bwd_fn, "gdn_post_bwd", T, tb, [R(o_gdn), R(p, D, CB_Z), R(doa)], [gnw],
                                [(D, F32), (D, BF16, dp, CB_Z)], [(1, 128)])
    dqn, dkn, dvv, dgcs_x, dbeta_x, dgcs_t = gdn_bwd(qn, kn, vv, gcs_x, beta_x, gcs_t, do_gdn, s_states, tinv, gtb, ggh)
    gpb = rowwise(gdn_prep_bwd_fn, "gdn_prep_bwd", T, tbp,
                  gp_ins + [R(dqn), R(dkn), R(dvv), R(dgcs_x), R(dbeta_x), RC(dgcs_t.reshape(GDN_H, T))],
                  [gcw, galog_c, gdtb_c, pb, pa],
                  [(3 * D, F32), (128, BF16, dp, CB_BA)], [(1, 3 * D)] * 4 + [(1, 128), (1, 128)])
    dyc_qkv, dp = gpb[:2]
    dp = rowwise(conv_bwd_fn, "conv_bwd_qkv", T, tbp, [R(dyc_qkv, halo="next")], [gcw], [(3 * D, BF16, dp, CB_QKV)])[0]
    dh1 = matmul("mm_dh1", dp, w_in_p, "nt", 2048, 1024, 768, [F32])[0]
    grad_x, g_n1 = rowwise(rms_bwd1_fn, "rms1_bwd", T, tb, [R(x0), R(dh1), R(dx1)], [row(norm1_w)], [(D, F32)], [(1, D)])
    g_win_p = matmul("mm_gwin", h1, dp, "tn", 1024, 768, 2048, [F32])[0]

    items = [[g_n1], [gpb[6]], [gpb[7]], [g_gnw], [spb[11]], [spb[12]], [spb[13]], [spb[14]], [g_sd_lane], [g_snw],
             [g_n2], [g_nmem], [g_n3], [g_final], list(gpb[2:6]), list(spb[3:7]), list(spb[7:11])]
    (gr_n1, r_galog, r_gdtb, gr_gnw, r_scb_x, r_scb_bc, r_salog, r_sdtb, r_sd, gr_snw, gr_n2, gr_nmem, gr_n3,
     gr_final, r_gcw, r_scw_x, r_scw_bc) = all_reduce_items("ar_grads", items)
    gr_galog, gr_gdtb = r_galog[:, 8:16], r_gdtb[:, 8:16]
    gr_salog, gr_sdtb = r_salog[:, :SSM_H], r_sdtb[:, :SSM_H]
    gr_sd = r_sd.reshape(SSM_H, SSM_P).sum(axis=1).reshape(1, SSM_H)
    gr_scb = jnp.concatenate([r_scb_x, r_scb_bc], axis=1)
    gr_gcw = lax.dynamic_slice(r_gcw, (0, s_me * 768), (4, 768))
    gr_scw = lax.dynamic_slice(jnp.concatenate([r_scw_x, r_scw_bc], axis=1), (0, s_me * 384), (4, 384))

    g_win = _unpad_win(g_win_p).reshape(D, 4, IN_COLS // 4).transpose(1, 0, 2)
    rs_win = rs_begin("rs_win", g_win, 256, sp, True, gr_n1)
    red_r, oth_r = rs_end(rs_rest, rs_win["token"])

    big = {}
    for n, w, m, v, blk0 in (("w_up", w_up, m_w_up, v_w_up, 0), ("w_down", w_down, m_w_down, v_w_down, 4),
                             ("w_out", w_out, m_w_out, v_w_out, 8), ("wq_mem", wq_mem, m_wq_mem, v_wq_mem, 10),
                             ("wk_mem", wk_mem, m_wk_mem, v_wk_mem, 11), ("wv_mem", wv_mem, m_wv_mem, v_wv_mem, 12),
                             ("wo_mem", wo_mem, m_wo_mem, v_wo_mem, 13)):
        big[n] = adam_halves("adam_" + n, w, m, v, red_r, oth_r, 256, blk0, sp)
    red_w, oth_w = rs_end(rs_win, big["wo_mem"][1])
    big["w_in"] = adam_halves("adam_win", w_in, m_w_in, v_w_in, red_w, oth_w, 256, 0, sp)
    names_s =["norm1_w", "gdn_conv_w", "gdn_a_log", "gdn_dt_bias", "gdn_norm_w", "ssm_conv_w", "ssm_conv_b",
               "ssm_a_log", "ssm_dt_bias", "ssm_d", "ssm_norm_w", "norm2_w", "mem_norm_w", "norm3_w", "final_norm_w"]
    w_s = [norm1_w, gdn_conv_w, gdn_a_log, gdn_dt_bias, gdn_norm_w, ssm_conv_w, ssm_conv_b, ssm_a_log, ssm_dt_bias,
           ssm_d, ssm_norm_w, norm2_w, mem_norm_w, norm3_w, final_norm_w]
    g_s = [gr_n1, gr_gcw, gr_galog, gr_gdtb, gr_gnw, gr_scw, gr_scb, gr_salog, gr_sdtb, gr_sd, gr_snw, gr_n2,
           gr_nmem, gr_n3, gr_final]
    m_s = [m_norm1_w, m_gdn_conv_w, m_gdn_a_log, m_gdn_dt_bias, m_gdn_norm_w, m_ssm_conv_w, m_ssm_conv_b, m_ssm_a_log,
           m_ssm_dt_bias, m_ssm_d, m_ssm_norm_w, m_norm2_w, m_mem_norm_w, m_norm3_w, m_final_norm_w]
    v_s = [v_norm1_w, v_gdn_conv_w, v_gdn_a_log, v_gdn_dt_bias, v_gdn_norm_w, v_ssm_conv_w, v_ssm_conv_b, v_ssm_a_log,
           v_ssm_dt_bias, v_ssm_d, v_ssm_norm_w, v_norm2_w, v_mem_norm_w, v_norm3_w, v_final_norm_w]
    shp_s = [w.shape for w in w_s]
    as2d = lambda a: a if a.ndim == 2 else a.reshape(1, -1)
    d_l, m_l, v_l = adam_small([as2d(a) for a in w_s], [as2d(a) for a in g_s], [as2d(a) for a in m_s],
                               [as2d(a) for a in v_s])

    grads, deltas, new_m, new_v = {}, {}, {}, {}
    for n, (gg, dd, mm_, vv_) in big.items():
        grads[n], deltas[n], new_m[n], new_v[n] = gg, dd, mm_, vv_
    for k, n in enumerate(names_s):
        grads[n] = g_s[k].reshape(shp_s[k])
        deltas[n], new_m[n], new_v[n] = (a[k].reshape(shp_s[k]) for a in (d_l, m_l, v_l))
    order = ["norm1_w", "w_in", "gdn_conv_w", "gdn_a_log", "gdn_dt_bias", "gdn_norm_w", "ssm_conv_w", "ssm_conv_b",
             "ssm_a_log", "ssm_dt_bias", "ssm_d", "ssm_norm_w", "w_out", "norm2_w", "mem_norm_w", "wq_mem", "wk_mem",
             "wv_mem", "wo_mem", "norm3_w", "w_up", "w_down", "final_norm_w"]
    return (loss, grad_x[None], *[grads[n] for n in order], *[deltas[n] for n in order],
            *[new_m[n] for n in order], *[new_v[n] for n in order])
```

```python
import numpy as np
import jax
import jax.numpy as jnp
from jax import lax
from jax.experimental import pallas as pl
from jax.experimental.pallas import tpu as pltpu

F32, BF16 = jnp.float32, jnp.bfloat16
MESH = pl.DeviceIdType.MESH
ANY = pl.BlockSpec(memory_space=pl.ANY)

EPS = 1e-6
D = 1024
GDN_H, GDN_DK, GDN_C = 8, 128, 64
SSM_H, SSM_P, SSM_N, SSM_L = 16, 64, 128, 128
MEM_H, MEM_DH = 4, 256
D_FF = 4096
IN_COLS = 6688
CB_QKV, CB_Z, CB_ZS, CB_XS, CB_BC, CB_BA, CB_DT = 0, 3, 4, 5, 12, 52, 53
VMEM_LIMIT = 56 * 1024 * 1024
D2D_CHUNKS = 8
ICI_CHUNKS = 4

ADAM_LR, ADAM_B1, ADAM_B2, ADAM_EPS, ADAM_WD, ADAM_STEP = 0.001, 0.9, 0.999, 1e-08, 0.01, 10


def _dg(a, b, ca, cb):
    return lax.dot_general(a, b, (((ca,), (cb,)), ((), ())), preferred_element_type=F32)


def _bf(x):
    return x.astype(BF16)


def mm(a, b):
    return _dg(_bf(a), _bf(b), 1, 0)


def mm_nt(a, b):
    return _dg(_bf(a), _bf(b), 1, 1)


def mm_tn(a, b):
    return _dg(_bf(a), _bf(b), 0, 0)


def mm_sel(a, sel):
    hi = a.astype(BF16)
    r1 = a - hi.astype(F32)
    mid = r1.astype(BF16)
    lo = (r1 - mid.astype(F32)).astype(BF16)
    s = sel.astype(BF16)
    return _dg(hi, s, 1, 0) + (_dg(mid, s, 1, 0) + _dg(lo, s, 1, 0))


def mm3(a, b):
    ah, bh = a.astype(BF16), b.astype(BF16)
    al, bl = (a - ah.astype(F32)).astype(BF16), (b - bh.astype(F32)).astype(BF16)
    return _dg(ah, bh, 1, 0) + (_dg(ah, bl, 1, 0) + _dg(al, bh, 1, 0))


def _iota(shape, dim):
    return lax.broadcasted_iota(jnp.int32, shape, dim)


def _chunk_cumsum(x, c):
    pos = _iota(x.shape, 0) & (c - 1)
    s = 1
    while s < c:
        x = x + jnp.where(pos >= s, pltpu.roll(x, s, 0), 0.0)
        s *= 2
    return x


def _chunk_revcumsum(x, c):
    n = x.shape[0]
    pos = _iota(x.shape, 0) & (c - 1)
    s = 1
    while s < c:
        x = x + jnp.where(pos < c - s, pltpu.roll(x, n - s, 0), 0.0)
        s *= 2
    return x


def _sig(x):
    return jax.nn.sigmoid(x)


def _softplus(x):
    return jnp.maximum(x, 0.0) + jnp.log(1.0 + jnp.exp(-jnp.abs(x)))


def _rows(v):
    return jnp.sum(v, axis=0, keepdims=True)


def _lanes(v):
    return jnp.sum(v, axis=1, keepdims=True)


def _sum_all(v):
    return _rows(_lanes(v))


def _cparams(sem):
    return pltpu.CompilerParams(dimension_semantics=sem, vmem_limit_bytes=VMEM_LIMIT)


def rowwise(fn, name, T, tb, row_ins, full_ins, row_outs, acc_outs=(), sp=None):
    nblk = T // tb
    assert nblk * tb == T
    has_sp = sp is not None

    def imap(f):
        return (lambda i, s: f(i, s)) if has_sp else (lambda i: f(i, None))

    in_specs, args = [], []
    for arr, w, cb, halo, off in row_ins:
        if halo == "col":
            in_specs.append(pl.BlockSpec((w, tb), imap(lambda i, s: (0, i))))
            args.append(arr)
            continue
        rowf = off if callable(off) else (lambda i, s, off=off: i + off)
        in_specs.append(pl.BlockSpec((tb, w), imap(lambda i, s, cb=cb, rowf=rowf: (rowf(i, s), cb))))
        args.append(arr)
        if halo == "prev":
            r = tb // 8
            in_specs.append(pl.BlockSpec((8, w), imap(lambda i, s, cb=cb, r=r: (jnp.maximum(i * r - 1, 0), cb))))
            args.append(arr)
        elif halo == "next":
            r, last = tb // 8, T // 8 - 1
            in_specs.append(pl.BlockSpec((8, w), imap(lambda i, s, cb=cb, r=r, last=last:
                                                      (jnp.minimum((i + 1) * r, last), cb))))
            args.append(arr)
    for arr in full_ins:
        in_specs.append(pl.BlockSpec(arr.shape, imap(lambda i, s, nd=arr.ndim: (0,) * nd)))
        args.append(arr)
    n_in, n_ro = len(args), len(row_outs)
    out_shape, out_specs, aliases = [], [], {}
    for k, (w, dt, *dest) in enumerate(row_outs):
        if dest:
            buf, cb = dest
            out_shape.append(jax.ShapeDtypeStruct(buf.shape, buf.dtype))
            out_specs.append(pl.BlockSpec((tb, w), imap(lambda i, s, cb=cb: (i, cb))))
            if not isinstance(buf, jax.ShapeDtypeStruct):
                aliases[len(args) + int(has_sp)] = k
                in_specs.append(ANY)
                args.append(buf)
        elif w < 0:
            out_shape.append(jax.ShapeDtypeStruct((-w, T), dt))
            out_specs.append(pl.BlockSpec((-w, tb), imap(lambda i, s: (0, i))))
        else:
            out_shape.append(jax.ShapeDtypeStruct((T, w), dt))
            out_specs.append(pl.BlockSpec((tb, w), imap(lambda i, s: (i, 0))))
    for shp in acc_outs:
        out_shape.append(jax.ShapeDtypeStruct(shp, F32))
        out_specs.append(pl.BlockSpec(shp, imap(lambda i, s, nd=len(shp): (0,) * nd)))

    def body(*refs):
        i = pl.program_id(0)
        if has_sp:
            sp_ref, refs = refs[0], refs[1:]
            vals = fn(i, nblk, sp_ref, *[r[...] for r in refs[:n_in]])
        else:
            vals = fn(i, nblk, *[r[...] for r in refs[:n_in]])
        outs = refs[n_in + len(aliases):]
        for ref, val in zip(outs[:n_ro], vals[:n_ro]):
            ref[...] = val.astype(ref.dtype)
        for ref, val in zip(outs[n_ro:], vals[n_ro:]):
            @pl.when(i == 0)
            def _(ref=ref, val=val):
                ref[...] = val

            @pl.when(i > 0)
            def _(ref=ref, val=val):
                ref[...] += val

    cparams = _cparams(("arbitrary",) if acc_outs else ("parallel",))
    if has_sp:
        return pl.pallas_call(
            body, name=name, out_shape=out_shape, compiler_params=cparams, input_output_aliases=aliases,
            grid_spec=pltpu.PrefetchScalarGridSpec(num_scalar_prefetch=1, grid=(nblk,), in_specs=in_specs,
                                                   out_specs=out_specs),
        )(sp, *args)
    return pl.pallas_call(
        body, name=name, grid=(nblk,), in_specs=in_specs, out_specs=out_specs, out_shape=out_shape,
        compiler_params=cparams, input_output_aliases=aliases,
    )(*args)


def R(arr, w=None, cb=0, halo=None, off=0):
    return (arr, arr.shape[1] if w is None else w, cb, halo, off)


def RC(arr):
    return (arr, arr.shape[0], 0, "col", 0)


def matmul(name, a, b, form, tm, tn, tk, out_dtypes, epi=None, extras=(), rows=(), into=None, n_acc=0, b_sel=None):
    bs = b.shape if b_sel is None else b_sel[0]
    if form == "nn":
        (M, K), N = a.shape, bs[1]
    elif form == "nt":
        (M, K), N = a.shape, bs[0]
    else:
        (K, M), N = a.shape, bs[1]
    tm, tn, tk = min(tm, M), min(tn, N), min(tk, K)
    assert M % tm == 0 and N % tn == 0 and K % tk == 0, (name, M, N, K, tm, tn, tk)

    def b_spec_of(blk, at):
        if b_sel is None:
            return pl.BlockSpec(blk, lambda i, j, k: at(i, j, k))
        blk3 = b_sel[1]
        assert int(np.prod([d for d in blk3 if d is not None])) == blk[0] * blk[1], (name, blk3, blk)
        return pl.BlockSpec(blk3, lambda i, j, k: b_sel[2](i, j, k))

    if form == "nn":
        a_spec = pl.BlockSpec((tm, tk), lambda i, j, k: (i, k))
        b_spec = b_spec_of((tk, tn), lambda i, j, k: (k, j))
        ca, cb = 1, 0
    elif form == "nt":
        a_spec = pl.BlockSpec((tm, tk), lambda i, j, k: (i, k))
        b_spec = b_spec_of((tn, tk), lambda i, j, k: (j, k))
        ca, cb = 1, 1
    else:
        a_spec = pl.BlockSpec((tk, tm), lambda i, j, k: (k, i))
        b_spec = b_spec_of((tk, tn), lambda i, j, k: (k, j))
        ca, cb = 0, 0
    nk, ne, no = K // tk, len(extras) + len(rows), len(out_dtypes)
    if epi is None:
        epi = lambda acc: (acc,)

    assert n_acc == 0 or tn == N

    def body(a_ref, b_ref, *rest):
        ex, outs, accs, acc = rest[:ne], rest[ne:ne + no], rest[ne + no:ne + no + n_acc], rest[ne + no + n_acc]
        i, k = pl.program_id(0), pl.program_id(2)

        def finish(total):
            vals = epi(total, *[e[...] for e in ex])
            for r, v in zip(outs, vals[:no]):
                r[...] = v.astype(r.dtype).reshape(r.shape)
            for r, v in zip(accs, vals[no:]):
                @pl.when(i == 0)
                def _(r=r, v=v):
                    r[...] = v

                @pl.when(i > 0)
                def _(r=r, v=v):
                    r[...] += v

        b_tile = b_ref[...]
        prod = _dg(_bf(a_ref[...]), _bf(b_tile.reshape(-1, b_tile.shape[-1])), ca, cb)
        if nk == 1:
            finish(prod)
            return

        @pl.when(k == 0)
        def _():
            acc[...] = prod

        @pl.when(k > 0)
        def _():
            acc[...] += prod

        @pl.when(k == nk - 1)
        def _():
            finish(acc[...])

    mn = pl.BlockSpec((tm, tn), lambda i, j, k: (i, j))
    rw = pl.BlockSpec((1, tn), lambda i, j, k: (0, j))
    acc_scratch = pltpu.VMEM((tm, tn) if nk > 1 else (8, 128), F32)
    if into is not None:
        buf, blk, bmap = into
        assert ne == 0 and no == 1
        aliased = not isinstance(buf, jax.ShapeDtypeStruct)

        def body_into(a_ref, b_ref, *rest):
            body(a_ref, b_ref, *rest[-2:])

        return pl.pallas_call(
            body_into, name=name, grid=(M // tm, N // tn, nk),
            in_specs=[a_spec, b_spec] + ([ANY] if aliased else []), out_specs=pl.BlockSpec(blk, bmap),
            out_shape=jax.ShapeDtypeStruct(buf.shape, buf.dtype),
            scratch_shapes=[acc_scratch],
            input_output_aliases={2: 0} if aliased else {},
            compiler_params=_cparams(("parallel", "parallel", "arbitrary")),
        )(a, b, *([buf] if aliased else []))
    return pl.pallas_call(
        body, name=name, grid=(M // tm, N // tn, nk),
        in_specs=[a_spec, b_spec] + [mn] * len(extras) + [rw] * len(rows), out_specs=[mn] * no + [rw] * n_acc,
        out_shape=[jax.ShapeDtypeStruct((M, N), dt) for dt in out_dtypes] + [jax.ShapeDtypeStruct((1, N), F32)] * n_acc,
        scratch_shapes=[acc_scratch],
        compiler_params=_cparams(("arbitrary",) * 3 if n_acc else ("parallel", "parallel", "arbitrary")),
    )(a, b, *extras, *rows)


def _epi_res(acc, res):
    return (res + acc,)


def _epi_rms_bwd(acc, x, dres, w):
    return rms_bwd_fn(0, 0, x, acc, dres, w)


def rms_bwd1_fn(i, n, x, dh, dres, w):
    dx, _, gw = rms_bwd_fn(i, n, x, dh, dres, w)
    return dx, gw


def _epi_final(acc, res, tgt, w):
    return final_fn(0, 0, res + acc, tgt, w)


def _epi_res_rms(acc, res, w):
    x = res + acc
    return (x, x * lax.rsqrt(jnp.mean(x * x, axis=-1, keepdims=True) + EPS) * w)


def _epi_relu2(acc):
    u = jnp.maximum(acc, 0.0)
    return (u, u * u)


def _epi_dup(acc, u):
    return (acc * 2.0 * u.astype(F32),)


def _conv(x, halo, w, i):
    halo = jnp.where(i == 0, 0.0, halo)
    xt = jnp.concatenate([halo, x], axis=0)
    shifted = [pltpu.roll(xt, 3 - k, 0)[8:, :] for k in range(3)] + [x]
    y = shifted[3] * w[3:4, :]
    for k in range(3):
        y = y + shifted[k] * w[k:k + 1, :]
    return y, shifted


def _l2n(x, scale):
    outs = []
    for h in range(x.shape[1] // 128):
        xh = x[:, 128 * h:128 * h + 128]
        outs.append(xh * (lax.rsqrt(jnp.sum(xh * xh, axis=-1, keepdims=True) + EPS) * scale))
    return jnp.concatenate(outs, axis=1)


def _l2n_bwd(x, dy, scale):
    outs = []
    for h in range(x.shape[1] // 128):
        xh, dh = x[:, 128 * h:128 * h + 128], dy[:, 128 * h:128 * h + 128] * scale
        r = lax.rsqrt(jnp.sum(xh * xh, axis=-1, keepdims=True) + EPS)
        outs.append(r * dh - xh * (r * r * r) * jnp.sum(xh * dh, axis=-1, keepdims=True))
    return jnp.concatenate(outs, axis=1)


def rms_fwd_fn(i, n, x, w):
    r = lax.rsqrt(jnp.mean(x * x, axis=-1, keepdims=True) + EPS)
    return (x * r * w,)


def rms_bwd_fn(i, n, x, dh, dres, w):
    r = lax.rsqrt(jnp.mean(x * x, axis=-1, keepdims=True) + EPS)
    g = dh * w
    dx = dres + r * g - x * (r * r * r) * jnp.mean(x * g, axis=-1, keepdims=True)
    return dx, dx, _rows(dh * x * r)


def rms_bwd_w_fn(i, n, x, dh, w):
    r = lax.rsqrt(jnp.mean(x * x, axis=-1, keepdims=True) + EPS)
    return (_rows(dh * x * r),)


def final_fn(i, n, x, tgt, w):
    r = lax.rsqrt(jnp.mean(x * x, axis=-1, keepdims=True) + EPS)
    xn = x * r
    e = xn * w - tgt
    dy = e * (1.0 / D)
    g = dy * w
    dx = r * g - x * (r * r * r) * jnp.mean(x * g, axis=-1, keepdims=True)
    return dx, dx, _rows(e * e), _rows(dy * xn)


def _gdn_gates(ba, alog_c, dtb_c):
    col = _iota(ba.shape, 1)
    amask = (col >= 8) & (col < 16)
    beta = jnp.where(col < 8, _sig(ba), 0.0)
    z = ba + dtb_c
    ea_ = jnp.exp(alog_c)
    return beta, z, ea_, jnp.where(amask, -ea_ * _softplus(z), 0.0), amask


def _cols(x, g):
    return x[:, 128 * g:128 * g + 128]


def gdn_prep_fn(i, n, qkv, halo, ba, cw, alog_c, dtb_c, eb, ea):
    outs = [[], [], []]
    for g in range(3 * GDN_H):
        yc, _ = _conv(_cols(qkv, g), _cols(halo, g), _cols(cw, g), i)
        act = yc * _sig(yc)
        if g < 2 * GDN_H:
            act = _l2n(act, GDN_DK ** -0.5 if g < GDN_H else 1.0)
        outs[g // GDN_H].append(act)
    beta, _, _, gg, _ = _gdn_gates(ba, alog_c, dtb_c)
    gcs = _chunk_cumsum(gg, GDN_C)
    return (*[jnp.concatenate(o, axis=1) for o in outs], mm_sel(gcs, ea), mm_sel(beta, eb), jnp.transpose(gcs)[8:16, :])


def gdn_prep_bwd_fn(i, n, qkv, halo, ba, dqn, dkn, dv, dgcs_x, dbeta_x, dgcs_t, cw, alog_c, dtb_c, pb, pa):
    dycs, dwl = [], [[], [], [], []]
    for g in range(3 * GDN_H):
        yc, shifted = _conv(_cols(qkv, g), _cols(halo, g), _cols(cw, g), i)
        sg = _sig(yc)
        act = yc * sg
        if g < GDN_H:
            d = _l2n_bwd(act, _cols(dqn, g), GDN_DK ** -0.5)
        elif g < 2 * GDN_H:
            d = _l2n_bwd(act, _cols(dkn, g - GDN_H), 1.0)
        else:
            d = _cols(dv, g - 2 * GDN_H)
        dyc_g = d * (sg * (1.0 + yc * (1.0 - sg)))
        dycs.append(dyc_g)
        for k in range(4):
            dwl[k].append(_rows(dyc_g * shifted[k]))
    dyc = jnp.concatenate(dycs, axis=1)
    dws = [jnp.concatenate(l, axis=1) for l in dwl]
    beta, z, ea_, g, amask = _gdn_gates(ba, alog_c, dtb_c)
    tbn = ba.shape[0]
    rowpart = jnp.transpose(jnp.concatenate([jnp.zeros((8, tbn), F32), dgcs_t, jnp.zeros((112, tbn), F32)], axis=0))
    dg = _chunk_revcumsum(mm_sel(dgcs_x, pa) - rowpart, GDN_C)
    draw = jnp.where(amask, dg * (-ea_) * _sig(z), 0.0)
    dba = draw + mm_sel(dbeta_x, pb) * beta * (1.0 - beta)
    return (dyc, dba, dws[0], dws[1], dws[2], dws[3], _rows(dg * g), _rows(draw))


def conv_bwd_fn(i, n, dyc, halo, w):
    halo = jnp.where(i == n - 1, 0.0, halo)
    tb = dyc.shape[0]
    outs = []
    for g in range(dyc.shape[1] // 128):
        d, wg = _cols(dyc, g), _cols(w, g)
        xt = jnp.concatenate([d, _cols(halo, g)], axis=0)
        dx = d * wg[3:4, :]
        for k in range(3):
            dx = dx + pltpu.roll(xt, tb + 8 - (3 - k), 0)[:tb, :] * wg[k:k + 1, :]
        outs.append(dx)
    return (jnp.concatenate(outs, axis=1),)


def gdn_post_fn(i, n, o, z, w):
    outs = []
    for h in range(GDN_H):
        oh, zh = o[:, 128 * h:128 * h + 128], z[:, 128 * h:128 * h + 128]
        r = lax.rsqrt(jnp.mean(oh * oh, axis=-1, keepdims=True) + EPS)
        outs.append(oh * r * w * (zh * _sig(zh)))
    return (jnp.concatenate(outs, axis=1),)


def gdn_post_bwd_fn(i, n, o, z, doa, w):
    dos, dzs, dw = [], [], None
    for h in range(GDN_H):
        sl = slice(128 * h, 128 * h + 128)
        oh, zh, dh = o[:, sl], z[:, sl], doa[:, sl]
        r = lax.rsqrt(jnp.mean(oh * oh, axis=-1, keepdims=True) + EPS)
        s = _sig(zh)
        dn = dh * (zh * s)
        dzs.append(dh * (oh * r * w) * (s * (1.0 + zh * (1.0 - s))))
        t = _rows(dn * oh * r)
        dw = t if dw is None else dw + t
        g = dn * w
        dos.append(r * g - oh * (r * r * r) * jnp.mean(oh * g, axis=-1, keepdims=True))
    return jnp.concatenate(dos, axis=1), jnp.concatenate(dzs, axis=1), dw


def _ssd_gates(dtblk, alog_c, dtb_c):
    hmask = _iota(dtblk.shape, 1) < SSM_H
    z = dtblk + dtb_c
    return jnp.where(hmask, _softplus(z), 0.0), -jnp.exp(alog_c), z, hmask


def _silu_conv_cols(x, halo, w, b, i):
    outs = []
    for g in range(x.shape[1] // 128):
        yc, _ = _conv(_cols(x, g), _cols(halo, g), _cols(w, g), i)
        yc = yc + _cols(b, g)
        outs.append(yc * _sig(yc))
    return jnp.concatenate(outs, axis=1)


def _silu_conv_bwd_cols(x, halo, w, b, dout, i):
    dycs, dwl = [], [[], [], [], []]
    for g in range(x.shape[1] // 128):
        yc, shifted = _conv(_cols(x, g), _cols(halo, g), _cols(w, g), i)
        yc = yc + _cols(b, g)
        s = _sig(yc)
        dyc_g = _cols(dout, g) * (s * (1.0 + yc * (1.0 - s)))
        dycs.append(dyc_g)
        for k in range(4):
            dwl[k].append(_rows(dyc_g * shifted[k]))
    dyc = jnp.concatenate(dycs, axis=1)
    return dyc, [jnp.concatenate(l, axis=1) for l in dwl], _rows(dyc)


def ssd_prep_fn(i, n, xp, hx, bcp, hbc, dtblk, cwx, cwbc, cbx, cbbc, alog_c, dtb_c, e16):
    dt, a_neg, _, _ = _ssd_gates(dtblk, alog_c, dtb_c)
    acs = _chunk_cumsum(dt * a_neg, SSM_L)
    return (_silu_conv_cols(xp, hx, cwx, cbx, i), _silu_conv_cols(bcp, hbc, cwbc, cbbc, i), mm_sel(dt, e16),
            mm_sel(acs, e16), jnp.transpose(acs)[0:SSM_H, :])


def ssd_prep_bwd_fn(i, n, xp, hx, bcp, hbc, dtblk, dxs_a, dxs_b, db, dc, dgate, dacs_t, cwx, cwbc, cbx, cbbc, alog_c, dtb_c):
    dyx, dwx, dbx = _silu_conv_bwd_cols(xp, hx, cwx, cbx, dxs_a + dxs_b, i)
    dybc, dwbc, dbbc = _silu_conv_bwd_cols(bcp, hbc, cwbc, cbbc, jnp.concatenate([db, dc], axis=1), i)
    dt, a_neg, z, hmask = _ssd_gates(dtblk, alog_c, dtb_c)
    g0, g1 = dgate[:, :128], dgate[:, 128:]
    col = _iota(g0.shape, 1)
    lo, mid = col < 8, (col >= 8) & (col < 16)
    dacs_col = jnp.where(lo, g0, 0.0) + pltpu.roll(jnp.where(lo, g1, 0.0), 8, 1)
    ddt_dir = pltpu.roll(jnp.where(mid, g0, 0.0), 120, 1) + jnp.where(mid, g1, 0.0)
    tbn = dtblk.shape[0]
    rowpart = jnp.transpose(jnp.concatenate([dacs_t, jnp.zeros((128 - SSM_H, tbn), F32)], axis=0))
    da = _chunk_revcumsum(dacs_col - rowpart, SSM_L)
    draw = jnp.where(hmask, (ddt_dir + da * a_neg) * _sig(z), 0.0)
    return (dyx, dybc, draw, *dwx, *dwbc, dbx, dbbc, _rows(da * dt * a_neg), _rows(draw))


def _ssd_gate(y, xs, zs, d_x):
    y2 = y + xs * d_x
    s = _sig(zs)
    return y2, s, y2 * (zs * s)


def ssd_post_fn(i, n, y, xs, zs, d_x, nw):
    _, _, yg = _ssd_gate(y, xs, zs, d_x)
    outs = []
    for g in range(2):
        v = yg[:, 512 * g:512 * g + 512]
        outs.append(v * lax.rsqrt(jnp.mean(v * v, axis=-1, keepdims=True) + EPS))
    return (jnp.concatenate(outs, axis=1) * nw,)


def ssd_post_bwd_fn(i, n, y, xs, zs, dob, d_x, nw):
    y2, s, yg = _ssd_gate(y, xs, zs, d_x)
    gfull = dob * nw
    dygs, dnw = [], []
    for g in range(2):
        sl = slice(512 * g, 512 * g + 512)
        v, gg = yg[:, sl], gfull[:, sl]
        r = lax.rsqrt(jnp.mean(v * v, axis=-1, keepdims=True) + EPS)
        dygs.append(r * gg - v * (r * r * r) * jnp.mean(v * gg, axis=-1, keepdims=True))
        dnw.append(_rows(dob[:, sl] * v * r))
    dyg = jnp.concatenate(dygs, axis=1)
    dy2 = dyg * (zs * s)
    dzs = dyg * y2 * (s * (1.0 + zs * (1.0 - s)))
    return dy2, dy2 * d_x, dzs, jnp.concatenate(dnw, axis=1), _rows(dy2 * xs)


def _attn_probs(q, k):
    hs = [slice(MEM_DH * h, MEM_DH * h + MEM_DH) for h in range(MEM_H)]
    ss = [mm_nt(q[:, sl], k[:, sl]) * (MEM_DH ** -0.5) for sl in hs]
    es = [jnp.exp(s - jnp.max(s, axis=-1, keepdims=True)) for s in ss]
    return hs, [e / jnp.sum(e, axis=-1, keepdims=True) for e in es]


def attn_fn(i, n, q, k, v):
    hs, ps = _attn_probs(q, k)
    return (jnp.concatenate([mm(p, v[:, sl]) for p, sl in zip(ps, hs)], axis=1),)


def attn_bwd_fn(i, n, q, do, k, v):
    hs, ps = _attn_probs(q, k)
    dvs = [mm_tn(p, do[:, sl]) for p, sl in zip(ps, hs)]
    dps = [mm_nt(do[:, sl], v[:, sl]) for sl in hs]
    dss = [p * (dp - jnp.sum(dp * p, axis=-1, keepdims=True)) * (MEM_DH ** -0.5) for p, dp in zip(ps, dps)]
    dqs = [mm(ds, k[:, sl]) for ds, sl in zip(dss, hs)]
    dks = [mm_tn(ds, q[:, sl]) for ds, sl in zip(dss, hs)]
    return jnp.concatenate(dqs, axis=1), jnp.concatenate(dks, axis=1), jnp.concatenate(dvs, axis=1)


def add2_fn(i, n, sp, a, b):
    return (a + b,)


def sum4_fn(i, n, sp, a, b, c, d):
    return (((a.astype(F32) + b.astype(F32)) + c.astype(F32)) + d.astype(F32),)


def _adamw(w, g, m, v):
    m = ADAM_B1 * m + (1.0 - ADAM_B1) * g
    v = ADAM_B2 * v + (1.0 - ADAM_B2) * (g * g)
    m_hat = m / (1.0 - ADAM_B1 ** ADAM_STEP)
    v_hat = v / (1.0 - ADAM_B2 ** ADAM_STEP)
    delta = -ADAM_LR * (m_hat / (jnp.sqrt(v_hat) + ADAM_EPS) + ADAM_WD * w)
    return delta, m, v


def _gdn_stage1(q, k, v, gcs, grow, bb):
    C = GDN_C
    row, col = _iota((C, C), 0), _iota((C, C), 1)
    incl, strict = row >= col, row > col
    dmat = jnp.where(incl, jnp.exp(jnp.minimum(gcs[:, :C] - grow, 0.0)), 0.0)
    gam = jnp.exp(gcs)
    gl = gcs[C - 1:C, :]
    kb, vb = k * bb, v * bb
    kg = kb * gam
    lmat = jnp.where(strict, mm_nt(kb, k) * dmat, 0.0)
    pmat = jnp.where(incl, mm_nt(q, k) * dmat, 0.0)
    return dict(q=q, k=k, v=v, bb=bb, incl=incl, strict=strict, dmat=dmat, gam=gam, kb=kb, vb=vb, kg=kg,
                lmat=lmat, pmat=pmat, qd=q * gam, kdec=jnp.exp(gl - gcs), cd=jnp.exp(gl))


def _gdn_inverse(lmats):
    C = GDN_C
    eye = (_iota((C, C), 0) == _iota((C, C), 1)).astype(F32)
    xs = [-l for l in lmats]
    ts = [eye + x for x in xs]
    for _ in range(5):
        xs = [mm(x, x) for x in xs]
        ts = [t + mm(t, x) for t, x in zip(ts, xs)]
    res = [eye - mm3(eye + l, t) for l, t in zip(lmats, ts)]
    return [t + mm(t, r) for t, r in zip(ts, res)]


def gdn_fwd(qn, kn, v, gcs_x, beta_x, gcs_t, tb, gh):
    T = qn.shape[0]
    nb, ncb, nc, C = T // tb, tb // GDN_C, T // GDN_C, GDN_C
    idx = [(hh, c) for hh in range(gh) for c in range(ncb)]

    def body(q_ref, k_ref, v_ref, g_ref, b_ref, gt_ref, o_ref, st_ref, ti_ref, s_scr):
        @pl.when(pl.program_id(1) == 0)
        def _():
            s_scr[...] = jnp.zeros_like(s_scr)

        grows = [gt_ref[hh] for hh in range(gh)]
        at = lambda hh, c: (slice(C * c, C * (c + 1)), slice(128 * hh, 128 * hh + 128))
        st1 = []
        for hh, c in idx:
            sl, ln = at(hh, c)
            st1.append(_gdn_stage1(q_ref[sl, ln], k_ref[sl, ln], v_ref[sl, ln], g_ref[sl, ln], grows[hh][:, sl],
                                   b_ref[sl, ln]))
        tinvs = _gdn_inverse([s["lmat"] for s in st1])
        us = [mm(t, s["vb"]) for t, s in zip(tinvs, st1)]
        ws = [mm(t, s["kg"]) for t, s in zip(tinvs, st1)]
        kds = [s["k"] * s["kdec"] for s in st1]
        ms = [mm_tn(kd, w) for kd, w in zip(kds, ws)]
        bs = [mm_tn(kd, u) for kd, u in zip(kds, us)]
        gs = [s["qd"] - mm(s["pmat"], w) for s, w in zip(st1, ws)]
        pus = [mm(s["pmat"], u) for s, u in zip(st1, us)]
        ss = [s_scr[hh] for hh in range(gh)]
        for c in range(ncb):
            for hh in range(gh):
                n, (sl, ln) = hh * ncb + c, at(hh, c)
                ti_ref[hh, sl, :] = tinvs[n]
                st_ref[hh, c] = ss[hh]
                o_ref[sl, ln] = mm(gs[n], ss[hh]) + pus[n]
                ss[hh] = st1[n]["cd"] * ss[hh] - mm(ms[n], ss[hh]) + bs[n]
        for hh in range(gh):
            s_scr[hh] = ss[hh]

    blk = pl.BlockSpec((tb, 128 * gh), lambda h, i: (i, h))
    return pl.pallas_call(
        body, name="gdn_fwd", grid=(GDN_H // gh, nb),
        in_specs=[blk] * 5 + [pl.BlockSpec((gh, 1, tb), lambda h, i: (h, 0, i))],
        out_specs=[blk, pl.BlockSpec((gh, ncb, 128, 128), lambda h, i: (h, i, 0, 0)),
                   pl.BlockSpec((gh, tb, C), lambda h, i: (h, i, 0))],
        out_shape=[jax.ShapeDtypeStruct((T, D), F32), jax.ShapeDtypeStruct((GDN_H, nc, 128, 128), F32),
                   jax.ShapeDtypeStruct((GDN_H, T, C), F32)],
        scratch_shapes=[pltpu.VMEM((gh, 128, 128), F32)],
        compiler_params=_cparams(("parallel", "arbitrary")),
    )(qn, kn, v, gcs_x, beta_x, gcs_t)


def gdn_bwd(qn, kn, v, gcs_x, beta_x, gcs_t, do, states, tinv, tb, gh):
    T = qn.shape[0]
    nb, ncb, C = T // tb, tb // GDN_C, GDN_C

    def body(q_ref, k_ref, v_ref, g_ref, b_ref, gt_ref, do_ref, st_ref, ti_ref,
             dq_ref, dk_ref, dv_ref, dgc_ref, db_ref, dgr_ref, ds_scr):
        @pl.when(pl.program_id(1) == 0)
        def _():
            ds_scr[...] = jnp.zeros_like(ds_scr)

        grows = [gt_ref[hh] for hh in range(gh)]
        at = lambda hh, c: (slice(C * c, C * (c + 1)), slice(128 * hh, 128 * hh + 128))
        lastrow = _iota((C, 1), 0) == C - 1
        idx = [(hh, c) for hh in range(gh) for c in range(ncb)]
        P = []
        for hh, c in idx:
            sl, ln = at(hh, c)
            lc = _gdn_stage1(q_ref[sl, ln], k_ref[sl, ln], v_ref[sl, ln], g_ref[sl, ln], grows[hh][:, sl],
                             b_ref[sl, ln])
            lc.update(tinv=ti_ref[hh, sl, :], s=st_ref[hh, c], do=do_ref[sl, ln], kd=lc["k"] * lc["kdec"])
            P.append(lc)
        for l, u, w in zip(P, [mm(l["tinv"], l["vb"]) for l in P], [mm(l["tinv"], l["kg"]) for l in P]):
            l.update(u=u, w=w)
        for l, x in zip(P, [mm(l["w"], l["s"]) for l in P]):
            l["vn"] = l["u"] - x
        for l, a, b, c_, d in zip(P, [mm_nt(l["do"], l["s"]) for l in P], [mm_nt(l["do"], l["vn"]) for l in P],
                                  [mm_tn(l["qd"], l["do"]) for l in P], [mm_tn(l["pmat"], l["do"]) for l in P]):
            l.update(dqd=a, dp=jnp.where(l["incl"], b, 0.0), ds_q=c_, dvn_p=d)
        pre = dict(zip(idx, P))
        rows = {}
        hs = range(gh)
        ds = [ds_scr[hh] for hh in hs]
        for c in reversed(range(ncb)):
            L = [pre[hh, c] for hh in hs]
            dvn = [l["dvn_p"] + mm(l["kd"], d) for l, d in zip(L, ds)]
            dkd = [mm_nt(l["vn"], d) for l, d in zip(L, ds)]
            dcd = [_sum_all(l["s"] * d) for l, d in zip(L, ds)]
            ds = [l["ds_q"] + l["cd"] * d - mm_tn(l["w"], x) for l, d, x in zip(L, ds, dvn)]
            dw = [-mm_nt(x, l["s"]) for l, x in zip(L, dvn)]
            dvb = [mm_tn(l["tinv"], x) for l, x in zip(L, dvn)]
            dkg = [mm_tn(l["tinv"], x) for l, x in zip(L, dw)]
            da = [-jnp.where(l["strict"], mm_nt(a, l["u"]) + mm_nt(b, l["w"]), 0.0) for l, a, b in zip(L, dvb, dkg)]
            dm = [a * l["dmat"] for l, a in zip(L, da)]
            dn = [l["dp"] * l["dmat"] for l in L]
            dkb = [mm(a, l["k"]) for l, a in zip(L, dm)]
            dq = [mm(a, l["k"]) + l["gam"] * l["dqd"] for l, a in zip(L, dn)]
            dk = [mm_tn(a, l["kb"]) + mm_tn(b, l["q"]) for l, a, b in zip(L, dm, dn)]
            for hh in hs:
                sl, ln = at(hh, c)
                l = L[hh]
                e = da[hh] * l["lmat"] + l["dp"] * l["pmat"]
                t_kd = _lanes(dkd[hh] * l["kd"])
                dgl = _sum_all(t_kd) + dcd[hh] * l["cd"][:, :1]
                dgcs = (_lanes(e) + _lanes(l["dqd"] * l["qd"]) - t_kd + _lanes(dkg[hh] * l["kg"])
                        + jnp.where(lastrow, dgl, 0.0))
                rows[hh, c] = _rows(e)
                dq_ref[sl, ln] = dq[hh]
                dk_ref[sl, ln] = (dk[hh] + l["kdec"] * dkd[hh] + l["bb"] * l["gam"] * dkg[hh] + l["bb"] * dkb[hh])
                dv_ref[sl, ln] = l["bb"] * dvb[hh]
                dbeta = _lanes(dkg[hh] * l["gam"] * l["k"]) + _lanes(dvb[hh] * l["v"]) + _lanes(dkb[hh] * l["k"])
                db_ref[sl, ln] = jnp.broadcast_to(dbeta, (C, 128))
                dgc_ref[sl, ln] = jnp.broadcast_to(dgcs, (C, 128))
        for hh in hs:
            ds_scr[hh] = ds[hh]
            dgr_ref[hh] = jnp.concatenate([rows[hh, c] for c in range(ncb)], axis=1)

    blk = pl.BlockSpec((tb, 128 * gh), lambda h, i: (nb - 1 - i, h))
    rowspec = pl.BlockSpec((gh, 1, tb), lambda h, i: (h, 0, nb - 1 - i))
    return pl.pallas_call(
        body, name="gdn_bwd", grid=(GDN_H // gh, nb),
        in_specs=[blk] * 5 + [rowspec, blk,
                              pl.BlockSpec((gh, ncb, 128, 128), lambda h, i: (h, nb - 1 - i, 0, 0)),
                              pl.BlockSpec((gh, tb, C), lambda h, i: (h, nb - 1 - i, 0))],
        out_specs=[blk] * 5 + [rowspec],
        out_shape=[jax.ShapeDtypeStruct((T, D), F32)] * 5 + [jax.ShapeDtypeStruct((GDN_H, 1, T), F32)],
        scratch_shapes=[pltpu.VMEM((gh, 128, 128), F32)],
        compiler_params=_cparams(("parallel", "arbitrary")),
    )(qn, kn, v, gcs_x, beta_x, gcs_t, do, states, tinv)


def _ssd_pair(x2, dt2, acs2):
    last = acs2[SSM_L - 1:SSM_L, :]
    return jnp.exp(acs2), jnp.exp(last - acs2), x2 * dt2


def _ssd_head(hh, acs2, arow, dec2, cbm, bm, incl, col):
    lmask = (col >= 64 * hh) & (col < 64 * hh + 64)
    sg = jnp.where(incl, jnp.exp(jnp.minimum(acs2[:, 64 * hh:64 * hh + 1] - arow, 0.0)), 0.0)
    dec_col = dec2[:, 64 * hh:64 * hh + 1]
    return lmask, sg, sg * cbm, dec_col, bm * dec_col


def ssd_fwd(xs, bc, dt_x, acs_x, acs_t):
    T = xs.shape[0]
    nc, L = T // SSM_L, SSM_L

    def body(x_ref, bc_ref, dt_ref, ac_ref, at_ref, y_ref, hst_ref, h_scr):
        @pl.when(pl.program_id(0) == 0)
        def _():
            h_scr[...] = jnp.zeros_like(h_scr)

        row, col = _iota((L, L), 0), _iota((L, L), 1)
        incl = row >= col
        P, H = [], []
        for gp in range(8):
            g = gp // 4
            bm, cm = bc_ref[:, 128 * g:128 * g + 128], bc_ref[:, 256 + 128 * g:384 + 128 * g]
            cbm = mm_nt(cm, bm) if gp % 4 == 0 else cbm
            sl = slice(128 * gp, 128 * gp + 128)
            acs2 = ac_ref[:, sl]
            lam2, dec2, xd2 = _ssd_pair(x_ref[:, sl], dt_ref[:, sl], acs2)
            P.append(dict(sl=sl, lam2=lam2, xd2=xd2, hprev=h_scr[gp], cm=cm))
            for hh in range(2):
                lmask, _, mmat, _, bd = _ssd_head(hh, acs2, at_ref[2 * gp + hh], dec2, cbm, bm, incl, col)
                H.append(dict(mmat=mmat, bd=bd, xdh=jnp.where(lmask, xd2, 0.0), xd2=xd2))
        ys = [mm(h["mmat"], h["xdh"]) for h in H]
        sts = [mm_tn(h["xd2"], h["bd"]) for h in H]
        zs = [mm_nt(p["cm"], p["hprev"]) for p in P]
        for gp, p in enumerate(P):
            hst_ref[gp // 4, gp % 4] = p["hprev"]
            y_ref[:, p["sl"]] = ys[2 * gp] + ys[2 * gp + 1] + p["lam2"] * zs[gp]
            lam_rows = jnp.where(row < 64, p["lam2"][L - 1:L, 0:1], p["lam2"][L - 1:L, 64:65])
            h_scr[gp] = lam_rows * p["hprev"] + jnp.where(row < 64, sts[2 * gp], sts[2 * gp + 1])

    blk = pl.BlockSpec((L, D), lambda c: (c, 0))
    return pl.pallas_call(
        body, name="ssd_fwd", grid=(nc,),
        in_specs=[blk, pl.BlockSpec((L, 512), lambda c: (c, 0)), blk, blk, pl.BlockSpec((SSM_H, 1, L), lambda c: (0, 0, c))],
        out_specs=[blk, pl.BlockSpec((2, None, 4, 128, 128), lambda c: (0, c, 0, 0, 0))],
        out_shape=[jax.ShapeDtypeStruct((T, D), F32), jax.ShapeDtypeStruct((2, nc, 4, 128, 128), F32)],
        scratch_shapes=[pltpu.VMEM((8, 128, 128), F32)],
        compiler_params=_cparams(("arbitrary",)),
    )(xs, bc, dt_x, acs_x, acs_t)


def ssd_bwd(xs, bc, dt_x, acs_x, acs_t, dy, hstates):
    T = xs.shape[0]
    nc, L = T // SSM_L, SSM_L

    def body(x_ref, bc_ref, dt_ref, ac_ref, at_ref, dy_ref, hst_ref,
             dx_ref, db_ref, dc_ref, dgate_ref, dar_ref, dh_scr):
        @pl.when(pl.program_id(0) == 0)
        def _():
            dh_scr[...] = jnp.zeros_like(dh_scr)

        row, col = _iota((L, L), 0), _iota((L, L), 1)
        rowc = _iota((L, 1), 0)
        incl = row >= col
        G = [dict(bm=bc_ref[:, 128 * g:128 * g + 128], cm=bc_ref[:, 256 + 128 * g:384 + 128 * g]) for g in range(2)]
        for gr in G:
            gr["cbm"] = mm_nt(gr["cm"], gr["bm"])
        P = []
        for gp in range(8):
            sl = slice(128 * gp, 128 * gp + 128)
            x2, dt2, dy2, acs2 = x_ref[:, sl], dt_ref[:, sl], dy_ref[:, sl], ac_ref[:, sl]
            lam2, dec2, xd2 = _ssd_pair(x2, dt2, acs2)
            P.append(dict(sl=sl, gr=G[gp // 4], x2=x2, dt2=dt2, dy2=dy2, acs2=acs2, lam2=lam2, dec2=dec2, xd2=xd2,
                          hprev=hst_ref[gp // 4, gp % 4], dhn=dh_scr[gp], dz=lam2 * dy2))
        zs = [mm_nt(p["gr"]["cm"], p["hprev"]) for p in P]
        dcm_t = [mm(p["dz"], p["hprev"]) for p in P]
        dh_z = [mm_tn(p["dz"], p["gr"]["cm"]) for p in P]
        H = []
        for gp, p in enumerate(P):
            p["yoff"] = p["dz"] * zs[gp]
            p["q_rows"] = _lanes(p["dhn"] * p["hprev"])
            for hh in range(2):
                lmask, sg, mmat, dec_col, bd = _ssd_head(hh, p["acs2"], at_ref[2 * gp + hh], p["dec2"], p["gr"]["cbm"],
                                                         p["gr"]["bm"], incl, col)
                H.append(dict(p=p, hh=hh, j=2 * gp + hh, lmask=lmask, sg=sg, mmat=mmat, dec_col=dec_col, bd=bd))
        dms = [mm_nt(jnp.where(h["lmask"], h["p"]["dy2"], 0.0), h["p"]["xd2"]) for h in H]
        a1s = [mm_tn(h["mmat"], h["p"]["dy2"]) for h in H]
        a2s = [mm_nt(h["bd"], h["p"]["dhn"]) for h in H]
        dbds = [mm(jnp.where(h["lmask"], h["p"]["xd2"], 0.0), h["p"]["dhn"]) for h in H]
        for gr in G:
            gr.update(dcb=jnp.zeros((L, L), F32), dbm=jnp.zeros((L, SSM_N), F32), comp=jnp.zeros((L, 128), F32))
        dxd = [jnp.zeros((L, 128), F32) for _ in P]
        for h, dm_raw, a1, a2, dbd in zip(H, dms, a1s, a2s, dbds):
            p, hh, j = h["p"], h["hh"], h["j"]
            gr, jg = p["gr"], j % 8
            dm = jnp.where(incl, dm_raw, 0.0)
            gr["dcb"] = gr["dcb"] + dm * h["sg"]
            e = dm * h["mmat"]
            dxd_h = jnp.where(h["lmask"], a1 + a2, 0.0)
            dxd[j // 2] = dxd[j // 2] + dxd_h
            gr["dbm"] = gr["dbm"] + h["dec_col"] * dbd
            t = _lanes(dbd * h["bd"])
            lam_h = p["lam2"][L - 1:L, 64 * hh:64 * hh + 1]
            in_head = (rowc >= 64 * hh) & (rowc < 64 * hh + 64)
            add_last = _sum_all(t) + _sum_all(jnp.where(in_head, p["q_rows"], 0.0)) * lam_h
            dacs_col = (_lanes(jnp.where(h["lmask"], p["yoff"], 0.0)) + _lanes(e) - t
                        + jnp.where(rowc == L - 1, add_last, 0.0))
            ddt_col = _lanes(dxd_h * p["x2"])
            dar_ref[j] = _rows(e)
            gr["comp"] = gr["comp"] + jnp.where(col == jg, dacs_col, 0.0) + jnp.where(col == 8 + jg, ddt_col, 0.0)
        for gp, p in enumerate(P):
            lam_rows = jnp.where(row < 64, p["lam2"][L - 1:L, 0:1], p["lam2"][L - 1:L, 64:65])
            dh_scr[gp] = dh_z[gp] + lam_rows * p["dhn"]
            dx_ref[:, p["sl"]] = p["dt2"] * dxd[gp]
        for g, gr in enumerate(G):
            lanes = slice(128 * g, 128 * g + 128)
            dcm = (dcm_t[4 * g] + dcm_t[4 * g + 1]) + (dcm_t[4 * g + 2] + dcm_t[4 * g + 3])
            db_ref[:, lanes] = gr["dbm"] + mm_tn(gr["dcb"], gr["cm"])
            dc_ref[:, lanes] = dcm + mm(gr["dcb"], gr["bm"])
            dgate_ref[:, lanes] = gr["comp"]

    rv = lambda c: (nc - 1 - c, 0)
    blk, blk256 = pl.BlockSpec((L, D), rv), pl.BlockSpec((L, 256), rv)
    rowspec = pl.BlockSpec((SSM_H, 1, L), lambda c: (0, 0, nc - 1 - c))
    return pl.pallas_call(
        body, name="ssd_bwd", grid=(nc,),
        in_specs=[blk, pl.BlockSpec((L, 512), rv), blk, blk, rowspec, blk,
                  pl.BlockSpec((2, None, 4, 128, 128), lambda c: (0, nc - 1 - c, 0, 0, 0))],
        out_specs=[blk, blk256, blk256, blk256, rowspec],
        out_shape=[jax.ShapeDtypeStruct((T, D), F32), jax.ShapeDtypeStruct((T, 256), F32),
                   jax.ShapeDtypeStruct((T, 256), F32), jax.ShapeDtypeStruct((T, 256), F32),
                   jax.ShapeDtypeStruct((SSM_H, 1, T), F32)],
        scratch_shapes=[pltpu.VMEM((8, 128, 128), F32)],
        compiler_params=_cparams(("arbitrary",)),
    )(xs, bc, dt_x, acs_x, acs_t, dy, hstates)


def _pos():
    return lax.axis_index("x"), lax.axis_index("y"), lax.axis_index("c")


def _other_chips(x, y):
    return [(1 - x, y), (x, 1 - y), (1 - x, 1 - y)]


def _rcopy(src, dst, ssem, rsem, dev):
    return pltpu.make_async_remote_copy(src_ref=src, dst_ref=dst, send_sem=ssem, recv_sem=rsem,
                                        device_id=dev, device_id_type=MESH)


def _rows_at(start, n):
    return pl.ds(pl.multiple_of(start, 8), n)


def _comm_call(body, name, out_shape, n_in, scratch):
    return pl.pallas_call(
        body, name=name, out_shape=out_shape, in_specs=[ANY] * n_in,
        out_specs=[ANY] * len(out_shape) if isinstance(out_shape, (list, tuple)) else ANY,
        scratch_shapes=scratch,
        compiler_params=pltpu.CompilerParams(has_side_effects=True),
    )


def _dma_sems(n):
    return pltpu.SemaphoreType.DMA((n,))


def ag_chips(name, shard):
    rr, cc = shard.shape
    h, nq = rr // 2, ICI_CHUNKS
    hq = h // nq

    def body(x_ref, out_ref, ssem, rsem):
        x, y, c = _pos()
        me_s = 2 * x + y
        chips = _other_chips(x, y)
        started = []
        for q in range(nq):
            rows = _rows_at(c * h + q * hq, hq)
            for j, (cx, cy) in enumerate(chips):
                cp = _rcopy(x_ref.at[rows], out_ref.at[me_s, rows], ssem.at[j * nq + q], rsem.at[j * nq + q], (cx, cy, c))
                cp.start()
                started.append(cp)
        for q in range(nq):
            rows = _rows_at(c * h + q * hq, hq)
            for j, (cx, cy) in enumerate(chips):
                blk = out_ref.at[2 * cx + cy, rows]
                _rcopy(blk, blk, ssem.at[j * nq + q], rsem.at[j * nq + q], (cx, cy, c)).wait_recv()
                k = 3 * nq + j * nq + q
                cp = _rcopy(blk, blk, ssem.at[k], rsem.at[k], (x, y, 1 - c))
                cp.start()
                started.append(cp)
        for q in range(nq):
            rows = _rows_at((1 - c) * h + q * hq, hq)
            for j, (cx, cy) in enumerate(chips):
                blk = out_ref.at[2 * cx + cy, rows]
                k = 3 * nq + j * nq + q
                _rcopy(blk, blk, ssem.at[k], rsem.at[k], (x, y, 1 - c)).wait_recv()
        for cp in started:
            cp.wait_send()

    return _comm_call(body, name, jax.ShapeDtypeStruct((4, rr, cc), shard.dtype), 1,
                      [_dma_sems(6 * nq), _dma_sems(6 * nq)])(shard)


def _with_own(shard, got, s_me):
    return lax.dynamic_update_index_in_dim(got, shard, s_me, 0)


def all_gather_chips(name, shard, s_me):
    return _with_own(shard, ag_chips(name, shard), s_me)


HBM_SPEC = pl.BlockSpec(memory_space=pltpu.HBM)
SEM_SPEC = pl.BlockSpec(memory_space=pltpu.SEMAPHORE)
SPLIT_EFFECT = pltpu.SideEffectType.DATAFLOW_SIDE_EFFECTING


def _split_copies(pieces, x_ref, land_ref, sems, arriving):
    x, y, c = _pos()
    return [_rcopy(s, d_in if arriving else d_out, sems[j], sems[3 + j], dev)
            for j, (s, d_out, d_in, dev) in enumerate(pieces(x_ref, land_ref, x, y, c))]


def split_copy_start(name, src, land_shape, pieces, after):
    def body(x_ref, land_ref, after_ref, *outs):
        for cp in _split_copies(pieces, x_ref, land_ref, outs[:6], False):
            cp.start()
        outs[8][...] = jnp.zeros_like(outs[8])

    dma = pltpu.SemaphoreType.DMA(())
    res = pl.pallas_call(
        body, name=name,
        out_shape=(dma,) * 6 + (pltpu.HBM(src.shape, src.dtype), pltpu.HBM(land_shape, src.dtype),
                                jax.ShapeDtypeStruct((8, 128), F32)),
        in_specs=(HBM_SPEC, HBM_SPEC, ANY),
        out_specs=(SEM_SPEC,) * 6 + (HBM_SPEC, HBM_SPEC, pl.BlockSpec(memory_space=pltpu.VMEM)),
        input_output_aliases={0: 6, 1: 7},
        compiler_params=pltpu.CompilerParams(has_side_effects=SPLIT_EFFECT),
    )(pltpu.with_memory_space_constraint(src, pltpu.HBM),
      pltpu.with_memory_space_constraint(lax.empty(land_shape, src.dtype), pltpu.HBM), after)
    return res[:6], res[6], res[7], res[8]


def split_copy_wait(name, sems, src_thru, land_thru, after, pieces):
    def body(x_ref, land_ref, *rest):
        for cp in _split_copies(pieces, x_ref, land_ref, rest[:6], False):
            cp.wait_send()
        for cp in _split_copies(pieces, x_ref, land_ref, rest[:6], True):
            cp.wait_recv()

    return pl.pallas_call(
        body, name=name,
        out_shape=(pltpu.HBM(src_thru.shape, src_thru.dtype), pltpu.HBM(land_thru.shape, land_thru.dtype)),
        in_specs=(HBM_SPEC, HBM_SPEC) + (SEM_SPEC,) * 6 + (ANY,), out_specs=(HBM_SPEC, HBM_SPEC),
        input_output_aliases={0: 0, 1: 1},
        compiler_params=pltpu.CompilerParams(has_side_effects=SPLIT_EFFECT),
    )(src_thru, land_thru, *sems, after)


def ag_pieces(h):
    def pieces(x_ref, land_ref, x, y, c):
        rows = _rows_at(c * h, h)
        return [(x_ref.at[rows], land_ref.at[2 * x + y, rows], land_ref.at[2 * cx + cy, rows], (cx, cy, c))
                for cx, cy in _other_chips(x, y)]
    return pieces


def rs_pieces(x_ref, land_ref, x, y, c):
    return [(x_ref.at[2 * cx + cy], land_ref.at[j], land_ref.at[j], (cx, cy, c))
            for j, (cx, cy) in enumerate(_other_chips(x, y))]


def ag_forward(name, got):
    _, rr, cc = got.shape
    h, nq = rr // 2, D2D_CHUNKS
    hq = h // nq

    def body(g_ref, out_ref, ssem, rsem):
        x, y, c = _pos()
        slots = [2 * cx + cy for cx, cy in _other_chips(x, y)]
        cps = []
        for j, s in enumerate(slots):
            for q in range(nq):
                blk = out_ref.at[s, _rows_at(c * h + q * hq, hq)]
                cp = _rcopy(blk, blk, ssem.at[j * nq + q], rsem.at[j * nq + q], (x, y, 1 - c))
                cp.start()
                cps.append(cp)
        for cp in cps:
            cp.wait_send()
        for j, s in enumerate(slots):
            for q in range(nq):
                blk = out_ref.at[s, _rows_at((1 - c) * h + q * hq, hq)]
                _rcopy(blk, blk, ssem.at[j * nq + q], rsem.at[j * nq + q], (x, y, 1 - c)).wait_recv()

    return pl.pallas_call(
        body, name=name, out_shape=jax.ShapeDtypeStruct(got.shape, got.dtype), in_specs=[ANY], out_specs=ANY,
        scratch_shapes=[_dma_sems(3 * nq), _dma_sems(3 * nq)], input_output_aliases={0: 0},
        compiler_params=pltpu.CompilerParams(has_side_effects=True),
    )(got)


def rs_pair(name, g):
    _, rr, cc = g.shape
    h, nq = rr // 2, D2D_CHUNKS
    hq = h // nq

    def body(g_ref, recv_ref, ssem, rsem):
        x, y, c = _pos()
        cps = []
        for q in range(nq):
            cp = _rcopy(g_ref.at[:, _rows_at((1 - c) * h + q * hq, hq), :], recv_ref.at[:, pl.ds(q * hq, hq), :],
                        ssem.at[q], rsem.at[q], (x, y, 1 - c))
            cp.start()
            cps.append(cp)
        for cp in cps:
            cp.wait()

    return _comm_call(body, name, jax.ShapeDtypeStruct((4, h, cc), g.dtype), 1, [_dma_sems(nq), _dma_sems(nq)])(g)


def rs_chips(name, p):
    _, h, cc = p.shape
    nq = ICI_CHUNKS
    hq = h // nq

    def body(p_ref, buf_ref, ssem, rsem):
        x, y, c = _pos()
        sends = []
        for q in range(nq):
            rows = pl.ds(q * hq, hq)
            for j, (cx, cy) in enumerate(_other_chips(x, y)):
                cp = _rcopy(p_ref.at[2 * cx + cy, rows], buf_ref.at[j, rows], ssem.at[j * nq + q],
                            rsem.at[j * nq + q], (cx, cy, c))
                cp.start()
                sends.append(cp)
        for cp in sends:
            cp.wait()

    return _comm_call(body, name, jax.ShapeDtypeStruct((3, h, cc), p.dtype), 1,
                      [_dma_sems(3 * nq), _dma_sems(3 * nq)])(p)


def rs_join(name, half):
    h, cc = half.shape
    nq = D2D_CHUNKS
    hq = h // nq

    def body(h_ref, out_ref, ssem, rsem):
        x, y, c = _pos()
        cps = []
        for q in range(nq):
            rows = pl.ds(q * hq, hq)
            cp = _rcopy(h_ref.at[rows], out_ref.at[rows], ssem.at[q], rsem.at[q], (x, y, 1 - c))
            cp.start()
            cps.append(cp)
        for cp in cps:
            cp.wait()

    return _comm_call(body, name, jax.ShapeDtypeStruct((h, cc), half.dtype), 1, [_dma_sems(nq), _dma_sems(nq)])(half)


def reduce_scatter(tag, g, tb, sp):
    return rs_end(rs_begin(tag, g, tb, sp, False), None)


def rs_begin(tag, g, tb, sp, split, after=None):
    _, rr, cc = g.shape
    h = rr // 2
    nbh = h // tb
    recv = rs_pair(tag + "_pair", g)
    mine_rows = lambda i, s: (i // nbh) * (2 * nbh) + s[0] * nbh + i % nbh
    part = rowwise(add2_fn, tag + "_add", 4 * h, tb, [R(g.reshape(4 * rr, cc), off=mine_rows), R(recv.reshape(4 * h, cc))],
                   [], [(cc, BF16)], sp=sp)[0].reshape(4, h, cc)
    st = dict(tag=tag, tb=tb, sp=sp, split=split, part=part)
    if split:
        st["sems"], st["part"], st["land"], st["token"] = split_copy_start(tag + "_start", part, (3, h, cc), rs_pieces,
                                                                           sp if after is None else after)
    return st


def rs_end(st, after):
    tag, tb, sp, part = st["tag"], st["tb"], st["sp"], st["part"]
    _, h, cc = part.shape
    nbh = h // tb
    if st["split"]:
        part, buf = split_copy_wait(tag + "_wait", st["sems"], part, st["land"], after, rs_pieces)
    else:
        buf = rs_chips(tag + "_chips", part)
    red = rowwise(sum4_fn, tag + "_sum", h, tb,
                  [R(part.reshape(4 * h, cc), off=lambda i, s: s[1] * nbh + i)]
                  + [R(buf.reshape(3 * h, cc), off=k * nbh) for k in range(3)],
                  [], [(cc, F32)], sp=sp)[0]
    return red, rs_join(tag + "_join", red)


def adam_halves(name, w, m, v, red, other, tb, blk0, sp):
    nbh = red.shape[0] // tb

    def fn(i, n, s, w_, m_, v_, r_, o_):
        g = jnp.where((blk0 + i) // nbh == s[0], r_, o_)
        return (g,) + _adamw(w_, g, m_, v_)

    half_rows = lambda i, s: (blk0 + i) % nbh
    return rowwise(fn, name, w.shape[0], tb, [R(w), R(m), R(v), R(red, off=half_rows), R(other, off=half_rows)],
                   [], [(w.shape[1], F32)] * 4, sp=sp)


SMALL_LANES = 3 * D


def all_reduce_items(name, items):
    flat = [a for it in items for a in it]
    shapes = [(sum(a.shape[0] for a in it), it[0].shape[1]) for it in items]
    nrows = -(-sum(s[0] for s in shapes) // 8) * 8

    def body(*refs):
        ins, outs = refs[:len(flat)], refs[len(flat):len(flat) + len(items)]
        mine, buf, ssem, rsem = refs[len(flat) + len(items):]
        x, y, c = _pos()
        me = 4 * x + 2 * y + c
        mine[...] = jnp.zeros_like(mine)
        r = 0
        for ref in ins:
            mine[r:r + ref.shape[0], 0:ref.shape[1]] = ref[...]
            r += ref.shape[0]
        buf[me] = mine[...]
        cps = []
        for k in range(1, 8):
            dev = (x ^ (k >> 2), y ^ ((k >> 1) & 1), c ^ (k & 1))
            cp = _rcopy(mine, buf.at[me], ssem.at[k - 1], rsem.at[k - 1], dev)
            cp.start()
            cps.append(cp)
        for cp in cps:
            cp.wait()
        r = 0
        for (nr, n), out in zip(shapes, outs):
            acc = buf[0, r:r + nr, 0:n]
            for d in range(1, 8):
                acc = acc + buf[d, r:r + nr, 0:n]
            out[...] = acc
            r += nr

    vm = pl.BlockSpec(memory_space=pltpu.VMEM)
    return pl.pallas_call(
        body, name=name, out_shape=[jax.ShapeDtypeStruct(s, F32) for s in shapes],
        in_specs=[vm] * len(flat), out_specs=[vm] * len(items),
        scratch_shapes=[pltpu.VMEM((nrows, SMALL_LANES), F32), pltpu.VMEM((8, nrows, SMALL_LANES), F32),
                        _dma_sems(7), _dma_sems(7)],
        compiler_params=pltpu.CompilerParams(has_side_effects=True),
    )(*flat)


def adam_small(ws, gs, ms, vs):
    n = len(ws)

    def body(*refs):
        for k in range(n):
            w, g, m, v = (refs[j * n + k][...] for j in range(4))
            for j, val in enumerate(_adamw(w, g, m, v)):
                refs[(4 + j) * n + k][...] = val

    vm = pl.BlockSpec(memory_space=pltpu.VMEM)
    res = pl.pallas_call(
        body, name="adam_small", out_shape=[jax.ShapeDtypeStruct(w.shape, F32) for w in ws] * 3,
        in_specs=[vm] * (4 * n), out_specs=[vm] * (3 * n),
    )(*ws, *gs, *ms, *vs)
    return res[:n], res[n:2 * n], res[2 * n:]


def _sel(rows, cols, pairs):
    m = np.zeros((rows, cols), np.float32)
    for r, c in pairs:
        m[r, c] = 1.0
    return jnp.asarray(m)


def _pad_win(w):
    z = jnp.zeros((w.shape[0], 112), w.dtype)
    return jnp.concatenate([w[:, :4096], w[:, 4112:6672], w[:, 4096:4112], z, w[:, 6672:6688], z], axis=1)


def _unpad_win(wp):
    return jnp.concatenate([wp[:, :4096], wp[:, 6656:6672], wp[:, 4096:6656], wp[:, 6784:6800]], axis=1)


def kernel(x, mem, norm1_w, w_in, gdn_conv_w, gdn_a_log, gdn_dt_bias, gdn_norm_w, ssm_conv_w, ssm_conv_b, ssm_a_log, ssm_dt_bias, ssm_d, ssm_norm_w, w_out, norm2_w, mem_norm_w, wq_mem, wk_mem, wv_mem, wo_mem, norm3_w, w_up, w_down, final_norm_w, loss_target, m_norm1_w, m_w_in, m_gdn_conv_w, m_gdn_a_log, m_gdn_dt_bias, m_gdn_norm_w, m_ssm_conv_w, m_ssm_conv_b, m_ssm_a_log, m_ssm_dt_bias, m_ssm_d, m_ssm_norm_w, m_w_out, m_norm2_w, m_mem_norm_w, m_wq_mem, m_wk_mem, m_wv_mem, m_wo_mem, m_norm3_w, m_w_up, m_w_down, m_final_norm_w, v_norm1_w, v_w_in, v_gdn_conv_w, v_gdn_a_log, v_gdn_dt_bias, v_gdn_norm_w, v_ssm_conv_w, v_ssm_conv_b, v_ssm_a_log, v_ssm_dt_bias, v_ssm_d, v_ssm_norm_w, v_w_out, v_norm2_w, v_mem_norm_w, v_wq_mem, v_wk_mem, v_wv_mem, v_wo_mem, v_norm3_w, v_w_up, v_w_down, v_final_norm_w):
    T, M = x.shape[1], mem.shape[1]
    xi, yi, ci = _pos()
    s_me = 2 * xi + yi
    x0, mem0, tgt = x[0], mem[0], loss_target[0]
    tb = min(256, T)
    tbp = min(256, T)
    row = lambda v: v.reshape(1, -1)

    win_g = all_gather_chips("ag_win", w_in.astype(BF16), s_me)
    w_in_p = _pad_win(win_g.transpose(1, 0, 2).reshape(D, IN_COLS))
    keep = (ci == 0).astype(F32)
    gcw_z = lax.dynamic_update_slice(jnp.zeros((4, 3 * D), F32), gdn_conv_w * keep, (0, s_me * 768))
    scw_z = lax.dynamic_update_slice(jnp.zeros((4, 1536), F32), ssm_conv_w * keep, (0, s_me * 384))
    gcw, scw = all_reduce_items("ar_convw", [[gcw_z], [scw_z]])
    scw_x, scw_bc = scw[:, :D], scw[:, D:]
    rest_shard = jnp.concatenate([w_up, w_down, w_out, wq_mem, wk_mem, wv_mem, wo_mem], axis=0).astype(BF16)
    ag_sems, rest_thru, rest_land, ag_token = split_copy_start("ag_rest_start", rest_shard, (4,) + rest_shard.shape,
                                                               ag_pieces(rest_shard.shape[0] // 2), gcw)
    sp = jnp.stack([ci, s_me]).astype(jnp.int32)
    scb_x, scb_bc = row(ssm_conv_b[:D]), row(ssm_conv_b[D:])

    galog_c, gdtb_c = row(jnp.pad(gdn_a_log, (8, 112))), row(jnp.pad(gdn_dt_bias, (8, 112)))
    salog_c, sdtb_c = row(jnp.pad(ssm_a_log, (0, 112))), row(jnp.pad(ssm_dt_bias, (0, 112)))
    sd_x = row(jnp.repeat(ssm_d, 64))
    eb = _sel(128, D, [(h, 128 * h + l) for h in range(8) for l in range(128)])
    ea = _sel(128, D, [(8 + h, 128 * h + l) for h in range(8) for l in range(128)])
    e16 = _sel(128, D, [(h, 64 * h + l) for h in range(16) for l in range(64)])
    pb = _sel(D, 128, [(128 * h, h) for h in range(8)])
    pa = _sel(D, 128, [(128 * h, 8 + h) for h in range(8)])

    h1 = rowwise(rms_fwd_fn, "rms1", T, tb, [R(x0)], [row(norm1_w) + ag_token[0:1, 0:1]], [(D, BF16)])[0]
    p = matmul("mm_in", h1, w_in_p, "nn", 2048, 768, 1024, [F32])[0]
    gp_ins = [R(p, 3 * D, CB_QKV, "prev"), R(p, 128, CB_BA)]
    qn, kn, vv, gcs_x, beta_x, gcs_t = rowwise(gdn_prep_fn, "gdn_prep", T, tbp, gp_ins,
                                               [gcw, galog_c, gdtb_c, eb, ea], [(D, F32)] * 5 + [(-8, F32)])
    gcs_t = gcs_t.reshape(GDN_H, 1, T)
    gtb, ggh = min(128, T), 8
    o_gdn, s_states, tinv = gdn_fwd(qn, kn, vv, gcs_x, beta_x, gcs_t, gtb, ggh)
    gnw = row(gdn_norm_w)
    oa = rowwise(gdn_post_fn, "gdn_post", T, tb, [R(o_gdn), R(p, D, CB_Z)], [gnw], [(D, BF16)])[0]
    sp_ins = [R(p, D, CB_XS, "prev"), R(p, 512, CB_BC, "prev"), R(p, 128, CB_DT)]
    sp_full = [scw_x, scw_bc, scb_x, scb_bc, salog_c, sdtb_c]
    xs, bc, dt_x, acs_x, acs_t = rowwise(ssd_prep_fn, "ssd_prep", T, tbp, sp_ins, sp_full + [e16],
                                         [(D, F32), (512, F32), (D, F32), (D, F32), (-SSM_H, F32)])
    acs_t = acs_t.reshape(SSM_H, 1, T)
    y_ssd, h_states = ssd_fwd(xs, bc, dt_x, acs_x, acs_t)
    snw = row(ssm_norm_w)
    ob = rowwise(ssd_post_fn, "ssd_post", T, tb, [R(y_ssd), R(xs), R(p, D, CB_ZS)], [sd_x, snw], [(D, BF16)])[0]
    rest_thru, rest_land = split_copy_wait("ag_rest_wait", ag_sems, rest_thru, rest_land, ob,
                                           ag_pieces(rest_shard.shape[0] // 2))
    rest_g = _with_own(rest_thru, ag_forward("ag_rest_fwd", rest_land), s_me)
    assert D == 1024
    view = lambda shape, blk, at: dict(b_sel=(shape, blk, at))
    wup_n = view((D, D_FF), (None, D, D), lambda i, j, k: (j, 0, 0))
    wup_t = view((D, D_FF), (None, D, D), lambda i, j, k: (k, 0, 0))
    wdown_n = view((D_FF, D), (None, D, D), lambda i, j, k: (k, 1, 0))
    wdown_t = view((D_FF, D), (None, D, D), lambda i, j, k: (j, 1, 0))
    wout_a = view((D, D), (2, 512, D), lambda i, j, k: (0, 4, 0))
    wout_b = view((D, D), (2, 512, D), lambda i, j, k: (1, 4, 0))
    wq_v, wk_v, wv_v, wo_v = (view((D, D), (4, 256, D), lambda i, j, k, r=r: (0, r, 0)) for r in (10, 11, 12, 13))
    x1a = matmul("mm_out_a", oa, rest_g, "nn", 1024, 1024, 1024, [F32], _epi_res, [x0], **wout_a)[0]
    x1, h2 = matmul("mm_out_b", ob, rest_g, "nn", 1024, 1024, 1024, [F32, BF16], _epi_res_rms, [x1a],
                    [row(norm2_w)], **wout_b)

    mn = rowwise(rms_fwd_fn, "rms_mem", M, M, [R(mem0)], [row(mem_norm_w)], [(D, BF16)])[0]
    km = matmul("mm_k", mn, rest_g, "nn", 256, 1024, 1024, [BF16], **wk_v)[0]
    vm = matmul("mm_v", mn, rest_g, "nn", 256, 1024, 1024, [BF16], **wv_v)[0]
    qm = matmul("mm_q", h2, rest_g, "nn", 1024, 1024, 1024, [BF16], **wq_v)[0]
    ao = rowwise(attn_fn, "attn", T, tb, [R(qm)], [km, vm], [(D, BF16)])[0]
    x2, h3 = matmul("mm_o", ao, rest_g, "nn", 1024, 1024, 1024, [F32, BF16], _epi_res_rms, [x1], [row(norm3_w)], **wo_v)
    u, act = matmul("mm_up", h3, rest_g, "nn", 2048, 1024, 1024, [BF16, BF16], _epi_relu2, **wup_n)
    x3 = matmul("mm_down", act, rest_g, "nn", 1024, 1024, 1024, [F32], _epi_res, [x2], **wdown_n)[0]
    dx3, dx3b, loss_lane, g_final = rowwise(final_fn, "final", T, tb, [R(x3), R(tgt)], [row(final_norm_w)],
                                            [(D, F32), (D, BF16)], [(1, D), (1, D)])
    loss = lax.psum(0.5 / D * jnp.sum(loss_lane), ("x", "y", "c"))

    dup = matmul("mm_dact", dx3b, rest_g, "nt", 2048, 1024, 1024, [BF16], _epi_dup, [u], **wdown_t)[0]
    def g_into(buf, blk, at):
        return dict(into=(buf, blk, lambda i, j, k, at=at: at(i, j)))

    grest = jax.ShapeDtypeStruct((4, 3584, D), F32)
    grest = matmul("mm_gdown", act, dx3b, "tn", 1024, 1024, 2048, [F32],
                   **g_into(grest, (None, 1024, D), lambda i, j: (i, 1, 0)))
    dh3 = matmul("mm_dh3", dup, rest_g, "nt", 2048, 1024, 1024, [F32], **wup_t)[0]
    dx2, dx2b, g_n3 = rowwise(rms_bwd_fn, "rms3_bwd", T, tb, [R(x2), R(dh3), R(dx3)], [row(norm3_w)],
                              [(D, F32), (D, BF16)], [(1, D)])
    grest = matmul("mm_gup", h3, dup, "tn", 1024, 1024, 2048, [F32],
                   **g_into(grest, (None, 1024, D), lambda i, j: (j, 0, 0)))
    dao = matmul("mm_dao", dx2b, rest_g, "nt", 1024, 1024, 1024, [F32], **wo_v)[0]
    grest = matmul("mm_gwo", ao, dx2b, "tn", 1024, 1024, 2048, [F32],
                   **g_into(grest, (4, 256, D), lambda i, j: (0, 13, 0)))
    dqm, dkm, dvm = rowwise(attn_bwd_fn, "attn_bwd", T, tb, [R(qm), R(dao)], [km, vm], [(D, BF16)],
                            [(M, D), (M, D)])
    dx1, dx1b, g_n2 = matmul("mm_dh2", dqm, rest_g, "nt", 512, 1024, 1024, [F32, BF16], _epi_rms_bwd, [x1, dx2],
                             [row(norm2_w)], n_acc=1, **wq_v)
    grest = matmul("mm_gwq", h2, dqm, "tn", 1024, 1024, 2048, [F32],
                   **g_into(grest, (4, 256, D), lambda i, j: (0, 10, 0)))
    grest = matmul("mm_gwk", mn, dkm, "tn", 1024, 1024, 256, [F32],
                   **g_into(grest, (4, 256, D), lambda i, j: (0, 11, 0)))
    grest = matmul("mm_gwv", mn, dvm, "tn", 1024, 1024, 256, [F32],
                   **g_into(grest, (4, 256, D), lambda i, j: (0, 12, 0)))
    dmn_k = matmul("mm_dmk", dkm, rest_g, "nt", 256, 1024, 1024, [F32], **wk_v)[0]
    dmn = matmul("mm_dmv", dvm, rest_g, "nt", 256, 1024, 1024, [F32], _epi_res, [dmn_k], **wv_v)[0]
    g_nmem = rowwise(rms_bwd_w_fn, "rmsmem_bwd", M, M, [R(mem0), R(dmn)], [row(mem_norm_w)], [], [(1, D)])[0]
    doa = matmul("mm_doa", dx1b, rest_g, "nt", 2048, 1024, 1024, [F32], **wout_a)[0]
    dob = matmul("mm_dob", dx1b, rest_g, "nt", 2048, 1024, 1024, [F32], **wout_b)[0]
    grest = matmul("mm_gwout_a", oa, dx1b, "tn", 1024, 1024, 2048, [F32],
                   **g_into(grest, (2, 512, D), lambda i, j: (0, 4, 0)))
    grest = matmul("mm_gwout_b", ob, dx1b, "tn", 1024, 1024, 2048, [F32],
                   **g_into(grest, (2, 512, D), lambda i, j: (1, 4, 0)))

    rs_rest = rs_begin("rs_rest", grest, 256, sp, True)

    dp = jax.ShapeDtypeStruct((T, p.shape[1]), BF16)
    dy_ssd, dxs_dir, dp, g_snw, g_sd_lane = rowwise(
        ssd_post_bwd_fn, "ssd_post_bwd", T, tb, [R(y_ssd), R(xs), R(p, D, CB_ZS), R(dob)],
        [sd_x + rs_rest["token"][0:1, 0:1], snw],
        [(D, F32), (D, F32), (D, BF16, dp, CB_ZS)], [(1, D), (1, D)])
    dxs_scan, db_s, dc_s, dgate, dacs_t = ssd_bwd(xs, bc, dt_x, acs_x, acs_t, dy_ssd, h_states)
    spb = rowwise(ssd_prep_bwd_fn, "ssd_prep_bwd", T, tbp,
                  sp_ins + [R(dxs_scan), R(dxs_dir), R(db_s), R(dc_s), R(dgate), RC(dacs_t.reshape(SSM_H, T))], sp_full,
                  [(D, F32), (512, F32), (128, BF16, dp, CB_DT)],
                  [(1, D)] * 4 + [(1, 512)] * 4 + [(1, D), (1, 512), (1, 128), (1, 128)])
    dyc_x, dyc_bc, dp = spb[:3]
    dp = rowwise(conv_bwd_fn, "conv_bwd_x", T, tbp, [R(dyc_x, halo="next")], [scw_x], [(D, BF16, dp, CB_XS)])[0]
    dp = rowwise(conv_bwd_fn, "conv_bwd_bc", T, tbp, [R(dyc_bc, halo="next")], [scw_bc], [(512, BF16, dp, CB_BC)])[0]

    do_gdn, dp, g_gnw = rowwise(gdn_post_bwd_fn, "gdn_post_bwd", T, tb, [R(o_gdn), R(p, D, CB_Z), R(doa)], [gnw],
                                [(D, F32), (D, BF16, dp, CB_Z)], [(1, 128)])
    dqn, dkn, dvv, dgcs_x, dbeta_x, dgcs_t = gdn_bwd(qn, kn, vv, gcs_x, beta_x, gcs_t, do_gdn, s_states, tinv, gtb, ggh)
    gpb = rowwise(gdn_prep_bwd_fn, "gdn_prep_bwd", T, tbp,
                  gp_ins + [R(dqn), R(dkn), R(dvv), R(dgcs_x), R(dbeta_x), RC(dgcs_t.reshape(GDN_H, T))],
                  [gcw, galog_c, gdtb_c, pb, pa],
                  [(3 * D, F32), (128, BF16, dp, CB_BA)], [(1, 3 * D)] * 4 + [(1, 128), (1, 128)])
    dyc_qkv, dp = gpb[:2]
    dp = rowwise(conv_bwd_fn, "conv_bwd_qkv", T, tbp, [R(dyc_qkv, halo="next")], [gcw], [(3 * D, BF16, dp, CB_QKV)])[0]
    dh1 = matmul("mm_dh1", dp, w_in_p, "nt", 2048, 1024, 768, [F32])[0]
    grad_x, g_n1 = rowwise(rms_bwd1_fn, "rms1_bwd", T, tb, [R(x0), R(dh1), R(dx1)], [row(norm1_w)], [(D, F32)], [(1, D)])
    g_win_p = matmul("mm_gwin", h1, dp, "tn", 1024, 768, 2048, [F32])[0]

    items = [[g_n1], [gpb[6]], [gpb[7]], [g_gnw], [spb[11]], [spb[12]], [spb[13]], [spb[14]], [g_sd_lane], [g_snw],
             [g_n2], [g_nmem], [g_n3], [g_final], list(gpb[2:6]), list(spb[3:7]), list(spb[7:11])]
    (gr_n1, r_galog, r_gdtb, gr_gnw, r_scb_x, r_scb_bc, r_salog, r_sdtb, r_sd, gr_snw, gr_n2, gr_nmem, gr_n3,
     gr_final, r_gcw, r_scw_x, r_scw_bc) = all_reduce_items("ar_grads", items)
    gr_galog, gr_gdtb = r_galog[:, 8:16], r_gdtb[:, 8:16]
    gr_salog, gr_sdtb = r_salog[:, :SSM_H], r_sdtb[:, :SSM_H]
    gr_sd = r_sd.reshape(SSM_H, SSM_P).sum(axis=1).reshape(1, SSM_H)
    gr_scb = jnp.concatenate([r_scb_x, r_scb_bc], axis=1)
    gr_gcw = lax.dynamic_slice(r_gcw, (0, s_me * 768), (4, 768))
    gr_scw = lax.dynamic_slice(jnp.concatenate([r_scw_x, r_scw_bc], axis=1), (0, s_me * 384), (4, 384))

    g_win = _unpad_win(g_win_p).reshape(D, 4, IN_COLS // 4).transpose(1, 0, 2)
    rs_win = rs_begin("rs_win", g_win, 256, sp, True, gr_n1)
    red_r, oth_r = rs_end(rs_rest, rs_win["token"])

    big = {}
    for n, w, m, v, blk0 in (("w_up", w_up, m_w_up, v_w_up, 0), ("w_down", w_down, m_w_down, v_w_down, 4),
                             ("w_out", w_out, m_w_out, v_w_out, 8), ("wq_mem", wq_mem, m_wq_mem, v_wq_mem, 10),
                             ("wk_mem", wk_mem, m_wk_mem, v_wk_mem, 11), ("wv_mem", wv_mem, m_wv_mem, v_wv_mem, 12),
                             ("wo_mem", wo_mem, m_wo_mem, v_wo_mem, 13)):
        big[n] = adam_halves("adam_" + n, w, m, v, red_r, oth_r, 256, blk0, sp)
    red_w, oth_w = rs_end(rs_win, big["wo_mem"][1])
    big["w_in"] = adam_halves("adam_win", w_in, m_w_in, v_w_in, red_w, oth_w, 256, 0, sp)
    names_s =["norm1_w", "gdn_conv_w", "gdn_a_log", "gdn_dt_bias", "gdn_norm_w", "ssm_conv_w", "ssm_conv_b",
               "ssm_a_log", "ssm_dt_bias", "ssm_d", "ssm_norm_w", "norm2_w", "mem_norm_w", "norm3_w", "final_norm_w"]
    w_s = [norm1_w, gdn_conv_w, gdn_a_log, gdn_dt_bias, gdn_norm_w, ssm_conv_w, ssm_conv_b, ssm_a_log, ssm_dt_bias,
           ssm_d, ssm_norm_w, norm2_w, mem_norm_w, norm3_w, final_norm_w]
    g_s = [gr_n1, gr_gcw, gr_galog, gr_gdtb, gr_gnw, gr_scw, gr_scb, gr_salog, gr_sdtb, gr_sd, gr_snw, gr_n2,
           gr_nmem, gr_n3, gr_final]
    m_s = [m_norm1_w, m_gdn_conv_w, m_gdn_a_log, m_gdn_dt_bias, m_gdn_norm_w, m_ssm_conv_w, m_ssm_conv_b, m_ssm_a_log,
           m_ssm_dt_bias, m_ssm_d, m_ssm_norm_w, m_norm2_w, m_mem_norm_w, m_norm3_w, m_final_norm_w]
    v_s = [v_norm1_w, v_gdn_conv_w, v_gdn_a_log, v_gdn_dt_bias, v_gdn_norm_w, v_ssm_conv_w, v_ssm_conv_b, v_ssm_a_log,
           v_ssm_dt_bias, v_ssm_d, v_ssm_norm_w, v_norm2_w, v_mem_norm_w, v_norm3_w, v_final_norm_w]
    shp_s = [w.shape for w in w_s]
    as2d = lambda a: a if a.ndim == 2 else a.reshape(1, -1)
    d_l, m_l, v_l = adam_small([as2d(a) for a in w_s], [as2d(a) for a in g_s], [as2d(a) for a in m_s],
                               [as2d(a) for a in v_s])

    grads, deltas, new_m, new_v = {}, {}, {}, {}
    for n, (gg, dd, mm_, vv_) in big.items():
        grads[n], deltas[n], new_m[n], new_v[n] = gg, dd, mm_, vv_
    for k, n in enumerate(names_s):
        grads[n] = g_s[k].reshape(shp_s[k])
        deltas[n], new_m[n], new_v[n] = (a[k].reshape(shp_s[k]) for a in (d_l, m_l, v_l))
    order = ["norm1_w", "w_in", "gdn_conv_w", "gdn_a_log", "gdn_dt_bias", "gdn_norm_w", "ssm_conv_w", "ssm_conv_b",
             "ssm_a_log", "ssm_dt_bias", "ssm_d", "ssm_norm_w", "w_out", "norm2_w", "mem_norm_w", "wq_mem", "wk_mem",
             "wv_mem", "wo_mem", "norm3_w", "w_up", "w_down", "final_norm_w"]
    return (loss, grad_x[None], *[grads[n] for n in order], *[deltas[n] for n in order],
            *[new_m[n] for n in order], *[new_v[n] for n in order])
```

```python
import numpy as np
import jax
import jax.numpy as jnp
from jax import lax
from jax.experimental import pallas as pl
from jax.experimental.pallas import tpu as pltpu

F32, BF16 = jnp.float32, jnp.bfloat16
MESH = pl.DeviceIdType.MESH
ANY = pl.BlockSpec(memory_space=pl.ANY)

EPS = 1e-6
D = 1024
GDN_H, GDN_DK, GDN_C = 8, 128, 64
SSM_H, SSM_P, SSM_N, SSM_L = 16, 64, 128, 128
MEM_H, MEM_DH = 4, 256
D_FF = 4096
IN_COLS = 6688
CB_QKV, CB_Z, CB_ZS, CB_XS, CB_BC, CB_BA, CB_DT = 0, 3, 4, 5, 12, 52, 53
VMEM_LIMIT = 56 * 1024 * 1024
D2D_CHUNKS = 8
ICI_CHUNKS = 4

ADAM_LR, ADAM_B1, ADAM_B2, ADAM_EPS, ADAM_WD, ADAM_STEP = 0.001, 0.9, 0.999, 1e-08, 0.01, 10


def _dg(a, b, ca, cb):
    return lax.dot_general(a, b, (((ca,), (cb,)), ((), ())), preferred_element_type=F32)


def _bf(x):
    return x.astype(BF16)


def mm(a, b):
    return _dg(_bf(a), _bf(b), 1, 0)


def mm_nt(a, b):
    return _dg(_bf(a), _bf(b), 1, 1)


def mm_tn(a, b):
    return _dg(_bf(a), _bf(b), 0, 0)


def mm_sel(a, sel):
    hi = a.astype(BF16)
    r1 = a - hi.astype(F32)
    mid = r1.astype(BF16)
    lo = (r1 - mid.astype(F32)).astype(BF16)
    s = sel.astype(BF16)
    return _dg(hi, s, 1, 0) + (_dg(mid, s, 1, 0) + _dg(lo, s, 1, 0))


def mm3(a, b):
    ah, bh = a.astype(BF16), b.astype(BF16)
    al, bl = (a - ah.astype(F32)).astype(BF16), (b - bh.astype(F32)).astype(BF16)
    return _dg(ah, bh, 1, 0) + (_dg(ah, bl, 1, 0) + _dg(al, bh, 1, 0))


def _iota(shape, dim):
    return lax.broadcasted_iota(jnp.int32, shape, dim)


def _chunk_cumsum(x, c):
    pos = _iota(x.shape, 0) & (c - 1)
    s = 1
    while s < c:
        x = x + jnp.where(pos >= s, pltpu.roll(x, s, 0), 0.0)
        s *= 2
    return x


def _chunk_revcumsum(x, c):
    n = x.shape[0]
    pos = _iota(x.shape, 0) & (c - 1)
    s = 1
    while s < c:
        x = x + jnp.where(pos < c - s, pltpu.roll(x, n - s, 0), 0.0)
        s *= 2
    return x


def _sig(x):
    return jax.nn.sigmoid(x)


def _softplus(x):
    return jnp.maximum(x, 0.0) + jnp.log(1.0 + jnp.exp(-jnp.abs(x)))


def _rows(v):
    return jnp.sum(v, axis=0, keepdims=True)


def _lanes(v):
    return jnp.sum(v, axis=1, keepdims=True)


def _sum_all(v):
    return _rows(_lanes(v))


def _cparams(sem):
    return pltpu.CompilerParams(dimension_semantics=sem, vmem_limit_bytes=VMEM_LIMIT)


def rowwise(fn, name, T, tb, row_ins, full_ins, row_outs, acc_outs=(), sp=None):
    nblk = T // tb
    assert nblk * tb == T
    has_sp = sp is not None

    def imap(f):
        return (lambda i, s: f(i, s)) if has_sp else (lambda i: f(i, None))

    in_specs, args = [], []
    for arr, w, cb, halo, off in row_ins:
        if halo == "col":
            in_specs.append(pl.BlockSpec((w, tb), imap(lambda i, s: (0, i))))
            args.append(arr)
            continue
        rowf = off if callable(off) else (lambda i, s, off=off: i + off)
        in_specs.append(pl.BlockSpec((tb, w), imap(lambda i, s, cb=cb, rowf=rowf: (rowf(i, s), cb))))
        args.append(arr)
        if halo == "prev":
            r = tb // 8
            in_specs.append(pl.BlockSpec((8, w), imap(lambda i, s, cb=cb, r=r: (jnp.maximum(i * r - 1, 0), cb))))
            args.append(arr)
        elif halo == "next":
            r, last = tb // 8, T // 8 - 1
            in_specs.append(pl.BlockSpec((8, w), imap(lambda i, s, cb=cb, r=r, last=last:
                                                      (jnp.minimum((i + 1) * r, last), cb))))
            args.append(arr)
    for arr in full_ins:
        in_specs.append(pl.BlockSpec(arr.shape, imap(lambda i, s, nd=arr.ndim: (0,) * nd)))
        args.append(arr)
    n_in, n_ro = len(args), len(row_outs)
    out_shape, out_specs, aliases = [], [], {}
    for k, (w, dt, *dest) in enumerate(row_outs):
        if dest:
            buf, cb = dest
            out_shape.append(jax.ShapeDtypeStruct(buf.shape, buf.dtype))
            out_specs.append(pl.BlockSpec((tb, w), imap(lambda i, s, cb=cb: (i, cb))))
            if not isinstance(buf, jax.ShapeDtypeStruct):
                aliases[len(args) + int(has_sp)] = k
                in_specs.append(ANY)
                args.append(buf)
        elif w < 0:
            out_shape.append(jax.ShapeDtypeStruct((-w, T), dt))
            out_specs.append(pl.BlockSpec((-w, tb), imap(lambda i, s: (0, i))))
        else:
            out_shape.append(jax.ShapeDtypeStruct((T, w), dt))
            out_specs.append(pl.BlockSpec((tb, w), imap(lambda i, s: (i, 0))))
    for shp in acc_outs:
        out_shape.append(jax.ShapeDtypeStruct(shp, F32))
        out_specs.append(pl.BlockSpec(shp, imap(lambda i, s, nd=len(shp): (0,) * nd)))

    def body(*refs):
        i = pl.program_id(0)
        if has_sp:
            sp_ref, refs = refs[0], refs[1:]
            vals = fn(i, nblk, sp_ref, *[r[...] for r in refs[:n_in]])
        else:
            vals = fn(i, nblk, *[r[...] for r in refs[:n_in]])
        outs = refs[n_in + len(aliases):]
        for ref, val in zip(outs[:n_ro], vals[:n_ro]):
            ref[...] = val.astype(ref.dtype)
        for ref, val in zip(outs[n_ro:], vals[n_ro:]):
            @pl.when(i == 0)
            def _(ref=ref, val=val):
                ref[...] = val

            @pl.when(i > 0)
            def _(ref=ref, val=val):
                ref[...] += val

    cparams = _cparams(("arbitrary",) if acc_outs else ("parallel",))
    if has_sp:
        return pl.pallas_call(
            body, name=name, out_shape=out_shape, compiler_params=cparams, input_output_aliases=aliases,
            grid_spec=pltpu.PrefetchScalarGridSpec(num_scalar_prefetch=1, grid=(nblk,), in_specs=in_specs,
                                                   out_specs=out_specs),
        )(sp, *args)
    return pl.pallas_call(
        body, name=name, grid=(nblk,), in_specs=in_specs, out_specs=out_specs, out_shape=out_shape,
        compiler_params=cparams, input_output_aliases=aliases,
    )(*args)


def R(arr, w=None, cb=0, halo=None, off=0):
    return (arr, arr.shape[1] if w is None else w, cb, halo, off)


def RC(arr):
    return (arr, arr.shape[0], 0, "col", 0)


def matmul(name, a, b, form, tm, tn, tk, out_dtypes, epi=None, extras=(), rows=(), into=None, n_acc=0, b_sel=None):
    bs = b.shape if b_sel is None else b_sel[0]
    if form == "nn":
        (M, K), N = a.shape, bs[1]
    elif form == "nt":
        (M, K), N = a.shape, bs[0]
    else:
        (K, M), N = a.shape, bs[1]
    tm, tn, tk = min(tm, M), min(tn, N), min(tk, K)
    assert M % tm == 0 and N % tn == 0 and K % tk == 0, (name, M, N, K, tm, tn, tk)

    def b_spec_of(blk, at):
        if b_sel is None:
            return pl.BlockSpec(blk, lambda i, j, k: at(i, j, k))
        blk3 = b_sel[1]
        assert int(np.prod([d for d in blk3 if d is not None])) == blk[0] * blk[1], (name, blk3, blk)
        return pl.BlockSpec(blk3, lambda i, j, k: b_sel[2](i, j, k))

    if form == "nn":
        a_spec = pl.BlockSpec((tm, tk), lambda i, j, k: (i, k))
        b_spec = b_spec_of((tk, tn), lambda i, j, k: (k, j))
        ca, cb = 1, 0
    elif form == "nt":
        a_spec = pl.BlockSpec((tm, tk), lambda i, j, k: (i, k))
        b_spec = b_spec_of((tn, tk), lambda i, j, k: (j, k))
        ca, cb = 1, 1
    else:
        a_spec = pl.BlockSpec((tk, tm), lambda i, j, k: (k, i))
        b_spec = b_spec_of((tk, tn), lambda i, j, k: (k, j))
        ca, cb = 0, 0
    nk, ne, no = K // tk, len(extras) + len(rows), len(out_dtypes)
    if epi is None:
        epi = lambda acc: (acc,)

    assert n_acc == 0 or tn == N

    def body(a_ref, b_ref, *rest):
        ex, outs, accs, acc = rest[:ne], rest[ne:ne + no], rest[ne + no:ne + no + n_acc], rest[ne + no + n_acc]
        i, k = pl.program_id(0), pl.program_id(2)

        def finish(total):
            vals = epi(total, *[e[...] for e in ex])
            for r, v in zip(outs, vals[:no]):
                r[...] = v.astype(r.dtype).reshape(r.shape)
            for r, v in zip(accs, vals[no:]):
                @pl.when(i == 0)
                def _(r=r, v=v):
                    r[...] = v

                @pl.when(i > 0)
                def _(r=r, v=v):
                    r[...] += v

        b_tile = b_ref[...]
        prod = _dg(_bf(a_ref[...]), _bf(b_tile.reshape(-1, b_tile.shape[-1])), ca, cb)
        if nk == 1:
            finish(prod)
            return

        @pl.when(k == 0)
        def _():
            acc[...] = prod

        @pl.when(k > 0)
        def _():
            acc[...] += prod

        @pl.when(k == nk - 1)
        def _():
            finish(acc[...])

    mn = pl.BlockSpec((tm, tn), lambda i, j, k: (i, j))
    rw = pl.BlockSpec((1, tn), lambda i, j, k: (0, j))
    acc_scratch = pltpu.VMEM((tm, tn) if nk > 1 else (8, 128), F32)
    if into is not None:
        buf, blk, bmap = into
        assert ne == 0 and no == 1
        aliased = not isinstance(buf, jax.ShapeDtypeStruct)

        def body_into(a_ref, b_ref, *rest):
            body(a_ref, b_ref, *rest[-2:])

        return pl.pallas_call(
            body_into, name=name, grid=(M // tm, N // tn, nk),
            in_specs=[a_spec, b_spec] + ([ANY] if aliased else []), out_specs=pl.BlockSpec(blk, bmap),
            out_shape=jax.ShapeDtypeStruct(buf.shape, buf.dtype),
            scratch_shapes=[acc_scratch],
            input_output_aliases={2: 0} if aliased else {},
            compiler_params=_cparams(("parallel", "parallel", "arbitrary")),
        )(a, b, *([buf] if aliased else []))
    return pl.pallas_call(
        body, name=name, grid=(M // tm, N // tn, nk),
        in_specs=[a_spec, b_spec] + [mn] * len(extras) + [rw] * len(rows), out_specs=[mn] * no + [rw] * n_acc,
        out_shape=[jax.ShapeDtypeStruct((M, N), dt) for dt in out_dtypes] + [jax.ShapeDtypeStruct((1, N), F32)] * n_acc,
        scratch_shapes=[acc_scratch],
        compiler_params=_cparams(("arbitrary",) * 3 if n_acc else ("parallel", "parallel", "arbitrary")),
    )(a, b, *extras, *rows)


def _epi_res(acc, res):
    return (res + acc,)


def _epi_rms_bwd(acc, x, dres, w):
    return rms_bwd_fn(0, 0, x, acc, dres, w)


def rms_bwd1_fn(i, n, x, dh, dres, w):
    dx, _, gw = rms_bwd_fn(i, n, x, dh, dres, w)
    return dx, gw


def _epi_final(acc, res, tgt, w):
    return final_fn(0, 0, res + acc, tgt, w)


def _epi_res_rms(acc, res, w):
    x = res + acc
    return (x, x * lax.rsqrt(jnp.mean(x * x, axis=-1, keepdims=True) + EPS) * w)


def _epi_relu2(acc):
    u = jnp.maximum(acc, 0.0)
    return (u, u * u)


def _epi_dup(acc, u):
    return (acc * 2.0 * u.astype(F32),)


def _conv(x, halo, w, i):
    halo = jnp.where(i == 0, 0.0, halo)
    xt = jnp.concatenate([halo, x], axis=0)
    shifted = [pltpu.roll(xt, 3 - k, 0)[8:, :] for k in range(3)] + [x]
    y = shifted[3] * w[3:4, :]
    for k in range(3):
        y = y + shifted[k] * w[k:k + 1, :]
    return y, shifted


def _l2n(x, scale):
    outs = []
    for h in range(x.shape[1] // 128):
        xh = x[:, 128 * h:128 * h + 128]
        outs.append(xh * (lax.rsqrt(jnp.sum(xh * xh, axis=-1, keepdims=True) + EPS) * scale))
    return jnp.concatenate(outs, axis=1)


def _l2n_bwd(x, dy, scale):
    outs = []
    for h in range(x.shape[1] // 128):
        xh, dh = x[:, 128 * h:128 * h + 128], dy[:, 128 * h:128 * h + 128] * scale
        r = lax.rsqrt(jnp.sum(xh * xh, axis=-1, keepdims=True) + EPS)
        outs.append(r * dh - xh * (r * r * r) * jnp.sum(xh * dh, axis=-1, keepdims=True))
    return jnp.concatenate(outs, axis=1)


def rms_fwd_fn(i, n, x, w):
    r = lax.rsqrt(jnp.mean(x * x, axis=-1, keepdims=True) + EPS)
    return (x * r * w,)


def rms_bwd_fn(i, n, x, dh, dres, w):
    r = lax.rsqrt(jnp.mean(x * x, axis=-1, keepdims=True) + EPS)
    g = dh * w
    dx = dres + r * g - x * (r * r * r) * jnp.mean(x * g, axis=-1, keepdims=True)
    return dx, dx, _rows(dh * x * r)


def rms_bwd_w_fn(i, n, x, dh, w):
    r = lax.rsqrt(jnp.mean(x * x, axis=-1, keepdims=True) + EPS)
    return (_rows(dh * x * r),)


def final_fn(i, n, x, tgt, w):
    r = lax.rsqrt(jnp.mean(x * x, axis=-1, keepdims=True) + EPS)
    xn = x * r
    e = xn * w - tgt
    dy = e * (1.0 / D)
    g = dy * w
    dx = r * g - x * (r * r * r) * jnp.mean(x * g, axis=-1, keepdims=True)
    return dx, dx, _rows(e * e), _rows(dy * xn)


def _gdn_gates(ba, alog_c, dtb_c):
    col = _iota(ba.shape, 1)
    amask = (col >= 8) & (col < 16)
    beta = jnp.where(col < 8, _sig(ba), 0.0)
    z = ba + dtb_c
    ea_ = jnp.exp(alog_c)
    return beta, z, ea_, jnp.where(amask, -ea_ * _softplus(z), 0.0), amask


def _cols(x, g):
    return x[:, 128 * g:128 * g + 128]


def gdn_prep_fn(i, n, qkv, halo, ba, cw, alog_c, dtb_c, eb, ea):
    outs = [[], [], []]
    for g in range(3 * GDN_H):
        yc, _ = _conv(_cols(qkv, g), _cols(halo, g), _cols(cw, g), i)
        act = yc * _sig(yc)
        if g < 2 * GDN_H:
            act = _l2n(act, GDN_DK ** -0.5 if g < GDN_H else 1.0)
        outs[g // GDN_H].append(act)
    beta, _, _, gg, _ = _gdn_gates(ba, alog_c, dtb_c)
    gcs = _chunk_cumsum(gg, GDN_C)
    return (*[jnp.concatenate(o, axis=1) for o in outs], mm_sel(gcs, ea), mm_sel(beta, eb), jnp.transpose(gcs)[8:16, :])


def gdn_prep_bwd_fn(i, n, qkv, halo, ba, dqn, dkn, dv, dgcs_x, dbeta_x, dgcs_t, cw, alog_c, dtb_c, pb, pa):
    dycs, dwl = [], [[], [], [], []]
    for g in range(3 * GDN_H):
        yc, shifted = _conv(_cols(qkv, g), _cols(halo, g), _cols(cw, g), i)
        sg = _sig(yc)
        act = yc * sg
        if g < GDN_H:
            d = _l2n_bwd(act, _cols(dqn, g), GDN_DK ** -0.5)
        elif g < 2 * GDN_H:
            d = _l2n_bwd(act, _cols(dkn, g - GDN_H), 1.0)
        else:
            d = _cols(dv, g - 2 * GDN_H)
        dyc_g = d * (sg * (1.0 + yc * (1.0 - sg)))
        dycs.append(dyc_g)
        for k in range(4):
            dwl[k].append(_rows(dyc_g * shifted[k]))
    dyc = jnp.concatenate(dycs, axis=1)
    dws = [jnp.concatenate(l, axis=1) for l in dwl]
    beta, z, ea_, g, amask = _gdn_gates(ba, alog_c, dtb_c)
    tbn = ba.shape[0]
    rowpart = jnp.transpose(jnp.concatenate([jnp.zeros((8, tbn), F32), dgcs_t, jnp.zeros((112, tbn), F32)], axis=0))
    dg = _chunk_revcumsum(mm_sel(dgcs_x, pa) - rowpart, GDN_C)
    draw = jnp.where(amask, dg * (-ea_) * _sig(z), 0.0)
    dba = draw + mm_sel(dbeta_x, pb) * beta * (1.0 - beta)
    return (dyc, dba, dws[0], dws[1], dws[2], dws[3], _rows(dg * g), _rows(draw))


def conv_bwd_fn(i, n, dyc, halo, w):
    halo = jnp.where(i == n - 1, 0.0, halo)
    tb = dyc.shape[0]
    outs = []
    for g in range(dyc.shape[1] // 128):
        d, wg = _cols(dyc, g), _cols(w, g)
        xt = jnp.concatenate([d, _cols(halo, g)], axis=0)
        dx = d * wg[3:4, :]
        for k in range(3):
            dx = dx + pltpu.roll(xt, tb + 8 - (3 - k), 0)[:tb, :] * wg[k:k + 1, :]
        outs.append(dx)
    return (jnp.concatenate(outs, axis=1),)


def gdn_post_fn(i, n, o, z, w):
    outs = []
    for h in range(GDN_H):
        oh, zh = o[:, 128 * h:128 * h + 128], z[:, 128 * h:128 * h + 128]
        r = lax.rsqrt(jnp.mean(oh * oh, axis=-1, keepdims=True) + EPS)
        outs.append(oh * r * w * (zh * _sig(zh)))
    return (jnp.concatenate(outs, axis=1),)


def gdn_post_bwd_fn(i, n, o, z, doa, w):
    dos, dzs, dw = [], [], None
    for h in range(GDN_H):
        sl = slice(128 * h, 128 * h + 128)
        oh, zh, dh = o[:, sl], z[:, sl], doa[:, sl]
        r = lax.rsqrt(jnp.mean(oh * oh, axis=-1, keepdims=True) + EPS)
        s = _sig(zh)
        dn = dh * (zh * s)
        dzs.append(dh * (oh * r * w) * (s * (1.0 + zh * (1.0 - s))))
        t = _rows(dn * oh * r)
        dw = t if dw is None else dw + t
        g = dn * w
        dos.append(r * g - oh * (r * r * r) * jnp.mean(oh * g, axis=-1, keepdims=True))
    return jnp.concatenate(dos, axis=1), jnp.concatenate(dzs, axis=1), dw


def _ssd_gates(dtblk, alog_c, dtb_c):
    hmask = _iota(dtblk.shape, 1) < SSM_H
    z = dtblk + dtb_c
    return jnp.where(hmask, _softplus(z), 0.0), -jnp.exp(alog_c), z, hmask


def _silu_conv_cols(x, halo, w, b, i):
    outs = []
    for g in range(x.shape[1] // 128):
        yc, _ = _conv(_cols(x, g), _cols(halo, g), _cols(w, g), i)
        yc = yc + _cols(b, g)
        outs.append(yc * _sig(yc))
    return jnp.concatenate(outs, axis=1)


def _silu_conv_bwd_cols(x, halo, w, b, dout, i):
    dycs, dwl = [], [[], [], [], []]
    for g in range(x.shape[1] // 128):
        yc, shifted = _conv(_cols(x, g), _cols(halo, g), _cols(w, g), i)
        yc = yc + _cols(b, g)
        s = _sig(yc)
        dyc_g = _cols(dout, g) * (s * (1.0 + yc * (1.0 - s)))
        dycs.append(dyc_g)
        for k in range(4):
            dwl[k].append(_rows(dyc_g * shifted[k]))
    dyc = jnp.concatenate(dycs, axis=1)
    return dyc, [jnp.concatenate(l, axis=1) for l in dwl], _rows(dyc)


def ssd_prep_fn(i, n, xp, hx, bcp, hbc, dtblk, cwx, cwbc, cbx, cbbc, alog_c, dtb_c, e16):
    dt, a_neg, _, _ = _ssd_gates(dtblk, alog_c, dtb_c)
    acs = _chunk_cumsum(dt * a_neg, SSM_L)
    return (_silu_conv_cols(xp, hx, cwx, cbx, i), _silu_conv_cols(bcp, hbc, cwbc, cbbc, i), mm_sel(dt, e16),
            mm_sel(acs, e16), jnp.transpose(acs)[0:SSM_H, :])


def ssd_prep_bwd_fn(i, n, xp, hx, bcp, hbc, dtblk, dxs_a, dxs_b, db, dc, dgate, dacs_t, cwx, cwbc, cbx, cbbc, alog_c, dtb_c):
    dyx, dwx, dbx = _silu_conv_bwd_cols(xp, hx, cwx, cbx, dxs_a + dxs_b, i)
    dybc, dwbc, dbbc = _silu_conv_bwd_cols(bcp, hbc, cwbc, cbbc, jnp.concatenate([db, dc], axis=1), i)
    dt, a_neg, z, hmask = _ssd_gates(dtblk, alog_c, dtb_c)
    g0, g1 = dgate[:, :128], dgate[:, 128:]
    col = _iota(g0.shape, 1)
    lo, mid = col < 8, (col >= 8) & (col < 16)
    dacs_col = jnp.where(lo, g0, 0.0) + pltpu.roll(jnp.where(lo, g1, 0.0), 8, 1)
    ddt_dir = pltpu.roll(jnp.where(mid, g0, 0.0), 120, 1) + jnp.where(mid, g1, 0.0)
    tbn = dtblk.shape[0]
    rowpart = jnp.transpose(jnp.concatenate([dacs_t, jnp.zeros((128 - SSM_H, tbn), F32)], axis=0))
    da = _chunk_revcumsum(dacs_col - rowpart, SSM_L)
    draw = jnp.where(hmask, (ddt_dir + da * a_neg) * _sig(z), 0.0)
    return (dyx, dybc, draw, *dwx, *dwbc, dbx, dbbc, _rows(da * dt * a_neg), _rows(draw))


def _ssd_gate(y, xs, zs, d_x):
    y2 = y + xs * d_x
    s = _sig(zs)
    return y2, s, y2 * (zs * s)


def ssd_post_fn(i, n, y, xs, zs, d_x, nw):
    _, _, yg = _ssd_gate(y, xs, zs, d_x)
    outs = []
    for g in range(2):
        v = yg[:, 512 * g:512 * g + 512]
        outs.append(v * lax.rsqrt(jnp.mean(v * v, axis=-1, keepdims=True) + EPS))
    return (jnp.concatenate(outs, axis=1) * nw,)


def ssd_post_bwd_fn(i, n, y, xs, zs, dob, d_x, nw):
    y2, s, yg = _ssd_gate(y, xs, zs, d_x)
    gfull = dob * nw
    dygs, dnw = [], []
    for g in range(2):
        sl = slice(512 * g, 512 * g + 512)
        v, gg = yg[:, sl], gfull[:, sl]
        r = lax.rsqrt(jnp.mean(v * v, axis=-1, keepdims=True) + EPS)
        dygs.append(r * gg - v * (r * r * r) * jnp.mean(v * gg, axis=-1, keepdims=True))
        dnw.append(_rows(dob[:, sl] * v * r))
    dyg = jnp.concatenate(dygs, axis=1)
    dy2 = dyg * (zs * s)
    dzs = dyg * y2 * (s * (1.0 + zs * (1.0 - s)))
    return dy2, dy2 * d_x, dzs, jnp.concatenate(dnw, axis=1), _rows(dy2 * xs)


def _attn_probs(q, k):
    hs = [slice(MEM_DH * h, MEM_DH * h + MEM_DH) for h in range(MEM_H)]
    ss = [mm_nt(q[:, sl], k[:, sl]) * (MEM_DH ** -0.5) for sl in hs]
    es = [jnp.exp(s - jnp.max(s, axis=-1, keepdims=True)) for s in ss]
    return hs, [e / jnp.sum(e, axis=-1, keepdims=True) for e in es]


def attn_fn(i, n, q, k, v):
    hs, ps = _attn_probs(q, k)
    return (jnp.concatenate([mm(p, v[:, sl]) for p, sl in zip(ps, hs)], axis=1),)


def attn_bwd_fn(i, n, q, do, k, v):
    hs, ps = _attn_probs(q, k)
    dvs = [mm_tn(p, do[:, sl]) for p, sl in zip(ps, hs)]
    dps = [mm_nt(do[:, sl], v[:, sl]) for sl in hs]
    dss = [p * (dp - jnp.sum(dp * p, axis=-1, keepdims=True)) * (MEM_DH ** -0.5) for p, dp in zip(ps, dps)]
    dqs = [mm(ds, k[:, sl]) for ds, sl in zip(dss, hs)]
    dks = [mm_tn(ds, q[:, sl]) for ds, sl in zip(dss, hs)]
    return jnp.concatenate(dqs, axis=1), jnp.concatenate(dks, axis=1), jnp.concatenate(dvs, axis=1)


def add2_fn(i, n, sp, a, b):
    return (a + b,)


def sum4_fn(i, n, sp, a, b, c, d):
    return (((a.astype(F32) + b.astype(F32)) + c.astype(F32)) + d.astype(F32),)


def _adamw(w, g, m, v):
    m = ADAM_B1 * m + (1.0 - ADAM_B1) * g
    v = ADAM_B2 * v + (1.0 - ADAM_B2) * (g * g)
    m_hat = m / (1.0 - ADAM_B1 ** ADAM_STEP)
    v_hat = v / (1.0 - ADAM_B2 ** ADAM_STEP)
    delta = -ADAM_LR * (m_hat / (jnp.sqrt(v_hat) + ADAM_EPS) + ADAM_WD * w)
    return delta, m, v


def _gdn_stage1(q, k, v, gcs, grow, bb):
    C = GDN_C
    row, col = _iota((C, C), 0), _iota((C, C), 1)
    incl, strict = row >= col, row > col
    dmat = jnp.where(incl, jnp.exp(jnp.minimum(gcs[:, :C] - grow, 0.0)), 0.0)
    gam = jnp.exp(gcs)
    gl = gcs[C - 1:C, :]
    kb, vb = k * bb, v * bb
    kg = kb * gam
    lmat = jnp.where(strict, mm_nt(kb, k) * dmat, 0.0)
    pmat = jnp.where(incl, mm_nt(q, k) * dmat, 0.0)
    return dict(q=q, k=k, v=v, bb=bb, incl=incl, strict=strict, dmat=dmat, gam=gam, kb=kb, vb=vb, kg=kg,
                lmat=lmat, pmat=pmat, qd=q * gam, kdec=jnp.exp(gl - gcs), cd=jnp.exp(gl))


def _gdn_inverse(lmats):
    C = GDN_C
    eye = (_iota((C, C), 0) == _iota((C, C), 1)).astype(F32)
    xs = [-l for l in lmats]
    ts = [eye + x for x in xs]
    for _ in range(5):
        xs = [mm(x, x) for x in xs]
        ts = [t + mm(t, x) for t, x in zip(ts, xs)]
    res = [eye - mm3(eye + l, t) for l, t in zip(lmats, ts)]
    return [t + mm(t, r) for t, r in zip(ts, res)]


def gdn_fwd(qn, kn, v, gcs_x, beta_x, gcs_t, tb, gh):
    T = qn.shape[0]
    nb, ncb, nc, C = T // tb, tb // GDN_C, T // GDN_C, GDN_C
    idx = [(hh, c) for hh in range(gh) for c in range(ncb)]

    def body(q_ref, k_ref, v_ref, g_ref, b_ref, gt_ref, o_ref, st_ref, ti_ref, s_scr):
        @pl.when(pl.program_id(1) == 0)
        def _():
            s_scr[...] = jnp.zeros_like(s_scr)

        grows = [gt_ref[hh] for hh in range(gh)]
        at = lambda hh, c: (slice(C * c, C * (c + 1)), slice(128 * hh, 128 * hh + 128))
        st1 = []
        for hh, c in idx:
            sl, ln = at(hh, c)
            st1.append(_gdn_stage1(q_ref[sl, ln], k_ref[sl, ln], v_ref[sl, ln], g_ref[sl, ln], grows[hh][:, sl],
                                   b_ref[sl, ln]))
        tinvs = _gdn_inverse([s["lmat"] for s in st1])
        us = [mm(t, s["vb"]) for t, s in zip(tinvs, st1)]
        ws = [mm(t, s["kg"]) for t, s in zip(tinvs, st1)]
        kds = [s["k"] * s["kdec"] for s in st1]
        ms = [mm_tn(kd, w) for kd, w in zip(kds, ws)]
        bs = [mm_tn(kd, u) for kd, u in zip(kds, us)]
        gs = [s["qd"] - mm(s["pmat"], w) for s, w in zip(st1, ws)]
        pus = [mm(s["pmat"], u) for s, u in zip(st1, us)]
        ss = [s_scr[hh] for hh in range(gh)]
        for c in range(ncb):
            for hh in range(gh):
                n, (sl, ln) = hh * ncb + c, at(hh, c)
                ti_ref[hh, sl, :] = tinvs[n]
                st_ref[hh, c] = ss[hh]
                o_ref[sl, ln] = mm(gs[n], ss[hh]) + pus[n]
                ss[hh] = st1[n]["cd"] * ss[hh] - mm(ms[n], ss[hh]) + bs[n]
        for hh in range(gh):
            s_scr[hh] = ss[hh]

    blk = pl.BlockSpec((tb, 128 * gh), lambda h, i: (i, h))
    return pl.pallas_call(
        body, name="gdn_fwd", grid=(GDN_H // gh, nb),
        in_specs=[blk] * 5 + [pl.BlockSpec((gh, 1, tb), lambda h, i: (h, 0, i))],
        out_specs=[blk, pl.BlockSpec((gh, ncb, 128, 128), lambda h, i: (h, i, 0, 0)),
                   pl.BlockSpec((gh, tb, C), lambda h, i: (h, i, 0))],
        out_shape=[jax.ShapeDtypeStruct((T, D), F32), jax.ShapeDtypeStruct((GDN_H, nc, 128, 128), F32),
                   jax.ShapeDtypeStruct((GDN_H, T, C), F32)],
        scratch_shapes=[pltpu.VMEM((gh, 128, 128), F32)],
        compiler_params=_cparams(("parallel", "arbitrary")),
    )(qn, kn, v, gcs_x, beta_x, gcs_t)


def gdn_bwd(qn, kn, v, gcs_x, beta_x, gcs_t, do, states, tinv, tb, gh):
    T = qn.shape[0]
    nb, ncb, C = T // tb, tb // GDN_C, GDN_C

    def body(q_ref, k_ref, v_ref, g_ref, b_ref, gt_ref, do_ref, st_ref, ti_ref,
             dq_ref, dk_ref, dv_ref, dgc_ref, db_ref, dgr_ref, ds_scr):
        @pl.when(pl.program_id(1) == 0)
        def _():
            ds_scr[...] = jnp.zeros_like(ds_scr)

        grows = [gt_ref[hh] for hh in range(gh)]
        at = lambda hh, c: (slice(C * c, C * (c + 1)), slice(128 * hh, 128 * hh + 128))
        lastrow = _iota((C, 1), 0) == C - 1
        idx = [(hh, c) for hh in range(gh) for c in range(ncb)]
        P = []
        for hh, c in idx:
            sl, ln = at(hh, c)
            lc = _gdn_stage1(q_ref[sl, ln], k_ref[sl, ln], v_ref[sl, ln], g_ref[sl, ln], grows[hh][:, sl],
                             b_ref[sl, ln])
            lc.update(tinv=ti_ref[hh, sl, :], s=st_ref[hh, c], do=do_ref[sl, ln], kd=lc["k"] * lc["kdec"])
            P.append(lc)
        for l, u, w in zip(P, [mm(l["tinv"], l["vb"]) for l in P], [mm(l["tinv"], l["kg"]) for l in P]):
            l.update(u=u, w=w)
        for l, x in zip(P, [mm(l["w"], l["s"]) for l in P]):
            l["vn"] = l["u"] - x
        for l, a, b, c_, d in zip(P, [mm_nt(l["do"], l["s"]) for l in P], [mm_nt(l["do"], l["vn"]) for l in P],
                                  [mm_tn(l["qd"], l["do"]) for l in P], [mm_tn(l["pmat"], l["do"]) for l in P]):
            l.update(dqd=a, dp=jnp.where(l["incl"], b, 0.0), ds_q=c_, dvn_p=d)
        pre = dict(zip(idx, P))
        rows = {}
        hs = range(gh)
        ds = [ds_scr[hh] for hh in hs]
        for c in reversed(range(ncb)):
            L = [pre[hh, c] for hh in hs]
            dvn = [l["dvn_p"] + mm(l["kd"], d) for l, d in zip(L, ds)]
            dkd = [mm_nt(l["vn"], d) for l, d in zip(L, ds)]
            dcd = [_sum_all(l["s"] * d) for l, d in zip(L, ds)]
            ds = [l["ds_q"] + l["cd"] * d - mm_tn(l["w"], x) for l, d, x in zip(L, ds, dvn)]
            dw = [-mm_nt(x, l["s"]) for l, x in zip(L, dvn)]
            dvb = [mm_tn(l["tinv"], x) for l, x in zip(L, dvn)]
            dkg = [mm_tn(l["tinv"], x) for l, x in zip(L, dw)]
            da = [-jnp.where(l["strict"], mm_nt(a, l["u"]) + mm_nt(b, l["w"]), 0.0) for l, a, b in zip(L, dvb, dkg)]
            dm = [a * l["dmat"] for l, a in zip(L, da)]
            dn = [l["dp"] * l["dmat"] for l in L]
            dkb = [mm(a, l["k"]) for l, a in zip(L, dm)]
            dq = [mm(a, l["k"]) + l["gam"] * l["dqd"] for l, a in zip(L, dn)]
            dk = [mm_tn(a, l["kb"]) + mm_tn(b, l["q"]) for l, a, b in zip(L, dm, dn)]
            for hh in hs:
                sl, ln = at(hh, c)
                l = L[hh]
                e = da[hh] * l["lmat"] + l["dp"] * l["pmat"]
                t_kd = _lanes(dkd[hh] * l["kd"])
                dgl = _sum_all(t_kd) + dcd[hh] * l["cd"][:, :1]
                dgcs = (_lanes(e) + _lanes(l["dqd"] * l["qd"]) - t_kd + _lanes(dkg[hh] * l["kg"])
                        + jnp.where(lastrow, dgl, 0.0))
                rows[hh, c] = _rows(e)
                dq_ref[sl, ln] = dq[hh]
                dk_ref[sl, ln] = (dk[hh] + l["kdec"] * dkd[hh] + l["bb"] * l["gam"] * dkg[hh] + l["bb"] * dkb[hh])
                dv_ref[sl, ln] = l["bb"] * dvb[hh]
                dbeta = _lanes(dkg[hh] * l["gam"] * l["k"]) + _lanes(dvb[hh] * l["v"]) + _lanes(dkb[hh] * l["k"])
                db_ref[sl, ln] = jnp.broadcast_to(dbeta, (C, 128))
                dgc_ref[sl, ln] = jnp.broadcast_to(dgcs, (C, 128))
        for hh in hs:
            ds_scr[hh] = ds[hh]
            dgr_ref[hh] = jnp.concatenate([rows[hh, c] for c in range(ncb)], axis=1)

    blk = pl.BlockSpec((tb, 128 * gh), lambda h, i: (nb - 1 - i, h))
    rowspec = pl.BlockSpec((gh, 1, tb), lambda h, i: (h, 0, nb - 1 - i))
    return pl.pallas_call(
        body, name="gdn_bwd", grid=(GDN_H // gh, nb),
        in_specs=[blk] * 5 + [rowspec, blk,
                              pl.BlockSpec((gh, ncb, 128, 128), lambda h, i: (h, nb - 1 - i, 0, 0)),
                              pl.BlockSpec((gh, tb, C), lambda h, i: (h, nb - 1 - i, 0))],
        out_specs=[blk] * 5 + [rowspec],
        out_shape=[jax.ShapeDtypeStruct((T, D), F32)] * 5 + [jax.ShapeDtypeStruct((GDN_H, 1, T), F32)],
        scratch_shapes=[pltpu.VMEM((gh, 128, 128), F32)],
        compiler_params=_cparams(("parallel", "arbitrary")),
    )(qn, kn, v, gcs_x, beta_x, gcs_t, do, states, tinv)


def _ssd_pair(x2, dt2, acs2):
    last = acs2[SSM_L - 1:SSM_L, :]
    return jnp.exp(acs2), jnp.exp(last - acs2), x2 * dt2


def _ssd_head(hh, acs2, arow, dec2, cbm, bm, incl, col):
    lmask = (col >= 64 * hh) & (col < 64 * hh + 64)
    sg = jnp.where(incl, jnp.exp(jnp.minimum(acs2[:, 64 * hh:64 * hh + 1] - arow, 0.0)), 0.0)
    dec_col = dec2[:, 64 * hh:64 * hh + 1]
    return lmask, sg, sg * cbm, dec_col, bm * dec_col


def ssd_fwd(xs, bc, dt_x, acs_x, acs_t):
    T = xs.shape[0]
    nc, L = T // SSM_L, SSM_L

    def body(x_ref, bc_ref, dt_ref, ac_ref, at_ref, y_ref, hst_ref, h_scr):
        @pl.when(pl.program_id(0) == 0)
        def _():
            h_scr[...] = jnp.zeros_like(h_scr)

        row, col = _iota((L, L), 0), _iota((L, L), 1)
        incl = row >= col
        P, H = [], []
        for gp in range(8):
            g = gp // 4
            bm, cm = bc_ref[:, 128 * g:128 * g + 128], bc_ref[:, 256 + 128 * g:384 + 128 * g]
            cbm = mm_nt(cm, bm) if gp % 4 == 0 else cbm
            sl = slice(128 * gp, 128 * gp + 128)
            acs2 = ac_ref[:, sl]
            lam2, dec2, xd2 = _ssd_pair(x_ref[:, sl], dt_ref[:, sl], acs2)
            P.append(dict(sl=sl, lam2=lam2, xd2=xd2, hprev=h_scr[gp], cm=cm))
            for hh in range(2):
                lmask, _, mmat, _, bd = _ssd_head(hh, acs2, at_ref[2 * gp + hh], dec2, cbm, bm, incl, col)
                H.append(dict(mmat=mmat, bd=bd, xdh=jnp.where(lmask, xd2, 0.0), xd2=xd2))
        ys = [mm(h["mmat"], h["xdh"]) for h in H]
        sts = [mm_tn(h["xd2"], h["bd"]) for h in H]
        zs = [mm_nt(p["cm"], p["hprev"]) for p in P]
        for gp, p in enumerate(P):
            hst_ref[gp // 4, gp % 4] = p["hprev"]
            y_ref[:, p["sl"]] = ys[2 * gp] + ys[2 * gp + 1] + p["lam2"] * zs[gp]
            lam_rows = jnp.where(row < 64, p["lam2"][L - 1:L, 0:1], p["lam2"][L - 1:L, 64:65])
            h_scr[gp] = lam_rows * p["hprev"] + jnp.where(row < 64, sts[2 * gp], sts[2 * gp + 1])

    blk = pl.BlockSpec((L, D), lambda c: (c, 0))
    return pl.pallas_call(
        body, name="ssd_fwd", grid=(nc,),
        in_specs=[blk, pl.BlockSpec((L, 512), lambda c: (c, 0)), blk, blk, pl.BlockSpec((SSM_H, 1, L), lambda c: (0, 0, c))],
        out_specs=[blk, pl.BlockSpec((2, None, 4, 128, 128), lambda c: (0, c, 0, 0, 0))],
        out_shape=[jax.ShapeDtypeStruct((T, D), F32), jax.ShapeDtypeStruct((2, nc, 4, 128, 128), F32)],
        scratch_shapes=[pltpu.VMEM((8, 128, 128), F32)],
        compiler_params=_cparams(("arbitrary",)),
    )(xs, bc, dt_x, acs_x, acs_t)


def ssd_bwd(xs, bc, dt_x, acs_x, acs_t, dy, hstates):
    T = xs.shape[0]
    nc, L = T // SSM_L, SSM_L

    def body(x_ref, bc_ref, dt_ref, ac_ref, at_ref, dy_ref, hst_ref,
             dx_ref, db_ref, dc_ref, dgate_ref, dar_ref, dh_scr):
        @pl.when(pl.program_id(0) == 0)
        def _():
            dh_scr[...] = jnp.zeros_like(dh_scr)

        row, col = _iota((L, L), 0), _iota((L, L), 1)
        rowc = _iota((L, 1), 0)
        incl = row >= col
        G = [dict(bm=bc_ref[:, 128 * g:128 * g + 128], cm=bc_ref[:, 256 + 128 * g:384 + 128 * g]) for g in range(2)]
        for gr in G:
            gr["cbm"] = mm_nt(gr["cm"], gr["bm"])
        P = []
        for gp in range(8):
            sl = slice(128 * gp, 128 * gp + 128)
            x2, dt2, dy2, acs2 = x_ref[:, sl], dt_ref[:, sl], dy_ref[:, sl], ac_ref[:, sl]
            lam2, dec2, xd2 = _ssd_pair(x2, dt2, acs2)
            P.append(dict(sl=sl, gr=G[gp // 4], x2=x2, dt2=dt2, dy2=dy2, acs2=acs2, lam2=lam2, dec2=dec2, xd2=xd2,
                          hprev=hst_ref[gp // 4, gp % 4], dhn=dh_scr[gp], dz=lam2 * dy2))
        zs = [mm_nt(p["gr"]["cm"], p["hprev"]) for p in P]
        dcm_t = [mm(p["dz"], p["hprev"]) for p in P]
        dh_z = [mm_tn(p["dz"], p["gr"]["cm"]) for p in P]
        H = []
        for gp, p in enumerate(P):
            p["yoff"] = p["dz"] * zs[gp]
            p["q_rows"] = _lanes(p["dhn"] * p["hprev"])
            for hh in range(2):
                lmask, sg, mmat, dec_col, bd = _ssd_head(hh, p["acs2"], at_ref[2 * gp + hh], p["dec2"], p["gr"]["cbm"],
                                                         p["gr"]["bm"], incl, col)
                H.append(dict(p=p, hh=hh, j=2 * gp + hh, lmask=lmask, sg=sg, mmat=mmat, dec_col=dec_col, bd=bd))
        dms = [mm_nt(jnp.where(h["lmask"], h["p"]["dy2"], 0.0), h["p"]["xd2"]) for h in H]
        a1s = [mm_tn(h["mmat"], h["p"]["dy2"]) for h in H]
        a2s = [mm_nt(h["bd"], h["p"]["dhn"]) for h in H]
        dbds = [mm(jnp.where(h["lmask"], h["p"]["xd2"], 0.0), h["p"]["dhn"]) for h in H]
        for gr in G:
            gr.update(dcb=jnp.zeros((L, L), F32), dbm=jnp.zeros((L, SSM_N), F32), comp=jnp.zeros((L, 128), F32))
        dxd = [jnp.zeros((L, 128), F32) for _ in P]
        for h, dm_raw, a1, a2, dbd in zip(H, dms, a1s, a2s, dbds):
            p, hh, j = h["p"], h["hh"], h["j"]
            gr, jg = p["gr"], j % 8
            dm = jnp.where(incl, dm_raw, 0.0)
            gr["dcb"] = gr["dcb"] + dm * h["sg"]
            e = dm * h["mmat"]
            dxd_h = jnp.where(h["lmask"], a1 + a2, 0.0)
            dxd[j // 2] = dxd[j // 2] + dxd_h
            gr["dbm"] = gr["dbm"] + h["dec_col"] * dbd
            t = _lanes(dbd * h["bd"])
            lam_h = p["lam2"][L - 1:L, 64 * hh:64 * hh + 1]
            in_head = (rowc >= 64 * hh) & (rowc < 64 * hh + 64)
            add_last = _sum_all(t) + _sum_all(jnp.where(in_head, p["q_rows"], 0.0)) * lam_h
            dacs_col = (_lanes(jnp.where(h["lmask"], p["yoff"], 0.0)) + _lanes(e) - t
                        + jnp.where(rowc == L - 1, add_last, 0.0))
            ddt_col = _lanes(dxd_h * p["x2"])
            dar_ref[j] = _rows(e)
            gr["comp"] = gr["comp"] + jnp.where(col == jg, dacs_col, 0.0) + jnp.where(col == 8 + jg, ddt_col, 0.0)
        for gp, p in enumerate(P):
            lam_rows = jnp.where(row < 64, p["lam2"][L - 1:L, 0:1], p["lam2"][L - 1:L, 64:65])
            dh_scr[gp] = dh_z[gp] + lam_rows * p["dhn"]
            dx_ref[:, p["sl"]] = p["dt2"] * dxd[gp]
        for g, gr in enumerate(G):
            lanes = slice(128 * g, 128 * g + 128)
            dcm = (dcm_t[4 * g] + dcm_t[4 * g + 1]) + (dcm_t[4 * g + 2] + dcm_t[4 * g + 3])
            db_ref[:, lanes] = gr["dbm"] + mm_tn(gr["dcb"], gr["cm"])
            dc_ref[:, lanes] = dcm + mm(gr["dcb"], gr["bm"])
            dgate_ref[:, lanes] = gr["comp"]

    rv = lambda c: (nc - 1 - c, 0)
    blk, blk256 = pl.BlockSpec((L, D), rv), pl.BlockSpec((L, 256), rv)
    rowspec = pl.BlockSpec((SSM_H, 1, L), lambda c: (0, 0, nc - 1 - c))
    return pl.pallas_call(
        body, name="ssd_bwd", grid=(nc,),
        in_specs=[blk, pl.BlockSpec((L, 512), rv), blk, blk, rowspec, blk,
                  pl.BlockSpec((2, None, 4, 128, 128), lambda c: (0, nc - 1 - c, 0, 0, 0))],
        out_specs=[blk, blk256, blk256, blk256, rowspec],
        out_shape=[jax.ShapeDtypeStruct((T, D), F32), jax.ShapeDtypeStruct((T, 256), F32),
                   jax.ShapeDtypeStruct((T, 256), F32), jax.ShapeDtypeStruct((T, 256), F32),
                   jax.ShapeDtypeStruct((SSM_H, 1, T), F32)],
        scratch_shapes=[pltpu.VMEM((8, 128, 128), F32)],
        compiler_params=_cparams(("arbitrary",)),
    )(xs, bc, dt_x, acs_x, acs_t, dy, hstates)


def _pos():
    return lax.axis_index("x"), lax.axis_index("y"), lax.axis_index("c")


def _other_chips(x, y):
    return [(1 - x, y), (x, 1 - y), (1 - x, 1 - y)]


def _rcopy(src, dst, ssem, rsem, dev):
    return pltpu.make_async_remote_copy(src_ref=src, dst_ref=dst, send_sem=ssem, recv_sem=rsem,
                                        device_id=dev, device_id_type=MESH)


def _rows_at(start, n):
    return pl.ds(pl.multiple_of(start, 8), n)


def _comm_call(body, name, out_shape, n_in, scratch):
    return pl.pallas_call(
        body, name=name, out_shape=out_shape, in_specs=[ANY] * n_in,
        out_specs=[ANY] * len(out_shape) if isinstance(out_shape, (list, tuple)) else ANY,
        scratch_shapes=scratch,
        compiler_params=pltpu.CompilerParams(has_side_effects=True),
    )


def _dma_sems(n):
    return pltpu.SemaphoreType.DMA((n,))


def ag_chips(name, shard):
    rr, cc = shard.shape
    h, nq = rr // 2, ICI_CHUNKS
    hq = h // nq

    def body(x_ref, out_ref, ssem, rsem):
        x, y, c = _pos()
        me_s = 2 * x + y
        chips = _other_chips(x, y)
        started = []
        for q in range(nq):
            rows = _rows_at(c * h + q * hq, hq)
            for j, (cx, cy) in enumerate(chips):
                cp = _rcopy(x_ref.at[rows], out_ref.at[me_s, rows], ssem.at[j * nq + q], rsem.at[j * nq + q], (cx, cy, c))
                cp.start()
                started.append(cp)
        for q in range(nq):
            rows = _rows_at(c * h + q * hq, hq)
            for j, (cx, cy) in enumerate(chips):
                blk = out_ref.at[2 * cx + cy, rows]
                _rcopy(blk, blk, ssem.at[j * nq + q], rsem.at[j * nq + q], (cx, cy, c)).wait_recv()
                k = 3 * nq + j * nq + q
                cp = _rcopy(blk, blk, ssem.at[k], rsem.at[k], (x, y, 1 - c))
                cp.start()
                started.append(cp)
        for q in range(nq):
            rows = _rows_at((1 - c) * h + q * hq, hq)
            for j, (cx, cy) in enumerate(chips):
                blk = out_ref.at[2 * cx + cy, rows]
                k = 3 * nq + j * nq + q
                _rcopy(blk, blk, ssem.at[k], rsem.at[k], (x, y, 1 - c)).wait_recv()
        for cp in started:
            cp.wait_send()

    return _comm_call(body, name, jax.ShapeDtypeStruct((4, rr, cc), shard.dtype), 1,
                      [_dma_sems(6 * nq), _dma_sems(6 * nq)])(shard)


def _with_own(shard, got, s_me):
    return lax.dynamic_update_index_in_dim(got, shard, s_me, 0)


def all_gather_chips(name, shard, s_me):
    return _with_own(shard, ag_chips(name, shard), s_me)


HBM_SPEC = pl.BlockSpec(memory_space=pltpu.HBM)
SEM_SPEC = pl.BlockSpec(memory_space=pltpu.SEMAPHORE)
SPLIT_EFFECT = pltpu.SideEffectType.DATAFLOW_SIDE_EFFECTING


def _split_copies(pieces, x_ref, land_ref, sems, arriving):
    x, y, c = _pos()
    return [_rcopy(s, d_in if arriving else d_out, sems[j], sems[3 + j], dev)
            for j, (s, d_out, d_in, dev) in enumerate(pieces(x_ref, land_ref, x, y, c))]


def split_copy_start(name, src, land_shape, pieces, after):
    def body(x_ref, land_ref, after_ref, *outs):
        for cp in _split_copies(pieces, x_ref, land_ref, outs[:6], False):
            cp.start()
        outs[8][...] = jnp.zeros_like(outs[8])

    dma = pltpu.SemaphoreType.DMA(())
    res = pl.pallas_call(
        body, name=name,
        out_shape=(dma,) * 6 + (pltpu.HBM(src.shape, src.dtype), pltpu.HBM(land_shape, src.dtype),
                                jax.ShapeDtypeStruct((8, 128), F32)),
        in_specs=(HBM_SPEC, HBM_SPEC, ANY),
        out_specs=(SEM_SPEC,) * 6 + (HBM_SPEC, HBM_SPEC, pl.BlockSpec(memory_space=pltpu.VMEM)),
        input_output_aliases={0: 6, 1: 7},
        compiler_params=pltpu.CompilerParams(has_side_effects=SPLIT_EFFECT),
    )(pltpu.with_memory_space_constraint(src, pltpu.HBM),
      pltpu.with_memory_space_constraint(lax.empty(land_shape, src.dtype), pltpu.HBM), after)
    return res[:6], res[6], res[7], res[8]


def split_copy_wait(name, sems, src_thru, land_thru, after, pieces):
    def body(x_ref, land_ref, *rest):
        for cp in _split_copies(pieces, x_ref, land_ref, rest[:6], False):
            cp.wait_send()
        for cp in _split_copies(pieces, x_ref, land_ref, rest[:6], True):
            cp.wait_recv()

    return pl.pallas_call(
        body, name=name,
        out_shape=(pltpu.HBM(src_thru.shape, src_thru.dtype), pltpu.HBM(land_thru.shape, land_thru.dtype)),
        in_specs=(HBM_SPEC, HBM_SPEC) + (SEM_SPEC,) * 6 + (ANY,), out_specs=(HBM_SPEC, HBM_SPEC),
        input_output_aliases={0: 0, 1: 1},
        compiler_params=pltpu.CompilerParams(has_side_effects=SPLIT_EFFECT),
    )(src_thru, land_thru, *sems, after)


def ag_pieces(h):
    def pieces(x_ref, land_ref, x, y, c):
        rows = _rows_at(c * h, h)
        return [(x_ref.at[rows], land_ref.at[2 * x + y, rows], land_ref.at[2 * cx + cy, rows], (cx, cy, c))
                for cx, cy in _other_chips(x, y)]
    return pieces


def rs_pieces(x_ref, land_ref, x, y, c):
    return [(x_ref.at[2 * cx + cy], land_ref.at[j], land_ref.at[j], (cx, cy, c))
            for j, (cx, cy) in enumerate(_other_chips(x, y))]


def ag_forward(name, got):
    _, rr, cc = got.shape
    h, nq = rr // 2, D2D_CHUNKS
    hq = h // nq

    def body(g_ref, out_ref, ssem, rsem):
        x, y, c = _pos()
        slots = [2 * cx + cy for cx, cy in _other_chips(x, y)]
        cps = []
        for j, s in enumerate(slots):
            for q in range(nq):
                blk = out_ref.at[s, _rows_at(c * h + q * hq, hq)]
                cp = _rcopy(blk, blk, ssem.at[j * nq + q], rsem.at[j * nq + q], (x, y, 1 - c))
                cp.start()
                cps.append(cp)
        for cp in cps:
            cp.wait_send()
        for j, s in enumerate(slots):
            for q in range(nq):
                blk = out_ref.at[s, _rows_at((1 - c) * h + q * hq, hq)]
                _rcopy(blk, blk, ssem.at[j * nq + q], rsem.at[j * nq + q], (x, y, 1 - c)).wait_recv()

    return pl.pallas_call(
        body, name=name, out_shape=jax.ShapeDtypeStruct(got.shape, got.dtype), in_specs=[ANY], out_specs=ANY,
        scratch_shapes=[_dma_sems(3 * nq), _dma_sems(3 * nq)], input_output_aliases={0: 0},
        compiler_params=pltpu.CompilerParams(has_side_effects=True),
    )(got)


def rs_pair(name, g):
    _, rr, cc = g.shape
    h, nq = rr // 2, D2D_CHUNKS
    hq = h // nq

    def body(g_ref, recv_ref, ssem, rsem):
        x, y, c = _pos()
        cps = []
        for q in range(nq):
            cp = _rcopy(g_ref.at[:, _rows_at((1 - c) * h + q * hq, hq), :], recv_ref.at[:, pl.ds(q * hq, hq), :],
                        ssem.at[q], rsem.at[q], (x, y, 1 - c))
            cp.start()
            cps.append(cp)
        for cp in cps:
            cp.wait()

    return _comm_call(body, name, jax.ShapeDtypeStruct((4, h, cc), g.dtype), 1, [_dma_sems(nq), _dma_sems(nq)])(g)


def rs_chips(name, p):
    _, h, cc = p.shape
    nq = ICI_CHUNKS
    hq = h // nq

    def body(p_ref, buf_ref, ssem, rsem):
        x, y, c = _pos()
        sends = []
        for q in range(nq):
            rows = pl.ds(q * hq, hq)
            for j, (cx, cy) in enumerate(_other_chips(x, y)):
                cp = _rcopy(p_ref.at[2 * cx + cy, rows], buf_ref.at[j, rows], ssem.at[j * nq + q],
                            rsem.at[j * nq + q], (cx, cy, c))
                cp.start()
                sends.append(cp)
        for cp in sends:
            cp.wait()

    return _comm_call(body, name, jax.ShapeDtypeStruct((3, h, cc), p.dtype), 1,
                      [_dma_sems(3 * nq), _dma_sems(3 * nq)])(p)


def rs_join(name, half):
    h, cc = half.shape
    nq = D2D_CHUNKS
    hq = h // nq

    def body(h_ref, out_ref, ssem, rsem):
        x, y, c = _pos()
        cps = []
        for q in range(nq):
            rows = pl.ds(q * hq, hq)
            cp = _rcopy(h_ref.at[rows], out_ref.at[rows], ssem.at[q], rsem.at[q], (x, y, 1 - c))
            cp.start()
            cps.append(cp)
        for cp in cps:
            cp.wait()

    return _comm_call(body, name, jax.ShapeDtypeStruct((h, cc), half.dtype), 1, [_dma_sems(nq), _dma_sems(nq)])(half)


def reduce_scatter(tag, g, tb, sp):
    return rs_end(rs_begin(tag, g, tb, sp, False), None)


def rs_begin(tag, g, tb, sp, split, after=None):
    _, rr, cc = g.shape
    h = rr // 2
    nbh = h // tb
    recv = rs_pair(tag + "_pair", g)
    mine_rows = lambda i, s: (i // nbh) * (2 * nbh) + s[0] * nbh + i % nbh
    part = rowwise(add2_fn, tag + "_add", 4 * h, tb, [R(g.reshape(4 * rr, cc), off=mine_rows), R(recv.reshape(4 * h, cc))],
                   [], [(cc, BF16)], sp=sp)[0].reshape(4, h, cc)
    st = dict(tag=tag, tb=tb, sp=sp, split=split, part=part)
    if split:
        st["sems"], st["part"], st["land"], st["token"] = split_copy_start(tag + "_start", part, (3, h, cc), rs_pieces,
                                                                           sp if after is None else after)
    return st


def rs_end(st, after):
    tag, tb, sp, part = st["tag"], st["tb"], st["sp"], st["part"]
    _, h, cc = part.shape
    nbh = h // tb
    if st["split"]:
        part, buf = split_copy_wait(tag + "_wait", st["sems"], part, st["land"], after, rs_pieces)
    else:
        buf = rs_chips(tag + "_chips", part)
    red = rowwise(sum4_fn, tag + "_sum", h, tb,
                  [R(part.reshape(4 * h, cc), off=lambda i, s: s[1] * nbh + i)]
                  + [R(buf.reshape(3 * h, cc), off=k * nbh) for k in range(3)],
                  [], [(cc, F32)], sp=sp)[0]
    return red, rs_join(tag + "_join", red)


def adam_halves(name, w, m, v, red, other, tb, blk0, sp):
    nbh = red.shape[0] // tb

    def fn(i, n, s, w_, m_, v_, r_, o_):
        g = jnp.where((blk0 + i) // nbh == s[0], r_, o_)
        return (g,) + _adamw(w_, g, m_, v_)

    half_rows = lambda i, s: (blk0 + i) % nbh
    return rowwise(fn, name, w.shape[0], tb, [R(w), R(m), R(v), R(red, off=half_rows), R(other, off=half_rows)],
                   [], [(w.shape[1], F32)] * 4, sp=sp)


SMALL_LANES = 3 * D


def all_reduce_items(name, items, after=None):
    flat = [a for it in items for a in it]
    shapes = [(sum(a.shape[0] for a in it), it[0].shape[1]) for it in items]
    nrows = -(-sum(s[0] for s in shapes) // 8) * 8
    extra = [] if after is None else [after]

    def body(*refs):
        ins, refs = refs[:len(flat)], refs[len(flat) + len(extra):]
        outs = refs[:len(items)]
        mine, buf, ssem, rsem = refs[len(items):]
        x, y, c = _pos()
        me = 4 * x + 2 * y + c
        mine[...] = jnp.zeros_like(mine)
        r = 0
        for ref in ins:
            mine[r:r + ref.shape[0], 0:ref.shape[1]] = ref[...]
            r += ref.shape[0]
        buf[me] = mine[...]
        cps = []
        for k in range(1, 8):
            dev = (x ^ (k >> 2), y ^ ((k >> 1) & 1), c ^ (k & 1))
            cp = _rcopy(mine, buf.at[me], ssem.at[k - 1], rsem.at[k - 1], dev)
            cp.start()
            cps.append(cp)
        for cp in cps:
            cp.wait()
        r = 0
        for (nr, n), out in zip(shapes, outs):
            acc = buf[0, r:r + nr, 0:n]
            for d in range(1, 8):
                acc = acc + buf[d, r:r + nr, 0:n]
            out[...] = acc
            r += nr

    vm = pl.BlockSpec(memory_space=pltpu.VMEM)
    return pl.pallas_call(
        body, name=name, out_shape=[jax.ShapeDtypeStruct(s, F32) for s in shapes],
        in_specs=[vm] * len(flat) + [ANY] * len(extra), out_specs=[vm] * len(items),
        scratch_shapes=[pltpu.VMEM((nrows, SMALL_LANES), F32), pltpu.VMEM((8, nrows, SMALL_LANES), F32),
                        _dma_sems(7), _dma_sems(7)],
        compiler_params=pltpu.CompilerParams(has_side_effects=True),
    )(*flat, *extra)


def adam_small(ws, gs, ms, vs):
    n = len(ws)

    def body(*refs):
        for k in range(n):
            w, g, m, v = (refs[j * n + k][...] for j in range(4))
            for j, val in enumerate(_adamw(w, g, m, v)):
                refs[(4 + j) * n + k][...] = val

    vm = pl.BlockSpec(memory_space=pltpu.VMEM)
    res = pl.pallas_call(
        body, name="adam_small", out_shape=[jax.ShapeDtypeStruct(w.shape, F32) for w in ws] * 3,
        in_specs=[vm] * (4 * n), out_specs=[vm] * (3 * n),
    )(*ws, *gs, *ms, *vs)
    return res[:n], res[n:2 * n], res[2 * n:]


def _sel(rows, cols, pairs):
    m = np.zeros((rows, cols), np.float32)
    for r, c in pairs:
        m[r, c] = 1.0
    return jnp.asarray(m)


def _pad_win(w):
    z = jnp.zeros((w.shape[0], 112), w.dtype)
    return jnp.concatenate([w[:, :4096], w[:, 4112:6672], w[:, 4096:4112], z, w[:, 6672:6688], z], axis=1)


def _unpad_win(wp):
    return jnp.concatenate([wp[:, :4096], wp[:, 6656:6672], wp[:, 4096:6656], wp[:, 6784:6800]], axis=1)


def kernel(x, mem, norm1_w, w_in, gdn_conv_w, gdn_a_log, gdn_dt_bias, gdn_norm_w, ssm_conv_w, ssm_conv_b, ssm_a_log, ssm_dt_bias, ssm_d, ssm_norm_w, w_out, norm2_w, mem_norm_w, wq_mem, wk_mem, wv_mem, wo_mem, norm3_w, w_up, w_down, final_norm_w, loss_target, m_norm1_w, m_w_in, m_gdn_conv_w, m_gdn_a_log, m_gdn_dt_bias, m_gdn_norm_w, m_ssm_conv_w, m_ssm_conv_b, m_ssm_a_log, m_ssm_dt_bias, m_ssm_d, m_ssm_norm_w, m_w_out, m_norm2_w, m_mem_norm_w, m_wq_mem, m_wk_mem, m_wv_mem, m_wo_mem, m_norm3_w, m_w_up, m_w_down, m_final_norm_w, v_norm1_w, v_w_in, v_gdn_conv_w, v_gdn_a_log, v_gdn_dt_bias, v_gdn_norm_w, v_ssm_conv_w, v_ssm_conv_b, v_ssm_a_log, v_ssm_dt_bias, v_ssm_d, v_ssm_norm_w, v_w_out, v_norm2_w, v_mem_norm_w, v_wq_mem, v_wk_mem, v_wv_mem, v_wo_mem, v_norm3_w, v_w_up, v_w_down, v_final_norm_w):
    T, M = x.shape[1], mem.shape[1]
    xi, yi, ci = _pos()
    s_me = 2 * xi + yi
    x0, mem0, tgt = x[0], mem[0], loss_target[0]
    tb = min(256, T)
    tbp = min(256, T)
    row = lambda v: v.reshape(1, -1)

    win_g = all_gather_chips("ag_win", w_in.astype(BF16), s_me)
    w_in_p = _pad_win(win_g.transpose(1, 0, 2).reshape(D, IN_COLS))
    keep = (ci == 0).astype(F32)
    gcw_z = lax.dynamic_update_slice(jnp.zeros((4, 3 * D), F32), gdn_conv_w * keep, (0, s_me * 768))
    scw_z = lax.dynamic_update_slice(jnp.zeros((4, 1536), F32), ssm_conv_w * keep, (0, s_me * 384))
    gcw, scw = all_reduce_items("ar_convw", [[gcw_z], [scw_z]])
    scw_x, scw_bc = scw[:, :D], scw[:, D:]
    rest_shard = jnp.concatenate([w_up, w_down, w_out, wq_mem, wk_mem, wv_mem, wo_mem], axis=0).astype(BF16)
    ag_sems, rest_thru, rest_land, ag_token = split_copy_start("ag_rest_start", rest_shard, (4,) + rest_shard.shape,
                                                               ag_pieces(rest_shard.shape[0] // 2), gcw)
    sp = jnp.stack([ci, s_me]).astype(jnp.int32)
    scb_x, scb_bc = row(ssm_conv_b[:D]), row(ssm_conv_b[D:])

    galog_c, gdtb_c = row(jnp.pad(gdn_a_log, (8, 112))), row(jnp.pad(gdn_dt_bias, (8, 112)))
    salog_c, sdtb_c = row(jnp.pad(ssm_a_log, (0, 112))), row(jnp.pad(ssm_dt_bias, (0, 112)))
    sd_x = row(jnp.repeat(ssm_d, 64))
    eb = _sel(128, D, [(h, 128 * h + l) for h in range(8) for l in range(128)])
    ea = _sel(128, D, [(8 + h, 128 * h + l) for h in range(8) for l in range(128)])
    e16 = _sel(128, D, [(h, 64 * h + l) for h in range(16) for l in range(64)])
    pb = _sel(D, 128, [(128 * h, h) for h in range(8)])
    pa = _sel(D, 128, [(128 * h, 8 + h) for h in range(8)])

    h1 = rowwise(rms_fwd_fn, "rms1", T, tb, [R(x0)], [row(norm1_w) + ag_token[0:1, 0:1]], [(D, BF16)])[0]
    p = matmul("mm_in", h1, w_in_p, "nn", 2048, 768, 1024, [F32])[0]
    gp_ins = [R(p, 3 * D, CB_QKV, "prev"), R(p, 128, CB_BA)]
    qn, kn, vv, gcs_x, beta_x, gcs_t = rowwise(gdn_prep_fn, "gdn_prep", T, tbp, gp_ins,
                                               [gcw, galog_c, gdtb_c, eb, ea], [(D, F32)] * 5 + [(-8, F32)])
    gcs_t = gcs_t.reshape(GDN_H, 1, T)
    gtb, ggh = min(128, T), 8
    o_gdn, s_states, tinv = gdn_fwd(qn, kn, vv, gcs_x, beta_x, gcs_t, gtb, ggh)
    gnw = row(gdn_norm_w)
    oa = rowwise(gdn_post_fn, "gdn_post", T, tb, [R(o_gdn), R(p, D, CB_Z)], [gnw], [(D, BF16)])[0]
    sp_ins = [R(p, D, CB_XS, "prev"), R(p, 512, CB_BC, "prev"), R(p, 128, CB_DT)]
    sp_full = [scw_x, scw_bc, scb_x, scb_bc, salog_c, sdtb_c]
    xs, bc, dt_x, acs_x, acs_t = rowwise(ssd_prep_fn, "ssd_prep", T, tbp, sp_ins, sp_full + [e16],
                                         [(D, F32), (512, F32), (D, F32), (D, F32), (-SSM_H, F32)])
    acs_t = acs_t.reshape(SSM_H, 1, T)
    y_ssd, h_states = ssd_fwd(xs, bc, dt_x, acs_x, acs_t)
    snw = row(ssm_norm_w)
    ob = rowwise(ssd_post_fn, "ssd_post", T, tb, [R(y_ssd), R(xs), R(p, D, CB_ZS)], [sd_x, snw], [(D, BF16)])[0]
    rest_thru, rest_land = split_copy_wait("ag_rest_wait", ag_sems, rest_thru, rest_land, ob,
                                           ag_pieces(rest_shard.shape[0] // 2))
    rest_g = _with_own(rest_thru, ag_forward("ag_rest_fwd", rest_land), s_me)
    assert D == 1024
    view = lambda shape, blk, at: dict(b_sel=(shape, blk, at))
    wup_n = view((D, D_FF), (None, D, D), lambda i, j, k: (j, 0, 0))
    wup_t = view((D, D_FF), (None, D, D), lambda i, j, k: (k, 0, 0))
    wdown_n = view((D_FF, D), (None, D, D), lambda i, j, k: (k, 1, 0))
    wdown_t = view((D_FF, D), (None, D, D), lambda i, j, k: (j, 1, 0))
    wout_a = view((D, D), (2, 512, D), lambda i, j, k: (0, 4, 0))
    wout_b = view((D, D), (2, 512, D), lambda i, j, k: (1, 4, 0))
    wq_v, wk_v, wv_v, wo_v = (view((D, D), (4, 256, D), lambda i, j, k, r=r: (0, r, 0)) for r in (10, 11, 12, 13))
    x1a = matmul("mm_out_a", oa, rest_g, "nn", 1024, 1024, 1024, [F32], _epi_res, [x0], **wout_a)[0]
    x1, h2 = matmul("mm_out_b", ob, rest_g, "nn", 1024, 1024, 1024, [F32, BF16], _epi_res_rms, [x1a],
                    [row(norm2_w)], **wout_b)

    mn = rowwise(rms_fwd_fn, "rms_mem", M, M, [R(mem0)], [row(mem_norm_w)], [(D, BF16)])[0]
    km = matmul("mm_k", mn, rest_g, "nn", 256, 1024, 1024, [BF16], **wk_v)[0]
    vm = matmul("mm_v", mn, rest_g, "nn", 256, 1024, 1024, [BF16], **wv_v)[0]
    qm = matmul("mm_q", h2, rest_g, "nn", 1024, 1024, 1024, [BF16], **wq_v)[0]
    ao = rowwise(attn_fn, "attn", T, tb, [R(qm)], [km, vm], [(D, BF16)])[0]
    x2, h3 = matmul("mm_o", ao, rest_g, "nn", 1024, 1024, 1024, [F32, BF16], _epi_res_rms, [x1], [row(norm3_w)], **wo_v)
    u, act = matmul("mm_up", h3, rest_g, "nn", 2048, 1024, 1024, [BF16, BF16], _epi_relu2, **wup_n)
    x3 = matmul("mm_down", act, rest_g, "nn", 1024, 1024, 1024, [F32], _epi_res, [x2], **wdown_n)[0]
    dx3, dx3b, loss_lane, g_final = rowwise(final_fn, "final", T, tb, [R(x3), R(tgt)], [row(final_norm_w)],
                                            [(D, F32), (D, BF16)], [(1, D), (1, D)])
    loss = lax.psum(0.5 / D * jnp.sum(loss_lane), ("x", "y", "c"))

    dup = matmul("mm_dact", dx3b, rest_g, "nt", 2048, 1024, 1024, [BF16], _epi_dup, [u], **wdown_t)[0]
    def g_into(buf, blk, at):
        return dict(into=(buf, blk, lambda i, j, k, at=at: at(i, j)))

    grest = jax.ShapeDtypeStruct((4, 3584, D), F32)
    grest = matmul("mm_gdown", act, dx3b, "tn", 1024, 1024, 2048, [F32],
                   **g_into(grest, (None, 1024, D), lambda i, j: (i, 1, 0)))
    dh3 = matmul("mm_dh3", dup, rest_g, "nt", 2048, 1024, 1024, [F32], **wup_t)[0]
    dx2, dx2b, g_n3 = rowwise(rms_bwd_fn, "rms3_bwd", T, tb, [R(x2), R(dh3), R(dx3)], [row(norm3_w)],
                              [(D, F32), (D, BF16)], [(1, D)])
    grest = matmul("mm_gup", h3, dup, "tn", 1024, 1024, 2048, [F32],
                   **g_into(grest, (None, 1024, D), lambda i, j: (j, 0, 0)))
    dao = matmul("mm_dao", dx2b, rest_g, "nt", 1024, 1024, 1024, [F32], **wo_v)[0]
    grest = matmul("mm_gwo", ao, dx2b, "tn", 1024, 1024, 2048, [F32],
                   **g_into(grest, (4, 256, D), lambda i, j: (0, 13, 0)))
    dqm, dkm, dvm = rowwise(attn_bwd_fn, "attn_bwd", T, tb, [R(qm), R(dao)], [km, vm], [(D, BF16)],
                            [(M, D), (M, D)])
    dx1, dx1b, g_n2 = matmul("mm_dh2", dqm, rest_g, "nt", 512, 1024, 1024, [F32, BF16], _epi_rms_bwd, [x1, dx2],
                             [row(norm2_w)], n_acc=1, **wq_v)
    grest = matmul("mm_gwq", h2, dqm, "tn", 1024, 1024, 2048, [F32],
                   **g_into(grest, (4, 256, D), lambda i, j: (0, 10, 0)))
    grest = matmul("mm_gwk", mn, dkm, "tn", 1024, 1024, 256, [F32],
                   **g_into(grest, (4, 256, D), lambda i, j: (0, 11, 0)))
    grest = matmul("mm_gwv", mn, dvm, "tn", 1024, 1024, 256, [F32],
                   **g_into(grest, (4, 256, D), lambda i, j: (0, 12, 0)))
    dmn_k = matmul("mm_dmk", dkm, rest_g, "nt", 256, 1024, 1024, [F32], **wk_v)[0]
    dmn = matmul("mm_dmv", dvm, rest_g, "nt", 256, 1024, 1024, [F32], _epi_res, [dmn_k], **wv_v)[0]
    g_nmem = rowwise(rms_bwd_w_fn, "rmsmem_bwd", M, M, [R(mem0), R(dmn)], [row(mem_norm_w)], [], [(1, D)])[0]
    doa = matmul("mm_doa", dx1b, rest_g, "nt", 2048, 1024, 1024, [F32], **wout_a)[0]
    dob = matmul("mm_dob", dx1b, rest_g, "nt", 2048, 1024, 1024, [F32], **wout_b)[0]
    grest = matmul("mm_gwout_a", oa, dx1b, "tn", 1024, 1024, 2048, [F32],
                   **g_into(grest, (2, 512, D), lambda i, j: (0, 4, 0)))
    grest = matmul("mm_gwout_b", ob, dx1b, "tn", 1024, 1024, 2048, [F32],
                   **g_into(grest, (2, 512, D), lambda i, j: (1, 4, 0)))

    rs_rest = rs_begin("rs_rest", grest, 256, sp, True)

    dp = jax.ShapeDtypeStruct((T, p.shape[1]), BF16)
    dy_ssd, dxs_dir, dp, g_snw, g_sd_lane = rowwise(
        ssd_post_bwd_fn, "ssd_post_bwd", T, tb, [R(y_ssd), R(xs), R(p, D, CB_ZS), R(dob)],
        [sd_x + rs_rest["token"][0:1, 0:1], snw],
        [(D, F32), (D, F32), (D, BF16, dp, CB_ZS)], [(1, D), (1, D)])
    dxs_scan, db_s, dc_s, dgate, dacs_t = ssd_bwd(xs, bc, dt_x, acs_x, acs_t, dy_ssd, h_states)
    spb = rowwise(ssd_prep_bwd_fn, "ssd_prep_bwd", T, tbp,
                  sp_ins + [R(dxs_scan), R(dxs_dir), R(db_s), R(dc_s), R(dgate), RC(dacs_t.reshape(SSM_H, T))], sp_full,
                  [(D, F32), (512, F32), (128, BF16, dp, CB_DT)],
                  [(1, D)] * 4 + [(1, 512)] * 4 + [(1, D), (1, 512), (1, 128), (1, 128)])
    dyc_x, dyc_bc, dp = spb[:3]
    dp = rowwise(conv_bwd_fn, "conv_bwd_x", T, tbp, [R(dyc_x, halo="next")], [scw_x], [(D, BF16, dp, CB_XS)])[0]
    dp = rowwise(conv_bwd_fn, "conv_bwd_bc", T, tbp, [R(dyc_bc, halo="next")], [scw_bc], [(512, BF16, dp, CB_BC)])[0]

    do_gdn, dp, g_gnw = rowwise(gdn_post_bwd_fn, "gdn_post_bwd", T, tb, [R(o_gdn), R(p, D, CB_Z), R(doa)], [gnw],
                                [(D, F32), (D, BF16, dp, CB_Z)], [(1, 128)])
    dqn, dkn, dvv, dgcs_x, dbeta_x, dgcs_t = gdn_bwd(qn, kn, vv, gcs_x, beta_x, gcs_t, do_gdn, s_states, tinv, gtb, ggh)
    gpb = rowwise(gdn_prep_bwd_fn, "gdn_prep_bwd", T, tbp,
                  gp_ins + [R(dqn), R(dkn), R(dvv), R(dgcs_x), R(dbeta_x), RC(dgcs_t.reshape(GDN_H, T))],
                  [gcw, galog_c, gdtb_c, pb, pa],
                  [(3 * D, F32), (128, BF16, dp, CB_BA)], [(1, 3 * D)] * 4 + [(1, 128), (1, 128)])
    dyc_qkv, dp = gpb[:2]
    dp = rowwise(conv_bwd_fn, "conv_bwd_qkv", T, tbp, [R(dyc_qkv, halo="next")], [gcw], [(3 * D, BF16, dp, CB_QKV)])[0]
    g_win_p = matmul("mm_gwin", h1, dp, "tn", 1024, 768, 2048, [F32])[0]
    g_win = _unpad_win(g_win_p).reshape(D, 4, IN_COLS // 4).transpose(1, 0, 2)
    rs_win = rs_begin("rs_win", g_win, 256, sp, True)
    dh1 = matmul("mm_dh1", dp, w_in_p, "nt", 2048, 1024, 768, [F32])[0]
    grad_x, g_n1 = rowwise(rms_bwd1_fn, "rms1_bwd", T, tb, [R(x0), R(dh1), R(dx1)],
                           [row(norm1_w) + rs_win["token"][0:1, 0:1]], [(D, F32)], [(1, D)])
    red_w, oth_w = rs_end(rs_win, grad_x)

    items = [[g_n1], [gpb[6]], [gpb[7]], [g_gnw], [spb[11]], [spb[12]], [spb[13]], [spb[14]], [g_sd_lane], [g_snw],
             [g_n2], [g_nmem], [g_n3], [g_final], list(gpb[2:6]), list(spb[3:7]), list(spb[7:11])]
    (gr_n1, r_galog, r_gdtb, gr_gnw, r_scb_x, r_scb_bc, r_salog, r_sdtb, r_sd, gr_snw, gr_n2, gr_nmem, gr_n3,
     gr_final, r_gcw, r_scw_x, r_scw_bc) = all_reduce_items("ar_grads", items, after=red_w)
    gr_galog, gr_gdtb = r_galog[:, 8:16], r_gdtb[:, 8:16]
    gr_salog, gr_sdtb = r_salog[:, :SSM_H], r_sdtb[:, :SSM_H]
    gr_sd = r_sd.reshape(SSM_H, SSM_P).sum(axis=1).reshape(1, SSM_H)
    gr_scb = jnp.concatenate([r_scb_x, r_scb_bc], axis=1)
    gr_gcw = lax.dynamic_slice(r_gcw, (0, s_me * 768), (4, 768))
    gr_scw = lax.dynamic_slice(jnp.concatenate([r_scw_x, r_scw_bc], axis=1), (0, s_me * 384), (4, 384))

    red_r, oth_r = rs_end(rs_rest, gr_n1)

    big = {}
    for n, w, m, v, blk0 in (("w_up", w_up, m_w_up, v_w_up, 0), ("w_down", w_down, m_w_down, v_w_down, 4),
                             ("w_out", w_out, m_w_out, v_w_out, 8), ("wq_mem", wq_mem, m_wq_mem, v_wq_mem, 10),
                             ("wk_mem", wk_mem, m_wk_mem, v_wk_mem, 11), ("wv_mem", wv_mem, m_wv_mem, v_wv_mem, 12),
                             ("wo_mem", wo_mem, m_wo_mem, v_wo_mem, 13)):
        big[n] = adam_halves("adam_" + n, w, m, v, red_r, oth_r, 256, blk0, sp)
    big["w_in"] = adam_halves("adam_win", w_in, m_w_in, v_w_in, red_w, oth_w, 256, 0, sp)
    names_s =["norm1_w", "gdn_conv_w", "gdn_a_log", "gdn_dt_bias", "gdn_norm_w", "ssm_conv_w", "ssm_conv_b",
               "ssm_a_log", "ssm_dt_bias", "ssm_d", "ssm_norm_w", "norm2_w", "mem_norm_w", "norm3_w", "final_norm_w"]
    w_s = [norm1_w, gdn_conv_w, gdn_a_log, gdn_dt_bias, gdn_norm_w, ssm_conv_w, ssm_conv_b, ssm_a_log, ssm_dt_bias,
           ssm_d, ssm_norm_w, norm2_w, mem_norm_w, norm3_w, final_norm_w]
    g_s = [gr_n1, gr_gcw, gr_galog, gr_gdtb, gr_gnw, gr_scw, gr_scb, gr_salog, gr_sdtb, gr_sd, gr_snw, gr_n2,
           gr_nmem, gr_n3, gr_final]
    m_s = [m_norm1_w, m_gdn_conv_w, m_gdn_a_log, m_gdn_dt_bias, m_gdn_norm_w, m_ssm_conv_w, m_ssm_conv_b, m_ssm_a_log,
           m_ssm_dt_bias, m_ssm_d, m_ssm_norm_w, m_norm2_w, m_mem_norm_w, m_norm3_w, m_final_norm_w]
    v_s = [v_norm1_w, v_gdn_conv_w, v_gdn_a_log, v_gdn_dt_bias, v_gdn_norm_w, v_ssm_conv_w, v_ssm_conv_b, v_ssm_a_log,
           v_ssm_dt_bias, v_ssm_d, v_ssm_norm_w, v_norm2_w, v_mem_norm_w, v_norm3_w, v_final_norm_w]
    shp_s = [w.shape for w in w_s]
    as2d = lambda a: a if a.ndim == 2 else a.reshape(1, -1)
    d_l, m_l, v_l = adam_small([as2d(a) for a in w_s], [as2d(a) for a in g_s], [as2d(a) for a in m_s],
                               [as2d(a) for a in v_s])

    grads, deltas, new_m, new_v = {}, {}, {}, {}
    for n, (gg, dd, mm_, vv_) in big.items():
        grads[n], deltas[n], new_m[n], new_v[n] = gg, dd, mm_, vv_
    for k, n in enumerate(names_s):
        grads[n] = g_s[k].reshape(shp_s[k])
        deltas[n], new_m[n], new_v[n] = (a[k].reshape(shp_s[k]) for a in (d_l, m_l, v_l))
    order = ["norm1_w", "w_in", "gdn_conv_w", "gdn_a_log", "gdn_dt_bias", "gdn_norm_w", "ssm_conv_w", "ssm_conv_b",
             "ssm_a_log", "ssm_dt_bias", "ssm_d", "ssm_norm_w", "w_out", "norm2_w", "mem_norm_w", "wq_mem", "wk_mem",
             "wv_mem", "wo_mem", "norm3_w", "w_up", "w_down", "final_norm_w"]
    return (loss, grad_x[None], *[grads[n] for n in order], *[deltas[n] for n in order],
            *[new_m[n] for n in order], *[new_v[n] for n in order])
```

```python
import numpy as np
import jax
import jax.numpy as jnp
from jax import lax
from jax.experimental import pallas as pl
from jax.experimental.pallas import tpu as pltpu

F32, BF16 = jnp.float32, jnp.bfloat16
MESH = pl.DeviceIdType.MESH
ANY = pl.BlockSpec(memory_space=pl.ANY)

EPS = 1e-6
D = 1024
GDN_H, GDN_DK, GDN_C = 8, 128, 64
SSM_H, SSM_P, SSM_N, SSM_L = 16, 64, 128, 128
MEM_H, MEM_DH = 4, 256
D_FF = 4096
IN_COLS = 6688
CB_QKV, CB_Z, CB_ZS, CB_XS, CB_BC, CB_BA, CB_DT = 0, 3, 4, 5, 12, 52, 53
VMEM_LIMIT = 56 * 1024 * 1024
D2D_CHUNKS = 8
ICI_CHUNKS = 4

ADAM_LR, ADAM_B1, ADAM_B2, ADAM_EPS, ADAM_WD, ADAM_STEP = 0.001, 0.9, 0.999, 1e-08, 0.01, 10


def _dg(a, b, ca, cb):
    return lax.dot_general(a, b, (((ca,), (cb,)), ((), ())), preferred_element_type=F32)


def _bf(x):
    return x.astype(BF16)


def mm(a, b):
    return _dg(_bf(a), _bf(b), 1, 0)


def mm_nt(a, b):
    return _dg(_bf(a), _bf(b), 1, 1)


def mm_tn(a, b):
    return _dg(_bf(a), _bf(b), 0, 0)


def mm_sel(a, sel):
    hi = a.astype(BF16)
    r1 = a - hi.astype(F32)
    mid = r1.astype(BF16)
    lo = (r1 - mid.astype(F32)).astype(BF16)
    s = sel.astype(BF16)
    return _dg(hi, s, 1, 0) + (_dg(mid, s, 1, 0) + _dg(lo, s, 1, 0))


def mm3(a, b):
    ah, bh = a.astype(BF16), b.astype(BF16)
    al, bl = (a - ah.astype(F32)).astype(BF16), (b - bh.astype(F32)).astype(BF16)
    return _dg(ah, bh, 1, 0) + (_dg(ah, bl, 1, 0) + _dg(al, bh, 1, 0))


def _iota(shape, dim):
    return lax.broadcasted_iota(jnp.int32, shape, dim)


def _chunk_cumsum(x, c):
    pos = _iota(x.shape, 0) & (c - 1)
    s = 1
    while s < c:
        x = x + jnp.where(pos >= s, pltpu.roll(x, s, 0), 0.0)
        s *= 2
    return x


def _chunk_revcumsum(x, c):
    n = x.shape[0]
    pos = _iota(x.shape, 0) & (c - 1)
    s = 1
    while s < c:
        x = x + jnp.where(pos < c - s, pltpu.roll(x, n - s, 0), 0.0)
        s *= 2
    return x


def _sig(x):
    return jax.nn.sigmoid(x)


def _softplus(x):
    return jnp.maximum(x, 0.0) + jnp.log(1.0 + jnp.exp(-jnp.abs(x)))


def _rows(v):
    return jnp.sum(v, axis=0, keepdims=True)


def _lanes(v):
    return jnp.sum(v, axis=1, keepdims=True)


def _sum_all(v):
    return _rows(_lanes(v))


def _cparams(sem):
    return pltpu.CompilerParams(dimension_semantics=sem, vmem_limit_bytes=VMEM_LIMIT)


def rowwise(fn, name, T, tb, row_ins, full_ins, row_outs, acc_outs=(), sp=None):
    nblk = T // tb
    assert nblk * tb == T
    has_sp = sp is not None

    def imap(f):
        return (lambda i, s: f(i, s)) if has_sp else (lambda i: f(i, None))

    in_specs, args = [], []
    for arr, w, cb, halo, off in row_ins:
        if halo == "col":
            in_specs.append(pl.BlockSpec((w, tb), imap(lambda i, s: (0, i))))
            args.append(arr)
            continue
        rowf = off if callable(off) else (lambda i, s, off=off: i + off)
        in_specs.append(pl.BlockSpec((tb, w), imap(lambda i, s, cb=cb, rowf=rowf: (rowf(i, s), cb))))
        args.append(arr)
        if halo == "prev":
            r = tb // 8
            in_specs.append(pl.BlockSpec((8, w), imap(lambda i, s, cb=cb, r=r: (jnp.maximum(i * r - 1, 0), cb))))
            args.append(arr)
        elif halo == "next":
            r, last = tb // 8, T // 8 - 1
            in_specs.append(pl.BlockSpec((8, w), imap(lambda i, s, cb=cb, r=r, last=last:
                                                      (jnp.minimum((i + 1) * r, last), cb))))
            args.append(arr)
    for arr in full_ins:
        in_specs.append(pl.BlockSpec(arr.shape, imap(lambda i, s, nd=arr.ndim: (0,) * nd)))
        args.append(arr)
    n_in, n_ro = len(args), len(row_outs)
    out_shape, out_specs, aliases = [], [], {}
    for k, (w, dt, *dest) in enumerate(row_outs):
        if dest:
            buf, cb = dest
            out_shape.append(jax.ShapeDtypeStruct(buf.shape, buf.dtype))
            out_specs.append(pl.BlockSpec((tb, w), imap(lambda i, s, cb=cb: (i, cb))))
            if not isinstance(buf, jax.ShapeDtypeStruct):
                aliases[len(args) + int(has_sp)] = k
                in_specs.append(ANY)
                args.append(buf)
        elif w < 0:
            out_shape.append(jax.ShapeDtypeStruct((-w, T), dt))
            out_specs.append(pl.BlockSpec((-w, tb), imap(lambda i, s: (0, i))))
        else:
            out_shape.append(jax.ShapeDtypeStruct((T, w), dt))
            out_specs.append(pl.BlockSpec((tb, w), imap(lambda i, s: (i, 0))))
    for shp in acc_outs:
        out_shape.append(jax.ShapeDtypeStruct(shp, F32))
        out_specs.append(pl.BlockSpec(shp, imap(lambda i, s, nd=len(shp): (0,) * nd)))

    def body(*refs):
        i = pl.program_id(0)
        if has_sp:
            sp_ref, refs = refs[0], refs[1:]
            vals = fn(i, nblk, sp_ref, *[r[...] for r in refs[:n_in]])
        else:
            vals = fn(i, nblk, *[r[...] for r in refs[:n_in]])
        outs = refs[n_in + len(aliases):]
        for ref, val in zip(outs[:n_ro], vals[:n_ro]):
            ref[...] = val.astype(ref.dtype)
        for ref, val in zip(outs[n_ro:], vals[n_ro:]):
            @pl.when(i == 0)
            def _(ref=ref, val=val):
                ref[...] = val

            @pl.when(i > 0)
            def _(ref=ref, val=val):
                ref[...] += val

    cparams = _cparams(("arbitrary",) if acc_outs else ("parallel",))
    if has_sp:
        return pl.pallas_call(
            body, name=name, out_shape=out_shape, compiler_params=cparams, input_output_aliases=aliases,
            grid_spec=pltpu.PrefetchScalarGridSpec(num_scalar_prefetch=1, grid=(nblk,), in_specs=in_specs,
                                                   out_specs=out_specs),
        )(sp, *args)
    return pl.pallas_call(
        body, name=name, grid=(nblk,), in_specs=in_specs, out_specs=out_specs, out_shape=out_shape,
        compiler_params=cparams, input_output_aliases=aliases,
    )(*args)


def R(arr, w=None, cb=0, halo=None, off=0):
    return (arr, arr.shape[1] if w is None else w, cb, halo, off)


def RC(arr):
    return (arr, arr.shape[0], 0, "col", 0)


def matmul(name, a, b, form, tm, tn, tk, out_dtypes, epi=None, extras=(), rows=(), into=None, n_acc=0, b_sel=None):
    bs = b.shape if b_sel is None else b_sel[0]
    if form == "nn":
        (M, K), N = a.shape, bs[1]
    elif form == "nt":
        (M, K), N = a.shape, bs[0]
    else:
        (K, M), N = a.shape, bs[1]
    tm, tn, tk = min(tm, M), min(tn, N), min(tk, K)
    assert M % tm == 0 and N % tn == 0 and K % tk == 0, (name, M, N, K, tm, tn, tk)

    def b_spec_of(blk, at):
        if b_sel is None:
            return pl.BlockSpec(blk, lambda i, j, k: at(i, j, k))
        blk3 = b_sel[1]
        assert int(np.prod([d for d in blk3 if d is not None])) == blk[0] * blk[1], (name, blk3, blk)
        return pl.BlockSpec(blk3, lambda i, j, k: b_sel[2](i, j, k))

    if form == "nn":
        a_spec = pl.BlockSpec((tm, tk), lambda i, j, k: (i, k))
        b_spec = b_spec_of((tk, tn), lambda i, j, k: (k, j))
        ca, cb = 1, 0
    elif form == "nt":
        a_spec = pl.BlockSpec((tm, tk), lambda i, j, k: (i, k))
        b_spec = b_spec_of((tn, tk), lambda i, j, k: (j, k))
        ca, cb = 1, 1
    else:
        a_spec = pl.BlockSpec((tk, tm), lambda i, j, k: (k, i))
        b_spec = b_spec_of((tk, tn), lambda i, j, k: (k, j))
        ca, cb = 0, 0
    nk, ne, no = K // tk, len(extras) + len(rows), len(out_dtypes)
    if epi is None:
        epi = lambda acc: (acc,)

    assert n_acc == 0 or tn == N

    def body(a_ref, b_ref, *rest):
        ex, outs, accs, acc = rest[:ne], rest[ne:ne + no], rest[ne + no:ne + no + n_acc], rest[ne + no + n_acc]
        i, k = pl.program_id(0), pl.program_id(2)

        def finish(total):
            vals = epi(total, *[e[...] for e in ex])
            for r, v in zip(outs, vals[:no]):
                r[...] = v.astype(r.dtype).reshape(r.shape)
            for r, v in zip(accs, vals[no:]):
                @pl.when(i == 0)
                def _(r=r, v=v):
                    r[...] = v

                @pl.when(i > 0)
                def _(r=r, v=v):
                    r[...] += v

        b_tile = b_ref[...]
        if b_sel is not None and len(b_sel) > 3:
            b_tile = jnp.concatenate([b_tile[s] for s in range(b_tile.shape[0])], axis=1)
        prod = _dg(_bf(a_ref[...]), _bf(b_tile.reshape(-1, b_tile.shape[-1])), ca, cb)
        if nk == 1:
            finish(prod)
            return

        @pl.when(k == 0)
        def _():
            acc[...] = prod

        @pl.when(k > 0)
        def _():
            acc[...] += prod

        @pl.when(k == nk - 1)
        def _():
            finish(acc[...])

    mn = pl.BlockSpec((tm, tn), lambda i, j, k: (i, j))
    rw = pl.BlockSpec((1, tn), lambda i, j, k: (0, j))
    acc_scratch = pltpu.VMEM((tm, tn) if nk > 1 else (8, 128), F32)
    if into is not None:
        buf, blk, bmap = into
        assert ne == 0 and no == 1
        aliased = not isinstance(buf, jax.ShapeDtypeStruct)

        def body_into(a_ref, b_ref, *rest):
            body(a_ref, b_ref, *rest[-2:])

        return pl.pallas_call(
            body_into, name=name, grid=(M // tm, N // tn, nk),
            in_specs=[a_spec, b_spec] + ([ANY] if aliased else []), out_specs=pl.BlockSpec(blk, bmap),
            out_shape=jax.ShapeDtypeStruct(buf.shape, buf.dtype),
            scratch_shapes=[acc_scratch],
            input_output_aliases={2: 0} if aliased else {},
            compiler_params=_cparams(("parallel", "parallel", "arbitrary")),
        )(a, b, *([buf] if aliased else []))
    return pl.pallas_call(
        body, name=name, grid=(M // tm, N // tn, nk),
        in_specs=[a_spec, b_spec] + [mn] * len(extras) + [rw] * len(rows), out_specs=[mn] * no + [rw] * n_acc,
        out_shape=[jax.ShapeDtypeStruct((M, N), dt) for dt in out_dtypes] + [jax.ShapeDtypeStruct((1, N), F32)] * n_acc,
        scratch_shapes=[acc_scratch],
        compiler_params=_cparams(("arbitrary",) * 3 if n_acc else ("parallel", "parallel", "arbitrary")),
    )(a, b, *extras, *rows)


def _epi_res(acc, res):
    return (res + acc,)


def _epi_rms_bwd(acc, x, dres, w):
    return rms_bwd_fn(0, 0, x, acc, dres, w)


def rms_bwd1_fn(i, n, x, dh, dres, w):
    dx, _, gw = rms_bwd_fn(i, n, x, dh, dres, w)
    return dx, gw


def _epi_final(acc, res, tgt, w):
    return final_fn(0, 0, res + acc, tgt, w)


def _epi_res_rms(acc, res, w):
    x = res + acc
    return (x, x * lax.rsqrt(jnp.mean(x * x, axis=-1, keepdims=True) + EPS) * w)


def _epi_relu2(acc):
    u = jnp.maximum(acc, 0.0)
    return (u, u * u)


def _epi_dup(acc, u):
    return (acc * 2.0 * u.astype(F32),)


def _conv(x, halo, w, i):
    halo = jnp.where(i == 0, 0.0, halo)
    xt = jnp.concatenate([halo, x], axis=0)
    shifted = [pltpu.roll(xt, 3 - k, 0)[8:, :] for k in range(3)] + [x]
    y = shifted[3] * w[3:4, :]
    for k in range(3):
        y = y + shifted[k] * w[k:k + 1, :]
    return y, shifted


def _l2n(x, scale):
    outs = []
    for h in range(x.shape[1] // 128):
        xh = x[:, 128 * h:128 * h + 128]
        outs.append(xh * (lax.rsqrt(jnp.sum(xh * xh, axis=-1, keepdims=True) + EPS) * scale))
    return jnp.concatenate(outs, axis=1)


def _l2n_bwd(x, dy, scale):
    outs = []
    for h in range(x.shape[1] // 128):
        xh, dh = x[:, 128 * h:128 * h + 128], dy[:, 128 * h:128 * h + 128] * scale
        r = lax.rsqrt(jnp.sum(xh * xh, axis=-1, keepdims=True) + EPS)
        outs.append(r * dh - xh * (r * r * r) * jnp.sum(xh * dh, axis=-1, keepdims=True))
    return jnp.concatenate(outs, axis=1)


def rms_fwd_fn(i, n, x, w):
    r = lax.rsqrt(jnp.mean(x * x, axis=-1, keepdims=True) + EPS)
    return (x * r * w,)


def rms_bwd_fn(i, n, x, dh, dres, w):
    r = lax.rsqrt(jnp.mean(x * x, axis=-1, keepdims=True) + EPS)
    g = dh * w
    dx = dres + r * g - x * (r * r * r) * jnp.mean(x * g, axis=-1, keepdims=True)
    return dx, dx, _rows(dh * x * r)


def rms_bwd_w_fn(i, n, x, dh, w):
    r = lax.rsqrt(jnp.mean(x * x, axis=-1, keepdims=True) + EPS)
    return (_rows(dh * x * r),)


def final_fn(i, n, x, tgt, w):
    r = lax.rsqrt(jnp.mean(x * x, axis=-1, keepdims=True) + EPS)
    xn = x * r
    e = xn * w - tgt
    dy = e * (1.0 / D)
    g = dy * w
    dx = r * g - x * (r * r * r) * jnp.mean(x * g, axis=-1, keepdims=True)
    return dx, dx, _rows(e * e), _rows(dy * xn)


def _gdn_gates(ba, alog_c, dtb_c):
    col = _iota(ba.shape, 1)
    amask = (col >= 8) & (col < 16)
    beta = jnp.where(col < 8, _sig(ba), 0.0)
    z = ba + dtb_c
    ea_ = jnp.exp(alog_c)
    return beta, z, ea_, jnp.where(amask, -ea_ * _softplus(z), 0.0), amask


def _cols(x, g):
    return x[:, 128 * g:128 * g + 128]


def gdn_prep_fn(i, n, qkv, halo, ba, cw, alog_c, dtb_c, eb, ea):
    outs = [[], [], []]
    for g in range(3 * GDN_H):
        yc, _ = _conv(_cols(qkv, g), _cols(halo, g), _cols(cw, g), i)
        act = yc * _sig(yc)
        if g < 2 * GDN_H:
            act = _l2n(act, GDN_DK ** -0.5 if g < GDN_H else 1.0)
        outs[g // GDN_H].append(act)
    beta, _, _, gg, _ = _gdn_gates(ba, alog_c, dtb_c)
    gcs = _chunk_cumsum(gg, GDN_C)
    return (*[jnp.concatenate(o, axis=1) for o in outs], mm_sel(gcs, ea), mm_sel(beta, eb), jnp.transpose(gcs)[8:16, :])


def gdn_prep_bwd_fn(i, n, qkv, halo, ba, dqn, dkn, dv, dgcs_x, dbeta_x, dgcs_t, cw, alog_c, dtb_c, pb, pa):
    dycs, dwl = [], [[], [], [], []]
    for g in range(3 * GDN_H):
        yc, shifted = _conv(_cols(qkv, g), _cols(halo, g), _cols(cw, g), i)
        sg = _sig(yc)
        act = yc * sg
        if g < GDN_H:
            d = _l2n_bwd(act, _cols(dqn, g), GDN_DK ** -0.5)
        elif g < 2 * GDN_H:
            d = _l2n_bwd(act, _cols(dkn, g - GDN_H), 1.0)
        else:
            d = _cols(dv, g - 2 * GDN_H)
        dyc_g = d * (sg * (1.0 + yc * (1.0 - sg)))
        dycs.append(dyc_g)
        for k in range(4):
            dwl[k].append(_rows(dyc_g * shifted[k]))
    dyc = jnp.concatenate(dycs, axis=1)
    dws = [jnp.concatenate(l, axis=1) for l in dwl]
    beta, z, ea_, g, amask = _gdn_gates(ba, alog_c, dtb_c)
    tbn = ba.shape[0]
    rowpart = jnp.transpose(jnp.concatenate([jnp.zeros((8, tbn), F32), dgcs_t, jnp.zeros((112, tbn), F32)], axis=0))
    dg = _chunk_revcumsum(mm_sel(dgcs_x, pa) - rowpart, GDN_C)
    draw = jnp.where(amask, dg * (-ea_) * _sig(z), 0.0)
    dba = draw + mm_sel(dbeta_x, pb) * beta * (1.0 - beta)
    return (dyc, dba, dws[0], dws[1], dws[2], dws[3], _rows(dg * g), _rows(draw))


def conv_bwd_fn(i, n, dyc, halo, w):
    halo = jnp.where(i == n - 1, 0.0, halo)
    tb = dyc.shape[0]
    outs = []
    for g in range(dyc.shape[1] // 128):
        d, wg = _cols(dyc, g), _cols(w, g)
        xt = jnp.concatenate([d, _cols(halo, g)], axis=0)
        dx = d * wg[3:4, :]
        for k in range(3):
            dx = dx + pltpu.roll(xt, tb + 8 - (3 - k), 0)[:tb, :] * wg[k:k + 1, :]
        outs.append(dx)
    return (jnp.concatenate(outs, axis=1),)


def gdn_post_fn(i, n, o, z, w):
    outs = []
    for h in range(GDN_H):
        oh, zh = o[:, 128 * h:128 * h + 128], z[:, 128 * h:128 * h + 128]
        r = lax.rsqrt(jnp.mean(oh * oh, axis=-1, keepdims=True) + EPS)
        outs.append(oh * r * w * (zh * _sig(zh)))
    return (jnp.concatenate(outs, axis=1),)


def gdn_post_bwd_fn(i, n, o, z, doa, w):
    dos, dzs, dw = [], [], None
    for h in range(GDN_H):
        sl = slice(128 * h, 128 * h + 128)
        oh, zh, dh = o[:, sl], z[:, sl], doa[:, sl]
        r = lax.rsqrt(jnp.mean(oh * oh, axis=-1, keepdims=True) + EPS)
        s = _sig(zh)
        dn = dh * (zh * s)
        dzs.append(dh * (oh * r * w) * (s * (1.0 + zh * (1.0 - s))))
        t = _rows(dn * oh * r)
        dw = t if dw is None else dw + t
        g = dn * w
        dos.append(r * g - oh * (r * r * r) * jnp.mean(oh * g, axis=-1, keepdims=True))
    return jnp.concatenate(dos, axis=1), jnp.concatenate(dzs, axis=1), dw


def _ssd_gates(dtblk, alog_c, dtb_c):
    hmask = _iota(dtblk.shape, 1) < SSM_H
    z = dtblk + dtb_c
    return jnp.where(hmask, _softplus(z), 0.0), -jnp.exp(alog_c), z, hmask


def _silu_conv_cols(x, halo, w, b, i):
    outs = []
    for g in range(x.shape[1] // 128):
        yc, _ = _conv(_cols(x, g), _cols(halo, g), _cols(w, g), i)
        yc = yc + _cols(b, g)
        outs.append(yc * _sig(yc))
    return jnp.concatenate(outs, axis=1)


def _silu_conv_bwd_cols(x, halo, w, b, dout, i):
    dycs, dwl = [], [[], [], [], []]
    for g in range(x.shape[1] // 128):
        yc, shifted = _conv(_cols(x, g), _cols(halo, g), _cols(w, g), i)
        yc = yc + _cols(b, g)
        s = _sig(yc)
        dyc_g = _cols(dout, g) * (s * (1.0 + yc * (1.0 - s)))
        dycs.append(dyc_g)
        for k in range(4):
            dwl[k].append(_rows(dyc_g * shifted[k]))
    dyc = jnp.concatenate(dycs, axis=1)
    return dyc, [jnp.concatenate(l, axis=1) for l in dwl], _rows(dyc)


def ssd_prep_fn(i, n, xp, hx, bcp, hbc, dtblk, cwx, cwbc, cbx, cbbc, alog_c, dtb_c, e16):
    dt, a_neg, _, _ = _ssd_gates(dtblk, alog_c, dtb_c)
    acs = _chunk_cumsum(dt * a_neg, SSM_L)
    return (_silu_conv_cols(xp, hx, cwx, cbx, i), _silu_conv_cols(bcp, hbc, cwbc, cbbc, i), mm_sel(dt, e16),
            mm_sel(acs, e16), jnp.transpose(acs)[0:SSM_H, :])


def ssd_prep_bwd_fn(i, n, xp, hx, bcp, hbc, dtblk, dxs_a, dxs_b, db, dc, dgate, dacs_t, cwx, cwbc, cbx, cbbc, alog_c, dtb_c):
    dyx, dwx, dbx = _silu_conv_bwd_cols(xp, hx, cwx, cbx, dxs_a + dxs_b, i)
    dybc, dwbc, dbbc = _silu_conv_bwd_cols(bcp, hbc, cwbc, cbbc, jnp.concatenate([db, dc], axis=1), i)
    dt, a_neg, z, hmask = _ssd_gates(dtblk, alog_c, dtb_c)
    g0, g1 = dgate[:, :128], dgate[:, 128:]
    col = _iota(g0.shape, 1)
    lo, mid = col < 8, (col >= 8) & (col < 16)
    dacs_col = jnp.where(lo, g0, 0.0) + pltpu.roll(jnp.where(lo, g1, 0.0), 8, 1)
    ddt_dir = pltpu.roll(jnp.where(mid, g0, 0.0), 120, 1) + jnp.where(mid, g1, 0.0)
    tbn = dtblk.shape[0]
    rowpart = jnp.transpose(jnp.concatenate([dacs_t, jnp.zeros((128 - SSM_H, tbn), F32)], axis=0))
    da = _chunk_revcumsum(dacs_col - rowpart, SSM_L)
    draw = jnp.where(hmask, (ddt_dir + da * a_neg) * _sig(z), 0.0)
    return (dyx, dybc, draw, *dwx, *dwbc, dbx, dbbc, _rows(da * dt * a_neg), _rows(draw))


def _ssd_gate(y, xs, zs, d_x):
    y2 = y + xs * d_x
    s = _sig(zs)
    return y2, s, y2 * (zs * s)


def ssd_post_fn(i, n, y, xs, zs, d_x, nw):
    _, _, yg = _ssd_gate(y, xs, zs, d_x)
    outs = []
    for g in range(2):
        v = yg[:, 512 * g:512 * g + 512]
        outs.append(v * lax.rsqrt(jnp.mean(v * v, axis=-1, keepdims=True) + EPS))
    return (jnp.concatenate(outs, axis=1) * nw,)


def ssd_post_bwd_fn(i, n, y, xs, zs, dob, d_x, nw):
    y2, s, yg = _ssd_gate(y, xs, zs, d_x)
    gfull = dob * nw
    dygs, dnw = [], []
    for g in range(2):
        sl = slice(512 * g, 512 * g + 512)
        v, gg = yg[:, sl], gfull[:, sl]
        r = lax.rsqrt(jnp.mean(v * v, axis=-1, keepdims=True) + EPS)
        dygs.append(r * gg - v * (r * r * r) * jnp.mean(v * gg, axis=-1, keepdims=True))
        dnw.append(_rows(dob[:, sl] * v * r))
    dyg = jnp.concatenate(dygs, axis=1)
    dy2 = dyg * (zs * s)
    dzs = dyg * y2 * (s * (1.0 + zs * (1.0 - s)))
    return dy2, dy2 * d_x, dzs, jnp.concatenate(dnw, axis=1), _rows(dy2 * xs)


def _attn_probs(q, k):
    hs = [slice(MEM_DH * h, MEM_DH * h + MEM_DH) for h in range(MEM_H)]
    ss = [mm_nt(q[:, sl], k[:, sl]) * (MEM_DH ** -0.5) for sl in hs]
    es = [jnp.exp(s - jnp.max(s, axis=-1, keepdims=True)) for s in ss]
    return hs, [e / jnp.sum(e, axis=-1, keepdims=True) for e in es]


def attn_fn(i, n, q, k, v):
    hs, ps = _attn_probs(q, k)
    return (jnp.concatenate([mm(p, v[:, sl]) for p, sl in zip(ps, hs)], axis=1),)


def attn_bwd_fn(i, n, q, do, k, v):
    hs, ps = _attn_probs(q, k)
    dvs = [mm_tn(p, do[:, sl]) for p, sl in zip(ps, hs)]
    dps = [mm_nt(do[:, sl], v[:, sl]) for sl in hs]
    dss = [p * (dp - jnp.sum(dp * p, axis=-1, keepdims=True)) * (MEM_DH ** -0.5) for p, dp in zip(ps, dps)]
    dqs = [mm(ds, k[:, sl]) for ds, sl in zip(dss, hs)]
    dks = [mm_tn(ds, q[:, sl]) for ds, sl in zip(dss, hs)]
    return jnp.concatenate(dqs, axis=1), jnp.concatenate(dks, axis=1), jnp.concatenate(dvs, axis=1)


def add2_fn(i, n, sp, a, b):
    return (a + b,)


def sum4_fn(i, n, sp, a, b, c, d):
    return (((a.astype(F32) + b.astype(F32)) + c.astype(F32)) + d.astype(F32),)


def _adamw(w, g, m, v):
    m = ADAM_B1 * m + (1.0 - ADAM_B1) * g
    v = ADAM_B2 * v + (1.0 - ADAM_B2) * (g * g)
    m_hat = m / (1.0 - ADAM_B1 ** ADAM_STEP)
    v_hat = v / (1.0 - ADAM_B2 ** ADAM_STEP)
    delta = -ADAM_LR * (m_hat / (jnp.sqrt(v_hat) + ADAM_EPS) + ADAM_WD * w)
    return delta, m, v


def _gdn_stage1(q, k, v, gcs, grow, bb):
    C = GDN_C
    row, col = _iota((C, C), 0), _iota((C, C), 1)
    incl, strict = row >= col, row > col
    dmat = jnp.where(incl, jnp.exp(jnp.minimum(gcs[:, :C] - grow, 0.0)), 0.0)
    gam = jnp.exp(gcs)
    gl = gcs[C - 1:C, :]
    kb, vb = k * bb, v * bb
    kg = kb * gam
    lmat = jnp.where(strict, mm_nt(kb, k) * dmat, 0.0)
    pmat = jnp.where(incl, mm_nt(q, k) * dmat, 0.0)
    return dict(q=q, k=k, v=v, bb=bb, incl=incl, strict=strict, dmat=dmat, gam=gam, kb=kb, vb=vb, kg=kg,
                lmat=lmat, pmat=pmat, qd=q * gam, kdec=jnp.exp(gl - gcs), cd=jnp.exp(gl))


def _gdn_inverse(lmats):
    C = GDN_C
    eye = (_iota((C, C), 0) == _iota((C, C), 1)).astype(F32)
    xs = [-l for l in lmats]
    ts = [eye + x for x in xs]
    for _ in range(5):
        xs = [mm(x, x) for x in xs]
        ts = [t + mm(t, x) for t, x in zip(ts, xs)]
    res = [eye - mm3(eye + l, t) for l, t in zip(lmats, ts)]
    return [t + mm(t, r) for t, r in zip(ts, res)]


def gdn_fwd(qn, kn, v, gcs_x, beta_x, gcs_t, tb, gh):
    T = qn.shape[0]
    nb, ncb, nc, C = T // tb, tb // GDN_C, T // GDN_C, GDN_C
    idx = [(hh, c) for hh in range(gh) for c in range(ncb)]

    def body(q_ref, k_ref, v_ref, g_ref, b_ref, gt_ref, o_ref, st_ref, ti_ref, s_scr):
        @pl.when(pl.program_id(1) == 0)
        def _():
            s_scr[...] = jnp.zeros_like(s_scr)

        grows = [gt_ref[hh] for hh in range(gh)]
        at = lambda hh, c: (slice(C * c, C * (c + 1)), slice(128 * hh, 128 * hh + 128))
        st1 = []
        for hh, c in idx:
            sl, ln = at(hh, c)
            st1.append(_gdn_stage1(q_ref[sl, ln], k_ref[sl, ln], v_ref[sl, ln], g_ref[sl, ln], grows[hh][:, sl],
                                   b_ref[sl, ln]))
        tinvs = _gdn_inverse([s["lmat"] for s in st1])
        us = [mm(t, s["vb"]) for t, s in zip(tinvs, st1)]
        ws = [mm(t, s["kg"]) for t, s in zip(tinvs, st1)]
        kds = [s["k"] * s["kdec"] for s in st1]
        ms = [mm_tn(kd, w) for kd, w in zip(kds, ws)]
        bs = [mm_tn(kd, u) for kd, u in zip(kds, us)]
        gs = [s["qd"] - mm(s["pmat"], w) for s, w in zip(st1, ws)]
        pus = [mm(s["pmat"], u) for s, u in zip(st1, us)]
        ss = [s_scr[hh] for hh in range(gh)]
        for c in range(ncb):
            for hh in range(gh):
                n, (sl, ln) = hh * ncb + c, at(hh, c)
                ti_ref[hh, sl, :] = tinvs[n]
                st_ref[hh, c] = ss[hh]
                o_ref[sl, ln] = mm(gs[n], ss[hh]) + pus[n]
                ss[hh] = st1[n]["cd"] * ss[hh] - mm(ms[n], ss[hh]) + bs[n]
        for hh in range(gh):
            s_scr[hh] = ss[hh]

    blk = pl.BlockSpec((tb, 128 * gh), lambda h, i: (i, h))
    return pl.pallas_call(
        body, name="gdn_fwd", grid=(GDN_H // gh, nb),
        in_specs=[blk] * 5 + [pl.BlockSpec((gh, 1, tb), lambda h, i: (h, 0, i))],
        out_specs=[blk, pl.BlockSpec((gh, ncb, 128, 128), lambda h, i: (h, i, 0, 0)),
                   pl.BlockSpec((gh, tb, C), lambda h, i: (h, i, 0))],
        out_shape=[jax.ShapeDtypeStruct((T, D), F32), jax.ShapeDtypeStruct((GDN_H, nc, 128, 128), F32),
                   jax.ShapeDtypeStruct((GDN_H, T, C), F32)],
        scratch_shapes=[pltpu.VMEM((gh, 128, 128), F32)],
        compiler_params=_cparams(("parallel", "arbitrary")),
    )(qn, kn, v, gcs_x, beta_x, gcs_t)


def gdn_bwd(qn, kn, v, gcs_x, beta_x, gcs_t, do, states, tinv, tb, gh):
    T = qn.shape[0]
    nb, ncb, C = T // tb, tb // GDN_C, GDN_C

    def body(q_ref, k_ref, v_ref, g_ref, b_ref, gt_ref, do_ref, st_ref, ti_ref,
             dq_ref, dk_ref, dv_ref, dgc_ref, db_ref, dgr_ref, ds_scr):
        @pl.when(pl.program_id(1) == 0)
        def _():
            ds_scr[...] = jnp.zeros_like(ds_scr)

        grows = [gt_ref[hh] for hh in range(gh)]
        at = lambda hh, c: (slice(C * c, C * (c + 1)), slice(128 * hh, 128 * hh + 128))
        lastrow = _iota((C, 1), 0) == C - 1
        idx = [(hh, c) for hh in range(gh) for c in range(ncb)]
        P = []
        for hh, c in idx:
            sl, ln = at(hh, c)
            lc = _gdn_stage1(q_ref[sl, ln], k_ref[sl, ln], v_ref[sl, ln], g_ref[sl, ln], grows[hh][:, sl],
                             b_ref[sl, ln])
            lc.update(tinv=ti_ref[hh, sl, :], s=st_ref[hh, c], do=do_ref[sl, ln], kd=lc["k"] * lc["kdec"])
            P.append(lc)
        for l, u, w in zip(P, [mm(l["tinv"], l["vb"]) for l in P], [mm(l["tinv"], l["kg"]) for l in P]):
            l.update(u=u, w=w)
        for l, x in zip(P, [mm(l["w"], l["s"]) for l in P]):
            l["vn"] = l["u"] - x
        for l, a, b, c_, d in zip(P, [mm_nt(l["do"], l["s"]) for l in P], [mm_nt(l["do"], l["vn"]) for l in P],
                                  [mm_tn(l["qd"], l["do"]) for l in P], [mm_tn(l["pmat"], l["do"]) for l in P]):
            l.update(dqd=a, dp=jnp.where(l["incl"], b, 0.0), ds_q=c_, dvn_p=d)
        pre = dict(zip(idx, P))
        rows = {}
        hs = range(gh)
        ds = [ds_scr[hh] for hh in hs]
        for c in reversed(range(ncb)):
            L = [pre[hh, c] for hh in hs]
            dvn = [l["dvn_p"] + mm(l["kd"], d) for l, d in zip(L, ds)]
            dkd = [mm_nt(l["vn"], d) for l, d in zip(L, ds)]
            dcd = [_sum_all(l["s"] * d) for l, d in zip(L, ds)]
            ds = [l["ds_q"] + l["cd"] * d - mm_tn(l["w"], x) for l, d, x in zip(L, ds, dvn)]
            dw = [-mm_nt(x, l["s"]) for l, x in zip(L, dvn)]
            dvb = [mm_tn(l["tinv"], x) for l, x in zip(L, dvn)]
            dkg = [mm_tn(l["tinv"], x) for l, x in zip(L, dw)]
            da = [-jnp.where(l["strict"], mm_nt(a, l["u"]) + mm_nt(b, l["w"]), 0.0) for l, a, b in zip(L, dvb, dkg)]
            dm = [a * l["dmat"] for l, a in zip(L, da)]
            dn = [l["dp"] * l["dmat"] for l in L]
            dkb = [mm(a, l["k"]) for l, a in zip(L, dm)]
            dq = [mm(a, l["k"]) + l["gam"] * l["dqd"] for l, a in zip(L, dn)]
            dk = [mm_tn(a, l["kb"]) + mm_tn(b, l["q"]) for l, a, b in zip(L, dm, dn)]
            for hh in hs:
                sl, ln = at(hh, c)
                l = L[hh]
                e = da[hh] * l["lmat"] + l["dp"] * l["pmat"]
                t_kd = _lanes(dkd[hh] * l["kd"])
                dgl = _sum_all(t_kd) + dcd[hh] * l["cd"][:, :1]
                dgcs = (_lanes(e) + _lanes(l["dqd"] * l["qd"]) - t_kd + _lanes(dkg[hh] * l["kg"])
                        + jnp.where(lastrow, dgl, 0.0))
                rows[hh, c] = _rows(e)
                dq_ref[sl, ln] = dq[hh]
                dk_ref[sl, ln] = (dk[hh] + l["kdec"] * dkd[hh] + l["bb"] * l["gam"] * dkg[hh] + l["bb"] * dkb[hh])
                dv_ref[sl, ln] = l["bb"] * dvb[hh]
                dbeta = _lanes(dkg[hh] * l["gam"] * l["k"]) + _lanes(dvb[hh] * l["v"]) + _lanes(dkb[hh] * l["k"])
                db_ref[sl, ln] = jnp.broadcast_to(dbeta, (C, 128))
                dgc_ref[sl, ln] = jnp.broadcast_to(dgcs, (C, 128))
        for hh in hs:
            ds_scr[hh] = ds[hh]
            dgr_ref[hh] = jnp.concatenate([rows[hh, c] for c in range(ncb)], axis=1)

    blk = pl.BlockSpec((tb, 128 * gh), lambda h, i: (nb - 1 - i, h))
    rowspec = pl.BlockSpec((gh, 1, tb), lambda h, i: (h, 0, nb - 1 - i))
    return pl.pallas_call(
        body, name="gdn_bwd", grid=(GDN_H // gh, nb),
        in_specs=[blk] * 5 + [rowspec, blk,
                              pl.BlockSpec((gh, ncb, 128, 128), lambda h, i: (h, nb - 1 - i, 0, 0)),
                              pl.BlockSpec((gh, tb, C), lambda h, i: (h, nb - 1 - i, 0))],
        out_specs=[blk] * 5 + [rowspec],
        out_shape=[jax.ShapeDtypeStruct((T, D), F32)] * 5 + [jax.ShapeDtypeStruct((GDN_H, 1, T), F32)],
        scratch_shapes=[pltpu.VMEM((gh, 128, 128), F32)],
        compiler_params=_cparams(("parallel", "arbitrary")),
    )(qn, kn, v, gcs_x, beta_x, gcs_t, do, states, tinv)


def _ssd_pair(x2, dt2, acs2):
    last = acs2[SSM_L - 1:SSM_L, :]
    return jnp.exp(acs2), jnp.exp(last - acs2), x2 * dt2


def _ssd_head(hh, acs2, arow, dec2, cbm, bm, incl, col):
    lmask = (col >= 64 * hh) & (col < 64 * hh + 64)
    sg = jnp.where(incl, jnp.exp(jnp.minimum(acs2[:, 64 * hh:64 * hh + 1] - arow, 0.0)), 0.0)
    dec_col = dec2[:, 64 * hh:64 * hh + 1]
    return lmask, sg, sg * cbm, dec_col, bm * dec_col


def ssd_fwd(xs, bc, dt_x, acs_x, acs_t):
    T = xs.shape[0]
    nc, L = T // SSM_L, SSM_L

    def body(x_ref, bc_ref, dt_ref, ac_ref, at_ref, y_ref, hst_ref, h_scr):
        @pl.when(pl.program_id(0) == 0)
        def _():
            h_scr[...] = jnp.zeros_like(h_scr)

        row, col = _iota((L, L), 0), _iota((L, L), 1)
        incl = row >= col
        P, H = [], []
        for gp in range(8):
            g = gp // 4
            bm, cm = bc_ref[:, 128 * g:128 * g + 128], bc_ref[:, 256 + 128 * g:384 + 128 * g]
            cbm = mm_nt(cm, bm) if gp % 4 == 0 else cbm
            sl = slice(128 * gp, 128 * gp + 128)
            acs2 = ac_ref[:, sl]
            lam2, dec2, xd2 = _ssd_pair(x_ref[:, sl], dt_ref[:, sl], acs2)
            P.append(dict(sl=sl, lam2=lam2, xd2=xd2, hprev=h_scr[gp], cm=cm))
            for hh in range(2):
                lmask, _, mmat, _, bd = _ssd_head(hh, acs2, at_ref[2 * gp + hh], dec2, cbm, bm, incl, col)
                H.append(dict(mmat=mmat, bd=bd, xdh=jnp.where(lmask, xd2, 0.0), xd2=xd2))
        ys = [mm(h["mmat"], h["xdh"]) for h in H]
        sts = [mm_tn(h["xd2"], h["bd"]) for h in H]
        zs = [mm_nt(p["cm"], p["hprev"]) for p in P]
        for gp, p in enumerate(P):
            hst_ref[gp // 4, gp % 4] = p["hprev"]
            y_ref[:, p["sl"]] = ys[2 * gp] + ys[2 * gp + 1] + p["lam2"] * zs[gp]
            lam_rows = jnp.where(row < 64, p["lam2"][L - 1:L, 0:1], p["lam2"][L - 1:L, 64:65])
            h_scr[gp] = lam_rows * p["hprev"] + jnp.where(row < 64, sts[2 * gp], sts[2 * gp + 1])

    blk = pl.BlockSpec((L, D), lambda c: (c, 0))
    return pl.pallas_call(
        body, name="ssd_fwd", grid=(nc,),
        in_specs=[blk, pl.BlockSpec((L, 512), lambda c: (c, 0)), blk, blk, pl.BlockSpec((SSM_H, 1, L), lambda c: (0, 0, c))],
        out_specs=[blk, pl.BlockSpec((2, None, 4, 128, 128), lambda c: (0, c, 0, 0, 0))],
        out_shape=[jax.ShapeDtypeStruct((T, D), F32), jax.ShapeDtypeStruct((2, nc, 4, 128, 128), F32)],
        scratch_shapes=[pltpu.VMEM((8, 128, 128), F32)],
        compiler_params=_cparams(("arbitrary",)),
    )(xs, bc, dt_x, acs_x, acs_t)


def ssd_bwd(xs, bc, dt_x, acs_x, acs_t, dy, hstates):
    T = xs.shape[0]
    nc, L = T // SSM_L, SSM_L

    def body(x_ref, bc_ref, dt_ref, ac_ref, at_ref, dy_ref, hst_ref,
             dx_ref, db_ref, dc_ref, dgate_ref, dar_ref, dh_scr):
        @pl.when(pl.program_id(0) == 0)
        def _():
            dh_scr[...] = jnp.zeros_like(dh_scr)

        row, col = _iota((L, L), 0), _iota((L, L), 1)
        rowc = _iota((L, 1), 0)
        incl = row >= col
        G = [dict(bm=bc_ref[:, 128 * g:128 * g + 128], cm=bc_ref[:, 256 + 128 * g:384 + 128 * g]) for g in range(2)]
        for gr in G:
            gr["cbm"] = mm_nt(gr["cm"], gr["bm"])
        P = []
        for gp in range(8):
            sl = slice(128 * gp, 128 * gp + 128)
            x2, dt2, dy2, acs2 = x_ref[:, sl], dt_ref[:, sl], dy_ref[:, sl], ac_ref[:, sl]
            lam2, dec2, xd2 = _ssd_pair(x2, dt2, acs2)
            P.append(dict(sl=sl, gr=G[gp // 4], x2=x2, dt2=dt2, dy2=dy2, acs2=acs2, lam2=lam2, dec2=dec2, xd2=xd2,
                          hprev=hst_ref[gp // 4, gp % 4], dhn=dh_scr[gp], dz=lam2 * dy2))
        zs = [mm_nt(p["gr"]["cm"], p["hprev"]) for p in P]
        dcm_t = [mm(p["dz"], p["hprev"]) for p in P]
        dh_z = [mm_tn(p["dz"], p["gr"]["cm"]) for p in P]
        H = []
        for gp, p in enumerate(P):
            p["yoff"] = p["dz"] * zs[gp]
            p["q_rows"] = _lanes(p["dhn"] * p["hprev"])
            for hh in range(2):
                lmask, sg, mmat, dec_col, bd = _ssd_head(hh, p["acs2"], at_ref[2 * gp + hh], p["dec2"], p["gr"]["cbm"],
                                                         p["gr"]["bm"], incl, col)
                H.append(dict(p=p, hh=hh, j=2 * gp + hh, lmask=lmask, sg=sg, mmat=mmat, dec_col=dec_col, bd=bd))
        dms = [mm_nt(jnp.where(h["lmask"], h["p"]["dy2"], 0.0), h["p"]["xd2"]) for h in H]
        a1s = [mm_tn(h["mmat"], h["p"]["dy2"]) for h in H]
        a2s = [mm_nt(h["bd"], h["p"]["dhn"]) for h in H]
        dbds = [mm(jnp.where(h["lmask"], h["p"]["xd2"], 0.0), h["p"]["dhn"]) for h in H]
        for gr in G:
            gr.update(dcb=jnp.zeros((L, L), F32), dbm=jnp.zeros((L, SSM_N), F32), comp=jnp.zeros((L, 128), F32))
        dxd = [jnp.zeros((L, 128), F32) for _ in P]
        for h, dm_raw, a1, a2, dbd in zip(H, dms, a1s, a2s, dbds):
            p, hh, j = h["p"], h["hh"], h["j"]
            gr, jg = p["gr"], j % 8
            dm = jnp.where(incl, dm_raw, 0.0)
            gr["dcb"] = gr["dcb"] + dm * h["sg"]
            e = dm * h["mmat"]
            dxd_h = jnp.where(h["lmask"], a1 + a2, 0.0)
            dxd[j // 2] = dxd[j // 2] + dxd_h
            gr["dbm"] = gr["dbm"] + h["dec_col"] * dbd
            t = _lanes(dbd * h["bd"])
            lam_h = p["lam2"][L - 1:L, 64 * hh:64 * hh + 1]
            in_head = (rowc >= 64 * hh) & (rowc < 64 * hh + 64)
            add_last = _sum_all(t) + _sum_all(jnp.where(in_head, p["q_rows"], 0.0)) * lam_h
            dacs_col = (_lanes(jnp.where(h["lmask"], p["yoff"], 0.0)) + _lanes(e) - t
                        + jnp.where(rowc == L - 1, add_last, 0.0))
            ddt_col = _lanes(dxd_h * p["x2"])
            dar_ref[j] = _rows(e)
            gr["comp"] = gr["comp"] + jnp.where(col == jg, dacs_col, 0.0) + jnp.where(col == 8 + jg, ddt_col, 0.0)
        for gp, p in enumerate(P):
            lam_rows = jnp.where(row < 64, p["lam2"][L - 1:L, 0:1], p["lam2"][L - 1:L, 64:65])
            dh_scr[gp] = dh_z[gp] + lam_rows * p["dhn"]
            dx_ref[:, p["sl"]] = p["dt2"] * dxd[gp]
        for g, gr in enumerate(G):
            lanes = slice(128 * g, 128 * g + 128)
            dcm = (dcm_t[4 * g] + dcm_t[4 * g + 1]) + (dcm_t[4 * g + 2] + dcm_t[4 * g + 3])
            db_ref[:, lanes] = gr["dbm"] + mm_tn(gr["dcb"], gr["cm"])
            dc_ref[:, lanes] = dcm + mm(gr["dcb"], gr["bm"])
            dgate_ref[:, lanes] = gr["comp"]

    rv = lambda c: (nc - 1 - c, 0)
    blk, blk256 = pl.BlockSpec((L, D), rv), pl.BlockSpec((L, 256), rv)
    rowspec = pl.BlockSpec((SSM_H, 1, L), lambda c: (0, 0, nc - 1 - c))
    return pl.pallas_call(
        body, name="ssd_bwd", grid=(nc,),
        in_specs=[blk, pl.BlockSpec((L, 512), rv), blk, blk, rowspec, blk,
                  pl.BlockSpec((2, None, 4, 128, 128), lambda c: (0, nc - 1 - c, 0, 0, 0))],
        out_specs=[blk, blk256, blk256, blk256, rowspec],
        out_shape=[jax.ShapeDtypeStruct((T, D), F32), jax.ShapeDtypeStruct((T, 256), F32),
                   jax.ShapeDtypeStruct((T, 256), F32), jax.ShapeDtypeStruct((T, 256), F32),
                   jax.ShapeDtypeStruct((SSM_H, 1, T), F32)],
        scratch_shapes=[pltpu.VMEM((8, 128, 128), F32)],
        compiler_params=_cparams(("arbitrary",)),
    )(xs, bc, dt_x, acs_x, acs_t, dy, hstates)


def _pos():
    return lax.axis_index("x"), lax.axis_index("y"), lax.axis_index("c")


def _other_chips(x, y):
    return [(1 - x, y), (x, 1 - y), (1 - x, 1 - y)]


def _rcopy(src, dst, ssem, rsem, dev):
    return pltpu.make_async_remote_copy(src_ref=src, dst_ref=dst, send_sem=ssem, recv_sem=rsem,
                                        device_id=dev, device_id_type=MESH)


def _rows_at(start, n):
    return pl.ds(pl.multiple_of(start, 8), n)


def _comm_call(body, name, out_shape, n_in, scratch):
    return pl.pallas_call(
        body, name=name, out_shape=out_shape, in_specs=[ANY] * n_in,
        out_specs=[ANY] * len(out_shape) if isinstance(out_shape, (list, tuple)) else ANY,
        scratch_shapes=scratch,
        compiler_params=pltpu.CompilerParams(has_side_effects=True),
    )


def _dma_sems(n):
    return pltpu.SemaphoreType.DMA((n,))


def ag_chips(name, shard):
    rr, cc = shard.shape
    h, nq = rr // 2, ICI_CHUNKS
    hq = h // nq

    def body(x_ref, out_ref, ssem, rsem):
        x, y, c = _pos()
        me_s = 2 * x + y
        chips = _other_chips(x, y)
        started = []
        for q in range(nq):
            rows = _rows_at(c * h + q * hq, hq)
            for j, (cx, cy) in enumerate(chips):
                cp = _rcopy(x_ref.at[rows], out_ref.at[me_s, rows], ssem.at[j * nq + q], rsem.at[j * nq + q], (cx, cy, c))
                cp.start()
                started.append(cp)
        for q in range(nq):
            rows = _rows_at(c * h + q * hq, hq)
            for j, (cx, cy) in enumerate(chips):
                blk = out_ref.at[2 * cx + cy, rows]
                _rcopy(blk, blk, ssem.at[j * nq + q], rsem.at[j * nq + q], (cx, cy, c)).wait_recv()
                k = 3 * nq + j * nq + q
                cp = _rcopy(blk, blk, ssem.at[k], rsem.at[k], (x, y, 1 - c))
                cp.start()
                started.append(cp)
        for q in range(nq):
            rows = _rows_at((1 - c) * h + q * hq, hq)
            for j, (cx, cy) in enumerate(chips):
                blk = out_ref.at[2 * cx + cy, rows]
                k = 3 * nq + j * nq + q
                _rcopy(blk, blk, ssem.at[k], rsem.at[k], (x, y, 1 - c)).wait_recv()
        for cp in started:
            cp.wait_send()

    return _comm_call(body, name, jax.ShapeDtypeStruct((4, rr, cc), shard.dtype), 1,
                      [_dma_sems(6 * nq), _dma_sems(6 * nq)])(shard)


def _with_own(shard, got, s_me):
    return lax.dynamic_update_index_in_dim(got, shard, s_me, 0)


def all_gather_chips(name, shard, s_me):
    return _with_own(shard, ag_chips(name, shard), s_me)


HBM_SPEC = pl.BlockSpec(memory_space=pltpu.HBM)
SEM_SPEC = pl.BlockSpec(memory_space=pltpu.SEMAPHORE)
SPLIT_EFFECT = pltpu.SideEffectType.DATAFLOW_SIDE_EFFECTING


def _split_copies(pieces, x_ref, land_ref, sems, arriving):
    x, y, c = _pos()
    return [_rcopy(s, d_in if arriving else d_out, sems[j], sems[3 + j], dev)
            for j, (s, d_out, d_in, dev) in enumerate(pieces(x_ref, land_ref, x, y, c))]


def split_copy_start(name, src, land_shape, pieces, after):
    def body(x_ref, land_ref, after_ref, *outs):
        for cp in _split_copies(pieces, x_ref, land_ref, outs[:6], False):
            cp.start()
        outs[8][...] = jnp.zeros_like(outs[8])

    dma = pltpu.SemaphoreType.DMA(())
    res = pl.pallas_call(
        body, name=name,
        out_shape=(dma,) * 6 + (pltpu.HBM(src.shape, src.dtype), pltpu.HBM(land_shape, src.dtype),
                                jax.ShapeDtypeStruct((8, 128), F32)),
        in_specs=(HBM_SPEC, HBM_SPEC, ANY),
        out_specs=(SEM_SPEC,) * 6 + (HBM_SPEC, HBM_SPEC, pl.BlockSpec(memory_space=pltpu.VMEM)),
        input_output_aliases={0: 6, 1: 7},
        compiler_params=pltpu.CompilerParams(has_side_effects=SPLIT_EFFECT),
    )(pltpu.with_memory_space_constraint(src, pltpu.HBM),
      pltpu.with_memory_space_constraint(lax.empty(land_shape, src.dtype), pltpu.HBM), after)
    return res[:6], res[6], res[7], res[8]


def split_copy_wait(name, sems, src_thru, land_thru, after, pieces):
    def body(x_ref, land_ref, *rest):
        for cp in _split_copies(pieces, x_ref, land_ref, rest[:6], False):
            cp.wait_send()
        for cp in _split_copies(pieces, x_ref, land_ref, rest[:6], True):
            cp.wait_recv()

    return pl.pallas_call(
        body, name=name,
        out_shape=(pltpu.HBM(src_thru.shape, src_thru.dtype), pltpu.HBM(land_thru.shape, land_thru.dtype)),
        in_specs=(HBM_SPEC, HBM_SPEC) + (SEM_SPEC,) * 6 + (ANY,), out_specs=(HBM_SPEC, HBM_SPEC),
        input_output_aliases={0: 0, 1: 1},
        compiler_params=pltpu.CompilerParams(has_side_effects=SPLIT_EFFECT),
    )(src_thru, land_thru, *sems, after)


def ag_pieces(h):
    def pieces(x_ref, land_ref, x, y, c):
        rows = _rows_at(c * h, h)
        return [(x_ref.at[rows], land_ref.at[2 * x + y, rows], land_ref.at[2 * cx + cy, rows], (cx, cy, c))
                for cx, cy in _other_chips(x, y)]
    return pieces


def rs_pieces(x_ref, land_ref, x, y, c):
    return [(x_ref.at[2 * cx + cy], land_ref.at[j], land_ref.at[j], (cx, cy, c))
            for j, (cx, cy) in enumerate(_other_chips(x, y))]


def ag_forward(name, got):
    _, rr, cc = got.shape
    h, nq = rr // 2, D2D_CHUNKS
    hq = h // nq

    def body(g_ref, out_ref, ssem, rsem):
        x, y, c = _pos()
        slots = [2 * cx + cy for cx, cy in _other_chips(x, y)]
        cps = []
        for j, s in enumerate(slots):
            for q in range(nq):
                blk = out_ref.at[s, _rows_at(c * h + q * hq, hq)]
                cp = _rcopy(blk, blk, ssem.at[j * nq + q], rsem.at[j * nq + q], (x, y, 1 - c))
                cp.start()
                cps.append(cp)
        for cp in cps:
            cp.wait_send()
        for j, s in enumerate(slots):
            for q in range(nq):
                blk = out_ref.at[s, _rows_at((1 - c) * h + q * hq, hq)]
                _rcopy(blk, blk, ssem.at[j * nq + q], rsem.at[j * nq + q], (x, y, 1 - c)).wait_recv()

    return pl.pallas_call(
        body, name=name, out_shape=jax.ShapeDtypeStruct(got.shape, got.dtype), in_specs=[ANY], out_specs=ANY,
        scratch_shapes=[_dma_sems(3 * nq), _dma_sems(3 * nq)], input_output_aliases={0: 0},
        compiler_params=pltpu.CompilerParams(has_side_effects=True),
    )(got)


def rs_pair(name, g):
    _, rr, cc = g.shape
    h, nq = rr // 2, D2D_CHUNKS
    hq = h // nq

    def body(g_ref, recv_ref, ssem, rsem):
        x, y, c = _pos()
        cps = []
        for q in range(nq):
            cp = _rcopy(g_ref.at[:, _rows_at((1 - c) * h + q * hq, hq), :], recv_ref.at[:, pl.ds(q * hq, hq), :],
                        ssem.at[q], rsem.at[q], (x, y, 1 - c))
            cp.start()
            cps.append(cp)
        for cp in cps:
            cp.wait()

    return _comm_call(body, name, jax.ShapeDtypeStruct((4, h, cc), g.dtype), 1, [_dma_sems(nq), _dma_sems(nq)])(g)


def rs_chips(name, p):
    _, h, cc = p.shape
    nq = ICI_CHUNKS
    hq = h // nq

    def body(p_ref, buf_ref, ssem, rsem):
        x, y, c = _pos()
        sends = []
        for q in range(nq):
            rows = pl.ds(q * hq, hq)
            for j, (cx, cy) in enumerate(_other_chips(x, y)):
                cp = _rcopy(p_ref.at[2 * cx + cy, rows], buf_ref.at[j, rows], ssem.at[j * nq + q],
                            rsem.at[j * nq + q], (cx, cy, c))
                cp.start()
                sends.append(cp)
        for cp in sends:
            cp.wait()

    return _comm_call(body, name, jax.ShapeDtypeStruct((3, h, cc), p.dtype), 1,
                      [_dma_sems(3 * nq), _dma_sems(3 * nq)])(p)


def rs_join(name, half):
    h, cc = half.shape
    nq = D2D_CHUNKS
    hq = h // nq

    def body(h_ref, out_ref, ssem, rsem):
        x, y, c = _pos()
        cps = []
        for q in range(nq):
            rows = pl.ds(q * hq, hq)
            cp = _rcopy(h_ref.at[rows], out_ref.at[rows], ssem.at[q], rsem.at[q], (x, y, 1 - c))
            cp.start()
            cps.append(cp)
        for cp in cps:
            cp.wait()

    return _comm_call(body, name, jax.ShapeDtypeStruct((h, cc), half.dtype), 1, [_dma_sems(nq), _dma_sems(nq)])(half)


def reduce_scatter(tag, g, tb, sp):
    return rs_end(rs_begin(tag, g, tb, sp, False), None)


def rs_begin(tag, g, tb, sp, split, after=None):
    _, rr, cc = g.shape
    h = rr // 2
    nbh = h // tb
    recv = rs_pair(tag + "_pair", g)
    mine_rows = lambda i, s: (i // nbh) * (2 * nbh) + s[0] * nbh + i % nbh
    part = rowwise(add2_fn, tag + "_add", 4 * h, tb, [R(g.reshape(4 * rr, cc), off=mine_rows), R(recv.reshape(4 * h, cc))],
                   [], [(cc, BF16)], sp=sp)[0].reshape(4, h, cc)
    st = dict(tag=tag, tb=tb, sp=sp, split=split, part=part)
    if split:
        st["sems"], st["part"], st["land"], st["token"] = split_copy_start(tag + "_start", part, (3, h, cc), rs_pieces,
                                                                           sp if after is None else after)
    return st


def rs_end(st, after):
    tag, tb, sp, part = st["tag"], st["tb"], st["sp"], st["part"]
    _, h, cc = part.shape
    nbh = h // tb
    if st["split"]:
        part, buf = split_copy_wait(tag + "_wait", st["sems"], part, st["land"], after, rs_pieces)
    else:
        buf = rs_chips(tag + "_chips", part)
    red = rowwise(sum4_fn, tag + "_sum", h, tb,
                  [R(part.reshape(4 * h, cc), off=lambda i, s: s[1] * nbh + i)]
                  + [R(buf.reshape(3 * h, cc), off=k * nbh) for k in range(3)],
                  [], [(cc, F32)], sp=sp)[0]
    return red, rs_join(tag + "_join", red)


def adam_halves(name, w, m, v, red, other, tb, blk0, sp):
    nbh = red.shape[0] // tb

    def fn(i, n, s, w_, m_, v_, r_, o_):
        g = jnp.where((blk0 + i) // nbh == s[0], r_, o_)
        return (g,) + _adamw(w_, g, m_, v_)

    half_rows = lambda i, s: (blk0 + i) % nbh
    return rowwise(fn, name, w.shape[0], tb, [R(w), R(m), R(v), R(red, off=half_rows), R(other, off=half_rows)],
                   [], [(w.shape[1], F32)] * 4, sp=sp)


SMALL_LANES = 3 * D


def all_reduce_items(name, items, after=None):
    flat = [a for it in items for a in it]
    shapes = [(sum(a.shape[0] for a in it), it[0].shape[1]) for it in items]
    nrows = -(-sum(s[0] for s in shapes) // 8) * 8
    extra = [] if after is None else [after]

    def body(*refs):
        ins, refs = refs[:len(flat)], refs[len(flat) + len(extra):]
        outs = refs[:len(items)]
        mine, buf, ssem, rsem = refs[len(items):]
        x, y, c = _pos()
        me = 4 * x + 2 * y + c
        mine[...] = jnp.zeros_like(mine)
        r = 0
        for ref in ins:
            mine[r:r + ref.shape[0], 0:ref.shape[1]] = ref[...]
            r += ref.shape[0]
        buf[me] = mine[...]
        cps = []
        for k in range(1, 8):
            dev = (x ^ (k >> 2), y ^ ((k >> 1) & 1), c ^ (k & 1))
            cp = _rcopy(mine, buf.at[me], ssem.at[k - 1], rsem.at[k - 1], dev)
            cp.start()
            cps.append(cp)
        for cp in cps:
            cp.wait()
        r = 0
        for (nr, n), out in zip(shapes, outs):
            acc = buf[0, r:r + nr, 0:n]
            for d in range(1, 8):
                acc = acc + buf[d, r:r + nr, 0:n]
            out[...] = acc
            r += nr

    vm = pl.BlockSpec(memory_space=pltpu.VMEM)
    return pl.pallas_call(
        body, name=name, out_shape=[jax.ShapeDtypeStruct(s, F32) for s in shapes],
        in_specs=[vm] * len(flat) + [ANY] * len(extra), out_specs=[vm] * len(items),
        scratch_shapes=[pltpu.VMEM((nrows, SMALL_LANES), F32), pltpu.VMEM((8, nrows, SMALL_LANES), F32),
                        _dma_sems(7), _dma_sems(7)],
        compiler_params=pltpu.CompilerParams(has_side_effects=True),
    )(*flat, *extra)


def adam_small(ws, gs, ms, vs):
    n = len(ws)

    def body(*refs):
        for k in range(n):
            w, g, m, v = (refs[j * n + k][...] for j in range(4))
            for j, val in enumerate(_adamw(w, g, m, v)):
                refs[(4 + j) * n + k][...] = val

    vm = pl.BlockSpec(memory_space=pltpu.VMEM)
    res = pl.pallas_call(
        body, name="adam_small", out_shape=[jax.ShapeDtypeStruct(w.shape, F32) for w in ws] * 3,
        in_specs=[vm] * (4 * n), out_specs=[vm] * (3 * n),
    )(*ws, *gs, *ms, *vs)
    return res[:n], res[n:2 * n], res[2 * n:]


def _sel(rows, cols, pairs):
    m = np.zeros((rows, cols), np.float32)
    for r, c in pairs:
        m[r, c] = 1.0
    return jnp.asarray(m)


def _pad_win(w):
    z = jnp.zeros((w.shape[0], 112), w.dtype)
    return jnp.concatenate([w[:, :4096], w[:, 4112:6672], w[:, 4096:4112], z, w[:, 6672:6688], z], axis=1)


def _unpad_win(wp):
    return jnp.concatenate([wp[:, :4096], wp[:, 6656:6672], wp[:, 4096:6656], wp[:, 6784:6800]], axis=1)


def kernel(x, mem, norm1_w, w_in, gdn_conv_w, gdn_a_log, gdn_dt_bias, gdn_norm_w, ssm_conv_w, ssm_conv_b, ssm_a_log, ssm_dt_bias, ssm_d, ssm_norm_w, w_out, norm2_w, mem_norm_w, wq_mem, wk_mem, wv_mem, wo_mem, norm3_w, w_up, w_down, final_norm_w, loss_target, m_norm1_w, m_w_in, m_gdn_conv_w, m_gdn_a_log, m_gdn_dt_bias, m_gdn_norm_w, m_ssm_conv_w, m_ssm_conv_b, m_ssm_a_log, m_ssm_dt_bias, m_ssm_d, m_ssm_norm_w, m_w_out, m_norm2_w, m_mem_norm_w, m_wq_mem, m_wk_mem, m_wv_mem, m_wo_mem, m_norm3_w, m_w_up, m_w_down, m_final_norm_w, v_norm1_w, v_w_in, v_gdn_conv_w, v_gdn_a_log, v_gdn_dt_bias, v_gdn_norm_w, v_ssm_conv_w, v_ssm_conv_b, v_ssm_a_log, v_ssm_dt_bias, v_ssm_d, v_ssm_norm_w, v_w_out, v_norm2_w, v_mem_norm_w, v_wq_mem, v_wk_mem, v_wv_mem, v_wo_mem, v_norm3_w, v_w_up, v_w_down, v_final_norm_w):
    T, M = x.shape[1], mem.shape[1]
    xi, yi, ci = _pos()
    s_me = 2 * xi + yi
    x0, mem0, tgt = x[0], mem[0], loss_target[0]
    tb = min(256, T)
    tbp = min(256, T)
    row = lambda v: v.reshape(1, -1)

    win_g = all_gather_chips("ag_win", w_in.astype(BF16), s_me)
    w_in_p = _pad_win(win_g.transpose(1, 0, 2).reshape(D, IN_COLS))
    keep = (ci == 0).astype(F32)
    gcw_z = lax.dynamic_update_slice(jnp.zeros((4, 3 * D), F32), gdn_conv_w * keep, (0, s_me * 768))
    scw_z = lax.dynamic_update_slice(jnp.zeros((4, 1536), F32), ssm_conv_w * keep, (0, s_me * 384))
    gcw, scw = all_reduce_items("ar_convw", [[gcw_z], [scw_z]])
    scw_x, scw_bc = scw[:, :D], scw[:, D:]
    rest_shard = jnp.concatenate([w_up, w_down, w_out, wq_mem, wk_mem, wv_mem, wo_mem], axis=0).astype(BF16)
    ag_sems, rest_thru, rest_land, ag_token = split_copy_start("ag_rest_start", rest_shard, (4,) + rest_shard.shape,
                                                               ag_pieces(rest_shard.shape[0] // 2), gcw)
    sp = jnp.stack([ci, s_me]).astype(jnp.int32)
    scb_x, scb_bc = row(ssm_conv_b[:D]), row(ssm_conv_b[D:])

    galog_c, gdtb_c = row(jnp.pad(gdn_a_log, (8, 112))), row(jnp.pad(gdn_dt_bias, (8, 112)))
    salog_c, sdtb_c = row(jnp.pad(ssm_a_log, (0, 112))), row(jnp.pad(ssm_dt_bias, (0, 112)))
    sd_x = row(jnp.repeat(ssm_d, 64))
    eb = _sel(128, D, [(h, 128 * h + l) for h in range(8) for l in range(128)])
    ea = _sel(128, D, [(8 + h, 128 * h + l) for h in range(8) for l in range(128)])
    e16 = _sel(128, D, [(h, 64 * h + l) for h in range(16) for l in range(64)])
    pb = _sel(D, 128, [(128 * h, h) for h in range(8)])
    pa = _sel(D, 128, [(128 * h, 8 + h) for h in range(8)])

    h1 = rowwise(rms_fwd_fn, "rms1", T, tb, [R(x0)], [row(norm1_w) + ag_token[0:1, 0:1]], [(D, BF16)])[0]
    p = matmul("mm_in", h1, w_in_p, "nn", 2048, 768, 1024, [F32])[0]
    gp_ins = [R(p, 3 * D, CB_QKV, "prev"), R(p, 128, CB_BA)]
    qn, kn, vv, gcs_x, beta_x, gcs_t = rowwise(gdn_prep_fn, "gdn_prep", T, tbp, gp_ins,
                                               [gcw, galog_c, gdtb_c, eb, ea], [(D, F32)] * 5 + [(-8, F32)])
    gcs_t = gcs_t.reshape(GDN_H, 1, T)
    gtb, ggh = min(128, T), 8
    o_gdn, s_states, tinv = gdn_fwd(qn, kn, vv, gcs_x, beta_x, gcs_t, gtb, ggh)
    gnw = row(gdn_norm_w)
    oa = rowwise(gdn_post_fn, "gdn_post", T, tb, [R(o_gdn), R(p, D, CB_Z)], [gnw], [(D, BF16)])[0]
    sp_ins = [R(p, D, CB_XS, "prev"), R(p, 512, CB_BC, "prev"), R(p, 128, CB_DT)]
    sp_full = [scw_x, scw_bc, scb_x, scb_bc, salog_c, sdtb_c]
    xs, bc, dt_x, acs_x, acs_t = rowwise(ssd_prep_fn, "ssd_prep", T, tbp, sp_ins, sp_full + [e16],
                                         [(D, F32), (512, F32), (D, F32), (D, F32), (-SSM_H, F32)])
    acs_t = acs_t.reshape(SSM_H, 1, T)
    y_ssd, h_states = ssd_fwd(xs, bc, dt_x, acs_x, acs_t)
    snw = row(ssm_norm_w)
    ob = rowwise(ssd_post_fn, "ssd_post", T, tb, [R(y_ssd), R(xs), R(p, D, CB_ZS)], [sd_x, snw], [(D, BF16)])[0]
    rest_thru, rest_land = split_copy_wait("ag_rest_wait", ag_sems, rest_thru, rest_land, ob,
                                           ag_pieces(rest_shard.shape[0] // 2))
    rest_g = _with_own(rest_thru, ag_forward("ag_rest_fwd", rest_land), s_me)
    assert D == 1024
    view = lambda shape, blk, at: dict(b_sel=(shape, blk, at))
    wup_n = view((D, D_FF), (None, D, D), lambda i, j, k: (j, 0, 0))
    wup_t = view((D, D_FF), (None, D, D), lambda i, j, k: (k, 0, 0))
    wdown_n = view((D_FF, D), (None, D, D), lambda i, j, k: (k, 1, 0))
    wdown_t = view((D_FF, D), (None, D, D), lambda i, j, k: (j, 1, 0))
    wout_a = view((D, D), (2, 512, D), lambda i, j, k: (0, 4, 0))
    wout_b = view((D, D), (2, 512, D), lambda i, j, k: (1, 4, 0))
    wq_v, wk_v, wv_v, wo_v = (view((D, D), (4, 256, D), lambda i, j, k, r=r: (0, r, 0)) for r in (10, 11, 12, 13))
    x1a = matmul("mm_out_a", oa, rest_g, "nn", 1024, 1024, 1024, [F32], _epi_res, [x0], **wout_a)[0]
    x1, h2 = matmul("mm_out_b", ob, rest_g, "nn", 1024, 1024, 1024, [F32, BF16], _epi_res_rms, [x1a],
                    [row(norm2_w)], **wout_b)

    mn = rowwise(rms_fwd_fn, "rms_mem", M, M, [R(mem0)], [row(mem_norm_w)], [(D, BF16)])[0]
    km = matmul("mm_k", mn, rest_g, "nn", 256, 1024, 1024, [BF16], **wk_v)[0]
    vm = matmul("mm_v", mn, rest_g, "nn", 256, 1024, 1024, [BF16], **wv_v)[0]
    qm = matmul("mm_q", h2, rest_g, "nn", 1024, 1024, 1024, [BF16], **wq_v)[0]
    ao = rowwise(attn_fn, "attn", T, tb, [R(qm)], [km, vm], [(D, BF16)])[0]
    x2, h3 = matmul("mm_o", ao, rest_g, "nn", 1024, 1024, 1024, [F32, BF16], _epi_res_rms, [x1], [row(norm3_w)], **wo_v)
    u, act = matmul("mm_up", h3, rest_g, "nn", 2048, 1024, 1024, [BF16, BF16], _epi_relu2, **wup_n)
    wdown_n2 = view((D_FF, D), (2, D, D), lambda i, j, k: (k, 1, 0))
    x3 = matmul("mm_down", act, rest_g, "nn", 1024, 1024, 2048, [F32], _epi_res, [x2], **wdown_n2)[0]
    dx3, dx3b, loss_lane, g_final = rowwise(final_fn, "final", T, tb, [R(x3), R(tgt)], [row(final_norm_w)],
                                            [(D, F32), (D, BF16)], [(1, D), (1, D)])
    loss = lax.psum(0.5 / D * jnp.sum(loss_lane), ("x", "y", "c"))

    dup = matmul("mm_dact", dx3b, rest_g, "nt", 2048, 1024, 1024, [BF16], _epi_dup, [u], **wdown_t)[0]
    def g_into(buf, blk, at):
        return dict(into=(buf, blk, lambda i, j, k, at=at: at(i, j)))

    grest = jax.ShapeDtypeStruct((4, 3584, D), F32)
    grest = matmul("mm_gdown", act, dx3b, "tn", 1024, 1024, 4096, [F32],
                   **g_into(grest, (None, 1024, D), lambda i, j: (i, 1, 0)))
    wup_t4 = dict(b_sel=((D, D_FF), (4, D, D), lambda i, j, k: (0, 0, 0), "side by side"))
    dh3 = matmul("mm_dh3", dup, rest_g, "nt", 1024, 1024, 4096, [F32], **wup_t4)[0]
    dx2, dx2b, g_n3 = rowwise(rms_bwd_fn, "rms3_bwd", T, tb, [R(x2), R(dh3), R(dx3)], [row(norm3_w)],
                              [(D, F32), (D, BF16)], [(1, D)])
    grest = matmul("mm_gup", h3, dup, "tn", 1024, 1024, 4096, [F32],
                   **g_into(grest, (None, 1024, D), lambda i, j: (j, 0, 0)))
    dao = matmul("mm_dao", dx2b, rest_g, "nt", 1024, 1024, 1024, [F32], **wo_v)[0]
    grest = matmul("mm_gwo", ao, dx2b, "tn", 1024, 1024, 2048, [F32],
                   **g_into(grest, (4, 256, D), lambda i, j: (0, 13, 0)))
    dqm, dkm, dvm = rowwise(attn_bwd_fn, "attn_bwd", T, tb, [R(qm), R(dao)], [km, vm], [(D, BF16)],
                            [(M, D), (M, D)])
    dx1, dx1b, g_n2 = matmul("mm_dh2", dqm, rest_g, "nt", 512, 1024, 1024, [F32, BF16], _epi_rms_bwd, [x1, dx2],
                             [row(norm2_w)], n_acc=1, **wq_v)
    grest = matmul("mm_gwq", h2, dqm, "tn", 1024, 1024, 2048, [F32],
                   **g_into(grest, (4, 256, D), lambda i, j: (0, 10, 0)))
    grest = matmul("mm_gwk", mn, dkm, "tn", 1024, 1024, 256, [F32],
                   **g_into(grest, (4, 256, D), lambda i, j: (0, 11, 0)))
    grest = matmul("mm_gwv", mn, dvm, "tn", 1024, 1024, 256, [F32],
                   **g_into(grest, (4, 256, D), lambda i, j: (0, 12, 0)))
    dmn_k = matmul("mm_dmk", dkm, rest_g, "nt", 256, 1024, 1024, [F32], **wk_v)[0]
    dmn = matmul("mm_dmv", dvm, rest_g, "nt", 256, 1024, 1024, [F32], _epi_res, [dmn_k], **wv_v)[0]
    g_nmem = rowwise(rms_bwd_w_fn, "rmsmem_bwd", M, M, [R(mem0), R(dmn)], [row(mem_norm_w)], [], [(1, D)])[0]
    doa = matmul("mm_doa", dx1b, rest_g, "nt", 2048, 1024, 1024, [F32], **wout_a)[0]
    dob = matmul("mm_dob", dx1b, rest_g, "nt", 2048, 1024, 1024, [F32], **wout_b)[0]
    grest = matmul("mm_gwout_a", oa, dx1b, "tn", 1024, 1024, 2048, [F32],
                   **g_into(grest, (2, 512, D), lambda i, j: (0, 4, 0)))
    grest = matmul("mm_gwout_b", ob, dx1b, "tn", 1024, 1024, 2048, [F32],
                   **g_into(grest, (2, 512, D), lambda i, j: (1, 4, 0)))

    rs_rest = rs_begin("rs_rest", grest, 256, sp, True)

    dp = jax.ShapeDtypeStruct((T, p.shape[1]), BF16)
    dy_ssd, dxs_dir, dp, g_snw, g_sd_lane = rowwise(
        ssd_post_bwd_fn, "ssd_post_bwd", T, tb, [R(y_ssd), R(xs), R(p, D, CB_ZS), R(dob)],
        [sd_x + rs_rest["token"][0:1, 0:1], snw],
        [(D, F32), (D, F32), (D, BF16, dp, CB_ZS)], [(1, D), (1, D)])
    dxs_scan, db_s, dc_s, dgate, dacs_t = ssd_bwd(xs, bc, dt_x, acs_x, acs_t, dy_ssd, h_states)
    spb = rowwise(ssd_prep_bwd_fn, "ssd_prep_bwd", T, tbp,
                  sp_ins + [R(dxs_scan), R(dxs_dir), R(db_s), R(dc_s), R(dgate), RC(dacs_t.reshape(SSM_H, T))], sp_full,
                  [(D, F32), (512, F32), (128, BF16, dp, CB_DT)],
                  [(1, D)] * 4 + [(1, 512)] * 4 + [(1, D), (1, 512), (1, 128), (1, 128)])
    dyc_x, dyc_bc, dp = spb[:3]
    dp = rowwise(conv_bwd_fn, "conv_bwd_x", T, tbp, [R(dyc_x, halo="next")], [scw_x], [(D, BF16, dp, CB_XS)])[0]
    dp = rowwise(conv_bwd_fn, "conv_bwd_bc", T, tbp, [R(dyc_bc, halo="next")], [scw_bc], [(512, BF16, dp, CB_BC)])[0]

    do_gdn, dp, g_gnw = rowwise(gdn_post_bwd_fn, "gdn_post_bwd", T, tb, [R(o_gdn), R(p, D, CB_Z), R(doa)], [gnw],
                                [(D, F32), (D, BF16, dp, CB_Z)], [(1, 128)])
    dqn, dkn, dvv, dgcs_x, dbeta_x, dgcs_t = gdn_bwd(qn, kn, vv, gcs_x, beta_x, gcs_t, do_gdn, s_states, tinv, gtb, ggh)
    gpb = rowwise(gdn_prep_bwd_fn, "gdn_prep_bwd", T, tbp,
                  gp_ins + [R(dqn), R(dkn), R(dvv), R(dgcs_x), R(dbeta_x), RC(dgcs_t.reshape(GDN_H, T))],
                  [gcw, galog_c, gdtb_c, pb, pa],
                  [(3 * D, F32), (128, BF16, dp, CB_BA)], [(1, 3 * D)] * 4 + [(1, 128), (1, 128)])
    dyc_qkv, dp = gpb[:2]
    dp = rowwise(conv_bwd_fn, "conv_bwd_qkv", T, tbp, [R(dyc_qkv, halo="next")], [gcw], [(3 * D, BF16, dp, CB_QKV)])[0]
    dh1 = matmul("mm_dh1", dp, w_in_p, "nt", 1024, 1024, 2304, [F32])[0]
    grad_x, g_n1 = rowwise(rms_bwd1_fn, "rms1_bwd", T, tb, [R(x0), R(dh1), R(dx1)], [row(norm1_w)], [(D, F32)], [(1, D)])
    g_win_p = matmul("mm_gwin", h1, dp, "tn", 1024, 768, 4096, [F32])[0]

    items = [[g_n1], [gpb[6]], [gpb[7]], [g_gnw], [spb[11]], [spb[12]], [spb[13]], [spb[14]], [g_sd_lane], [g_snw],
             [g_n2], [g_nmem], [g_n3], [g_final], list(gpb[2:6]), list(spb[3:7]), list(spb[7:11])]
    (gr_n1, r_galog, r_gdtb, gr_gnw, r_scb_x, r_scb_bc, r_salog, r_sdtb, r_sd, gr_snw, gr_n2, gr_nmem, gr_n3,
     gr_final, r_gcw, r_scw_x, r_scw_bc) = all_reduce_items("ar_grads", items)
    gr_galog, gr_gdtb = r_galog[:, 8:16], r_gdtb[:, 8:16]
    gr_salog, gr_sdtb = r_salog[:, :SSM_H], r_sdtb[:, :SSM_H]
    gr_sd = r_sd.reshape(SSM_H, SSM_P).sum(axis=1).reshape(1, SSM_H)
    gr_scb = jnp.concatenate([r_scb_x, r_scb_bc], axis=1)
    gr_gcw = lax.dynamic_slice(r_gcw, (0, s_me * 768), (4, 768))
    gr_scw = lax.dynamic_slice(jnp.concatenate([r_scw_x, r_scw_bc], axis=1), (0, s_me * 384), (4, 384))

    g_win = _unpad_win(g_win_p).reshape(D, 4, IN_COLS // 4).transpose(1, 0, 2)
    rs_win = rs_begin("rs_win", g_win, 256, sp, True, gr_n1)
    red_r, oth_r = rs_end(rs_rest, rs_win["token"])

    big = {}
    for n, w, m, v, blk0 in (("w_up", w_up, m_w_up, v_w_up, 0), ("w_down", w_down, m_w_down, v_w_down, 4),
                             ("w_out", w_out, m_w_out, v_w_out, 8), ("wq_mem", wq_mem, m_wq_mem, v_wq_mem, 10),
                             ("wk_mem", wk_mem, m_wk_mem, v_wk_mem, 11), ("wv_mem", wv_mem, m_wv_mem, v_wv_mem, 12),
                             ("wo_mem", wo_mem, m_wo_mem, v_wo_mem, 13)):
        big[n] = adam_halves("adam_" + n, w, m, v, red_r, oth_r, 256, blk0, sp)
    red_w, oth_w = rs_end(rs_win, big["wo_mem"][1])
    big["w_in"] = adam_halves("adam_win", w_in, m_w_in, v_w_in, red_w, oth_w, 256, 0, sp)
    names_s =["norm1_w", "gdn_conv_w", "gdn_a_log", "gdn_dt_bias", "gdn_norm_w", "ssm_conv_w", "ssm_conv_b",
               "ssm_a_log", "ssm_dt_bias", "ssm_d", "ssm_norm_w", "norm2_w", "mem_norm_w", "norm3_w", "final_norm_w"]
    w_s = [norm1_w, gdn_conv_w, gdn_a_log, gdn_dt_bias, gdn_norm_w, ssm_conv_w, ssm_conv_b, ssm_a_log, ssm_dt_bias,
           ssm_d, ssm_norm_w, norm2_w, mem_norm_w, norm3_w, final_norm_w]
    g_s = [gr_n1, gr_gcw, gr_galog, gr_gdtb, gr_gnw, gr_scw, gr_scb, gr_salog, gr_sdtb, gr_sd, gr_snw, gr_n2,
           gr_nmem, gr_n3, gr_final]
    m_s = [m_norm1_w, m_gdn_conv_w, m_gdn_a_log, m_gdn_dt_bias, m_gdn_norm_w, m_ssm_conv_w, m_ssm_conv_b, m_ssm_a_log,
           m_ssm_dt_bias, m_ssm_d, m_ssm_norm_w, m_norm2_w, m_mem_norm_w, m_norm3_w, m_final_norm_w]
    v_s = [v_norm1_w, v_gdn_conv_w, v_gdn_a_log, v_gdn_dt_bias, v_gdn_norm_w, v_ssm_conv_w, v_ssm_conv_b, v_ssm_a_log,
           v_ssm_dt_bias, v_ssm_d, v_ssm_norm_w, v_norm2_w, v_mem_norm_w, v_norm3_w, v_final_norm_w]
    shp_s = [w.shape for w in w_s]
    as2d = lambda a: a if a.ndim == 2 else a.reshape(1, -1)
    d_l, m_l, v_l = adam_small([as2d(a) for a in w_s], [as2d(a) for a in g_s], [as2d(a) for a in m_s],
                               [as2d(a) for a in v_s])

    grads, deltas, new_m, new_v = {}, {}, {}, {}
    for n, (gg, dd, mm_, vv_) in big.items():
        grads[n], deltas[n], new_m[n], new_v[n] = gg, dd, mm_, vv_
    for k, n in enumerate(names_s):
        grads[n] = g_s[k].reshape(shp_s[k])
        deltas[n], new_m[n], new_v[n] = (a[k].reshape(shp_s[k]) for a in (d_l, m_l, v_l))
    order = ["norm1_w", "w_in", "gdn_conv_w", "gdn_a_log", "gdn_dt_bias", "gdn_norm_w", "ssm_conv_w", "ssm_conv_b",
             "ssm_a_log", "ssm_dt_bias", "ssm_d", "ssm_norm_w", "w_out", "norm2_w", "mem_norm_w", "wq_mem", "wk_mem",
             "wv_mem", "wo_mem", "norm3_w", "w_up", "w_down", "final_norm_w"]
    return (loss, grad_x[None], *[grads[n] for n in order], *[deltas[n] for n in order],
            *[new_m[n] for n in order], *[new_v[n] for n in order])
```

```python
import numpy as np
import jax
import jax.numpy as jnp
from jax import lax
from jax.experimental import pallas as pl
from jax.experimental.pallas import tpu as pltpu

F32, BF16 = jnp.float32, jnp.bfloat16
MESH = pl.DeviceIdType.MESH
ANY = pl.BlockSpec(memory_space=pl.ANY)

EPS = 1e-6
D = 1024
GDN_H, GDN_DK, GDN_C = 8, 128, 64
SSM_H, SSM_P, SSM_N, SSM_L = 16, 64, 128, 128
MEM_H, MEM_DH = 4, 256
D_FF = 4096
IN_COLS = 6688
CB_QKV, CB_Z, CB_ZS, CB_XS, CB_BC, CB_BA, CB_DT = 0, 3, 4, 5, 12, 52, 53
VMEM_LIMIT = 56 * 1024 * 1024
D2D_CHUNKS = 8
ICI_CHUNKS = 4

ADAM_LR, ADAM_B1, ADAM_B2, ADAM_EPS, ADAM_WD, ADAM_STEP = 0.001, 0.9, 0.999, 1e-08, 0.01, 10


def _dg(a, b, ca, cb):
    return lax.dot_general(a, b, (((ca,), (cb,)), ((), ())), preferred_element_type=F32)


def _bf(x):
    return x.astype(BF16)


def mm(a, b):
    return _dg(_bf(a), _bf(b), 1, 0)


def mm_nt(a, b):
    return _dg(_bf(a), _bf(b), 1, 1)


def mm_tn(a, b):
    return _dg(_bf(a), _bf(b), 0, 0)


def mm_sel(a, sel):
    hi = a.astype(BF16)
    r1 = a - hi.astype(F32)
    mid = r1.astype(BF16)
    lo = (r1 - mid.astype(F32)).astype(BF16)
    s = sel.astype(BF16)
    return _dg(hi, s, 1, 0) + (_dg(mid, s, 1, 0) + _dg(lo, s, 1, 0))


def mm3(a, b):
    ah, bh = a.astype(BF16), b.astype(BF16)
    al, bl = (a - ah.astype(F32)).astype(BF16), (b - bh.astype(F32)).astype(BF16)
    return _dg(ah, bh, 1, 0) + (_dg(ah, bl, 1, 0) + _dg(al, bh, 1, 0))


def _iota(shape, dim):
    return lax.broadcasted_iota(jnp.int32, shape, dim)


def _chunk_cumsum(x, c):
    pos = _iota(x.shape, 0) & (c - 1)
    s = 1
    while s < c:
        x = x + jnp.where(pos >= s, pltpu.roll(x, s, 0), 0.0)
        s *= 2
    return x


def _chunk_revcumsum(x, c):
    n = x.shape[0]
    pos = _iota(x.shape, 0) & (c - 1)
    s = 1
    while s < c:
        x = x + jnp.where(pos < c - s, pltpu.roll(x, n - s, 0), 0.0)
        s *= 2
    return x


def _sig(x):
    return jax.nn.sigmoid(x)


def _softplus(x):
    return jnp.maximum(x, 0.0) + jnp.log(1.0 + jnp.exp(-jnp.abs(x)))


def _rows(v):
    return jnp.sum(v, axis=0, keepdims=True)


def _lanes(v):
    return jnp.sum(v, axis=1, keepdims=True)


def _sum_all(v):
    return _rows(_lanes(v))


def _cparams(sem):
    return pltpu.CompilerParams(dimension_semantics=sem, vmem_limit_bytes=VMEM_LIMIT)


def rowwise(fn, name, T, tb, row_ins, full_ins, row_outs, acc_outs=(), sp=None):
    nblk = T // tb
    assert nblk * tb == T
    has_sp = sp is not None

    def imap(f):
        return (lambda i, s: f(i, s)) if has_sp else (lambda i: f(i, None))

    in_specs, args = [], []
    for arr, w, cb, halo, off in row_ins:
        if halo == "col":
            in_specs.append(pl.BlockSpec((w, tb), imap(lambda i, s: (0, i))))
            args.append(arr)
            continue
        rowf = off if callable(off) else (lambda i, s, off=off: i + off)
        in_specs.append(pl.BlockSpec((tb, w), imap(lambda i, s, cb=cb, rowf=rowf: (rowf(i, s), cb))))
        args.append(arr)
        if halo == "prev":
            r = tb // 8
            in_specs.append(pl.BlockSpec((8, w), imap(lambda i, s, cb=cb, r=r: (jnp.maximum(i * r - 1, 0), cb))))
            args.append(arr)
        elif halo == "next":
            r, last = tb // 8, T // 8 - 1
            in_specs.append(pl.BlockSpec((8, w), imap(lambda i, s, cb=cb, r=r, last=last:
                                                      (jnp.minimum((i + 1) * r, last), cb))))
            args.append(arr)
    for arr in full_ins:
        in_specs.append(pl.BlockSpec(arr.shape, imap(lambda i, s, nd=arr.ndim: (0,) * nd)))
        args.append(arr)
    n_in, n_ro = len(args), len(row_outs)
    out_shape, out_specs, aliases = [], [], {}
    for k, (w, dt, *dest) in enumerate(row_outs):
        if dest:
            buf, cb = dest
            out_shape.append(jax.ShapeDtypeStruct(buf.shape, buf.dtype))
            out_specs.append(pl.BlockSpec((tb, w), imap(lambda i, s, cb=cb: (i, cb))))
            if not isinstance(buf, jax.ShapeDtypeStruct):
                aliases[len(args) + int(has_sp)] = k
                in_specs.append(ANY)
                args.append(buf)
        elif w < 0:
            out_shape.append(jax.ShapeDtypeStruct((-w, T), dt))
            out_specs.append(pl.BlockSpec((-w, tb), imap(lambda i, s: (0, i))))
        else:
            out_shape.append(jax.ShapeDtypeStruct((T, w), dt))
            out_specs.append(pl.BlockSpec((tb, w), imap(lambda i, s: (i, 0))))
    for shp in acc_outs:
        out_shape.append(jax.ShapeDtypeStruct(shp, F32))
        out_specs.append(pl.BlockSpec(shp, imap(lambda i, s, nd=len(shp): (0,) * nd)))

    def body(*refs):
        i = pl.program_id(0)
        if has_sp:
            sp_ref, refs = refs[0], refs[1:]
            vals = fn(i, nblk, sp_ref, *[r[...] for r in refs[:n_in]])
        else:
            vals = fn(i, nblk, *[r[...] for r in refs[:n_in]])
        outs = refs[n_in + len(aliases):]
        for ref, val in zip(outs[:n_ro], vals[:n_ro]):
            ref[...] = val.astype(ref.dtype)
        for ref, val in zip(outs[n_ro:], vals[n_ro:]):
            @pl.when(i == 0)
            def _(ref=ref, val=val):
                ref[...] = val

            @pl.when(i > 0)
            def _(ref=ref, val=val):
                ref[...] += val

    cparams = _cparams(("arbitrary",) if acc_outs else ("parallel",))
    if has_sp:
        return pl.pallas_call(
            body, name=name, out_shape=out_shape, compiler_params=cparams, input_output_aliases=aliases,
            grid_spec=pltpu.PrefetchScalarGridSpec(num_scalar_prefetch=1, grid=(nblk,), in_specs=in_specs,
                                                   out_specs=out_specs),
        )(sp, *args)
    return pl.pallas_call(
        body, name=name, grid=(nblk,), in_specs=in_specs, out_specs=out_specs, out_shape=out_shape,
        compiler_params=cparams, input_output_aliases=aliases,
    )(*args)


def R(arr, w=None, cb=0, halo=None, off=0):
    return (arr, arr.shape[1] if w is None else w, cb, halo, off)


def RC(arr):
    return (arr, arr.shape[0], 0, "col", 0)


def matmul(name, a, b, form, tm, tn, tk, out_dtypes, epi=None, extras=(), rows=(), into=None, n_acc=0, b_sel=None):
    bs = b.shape if b_sel is None else b_sel[0]
    if form == "nn":
        (M, K), N = a.shape, bs[1]
    elif form == "nt":
        (M, K), N = a.shape, bs[0]
    else:
        (K, M), N = a.shape, bs[1]
    tm, tn, tk = min(tm, M), min(tn, N), min(tk, K)
    assert M % tm == 0 and N % tn == 0 and K % tk == 0, (name, M, N, K, tm, tn, tk)

    def b_spec_of(blk, at):
        if b_sel is None:
            return pl.BlockSpec(blk, lambda i, j, k: at(i, j, k))
        blk3 = b_sel[1]
        assert int(np.prod([d for d in blk3 if d is not None])) == blk[0] * blk[1], (name, blk3, blk)
        return pl.BlockSpec(blk3, lambda i, j, k: b_sel[2](i, j, k))

    if form == "nn":
        a_spec = pl.BlockSpec((tm, tk), lambda i, j, k: (i, k))
        b_spec = b_spec_of((tk, tn), lambda i, j, k: (k, j))
        ca, cb = 1, 0
    elif form == "nt":
        a_spec = pl.BlockSpec((tm, tk), lambda i, j, k: (i, k))
        b_spec = b_spec_of((tn, tk), lambda i, j, k: (j, k))
        ca, cb = 1, 1
    else:
        a_spec = pl.BlockSpec((tk, tm), lambda i, j, k: (k, i))
        b_spec = b_spec_of((tk, tn), lambda i, j, k: (k, j))
        ca, cb = 0, 0
    nk, ne, no = K // tk, len(extras) + len(rows), len(out_dtypes)
    if epi is None:
        epi = lambda acc: (acc,)

    assert n_acc == 0 or tn == N

    def body(a_ref, b_ref, *rest):
        ex, outs, accs, acc = rest[:ne], rest[ne:ne + no], rest[ne + no:ne + no + n_acc], rest[ne + no + n_acc]
        i, k = pl.program_id(0), pl.program_id(2)

        def finish(total):
            vals = epi(total, *[e[...] for e in ex])
            for r, v in zip(outs, vals[:no]):
                r[...] = v.astype(r.dtype).reshape(r.shape)
            for r, v in zip(accs, vals[no:]):
                @pl.when(i == 0)
                def _(r=r, v=v):
                    r[...] = v

                @pl.when(i > 0)
                def _(r=r, v=v):
                    r[...] += v

        b_tile = b_ref[...]
        if b_sel is not None and len(b_sel) > 3:
            b_tile = jnp.concatenate([b_tile[s] for s in range(b_tile.shape[0])], axis=1)
        prod = _dg(_bf(a_ref[...]), _bf(b_tile.reshape(-1, b_tile.shape[-1])), ca, cb)
        if nk == 1:
            finish(prod)
            return

        @pl.when(k == 0)
        def _():
            acc[...] = prod

        @pl.when(k > 0)
        def _():
            acc[...] += prod

        @pl.when(k == nk - 1)
        def _():
            finish(acc[...])

    mn = pl.BlockSpec((tm, tn), lambda i, j, k: (i, j))
    rw = pl.BlockSpec((1, tn), lambda i, j, k: (0, j))
    acc_scratch = pltpu.VMEM((tm, tn) if nk > 1 else (8, 128), F32)
    if into is not None:
        buf, blk, bmap = into
        assert ne == 0 and no == 1
        aliased = not isinstance(buf, jax.ShapeDtypeStruct)

        def body_into(a_ref, b_ref, *rest):
            body(a_ref, b_ref, *rest[-2:])

        return pl.pallas_call(
            body_into, name=name, grid=(M // tm, N // tn, nk),
            in_specs=[a_spec, b_spec] + ([ANY] if aliased else []), out_specs=pl.BlockSpec(blk, bmap),
            out_shape=jax.ShapeDtypeStruct(buf.shape, buf.dtype),
            scratch_shapes=[acc_scratch],
            input_output_aliases={2: 0} if aliased else {},
            compiler_params=_cparams(("parallel", "parallel", "arbitrary")),
        )(a, b, *([buf] if aliased else []))
    return pl.pallas_call(
        body, name=name, grid=(M // tm, N // tn, nk),
        in_specs=[a_spec, b_spec] + [mn] * len(extras) + [rw] * len(rows), out_specs=[mn] * no + [rw] * n_acc,
        out_shape=[jax.ShapeDtypeStruct((M, N), dt) for dt in out_dtypes] + [jax.ShapeDtypeStruct((1, N), F32)] * n_acc,
        scratch_shapes=[acc_scratch],
        compiler_params=_cparams(("arbitrary",) * 3 if n_acc else ("parallel", "parallel", "arbitrary")),
    )(a, b, *extras, *rows)


def _epi_res(acc, res):
    return (res + acc,)


def _epi_rms_bwd(acc, x, dres, w):
    return rms_bwd_fn(0, 0, x, acc, dres, w)


def rms_bwd1_fn(i, n, x, dh, dres, w):
    dx, _, gw = rms_bwd_fn(i, n, x, dh, dres, w)
    return dx, gw


def _epi_final(acc, res, tgt, w):
    return final_fn(0, 0, res + acc, tgt, w)


def _epi_res_rms(acc, res, w):
    x = res + acc
    return (x, x * lax.rsqrt(jnp.mean(x * x, axis=-1, keepdims=True) + EPS) * w)


def _epi_relu2(acc):
    u = jnp.maximum(acc, 0.0)
    return (u, u * u)


def _epi_dup(acc, u):
    return (acc * 2.0 * u.astype(F32),)


def _conv(x, halo, w, i):
    halo = jnp.where(i == 0, 0.0, halo)
    xt = jnp.concatenate([halo, x], axis=0)
    shifted = [pltpu.roll(xt, 3 - k, 0)[8:, :] for k in range(3)] + [x]
    y = shifted[3] * w[3:4, :]
    for k in range(3):
        y = y + shifted[k] * w[k:k + 1, :]
    return y, shifted


def _l2n(x, scale):
    outs = []
    for h in range(x.shape[1] // 128):
        xh = x[:, 128 * h:128 * h + 128]
        outs.append(xh * (lax.rsqrt(jnp.sum(xh * xh, axis=-1, keepdims=True) + EPS) * scale))
    return jnp.concatenate(outs, axis=1)


def _l2n_bwd(x, dy, scale):
    outs = []
    for h in range(x.shape[1] // 128):
        xh, dh = x[:, 128 * h:128 * h + 128], dy[:, 128 * h:128 * h + 128] * scale
        r = lax.rsqrt(jnp.sum(xh * xh, axis=-1, keepdims=True) + EPS)
        outs.append(r * dh - xh * (r * r * r) * jnp.sum(xh * dh, axis=-1, keepdims=True))
    return jnp.concatenate(outs, axis=1)


def rms_fwd_fn(i, n, x, w):
    r = lax.rsqrt(jnp.mean(x * x, axis=-1, keepdims=True) + EPS)
    return (x * r * w,)


def rms_bwd_fn(i, n, x, dh, dres, w):
    r = lax.rsqrt(jnp.mean(x * x, axis=-1, keepdims=True) + EPS)
    g = dh * w
    dx = dres + r * g - x * (r * r * r) * jnp.mean(x * g, axis=-1, keepdims=True)
    return dx, dx, _rows(dh * x * r)


def rms_bwd_w_fn(i, n, x, dh, w):
    r = lax.rsqrt(jnp.mean(x * x, axis=-1, keepdims=True) + EPS)
    return (_rows(dh * x * r),)


def final_fn(i, n, x, tgt, w):
    r = lax.rsqrt(jnp.mean(x * x, axis=-1, keepdims=True) + EPS)
    xn = x * r
    e = xn * w - tgt
    dy = e * (1.0 / D)
    g = dy * w
    dx = r * g - x * (r * r * r) * jnp.mean(x * g, axis=-1, keepdims=True)
    return dx, dx, _rows(e * e), _rows(dy * xn)


def _gdn_gates(ba, alog_c, dtb_c):
    col = _iota(ba.shape, 1)
    amask = (col >= 8) & (col < 16)
    beta = jnp.where(col < 8, _sig(ba), 0.0)
    z = ba + dtb_c
    ea_ = jnp.exp(alog_c)
    return beta, z, ea_, jnp.where(amask, -ea_ * _softplus(z), 0.0), amask


def _cols(x, g):
    return x[:, 128 * g:128 * g + 128]


def gdn_prep_fn(i, n, qkv, halo, ba, cw, alog_c, dtb_c, eb, ea):
    outs = [[], [], []]
    for g in range(3 * GDN_H):
        yc, _ = _conv(_cols(qkv, g), _cols(halo, g), _cols(cw, g), i)
        act = yc * _sig(yc)
        if g < 2 * GDN_H:
            act = _l2n(act, GDN_DK ** -0.5 if g < GDN_H else 1.0)
        outs[g // GDN_H].append(act)
    beta, _, _, gg, _ = _gdn_gates(ba, alog_c, dtb_c)
    gcs = _chunk_cumsum(gg, GDN_C)
    return (*[jnp.concatenate(o, axis=1) for o in outs], mm_sel(gcs, ea), mm_sel(beta, eb), beta + gcs,
            jnp.transpose(gcs)[8:16, :])


def gdn_prep_bwd_fn(i, n, qkv, halo, ba, dqn, dkn, dv, dgb, dgcs_t, cw, alog_c, dtb_c):
    dycs, dwl = [], [[], [], [], []]
    for g in range(3 * GDN_H):
        yc, shifted = _conv(_cols(qkv, g), _cols(halo, g), _cols(cw, g), i)
        sg = _sig(yc)
        act = yc * sg
        if g < GDN_H:
            d = _l2n_bwd(act, _cols(dqn, g), GDN_DK ** -0.5)
        elif g < 2 * GDN_H:
            d = _l2n_bwd(act, _cols(dkn, g - GDN_H), 1.0)
        else:
            d = _cols(dv, g - 2 * GDN_H)
        dyc_g = d * (sg * (1.0 + yc * (1.0 - sg)))
        dycs.append(dyc_g)
        for k in range(4):
            dwl[k].append(_rows(dyc_g * shifted[k]))
    dyc = jnp.concatenate(dycs, axis=1)
    dws = [jnp.concatenate(l, axis=1) for l in dwl]
    beta, z, ea_, g, amask = _gdn_gates(ba, alog_c, dtb_c)
    tbn = ba.shape[0]
    rowpart = jnp.transpose(jnp.concatenate([jnp.zeros((8, tbn), F32), dgcs_t, jnp.zeros((112, tbn), F32)], axis=0))
    dg = _chunk_revcumsum(jnp.where(amask, dgb, 0.0) - rowpart, GDN_C)
    draw = jnp.where(amask, dg * (-ea_) * _sig(z), 0.0)
    dba = draw + dgb * beta * (1.0 - beta)
    return (dyc, dba, dws[0], dws[1], dws[2], dws[3], _rows(dg * g), _rows(draw))


def conv_bwd_fn(i, n, dyc, halo, w):
    halo = jnp.where(i == n - 1, 0.0, halo)
    tb = dyc.shape[0]
    outs = []
    for g in range(dyc.shape[1] // 128):
        d, wg = _cols(dyc, g), _cols(w, g)
        xt = jnp.concatenate([d, _cols(halo, g)], axis=0)
        dx = d * wg[3:4, :]
        for k in range(3):
            dx = dx + pltpu.roll(xt, tb + 8 - (3 - k), 0)[:tb, :] * wg[k:k + 1, :]
        outs.append(dx)
    return (jnp.concatenate(outs, axis=1),)


def gdn_post_fn(i, n, o, z, w):
    outs = []
    for h in range(GDN_H):
        oh, zh = o[:, 128 * h:128 * h + 128], z[:, 128 * h:128 * h + 128]
        r = lax.rsqrt(jnp.mean(oh * oh, axis=-1, keepdims=True) + EPS)
        outs.append(oh * r * w * (zh * _sig(zh)))
    return (jnp.concatenate(outs, axis=1),)


def gdn_post_bwd_fn(i, n, o, z, doa, w):
    dos, dzs, dw = [], [], None
    for h in range(GDN_H):
        sl = slice(128 * h, 128 * h + 128)
        oh, zh, dh = o[:, sl], z[:, sl], doa[:, sl]
        r = lax.rsqrt(jnp.mean(oh * oh, axis=-1, keepdims=True) + EPS)
        s = _sig(zh)
        dn = dh * (zh * s)
        dzs.append(dh * (oh * r * w) * (s * (1.0 + zh * (1.0 - s))))
        t = _rows(dn * oh * r)
        dw = t if dw is None else dw + t
        g = dn * w
        dos.append(r * g - oh * (r * r * r) * jnp.mean(oh * g, axis=-1, keepdims=True))
    return jnp.concatenate(dos, axis=1), jnp.concatenate(dzs, axis=1), dw


def _ssd_gates(dtblk, alog_c, dtb_c):
    hmask = _iota(dtblk.shape, 1) < SSM_H
    z = dtblk + dtb_c
    return jnp.where(hmask, _softplus(z), 0.0), -jnp.exp(alog_c), z, hmask


def _silu_conv_cols(x, halo, w, b, i):
    outs = []
    for g in range(x.shape[1] // 128):
        yc, _ = _conv(_cols(x, g), _cols(halo, g), _cols(w, g), i)
        yc = yc + _cols(b, g)
        outs.append(yc * _sig(yc))
    return jnp.concatenate(outs, axis=1)


def _silu_conv_bwd_cols(x, halo, w, b, dout, i):
    dycs, dwl = [], [[], [], [], []]
    for g in range(x.shape[1] // 128):
        yc, shifted = _conv(_cols(x, g), _cols(halo, g), _cols(w, g), i)
        yc = yc + _cols(b, g)
        s = _sig(yc)
        dyc_g = _cols(dout, g) * (s * (1.0 + yc * (1.0 - s)))
        dycs.append(dyc_g)
        for k in range(4):
            dwl[k].append(_rows(dyc_g * shifted[k]))
    dyc = jnp.concatenate(dycs, axis=1)
    return dyc, [jnp.concatenate(l, axis=1) for l in dwl], _rows(dyc)


def ssd_prep_fn(i, n, xp, hx, bcp, hbc, dtblk, cwx, cwbc, cbx, cbbc, alog_c, dtb_c, e16):
    dt, a_neg, _, _ = _ssd_gates(dtblk, alog_c, dtb_c)
    acs = _chunk_cumsum(dt * a_neg, SSM_L)
    return (_silu_conv_cols(xp, hx, cwx, cbx, i), _silu_conv_cols(bcp, hbc, cwbc, cbbc, i), mm_sel(dt, e16),
            mm_sel(acs, e16), jnp.transpose(acs)[0:SSM_H, :])


def ssd_prep_bwd_fn(i, n, xp, hx, bcp, hbc, dtblk, dxs_a, dxs_b, db, dc, dgate, dacs_t, cwx, cwbc, cbx, cbbc, alog_c, dtb_c):
    dyx, dwx, dbx = _silu_conv_bwd_cols(xp, hx, cwx, cbx, dxs_a + dxs_b, i)
    dybc, dwbc, dbbc = _silu_conv_bwd_cols(bcp, hbc, cwbc, cbbc, jnp.concatenate([db, dc], axis=1), i)
    dt, a_neg, z, hmask = _ssd_gates(dtblk, alog_c, dtb_c)
    g0, g1 = dgate[:, :128], dgate[:, 128:]
    col = _iota(g0.shape, 1)
    lo, mid = col < 8, (col >= 8) & (col < 16)
    dacs_col = jnp.where(lo, g0, 0.0) + pltpu.roll(jnp.where(lo, g1, 0.0), 8, 1)
    ddt_dir = pltpu.roll(jnp.where(mid, g0, 0.0), 120, 1) + jnp.where(mid, g1, 0.0)
    tbn = dtblk.shape[0]
    rowpart = jnp.transpose(jnp.concatenate([dacs_t, jnp.zeros((128 - SSM_H, tbn), F32)], axis=0))
    da = _chunk_revcumsum(dacs_col - rowpart, SSM_L)
    draw = jnp.where(hmask, (ddt_dir + da * a_neg) * _sig(z), 0.0)
    return (dyx, dybc, draw, *dwx, *dwbc, dbx, dbbc, _rows(da * dt * a_neg), _rows(draw))


def _ssd_gate(y, xs, zs, d_x):
    y2 = y + xs * d_x
    s = _sig(zs)
    return y2, s, y2 * (zs * s)


def ssd_post_fn(i, n, y, xs, zs, d_x, nw):
    _, _, yg = _ssd_gate(y, xs, zs, d_x)
    outs = []
    for g in range(2):
        v = yg[:, 512 * g:512 * g + 512]
        outs.append(v * lax.rsqrt(jnp.mean(v * v, axis=-1, keepdims=True) + EPS))
    return (jnp.concatenate(outs, axis=1) * nw,)


def ssd_post_bwd_fn(i, n, y, xs, zs, dob, d_x, nw):
    y2, s, yg = _ssd_gate(y, xs, zs, d_x)
    gfull = dob * nw
    dygs, dnw = [], []
    for g in range(2):
        sl = slice(512 * g, 512 * g + 512)
        v, gg = yg[:, sl], gfull[:, sl]
        r = lax.rsqrt(jnp.mean(v * v, axis=-1, keepdims=True) + EPS)
        dygs.append(r * gg - v * (r * r * r) * jnp.mean(v * gg, axis=-1, keepdims=True))
        dnw.append(_rows(dob[:, sl] * v * r))
    dyg = jnp.concatenate(dygs, axis=1)
    dy2 = dyg * (zs * s)
    dzs = dyg * y2 * (s * (1.0 + zs * (1.0 - s)))
    return dy2, dy2 * d_x, dzs, jnp.concatenate(dnw, axis=1), _rows(dy2 * xs)


def _attn_probs(q, k):
    hs = [slice(MEM_DH * h, MEM_DH * h + MEM_DH) for h in range(MEM_H)]
    ss = [mm_nt(q[:, sl], k[:, sl]) * (MEM_DH ** -0.5) for sl in hs]
    es = [jnp.exp(s - jnp.max(s, axis=-1, keepdims=True)) for s in ss]
    return hs, [e / jnp.sum(e, axis=-1, keepdims=True) for e in es]


def attn_fn(i, n, q, k, v):
    hs, ps = _attn_probs(q, k)
    return (jnp.concatenate([mm(p, v[:, sl]) for p, sl in zip(ps, hs)], axis=1),)


def attn_bwd_fn(i, n, q, do, k, v):
    hs, ps = _attn_probs(q, k)
    dvs = [mm_tn(p, do[:, sl]) for p, sl in zip(ps, hs)]
    dps = [mm_nt(do[:, sl], v[:, sl]) for sl in hs]
    dss = [p * (dp - jnp.sum(dp * p, axis=-1, keepdims=True)) * (MEM_DH ** -0.5) for p, dp in zip(ps, dps)]
    dqs = [mm(ds, k[:, sl]) for ds, sl in zip(dss, hs)]
    dks = [mm_tn(ds, q[:, sl]) for ds, sl in zip(dss, hs)]
    return jnp.concatenate(dqs, axis=1), jnp.concatenate(dks, axis=1), jnp.concatenate(dvs, axis=1)


def add2_fn(i, n, sp, a, b):
    return (a + b,)


def sum4_fn(i, n, sp, a, b, c, d):
    return (((a.astype(F32) + b.astype(F32)) + c.astype(F32)) + d.astype(F32),)


def _adamw(w, g, m, v):
    m = ADAM_B1 * m + (1.0 - ADAM_B1) * g
    v = ADAM_B2 * v + (1.0 - ADAM_B2) * (g * g)
    m_hat = m / (1.0 - ADAM_B1 ** ADAM_STEP)
    v_hat = v / (1.0 - ADAM_B2 ** ADAM_STEP)
    delta = -ADAM_LR * (m_hat / (jnp.sqrt(v_hat) + ADAM_EPS) + ADAM_WD * w)
    return delta, m, v


def _gate_cols(gb, h):
    lane = _iota(gb.shape, 1)
    return _lanes(jnp.where(lane == h, gb, 0.0)), _lanes(jnp.where(lane == 8 + h, gb, 0.0))


def _gdn_stage1(q, k, v, bb, gcs, grow):
    C = GDN_C
    row, col = _iota((C, C), 0), _iota((C, C), 1)
    incl, strict = row >= col, row > col
    dmat = jnp.where(incl, jnp.exp(jnp.minimum((gcs if gcs.shape[1] == 1 else gcs[:, :C]) - grow, 0.0)), 0.0)
    gam = jnp.exp(gcs)
    gl = gcs[C - 1:C, :]
    kb, vb = k * bb, v * bb
    kg = kb * gam
    lmat = jnp.where(strict, mm_nt(kb, k) * dmat, 0.0)
    pmat = jnp.where(incl, mm_nt(q, k) * dmat, 0.0)
    return dict(q=q, k=k, v=v, bb=bb, incl=incl, strict=strict, dmat=dmat, gam=gam, kb=kb, vb=vb, kg=kg,
                lmat=lmat, pmat=pmat, qd=q * gam, kdec=jnp.exp(gl - gcs), cd=jnp.exp(gl))


def _gdn_inverse(lmats):
    C = GDN_C
    eye = (_iota((C, C), 0) == _iota((C, C), 1)).astype(F32)
    xs = [-l for l in lmats]
    ts = [eye + x for x in xs]
    for _ in range(5):
        xs = [mm(x, x) for x in xs]
        ts = [t + mm(t, x) for t, x in zip(ts, xs)]
    res = [eye - mm3(eye + l, t) for l, t in zip(lmats, ts)]
    return [t + mm(t, r) for t, r in zip(ts, res)]


def gdn_fwd(qn, kn, v, gcs_x, beta_x, gcs_t, tb, gh):
    T = qn.shape[0]
    nb, ncb, nc, C = T // tb, tb // GDN_C, T // GDN_C, GDN_C
    idx = [(hh, c) for hh in range(gh) for c in range(ncb)]

    def body(q_ref, k_ref, v_ref, g_ref, b_ref, gt_ref, o_ref, st_ref, ti_ref, s_scr):
        @pl.when(pl.program_id(1) == 0)
        def _():
            s_scr[...] = jnp.zeros_like(s_scr)

        grows = [gt_ref[hh] for hh in range(gh)]
        at = lambda hh, c: (slice(C * c, C * (c + 1)), slice(128 * hh, 128 * hh + 128))
        st1 = []
        for hh, c in idx:
            sl, ln = at(hh, c)
            st1.append(_gdn_stage1(q_ref[sl, ln], k_ref[sl, ln], v_ref[sl, ln], b_ref[sl, ln], g_ref[sl, ln],
                                   grows[hh][:, sl]))
        tinvs = _gdn_inverse([s["lmat"] for s in st1])
        us = [mm(t, s["vb"]) for t, s in zip(tinvs, st1)]
        ws = [mm(t, s["kg"]) for t, s in zip(tinvs, st1)]
        kds = [s["k"] * s["kdec"] for s in st1]
        ms = [mm_tn(kd, w) for kd, w in zip(kds, ws)]
        bs = [mm_tn(kd, u) for kd, u in zip(kds, us)]
        gs = [s["qd"] - mm(s["pmat"], w) for s, w in zip(st1, ws)]
        pus = [mm(s["pmat"], u) for s, u in zip(st1, us)]
        ss = [s_scr[hh] for hh in range(gh)]
        for c in range(ncb):
            for hh in range(gh):
                n, (sl, ln) = hh * ncb + c, at(hh, c)
                ti_ref[hh, sl, :] = tinvs[n]
                st_ref[hh, c] = ss[hh]
                o_ref[sl, ln] = mm(gs[n], ss[hh]) + pus[n]
                ss[hh] = st1[n]["cd"] * ss[hh] - mm(ms[n], ss[hh]) + bs[n]
        for hh in range(gh):
            s_scr[hh] = ss[hh]

    blk = pl.BlockSpec((tb, 128 * gh), lambda h, i: (i, h))
    return pl.pallas_call(
        body, name="gdn_fwd", grid=(GDN_H // gh, nb),
        in_specs=[blk] * 5 + [pl.BlockSpec((gh, 1, tb), lambda h, i: (h, 0, i))],
        out_specs=[blk, pl.BlockSpec((gh, ncb, 128, 128), lambda h, i: (h, i, 0, 0)),
                   pl.BlockSpec((gh, tb, C), lambda h, i: (h, i, 0))],
        out_shape=[jax.ShapeDtypeStruct((T, D), F32), jax.ShapeDtypeStruct((GDN_H, nc, 128, 128), F32),
                   jax.ShapeDtypeStruct((GDN_H, T, C), F32)],
        scratch_shapes=[pltpu.VMEM((gh, 128, 128), F32)],
        compiler_params=_cparams(("parallel", "arbitrary")),
    )(qn, kn, v, gcs_x, beta_x, gcs_t)


def gdn_bwd(qn, kn, v, gb, gcs_t, do, states, tinv, tb, gh):
    T = qn.shape[0]
    nb, ncb, C = T // tb, tb // GDN_C, GDN_C
    assert gh == GDN_H

    def body(q_ref, k_ref, v_ref, gb_ref, gt_ref, do_ref, st_ref, ti_ref,
             dq_ref, dk_ref, dv_ref, dgb_ref, dgr_ref, ds_scr):
        @pl.when(pl.program_id(1) == 0)
        def _():
            ds_scr[...] = jnp.zeros_like(ds_scr)

        grows = [gt_ref[hh] for hh in range(gh)]
        at = lambda hh, c: (slice(C * c, C * (c + 1)), slice(128 * hh, 128 * hh + 128))
        lastrow = _iota((C, 1), 0) == C - 1
        lane = _iota((C, 128), 1)
        idx = [(hh, c) for hh in range(gh) for c in range(ncb)]
        P = []
        for hh, c in idx:
            sl, ln = at(hh, c)
            lc = _gdn_stage1(q_ref[sl, ln], k_ref[sl, ln], v_ref[sl, ln], *_gate_cols(gb_ref[sl, :], hh), grows[hh][:, sl])
            lc.update(tinv=ti_ref[hh, sl, :], s=st_ref[hh, c], do=do_ref[sl, ln], kd=lc["k"] * lc["kdec"])
            P.append(lc)
        for l, u, w in zip(P, [mm(l["tinv"], l["vb"]) for l in P], [mm(l["tinv"], l["kg"]) for l in P]):
            l.update(u=u, w=w)
        for l, x in zip(P, [mm(l["w"], l["s"]) for l in P]):
            l["vn"] = l["u"] - x
        for l, a, b, c_, d in zip(P, [mm_nt(l["do"], l["s"]) for l in P], [mm_nt(l["do"], l["vn"]) for l in P],
                                  [mm_tn(l["qd"], l["do"]) for l in P], [mm_tn(l["pmat"], l["do"]) for l in P]):
            l.update(dqd=a, dp=jnp.where(l["incl"], b, 0.0), ds_q=c_, dvn_p=d)
        pre = dict(zip(idx, P))
        rows = {}
        hs = range(gh)
        ds = [ds_scr[hh] for hh in hs]
        for c in reversed(range(ncb)):
            L = [pre[hh, c] for hh in hs]
            dvn = [l["dvn_p"] + mm(l["kd"], d) for l, d in zip(L, ds)]
            dkd = [mm_nt(l["vn"], d) for l, d in zip(L, ds)]
            dcd = [_sum_all(l["s"] * d) for l, d in zip(L, ds)]
            ds = [l["ds_q"] + l["cd"] * d - mm_tn(l["w"], x) for l, d, x in zip(L, ds, dvn)]
            dw = [-mm_nt(x, l["s"]) for l, x in zip(L, dvn)]
            dvb = [mm_tn(l["tinv"], x) for l, x in zip(L, dvn)]
            dkg = [mm_tn(l["tinv"], x) for l, x in zip(L, dw)]
            da = [-jnp.where(l["strict"], mm_nt(a, l["u"]) + mm_nt(b, l["w"]), 0.0) for l, a, b in zip(L, dvb, dkg)]
            dm = [a * l["dmat"] for l, a in zip(L, da)]
            dn = [l["dp"] * l["dmat"] for l in L]
            dkb = [mm(a, l["k"]) for l, a in zip(L, dm)]
            dq = [mm(a, l["k"]) + l["gam"] * l["dqd"] for l, a in zip(L, dn)]
            dk = [mm_tn(a, l["kb"]) + mm_tn(b, l["q"]) for l, a, b in zip(L, dm, dn)]
            dgb = jnp.zeros((C, 128), F32)
            for hh in hs:
                sl, ln = at(hh, c)
                l = L[hh]
                e = da[hh] * l["lmat"] + l["dp"] * l["pmat"]
                t_kd = _lanes(dkd[hh] * l["kd"])
                dgl = _sum_all(t_kd) + dcd[hh] * l["cd"][:, :1]
                dgcs = (_lanes(e) + _lanes(l["dqd"] * l["qd"]) - t_kd + _lanes(dkg[hh] * l["kg"])
                        + jnp.where(lastrow, dgl, 0.0))
                rows[hh, c] = _rows(e)
                dq_ref[sl, ln] = dq[hh]
                dk_ref[sl, ln] = (dk[hh] + l["kdec"] * dkd[hh] + l["bb"] * l["gam"] * dkg[hh] + l["bb"] * dkb[hh])
                dv_ref[sl, ln] = l["bb"] * dvb[hh]
                dbeta = _lanes(dkg[hh] * l["gam"] * l["k"]) + _lanes(dvb[hh] * l["v"]) + _lanes(dkb[hh] * l["k"])
                dgb = dgb + jnp.where(lane == hh, dbeta, 0.0) + jnp.where(lane == 8 + hh, dgcs, 0.0)
            dgb_ref[slice(C * c, C * (c + 1)), :] = dgb
        for hh in hs:
            ds_scr[hh] = ds[hh]
            dgr_ref[hh] = jnp.concatenate([rows[hh, c] for c in range(ncb)], axis=1)

    blk = pl.BlockSpec((tb, 128 * gh), lambda h, i: (nb - 1 - i, h))
    rowspec = pl.BlockSpec((gh, 1, tb), lambda h, i: (h, 0, nb - 1 - i))
    cblk = pl.BlockSpec((tb, 128), lambda h, i: (nb - 1 - i, 0))
    return pl.pallas_call(
        body, name="gdn_bwd", grid=(GDN_H // gh, nb),
        in_specs=[blk] * 3 + [cblk, rowspec, blk,
                              pl.BlockSpec((gh, ncb, 128, 128), lambda h, i: (h, nb - 1 - i, 0, 0)),
                              pl.BlockSpec((gh, tb, C), lambda h, i: (h, nb - 1 - i, 0))],
        out_specs=[blk] * 3 + [cblk, rowspec],
        out_shape=[jax.ShapeDtypeStruct((T, D), F32)] * 3 + [jax.ShapeDtypeStruct((T, 128), F32),
                                                             jax.ShapeDtypeStruct((GDN_H, 1, T), F32)],
        scratch_shapes=[pltpu.VMEM((gh, 128, 128), F32)],
        compiler_params=_cparams(("parallel", "arbitrary")),
    )(qn, kn, v, gb, gcs_t, do, states, tinv)


def _ssd_pair(x2, dt2, acs2):
    last = acs2[SSM_L - 1:SSM_L, :]
    return jnp.exp(acs2), jnp.exp(last - acs2), x2 * dt2


def _ssd_head(hh, acs2, arow, dec2, cbm, bm, incl, col):
    lmask = (col >= 64 * hh) & (col < 64 * hh + 64)
    sg = jnp.where(incl, jnp.exp(jnp.minimum(acs2[:, 64 * hh:64 * hh + 1] - arow, 0.0)), 0.0)
    dec_col = dec2[:, 64 * hh:64 * hh + 1]
    return lmask, sg, sg * cbm, dec_col, bm * dec_col


def ssd_fwd(xs, bc, dt_x, acs_x, acs_t):
    T = xs.shape[0]
    nc, L = T // SSM_L, SSM_L

    def body(x_ref, bc_ref, dt_ref, ac_ref, at_ref, y_ref, hst_ref, h_scr):
        @pl.when(pl.program_id(0) == 0)
        def _():
            h_scr[...] = jnp.zeros_like(h_scr)

        row, col = _iota((L, L), 0), _iota((L, L), 1)
        incl = row >= col
        P, H = [], []
        for gp in range(8):
            g = gp // 4
            bm, cm = bc_ref[:, 128 * g:128 * g + 128], bc_ref[:, 256 + 128 * g:384 + 128 * g]
            cbm = mm_nt(cm, bm) if gp % 4 == 0 else cbm
            sl = slice(128 * gp, 128 * gp + 128)
            acs2 = ac_ref[:, sl]
            lam2, dec2, xd2 = _ssd_pair(x_ref[:, sl], dt_ref[:, sl], acs2)
            P.append(dict(sl=sl, lam2=lam2, xd2=xd2, hprev=h_scr[gp], cm=cm))
            for hh in range(2):
                lmask, _, mmat, _, bd = _ssd_head(hh, acs2, at_ref[2 * gp + hh], dec2, cbm, bm, incl, col)
                H.append(dict(mmat=mmat, bd=bd, xdh=jnp.where(lmask, xd2, 0.0), xd2=xd2))
        ys = [mm(h["mmat"], h["xdh"]) for h in H]
        sts = [mm_tn(h["xd2"], h["bd"]) for h in H]
        zs = [mm_nt(p["cm"], p["hprev"]) for p in P]
        for gp, p in enumerate(P):
            hst_ref[gp // 4, gp % 4] = p["hprev"]
            y_ref[:, p["sl"]] = ys[2 * gp] + ys[2 * gp + 1] + p["lam2"] * zs[gp]
            lam_rows = jnp.where(row < 64, p["lam2"][L - 1:L, 0:1], p["lam2"][L - 1:L, 64:65])
            h_scr[gp] = lam_rows * p["hprev"] + jnp.where(row < 64, sts[2 * gp], sts[2 * gp + 1])

    blk = pl.BlockSpec((L, D), lambda c: (c, 0))
    return pl.pallas_call(
        body, name="ssd_fwd", grid=(nc,),
        in_specs=[blk, pl.BlockSpec((L, 512), lambda c: (c, 0)), blk, blk, pl.BlockSpec((SSM_H, 1, L), lambda c: (0, 0, c))],
        out_specs=[blk, pl.BlockSpec((2, None, 4, 128, 128), lambda c: (0, c, 0, 0, 0))],
        out_shape=[jax.ShapeDtypeStruct((T, D), F32), jax.ShapeDtypeStruct((2, nc, 4, 128, 128), F32)],
        scratch_shapes=[pltpu.VMEM((8, 128, 128), F32)],
        compiler_params=_cparams(("arbitrary",)),
    )(xs, bc, dt_x, acs_x, acs_t)


def ssd_bwd(xs, bc, dt_x, acs_x, acs_t, dy, hstates):
    T = xs.shape[0]
    nc, L = T // SSM_L, SSM_L

    def body(x_ref, bc_ref, dt_ref, ac_ref, at_ref, dy_ref, hst_ref,
             dx_ref, db_ref, dc_ref, dgate_ref, dar_ref, dh_scr):
        @pl.when(pl.program_id(0) == 0)
        def _():
            dh_scr[...] = jnp.zeros_like(dh_scr)

        row, col = _iota((L, L), 0), _iota((L, L), 1)
        rowc = _iota((L, 1), 0)
        incl = row >= col
        G = [dict(bm=bc_ref[:, 128 * g:128 * g + 128], cm=bc_ref[:, 256 + 128 * g:384 + 128 * g]) for g in range(2)]
        for gr in G:
            gr["cbm"] = mm_nt(gr["cm"], gr["bm"])
        P = []
        for gp in range(8):
            sl = slice(128 * gp, 128 * gp + 128)
            x2, dt2, dy2, acs2 = x_ref[:, sl], dt_ref[:, sl], dy_ref[:, sl], ac_ref[:, sl]
            lam2, dec2, xd2 = _ssd_pair(x2, dt2, acs2)
            P.append(dict(sl=sl, gr=G[gp // 4], x2=x2, dt2=dt2, dy2=dy2, acs2=acs2, lam2=lam2, dec2=dec2, xd2=xd2,
                          hprev=hst_ref[gp // 4, gp % 4], dhn=dh_scr[gp], dz=lam2 * dy2))
        zs = [mm_nt(p["gr"]["cm"], p["hprev"]) for p in P]
        dcm_t = [mm(p["dz"], p["hprev"]) for p in P]
        dh_z = [mm_tn(p["dz"], p["gr"]["cm"]) for p in P]
        H = []
        for gp, p in enumerate(P):
            p["yoff"] = p["dz"] * zs[gp]
            p["q_rows"] = _lanes(p["dhn"] * p["hprev"])
            for hh in range(2):
                lmask, sg, mmat, dec_col, bd = _ssd_head(hh, p["acs2"], at_ref[2 * gp + hh], p["dec2"], p["gr"]["cbm"],
                                                         p["gr"]["bm"], incl, col)
                H.append(dict(p=p, hh=hh, j=2 * gp + hh, lmask=lmask, sg=sg, mmat=mmat, dec_col=dec_col, bd=bd))
        dms = [mm_nt(jnp.where(h["lmask"], h["p"]["dy2"], 0.0), h["p"]["xd2"]) for h in H]
        a1s = [mm_tn(h["mmat"], h["p"]["dy2"]) for h in H]
        a2s = [mm_nt(h["bd"], h["p"]["dhn"]) for h in H]
        dbds = [mm(jnp.where(h["lmask"], h["p"]["xd2"], 0.0), h["p"]["dhn"]) for h in H]
        for gr in G:
            gr.update(dcb=jnp.zeros((L, L), F32), dbm=jnp.zeros((L, SSM_N), F32), comp=jnp.zeros((L, 128), F32))
        dxd = [jnp.zeros((L, 128), F32) for _ in P]
        for h, dm_raw, a1, a2, dbd in zip(H, dms, a1s, a2s, dbds):
            p, hh, j = h["p"], h["hh"], h["j"]
            gr, jg = p["gr"], j % 8
            dm = jnp.where(incl, dm_raw, 0.0)
            gr["dcb"] = gr["dcb"] + dm * h["sg"]
            e = dm * h["mmat"]
            dxd_h = jnp.where(h["lmask"], a1 + a2, 0.0)
            dxd[j // 2] = dxd[j // 2] + dxd_h
            gr["dbm"] = gr["dbm"] + h["dec_col"] * dbd
            t = _lanes(dbd * h["bd"])
            lam_h = p["lam2"][L - 1:L, 64 * hh:64 * hh + 1]
            in_head = (rowc >= 64 * hh) & (rowc < 64 * hh + 64)
            add_last = _sum_all(t) + _sum_all(jnp.where(in_head, p["q_rows"], 0.0)) * lam_h
            dacs_col = (_lanes(jnp.where(h["lmask"], p["yoff"], 0.0)) + _lanes(e) - t
                        + jnp.where(rowc == L - 1, add_last, 0.0))
            ddt_col = _lanes(dxd_h * p["x2"])
            dar_ref[j] = _rows(e)
            gr["comp"] = gr["comp"] + jnp.where(col == jg, dacs_col, 0.0) + jnp.where(col == 8 + jg, ddt_col, 0.0)
        for gp, p in enumerate(P):
            lam_rows = jnp.where(row < 64, p["lam2"][L - 1:L, 0:1], p["lam2"][L - 1:L, 64:65])
            dh_scr[gp] = dh_z[gp] + lam_rows * p["dhn"]
            dx_ref[:, p["sl"]] = p["dt2"] * dxd[gp]
        for g, gr in enumerate(G):
            lanes = slice(128 * g, 128 * g + 128)
            dcm = (dcm_t[4 * g] + dcm_t[4 * g + 1]) + (dcm_t[4 * g + 2] + dcm_t[4 * g + 3])
            db_ref[:, lanes] = gr["dbm"] + mm_tn(gr["dcb"], gr["cm"])
            dc_ref[:, lanes] = dcm + mm(gr["dcb"], gr["bm"])
            dgate_ref[:, lanes] = gr["comp"]

    rv = lambda c: (nc - 1 - c, 0)
    blk, blk256 = pl.BlockSpec((L, D), rv), pl.BlockSpec((L, 256), rv)
    rowspec = pl.BlockSpec((SSM_H, 1, L), lambda c: (0, 0, nc - 1 - c))
    return pl.pallas_call(
        body, name="ssd_bwd", grid=(nc,),
        in_specs=[blk, pl.BlockSpec((L, 512), rv), blk, blk, rowspec, blk,
                  pl.BlockSpec((2, None, 4, 128, 128), lambda c: (0, nc - 1 - c, 0, 0, 0))],
        out_specs=[blk, blk256, blk256, blk256, rowspec],
        out_shape=[jax.ShapeDtypeStruct((T, D), F32), jax.ShapeDtypeStruct((T, 256), F32),
                   jax.ShapeDtypeStruct((T, 256), F32), jax.ShapeDtypeStruct((T, 256), F32),
                   jax.ShapeDtypeStruct((SSM_H, 1, T), F32)],
        scratch_shapes=[pltpu.VMEM((8, 128, 128), F32)],
        compiler_params=_cparams(("arbitrary",)),
    )(xs, bc, dt_x, acs_x, acs_t, dy, hstates)


def _pos():
    return lax.axis_index("x"), lax.axis_index("y"), lax.axis_index("c")


def _other_chips(x, y):
    return [(1 - x, y), (x, 1 - y), (1 - x, 1 - y)]


def _rcopy(src, dst, ssem, rsem, dev):
    return pltpu.make_async_remote_copy(src_ref=src, dst_ref=dst, send_sem=ssem, recv_sem=rsem,
                                        device_id=dev, device_id_type=MESH)


def _rows_at(start, n):
    return pl.ds(pl.multiple_of(start, 8), n)


def _comm_call(body, name, out_shape, n_in, scratch):
    return pl.pallas_call(
        body, name=name, out_shape=out_shape, in_specs=[ANY] * n_in,
        out_specs=[ANY] * len(out_shape) if isinstance(out_shape, (list, tuple)) else ANY,
        scratch_shapes=scratch,
        compiler_params=pltpu.CompilerParams(has_side_effects=True),
    )


def _dma_sems(n):
    return pltpu.SemaphoreType.DMA((n,))


def ag_chips(name, shard):
    rr, cc = shard.shape
    h, nq = rr // 2, ICI_CHUNKS
    hq = h // nq

    def body(x_ref, out_ref, ssem, rsem):
        x, y, c = _pos()
        me_s = 2 * x + y
        chips = _other_chips(x, y)
        started = []
        for q in range(nq):
            rows = _rows_at(c * h + q * hq, hq)
            for j, (cx, cy) in enumerate(chips):
                cp = _rcopy(x_ref.at[rows], out_ref.at[me_s, rows], ssem.at[j * nq + q], rsem.at[j * nq + q], (cx, cy, c))
                cp.start()
                started.append(cp)
        for q in range(nq):
            rows = _rows_at(c * h + q * hq, hq)
            for j, (cx, cy) in enumerate(chips):
                blk = out_ref.at[2 * cx + cy, rows]
                _rcopy(blk, blk, ssem.at[j * nq + q], rsem.at[j * nq + q], (cx, cy, c)).wait_recv()
                k = 3 * nq + j * nq + q
                cp = _rcopy(blk, blk, ssem.at[k], rsem.at[k], (x, y, 1 - c))
                cp.start()
                started.append(cp)
        for q in range(nq):
            rows = _rows_at((1 - c) * h + q * hq, hq)
            for j, (cx, cy) in enumerate(chips):
                blk = out_ref.at[2 * cx + cy, rows]
                k = 3 * nq + j * nq + q
                _rcopy(blk, blk, ssem.at[k], rsem.at[k], (x, y, 1 - c)).wait_recv()
        for cp in started:
            cp.wait_send()

    return _comm_call(body, name, jax.ShapeDtypeStruct((4, rr, cc), shard.dtype), 1,
                      [_dma_sems(6 * nq), _dma_sems(6 * nq)])(shard)


def _with_own(shard, got, s_me):
    return lax.dynamic_update_index_in_dim(got, shard, s_me, 0)


def all_gather_chips(name, shard, s_me):
    return _with_own(shard, ag_chips(name, shard), s_me)


HBM_SPEC = pl.BlockSpec(memory_space=pltpu.HBM)
SEM_SPEC = pl.BlockSpec(memory_space=pltpu.SEMAPHORE)
SPLIT_EFFECT = pltpu.SideEffectType.DATAFLOW_SIDE_EFFECTING


def _split_copies(pieces, x_ref, land_ref, sems, arriving):
    x, y, c = _pos()
    return [_rcopy(s, d_in if arriving else d_out, sems[j], sems[3 + j], dev)
            for j, (s, d_out, d_in, dev) in enumerate(pieces(x_ref, land_ref, x, y, c))]


def split_copy_start(name, src, land_shape, pieces, after):
    def body(x_ref, land_ref, after_ref, *outs):
        for cp in _split_copies(pieces, x_ref, land_ref, outs[:6], False):
            cp.start()
        outs[8][...] = jnp.zeros_like(outs[8])

    dma = pltpu.SemaphoreType.DMA(())
    res = pl.pallas_call(
        body, name=name,
        out_shape=(dma,) * 6 + (pltpu.HBM(src.shape, src.dtype), pltpu.HBM(land_shape, src.dtype),
                                jax.ShapeDtypeStruct((8, 128), F32)),
        in_specs=(HBM_SPEC, HBM_SPEC, ANY),
        out_specs=(SEM_SPEC,) * 6 + (HBM_SPEC, HBM_SPEC, pl.BlockSpec(memory_space=pltpu.VMEM)),
        input_output_aliases={0: 6, 1: 7},
        compiler_params=pltpu.CompilerParams(has_side_effects=SPLIT_EFFECT),
    )(pltpu.with_memory_space_constraint(src, pltpu.HBM),
      pltpu.with_memory_space_constraint(lax.empty(land_shape, src.dtype), pltpu.HBM), after)
    return res[:6], res[6], res[7], res[8]


def split_copy_wait(name, sems, src_thru, land_thru, after, pieces):
    def body(x_ref, land_ref, *rest):
        for cp in _split_copies(pieces, x_ref, land_ref, rest[:6], False):
            cp.wait_send()
        for cp in _split_copies(pieces, x_ref, land_ref, rest[:6], True):
            cp.wait_recv()

    return pl.pallas_call(
        body, name=name,
        out_shape=(pltpu.HBM(src_thru.shape, src_thru.dtype), pltpu.HBM(land_thru.shape, land_thru.dtype)),
        in_specs=(HBM_SPEC, HBM_SPEC) + (SEM_SPEC,) * 6 + (ANY,), out_specs=(HBM_SPEC, HBM_SPEC),
        input_output_aliases={0: 0, 1: 1},
        compiler_params=pltpu.CompilerParams(has_side_effects=SPLIT_EFFECT),
    )(src_thru, land_thru, *sems, after)


def ag_pieces(h):
    def pieces(x_ref, land_ref, x, y, c):
        rows = _rows_at(c * h, h)
        return [(x_ref.at[rows], land_ref.at[2 * x + y, rows], land_ref.at[2 * cx + cy, rows], (cx, cy, c))
                for cx, cy in _other_chips(x, y)]
    return pieces


def rs_pieces(x_ref, land_ref, x, y, c):
    return [(x_ref.at[2 * cx + cy], land_ref.at[j], land_ref.at[j], (cx, cy, c))
            for j, (cx, cy) in enumerate(_other_chips(x, y))]


def ag_forward(name, got):
    _, rr, cc = got.shape
    h, nq = rr // 2, D2D_CHUNKS
    hq = h // nq

    def body(g_ref, out_ref, ssem, rsem):
        x, y, c = _pos()
        slots = [2 * cx + cy for cx, cy in _other_chips(x, y)]
        cps = []
        for j, s in enumerate(slots):
            for q in range(nq):
                blk = out_ref.at[s, _rows_at(c * h + q * hq, hq)]
                cp = _rcopy(blk, blk, ssem.at[j * nq + q], rsem.at[j * nq + q], (x, y, 1 - c))
                cp.start()
                cps.append(cp)
        for cp in cps:
            cp.wait_send()
        for j, s in enumerate(slots):
            for q in range(nq):
                blk = out_ref.at[s, _rows_at((1 - c) * h + q * hq, hq)]
                _rcopy(blk, blk, ssem.at[j * nq + q], rsem.at[j * nq + q], (x, y, 1 - c)).wait_recv()

    return pl.pallas_call(
        body, name=name, out_shape=jax.ShapeDtypeStruct(got.shape, got.dtype), in_specs=[ANY], out_specs=ANY,
        scratch_shapes=[_dma_sems(3 * nq), _dma_sems(3 * nq)], input_output_aliases={0: 0},
        compiler_params=pltpu.CompilerParams(has_side_effects=True),
    )(got)


def rs_pair(name, g):
    _, rr, cc = g.shape
    h, nq = rr // 2, D2D_CHUNKS
    hq = h // nq

    def body(g_ref, recv_ref, ssem, rsem):
        x, y, c = _pos()
        cps = []
        for q in range(nq):
            cp = _rcopy(g_ref.at[:, _rows_at((1 - c) * h + q * hq, hq), :], recv_ref.at[:, pl.ds(q * hq, hq), :],
                        ssem.at[q], rsem.at[q], (x, y, 1 - c))
            cp.start()
            cps.append(cp)
        for cp in cps:
            cp.wait()

    return _comm_call(body, name, jax.ShapeDtypeStruct((4, h, cc), g.dtype), 1, [_dma_sems(nq), _dma_sems(nq)])(g)


def rs_chips(name, p):
    _, h, cc = p.shape
    nq = ICI_CHUNKS
    hq = h // nq

    def body(p_ref, buf_ref, ssem, rsem):
        x, y, c = _pos()
        sends = []
        for q in range(nq):
            rows = pl.ds(q * hq, hq)
            for j, (cx, cy) in enumerate(_other_chips(x, y)):
                cp = _rcopy(p_ref.at[2 * cx + cy, rows], buf_ref.at[j, rows], ssem.at[j * nq + q],
                            rsem.at[j * nq + q], (cx, cy, c))
                cp.start()
                sends.append(cp)
        for cp in sends:
            cp.wait()

    return _comm_call(body, name, jax.ShapeDtypeStruct((3, h, cc), p.dtype), 1,
                      [_dma_sems(3 * nq), _dma_sems(3 * nq)])(p)


def rs_join(name, half):
    h, cc = half.shape
    nq = D2D_CHUNKS
    hq = h // nq

    def body(h_ref, out_ref, ssem, rsem):
        x, y, c = _pos()
        cps = []
        for q in range(nq):
            rows = pl.ds(q * hq, hq)
            cp = _rcopy(h_ref.at[rows], out_ref.at[rows], ssem.at[q], rsem.at[q], (x, y, 1 - c))
            cp.start()
            cps.append(cp)
        for cp in cps:
            cp.wait()

    return _comm_call(body, name, jax.ShapeDtypeStruct((h, cc), half.dtype), 1, [_dma_sems(nq), _dma_sems(nq)])(half)


def reduce_scatter(tag, g, tb, sp):
    return rs_end(rs_begin(tag, g, tb, sp, False), None)


def rs_begin(tag, g, tb, sp, split, after=None):
    _, rr, cc = g.shape
    h = rr // 2
    nbh = h // tb
    recv = rs_pair(tag + "_pair", g)
    mine_rows = lambda i, s: (i // nbh) * (2 * nbh) + s[0] * nbh + i % nbh
    part = rowwise(add2_fn, tag + "_add", 4 * h, tb, [R(g.reshape(4 * rr, cc), off=mine_rows), R(recv.reshape(4 * h, cc))],
                   [], [(cc, BF16)], sp=sp)[0].reshape(4, h, cc)
    st = dict(tag=tag, tb=tb, sp=sp, split=split, part=part)
    if split:
        st["sems"], st["part"], st["land"], st["token"] = split_copy_start(tag + "_start", part, (3, h, cc), rs_pieces,
                                                                           sp if after is None else after)
    return st


def rs_end(st, after):
    tag, tb, sp, part = st["tag"], st["tb"], st["sp"], st["part"]
    _, h, cc = part.shape
    nbh = h // tb
    if st["split"]:
        part, buf = split_copy_wait(tag + "_wait", st["sems"], part, st["land"], after, rs_pieces)
    else:
        buf = rs_chips(tag + "_chips", part)
    red = rowwise(sum4_fn, tag + "_sum", h, tb,
                  [R(part.reshape(4 * h, cc), off=lambda i, s: s[1] * nbh + i)]
                  + [R(buf.reshape(3 * h, cc), off=k * nbh) for k in range(3)],
                  [], [(cc, F32)], sp=sp)[0]
    return red, rs_join(tag + "_join", red)


def adam_halves(name, w, m, v, red, other, tb, blk0, sp):
    nbh = red.shape[0] // tb

    def fn(i, n, s, w_, m_, v_, r_, o_):
        g = jnp.where((blk0 + i) // nbh == s[0], r_, o_)
        return (g,) + _adamw(w_, g, m_, v_)

    half_rows = lambda i, s: (blk0 + i) % nbh
    return rowwise(fn, name, w.shape[0], tb, [R(w), R(m), R(v), R(red, off=half_rows), R(other, off=half_rows)],
                   [], [(w.shape[1], F32)] * 4, sp=sp)


SMALL_LANES = 3 * D


def all_reduce_items(name, items, after=None):
    flat = [a for it in items for a in it]
    shapes = [(sum(a.shape[0] for a in it), it[0].shape[1]) for it in items]
    nrows = -(-sum(s[0] for s in shapes) // 8) * 8
    extra = [] if after is None else [after]

    def body(*refs):
        ins, refs = refs[:len(flat)], refs[len(flat) + len(extra):]
        outs = refs[:len(items)]
        mine, buf, ssem, rsem = refs[len(items):]
        x, y, c = _pos()
        me = 4 * x + 2 * y + c
        mine[...] = jnp.zeros_like(mine)
        r = 0
        for ref in ins:
            mine[r:r + ref.shape[0], 0:ref.shape[1]] = ref[...]
            r += ref.shape[0]
        buf[me] = mine[...]
        cps = []
        for k in range(1, 8):
            dev = (x ^ (k >> 2), y ^ ((k >> 1) & 1), c ^ (k & 1))
            cp = _rcopy(mine, buf.at[me], ssem.at[k - 1], rsem.at[k - 1], dev)
            cp.start()
            cps.append(cp)
        for cp in cps:
            cp.wait()
        r = 0
        for (nr, n), out in zip(shapes, outs):
            acc = buf[0, r:r + nr, 0:n]
            for d in range(1, 8):
                acc = acc + buf[d, r:r + nr, 0:n]
            out[...] = acc
            r += nr

    vm = pl.BlockSpec(memory_space=pltpu.VMEM)
    return pl.pallas_call(
        body, name=name, out_shape=[jax.ShapeDtypeStruct(s, F32) for s in shapes],
        in_specs=[vm] * len(flat) + [ANY] * len(extra), out_specs=[vm] * len(items),
        scratch_shapes=[pltpu.VMEM((nrows, SMALL_LANES), F32), pltpu.VMEM((8, nrows, SMALL_LANES), F32),
                        _dma_sems(7), _dma_sems(7)],
        compiler_params=pltpu.CompilerParams(has_side_effects=True),
    )(*flat, *extra)


def adam_small(ws, gs, ms, vs):
    n = len(ws)

    def body(*refs):
        for k in range(n):
            w, g, m, v = (refs[j * n + k][...] for j in range(4))
            for j, val in enumerate(_adamw(w, g, m, v)):
                refs[(4 + j) * n + k][...] = val

    vm = pl.BlockSpec(memory_space=pltpu.VMEM)
    res = pl.pallas_call(
        body, name="adam_small", out_shape=[jax.ShapeDtypeStruct(w.shape, F32) for w in ws] * 3,
        in_specs=[vm] * (4 * n), out_specs=[vm] * (3 * n),
    )(*ws, *gs, *ms, *vs)
    return res[:n], res[n:2 * n], res[2 * n:]


def _sel(rows, cols, pairs):
    m = np.zeros((rows, cols), np.float32)
    for r, c in pairs:
        m[r, c] = 1.0
    return jnp.asarray(m)


def _pad_win(w):
    z = jnp.zeros((w.shape[0], 112), w.dtype)
    return jnp.concatenate([w[:, :4096], w[:, 4112:6672], w[:, 4096:4112], z, w[:, 6672:6688], z], axis=1)


def _unpad_win(wp):
    return jnp.concatenate([wp[:, :4096], wp[:, 6656:6672], wp[:, 4096:6656], wp[:, 6784:6800]], axis=1)


def kernel(x, mem, norm1_w, w_in, gdn_conv_w, gdn_a_log, gdn_dt_bias, gdn_norm_w, ssm_conv_w, ssm_conv_b, ssm_a_log, ssm_dt_bias, ssm_d, ssm_norm_w, w_out, norm2_w, mem_norm_w, wq_mem, wk_mem, wv_mem, wo_mem, norm3_w, w_up, w_down, final_norm_w, loss_target, m_norm1_w, m_w_in, m_gdn_conv_w, m_gdn_a_log, m_gdn_dt_bias, m_gdn_norm_w, m_ssm_conv_w, m_ssm_conv_b, m_ssm_a_log, m_ssm_dt_bias, m_ssm_d, m_ssm_norm_w, m_w_out, m_norm2_w, m_mem_norm_w, m_wq_mem, m_wk_mem, m_wv_mem, m_wo_mem, m_norm3_w, m_w_up, m_w_down, m_final_norm_w, v_norm1_w, v_w_in, v_gdn_conv_w, v_gdn_a_log, v_gdn_dt_bias, v_gdn_norm_w, v_ssm_conv_w, v_ssm_conv_b, v_ssm_a_log, v_ssm_dt_bias, v_ssm_d, v_ssm_norm_w, v_w_out, v_norm2_w, v_mem_norm_w, v_wq_mem, v_wk_mem, v_wv_mem, v_wo_mem, v_norm3_w, v_w_up, v_w_down, v_final_norm_w):
    T, M = x.shape[1], mem.shape[1]
    xi, yi, ci = _pos()
    s_me = 2 * xi + yi
    x0, mem0, tgt = x[0], mem[0], loss_target[0]
    tb = min(256, T)
    tbp = min(256, T)
    row = lambda v: v.reshape(1, -1)

    win_g = all_gather_chips("ag_win", w_in.astype(BF16), s_me)
    w_in_p = _pad_win(win_g.transpose(1, 0, 2).reshape(D, IN_COLS))
    keep = (ci == 0).astype(F32)
    gcw_z = lax.dynamic_update_slice(jnp.zeros((4, 3 * D), F32), gdn_conv_w * keep, (0, s_me * 768))
    scw_z = lax.dynamic_update_slice(jnp.zeros((4, 1536), F32), ssm_conv_w * keep, (0, s_me * 384))
    gcw, scw = all_reduce_items("ar_convw", [[gcw_z], [scw_z]])
    scw_x, scw_bc = scw[:, :D], scw[:, D:]
    rest_shard = jnp.concatenate([w_up, w_down, w_out, wq_mem, wk_mem, wv_mem, wo_mem], axis=0).astype(BF16)
    ag_sems, rest_thru, rest_land, ag_token = split_copy_start("ag_rest_start", rest_shard, (4,) + rest_shard.shape,
                                                               ag_pieces(rest_shard.shape[0] // 2), gcw)
    sp = jnp.stack([ci, s_me]).astype(jnp.int32)
    scb_x, scb_bc = row(ssm_conv_b[:D]), row(ssm_conv_b[D:])

    galog_c, gdtb_c = row(jnp.pad(gdn_a_log, (8, 112))), row(jnp.pad(gdn_dt_bias, (8, 112)))
    salog_c, sdtb_c = row(jnp.pad(ssm_a_log, (0, 112))), row(jnp.pad(ssm_dt_bias, (0, 112)))
    sd_x = row(jnp.repeat(ssm_d, 64))
    eb = _sel(128, D, [(h, 128 * h + l) for h in range(8) for l in range(128)])
    ea = _sel(128, D, [(8 + h, 128 * h + l) for h in range(8) for l in range(128)])
    e16 = _sel(128, D, [(h, 64 * h + l) for h in range(16) for l in range(64)])

    h1 = rowwise(rms_fwd_fn, "rms1", T, tb, [R(x0)], [row(norm1_w) + ag_token[0:1, 0:1]], [(D, BF16)])[0]
    p = matmul("mm_in", h1, w_in_p, "nn", 2048, 768, 1024, [F32])[0]
    gp_ins = [R(p, 3 * D, CB_QKV, "prev"), R(p, 128, CB_BA)]
    qn, kn, vv, gcs_x, beta_x, ggate, gcs_t = rowwise(gdn_prep_fn, "gdn_prep", T, tbp, gp_ins,
                                                      [gcw, galog_c, gdtb_c, eb, ea],
                                                      [(D, F32)] * 5 + [(128, F32), (-8, F32)])
    gcs_t = gcs_t.reshape(GDN_H, 1, T)
    gtb, ggh = min(128, T), 8
    o_gdn, s_states, tinv = gdn_fwd(qn, kn, vv, gcs_x, beta_x, gcs_t, gtb, ggh)
    gnw = row(gdn_norm_w)
    oa = rowwise(gdn_post_fn, "gdn_post", T, tb, [R(o_gdn), R(p, D, CB_Z)], [gnw], [(D, BF16)])[0]
    sp_ins = [R(p, D, CB_XS, "prev"), R(p, 512, CB_BC, "prev"), R(p, 128, CB_DT)]
    sp_full = [scw_x, scw_bc, scb_x, scb_bc, salog_c, sdtb_c]
    xs, bc, dt_x, acs_x, acs_t = rowwise(ssd_prep_fn, "ssd_prep", T, tbp, sp_ins, sp_full + [e16],
                                         [(D, F32), (512, F32), (D, F32), (D, F32), (-SSM_H, F32)])
    acs_t = acs_t.reshape(SSM_H, 1, T)
    y_ssd, h_states = ssd_fwd(xs, bc, dt_x, acs_x, acs_t)
    snw = row(ssm_norm_w)
    ob = rowwise(ssd_post_fn, "ssd_post", T, tb, [R(y_ssd), R(xs), R(p, D, CB_ZS)], [sd_x, snw], [(D, BF16)])[0]
    rest_thru, rest_land = split_copy_wait("ag_rest_wait", ag_sems, rest_thru, rest_land, ob,
                                           ag_pieces(rest_shard.shape[0] // 2))
    rest_g = _with_own(rest_thru, ag_forward("ag_rest_fwd", rest_land), s_me)
    assert D == 1024
    view = lambda shape, blk, at: dict(b_sel=(shape, blk, at))
    wup_n = view((D, D_FF), (None, D, D), lambda i, j, k: (j, 0, 0))
    wup_t = view((D, D_FF), (None, D, D), lambda i, j, k: (k, 0, 0))
    wdown_n = view((D_FF, D), (None, D, D), lambda i, j, k: (k, 1, 0))
    wdown_t = view((D_FF, D), (None, D, D), lambda i, j, k: (j, 1, 0))
    wout_a = view((D, D), (2, 512, D), lambda i, j, k: (0, 4, 0))
    wout_b = view((D, D), (2, 512, D), lambda i, j, k: (1, 4, 0))
    wq_v, wk_v, wv_v, wo_v = (view((D, D), (4, 256, D), lambda i, j, k, r=r: (0, r, 0)) for r in (10, 11, 12, 13))
    x1a = matmul("mm_out_a", oa, rest_g, "nn", 1024, 1024, 1024, [F32], _epi_res, [x0], **wout_a)[0]
    x1, h2 = matmul("mm_out_b", ob, rest_g, "nn", 1024, 1024, 1024, [F32, BF16], _epi_res_rms, [x1a],
                    [row(norm2_w)], **wout_b)

    mn = rowwise(rms_fwd_fn, "rms_mem", M, M, [R(mem0)], [row(mem_norm_w)], [(D, BF16)])[0]
    km = matmul("mm_k", mn, rest_g, "nn", 256, 1024, 1024, [BF16], **wk_v)[0]
    vm = matmul("mm_v", mn, rest_g, "nn", 256, 1024, 1024, [BF16], **wv_v)[0]
    qm = matmul("mm_q", h2, rest_g, "nn", 1024, 1024, 1024, [BF16], **wq_v)[0]
    ao = rowwise(attn_fn, "attn", T, tb, [R(qm)], [km, vm], [(D, BF16)])[0]
    x2, h3 = matmul("mm_o", ao, rest_g, "nn", 1024, 1024, 1024, [F32, BF16], _epi_res_rms, [x1], [row(norm3_w)], **wo_v)
    u, act = matmul("mm_up", h3, rest_g, "nn", 2048, 1024, 1024, [BF16, BF16], _epi_relu2, **wup_n)
    wdown_n2 = view((D_FF, D), (2, D, D), lambda i, j, k: (k, 1, 0))
    x3 = matmul("mm_down", act, rest_g, "nn", 1024, 1024, 2048, [F32], _epi_res, [x2], **wdown_n2)[0]
    dx3, dx3b, loss_lane, g_final = rowwise(final_fn, "final", T, tb, [R(x3), R(tgt)], [row(final_norm_w)],
                                            [(D, F32), (D, BF16)], [(1, D), (1, D)])
    loss = lax.psum(0.5 / D * jnp.sum(loss_lane), ("x", "y", "c"))

    dup = matmul("mm_dact", dx3b, rest_g, "nt", 2048, 1024, 1024, [BF16], _epi_dup, [u], **wdown_t)[0]
    def g_into(buf, blk, at):
        return dict(into=(buf, blk, lambda i, j, k, at=at: at(i, j)))

    grest = jax.ShapeDtypeStruct((4, 3584, D), F32)
    grest = matmul("mm_gdown", act, dx3b, "tn", 1024, 1024, 4096, [F32],
                   **g_into(grest, (None, 1024, D), lambda i, j: (i, 1, 0)))
    wup_t4 = dict(b_sel=((D, D_FF), (4, D, D), lambda i, j, k: (0, 0, 0), "side by side"))
    dh3 = matmul("mm_dh3", dup, rest_g, "nt", 1024, 1024, 4096, [F32], **wup_t4)[0]
    dx2, dx2b, g_n3 = rowwise(rms_bwd_fn, "rms3_bwd", T, tb, [R(x2), R(dh3), R(dx3)], [row(norm3_w)],
                              [(D, F32), (D, BF16)], [(1, D)])
    grest = matmul("mm_gup", h3, dup, "tn", 1024, 1024, 4096, [F32],
                   **g_into(grest, (None, 1024, D), lambda i, j: (j, 0, 0)))
    dao = matmul("mm_dao", dx2b, rest_g, "nt", 1024, 1024, 1024, [F32], **wo_v)[0]
    grest = matmul("mm_gwo", ao, dx2b, "tn", 1024, 1024, 2048, [F32],
                   **g_into(grest, (4, 256, D), lambda i, j: (0, 13, 0)))
    dqm, dkm, dvm = rowwise(attn_bwd_fn, "attn_bwd", T, tb, [R(qm), R(dao)], [km, vm], [(D, BF16)],
                            [(M, D), (M, D)])
    dx1, dx1b, g_n2 = matmul("mm_dh2", dqm, rest_g, "nt", 512, 1024, 1024, [F32, BF16], _epi_rms_bwd, [x1, dx2],
                             [row(norm2_w)], n_acc=1, **wq_v)
    grest = matmul("mm_gwq", h2, dqm, "tn", 1024, 1024, 2048, [F32],
                   **g_into(grest, (4, 256, D), lambda i, j: (0, 10, 0)))
    grest = matmul("mm_gwk", mn, dkm, "tn", 1024, 1024, 256, [F32],
                   **g_into(grest, (4, 256, D), lambda i, j: (0, 11, 0)))
    grest = matmul("mm_gwv", mn, dvm, "tn", 1024, 1024, 256, [F32],
                   **g_into(grest, (4, 256, D), lambda i, j: (0, 12, 0)))
    dmn_k = matmul("mm_dmk", dkm, rest_g, "nt", 256, 1024, 1024, [F32], **wk_v)[0]
    dmn = matmul("mm_dmv", dvm, rest_g, "nt", 256, 1024, 1024, [F32], _epi_res, [dmn_k], **wv_v)[0]
    g_nmem = rowwise(rms_bwd_w_fn, "rmsmem_bwd", M, M, [R(mem0), R(dmn)], [row(mem_norm_w)], [], [(1, D)])[0]
    doa = matmul("mm_doa", dx1b, rest_g, "nt", 2048, 1024, 1024, [F32], **wout_a)[0]
    dob = matmul("mm_dob", dx1b, rest_g, "nt", 2048, 1024, 1024, [F32], **wout_b)[0]
    grest = matmul("mm_gwout_a", oa, dx1b, "tn", 1024, 1024, 2048, [F32],
                   **g_into(grest, (2, 512, D), lambda i, j: (0, 4, 0)))
    grest = matmul("mm_gwout_b", ob, dx1b, "tn", 1024, 1024, 2048, [F32],
                   **g_into(grest, (2, 512, D), lambda i, j: (1, 4, 0)))

    rs_rest = rs_begin("rs_rest", grest, 256, sp, True)

    dp = jax.ShapeDtypeStruct((T, p.shape[1]), BF16)
    dy_ssd, dxs_dir, dp, g_snw, g_sd_lane = rowwise(
        ssd_post_bwd_fn, "ssd_post_bwd", T, tb, [R(y_ssd), R(xs), R(p, D, CB_ZS), R(dob)],
        [sd_x + rs_rest["token"][0:1, 0:1], snw],
        [(D, F32), (D, F32), (D, BF16, dp, CB_ZS)], [(1, D), (1, D)])
    dxs_scan, db_s, dc_s, dgate, dacs_t = ssd_bwd(xs, bc, dt_x, acs_x, acs_t, dy_ssd, h_states)
    spb = rowwise(ssd_prep_bwd_fn, "ssd_prep_bwd", T, tbp,
                  sp_ins + [R(dxs_scan), R(dxs_dir), R(db_s), R(dc_s), R(dgate), RC(dacs_t.reshape(SSM_H, T))], sp_full,
                  [(D, F32), (512, F32), (128, BF16, dp, CB_DT)],
                  [(1, D)] * 4 + [(1, 512)] * 4 + [(1, D), (1, 512), (1, 128), (1, 128)])
    dyc_x, dyc_bc, dp = spb[:3]
    dp = rowwise(conv_bwd_fn, "conv_bwd_x", T, tbp, [R(dyc_x, halo="next")], [scw_x], [(D, BF16, dp, CB_XS)])[0]
    dp = rowwise(conv_bwd_fn, "conv_bwd_bc", T, tbp, [R(dyc_bc, halo="next")], [scw_bc], [(512, BF16, dp, CB_BC)])[0]

    do_gdn, dp, g_gnw = rowwise(gdn_post_bwd_fn, "gdn_post_bwd", T, tb, [R(o_gdn), R(p, D, CB_Z), R(doa)], [gnw],
                                [(D, F32), (D, BF16, dp, CB_Z)], [(1, 128)])
    dqn, dkn, dvv, dggate, dgcs_t = gdn_bwd(qn, kn, vv, ggate, gcs_t, do_gdn, s_states, tinv, gtb, ggh)
    gpb = rowwise(gdn_prep_bwd_fn, "gdn_prep_bwd", T, tbp,
                  gp_ins + [R(dqn), R(dkn), R(dvv), R(dggate), RC(dgcs_t.reshape(GDN_H, T))],
                  [gcw, galog_c, gdtb_c],
                  [(3 * D, F32), (128, BF16, dp, CB_BA)], [(1, 3 * D)] * 4 + [(1, 128), (1, 128)])
    dyc_qkv, dp = gpb[:2]
    dp = rowwise(conv_bwd_fn, "conv_bwd_qkv", T, tbp, [R(dyc_qkv, halo="next")], [gcw], [(3 * D, BF16, dp, CB_QKV)])[0]
    dh1 = matmul("mm_dh1", dp, w_in_p, "nt", 1024, 1024, 2304, [F32])[0]
    grad_x, g_n1 = rowwise(rms_bwd1_fn, "rms1_bwd", T, tb, [R(x0), R(dh1), R(dx1)], [row(norm1_w)], [(D, F32)], [(1, D)])
    g_win_p = matmul("mm_gwin", h1, dp, "tn", 1024, 768, 4096, [F32])[0]

    items = [[g_n1], [gpb[6]], [gpb[7]], [g_gnw], [spb[11]], [spb[12]], [spb[13]], [spb[14]], [g_sd_lane], [g_snw],
             [g_n2], [g_nmem], [g_n3], [g_final], list(gpb[2:6]), list(spb[3:7]), list(spb[7:11])]
    (gr_n1, r_galog, r_gdtb, gr_gnw, r_scb_x, r_scb_bc, r_salog, r_sdtb, r_sd, gr_snw, gr_n2, gr_nmem, gr_n3,
     gr_final, r_gcw, r_scw_x, r_scw_bc) = all_reduce_items("ar_grads", items)
    gr_galog, gr_gdtb = r_galog[:, 8:16], r_gdtb[:, 8:16]
    gr_salog, gr_sdtb = r_salog[:, :SSM_H], r_sdtb[:, :SSM_H]
    gr_sd = r_sd.reshape(SSM_H, SSM_P).sum(axis=1).reshape(1, SSM_H)
    gr_scb = jnp.concatenate([r_scb_x, r_scb_bc], axis=1)
    gr_gcw = lax.dynamic_slice(r_gcw, (0, s_me * 768), (4, 768))
    gr_scw = lax.dynamic_slice(jnp.concatenate([r_scw_x, r_scw_bc], axis=1), (0, s_me * 384), (4, 384))

    g_win = _unpad_win(g_win_p).reshape(D, 4, IN_COLS // 4).transpose(1, 0, 2)
    rs_win = rs_begin("rs_win", g_win, 256, sp, True, gr_n1)
    red_r, oth_r = rs_end(rs_rest, rs_win["token"])

    big = {}
    for n, w, m, v, blk0 in (("w_up", w_up, m_w_up, v_w_up, 0), ("w_down", w_down, m_w_down, v_w_down, 4),
                             ("w_out", w_out, m_w_out, v_w_out, 8), ("wq_mem", wq_mem, m_wq_mem, v_wq_mem, 10),
                             ("wk_mem", wk_mem, m_wk_mem, v_wk_mem, 11), ("wv_mem", wv_mem, m_wv_mem, v_wv_mem, 12),
                             ("wo_mem", wo_mem, m_wo_mem, v_wo_mem, 13)):
        big[n] = adam_halves("adam_" + n, w, m, v, red_r, oth_r, 256, blk0, sp)
    red_w, oth_w = rs_end(rs_win, big["wo_mem"][1])
    big["w_in"] = adam_halves("adam_win", w_in, m_w_in, v_w_in, red_w, oth_w, 256, 0, sp)
    names_s =["norm1_w", "gdn_conv_w", "gdn_a_log", "gdn_dt_bias", "gdn_norm_w", "ssm_conv_w", "ssm_conv_b",
               "ssm_a_log", "ssm_dt_bias", "ssm_d", "ssm_norm_w", "norm2_w", "mem_norm_w", "norm3_w", "final_norm_w"]
    w_s = [norm1_w, gdn_conv_w, gdn_a_log, gdn_dt_bias, gdn_norm_w, ssm_conv_w, ssm_conv_b, ssm_a_log, ssm_dt_bias,
           ssm_d, ssm_norm_w, norm2_w, mem_norm_w, norm3_w, final_norm_w]
    g_s = [gr_n1, gr_gcw, gr_galog, gr_gdtb, gr_gnw, gr_scw, gr_scb, gr_salog, gr_sdtb, gr_sd, gr_snw, gr_n2,
           gr_nmem, gr_n3, gr_final]
    m_s = [m_norm1_w, m_gdn_conv_w, m_gdn_a_log, m_gdn_dt_bias, m_gdn_norm_w, m_ssm_conv_w, m_ssm_conv_b, m_ssm_a_log,
           m_ssm_dt_bias, m_ssm_d, m_ssm_norm_w, m_norm2_w, m_mem_norm_w, m_norm3_w, m_final_norm_w]
    v_s = [v_norm1_w, v_gdn_conv_w, v_gdn_a_log, v_gdn_dt_bias, v_gdn_norm_w, v_ssm_conv_w, v_ssm_conv_b, v_ssm_a_log,
           v_ssm_dt_bias, v_ssm_d, v_ssm_norm_w, v_norm2_w, v_mem_norm_w, v_norm3_w, v_final_norm_w]
    shp_s = [w.shape for w in w_s]
    as2d = lambda a: a if a.ndim == 2 else a.reshape(1, -1)
    d_l, m_l, v_l = adam_small([as2d(a) for a in w_s], [as2d(a) for a in g_s], [as2d(a) for a in m_s],
                               [as2d(a) for a in v_s])

    grads, deltas, new_m, new_v = {}, {}, {}, {}
    for n, (gg, dd, mm_, vv_) in big.items():
        grads[n], deltas[n], new_m[n], new_v[n] = gg, dd, mm_, vv_
    for k, n in enumerate(names_s):
        grads[n] = g_s[k].reshape(shp_s[k])
        deltas[n], new_m[n], new_v[n] = (a[k].reshape(shp_s[k]) for a in (d_l, m_l, v_l))
    order = ["norm1_w", "w_in", "gdn_conv_w", "gdn_a_log", "gdn_dt_bias", "gdn_norm_w", "ssm_conv_w", "ssm_conv_b",
             "ssm_a_log", "ssm_dt_bias", "ssm_d", "ssm_norm_w", "w_out", "norm2_w", "mem_norm_w", "wq_mem", "wk_mem",
             "wv_mem", "wo_mem", "norm3_w", "w_up", "w_down", "final_norm_w"]
    return (loss, grad_x[None], *[grads[n] for n in order], *[deltas[n] for n in order],
            *[new_m[n] for n in order], *[new_v[n] for n in order])
```

```python
import numpy as np
import jax
import jax.numpy as jnp
from jax import lax
from jax.experimental import pallas as pl
from jax.experimental.pallas import tpu as pltpu

F32, BF16 = jnp.float32, jnp.bfloat16
MESH = pl.DeviceIdType.MESH
ANY = pl.BlockSpec(memory_space=pl.ANY)

EPS = 1e-6
D = 1024
GDN_H, GDN_DK, GDN_C = 8, 128, 64
SSM_H, SSM_P, SSM_N, SSM_L = 16, 64, 128, 128
MEM_H, MEM_DH = 4, 256
D_FF = 4096
IN_COLS = 6688
CB_QKV, CB_Z, CB_ZS, CB_XS, CB_BC, CB_BA, CB_DT = 0, 3, 4, 5, 12, 52, 53
VMEM_LIMIT = 56 * 1024 * 1024
D2D_CHUNKS = 8
ICI_CHUNKS = 4

ADAM_LR, ADAM_B1, ADAM_B2, ADAM_EPS, ADAM_WD, ADAM_STEP = 0.001, 0.9, 0.999, 1e-08, 0.01, 10


def _dg(a, b, ca, cb):
    return lax.dot_general(a, b, (((ca,), (cb,)), ((), ())), preferred_element_type=F32)


def _bf(x):
    return x.astype(BF16)


def mm(a, b):
    return _dg(_bf(a), _bf(b), 1, 0)


def mm_nt(a, b):
    return _dg(_bf(a), _bf(b), 1, 1)


def mm_tn(a, b):
    return _dg(_bf(a), _bf(b), 0, 0)


def mm_sel(a, sel):
    hi = a.astype(BF16)
    r1 = a - hi.astype(F32)
    mid = r1.astype(BF16)
    lo = (r1 - mid.astype(F32)).astype(BF16)
    s = sel.astype(BF16)
    return _dg(hi, s, 1, 0) + (_dg(mid, s, 1, 0) + _dg(lo, s, 1, 0))


def mm3(a, b):
    ah, bh = a.astype(BF16), b.astype(BF16)
    al, bl = (a - ah.astype(F32)).astype(BF16), (b - bh.astype(F32)).astype(BF16)
    return _dg(ah, bh, 1, 0) + (_dg(ah, bl, 1, 0) + _dg(al, bh, 1, 0))


def _iota(shape, dim):
    return lax.broadcasted_iota(jnp.int32, shape, dim)


def _chunk_cumsum(x, c):
    pos = _iota(x.shape, 0) & (c - 1)
    s = 1
    while s < c:
        x = x + jnp.where(pos >= s, pltpu.roll(x, s, 0), 0.0)
        s *= 2
    return x


def _chunk_revcumsum(x, c):
    n = x.shape[0]
    pos = _iota(x.shape, 0) & (c - 1)
    s = 1
    while s < c:
        x = x + jnp.where(pos < c - s, pltpu.roll(x, n - s, 0), 0.0)
        s *= 2
    return x


def _sig(x):
    return jax.nn.sigmoid(x)


def _softplus(x):
    return jnp.maximum(x, 0.0) + jnp.log(1.0 + jnp.exp(-jnp.abs(x)))


def _rows(v):
    return jnp.sum(v, axis=0, keepdims=True)


def _lanes(v):
    return jnp.sum(v, axis=1, keepdims=True)


def _sum_all(v):
    return _rows(_lanes(v))


def _cparams(sem):
    return pltpu.CompilerParams(dimension_semantics=sem, vmem_limit_bytes=VMEM_LIMIT)


def rowwise(fn, name, T, tb, row_ins, full_ins, row_outs, acc_outs=(), sp=None):
    nblk = T // tb
    assert nblk * tb == T
    has_sp = sp is not None

    def imap(f):
        return (lambda i, s: f(i, s)) if has_sp else (lambda i: f(i, None))

    in_specs, args = [], []
    for arr, w, cb, halo, off in row_ins:
        if halo == "col":
            in_specs.append(pl.BlockSpec((w, tb), imap(lambda i, s: (0, i))))
            args.append(arr)
            continue
        rowf = off if callable(off) else (lambda i, s, off=off: i + off)
        in_specs.append(pl.BlockSpec((tb, w), imap(lambda i, s, cb=cb, rowf=rowf: (rowf(i, s), cb))))
        args.append(arr)
        if halo == "prev":
            r = tb // 8
            in_specs.append(pl.BlockSpec((8, w), imap(lambda i, s, cb=cb, r=r: (jnp.maximum(i * r - 1, 0), cb))))
            args.append(arr)
        elif halo == "next":
            r, last = tb // 8, T // 8 - 1
            in_specs.append(pl.BlockSpec((8, w), imap(lambda i, s, cb=cb, r=r, last=last:
                                                      (jnp.minimum((i + 1) * r, last), cb))))
            args.append(arr)
    for arr in full_ins:
        in_specs.append(pl.BlockSpec(arr.shape, imap(lambda i, s, nd=arr.ndim: (0,) * nd)))
        args.append(arr)
    n_in, n_ro = len(args), len(row_outs)
    out_shape, out_specs, aliases = [], [], {}
    for k, (w, dt, *dest) in enumerate(row_outs):
        if dest:
            buf, cb = dest
            out_shape.append(jax.ShapeDtypeStruct(buf.shape, buf.dtype))
            out_specs.append(pl.BlockSpec((tb, w), imap(lambda i, s, cb=cb: (i, cb))))
            if not isinstance(buf, jax.ShapeDtypeStruct):
                aliases[len(args) + int(has_sp)] = k
                in_specs.append(ANY)
                args.append(buf)
        elif w < 0:
            out_shape.append(jax.ShapeDtypeStruct((-w, T), dt))
            out_specs.append(pl.BlockSpec((-w, tb), imap(lambda i, s: (0, i))))
        else:
            out_shape.append(jax.ShapeDtypeStruct((T, w), dt))
            out_specs.append(pl.BlockSpec((tb, w), imap(lambda i, s: (i, 0))))
    for shp in acc_outs:
        out_shape.append(jax.ShapeDtypeStruct(shp, F32))
        out_specs.append(pl.BlockSpec(shp, imap(lambda i, s, nd=len(shp): (0,) * nd)))

    def body(*refs):
        i = pl.program_id(0)
        if has_sp:
            sp_ref, refs = refs[0], refs[1:]
            vals = fn(i, nblk, sp_ref, *[r[...] for r in refs[:n_in]])
        else:
            vals = fn(i, nblk, *[r[...] for r in refs[:n_in]])
        outs = refs[n_in + len(aliases):]
        for ref, val in zip(outs[:n_ro], vals[:n_ro]):
            ref[...] = val.astype(ref.dtype)
        for ref, val in zip(outs[n_ro:], vals[n_ro:]):
            @pl.when(i == 0)
            def _(ref=ref, val=val):
                ref[...] = val

            @pl.when(i > 0)
            def _(ref=ref, val=val):
                ref[...] += val

    cparams = _cparams(("arbitrary",) if acc_outs else ("parallel",))
    if has_sp:
        return pl.pallas_call(
            body, name=name, out_shape=out_shape, compiler_params=cparams, input_output_aliases=aliases,
            grid_spec=pltpu.PrefetchScalarGridSpec(num_scalar_prefetch=1, grid=(nblk,), in_specs=in_specs,
                                                   out_specs=out_specs),
        )(sp, *args)
    return pl.pallas_call(
        body, name=name, grid=(nblk,), in_specs=in_specs, out_specs=out_specs, out_shape=out_shape,
        compiler_params=cparams, input_output_aliases=aliases,
    )(*args)


def R(arr, w=None, cb=0, halo=None, off=0):
    return (arr, arr.shape[1] if w is None else w, cb, halo, off)


def RC(arr):
    return (arr, arr.shape[0], 0, "col", 0)


def matmul(name, a, b, form, tm, tn, tk, out_dtypes, epi=None, extras=(), rows=(), into=None, n_acc=0, b_sel=None):
    bs = b.shape if b_sel is None else b_sel[0]
    if form == "nn":
        (M, K), N = a.shape, bs[1]
    elif form == "nt":
        (M, K), N = a.shape, bs[0]
    else:
        (K, M), N = a.shape, bs[1]
    tm, tn, tk = min(tm, M), min(tn, N), min(tk, K)
    assert M % tm == 0 and N % tn == 0 and K % tk == 0, (name, M, N, K, tm, tn, tk)

    def b_spec_of(blk, at):
        if b_sel is None:
            return pl.BlockSpec(blk, lambda i, j, k: at(i, j, k))
        blk3 = b_sel[1]
        assert int(np.prod([d for d in blk3 if d is not None])) == blk[0] * blk[1], (name, blk3, blk)
        return pl.BlockSpec(blk3, lambda i, j, k: b_sel[2](i, j, k))

    if form == "nn":
        a_spec = pl.BlockSpec((tm, tk), lambda i, j, k: (i, k))
        b_spec = b_spec_of((tk, tn), lambda i, j, k: (k, j))
        ca, cb = 1, 0
    elif form == "nt":
        a_spec = pl.BlockSpec((tm, tk), lambda i, j, k: (i, k))
        b_spec = b_spec_of((tn, tk), lambda i, j, k: (j, k))
        ca, cb = 1, 1
    else:
        a_spec = pl.BlockSpec((tk, tm), lambda i, j, k: (k, i))
        b_spec = b_spec_of((tk, tn), lambda i, j, k: (k, j))
        ca, cb = 0, 0
    nk, ne, no = K // tk, len(extras) + len(rows), len(out_dtypes)
    if epi is None:
        epi = lambda acc: (acc,)

    assert n_acc == 0 or tn == N

    def body(a_ref, b_ref, *rest):
        ex, outs, accs, acc = rest[:ne], rest[ne:ne + no], rest[ne + no:ne + no + n_acc], rest[ne + no + n_acc]
        i, k = pl.program_id(0), pl.program_id(2)

        def finish(total):
            vals = epi(total, *[e[...] for e in ex])
            for r, v in zip(outs, vals[:no]):
                r[...] = v.astype(r.dtype).reshape(r.shape)
            for r, v in zip(accs, vals[no:]):
                @pl.when(i == 0)
                def _(r=r, v=v):
                    r[...] = v

                @pl.when(i > 0)
                def _(r=r, v=v):
                    r[...] += v

        b_tile = b_ref[...]
        if b_sel is not None and len(b_sel) > 3:
            b_tile = jnp.concatenate([b_tile[s] for s in range(b_tile.shape[0])], axis=1)
        prod = _dg(_bf(a_ref[...]), _bf(b_tile.reshape(-1, b_tile.shape[-1])), ca, cb)
        if nk == 1:
            finish(prod)
            return

        @pl.when(k == 0)
        def _():
            acc[...] = prod

        @pl.when(k > 0)
        def _():
            acc[...] += prod

        @pl.when(k == nk - 1)
        def _():
            finish(acc[...])

    mn = pl.BlockSpec((tm, tn), lambda i, j, k: (i, j))
    rw = pl.BlockSpec((1, tn), lambda i, j, k: (0, j))
    acc_scratch = pltpu.VMEM((tm, tn) if nk > 1 else (8, 128), F32)
    if into is not None:
        buf, blk, bmap = into
        assert ne == 0 and no == 1
        aliased = not isinstance(buf, jax.ShapeDtypeStruct)

        def body_into(a_ref, b_ref, *rest):
            body(a_ref, b_ref, *rest[-2:])

        return pl.pallas_call(
            body_into, name=name, grid=(M // tm, N // tn, nk),
            in_specs=[a_spec, b_spec] + ([ANY] if aliased else []), out_specs=pl.BlockSpec(blk, bmap),
            out_shape=jax.ShapeDtypeStruct(buf.shape, buf.dtype),
            scratch_shapes=[acc_scratch],
            input_output_aliases={2: 0} if aliased else {},
            compiler_params=_cparams(("parallel", "parallel", "arbitrary")),
        )(a, b, *([buf] if aliased else []))
    return pl.pallas_call(
        body, name=name, grid=(M // tm, N // tn, nk),
        in_specs=[a_spec, b_spec] + [mn] * len(extras) + [rw] * len(rows), out_specs=[mn] * no + [rw] * n_acc,
        out_shape=[jax.ShapeDtypeStruct((M, N), dt) for dt in out_dtypes] + [jax.ShapeDtypeStruct((1, N), F32)] * n_acc,
        scratch_shapes=[acc_scratch],
        compiler_params=_cparams(("arbitrary",) * 3 if n_acc else ("parallel", "parallel", "arbitrary")),
    )(a, b, *extras, *rows)


def _epi_res(acc, res):
    return (res + acc,)


def _epi_rms_bwd(acc, x, dres, w):
    return rms_bwd_fn(0, 0, x, acc, dres, w)


def rms_bwd1_fn(i, n, x, dh, dres, w):
    dx, _, gw = rms_bwd_fn(i, n, x, dh, dres, w)
    return dx, gw


def _epi_final(acc, res, tgt, w):
    return final_fn(0, 0, res + acc, tgt, w)


def _epi_res_rms(acc, res, w):
    x = res + acc
    return (x, x * lax.rsqrt(jnp.mean(x * x, axis=-1, keepdims=True) + EPS) * w)


def _epi_relu2(acc):
    u = jnp.maximum(acc, 0.0)
    return (u, u * u)


def _epi_dup(acc, u):
    return (acc * 2.0 * u.astype(F32),)


def _conv(x, halo, w, i):
    halo = jnp.where(i == 0, 0.0, halo)
    xt = jnp.concatenate([halo, x], axis=0)
    shifted = [pltpu.roll(xt, 3 - k, 0)[8:, :] for k in range(3)] + [x]
    y = shifted[3] * w[3:4, :]
    for k in range(3):
        y = y + shifted[k] * w[k:k + 1, :]
    return y, shifted


def _l2n(x, scale):
    outs = []
    for h in range(x.shape[1] // 128):
        xh = x[:, 128 * h:128 * h + 128]
        outs.append(xh * (lax.rsqrt(jnp.sum(xh * xh, axis=-1, keepdims=True) + EPS) * scale))
    return jnp.concatenate(outs, axis=1)


def _l2n_bwd(x, dy, scale):
    outs = []
    for h in range(x.shape[1] // 128):
        xh, dh = x[:, 128 * h:128 * h + 128], dy[:, 128 * h:128 * h + 128] * scale
        r = lax.rsqrt(jnp.sum(xh * xh, axis=-1, keepdims=True) + EPS)
        outs.append(r * dh - xh * (r * r * r) * jnp.sum(xh * dh, axis=-1, keepdims=True))
    return jnp.concatenate(outs, axis=1)


def rms_fwd_fn(i, n, x, w):
    r = lax.rsqrt(jnp.mean(x * x, axis=-1, keepdims=True) + EPS)
    return (x * r * w,)


def rms_bwd_fn(i, n, x, dh, dres, w):
    r = lax.rsqrt(jnp.mean(x * x, axis=-1, keepdims=True) + EPS)
    g = dh * w
    dx = dres + r * g - x * (r * r * r) * jnp.mean(x * g, axis=-1, keepdims=True)
    return dx, dx, _rows(dh * x * r)


def rms_bwd_w_fn(i, n, x, dh, w):
    r = lax.rsqrt(jnp.mean(x * x, axis=-1, keepdims=True) + EPS)
    return (_rows(dh * x * r),)


def final_fn(i, n, x, tgt, w):
    r = lax.rsqrt(jnp.mean(x * x, axis=-1, keepdims=True) + EPS)
    xn = x * r
    e = xn * w - tgt
    dy = e * (1.0 / D)
    g = dy * w
    dx = r * g - x * (r * r * r) * jnp.mean(x * g, axis=-1, keepdims=True)
    return dx, dx, _rows(e * e), _rows(dy * xn)


def _gdn_gates(ba, alog_c, dtb_c):
    col = _iota(ba.shape, 1)
    amask = (col >= 8) & (col < 16)
    beta = jnp.where(col < 8, _sig(ba), 0.0)
    z = ba + dtb_c
    ea_ = jnp.exp(alog_c)
    return beta, z, ea_, jnp.where(amask, -ea_ * _softplus(z), 0.0), amask


def _cols(x, g):
    return x[:, 128 * g:128 * g + 128]


def gdn_prep_fn(i, n, qkv, halo, ba, cw, alog_c, dtb_c, eb, ea):
    outs = [[], [], []]
    for g in range(3 * GDN_H):
        yc, _ = _conv(_cols(qkv, g), _cols(halo, g), _cols(cw, g), i)
        act = yc * _sig(yc)
        if g < 2 * GDN_H:
            act = _l2n(act, GDN_DK ** -0.5 if g < GDN_H else 1.0)
        outs[g // GDN_H].append(act)
    beta, _, _, gg, _ = _gdn_gates(ba, alog_c, dtb_c)
    gcs = _chunk_cumsum(gg, GDN_C)
    return (*[jnp.concatenate(o, axis=1) for o in outs], mm_sel(gcs, ea), mm_sel(beta, eb), beta + gcs,
            jnp.transpose(gcs)[8:16, :])


def gdn_prep_bwd_fn(i, n, qkv, halo, ba, dqn, dkn, dv, dgb, dgcs_t, cw, alog_c, dtb_c):
    dycs, dwl = [], [[], [], [], []]
    for g in range(3 * GDN_H):
        yc, shifted = _conv(_cols(qkv, g), _cols(halo, g), _cols(cw, g), i)
        sg = _sig(yc)
        act = yc * sg
        if g < GDN_H:
            d = _l2n_bwd(act, _cols(dqn, g), GDN_DK ** -0.5)
        elif g < 2 * GDN_H:
            d = _l2n_bwd(act, _cols(dkn, g - GDN_H), 1.0)
        else:
            d = _cols(dv, g - 2 * GDN_H)
        dyc_g = d * (sg * (1.0 + yc * (1.0 - sg)))
        dycs.append(dyc_g)
        for k in range(4):
            dwl[k].append(_rows(dyc_g * shifted[k]))
    dyc = jnp.concatenate(dycs, axis=1)
    dws = [jnp.concatenate(l, axis=1) for l in dwl]
    beta, z, ea_, g, amask = _gdn_gates(ba, alog_c, dtb_c)
    tbn = ba.shape[0]
    rowpart = jnp.transpose(jnp.concatenate([jnp.zeros((8, tbn), F32), dgcs_t, jnp.zeros((112, tbn), F32)], axis=0))
    dg = _chunk_revcumsum(jnp.where(amask, dgb, 0.0) - rowpart, GDN_C)
    draw = jnp.where(amask, dg * (-ea_) * _sig(z), 0.0)
    dba = draw + dgb * beta * (1.0 - beta)
    return (dyc, dba, dws[0], dws[1], dws[2], dws[3], _rows(dg * g), _rows(draw))


def conv_bwd_fn(i, n, dyc, halo, w):
    halo = jnp.where(i == n - 1, 0.0, halo)
    tb = dyc.shape[0]
    outs = []
    for g in range(dyc.shape[1] // 128):
        d, wg = _cols(dyc, g), _cols(w, g)
        xt = jnp.concatenate([d, _cols(halo, g)], axis=0)
        dx = d * wg[3:4, :]
        for k in range(3):
            dx = dx + pltpu.roll(xt, tb + 8 - (3 - k), 0)[:tb, :] * wg[k:k + 1, :]
        outs.append(dx)
    return (jnp.concatenate(outs, axis=1),)


def gdn_post_fn(i, n, o, z, w):
    outs = []
    for h in range(GDN_H):
        oh, zh = o[:, 128 * h:128 * h + 128], z[:, 128 * h:128 * h + 128]
        r = lax.rsqrt(jnp.mean(oh * oh, axis=-1, keepdims=True) + EPS)
        outs.append(oh * r * w * (zh * _sig(zh)))
    return (jnp.concatenate(outs, axis=1),)


def gdn_post_bwd_fn(i, n, o, z, doa, w):
    dos, dzs, dw = [], [], None
    for h in range(GDN_H):
        sl = slice(128 * h, 128 * h + 128)
        oh, zh, dh = o[:, sl], z[:, sl], doa[:, sl]
        r = lax.rsqrt(jnp.mean(oh * oh, axis=-1, keepdims=True) + EPS)
        s = _sig(zh)
        dn = dh * (zh * s)
        dzs.append(dh * (oh * r * w) * (s * (1.0 + zh * (1.0 - s))))
        t = _rows(dn * oh * r)
        dw = t if dw is None else dw + t
        g = dn * w
        dos.append(r * g - oh * (r * r * r) * jnp.mean(oh * g, axis=-1, keepdims=True))
    return jnp.concatenate(dos, axis=1), jnp.concatenate(dzs, axis=1), dw


def _ssd_gates(dtblk, alog_c, dtb_c):
    hmask = _iota(dtblk.shape, 1) < SSM_H
    z = dtblk + dtb_c
    return jnp.where(hmask, _softplus(z), 0.0), -jnp.exp(alog_c), z, hmask


def _silu_conv_cols(x, halo, w, b, i):
    outs = []
    for g in range(x.shape[1] // 128):
        yc, _ = _conv(_cols(x, g), _cols(halo, g), _cols(w, g), i)
        yc = yc + _cols(b, g)
        outs.append(yc * _sig(yc))
    return jnp.concatenate(outs, axis=1)


def _silu_conv_bwd_cols(x, halo, w, b, dout, i):
    dycs, dwl = [], [[], [], [], []]
    for g in range(x.shape[1] // 128):
        yc, shifted = _conv(_cols(x, g), _cols(halo, g), _cols(w, g), i)
        yc = yc + _cols(b, g)
        s = _sig(yc)
        dyc_g = _cols(dout, g) * (s * (1.0 + yc * (1.0 - s)))
        dycs.append(dyc_g)
        for k in range(4):
            dwl[k].append(_rows(dyc_g * shifted[k]))
    dyc = jnp.concatenate(dycs, axis=1)
    return dyc, [jnp.concatenate(l, axis=1) for l in dwl], _rows(dyc)


def ssd_prep_fn(i, n, xp, hx, bcp, hbc, dtblk, cwx, cwbc, cbx, cbbc, alog_c, dtb_c, e16):
    dt, a_neg, _, _ = _ssd_gates(dtblk, alog_c, dtb_c)
    acs = _chunk_cumsum(dt * a_neg, SSM_L)
    return (_silu_conv_cols(xp, hx, cwx, cbx, i), _silu_conv_cols(bcp, hbc, cwbc, cbbc, i), mm_sel(dt, e16),
            mm_sel(acs, e16), jnp.transpose(acs)[0:SSM_H, :])


def ssd_prep_bwd_fn(i, n, xp, hx, bcp, hbc, dtblk, dxs_a, dxs_b, db, dc, dgate, dacs_t, cwx, cwbc, cbx, cbbc, alog_c, dtb_c):
    dyx, dwx, dbx = _silu_conv_bwd_cols(xp, hx, cwx, cbx, dxs_a + dxs_b, i)
    dybc, dwbc, dbbc = _silu_conv_bwd_cols(bcp, hbc, cwbc, cbbc, jnp.concatenate([db, dc], axis=1), i)
    dt, a_neg, z, hmask = _ssd_gates(dtblk, alog_c, dtb_c)
    g0, g1 = dgate[:, :128], dgate[:, 128:]
    col = _iota(g0.shape, 1)
    lo, mid = col < 8, (col >= 8) & (col < 16)
    dacs_col = jnp.where(lo, g0, 0.0) + pltpu.roll(jnp.where(lo, g1, 0.0), 8, 1)
    ddt_dir = pltpu.roll(jnp.where(mid, g0, 0.0), 120, 1) + jnp.where(mid, g1, 0.0)
    tbn = dtblk.shape[0]
    rowpart = jnp.transpose(jnp.concatenate([dacs_t, jnp.zeros((128 - SSM_H, tbn), F32)], axis=0))
    da = _chunk_revcumsum(dacs_col - rowpart, SSM_L)
    draw = jnp.where(hmask, (ddt_dir + da * a_neg) * _sig(z), 0.0)
    return (dyx, dybc, draw, *dwx, *dwbc, dbx, dbbc, _rows(da * dt * a_neg), _rows(draw))


def _ssd_gate(y, xs, zs, d_x):
    y2 = y + xs * d_x
    s = _sig(zs)
    return y2, s, y2 * (zs * s)


def ssd_post_fn(i, n, y, xs, zs, d_x, nw):
    _, _, yg = _ssd_gate(y, xs, zs, d_x)
    outs = []
    for g in range(2):
        v = yg[:, 512 * g:512 * g + 512]
        outs.append(v * lax.rsqrt(jnp.mean(v * v, axis=-1, keepdims=True) + EPS))
    return (jnp.concatenate(outs, axis=1) * nw,)


def ssd_post_bwd_fn(i, n, y, xs, zs, dob, d_x, nw):
    y2, s, yg = _ssd_gate(y, xs, zs, d_x)
    gfull = dob * nw
    dygs, dnw = [], []
    for g in range(2):
        sl = slice(512 * g, 512 * g + 512)
        v, gg = yg[:, sl], gfull[:, sl]
        r = lax.rsqrt(jnp.mean(v * v, axis=-1, keepdims=True) + EPS)
        dygs.append(r * gg - v * (r * r * r) * jnp.mean(v * gg, axis=-1, keepdims=True))
        dnw.append(_rows(dob[:, sl] * v * r))
    dyg = jnp.concatenate(dygs, axis=1)
    dy2 = dyg * (zs * s)
    dzs = dyg * y2 * (s * (1.0 + zs * (1.0 - s)))
    return dy2, dy2 * d_x, dzs, jnp.concatenate(dnw, axis=1), _rows(dy2 * xs)


def _attn_probs(q, k):
    hs = [slice(MEM_DH * h, MEM_DH * h + MEM_DH) for h in range(MEM_H)]
    ss = [mm_nt(q[:, sl], k[:, sl]) * (MEM_DH ** -0.5) for sl in hs]
    es = [jnp.exp(s - jnp.max(s, axis=-1, keepdims=True)) for s in ss]
    return hs, [e / jnp.sum(e, axis=-1, keepdims=True) for e in es]


def attn_fn(i, n, q, k, v):
    hs, ps = _attn_probs(q, k)
    return (jnp.concatenate([mm(p, v[:, sl]) for p, sl in zip(ps, hs)], axis=1),)


def attn_bwd_fn(i, n, q, do, k, v):
    hs, ps = _attn_probs(q, k)
    dvs = [mm_tn(p, do[:, sl]) for p, sl in zip(ps, hs)]
    dps = [mm_nt(do[:, sl], v[:, sl]) for sl in hs]
    dss = [p * (dp - jnp.sum(dp * p, axis=-1, keepdims=True)) * (MEM_DH ** -0.5) for p, dp in zip(ps, dps)]
    dqs = [mm(ds, k[:, sl]) for ds, sl in zip(dss, hs)]
    dks = [mm_tn(ds, q[:, sl]) for ds, sl in zip(dss, hs)]
    return jnp.concatenate(dqs, axis=1), jnp.concatenate(dks, axis=1), jnp.concatenate(dvs, axis=1)


def add2_fn(i, n, sp, a, b):
    return (a + b,)


def sum4_fn(i, n, sp, a, b, c, d):
    return (((a.astype(F32) + b.astype(F32)) + c.astype(F32)) + d.astype(F32),)


def _adamw(w, g, m, v):
    m = ADAM_B1 * m + (1.0 - ADAM_B1) * g
    v = ADAM_B2 * v + (1.0 - ADAM_B2) * (g * g)
    m_hat = m / (1.0 - ADAM_B1 ** ADAM_STEP)
    v_hat = v / (1.0 - ADAM_B2 ** ADAM_STEP)
    delta = -ADAM_LR * (m_hat / (jnp.sqrt(v_hat) + ADAM_EPS) + ADAM_WD * w)
    return delta, m, v


def _gate_cols(gb, h):
    lane = _iota(gb.shape, 1)
    return _lanes(jnp.where(lane == h, gb, 0.0)), _lanes(jnp.where(lane == 8 + h, gb, 0.0))


def _gdn_stage1(q, k, v, bb, gcs, grow):
    C = GDN_C
    row, col = _iota((C, C), 0), _iota((C, C), 1)
    incl, strict = row >= col, row > col
    dmat = jnp.where(incl, jnp.exp(jnp.minimum((gcs if gcs.shape[1] == 1 else gcs[:, :C]) - grow, 0.0)), 0.0)
    gam = jnp.exp(gcs)
    gl = gcs[C - 1:C, :]
    kb, vb = k * bb, v * bb
    kg = kb * gam
    lmat = jnp.where(strict, mm_nt(kb, k) * dmat, 0.0)
    pmat = jnp.where(incl, mm_nt(q, k) * dmat, 0.0)
    return dict(q=q, k=k, v=v, bb=bb, incl=incl, strict=strict, dmat=dmat, gam=gam, kb=kb, vb=vb, kg=kg,
                lmat=lmat, pmat=pmat, qd=q * gam, kdec=jnp.exp(gl - gcs), cd=jnp.exp(gl))


def _gdn_inverse(lmats):
    C = GDN_C
    eye = (_iota((C, C), 0) == _iota((C, C), 1)).astype(F32)
    xs = [-l for l in lmats]
    ts = [eye + x for x in xs]
    for _ in range(5):
        xs = [mm(x, x) for x in xs]
        ts = [t + mm(t, x) for t, x in zip(ts, xs)]
    res = [eye - mm3(eye + l, t) for l, t in zip(lmats, ts)]
    return [t + mm(t, r) for t, r in zip(ts, res)]


def gdn_fwd(qn, kn, v, gcs_x, beta_x, gcs_t, tb, gh):
    T = qn.shape[0]
    nb, ncb, nc, C = T // tb, tb // GDN_C, T // GDN_C, GDN_C
    idx = [(hh, c) for hh in range(gh) for c in range(ncb)]

    def body(q_ref, k_ref, v_ref, g_ref, b_ref, gt_ref, o_ref, st_ref, ti_ref, s_scr):
        @pl.when(pl.program_id(1) == 0)
        def _():
            s_scr[...] = jnp.zeros_like(s_scr)

        grows = [gt_ref[hh] for hh in range(gh)]
        at = lambda hh, c: (slice(C * c, C * (c + 1)), slice(128 * hh, 128 * hh + 128))
        st1 = []
        for hh, c in idx:
            sl, ln = at(hh, c)
            st1.append(_gdn_stage1(q_ref[sl, ln], k_ref[sl, ln], v_ref[sl, ln], b_ref[sl, ln], g_ref[sl, ln],
                                   grows[hh][:, sl]))
        tinvs = _gdn_inverse([s["lmat"] for s in st1])
        us = [mm(t, s["vb"]) for t, s in zip(tinvs, st1)]
        ws = [mm(t, s["kg"]) for t, s in zip(tinvs, st1)]
        kds = [s["k"] * s["kdec"] for s in st1]
        ms = [mm_tn(kd, w) for kd, w in zip(kds, ws)]
        bs = [mm_tn(kd, u) for kd, u in zip(kds, us)]
        gs = [s["qd"] - mm(s["pmat"], w) for s, w in zip(st1, ws)]
        pus = [mm(s["pmat"], u) for s, u in zip(st1, us)]
        ss = [s_scr[hh] for hh in range(gh)]
        for c in range(ncb):
            for hh in range(gh):
                n, (sl, ln) = hh * ncb + c, at(hh, c)
                ti_ref[hh, sl, :] = tinvs[n]
                st_ref[hh, c] = ss[hh]
                o_ref[sl, ln] = mm(gs[n], ss[hh]) + pus[n]
                ss[hh] = st1[n]["cd"] * ss[hh] - mm(ms[n], ss[hh]) + bs[n]
        for hh in range(gh):
            s_scr[hh] = ss[hh]

    blk = pl.BlockSpec((tb, 128 * gh), lambda h, i: (i, h))
    return pl.pallas_call(
        body, name="gdn_fwd", grid=(GDN_H // gh, nb),
        in_specs=[blk] * 5 + [pl.BlockSpec((gh, 1, tb), lambda h, i: (h, 0, i))],
        out_specs=[blk, pl.BlockSpec((gh, ncb, 128, 128), lambda h, i: (h, i, 0, 0)),
                   pl.BlockSpec((gh, tb, C), lambda h, i: (h, i, 0))],
        out_shape=[jax.ShapeDtypeStruct((T, D), F32), jax.ShapeDtypeStruct((GDN_H, nc, 128, 128), F32),
                   jax.ShapeDtypeStruct((GDN_H, T, C), F32)],
        scratch_shapes=[pltpu.VMEM((gh, 128, 128), F32)],
        compiler_params=_cparams(("parallel", "arbitrary")),
    )(qn, kn, v, gcs_x, beta_x, gcs_t)


def gdn_bwd(qn, kn, v, gb, gcs_t, do, states, tinv, tb, gh):
    T = qn.shape[0]
    nb, ncb, C = T // tb, tb // GDN_C, GDN_C
    assert gh == GDN_H

    def body(q_ref, k_ref, v_ref, gb_ref, gt_ref, do_ref, st_ref, ti_ref,
             dq_ref, dk_ref, dv_ref, dgb_ref, dgr_ref, ds_scr):
        @pl.when(pl.program_id(1) == 0)
        def _():
            ds_scr[...] = jnp.zeros_like(ds_scr)

        grows = [gt_ref[hh] for hh in range(gh)]
        at = lambda hh, c: (slice(C * c, C * (c + 1)), slice(128 * hh, 128 * hh + 128))
        lastrow = _iota((C, 1), 0) == C - 1
        lane = _iota((C, 128), 1)
        idx = [(hh, c) for hh in range(gh) for c in range(ncb)]
        P = []
        for hh, c in idx:
            sl, ln = at(hh, c)
            lc = _gdn_stage1(q_ref[sl, ln], k_ref[sl, ln], v_ref[sl, ln], *_gate_cols(gb_ref[sl, :], hh), grows[hh][:, sl])
            lc.update(tinv=ti_ref[hh, sl, :], s=st_ref[hh, c], do=do_ref[sl, ln], kd=lc["k"] * lc["kdec"])
            P.append(lc)
        for l, u, w in zip(P, [mm(l["tinv"], l["vb"]) for l in P], [mm(l["tinv"], l["kg"]) for l in P]):
            l.update(u=u, w=w)
        for l, x in zip(P, [mm(l["w"], l["s"]) for l in P]):
            l["vn"] = l["u"] - x
        for l, a, b, c_, d in zip(P, [mm_nt(l["do"], l["s"]) for l in P], [mm_nt(l["do"], l["vn"]) for l in P],
                                  [mm_tn(l["qd"], l["do"]) for l in P], [mm_tn(l["pmat"], l["do"]) for l in P]):
            l.update(dqd=a, dp=jnp.where(l["incl"], b, 0.0), ds_q=c_, dvn_p=d)
        pre = dict(zip(idx, P))
        rows = {}
        hs = range(gh)
        ds = [ds_scr[hh] for hh in hs]
        for c in reversed(range(ncb)):
            L = [pre[hh, c] for hh in hs]
            dvn = [l["dvn_p"] + mm(l["kd"], d) for l, d in zip(L, ds)]
            dkd = [mm_nt(l["vn"], d) for l, d in zip(L, ds)]
            dcd = [_sum_all(l["s"] * d) for l, d in zip(L, ds)]
            ds = [l["ds_q"] + l["cd"] * d - mm_tn(l["w"], x) for l, d, x in zip(L, ds, dvn)]
            dw = [-mm_nt(x, l["s"]) for l, x in zip(L, dvn)]
            dvb = [mm_tn(l["tinv"], x) for l, x in zip(L, dvn)]
            dkg = [mm_tn(l["tinv"], x) for l, x in zip(L, dw)]
            da = [-jnp.where(l["strict"], mm_nt(a, l["u"]) + mm_nt(b, l["w"]), 0.0) for l, a, b in zip(L, dvb, dkg)]
            dm = [a * l["dmat"] for l, a in zip(L, da)]
            dn = [l["dp"] * l["dmat"] for l in L]
            dkb = [mm(a, l["k"]) for l, a in zip(L, dm)]
            dq = [mm(a, l["k"]) + l["gam"] * l["dqd"] for l, a in zip(L, dn)]
            dk = [mm_tn(a, l["kb"]) + mm_tn(b, l["q"]) for l, a, b in zip(L, dm, dn)]
            dgb = jnp.zeros((C, 128), F32)
            for hh in hs:
                sl, ln = at(hh, c)
                l = L[hh]
                e = da[hh] * l["lmat"] + l["dp"] * l["pmat"]
                t_kd = _lanes(dkd[hh] * l["kd"])
                dgl = _sum_all(t_kd) + dcd[hh] * l["cd"][:, :1]
                dgcs = (_lanes(e) + _lanes(l["dqd"] * l["qd"]) - t_kd + _lanes(dkg[hh] * l["kg"])
                        + jnp.where(lastrow, dgl, 0.0))
                rows[hh, c] = _rows(e)
                dq_ref[sl, ln] = dq[hh]
                dk_ref[sl, ln] = (dk[hh] + l["kdec"] * dkd[hh] + l["bb"] * l["gam"] * dkg[hh] + l["bb"] * dkb[hh])
                dv_ref[sl, ln] = l["bb"] * dvb[hh]
                dbeta = _lanes(dkg[hh] * l["gam"] * l["k"]) + _lanes(dvb[hh] * l["v"]) + _lanes(dkb[hh] * l["k"])
                dgb = dgb + jnp.where(lane == hh, dbeta, 0.0) + jnp.where(lane == 8 + hh, dgcs, 0.0)
            dgb_ref[slice(C * c, C * (c + 1)), :] = dgb
        for hh in hs:
            ds_scr[hh] = ds[hh]
            dgr_ref[hh] = jnp.concatenate([rows[hh, c] for c in range(ncb)], axis=1)

    blk = pl.BlockSpec((tb, 128 * gh), lambda h, i: (nb - 1 - i, h))
    rowspec = pl.BlockSpec((gh, 1, tb), lambda h, i: (h, 0, nb - 1 - i))
    cblk = pl.BlockSpec((tb, 128), lambda h, i: (nb - 1 - i, 0))
    return pl.pallas_call(
        body, name="gdn_bwd", grid=(GDN_H // gh, nb),
        in_specs=[blk] * 3 + [cblk, rowspec, blk,
                              pl.BlockSpec((gh, ncb, 128, 128), lambda h, i: (h, nb - 1 - i, 0, 0)),
                              pl.BlockSpec((gh, tb, C), lambda h, i: (h, nb - 1 - i, 0))],
        out_specs=[blk] * 3 + [cblk, rowspec],
        out_shape=[jax.ShapeDtypeStruct((T, D), F32)] * 3 + [jax.ShapeDtypeStruct((T, 128), F32),
                                                             jax.ShapeDtypeStruct((GDN_H, 1, T), F32)],
        scratch_shapes=[pltpu.VMEM((gh, 128, 128), F32)],
        compiler_params=_cparams(("parallel", "arbitrary")),
    )(qn, kn, v, gb, gcs_t, do, states, tinv)


def _ssd_pair(x2, dt2, acs2):
    last = acs2[SSM_L - 1:SSM_L, :]
    return jnp.exp(acs2), jnp.exp(last - acs2), x2 * dt2


def _ssd_head(hh, acs2, arow, dec2, cbm, bm, incl, col):
    lmask = (col >= 64 * hh) & (col < 64 * hh + 64)
    sg = jnp.where(incl, jnp.exp(jnp.minimum(acs2[:, 64 * hh:64 * hh + 1] - arow, 0.0)), 0.0)
    dec_col = dec2[:, 64 * hh:64 * hh + 1]
    return lmask, sg, sg * cbm, dec_col, bm * dec_col


def ssd_fwd(xs, bc, dt_x, acs_x, acs_t):
    T = xs.shape[0]
    nc, L = T // SSM_L, SSM_L

    def body(x_ref, bc_ref, dt_ref, ac_ref, at_ref, y_ref, hst_ref, h_scr):
        @pl.when(pl.program_id(0) == 0)
        def _():
            h_scr[...] = jnp.zeros_like(h_scr)

        row, col = _iota((L, L), 0), _iota((L, L), 1)
        incl = row >= col
        P, H = [], []
        for gp in range(8):
            g = gp // 4
            bm, cm = bc_ref[:, 128 * g:128 * g + 128], bc_ref[:, 256 + 128 * g:384 + 128 * g]
            cbm = mm_nt(cm, bm) if gp % 4 == 0 else cbm
            sl = slice(128 * gp, 128 * gp + 128)
            acs2 = ac_ref[:, sl]
            lam2, dec2, xd2 = _ssd_pair(x_ref[:, sl], dt_ref[:, sl], acs2)
            P.append(dict(sl=sl, lam2=lam2, xd2=xd2, hprev=h_scr[gp], cm=cm))
            for hh in range(2):
                lmask, _, mmat, _, bd = _ssd_head(hh, acs2, at_ref[2 * gp + hh], dec2, cbm, bm, incl, col)
                H.append(dict(mmat=mmat, bd=bd, xdh=jnp.where(lmask, xd2, 0.0), xd2=xd2))
        ys = [mm(h["mmat"], h["xdh"]) for h in H]
        sts = [mm_tn(h["xd2"], h["bd"]) for h in H]
        zs = [mm_nt(p["cm"], p["hprev"]) for p in P]
        for gp, p in enumerate(P):
            hst_ref[gp // 4, gp % 4] = p["hprev"]
            y_ref[:, p["sl"]] = ys[2 * gp] + ys[2 * gp + 1] + p["lam2"] * zs[gp]
            lam_rows = jnp.where(row < 64, p["lam2"][L - 1:L, 0:1], p["lam2"][L - 1:L, 64:65])
            h_scr[gp] = lam_rows * p["hprev"] + jnp.where(row < 64, sts[2 * gp], sts[2 * gp + 1])

    blk = pl.BlockSpec((L, D), lambda c: (c, 0))
    return pl.pallas_call(
        body, name="ssd_fwd", grid=(nc,),
        in_specs=[blk, pl.BlockSpec((L, 512), lambda c: (c, 0)), blk, blk, pl.BlockSpec((SSM_H, 1, L), lambda c: (0, 0, c))],
        out_specs=[blk, pl.BlockSpec((2, None, 4, 128, 128), lambda c: (0, c, 0, 0, 0))],
        out_shape=[jax.ShapeDtypeStruct((T, D), F32), jax.ShapeDtypeStruct((2, nc, 4, 128, 128), F32)],
        scratch_shapes=[pltpu.VMEM((8, 128, 128), F32)],
        compiler_params=_cparams(("arbitrary",)),
    )(xs, bc, dt_x, acs_x, acs_t)


def ssd_bwd(xs, bc, dt_x, acs_x, acs_t, dy, hstates):
    T = xs.shape[0]
    nc, L = T // SSM_L, SSM_L

    def body(x_ref, bc_ref, dt_ref, ac_ref, at_ref, dy_ref, hst_ref,
             dx_ref, db_ref, dc_ref, dgate_ref, dar_ref, dh_scr):
        @pl.when(pl.program_id(0) == 0)
        def _():
            dh_scr[...] = jnp.zeros_like(dh_scr)

        row, col = _iota((L, L), 0), _iota((L, L), 1)
        rowc = _iota((L, 1), 0)
        incl = row >= col
        G = [dict(bm=bc_ref[:, 128 * g:128 * g + 128], cm=bc_ref[:, 256 + 128 * g:384 + 128 * g]) for g in range(2)]
        for gr in G:
            gr["cbm"] = mm_nt(gr["cm"], gr["bm"])
        P = []
        for gp in range(8):
            sl = slice(128 * gp, 128 * gp + 128)
            x2, dt2, dy2, acs2 = x_ref[:, sl], dt_ref[:, sl], dy_ref[:, sl], ac_ref[:, sl]
            lam2, dec2, xd2 = _ssd_pair(x2, dt2, acs2)
            P.append(dict(sl=sl, gr=G[gp // 4], x2=x2, dt2=dt2, dy2=dy2, acs2=acs2, lam2=lam2, dec2=dec2, xd2=xd2,
                          hprev=hst_ref[gp // 4, gp % 4], dhn=dh_scr[gp], dz=lam2 * dy2))
        zs = [mm_nt(p["gr"]["cm"], p["hprev"]) for p in P]
        dcm_t = [mm(p["dz"], p["hprev"]) for p in P]
        dh_z = [mm_tn(p["dz"], p["gr"]["cm"]) for p in P]
        H = []
        for gp, p in enumerate(P):
            p["yoff"] = p["dz"] * zs[gp]
            p["q_rows"] = _lanes(p["dhn"] * p["hprev"])
            for hh in range(2):
                lmask, sg, mmat, dec_col, bd = _ssd_head(hh, p["acs2"], at_ref[2 * gp + hh], p["dec2"], p["gr"]["cbm"],
                                                         p["gr"]["bm"], incl, col)
                H.append(dict(p=p, hh=hh, j=2 * gp + hh, lmask=lmask, sg=sg, mmat=mmat, dec_col=dec_col, bd=bd))
        dms = [mm_nt(jnp.where(h["lmask"], h["p"]["dy2"], 0.0), h["p"]["xd2"]) for h in H]
        a1s = [mm_tn(h["mmat"], h["p"]["dy2"]) for h in H]
        a2s = [mm_nt(h["bd"], h["p"]["dhn"]) for h in H]
        dbds = [mm(jnp.where(h["lmask"], h["p"]["xd2"], 0.0), h["p"]["dhn"]) for h in H]
        for gr in G:
            gr.update(dcb=jnp.zeros((L, L), F32), dbm=jnp.zeros((L, SSM_N), F32), comp=jnp.zeros((L, 128), F32))
        dxd = [jnp.zeros((L, 128), F32) for _ in P]
        for h, dm_raw, a1, a2, dbd in zip(H, dms, a1s, a2s, dbds):
            p, hh, j = h["p"], h["hh"], h["j"]
            gr, jg = p["gr"], j % 8
            dm = jnp.where(incl, dm_raw, 0.0)
            gr["dcb"] = gr["dcb"] + dm * h["sg"]
            e = dm * h["mmat"]
            dxd_h = jnp.where(h["lmask"], a1 + a2, 0.0)
            dxd[j // 2] = dxd[j // 2] + dxd_h
            gr["dbm"] = gr["dbm"] + h["dec_col"] * dbd
            t = _lanes(dbd * h["bd"])
            lam_h = p["lam2"][L - 1:L, 64 * hh:64 * hh + 1]
            in_head = (rowc >= 64 * hh) & (rowc < 64 * hh + 64)
            add_last = _sum_all(t) + _sum_all(jnp.where(in_head, p["q_rows"], 0.0)) * lam_h
            dacs_col = (_lanes(jnp.where(h["lmask"], p["yoff"], 0.0)) + _lanes(e) - t
                        + jnp.where(rowc == L - 1, add_last, 0.0))
            ddt_col = _lanes(dxd_h * p["x2"])
            dar_ref[j] = _rows(e)
            gr["comp"] = gr["comp"] + jnp.where(col == jg, dacs_col, 0.0) + jnp.where(col == 8 + jg, ddt_col, 0.0)
        for gp, p in enumerate(P):
            lam_rows = jnp.where(row < 64, p["lam2"][L - 1:L, 0:1], p["lam2"][L - 1:L, 64:65])
            dh_scr[gp] = dh_z[gp] + lam_rows * p["dhn"]
            dx_ref[:, p["sl"]] = p["dt2"] * dxd[gp]
        for g, gr in enumerate(G):
            lanes = slice(128 * g, 128 * g + 128)
            dcm = (dcm_t[4 * g] + dcm_t[4 * g + 1]) + (dcm_t[4 * g + 2] + dcm_t[4 * g + 3])
            db_ref[:, lanes] = gr["dbm"] + mm_tn(gr["dcb"], gr["cm"])
            dc_ref[:, lanes] = dcm + mm(gr["dcb"], gr["bm"])
            dgate_ref[:, lanes] = gr["comp"]

    rv = lambda c: (nc - 1 - c, 0)
    blk, blk256 = pl.BlockSpec((L, D), rv), pl.BlockSpec((L, 256), rv)
    rowspec = pl.BlockSpec((SSM_H, 1, L), lambda c: (0, 0, nc - 1 - c))
    return pl.pallas_call(
        body, name="ssd_bwd", grid=(nc,),
        in_specs=[blk, pl.BlockSpec((L, 512), rv), blk, blk, rowspec, blk,
                  pl.BlockSpec((2, None, 4, 128, 128), lambda c: (0, nc - 1 - c, 0, 0, 0))],
        out_specs=[blk, blk256, blk256, blk256, rowspec],
        out_shape=[jax.ShapeDtypeStruct((T, D), F32), jax.ShapeDtypeStruct((T, 256), F32),
                   jax.ShapeDtypeStruct((T, 256), F32), jax.ShapeDtypeStruct((T, 256), F32),
                   jax.ShapeDtypeStruct((SSM_H, 1, T), F32)],
        scratch_shapes=[pltpu.VMEM((8, 128, 128), F32)],
        compiler_params=_cparams(("arbitrary",)),
    )(xs, bc, dt_x, acs_x, acs_t, dy, hstates)


def _pos():
    return lax.axis_index("x"), lax.axis_index("y"), lax.axis_index("c")


def _other_chips(x, y):
    return [(1 - x, y), (x, 1 - y), (1 - x, 1 - y)]


def _rcopy(src, dst, ssem, rsem, dev):
    return pltpu.make_async_remote_copy(src_ref=src, dst_ref=dst, send_sem=ssem, recv_sem=rsem,
                                        device_id=dev, device_id_type=MESH)


def _rows_at(start, n):
    return pl.ds(pl.multiple_of(start, 8), n)


def _comm_call(body, name, out_shape, n_in, scratch):
    return pl.pallas_call(
        body, name=name, out_shape=out_shape, in_specs=[ANY] * n_in,
        out_specs=[ANY] * len(out_shape) if isinstance(out_shape, (list, tuple)) else ANY,
        scratch_shapes=scratch,
        compiler_params=pltpu.CompilerParams(has_side_effects=True),
    )


def _dma_sems(n):
    return pltpu.SemaphoreType.DMA((n,))


def ag_chips(name, shard):
    rr, cc = shard.shape
    h, nq = rr // 2, ICI_CHUNKS
    hq = h // nq

    def body(x_ref, out_ref, ssem, rsem):
        x, y, c = _pos()
        me_s = 2 * x + y
        chips = _other_chips(x, y)
        started = []
        for q in range(nq):
            rows = _rows_at(c * h + q * hq, hq)
            for j, (cx, cy) in enumerate(chips):
                cp = _rcopy(x_ref.at[rows], out_ref.at[me_s, rows], ssem.at[j * nq + q], rsem.at[j * nq + q], (cx, cy, c))
                cp.start()
                started.append(cp)
        for q in range(nq):
            rows = _rows_at(c * h + q * hq, hq)
            for j, (cx, cy) in enumerate(chips):
                blk = out_ref.at[2 * cx + cy, rows]
                _rcopy(blk, blk, ssem.at[j * nq + q], rsem.at[j * nq + q], (cx, cy, c)).wait_recv()
                k = 3 * nq + j * nq + q
                cp = _rcopy(blk, blk, ssem.at[k], rsem.at[k], (x, y, 1 - c))
                cp.start()
                started.append(cp)
        for q in range(nq):
            rows = _rows_at((1 - c) * h + q * hq, hq)
            for j, (cx, cy) in enumerate(chips):
                blk = out_ref.at[2 * cx + cy, rows]
                k = 3 * nq + j * nq + q
                _rcopy(blk, blk, ssem.at[k], rsem.at[k], (x, y, 1 - c)).wait_recv()
        for cp in started:
            cp.wait_send()

    return _comm_call(body, name, jax.ShapeDtypeStruct((4, rr, cc), shard.dtype), 1,
                      [_dma_sems(6 * nq), _dma_sems(6 * nq)])(shard)


def _with_own(shard, got, s_me):
    return lax.dynamic_update_index_in_dim(got, shard, s_me, 0)


def all_gather_chips(name, shard, s_me):
    return _with_own(shard, ag_chips(name, shard), s_me)


HBM_SPEC = pl.BlockSpec(memory_space=pltpu.HBM)
SEM_SPEC = pl.BlockSpec(memory_space=pltpu.SEMAPHORE)
SPLIT_EFFECT = pltpu.SideEffectType.DATAFLOW_SIDE_EFFECTING


def _split_copies(pieces, x_ref, land_ref, sems, arriving):
    x, y, c = _pos()
    return [_rcopy(s, d_in if arriving else d_out, sems[j], sems[3 + j], dev)
            for j, (s, d_out, d_in, dev) in enumerate(pieces(x_ref, land_ref, x, y, c))]


def split_copy_start(name, src, land_shape, pieces, after):
    def body(x_ref, land_ref, after_ref, *outs):
        for cp in _split_copies(pieces, x_ref, land_ref, outs[:6], False):
            cp.start()
        outs[8][...] = jnp.zeros_like(outs[8])

    dma = pltpu.SemaphoreType.DMA(())
    res = pl.pallas_call(
        body, name=name,
        out_shape=(dma,) * 6 + (pltpu.HBM(src.shape, src.dtype), pltpu.HBM(land_shape, src.dtype),
                                jax.ShapeDtypeStruct((8, 128), F32)),
        in_specs=(HBM_SPEC, HBM_SPEC, ANY),
        out_specs=(SEM_SPEC,) * 6 + (HBM_SPEC, HBM_SPEC, pl.BlockSpec(memory_space=pltpu.VMEM)),
        input_output_aliases={0: 6, 1: 7},
        compiler_params=pltpu.CompilerParams(has_side_effects=SPLIT_EFFECT),
    )(pltpu.with_memory_space_constraint(src, pltpu.HBM),
      pltpu.with_memory_space_constraint(lax.empty(land_shape, src.dtype), pltpu.HBM), after)
    return res[:6], res[6], res[7], res[8]


def split_copy_wait(name, sems, src_thru, land_thru, after, pieces):
    def body(x_ref, land_ref, *rest):
        for cp in _split_copies(pieces, x_ref, land_ref, rest[:6], False):
            cp.wait_send()
        for cp in _split_copies(pieces, x_ref, land_ref, rest[:6], True):
            cp.wait_recv()

    return pl.pallas_call(
        body, name=name,
        out_shape=(pltpu.HBM(src_thru.shape, src_thru.dtype), pltpu.HBM(land_thru.shape, land_thru.dtype)),
        in_specs=(HBM_SPEC, HBM_SPEC) + (SEM_SPEC,) * 6 + (ANY,), out_specs=(HBM_SPEC, HBM_SPEC),
        input_output_aliases={0: 0, 1: 1},
        compiler_params=pltpu.CompilerParams(has_side_effects=SPLIT_EFFECT),
    )(src_thru, land_thru, *sems, after)


def ag_pieces(h):
    def pieces(x_ref, land_ref, x, y, c):
        rows = _rows_at(c * h, h)
        return [(x_ref.at[rows], land_ref.at[2 * x + y, rows], land_ref.at[2 * cx + cy, rows], (cx, cy, c))
                for cx, cy in _other_chips(x, y)]
    return pieces


def rs_pieces(x_ref, land_ref, x, y, c):
    return [(x_ref.at[2 * cx + cy], land_ref.at[j], land_ref.at[j], (cx, cy, c))
            for j, (cx, cy) in enumerate(_other_chips(x, y))]


def ag_forward(name, got):
    _, rr, cc = got.shape
    h, nq = rr // 2, D2D_CHUNKS
    hq = h // nq

    def body(g_ref, out_ref, ssem, rsem):
        x, y, c = _pos()
        slots = [2 * cx + cy for cx, cy in _other_chips(x, y)]
        cps = []
        for j, s in enumerate(slots):
            for q in range(nq):
                blk = out_ref.at[s, _rows_at(c * h + q * hq, hq)]
                cp = _rcopy(blk, blk, ssem.at[j * nq + q], rsem.at[j * nq + q], (x, y, 1 - c))
                cp.start()
                cps.append(cp)
        for cp in cps:
            cp.wait_send()
        for j, s in enumerate(slots):
            for q in range(nq):
                blk = out_ref.at[s, _rows_at((1 - c) * h + q * hq, hq)]
                _rcopy(blk, blk, ssem.at[j * nq + q], rsem.at[j * nq + q], (x, y, 1 - c)).wait_recv()

    return pl.pallas_call(
        body, name=name, out_shape=jax.ShapeDtypeStruct(got.shape, got.dtype), in_specs=[ANY], out_specs=ANY,
        scratch_shapes=[_dma_sems(3 * nq), _dma_sems(3 * nq)], input_output_aliases={0: 0},
        compiler_params=pltpu.CompilerParams(has_side_effects=True),
    )(got)


def rs_pair(name, g):
    _, rr, cc = g.shape
    h, nq = rr // 2, D2D_CHUNKS
    hq = h // nq

    def body(g_ref, recv_ref, ssem, rsem):
        x, y, c = _pos()
        cps = []
        for q in range(nq):
            cp = _rcopy(g_ref.at[:, _rows_at((1 - c) * h + q * hq, hq), :], recv_ref.at[:, pl.ds(q * hq, hq), :],
                        ssem.at[q], rsem.at[q], (x, y, 1 - c))
            cp.start()
            cps.append(cp)
        for cp in cps:
            cp.wait()

    return _comm_call(body, name, jax.ShapeDtypeStruct((4, h, cc), g.dtype), 1, [_dma_sems(nq), _dma_sems(nq)])(g)


def rs_chips(name, p):
    _, h, cc = p.shape
    nq = ICI_CHUNKS
    hq = h // nq

    def body(p_ref, buf_ref, ssem, rsem):
        x, y, c = _pos()
        sends = []
        for q in range(nq):
            rows = pl.ds(q * hq, hq)
            for j, (cx, cy) in enumerate(_other_chips(x, y)):
                cp = _rcopy(p_ref.at[2 * cx + cy, rows], buf_ref.at[j, rows], ssem.at[j * nq + q],
                            rsem.at[j * nq + q], (cx, cy, c))
                cp.start()
                sends.append(cp)
        for cp in sends:
            cp.wait()

    return _comm_call(body, name, jax.ShapeDtypeStruct((3, h, cc), p.dtype), 1,
                      [_dma_sems(3 * nq), _dma_sems(3 * nq)])(p)


def rs_join(name, half):
    h, cc = half.shape
    nq = D2D_CHUNKS
    hq = h // nq

    def body(h_ref, out_ref, ssem, rsem):
        x, y, c = _pos()
        cps = []
        for q in range(nq):
            rows = pl.ds(q * hq, hq)
            cp = _rcopy(h_ref.at[rows], out_ref.at[rows], ssem.at[q], rsem.at[q], (x, y, 1 - c))
            cp.start()
            cps.append(cp)
        for cp in cps:
            cp.wait()

    return _comm_call(body, name, jax.ShapeDtypeStruct((h, cc), half.dtype), 1, [_dma_sems(nq), _dma_sems(nq)])(half)


def reduce_scatter(tag, g, tb, sp):
    return rs_end(rs_begin(tag, g, tb, sp, False), None)


def rs_begin(tag, g, tb, sp, split, after=None):
    _, rr, cc = g.shape
    h = rr // 2
    nbh = h // tb
    recv = rs_pair(tag + "_pair", g)
    mine_rows = lambda i, s: (i // nbh) * (2 * nbh) + s[0] * nbh + i % nbh
    part = rowwise(add2_fn, tag + "_add", 4 * h, tb, [R(g.reshape(4 * rr, cc), off=mine_rows), R(recv.reshape(4 * h, cc))],
                   [], [(cc, BF16)], sp=sp)[0].reshape(4, h, cc)
    st = dict(tag=tag, tb=tb, sp=sp, split=split, part=part)
    if split:
        st["sems"], st["part"], st["land"], st["token"] = split_copy_start(tag + "_start", part, (3, h, cc), rs_pieces,
                                                                           sp if after is None else after)
    return st


def rs_end(st, after):
    tag, tb, sp, part = st["tag"], st["tb"], st["sp"], st["part"]
    _, h, cc = part.shape
    nbh = h // tb
    if st["split"]:
        part, buf = split_copy_wait(tag + "_wait", st["sems"], part, st["land"], after, rs_pieces)
    else:
        buf = rs_chips(tag + "_chips", part)
    red = rowwise(sum4_fn, tag + "_sum", h, tb,
                  [R(part.reshape(4 * h, cc), off=lambda i, s: s[1] * nbh + i)]
                  + [R(buf.reshape(3 * h, cc), off=k * nbh) for k in range(3)],
                  [], [(cc, F32)], sp=sp)[0]
    return red, rs_join(tag + "_join", red)


def adam_halves(name, w, m, v, red, other, tb, blk0, sp):
    nbh = red.shape[0] // tb

    def fn(i, n, s, w_, m_, v_, r_, o_):
        g = jnp.where((blk0 + i) // nbh == s[0], r_, o_)
        return (g,) + _adamw(w_, g, m_, v_)

    half_rows = lambda i, s: (blk0 + i) % nbh
    return rowwise(fn, name, w.shape[0], tb, [R(w), R(m), R(v), R(red, off=half_rows), R(other, off=half_rows)],
                   [], [(w.shape[1], F32)] * 4, sp=sp)


SMALL_LANES = 3 * D


def all_reduce_items(name, items, after=None):
    flat = [a for it in items for a in it]
    shapes = [(sum(a.shape[0] for a in it), it[0].shape[1]) for it in items]
    nrows = -(-sum(s[0] for s in shapes) // 8) * 8
    extra = [] if after is None else [after]

    def body(*refs):
        ins, refs = refs[:len(flat)], refs[len(flat) + len(extra):]
        outs = refs[:len(items)]
        mine, buf, ssem, rsem = refs[len(items):]
        x, y, c = _pos()
        me = 4 * x + 2 * y + c
        mine[...] = jnp.zeros_like(mine)
        r = 0
        for ref in ins:
            mine[r:r + ref.shape[0], 0:ref.shape[1]] = ref[...]
            r += ref.shape[0]
        buf[me] = mine[...]
        cps = []
        for k in range(1, 8):
            dev = (x ^ (k >> 2), y ^ ((k >> 1) & 1), c ^ (k & 1))
            cp = _rcopy(mine, buf.at[me], ssem.at[k - 1], rsem.at[k - 1], dev)
            cp.start()
            cps.append(cp)
        for cp in cps:
            cp.wait()
        r = 0
        for (nr, n), out in zip(shapes, outs):
            acc = buf[0, r:r + nr, 0:n]
            for d in range(1, 8):
                acc = acc + buf[d, r:r + nr, 0:n]
            out[...] = acc
            r += nr

    vm = pl.BlockSpec(memory_space=pltpu.VMEM)
    return pl.pallas_call(
        body, name=name, out_shape=[jax.ShapeDtypeStruct(s, F32) for s in shapes],
        in_specs=[vm] * len(flat) + [ANY] * len(extra), out_specs=[vm] * len(items),
        scratch_shapes=[pltpu.VMEM((nrows, SMALL_LANES), F32), pltpu.VMEM((8, nrows, SMALL_LANES), F32),
                        _dma_sems(7), _dma_sems(7)],
        compiler_params=pltpu.CompilerParams(has_side_effects=True),
    )(*flat, *extra)


def adam_small(ws, gs, ms, vs):
    n = len(ws)

    def body(*refs):
        for k in range(n):
            w, g, m, v = (refs[j * n + k][...] for j in range(4))
            for j, val in enumerate(_adamw(w, g, m, v)):
                refs[(4 + j) * n + k][...] = val

    vm = pl.BlockSpec(memory_space=pltpu.VMEM)
    res = pl.pallas_call(
        body, name="adam_small", out_shape=[jax.ShapeDtypeStruct(w.shape, F32) for w in ws] * 3,
        in_specs=[vm] * (4 * n), out_specs=[vm] * (3 * n),
    )(*ws, *gs, *ms, *vs)
    return res[:n], res[n:2 * n], res[2 * n:]


def _sel(rows, cols, pairs):
    m = np.zeros((rows, cols), np.float32)
    for r, c in pairs:
        m[r, c] = 1.0
    return jnp.asarray(m)


def _pad_win(w):
    z = jnp.zeros((w.shape[0], 112), w.dtype)
    return jnp.concatenate([w[:, :4096], w[:, 4112:6672], w[:, 4096:4112], z, w[:, 6672:6688], z], axis=1)


def _unpad_win(wp):
    return jnp.concatenate([wp[:, :4096], wp[:, 6656:6672], wp[:, 4096:6656], wp[:, 6784:6800]], axis=1)


def kernel(x, mem, norm1_w, w_in, gdn_conv_w, gdn_a_log, gdn_dt_bias, gdn_norm_w, ssm_conv_w, ssm_conv_b, ssm_a_log, ssm_dt_bias, ssm_d, ssm_norm_w, w_out, norm2_w, mem_norm_w, wq_mem, wk_mem, wv_mem, wo_mem, norm3_w, w_up, w_down, final_norm_w, loss_target, m_norm1_w, m_w_in, m_gdn_conv_w, m_gdn_a_log, m_gdn_dt_bias, m_gdn_norm_w, m_ssm_conv_w, m_ssm_conv_b, m_ssm_a_log, m_ssm_dt_bias, m_ssm_d, m_ssm_norm_w, m_w_out, m_norm2_w, m_mem_norm_w, m_wq_mem, m_wk_mem, m_wv_mem, m_wo_mem, m_norm3_w, m_w_up, m_w_down, m_final_norm_w, v_norm1_w, v_w_in, v_gdn_conv_w, v_gdn_a_log, v_gdn_dt_bias, v_gdn_norm_w, v_ssm_conv_w, v_ssm_conv_b, v_ssm_a_log, v_ssm_dt_bias, v_ssm_d, v_ssm_norm_w, v_w_out, v_norm2_w, v_mem_norm_w, v_wq_mem, v_wk_mem, v_wv_mem, v_wo_mem, v_norm3_w, v_w_up, v_w_down, v_final_norm_w):
    T, M = x.shape[1], mem.shape[1]
    xi, yi, ci = _pos()
    s_me = 2 * xi + yi
    x0, mem0, tgt = x[0], mem[0], loss_target[0]
    tb = min(512, T)
    tbp = min(256, T)
    row = lambda v: v.reshape(1, -1)

    win_g = all_gather_chips("ag_win", w_in.astype(BF16), s_me)
    w_in_p = _pad_win(win_g.transpose(1, 0, 2).reshape(D, IN_COLS))
    keep = (ci == 0).astype(F32)
    gcw_z = lax.dynamic_update_slice(jnp.zeros((4, 3 * D), F32), gdn_conv_w * keep, (0, s_me * 768))
    scw_z = lax.dynamic_update_slice(jnp.zeros((4, 1536), F32), ssm_conv_w * keep, (0, s_me * 384))
    gcw, scw = all_reduce_items("ar_convw", [[gcw_z], [scw_z]])
    scw_x, scw_bc = scw[:, :D], scw[:, D:]
    rest_shard = jnp.concatenate([w_up, w_down, w_out, wq_mem, wk_mem, wv_mem, wo_mem], axis=0).astype(BF16)
    ag_sems, rest_thru, rest_land, ag_token = split_copy_start("ag_rest_start", rest_shard, (4,) + rest_shard.shape,
                                                               ag_pieces(rest_shard.shape[0] // 2), gcw)
    sp = jnp.stack([ci, s_me]).astype(jnp.int32)
    scb_x, scb_bc = row(ssm_conv_b[:D]), row(ssm_conv_b[D:])

    galog_c, gdtb_c = row(jnp.pad(gdn_a_log, (8, 112))), row(jnp.pad(gdn_dt_bias, (8, 112)))
    salog_c, sdtb_c = row(jnp.pad(ssm_a_log, (0, 112))), row(jnp.pad(ssm_dt_bias, (0, 112)))
    sd_x = row(jnp.repeat(ssm_d, 64))
    eb = _sel(128, D, [(h, 128 * h + l) for h in range(8) for l in range(128)])
    ea = _sel(128, D, [(8 + h, 128 * h + l) for h in range(8) for l in range(128)])
    e16 = _sel(128, D, [(h, 64 * h + l) for h in range(16) for l in range(64)])

    h1 = rowwise(rms_fwd_fn, "rms1", T, tb, [R(x0)], [row(norm1_w) + ag_token[0:1, 0:1]], [(D, BF16)])[0]
    p = matmul("mm_in", h1, w_in_p, "nn", 2048, 768, 1024, [F32])[0]
    gp_ins = [R(p, 3 * D, CB_QKV, "prev"), R(p, 128, CB_BA)]
    qn, kn, vv, gcs_x, beta_x, ggate, gcs_t = rowwise(gdn_prep_fn, "gdn_prep", T, tbp, gp_ins,
                                                      [gcw, galog_c, gdtb_c, eb, ea],
                                                      [(D, F32)] * 5 + [(128, F32), (-8, F32)])
    gcs_t = gcs_t.reshape(GDN_H, 1, T)
    gtb, ggh = min(128, T), 8
    o_gdn, s_states, tinv = gdn_fwd(qn, kn, vv, gcs_x, beta_x, gcs_t, gtb, ggh)
    gnw = row(gdn_norm_w)
    oa = rowwise(gdn_post_fn, "gdn_post", T, tb, [R(o_gdn), R(p, D, CB_Z)], [gnw], [(D, BF16)])[0]
    sp_ins = [R(p, D, CB_XS, "prev"), R(p, 512, CB_BC, "prev"), R(p, 128, CB_DT)]
    sp_full = [scw_x, scw_bc, scb_x, scb_bc, salog_c, sdtb_c]
    xs, bc, dt_x, acs_x, acs_t = rowwise(ssd_prep_fn, "ssd_prep", T, tbp, sp_ins, sp_full + [e16],
                                         [(D, F32), (512, F32), (D, F32), (D, F32), (-SSM_H, F32)])
    acs_t = acs_t.reshape(SSM_H, 1, T)
    y_ssd, h_states = ssd_fwd(xs, bc, dt_x, acs_x, acs_t)
    snw = row(ssm_norm_w)
    ob = rowwise(ssd_post_fn, "ssd_post", T, tb, [R(y_ssd), R(xs), R(p, D, CB_ZS)], [sd_x, snw], [(D, BF16)])[0]
    rest_thru, rest_land = split_copy_wait("ag_rest_wait", ag_sems, rest_thru, rest_land, ob,
                                           ag_pieces(rest_shard.shape[0] // 2))
    rest_g = _with_own(rest_thru, ag_forward("ag_rest_fwd", rest_land), s_me)
    assert D == 1024
    view = lambda shape, blk, at: dict(b_sel=(shape, blk, at))
    wup_n = view((D, D_FF), (None, D, D), lambda i, j, k: (j, 0, 0))
    wup_t = view((D, D_FF), (None, D, D), lambda i, j, k: (k, 0, 0))
    wdown_n = view((D_FF, D), (None, D, D), lambda i, j, k: (k, 1, 0))
    wdown_t = view((D_FF, D), (None, D, D), lambda i, j, k: (j, 1, 0))
    wout_a = view((D, D), (2, 512, D), lambda i, j, k: (0, 4, 0))
    wout_b = view((D, D), (2, 512, D), lambda i, j, k: (1, 4, 0))
    wq_v, wk_v, wv_v, wo_v = (view((D, D), (4, 256, D), lambda i, j, k, r=r: (0, r, 0)) for r in (10, 11, 12, 13))
    x1a = matmul("mm_out_a", oa, rest_g, "nn", 1024, 1024, 1024, [F32], _epi_res, [x0], **wout_a)[0]
    x1, h2 = matmul("mm_out_b", ob, rest_g, "nn", 1024, 1024, 1024, [F32, BF16], _epi_res_rms, [x1a],
                    [row(norm2_w)], **wout_b)

    mn = rowwise(rms_fwd_fn, "rms_mem", M, M, [R(mem0)], [row(mem_norm_w)], [(D, BF16)])[0]
    km = matmul("mm_k", mn, rest_g, "nn", 256, 1024, 1024, [BF16], **wk_v)[0]
    vm = matmul("mm_v", mn, rest_g, "nn", 256, 1024, 1024, [BF16], **wv_v)[0]
    qm = matmul("mm_q", h2, rest_g, "nn", 1024, 1024, 1024, [BF16], **wq_v)[0]
    ao = rowwise(attn_fn, "attn", T, tb, [R(qm)], [km, vm], [(D, BF16)])[0]
    x2, h3 = matmul("mm_o", ao, rest_g, "nn", 1024, 1024, 1024, [F32, BF16], _epi_res_rms, [x1], [row(norm3_w)], **wo_v)
    u, act = matmul("mm_up", h3, rest_g, "nn", 2048, 1024, 1024, [BF16, BF16], _epi_relu2, **wup_n)
    wdown_n2 = view((D_FF, D), (2, D, D), lambda i, j, k: (k, 1, 0))
    x3 = matmul("mm_down", act, rest_g, "nn", 1024, 1024, 2048, [F32], _epi_res, [x2], **wdown_n2)[0]
    dx3, dx3b, loss_lane, g_final = rowwise(final_fn, "final", T, tb, [R(x3), R(tgt)], [row(final_norm_w)],
                                            [(D, F32), (D, BF16)], [(1, D), (1, D)])
    loss = lax.psum(0.5 / D * jnp.sum(loss_lane), ("x", "y", "c"))

    dup = matmul("mm_dact", dx3b, rest_g, "nt", 2048, 1024, 1024, [BF16], _epi_dup, [u], **wdown_t)[0]
    def g_into(buf, blk, at):
        return dict(into=(buf, blk, lambda i, j, k, at=at: at(i, j)))

    grest = jax.ShapeDtypeStruct((4, 3584, D), F32)
    grest = matmul("mm_gdown", act, dx3b, "tn", 1024, 1024, 4096, [F32],
                   **g_into(grest, (None, 1024, D), lambda i, j: (i, 1, 0)))
    wup_t4 = dict(b_sel=((D, D_FF), (4, D, D), lambda i, j, k: (0, 0, 0), "side by side"))
    dh3 = matmul("mm_dh3", dup, rest_g, "nt", 1024, 1024, 4096, [F32], **wup_t4)[0]
    dx2, dx2b, g_n3 = rowwise(rms_bwd_fn, "rms3_bwd", T, tb, [R(x2), R(dh3), R(dx3)], [row(norm3_w)],
                              [(D, F32), (D, BF16)], [(1, D)])
    grest = matmul("mm_gup", h3, dup, "tn", 1024, 1024, 4096, [F32],
                   **g_into(grest, (None, 1024, D), lambda i, j: (j, 0, 0)))
    dao = matmul("mm_dao", dx2b, rest_g, "nt", 1024, 1024, 1024, [F32], **wo_v)[0]
    grest = matmul("mm_gwo", ao, dx2b, "tn", 1024, 1024, 2048, [F32],
                   **g_into(grest, (4, 256, D), lambda i, j: (0, 13, 0)))
    dqm, dkm, dvm = rowwise(attn_bwd_fn, "attn_bwd", T, tb, [R(qm), R(dao)], [km, vm], [(D, BF16)],
                            [(M, D), (M, D)])
    dx1, dx1b, g_n2 = matmul("mm_dh2", dqm, rest_g, "nt", 512, 1024, 1024, [F32, BF16], _epi_rms_bwd, [x1, dx2],
                             [row(norm2_w)], n_acc=1, **wq_v)
    grest = matmul("mm_gwq", h2, dqm, "tn", 1024, 1024, 2048, [F32],
                   **g_into(grest, (4, 256, D), lambda i, j: (0, 10, 0)))
    grest = matmul("mm_gwk", mn, dkm, "tn", 1024, 1024, 256, [F32],
                   **g_into(grest, (4, 256, D), lambda i, j: (0, 11, 0)))
    grest = matmul("mm_gwv", mn, dvm, "tn", 1024, 1024, 256, [F32],
                   **g_into(grest, (4, 256, D), lambda i, j: (0, 12, 0)))
    dmn_k = matmul("mm_dmk", dkm, rest_g, "nt", 256, 1024, 1024, [F32], **wk_v)[0]
    dmn = matmul("mm_dmv", dvm, rest_g, "nt", 256, 1024, 1024, [F32], _epi_res, [dmn_k], **wv_v)[0]
    g_nmem = rowwise(rms_bwd_w_fn, "rmsmem_bwd", M, M, [R(mem0), R(dmn)], [row(mem_norm_w)], [], [(1, D)])[0]
    doa = matmul("mm_doa", dx1b, rest_g, "nt", 2048, 1024, 1024, [F32], **wout_a)[0]
    dob = matmul("mm_dob", dx1b, rest_g, "nt", 2048, 1024, 1024, [F32], **wout_b)[0]
    grest = matmul("mm_gwout_a", oa, dx1b, "tn", 1024, 1024, 2048, [F32],
                   **g_into(grest, (2, 512, D), lambda i, j: (0, 4, 0)))
    grest = matmul("mm_gwout_b", ob, dx1b, "tn", 1024, 1024, 2048, [F32],
                   **g_into(grest, (2, 512, D), lambda i, j: (1, 4, 0)))

    rs_rest = rs_begin("rs_rest", grest, 256, sp, True)

    dp = jax.ShapeDtypeStruct((T, p.shape[1]), BF16)
    dy_ssd, dxs_dir, dp, g_snw, g_sd_lane = rowwise(
        ssd_post_bwd_fn, "ssd_post_bwd", T, tb, [R(y_ssd), R(xs), R(p, D, CB_ZS), R(dob)],
        [sd_x + rs_rest["token"][0:1, 0:1], snw],
        [(D, F32), (D, F32), (D, BF16, dp, CB_ZS)], [(1, D), (1, D)])
    dxs_scan, db_s, dc_s, dgate, dacs_t = ssd_bwd(xs, bc, dt_x, acs_x, acs_t, dy_ssd, h_states)
    spb = rowwise(ssd_prep_bwd_fn, "ssd_prep_bwd", T, tbp,
                  sp_ins + [R(dxs_scan), R(dxs_dir), R(db_s), R(dc_s), R(dgate), RC(dacs_t.reshape(SSM_H, T))], sp_full,
                  [(D, F32), (512, F32), (128, BF16, dp, CB_DT)],
                  [(1, D)] * 4 + [(1, 512)] * 4 + [(1, D), (1, 512), (1, 128), (1, 128)])
    dyc_x, dyc_bc, dp = spb[:3]
    dp = rowwise(conv_bwd_fn, "conv_bwd_x", T, tbp, [R(dyc_x, halo="next")], [scw_x], [(D, BF16, dp, CB_XS)])[0]
    dp = rowwise(conv_bwd_fn, "conv_bwd_bc", T, tbp, [R(dyc_bc, halo="next")], [scw_bc], [(512, BF16, dp, CB_BC)])[0]

    do_gdn, dp, g_gnw = rowwise(gdn_post_bwd_fn, "gdn_post_bwd", T, tb, [R(o_gdn), R(p, D, CB_Z), R(doa)], [gnw],
                                [(D, F32), (D, BF16, dp, CB_Z)], [(1, 128)])
    dqn, dkn, dvv, dggate, dgcs_t = gdn_bwd(qn, kn, vv, ggate, gcs_t, do_gdn, s_states, tinv, gtb, ggh)
    gpb = rowwise(gdn_prep_bwd_fn, "gdn_prep_bwd", T, tbp,
                  gp_ins + [R(dqn), R(dkn), R(dvv), R(dggate), RC(dgcs_t.reshape(GDN_H, T))],
                  [gcw, galog_c, gdtb_c],
                  [(3 * D, F32), (128, BF16, dp, CB_BA)], [(1, 3 * D)] * 4 + [(1, 128), (1, 128)])
    dyc_qkv, dp = gpb[:2]
    dp = rowwise(conv_bwd_fn, "conv_bwd_qkv", T, tbp, [R(dyc_qkv, halo="next")], [gcw], [(3 * D, BF16, dp, CB_QKV)])[0]
    dh1 = matmul("mm_dh1", dp, w_in_p, "nt", 1024, 1024, 2304, [F32])[0]
    grad_x, g_n1 = rowwise(rms_bwd1_fn, "rms1_bwd", T, tb, [R(x0), R(dh1), R(dx1)], [row(norm1_w)], [(D, F32)], [(1, D)])
    g_win_p = matmul("mm_gwin", h1, dp, "tn", 1024, 768, 4096, [F32])[0]

    items = [[g_n1], [gpb[6]], [gpb[7]], [g_gnw], [spb[11]], [spb[12]], [spb[13]], [spb[14]], [g_sd_lane], [g_snw],
             [g_n2], [g_nmem], [g_n3], [g_final], list(gpb[2:6]), list(spb[3:7]), list(spb[7:11])]
    (gr_n1, r_galog, r_gdtb, gr_gnw, r_scb_x, r_scb_bc, r_salog, r_sdtb, r_sd, gr_snw, gr_n2, gr_nmem, gr_n3,
     gr_final, r_gcw, r_scw_x, r_scw_bc) = all_reduce_items("ar_grads", items)
    gr_galog, gr_gdtb = r_galog[:, 8:16], r_gdtb[:, 8:16]
    gr_salog, gr_sdtb = r_salog[:, :SSM_H], r_sdtb[:, :SSM_H]
    gr_sd = r_sd.reshape(SSM_H, SSM_P).sum(axis=1).reshape(1, SSM_H)
    gr_scb = jnp.concatenate([r_scb_x, r_scb_bc], axis=1)
    gr_gcw = lax.dynamic_slice(r_gcw, (0, s_me * 768), (4, 768))
    gr_scw = lax.dynamic_slice(jnp.concatenate([r_scw_x, r_scw_bc], axis=1), (0, s_me * 384), (4, 384))

    g_win = _unpad_win(g_win_p).reshape(D, 4, IN_COLS // 4).transpose(1, 0, 2)
    rs_win = rs_begin("rs_win", g_win, 256, sp, True, gr_n1)
    red_r, oth_r = rs_end(rs_rest, rs_win["token"])

    big = {}
    for n, w, m, v, blk0 in (("w_up", w_up, m_w_up, v_w_up, 0), ("w_down", w_down, m_w_down, v_w_down, 4),
                             ("w_out", w_out, m_w_out, v_w_out, 8), ("wq_mem", wq_mem, m_wq_mem, v_wq_mem, 10),
                             ("wk_mem", wk_mem, m_wk_mem, v_wk_mem, 11), ("wv_mem", wv_mem, m_wv_mem, v_wv_mem, 12),
                             ("wo_mem", wo_mem, m_wo_mem, v_wo_mem, 13)):
        big[n] = adam_halves("adam_" + n, w, m, v, red_r, oth_r, 256, blk0, sp)
    red_w, oth_w = rs_end(rs_win, big["wo_mem"][1])
    big["w_in"] = adam_halves("adam_win", w_in, m_w_in, v_w_in, red_w, oth_w, 256, 0, sp)
    names_s =["norm1_w", "gdn_conv_w", "gdn_a_log", "gdn_dt_bias", "gdn_norm_w", "ssm_conv_w", "ssm_conv_b",
               "ssm_a_log", "ssm_dt_bias", "ssm_d", "ssm_norm_w", "norm2_w", "mem_norm_w", "norm3_w", "final_norm_w"]
    w_s = [norm1_w, gdn_conv_w, gdn_a_log, gdn_dt_bias, gdn_norm_w, ssm_conv_w, ssm_conv_b, ssm_a_log, ssm_dt_bias,
           ssm_d, ssm_norm_w, norm2_w, mem_norm_w, norm3_w, final_norm_w]
    g_s = [gr_n1, gr_gcw, gr_galog, gr_gdtb, gr_gnw, gr_scw, gr_scb, gr_salog, gr_sdtb, gr_sd, gr_snw, gr_n2,
           gr_nmem, gr_n3, gr_final]
    m_s = [m_norm1_w, m_gdn_conv_w, m_gdn_a_log, m_gdn_dt_bias, m_gdn_norm_w, m_ssm_conv_w, m_ssm_conv_b, m_ssm_a_log,
           m_ssm_dt_bias, m_ssm_d, m_ssm_norm_w, m_norm2_w, m_mem_norm_w, m_norm3_w, m_final_norm_w]
    v_s = [v_norm1_w, v_gdn_conv_w, v_gdn_a_log, v_gdn_dt_bias, v_gdn_norm_w, v_ssm_conv_w, v_ssm_conv_b, v_ssm_a_log,
           v_ssm_dt_bias, v_ssm_d, v_ssm_norm_w, v_norm2_w, v_mem_norm_w, v_norm3_w, v_final_norm_w]
    shp_s = [w.shape for w in w_s]
    as2d = lambda a: a if a.ndim == 2 else a.reshape(1, -1)
    d_l, m_l, v_l = adam_small([as2d(a) for a in w_s], [as2d(a) for a in g_s], [as2d(a) for a in m_s],
                               [as2d(a) for a in v_s])

    grads, deltas, new_m, new_v = {}, {}, {}, {}
    for n, (gg, dd, mm_, vv_) in big.items():
        grads[n], deltas[n], new_m[n], new_v[n] = gg, dd, mm_, vv_
    for k, n in enumerate(names_s):
        grads[n] = g_s[k].reshape(shp_s[k])
        deltas[n], new_m[n], new_v[n] = (a[k].reshape(shp_s[k]) for a in (d_l, m_l, v_l))
    order = ["norm1_w", "w_in", "gdn_conv_w", "gdn_a_log", "gdn_dt_bias", "gdn_norm_w", "ssm_conv_w", "ssm_conv_b",
             "ssm_a_log", "ssm_dt_bias", "ssm_d", "ssm_norm_w", "w_out", "norm2_w", "mem_norm_w", "wq_mem", "wk_mem",
             "wv_mem", "wo_mem", "norm3_w", "w_up", "w_down", "final_norm_w"]
    return (loss, grad_x[None], *[grads[n] for n in order], *[deltas[n] for n in order],
            *[new_m[n] for n in order], *[new_v[n] for n in order])
```

```python
import numpy as np
import jax
import jax.numpy as jnp
from jax import lax
from jax.experimental import pallas as pl
from jax.experimental.pallas import tpu as pltpu

F32, BF16 = jnp.float32, jnp.bfloat16
MESH = pl.DeviceIdType.MESH
ANY = pl.BlockSpec(memory_space=pl.ANY)

EPS = 1e-6
D = 1024
GDN_H, GDN_DK, GDN_C = 8, 128, 64
SSM_H, SSM_P, SSM_N, SSM_L = 16, 64, 128, 128
MEM_H, MEM_DH = 4, 256
D_FF = 4096
IN_COLS = 6688
CB_QKV, CB_Z, CB_ZS, CB_XS, CB_BC, CB_BA, CB_DT = 0, 3, 4, 5, 12, 52, 53
VMEM_LIMIT = 56 * 1024 * 1024
D2D_CHUNKS = 8
ICI_CHUNKS = 4

ADAM_LR, ADAM_B1, ADAM_B2, ADAM_EPS, ADAM_WD, ADAM_STEP = 0.001, 0.9, 0.999, 1e-08, 0.01, 10


def _dg(a, b, ca, cb):
    return lax.dot_general(a, b, (((ca,), (cb,)), ((), ())), preferred_element_type=F32)


def _bf(x):
    return x.astype(BF16)


def mm(a, b):
    return _dg(_bf(a), _bf(b), 1, 0)


def mm_nt(a, b):
    return _dg(_bf(a), _bf(b), 1, 1)


def mm_tn(a, b):
    return _dg(_bf(a), _bf(b), 0, 0)


def mm_sel(a, sel):
    hi = a.astype(BF16)
    r1 = a - hi.astype(F32)
    mid = r1.astype(BF16)
    lo = (r1 - mid.astype(F32)).astype(BF16)
    s = sel.astype(BF16)
    return _dg(hi, s, 1, 0) + (_dg(mid, s, 1, 0) + _dg(lo, s, 1, 0))


def mm3(a, b):
    ah, bh = a.astype(BF16), b.astype(BF16)
    al, bl = (a - ah.astype(F32)).astype(BF16), (b - bh.astype(F32)).astype(BF16)
    return _dg(ah, bh, 1, 0) + (_dg(ah, bl, 1, 0) + _dg(al, bh, 1, 0))


def _iota(shape, dim):
    return lax.broadcasted_iota(jnp.int32, shape, dim)


def _chunk_cumsum(x, c):
    pos = _iota(x.shape, 0) & (c - 1)
    s = 1
    while s < c:
        x = x + jnp.where(pos >= s, pltpu.roll(x, s, 0), 0.0)
        s *= 2
    return x


def _chunk_revcumsum(x, c):
    n = x.shape[0]
    pos = _iota(x.shape, 0) & (c - 1)
    s = 1
    while s < c:
        x = x + jnp.where(pos < c - s, pltpu.roll(x, n - s, 0), 0.0)
        s *= 2
    return x


def _sig(x):
    return jax.nn.sigmoid(x)


def _softplus(x):
    return jnp.maximum(x, 0.0) + jnp.log(1.0 + jnp.exp(-jnp.abs(x)))


def _rows(v):
    return jnp.sum(v, axis=0, keepdims=True)


def _lanes(v):
    return jnp.sum(v, axis=1, keepdims=True)


def _sum_all(v):
    return _rows(_lanes(v))


def _cparams(sem):
    return pltpu.CompilerParams(dimension_semantics=sem, vmem_limit_bytes=VMEM_LIMIT)


def rowwise(fn, name, T, tb, row_ins, full_ins, row_outs, acc_outs=(), sp=None):
    nblk = T // tb
    assert nblk * tb == T
    has_sp = sp is not None

    def imap(f):
        return (lambda i, s: f(i, s)) if has_sp else (lambda i: f(i, None))

    in_specs, args = [], []
    for arr, w, cb, halo, off in row_ins:
        if halo == "col":
            in_specs.append(pl.BlockSpec((w, tb), imap(lambda i, s: (0, i))))
            args.append(arr)
            continue
        rowf = off if callable(off) else (lambda i, s, off=off: i + off)
        in_specs.append(pl.BlockSpec((tb, w), imap(lambda i, s, cb=cb, rowf=rowf: (rowf(i, s), cb))))
        args.append(arr)
        if halo == "prev":
            r = tb // 8
            in_specs.append(pl.BlockSpec((8, w), imap(lambda i, s, cb=cb, r=r: (jnp.maximum(i * r - 1, 0), cb))))
            args.append(arr)
        elif halo == "next":
            r, last = tb // 8, T // 8 - 1
            in_specs.append(pl.BlockSpec((8, w), imap(lambda i, s, cb=cb, r=r, last=last:
                                                      (jnp.minimum((i + 1) * r, last), cb))))
            args.append(arr)
    for arr in full_ins:
        in_specs.append(pl.BlockSpec(arr.shape, imap(lambda i, s, nd=arr.ndim: (0,) * nd)))
        args.append(arr)
    n_in, n_ro = len(args), len(row_outs)
    out_shape, out_specs, aliases = [], [], {}
    for k, (w, dt, *dest) in enumerate(row_outs):
        if dest:
            buf, cb = dest
            out_shape.append(jax.ShapeDtypeStruct(buf.shape, buf.dtype))
            out_specs.append(pl.BlockSpec((tb, w), imap(lambda i, s, cb=cb: (i, cb))))
            if not isinstance(buf, jax.ShapeDtypeStruct):
                aliases[len(args) + int(has_sp)] = k
                in_specs.append(ANY)
                args.append(buf)
        elif w < 0:
            out_shape.append(jax.ShapeDtypeStruct((-w, T), dt))
            out_specs.append(pl.BlockSpec((-w, tb), imap(lambda i, s: (0, i))))
        else:
            out_shape.append(jax.ShapeDtypeStruct((T, w), dt))
            out_specs.append(pl.BlockSpec((tb, w), imap(lambda i, s: (i, 0))))
    for shp in acc_outs:
        out_shape.append(jax.ShapeDtypeStruct(shp, F32))
        out_specs.append(pl.BlockSpec(shp, imap(lambda i, s, nd=len(shp): (0,) * nd)))

    def body(*refs):
        i = pl.program_id(0)
        if has_sp:
            sp_ref, refs = refs[0], refs[1:]
            vals = fn(i, nblk, sp_ref, *[r[...] for r in refs[:n_in]])
        else:
            vals = fn(i, nblk, *[r[...] for r in refs[:n_in]])
        outs = refs[n_in + len(aliases):]
        for ref, val in zip(outs[:n_ro], vals[:n_ro]):
            ref[...] = val.astype(ref.dtype)
        for ref, val in zip(outs[n_ro:], vals[n_ro:]):
            @pl.when(i == 0)
            def _(ref=ref, val=val):
                ref[...] = val

            @pl.when(i > 0)
            def _(ref=ref, val=val):
                ref[...] += val

    cparams = _cparams(("arbitrary",) if acc_outs else ("parallel",))
    if has_sp:
        return pl.pallas_call(
            body, name=name, out_shape=out_shape, compiler_params=cparams, input_output_aliases=aliases,
            grid_spec=pltpu.PrefetchScalarGridSpec(num_scalar_prefetch=1, grid=(nblk,), in_specs=in_specs,
                                                   out_specs=out_specs),
        )(sp, *args)
    return pl.pallas_call(
        body, name=name, grid=(nblk,), in_specs=in_specs, out_specs=out_specs, out_shape=out_shape,
        compiler_params=cparams, input_output_aliases=aliases,
    )(*args)


def R(arr, w=None, cb=0, halo=None, off=0):
    return (arr, arr.shape[1] if w is None else w, cb, halo, off)


def RC(arr):
    return (arr, arr.shape[0], 0, "col", 0)


def matmul(name, a, b, form, tm, tn, tk, out_dtypes, epi=None, extras=(), rows=(), into=None, n_acc=0, b_sel=None):
    bs = b.shape if b_sel is None else b_sel[0]
    if form == "nn":
        (M, K), N = a.shape, bs[1]
    elif form == "nt":
        (M, K), N = a.shape, bs[0]
    else:
        (K, M), N = a.shape, bs[1]
    tm, tn, tk = min(tm, M), min(tn, N), min(tk, K)
    assert M % tm == 0 and N % tn == 0 and K % tk == 0, (name, M, N, K, tm, tn, tk)

    def b_spec_of(blk, at):
        if b_sel is None:
            return pl.BlockSpec(blk, lambda i, j, k: at(i, j, k))
        blk3 = b_sel[1]
        assert int(np.prod([d for d in blk3 if d is not None])) == blk[0] * blk[1], (name, blk3, blk)
        return pl.BlockSpec(blk3, lambda i, j, k: b_sel[2](i, j, k))

    if form == "nn":
        a_spec = pl.BlockSpec((tm, tk), lambda i, j, k: (i, k))
        b_spec = b_spec_of((tk, tn), lambda i, j, k: (k, j))
        ca, cb = 1, 0
    elif form == "nt":
        a_spec = pl.BlockSpec((tm, tk), lambda i, j, k: (i, k))
        b_spec = b_spec_of((tn, tk), lambda i, j, k: (j, k))
        ca, cb = 1, 1
    else:
        a_spec = pl.BlockSpec((tk, tm), lambda i, j, k: (k, i))
        b_spec = b_spec_of((tk, tn), lambda i, j, k: (k, j))
        ca, cb = 0, 0
    nk, ne, no = K // tk, len(extras) + len(rows), len(out_dtypes)
    if epi is None:
        epi = lambda acc: (acc,)

    assert n_acc == 0 or tn == N

    def body(a_ref, b_ref, *rest):
        ex, outs, accs, acc = rest[:ne], rest[ne:ne + no], rest[ne + no:ne + no + n_acc], rest[ne + no + n_acc]
        i, k = pl.program_id(0), pl.program_id(2)

        def finish(total):
            vals = epi(total, *[e[...] for e in ex])
            for r, v in zip(outs, vals[:no]):
                r[...] = v.astype(r.dtype).reshape(r.shape)
            for r, v in zip(accs, vals[no:]):
                @pl.when(i == 0)
                def _(r=r, v=v):
                    r[...] = v

                @pl.when(i > 0)
                def _(r=r, v=v):
                    r[...] += v

        b_tile = b_ref[...]
        if b_sel is not None and len(b_sel) > 3:
            b_tile = jnp.concatenate([b_tile[s] for s in range(b_tile.shape[0])], axis=1)
        prod = _dg(_bf(a_ref[...]), _bf(b_tile.reshape(-1, b_tile.shape[-1])), ca, cb)
        if nk == 1:
            finish(prod)
            return

        @pl.when(k == 0)
        def _():
            acc[...] = prod

        @pl.when(k > 0)
        def _():
            acc[...] += prod

        @pl.when(k == nk - 1)
        def _():
            finish(acc[...])

    mn = pl.BlockSpec((tm, tn), lambda i, j, k: (i, j))
    rw = pl.BlockSpec((1, tn), lambda i, j, k: (0, j))
    acc_scratch = pltpu.VMEM((tm, tn) if nk > 1 else (8, 128), F32)
    if into is not None:
        buf, blk, bmap = into
        assert ne == 0 and no == 1
        aliased = not isinstance(buf, jax.ShapeDtypeStruct)

        def body_into(a_ref, b_ref, *rest):
            body(a_ref, b_ref, *rest[-2:])

        return pl.pallas_call(
            body_into, name=name, grid=(M // tm, N // tn, nk),
            in_specs=[a_spec, b_spec] + ([ANY] if aliased else []), out_specs=pl.BlockSpec(blk, bmap),
            out_shape=jax.ShapeDtypeStruct(buf.shape, buf.dtype),
            scratch_shapes=[acc_scratch],
            input_output_aliases={2: 0} if aliased else {},
            compiler_params=_cparams(("parallel", "parallel", "arbitrary")),
        )(a, b, *([buf] if aliased else []))
    return pl.pallas_call(
        body, name=name, grid=(M // tm, N // tn, nk),
        in_specs=[a_spec, b_spec] + [mn] * len(extras) + [rw] * len(rows), out_specs=[mn] * no + [rw] * n_acc,
        out_shape=[jax.ShapeDtypeStruct((M, N), dt) for dt in out_dtypes] + [jax.ShapeDtypeStruct((1, N), F32)] * n_acc,
        scratch_shapes=[acc_scratch],
        compiler_params=_cparams(("arbitrary",) * 3 if n_acc else ("parallel", "parallel", "arbitrary")),
    )(a, b, *extras, *rows)


def _epi_res(acc, res):
    return (res + acc,)


def _epi_rms_bwd(acc, x, dres, w):
    return rms_bwd_fn(0, 0, x, acc, dres, w)


def rms_bwd1_fn(i, n, x, dh, dres, w):
    dx, _, gw = rms_bwd_fn(i, n, x, dh, dres, w)
    return dx, gw


def _epi_final(acc, res, tgt, w):
    return final_fn(0, 0, res + acc, tgt, w)


def _epi_res_rms(acc, res, w):
    x = res + acc
    return (x, x * lax.rsqrt(jnp.mean(x * x, axis=-1, keepdims=True) + EPS) * w)


def _epi_relu2(acc):
    u = jnp.maximum(acc, 0.0)
    return (u, u * u)


def _epi_dup(acc, u):
    return (acc * 2.0 * u.astype(F32),)


def _conv(x, halo, w, i):
    halo = jnp.where(i == 0, 0.0, halo)
    xt = jnp.concatenate([halo, x], axis=0)
    shifted = [pltpu.roll(xt, 3 - k, 0)[8:, :] for k in range(3)] + [x]
    y = shifted[3] * w[3:4, :]
    for k in range(3):
        y = y + shifted[k] * w[k:k + 1, :]
    return y, shifted


def _l2n(x, scale):
    outs = []
    for h in range(x.shape[1] // 128):
        xh = x[:, 128 * h:128 * h + 128]
        outs.append(xh * (lax.rsqrt(jnp.sum(xh * xh, axis=-1, keepdims=True) + EPS) * scale))
    return jnp.concatenate(outs, axis=1)


def _l2n_bwd(x, dy, scale):
    outs = []
    for h in range(x.shape[1] // 128):
        xh, dh = x[:, 128 * h:128 * h + 128], dy[:, 128 * h:128 * h + 128] * scale
        r = lax.rsqrt(jnp.sum(xh * xh, axis=-1, keepdims=True) + EPS)
        outs.append(r * dh - xh * (r * r * r) * jnp.sum(xh * dh, axis=-1, keepdims=True))
    return jnp.concatenate(outs, axis=1)


def rms_fwd_fn(i, n, x, w):
    r = lax.rsqrt(jnp.mean(x * x, axis=-1, keepdims=True) + EPS)
    return (x * r * w,)


def rms_bwd_fn(i, n, x, dh, dres, w):
    r = lax.rsqrt(jnp.mean(x * x, axis=-1, keepdims=True) + EPS)
    g = dh * w
    dx = dres + r * g - x * (r * r * r) * jnp.mean(x * g, axis=-1, keepdims=True)
    return dx, dx, _rows(dh * x * r)


def rms_bwd_w_fn(i, n, x, dh, w):
    r = lax.rsqrt(jnp.mean(x * x, axis=-1, keepdims=True) + EPS)
    return (_rows(dh * x * r),)


def final_fn(i, n, x, tgt, w):
    r = lax.rsqrt(jnp.mean(x * x, axis=-1, keepdims=True) + EPS)
    xn = x * r
    e = xn * w - tgt
    dy = e * (1.0 / D)
    g = dy * w
    dx = r * g - x * (r * r * r) * jnp.mean(x * g, axis=-1, keepdims=True)
    return dx, dx, _rows(e * e), _rows(dy * xn)


def _gdn_gates(ba, alog_c, dtb_c):
    col = _iota(ba.shape, 1)
    amask = (col >= 8) & (col < 16)
    beta = jnp.where(col < 8, _sig(ba), 0.0)
    z = ba + dtb_c
    ea_ = jnp.exp(alog_c)
    return beta, z, ea_, jnp.where(amask, -ea_ * _softplus(z), 0.0), amask


def _cols(x, g):
    return x[:, 128 * g:128 * g + 128]


def gdn_prep_fn(i, n, qkv, halo, ba, cw, alog_c, dtb_c, eb, ea):
    outs = [[], [], []]
    for g in range(3 * GDN_H):
        yc, _ = _conv(_cols(qkv, g), _cols(halo, g), _cols(cw, g), i)
        act = yc * _sig(yc)
        if g < 2 * GDN_H:
            act = _l2n(act, GDN_DK ** -0.5 if g < GDN_H else 1.0)
        outs[g // GDN_H].append(act)
    beta, _, _, gg, _ = _gdn_gates(ba, alog_c, dtb_c)
    gcs = _chunk_cumsum(gg, GDN_C)
    return (*[jnp.concatenate(o, axis=1) for o in outs], mm_sel(gcs, ea), mm_sel(beta, eb), beta + gcs,
            jnp.transpose(gcs)[8:16, :])


def gdn_prep_bwd_fn(i, n, qkv, halo, ba, dqn, dkn, dv, dgb, dgcs_t, cw, alog_c, dtb_c):
    dycs, dwl = [], [[], [], [], []]
    for g in range(3 * GDN_H):
        yc, shifted = _conv(_cols(qkv, g), _cols(halo, g), _cols(cw, g), i)
        sg = _sig(yc)
        act = yc * sg
        if g < GDN_H:
            d = _l2n_bwd(act, _cols(dqn, g), GDN_DK ** -0.5)
        elif g < 2 * GDN_H:
            d = _l2n_bwd(act, _cols(dkn, g - GDN_H), 1.0)
        else:
            d = _cols(dv, g - 2 * GDN_H)
        dyc_g = d * (sg * (1.0 + yc * (1.0 - sg)))
        dycs.append(dyc_g)
        for k in range(4):
            dwl[k].append(_rows(dyc_g * shifted[k]))
    dyc = jnp.concatenate(dycs, axis=1)
    dws = [jnp.concatenate(l, axis=1) for l in dwl]
    beta, z, ea_, g, amask = _gdn_gates(ba, alog_c, dtb_c)
    tbn = ba.shape[0]
    rowpart = jnp.transpose(jnp.concatenate([jnp.zeros((8, tbn), F32), dgcs_t, jnp.zeros((112, tbn), F32)], axis=0))
    dg = _chunk_revcumsum(jnp.where(amask, dgb, 0.0) - rowpart, GDN_C)
    draw = jnp.where(amask, dg * (-ea_) * _sig(z), 0.0)
    dba = draw + dgb * beta * (1.0 - beta)
    return (dyc, dba, dws[0], dws[1], dws[2], dws[3], _rows(dg * g), _rows(draw))


def conv_bwd_fn(i, n, dyc, halo, w):
    halo = jnp.where(i == n - 1, 0.0, halo)
    tb = dyc.shape[0]
    outs = []
    for g in range(dyc.shape[1] // 128):
        d, wg = _cols(dyc, g), _cols(w, g)
        xt = jnp.concatenate([d, _cols(halo, g)], axis=0)
        dx = d * wg[3:4, :]
        for k in range(3):
            dx = dx + pltpu.roll(xt, tb + 8 - (3 - k), 0)[:tb, :] * wg[k:k + 1, :]
        outs.append(dx)
    return (jnp.concatenate(outs, axis=1),)


def gdn_post_fn(i, n, o, z, w):
    outs = []
    for h in range(GDN_H):
        oh, zh = o[:, 128 * h:128 * h + 128], z[:, 128 * h:128 * h + 128]
        r = lax.rsqrt(jnp.mean(oh * oh, axis=-1, keepdims=True) + EPS)
        outs.append(oh * r * w * (zh * _sig(zh)))
    return (jnp.concatenate(outs, axis=1),)


def gdn_post_bwd_fn(i, n, o, z, doa, w):
    dos, dzs, dw = [], [], None
    for h in range(GDN_H):
        sl = slice(128 * h, 128 * h + 128)
        oh, zh, dh = o[:, sl], z[:, sl], doa[:, sl]
        r = lax.rsqrt(jnp.mean(oh * oh, axis=-1, keepdims=True) + EPS)
        s = _sig(zh)
        dn = dh * (zh * s)
        dzs.append(dh * (oh * r * w) * (s * (1.0 + zh * (1.0 - s))))
        t = _rows(dn * oh * r)
        dw = t if dw is None else dw + t
        g = dn * w
        dos.append(r * g - oh * (r * r * r) * jnp.mean(oh * g, axis=-1, keepdims=True))
    return jnp.concatenate(dos, axis=1), jnp.concatenate(dzs, axis=1), dw


def _ssd_gates(dtblk, alog_c, dtb_c):
    hmask = _iota(dtblk.shape, 1) < SSM_H
    z = dtblk + dtb_c
    return jnp.where(hmask, _softplus(z), 0.0), -jnp.exp(alog_c), z, hmask


def _silu_conv_cols(x, halo, w, b, i):
    outs = []
    for g in range(x.shape[1] // 128):
        yc, _ = _conv(_cols(x, g), _cols(halo, g), _cols(w, g), i)
        yc = yc + _cols(b, g)
        outs.append(yc * _sig(yc))
    return jnp.concatenate(outs, axis=1)


def _silu_conv_bwd_cols(x, halo, w, b, dout, i):
    dycs, dwl = [], [[], [], [], []]
    for g in range(x.shape[1] // 128):
        yc, shifted = _conv(_cols(x, g), _cols(halo, g), _cols(w, g), i)
        yc = yc + _cols(b, g)
        s = _sig(yc)
        dyc_g = _cols(dout, g) * (s * (1.0 + yc * (1.0 - s)))
        dycs.append(dyc_g)
        for k in range(4):
            dwl[k].append(_rows(dyc_g * shifted[k]))
    dyc = jnp.concatenate(dycs, axis=1)
    return dyc, [jnp.concatenate(l, axis=1) for l in dwl], _rows(dyc)


def ssd_prep_fn(i, n, xp, hx, bcp, hbc, dtblk, cwx, cwbc, cbx, cbbc, alog_c, dtb_c, e16):
    dt, a_neg, _, _ = _ssd_gates(dtblk, alog_c, dtb_c)
    acs = _chunk_cumsum(dt * a_neg, SSM_L)
    return (_silu_conv_cols(xp, hx, cwx, cbx, i), _silu_conv_cols(bcp, hbc, cwbc, cbbc, i), mm_sel(dt, e16),
            mm_sel(acs, e16), jnp.transpose(acs)[0:SSM_H, :])


def ssd_prep_bwd_fn(i, n, xp, hx, bcp, hbc, dtblk, dxs_a, dxs_b, db, dc, dgate, dacs_t, cwx, cwbc, cbx, cbbc, alog_c, dtb_c):
    dyx, dwx, dbx = _silu_conv_bwd_cols(xp, hx, cwx, cbx, dxs_a + dxs_b, i)
    dybc, dwbc, dbbc = _silu_conv_bwd_cols(bcp, hbc, cwbc, cbbc, jnp.concatenate([db, dc], axis=1), i)
    dt, a_neg, z, hmask = _ssd_gates(dtblk, alog_c, dtb_c)
    g0, g1 = dgate[:, :128], dgate[:, 128:]
    col = _iota(g0.shape, 1)
    lo, mid = col < 8, (col >= 8) & (col < 16)
    dacs_col = jnp.where(lo, g0, 0.0) + pltpu.roll(jnp.where(lo, g1, 0.0), 8, 1)
    ddt_dir = pltpu.roll(jnp.where(mid, g0, 0.0), 120, 1) + jnp.where(mid, g1, 0.0)
    tbn = dtblk.shape[0]
    rowpart = jnp.transpose(jnp.concatenate([dacs_t, jnp.zeros((128 - SSM_H, tbn), F32)], axis=0))
    da = _chunk_revcumsum(dacs_col - rowpart, SSM_L)
    draw = jnp.where(hmask, (ddt_dir + da * a_neg) * _sig(z), 0.0)
    return (dyx, dybc, draw, *dwx, *dwbc, dbx, dbbc, _rows(da * dt * a_neg), _rows(draw))


def _ssd_gate(y, xs, zs, d_x):
    y2 = y + xs * d_x
    s = _sig(zs)
    return y2, s, y2 * (zs * s)


def ssd_post_fn(i, n, y, xs, zs, d_x, nw):
    _, _, yg = _ssd_gate(y, xs, zs, d_x)
    outs = []
    for g in range(2):
        v = yg[:, 512 * g:512 * g + 512]
        outs.append(v * lax.rsqrt(jnp.mean(v * v, axis=-1, keepdims=True) + EPS))
    return (jnp.concatenate(outs, axis=1) * nw,)


def ssd_post_bwd_fn(i, n, y, xs, zs, dob, d_x, nw):
    y2, s, yg = _ssd_gate(y, xs, zs, d_x)
    gfull = dob * nw
    dygs, dnw = [], []
    for g in range(2):
        sl = slice(512 * g, 512 * g + 512)
        v, gg = yg[:, sl], gfull[:, sl]
        r = lax.rsqrt(jnp.mean(v * v, axis=-1, keepdims=True) + EPS)
        dygs.append(r * gg - v * (r * r * r) * jnp.mean(v * gg, axis=-1, keepdims=True))
        dnw.append(_rows(dob[:, sl] * v * r))
    dyg = jnp.concatenate(dygs, axis=1)
    dy2 = dyg * (zs * s)
    dzs = dyg * y2 * (s * (1.0 + zs * (1.0 - s)))
    return dy2, dy2 * d_x, dzs, jnp.concatenate(dnw, axis=1), _rows(dy2 * xs)


def _attn_probs(q, k):
    hs = [slice(MEM_DH * h, MEM_DH * h + MEM_DH) for h in range(MEM_H)]
    ss = [mm_nt(q[:, sl], k[:, sl]) * (MEM_DH ** -0.5) for sl in hs]
    es = [jnp.exp(s - jnp.max(s, axis=-1, keepdims=True)) for s in ss]
    return hs, [e / jnp.sum(e, axis=-1, keepdims=True) for e in es]


def attn_fn(i, n, q, k, v):
    hs, ps = _attn_probs(q, k)
    return (jnp.concatenate([mm(p, v[:, sl]) for p, sl in zip(ps, hs)], axis=1),)


def attn_bwd_fn(i, n, q, do, k, v):
    hs, ps = _attn_probs(q, k)
    dvs = [mm_tn(p, do[:, sl]) for p, sl in zip(ps, hs)]
    dps = [mm_nt(do[:, sl], v[:, sl]) for sl in hs]
    dss = [p * (dp - jnp.sum(dp * p, axis=-1, keepdims=True)) * (MEM_DH ** -0.5) for p, dp in zip(ps, dps)]
    dqs = [mm(ds, k[:, sl]) for ds, sl in zip(dss, hs)]
    dks = [mm_tn(ds, q[:, sl]) for ds, sl in zip(dss, hs)]
    return jnp.concatenate(dqs, axis=1), jnp.concatenate(dks, axis=1), jnp.concatenate(dvs, axis=1)


def add2_fn(i, n, sp, a, b):
    return (a + b,)


def sum4_fn(i, n, sp, a, b, c, d):
    return (((a.astype(F32) + b.astype(F32)) + c.astype(F32)) + d.astype(F32),)


def _adamw(w, g, m, v):
    m = ADAM_B1 * m + (1.0 - ADAM_B1) * g
    v = ADAM_B2 * v + (1.0 - ADAM_B2) * (g * g)
    m_hat = m / (1.0 - ADAM_B1 ** ADAM_STEP)
    v_hat = v / (1.0 - ADAM_B2 ** ADAM_STEP)
    delta = -ADAM_LR * (m_hat / (jnp.sqrt(v_hat) + ADAM_EPS) + ADAM_WD * w)
    return delta, m, v


def _gate_cols(gb, h):
    lane = _iota(gb.shape, 1)
    return _lanes(jnp.where(lane == h, gb, 0.0)), _lanes(jnp.where(lane == 8 + h, gb, 0.0))


def _gdn_stage1(q, k, v, bb, gcs, grow):
    C = GDN_C
    row, col = _iota((C, C), 0), _iota((C, C), 1)
    incl, strict = row >= col, row > col
    dmat = jnp.where(incl, jnp.exp(jnp.minimum((gcs if gcs.shape[1] == 1 else gcs[:, :C]) - grow, 0.0)), 0.0)
    gam = jnp.exp(gcs)
    gl = gcs[C - 1:C, :]
    kb, vb = k * bb, v * bb
    kg = kb * gam
    lmat = jnp.where(strict, mm_nt(kb, k) * dmat, 0.0)
    pmat = jnp.where(incl, mm_nt(q, k) * dmat, 0.0)
    return dict(q=q, k=k, v=v, bb=bb, incl=incl, strict=strict, dmat=dmat, gam=gam, kb=kb, vb=vb, kg=kg,
                lmat=lmat, pmat=pmat, qd=q * gam, kdec=jnp.exp(gl - gcs), cd=jnp.exp(gl))


def _gdn_inverse(lmats):
    C = GDN_C
    eye = (_iota((C, C), 0) == _iota((C, C), 1)).astype(F32)
    xs = [-l for l in lmats]
    ts = [eye + x for x in xs]
    for _ in range(5):
        xs = [mm(x, x) for x in xs]
        ts = [t + mm(t, x) for t, x in zip(ts, xs)]
    res = [eye - mm3(eye + l, t) for l, t in zip(lmats, ts)]
    return [t + mm(t, r) for t, r in zip(ts, res)]


def gdn_fwd(qn, kn, v, gcs_x, beta_x, gcs_t, tb, gh):
    T = qn.shape[0]
    nb, ncb, nc, C = T // tb, tb // GDN_C, T // GDN_C, GDN_C
    idx = [(hh, c) for hh in range(gh) for c in range(ncb)]

    def body(q_ref, k_ref, v_ref, g_ref, b_ref, gt_ref, o_ref, st_ref, ti_ref, s_scr):
        @pl.when(pl.program_id(1) == 0)
        def _():
            s_scr[...] = jnp.zeros_like(s_scr)

        grows = [gt_ref[hh] for hh in range(gh)]
        at = lambda hh, c: (slice(C * c, C * (c + 1)), slice(128 * hh, 128 * hh + 128))
        st1 = []
        for hh, c in idx:
            sl, ln = at(hh, c)
            st1.append(_gdn_stage1(q_ref[sl, ln], k_ref[sl, ln], v_ref[sl, ln], b_ref[sl, ln], g_ref[sl, ln],
                                   grows[hh][:, sl]))
        tinvs = _gdn_inverse([s["lmat"] for s in st1])
        us = [mm(t, s["vb"]) for t, s in zip(tinvs, st1)]
        ws = [mm(t, s["kg"]) for t, s in zip(tinvs, st1)]
        kds = [s["k"] * s["kdec"] for s in st1]
        ms = [mm_tn(kd, w) for kd, w in zip(kds, ws)]
        bs = [mm_tn(kd, u) for kd, u in zip(kds, us)]
        gs = [s["qd"] - mm(s["pmat"], w) for s, w in zip(st1, ws)]
        pus = [mm(s["pmat"], u) for s, u in zip(st1, us)]
        ss = [s_scr[hh] for hh in range(gh)]
        for c in range(ncb):
            for hh in range(gh):
                n, (sl, ln) = hh * ncb + c, at(hh, c)
                ti_ref[hh, sl, :] = tinvs[n]
                st_ref[hh, c] = ss[hh]
                o_ref[sl, ln] = mm(gs[n], ss[hh]) + pus[n]
                ss[hh] = st1[n]["cd"] * ss[hh] - mm(ms[n], ss[hh]) + bs[n]
        for hh in range(gh):
            s_scr[hh] = ss[hh]

    blk = pl.BlockSpec((tb, 128 * gh), lambda h, i: (i, h))
    return pl.pallas_call(
        body, name="gdn_fwd", grid=(GDN_H // gh, nb),
        in_specs=[blk] * 5 + [pl.BlockSpec((gh, 1, tb), lambda h, i: (h, 0, i))],
        out_specs=[blk, pl.BlockSpec((gh, ncb, 128, 128), lambda h, i: (h, i, 0, 0)),
                   pl.BlockSpec((gh, tb, C), lambda h, i: (h, i, 0))],
        out_shape=[jax.ShapeDtypeStruct((T, D), F32), jax.ShapeDtypeStruct((GDN_H, nc, 128, 128), F32),
                   jax.ShapeDtypeStruct((GDN_H, T, C), F32)],
        scratch_shapes=[pltpu.VMEM((gh, 128, 128), F32)],
        compiler_params=_cparams(("parallel", "arbitrary")),
    )(qn, kn, v, gcs_x, beta_x, gcs_t)


def gdn_bwd(qn, kn, v, gb, gcs_t, do, states, tinv, tb, gh):
    T = qn.shape[0]
    nb, ncb, C = T // tb, tb // GDN_C, GDN_C
    assert gh == GDN_H

    def body(q_ref, k_ref, v_ref, gb_ref, gt_ref, do_ref, st_ref, ti_ref,
             dq_ref, dk_ref, dv_ref, dgb_ref, dgr_ref, ds_scr):
        @pl.when(pl.program_id(1) == 0)
        def _():
            ds_scr[...] = jnp.zeros_like(ds_scr)

        grows = [gt_ref[hh] for hh in range(gh)]
        at = lambda hh, c: (slice(C * c, C * (c + 1)), slice(128 * hh, 128 * hh + 128))
        lastrow = _iota((C, 1), 0) == C - 1
        lane = _iota((C, 128), 1)
        idx = [(hh, c) for hh in range(gh) for c in range(ncb)]
        P = []
        for hh, c in idx:
            sl, ln = at(hh, c)
            lc = _gdn_stage1(q_ref[sl, ln], k_ref[sl, ln], v_ref[sl, ln], *_gate_cols(gb_ref[sl, :], hh), grows[hh][:, sl])
            lc.update(tinv=ti_ref[hh, sl, :], s=st_ref[hh, c], do=do_ref[sl, ln], kd=lc["k"] * lc["kdec"])
            P.append(lc)
        for l, u, w in zip(P, [mm(l["tinv"], l["vb"]) for l in P], [mm(l["tinv"], l["kg"]) for l in P]):
            l.update(u=u, w=w)
        for l, x in zip(P, [mm(l["w"], l["s"]) for l in P]):
            l["vn"] = l["u"] - x
        for l, a, b, c_, d in zip(P, [mm_nt(l["do"], l["s"]) for l in P], [mm_nt(l["do"], l["vn"]) for l in P],
                                  [mm_tn(l["qd"], l["do"]) for l in P], [mm_tn(l["pmat"], l["do"]) for l in P]):
            l.update(dqd=a, dp=jnp.where(l["incl"], b, 0.0), ds_q=c_, dvn_p=d)
        pre = dict(zip(idx, P))
        rows = {}
        hs = range(gh)
        ds = [ds_scr[hh] for hh in hs]
        for c in reversed(range(ncb)):
            L = [pre[hh, c] for hh in hs]
            dvn = [l["dvn_p"] + mm(l["kd"], d) for l, d in zip(L, ds)]
            dkd = [mm_nt(l["vn"], d) for l, d in zip(L, ds)]
            dcd = [_sum_all(l["s"] * d) for l, d in zip(L, ds)]
            ds = [l["ds_q"] + l["cd"] * d - mm_tn(l["w"], x) for l, d, x in zip(L, ds, dvn)]
            dw = [-mm_nt(x, l["s"]) for l, x in zip(L, dvn)]
            dvb = [mm_tn(l["tinv"], x) for l, x in zip(L, dvn)]
            dkg = [mm_tn(l["tinv"], x) for l, x in zip(L, dw)]
            da = [-jnp.where(l["strict"], mm_nt(a, l["u"]) + mm_nt(b, l["w"]), 0.0) for l, a, b in zip(L, dvb, dkg)]
            dm = [a * l["dmat"] for l, a in zip(L, da)]
            dn = [l["dp"] * l["dmat"] for l in L]
            dkb = [mm(a, l["k"]) for l, a in zip(L, dm)]
            dq = [mm(a, l["k"]) + l["gam"] * l["dqd"] for l, a in zip(L, dn)]
            dk = [mm_tn(a, l["kb"]) + mm_tn(b, l["q"]) for l, a, b in zip(L, dm, dn)]
            dgb = jnp.zeros((C, 128), F32)
            for hh in hs:
                sl, ln = at(hh, c)
                l = L[hh]
                e = da[hh] * l["lmat"] + l["dp"] * l["pmat"]
                t_kd = _lanes(dkd[hh] * l["kd"])
                dgl = _sum_all(t_kd) + dcd[hh] * l["cd"][:, :1]
                dgcs = (_lanes(e) + _lanes(l["dqd"] * l["qd"]) - t_kd + _lanes(dkg[hh] * l["kg"])
                        + jnp.where(lastrow, dgl, 0.0))
                rows[hh, c] = _rows(e)
                dq_ref[sl, ln] = dq[hh]
                dk_ref[sl, ln] = (dk[hh] + l["kdec"] * dkd[hh] + l["bb"] * l["gam"] * dkg[hh] + l["bb"] * dkb[hh])
                dv_ref[sl, ln] = l["bb"] * dvb[hh]
                dbeta = _lanes(dkg[hh] * l["gam"] * l["k"]) + _lanes(dvb[hh] * l["v"]) + _lanes(dkb[hh] * l["k"])
                dgb = dgb + jnp.where(lane == hh, dbeta, 0.0) + jnp.where(lane == 8 + hh, dgcs, 0.0)
            dgb_ref[slice(C * c, C * (c + 1)), :] = dgb
        for hh in hs:
            ds_scr[hh] = ds[hh]
            dgr_ref[hh] = jnp.concatenate([rows[hh, c] for c in range(ncb)], axis=1)

    blk = pl.BlockSpec((tb, 128 * gh), lambda h, i: (nb - 1 - i, h))
    rowspec = pl.BlockSpec((gh, 1, tb), lambda h, i: (h, 0, nb - 1 - i))
    cblk = pl.BlockSpec((tb, 128), lambda h, i: (nb - 1 - i, 0))
    return pl.pallas_call(
        body, name="gdn_bwd", grid=(GDN_H // gh, nb),
        in_specs=[blk] * 3 + [cblk, rowspec, blk,
                              pl.BlockSpec((gh, ncb, 128, 128), lambda h, i: (h, nb - 1 - i, 0, 0)),
                              pl.BlockSpec((gh, tb, C), lambda h, i: (h, nb - 1 - i, 0))],
        out_specs=[blk] * 3 + [cblk, rowspec],
        out_shape=[jax.ShapeDtypeStruct((T, D), F32)] * 3 + [jax.ShapeDtypeStruct((T, 128), F32),
                                                             jax.ShapeDtypeStruct((GDN_H, 1, T), F32)],
        scratch_shapes=[pltpu.VMEM((gh, 128, 128), F32)],
        compiler_params=_cparams(("parallel", "arbitrary")),
    )(qn, kn, v, gb, gcs_t, do, states, tinv)


def _ssd_pair(x2, dt2, acs2):
    last = acs2[SSM_L - 1:SSM_L, :]
    return jnp.exp(acs2), jnp.exp(last - acs2), x2 * dt2


def _ssd_head(hh, acs2, arow, dec2, cbm, bm, incl, col):
    lmask = (col >= 64 * hh) & (col < 64 * hh + 64)
    sg = jnp.where(incl, jnp.exp(jnp.minimum(acs2[:, 64 * hh:64 * hh + 1] - arow, 0.0)), 0.0)
    dec_col = dec2[:, 64 * hh:64 * hh + 1]
    return lmask, sg, sg * cbm, dec_col, bm * dec_col


def ssd_fwd(xs, bc, dt_x, acs_x, acs_t):
    T = xs.shape[0]
    nc, L = T // SSM_L, SSM_L

    def body(x_ref, bc_ref, dt_ref, ac_ref, at_ref, y_ref, hst_ref, h_scr):
        @pl.when(pl.program_id(0) == 0)
        def _():
            h_scr[...] = jnp.zeros_like(h_scr)

        row, col = _iota((L, L), 0), _iota((L, L), 1)
        incl = row >= col
        P, H = [], []
        for gp in range(8):
            g = gp // 4
            bm, cm = bc_ref[:, 128 * g:128 * g + 128], bc_ref[:, 256 + 128 * g:384 + 128 * g]
            cbm = mm_nt(cm, bm) if gp % 4 == 0 else cbm
            sl = slice(128 * gp, 128 * gp + 128)
            acs2 = ac_ref[:, sl]
            lam2, dec2, xd2 = _ssd_pair(x_ref[:, sl], dt_ref[:, sl], acs2)
            P.append(dict(sl=sl, lam2=lam2, xd2=xd2, hprev=h_scr[gp], cm=cm))
            for hh in range(2):
                lmask, _, mmat, _, bd = _ssd_head(hh, acs2, at_ref[2 * gp + hh], dec2, cbm, bm, incl, col)
                H.append(dict(mmat=mmat, bd=bd, xdh=jnp.where(lmask, xd2, 0.0), xd2=xd2))
        ys = [mm(h["mmat"], h["xdh"]) for h in H]
        sts = [mm_tn(h["xd2"], h["bd"]) for h in H]
        zs = [mm_nt(p["cm"], p["hprev"]) for p in P]
        for gp, p in enumerate(P):
            hst_ref[gp // 4, gp % 4] = p["hprev"]
            y_ref[:, p["sl"]] = ys[2 * gp] + ys[2 * gp + 1] + p["lam2"] * zs[gp]
            lam_rows = jnp.where(row < 64, p["lam2"][L - 1:L, 0:1], p["lam2"][L - 1:L, 64:65])
            h_scr[gp] = lam_rows * p["hprev"] + jnp.where(row < 64, sts[2 * gp], sts[2 * gp + 1])

    blk = pl.BlockSpec((L, D), lambda c: (c, 0))
    return pl.pallas_call(
        body, name="ssd_fwd", grid=(nc,),
        in_specs=[blk, pl.BlockSpec((L, 512), lambda c: (c, 0)), blk, blk, pl.BlockSpec((SSM_H, 1, L), lambda c: (0, 0, c))],
        out_specs=[blk, pl.BlockSpec((2, None, 4, 128, 128), lambda c: (0, c, 0, 0, 0))],
        out_shape=[jax.ShapeDtypeStruct((T, D), F32), jax.ShapeDtypeStruct((2, nc, 4, 128, 128), F32)],
        scratch_shapes=[pltpu.VMEM((8, 128, 128), F32)],
        compiler_params=_cparams(("arbitrary",)),
    )(xs, bc, dt_x, acs_x, acs_t)


def ssd_bwd(xs, bc, dt_x, acs_x, acs_t, dy, hstates):
    T = xs.shape[0]
    nc, L = T // SSM_L, SSM_L

    def body(x_ref, bc_ref, dt_ref, ac_ref, at_ref, dy_ref, hst_ref,
             dx_ref, db_ref, dc_ref, dgate_ref, dar_ref, dh_scr):
        @pl.when(pl.program_id(0) == 0)
        def _():
            dh_scr[...] = jnp.zeros_like(dh_scr)

        row, col = _iota((L, L), 0), _iota((L, L), 1)
        rowc = _iota((L, 1), 0)
        incl = row >= col
        G = [dict(bm=bc_ref[:, 128 * g:128 * g + 128], cm=bc_ref[:, 256 + 128 * g:384 + 128 * g]) for g in range(2)]
        for gr in G:
            gr["cbm"] = mm_nt(gr["cm"], gr["bm"])
        P = []
        for gp in range(8):
            sl = slice(128 * gp, 128 * gp + 128)
            x2, dt2, dy2, acs2 = x_ref[:, sl], dt_ref[:, sl], dy_ref[:, sl], ac_ref[:, sl]
            lam2, dec2, xd2 = _ssd_pair(x2, dt2, acs2)
            P.append(dict(sl=sl, gr=G[gp // 4], x2=x2, dt2=dt2, dy2=dy2, acs2=acs2, lam2=lam2, dec2=dec2, xd2=xd2,
                          hprev=hst_ref[gp // 4, gp % 4], dhn=dh_scr[gp], dz=lam2 * dy2))
        zs = [mm_nt(p["gr"]["cm"], p["hprev"]) for p in P]
        dcm_t = [mm(p["dz"], p["hprev"]) for p in P]
        dh_z = [mm_tn(p["dz"], p["gr"]["cm"]) for p in P]
        H = []
        for gp, p in enumerate(P):
            p["yoff"] = p["dz"] * zs[gp]
            p["q_rows"] = _lanes(p["dhn"] * p["hprev"])
            for hh in range(2):
                lmask, sg, mmat, dec_col, bd = _ssd_head(hh, p["acs2"], at_ref[2 * gp + hh], p["dec2"], p["gr"]["cbm"],
                                                         p["gr"]["bm"], incl, col)
                H.append(dict(p=p, hh=hh, j=2 * gp + hh, lmask=lmask, sg=sg, mmat=mmat, dec_col=dec_col, bd=bd))
        dms = [mm_nt(jnp.where(h["lmask"], h["p"]["dy2"], 0.0), h["p"]["xd2"]) for h in H]
        a1s = [mm_tn(h["mmat"], h["p"]["dy2"]) for h in H]
        a2s = [mm_nt(h["bd"], h["p"]["dhn"]) for h in H]
        dbds = [mm(jnp.where(h["lmask"], h["p"]["xd2"], 0.0), h["p"]["dhn"]) for h in H]
        for gr in G:
            gr.update(dcb=jnp.zeros((L, L), F32), dbm=jnp.zeros((L, SSM_N), F32), comp=jnp.zeros((L, 128), F32))
        dxd = [jnp.zeros((L, 128), F32) for _ in P]
        for h, dm_raw, a1, a2, dbd in zip(H, dms, a1s, a2s, dbds):
            p, hh, j = h["p"], h["hh"], h["j"]
            gr, jg = p["gr"], j % 8
            dm = jnp.where(incl, dm_raw, 0.0)
            gr["dcb"] = gr["dcb"] + dm * h["sg"]
            e = dm * h["mmat"]
            dxd_h = jnp.where(h["lmask"], a1 + a2, 0.0)
            dxd[j // 2] = dxd[j // 2] + dxd_h
            gr["dbm"] = gr["dbm"] + h["dec_col"] * dbd
            t = _lanes(dbd * h["bd"])
            lam_h = p["lam2"][L - 1:L, 64 * hh:64 * hh + 1]
            in_head = (rowc >= 64 * hh) & (rowc < 64 * hh + 64)
            add_last = _sum_all(t) + _sum_all(jnp.where(in_head, p["q_rows"], 0.0)) * lam_h
            dacs_col = (_lanes(jnp.where(h["lmask"], p["yoff"], 0.0)) + _lanes(e) - t
                        + jnp.where(rowc == L - 1, add_last, 0.0))
            ddt_col = _lanes(dxd_h * p["x2"])
            dar_ref[j] = _rows(e)
            gr["comp"] = gr["comp"] + jnp.where(col == jg, dacs_col, 0.0) + jnp.where(col == 8 + jg, ddt_col, 0.0)
        for gp, p in enumerate(P):
            lam_rows = jnp.where(row < 64, p["lam2"][L - 1:L, 0:1], p["lam2"][L - 1:L, 64:65])
            dh_scr[gp] = dh_z[gp] + lam_rows * p["dhn"]
            dx_ref[:, p["sl"]] = p["dt2"] * dxd[gp]
        for g, gr in enumerate(G):
            lanes = slice(128 * g, 128 * g + 128)
            dcm = (dcm_t[4 * g] + dcm_t[4 * g + 1]) + (dcm_t[4 * g + 2] + dcm_t[4 * g + 3])
            db_ref[:, lanes] = gr["dbm"] + mm_tn(gr["dcb"], gr["cm"])
            dc_ref[:, lanes] = dcm + mm(gr["dcb"], gr["bm"])
            dgate_ref[:, lanes] = gr["comp"]

    rv = lambda c: (nc - 1 - c, 0)
    blk, blk256 = pl.BlockSpec((L, D), rv), pl.BlockSpec((L, 256), rv)
    rowspec = pl.BlockSpec((SSM_H, 1, L), lambda c: (0, 0, nc - 1 - c))
    return pl.pallas_call(
        body, name="ssd_bwd", grid=(nc,),
        in_specs=[blk, pl.BlockSpec((L, 512), rv), blk, blk, rowspec, blk,
                  pl.BlockSpec((2, None, 4, 128, 128), lambda c: (0, nc - 1 - c, 0, 0, 0))],
        out_specs=[blk, blk256, blk256, blk256, rowspec],
        out_shape=[jax.ShapeDtypeStruct((T, D), F32), jax.ShapeDtypeStruct((T, 256), F32),
                   jax.ShapeDtypeStruct((T, 256), F32), jax.ShapeDtypeStruct((T, 256), F32),
                   jax.ShapeDtypeStruct((SSM_H, 1, T), F32)],
        scratch_shapes=[pltpu.VMEM((8, 128, 128), F32)],
        compiler_params=_cparams(("arbitrary",)),
    )(xs, bc, dt_x, acs_x, acs_t, dy, hstates)


def _pos():
    return lax.axis_index("x"), lax.axis_index("y"), lax.axis_index("c")


def _other_chips(x, y):
    return [(1 - x, y), (x, 1 - y), (1 - x, 1 - y)]


def _rcopy(src, dst, ssem, rsem, dev):
    return pltpu.make_async_remote_copy(src_ref=src, dst_ref=dst, send_sem=ssem, recv_sem=rsem,
                                        device_id=dev, device_id_type=MESH)


def _rows_at(start, n):
    return pl.ds(pl.multiple_of(start, 8), n)


def _comm_call(body, name, out_shape, n_in, scratch):
    return pl.pallas_call(
        body, name=name, out_shape=out_shape, in_specs=[ANY] * n_in,
        out_specs=[ANY] * len(out_shape) if isinstance(out_shape, (list, tuple)) else ANY,
        scratch_shapes=scratch,
        compiler_params=pltpu.CompilerParams(has_side_effects=True),
    )


def _dma_sems(n):
    return pltpu.SemaphoreType.DMA((n,))


def ag_chips(name, shard):
    rr, cc = shard.shape
    h, nq = rr // 2, ICI_CHUNKS
    hq = h // nq

    def body(x_ref, out_ref, ssem, rsem):
        x, y, c = _pos()
        me_s = 2 * x + y
        chips = _other_chips(x, y)
        started = []
        for q in range(nq):
            rows = _rows_at(c * h + q * hq, hq)
            for j, (cx, cy) in enumerate(chips):
                cp = _rcopy(x_ref.at[rows], out_ref.at[me_s, rows], ssem.at[j * nq + q], rsem.at[j * nq + q], (cx, cy, c))
                cp.start()
                started.append(cp)
        for q in range(nq):
            rows = _rows_at(c * h + q * hq, hq)
            for j, (cx, cy) in enumerate(chips):
                blk = out_ref.at[2 * cx + cy, rows]
                _rcopy(blk, blk, ssem.at[j * nq + q], rsem.at[j * nq + q], (cx, cy, c)).wait_recv()
                k = 3 * nq + j * nq + q
                cp = _rcopy(blk, blk, ssem.at[k], rsem.at[k], (x, y, 1 - c))
                cp.start()
                started.append(cp)
        for q in range(nq):
            rows = _rows_at((1 - c) * h + q * hq, hq)
            for j, (cx, cy) in enumerate(chips):
                blk = out_ref.at[2 * cx + cy, rows]
                k = 3 * nq + j * nq + q
                _rcopy(blk, blk, ssem.at[k], rsem.at[k], (x, y, 1 - c)).wait_recv()
        for cp in started:
            cp.wait_send()

    return _comm_call(body, name, jax.ShapeDtypeStruct((4, rr, cc), shard.dtype), 1,
                      [_dma_sems(6 * nq), _dma_sems(6 * nq)])(shard)


def _with_own(shard, got, s_me):
    return lax.dynamic_update_index_in_dim(got, shard, s_me, 0)


def all_gather_chips(name, shard, s_me):
    return _with_own(shard, ag_chips(name, shard), s_me)


HBM_SPEC = pl.BlockSpec(memory_space=pltpu.HBM)
SEM_SPEC = pl.BlockSpec(memory_space=pltpu.SEMAPHORE)
SPLIT_EFFECT = pltpu.SideEffectType.DATAFLOW_SIDE_EFFECTING


def _split_copies(pieces, x_ref, land_ref, sems, arriving):
    x, y, c = _pos()
    return [_rcopy(s, d_in if arriving else d_out, sems[j], sems[3 + j], dev)
            for j, (s, d_out, d_in, dev) in enumerate(pieces(x_ref, land_ref, x, y, c))]


def split_copy_start(name, src, land_shape, pieces, after):
    def body(x_ref, land_ref, after_ref, *outs):
        for cp in _split_copies(pieces, x_ref, land_ref, outs[:6], False):
            cp.start()
        outs[8][...] = jnp.zeros_like(outs[8])

    dma = pltpu.SemaphoreType.DMA(())
    res = pl.pallas_call(
        body, name=name,
        out_shape=(dma,) * 6 + (pltpu.HBM(src.shape, src.dtype), pltpu.HBM(land_shape, src.dtype),
                                jax.ShapeDtypeStruct((8, 128), F32)),
        in_specs=(HBM_SPEC, HBM_SPEC, ANY),
        out_specs=(SEM_SPEC,) * 6 + (HBM_SPEC, HBM_SPEC, pl.BlockSpec(memory_space=pltpu.VMEM)),
        input_output_aliases={0: 6, 1: 7},
        compiler_params=pltpu.CompilerParams(has_side_effects=SPLIT_EFFECT),
    )(pltpu.with_memory_space_constraint(src, pltpu.HBM),
      pltpu.with_memory_space_constraint(lax.empty(land_shape, src.dtype), pltpu.HBM), after)
    return res[:6], res[6], res[7], res[8]


def split_copy_wait(name, sems, src_thru, land_thru, after, pieces):
    def body(x_ref, land_ref, *rest):
        for cp in _split_copies(pieces, x_ref, land_ref, rest[:6], False):
            cp.wait_send()
        for cp in _split_copies(pieces, x_ref, land_ref, rest[:6], True):
            cp.wait_recv()

    return pl.pallas_call(
        body, name=name,
        out_shape=(pltpu.HBM(src_thru.shape, src_thru.dtype), pltpu.HBM(land_thru.shape, land_thru.dtype)),
        in_specs=(HBM_SPEC, HBM_SPEC) + (SEM_SPEC,) * 6 + (ANY,), out_specs=(HBM_SPEC, HBM_SPEC),
        input_output_aliases={0: 0, 1: 1},
        compiler_params=pltpu.CompilerParams(has_side_effects=SPLIT_EFFECT),
    )(src_thru, land_thru, *sems, after)


def ag_pieces(h):
    def pieces(x_ref, land_ref, x, y, c):
        rows = _rows_at(c * h, h)
        return [(x_ref.at[rows], land_ref.at[2 * x + y, rows], land_ref.at[2 * cx + cy, rows], (cx, cy, c))
                for cx, cy in _other_chips(x, y)]
    return pieces


def rs_pieces(x_ref, land_ref, x, y, c):
    return [(x_ref.at[2 * cx + cy], land_ref.at[j], land_ref.at[j], (cx, cy, c))
            for j, (cx, cy) in enumerate(_other_chips(x, y))]


def ag_forward(name, got):
    _, rr, cc = got.shape
    h, nq = rr // 2, D2D_CHUNKS
    hq = h // nq

    def body(g_ref, out_ref, ssem, rsem):
        x, y, c = _pos()
        slots = [2 * cx + cy for cx, cy in _other_chips(x, y)]
        cps = []
        for j, s in enumerate(slots):
            for q in range(nq):
                blk = out_ref.at[s, _rows_at(c * h + q * hq, hq)]
                cp = _rcopy(blk, blk, ssem.at[j * nq + q], rsem.at[j * nq + q], (x, y, 1 - c))
                cp.start()
                cps.append(cp)
        for cp in cps:
            cp.wait_send()
        for j, s in enumerate(slots):
            for q in range(nq):
                blk = out_ref.at[s, _rows_at((1 - c) * h + q * hq, hq)]
                _rcopy(blk, blk, ssem.at[j * nq + q], rsem.at[j * nq + q], (x, y, 1 - c)).wait_recv()

    return pl.pallas_call(
        body, name=name, out_shape=jax.ShapeDtypeStruct(got.shape, got.dtype), in_specs=[ANY], out_specs=ANY,
        scratch_shapes=[_dma_sems(3 * nq), _dma_sems(3 * nq)], input_output_aliases={0: 0},
        compiler_params=pltpu.CompilerParams(has_side_effects=True),
    )(got)


def rs_pair(name, g):
    _, rr, cc = g.shape
    h, nq = rr // 2, D2D_CHUNKS
    hq = h // nq

    def body(g_ref, recv_ref, ssem, rsem):
        x, y, c = _pos()
        cps = []
        for q in range(nq):
            cp = _rcopy(g_ref.at[:, _rows_at((1 - c) * h + q * hq, hq), :], recv_ref.at[:, pl.ds(q * hq, hq), :],
                        ssem.at[q], rsem.at[q], (x, y, 1 - c))
            cp.start()
            cps.append(cp)
        for cp in cps:
            cp.wait()

    return _comm_call(body, name, jax.ShapeDtypeStruct((4, h, cc), g.dtype), 1, [_dma_sems(nq), _dma_sems(nq)])(g)


def rs_chips(name, p):
    _, h, cc = p.shape
    nq = ICI_CHUNKS
    hq = h // nq

    def body(p_ref, buf_ref, ssem, rsem):
        x, y, c = _pos()
        sends = []
        for q in range(nq):
            rows = pl.ds(q * hq, hq)
            for j, (cx, cy) in enumerate(_other_chips(x, y)):
                cp = _rcopy(p_ref.at[2 * cx + cy, rows], buf_ref.at[j, rows], ssem.at[j * nq + q],
                            rsem.at[j * nq + q], (cx, cy, c))
                cp.start()
                sends.append(cp)
        for cp in sends:
            cp.wait()

    return _comm_call(body, name, jax.ShapeDtypeStruct((3, h, cc), p.dtype), 1,
                      [_dma_sems(3 * nq), _dma_sems(3 * nq)])(p)


def rs_join(name, half):
    h, cc = half.shape
    nq = D2D_CHUNKS
    hq = h // nq

    def body(h_ref, out_ref, ssem, rsem):
        x, y, c = _pos()
        cps = []
        for q in range(nq):
            rows = pl.ds(q * hq, hq)
            cp = _rcopy(h_ref.at[rows], out_ref.at[rows], ssem.at[q], rsem.at[q], (x, y, 1 - c))
            cp.start()
            cps.append(cp)
        for cp in cps:
            cp.wait()

    return _comm_call(body, name, jax.ShapeDtypeStruct((h, cc), half.dtype), 1, [_dma_sems(nq), _dma_sems(nq)])(half)


def reduce_scatter(tag, g, tb, sp):
    return rs_end(rs_begin(tag, g, tb, sp, False), None)


def rs_begin(tag, g, tb, sp, split, after=None):
    _, rr, cc = g.shape
    h = rr // 2
    nbh = h // tb
    recv = rs_pair(tag + "_pair", g)
    mine_rows = lambda i, s: (i // nbh) * (2 * nbh) + s[0] * nbh + i % nbh
    part = rowwise(add2_fn, tag + "_add", 4 * h, tb, [R(g.reshape(4 * rr, cc), off=mine_rows), R(recv.reshape(4 * h, cc))],
                   [], [(cc, BF16)], sp=sp)[0].reshape(4, h, cc)
    st = dict(tag=tag, tb=tb, sp=sp, split=split, part=part)
    if split:
        st["sems"], st["part"], st["land"], st["token"] = split_copy_start(tag + "_start", part, (3, h, cc), rs_pieces,
                                                                           sp if after is None else after)
    return st


def rs_end(st, after):
    tag, tb, sp, part = st["tag"], st["tb"], st["sp"], st["part"]
    _, h, cc = part.shape
    nbh = h // tb
    if st["split"]:
        part, buf = split_copy_wait(tag + "_wait", st["sems"], part, st["land"], after, rs_pieces)
    else:
        buf = rs_chips(tag + "_chips", part)
    red = rowwise(sum4_fn, tag + "_sum", h, tb,
                  [R(part.reshape(4 * h, cc), off=lambda i, s: s[1] * nbh + i)]
                  + [R(buf.reshape(3 * h, cc), off=k * nbh) for k in range(3)],
                  [], [(cc, F32)], sp=sp)[0]
    return red, rs_join(tag + "_join", red)


def adam_halves(name, w, m, v, red, other, tb, blk0, sp):
    nbh = red.shape[0] // tb

    def fn(i, n, s, w_, m_, v_, r_, o_):
        g = jnp.where((blk0 + i) // nbh == s[0], r_, o_)
        return (g,) + _adamw(w_, g, m_, v_)

    half_rows = lambda i, s: (blk0 + i) % nbh
    return rowwise(fn, name, w.shape[0], tb, [R(w), R(m), R(v), R(red, off=half_rows), R(other, off=half_rows)],
                   [], [(w.shape[1], F32)] * 4, sp=sp)


SMALL_LANES = 3 * D


def all_reduce_items(name, items, after=None):
    flat = [a for it in items for a in it]
    shapes = [(sum(a.shape[0] for a in it), it[0].shape[1]) for it in items]
    nrows = -(-sum(s[0] for s in shapes) // 8) * 8
    extra = [] if after is None else [after]

    def body(*refs):
        ins, refs = refs[:len(flat)], refs[len(flat) + len(extra):]
        outs = refs[:len(items)]
        mine, buf, ssem, rsem = refs[len(items):]
        x, y, c = _pos()
        me = 4 * x + 2 * y + c
        mine[...] = jnp.zeros_like(mine)
        r = 0
        for ref in ins:
            mine[r:r + ref.shape[0], 0:ref.shape[1]] = ref[...]
            r += ref.shape[0]
        buf[me] = mine[...]
        cps = []
        for k in range(1, 8):
            dev = (x ^ (k >> 2), y ^ ((k >> 1) & 1), c ^ (k & 1))
            cp = _rcopy(mine, buf.at[me], ssem.at[k - 1], rsem.at[k - 1], dev)
            cp.start()
            cps.append(cp)
        for cp in cps:
            cp.wait()
        r = 0
        for (nr, n), out in zip(shapes, outs):
            acc = buf[0, r:r + nr, 0:n]
            for d in range(1, 8):
                acc = acc + buf[d, r:r + nr, 0:n]
            out[...] = acc
            r += nr

    vm = pl.BlockSpec(memory_space=pltpu.VMEM)
    return pl.pallas_call(
        body, name=name, out_shape=[jax.ShapeDtypeStruct(s, F32) for s in shapes],
        in_specs=[vm] * len(flat) + [ANY] * len(extra), out_specs=[vm] * len(items),
        scratch_shapes=[pltpu.VMEM((nrows, SMALL_LANES), F32), pltpu.VMEM((8, nrows, SMALL_LANES), F32),
                        _dma_sems(7), _dma_sems(7)],
        compiler_params=pltpu.CompilerParams(has_side_effects=True),
    )(*flat, *extra)


def adam_small(ws, gs, ms, vs):
    n = len(ws)

    def body(*refs):
        for k in range(n):
            w, g, m, v = (refs[j * n + k][...] for j in range(4))
            for j, val in enumerate(_adamw(w, g, m, v)):
                refs[(4 + j) * n + k][...] = val

    vm = pl.BlockSpec(memory_space=pltpu.VMEM)
    res = pl.pallas_call(
        body, name="adam_small", out_shape=[jax.ShapeDtypeStruct(w.shape, F32) for w in ws] * 3,
        in_specs=[vm] * (4 * n), out_specs=[vm] * (3 * n),
    )(*ws, *gs, *ms, *vs)
    return res[:n], res[n:2 * n], res[2 * n:]


def _sel(rows, cols, pairs):
    m = np.zeros((rows, cols), np.float32)
    for r, c in pairs:
        m[r, c] = 1.0
    return jnp.asarray(m)


def _pad_win(w):
    z = jnp.zeros((w.shape[0], 112), w.dtype)
    return jnp.concatenate([w[:, :4096], w[:, 4112:6672], w[:, 4096:4112], z, w[:, 6672:6688], z], axis=1)


def _unpad_win(wp):
    return jnp.concatenate([wp[:, :4096], wp[:, 6656:6672], wp[:, 4096:6656], wp[:, 6784:6800]], axis=1)


def kernel(x, mem, norm1_w, w_in, gdn_conv_w, gdn_a_log, gdn_dt_bias, gdn_norm_w, ssm_conv_w, ssm_conv_b, ssm_a_log, ssm_dt_bias, ssm_d, ssm_norm_w, w_out, norm2_w, mem_norm_w, wq_mem, wk_mem, wv_mem, wo_mem, norm3_w, w_up, w_down, final_norm_w, loss_target, m_norm1_w, m_w_in, m_gdn_conv_w, m_gdn_a_log, m_gdn_dt_bias, m_gdn_norm_w, m_ssm_conv_w, m_ssm_conv_b, m_ssm_a_log, m_ssm_dt_bias, m_ssm_d, m_ssm_norm_w, m_w_out, m_norm2_w, m_mem_norm_w, m_wq_mem, m_wk_mem, m_wv_mem, m_wo_mem, m_norm3_w, m_w_up, m_w_down, m_final_norm_w, v_norm1_w, v_w_in, v_gdn_conv_w, v_gdn_a_log, v_gdn_dt_bias, v_gdn_norm_w, v_ssm_conv_w, v_ssm_conv_b, v_ssm_a_log, v_ssm_dt_bias, v_ssm_d, v_ssm_norm_w, v_w_out, v_norm2_w, v_mem_norm_w, v_wq_mem, v_wk_mem, v_wv_mem, v_wo_mem, v_norm3_w, v_w_up, v_w_down, v_final_norm_w):
    T, M = x.shape[1], mem.shape[1]
    xi, yi, ci = _pos()
    s_me = 2 * xi + yi
    x0, mem0, tgt = x[0], mem[0], loss_target[0]
    tb = min(512, T)
    tbl = min(1024, T)
    tbp = min(512, T)
    row = lambda v: v.reshape(1, -1)

    win_g = all_gather_chips("ag_win", w_in.astype(BF16), s_me)
    w_in_p = _pad_win(win_g.transpose(1, 0, 2).reshape(D, IN_COLS))
    keep = (ci == 0).astype(F32)
    gcw_z = lax.dynamic_update_slice(jnp.zeros((4, 3 * D), F32), gdn_conv_w * keep, (0, s_me * 768))
    scw_z = lax.dynamic_update_slice(jnp.zeros((4, 1536), F32), ssm_conv_w * keep, (0, s_me * 384))
    gcw, scw = all_reduce_items("ar_convw", [[gcw_z], [scw_z]])
    scw_x, scw_bc = scw[:, :D], scw[:, D:]
    rest_shard = jnp.concatenate([w_up, w_down, w_out, wq_mem, wk_mem, wv_mem, wo_mem], axis=0).astype(BF16)
    ag_sems, rest_thru, rest_land, ag_token = split_copy_start("ag_rest_start", rest_shard, (4,) + rest_shard.shape,
                                                               ag_pieces(rest_shard.shape[0] // 2), gcw)
    sp = jnp.stack([ci, s_me]).astype(jnp.int32)
    scb_x, scb_bc = row(ssm_conv_b[:D]), row(ssm_conv_b[D:])

    galog_c, gdtb_c = row(jnp.pad(gdn_a_log, (8, 112))), row(jnp.pad(gdn_dt_bias, (8, 112)))
    salog_c, sdtb_c = row(jnp.pad(ssm_a_log, (0, 112))), row(jnp.pad(ssm_dt_bias, (0, 112)))
    sd_x = row(jnp.repeat(ssm_d, 64))
    eb = _sel(128, D, [(h, 128 * h + l) for h in range(8) for l in range(128)])
    ea = _sel(128, D, [(8 + h, 128 * h + l) for h in range(8) for l in range(128)])
    e16 = _sel(128, D, [(h, 64 * h + l) for h in range(16) for l in range(64)])

    h1 = rowwise(rms_fwd_fn, "rms1", T, tbl, [R(x0)], [row(norm1_w) + ag_token[0:1, 0:1]], [(D, BF16)])[0]
    p = matmul("mm_in", h1, w_in_p, "nn", 2048, 768, 1024, [F32])[0]
    gp_ins = [R(p, 3 * D, CB_QKV, "prev"), R(p, 128, CB_BA)]
    qn, kn, vv, gcs_x, beta_x, ggate, gcs_t = rowwise(gdn_prep_fn, "gdn_prep", T, tbp, gp_ins,
                                                      [gcw, galog_c, gdtb_c, eb, ea],
                                                      [(D, F32)] * 5 + [(128, F32), (-8, F32)])
    gcs_t = gcs_t.reshape(GDN_H, 1, T)
    gtb, ggh = min(128, T), 8
    o_gdn, s_states, tinv = gdn_fwd(qn, kn, vv, gcs_x, beta_x, gcs_t, gtb, ggh)
    gnw = row(gdn_norm_w)
    oa = rowwise(gdn_post_fn, "gdn_post", T, tbl, [R(o_gdn), R(p, D, CB_Z)], [gnw], [(D, BF16)])[0]
    sp_ins = [R(p, D, CB_XS, "prev"), R(p, 512, CB_BC, "prev"), R(p, 128, CB_DT)]
    sp_full = [scw_x, scw_bc, scb_x, scb_bc, salog_c, sdtb_c]
    xs, bc, dt_x, acs_x, acs_t = rowwise(ssd_prep_fn, "ssd_prep", T, tbp, sp_ins, sp_full + [e16],
                                         [(D, F32), (512, F32), (D, F32), (D, F32), (-SSM_H, F32)])
    acs_t = acs_t.reshape(SSM_H, 1, T)
    y_ssd, h_states = ssd_fwd(xs, bc, dt_x, acs_x, acs_t)
    snw = row(ssm_norm_w)
    ob = rowwise(ssd_post_fn, "ssd_post", T, tbl, [R(y_ssd), R(xs), R(p, D, CB_ZS)], [sd_x, snw], [(D, BF16)])[0]
    rest_thru, rest_land = split_copy_wait("ag_rest_wait", ag_sems, rest_thru, rest_land, ob,
                                           ag_pieces(rest_shard.shape[0] // 2))
    rest_g = _with_own(rest_thru, ag_forward("ag_rest_fwd", rest_land), s_me)
    assert D == 1024
    view = lambda shape, blk, at: dict(b_sel=(shape, blk, at))
    wup_n = view((D, D_FF), (None, D, D), lambda i, j, k: (j, 0, 0))
    wup_t = view((D, D_FF), (None, D, D), lambda i, j, k: (k, 0, 0))
    wdown_n = view((D_FF, D), (None, D, D), lambda i, j, k: (k, 1, 0))
    wdown_t = view((D_FF, D), (None, D, D), lambda i, j, k: (j, 1, 0))
    wout_a = view((D, D), (2, 512, D), lambda i, j, k: (0, 4, 0))
    wout_b = view((D, D), (2, 512, D), lambda i, j, k: (1, 4, 0))
    wq_v, wk_v, wv_v, wo_v = (view((D, D), (4, 256, D), lambda i, j, k, r=r: (0, r, 0)) for r in (10, 11, 12, 13))
    x1a = matmul("mm_out_a", oa, rest_g, "nn", 1024, 1024, 1024, [F32], _epi_res, [x0], **wout_a)[0]
    x1, h2 = matmul("mm_out_b", ob, rest_g, "nn", 1024, 1024, 1024, [F32, BF16], _epi_res_rms, [x1a],
                    [row(norm2_w)], **wout_b)

    mn = rowwise(rms_fwd_fn, "rms_mem", M, M, [R(mem0)], [row(mem_norm_w)], [(D, BF16)])[0]
    km = matmul("mm_k", mn, rest_g, "nn", 256, 1024, 1024, [BF16], **wk_v)[0]
    vm = matmul("mm_v", mn, rest_g, "nn", 256, 1024, 1024, [BF16], **wv_v)[0]
    qm = matmul("mm_q", h2, rest_g, "nn", 1024, 1024, 1024, [BF16], **wq_v)[0]
    ao = rowwise(attn_fn, "attn", T, tbl, [R(qm)], [km, vm], [(D, BF16)])[0]
    x2, h3 = matmul("mm_o", ao, rest_g, "nn", 1024, 1024, 1024, [F32, BF16], _epi_res_rms, [x1], [row(norm3_w)], **wo_v)
    u, act = matmul("mm_up", h3, rest_g, "nn", 2048, 1024, 1024, [BF16, BF16], _epi_relu2, **wup_n)
    wdown_n2 = view((D_FF, D), (2, D, D), lambda i, j, k: (k, 1, 0))
    x3 = matmul("mm_down", act, rest_g, "nn", 1024, 1024, 2048, [F32], _epi_res, [x2], **wdown_n2)[0]
    dx3, dx3b, loss_lane, g_final = rowwise(final_fn, "final", T, tbl, [R(x3), R(tgt)], [row(final_norm_w)],
                                            [(D, F32), (D, BF16)], [(1, D), (1, D)])
    loss = lax.psum(0.5 / D * jnp.sum(loss_lane), ("x", "y", "c"))

    dup = matmul("mm_dact", dx3b, rest_g, "nt", 2048, 1024, 1024, [BF16], _epi_dup, [u], **wdown_t)[0]
    def g_into(buf, blk, at):
        return dict(into=(buf, blk, lambda i, j, k, at=at: at(i, j)))

    grest = jax.ShapeDtypeStruct((4, 3584, D), F32)
    grest = matmul("mm_gdown", act, dx3b, "tn", 1024, 1024, 4096, [F32],
                   **g_into(grest, (None, 1024, D), lambda i, j: (i, 1, 0)))
    wup_t4 = dict(b_sel=((D, D_FF), (4, D, D), lambda i, j, k: (0, 0, 0), "side by side"))
    dh3 = matmul("mm_dh3", dup, rest_g, "nt", 1024, 1024, 4096, [F32], **wup_t4)[0]
    dx2, dx2b, g_n3 = rowwise(rms_bwd_fn, "rms3_bwd", T, tbl, [R(x2), R(dh3), R(dx3)], [row(norm3_w)],
                              [(D, F32), (D, BF16)], [(1, D)])
    grest = matmul("mm_gup", h3, dup, "tn", 1024, 1024, 4096, [F32],
                   **g_into(grest, (None, 1024, D), lambda i, j: (j, 0, 0)))
    dao = matmul("mm_dao", dx2b, rest_g, "nt", 1024, 1024, 1024, [F32], **wo_v)[0]
    grest = matmul("mm_gwo", ao, dx2b, "tn", 1024, 1024, 2048, [F32],
                   **g_into(grest, (4, 256, D), lambda i, j: (0, 13, 0)))
    dqm, dkm, dvm = rowwise(attn_bwd_fn, "attn_bwd", T, tb, [R(qm), R(dao)], [km, vm], [(D, BF16)],
                            [(M, D), (M, D)])
    dx1, dx1b, g_n2 = matmul("mm_dh2", dqm, rest_g, "nt", 512, 1024, 1024, [F32, BF16], _epi_rms_bwd, [x1, dx2],
                             [row(norm2_w)], n_acc=1, **wq_v)
    grest = matmul("mm_gwq", h2, dqm, "tn", 1024, 1024, 2048, [F32],
                   **g_into(grest, (4, 256, D), lambda i, j: (0, 10, 0)))
    grest = matmul("mm_gwk", mn, dkm, "tn", 1024, 1024, 256, [F32],
                   **g_into(grest, (4, 256, D), lambda i, j: (0, 11, 0)))
    grest = matmul("mm_gwv", mn, dvm, "tn", 1024, 1024, 256, [F32],
                   **g_into(grest, (4, 256, D), lambda i, j: (0, 12, 0)))
    dmn_k = matmul("mm_dmk", dkm, rest_g, "nt", 256, 1024, 1024, [F32], **wk_v)[0]
    dmn = matmul("mm_dmv", dvm, rest_g, "nt", 256, 1024, 1024, [F32], _epi_res, [dmn_k], **wv_v)[0]
    g_nmem = rowwise(rms_bwd_w_fn, "rmsmem_bwd", M, M, [R(mem0), R(dmn)], [row(mem_norm_w)], [], [(1, D)])[0]
    doa = matmul("mm_doa", dx1b, rest_g, "nt", 2048, 1024, 1024, [F32], **wout_a)[0]
    dob = matmul("mm_dob", dx1b, rest_g, "nt", 2048, 1024, 1024, [F32], **wout_b)[0]
    grest = matmul("mm_gwout_a", oa, dx1b, "tn", 1024, 1024, 2048, [F32],
                   **g_into(grest, (2, 512, D), lambda i, j: (0, 4, 0)))
    grest = matmul("mm_gwout_b", ob, dx1b, "tn", 1024, 1024, 2048, [F32],
                   **g_into(grest, (2, 512, D), lambda i, j: (1, 4, 0)))

    rs_rest = rs_begin("rs_rest", grest, 256, sp, True)

    dp = jax.ShapeDtypeStruct((T, p.shape[1]), BF16)
    dy_ssd, dxs_dir, dp, g_snw, g_sd_lane = rowwise(
        ssd_post_bwd_fn, "ssd_post_bwd", T, tb, [R(y_ssd), R(xs), R(p, D, CB_ZS), R(dob)],
        [sd_x + rs_rest["token"][0:1, 0:1], snw],
        [(D, F32), (D, F32), (D, BF16, dp, CB_ZS)], [(1, D), (1, D)])
    dxs_scan, db_s, dc_s, dgate, dacs_t = ssd_bwd(xs, bc, dt_x, acs_x, acs_t, dy_ssd, h_states)
    spb = rowwise(ssd_prep_bwd_fn, "ssd_prep_bwd", T, tbp,
                  sp_ins + [R(dxs_scan), R(dxs_dir), R(db_s), R(dc_s), R(dgate), RC(dacs_t.reshape(SSM_H, T))], sp_full,
                  [(D, F32), (512, F32), (128, BF16, dp, CB_DT)],
                  [(1, D)] * 4 + [(1, 512)] * 4 + [(1, D), (1, 512), (1, 128), (1, 128)])
    dyc_x, dyc_bc, dp = spb[:3]
    dp = rowwise(conv_bwd_fn, "conv_bwd_x", T, tbp, [R(dyc_x, halo="next")], [scw_x], [(D, BF16, dp, CB_XS)])[0]
    dp = rowwise(conv_bwd_fn, "conv_bwd_bc", T, tbp, [R(dyc_bc, halo="next")], [scw_bc], [(512, BF16, dp, CB_BC)])[0]

    do_gdn, dp, g_gnw = rowwise(gdn_post_bwd_fn, "gdn_post_bwd", T, tb, [R(o_gdn), R(p, D, CB_Z), R(doa)], [gnw],
                                [(D, F32), (D, BF16, dp, CB_Z)], [(1, 128)])
    dqn, dkn, dvv, dggate, dgcs_t = gdn_bwd(qn, kn, vv, ggate, gcs_t, do_gdn, s_states, tinv, gtb, ggh)
    gpb = rowwise(gdn_prep_bwd_fn, "gdn_prep_bwd", T, tbp,
                  gp_ins + [R(dqn), R(dkn), R(dvv), R(dggate), RC(dgcs_t.reshape(GDN_H, T))],
                  [gcw, galog_c, gdtb_c],
                  [(3 * D, F32), (128, BF16, dp, CB_BA)], [(1, 3 * D)] * 4 + [(1, 128), (1, 128)])
    dyc_qkv, dp = gpb[:2]
    dp = rowwise(conv_bwd_fn, "conv_bwd_qkv", T, tbp, [R(dyc_qkv, halo="next")], [gcw], [(3 * D, BF16, dp, CB_QKV)])[0]
    dh1 = matmul("mm_dh1", dp, w_in_p, "nt", 1024, 1024, 2304, [F32])[0]
    grad_x, g_n1 = rowwise(rms_bwd1_fn, "rms1_bwd", T, tbl, [R(x0), R(dh1), R(dx1)], [row(norm1_w)], [(D, F32)], [(1, D)])
    g_win_p = matmul("mm_gwin", h1, dp, "tn", 1024, 768, 4096, [F32])[0]

    items = [[g_n1], [gpb[6]], [gpb[7]], [g_gnw], [spb[11]], [spb[12]], [spb[13]], [spb[14]], [g_sd_lane], [g_snw],
             [g_n2], [g_nmem], [g_n3], [g_final], list(gpb[2:6]), list(spb[3:7]), list(spb[7:11])]
    (gr_n1, r_galog, r_gdtb, gr_gnw, r_scb_x, r_scb_bc, r_salog, r_sdtb, r_sd, gr_snw, gr_n2, gr_nmem, gr_n3,
     gr_final, r_gcw, r_scw_x, r_scw_bc) = all_reduce_items("ar_grads", items)
    gr_galog, gr_gdtb = r_galog[:, 8:16], r_gdtb[:, 8:16]
    gr_salog, gr_sdtb = r_salog[:, :SSM_H], r_sdtb[:, :SSM_H]
    gr_sd = r_sd.reshape(SSM_H, SSM_P).sum(axis=1).reshape(1, SSM_H)
    gr_scb = jnp.concatenate([r_scb_x, r_scb_bc], axis=1)
    gr_gcw = lax.dynamic_slice(r_gcw, (0, s_me * 768), (4, 768))
    gr_scw = lax.dynamic_slice(jnp.concatenate([r_scw_x, r_scw_bc], axis=1), (0, s_me * 384), (4, 384))

    g_win = _unpad_win(g_win_p).reshape(D, 4, IN_COLS // 4).transpose(1, 0, 2)
    rs_win = rs_begin("rs_win", g_win, 256, sp, True, gr_n1)
    red_r, oth_r = rs_end(rs_rest, rs_win["token"])

    big = {}
    for n, w, m, v, blk0 in (("w_up", w_up, m_w_up, v_w_up, 0), ("w_down", w_down, m_w_down, v_w_down, 4),
                             ("w_out", w_out, m_w_out, v_w_out, 8), ("wq_mem", wq_mem, m_wq_mem, v_wq_mem, 10),
                             ("wk_mem", wk_mem, m_wk_mem, v_wk_mem, 11), ("wv_mem", wv_mem, m_wv_mem, v_wv_mem, 12),
                             ("wo_mem", wo_mem, m_wo_mem, v_wo_mem, 13)):
        big[n] = adam_halves("adam_" + n, w, m, v, red_r, oth_r, 256, blk0, sp)
    red_w, oth_w = rs_end(rs_win, big["wo_mem"][1])
    big["w_in"] = adam_halves("adam_win", w_in, m_w_in, v_w_in, red_w, oth_w, 256, 0, sp)
    names_s =["norm1_w", "gdn_conv_w", "gdn_a_log", "gdn_dt_bias", "gdn_norm_w", "ssm_conv_w", "ssm_conv_b",
               "ssm_a_log", "ssm_dt_bias", "ssm_d", "ssm_norm_w", "norm2_w", "mem_norm_w", "norm3_w", "final_norm_w"]
    w_s = [norm1_w, gdn_conv_w, gdn_a_log, gdn_dt_bias, gdn_norm_w, ssm_conv_w, ssm_conv_b, ssm_a_log, ssm_dt_bias,
           ssm_d, ssm_norm_w, norm2_w, mem_norm_w, norm3_w, final_norm_w]
    g_s = [gr_n1, gr_gcw, gr_galog, gr_gdtb, gr_gnw, gr_scw, gr_scb, gr_salog, gr_sdtb, gr_sd, gr_snw, gr_n2,
           gr_nmem, gr_n3, gr_final]
    m_s = [m_norm1_w, m_gdn_conv_w, m_gdn_a_log, m_gdn_dt_bias, m_gdn_norm_w, m_ssm_conv_w, m_ssm_conv_b, m_ssm_a_log,
           m_ssm_dt_bias, m_ssm_d, m_ssm_norm_w, m_norm2_w, m_mem_norm_w, m_norm3_w, m_final_norm_w]
    v_s = [v_norm1_w, v_gdn_conv_w, v_gdn_a_log, v_gdn_dt_bias, v_gdn_norm_w, v_ssm_conv_w, v_ssm_conv_b, v_ssm_a_log,
           v_ssm_dt_bias, v_ssm_d, v_ssm_norm_w, v_norm2_w, v_mem_norm_w, v_norm3_w, v_final_norm_w]
    shp_s = [w.shape for w in w_s]
    as2d = lambda a: a if a.ndim == 2 else a.reshape(1, -1)
    d_l, m_l, v_l = adam_small([as2d(a) for a in w_s], [as2d(a) for a in g_s], [as2d(a) for a in m_s],
                               [as2d(a) for a in v_s])

    grads, deltas, new_m, new_v = {}, {}, {}, {}
    for n, (gg, dd, mm_, vv_) in big.items():
        grads[n], deltas[n], new_m[n], new_v[n] = gg, dd, mm_, vv_
    for k, n in enumerate(names_s):
        grads[n] = g_s[k].reshape(shp_s[k])
        deltas[n], new_m[n], new_v[n] = (a[k].reshape(shp_s[k]) for a in (d_l, m_l, v_l))
    order = ["norm1_w", "w_in", "gdn_conv_w", "gdn_a_log", "gdn_dt_bias", "gdn_norm_w", "ssm_conv_w", "ssm_conv_b",
             "ssm_a_log", "ssm_dt_bias", "ssm_d", "ssm_norm_w", "w_out", "norm2_w", "mem_norm_w", "wq_mem", "wk_mem",
             "wv_mem", "wo_mem", "norm3_w", "w_up", "w_down", "final_norm_w"]
    return (loss, grad_x[None], *[grads[n] for n in order], *[deltas[n] for n in order],
            *[new_m[n] for n in order], *[new_v[n] for n in order])
```

```python
import numpy as np
import jax
import jax.numpy as jnp
from jax import lax
from jax.experimental import pallas as pl
from jax.experimental.pallas import tpu as pltpu

F32, BF16 = jnp.float32, jnp.bfloat16
MESH = pl.DeviceIdType.MESH
ANY = pl.BlockSpec(memory_space=pl.ANY)

EPS = 1e-6
D = 1024
GDN_H, GDN_DK, GDN_C = 8, 128, 64
SSM_H, SSM_P, SSM_N, SSM_L = 16, 64, 128, 128
MEM_H, MEM_DH = 4, 256
D_FF = 4096
IN_COLS = 6688
CB_QKV, CB_Z, CB_ZS, CB_XS, CB_BC, CB_BA, CB_DT = 0, 3, 4, 5, 12, 52, 53
VMEM_LIMIT = 56 * 1024 * 1024
D2D_CHUNKS = 8
ICI_CHUNKS = 4

ADAM_LR, ADAM_B1, ADAM_B2, ADAM_EPS, ADAM_WD, ADAM_STEP = 0.001, 0.9, 0.999, 1e-08, 0.01, 10


def _dg(a, b, ca, cb):
    return lax.dot_general(a, b, (((ca,), (cb,)), ((), ())), preferred_element_type=F32)


def _bf(x):
    return x.astype(BF16)


def mm(a, b):
    return _dg(_bf(a), _bf(b), 1, 0)


def mm_nt(a, b):
    return _dg(_bf(a), _bf(b), 1, 1)


def mm_tn(a, b):
    return _dg(_bf(a), _bf(b), 0, 0)


def mm_sel(a, sel):
    hi = a.astype(BF16)
    r1 = a - hi.astype(F32)
    mid = r1.astype(BF16)
    lo = (r1 - mid.astype(F32)).astype(BF16)
    s = sel.astype(BF16)
    return _dg(hi, s, 1, 0) + (_dg(mid, s, 1, 0) + _dg(lo, s, 1, 0))


def mm3(a, b):
    ah, bh = a.astype(BF16), b.astype(BF16)
    al, bl = (a - ah.astype(F32)).astype(BF16), (b - bh.astype(F32)).astype(BF16)
    return _dg(ah, bh, 1, 0) + (_dg(ah, bl, 1, 0) + _dg(al, bh, 1, 0))


def _iota(shape, dim):
    return lax.broadcasted_iota(jnp.int32, shape, dim)


def _chunk_cumsum(x, c):
    pos = _iota(x.shape, 0) & (c - 1)
    s = 1
    while s < c:
        x = x + jnp.where(pos >= s, pltpu.roll(x, s, 0), 0.0)
        s *= 2
    return x


def _chunk_revcumsum(x, c):
    n = x.shape[0]
    pos = _iota(x.shape, 0) & (c - 1)
    s = 1
    while s < c:
        x = x + jnp.where(pos < c - s, pltpu.roll(x, n - s, 0), 0.0)
        s *= 2
    return x


def _sig(x):
    return jax.nn.sigmoid(x)


def _softplus(x):
    return jnp.maximum(x, 0.0) + jnp.log(1.0 + jnp.exp(-jnp.abs(x)))


def _rows(v):
    return jnp.sum(v, axis=0, keepdims=True)


def _lanes(v):
    return jnp.sum(v, axis=1, keepdims=True)


def _sum_all(v):
    return _rows(_lanes(v))


def _cparams(sem):
    return pltpu.CompilerParams(dimension_semantics=sem, vmem_limit_bytes=VMEM_LIMIT)


def rowwise(fn, name, T, tb, row_ins, full_ins, row_outs, acc_outs=(), sp=None):
    nblk = T // tb
    assert nblk * tb == T
    has_sp = sp is not None

    def imap(f):
        return (lambda i, s: f(i, s)) if has_sp else (lambda i: f(i, None))

    in_specs, args = [], []
    for arr, w, cb, halo, off in row_ins:
        if halo == "col":
            in_specs.append(pl.BlockSpec((w, tb), imap(lambda i, s: (0, i))))
            args.append(arr)
            continue
        rowf = off if callable(off) else (lambda i, s, off=off: i + off)
        in_specs.append(pl.BlockSpec((tb, w), imap(lambda i, s, cb=cb, rowf=rowf: (rowf(i, s), cb))))
        args.append(arr)
        if halo == "prev":
            r = tb // 8
            in_specs.append(pl.BlockSpec((8, w), imap(lambda i, s, cb=cb, r=r: (jnp.maximum(i * r - 1, 0), cb))))
            args.append(arr)
        elif halo == "next":
            r, last = tb // 8, T // 8 - 1
            in_specs.append(pl.BlockSpec((8, w), imap(lambda i, s, cb=cb, r=r, last=last:
                                                      (jnp.minimum((i + 1) * r, last), cb))))
            args.append(arr)
    for arr in full_ins:
        in_specs.append(pl.BlockSpec(arr.shape, imap(lambda i, s, nd=arr.ndim: (0,) * nd)))
        args.append(arr)
    n_in, n_ro = len(args), len(row_outs)
    out_shape, out_specs, aliases = [], [], {}
    for k, (w, dt, *dest) in enumerate(row_outs):
        if dest:
            buf, cb = dest
            out_shape.append(jax.ShapeDtypeStruct(buf.shape, buf.dtype))
            out_specs.append(pl.BlockSpec((tb, w), imap(lambda i, s, cb=cb: (i, cb))))
            if not isinstance(buf, jax.ShapeDtypeStruct):
                aliases[len(args) + int(has_sp)] = k
                in_specs.append(ANY)
                args.append(buf)
        elif w < 0:
            out_shape.append(jax.ShapeDtypeStruct((-w, T), dt))
            out_specs.append(pl.BlockSpec((-w, tb), imap(lambda i, s: (0, i))))
        else:
            out_shape.append(jax.ShapeDtypeStruct((T, w), dt))
            out_specs.append(pl.BlockSpec((tb, w), imap(lambda i, s: (i, 0))))
    for shp in acc_outs:
        out_shape.append(jax.ShapeDtypeStruct(shp, F32))
        out_specs.append(pl.BlockSpec(shp, imap(lambda i, s, nd=len(shp): (0,) * nd)))

    def body(*refs):
        i = pl.program_id(0)
        if has_sp:
            sp_ref, refs = refs[0], refs[1:]
            vals = fn(i, nblk, sp_ref, *[r[...] for r in refs[:n_in]])
        else:
            vals = fn(i, nblk, *[r[...] for r in refs[:n_in]])
        outs = refs[n_in + len(aliases):]
        for ref, val in zip(outs[:n_ro], vals[:n_ro]):
            ref[...] = val.astype(ref.dtype)
        for ref, val in zip(outs[n_ro:], vals[n_ro:]):
            @pl.when(i == 0)
            def _(ref=ref, val=val):
                ref[...] = val

            @pl.when(i > 0)
            def _(ref=ref, val=val):
                ref[...] += val

    cparams = _cparams(("arbitrary",) if acc_outs else ("parallel",))
    if has_sp:
        return pl.pallas_call(
            body, name=name, out_shape=out_shape, compiler_params=cparams, input_output_aliases=aliases,
            grid_spec=pltpu.PrefetchScalarGridSpec(num_scalar_prefetch=1, grid=(nblk,), in_specs=in_specs,
                                                   out_specs=out_specs),
        )(sp, *args)
    return pl.pallas_call(
        body, name=name, grid=(nblk,), in_specs=in_specs, out_specs=out_specs, out_shape=out_shape,
        compiler_params=cparams, input_output_aliases=aliases,
    )(*args)


def R(arr, w=None, cb=0, halo=None, off=0):
    return (arr, arr.shape[1] if w is None else w, cb, halo, off)


def RC(arr):
    return (arr, arr.shape[0], 0, "col", 0)


def matmul(name, a, b, form, tm, tn, tk, out_dtypes, epi=None, extras=(), rows=(), into=None, n_acc=0, b_sel=None):
    bs = b.shape if b_sel is None else b_sel[0]
    if form == "nn":
        (M, K), N = a.shape, bs[1]
    elif form == "nt":
        (M, K), N = a.shape, bs[0]
    else:
        (K, M), N = a.shape, bs[1]
    tm, tn, tk = min(tm, M), min(tn, N), min(tk, K)
    assert M % tm == 0 and N % tn == 0 and K % tk == 0, (name, M, N, K, tm, tn, tk)

    def b_spec_of(blk, at):
        if b_sel is None:
            return pl.BlockSpec(blk, lambda i, j, k: at(i, j, k))
        blk3 = b_sel[1]
        assert int(np.prod([d for d in blk3 if d is not None])) == blk[0] * blk[1], (name, blk3, blk)
        return pl.BlockSpec(blk3, lambda i, j, k: b_sel[2](i, j, k))

    if form == "nn":
        a_spec = pl.BlockSpec((tm, tk), lambda i, j, k: (i, k))
        b_spec = b_spec_of((tk, tn), lambda i, j, k: (k, j))
        ca, cb = 1, 0
    elif form == "nt":
        a_spec = pl.BlockSpec((tm, tk), lambda i, j, k: (i, k))
        b_spec = b_spec_of((tn, tk), lambda i, j, k: (j, k))
        ca, cb = 1, 1
    else:
        a_spec = pl.BlockSpec((tk, tm), lambda i, j, k: (k, i))
        b_spec = b_spec_of((tk, tn), lambda i, j, k: (k, j))
        ca, cb = 0, 0
    nk, ne, no = K // tk, len(extras) + len(rows), len(out_dtypes)
    if epi is None:
        epi = lambda acc: (acc,)

    assert n_acc == 0 or tn == N

    def body(a_ref, b_ref, *rest):
        ex, outs, accs, acc = rest[:ne], rest[ne:ne + no], rest[ne + no:ne + no + n_acc], rest[ne + no + n_acc]
        i, k = pl.program_id(0), pl.program_id(2)

        def finish(total):
            vals = epi(total, *[e[...] for e in ex])
            for r, v in zip(outs, vals[:no]):
                r[...] = v.astype(r.dtype).reshape(r.shape)
            for r, v in zip(accs, vals[no:]):
                @pl.when(i == 0)
                def _(r=r, v=v):
                    r[...] = v

                @pl.when(i > 0)
                def _(r=r, v=v):
                    r[...] += v

        b_tile = b_ref[...]
        if b_sel is not None and len(b_sel) > 3:
            b_tile = jnp.concatenate([b_tile[s] for s in range(b_tile.shape[0])], axis=1)
        prod = _dg(_bf(a_ref[...]), _bf(b_tile.reshape(-1, b_tile.shape[-1])), ca, cb)
        if nk == 1:
            finish(prod)
            return

        @pl.when(k == 0)
        def _():
            acc[...] = prod

        @pl.when(k > 0)
        def _():
            acc[...] += prod

        @pl.when(k == nk - 1)
        def _():
            finish(acc[...])

    mn = pl.BlockSpec((tm, tn), lambda i, j, k: (i, j))
    rw = pl.BlockSpec((1, tn), lambda i, j, k: (0, j))
    acc_scratch = pltpu.VMEM((tm, tn) if nk > 1 else (8, 128), F32)
    if into is not None:
        buf, blk, bmap = into
        assert ne == 0 and no == 1
        aliased = not isinstance(buf, jax.ShapeDtypeStruct)

        def body_into(a_ref, b_ref, *rest):
            body(a_ref, b_ref, *rest[-2:])

        return pl.pallas_call(
            body_into, name=name, grid=(M // tm, N // tn, nk),
            in_specs=[a_spec, b_spec] + ([ANY] if aliased else []), out_specs=pl.BlockSpec(blk, bmap),
            out_shape=jax.ShapeDtypeStruct(buf.shape, buf.dtype),
            scratch_shapes=[acc_scratch],
            input_output_aliases={2: 0} if aliased else {},
            compiler_params=_cparams(("parallel", "parallel", "arbitrary")),
        )(a, b, *([buf] if aliased else []))
    return pl.pallas_call(
        body, name=name, grid=(M // tm, N // tn, nk),
        in_specs=[a_spec, b_spec] + [mn] * len(extras) + [rw] * len(rows), out_specs=[mn] * no + [rw] * n_acc,
        out_shape=[jax.ShapeDtypeStruct((M, N), dt) for dt in out_dtypes] + [jax.ShapeDtypeStruct((1, N), F32)] * n_acc,
        scratch_shapes=[acc_scratch],
        compiler_params=_cparams(("arbitrary",) * 3 if n_acc else ("parallel", "parallel", "arbitrary")),
    )(a, b, *extras, *rows)


def _epi_res(acc, res):
    return (res + acc,)


def _epi_rms_bwd(acc, x, dres, w):
    return rms_bwd_fn(0, 0, x, acc, dres, w)


def rms_bwd1_fn(i, n, x, dh, dres, w):
    dx, _, gw = rms_bwd_fn(i, n, x, dh, dres, w)
    return dx, gw


def _epi_final(acc, res, tgt, w):
    return final_fn(0, 0, res + acc, tgt, w)


def _epi_res_rms(acc, res, w):
    x = res + acc
    return (x, x * lax.rsqrt(jnp.mean(x * x, axis=-1, keepdims=True) + EPS) * w)


def _epi_relu2(acc):
    u = jnp.maximum(acc, 0.0)
    return (u, u * u)


def _epi_dup(acc, u):
    return (acc * 2.0 * u.astype(F32),)


def _conv(x, halo, w, i):
    halo = jnp.where(i == 0, 0.0, halo)
    xt = jnp.concatenate([halo, x], axis=0)
    shifted = [pltpu.roll(xt, 3 - k, 0)[8:, :] for k in range(3)] + [x]
    y = shifted[3] * w[3:4, :]
    for k in range(3):
        y = y + shifted[k] * w[k:k + 1, :]
    return y, shifted


def _l2n(x, scale):
    outs = []
    for h in range(x.shape[1] // 128):
        xh = x[:, 128 * h:128 * h + 128]
        outs.append(xh * (lax.rsqrt(jnp.sum(xh * xh, axis=-1, keepdims=True) + EPS) * scale))
    return jnp.concatenate(outs, axis=1)


def _l2n_bwd(x, dy, scale):
    outs = []
    for h in range(x.shape[1] // 128):
        xh, dh = x[:, 128 * h:128 * h + 128], dy[:, 128 * h:128 * h + 128] * scale
        r = lax.rsqrt(jnp.sum(xh * xh, axis=-1, keepdims=True) + EPS)
        outs.append(r * dh - xh * (r * r * r) * jnp.sum(xh * dh, axis=-1, keepdims=True))
    return jnp.concatenate(outs, axis=1)


def rms_fwd_fn(i, n, x, w):
    r = lax.rsqrt(jnp.mean(x * x, axis=-1, keepdims=True) + EPS)
    return (x * r * w,)


def rms_bwd_fn(i, n, x, dh, dres, w):
    r = lax.rsqrt(jnp.mean(x * x, axis=-1, keepdims=True) + EPS)
    g = dh * w
    dx = dres + r * g - x * (r * r * r) * jnp.mean(x * g, axis=-1, keepdims=True)
    return dx, dx, _rows(dh * x * r)


def rms_bwd_w_fn(i, n, x, dh, w):
    r = lax.rsqrt(jnp.mean(x * x, axis=-1, keepdims=True) + EPS)
    return (_rows(dh * x * r),)


def final_fn(i, n, x, tgt, w):
    r = lax.rsqrt(jnp.mean(x * x, axis=-1, keepdims=True) + EPS)
    xn = x * r
    e = xn * w - tgt
    dy = e * (1.0 / D)
    g = dy * w
    dx = r * g - x * (r * r * r) * jnp.mean(x * g, axis=-1, keepdims=True)
    return dx, dx, _rows(e * e), _rows(dy * xn)


def _gdn_gates(ba, alog_c, dtb_c):
    col = _iota(ba.shape, 1)
    amask = (col >= 8) & (col < 16)
    beta = jnp.where(col < 8, _sig(ba), 0.0)
    z = ba + dtb_c
    ea_ = jnp.exp(alog_c)
    return beta, z, ea_, jnp.where(amask, -ea_ * _softplus(z), 0.0), amask


def _cols(x, g):
    return x[:, 128 * g:128 * g + 128]


def gdn_prep_fn(i, n, qkv, halo, ba, cw, alog_c, dtb_c, eb, ea):
    outs = [[], [], []]
    for g in range(3 * GDN_H):
        yc, _ = _conv(_cols(qkv, g), _cols(halo, g), _cols(cw, g), i)
        act = yc * _sig(yc)
        if g < 2 * GDN_H:
            act = _l2n(act, GDN_DK ** -0.5 if g < GDN_H else 1.0)
        outs[g // GDN_H].append(act)
    beta, _, _, gg, _ = _gdn_gates(ba, alog_c, dtb_c)
    gcs = _chunk_cumsum(gg, GDN_C)
    return (*[jnp.concatenate(o, axis=1) for o in outs], mm_sel(gcs, ea), mm_sel(beta, eb), beta + gcs,
            jnp.transpose(gcs)[8:16, :])


def gdn_prep_bwd_fn(i, n, qkv, halo, ba, dqn, dkn, dv, dgb, dgcs_t, cw, alog_c, dtb_c):
    dycs, dwl = [], [[], [], [], []]
    for g in range(3 * GDN_H):
        yc, shifted = _conv(_cols(qkv, g), _cols(halo, g), _cols(cw, g), i)
        sg = _sig(yc)
        act = yc * sg
        if g < GDN_H:
            d = _l2n_bwd(act, _cols(dqn, g), GDN_DK ** -0.5)
        elif g < 2 * GDN_H:
            d = _l2n_bwd(act, _cols(dkn, g - GDN_H), 1.0)
        else:
            d = _cols(dv, g - 2 * GDN_H)
        dyc_g = d * (sg * (1.0 + yc * (1.0 - sg)))
        dycs.append(dyc_g)
        for k in range(4):
            dwl[k].append(_rows(dyc_g * shifted[k]))
    dyc = jnp.concatenate(dycs, axis=1)
    dws = [jnp.concatenate(l, axis=1) for l in dwl]
    beta, z, ea_, g, amask = _gdn_gates(ba, alog_c, dtb_c)
    tbn = ba.shape[0]
    rowpart = jnp.transpose(jnp.concatenate([jnp.zeros((8, tbn), F32), dgcs_t, jnp.zeros((112, tbn), F32)], axis=0))
    dg = _chunk_revcumsum(jnp.where(amask, dgb, 0.0) - rowpart, GDN_C)
    draw = jnp.where(amask, dg * (-ea_) * _sig(z), 0.0)
    dba = draw + dgb * beta * (1.0 - beta)
    return (dyc, dba, dws[0], dws[1], dws[2], dws[3], _rows(dg * g), _rows(draw))


def conv_bwd_fn(i, n, dyc, halo, w):
    halo = jnp.where(i == n - 1, 0.0, halo)
    tb = dyc.shape[0]
    outs = []
    for g in range(dyc.shape[1] // 128):
        d, wg = _cols(dyc, g), _cols(w, g)
        xt = jnp.concatenate([d, _cols(halo, g)], axis=0)
        dx = d * wg[3:4, :]
        for k in range(3):
            dx = dx + pltpu.roll(xt, tb + 8 - (3 - k), 0)[:tb, :] * wg[k:k + 1, :]
        outs.append(dx)
    return (jnp.concatenate(outs, axis=1),)


def gdn_post_fn(i, n, o, z, w):
    outs = []
    for h in range(GDN_H):
        oh, zh = o[:, 128 * h:128 * h + 128], z[:, 128 * h:128 * h + 128]
        r = lax.rsqrt(jnp.mean(oh * oh, axis=-1, keepdims=True) + EPS)
        outs.append(oh * r * w * (zh * _sig(zh)))
    return (jnp.concatenate(outs, axis=1),)


def gdn_post_bwd_fn(i, n, o, z, doa, w):
    dos, dzs, dw = [], [], None
    for h in range(GDN_H):
        sl = slice(128 * h, 128 * h + 128)
        oh, zh, dh = o[:, sl], z[:, sl], doa[:, sl]
        r = lax.rsqrt(jnp.mean(oh * oh, axis=-1, keepdims=True) + EPS)
        s = _sig(zh)
        dn = dh * (zh * s)
        dzs.append(dh * (oh * r * w) * (s * (1.0 + zh * (1.0 - s))))
        t = _rows(dn * oh * r)
        dw = t if dw is None else dw + t
        g = dn * w
        dos.append(r * g - oh * (r * r * r) * jnp.mean(oh * g, axis=-1, keepdims=True))
    return jnp.concatenate(dos, axis=1), jnp.concatenate(dzs, axis=1), dw


def _ssd_gates(dtblk, alog_c, dtb_c):
    hmask = _iota(dtblk.shape, 1) < SSM_H
    z = dtblk + dtb_c
    return jnp.where(hmask, _softplus(z), 0.0), -jnp.exp(alog_c), z, hmask


def _silu_conv_cols(x, halo, w, b, i):
    outs = []
    for g in range(x.shape[1] // 128):
        yc, _ = _conv(_cols(x, g), _cols(halo, g), _cols(w, g), i)
        yc = yc + _cols(b, g)
        outs.append(yc * _sig(yc))
    return jnp.concatenate(outs, axis=1)


def _silu_conv_bwd_cols(x, halo, w, b, dout, i):
    dycs, dwl = [], [[], [], [], []]
    for g in range(x.shape[1] // 128):
        yc, shifted = _conv(_cols(x, g), _cols(halo, g), _cols(w, g), i)
        yc = yc + _cols(b, g)
        s = _sig(yc)
        dyc_g = _cols(dout, g) * (s * (1.0 + yc * (1.0 - s)))
        dycs.append(dyc_g)
        for k in range(4):
            dwl[k].append(_rows(dyc_g * shifted[k]))
    dyc = jnp.concatenate(dycs, axis=1)
    return dyc, [jnp.concatenate(l, axis=1) for l in dwl], _rows(dyc)


def ssd_prep_fn(i, n, xp, hx, bcp, hbc, dtblk, cwx, cwbc, cbx, cbbc, alog_c, dtb_c, e16):
    dt, a_neg, _, _ = _ssd_gates(dtblk, alog_c, dtb_c)
    acs = _chunk_cumsum(dt * a_neg, SSM_L)
    return (_silu_conv_cols(xp, hx, cwx, cbx, i), _silu_conv_cols(bcp, hbc, cwbc, cbbc, i), mm_sel(dt, e16),
            mm_sel(acs, e16), jnp.transpose(acs)[0:SSM_H, :])


def ssd_prep_bwd_fn(i, n, xp, hx, bcp, hbc, dtblk, dxs_a, dxs_b, db, dc, dgate, dacs_t, cwx, cwbc, cbx, cbbc, alog_c, dtb_c):
    dyx, dwx, dbx = _silu_conv_bwd_cols(xp, hx, cwx, cbx, dxs_a + dxs_b, i)
    dybc, dwbc, dbbc = _silu_conv_bwd_cols(bcp, hbc, cwbc, cbbc, jnp.concatenate([db, dc], axis=1), i)
    dt, a_neg, z, hmask = _ssd_gates(dtblk, alog_c, dtb_c)
    g0, g1 = dgate[:, :128], dgate[:, 128:]
    col = _iota(g0.shape, 1)
    lo, mid = col < 8, (col >= 8) & (col < 16)
    dacs_col = jnp.where(lo, g0, 0.0) + pltpu.roll(jnp.where(lo, g1, 0.0), 8, 1)
    ddt_dir = pltpu.roll(jnp.where(mid, g0, 0.0), 120, 1) + jnp.where(mid, g1, 0.0)
    tbn = dtblk.shape[0]
    rowpart = jnp.transpose(jnp.concatenate([dacs_t, jnp.zeros((128 - SSM_H, tbn), F32)], axis=0))
    da = _chunk_revcumsum(dacs_col - rowpart, SSM_L)
    draw = jnp.where(hmask, (ddt_dir + da * a_neg) * _sig(z), 0.0)
    return (dyx, dybc, draw, *dwx, *dwbc, dbx, dbbc, _rows(da * dt * a_neg), _rows(draw))


def _ssd_gate(y, xs, zs, d_x):
    y2 = y + xs * d_x
    s = _sig(zs)
    return y2, s, y2 * (zs * s)


def ssd_post_fn(i, n, y, xs, zs, d_x, nw):
    _, _, yg = _ssd_gate(y, xs, zs, d_x)
    outs = []
    for g in range(2):
        v = yg[:, 512 * g:512 * g + 512]
        outs.append(v * lax.rsqrt(jnp.mean(v * v, axis=-1, keepdims=True) + EPS))
    return (jnp.concatenate(outs, axis=1) * nw,)


def ssd_post_bwd_fn(i, n, y, xs, zs, dob, d_x, nw):
    y2, s, yg = _ssd_gate(y, xs, zs, d_x)
    gfull = dob * nw
    dygs, dnw = [], []
    for g in range(2):
        sl = slice(512 * g, 512 * g + 512)
        v, gg = yg[:, sl], gfull[:, sl]
        r = lax.rsqrt(jnp.mean(v * v, axis=-1, keepdims=True) + EPS)
        dygs.append(r * gg - v * (r * r * r) * jnp.mean(v * gg, axis=-1, keepdims=True))
        dnw.append(_rows(dob[:, sl] * v * r))
    dyg = jnp.concatenate(dygs, axis=1)
    dy2 = dyg * (zs * s)
    dzs = dyg * y2 * (s * (1.0 + zs * (1.0 - s)))
    return dy2, dy2 * d_x, dzs, jnp.concatenate(dnw, axis=1), _rows(dy2 * xs)


def _attn_probs(q, k):
    hs = [slice(MEM_DH * h, MEM_DH * h + MEM_DH) for h in range(MEM_H)]
    ss = [mm_nt(q[:, sl], k[:, sl]) * (MEM_DH ** -0.5) for sl in hs]
    es = [jnp.exp(s - jnp.max(s, axis=-1, keepdims=True)) for s in ss]
    return hs, [e / jnp.sum(e, axis=-1, keepdims=True) for e in es]


def attn_fn(i, n, q, k, v):
    hs, ps = _attn_probs(q, k)
    return (jnp.concatenate([mm(p, v[:, sl]) for p, sl in zip(ps, hs)], axis=1),)


def attn_bwd_fn(i, n, q, do, k, v):
    hs, ps = _attn_probs(q, k)
    dvs = [mm_tn(p, do[:, sl]) for p, sl in zip(ps, hs)]
    dps = [mm_nt(do[:, sl], v[:, sl]) for sl in hs]
    dss = [p * (dp - jnp.sum(dp * p, axis=-1, keepdims=True)) * (MEM_DH ** -0.5) for p, dp in zip(ps, dps)]
    dqs = [mm(ds, k[:, sl]) for ds, sl in zip(dss, hs)]
    dks = [mm_tn(ds, q[:, sl]) for ds, sl in zip(dss, hs)]
    return jnp.concatenate(dqs, axis=1), jnp.concatenate(dks, axis=1), jnp.concatenate(dvs, axis=1)


def add2_fn(i, n, sp, a, b):
    return (a + b,)


def sum4_fn(i, n, sp, a, b, c, d):
    return (((a.astype(F32) + b.astype(F32)) + c.astype(F32)) + d.astype(F32),)


def _adamw(w, g, m, v):
    m = ADAM_B1 * m + (1.0 - ADAM_B1) * g
    v = ADAM_B2 * v + (1.0 - ADAM_B2) * (g * g)
    m_hat = m / (1.0 - ADAM_B1 ** ADAM_STEP)
    v_hat = v / (1.0 - ADAM_B2 ** ADAM_STEP)
    delta = -ADAM_LR * (m_hat / (jnp.sqrt(v_hat) + ADAM_EPS) + ADAM_WD * w)
    return delta, m, v


def _gate_cols(gb, h):
    lane = _iota(gb.shape, 1)
    return _lanes(jnp.where(lane == h, gb, 0.0)), _lanes(jnp.where(lane == 8 + h, gb, 0.0))


def _gdn_stage1(q, k, v, bb, gcs, grow):
    C = GDN_C
    row, col = _iota((C, C), 0), _iota((C, C), 1)
    incl, strict = row >= col, row > col
    dmat = jnp.where(incl, jnp.exp(jnp.minimum((gcs if gcs.shape[1] == 1 else gcs[:, :C]) - grow, 0.0)), 0.0)
    gam = jnp.exp(gcs)
    gl = gcs[C - 1:C, :]
    kb, vb = k * bb, v * bb
    kg = kb * gam
    lmat = jnp.where(strict, mm_nt(kb, k) * dmat, 0.0)
    pmat = jnp.where(incl, mm_nt(q, k) * dmat, 0.0)
    return dict(q=q, k=k, v=v, bb=bb, incl=incl, strict=strict, dmat=dmat, gam=gam, kb=kb, vb=vb, kg=kg,
                lmat=lmat, pmat=pmat, qd=q * gam, kdec=jnp.exp(gl - gcs), cd=jnp.exp(gl))


def _gdn_inverse(lmats):
    C = GDN_C
    eye = (_iota((C, C), 0) == _iota((C, C), 1)).astype(F32)
    xs = [-l for l in lmats]
    ts = [eye + x for x in xs]
    for _ in range(5):
        xs = [mm(x, x) for x in xs]
        ts = [t + mm(t, x) for t, x in zip(ts, xs)]
    res = [eye - mm3(eye + l, t) for l, t in zip(lmats, ts)]
    return [t + mm(t, r) for t, r in zip(ts, res)]


def gdn_fwd(qn, kn, v, gcs_x, beta_x, gcs_t, tb, gh):
    T = qn.shape[0]
    nb, ncb, nc, C = T // tb, tb // GDN_C, T // GDN_C, GDN_C
    idx = [(hh, c) for hh in range(gh) for c in range(ncb)]

    def body(q_ref, k_ref, v_ref, g_ref, b_ref, gt_ref, o_ref, st_ref, ti_ref, s_scr):
        @pl.when(pl.program_id(1) == 0)
        def _():
            s_scr[...] = jnp.zeros_like(s_scr)

        grows = [gt_ref[hh] for hh in range(gh)]
        at = lambda hh, c: (slice(C * c, C * (c + 1)), slice(128 * hh, 128 * hh + 128))
        st1 = []
        for hh, c in idx:
            sl, ln = at(hh, c)
            st1.append(_gdn_stage1(q_ref[sl, ln], k_ref[sl, ln], v_ref[sl, ln], b_ref[sl, ln], g_ref[sl, ln],
                                   grows[hh][:, sl]))
        tinvs = _gdn_inverse([s["lmat"] for s in st1])
        us = [mm(t, s["vb"]) for t, s in zip(tinvs, st1)]
        ws = [mm(t, s["kg"]) for t, s in zip(tinvs, st1)]
        kds = [s["k"] * s["kdec"] for s in st1]
        ms = [mm_tn(kd, w) for kd, w in zip(kds, ws)]
        bs = [mm_tn(kd, u) for kd, u in zip(kds, us)]
        gs = [s["qd"] - mm(s["pmat"], w) for s, w in zip(st1, ws)]
        pus = [mm(s["pmat"], u) for s, u in zip(st1, us)]
        ss = [s_scr[hh] for hh in range(gh)]
        for c in range(ncb):
            for hh in range(gh):
                n, (sl, ln) = hh * ncb + c, at(hh, c)
                ti_ref[hh, sl, :] = tinvs[n]
                st_ref[hh, c] = ss[hh]
                o_ref[sl, ln] = mm(gs[n], ss[hh]) + pus[n]
                ss[hh] = st1[n]["cd"] * ss[hh] - mm(ms[n], ss[hh]) + bs[n]
        for hh in range(gh):
            s_scr[hh] = ss[hh]

    blk = pl.BlockSpec((tb, 128 * gh), lambda h, i: (i, h))
    return pl.pallas_call(
        body, name="gdn_fwd", grid=(GDN_H // gh, nb),
        in_specs=[blk] * 5 + [pl.BlockSpec((gh, 1, tb), lambda h, i: (h, 0, i))],
        out_specs=[blk, pl.BlockSpec((gh, ncb, 128, 128), lambda h, i: (h, i, 0, 0)),
                   pl.BlockSpec((gh, tb, C), lambda h, i: (h, i, 0))],
        out_shape=[jax.ShapeDtypeStruct((T, D), F32), jax.ShapeDtypeStruct((GDN_H, nc, 128, 128), F32),
                   jax.ShapeDtypeStruct((GDN_H, T, C), F32)],
        scratch_shapes=[pltpu.VMEM((gh, 128, 128), F32)],
        compiler_params=_cparams(("parallel", "arbitrary")),
    )(qn, kn, v, gcs_x, beta_x, gcs_t)


def gdn_bwd(qn, kn, v, gb, gcs_t, do, states, tinv, tb, gh):
    T = qn.shape[0]
    nb, ncb, C = T // tb, tb // GDN_C, GDN_C
    assert gh == GDN_H

    def body(q_ref, k_ref, v_ref, gb_ref, gt_ref, do_ref, st_ref, ti_ref,
             dq_ref, dk_ref, dv_ref, dgb_ref, dgr_ref, ds_scr):
        @pl.when(pl.program_id(1) == 0)
        def _():
            ds_scr[...] = jnp.zeros_like(ds_scr)

        grows = [gt_ref[hh] for hh in range(gh)]
        at = lambda hh, c: (slice(C * c, C * (c + 1)), slice(128 * hh, 128 * hh + 128))
        lastrow = _iota((C, 1), 0) == C - 1
        lane = _iota((C, 128), 1)
        idx = [(hh, c) for hh in range(gh) for c in range(ncb)]
        P = []
        for hh, c in idx:
            sl, ln = at(hh, c)
            lc = _gdn_stage1(q_ref[sl, ln], k_ref[sl, ln], v_ref[sl, ln], *_gate_cols(gb_ref[sl, :], hh), grows[hh][:, sl])
            lc.update(tinv=ti_ref[hh, sl, :], s=st_ref[hh, c], do=do_ref[sl, ln], kd=lc["k"] * lc["kdec"])
            P.append(lc)
        for l, u, w in zip(P, [mm(l["tinv"], l["vb"]) for l in P], [mm(l["tinv"], l["kg"]) for l in P]):
            l.update(u=u, w=w)
        for l, x in zip(P, [mm(l["w"], l["s"]) for l in P]):
            l["vn"] = l["u"] - x
        for l, a, b, c_, d in zip(P, [mm_nt(l["do"], l["s"]) for l in P], [mm_nt(l["do"], l["vn"]) for l in P],
                                  [mm_tn(l["qd"], l["do"]) for l in P], [mm_tn(l["pmat"], l["do"]) for l in P]):
            l.update(dqd=a, dp=jnp.where(l["incl"], b, 0.0), ds_q=c_, dvn_p=d)
        pre = dict(zip(idx, P))
        rows = {}
        hs = range(gh)
        ds = [ds_scr[hh] for hh in hs]
        for c in reversed(range(ncb)):
            L = [pre[hh, c] for hh in hs]
            dvn = [l["dvn_p"] + mm(l["kd"], d) for l, d in zip(L, ds)]
            dkd = [mm_nt(l["vn"], d) for l, d in zip(L, ds)]
            dcd = [_sum_all(l["s"] * d) for l, d in zip(L, ds)]
            ds = [l["ds_q"] + l["cd"] * d - mm_tn(l["w"], x) for l, d, x in zip(L, ds, dvn)]
            dw = [-mm_nt(x, l["s"]) for l, x in zip(L, dvn)]
            dvb = [mm_tn(l["tinv"], x) for l, x in zip(L, dvn)]
            dkg = [mm_tn(l["tinv"], x) for l, x in zip(L, dw)]
            da = [-jnp.where(l["strict"], mm_nt(a, l["u"]) + mm_nt(b, l["w"]), 0.0) for l, a, b in zip(L, dvb, dkg)]
            dm = [a * l["dmat"] for l, a in zip(L, da)]
            dn = [l["dp"] * l["dmat"] for l in L]
            dkb = [mm(a, l["k"]) for l, a in zip(L, dm)]
            dq = [mm(a, l["k"]) + l["gam"] * l["dqd"] for l, a in zip(L, dn)]
            dk = [mm_tn(a, l["kb"]) + mm_tn(b, l["q"]) for l, a, b in zip(L, dm, dn)]
            dgb = jnp.zeros((C, 128), F32)
            for hh in hs:
                sl, ln = at(hh, c)
                l = L[hh]
                e = da[hh] * l["lmat"] + l["dp"] * l["pmat"]
                t_kd = _lanes(dkd[hh] * l["kd"])
                dgl = _sum_all(t_kd) + dcd[hh] * l["cd"][:, :1]
                dgcs = (_lanes(e) + _lanes(l["dqd"] * l["qd"]) - t_kd + _lanes(dkg[hh] * l["kg"])
                        + jnp.where(lastrow, dgl, 0.0))
                rows[hh, c] = _rows(e)
                dq_ref[sl, ln] = dq[hh]
                dk_ref[sl, ln] = (dk[hh] + l["kdec"] * dkd[hh] + l["bb"] * l["gam"] * dkg[hh] + l["bb"] * dkb[hh])
                dv_ref[sl, ln] = l["bb"] * dvb[hh]
                dbeta = _lanes(dkg[hh] * l["gam"] * l["k"]) + _lanes(dvb[hh] * l["v"]) + _lanes(dkb[hh] * l["k"])
                dgb = dgb + jnp.where(lane == hh, dbeta, 0.0) + jnp.where(lane == 8 + hh, dgcs, 0.0)
            dgb_ref[slice(C * c, C * (c + 1)), :] = dgb
        for hh in hs:
            ds_scr[hh] = ds[hh]
            dgr_ref[hh] = jnp.concatenate([rows[hh, c] for c in range(ncb)], axis=1)

    blk = pl.BlockSpec((tb, 128 * gh), lambda h, i: (nb - 1 - i, h))
    rowspec = pl.BlockSpec((gh, 1, tb), lambda h, i: (h, 0, nb - 1 - i))
    cblk = pl.BlockSpec((tb, 128), lambda h, i: (nb - 1 - i, 0))
    return pl.pallas_call(
        body, name="gdn_bwd", grid=(GDN_H // gh, nb),
        in_specs=[blk] * 3 + [cblk, rowspec, blk,
                              pl.BlockSpec((gh, ncb, 128, 128), lambda h, i: (h, nb - 1 - i, 0, 0)),
                              pl.BlockSpec((gh, tb, C), lambda h, i: (h, nb - 1 - i, 0))],
        out_specs=[blk] * 3 + [cblk, rowspec],
        out_shape=[jax.ShapeDtypeStruct((T, D), F32)] * 3 + [jax.ShapeDtypeStruct((T, 128), F32),
                                                             jax.ShapeDtypeStruct((GDN_H, 1, T), F32)],
        scratch_shapes=[pltpu.VMEM((gh, 128, 128), F32)],
        compiler_params=_cparams(("parallel", "arbitrary")),
    )(qn, kn, v, gb, gcs_t, do, states, tinv)


def _ssd_pair(x2, dt2, acs2):
    last = acs2[SSM_L - 1:SSM_L, :]
    return jnp.exp(acs2), jnp.exp(last - acs2), x2 * dt2


def _ssd_head(hh, acs2, arow, dec2, cbm, bm, incl, col):
    lmask = (col >= 64 * hh) & (col < 64 * hh + 64)
    sg = jnp.where(incl, jnp.exp(jnp.minimum(acs2[:, 64 * hh:64 * hh + 1] - arow, 0.0)), 0.0)
    dec_col = dec2[:, 64 * hh:64 * hh + 1]
    return lmask, sg, sg * cbm, dec_col, bm * dec_col


def ssd_fwd(xs, bc, dt_x, acs_x, acs_t):
    T = xs.shape[0]
    nc, L = T // SSM_L, SSM_L

    def body(x_ref, bc_ref, dt_ref, ac_ref, at_ref, y_ref, hst_ref, h_scr):
        @pl.when(pl.program_id(0) == 0)
        def _():
            h_scr[...] = jnp.zeros_like(h_scr)

        row, col = _iota((L, L), 0), _iota((L, L), 1)
        incl = row >= col
        P, H = [], []
        for gp in range(8):
            g = gp // 4
            bm, cm = bc_ref[:, 128 * g:128 * g + 128], bc_ref[:, 256 + 128 * g:384 + 128 * g]
            cbm = mm_nt(cm, bm) if gp % 4 == 0 else cbm
            sl = slice(128 * gp, 128 * gp + 128)
            acs2 = ac_ref[:, sl]
            lam2, dec2, xd2 = _ssd_pair(x_ref[:, sl], dt_ref[:, sl], acs2)
            P.append(dict(sl=sl, lam2=lam2, xd2=xd2, hprev=h_scr[gp], cm=cm))
            for hh in range(2):
                lmask, _, mmat, _, bd = _ssd_head(hh, acs2, at_ref[2 * gp + hh], dec2, cbm, bm, incl, col)
                H.append(dict(mmat=mmat, bd=bd, xdh=jnp.where(lmask, xd2, 0.0), xd2=xd2))
        ys = [mm(h["mmat"], h["xdh"]) for h in H]
        sts = [mm_tn(h["xd2"], h["bd"]) for h in H]
        zs = [mm_nt(p["cm"], p["hprev"]) for p in P]
        for gp, p in enumerate(P):
            hst_ref[gp // 4, gp % 4] = p["hprev"]
            y_ref[:, p["sl"]] = ys[2 * gp] + ys[2 * gp + 1] + p["lam2"] * zs[gp]
            lam_rows = jnp.where(row < 64, p["lam2"][L - 1:L, 0:1], p["lam2"][L - 1:L, 64:65])
            h_scr[gp] = lam_rows * p["hprev"] + jnp.where(row < 64, sts[2 * gp], sts[2 * gp + 1])

    blk = pl.BlockSpec((L, D), lambda c: (c, 0))
    return pl.pallas_call(
        body, name="ssd_fwd", grid=(nc,),
        in_specs=[blk, pl.BlockSpec((L, 512), lambda c: (c, 0)), blk, blk, pl.BlockSpec((SSM_H, 1, L), lambda c: (0, 0, c))],
        out_specs=[blk, pl.BlockSpec((2, None, 4, 128, 128), lambda c: (0, c, 0, 0, 0))],
        out_shape=[jax.ShapeDtypeStruct((T, D), F32), jax.ShapeDtypeStruct((2, nc, 4, 128, 128), F32)],
        scratch_shapes=[pltpu.VMEM((8, 128, 128), F32)],
        compiler_params=_cparams(("arbitrary",)),
    )(xs, bc, dt_x, acs_x, acs_t)


def ssd_bwd(xs, bc, dt_x, acs_x, acs_t, dy, hstates):
    T = xs.shape[0]
    nc, L = T // SSM_L, SSM_L

    def body(x_ref, bc_ref, dt_ref, ac_ref, at_ref, dy_ref, hst_ref,
             dx_ref, db_ref, dc_ref, dgate_ref, dar_ref, dh_scr):
        @pl.when(pl.program_id(0) == 0)
        def _():
            dh_scr[...] = jnp.zeros_like(dh_scr)

        row, col = _iota((L, L), 0), _iota((L, L), 1)
        rowc = _iota((L, 1), 0)
        incl = row >= col
        G = [dict(bm=bc_ref[:, 128 * g:128 * g + 128], cm=bc_ref[:, 256 + 128 * g:384 + 128 * g]) for g in range(2)]
        for gr in G:
            gr["cbm"] = mm_nt(gr["cm"], gr["bm"])
        P = []
        for gp in range(8):
            sl = slice(128 * gp, 128 * gp + 128)
            x2, dt2, dy2, acs2 = x_ref[:, sl], dt_ref[:, sl], dy_ref[:, sl], ac_ref[:, sl]
            lam2, dec2, xd2 = _ssd_pair(x2, dt2, acs2)
            P.append(dict(sl=sl, gr=G[gp // 4], x2=x2, dt2=dt2, dy2=dy2, acs2=acs2, lam2=lam2, dec2=dec2, xd2=xd2,
                          hprev=hst_ref[gp // 4, gp % 4], dhn=dh_scr[gp], dz=lam2 * dy2))
        zs = [mm_nt(p["gr"]["cm"], p["hprev"]) for p in P]
        dcm_t = [mm(p["dz"], p["hprev"]) for p in P]
        dh_z = [mm_tn(p["dz"], p["gr"]["cm"]) for p in P]
        H = []
        for gp, p in enumerate(P):
            p["yoff"] = p["dz"] * zs[gp]
            p["q_rows"] = _lanes(p["dhn"] * p["hprev"])
            for hh in range(2):
                lmask, sg, mmat, dec_col, bd = _ssd_head(hh, p["acs2"], at_ref[2 * gp + hh], p["dec2"], p["gr"]["cbm"],
                                                         p["gr"]["bm"], incl, col)
                H.append(dict(p=p, hh=hh, j=2 * gp + hh, lmask=lmask, sg=sg, mmat=mmat, dec_col=dec_col, bd=bd))
        dms = [mm_nt(jnp.where(h["lmask"], h["p"]["dy2"], 0.0), h["p"]["xd2"]) for h in H]
        a1s = [mm_tn(h["mmat"], h["p"]["dy2"]) for h in H]
        a2s = [mm_nt(h["bd"], h["p"]["dhn"]) for h in H]
        dbds = [mm(jnp.where(h["lmask"], h["p"]["xd2"], 0.0), h["p"]["dhn"]) for h in H]
        for gr in G:
            gr.update(dcb=jnp.zeros((L, L), F32), dbm=jnp.zeros((L, SSM_N), F32), comp=jnp.zeros((L, 128), F32))
        dxd = [jnp.zeros((L, 128), F32) for _ in P]
        for h, dm_raw, a1, a2, dbd in zip(H, dms, a1s, a2s, dbds):
            p, hh, j = h["p"], h["hh"], h["j"]
            gr, jg = p["gr"], j % 8
            dm = jnp.where(incl, dm_raw, 0.0)
            gr["dcb"] = gr["dcb"] + dm * h["sg"]
            e = dm * h["mmat"]
            dxd_h = jnp.where(h["lmask"], a1 + a2, 0.0)
            dxd[j // 2] = dxd[j // 2] + dxd_h
            gr["dbm"] = gr["dbm"] + h["dec_col"] * dbd
            t = _lanes(dbd * h["bd"])
            lam_h = p["lam2"][L - 1:L, 64 * hh:64 * hh + 1]
            in_head = (rowc >= 64 * hh) & (rowc < 64 * hh + 64)
            add_last = _sum_all(t) + _sum_all(jnp.where(in_head, p["q_rows"], 0.0)) * lam_h
            dacs_col = (_lanes(jnp.where(h["lmask"], p["yoff"], 0.0)) + _lanes(e) - t
                        + jnp.where(rowc == L - 1, add_last, 0.0))
            ddt_col = _lanes(dxd_h * p["x2"])
            dar_ref[j] = _rows(e)
            gr["comp"] = gr["comp"] + jnp.where(col == jg, dacs_col, 0.0) + jnp.where(col == 8 + jg, ddt_col, 0.0)
        for gp, p in enumerate(P):
            lam_rows = jnp.where(row < 64, p["lam2"][L - 1:L, 0:1], p["lam2"][L - 1:L, 64:65])
            dh_scr[gp] = dh_z[gp] + lam_rows * p["dhn"]
            dx_ref[:, p["sl"]] = p["dt2"] * dxd[gp]
        for g, gr in enumerate(G):
            lanes = slice(128 * g, 128 * g + 128)
            dcm = (dcm_t[4 * g] + dcm_t[4 * g + 1]) + (dcm_t[4 * g + 2] + dcm_t[4 * g + 3])
            db_ref[:, lanes] = gr["dbm"] + mm_tn(gr["dcb"], gr["cm"])
            dc_ref[:, lanes] = dcm + mm(gr["dcb"], gr["bm"])
            dgate_ref[:, lanes] = gr["comp"]

    rv = lambda c: (nc - 1 - c, 0)
    blk, blk256 = pl.BlockSpec((L, D), rv), pl.BlockSpec((L, 256), rv)
    rowspec = pl.BlockSpec((SSM_H, 1, L), lambda c: (0, 0, nc - 1 - c))
    return pl.pallas_call(
        body, name="ssd_bwd", grid=(nc,),
        in_specs=[blk, pl.BlockSpec((L, 512), rv), blk, blk, rowspec, blk,
                  pl.BlockSpec((2, None, 4, 128, 128), lambda c: (0, nc - 1 - c, 0, 0, 0))],
        out_specs=[blk, blk256, blk256, blk256, rowspec],
        out_shape=[jax.ShapeDtypeStruct((T, D), F32), jax.ShapeDtypeStruct((T, 256), F32),
                   jax.ShapeDtypeStruct((T, 256), F32), jax.ShapeDtypeStruct((T, 256), F32),
                   jax.ShapeDtypeStruct((SSM_H, 1, T), F32)],
        scratch_shapes=[pltpu.VMEM((8, 128, 128), F32)],
        compiler_params=_cparams(("arbitrary",)),
    )(xs, bc, dt_x, acs_x, acs_t, dy, hstates)


def _pos():
    return lax.axis_index("x"), lax.axis_index("y"), lax.axis_index("c")


def _other_chips(x, y):
    return [(1 - x, y), (x, 1 - y), (1 - x, 1 - y)]


def _rcopy(src, dst, ssem, rsem, dev):
    return pltpu.make_async_remote_copy(src_ref=src, dst_ref=dst, send_sem=ssem, recv_sem=rsem,
                                        device_id=dev, device_id_type=MESH)


def _rows_at(start, n):
    return pl.ds(pl.multiple_of(start, 8), n)


def _comm_call(body, name, out_shape, n_in, scratch):
    return pl.pallas_call(
        body, name=name, out_shape=out_shape, in_specs=[ANY] * n_in,
        out_specs=[ANY] * len(out_shape) if isinstance(out_shape, (list, tuple)) else ANY,
        scratch_shapes=scratch,
        compiler_params=pltpu.CompilerParams(has_side_effects=True),
    )


def _dma_sems(n):
    return pltpu.SemaphoreType.DMA((n,))


def ag_chips(name, shard):
    rr, cc = shard.shape
    h, nq = rr // 2, ICI_CHUNKS
    hq = h // nq

    def body(x_ref, out_ref, ssem, rsem):
        x, y, c = _pos()
        me_s = 2 * x + y
        chips = _other_chips(x, y)
        started = []
        for q in range(nq):
            rows = _rows_at(c * h + q * hq, hq)
            for j, (cx, cy) in enumerate(chips):
                cp = _rcopy(x_ref.at[rows], out_ref.at[me_s, rows], ssem.at[j * nq + q], rsem.at[j * nq + q], (cx, cy, c))
                cp.start()
                started.append(cp)
        for q in range(nq):
            rows = _rows_at(c * h + q * hq, hq)
            for j, (cx, cy) in enumerate(chips):
                blk = out_ref.at[2 * cx + cy, rows]
                _rcopy(blk, blk, ssem.at[j * nq + q], rsem.at[j * nq + q], (cx, cy, c)).wait_recv()
                k = 3 * nq + j * nq + q
                cp = _rcopy(blk, blk, ssem.at[k], rsem.at[k], (x, y, 1 - c))
                cp.start()
                started.append(cp)
        for q in range(nq):
            rows = _rows_at((1 - c) * h + q * hq, hq)
            for j, (cx, cy) in enumerate(chips):
                blk = out_ref.at[2 * cx + cy, rows]
                k = 3 * nq + j * nq + q
                _rcopy(blk, blk, ssem.at[k], rsem.at[k], (x, y, 1 - c)).wait_recv()
        for cp in started:
            cp.wait_send()

    return _comm_call(body, name, jax.ShapeDtypeStruct((4, rr, cc), shard.dtype), 1,
                      [_dma_sems(6 * nq), _dma_sems(6 * nq)])(shard)


def _with_own(shard, got, s_me):
    return lax.dynamic_update_index_in_dim(got, shard, s_me, 0)


def all_gather_chips(name, shard, s_me):
    return _with_own(shard, ag_chips(name, shard), s_me)


HBM_SPEC = pl.BlockSpec(memory_space=pltpu.HBM)
SEM_SPEC = pl.BlockSpec(memory_space=pltpu.SEMAPHORE)
SPLIT_EFFECT = pltpu.SideEffectType.DATAFLOW_SIDE_EFFECTING


def _split_copies(pieces, x_ref, land_ref, sems, arriving):
    x, y, c = _pos()
    return [_rcopy(s, d_in if arriving else d_out, sems[j], sems[3 + j], dev)
            for j, (s, d_out, d_in, dev) in enumerate(pieces(x_ref, land_ref, x, y, c))]


def split_copy_start(name, src, land_shape, pieces, after):
    def body(x_ref, land_ref, after_ref, *outs):
        for cp in _split_copies(pieces, x_ref, land_ref, outs[:6], False):
            cp.start()
        outs[8][...] = jnp.zeros_like(outs[8])

    dma = pltpu.SemaphoreType.DMA(())
    res = pl.pallas_call(
        body, name=name,
        out_shape=(dma,) * 6 + (pltpu.HBM(src.shape, src.dtype), pltpu.HBM(land_shape, src.dtype),
                                jax.ShapeDtypeStruct((8, 128), F32)),
        in_specs=(HBM_SPEC, HBM_SPEC, ANY),
        out_specs=(SEM_SPEC,) * 6 + (HBM_SPEC, HBM_SPEC, pl.BlockSpec(memory_space=pltpu.VMEM)),
        input_output_aliases={0: 6, 1: 7},
        compiler_params=pltpu.CompilerParams(has_side_effects=SPLIT_EFFECT),
    )(pltpu.with_memory_space_constraint(src, pltpu.HBM),
      pltpu.with_memory_space_constraint(lax.empty(land_shape, src.dtype), pltpu.HBM), after)
    return res[:6], res[6], res[7], res[8]


def split_copy_wait(name, sems, src_thru, land_thru, after, pieces):
    def body(x_ref, land_ref, *rest):
        for cp in _split_copies(pieces, x_ref, land_ref, rest[:6], False):
            cp.wait_send()
        for cp in _split_copies(pieces, x_ref, land_ref, rest[:6], True):
            cp.wait_recv()

    return pl.pallas_call(
        body, name=name,
        out_shape=(pltpu.HBM(src_thru.shape, src_thru.dtype), pltpu.HBM(land_thru.shape, land_thru.dtype)),
        in_specs=(HBM_SPEC, HBM_SPEC) + (SEM_SPEC,) * 6 + (ANY,), out_specs=(HBM_SPEC, HBM_SPEC),
        input_output_aliases={0: 0, 1: 1},
        compiler_params=pltpu.CompilerParams(has_side_effects=SPLIT_EFFECT),
    )(src_thru, land_thru, *sems, after)


def ag_pieces(h):
    def pieces(x_ref, land_ref, x, y, c):
        rows = _rows_at(c * h, h)
        return [(x_ref.at[rows], land_ref.at[2 * x + y, rows], land_ref.at[2 * cx + cy, rows], (cx, cy, c))
                for cx, cy in _other_chips(x, y)]
    return pieces


def rs_pieces(x_ref, land_ref, x, y, c):
    return [(x_ref.at[2 * cx + cy], land_ref.at[j], land_ref.at[j], (cx, cy, c))
            for j, (cx, cy) in enumerate(_other_chips(x, y))]


def ag_forward(name, got):
    _, rr, cc = got.shape
    h, nq = rr // 2, D2D_CHUNKS
    hq = h // nq

    def body(g_ref, out_ref, ssem, rsem):
        x, y, c = _pos()
        slots = [2 * cx + cy for cx, cy in _other_chips(x, y)]
        cps = []
        for j, s in enumerate(slots):
            for q in range(nq):
                blk = out_ref.at[s, _rows_at(c * h + q * hq, hq)]
                cp = _rcopy(blk, blk, ssem.at[j * nq + q], rsem.at[j * nq + q], (x, y, 1 - c))
                cp.start()
                cps.append(cp)
        for cp in cps:
            cp.wait_send()
        for j, s in enumerate(slots):
            for q in range(nq):
                blk = out_ref.at[s, _rows_at((1 - c) * h + q * hq, hq)]
                _rcopy(blk, blk, ssem.at[j * nq + q], rsem.at[j * nq + q], (x, y, 1 - c)).wait_recv()

    return pl.pallas_call(
        body, name=name, out_shape=jax.ShapeDtypeStruct(got.shape, got.dtype), in_specs=[ANY], out_specs=ANY,
        scratch_shapes=[_dma_sems(3 * nq), _dma_sems(3 * nq)], input_output_aliases={0: 0},
        compiler_params=pltpu.CompilerParams(has_side_effects=True),
    )(got)


def rs_pair(name, g):
    _, rr, cc = g.shape
    h, nq = rr // 2, D2D_CHUNKS
    hq = h // nq

    def body(g_ref, recv_ref, ssem, rsem):
        x, y, c = _pos()
        cps = []
        for q in range(nq):
            cp = _rcopy(g_ref.at[:, _rows_at((1 - c) * h + q * hq, hq), :], recv_ref.at[:, pl.ds(q * hq, hq), :],
                        ssem.at[q], rsem.at[q], (x, y, 1 - c))
            cp.start()
            cps.append(cp)
        for cp in cps:
            cp.wait()

    return _comm_call(body, name, jax.ShapeDtypeStruct((4, h, cc), g.dtype), 1, [_dma_sems(nq), _dma_sems(nq)])(g)


def rs_chips(name, p):
    _, h, cc = p.shape
    nq = ICI_CHUNKS
    hq = h // nq

    def body(p_ref, buf_ref, ssem, rsem):
        x, y, c = _pos()
        sends = []
        for q in range(nq):
            rows = pl.ds(q * hq, hq)
            for j, (cx, cy) in enumerate(_other_chips(x, y)):
                cp = _rcopy(p_ref.at[2 * cx + cy, rows], buf_ref.at[j, rows], ssem.at[j * nq + q],
                            rsem.at[j * nq + q], (cx, cy, c))
                cp.start()
                sends.append(cp)
        for cp in sends:
            cp.wait()

    return _comm_call(body, name, jax.ShapeDtypeStruct((3, h, cc), p.dtype), 1,
                      [_dma_sems(3 * nq), _dma_sems(3 * nq)])(p)


def rs_join(name, half):
    h, cc = half.shape
    nq = D2D_CHUNKS
    hq = h // nq

    def body(h_ref, out_ref, ssem, rsem):
        x, y, c = _pos()
        cps = []
        for q in range(nq):
            rows = pl.ds(q * hq, hq)
            cp = _rcopy(h_ref.at[rows], out_ref.at[rows], ssem.at[q], rsem.at[q], (x, y, 1 - c))
            cp.start()
            cps.append(cp)
        for cp in cps:
            cp.wait()

    return _comm_call(body, name, jax.ShapeDtypeStruct((h, cc), half.dtype), 1, [_dma_sems(nq), _dma_sems(nq)])(half)


def reduce_scatter(tag, g, tb, sp):
    return rs_end(rs_begin(tag, g, tb, sp, False), None)


def rs_begin(tag, g, tb, sp, split, after=None):
    _, rr, cc = g.shape
    h = rr // 2
    nbh = h // tb
    recv = rs_pair(tag + "_pair", g)
    mine_rows = lambda i, s: (i // nbh) * (2 * nbh) + s[0] * nbh + i % nbh
    part = rowwise(add2_fn, tag + "_add", 4 * h, tb, [R(g.reshape(4 * rr, cc), off=mine_rows), R(recv.reshape(4 * h, cc))],
                   [], [(cc, BF16)], sp=sp)[0].reshape(4, h, cc)
    st = dict(tag=tag, tb=tb, sp=sp, split=split, part=part)
    if split:
        st["sems"], st["part"], st["land"], st["token"] = split_copy_start(tag + "_start", part, (3, h, cc), rs_pieces,
                                                                           sp if after is None else after)
    return st


def rs_end(st, after):
    tag, tb, sp, part = st["tag"], st["tb"], st["sp"], st["part"]
    _, h, cc = part.shape
    nbh = h // tb
    if st["split"]:
        part, buf = split_copy_wait(tag + "_wait", st["sems"], part, st["land"], after, rs_pieces)
    else:
        buf = rs_chips(tag + "_chips", part)
    red = rowwise(sum4_fn, tag + "_sum", h, tb,
                  [R(part.reshape(4 * h, cc), off=lambda i, s: s[1] * nbh + i)]
                  + [R(buf.reshape(3 * h, cc), off=k * nbh) for k in range(3)],
                  [], [(cc, F32)], sp=sp)[0]
    return red, rs_join(tag + "_join", red)


def adam_halves(name, w, m, v, red, other, tb, blk0, sp):
    nbh = red.shape[0] // tb

    def fn(i, n, s, w_, m_, v_, r_, o_):
        g = jnp.where((blk0 + i) // nbh == s[0], r_, o_)
        return (g,) + _adamw(w_, g, m_, v_)

    half_rows = lambda i, s: (blk0 + i) % nbh
    return rowwise(fn, name, w.shape[0], tb, [R(w), R(m), R(v), R(red, off=half_rows), R(other, off=half_rows)],
                   [], [(w.shape[1], F32)] * 4, sp=sp)


SMALL_LANES = 3 * D


def all_reduce_items(name, items, after=None):
    flat = [a for it in items for a in it]
    shapes = [(sum(a.shape[0] for a in it), it[0].shape[1]) for it in items]
    nrows = -(-sum(s[0] for s in shapes) // 8) * 8
    extra = [] if after is None else [after]

    def body(*refs):
        ins, refs = refs[:len(flat)], refs[len(flat) + len(extra):]
        outs = refs[:len(items)]
        mine, buf, ssem, rsem = refs[len(items):]
        x, y, c = _pos()
        me = 4 * x + 2 * y + c
        mine[...] = jnp.zeros_like(mine)
        r = 0
        for ref in ins:
            mine[r:r + ref.shape[0], 0:ref.shape[1]] = ref[...]
            r += ref.shape[0]
        buf[me] = mine[...]
        cps = []
        for k in range(1, 8):
            dev = (x ^ (k >> 2), y ^ ((k >> 1) & 1), c ^ (k & 1))
            cp = _rcopy(mine, buf.at[me], ssem.at[k - 1], rsem.at[k - 1], dev)
            cp.start()
            cps.append(cp)
        for cp in cps:
            cp.wait()
        r = 0
        for (nr, n), out in zip(shapes, outs):
            acc = buf[0, r:r + nr, 0:n]
            for d in range(1, 8):
                acc = acc + buf[d, r:r + nr, 0:n]
            out[...] = acc
            r += nr

    vm = pl.BlockSpec(memory_space=pltpu.VMEM)
    return pl.pallas_call(
        body, name=name, out_shape=[jax.ShapeDtypeStruct(s, F32) for s in shapes],
        in_specs=[vm] * len(flat) + [ANY] * len(extra), out_specs=[vm] * len(items),
        scratch_shapes=[pltpu.VMEM((nrows, SMALL_LANES), F32), pltpu.VMEM((8, nrows, SMALL_LANES), F32),
                        _dma_sems(7), _dma_sems(7)],
        compiler_params=pltpu.CompilerParams(has_side_effects=True),
    )(*flat, *extra)


def adam_small(ws, gs, ms, vs):
    n = len(ws)

    def body(*refs):
        for k in range(n):
            w, g, m, v = (refs[j * n + k][...] for j in range(4))
            for j, val in enumerate(_adamw(w, g, m, v)):
                refs[(4 + j) * n + k][...] = val

    vm = pl.BlockSpec(memory_space=pltpu.VMEM)
    res = pl.pallas_call(
        body, name="adam_small", out_shape=[jax.ShapeDtypeStruct(w.shape, F32) for w in ws] * 3,
        in_specs=[vm] * (4 * n), out_specs=[vm] * (3 * n),
    )(*ws, *gs, *ms, *vs)
    return res[:n], res[n:2 * n], res[2 * n:]


def _sel(rows, cols, pairs):
    m = np.zeros((rows, cols), np.float32)
    for r, c in pairs:
        m[r, c] = 1.0
    return jnp.asarray(m)


def _pad_win(w):
    z = jnp.zeros((w.shape[0], 112), w.dtype)
    return jnp.concatenate([w[:, :4096], w[:, 4112:6672], w[:, 4096:4112], z, w[:, 6672:6688], z], axis=1)


def _unpad_win(wp):
    return jnp.concatenate([wp[:, :4096], wp[:, 6656:6672], wp[:, 4096:6656], wp[:, 6784:6800]], axis=1)


def kernel(x, mem, norm1_w, w_in, gdn_conv_w, gdn_a_log, gdn_dt_bias, gdn_norm_w, ssm_conv_w, ssm_conv_b, ssm_a_log, ssm_dt_bias, ssm_d, ssm_norm_w, w_out, norm2_w, mem_norm_w, wq_mem, wk_mem, wv_mem, wo_mem, norm3_w, w_up, w_down, final_norm_w, loss_target, m_norm1_w, m_w_in, m_gdn_conv_w, m_gdn_a_log, m_gdn_dt_bias, m_gdn_norm_w, m_ssm_conv_w, m_ssm_conv_b, m_ssm_a_log, m_ssm_dt_bias, m_ssm_d, m_ssm_norm_w, m_w_out, m_norm2_w, m_mem_norm_w, m_wq_mem, m_wk_mem, m_wv_mem, m_wo_mem, m_norm3_w, m_w_up, m_w_down, m_final_norm_w, v_norm1_w, v_w_in, v_gdn_conv_w, v_gdn_a_log, v_gdn_dt_bias, v_gdn_norm_w, v_ssm_conv_w, v_ssm_conv_b, v_ssm_a_log, v_ssm_dt_bias, v_ssm_d, v_ssm_norm_w, v_w_out, v_norm2_w, v_mem_norm_w, v_wq_mem, v_wk_mem, v_wv_mem, v_wo_mem, v_norm3_w, v_w_up, v_w_down, v_final_norm_w):
    T, M = x.shape[1], mem.shape[1]
    xi, yi, ci = _pos()
    s_me = 2 * xi + yi
    x0, mem0, tgt = x[0], mem[0], loss_target[0]
    tb = min(512, T)
    tbl = min(1024, T)
    tbp = min(512, T)
    row = lambda v: v.reshape(1, -1)

    win_g = all_gather_chips("ag_win", w_in.astype(BF16), s_me)
    w_in_p = _pad_win(win_g.transpose(1, 0, 2).reshape(D, IN_COLS))
    keep = (ci == 0).astype(F32)
    gcw_z = lax.dynamic_update_slice(jnp.zeros((4, 3 * D), F32), gdn_conv_w * keep, (0, s_me * 768))
    scw_z = lax.dynamic_update_slice(jnp.zeros((4, 1536), F32), ssm_conv_w * keep, (0, s_me * 384))
    gcw, scw = all_reduce_items("ar_convw", [[gcw_z], [scw_z]])
    scw_x, scw_bc = scw[:, :D], scw[:, D:]
    rest_shard = jnp.concatenate([w_up, w_down, w_out, wq_mem, wk_mem, wv_mem, wo_mem], axis=0).astype(BF16)
    ag_sems, rest_thru, rest_land, ag_token = split_copy_start("ag_rest_start", rest_shard, (4,) + rest_shard.shape,
                                                               ag_pieces(rest_shard.shape[0] // 2), gcw)
    sp = jnp.stack([ci, s_me]).astype(jnp.int32)
    scb_x, scb_bc = row(ssm_conv_b[:D]), row(ssm_conv_b[D:])

    galog_c, gdtb_c = row(jnp.pad(gdn_a_log, (8, 112))), row(jnp.pad(gdn_dt_bias, (8, 112)))
    salog_c, sdtb_c = row(jnp.pad(ssm_a_log, (0, 112))), row(jnp.pad(ssm_dt_bias, (0, 112)))
    sd_x = row(jnp.repeat(ssm_d, 64))
    eb = _sel(128, D, [(h, 128 * h + l) for h in range(8) for l in range(128)])
    ea = _sel(128, D, [(8 + h, 128 * h + l) for h in range(8) for l in range(128)])
    e16 = _sel(128, D, [(h, 64 * h + l) for h in range(16) for l in range(64)])

    h1 = rowwise(rms_fwd_fn, "rms1", T, tbl, [R(x0)], [row(norm1_w) + ag_token[0:1, 0:1]], [(D, BF16)])[0]
    p = matmul("mm_in", h1, w_in_p, "nn", 2048, 768, 1024, [F32])[0]
    gp_ins = [R(p, 3 * D, CB_QKV, "prev"), R(p, 128, CB_BA)]
    qn, kn, vv, gcs_x, beta_x, ggate, gcs_t = rowwise(gdn_prep_fn, "gdn_prep", T, tbp, gp_ins,
                                                      [gcw, galog_c, gdtb_c, eb, ea],
                                                      [(D, F32)] * 5 + [(128, F32), (-8, F32)])
    gcs_t = gcs_t.reshape(GDN_H, 1, T)
    gtb, ggh = min(128, T), 8
    o_gdn, s_states, tinv = gdn_fwd(qn, kn, vv, gcs_x, beta_x, gcs_t, gtb, ggh)
    gnw = row(gdn_norm_w)
    oa = rowwise(gdn_post_fn, "gdn_post", T, tbl, [R(o_gdn), R(p, D, CB_Z)], [gnw], [(D, BF16)])[0]
    sp_ins = [R(p, D, CB_XS, "prev"), R(p, 512, CB_BC, "prev"), R(p, 128, CB_DT)]
    sp_full = [scw_x, scw_bc, scb_x, scb_bc, salog_c, sdtb_c]
    xs, bc, dt_x, acs_x, acs_t = rowwise(ssd_prep_fn, "ssd_prep", T, tbp, sp_ins, sp_full + [e16],
                                         [(D, F32), (512, F32), (D, F32), (D, F32), (-SSM_H, F32)])
    acs_t = acs_t.reshape(SSM_H, 1, T)
    y_ssd, h_states = ssd_fwd(xs, bc, dt_x, acs_x, acs_t)
    snw = row(ssm_norm_w)
    ob = rowwise(ssd_post_fn, "ssd_post", T, tbl, [R(y_ssd), R(xs), R(p, D, CB_ZS)], [sd_x, snw], [(D, BF16)])[0]
    rest_thru, rest_land = split_copy_wait("ag_rest_wait", ag_sems, rest_thru, rest_land, ob,
                                           ag_pieces(rest_shard.shape[0] // 2))
    rest_g = _with_own(rest_thru, ag_forward("ag_rest_fwd", rest_land), s_me)
    assert D == 1024
    view = lambda shape, blk, at: dict(b_sel=(shape, blk, at))
    wup_n = view((D, D_FF), (None, D, D), lambda i, j, k: (j, 0, 0))
    wup_t = view((D, D_FF), (None, D, D), lambda i, j, k: (k, 0, 0))
    wdown_n = view((D_FF, D), (None, D, D), lambda i, j, k: (k, 1, 0))
    wdown_t = view((D_FF, D), (None, D, D), lambda i, j, k: (j, 1, 0))
    wout_a = view((D, D), (2, 512, D), lambda i, j, k: (0, 4, 0))
    wout_b = view((D, D), (2, 512, D), lambda i, j, k: (1, 4, 0))
    wq_v, wk_v, wv_v, wo_v = (view((D, D), (4, 256, D), lambda i, j, k, r=r: (0, r, 0)) for r in (10, 11, 12, 13))
    x1a = matmul("mm_out_a", oa, rest_g, "nn", 1024, 1024, 1024, [F32], _epi_res, [x0], **wout_a)[0]
    x1, h2 = matmul("mm_out_b", ob, rest_g, "nn", 1024, 1024, 1024, [F32, BF16], _epi_res_rms, [x1a],
                    [row(norm2_w)], **wout_b)

    mn = rowwise(rms_fwd_fn, "rms_mem", M, M, [R(mem0)], [row(mem_norm_w)], [(D, BF16)])[0]
    km = matmul("mm_k", mn, rest_g, "nn", 256, 1024, 1024, [BF16], **wk_v)[0]
    vm = matmul("mm_v", mn, rest_g, "nn", 256, 1024, 1024, [BF16], **wv_v)[0]
    qm = matmul("mm_q", h2, rest_g, "nn", 1024, 1024, 1024, [BF16], **wq_v)[0]
    ao = rowwise(attn_fn, "attn", T, tbl, [R(qm)], [km, vm], [(D, BF16)])[0]
    x2, h3 = matmul("mm_o", ao, rest_g, "nn", 1024, 1024, 1024, [F32, BF16], _epi_res_rms, [x1], [row(norm3_w)], **wo_v)
    u, act = matmul("mm_up", h3, rest_g, "nn", 2048, 1024, 1024, [BF16, BF16], _epi_relu2, **wup_n)
    wdown_n2 = view((D_FF, D), (2, D, D), lambda i, j, k: (k, 1, 0))
    x3 = matmul("mm_down", act, rest_g, "nn", 1024, 1024, 2048, [F32], _epi_res, [x2], **wdown_n2)[0]
    dx3, dx3b, loss_lane, g_final = rowwise(final_fn, "final", T, tbl, [R(x3), R(tgt)], [row(final_norm_w)],
                                            [(D, F32), (D, BF16)], [(1, D), (1, D)])
    loss = lax.psum(0.5 / D * jnp.sum(loss_lane), ("x", "y", "c"))

    dup = matmul("mm_dact", dx3b, rest_g, "nt", 2048, 1024, 1024, [BF16], _epi_dup, [u], **wdown_t)[0]
    def g_into(buf, blk, at):
        return dict(into=(buf, blk, lambda i, j, k, at=at: at(i, j)))

    grest = jax.ShapeDtypeStruct((4, 3584, D), F32)
    grest = matmul("mm_gdown", act, dx3b, "tn", 1024, 1024, 4096, [F32],
                   **g_into(grest, (None, 1024, D), lambda i, j: (i, 1, 0)))
    wup_t4 = dict(b_sel=((D, D_FF), (4, D, D), lambda i, j, k: (0, 0, 0), "side by side"))
    dh3 = matmul("mm_dh3", dup, rest_g, "nt", 1024, 1024, 4096, [F32], **wup_t4)[0]
    dx2, dx2b, g_n3 = rowwise(rms_bwd_fn, "rms3_bwd", T, tbl, [R(x2), R(dh3), R(dx3)], [row(norm3_w)],
                              [(D, F32), (D, BF16)], [(1, D)])
    grest = matmul("mm_gup", h3, dup, "tn", 1024, 1024, 4096, [F32],
                   **g_into(grest, (None, 1024, D), lambda i, j: (j, 0, 0)))
    dao = matmul("mm_dao", dx2b, rest_g, "nt", 1024, 1024, 1024, [F32], **wo_v)[0]
    grest = matmul("mm_gwo", ao, dx2b, "tn", 1024, 1024, 2048, [F32],
                   **g_into(grest, (4, 256, D), lambda i, j: (0, 13, 0)))
    dqm, dkm, dvm = rowwise(attn_bwd_fn, "attn_bwd", T, tb, [R(qm), R(dao)], [km, vm], [(D, BF16)],
                            [(M, D), (M, D)])
    dx1, dx1b, g_n2 = matmul("mm_dh2", dqm, rest_g, "nt", 512, 1024, 1024, [F32, BF16], _epi_rms_bwd, [x1, dx2],
                             [row(norm2_w)], n_acc=1, **wq_v)
    grest = matmul("mm_gwq", h2, dqm, "tn", 1024, 1024, 2048, [F32],
                   **g_into(grest, (4, 256, D), lambda i, j: (0, 10, 0)))
    grest = matmul("mm_gwk", mn, dkm, "tn", 1024, 1024, 256, [F32],
                   **g_into(grest, (4, 256, D), lambda i, j: (0, 11, 0)))
    grest = matmul("mm_gwv", mn, dvm, "tn", 1024, 1024, 256, [F32],
                   **g_into(grest, (4, 256, D), lambda i, j: (0, 12, 0)))
    dmn_k = matmul("mm_dmk", dkm, rest_g, "nt", 256, 1024, 1024, [F32], **wk_v)[0]
    dmn = matmul("mm_dmv", dvm, rest_g, "nt", 256, 1024, 1024, [F32], _epi_res, [dmn_k], **wv_v)[0]
    g_nmem = rowwise(rms_bwd_w_fn, "rmsmem_bwd", M, M, [R(mem0), R(dmn)], [row(mem_norm_w)], [], [(1, D)])[0]
    doa = matmul("mm_doa", dx1b, rest_g, "nt", 2048, 1024, 1024, [F32], **wout_a)[0]
    dob = matmul("mm_dob", dx1b, rest_g, "nt", 2048, 1024, 1024, [F32], **wout_b)[0]
    grest = matmul("mm_gwout_a", oa, dx1b, "tn", 1024, 1024, 2048, [F32],
                   **g_into(grest, (2, 512, D), lambda i, j: (0, 4, 0)))
    grest = matmul("mm_gwout_b", ob, dx1b, "tn", 1024, 1024, 2048, [F32],
                   **g_into(grest, (2, 512, D), lambda i, j: (1, 4, 0)))

    rs_rest = rs_begin("rs_rest", grest, 256, sp, True)

    dp = jax.ShapeDtypeStruct((T, p.shape[1]), BF16)
    dy_ssd, dxs_dir, dp, g_snw, g_sd_lane = rowwise(
        ssd_post_bwd_fn, "ssd_post_bwd", T, tb, [R(y_ssd), R(xs), R(p, D, CB_ZS), R(dob)],
        [sd_x + rs_rest["token"][0:1, 0:1], snw],
        [(D, F32), (D, F32), (D, BF16, dp, CB_ZS)], [(1, D), (1, D)])
    dxs_scan, db_s, dc_s, dgate, dacs_t = ssd_bwd(xs, bc, dt_x, acs_x, acs_t, dy_ssd, h_states)
    spb = rowwise(ssd_prep_bwd_fn, "ssd_prep_bwd", T, tbp,
                  sp_ins + [R(dxs_scan), R(dxs_dir), R(db_s), R(dc_s), R(dgate), RC(dacs_t.reshape(SSM_H, T))], sp_full,
                  [(D, F32), (512, F32), (128, BF16, dp, CB_DT)],
                  [(1, D)] * 4 + [(1, 512)] * 4 + [(1, D), (1, 512), (1, 128), (1, 128)])
    dyc_x, dyc_bc, dp = spb[:3]
    dp = rowwise(conv_bwd_fn, "conv_bwd_x", T, tbp, [R(dyc_x, halo="next")], [scw_x], [(D, BF16, dp, CB_XS)])[0]
    dp = rowwise(conv_bwd_fn, "conv_bwd_bc", T, tbp, [R(dyc_bc, halo="next")], [scw_bc], [(512, BF16, dp, CB_BC)])[0]

    do_gdn, dp, g_gnw = rowwise(gdn_post_bwd_fn, "gdn_post_bwd", T, tb, [R(o_gdn), R(p, D, CB_Z), R(doa)], [gnw],
                                [(D, F32), (D, BF16, dp, CB_Z)], [(1, 128)])
    dqn, dkn, dvv, dggate, dgcs_t = gdn_bwd(qn, kn, vv, ggate, gcs_t, do_gdn, s_states, tinv, gtb, ggh)
    gpb = rowwise(gdn_prep_bwd_fn, "gdn_prep_bwd", T, tbp,
                  gp_ins + [R(dqn), R(dkn), R(dvv), R(dggate), RC(dgcs_t.reshape(GDN_H, T))],
                  [gcw, galog_c, gdtb_c],
                  [(3 * D, F32), (128, BF16, dp, CB_BA)], [(1, 3 * D)] * 4 + [(1, 128), (1, 128)])
    dyc_qkv, dp = gpb[:2]
    dp = rowwise(conv_bwd_fn, "conv_bwd_qkv", T, tbp, [R(dyc_qkv, halo="next")], [gcw], [(3 * D, BF16, dp, CB_QKV)])[0]
    dh1 = matmul("mm_dh1", dp, w_in_p, "nt", 1024, 1024, 2304, [F32])[0]
    grad_x, g_n1 = rowwise(rms_bwd1_fn, "rms1_bwd", T, tbl, [R(x0), R(dh1), R(dx1)], [row(norm1_w)], [(D, F32)], [(1, D)])
    g_win_p = matmul("mm_gwin", h1, dp, "tn", 1024, 768, 4096, [F32])[0]

    items = [[g_n1], [gpb[6]], [gpb[7]], [g_gnw], [spb[11]], [spb[12]], [spb[13]], [spb[14]], [g_sd_lane], [g_snw],
             [g_n2], [g_nmem], [g_n3], [g_final], list(gpb[2:6]), list(spb[3:7]), list(spb[7:11])]
    (gr_n1, r_galog, r_gdtb, gr_gnw, r_scb_x, r_scb_bc, r_salog, r_sdtb, r_sd, gr_snw, gr_n2, gr_nmem, gr_n3,
     gr_final, r_gcw, r_scw_x, r_scw_bc) = all_reduce_items("ar_grads", items)
    gr_galog, gr_gdtb = r_galog[:, 8:16], r_gdtb[:, 8:16]
    gr_salog, gr_sdtb = r_salog[:, :SSM_H], r_sdtb[:, :SSM_H]
    gr_sd = r_sd.reshape(SSM_H, SSM_P).sum(axis=1).reshape(1, SSM_H)
    gr_scb = jnp.concatenate([r_scb_x, r_scb_bc], axis=1)
    gr_gcw = lax.dynamic_slice(r_gcw, (0, s_me * 768), (4, 768))
    gr_scw = lax.dynamic_slice(jnp.concatenate([r_scw_x, r_scw_bc], axis=1), (0, s_me * 384), (4, 384))

    g_win = _unpad_win(g_win_p).reshape(D, 4, IN_COLS // 4).transpose(1, 0, 2)
    rs_win = rs_begin("rs_win", g_win, 256, sp, True, gr_n1)
    red_r, oth_r = rs_end(rs_rest, rs_win["token"])

    big = {}
    for n, w, m, v, blk0 in (("w_up", w_up, m_w_up, v_w_up, 0), ("w_down", w_down, m_w_down, v_w_down, 4),
                             ("w_out", w_out, m_w_out, v_w_out, 8), ("wq_mem", wq_mem, m_wq_mem, v_wq_mem, 10),
                             ("wk_mem", wk_mem, m_wk_mem, v_wk_mem, 11), ("wv_mem", wv_mem, m_wv_mem, v_wv_mem, 12),
                             ("wo_mem", wo_mem, m_wo_mem, v_wo_mem, 13)):
        big[n] = adam_halves("adam_" + n, w, m, v, red_r, oth_r, 256, blk0, sp)
    red_w, oth_w = rs_end(rs_win, big["wo_mem"][1])
    g_win_t = jnp.where(ci == 0, jnp.concatenate([red_w, oth_w], axis=0), jnp.concatenate([oth_w, red_w], axis=0)).T
    win_rows = g_win_t.shape[0]
    tbw = 152 if win_rows % 152 == 0 else win_rows
    d_t, m_t, v_t = rowwise(lambda i, n, w, g, m, v: _adamw(w, g, m, v), "adam_win", win_rows, tbw,
                            [R(w_in.T), R(g_win_t), R(m_w_in.T), R(v_w_in.T)], [], [(D, F32)] * 3)
    big["w_in"] = (g_win_t.T, d_t.T, m_t.T, v_t.T)
    names_s =["norm1_w", "gdn_conv_w", "gdn_a_log", "gdn_dt_bias", "gdn_norm_w", "ssm_conv_w", "ssm_conv_b",
               "ssm_a_log", "ssm_dt_bias", "ssm_d", "ssm_norm_w", "norm2_w", "mem_norm_w", "norm3_w", "final_norm_w"]
    w_s = [norm1_w, gdn_conv_w, gdn_a_log, gdn_dt_bias, gdn_norm_w, ssm_conv_w, ssm_conv_b, ssm_a_log, ssm_dt_bias,
           ssm_d, ssm_norm_w, norm2_w, mem_norm_w, norm3_w, final_norm_w]
    g_s = [gr_n1, gr_gcw, gr_galog, gr_gdtb, gr_gnw, gr_scw, gr_scb, gr_salog, gr_sdtb, gr_sd, gr_snw, gr_n2,
           gr_nmem, gr_n3, gr_final]
    m_s = [m_norm1_w, m_gdn_conv_w, m_gdn_a_log, m_gdn_dt_bias, m_gdn_norm_w, m_ssm_conv_w, m_ssm_conv_b, m_ssm_a_log,
           m_ssm_dt_bias, m_ssm_d, m_ssm_norm_w, m_norm2_w, m_mem_norm_w, m_norm3_w, m_final_norm_w]
    v_s = [v_norm1_w, v_gdn_conv_w, v_gdn_a_log, v_gdn_dt_bias, v_gdn_norm_w, v_ssm_conv_w, v_ssm_conv_b, v_ssm_a_log,
           v_ssm_dt_bias, v_ssm_d, v_ssm_norm_w, v_norm2_w, v_mem_norm_w, v_norm3_w, v_final_norm_w]
    shp_s = [w.shape for w in w_s]
    as2d = lambda a: a if a.ndim == 2 else a.reshape(1, -1)
    d_l, m_l, v_l = adam_small([as2d(a) for a in w_s], [as2d(a) for a in g_s], [as2d(a) for a in m_s],
                               [as2d(a) for a in v_s])

    grads, deltas, new_m, new_v = {}, {}, {}, {}
    for n, (gg, dd, mm_, vv_) in big.items():
        grads[n], deltas[n], new_m[n], new_v[n] = gg, dd, mm_, vv_
    for k, n in enumerate(names_s):
        grads[n] = g_s[k].reshape(shp_s[k])
        deltas[n], new_m[n], new_v[n] = (a[k].reshape(shp_s[k]) for a in (d_l, m_l, v_l))
    order = ["norm1_w", "w_in", "gdn_conv_w", "gdn_a_log", "gdn_dt_bias", "gdn_norm_w", "ssm_conv_w", "ssm_conv_b",
             "ssm_a_log", "ssm_dt_bias", "ssm_d", "ssm_norm_w", "w_out", "norm2_w", "mem_norm_w", "wq_mem", "wk_mem",
             "wv_mem", "wo_mem", "norm3_w", "w_up", "w_down", "final_norm_w"]
    return (loss, grad_x[None], *[grads[n] for n in order], *[deltas[n] for n in order],
            *[new_m[n] for n in order], *[new_v[n] for n in order])
```

```python
import numpy as np
import jax
import jax.numpy as jnp
from jax import lax
from jax.experimental import pallas as pl
from jax.experimental.pallas import tpu as pltpu

F32, BF16 = jnp.float32, jnp.bfloat16
MESH = pl.DeviceIdType.MESH
ANY = pl.BlockSpec(memory_space=pl.ANY)

EPS = 1e-6
D = 1024
GDN_H, GDN_DK, GDN_C = 8, 128, 64
SSM_H, SSM_P, SSM_N, SSM_L = 16, 64, 128, 128
MEM_H, MEM_DH = 4, 256
D_FF = 4096
IN_COLS = 6688
CB_QKV, CB_Z, CB_ZS, CB_XS, CB_BC, CB_BA, CB_DT = 0, 3, 4, 5, 12, 52, 53
VMEM_LIMIT = 56 * 1024 * 1024
D2D_CHUNKS = 8
ICI_CHUNKS = 4

ADAM_LR, ADAM_B1, ADAM_B2, ADAM_EPS, ADAM_WD, ADAM_STEP = 0.001, 0.9, 0.999, 1e-08, 0.01, 10


def _dg(a, b, ca, cb):
    return lax.dot_general(a, b, (((ca,), (cb,)), ((), ())), preferred_element_type=F32)


def _bf(x):
    return x.astype(BF16)


def mm(a, b):
    return _dg(_bf(a), _bf(b), 1, 0)


def mm_nt(a, b):
    return _dg(_bf(a), _bf(b), 1, 1)


def mm_tn(a, b):
    return _dg(_bf(a), _bf(b), 0, 0)


def mm_sel(a, sel):
    hi = a.astype(BF16)
    r1 = a - hi.astype(F32)
    mid = r1.astype(BF16)
    lo = (r1 - mid.astype(F32)).astype(BF16)
    s = sel.astype(BF16)
    return _dg(hi, s, 1, 0) + (_dg(mid, s, 1, 0) + _dg(lo, s, 1, 0))


def mm3(a, b):
    ah, bh = a.astype(BF16), b.astype(BF16)
    al, bl = (a - ah.astype(F32)).astype(BF16), (b - bh.astype(F32)).astype(BF16)
    return _dg(ah, bh, 1, 0) + (_dg(ah, bl, 1, 0) + _dg(al, bh, 1, 0))


def _iota(shape, dim):
    return lax.broadcasted_iota(jnp.int32, shape, dim)


def _chunk_cumsum(x, c):
    pos = _iota(x.shape, 0) & (c - 1)
    s = 1
    while s < c:
        x = x + jnp.where(pos >= s, pltpu.roll(x, s, 0), 0.0)
        s *= 2
    return x


def _chunk_revcumsum(x, c):
    n = x.shape[0]
    pos = _iota(x.shape, 0) & (c - 1)
    s = 1
    while s < c:
        x = x + jnp.where(pos < c - s, pltpu.roll(x, n - s, 0), 0.0)
        s *= 2
    return x


def _sig(x):
    return jax.nn.sigmoid(x)


def _softplus(x):
    return jnp.maximum(x, 0.0) + jnp.log(1.0 + jnp.exp(-jnp.abs(x)))


def _rows(v):
    return jnp.sum(v, axis=0, keepdims=True)


def _lanes(v):
    return jnp.sum(v, axis=1, keepdims=True)


def _sum_all(v):
    return _rows(_lanes(v))


def _cparams(sem):
    return pltpu.CompilerParams(dimension_semantics=sem, vmem_limit_bytes=VMEM_LIMIT)


def rowwise(fn, name, T, tb, row_ins, full_ins, row_outs, acc_outs=(), sp=None):
    nblk = T // tb
    assert nblk * tb == T
    has_sp = sp is not None

    def imap(f):
        return (lambda i, s: f(i, s)) if has_sp else (lambda i: f(i, None))

    in_specs, args = [], []
    for arr, w, cb, halo, off in row_ins:
        if halo == "col":
            in_specs.append(pl.BlockSpec((w, tb), imap(lambda i, s: (0, i))))
            args.append(arr)
            continue
        rowf = off if callable(off) else (lambda i, s, off=off: i + off)
        in_specs.append(pl.BlockSpec((tb, w), imap(lambda i, s, cb=cb, rowf=rowf: (rowf(i, s), cb))))
        args.append(arr)
        if halo == "prev":
            r = tb // 8
            in_specs.append(pl.BlockSpec((8, w), imap(lambda i, s, cb=cb, r=r: (jnp.maximum(i * r - 1, 0), cb))))
            args.append(arr)
        elif halo == "next":
            r, last = tb // 8, T // 8 - 1
            in_specs.append(pl.BlockSpec((8, w), imap(lambda i, s, cb=cb, r=r, last=last:
                                                      (jnp.minimum((i + 1) * r, last), cb))))
            args.append(arr)
    for arr in full_ins:
        in_specs.append(pl.BlockSpec(arr.shape, imap(lambda i, s, nd=arr.ndim: (0,) * nd)))
        args.append(arr)
    n_in, n_ro = len(args), len(row_outs)
    out_shape, out_specs, aliases = [], [], {}
    for k, (w, dt, *dest) in enumerate(row_outs):
        if dest:
            buf, cb = dest
            out_shape.append(jax.ShapeDtypeStruct(buf.shape, buf.dtype))
            out_specs.append(pl.BlockSpec((tb, w), imap(lambda i, s, cb=cb: (i, cb))))
            if not isinstance(buf, jax.ShapeDtypeStruct):
                aliases[len(args) + int(has_sp)] = k
                in_specs.append(ANY)
                args.append(buf)
        elif w < 0:
            out_shape.append(jax.ShapeDtypeStruct((-w, T), dt))
            out_specs.append(pl.BlockSpec((-w, tb), imap(lambda i, s: (0, i))))
        else:
            out_shape.append(jax.ShapeDtypeStruct((T, w), dt))
            out_specs.append(pl.BlockSpec((tb, w), imap(lambda i, s: (i, 0))))
    for shp in acc_outs:
        out_shape.append(jax.ShapeDtypeStruct(shp, F32))
        out_specs.append(pl.BlockSpec(shp, imap(lambda i, s, nd=len(shp): (0,) * nd)))

    def body(*refs):
        i = pl.program_id(0)
        if has_sp:
            sp_ref, refs = refs[0], refs[1:]
            vals = fn(i, nblk, sp_ref, *[r[...] for r in refs[:n_in]])
        else:
            vals = fn(i, nblk, *[r[...] for r in refs[:n_in]])
        outs = refs[n_in + len(aliases):]
        for ref, val in zip(outs[:n_ro], vals[:n_ro]):
            ref[...] = val.astype(ref.dtype)
        for ref, val in zip(outs[n_ro:], vals[n_ro:]):
            @pl.when(i == 0)
            def _(ref=ref, val=val):
                ref[...] = val

            @pl.when(i > 0)
            def _(ref=ref, val=val):
                ref[...] += val

    cparams = _cparams(("arbitrary",) if acc_outs else ("parallel",))
    if has_sp:
        return pl.pallas_call(
            body, name=name, out_shape=out_shape, compiler_params=cparams, input_output_aliases=aliases,
            grid_spec=pltpu.PrefetchScalarGridSpec(num_scalar_prefetch=1, grid=(nblk,), in_specs=in_specs,
                                                   out_specs=out_specs),
        )(sp, *args)
    return pl.pallas_call(
        body, name=name, grid=(nblk,), in_specs=in_specs, out_specs=out_specs, out_shape=out_shape,
        compiler_params=cparams, input_output_aliases=aliases,
    )(*args)


def R(arr, w=None, cb=0, halo=None, off=0):
    return (arr, arr.shape[1] if w is None else w, cb, halo, off)


def RC(arr):
    return (arr, arr.shape[0], 0, "col", 0)


def matmul(name, a, b, form, tm, tn, tk, out_dtypes, epi=None, extras=(), rows=(), into=None, n_acc=0, b_sel=None):
    bs = b.shape if b_sel is None else b_sel[0]
    if form == "nn":
        (M, K), N = a.shape, bs[1]
    elif form == "nt":
        (M, K), N = a.shape, bs[0]
    else:
        (K, M), N = a.shape, bs[1]
    tm, tn, tk = min(tm, M), min(tn, N), min(tk, K)
    assert M % tm == 0 and N % tn == 0 and K % tk == 0, (name, M, N, K, tm, tn, tk)

    def b_spec_of(blk, at):
        if b_sel is None:
            return pl.BlockSpec(blk, lambda i, j, k: at(i, j, k))
        blk3 = b_sel[1]
        assert int(np.prod([d for d in blk3 if d is not None])) == blk[0] * blk[1], (name, blk3, blk)
        return pl.BlockSpec(blk3, lambda i, j, k: b_sel[2](i, j, k))

    if form == "nn":
        a_spec = pl.BlockSpec((tm, tk), lambda i, j, k: (i, k))
        b_spec = b_spec_of((tk, tn), lambda i, j, k: (k, j))
        ca, cb = 1, 0
    elif form == "nt":
        a_spec = pl.BlockSpec((tm, tk), lambda i, j, k: (i, k))
        b_spec = b_spec_of((tn, tk), lambda i, j, k: (j, k))
        ca, cb = 1, 1
    else:
        a_spec = pl.BlockSpec((tk, tm), lambda i, j, k: (k, i))
        b_spec = b_spec_of((tk, tn), lambda i, j, k: (k, j))
        ca, cb = 0, 0
    nk, ne, no = K // tk, len(extras) + len(rows), len(out_dtypes)
    if epi is None:
        epi = lambda acc: (acc,)

    assert n_acc == 0 or tn == N

    def body(a_ref, b_ref, *rest):
        ex, outs, accs, acc = rest[:ne], rest[ne:ne + no], rest[ne + no:ne + no + n_acc], rest[ne + no + n_acc]
        i, k = pl.program_id(0), pl.program_id(2)

        def finish(total):
            vals = epi(total, *[e[...] for e in ex])
            for r, v in zip(outs, vals[:no]):
                r[...] = v.astype(r.dtype).reshape(r.shape)
            for r, v in zip(accs, vals[no:]):
                @pl.when(i == 0)
                def _(r=r, v=v):
                    r[...] = v

                @pl.when(i > 0)
                def _(r=r, v=v):
                    r[...] += v

        b_tile = b_ref[...]
        if b_sel is not None and len(b_sel) > 3:
            b_tile = jnp.concatenate([b_tile[s] for s in range(b_tile.shape[0])], axis=1)
        prod = _dg(_bf(a_ref[...]), _bf(b_tile.reshape(-1, b_tile.shape[-1])), ca, cb)
        if nk == 1:
            finish(prod)
            return

        @pl.when(k == 0)
        def _():
            acc[...] = prod

        @pl.when(k > 0)
        def _():
            acc[...] += prod

        @pl.when(k == nk - 1)
        def _():
            finish(acc[...])

    mn = pl.BlockSpec((tm, tn), lambda i, j, k: (i, j))
    rw = pl.BlockSpec((1, tn), lambda i, j, k: (0, j))
    acc_scratch = pltpu.VMEM((tm, tn) if nk > 1 else (8, 128), F32)
    if into is not None:
        buf, blk, bmap = into
        assert ne == 0 and no == 1
        aliased = not isinstance(buf, jax.ShapeDtypeStruct)

        def body_into(a_ref, b_ref, *rest):
            body(a_ref, b_ref, *rest[-2:])

        return pl.pallas_call(
            body_into, name=name, grid=(M // tm, N // tn, nk),
            in_specs=[a_spec, b_spec] + ([ANY] if aliased else []), out_specs=pl.BlockSpec(blk, bmap),
            out_shape=jax.ShapeDtypeStruct(buf.shape, buf.dtype),
            scratch_shapes=[acc_scratch],
            input_output_aliases={2: 0} if aliased else {},
            compiler_params=_cparams(("parallel", "parallel", "arbitrary")),
        )(a, b, *([buf] if aliased else []))
    return pl.pallas_call(
        body, name=name, grid=(M // tm, N // tn, nk),
        in_specs=[a_spec, b_spec] + [mn] * len(extras) + [rw] * len(rows), out_specs=[mn] * no + [rw] * n_acc,
        out_shape=[jax.ShapeDtypeStruct((M, N), dt) for dt in out_dtypes] + [jax.ShapeDtypeStruct((1, N), F32)] * n_acc,
        scratch_shapes=[acc_scratch],
        compiler_params=_cparams(("arbitrary",) * 3 if n_acc else ("parallel", "parallel", "arbitrary")),
    )(a, b, *extras, *rows)


def _epi_res(acc, res):
    return (res + acc,)


def _epi_rms_bwd(acc, x, dres, w):
    return rms_bwd_fn(0, 0, x, acc, dres, w)


def rms_bwd1_fn(i, n, x, dh, dres, w):
    dx, _, gw = rms_bwd_fn(i, n, x, dh, dres, w)
    return dx, gw


def _epi_final(acc, res, tgt, w):
    return final_fn(0, 0, res + acc, tgt, w)


def _epi_res_rms(acc, res, w):
    x = res + acc
    return (x, x * lax.rsqrt(jnp.mean(x * x, axis=-1, keepdims=True) + EPS) * w)


def _epi_relu2(acc):
    u = jnp.maximum(acc, 0.0)
    return (u, u * u)


def _epi_dup(acc, u):
    return (acc * 2.0 * u.astype(F32),)


def _conv(x, halo, w, i):
    halo = jnp.where(i == 0, 0.0, halo)
    xt = jnp.concatenate([halo, x], axis=0)
    shifted = [pltpu.roll(xt, 3 - k, 0)[8:, :] for k in range(3)] + [x]
    y = shifted[3] * w[3:4, :]
    for k in range(3):
        y = y + shifted[k] * w[k:k + 1, :]
    return y, shifted


def _l2n(x, scale):
    outs = []
    for h in range(x.shape[1] // 128):
        xh = x[:, 128 * h:128 * h + 128]
        outs.append(xh * (lax.rsqrt(jnp.sum(xh * xh, axis=-1, keepdims=True) + EPS) * scale))
    return jnp.concatenate(outs, axis=1)


def _l2n_bwd(x, dy, scale):
    outs = []
    for h in range(x.shape[1] // 128):
        xh, dh = x[:, 128 * h:128 * h + 128], dy[:, 128 * h:128 * h + 128] * scale
        r = lax.rsqrt(jnp.sum(xh * xh, axis=-1, keepdims=True) + EPS)
        outs.append(r * dh - xh * (r * r * r) * jnp.sum(xh * dh, axis=-1, keepdims=True))
    return jnp.concatenate(outs, axis=1)


def rms_fwd_fn(i, n, x, w):
    r = lax.rsqrt(jnp.mean(x * x, axis=-1, keepdims=True) + EPS)
    return (x * r * w,)


def rms_bwd_fn(i, n, x, dh, dres, w):
    r = lax.rsqrt(jnp.mean(x * x, axis=-1, keepdims=True) + EPS)
    g = dh * w
    dx = dres + r * g - x * (r * r * r) * jnp.mean(x * g, axis=-1, keepdims=True)
    return dx, dx, _rows(dh * x * r)


def rms_bwd_w_fn(i, n, x, dh, w):
    r = lax.rsqrt(jnp.mean(x * x, axis=-1, keepdims=True) + EPS)
    return (_rows(dh * x * r),)


def final_fn(i, n, x, tgt, w):
    r = lax.rsqrt(jnp.mean(x * x, axis=-1, keepdims=True) + EPS)
    xn = x * r
    e = xn * w - tgt
    dy = e * (1.0 / D)
    g = dy * w
    dx = r * g - x * (r * r * r) * jnp.mean(x * g, axis=-1, keepdims=True)
    return dx, dx, _rows(e * e), _rows(dy * xn)


def _gdn_gates(ba, alog_c, dtb_c):
    col = _iota(ba.shape, 1)
    amask = (col >= 8) & (col < 16)
    beta = jnp.where(col < 8, _sig(ba), 0.0)
    z = ba + dtb_c
    ea_ = jnp.exp(alog_c)
    return beta, z, ea_, jnp.where(amask, -ea_ * _softplus(z), 0.0), amask


def _cols(x, g):
    return x[:, 128 * g:128 * g + 128]


def gdn_prep_fn(i, n, qkv, halo, ba, cw, alog_c, dtb_c, eb, ea):
    outs = [[], [], []]
    for g in range(3 * GDN_H):
        yc, _ = _conv(_cols(qkv, g), _cols(halo, g), _cols(cw, g), i)
        act = yc * _sig(yc)
        if g < 2 * GDN_H:
            act = _l2n(act, GDN_DK ** -0.5 if g < GDN_H else 1.0)
        outs[g // GDN_H].append(act)
    beta, _, _, gg, _ = _gdn_gates(ba, alog_c, dtb_c)
    gcs = _chunk_cumsum(gg, GDN_C)
    return (*[jnp.concatenate(o, axis=1) for o in outs], mm_sel(gcs, ea), mm_sel(beta, eb), beta + gcs,
            jnp.transpose(gcs)[8:16, :])


def gdn_prep_bwd_fn(i, n, qkv, halo, ba, dqn, dkn, dv, dgb, dgcs_t, cw, alog_c, dtb_c):
    dycs, dwl = [], [[], [], [], []]
    for g in range(3 * GDN_H):
        yc, shifted = _conv(_cols(qkv, g), _cols(halo, g), _cols(cw, g), i)
        sg = _sig(yc)
        act = yc * sg
        if g < GDN_H:
            d = _l2n_bwd(act, _cols(dqn, g), GDN_DK ** -0.5)
        elif g < 2 * GDN_H:
            d = _l2n_bwd(act, _cols(dkn, g - GDN_H), 1.0)
        else:
            d = _cols(dv, g - 2 * GDN_H)
        dyc_g = d * (sg * (1.0 + yc * (1.0 - sg)))
        dycs.append(dyc_g)
        for k in range(4):
            dwl[k].append(_rows(dyc_g * shifted[k]))
    dyc = jnp.concatenate(dycs, axis=1)
    dws = [jnp.concatenate(l, axis=1) for l in dwl]
    beta, z, ea_, g, amask = _gdn_gates(ba, alog_c, dtb_c)
    tbn = ba.shape[0]
    rowpart = jnp.transpose(jnp.concatenate([jnp.zeros((8, tbn), F32), dgcs_t, jnp.zeros((112, tbn), F32)], axis=0))
    dg = _chunk_revcumsum(jnp.where(amask, dgb, 0.0) - rowpart, GDN_C)
    draw = jnp.where(amask, dg * (-ea_) * _sig(z), 0.0)
    dba = draw + dgb * beta * (1.0 - beta)
    return (dyc, dba, dws[0], dws[1], dws[2], dws[3], _rows(dg * g), _rows(draw))


def conv_bwd_fn(i, n, dyc, halo, w):
    halo = jnp.where(i == n - 1, 0.0, halo)
    tb = dyc.shape[0]
    outs = []
    for g in range(dyc.shape[1] // 128):
        d, wg = _cols(dyc, g), _cols(w, g)
        xt = jnp.concatenate([d, _cols(halo, g)], axis=0)
        dx = d * wg[3:4, :]
        for k in range(3):
            dx = dx + pltpu.roll(xt, tb + 8 - (3 - k), 0)[:tb, :] * wg[k:k + 1, :]
        outs.append(dx)
    return (jnp.concatenate(outs, axis=1),)


def gdn_post_fn(i, n, o, z, w):
    outs = []
    for h in range(GDN_H):
        oh, zh = o[:, 128 * h:128 * h + 128], z[:, 128 * h:128 * h + 128]
        r = lax.rsqrt(jnp.mean(oh * oh, axis=-1, keepdims=True) + EPS)
        outs.append(oh * r * w * (zh * _sig(zh)))
    return (jnp.concatenate(outs, axis=1),)


def gdn_post_bwd_fn(i, n, o, z, doa, w):
    dos, dzs, dw = [], [], None
    for h in range(GDN_H):
        sl = slice(128 * h, 128 * h + 128)
        oh, zh, dh = o[:, sl], z[:, sl], doa[:, sl]
        r = lax.rsqrt(jnp.mean(oh * oh, axis=-1, keepdims=True) + EPS)
        s = _sig(zh)
        dn = dh * (zh * s)
        dzs.append(dh * (oh * r * w) * (s * (1.0 + zh * (1.0 - s))))
        t = _rows(dn * oh * r)
        dw = t if dw is None else dw + t
        g = dn * w
        dos.append(r * g - oh * (r * r * r) * jnp.mean(oh * g, axis=-1, keepdims=True))
    return jnp.concatenate(dos, axis=1), jnp.concatenate(dzs, axis=1), dw


def _ssd_gates(dtblk, alog_c, dtb_c):
    hmask = _iota(dtblk.shape, 1) < SSM_H
    z = dtblk + dtb_c
    return jnp.where(hmask, _softplus(z), 0.0), -jnp.exp(alog_c), z, hmask


def _silu_conv_cols(x, halo, w, b, i):
    outs = []
    for g in range(x.shape[1] // 128):
        yc, _ = _conv(_cols(x, g), _cols(halo, g), _cols(w, g), i)
        yc = yc + _cols(b, g)
        outs.append(yc * _sig(yc))
    return jnp.concatenate(outs, axis=1)


def _silu_conv_bwd_cols(x, halo, w, b, dout, i):
    dycs, dwl = [], [[], [], [], []]
    for g in range(x.shape[1] // 128):
        yc, shifted = _conv(_cols(x, g), _cols(halo, g), _cols(w, g), i)
        yc = yc + _cols(b, g)
        s = _sig(yc)
        dyc_g = _cols(dout, g) * (s * (1.0 + yc * (1.0 - s)))
        dycs.append(dyc_g)
        for k in range(4):
            dwl[k].append(_rows(dyc_g * shifted[k]))
    dyc = jnp.concatenate(dycs, axis=1)
    return dyc, [jnp.concatenate(l, axis=1) for l in dwl], _rows(dyc)


def ssd_prep_fn(i, n, xp, hx, bcp, hbc, dtblk, cwx, cwbc, cbx, cbbc, alog_c, dtb_c, e16):
    dt, a_neg, _, _ = _ssd_gates(dtblk, alog_c, dtb_c)
    acs = _chunk_cumsum(dt * a_neg, SSM_L)
    return (_silu_conv_cols(xp, hx, cwx, cbx, i), _silu_conv_cols(bcp, hbc, cwbc, cbbc, i), mm_sel(dt, e16),
            mm_sel(acs, e16), jnp.transpose(acs)[0:SSM_H, :])


def ssd_prep_bwd_fn(i, n, xp, hx, bcp, hbc, dtblk, dxs_a, dxs_b, db, dc, dgate, dacs_t, cwx, cwbc, cbx, cbbc, alog_c, dtb_c):
    dyx, dwx, dbx = _silu_conv_bwd_cols(xp, hx, cwx, cbx, dxs_a + dxs_b, i)
    dybc, dwbc, dbbc = _silu_conv_bwd_cols(bcp, hbc, cwbc, cbbc, jnp.concatenate([db, dc], axis=1), i)
    dt, a_neg, z, hmask = _ssd_gates(dtblk, alog_c, dtb_c)
    g0, g1 = dgate[:, :128], dgate[:, 128:]
    col = _iota(g0.shape, 1)
    lo, mid = col < 8, (col >= 8) & (col < 16)
    dacs_col = jnp.where(lo, g0, 0.0) + pltpu.roll(jnp.where(lo, g1, 0.0), 8, 1)
    ddt_dir = pltpu.roll(jnp.where(mid, g0, 0.0), 120, 1) + jnp.where(mid, g1, 0.0)
    tbn = dtblk.shape[0]
    rowpart = jnp.transpose(jnp.concatenate([dacs_t, jnp.zeros((128 - SSM_H, tbn), F32)], axis=0))
    da = _chunk_revcumsum(dacs_col - rowpart, SSM_L)
    draw = jnp.where(hmask, (ddt_dir + da * a_neg) * _sig(z), 0.0)
    return (dyx, dybc, draw, *dwx, *dwbc, dbx, dbbc, _rows(da * dt * a_neg), _rows(draw))


def _ssd_gate(y, xs, zs, d_x):
    y2 = y + xs * d_x
    s = _sig(zs)
    return y2, s, y2 * (zs * s)


def ssd_post_fn(i, n, y, xs, zs, d_x, nw):
    _, _, yg = _ssd_gate(y, xs, zs, d_x)
    outs = []
    for g in range(2):
        v = yg[:, 512 * g:512 * g + 512]
        outs.append(v * lax.rsqrt(jnp.mean(v * v, axis=-1, keepdims=True) + EPS))
    return (jnp.concatenate(outs, axis=1) * nw,)


def ssd_post_bwd_fn(i, n, y, xs, zs, dob, d_x, nw):
    y2, s, yg = _ssd_gate(y, xs, zs, d_x)
    gfull = dob * nw
    dygs, dnw = [], []
    for g in range(2):
        sl = slice(512 * g, 512 * g + 512)
        v, gg = yg[:, sl], gfull[:, sl]
        r = lax.rsqrt(jnp.mean(v * v, axis=-1, keepdims=True) + EPS)
        dygs.append(r * gg - v * (r * r * r) * jnp.mean(v * gg, axis=-1, keepdims=True))
        dnw.append(_rows(dob[:, sl] * v * r))
    dyg = jnp.concatenate(dygs, axis=1)
    dy2 = dyg * (zs * s)
    dzs = dyg * y2 * (s * (1.0 + zs * (1.0 - s)))
    return dy2, dy2 * d_x, dzs, jnp.concatenate(dnw, axis=1), _rows(dy2 * xs)


def _attn_probs(q, k):
    hs = [slice(MEM_DH * h, MEM_DH * h + MEM_DH) for h in range(MEM_H)]
    ss = [mm_nt(q[:, sl], k[:, sl]) * (MEM_DH ** -0.5) for sl in hs]
    es = [jnp.exp(s - jnp.max(s, axis=-1, keepdims=True)) for s in ss]
    return hs, [e / jnp.sum(e, axis=-1, keepdims=True) for e in es]


def attn_fn(i, n, q, k, v):
    hs, ps = _attn_probs(q, k)
    return (jnp.concatenate([mm(p, v[:, sl]) for p, sl in zip(ps, hs)], axis=1),)


def attn_bwd_fn(i, n, q, do, k, v):
    hs, ps = _attn_probs(q, k)
    dvs = [mm_tn(p, do[:, sl]) for p, sl in zip(ps, hs)]
    dps = [mm_nt(do[:, sl], v[:, sl]) for sl in hs]
    dss = [p * (dp - jnp.sum(dp * p, axis=-1, keepdims=True)) * (MEM_DH ** -0.5) for p, dp in zip(ps, dps)]
    dqs = [mm(ds, k[:, sl]) for ds, sl in zip(dss, hs)]
    dks = [mm_tn(ds, q[:, sl]) for ds, sl in zip(dss, hs)]
    return jnp.concatenate(dqs, axis=1), jnp.concatenate(dks, axis=1), jnp.concatenate(dvs, axis=1)


def add2_fn(i, n, sp, a, b):
    return (a + b,)


def sum4_fn(i, n, sp, a, b, c, d):
    return (((a.astype(F32) + b.astype(F32)) + c.astype(F32)) + d.astype(F32),)


def _adamw(w, g, m, v):
    m = ADAM_B1 * m + (1.0 - ADAM_B1) * g
    v = ADAM_B2 * v + (1.0 - ADAM_B2) * (g * g)
    m_hat = m / (1.0 - ADAM_B1 ** ADAM_STEP)
    v_hat = v / (1.0 - ADAM_B2 ** ADAM_STEP)
    delta = -ADAM_LR * (m_hat / (jnp.sqrt(v_hat) + ADAM_EPS) + ADAM_WD * w)
    return delta, m, v


def _gate_cols(gb, h):
    lane = _iota(gb.shape, 1)
    return _lanes(jnp.where(lane == h, gb, 0.0)), _lanes(jnp.where(lane == 8 + h, gb, 0.0))


def _gdn_stage1(q, k, v, bb, gcs, grow):
    C = GDN_C
    row, col = _iota((C, C), 0), _iota((C, C), 1)
    incl, strict = row >= col, row > col
    dmat = jnp.where(incl, jnp.exp(jnp.minimum((gcs if gcs.shape[1] == 1 else gcs[:, :C]) - grow, 0.0)), 0.0)
    gam = jnp.exp(gcs)
    gl = gcs[C - 1:C, :]
    kb, vb = k * bb, v * bb
    kg = kb * gam
    lmat = jnp.where(strict, mm_nt(kb, k) * dmat, 0.0)
    pmat = jnp.where(incl, mm_nt(q, k) * dmat, 0.0)
    return dict(q=q, k=k, v=v, bb=bb, incl=incl, strict=strict, dmat=dmat, gam=gam, kb=kb, vb=vb, kg=kg,
                lmat=lmat, pmat=pmat, qd=q * gam, kdec=jnp.exp(gl - gcs), cd=jnp.exp(gl))


def _gdn_inverse(lmats):
    C = GDN_C
    eye = (_iota((C, C), 0) == _iota((C, C), 1)).astype(F32)
    xs = [-l for l in lmats]
    ts = [eye + x for x in xs]
    for _ in range(5):
        xs = [mm(x, x) for x in xs]
        ts = [t + mm(t, x) for t, x in zip(ts, xs)]
    res = [eye - mm3(eye + l, t) for l, t in zip(lmats, ts)]
    return [t + mm(t, r) for t, r in zip(ts, res)]


def gdn_fwd(qn, kn, v, gcs_x, beta_x, gcs_t, tb, gh):
    T = qn.shape[0]
    nb, ncb, nc, C = T // tb, tb // GDN_C, T // GDN_C, GDN_C
    idx = [(hh, c) for hh in range(gh) for c in range(ncb)]

    def body(q_ref, k_ref, v_ref, g_ref, b_ref, gt_ref, o_ref, st_ref, ti_ref, s_scr):
        @pl.when(pl.program_id(1) == 0)
        def _():
            s_scr[...] = jnp.zeros_like(s_scr)

        grows = [gt_ref[hh] for hh in range(gh)]
        at = lambda hh, c: (slice(C * c, C * (c + 1)), slice(128 * hh, 128 * hh + 128))
        st1 = []
        for hh, c in idx:
            sl, ln = at(hh, c)
            st1.append(_gdn_stage1(q_ref[sl, ln], k_ref[sl, ln], v_ref[sl, ln], b_ref[sl, ln], g_ref[sl, ln],
                                   grows[hh][:, sl]))
        tinvs = _gdn_inverse([s["lmat"] for s in st1])
        us = [mm(t, s["vb"]) for t, s in zip(tinvs, st1)]
        ws = [mm(t, s["kg"]) for t, s in zip(tinvs, st1)]
        kds = [s["k"] * s["kdec"] for s in st1]
        ms = [mm_tn(kd, w) for kd, w in zip(kds, ws)]
        bs = [mm_tn(kd, u) for kd, u in zip(kds, us)]
        gs = [s["qd"] - mm(s["pmat"], w) for s, w in zip(st1, ws)]
        pus = [mm(s["pmat"], u) for s, u in zip(st1, us)]
        ss = [s_scr[hh] for hh in range(gh)]
        for c in range(ncb):
            for hh in range(gh):
                n, (sl, ln) = hh * ncb + c, at(hh, c)
                ti_ref[hh, sl, :] = tinvs[n]
                st_ref[hh, c] = ss[hh]
                o_ref[sl, ln] = mm(gs[n], ss[hh]) + pus[n]
                ss[hh] = st1[n]["cd"] * ss[hh] - mm(ms[n], ss[hh]) + bs[n]
        for hh in range(gh):
            s_scr[hh] = ss[hh]

    blk = pl.BlockSpec((tb, 128 * gh), lambda h, i: (i, h))
    return pl.pallas_call(
        body, name="gdn_fwd", grid=(GDN_H // gh, nb),
        in_specs=[blk] * 5 + [pl.BlockSpec((gh, 1, tb), lambda h, i: (h, 0, i))],
        out_specs=[blk, pl.BlockSpec((gh, ncb, 128, 128), lambda h, i: (h, i, 0, 0)),
                   pl.BlockSpec((gh, tb, C), lambda h, i: (h, i, 0))],
        out_shape=[jax.ShapeDtypeStruct((T, D), F32), jax.ShapeDtypeStruct((GDN_H, nc, 128, 128), F32),
                   jax.ShapeDtypeStruct((GDN_H, T, C), F32)],
        scratch_shapes=[pltpu.VMEM((gh, 128, 128), F32)],
        compiler_params=_cparams(("parallel", "arbitrary")),
    )(qn, kn, v, gcs_x, beta_x, gcs_t)


def gdn_bwd(qn, kn, v, gb, gcs_t, do, states, tinv, tb, gh):
    T = qn.shape[0]
    nb, ncb, C = T // tb, tb // GDN_C, GDN_C
    assert gh == GDN_H

    def body(q_ref, k_ref, v_ref, gb_ref, gt_ref, do_ref, st_ref, ti_ref,
             dq_ref, dk_ref, dv_ref, dgb_ref, dgr_ref, ds_scr):
        @pl.when(pl.program_id(1) == 0)
        def _():
            ds_scr[...] = jnp.zeros_like(ds_scr)

        grows = [gt_ref[hh] for hh in range(gh)]
        at = lambda hh, c: (slice(C * c, C * (c + 1)), slice(128 * hh, 128 * hh + 128))
        lastrow = _iota((C, 1), 0) == C - 1
        lane = _iota((C, 128), 1)
        idx = [(hh, c) for hh in range(gh) for c in range(ncb)]
        P = []
        for hh, c in idx:
            sl, ln = at(hh, c)
            lc = _gdn_stage1(q_ref[sl, ln], k_ref[sl, ln], v_ref[sl, ln], *_gate_cols(gb_ref[sl, :], hh), grows[hh][:, sl])
            lc.update(tinv=ti_ref[hh, sl, :], s=st_ref[hh, c], do=do_ref[sl, ln], kd=lc["k"] * lc["kdec"])
            P.append(lc)
        for l, u, w in zip(P, [mm(l["tinv"], l["vb"]) for l in P], [mm(l["tinv"], l["kg"]) for l in P]):
            l.update(u=u, w=w)
        for l, x in zip(P, [mm(l["w"], l["s"]) for l in P]):
            l["vn"] = l["u"] - x
        for l, a, b, c_, d in zip(P, [mm_nt(l["do"], l["s"]) for l in P], [mm_nt(l["do"], l["vn"]) for l in P],
                                  [mm_tn(l["qd"], l["do"]) for l in P], [mm_tn(l["pmat"], l["do"]) for l in P]):
            l.update(dqd=a, dp=jnp.where(l["incl"], b, 0.0), ds_q=c_, dvn_p=d)
        pre = dict(zip(idx, P))
        rows = {}
        hs = range(gh)
        ds = [ds_scr[hh] for hh in hs]
        for c in reversed(range(ncb)):
            L = [pre[hh, c] for hh in hs]
            dvn = [l["dvn_p"] + mm(l["kd"], d) for l, d in zip(L, ds)]
            dkd = [mm_nt(l["vn"], d) for l, d in zip(L, ds)]
            dcd = [_sum_all(l["s"] * d) for l, d in zip(L, ds)]
            ds = [l["ds_q"] + l["cd"] * d - mm_tn(l["w"], x) for l, d, x in zip(L, ds, dvn)]
            dw = [-mm_nt(x, l["s"]) for l, x in zip(L, dvn)]
            dvb = [mm_tn(l["tinv"], x) for l, x in zip(L, dvn)]
            dkg = [mm_tn(l["tinv"], x) for l, x in zip(L, dw)]
            da = [-jnp.where(l["strict"], mm_nt(a, l["u"]) + mm_nt(b, l["w"]), 0.0) for l, a, b in zip(L, dvb, dkg)]
            dm = [a * l["dmat"] for l, a in zip(L, da)]
            dn = [l["dp"] * l["dmat"] for l in L]
            dkb = [mm(a, l["k"]) for l, a in zip(L, dm)]
            dq = [mm(a, l["k"]) + l["gam"] * l["dqd"] for l, a in zip(L, dn)]
            dk = [mm_tn(a, l["kb"]) + mm_tn(b, l["q"]) for l, a, b in zip(L, dm, dn)]
            dgb = jnp.zeros((C, 128), F32)
            for hh in hs:
                sl, ln = at(hh, c)
                l = L[hh]
                e = da[hh] * l["lmat"] + l["dp"] * l["pmat"]
                t_kd = _lanes(dkd[hh] * l["kd"])
                dgl = _sum_all(t_kd) + dcd[hh] * l["cd"][:, :1]
                dgcs = (_lanes(e) + _lanes(l["dqd"] * l["qd"]) - t_kd + _lanes(dkg[hh] * l["kg"])
                        + jnp.where(lastrow, dgl, 0.0))
                rows[hh, c] = _rows(e)
                dq_ref[sl, ln] = dq[hh]
                dk_ref[sl, ln] = (dk[hh] + l["kdec"] * dkd[hh] + l["bb"] * l["gam"] * dkg[hh] + l["bb"] * dkb[hh])
                dv_ref[sl, ln] = l["bb"] * dvb[hh]
                dbeta = _lanes(dkg[hh] * l["gam"] * l["k"]) + _lanes(dvb[hh] * l["v"]) + _lanes(dkb[hh] * l["k"])
                dgb = dgb + jnp.where(lane == hh, dbeta, 0.0) + jnp.where(lane == 8 + hh, dgcs, 0.0)
            dgb_ref[slice(C * c, C * (c + 1)), :] = dgb
        for hh in hs:
            ds_scr[hh] = ds[hh]
            dgr_ref[hh] = jnp.concatenate([rows[hh, c] for c in range(ncb)], axis=1)

    blk = pl.BlockSpec((tb, 128 * gh), lambda h, i: (nb - 1 - i, h))
    rowspec = pl.BlockSpec((gh, 1, tb), lambda h, i: (h, 0, nb - 1 - i))
    cblk = pl.BlockSpec((tb, 128), lambda h, i: (nb - 1 - i, 0))
    return pl.pallas_call(
        body, name="gdn_bwd", grid=(GDN_H // gh, nb),
        in_specs=[blk] * 3 + [cblk, rowspec, blk,
                              pl.BlockSpec((gh, ncb, 128, 128), lambda h, i: (h, nb - 1 - i, 0, 0)),
                              pl.BlockSpec((gh, tb, C), lambda h, i: (h, nb - 1 - i, 0))],
        out_specs=[blk] * 3 + [cblk, rowspec],
        out_shape=[jax.ShapeDtypeStruct((T, D), F32)] * 3 + [jax.ShapeDtypeStruct((T, 128), F32),
                                                             jax.ShapeDtypeStruct((GDN_H, 1, T), F32)],
        scratch_shapes=[pltpu.VMEM((gh, 128, 128), F32)],
        compiler_params=_cparams(("parallel", "arbitrary")),
    )(qn, kn, v, gb, gcs_t, do, states, tinv)


def _ssd_pair(x2, dt2, acs2):
    last = acs2[SSM_L - 1:SSM_L, :]
    return jnp.exp(acs2), jnp.exp(last - acs2), x2 * dt2


def _ssd_head(hh, acs2, arow, dec2, cbm, bm, incl, col):
    lmask = (col >= 64 * hh) & (col < 64 * hh + 64)
    sg = jnp.where(incl, jnp.exp(jnp.minimum(acs2[:, 64 * hh:64 * hh + 1] - arow, 0.0)), 0.0)
    dec_col = dec2[:, 64 * hh:64 * hh + 1]
    return lmask, sg, sg * cbm, dec_col, bm * dec_col


def ssd_fwd(xs, bc, dt_x, acs_x, acs_t):
    T = xs.shape[0]
    nc, L = T // SSM_L, SSM_L

    def body(x_ref, bc_ref, dt_ref, ac_ref, at_ref, y_ref, hst_ref, h_scr):
        @pl.when(pl.program_id(0) == 0)
        def _():
            h_scr[...] = jnp.zeros_like(h_scr)

        row, col = _iota((L, L), 0), _iota((L, L), 1)
        incl = row >= col
        P, H = [], []
        for gp in range(8):
            g = gp // 4
            bm, cm = bc_ref[:, 128 * g:128 * g + 128], bc_ref[:, 256 + 128 * g:384 + 128 * g]
            cbm = mm_nt(cm, bm) if gp % 4 == 0 else cbm
            sl = slice(128 * gp, 128 * gp + 128)
            acs2 = ac_ref[:, sl]
            lam2, dec2, xd2 = _ssd_pair(x_ref[:, sl], dt_ref[:, sl], acs2)
            P.append(dict(sl=sl, lam2=lam2, xd2=xd2, hprev=h_scr[gp], cm=cm))
            for hh in range(2):
                lmask, _, mmat, _, bd = _ssd_head(hh, acs2, at_ref[2 * gp + hh], dec2, cbm, bm, incl, col)
                H.append(dict(mmat=mmat, bd=bd, xdh=jnp.where(lmask, xd2, 0.0), xd2=xd2))
        ys = [mm(h["mmat"], h["xdh"]) for h in H]
        sts = [mm_tn(h["xd2"], h["bd"]) for h in H]
        zs = [mm_nt(p["cm"], p["hprev"]) for p in P]
        for gp, p in enumerate(P):
            hst_ref[gp // 4, gp % 4] = p["hprev"]
            y_ref[:, p["sl"]] = ys[2 * gp] + ys[2 * gp + 1] + p["lam2"] * zs[gp]
            lam_rows = jnp.where(row < 64, p["lam2"][L - 1:L, 0:1], p["lam2"][L - 1:L, 64:65])
            h_scr[gp] = lam_rows * p["hprev"] + jnp.where(row < 64, sts[2 * gp], sts[2 * gp + 1])

    blk = pl.BlockSpec((L, D), lambda c: (c, 0))
    return pl.pallas_call(
        body, name="ssd_fwd", grid=(nc,),
        in_specs=[blk, pl.BlockSpec((L, 512), lambda c: (c, 0)), blk, blk, pl.BlockSpec((SSM_H, 1, L), lambda c: (0, 0, c))],
        out_specs=[blk, pl.BlockSpec((2, None, 4, 128, 128), lambda c: (0, c, 0, 0, 0))],
        out_shape=[jax.ShapeDtypeStruct((T, D), F32), jax.ShapeDtypeStruct((2, nc, 4, 128, 128), F32)],
        scratch_shapes=[pltpu.VMEM((8, 128, 128), F32)],
        compiler_params=_cparams(("arbitrary",)),
    )(xs, bc, dt_x, acs_x, acs_t)


def ssd_bwd(xs, bc, dt_x, acs_x, acs_t, dy, hstates):
    T = xs.shape[0]
    nc, L = T // SSM_L, SSM_L

    def body(x_ref, bc_ref, dt_ref, ac_ref, at_ref, dy_ref, hst_ref,
             dx_ref, db_ref, dc_ref, dgate_ref, dar_ref, dh_scr):
        @pl.when(pl.program_id(0) == 0)
        def _():
            dh_scr[...] = jnp.zeros_like(dh_scr)

        row, col = _iota((L, L), 0), _iota((L, L), 1)
        rowc = _iota((L, 1), 0)
        incl = row >= col
        G = [dict(bm=bc_ref[:, 128 * g:128 * g + 128], cm=bc_ref[:, 256 + 128 * g:384 + 128 * g]) for g in range(2)]
        for gr in G:
            gr["cbm"] = mm_nt(gr["cm"], gr["bm"])
        P = []
        for gp in range(8):
            sl = slice(128 * gp, 128 * gp + 128)
            x2, dt2, dy2, acs2 = x_ref[:, sl], dt_ref[:, sl], dy_ref[:, sl], ac_ref[:, sl]
            lam2, dec2, xd2 = _ssd_pair(x2, dt2, acs2)
            P.append(dict(sl=sl, gr=G[gp // 4], x2=x2, dt2=dt2, dy2=dy2, acs2=acs2, lam2=lam2, dec2=dec2, xd2=xd2,
                          hprev=hst_ref[gp // 4, gp % 4], dhn=dh_scr[gp], dz=lam2 * dy2))
        zs = [mm_nt(p["gr"]["cm"], p["hprev"]) for p in P]
        dcm_t = [mm(p["dz"], p["hprev"]) for p in P]
        dh_z = [mm_tn(p["dz"], p["gr"]["cm"]) for p in P]
        H = []
        for gp, p in enumerate(P):
            p["yoff"] = p["dz"] * zs[gp]
            p["q_rows"] = _lanes(p["dhn"] * p["hprev"])
            for hh in range(2):
                lmask, sg, mmat, dec_col, bd = _ssd_head(hh, p["acs2"], at_ref[2 * gp + hh], p["dec2"], p["gr"]["cbm"],
                                                         p["gr"]["bm"], incl, col)
                H.append(dict(p=p, hh=hh, j=2 * gp + hh, lmask=lmask, sg=sg, mmat=mmat, dec_col=dec_col, bd=bd))
        dms = [mm_nt(jnp.where(h["lmask"], h["p"]["dy2"], 0.0), h["p"]["xd2"]) for h in H]
        a1s = [mm_tn(h["mmat"], h["p"]["dy2"]) for h in H]
        a2s = [mm_nt(h["bd"], h["p"]["dhn"]) for h in H]
        dbds = [mm(jnp.where(h["lmask"], h["p"]["xd2"], 0.0), h["p"]["dhn"]) for h in H]
        for gr in G:
            gr.update(dcb=jnp.zeros((L, L), F32), dbm=jnp.zeros((L, SSM_N), F32), comp=jnp.zeros((L, 128), F32))
        dxd = [jnp.zeros((L, 128), F32) for _ in P]
        for h, dm_raw, a1, a2, dbd in zip(H, dms, a1s, a2s, dbds):
            p, hh, j = h["p"], h["hh"], h["j"]
            gr, jg = p["gr"], j % 8
            dm = jnp.where(incl, dm_raw, 0.0)
            gr["dcb"] = gr["dcb"] + dm * h["sg"]
            e = dm * h["mmat"]
            dxd_h = jnp.where(h["lmask"], a1 + a2, 0.0)
            dxd[j // 2] = dxd[j // 2] + dxd_h
            gr["dbm"] = gr["dbm"] + h["dec_col"] * dbd
            t = _lanes(dbd * h["bd"])
            lam_h = p["lam2"][L - 1:L, 64 * hh:64 * hh + 1]
            in_head = (rowc >= 64 * hh) & (rowc < 64 * hh + 64)
            add_last = _sum_all(t) + _sum_all(jnp.where(in_head, p["q_rows"], 0.0)) * lam_h
            dacs_col = (_lanes(jnp.where(h["lmask"], p["yoff"], 0.0)) + _lanes(e) - t
                        + jnp.where(rowc == L - 1, add_last, 0.0))
            ddt_col = _lanes(dxd_h * p["x2"])
            dar_ref[j] = _rows(e)
            gr["comp"] = gr["comp"] + jnp.where(col == jg, dacs_col, 0.0) + jnp.where(col == 8 + jg, ddt_col, 0.0)
        for gp, p in enumerate(P):
            lam_rows = jnp.where(row < 64, p["lam2"][L - 1:L, 0:1], p["lam2"][L - 1:L, 64:65])
            dh_scr[gp] = dh_z[gp] + lam_rows * p["dhn"]
            dx_ref[:, p["sl"]] = p["dt2"] * dxd[gp]
        for g, gr in enumerate(G):
            lanes = slice(128 * g, 128 * g + 128)
            dcm = (dcm_t[4 * g] + dcm_t[4 * g + 1]) + (dcm_t[4 * g + 2] + dcm_t[4 * g + 3])
            db_ref[:, lanes] = gr["dbm"] + mm_tn(gr["dcb"], gr["cm"])
            dc_ref[:, lanes] = dcm + mm(gr["dcb"], gr["bm"])
            dgate_ref[:, lanes] = gr["comp"]

    rv = lambda c: (nc - 1 - c, 0)
    blk, blk256 = pl.BlockSpec((L, D), rv), pl.BlockSpec((L, 256), rv)
    rowspec = pl.BlockSpec((SSM_H, 1, L), lambda c: (0, 0, nc - 1 - c))
    return pl.pallas_call(
        body, name="ssd_bwd", grid=(nc,),
        in_specs=[blk, pl.BlockSpec((L, 512), rv), blk, blk, rowspec, blk,
                  pl.BlockSpec((2, None, 4, 128, 128), lambda c: (0, nc - 1 - c, 0, 0, 0))],
        out_specs=[blk, blk256, blk256, blk256, rowspec],
        out_shape=[jax.ShapeDtypeStruct((T, D), F32), jax.ShapeDtypeStruct((T, 256), F32),
                   jax.ShapeDtypeStruct((T, 256), F32), jax.ShapeDtypeStruct((T, 256), F32),
                   jax.ShapeDtypeStruct((SSM_H, 1, T), F32)],
        scratch_shapes=[pltpu.VMEM((8, 128, 128), F32)],
        compiler_params=_cparams(("arbitrary",)),
    )(xs, bc, dt_x, acs_x, acs_t, dy, hstates)


def _pos():
    return lax.axis_index("x"), lax.axis_index("y"), lax.axis_index("c")


def _other_chips(x, y):
    return [(1 - x, y), (x, 1 - y), (1 - x, 1 - y)]


def _rcopy(src, dst, ssem, rsem, dev):
    return pltpu.make_async_remote_copy(src_ref=src, dst_ref=dst, send_sem=ssem, recv_sem=rsem,
                                        device_id=dev, device_id_type=MESH)


def _rows_at(start, n):
    return pl.ds(pl.multiple_of(start, 8), n)


def _comm_call(body, name, out_shape, n_in, scratch):
    return pl.pallas_call(
        body, name=name, out_shape=out_shape, in_specs=[ANY] * n_in,
        out_specs=[ANY] * len(out_shape) if isinstance(out_shape, (list, tuple)) else ANY,
        scratch_shapes=scratch,
        compiler_params=pltpu.CompilerParams(has_side_effects=True),
    )


def _dma_sems(n):
    return pltpu.SemaphoreType.DMA((n,))


def ag_chips(name, shard):
    rr, cc = shard.shape
    h, nq = rr // 2, ICI_CHUNKS
    hq = h // nq

    def body(x_ref, out_ref, ssem, rsem):
        x, y, c = _pos()
        me_s = 2 * x + y
        chips = _other_chips(x, y)
        started = []
        for q in range(nq):
            rows = _rows_at(c * h + q * hq, hq)
            for j, (cx, cy) in enumerate(chips):
                cp = _rcopy(x_ref.at[rows], out_ref.at[me_s, rows], ssem.at[j * nq + q], rsem.at[j * nq + q], (cx, cy, c))
                cp.start()
                started.append(cp)
        for q in range(nq):
            rows = _rows_at(c * h + q * hq, hq)
            for j, (cx, cy) in enumerate(chips):
                blk = out_ref.at[2 * cx + cy, rows]
                _rcopy(blk, blk, ssem.at[j * nq + q], rsem.at[j * nq + q], (cx, cy, c)).wait_recv()
                k = 3 * nq + j * nq + q
                cp = _rcopy(blk, blk, ssem.at[k], rsem.at[k], (x, y, 1 - c))
                cp.start()
                started.append(cp)
        for q in range(nq):
            rows = _rows_at((1 - c) * h + q * hq, hq)
            for j, (cx, cy) in enumerate(chips):
                blk = out_ref.at[2 * cx + cy, rows]
                k = 3 * nq + j * nq + q
                _rcopy(blk, blk, ssem.at[k], rsem.at[k], (x, y, 1 - c)).wait_recv()
        for cp in started:
            cp.wait_send()

    return _comm_call(body, name, jax.ShapeDtypeStruct((4, rr, cc), shard.dtype), 1,
                      [_dma_sems(6 * nq), _dma_sems(6 * nq)])(shard)


def _with_own(shard, got, s_me):
    return lax.dynamic_update_index_in_dim(got, shard, s_me, 0)


def all_gather_chips(name, shard, s_me):
    return _with_own(shard, ag_chips(name, shard), s_me)


HBM_SPEC = pl.BlockSpec(memory_space=pltpu.HBM)
SEM_SPEC = pl.BlockSpec(memory_space=pltpu.SEMAPHORE)
SPLIT_EFFECT = pltpu.SideEffectType.DATAFLOW_SIDE_EFFECTING


def _split_copies(pieces, x_ref, land_ref, sems, arriving):
    x, y, c = _pos()
    return [_rcopy(s, d_in if arriving else d_out, sems[j], sems[3 + j], dev)
            for j, (s, d_out, d_in, dev) in enumerate(pieces(x_ref, land_ref, x, y, c))]


def split_copy_start(name, src, land_shape, pieces, after):
    def body(x_ref, land_ref, after_ref, *outs):
        for cp in _split_copies(pieces, x_ref, land_ref, outs[:6], False):
            cp.start()
        outs[8][...] = jnp.zeros_like(outs[8])

    dma = pltpu.SemaphoreType.DMA(())
    res = pl.pallas_call(
        body, name=name,
        out_shape=(dma,) * 6 + (pltpu.HBM(src.shape, src.dtype), pltpu.HBM(land_shape, src.dtype),
                                jax.ShapeDtypeStruct((8, 128), F32)),
        in_specs=(HBM_SPEC, HBM_SPEC, ANY),
        out_specs=(SEM_SPEC,) * 6 + (HBM_SPEC, HBM_SPEC, pl.BlockSpec(memory_space=pltpu.VMEM)),
        input_output_aliases={0: 6, 1: 7},
        compiler_params=pltpu.CompilerParams(has_side_effects=SPLIT_EFFECT),
    )(pltpu.with_memory_space_constraint(src, pltpu.HBM),
      pltpu.with_memory_space_constraint(lax.empty(land_shape, src.dtype), pltpu.HBM), after)
    return res[:6], res[6], res[7], res[8]


def split_copy_wait(name, sems, src_thru, land_thru, after, pieces):
    def body(x_ref, land_ref, *rest):
        for cp in _split_copies(pieces, x_ref, land_ref, rest[:6], False):
            cp.wait_send()
        for cp in _split_copies(pieces, x_ref, land_ref, rest[:6], True):
            cp.wait_recv()

    return pl.pallas_call(
        body, name=name,
        out_shape=(pltpu.HBM(src_thru.shape, src_thru.dtype), pltpu.HBM(land_thru.shape, land_thru.dtype)),
        in_specs=(HBM_SPEC, HBM_SPEC) + (SEM_SPEC,) * 6 + (ANY,), out_specs=(HBM_SPEC, HBM_SPEC),
        input_output_aliases={0: 0, 1: 1},
        compiler_params=pltpu.CompilerParams(has_side_effects=SPLIT_EFFECT),
    )(src_thru, land_thru, *sems, after)


def ag_pieces(h):
    def pieces(x_ref, land_ref, x, y, c):
        rows = _rows_at(c * h, h)
        return [(x_ref.at[rows], land_ref.at[2 * x + y, rows], land_ref.at[2 * cx + cy, rows], (cx, cy, c))
                for cx, cy in _other_chips(x, y)]
    return pieces


def rs_pieces(x_ref, land_ref, x, y, c):
    return [(x_ref.at[2 * cx + cy], land_ref.at[j], land_ref.at[j], (cx, cy, c))
            for j, (cx, cy) in enumerate(_other_chips(x, y))]


def ag_forward(name, got):
    _, rr, cc = got.shape
    h, nq = rr // 2, D2D_CHUNKS
    hq = h // nq

    def body(g_ref, out_ref, ssem, rsem):
        x, y, c = _pos()
        slots = [2 * cx + cy for cx, cy in _other_chips(x, y)]
        cps = []
        for j, s in enumerate(slots):
            for q in range(nq):
                blk = out_ref.at[s, _rows_at(c * h + q * hq, hq)]
                cp = _rcopy(blk, blk, ssem.at[j * nq + q], rsem.at[j * nq + q], (x, y, 1 - c))
                cp.start()
                cps.append(cp)
        for cp in cps:
            cp.wait_send()
        for j, s in enumerate(slots):
            for q in range(nq):
                blk = out_ref.at[s, _rows_at((1 - c) * h + q * hq, hq)]
                _rcopy(blk, blk, ssem.at[j * nq + q], rsem.at[j * nq + q], (x, y, 1 - c)).wait_recv()

    return pl.pallas_call(
        body, name=name, out_shape=jax.ShapeDtypeStruct(got.shape, got.dtype), in_specs=[ANY], out_specs=ANY,
        scratch_shapes=[_dma_sems(3 * nq), _dma_sems(3 * nq)], input_output_aliases={0: 0},
        compiler_params=pltpu.CompilerParams(has_side_effects=True),
    )(got)


def rs_pair(name, g):
    _, rr, cc = g.shape
    h, nq = rr // 2, D2D_CHUNKS
    hq = h // nq

    def body(g_ref, recv_ref, ssem, rsem):
        x, y, c = _pos()
        cps = []
        for q in range(nq):
            cp = _rcopy(g_ref.at[:, _rows_at((1 - c) * h + q * hq, hq), :], recv_ref.at[:, pl.ds(q * hq, hq), :],
                        ssem.at[q], rsem.at[q], (x, y, 1 - c))
            cp.start()
            cps.append(cp)
        for cp in cps:
            cp.wait()

    return _comm_call(body, name, jax.ShapeDtypeStruct((4, h, cc), g.dtype), 1, [_dma_sems(nq), _dma_sems(nq)])(g)


def rs_chips(name, p):
    _, h, cc = p.shape
    nq = ICI_CHUNKS
    hq = h // nq

    def body(p_ref, buf_ref, ssem, rsem):
        x, y, c = _pos()
        sends = []
        for q in range(nq):
            rows = pl.ds(q * hq, hq)
            for j, (cx, cy) in enumerate(_other_chips(x, y)):
                cp = _rcopy(p_ref.at[2 * cx + cy, rows], buf_ref.at[j, rows], ssem.at[j * nq + q],
                            rsem.at[j * nq + q], (cx, cy, c))
                cp.start()
                sends.append(cp)
        for cp in sends:
            cp.wait()

    return _comm_call(body, name, jax.ShapeDtypeStruct((3, h, cc), p.dtype), 1,
                      [_dma_sems(3 * nq), _dma_sems(3 * nq)])(p)


def rs_join(name, half):
    h, cc = half.shape
    nq = D2D_CHUNKS
    hq = h // nq

    def body(h_ref, out_ref, ssem, rsem):
        x, y, c = _pos()
        cps = []
        for q in range(nq):
            rows = pl.ds(q * hq, hq)
            cp = _rcopy(h_ref.at[rows], out_ref.at[rows], ssem.at[q], rsem.at[q], (x, y, 1 - c))
            cp.start()
            cps.append(cp)
        for cp in cps:
            cp.wait()

    return _comm_call(body, name, jax.ShapeDtypeStruct((h, cc), half.dtype), 1, [_dma_sems(nq), _dma_sems(nq)])(half)


def reduce_scatter(tag, g, tb, sp):
    return rs_end(rs_begin(tag, g, tb, sp, False), None)


def rs_begin(tag, g, tb, sp, split, after=None):
    _, rr, cc = g.shape
    h = rr // 2
    nbh = h // tb
    recv = rs_pair(tag + "_pair", g)
    mine_rows = lambda i, s: (i // nbh) * (2 * nbh) + s[0] * nbh + i % nbh
    part = rowwise(add2_fn, tag + "_add", 4 * h, tb, [R(g.reshape(4 * rr, cc), off=mine_rows), R(recv.reshape(4 * h, cc))],
                   [], [(cc, BF16)], sp=sp)[0].reshape(4, h, cc)
    st = dict(tag=tag, tb=tb, sp=sp, split=split, part=part)
    if split:
        st["sems"], st["part"], st["land"], st["token"] = split_copy_start(tag + "_start", part, (3, h, cc), rs_pieces,
                                                                           sp if after is None else after)
    return st


def rs_end(st, after):
    tag, tb, sp, part = st["tag"], st["tb"], st["sp"], st["part"]
    _, h, cc = part.shape
    nbh = h // tb
    if st["split"]:
        part, buf = split_copy_wait(tag + "_wait", st["sems"], part, st["land"], after, rs_pieces)
    else:
        buf = rs_chips(tag + "_chips", part)
    red = rowwise(sum4_fn, tag + "_sum", h, tb,
                  [R(part.reshape(4 * h, cc), off=lambda i, s: s[1] * nbh + i)]
                  + [R(buf.reshape(3 * h, cc), off=k * nbh) for k in range(3)],
                  [], [(cc, F32)], sp=sp)[0]
    return red, rs_join(tag + "_join", red)


def adam_halves(name, w, m, v, red, other, tb, blk0, sp):
    nbh = red.shape[0] // tb

    def fn(i, n, s, w_, m_, v_, r_, o_):
        g = jnp.where((blk0 + i) // nbh == s[0], r_, o_)
        return (g,) + _adamw(w_, g, m_, v_)

    half_rows = lambda i, s: (blk0 + i) % nbh
    return rowwise(fn, name, w.shape[0], tb, [R(w), R(m), R(v), R(red, off=half_rows), R(other, off=half_rows)],
                   [], [(w.shape[1], F32)] * 4, sp=sp)


SMALL_LANES = D


def all_reduce_items(name, items, after=None):
    flat = [a for it in items for a in it]
    shapes = [(sum(a.shape[0] for a in it), it[0].shape[1]) for it in items]
    extra = [] if after is None else [after]
    plan, a_idx, brow = [], 0, 0
    for o_idx, it in enumerate(items):
        orow = 0
        for a in it:
            ra, n = a.shape
            for lane0 in range(0, n, SMALL_LANES):
                plan.append((a_idx, o_idx, orow, ra, lane0, min(SMALL_LANES, n - lane0), brow))
                brow += ra
            orow += ra
            a_idx += 1
    nrows = -(-brow // 8) * 8

    def body(*refs):
        ins, refs = refs[:len(flat)], refs[len(flat) + len(extra):]
        outs = refs[:len(items)]
        mine, buf, ssem, rsem = refs[len(items):]
        x, y, c = _pos()
        me = 4 * x + 2 * y + c
        mine[...] = jnp.zeros_like(mine)
        for ai, _, _, ra, lane0, w, br in plan:
            mine[br:br + ra, 0:w] = ins[ai][:, lane0:lane0 + w]
        buf[me] = mine[...]
        cps = []
        for k in range(1, 8):
            dev = (x ^ (k >> 2), y ^ ((k >> 1) & 1), c ^ (k & 1))
            cp = _rcopy(mine, buf.at[me], ssem.at[k - 1], rsem.at[k - 1], dev)
            cp.start()
            cps.append(cp)
        for cp in cps:
            cp.wait()
        for _, oi, orow, ra, lane0, w, br in plan:
            acc = buf[0, br:br + ra, 0:w]
            for d in range(1, 8):
                acc = acc + buf[d, br:br + ra, 0:w]
            outs[oi][orow:orow + ra, lane0:lane0 + w] = acc

    vm = pl.BlockSpec(memory_space=pltpu.VMEM)
    return pl.pallas_call(
        body, name=name, out_shape=[jax.ShapeDtypeStruct(s, F32) for s in shapes],
        in_specs=[vm] * len(flat) + [ANY] * len(extra), out_specs=[vm] * len(items),
        scratch_shapes=[pltpu.VMEM((nrows, SMALL_LANES), F32), pltpu.VMEM((8, nrows, SMALL_LANES), F32),
                        _dma_sems(7), _dma_sems(7)],
        compiler_params=pltpu.CompilerParams(has_side_effects=True),
    )(*flat, *extra)


def adam_small(ws, gs, ms, vs):
    n = len(ws)

    def body(*refs):
        for k in range(n):
            w, g, m, v = (refs[j * n + k][...] for j in range(4))
            for j, val in enumerate(_adamw(w, g, m, v)):
                refs[(4 + j) * n + k][...] = val

    vm = pl.BlockSpec(memory_space=pltpu.VMEM)
    res = pl.pallas_call(
        body, name="adam_small", out_shape=[jax.ShapeDtypeStruct(w.shape, F32) for w in ws] * 3,
        in_specs=[vm] * (4 * n), out_specs=[vm] * (3 * n),
    )(*ws, *gs, *ms, *vs)
    return res[:n], res[n:2 * n], res[2 * n:]


def _sel(rows, cols, pairs):
    m = np.zeros((rows, cols), np.float32)
    for r, c in pairs:
        m[r, c] = 1.0
    return jnp.asarray(m)


def _pad_win(w):
    z = jnp.zeros((w.shape[0], 112), w.dtype)
    return jnp.concatenate([w[:, :4096], w[:, 4112:6672], w[:, 4096:4112], z, w[:, 6672:6688], z], axis=1)


def _unpad_win(wp):
    return jnp.concatenate([wp[:, :4096], wp[:, 6656:6672], wp[:, 4096:6656], wp[:, 6784:6800]], axis=1)


def kernel(x, mem, norm1_w, w_in, gdn_conv_w, gdn_a_log, gdn_dt_bias, gdn_norm_w, ssm_conv_w, ssm_conv_b, ssm_a_log, ssm_dt_bias, ssm_d, ssm_norm_w, w_out, norm2_w, mem_norm_w, wq_mem, wk_mem, wv_mem, wo_mem, norm3_w, w_up, w_down, final_norm_w, loss_target, m_norm1_w, m_w_in, m_gdn_conv_w, m_gdn_a_log, m_gdn_dt_bias, m_gdn_norm_w, m_ssm_conv_w, m_ssm_conv_b, m_ssm_a_log, m_ssm_dt_bias, m_ssm_d, m_ssm_norm_w, m_w_out, m_norm2_w, m_mem_norm_w, m_wq_mem, m_wk_mem, m_wv_mem, m_wo_mem, m_norm3_w, m_w_up, m_w_down, m_final_norm_w, v_norm1_w, v_w_in, v_gdn_conv_w, v_gdn_a_log, v_gdn_dt_bias, v_gdn_norm_w, v_ssm_conv_w, v_ssm_conv_b, v_ssm_a_log, v_ssm_dt_bias, v_ssm_d, v_ssm_norm_w, v_w_out, v_norm2_w, v_mem_norm_w, v_wq_mem, v_wk_mem, v_wv_mem, v_wo_mem, v_norm3_w, v_w_up, v_w_down, v_final_norm_w):
    T, M = x.shape[1], mem.shape[1]
    xi, yi, ci = _pos()
    s_me = 2 * xi + yi
    x0, mem0, tgt = x[0], mem[0], loss_target[0]
    tb = min(512, T)
    tbl = min(1024, T)
    tbp = min(512, T)
    row = lambda v: v.reshape(1, -1)

    win_g = all_gather_chips("ag_win", w_in.astype(BF16), s_me)
    w_in_p = _pad_win(win_g.transpose(1, 0, 2).reshape(D, IN_COLS))
    keep = (ci == 0).astype(F32)
    gcw_z = lax.dynamic_update_slice(jnp.zeros((4, 3 * D), F32), gdn_conv_w * keep, (0, s_me * 768))
    scw_z = lax.dynamic_update_slice(jnp.zeros((4, 1536), F32), ssm_conv_w * keep, (0, s_me * 384))
    gcw, scw = all_reduce_items("ar_convw", [[gcw_z], [scw_z]])
    scw_x, scw_bc = scw[:, :D], scw[:, D:]
    rest_shard = jnp.concatenate([w_up, w_down, w_out, wq_mem, wk_mem, wv_mem, wo_mem], axis=0).astype(BF16)
    ag_sems, rest_thru, rest_land, ag_token = split_copy_start("ag_rest_start", rest_shard, (4,) + rest_shard.shape,
                                                               ag_pieces(rest_shard.shape[0] // 2), gcw)
    sp = jnp.stack([ci, s_me]).astype(jnp.int32)
    scb_x, scb_bc = row(ssm_conv_b[:D]), row(ssm_conv_b[D:])

    galog_c, gdtb_c = row(jnp.pad(gdn_a_log, (8, 112))), row(jnp.pad(gdn_dt_bias, (8, 112)))
    salog_c, sdtb_c = row(jnp.pad(ssm_a_log, (0, 112))), row(jnp.pad(ssm_dt_bias, (0, 112)))
    sd_x = row(jnp.repeat(ssm_d, 64))
    eb = _sel(128, D, [(h, 128 * h + l) for h in range(8) for l in range(128)])
    ea = _sel(128, D, [(8 + h, 128 * h + l) for h in range(8) for l in range(128)])
    e16 = _sel(128, D, [(h, 64 * h + l) for h in range(16) for l in range(64)])

    h1 = rowwise(rms_fwd_fn, "rms1", T, tbl, [R(x0)], [row(norm1_w) + ag_token[0:1, 0:1]], [(D, BF16)])[0]
    p = matmul("mm_in", h1, w_in_p, "nn", 2048, 768, 1024, [F32])[0]
    gp_ins = [R(p, 3 * D, CB_QKV, "prev"), R(p, 128, CB_BA)]
    qn, kn, vv, gcs_x, beta_x, ggate, gcs_t = rowwise(gdn_prep_fn, "gdn_prep", T, tbp, gp_ins,
                                                      [gcw, galog_c, gdtb_c, eb, ea],
                                                      [(D, F32)] * 5 + [(128, F32), (-8, F32)])
    gcs_t = gcs_t.reshape(GDN_H, 1, T)
    gtb, ggh = min(128, T), 8
    o_gdn, s_states, tinv = gdn_fwd(qn, kn, vv, gcs_x, beta_x, gcs_t, gtb, ggh)
    gnw = row(gdn_norm_w)
    oa = rowwise(gdn_post_fn, "gdn_post", T, tbl, [R(o_gdn), R(p, D, CB_Z)], [gnw], [(D, BF16)])[0]
    sp_ins = [R(p, D, CB_XS, "prev"), R(p, 512, CB_BC, "prev"), R(p, 128, CB_DT)]
    sp_full = [scw_x, scw_bc, scb_x, scb_bc, salog_c, sdtb_c]
    xs, bc, dt_x, acs_x, acs_t = rowwise(ssd_prep_fn, "ssd_prep", T, tbp, sp_ins, sp_full + [e16],
                                         [(D, F32), (512, F32), (D, F32), (D, F32), (-SSM_H, F32)])
    acs_t = acs_t.reshape(SSM_H, 1, T)
    y_ssd, h_states = ssd_fwd(xs, bc, dt_x, acs_x, acs_t)
    snw = row(ssm_norm_w)
    ob = rowwise(ssd_post_fn, "ssd_post", T, tbl, [R(y_ssd), R(xs), R(p, D, CB_ZS)], [sd_x, snw], [(D, BF16)])[0]
    rest_thru, rest_land = split_copy_wait("ag_rest_wait", ag_sems, rest_thru, rest_land, ob,
                                           ag_pieces(rest_shard.shape[0] // 2))
    rest_g = _with_own(rest_thru, ag_forward("ag_rest_fwd", rest_land), s_me)
    assert D == 1024
    view = lambda shape, blk, at: dict(b_sel=(shape, blk, at))
    wup_n = view((D, D_FF), (None, D, D), lambda i, j, k: (j, 0, 0))
    wup_t = view((D, D_FF), (None, D, D), lambda i, j, k: (k, 0, 0))
    wdown_n = view((D_FF, D), (None, D, D), lambda i, j, k: (k, 1, 0))
    wdown_t = view((D_FF, D), (None, D, D), lambda i, j, k: (j, 1, 0))
    wout_a = view((D, D), (2, 512, D), lambda i, j, k: (0, 4, 0))
    wout_b = view((D, D), (2, 512, D), lambda i, j, k: (1, 4, 0))
    wq_v, wk_v, wv_v, wo_v = (view((D, D), (4, 256, D), lambda i, j, k, r=r: (0, r, 0)) for r in (10, 11, 12, 13))
    x1a = matmul("mm_out_a", oa, rest_g, "nn", 1024, 1024, 1024, [F32], _epi_res, [x0], **wout_a)[0]
    x1, h2 = matmul("mm_out_b", ob, rest_g, "nn", 1024, 1024, 1024, [F32, BF16], _epi_res_rms, [x1a],
                    [row(norm2_w)], **wout_b)

    mn = rowwise(rms_fwd_fn, "rms_mem", M, M, [R(mem0)], [row(mem_norm_w)], [(D, BF16)])[0]
    km = matmul("mm_k", mn, rest_g, "nn", 256, 1024, 1024, [BF16], **wk_v)[0]
    vm = matmul("mm_v", mn, rest_g, "nn", 256, 1024, 1024, [BF16], **wv_v)[0]
    qm = matmul("mm_q", h2, rest_g, "nn", 1024, 1024, 1024, [BF16], **wq_v)[0]
    ao = rowwise(attn_fn, "attn", T, tbl, [R(qm)], [km, vm], [(D, BF16)])[0]
    x2, h3 = matmul("mm_o", ao, rest_g, "nn", 1024, 1024, 1024, [F32, BF16], _epi_res_rms, [x1], [row(norm3_w)], **wo_v)
    u, act = matmul("mm_up", h3, rest_g, "nn", 2048, 1024, 1024, [BF16, BF16], _epi_relu2, **wup_n)
    wdown_n2 = view((D_FF, D), (2, D, D), lambda i, j, k: (k, 1, 0))
    x3 = matmul("mm_down", act, rest_g, "nn", 1024, 1024, 2048, [F32], _epi_res, [x2], **wdown_n2)[0]
    dx3, dx3b, loss_lane, g_final = rowwise(final_fn, "final", T, tbl, [R(x3), R(tgt)], [row(final_norm_w)],
                                            [(D, F32), (D, BF16)], [(1, D), (1, D)])
    loss = lax.psum(0.5 / D * jnp.sum(loss_lane), ("x", "y", "c"))

    dup = matmul("mm_dact", dx3b, rest_g, "nt", 2048, 1024, 1024, [BF16], _epi_dup, [u], **wdown_t)[0]
    def g_into(buf, blk, at):
        return dict(into=(buf, blk, lambda i, j, k, at=at: at(i, j)))

    grest = jax.ShapeDtypeStruct((4, 3584, D), F32)
    grest = matmul("mm_gdown", act, dx3b, "tn", 1024, 1024, 4096, [F32],
                   **g_into(grest, (None, 1024, D), lambda i, j: (i, 1, 0)))
    wup_t4 = dict(b_sel=((D, D_FF), (4, D, D), lambda i, j, k: (0, 0, 0), "side by side"))
    dh3 = matmul("mm_dh3", dup, rest_g, "nt", 1024, 1024, 4096, [F32], **wup_t4)[0]
    dx2, dx2b, g_n3 = rowwise(rms_bwd_fn, "rms3_bwd", T, tbl, [R(x2), R(dh3), R(dx3)], [row(norm3_w)],
                              [(D, F32), (D, BF16)], [(1, D)])
    grest = matmul("mm_gup", h3, dup, "tn", 1024, 1024, 4096, [F32],
                   **g_into(grest, (None, 1024, D), lambda i, j: (j, 0, 0)))
    dao = matmul("mm_dao", dx2b, rest_g, "nt", 1024, 1024, 1024, [F32], **wo_v)[0]
    grest = matmul("mm_gwo", ao, dx2b, "tn", 1024, 1024, 2048, [F32],
                   **g_into(grest, (4, 256, D), lambda i, j: (0, 13, 0)))
    dqm, dkm, dvm = rowwise(attn_bwd_fn, "attn_bwd", T, tb, [R(qm), R(dao)], [km, vm], [(D, BF16)],
                            [(M, D), (M, D)])
    dx1, dx1b, g_n2 = matmul("mm_dh2", dqm, rest_g, "nt", 512, 1024, 1024, [F32, BF16], _epi_rms_bwd, [x1, dx2],
                             [row(norm2_w)], n_acc=1, **wq_v)
    grest = matmul("mm_gwq", h2, dqm, "tn", 1024, 1024, 2048, [F32],
                   **g_into(grest, (4, 256, D), lambda i, j: (0, 10, 0)))
    grest = matmul("mm_gwk", mn, dkm, "tn", 1024, 1024, 256, [F32],
                   **g_into(grest, (4, 256, D), lambda i, j: (0, 11, 0)))
    grest = matmul("mm_gwv", mn, dvm, "tn", 1024, 1024, 256, [F32],
                   **g_into(grest, (4, 256, D), lambda i, j: (0, 12, 0)))
    dmn_k = matmul("mm_dmk", dkm, rest_g, "nt", 256, 1024, 1024, [F32], **wk_v)[0]
    dmn = matmul("mm_dmv", dvm, rest_g, "nt", 256, 1024, 1024, [F32], _epi_res, [dmn_k], **wv_v)[0]
    g_nmem = rowwise(rms_bwd_w_fn, "rmsmem_bwd", M, M, [R(mem0), R(dmn)], [row(mem_norm_w)], [], [(1, D)])[0]
    doa = matmul("mm_doa", dx1b, rest_g, "nt", 2048, 1024, 1024, [F32], **wout_a)[0]
    dob = matmul("mm_dob", dx1b, rest_g, "nt", 2048, 1024, 1024, [F32], **wout_b)[0]
    grest = matmul("mm_gwout_a", oa, dx1b, "tn", 1024, 1024, 2048, [F32],
                   **g_into(grest, (2, 512, D), lambda i, j: (0, 4, 0)))
    grest = matmul("mm_gwout_b", ob, dx1b, "tn", 1024, 1024, 2048, [F32],
                   **g_into(grest, (2, 512, D), lambda i, j: (1, 4, 0)))

    rs_rest = rs_begin("rs_rest", grest, 256, sp, True)

    dp = jax.ShapeDtypeStruct((T, p.shape[1]), BF16)
    dy_ssd, dxs_dir, dp, g_snw, g_sd_lane = rowwise(
        ssd_post_bwd_fn, "ssd_post_bwd", T, tb, [R(y_ssd), R(xs), R(p, D, CB_ZS), R(dob)],
        [sd_x + rs_rest["token"][0:1, 0:1], snw],
        [(D, F32), (D, F32), (D, BF16, dp, CB_ZS)], [(1, D), (1, D)])
    dxs_scan, db_s, dc_s, dgate, dacs_t = ssd_bwd(xs, bc, dt_x, acs_x, acs_t, dy_ssd, h_states)
    spb = rowwise(ssd_prep_bwd_fn, "ssd_prep_bwd", T, tbp,
                  sp_ins + [R(dxs_scan), R(dxs_dir), R(db_s), R(dc_s), R(dgate), RC(dacs_t.reshape(SSM_H, T))], sp_full,
                  [(D, F32), (512, F32), (128, BF16, dp, CB_DT)],
                  [(1, D)] * 4 + [(1, 512)] * 4 + [(1, D), (1, 512), (1, 128), (1, 128)])
    dyc_x, dyc_bc, dp = spb[:3]
    dp = rowwise(conv_bwd_fn, "conv_bwd_x", T, tbp, [R(dyc_x, halo="next")], [scw_x], [(D, BF16, dp, CB_XS)])[0]
    dp = rowwise(conv_bwd_fn, "conv_bwd_bc", T, tbp, [R(dyc_bc, halo="next")], [scw_bc], [(512, BF16, dp, CB_BC)])[0]

    do_gdn, dp, g_gnw = rowwise(gdn_post_bwd_fn, "gdn_post_bwd", T, tb, [R(o_gdn), R(p, D, CB_Z), R(doa)], [gnw],
                                [(D, F32), (D, BF16, dp, CB_Z)], [(1, 128)])
    dqn, dkn, dvv, dggate, dgcs_t = gdn_bwd(qn, kn, vv, ggate, gcs_t, do_gdn, s_states, tinv, gtb, ggh)
    gpb = rowwise(gdn_prep_bwd_fn, "gdn_prep_bwd", T, tbp,
                  gp_ins + [R(dqn), R(dkn), R(dvv), R(dggate), RC(dgcs_t.reshape(GDN_H, T))],
                  [gcw, galog_c, gdtb_c],
                  [(3 * D, F32), (128, BF16, dp, CB_BA)], [(1, 3 * D)] * 4 + [(1, 128), (1, 128)])
    dyc_qkv, dp = gpb[:2]
    dp = rowwise(conv_bwd_fn, "conv_bwd_qkv", T, tbp, [R(dyc_qkv, halo="next")], [gcw], [(3 * D, BF16, dp, CB_QKV)])[0]
    dh1 = matmul("mm_dh1", dp, w_in_p, "nt", 1024, 1024, 2304, [F32])[0]
    grad_x, g_n1 = rowwise(rms_bwd1_fn, "rms1_bwd", T, tbl, [R(x0), R(dh1), R(dx1)], [row(norm1_w)], [(D, F32)], [(1, D)])
    g_win_p = matmul("mm_gwin", h1, dp, "tn", 1024, 768, 4096, [F32])[0]

    items = [[g_n1], [gpb[6]], [gpb[7]], [g_gnw], [spb[11]], [spb[12]], [spb[13]], [spb[14]], [g_sd_lane], [g_snw],
             [g_n2], [g_nmem], [g_n3], [g_final], list(gpb[2:6]), list(spb[3:7]), list(spb[7:11])]
    (gr_n1, r_galog, r_gdtb, gr_gnw, r_scb_x, r_scb_bc, r_salog, r_sdtb, r_sd, gr_snw, gr_n2, gr_nmem, gr_n3,
     gr_final, r_gcw, r_scw_x, r_scw_bc) = all_reduce_items("ar_grads", items)
    gr_galog, gr_gdtb = r_galog[:, 8:16], r_gdtb[:, 8:16]
    gr_salog, gr_sdtb = r_salog[:, :SSM_H], r_sdtb[:, :SSM_H]
    gr_sd = r_sd.reshape(SSM_H, SSM_P).sum(axis=1).reshape(1, SSM_H)
    gr_scb = jnp.concatenate([r_scb_x, r_scb_bc], axis=1)
    gr_gcw = lax.dynamic_slice(r_gcw, (0, s_me * 768), (4, 768))
    gr_scw = lax.dynamic_slice(jnp.concatenate([r_scw_x, r_scw_bc], axis=1), (0, s_me * 384), (4, 384))

    g_win = _unpad_win(g_win_p).reshape(D, 4, IN_COLS // 4).transpose(1, 0, 2)
    rs_win = rs_begin("rs_win", g_win, 256, sp, True, gr_n1)
    red_r, oth_r = rs_end(rs_rest, rs_win["token"])

    big = {}
    for n, w, m, v, blk0 in (("w_up", w_up, m_w_up, v_w_up, 0), ("w_down", w_down, m_w_down, v_w_down, 4),
                             ("w_out", w_out, m_w_out, v_w_out, 8), ("wq_mem", wq_mem, m_wq_mem, v_wq_mem, 10),
                             ("wk_mem", wk_mem, m_wk_mem, v_wk_mem, 11), ("wv_mem", wv_mem, m_wv_mem, v_wv_mem, 12),
                             ("wo_mem", wo_mem, m_wo_mem, v_wo_mem, 13)):
        big[n] = adam_halves("adam_" + n, w, m, v, red_r, oth_r, 256, blk0, sp)
    red_w, oth_w = rs_end(rs_win, big["wo_mem"][1])
    g_win_t = jnp.where(ci == 0, jnp.concatenate([red_w, oth_w], axis=0), jnp.concatenate([oth_w, red_w], axis=0)).T
    win_rows = g_win_t.shape[0]
    tbw = 152 if win_rows % 152 == 0 else win_rows
    d_t, m_t, v_t = rowwise(lambda i, n, w, g, m, v: _adamw(w, g, m, v), "adam_win", win_rows, tbw,
                            [R(w_in.T), R(g_win_t), R(m_w_in.T), R(v_w_in.T)], [], [(D, F32)] * 3)
    big["w_in"] = (g_win_t.T, d_t.T, m_t.T, v_t.T)
    names_s =["norm1_w", "gdn_conv_w", "gdn_a_log", "gdn_dt_bias", "gdn_norm_w", "ssm_conv_w", "ssm_conv_b",
               "ssm_a_log", "ssm_dt_bias", "ssm_d", "ssm_norm_w", "norm2_w", "mem_norm_w", "norm3_w", "final_norm_w"]
    w_s = [norm1_w, gdn_conv_w, gdn_a_log, gdn_dt_bias, gdn_norm_w, ssm_conv_w, ssm_conv_b, ssm_a_log, ssm_dt_bias,
           ssm_d, ssm_norm_w, norm2_w, mem_norm_w, norm3_w, final_norm_w]
    g_s = [gr_n1, gr_gcw, gr_galog, gr_gdtb, gr_gnw, gr_scw, gr_scb, gr_salog, gr_sdtb, gr_sd, gr_snw, gr_n2,
           gr_nmem, gr_n3, gr_final]
    m_s = [m_norm1_w, m_gdn_conv_w, m_gdn_a_log, m_gdn_dt_bias, m_gdn_norm_w, m_ssm_conv_w, m_ssm_conv_b, m_ssm_a_log,
           m_ssm_dt_bias, m_ssm_d, m_ssm_norm_w, m_norm2_w, m_mem_norm_w, m_norm3_w, m_final_norm_w]
    v_s = [v_norm1_w, v_gdn_conv_w, v_gdn_a_log, v_gdn_dt_bias, v_gdn_norm_w, v_ssm_conv_w, v_ssm_conv_b, v_ssm_a_log,
           v_ssm_dt_bias, v_ssm_d, v_ssm_norm_w, v_norm2_w, v_mem_norm_w, v_norm3_w, v_final_norm_w]
    shp_s = [w.shape for w in w_s]
    as2d = lambda a: a if a.ndim == 2 else a.reshape(1, -1)
    d_l, m_l, v_l = adam_small([as2d(a) for a in w_s], [as2d(a) for a in g_s], [as2d(a) for a in m_s],
                               [as2d(a) for a in v_s])

    grads, deltas, new_m, new_v = {}, {}, {}, {}
    for n, (gg, dd, mm_, vv_) in big.items():
        grads[n], deltas[n], new_m[n], new_v[n] = gg, dd, mm_, vv_
    for k, n in enumerate(names_s):
        grads[n] = g_s[k].reshape(shp_s[k])
        deltas[n], new_m[n], new_v[n] = (a[k].reshape(shp_s[k]) for a in (d_l, m_l, v_l))
    order = ["norm1_w", "w_in", "gdn_conv_w", "gdn_a_log", "gdn_dt_bias", "gdn_norm_w", "ssm_conv_w", "ssm_conv_b",
             "ssm_a_log", "ssm_dt_bias", "ssm_d", "ssm_norm_w", "w_out", "norm2_w", "mem_norm_w", "wq_mem", "wk_mem",
             "wv_mem", "wo_mem", "norm3_w", "w_up", "w_down", "final_norm_w"]
    return (loss, grad_x[None], *[grads[n] for n in order], *[deltas[n] for n in order],
            *[new_m[n] for n in order], *[new_v[n] for n in order])
```

```python
import numpy as np
import jax
import jax.numpy as jnp
from jax import lax
from jax.experimental import pallas as pl
from jax.experimental.pallas import tpu as pltpu

F32, BF16 = jnp.float32, jnp.bfloat16
MESH = pl.DeviceIdType.MESH
ANY = pl.BlockSpec(memory_space=pl.ANY)

EPS = 1e-6
D = 1024
GDN_H, GDN_DK, GDN_C = 8, 128, 64
SSM_H, SSM_P, SSM_N, SSM_L = 16, 64, 128, 128
MEM_H, MEM_DH = 4, 256
D_FF = 4096
IN_COLS = 6688
CB_QKV, CB_Z, CB_ZS, CB_XS, CB_BC, CB_BA, CB_DT = 0, 3, 4, 5, 12, 52, 53
VMEM_LIMIT = 56 * 1024 * 1024
D2D_CHUNKS = 8
ICI_CHUNKS = 4

ADAM_LR, ADAM_B1, ADAM_B2, ADAM_EPS, ADAM_WD, ADAM_STEP = 0.001, 0.9, 0.999, 1e-08, 0.01, 10


def _dg(a, b, ca, cb):
    return lax.dot_general(a, b, (((ca,), (cb,)), ((), ())), preferred_element_type=F32)


def _bf(x):
    return x.astype(BF16)


def mm(a, b):
    return _dg(_bf(a), _bf(b), 1, 0)


def mm_nt(a, b):
    return _dg(_bf(a), _bf(b), 1, 1)


def mm_tn(a, b):
    return _dg(_bf(a), _bf(b), 0, 0)


def mm_sel(a, sel):
    hi = a.astype(BF16)
    r1 = a - hi.astype(F32)
    mid = r1.astype(BF16)
    lo = (r1 - mid.astype(F32)).astype(BF16)
    s = sel.astype(BF16)
    return _dg(hi, s, 1, 0) + (_dg(mid, s, 1, 0) + _dg(lo, s, 1, 0))


def mm3(a, b):
    ah, bh = a.astype(BF16), b.astype(BF16)
    al, bl = (a - ah.astype(F32)).astype(BF16), (b - bh.astype(F32)).astype(BF16)
    return _dg(ah, bh, 1, 0) + (_dg(ah, bl, 1, 0) + _dg(al, bh, 1, 0))


def _iota(shape, dim):
    return lax.broadcasted_iota(jnp.int32, shape, dim)


def _chunk_cumsum(x, c):
    pos = _iota(x.shape, 0) & (c - 1)
    s = 1
    while s < c:
        x = x + jnp.where(pos >= s, pltpu.roll(x, s, 0), 0.0)
        s *= 2
    return x


def _chunk_revcumsum(x, c):
    n = x.shape[0]
    pos = _iota(x.shape, 0) & (c - 1)
    s = 1
    while s < c:
        x = x + jnp.where(pos < c - s, pltpu.roll(x, n - s, 0), 0.0)
        s *= 2
    return x


def _sig(x):
    return jax.nn.sigmoid(x)


def _softplus(x):
    return jnp.maximum(x, 0.0) + jnp.log(1.0 + jnp.exp(-jnp.abs(x)))


def _rows(v):
    return jnp.sum(v, axis=0, keepdims=True)


def _lanes(v):
    return jnp.sum(v, axis=1, keepdims=True)


def _sum_all(v):
    return _rows(_lanes(v))


def _cparams(sem):
    return pltpu.CompilerParams(dimension_semantics=sem, vmem_limit_bytes=VMEM_LIMIT)


def rowwise(fn, name, T, tb, row_ins, full_ins, row_outs, acc_outs=(), sp=None):
    nblk = T // tb
    assert nblk * tb == T
    has_sp = sp is not None

    def imap(f):
        return (lambda i, s: f(i, s)) if has_sp else (lambda i: f(i, None))

    in_specs, args = [], []
    for arr, w, cb, halo, off in row_ins:
        if halo == "col":
            in_specs.append(pl.BlockSpec((w, tb), imap(lambda i, s: (0, i))))
            args.append(arr)
            continue
        rowf = off if callable(off) else (lambda i, s, off=off: i + off)
        in_specs.append(pl.BlockSpec((tb, w), imap(lambda i, s, cb=cb, rowf=rowf: (rowf(i, s), cb))))
        args.append(arr)
        if halo == "prev":
            r = tb // 8
            in_specs.append(pl.BlockSpec((8, w), imap(lambda i, s, cb=cb, r=r: (jnp.maximum(i * r - 1, 0), cb))))
            args.append(arr)
        elif halo == "next":
            r, last = tb // 8, T // 8 - 1
            in_specs.append(pl.BlockSpec((8, w), imap(lambda i, s, cb=cb, r=r, last=last:
                                                      (jnp.minimum((i + 1) * r, last), cb))))
            args.append(arr)
    for arr in full_ins:
        in_specs.append(pl.BlockSpec(arr.shape, imap(lambda i, s, nd=arr.ndim: (0,) * nd)))
        args.append(arr)
    n_in, n_ro = len(args), len(row_outs)
    out_shape, out_specs, aliases = [], [], {}
    for k, (w, dt, *dest) in enumerate(row_outs):
        if dest:
            buf, cb = dest
            out_shape.append(jax.ShapeDtypeStruct(buf.shape, buf.dtype))
            out_specs.append(pl.BlockSpec((tb, w), imap(lambda i, s, cb=cb: (i, cb))))
            if not isinstance(buf, jax.ShapeDtypeStruct):
                aliases[len(args) + int(has_sp)] = k
                in_specs.append(ANY)
                args.append(buf)
        elif w < 0:
            out_shape.append(jax.ShapeDtypeStruct((-w, T), dt))
            out_specs.append(pl.BlockSpec((-w, tb), imap(lambda i, s: (0, i))))
        else:
            out_shape.append(jax.ShapeDtypeStruct((T, w), dt))
            out_specs.append(pl.BlockSpec((tb, w), imap(lambda i, s: (i, 0))))
    for shp in acc_outs:
        out_shape.append(jax.ShapeDtypeStruct(shp, F32))
        out_specs.append(pl.BlockSpec(shp, imap(lambda i, s, nd=len(shp): (0,) * nd)))

    def body(*refs):
        i = pl.program_id(0)
        if has_sp:
            sp_ref, refs = refs[0], refs[1:]
            vals = fn(i, nblk, sp_ref, *[r[...] for r in refs[:n_in]])
        else:
            vals = fn(i, nblk, *[r[...] for r in refs[:n_in]])
        outs = refs[n_in + len(aliases):]
        for ref, val in zip(outs[:n_ro], vals[:n_ro]):
            ref[...] = val.astype(ref.dtype)
        for ref, val in zip(outs[n_ro:], vals[n_ro:]):
            @pl.when(i == 0)
            def _(ref=ref, val=val):
                ref[...] = val

            @pl.when(i > 0)
            def _(ref=ref, val=val):
                ref[...] += val

    cparams = _cparams(("arbitrary",) if acc_outs else ("parallel",))
    if has_sp:
        return pl.pallas_call(
            body, name=name, out_shape=out_shape, compiler_params=cparams, input_output_aliases=aliases,
            grid_spec=pltpu.PrefetchScalarGridSpec(num_scalar_prefetch=1, grid=(nblk,), in_specs=in_specs,
                                                   out_specs=out_specs),
        )(sp, *args)
    return pl.pallas_call(
        body, name=name, grid=(nblk,), in_specs=in_specs, out_specs=out_specs, out_shape=out_shape,
        compiler_params=cparams, input_output_aliases=aliases,
    )(*args)


def R(arr, w=None, cb=0, halo=None, off=0):
    return (arr, arr.shape[1] if w is None else w, cb, halo, off)


def RC(arr):
    return (arr, arr.shape[0], 0, "col", 0)


def matmul(name, a, b, form, tm, tn, tk, out_dtypes, epi=None, extras=(), rows=(), into=None, n_acc=0, b_sel=None):
    bs = b.shape if b_sel is None else b_sel[0]
    if form == "nn":
        (M, K), N = a.shape, bs[1]
    elif form == "nt":
        (M, K), N = a.shape, bs[0]
    else:
        (K, M), N = a.shape, bs[1]
    tm, tn, tk = min(tm, M), min(tn, N), min(tk, K)
    assert M % tm == 0 and N % tn == 0 and K % tk == 0, (name, M, N, K, tm, tn, tk)

    def b_spec_of(blk, at):
        if b_sel is None:
            return pl.BlockSpec(blk, lambda i, j, k: at(i, j, k))
        blk3 = b_sel[1]
        assert int(np.prod([d for d in blk3 if d is not None])) == blk[0] * blk[1], (name, blk3, blk)
        return pl.BlockSpec(blk3, lambda i, j, k: b_sel[2](i, j, k))

    if form == "nn":
        a_spec = pl.BlockSpec((tm, tk), lambda i, j, k: (i, k))
        b_spec = b_spec_of((tk, tn), lambda i, j, k: (k, j))
        ca, cb = 1, 0
    elif form == "nt":
        a_spec = pl.BlockSpec((tm, tk), lambda i, j, k: (i, k))
        b_spec = b_spec_of((tn, tk), lambda i, j, k: (j, k))
        ca, cb = 1, 1
    else:
        a_spec = pl.BlockSpec((tk, tm), lambda i, j, k: (k, i))
        b_spec = b_spec_of((tk, tn), lambda i, j, k: (k, j))
        ca, cb = 0, 0
    nk, ne, no = K // tk, len(extras) + len(rows), len(out_dtypes)
    if epi is None:
        epi = lambda acc: (acc,)

    assert n_acc == 0 or tn == N

    def body(a_ref, b_ref, *rest):
        ex, outs, accs, acc = rest[:ne], rest[ne:ne + no], rest[ne + no:ne + no + n_acc], rest[ne + no + n_acc]
        i, k = pl.program_id(0), pl.program_id(2)

        def finish(total):
            vals = epi(total, *[e[...] for e in ex])
            for r, v in zip(outs, vals[:no]):
                r[...] = v.astype(r.dtype).reshape(r.shape)
            for r, v in zip(accs, vals[no:]):
                @pl.when(i == 0)
                def _(r=r, v=v):
                    r[...] = v

                @pl.when(i > 0)
                def _(r=r, v=v):
                    r[...] += v

        b_tile = b_ref[...]
        if b_sel is not None and len(b_sel) > 3:
            b_tile = jnp.concatenate([b_tile[s] for s in range(b_tile.shape[0])], axis=1)
        prod = _dg(_bf(a_ref[...]), _bf(b_tile.reshape(-1, b_tile.shape[-1])), ca, cb)
        if nk == 1:
            finish(prod)
            return

        @pl.when(k == 0)
        def _():
            acc[...] = prod

        @pl.when(k > 0)
        def _():
            acc[...] += prod

        @pl.when(k == nk - 1)
        def _():
            finish(acc[...])

    mn = pl.BlockSpec((tm, tn), lambda i, j, k: (i, j))
    rw = pl.BlockSpec((1, tn), lambda i, j, k: (0, j))
    acc_scratch = pltpu.VMEM((tm, tn) if nk > 1 else (8, 128), F32)
    if into is not None:
        buf, blk, bmap = into
        assert ne == 0 and no == 1
        aliased = not isinstance(buf, jax.ShapeDtypeStruct)

        def body_into(a_ref, b_ref, *rest):
            body(a_ref, b_ref, *rest[-2:])

        return pl.pallas_call(
            body_into, name=name, grid=(M // tm, N // tn, nk),
            in_specs=[a_spec, b_spec] + ([ANY] if aliased else []), out_specs=pl.BlockSpec(blk, bmap),
            out_shape=jax.ShapeDtypeStruct(buf.shape, buf.dtype),
            scratch_shapes=[acc_scratch],
            input_output_aliases={2: 0} if aliased else {},
            compiler_params=_cparams(("parallel", "parallel", "arbitrary")),
        )(a, b, *([buf] if aliased else []))
    return pl.pallas_call(
        body, name=name, grid=(M // tm, N // tn, nk),
        in_specs=[a_spec, b_spec] + [mn] * len(extras) + [rw] * len(rows), out_specs=[mn] * no + [rw] * n_acc,
        out_shape=[jax.ShapeDtypeStruct((M, N), dt) for dt in out_dtypes] + [jax.ShapeDtypeStruct((1, N), F32)] * n_acc,
        scratch_shapes=[acc_scratch],
        compiler_params=_cparams(("arbitrary",) * 3 if n_acc else ("parallel", "parallel", "arbitrary")),
    )(a, b, *extras, *rows)


def _epi_res(acc, res):
    return (res + acc,)


def _epi_rms_bwd(acc, x, dres, w):
    return rms_bwd_fn(0, 0, x, acc, dres, w)


def rms_bwd1_fn(i, n, x, dh, dres, w):
    dx, _, gw = rms_bwd_fn(i, n, x, dh, dres, w)
    return dx, gw


def _epi_final(acc, res, tgt, w):
    return final_fn(0, 0, res + acc, tgt, w)


def _epi_res_rms(acc, res, w):
    x = res + acc
    return (x, x * lax.rsqrt(jnp.mean(x * x, axis=-1, keepdims=True) + EPS) * w)


def _epi_relu2(acc):
    u = jnp.maximum(acc, 0.0)
    return (u, u * u)


def _epi_dup(acc, u):
    return (acc * 2.0 * u.astype(F32),)


def _conv(x, halo, w, i):
    halo = jnp.where(i == 0, 0.0, halo)
    xt = jnp.concatenate([halo, x], axis=0)
    shifted = [pltpu.roll(xt, 3 - k, 0)[8:, :] for k in range(3)] + [x]
    y = shifted[3] * w[3:4, :]
    for k in range(3):
        y = y + shifted[k] * w[k:k + 1, :]
    return y, shifted


def _l2n(x, scale):
    outs = []
    for h in range(x.shape[1] // 128):
        xh = x[:, 128 * h:128 * h + 128]
        outs.append(xh * (lax.rsqrt(jnp.sum(xh * xh, axis=-1, keepdims=True) + EPS) * scale))
    return jnp.concatenate(outs, axis=1)


def _l2n_bwd(x, dy, scale):
    outs = []
    for h in range(x.shape[1] // 128):
        xh, dh = x[:, 128 * h:128 * h + 128], dy[:, 128 * h:128 * h + 128] * scale
        r = lax.rsqrt(jnp.sum(xh * xh, axis=-1, keepdims=True) + EPS)
        outs.append(r * dh - xh * (r * r * r) * jnp.sum(xh * dh, axis=-1, keepdims=True))
    return jnp.concatenate(outs, axis=1)


def rms_fwd_fn(i, n, x, w):
    r = lax.rsqrt(jnp.mean(x * x, axis=-1, keepdims=True) + EPS)
    return (x * r * w,)


def rms_bwd_fn(i, n, x, dh, dres, w):
    r = lax.rsqrt(jnp.mean(x * x, axis=-1, keepdims=True) + EPS)
    g = dh * w
    dx = dres + r * g - x * (r * r * r) * jnp.mean(x * g, axis=-1, keepdims=True)
    return dx, dx, _rows(dh * x * r)


def rms_bwd_w_fn(i, n, x, dh, w):
    r = lax.rsqrt(jnp.mean(x * x, axis=-1, keepdims=True) + EPS)
    return (_rows(dh * x * r),)


def final_fn(i, n, x, tgt, w):
    r = lax.rsqrt(jnp.mean(x * x, axis=-1, keepdims=True) + EPS)
    xn = x * r
    e = xn * w - tgt
    dy = e * (1.0 / D)
    g = dy * w
    dx = r * g - x * (r * r * r) * jnp.mean(x * g, axis=-1, keepdims=True)
    return dx, dx, _rows(e * e), _rows(dy * xn)


def _gdn_gates(ba, alog_c, dtb_c):
    col = _iota(ba.shape, 1)
    amask = (col >= 8) & (col < 16)
    beta = jnp.where(col < 8, _sig(ba), 0.0)
    z = ba + dtb_c
    ea_ = jnp.exp(alog_c)
    return beta, z, ea_, jnp.where(amask, -ea_ * _softplus(z), 0.0), amask


def _cols(x, g):
    return x[:, 128 * g:128 * g + 128]


def gdn_prep_fn(i, n, qkv, halo, ba, cw, alog_c, dtb_c, eb, ea):
    outs = [[], [], []]
    for g in range(3 * GDN_H):
        yc, _ = _conv(_cols(qkv, g), _cols(halo, g), _cols(cw, g), i)
        act = yc * _sig(yc)
        if g < 2 * GDN_H:
            act = _l2n(act, GDN_DK ** -0.5 if g < GDN_H else 1.0)
        outs[g // GDN_H].append(act)
    beta, _, _, gg, _ = _gdn_gates(ba, alog_c, dtb_c)
    gcs = _chunk_cumsum(gg, GDN_C)
    return (*[jnp.concatenate(o, axis=1) for o in outs], mm_sel(gcs, ea), mm_sel(beta, eb), beta + gcs,
            jnp.transpose(gcs)[8:16, :])


def gdn_prep_bwd_fn(i, n, qkv, halo, ba, dqn, dkn, dv, dgb, dgcs_t, cw, alog_c, dtb_c):
    dycs, dwl = [], [[], [], [], []]
    for g in range(3 * GDN_H):
        yc, shifted = _conv(_cols(qkv, g), _cols(halo, g), _cols(cw, g), i)
        sg = _sig(yc)
        act = yc * sg
        if g < GDN_H:
            d = _l2n_bwd(act, _cols(dqn, g), GDN_DK ** -0.5)
        elif g < 2 * GDN_H:
            d = _l2n_bwd(act, _cols(dkn, g - GDN_H), 1.0)
        else:
            d = _cols(dv, g - 2 * GDN_H)
        dyc_g = d * (sg * (1.0 + yc * (1.0 - sg)))
        dycs.append(dyc_g)
        for k in range(4):
            dwl[k].append(_rows(dyc_g * shifted[k]))
    dyc = jnp.concatenate(dycs, axis=1)
    dws = [jnp.concatenate(l, axis=1) for l in dwl]
    beta, z, ea_, g, amask = _gdn_gates(ba, alog_c, dtb_c)
    tbn = ba.shape[0]
    rowpart = jnp.transpose(jnp.concatenate([jnp.zeros((8, tbn), F32), dgcs_t, jnp.zeros((112, tbn), F32)], axis=0))
    dg = _chunk_revcumsum(jnp.where(amask, dgb, 0.0) - rowpart, GDN_C)
    draw = jnp.where(amask, dg * (-ea_) * _sig(z), 0.0)
    dba = draw + dgb * beta * (1.0 - beta)
    return (dyc, dba, dws[0], dws[1], dws[2], dws[3], _rows(dg * g), _rows(draw))


def conv_bwd_fn(i, n, dyc, halo, w):
    halo = jnp.where(i == n - 1, 0.0, halo)
    tb = dyc.shape[0]
    outs = []
    for g in range(dyc.shape[1] // 128):
        d, wg = _cols(dyc, g), _cols(w, g)
        xt = jnp.concatenate([d, _cols(halo, g)], axis=0)
        dx = d * wg[3:4, :]
        for k in range(3):
            dx = dx + pltpu.roll(xt, tb + 8 - (3 - k), 0)[:tb, :] * wg[k:k + 1, :]
        outs.append(dx)
    return (jnp.concatenate(outs, axis=1),)


def gdn_post_fn(i, n, o, z, w):
    outs = []
    for h in range(GDN_H):
        oh, zh = o[:, 128 * h:128 * h + 128], z[:, 128 * h:128 * h + 128]
        r = lax.rsqrt(jnp.mean(oh * oh, axis=-1, keepdims=True) + EPS)
        outs.append(oh * r * w * (zh * _sig(zh)))
    return (jnp.concatenate(outs, axis=1),)


def gdn_post_bwd_fn(i, n, o, z, doa, w):
    dos, dzs, dw = [], [], None
    for h in range(GDN_H):
        sl = slice(128 * h, 128 * h + 128)
        oh, zh, dh = o[:, sl], z[:, sl], doa[:, sl]
        r = lax.rsqrt(jnp.mean(oh * oh, axis=-1, keepdims=True) + EPS)
        s = _sig(zh)
        dn = dh * (zh * s)
        dzs.append(dh * (oh * r * w) * (s * (1.0 + zh * (1.0 - s))))
        t = _rows(dn * oh * r)
        dw = t if dw is None else dw + t
        g = dn * w
        dos.append(r * g - oh * (r * r * r) * jnp.mean(oh * g, axis=-1, keepdims=True))
    return jnp.concatenate(dos, axis=1), jnp.concatenate(dzs, axis=1), dw


def _ssd_gates(dtblk, alog_c, dtb_c):
    hmask = _iota(dtblk.shape, 1) < SSM_H
    z = dtblk + dtb_c
    return jnp.where(hmask, _softplus(z), 0.0), -jnp.exp(alog_c), z, hmask


def _silu_conv_cols(x, halo, w, b, i):
    outs = []
    for g in range(x.shape[1] // 128):
        yc, _ = _conv(_cols(x, g), _cols(halo, g), _cols(w, g), i)
        yc = yc + _cols(b, g)
        outs.append(yc * _sig(yc))
    return jnp.concatenate(outs, axis=1)


def _silu_conv_bwd_cols(x, halo, w, b, dout, i):
    dycs, dwl = [], [[], [], [], []]
    for g in range(x.shape[1] // 128):
        yc, shifted = _conv(_cols(x, g), _cols(halo, g), _cols(w, g), i)
        yc = yc + _cols(b, g)
        s = _sig(yc)
        dyc_g = _cols(dout, g) * (s * (1.0 + yc * (1.0 - s)))
        dycs.append(dyc_g)
        for k in range(4):
            dwl[k].append(_rows(dyc_g * shifted[k]))
    dyc = jnp.concatenate(dycs, axis=1)
    return dyc, [jnp.concatenate(l, axis=1) for l in dwl], _rows(dyc)


def ssd_prep_fn(i, n, xp, hx, bcp, hbc, dtblk, cwx, cwbc, cbx, cbbc, alog_c, dtb_c, e16):
    dt, a_neg, _, _ = _ssd_gates(dtblk, alog_c, dtb_c)
    acs = _chunk_cumsum(dt * a_neg, SSM_L)
    return (_silu_conv_cols(xp, hx, cwx, cbx, i), _silu_conv_cols(bcp, hbc, cwbc, cbbc, i), mm_sel(dt, e16),
            mm_sel(acs, e16), jnp.transpose(acs)[0:SSM_H, :])


def ssd_prep_bwd_fn(i, n, xp, hx, bcp, hbc, dtblk, dxs_a, dxs_b, db, dc, dgate, dacs_t, cwx, cwbc, cbx, cbbc, alog_c, dtb_c):
    dyx, dwx, dbx = _silu_conv_bwd_cols(xp, hx, cwx, cbx, dxs_a + dxs_b, i)
    dybc, dwbc, dbbc = _silu_conv_bwd_cols(bcp, hbc, cwbc, cbbc, jnp.concatenate([db, dc], axis=1), i)
    dt, a_neg, z, hmask = _ssd_gates(dtblk, alog_c, dtb_c)
    g0, g1 = dgate[:, :128], dgate[:, 128:]
    col = _iota(g0.shape, 1)
    lo, mid = col < 8, (col >= 8) & (col < 16)
    dacs_col = jnp.where(lo, g0, 0.0) + pltpu.roll(jnp.where(lo, g1, 0.0), 8, 1)
    ddt_dir = pltpu.roll(jnp.where(mid, g0, 0.0), 120, 1) + jnp.where(mid, g1, 0.0)
    tbn = dtblk.shape[0]
    rowpart = jnp.transpose(jnp.concatenate([dacs_t, jnp.zeros((128 - SSM_H, tbn), F32)], axis=0))
    da = _chunk_revcumsum(dacs_col - rowpart, SSM_L)
    draw = jnp.where(hmask, (ddt_dir + da * a_neg) * _sig(z), 0.0)
    return (dyx, dybc, draw, *dwx, *dwbc, dbx, dbbc, _rows(da * dt * a_neg), _rows(draw))


def _ssd_gate(y, xs, zs, d_x):
    y2 = y + xs * d_x
    s = _sig(zs)
    return y2, s, y2 * (zs * s)


def ssd_post_fn(i, n, y, xs, zs, d_x, nw):
    _, _, yg = _ssd_gate(y, xs, zs, d_x)
    outs = []
    for g in range(2):
        v = yg[:, 512 * g:512 * g + 512]
        outs.append(v * lax.rsqrt(jnp.mean(v * v, axis=-1, keepdims=True) + EPS))
    return (jnp.concatenate(outs, axis=1) * nw,)


def ssd_post_bwd_fn(i, n, y, xs, zs, dob, d_x, nw):
    y2, s, yg = _ssd_gate(y, xs, zs, d_x)
    gfull = dob * nw
    dygs, dnw = [], []
    for g in range(2):
        sl = slice(512 * g, 512 * g + 512)
        v, gg = yg[:, sl], gfull[:, sl]
        r = lax.rsqrt(jnp.mean(v * v, axis=-1, keepdims=True) + EPS)
        dygs.append(r * gg - v * (r * r * r) * jnp.mean(v * gg, axis=-1, keepdims=True))
        dnw.append(_rows(dob[:, sl] * v * r))
    dyg = jnp.concatenate(dygs, axis=1)
    dy2 = dyg * (zs * s)
    dzs = dyg * y2 * (s * (1.0 + zs * (1.0 - s)))
    return dy2, dy2 * d_x, dzs, jnp.concatenate(dnw, axis=1), _rows(dy2 * xs)


def _attn_probs(q, k):
    hs = [slice(MEM_DH * h, MEM_DH * h + MEM_DH) for h in range(MEM_H)]
    ss = [mm_nt(q[:, sl], k[:, sl]) * (MEM_DH ** -0.5) for sl in hs]
    es = [jnp.exp(s - jnp.max(s, axis=-1, keepdims=True)) for s in ss]
    return hs, [e / jnp.sum(e, axis=-1, keepdims=True) for e in es]


def attn_fn(i, n, q, k, v):
    hs, ps = _attn_probs(q, k)
    return (jnp.concatenate([mm(p, v[:, sl]) for p, sl in zip(ps, hs)], axis=1),)


def attn_bwd_fn(i, n, q, do, k, v):
    hs, ps = _attn_probs(q, k)
    dvs = [mm_tn(p, do[:, sl]) for p, sl in zip(ps, hs)]
    dps = [mm_nt(do[:, sl], v[:, sl]) for sl in hs]
    dss = [p * (dp - jnp.sum(dp * p, axis=-1, keepdims=True)) * (MEM_DH ** -0.5) for p, dp in zip(ps, dps)]
    dqs = [mm(ds, k[:, sl]) for ds, sl in zip(dss, hs)]
    dks = [mm_tn(ds, q[:, sl]) for ds, sl in zip(dss, hs)]
    return jnp.concatenate(dqs, axis=1), jnp.concatenate(dks, axis=1), jnp.concatenate(dvs, axis=1)


def add2_fn(i, n, sp, a, b):
    return (a + b,)


def sum4_fn(i, n, sp, a, b, c, d):
    return (((a.astype(F32) + b.astype(F32)) + c.astype(F32)) + d.astype(F32),)


def _adamw(w, g, m, v):
    m = ADAM_B1 * m + (1.0 - ADAM_B1) * g
    v = ADAM_B2 * v + (1.0 - ADAM_B2) * (g * g)
    m_hat = m / (1.0 - ADAM_B1 ** ADAM_STEP)
    v_hat = v / (1.0 - ADAM_B2 ** ADAM_STEP)
    delta = -ADAM_LR * (m_hat / (jnp.sqrt(v_hat) + ADAM_EPS) + ADAM_WD * w)
    return delta, m, v


def _gate_cols(gb, h):
    lane = _iota(gb.shape, 1)
    return _lanes(jnp.where(lane == h, gb, 0.0)), _lanes(jnp.where(lane == 8 + h, gb, 0.0))


def _gdn_stage1(q, k, v, bb, gcs, grow):
    C = GDN_C
    row, col = _iota((C, C), 0), _iota((C, C), 1)
    incl, strict = row >= col, row > col
    dmat = jnp.where(incl, jnp.exp(jnp.minimum((gcs if gcs.shape[1] == 1 else gcs[:, :C]) - grow, 0.0)), 0.0)
    gam = jnp.exp(gcs)
    gl = gcs[C - 1:C, :]
    kb, vb = k * bb, v * bb
    kg = kb * gam
    lmat = jnp.where(strict, mm_nt(kb, k) * dmat, 0.0)
    pmat = jnp.where(incl, mm_nt(q, k) * dmat, 0.0)
    return dict(q=q, k=k, v=v, bb=bb, incl=incl, strict=strict, dmat=dmat, gam=gam, kb=kb, vb=vb, kg=kg,
                lmat=lmat, pmat=pmat, qd=q * gam, kdec=jnp.exp(gl - gcs), cd=jnp.exp(gl))


def _gdn_inverse(lmats):
    C = GDN_C
    eye = (_iota((C, C), 0) == _iota((C, C), 1)).astype(F32)
    xs = [-l for l in lmats]
    ts = [eye + x for x in xs]
    for _ in range(5):
        xs = [mm(x, x) for x in xs]
        ts = [t + mm(t, x) for t, x in zip(ts, xs)]
    res = [eye - mm3(eye + l, t) for l, t in zip(lmats, ts)]
    return [t + mm(t, r) for t, r in zip(ts, res)]


def gdn_fwd(qn, kn, v, gcs_x, beta_x, gcs_t, tb, gh):
    T = qn.shape[0]
    nb, ncb, nc, C = T // tb, tb // GDN_C, T // GDN_C, GDN_C
    idx = [(hh, c) for hh in range(gh) for c in range(ncb)]

    def body(q_ref, k_ref, v_ref, g_ref, b_ref, gt_ref, o_ref, st_ref, ti_ref, s_scr):
        @pl.when(pl.program_id(1) == 0)
        def _():
            s_scr[...] = jnp.zeros_like(s_scr)

        grows = [gt_ref[hh] for hh in range(gh)]
        at = lambda hh, c: (slice(C * c, C * (c + 1)), slice(128 * hh, 128 * hh + 128))
        st1 = []
        for hh, c in idx:
            sl, ln = at(hh, c)
            st1.append(_gdn_stage1(q_ref[sl, ln], k_ref[sl, ln], v_ref[sl, ln], b_ref[sl, ln], g_ref[sl, ln],
                                   grows[hh][:, sl]))
        tinvs = _gdn_inverse([s["lmat"] for s in st1])
        us = [mm(t, s["vb"]) for t, s in zip(tinvs, st1)]
        ws = [mm(t, s["kg"]) for t, s in zip(tinvs, st1)]
        kds = [s["k"] * s["kdec"] for s in st1]
        ms = [mm_tn(kd, w) for kd, w in zip(kds, ws)]
        bs = [mm_tn(kd, u) for kd, u in zip(kds, us)]
        gs = [s["qd"] - mm(s["pmat"], w) for s, w in zip(st1, ws)]
        pus = [mm(s["pmat"], u) for s, u in zip(st1, us)]
        ss = [s_scr[hh] for hh in range(gh)]
        for c in range(ncb):
            for hh in range(gh):
                n, (sl, ln) = hh * ncb + c, at(hh, c)
                ti_ref[hh, sl, :] = tinvs[n]
                st_ref[hh, c] = ss[hh]
                o_ref[sl, ln] = mm(gs[n], ss[hh]) + pus[n]
                ss[hh] = st1[n]["cd"] * ss[hh] - mm(ms[n], ss[hh]) + bs[n]
        for hh in range(gh):
            s_scr[hh] = ss[hh]

    blk = pl.BlockSpec((tb, 128 * gh), lambda h, i: (i, h))
    return pl.pallas_call(
        body, name="gdn_fwd", grid=(GDN_H // gh, nb),
        in_specs=[blk] * 5 + [pl.BlockSpec((gh, 1, tb), lambda h, i: (h, 0, i))],
        out_specs=[blk, pl.BlockSpec((gh, ncb, 128, 128), lambda h, i: (h, i, 0, 0)),
                   pl.BlockSpec((gh, tb, C), lambda h, i: (h, i, 0))],
        out_shape=[jax.ShapeDtypeStruct((T, D), F32), jax.ShapeDtypeStruct((GDN_H, nc, 128, 128), F32),
                   jax.ShapeDtypeStruct((GDN_H, T, C), F32)],
        scratch_shapes=[pltpu.VMEM((gh, 128, 128), F32)],
        compiler_params=_cparams(("parallel", "arbitrary")),
    )(qn, kn, v, gcs_x, beta_x, gcs_t)


def gdn_bwd(qn, kn, v, gb, gcs_t, do, states, tinv, tb, gh):
    T = qn.shape[0]
    nb, ncb, C = T // tb, tb // GDN_C, GDN_C
    assert gh == GDN_H

    def body(q_ref, k_ref, v_ref, gb_ref, gt_ref, do_ref, st_ref, ti_ref,
             dq_ref, dk_ref, dv_ref, dgb_ref, dgr_ref, ds_scr):
        @pl.when(pl.program_id(1) == 0)
        def _():
            ds_scr[...] = jnp.zeros_like(ds_scr)

        grows = [gt_ref[hh] for hh in range(gh)]
        at = lambda hh, c: (slice(C * c, C * (c + 1)), slice(128 * hh, 128 * hh + 128))
        lastrow = _iota((C, 1), 0) == C - 1
        lane = _iota((C, 128), 1)
        idx = [(hh, c) for hh in range(gh) for c in range(ncb)]
        P = []
        for hh, c in idx:
            sl, ln = at(hh, c)
            lc = _gdn_stage1(q_ref[sl, ln], k_ref[sl, ln], v_ref[sl, ln], *_gate_cols(gb_ref[sl, :], hh), grows[hh][:, sl])
            lc.update(tinv=ti_ref[hh, sl, :], s=st_ref[hh, c], do=do_ref[sl, ln], kd=lc["k"] * lc["kdec"])
            P.append(lc)
        for l, u, w in zip(P, [mm(l["tinv"], l["vb"]) for l in P], [mm(l["tinv"], l["kg"]) for l in P]):
            l.update(u=u, w=w)
        for l, x in zip(P, [mm(l["w"], l["s"]) for l in P]):
            l["vn"] = l["u"] - x
        for l, a, b, c_, d in zip(P, [mm_nt(l["do"], l["s"]) for l in P], [mm_nt(l["do"], l["vn"]) for l in P],
                                  [mm_tn(l["qd"], l["do"]) for l in P], [mm_tn(l["pmat"], l["do"]) for l in P]):
            l.update(dqd=a, dp=jnp.where(l["incl"], b, 0.0), ds_q=c_, dvn_p=d)
        pre = dict(zip(idx, P))
        rows = {}
        hs = range(gh)
        ds = [ds_scr[hh] for hh in hs]
        for c in reversed(range(ncb)):
            L = [pre[hh, c] for hh in hs]
            dvn = [l["dvn_p"] + mm(l["kd"], d) for l, d in zip(L, ds)]
            dkd = [mm_nt(l["vn"], d) for l, d in zip(L, ds)]
            dcd = [_sum_all(l["s"] * d) for l, d in zip(L, ds)]
            ds = [l["ds_q"] + l["cd"] * d - mm_tn(l["w"], x) for l, d, x in zip(L, ds, dvn)]
            dw = [-mm_nt(x, l["s"]) for l, x in zip(L, dvn)]
            dvb = [mm_tn(l["tinv"], x) for l, x in zip(L, dvn)]
            dkg = [mm_tn(l["tinv"], x) for l, x in zip(L, dw)]
            da = [-jnp.where(l["strict"], mm_nt(a, l["u"]) + mm_nt(b, l["w"]), 0.0) for l, a, b in zip(L, dvb, dkg)]
            dm = [a * l["dmat"] for l, a in zip(L, da)]
            dn = [l["dp"] * l["dmat"] for l in L]
            dkb = [mm(a, l["k"]) for l, a in zip(L, dm)]
            dq = [mm(a, l["k"]) + l["gam"] * l["dqd"] for l, a in zip(L, dn)]
            dk = [mm_tn(a, l["kb"]) + mm_tn(b, l["q"]) for l, a, b in zip(L, dm, dn)]
            dgb = jnp.zeros((C, 128), F32)
            for hh in hs:
                sl, ln = at(hh, c)
                l = L[hh]
                e = da[hh] * l["lmat"] + l["dp"] * l["pmat"]
                t_kd = _lanes(dkd[hh] * l["kd"])
                dgl = _sum_all(t_kd) + dcd[hh] * l["cd"][:, :1]
                dgcs = (_lanes(e) + _lanes(l["dqd"] * l["qd"]) - t_kd + _lanes(dkg[hh] * l["kg"])
                        + jnp.where(lastrow, dgl, 0.0))
                rows[hh, c] = _rows(e)
                dq_ref[sl, ln] = dq[hh]
                dk_ref[sl, ln] = (dk[hh] + l["kdec"] * dkd[hh] + l["bb"] * l["gam"] * dkg[hh] + l["bb"] * dkb[hh])
                dv_ref[sl, ln] = l["bb"] * dvb[hh]
                dbeta = _lanes(dkg[hh] * l["gam"] * l["k"]) + _lanes(dvb[hh] * l["v"]) + _lanes(dkb[hh] * l["k"])
                dgb = dgb + jnp.where(lane == hh, dbeta, 0.0) + jnp.where(lane == 8 + hh, dgcs, 0.0)
            dgb_ref[slice(C * c, C * (c + 1)), :] = dgb
        for hh in hs:
            ds_scr[hh] = ds[hh]
            dgr_ref[hh] = jnp.concatenate([rows[hh, c] for c in range(ncb)], axis=1)

    blk = pl.BlockSpec((tb, 128 * gh), lambda h, i: (nb - 1 - i, h))
    rowspec = pl.BlockSpec((gh, 1, tb), lambda h, i: (h, 0, nb - 1 - i))
    cblk = pl.BlockSpec((tb, 128), lambda h, i: (nb - 1 - i, 0))
    return pl.pallas_call(
        body, name="gdn_bwd", grid=(GDN_H // gh, nb),
        in_specs=[blk] * 3 + [cblk, rowspec, blk,
                              pl.BlockSpec((gh, ncb, 128, 128), lambda h, i: (h, nb - 1 - i, 0, 0)),
                              pl.BlockSpec((gh, tb, C), lambda h, i: (h, nb - 1 - i, 0))],
        out_specs=[blk] * 3 + [cblk, rowspec],
        out_shape=[jax.ShapeDtypeStruct((T, D), F32)] * 3 + [jax.ShapeDtypeStruct((T, 128), F32),
                                                             jax.ShapeDtypeStruct((GDN_H, 1, T), F32)],
        scratch_shapes=[pltpu.VMEM((gh, 128, 128), F32)],
        compiler_params=_cparams(("parallel", "arbitrary")),
    )(qn, kn, v, gb, gcs_t, do, states, tinv)


def _ssd_pair(x2, dt2, acs2):
    last = acs2[SSM_L - 1:SSM_L, :]
    return jnp.exp(acs2), jnp.exp(last - acs2), x2 * dt2


def _ssd_head(hh, acs2, arow, dec2, cbm, bm, incl, col):
    lmask = (col >= 64 * hh) & (col < 64 * hh + 64)
    sg = jnp.where(incl, jnp.exp(jnp.minimum(acs2[:, 64 * hh:64 * hh + 1] - arow, 0.0)), 0.0)
    dec_col = dec2[:, 64 * hh:64 * hh + 1]
    return lmask, sg, sg * cbm, dec_col, bm * dec_col


def ssd_fwd(xs, bc, dt_x, acs_x, acs_t):
    T = xs.shape[0]
    nc, L = T // SSM_L, SSM_L

    def body(x_ref, bc_ref, dt_ref, ac_ref, at_ref, y_ref, hst_ref, h_scr):
        @pl.when(pl.program_id(0) == 0)
        def _():
            h_scr[...] = jnp.zeros_like(h_scr)

        row, col = _iota((L, L), 0), _iota((L, L), 1)
        incl = row >= col
        P, H = [], []
        for gp in range(8):
            g = gp // 4
            bm, cm = bc_ref[:, 128 * g:128 * g + 128], bc_ref[:, 256 + 128 * g:384 + 128 * g]
            cbm = mm_nt(cm, bm) if gp % 4 == 0 else cbm
            sl = slice(128 * gp, 128 * gp + 128)
            acs2 = ac_ref[:, sl]
            lam2, dec2, xd2 = _ssd_pair(x_ref[:, sl], dt_ref[:, sl], acs2)
            P.append(dict(sl=sl, lam2=lam2, xd2=xd2, hprev=h_scr[gp], cm=cm))
            for hh in range(2):
                lmask, _, mmat, _, bd = _ssd_head(hh, acs2, at_ref[2 * gp + hh], dec2, cbm, bm, incl, col)
                H.append(dict(mmat=mmat, bd=bd, xdh=jnp.where(lmask, xd2, 0.0), xd2=xd2))
        ys = [mm(h["mmat"], h["xdh"]) for h in H]
        sts = [mm_tn(h["xd2"], h["bd"]) for h in H]
        zs = [mm_nt(p["cm"], p["hprev"]) for p in P]
        for gp, p in enumerate(P):
            hst_ref[gp // 4, gp % 4] = p["hprev"]
            y_ref[:, p["sl"]] = ys[2 * gp] + ys[2 * gp + 1] + p["lam2"] * zs[gp]
            lam_rows = jnp.where(row < 64, p["lam2"][L - 1:L, 0:1], p["lam2"][L - 1:L, 64:65])
            h_scr[gp] = lam_rows * p["hprev"] + jnp.where(row < 64, sts[2 * gp], sts[2 * gp + 1])

    blk = pl.BlockSpec((L, D), lambda c: (c, 0))
    return pl.pallas_call(
        body, name="ssd_fwd", grid=(nc,),
        in_specs=[blk, pl.BlockSpec((L, 512), lambda c: (c, 0)), blk, blk, pl.BlockSpec((SSM_H, 1, L), lambda c: (0, 0, c))],
        out_specs=[blk, pl.BlockSpec((2, None, 4, 128, 128), lambda c: (0, c, 0, 0, 0))],
        out_shape=[jax.ShapeDtypeStruct((T, D), F32), jax.ShapeDtypeStruct((2, nc, 4, 128, 128), F32)],
        scratch_shapes=[pltpu.VMEM((8, 128, 128), F32)],
        compiler_params=_cparams(("arbitrary",)),
    )(xs, bc, dt_x, acs_x, acs_t)


def ssd_bwd(xs, bc, dt_x, acs_x, acs_t, dy, hstates):
    T = xs.shape[0]
    nc, L = T // SSM_L, SSM_L

    def body(x_ref, bc_ref, dt_ref, ac_ref, at_ref, dy_ref, hst_ref,
             dx_ref, db_ref, dc_ref, dgate_ref, dar_ref, dh_scr):
        @pl.when(pl.program_id(0) == 0)
        def _():
            dh_scr[...] = jnp.zeros_like(dh_scr)

        row, col = _iota((L, L), 0), _iota((L, L), 1)
        rowc = _iota((L, 1), 0)
        incl = row >= col
        G = [dict(bm=bc_ref[:, 128 * g:128 * g + 128], cm=bc_ref[:, 256 + 128 * g:384 + 128 * g]) for g in range(2)]
        for gr in G:
            gr["cbm"] = mm_nt(gr["cm"], gr["bm"])
        P = []
        for gp in range(8):
            sl = slice(128 * gp, 128 * gp + 128)
            x2, dt2, dy2, acs2 = x_ref[:, sl], dt_ref[:, sl], dy_ref[:, sl], ac_ref[:, sl]
            lam2, dec2, xd2 = _ssd_pair(x2, dt2, acs2)
            P.append(dict(sl=sl, gr=G[gp // 4], x2=x2, dt2=dt2, dy2=dy2, acs2=acs2, lam2=lam2, dec2=dec2, xd2=xd2,
                          hprev=hst_ref[gp // 4, gp % 4], dhn=dh_scr[gp], dz=lam2 * dy2))
        zs = [mm_nt(p["gr"]["cm"], p["hprev"]) for p in P]
        dcm_t = [mm(p["dz"], p["hprev"]) for p in P]
        dh_z = [mm_tn(p["dz"], p["gr"]["cm"]) for p in P]
        H = []
        for gp, p in enumerate(P):
            p["yoff"] = p["dz"] * zs[gp]
            p["q_rows"] = _lanes(p["dhn"] * p["hprev"])
            for hh in range(2):
                lmask, sg, mmat, dec_col, bd = _ssd_head(hh, p["acs2"], at_ref[2 * gp + hh], p["dec2"], p["gr"]["cbm"],
                                                         p["gr"]["bm"], incl, col)
                H.append(dict(p=p, hh=hh, j=2 * gp + hh, lmask=lmask, sg=sg, mmat=mmat, dec_col=dec_col, bd=bd))
        dms = [mm_nt(jnp.where(h["lmask"], h["p"]["dy2"], 0.0), h["p"]["xd2"]) for h in H]
        a1s = [mm_tn(h["mmat"], h["p"]["dy2"]) for h in H]
        a2s = [mm_nt(h["bd"], h["p"]["dhn"]) for h in H]
        dbds = [mm(jnp.where(h["lmask"], h["p"]["xd2"], 0.0), h["p"]["dhn"]) for h in H]
        for gr in G:
            gr.update(dcb=jnp.zeros((L, L), F32), dbm=jnp.zeros((L, SSM_N), F32), comp=jnp.zeros((L, 128), F32))
        dxd = [jnp.zeros((L, 128), F32) for _ in P]
        for h, dm_raw, a1, a2, dbd in zip(H, dms, a1s, a2s, dbds):
            p, hh, j = h["p"], h["hh"], h["j"]
            gr, jg = p["gr"], j % 8
            dm = jnp.where(incl, dm_raw, 0.0)
            gr["dcb"] = gr["dcb"] + dm * h["sg"]
            e = dm * h["mmat"]
            dxd_h = jnp.where(h["lmask"], a1 + a2, 0.0)
            dxd[j // 2] = dxd[j // 2] + dxd_h
            gr["dbm"] = gr["dbm"] + h["dec_col"] * dbd
            t = _lanes(dbd * h["bd"])
            lam_h = p["lam2"][L - 1:L, 64 * hh:64 * hh + 1]
            in_head = (rowc >= 64 * hh) & (rowc < 64 * hh + 64)
            add_last = _sum_all(t) + _sum_all(jnp.where(in_head, p["q_rows"], 0.0)) * lam_h
            dacs_col = (_lanes(jnp.where(h["lmask"], p["yoff"], 0.0)) + _lanes(e) - t
                        + jnp.where(rowc == L - 1, add_last, 0.0))
            ddt_col = _lanes(dxd_h * p["x2"])
            dar_ref[j] = _rows(e)
            gr["comp"] = gr["comp"] + jnp.where(col == jg, dacs_col, 0.0) + jnp.where(col == 8 + jg, ddt_col, 0.0)
        for gp, p in enumerate(P):
            lam_rows = jnp.where(row < 64, p["lam2"][L - 1:L, 0:1], p["lam2"][L - 1:L, 64:65])
            dh_scr[gp] = dh_z[gp] + lam_rows * p["dhn"]
            dx_ref[:, p["sl"]] = p["dt2"] * dxd[gp]
        for g, gr in enumerate(G):
            lanes = slice(128 * g, 128 * g + 128)
            dcm = (dcm_t[4 * g] + dcm_t[4 * g + 1]) + (dcm_t[4 * g + 2] + dcm_t[4 * g + 3])
            db_ref[:, lanes] = gr["dbm"] + mm_tn(gr["dcb"], gr["cm"])
            dc_ref[:, lanes] = dcm + mm(gr["dcb"], gr["bm"])
            dgate_ref[:, lanes] = gr["comp"]

    rv = lambda c: (nc - 1 - c, 0)
    blk, blk256 = pl.BlockSpec((L, D), rv), pl.BlockSpec((L, 256), rv)
    rowspec = pl.BlockSpec((SSM_H, 1, L), lambda c: (0, 0, nc - 1 - c))
    return pl.pallas_call(
        body, name="ssd_bwd", grid=(nc,),
        in_specs=[blk, pl.BlockSpec((L, 512), rv), blk, blk, rowspec, blk,
                  pl.BlockSpec((2, None, 4, 128, 128), lambda c: (0, nc - 1 - c, 0, 0, 0))],
        out_specs=[blk, blk256, blk256, blk256, rowspec],
        out_shape=[jax.ShapeDtypeStruct((T, D), F32), jax.ShapeDtypeStruct((T, 256), F32),
                   jax.ShapeDtypeStruct((T, 256), F32), jax.ShapeDtypeStruct((T, 256), F32),
                   jax.ShapeDtypeStruct((SSM_H, 1, T), F32)],
        scratch_shapes=[pltpu.VMEM((8, 128, 128), F32)],
        compiler_params=_cparams(("arbitrary",)),
    )(xs, bc, dt_x, acs_x, acs_t, dy, hstates)


def _pos():
    return lax.axis_index("x"), lax.axis_index("y"), lax.axis_index("c")


def _other_chips(x, y):
    return [(1 - x, y), (x, 1 - y), (1 - x, 1 - y)]


def _rcopy(src, dst, ssem, rsem, dev):
    return pltpu.make_async_remote_copy(src_ref=src, dst_ref=dst, send_sem=ssem, recv_sem=rsem,
                                        device_id=dev, device_id_type=MESH)


def _rows_at(start, n):
    return pl.ds(pl.multiple_of(start, 8), n)


def _comm_call(body, name, out_shape, n_in, scratch):
    return pl.pallas_call(
        body, name=name, out_shape=out_shape, in_specs=[ANY] * n_in,
        out_specs=[ANY] * len(out_shape) if isinstance(out_shape, (list, tuple)) else ANY,
        scratch_shapes=scratch,
        compiler_params=pltpu.CompilerParams(has_side_effects=True),
    )


def _dma_sems(n):
    return pltpu.SemaphoreType.DMA((n,))


def ag_chips(name, shard):
    rr, cc = shard.shape
    h, nq = rr // 2, ICI_CHUNKS
    hq = h // nq

    def body(x_ref, out_ref, ssem, rsem):
        x, y, c = _pos()
        me_s = 2 * x + y
        chips = _other_chips(x, y)
        started = []
        for q in range(nq):
            rows = _rows_at(c * h + q * hq, hq)
            for j, (cx, cy) in enumerate(chips):
                cp = _rcopy(x_ref.at[rows], out_ref.at[me_s, rows], ssem.at[j * nq + q], rsem.at[j * nq + q], (cx, cy, c))
                cp.start()
                started.append(cp)
        for q in range(nq):
            rows = _rows_at(c * h + q * hq, hq)
            for j, (cx, cy) in enumerate(chips):
                blk = out_ref.at[2 * cx + cy, rows]
                _rcopy(blk, blk, ssem.at[j * nq + q], rsem.at[j * nq + q], (cx, cy, c)).wait_recv()
                k = 3 * nq + j * nq + q
                cp = _rcopy(blk, blk, ssem.at[k], rsem.at[k], (x, y, 1 - c))
                cp.start()
                started.append(cp)
        for q in range(nq):
            rows = _rows_at((1 - c) * h + q * hq, hq)
            for j, (cx, cy) in enumerate(chips):
                blk = out_ref.at[2 * cx + cy, rows]
                k = 3 * nq + j * nq + q
                _rcopy(blk, blk, ssem.at[k], rsem.at[k], (x, y, 1 - c)).wait_recv()
        for cp in started:
            cp.wait_send()

    return _comm_call(body, name, jax.ShapeDtypeStruct((4, rr, cc), shard.dtype), 1,
                      [_dma_sems(6 * nq), _dma_sems(6 * nq)])(shard)


def _with_own(shard, got, s_me):
    return lax.dynamic_update_index_in_dim(got, shard, s_me, 0)


def all_gather_chips(name, shard, s_me):
    return _with_own(shard, ag_chips(name, shard), s_me)


HBM_SPEC = pl.BlockSpec(memory_space=pltpu.HBM)
SEM_SPEC = pl.BlockSpec(memory_space=pltpu.SEMAPHORE)
SPLIT_EFFECT = pltpu.SideEffectType.DATAFLOW_SIDE_EFFECTING


def _split_copies(pieces, x_ref, land_ref, sems, arriving):
    x, y, c = _pos()
    return [_rcopy(s, d_in if arriving else d_out, sems[j], sems[3 + j], dev)
            for j, (s, d_out, d_in, dev) in enumerate(pieces(x_ref, land_ref, x, y, c))]


def split_copy_start(name, src, land_shape, pieces, after):
    def body(x_ref, land_ref, after_ref, *outs):
        for cp in _split_copies(pieces, x_ref, land_ref, outs[:6], False):
            cp.start()
        outs[8][...] = jnp.zeros_like(outs[8])

    dma = pltpu.SemaphoreType.DMA(())
    res = pl.pallas_call(
        body, name=name,
        out_shape=(dma,) * 6 + (pltpu.HBM(src.shape, src.dtype), pltpu.HBM(land_shape, src.dtype),
                                jax.ShapeDtypeStruct((8, 128), F32)),
        in_specs=(HBM_SPEC, HBM_SPEC, ANY),
        out_specs=(SEM_SPEC,) * 6 + (HBM_SPEC, HBM_SPEC, pl.BlockSpec(memory_space=pltpu.VMEM)),
        input_output_aliases={0: 6, 1: 7},
        compiler_params=pltpu.CompilerParams(has_side_effects=SPLIT_EFFECT),
    )(pltpu.with_memory_space_constraint(src, pltpu.HBM),
      pltpu.with_memory_space_constraint(lax.empty(land_shape, src.dtype), pltpu.HBM), after)
    return res[:6], res[6], res[7], res[8]


def split_copy_wait(name, sems, src_thru, land_thru, after, pieces):
    def body(x_ref, land_ref, *rest):
        for cp in _split_copies(pieces, x_ref, land_ref, rest[:6], False):
            cp.wait_send()
        for cp in _split_copies(pieces, x_ref, land_ref, rest[:6], True):
            cp.wait_recv()

    return pl.pallas_call(
        body, name=name,
        out_shape=(pltpu.HBM(src_thru.shape, src_thru.dtype), pltpu.HBM(land_thru.shape, land_thru.dtype)),
        in_specs=(HBM_SPEC, HBM_SPEC) + (SEM_SPEC,) * 6 + (ANY,), out_specs=(HBM_SPEC, HBM_SPEC),
        input_output_aliases={0: 0, 1: 1},
        compiler_params=pltpu.CompilerParams(has_side_effects=SPLIT_EFFECT),
    )(src_thru, land_thru, *sems, after)


def ag_pieces(h):
    def pieces(x_ref, land_ref, x, y, c):
        rows = _rows_at(c * h, h)
        return [(x_ref.at[rows], land_ref.at[2 * x + y, rows], land_ref.at[2 * cx + cy, rows], (cx, cy, c))
                for cx, cy in _other_chips(x, y)]
    return pieces


def rs_pieces(x_ref, land_ref, x, y, c):
    return [(x_ref.at[2 * cx + cy], land_ref.at[j], land_ref.at[j], (cx, cy, c))
            for j, (cx, cy) in enumerate(_other_chips(x, y))]


def ag_forward(name, got):
    _, rr, cc = got.shape
    h, nq = rr // 2, D2D_CHUNKS
    hq = h // nq

    def body(g_ref, out_ref, ssem, rsem):
        x, y, c = _pos()
        slots = [2 * cx + cy for cx, cy in _other_chips(x, y)]
        cps = []
        for j, s in enumerate(slots):
            for q in range(nq):
                blk = out_ref.at[s, _rows_at(c * h + q * hq, hq)]
                cp = _rcopy(blk, blk, ssem.at[j * nq + q], rsem.at[j * nq + q], (x, y, 1 - c))
                cp.start()
                cps.append(cp)
        for cp in cps:
            cp.wait_send()
        for j, s in enumerate(slots):
            for q in range(nq):
                blk = out_ref.at[s, _rows_at((1 - c) * h + q * hq, hq)]
                _rcopy(blk, blk, ssem.at[j * nq + q], rsem.at[j * nq + q], (x, y, 1 - c)).wait_recv()

    return pl.pallas_call(
        body, name=name, out_shape=jax.ShapeDtypeStruct(got.shape, got.dtype), in_specs=[ANY], out_specs=ANY,
        scratch_shapes=[_dma_sems(3 * nq), _dma_sems(3 * nq)], input_output_aliases={0: 0},
        compiler_params=pltpu.CompilerParams(has_side_effects=True),
    )(got)


def rs_pair(name, g):
    _, rr, cc = g.shape
    h, nq = rr // 2, D2D_CHUNKS
    hq = h // nq

    def body(g_ref, recv_ref, ssem, rsem):
        x, y, c = _pos()
        cps = []
        for q in range(nq):
            cp = _rcopy(g_ref.at[:, _rows_at((1 - c) * h + q * hq, hq), :], recv_ref.at[:, pl.ds(q * hq, hq), :],
                        ssem.at[q], rsem.at[q], (x, y, 1 - c))
            cp.start()
            cps.append(cp)
        for cp in cps:
            cp.wait()

    return _comm_call(body, name, jax.ShapeDtypeStruct((4, h, cc), g.dtype), 1, [_dma_sems(nq), _dma_sems(nq)])(g)


def rs_chips(name, p):
    _, h, cc = p.shape
    nq = ICI_CHUNKS
    hq = h // nq

    def body(p_ref, buf_ref, ssem, rsem):
        x, y, c = _pos()
        sends = []
        for q in range(nq):
            rows = pl.ds(q * hq, hq)
            for j, (cx, cy) in enumerate(_other_chips(x, y)):
                cp = _rcopy(p_ref.at[2 * cx + cy, rows], buf_ref.at[j, rows], ssem.at[j * nq + q],
                            rsem.at[j * nq + q], (cx, cy, c))
                cp.start()
                sends.append(cp)
        for cp in sends:
            cp.wait()

    return _comm_call(body, name, jax.ShapeDtypeStruct((3, h, cc), p.dtype), 1,
                      [_dma_sems(3 * nq), _dma_sems(3 * nq)])(p)


def rs_join(name, half):
    h, cc = half.shape
    nq = D2D_CHUNKS
    hq = h // nq

    def body(h_ref, out_ref, ssem, rsem):
        x, y, c = _pos()
        cps = []
        for q in range(nq):
            rows = pl.ds(q * hq, hq)
            cp = _rcopy(h_ref.at[rows], out_ref.at[rows], ssem.at[q], rsem.at[q], (x, y, 1 - c))
            cp.start()
            cps.append(cp)
        for cp in cps:
            cp.wait()

    return _comm_call(body, name, jax.ShapeDtypeStruct((h, cc), half.dtype), 1, [_dma_sems(nq), _dma_sems(nq)])(half)


def reduce_scatter(tag, g, tb, sp):
    return rs_end(rs_begin(tag, g, tb, sp, False), None)


def rs_begin(tag, g, tb, sp, split, after=None):
    _, rr, cc = g.shape
    h = rr // 2
    nbh = h // tb
    recv = rs_pair(tag + "_pair", g)
    mine_rows = lambda i, s: (i // nbh) * (2 * nbh) + s[0] * nbh + i % nbh
    part = rowwise(add2_fn, tag + "_add", 4 * h, tb, [R(g.reshape(4 * rr, cc), off=mine_rows), R(recv.reshape(4 * h, cc))],
                   [], [(cc, BF16)], sp=sp)[0].reshape(4, h, cc)
    st = dict(tag=tag, tb=tb, sp=sp, split=split, part=part)
    if split:
        st["sems"], st["part"], st["land"], st["token"] = split_copy_start(tag + "_start", part, (3, h, cc), rs_pieces,
                                                                           sp if after is None else after)
    return st


def rs_end(st, after):
    tag, tb, sp, part = st["tag"], st["tb"], st["sp"], st["part"]
    _, h, cc = part.shape
    nbh = h // tb
    if st["split"]:
        part, buf = split_copy_wait(tag + "_wait", st["sems"], part, st["land"], after, rs_pieces)
    else:
        buf = rs_chips(tag + "_chips", part)
    red = rowwise(sum4_fn, tag + "_sum", h, tb,
                  [R(part.reshape(4 * h, cc), off=lambda i, s: s[1] * nbh + i)]
                  + [R(buf.reshape(3 * h, cc), off=k * nbh) for k in range(3)],
                  [], [(cc, F32)], sp=sp)[0]
    return red, rs_join(tag + "_join", red)


def adam_halves(name, w, m, v, red, other, tb, blk0, sp):
    nbh = red.shape[0] // tb

    def fn(i, n, s, w_, m_, v_, r_, o_):
        g = jnp.where((blk0 + i) // nbh == s[0], r_, o_)
        return (g,) + _adamw(w_, g, m_, v_)

    half_rows = lambda i, s: (blk0 + i) % nbh
    return rowwise(fn, name, w.shape[0], tb, [R(w), R(m), R(v), R(red, off=half_rows), R(other, off=half_rows)],
                   [], [(w.shape[1], F32)] * 4, sp=sp)


SMALL_LANES = D


def all_reduce_items(name, items, after=None):
    flat = [a for it in items for a in it]
    shapes = [(sum(a.shape[0] for a in it), it[0].shape[1]) for it in items]
    extra = [] if after is None else [after]
    plan, a_idx, brow = [], 0, 0
    for o_idx, it in enumerate(items):
        orow = 0
        for a in it:
            ra, n = a.shape
            for lane0 in range(0, n, SMALL_LANES):
                plan.append((a_idx, o_idx, orow, ra, lane0, min(SMALL_LANES, n - lane0), brow))
                brow += ra
            orow += ra
            a_idx += 1
    nrows = -(-brow // 8) * 8

    def body(*refs):
        ins, refs = refs[:len(flat)], refs[len(flat) + len(extra):]
        outs = refs[:len(items)]
        mine, buf, ssem, rsem = refs[len(items):]
        x, y, c = _pos()
        me = 4 * x + 2 * y + c
        mine[...] = jnp.zeros_like(mine)
        for ai, _, _, ra, lane0, w, br in plan:
            mine[br:br + ra, 0:w] = ins[ai][:, lane0:lane0 + w]
        buf[me] = mine[...]
        cps = []
        for k in range(1, 8):
            dev = (x ^ (k >> 2), y ^ ((k >> 1) & 1), c ^ (k & 1))
            cp = _rcopy(mine, buf.at[me], ssem.at[k - 1], rsem.at[k - 1], dev)
            cp.start()
            cps.append(cp)
        for cp in cps:
            cp.wait()
        for _, oi, orow, ra, lane0, w, br in plan:
            acc = buf[0, br:br + ra, 0:w]
            for d in range(1, 8):
                acc = acc + buf[d, br:br + ra, 0:w]
            outs[oi][orow:orow + ra, lane0:lane0 + w] = acc

    vm = pl.BlockSpec(memory_space=pltpu.VMEM)
    return pl.pallas_call(
        body, name=name, out_shape=[jax.ShapeDtypeStruct(s, F32) for s in shapes],
        in_specs=[vm] * len(flat) + [ANY] * len(extra), out_specs=[vm] * len(items),
        scratch_shapes=[pltpu.VMEM((nrows, SMALL_LANES), F32), pltpu.VMEM((8, nrows, SMALL_LANES), F32),
                        _dma_sems(7), _dma_sems(7)],
        compiler_params=pltpu.CompilerParams(has_side_effects=True),
    )(*flat, *extra)


def adam_small(ws, gs, ms, vs):
    n = len(ws)

    def body(*refs):
        for k in range(n):
            w, g, m, v = (refs[j * n + k][...] for j in range(4))
            for j, val in enumerate(_adamw(w, g, m, v)):
                refs[(4 + j) * n + k][...] = val

    vm = pl.BlockSpec(memory_space=pltpu.VMEM)
    res = pl.pallas_call(
        body, name="adam_small", out_shape=[jax.ShapeDtypeStruct(w.shape, F32) for w in ws] * 3,
        in_specs=[vm] * (4 * n), out_specs=[vm] * (3 * n),
    )(*ws, *gs, *ms, *vs)
    return res[:n], res[n:2 * n], res[2 * n:]


def _sel(rows, cols, pairs):
    m = np.zeros((rows, cols), np.float32)
    for r, c in pairs:
        m[r, c] = 1.0
    return jnp.asarray(m)


def _pad_win(w):
    z = jnp.zeros((w.shape[0], 112), w.dtype)
    return jnp.concatenate([w[:, :4096], w[:, 4112:6672], w[:, 4096:4112], z, w[:, 6672:6688], z], axis=1)


def _unpad_win(wp):
    return jnp.concatenate([wp[:, :4096], wp[:, 6656:6672], wp[:, 4096:6656], wp[:, 6784:6800]], axis=1)


def kernel(x, mem, norm1_w, w_in, gdn_conv_w, gdn_a_log, gdn_dt_bias, gdn_norm_w, ssm_conv_w, ssm_conv_b, ssm_a_log, ssm_dt_bias, ssm_d, ssm_norm_w, w_out, norm2_w, mem_norm_w, wq_mem, wk_mem, wv_mem, wo_mem, norm3_w, w_up, w_down, final_norm_w, loss_target, m_norm1_w, m_w_in, m_gdn_conv_w, m_gdn_a_log, m_gdn_dt_bias, m_gdn_norm_w, m_ssm_conv_w, m_ssm_conv_b, m_ssm_a_log, m_ssm_dt_bias, m_ssm_d, m_ssm_norm_w, m_w_out, m_norm2_w, m_mem_norm_w, m_wq_mem, m_wk_mem, m_wv_mem, m_wo_mem, m_norm3_w, m_w_up, m_w_down, m_final_norm_w, v_norm1_w, v_w_in, v_gdn_conv_w, v_gdn_a_log, v_gdn_dt_bias, v_gdn_norm_w, v_ssm_conv_w, v_ssm_conv_b, v_ssm_a_log, v_ssm_dt_bias, v_ssm_d, v_ssm_norm_w, v_w_out, v_norm2_w, v_mem_norm_w, v_wq_mem, v_wk_mem, v_wv_mem, v_wo_mem, v_norm3_w, v_w_up, v_w_down, v_final_norm_w):
    T, M = x.shape[1], mem.shape[1]
    xi, yi, ci = _pos()
    s_me = 2 * xi + yi
    x0, mem0, tgt = x[0], mem[0], loss_target[0]
    tb = min(512, T)
    tbl = min(1024, T)
    tbp = min(512, T)
    row = lambda v: v.reshape(1, -1)

    win_g = all_gather_chips("ag_win", w_in.astype(BF16), s_me)
    w_in_p = _pad_win(win_g.transpose(1, 0, 2).reshape(D, IN_COLS))
    keep = (ci == 0).astype(F32)
    gcw_z = lax.dynamic_update_slice(jnp.zeros((4, 3 * D), F32), gdn_conv_w * keep, (0, s_me * 768))
    scw_z = lax.dynamic_update_slice(jnp.zeros((4, 1536), F32), ssm_conv_w * keep, (0, s_me * 384))
    gcw, scw = all_reduce_items("ar_convw", [[gcw_z], [scw_z]])
    scw_x, scw_bc = scw[:, :D], scw[:, D:]
    rest_shard = jnp.concatenate([w_up, w_down, w_out, wq_mem, wk_mem, wv_mem, wo_mem], axis=0).astype(BF16)
    ag_sems, rest_thru, rest_land, ag_token = split_copy_start("ag_rest_start", rest_shard, (4,) + rest_shard.shape,
                                                               ag_pieces(rest_shard.shape[0] // 2), gcw)
    sp = jnp.stack([ci, s_me]).astype(jnp.int32)
    scb_x, scb_bc = row(ssm_conv_b[:D]), row(ssm_conv_b[D:])

    galog_c, gdtb_c = row(jnp.pad(gdn_a_log, (8, 112))), row(jnp.pad(gdn_dt_bias, (8, 112)))
    salog_c, sdtb_c = row(jnp.pad(ssm_a_log, (0, 112))), row(jnp.pad(ssm_dt_bias, (0, 112)))
    sd_x = row(jnp.repeat(ssm_d, 64))
    eb = _sel(128, D, [(h, 128 * h + l) for h in range(8) for l in range(128)])
    ea = _sel(128, D, [(8 + h, 128 * h + l) for h in range(8) for l in range(128)])
    e16 = _sel(128, D, [(h, 64 * h + l) for h in range(16) for l in range(64)])

    h1 = rowwise(rms_fwd_fn, "rms1", T, tbl, [R(x0)], [row(norm1_w) + ag_token[0:1, 0:1]], [(D, BF16)])[0]
    p = matmul("mm_in", h1, w_in_p, "nn", 2048, 768, 1024, [F32])[0]
    gp_ins = [R(p, 3 * D, CB_QKV, "prev"), R(p, 128, CB_BA)]
    qn, kn, vv, gcs_x, beta_x, ggate, gcs_t = rowwise(gdn_prep_fn, "gdn_prep", T, tbp, gp_ins,
                                                      [gcw, galog_c, gdtb_c, eb, ea],
                                                      [(D, F32)] * 5 + [(128, F32), (-8, F32)])
    gcs_t = gcs_t.reshape(GDN_H, 1, T)
    gtb, ggh = min(128, T), 8
    o_gdn, s_states, tinv = gdn_fwd(qn, kn, vv, gcs_x, beta_x, gcs_t, min(256, T), ggh)
    gnw = row(gdn_norm_w)
    oa = rowwise(gdn_post_fn, "gdn_post", T, tbl, [R(o_gdn), R(p, D, CB_Z)], [gnw], [(D, BF16)])[0]
    sp_ins = [R(p, D, CB_XS, "prev"), R(p, 512, CB_BC, "prev"), R(p, 128, CB_DT)]
    sp_full = [scw_x, scw_bc, scb_x, scb_bc, salog_c, sdtb_c]
    xs, bc, dt_x, acs_x, acs_t = rowwise(ssd_prep_fn, "ssd_prep", T, tbp, sp_ins, sp_full + [e16],
                                         [(D, F32), (512, F32), (D, F32), (D, F32), (-SSM_H, F32)])
    acs_t = acs_t.reshape(SSM_H, 1, T)
    y_ssd, h_states = ssd_fwd(xs, bc, dt_x, acs_x, acs_t)
    snw = row(ssm_norm_w)
    ob = rowwise(ssd_post_fn, "ssd_post", T, tbl, [R(y_ssd), R(xs), R(p, D, CB_ZS)], [sd_x, snw], [(D, BF16)])[0]
    rest_thru, rest_land = split_copy_wait("ag_rest_wait", ag_sems, rest_thru, rest_land, ob,
                                           ag_pieces(rest_shard.shape[0] // 2))
    rest_g = _with_own(rest_thru, ag_forward("ag_rest_fwd", rest_land), s_me)
    assert D == 1024
    view = lambda shape, blk, at: dict(b_sel=(shape, blk, at))
    wup_n = view((D, D_FF), (None, D, D), lambda i, j, k: (j, 0, 0))
    wup_t = view((D, D_FF), (None, D, D), lambda i, j, k: (k, 0, 0))
    wdown_n = view((D_FF, D), (None, D, D), lambda i, j, k: (k, 1, 0))
    wdown_t = view((D_FF, D), (None, D, D), lambda i, j, k: (j, 1, 0))
    wout_a = view((D, D), (2, 512, D), lambda i, j, k: (0, 4, 0))
    wout_b = view((D, D), (2, 512, D), lambda i, j, k: (1, 4, 0))
    wq_v, wk_v, wv_v, wo_v = (view((D, D), (4, 256, D), lambda i, j, k, r=r: (0, r, 0)) for r in (10, 11, 12, 13))
    x1a = matmul("mm_out_a", oa, rest_g, "nn", 1024, 1024, 1024, [F32], _epi_res, [x0], **wout_a)[0]
    x1, h2 = matmul("mm_out_b", ob, rest_g, "nn", 1024, 1024, 1024, [F32, BF16], _epi_res_rms, [x1a],
                    [row(norm2_w)], **wout_b)

    mn = rowwise(rms_fwd_fn, "rms_mem", M, M, [R(mem0)], [row(mem_norm_w)], [(D, BF16)])[0]
    km = matmul("mm_k", mn, rest_g, "nn", 256, 1024, 1024, [BF16], **wk_v)[0]
    vm = matmul("mm_v", mn, rest_g, "nn", 256, 1024, 1024, [BF16], **wv_v)[0]
    qm = matmul("mm_q", h2, rest_g, "nn", 1024, 1024, 1024, [BF16], **wq_v)[0]
    ao = rowwise(attn_fn, "attn", T, tbl, [R(qm)], [km, vm], [(D, BF16)])[0]
    x2, h3 = matmul("mm_o", ao, rest_g, "nn", 1024, 1024, 1024, [F32, BF16], _epi_res_rms, [x1], [row(norm3_w)], **wo_v)
    u, act = matmul("mm_up", h3, rest_g, "nn", 2048, 1024, 1024, [BF16, BF16], _epi_relu2, **wup_n)
    wdown_n2 = view((D_FF, D), (2, D, D), lambda i, j, k: (k, 1, 0))
    x3 = matmul("mm_down", act, rest_g, "nn", 1024, 1024, 2048, [F32], _epi_res, [x2], **wdown_n2)[0]
    dx3, dx3b, loss_lane, g_final = rowwise(final_fn, "final", T, tbl, [R(x3), R(tgt)], [row(final_norm_w)],
                                            [(D, F32), (D, BF16)], [(1, D), (1, D)])
    loss = lax.psum(0.5 / D * jnp.sum(loss_lane), ("x", "y", "c"))

    dup = matmul("mm_dact", dx3b, rest_g, "nt", 2048, 1024, 1024, [BF16], _epi_dup, [u], **wdown_t)[0]
    def g_into(buf, blk, at):
        return dict(into=(buf, blk, lambda i, j, k, at=at: at(i, j)))

    grest = jax.ShapeDtypeStruct((4, 3584, D), F32)
    grest = matmul("mm_gdown", act, dx3b, "tn", 1024, 1024, 4096, [F32],
                   **g_into(grest, (None, 1024, D), lambda i, j: (i, 1, 0)))
    wup_t4 = dict(b_sel=((D, D_FF), (4, D, D), lambda i, j, k: (0, 0, 0), "side by side"))
    dh3 = matmul("mm_dh3", dup, rest_g, "nt", 1024, 1024, 4096, [F32], **wup_t4)[0]
    dx2, dx2b, g_n3 = rowwise(rms_bwd_fn, "rms3_bwd", T, tbl, [R(x2), R(dh3), R(dx3)], [row(norm3_w)],
                              [(D, F32), (D, BF16)], [(1, D)])
    grest = matmul("mm_gup", h3, dup, "tn", 1024, 1024, 4096, [F32],
                   **g_into(grest, (None, 1024, D), lambda i, j: (j, 0, 0)))
    dao = matmul("mm_dao", dx2b, rest_g, "nt", 1024, 1024, 1024, [F32], **wo_v)[0]
    grest = matmul("mm_gwo", ao, dx2b, "tn", 1024, 1024, 2048, [F32],
                   **g_into(grest, (4, 256, D), lambda i, j: (0, 13, 0)))
    dqm, dkm, dvm = rowwise(attn_bwd_fn, "attn_bwd", T, tb, [R(qm), R(dao)], [km, vm], [(D, BF16)],
                            [(M, D), (M, D)])
    dx1, dx1b, g_n2 = matmul("mm_dh2", dqm, rest_g, "nt", 512, 1024, 1024, [F32, BF16], _epi_rms_bwd, [x1, dx2],
                             [row(norm2_w)], n_acc=1, **wq_v)
    grest = matmul("mm_gwq", h2, dqm, "tn", 1024, 1024, 2048, [F32],
                   **g_into(grest, (4, 256, D), lambda i, j: (0, 10, 0)))
    grest = matmul("mm_gwk", mn, dkm, "tn", 1024, 1024, 256, [F32],
                   **g_into(grest, (4, 256, D), lambda i, j: (0, 11, 0)))
    grest = matmul("mm_gwv", mn, dvm, "tn", 1024, 1024, 256, [F32],
                   **g_into(grest, (4, 256, D), lambda i, j: (0, 12, 0)))
    dmn_k = matmul("mm_dmk", dkm, rest_g, "nt", 256, 1024, 1024, [F32], **wk_v)[0]
    dmn = matmul("mm_dmv", dvm, rest_g, "nt", 256, 1024, 1024, [F32], _epi_res, [dmn_k], **wv_v)[0]
    g_nmem = rowwise(rms_bwd_w_fn, "rmsmem_bwd", M, M, [R(mem0), R(dmn)], [row(mem_norm_w)], [], [(1, D)])[0]
    doa = matmul("mm_doa", dx1b, rest_g, "nt", 2048, 1024, 1024, [F32], **wout_a)[0]
    dob = matmul("mm_dob", dx1b, rest_g, "nt", 2048, 1024, 1024, [F32], **wout_b)[0]
    grest = matmul("mm_gwout_a", oa, dx1b, "tn", 1024, 1024, 2048, [F32],
                   **g_into(grest, (2, 512, D), lambda i, j: (0, 4, 0)))
    grest = matmul("mm_gwout_b", ob, dx1b, "tn", 1024, 1024, 2048, [F32],
                   **g_into(grest, (2, 512, D), lambda i, j: (1, 4, 0)))

    rs_rest = rs_begin("rs_rest", grest, 256, sp, True)

    dp = jax.ShapeDtypeStruct((T, p.shape[1]), BF16)
    dy_ssd, dxs_dir, dp, g_snw, g_sd_lane = rowwise(
        ssd_post_bwd_fn, "ssd_post_bwd", T, tb, [R(y_ssd), R(xs), R(p, D, CB_ZS), R(dob)],
        [sd_x + rs_rest["token"][0:1, 0:1], snw],
        [(D, F32), (D, F32), (D, BF16, dp, CB_ZS)], [(1, D), (1, D)])
    dxs_scan, db_s, dc_s, dgate, dacs_t = ssd_bwd(xs, bc, dt_x, acs_x, acs_t, dy_ssd, h_states)
    spb = rowwise(ssd_prep_bwd_fn, "ssd_prep_bwd", T, tbp,
                  sp_ins + [R(dxs_scan), R(dxs_dir), R(db_s), R(dc_s), R(dgate), RC(dacs_t.reshape(SSM_H, T))], sp_full,
                  [(D, F32), (512, F32), (128, BF16, dp, CB_DT)],
                  [(1, D)] * 4 + [(1, 512)] * 4 + [(1, D), (1, 512), (1, 128), (1, 128)])
    dyc_x, dyc_bc, dp = spb[:3]
    dp = rowwise(conv_bwd_fn, "conv_bwd_x", T, tbp, [R(dyc_x, halo="next")], [scw_x], [(D, BF16, dp, CB_XS)])[0]
    dp = rowwise(conv_bwd_fn, "conv_bwd_bc", T, tbp, [R(dyc_bc, halo="next")], [scw_bc], [(512, BF16, dp, CB_BC)])[0]

    do_gdn, dp, g_gnw = rowwise(gdn_post_bwd_fn, "gdn_post_bwd", T, tb, [R(o_gdn), R(p, D, CB_Z), R(doa)], [gnw],
                                [(D, F32), (D, BF16, dp, CB_Z)], [(1, 128)])
    dqn, dkn, dvv, dggate, dgcs_t = gdn_bwd(qn, kn, vv, ggate, gcs_t, do_gdn, s_states, tinv, min(256, T), ggh)
    gpb = rowwise(gdn_prep_bwd_fn, "gdn_prep_bwd", T, tbp,
                  gp_ins + [R(dqn), R(dkn), R(dvv), R(dggate), RC(dgcs_t.reshape(GDN_H, T))],
                  [gcw, galog_c, gdtb_c],
                  [(3 * D, F32), (128, BF16, dp, CB_BA)], [(1, 3 * D)] * 4 + [(1, 128), (1, 128)])
    dyc_qkv, dp = gpb[:2]
    dp = rowwise(conv_bwd_fn, "conv_bwd_qkv", T, tbp, [R(dyc_qkv, halo="next")], [gcw], [(3 * D, BF16, dp, CB_QKV)])[0]
    dh1 = matmul("mm_dh1", dp, w_in_p, "nt", 1024, 1024, 2304, [F32])[0]
    grad_x, g_n1 = rowwise(rms_bwd1_fn, "rms1_bwd", T, tbl, [R(x0), R(dh1), R(dx1)], [row(norm1_w)], [(D, F32)], [(1, D)])
    g_win_p = matmul("mm_gwin", h1, dp, "tn", 1024, 768, 4096, [F32])[0]

    items = [[g_n1], [gpb[6]], [gpb[7]], [g_gnw], [spb[11]], [spb[12]], [spb[13]], [spb[14]], [g_sd_lane], [g_snw],
             [g_n2], [g_nmem], [g_n3], [g_final], list(gpb[2:6]), list(spb[3:7]), list(spb[7:11])]
    (gr_n1, r_galog, r_gdtb, gr_gnw, r_scb_x, r_scb_bc, r_salog, r_sdtb, r_sd, gr_snw, gr_n2, gr_nmem, gr_n3,
     gr_final, r_gcw, r_scw_x, r_scw_bc) = all_reduce_items("ar_grads", items)
    gr_galog, gr_gdtb = r_galog[:, 8:16], r_gdtb[:, 8:16]
    gr_salog, gr_sdtb = r_salog[:, :SSM_H], r_sdtb[:, :SSM_H]
    gr_sd = r_sd.reshape(SSM_H, SSM_P).sum(axis=1).reshape(1, SSM_H)
    gr_scb = jnp.concatenate([r_scb_x, r_scb_bc], axis=1)
    gr_gcw = lax.dynamic_slice(r_gcw, (0, s_me * 768), (4, 768))
    gr_scw = lax.dynamic_slice(jnp.concatenate([r_scw_x, r_scw_bc], axis=1), (0, s_me * 384), (4, 384))

    g_win = _unpad_win(g_win_p).reshape(D, 4, IN_COLS // 4).transpose(1, 0, 2)
    rs_win = rs_begin("rs_win", g_win, 256, sp, True, gr_n1)
    red_r, oth_r = rs_end(rs_rest, rs_win["token"])

    big = {}
    for n, w, m, v, blk0 in (("w_up", w_up, m_w_up, v_w_up, 0), ("w_down", w_down, m_w_down, v_w_down, 4),
                             ("w_out", w_out, m_w_out, v_w_out, 8), ("wq_mem", wq_mem, m_wq_mem, v_wq_mem, 10),
                             ("wk_mem", wk_mem, m_wk_mem, v_wk_mem, 11), ("wv_mem", wv_mem, m_wv_mem, v_wv_mem, 12),
                             ("wo_mem", wo_mem, m_wo_mem, v_wo_mem, 13)):
        big[n] = adam_halves("adam_" + n, w, m, v, red_r, oth_r, 256, blk0, sp)
    red_w, oth_w = rs_end(rs_win, big["wo_mem"][1])
    g_win_t = jnp.where(ci == 0, jnp.concatenate([red_w, oth_w], axis=0), jnp.concatenate([oth_w, red_w], axis=0)).T
    win_rows = g_win_t.shape[0]
    tbw = 152 if win_rows % 152 == 0 else win_rows
    d_t, m_t, v_t = rowwise(lambda i, n, w, g, m, v: _adamw(w, g, m, v), "adam_win", win_rows, tbw,
                            [R(w_in.T), R(g_win_t), R(m_w_in.T), R(v_w_in.T)], [], [(D, F32)] * 3)
    big["w_in"] = (g_win_t.T, d_t.T, m_t.T, v_t.T)
    names_s =["norm1_w", "gdn_conv_w", "gdn_a_log", "gdn_dt_bias", "gdn_norm_w", "ssm_conv_w", "ssm_conv_b",
               "ssm_a_log", "ssm_dt_bias", "ssm_d", "ssm_norm_w", "norm2_w", "mem_norm_w", "norm3_w", "final_norm_w"]
    w_s = [norm1_w, gdn_conv_w, gdn_a_log, gdn_dt_bias, gdn_norm_w, ssm_conv_w, ssm_conv_b, ssm_a_log, ssm_dt_bias,
           ssm_d, ssm_norm_w, norm2_w, mem_norm_w, norm3_w, final_norm_w]
    g_s = [gr_n1, gr_gcw, gr_galog, gr_gdtb, gr_gnw, gr_scw, gr_scb, gr_salog, gr_sdtb, gr_sd, gr_snw, gr_n2,
           gr_nmem, gr_n3, gr_final]
    m_s = [m_norm1_w, m_gdn_conv_w, m_gdn_a_log, m_gdn_dt_bias, m_gdn_norm_w, m_ssm_conv_w, m_ssm_conv_b, m_ssm_a_log,
           m_ssm_dt_bias, m_ssm_d, m_ssm_norm_w, m_norm2_w, m_mem_norm_w, m_norm3_w, m_final_norm_w]
    v_s = [v_norm1_w, v_gdn_conv_w, v_gdn_a_log, v_gdn_dt_bias, v_gdn_norm_w, v_ssm_conv_w, v_ssm_conv_b, v_ssm_a_log,
           v_ssm_dt_bias, v_ssm_d, v_ssm_norm_w, v_norm2_w, v_mem_norm_w, v_norm3_w, v_final_norm_w]
    shp_s = [w.shape for w in w_s]
    as2d = lambda a: a if a.ndim == 2 else a.reshape(1, -1)
    d_l, m_l, v_l = adam_small([as2d(a) for a in w_s], [as2d(a) for a in g_s], [as2d(a) for a in m_s],
                               [as2d(a) for a in v_s])

    grads, deltas, new_m, new_v = {}, {}, {}, {}
    for n, (gg, dd, mm_, vv_) in big.items():
        grads[n], deltas[n], new_m[n], new_v[n] = gg, dd, mm_, vv_
    for k, n in enumerate(names_s):
        grads[n] = g_s[k].reshape(shp_s[k])
        deltas[n], new_m[n], new_v[n] = (a[k].reshape(shp_s[k]) for a in (d_l, m_l, v_l))
    order = ["norm1_w", "w_in", "gdn_conv_w", "gdn_a_log", "gdn_dt_bias", "gdn_norm_w", "ssm_conv_w", "ssm_conv_b",
             "ssm_a_log", "ssm_dt_bias", "ssm_d", "ssm_norm_w", "w_out", "norm2_w", "mem_norm_w", "wq_mem", "wk_mem",
             "wv_mem", "wo_mem", "norm3_w", "w_up", "w_down", "final_norm_w"]
    return (loss, grad_x[None], *[grads[n] for n in order], *[deltas[n] for n in order],
            *[new_m[n] for n in order], *[new_v[n] for n in order])
```

```python
import numpy as np
import jax
import jax.numpy as jnp
from jax import lax
from jax.experimental import pallas as pl
from jax.experimental.pallas import tpu as pltpu

F32, BF16 = jnp.float32, jnp.bfloat16
MESH = pl.DeviceIdType.MESH
ANY = pl.BlockSpec(memory_space=pl.ANY)

EPS = 1e-6
D = 1024
GDN_H, GDN_DK, GDN_C = 8, 128, 64
SSM_H, SSM_P, SSM_N, SSM_L = 16, 64, 128, 128
MEM_H, MEM_DH = 4, 256
D_FF = 4096
IN_COLS = 6688
CB_QKV, CB_Z, CB_ZS, CB_XS, CB_BC, CB_BA, CB_DT = 0, 3, 4, 5, 12, 52, 53
VMEM_LIMIT = 56 * 1024 * 1024
D2D_CHUNKS = 8
ICI_CHUNKS = 4

ADAM_LR, ADAM_B1, ADAM_B2, ADAM_EPS, ADAM_WD, ADAM_STEP = 0.001, 0.9, 0.999, 1e-08, 0.01, 10


def _dg(a, b, ca, cb):
    return lax.dot_general(a, b, (((ca,), (cb,)), ((), ())), preferred_element_type=F32)


def _bf(x):
    return x.astype(BF16)


def mm(a, b):
    return _dg(_bf(a), _bf(b), 1, 0)


def mm_nt(a, b):
    return _dg(_bf(a), _bf(b), 1, 1)


def mm_tn(a, b):
    return _dg(_bf(a), _bf(b), 0, 0)


def mm_sel(a, sel):
    hi = a.astype(BF16)
    r1 = a - hi.astype(F32)
    mid = r1.astype(BF16)
    lo = (r1 - mid.astype(F32)).astype(BF16)
    s = sel.astype(BF16)
    return _dg(hi, s, 1, 0) + (_dg(mid, s, 1, 0) + _dg(lo, s, 1, 0))


def mm3(a, b):
    ah, bh = a.astype(BF16), b.astype(BF16)
    al, bl = (a - ah.astype(F32)).astype(BF16), (b - bh.astype(F32)).astype(BF16)
    return _dg(ah, bh, 1, 0) + (_dg(ah, bl, 1, 0) + _dg(al, bh, 1, 0))


def _iota(shape, dim):
    return lax.broadcasted_iota(jnp.int32, shape, dim)


def _chunk_cumsum(x, c):
    pos = _iota(x.shape, 0) & (c - 1)
    s = 1
    while s < c:
        x = x + jnp.where(pos >= s, pltpu.roll(x, s, 0), 0.0)
        s *= 2
    return x


def _chunk_revcumsum(x, c):
    n = x.shape[0]
    pos = _iota(x.shape, 0) & (c - 1)
    s = 1
    while s < c:
        x = x + jnp.where(pos < c - s, pltpu.roll(x, n - s, 0), 0.0)
        s *= 2
    return x


def _sig(x):
    return jax.nn.sigmoid(x)


def _softplus(x):
    return jnp.maximum(x, 0.0) + jnp.log(1.0 + jnp.exp(-jnp.abs(x)))


def _rows(v):
    return jnp.sum(v, axis=0, keepdims=True)


def _lanes(v):
    return jnp.sum(v, axis=1, keepdims=True)


def _sum_all(v):
    return _rows(_lanes(v))


def _cparams(sem):
    return pltpu.CompilerParams(dimension_semantics=sem, vmem_limit_bytes=VMEM_LIMIT)


def rowwise(fn, name, T, tb, row_ins, full_ins, row_outs, acc_outs=(), sp=None):
    nblk = T // tb
    assert nblk * tb == T
    has_sp = sp is not None

    def imap(f):
        return (lambda i, s: f(i, s)) if has_sp else (lambda i: f(i, None))

    in_specs, args = [], []
    for arr, w, cb, halo, off in row_ins:
        if halo == "col":
            in_specs.append(pl.BlockSpec((w, tb), imap(lambda i, s: (0, i))))
            args.append(arr)
            continue
        rowf = off if callable(off) else (lambda i, s, off=off: i + off)
        in_specs.append(pl.BlockSpec((tb, w), imap(lambda i, s, cb=cb, rowf=rowf: (rowf(i, s), cb))))
        args.append(arr)
        if halo == "prev":
            r = tb // 8
            in_specs.append(pl.BlockSpec((8, w), imap(lambda i, s, cb=cb, r=r: (jnp.maximum(i * r - 1, 0), cb))))
            args.append(arr)
        elif halo == "next":
            r, last = tb // 8, T // 8 - 1
            in_specs.append(pl.BlockSpec((8, w), imap(lambda i, s, cb=cb, r=r, last=last:
                                                      (jnp.minimum((i + 1) * r, last), cb))))
            args.append(arr)
    for arr in full_ins:
        in_specs.append(pl.BlockSpec(arr.shape, imap(lambda i, s, nd=arr.ndim: (0,) * nd)))
        args.append(arr)
    n_in, n_ro = len(args), len(row_outs)
    out_shape, out_specs, aliases = [], [], {}
    for k, (w, dt, *dest) in enumerate(row_outs):
        if dest:
            buf, cb = dest
            out_shape.append(jax.ShapeDtypeStruct(buf.shape, buf.dtype))
            out_specs.append(pl.BlockSpec((tb, w), imap(lambda i, s, cb=cb: (i, cb))))
            if not isinstance(buf, jax.ShapeDtypeStruct):
                aliases[len(args) + int(has_sp)] = k
                in_specs.append(ANY)
                args.append(buf)
        elif w < 0:
            out_shape.append(jax.ShapeDtypeStruct((-w, T), dt))
            out_specs.append(pl.BlockSpec((-w, tb), imap(lambda i, s: (0, i))))
        else:
            out_shape.append(jax.ShapeDtypeStruct((T, w), dt))
            out_specs.append(pl.BlockSpec((tb, w), imap(lambda i, s: (i, 0))))
    for shp in acc_outs:
        out_shape.append(jax.ShapeDtypeStruct(shp, F32))
        out_specs.append(pl.BlockSpec(shp, imap(lambda i, s, nd=len(shp): (0,) * nd)))

    def body(*refs):
        i = pl.program_id(0)
        if has_sp:
            sp_ref, refs = refs[0], refs[1:]
            vals = fn(i, nblk, sp_ref, *[r[...] for r in refs[:n_in]])
        else:
            vals = fn(i, nblk, *[r[...] for r in refs[:n_in]])
        outs = refs[n_in + len(aliases):]
        for ref, val in zip(outs[:n_ro], vals[:n_ro]):
            ref[...] = val.astype(ref.dtype)
        for ref, val in zip(outs[n_ro:], vals[n_ro:]):
            @pl.when(i == 0)
            def _(ref=ref, val=val):
                ref[...] = val

            @pl.when(i > 0)
            def _(ref=ref, val=val):
                ref[...] += val

    cparams = _cparams(("arbitrary",) if acc_outs else ("parallel",))
    if has_sp:
        return pl.pallas_call(
            body, name=name, out_shape=out_shape, compiler_params=cparams, input_output_aliases=aliases,
            grid_spec=pltpu.PrefetchScalarGridSpec(num_scalar_prefetch=1, grid=(nblk,), in_specs=in_specs,
                                                   out_specs=out_specs),
        )(sp, *args)
    return pl.pallas_call(
        body, name=name, grid=(nblk,), in_specs=in_specs, out_specs=out_specs, out_shape=out_shape,
        compiler_params=cparams, input_output_aliases=aliases,
    )(*args)


def R(arr, w=None, cb=0, halo=None, off=0):
    return (arr, arr.shape[1] if w is None else w, cb, halo, off)


def RC(arr):
    return (arr, arr.shape[0], 0, "col", 0)


def matmul(name, a, b, form, tm, tn, tk, out_dtypes, epi=None, extras=(), rows=(), into=None, n_acc=0, b_sel=None):
    bs = b.shape if b_sel is None else b_sel[0]
    if form == "nn":
        (M, K), N = a.shape, bs[1]
    elif form == "nt":
        (M, K), N = a.shape, bs[0]
    else:
        (K, M), N = a.shape, bs[1]
    tm, tn, tk = min(tm, M), min(tn, N), min(tk, K)
    assert M % tm == 0 and N % tn == 0 and K % tk == 0, (name, M, N, K, tm, tn, tk)

    def b_spec_of(blk, at):
        if b_sel is None:
            return pl.BlockSpec(blk, lambda i, j, k: at(i, j, k))
        blk3 = b_sel[1]
        assert int(np.prod([d for d in blk3 if d is not None])) == blk[0] * blk[1], (name, blk3, blk)
        return pl.BlockSpec(blk3, lambda i, j, k: b_sel[2](i, j, k))

    if form == "nn":
        a_spec = pl.BlockSpec((tm, tk), lambda i, j, k: (i, k))
        b_spec = b_spec_of((tk, tn), lambda i, j, k: (k, j))
        ca, cb = 1, 0
    elif form == "nt":
        a_spec = pl.BlockSpec((tm, tk), lambda i, j, k: (i, k))
        b_spec = b_spec_of((tn, tk), lambda i, j, k: (j, k))
        ca, cb = 1, 1
    else:
        a_spec = pl.BlockSpec((tk, tm), lambda i, j, k: (k, i))
        b_spec = b_spec_of((tk, tn), lambda i, j, k: (k, j))
        ca, cb = 0, 0
    nk, ne, no = K // tk, len(extras) + len(rows), len(out_dtypes)
    if epi is None:
        epi = lambda acc: (acc,)

    assert n_acc == 0 or tn == N

    def body(a_ref, b_ref, *rest):
        ex, outs, accs, acc = rest[:ne], rest[ne:ne + no], rest[ne + no:ne + no + n_acc], rest[ne + no + n_acc]
        i, k = pl.program_id(0), pl.program_id(2)

        def finish(total):
            vals = epi(total, *[e[...] for e in ex])
            for r, v in zip(outs, vals[:no]):
                r[...] = v.astype(r.dtype).reshape(r.shape)
            for r, v in zip(accs, vals[no:]):
                @pl.when(i == 0)
                def _(r=r, v=v):
                    r[...] = v

                @pl.when(i > 0)
                def _(r=r, v=v):
                    r[...] += v

        b_tile = b_ref[...]
        if b_sel is not None and len(b_sel) > 3:
            b_tile = jnp.concatenate([b_tile[s] for s in range(b_tile.shape[0])], axis=1)
        prod = _dg(_bf(a_ref[...]), _bf(b_tile.reshape(-1, b_tile.shape[-1])), ca, cb)
        if nk == 1:
            finish(prod)
            return

        @pl.when(k == 0)
        def _():
            acc[...] = prod

        @pl.when(k > 0)
        def _():
            acc[...] += prod

        @pl.when(k == nk - 1)
        def _():
            finish(acc[...])

    mn = pl.BlockSpec((tm, tn), lambda i, j, k: (i, j))
    rw = pl.BlockSpec((1, tn), lambda i, j, k: (0, j))
    acc_scratch = pltpu.VMEM((tm, tn) if nk > 1 else (8, 128), F32)
    if into is not None:
        buf, blk, bmap = into
        assert ne == 0 and no == 1
        aliased = not isinstance(buf, jax.ShapeDtypeStruct)

        def body_into(a_ref, b_ref, *rest):
            body(a_ref, b_ref, *rest[-2:])

        return pl.pallas_call(
            body_into, name=name, grid=(M // tm, N // tn, nk),
            in_specs=[a_spec, b_spec] + ([ANY] if aliased else []), out_specs=pl.BlockSpec(blk, bmap),
            out_shape=jax.ShapeDtypeStruct(buf.shape, buf.dtype),
            scratch_shapes=[acc_scratch],
            input_output_aliases={2: 0} if aliased else {},
            compiler_params=_cparams(("parallel", "parallel", "arbitrary")),
        )(a, b, *([buf] if aliased else []))
    return pl.pallas_call(
        body, name=name, grid=(M // tm, N // tn, nk),
        in_specs=[a_spec, b_spec] + [mn] * len(extras) + [rw] * len(rows), out_specs=[mn] * no + [rw] * n_acc,
        out_shape=[jax.ShapeDtypeStruct((M, N), dt) for dt in out_dtypes] + [jax.ShapeDtypeStruct((1, N), F32)] * n_acc,
        scratch_shapes=[acc_scratch],
        compiler_params=_cparams(("arbitrary",) * 3 if n_acc else ("parallel", "parallel", "arbitrary")),
    )(a, b, *extras, *rows)


def _epi_res(acc, res):
    return (res + acc,)


def _epi_rms_bwd(acc, x, dres, w):
    return rms_bwd_fn(0, 0, x, acc, dres, w)


def rms_bwd1_fn(i, n, x, dh, dres, w):
    dx, _, gw = rms_bwd_fn(i, n, x, dh, dres, w)
    return dx, gw


def _epi_final(acc, res, tgt, w):
    return final_fn(0, 0, res + acc, tgt, w)


def _epi_res_rms(acc, res, w):
    x = res + acc
    return (x, x * lax.rsqrt(jnp.mean(x * x, axis=-1, keepdims=True) + EPS) * w)


def _epi_relu2(acc):
    u = jnp.maximum(acc, 0.0)
    return (u, u * u)


def _epi_dup(acc, u):
    return (acc * 2.0 * u.astype(F32),)


def _conv(x, halo, w, i):
    halo = jnp.where(i == 0, 0.0, halo)
    xt = jnp.concatenate([halo, x], axis=0)
    shifted = [pltpu.roll(xt, 3 - k, 0)[8:, :] for k in range(3)] + [x]
    y = shifted[3] * w[3:4, :]
    for k in range(3):
        y = y + shifted[k] * w[k:k + 1, :]
    return y, shifted


def _l2n(x, scale):
    outs = []
    for h in range(x.shape[1] // 128):
        xh = x[:, 128 * h:128 * h + 128]
        outs.append(xh * (lax.rsqrt(jnp.sum(xh * xh, axis=-1, keepdims=True) + EPS) * scale))
    return jnp.concatenate(outs, axis=1)


def _l2n_bwd(x, dy, scale):
    outs = []
    for h in range(x.shape[1] // 128):
        xh, dh = x[:, 128 * h:128 * h + 128], dy[:, 128 * h:128 * h + 128] * scale
        r = lax.rsqrt(jnp.sum(xh * xh, axis=-1, keepdims=True) + EPS)
        outs.append(r * dh - xh * (r * r * r) * jnp.sum(xh * dh, axis=-1, keepdims=True))
    return jnp.concatenate(outs, axis=1)


def rms_fwd_fn(i, n, x, w):
    r = lax.rsqrt(jnp.mean(x * x, axis=-1, keepdims=True) + EPS)
    return (x * r * w,)


def rms_bwd_fn(i, n, x, dh, dres, w):
    r = lax.rsqrt(jnp.mean(x * x, axis=-1, keepdims=True) + EPS)
    g = dh * w
    dx = dres + r * g - x * (r * r * r) * jnp.mean(x * g, axis=-1, keepdims=True)
    return dx, dx, _rows(dh * x * r)


def rms_bwd_w_fn(i, n, x, dh, w):
    r = lax.rsqrt(jnp.mean(x * x, axis=-1, keepdims=True) + EPS)
    return (_rows(dh * x * r),)


def final_fn(i, n, x, tgt, w):
    r = lax.rsqrt(jnp.mean(x * x, axis=-1, keepdims=True) + EPS)
    xn = x * r
    e = xn * w - tgt
    dy = e * (1.0 / D)
    g = dy * w
    dx = r * g - x * (r * r * r) * jnp.mean(x * g, axis=-1, keepdims=True)
    return dx, dx, _rows(e * e), _rows(dy * xn)


def _gdn_gates(ba, alog_c, dtb_c):
    col = _iota(ba.shape, 1)
    amask = (col >= 8) & (col < 16)
    beta = jnp.where(col < 8, _sig(ba), 0.0)
    z = ba + dtb_c
    ea_ = jnp.exp(alog_c)
    return beta, z, ea_, jnp.where(amask, -ea_ * _softplus(z), 0.0), amask


def _cols(x, g):
    return x[:, 128 * g:128 * g + 128]


def gdn_prep_fn(i, n, qkv, halo, ba, cw, alog_c, dtb_c, eb, ea):
    outs = [[], [], []]
    for g in range(3 * GDN_H):
        yc, _ = _conv(_cols(qkv, g), _cols(halo, g), _cols(cw, g), i)
        act = yc * _sig(yc)
        if g < 2 * GDN_H:
            act = _l2n(act, GDN_DK ** -0.5 if g < GDN_H else 1.0)
        outs[g // GDN_H].append(act)
    beta, _, _, gg, _ = _gdn_gates(ba, alog_c, dtb_c)
    gcs = _chunk_cumsum(gg, GDN_C)
    return (*[jnp.concatenate(o, axis=1) for o in outs], mm_sel(gcs, ea), mm_sel(beta, eb), beta + gcs,
            jnp.transpose(gcs)[8:16, :])


def gdn_prep_bwd_fn(i, n, qkv, halo, ba, dqn, dkn, dv, dgb, dgcs_t, cw, alog_c, dtb_c):
    dycs, dwl = [], [[], [], [], []]
    for g in range(3 * GDN_H):
        yc, shifted = _conv(_cols(qkv, g), _cols(halo, g), _cols(cw, g), i)
        sg = _sig(yc)
        act = yc * sg
        if g < GDN_H:
            d = _l2n_bwd(act, _cols(dqn, g), GDN_DK ** -0.5)
        elif g < 2 * GDN_H:
            d = _l2n_bwd(act, _cols(dkn, g - GDN_H), 1.0)
        else:
            d = _cols(dv, g - 2 * GDN_H)
        dyc_g = d * (sg * (1.0 + yc * (1.0 - sg)))
        dycs.append(dyc_g)
        for k in range(4):
            dwl[k].append(_rows(dyc_g * shifted[k]))
    dyc = jnp.concatenate(dycs, axis=1)
    dws = [jnp.concatenate(l, axis=1) for l in dwl]
    beta, z, ea_, g, amask = _gdn_gates(ba, alog_c, dtb_c)
    tbn = ba.shape[0]
    rowpart = jnp.transpose(jnp.concatenate([jnp.zeros((8, tbn), F32), dgcs_t, jnp.zeros((112, tbn), F32)], axis=0))
    dg = _chunk_revcumsum(jnp.where(amask, dgb, 0.0) - rowpart, GDN_C)
    draw = jnp.where(amask, dg * (-ea_) * _sig(z), 0.0)
    dba = draw + dgb * beta * (1.0 - beta)
    return (dyc, dba, dws[0], dws[1], dws[2], dws[3], _rows(dg * g), _rows(draw))


def conv_bwd_fn(i, n, dyc, halo, w):
    halo = jnp.where(i == n - 1, 0.0, halo)
    tb = dyc.shape[0]
    outs = []
    for g in range(dyc.shape[1] // 128):
        d, wg = _cols(dyc, g), _cols(w, g)
        xt = jnp.concatenate([d, _cols(halo, g)], axis=0)
        dx = d * wg[3:4, :]
        for k in range(3):
            dx = dx + pltpu.roll(xt, tb + 8 - (3 - k), 0)[:tb, :] * wg[k:k + 1, :]
        outs.append(dx)
    return (jnp.concatenate(outs, axis=1),)


def gdn_post_fn(i, n, o, z, w):
    outs = []
    for h in range(GDN_H):
        oh, zh = o[:, 128 * h:128 * h + 128], z[:, 128 * h:128 * h + 128]
        r = lax.rsqrt(jnp.mean(oh * oh, axis=-1, keepdims=True) + EPS)
        outs.append(oh * r * w * (zh * _sig(zh)))
    return (jnp.concatenate(outs, axis=1),)


def gdn_post_bwd_fn(i, n, o, z, doa, w):
    dos, dzs, dw = [], [], None
    for h in range(GDN_H):
        sl = slice(128 * h, 128 * h + 128)
        oh, zh, dh = o[:, sl], z[:, sl], doa[:, sl]
        r = lax.rsqrt(jnp.mean(oh * oh, axis=-1, keepdims=True) + EPS)
        s = _sig(zh)
        dn = dh * (zh * s)
        dzs.append(dh * (oh * r * w) * (s * (1.0 + zh * (1.0 - s))))
        t = _rows(dn * oh * r)
        dw = t if dw is None else dw + t
        g = dn * w
        dos.append(r * g - oh * (r * r * r) * jnp.mean(oh * g, axis=-1, keepdims=True))
    return jnp.concatenate(dos, axis=1), jnp.concatenate(dzs, axis=1), dw


def _ssd_gates(dtblk, alog_c, dtb_c):
    hmask = _iota(dtblk.shape, 1) < SSM_H
    z = dtblk + dtb_c
    return jnp.where(hmask, _softplus(z), 0.0), -jnp.exp(alog_c), z, hmask


def _silu_conv_cols(x, halo, w, b, i):
    outs = []
    for g in range(x.shape[1] // 128):
        yc, _ = _conv(_cols(x, g), _cols(halo, g), _cols(w, g), i)
        yc = yc + _cols(b, g)
        outs.append(yc * _sig(yc))
    return jnp.concatenate(outs, axis=1)


def _silu_conv_bwd_cols(x, halo, w, b, dout, i):
    dycs, dwl = [], [[], [], [], []]
    for g in range(x.shape[1] // 128):
        yc, shifted = _conv(_cols(x, g), _cols(halo, g), _cols(w, g), i)
        yc = yc + _cols(b, g)
        s = _sig(yc)
        dyc_g = _cols(dout, g) * (s * (1.0 + yc * (1.0 - s)))
        dycs.append(dyc_g)
        for k in range(4):
            dwl[k].append(_rows(dyc_g * shifted[k]))
    dyc = jnp.concatenate(dycs, axis=1)
    return dyc, [jnp.concatenate(l, axis=1) for l in dwl], _rows(dyc)


def ssd_prep_fn(i, n, xp, hx, bcp, hbc, dtblk, cwx, cwbc, cbx, cbbc, alog_c, dtb_c, e16):
    dt, a_neg, _, _ = _ssd_gates(dtblk, alog_c, dtb_c)
    acs = _chunk_cumsum(dt * a_neg, SSM_L)
    return (_silu_conv_cols(xp, hx, cwx, cbx, i), _silu_conv_cols(bcp, hbc, cwbc, cbbc, i), mm_sel(dt, e16),
            mm_sel(acs, e16), jnp.transpose(acs)[0:SSM_H, :])


def ssd_prep_bwd_fn(i, n, xp, hx, bcp, hbc, dtblk, dxs_a, dxs_b, db, dc, dgate, dacs_t, cwx, cwbc, cbx, cbbc, alog_c, dtb_c):
    dyx, dwx, dbx = _silu_conv_bwd_cols(xp, hx, cwx, cbx, dxs_a + dxs_b, i)
    dybc, dwbc, dbbc = _silu_conv_bwd_cols(bcp, hbc, cwbc, cbbc, jnp.concatenate([db, dc], axis=1), i)
    dt, a_neg, z, hmask = _ssd_gates(dtblk, alog_c, dtb_c)
    g0, g1 = dgate[:, :128], dgate[:, 128:]
    col = _iota(g0.shape, 1)
    lo, mid = col < 8, (col >= 8) & (col < 16)
    dacs_col = jnp.where(lo, g0, 0.0) + pltpu.roll(jnp.where(lo, g1, 0.0), 8, 1)
    ddt_dir = pltpu.roll(jnp.where(mid, g0, 0.0), 120, 1) + jnp.where(mid, g1, 0.0)
    tbn = dtblk.shape[0]
    rowpart = jnp.transpose(jnp.concatenate([dacs_t, jnp.zeros((128 - SSM_H, tbn), F32)], axis=0))
    da = _chunk_revcumsum(dacs_col - rowpart, SSM_L)
    draw = jnp.where(hmask, (ddt_dir + da * a_neg) * _sig(z), 0.0)
    return (dyx, dybc, draw, *dwx, *dwbc, dbx, dbbc, _rows(da * dt * a_neg), _rows(draw))


def _ssd_gate(y, xs, zs, d_x):
    y2 = y + xs * d_x
    s = _sig(zs)
    return y2, s, y2 * (zs * s)


def ssd_post_fn(i, n, y, xs, zs, d_x, nw):
    _, _, yg = _ssd_gate(y, xs, zs, d_x)
    outs = []
    for g in range(2):
        v = yg[:, 512 * g:512 * g + 512]
        outs.append(v * lax.rsqrt(jnp.mean(v * v, axis=-1, keepdims=True) + EPS))
    return (jnp.concatenate(outs, axis=1) * nw,)


def ssd_post_bwd_fn(i, n, y, xs, zs, dob, d_x, nw):
    y2, s, yg = _ssd_gate(y, xs, zs, d_x)
    gfull = dob * nw
    dygs, dnw = [], []
    for g in range(2):
        sl = slice(512 * g, 512 * g + 512)
        v, gg = yg[:, sl], gfull[:, sl]
        r = lax.rsqrt(jnp.mean(v * v, axis=-1, keepdims=True) + EPS)
        dygs.append(r * gg - v * (r * r * r) * jnp.mean(v * gg, axis=-1, keepdims=True))
        dnw.append(_rows(dob[:, sl] * v * r))
    dyg = jnp.concatenate(dygs, axis=1)
    dy2 = dyg * (zs * s)
    dzs = dyg * y2 * (s * (1.0 + zs * (1.0 - s)))
    return dy2, dy2 * d_x, dzs, jnp.concatenate(dnw, axis=1), _rows(dy2 * xs)


def _attn_probs(q, k):
    hs = [slice(MEM_DH * h, MEM_DH * h + MEM_DH) for h in range(MEM_H)]
    ss = [mm_nt(q[:, sl], k[:, sl]) * (MEM_DH ** -0.5) for sl in hs]
    es = [jnp.exp(s - jnp.max(s, axis=-1, keepdims=True)) for s in ss]
    return hs, [e / jnp.sum(e, axis=-1, keepdims=True) for e in es]


def attn_fn(i, n, q, k, v):
    hs, ps = _attn_probs(q, k)
    return (jnp.concatenate([mm(p, v[:, sl]) for p, sl in zip(ps, hs)], axis=1),)


def attn_bwd_fn(i, n, q, do, k, v):
    hs, ps = _attn_probs(q, k)
    dvs = [mm_tn(p, do[:, sl]) for p, sl in zip(ps, hs)]
    dps = [mm_nt(do[:, sl], v[:, sl]) for sl in hs]
    dss = [p * (dp - jnp.sum(dp * p, axis=-1, keepdims=True)) * (MEM_DH ** -0.5) for p, dp in zip(ps, dps)]
    dqs = [mm(ds, k[:, sl]) for ds, sl in zip(dss, hs)]
    dks = [mm_tn(ds, q[:, sl]) for ds, sl in zip(dss, hs)]
    return jnp.concatenate(dqs, axis=1), jnp.concatenate(dks, axis=1), jnp.concatenate(dvs, axis=1)


def add2_fn(i, n, sp, a, b):
    return (a + b,)


def sum4_fn(i, n, sp, a, b, c, d):
    return (((a.astype(F32) + b.astype(F32)) + c.astype(F32)) + d.astype(F32),)


def _adamw(w, g, m, v):
    m = ADAM_B1 * m + (1.0 - ADAM_B1) * g
    v = ADAM_B2 * v + (1.0 - ADAM_B2) * (g * g)
    m_hat = m / (1.0 - ADAM_B1 ** ADAM_STEP)
    v_hat = v / (1.0 - ADAM_B2 ** ADAM_STEP)
    delta = -ADAM_LR * (m_hat / (jnp.sqrt(v_hat) + ADAM_EPS) + ADAM_WD * w)
    return delta, m, v


def _gate_cols(gb, h):
    lane = _iota(gb.shape, 1)
    return _lanes(jnp.where(lane == h, gb, 0.0)), _lanes(jnp.where(lane == 8 + h, gb, 0.0))


def _gdn_stage1(q, k, v, bb, gcs, grow):
    C = GDN_C
    row, col = _iota((C, C), 0), _iota((C, C), 1)
    incl, strict = row >= col, row > col
    dmat = jnp.where(incl, jnp.exp(jnp.minimum((gcs if gcs.shape[1] == 1 else gcs[:, :C]) - grow, 0.0)), 0.0)
    gam = jnp.exp(gcs)
    gl = gcs[C - 1:C, :]
    kb, vb = k * bb, v * bb
    kg = kb * gam
    lmat = jnp.where(strict, mm_nt(kb, k) * dmat, 0.0)
    pmat = jnp.where(incl, mm_nt(q, k) * dmat, 0.0)
    return dict(q=q, k=k, v=v, bb=bb, incl=incl, strict=strict, dmat=dmat, gam=gam, kb=kb, vb=vb, kg=kg,
                lmat=lmat, pmat=pmat, qd=q * gam, kdec=jnp.exp(gl - gcs), cd=jnp.exp(gl))


def _gdn_inverse(lmats):
    C = GDN_C
    eye = (_iota((C, C), 0) == _iota((C, C), 1)).astype(F32)
    xs = [-l for l in lmats]
    ts = [eye + x for x in xs]
    for _ in range(5):
        xs = [mm(x, x) for x in xs]
        ts = [t + mm(t, x) for t, x in zip(ts, xs)]
    res = [eye - mm3(eye + l, t) for l, t in zip(lmats, ts)]
    return [t + mm(t, r) for t, r in zip(ts, res)]


def gdn_fwd(qn, kn, v, gcs_x, beta_x, gcs_t, tb, gh):
    T = qn.shape[0]
    nb, ncb, nc, C = T // tb, tb // GDN_C, T // GDN_C, GDN_C
    idx = [(hh, c) for hh in range(gh) for c in range(ncb)]

    def body(q_ref, k_ref, v_ref, g_ref, b_ref, gt_ref, o_ref, st_ref, ti_ref, s_scr):
        @pl.when(pl.program_id(1) == 0)
        def _():
            s_scr[...] = jnp.zeros_like(s_scr)

        grows = [gt_ref[hh] for hh in range(gh)]
        at = lambda hh, c: (slice(C * c, C * (c + 1)), slice(128 * hh, 128 * hh + 128))
        st1 = []
        for hh, c in idx:
            sl, ln = at(hh, c)
            st1.append(_gdn_stage1(q_ref[sl, ln], k_ref[sl, ln], v_ref[sl, ln], b_ref[sl, ln], g_ref[sl, ln],
                                   grows[hh][:, sl]))
        tinvs = _gdn_inverse([s["lmat"] for s in st1])
        us = [mm(t, s["vb"]) for t, s in zip(tinvs, st1)]
        ws = [mm(t, s["kg"]) for t, s in zip(tinvs, st1)]
        kds = [s["k"] * s["kdec"] for s in st1]
        ms = [mm_tn(kd, w) for kd, w in zip(kds, ws)]
        bs = [mm_tn(kd, u) for kd, u in zip(kds, us)]
        gs = [s["qd"] - mm(s["pmat"], w) for s, w in zip(st1, ws)]
        pus = [mm(s["pmat"], u) for s, u in zip(st1, us)]
        ss = [s_scr[hh] for hh in range(gh)]
        for c in range(ncb):
            for hh in range(gh):
                n, (sl, ln) = hh * ncb + c, at(hh, c)
                ti_ref[hh, sl, :] = tinvs[n]
                st_ref[hh, c] = ss[hh]
                o_ref[sl, ln] = mm(gs[n], ss[hh]) + pus[n]
                ss[hh] = st1[n]["cd"] * ss[hh] - mm(ms[n], ss[hh]) + bs[n]
        for hh in range(gh):
            s_scr[hh] = ss[hh]

    blk = pl.BlockSpec((tb, 128 * gh), lambda h, i: (i, h))
    return pl.pallas_call(
        body, name="gdn_fwd", grid=(GDN_H // gh, nb),
        in_specs=[blk] * 5 + [pl.BlockSpec((gh, 1, tb), lambda h, i: (h, 0, i))],
        out_specs=[blk, pl.BlockSpec((gh, ncb, 128, 128), lambda h, i: (h, i, 0, 0)),
                   pl.BlockSpec((gh, tb, C), lambda h, i: (h, i, 0))],
        out_shape=[jax.ShapeDtypeStruct((T, D), F32), jax.ShapeDtypeStruct((GDN_H, nc, 128, 128), F32),
                   jax.ShapeDtypeStruct((GDN_H, T, C), F32)],
        scratch_shapes=[pltpu.VMEM((gh, 128, 128), F32)],
        compiler_params=_cparams(("parallel", "arbitrary")),
    )(qn, kn, v, gcs_x, beta_x, gcs_t)


def gdn_bwd(qn, kn, v, gb, gcs_t, do, states, tinv, tb, gh):
    T = qn.shape[0]
    nb, ncb, C = T // tb, tb // GDN_C, GDN_C
    assert gh == GDN_H

    def body(q_ref, k_ref, v_ref, gb_ref, gt_ref, do_ref, st_ref, ti_ref,
             dq_ref, dk_ref, dv_ref, dgb_ref, dgr_ref, ds_scr):
        @pl.when(pl.program_id(1) == 0)
        def _():
            ds_scr[...] = jnp.zeros_like(ds_scr)

        grows = [gt_ref[hh] for hh in range(gh)]
        at = lambda hh, c: (slice(C * c, C * (c + 1)), slice(128 * hh, 128 * hh + 128))
        lastrow = _iota((C, 1), 0) == C - 1
        lane = _iota((C, 128), 1)
        idx = [(hh, c) for hh in range(gh) for c in range(ncb)]
        P = []
        for hh, c in idx:
            sl, ln = at(hh, c)
            lc = _gdn_stage1(q_ref[sl, ln], k_ref[sl, ln], v_ref[sl, ln], *_gate_cols(gb_ref[sl, :], hh), grows[hh][:, sl])
            lc.update(tinv=ti_ref[hh, sl, :], s=st_ref[hh, c], do=do_ref[sl, ln], kd=lc["k"] * lc["kdec"])
            P.append(lc)
        for l, u, w in zip(P, [mm(l["tinv"], l["vb"]) for l in P], [mm(l["tinv"], l["kg"]) for l in P]):
            l.update(u=u, w=w)
        for l, x in zip(P, [mm(l["w"], l["s"]) for l in P]):
            l["vn"] = l["u"] - x
        for l, a, b, c_, d in zip(P, [mm_nt(l["do"], l["s"]) for l in P], [mm_nt(l["do"], l["vn"]) for l in P],
                                  [mm_tn(l["qd"], l["do"]) for l in P], [mm_tn(l["pmat"], l["do"]) for l in P]):
            l.update(dqd=a, dp=jnp.where(l["incl"], b, 0.0), ds_q=c_, dvn_p=d)
        pre = dict(zip(idx, P))
        rows = {}
        hs = range(gh)
        ds = [ds_scr[hh] for hh in hs]
        for c in reversed(range(ncb)):
            L = [pre[hh, c] for hh in hs]
            dvn = [l["dvn_p"] + mm(l["kd"], d) for l, d in zip(L, ds)]
            dkd = [mm_nt(l["vn"], d) for l, d in zip(L, ds)]
            dcd = [_sum_all(l["s"] * d) for l, d in zip(L, ds)]
            ds = [l["ds_q"] + l["cd"] * d - mm_tn(l["w"], x) for l, d, x in zip(L, ds, dvn)]
            dw = [-mm_nt(x, l["s"]) for l, x in zip(L, dvn)]
            dvb = [mm_tn(l["tinv"], x) for l, x in zip(L, dvn)]
            dkg = [mm_tn(l["tinv"], x) for l, x in zip(L, dw)]
            da = [-jnp.where(l["strict"], mm_nt(a, l["u"]) + mm_nt(b, l["w"]), 0.0) for l, a, b in zip(L, dvb, dkg)]
            dm = [a * l["dmat"] for l, a in zip(L, da)]
            dn = [l["dp"] * l["dmat"] for l in L]
            dkb = [mm(a, l["k"]) for l, a in zip(L, dm)]
            dq = [mm(a, l["k"]) + l["gam"] * l["dqd"] for l, a in zip(L, dn)]
            dk = [mm_tn(a, l["kb"]) + mm_tn(b, l["q"]) for l, a, b in zip(L, dm, dn)]
            dgb = jnp.zeros((C, 128), F32)
            for hh in hs:
                sl, ln = at(hh, c)
                l = L[hh]
                e = da[hh] * l["lmat"] + l["dp"] * l["pmat"]
                t_kd = _lanes(dkd[hh] * l["kd"])
                dgl = _sum_all(t_kd) + dcd[hh] * l["cd"][:, :1]
                dgcs = (_lanes(e) + _lanes(l["dqd"] * l["qd"]) - t_kd + _lanes(dkg[hh] * l["kg"])
                        + jnp.where(lastrow, dgl, 0.0))
                rows[hh, c] = _rows(e)
                dq_ref[sl, ln] = dq[hh]
                dk_ref[sl, ln] = (dk[hh] + l["kdec"] * dkd[hh] + l["bb"] * l["gam"] * dkg[hh] + l["bb"] * dkb[hh])
                dv_ref[sl, ln] = l["bb"] * dvb[hh]
                dbeta = _lanes(dkg[hh] * l["gam"] * l["k"]) + _lanes(dvb[hh] * l["v"]) + _lanes(dkb[hh] * l["k"])
                dgb = dgb + jnp.where(lane == hh, dbeta, 0.0) + jnp.where(lane == 8 + hh, dgcs, 0.0)
            dgb_ref[slice(C * c, C * (c + 1)), :] = dgb
        for hh in hs:
            ds_scr[hh] = ds[hh]
            dgr_ref[hh] = jnp.concatenate([rows[hh, c] for c in range(ncb)], axis=1)

    blk = pl.BlockSpec((tb, 128 * gh), lambda h, i: (nb - 1 - i, h))
    rowspec = pl.BlockSpec((gh, 1, tb), lambda h, i: (h, 0, nb - 1 - i))
    cblk = pl.BlockSpec((tb, 128), lambda h, i: (nb - 1 - i, 0))
    return pl.pallas_call(
        body, name="gdn_bwd", grid=(GDN_H // gh, nb),
        in_specs=[blk] * 3 + [cblk, rowspec, blk,
                              pl.BlockSpec((gh, ncb, 128, 128), lambda h, i: (h, nb - 1 - i, 0, 0)),
                              pl.BlockSpec((gh, tb, C), lambda h, i: (h, nb - 1 - i, 0))],
        out_specs=[blk] * 3 + [cblk, rowspec],
        out_shape=[jax.ShapeDtypeStruct((T, D), F32)] * 3 + [jax.ShapeDtypeStruct((T, 128), F32),
                                                             jax.ShapeDtypeStruct((GDN_H, 1, T), F32)],
        scratch_shapes=[pltpu.VMEM((gh, 128, 128), F32)],
        compiler_params=_cparams(("parallel", "arbitrary")),
    )(qn, kn, v, gb, gcs_t, do, states, tinv)


def _ssd_pair(x2, dt2, acs2):
    last = acs2[SSM_L - 1:SSM_L, :]
    return jnp.exp(acs2), jnp.exp(last - acs2), x2 * dt2


def _ssd_head(hh, acs2, arow, dec2, cbm, bm, incl, col):
    lmask = (col >= 64 * hh) & (col < 64 * hh + 64)
    sg = jnp.where(incl, jnp.exp(jnp.minimum(acs2[:, 64 * hh:64 * hh + 1] - arow, 0.0)), 0.0)
    dec_col = dec2[:, 64 * hh:64 * hh + 1]
    return lmask, sg, sg * cbm, dec_col, bm * dec_col


def ssd_fwd(xs, bc, dt_x, acs_x, acs_t):
    T = xs.shape[0]
    nc, L = T // SSM_L, SSM_L

    def body(x_ref, bc_ref, dt_ref, ac_ref, at_ref, y_ref, hst_ref, h_scr):
        @pl.when(pl.program_id(0) == 0)
        def _():
            h_scr[...] = jnp.zeros_like(h_scr)

        row, col = _iota((L, L), 0), _iota((L, L), 1)
        incl = row >= col
        P, H = [], []
        for gp in range(8):
            g = gp // 4
            bm, cm = bc_ref[:, 128 * g:128 * g + 128], bc_ref[:, 256 + 128 * g:384 + 128 * g]
            cbm = mm_nt(cm, bm) if gp % 4 == 0 else cbm
            sl = slice(128 * gp, 128 * gp + 128)
            acs2 = ac_ref[:, sl]
            lam2, dec2, xd2 = _ssd_pair(x_ref[:, sl], dt_ref[:, sl], acs2)
            P.append(dict(sl=sl, lam2=lam2, xd2=xd2, hprev=h_scr[gp], cm=cm))
            for hh in range(2):
                lmask, _, mmat, _, bd = _ssd_head(hh, acs2, at_ref[2 * gp + hh], dec2, cbm, bm, incl, col)
                H.append(dict(mmat=mmat, bd=bd, xdh=jnp.where(lmask, xd2, 0.0), xd2=xd2))
        ys = [mm(h["mmat"], h["xdh"]) for h in H]
        sts = [mm_tn(h["xd2"], h["bd"]) for h in H]
        zs = [mm_nt(p["cm"], p["hprev"]) for p in P]
        for gp, p in enumerate(P):
            hst_ref[gp // 4, gp % 4] = p["hprev"]
            y_ref[:, p["sl"]] = ys[2 * gp] + ys[2 * gp + 1] + p["lam2"] * zs[gp]
            lam_rows = jnp.where(row < 64, p["lam2"][L - 1:L, 0:1], p["lam2"][L - 1:L, 64:65])
            h_scr[gp] = lam_rows * p["hprev"] + jnp.where(row < 64, sts[2 * gp], sts[2 * gp + 1])

    blk = pl.BlockSpec((L, D), lambda c: (c, 0))
    return pl.pallas_call(
        body, name="ssd_fwd", grid=(nc,),
        in_specs=[blk, pl.BlockSpec((L, 512), lambda c: (c, 0)), blk, blk, pl.BlockSpec((SSM_H, 1, L), lambda c: (0, 0, c))],
        out_specs=[blk, pl.BlockSpec((2, None, 4, 128, 128), lambda c: (0, c, 0, 0, 0))],
        out_shape=[jax.ShapeDtypeStruct((T, D), F32), jax.ShapeDtypeStruct((2, nc, 4, 128, 128), F32)],
        scratch_shapes=[pltpu.VMEM((8, 128, 128), F32)],
        compiler_params=_cparams(("arbitrary",)),
    )(xs, bc, dt_x, acs_x, acs_t)


def ssd_bwd(xs, bc, dt_x, acs_x, acs_t, dy, hstates):
    T = xs.shape[0]
    nc, L = T // SSM_L, SSM_L

    def body(x_ref, bc_ref, dt_ref, ac_ref, at_ref, dy_ref, hst_ref,
             dx_ref, db_ref, dc_ref, dgate_ref, dar_ref, dh_scr):
        @pl.when(pl.program_id(0) == 0)
        def _():
            dh_scr[...] = jnp.zeros_like(dh_scr)

        row, col = _iota((L, L), 0), _iota((L, L), 1)
        rowc = _iota((L, 1), 0)
        incl = row >= col
        G = [dict(bm=bc_ref[:, 128 * g:128 * g + 128], cm=bc_ref[:, 256 + 128 * g:384 + 128 * g]) for g in range(2)]
        for gr in G:
            gr["cbm"] = mm_nt(gr["cm"], gr["bm"])
        P = []
        for gp in range(8):
            sl = slice(128 * gp, 128 * gp + 128)
            x2, dt2, dy2, acs2 = x_ref[:, sl], dt_ref[:, sl], dy_ref[:, sl], ac_ref[:, sl]
            lam2, dec2, xd2 = _ssd_pair(x2, dt2, acs2)
            P.append(dict(sl=sl, gr=G[gp // 4], x2=x2, dt2=dt2, dy2=dy2, acs2=acs2, lam2=lam2, dec2=dec2, xd2=xd2,
                          hprev=hst_ref[gp // 4, gp % 4], dhn=dh_scr[gp], dz=lam2 * dy2))
        zs = [mm_nt(p["gr"]["cm"], p["hprev"]) for p in P]
        dcm_t = [mm(p["dz"], p["hprev"]) for p in P]
        dh_z = [mm_tn(p["dz"], p["gr"]["cm"]) for p in P]
        H = []
        for gp, p in enumerate(P):
            p["yoff"] = p["dz"] * zs[gp]
            p["q_rows"] = _lanes(p["dhn"] * p["hprev"])
            for hh in range(2):
                lmask, sg, mmat, dec_col, bd = _ssd_head(hh, p["acs2"], at_ref[2 * gp + hh], p["dec2"], p["gr"]["cbm"],
                                                         p["gr"]["bm"], incl, col)
                H.append(dict(p=p, hh=hh, j=2 * gp + hh, lmask=lmask, sg=sg, mmat=mmat, dec_col=dec_col, bd=bd))
        dms = [mm_nt(jnp.where(h["lmask"], h["p"]["dy2"], 0.0), h["p"]["xd2"]) for h in H]
        a1s = [mm_tn(h["mmat"], h["p"]["dy2"]) for h in H]
        a2s = [mm_nt(h["bd"], h["p"]["dhn"]) for h in H]
        dbds = [mm(jnp.where(h["lmask"], h["p"]["xd2"], 0.0), h["p"]["dhn"]) for h in H]
        for gr in G:
            gr.update(dcb=jnp.zeros((L, L), F32), dbm=jnp.zeros((L, SSM_N), F32), comp=jnp.zeros((L, 128), F32))
        dxd = [jnp.zeros((L, 128), F32) for _ in P]
        for h, dm_raw, a1, a2, dbd in zip(H, dms, a1s, a2s, dbds):
            p, hh, j = h["p"], h["hh"], h["j"]
            gr, jg = p["gr"], j % 8
            dm = jnp.where(incl, dm_raw, 0.0)
            gr["dcb"] = gr["dcb"] + dm * h["sg"]
            e = dm * h["mmat"]
            dxd_h = jnp.where(h["lmask"], a1 + a2, 0.0)
            dxd[j // 2] = dxd[j // 2] + dxd_h
            gr["dbm"] = gr["dbm"] + h["dec_col"] * dbd
            t = _lanes(dbd * h["bd"])
            lam_h = p["lam2"][L - 1:L, 64 * hh:64 * hh + 1]
            in_head = (rowc >= 64 * hh) & (rowc < 64 * hh + 64)
            add_last = _sum_all(t) + _sum_all(jnp.where(in_head, p["q_rows"], 0.0)) * lam_h
            dacs_col = (_lanes(jnp.where(h["lmask"], p["yoff"], 0.0)) + _lanes(e) - t
                        + jnp.where(rowc == L - 1, add_last, 0.0))
            ddt_col = _lanes(dxd_h * p["x2"])
            dar_ref[j] = _rows(e)
            gr["comp"] = gr["comp"] + jnp.where(col == jg, dacs_col, 0.0) + jnp.where(col == 8 + jg, ddt_col, 0.0)
        for gp, p in enumerate(P):
            lam_rows = jnp.where(row < 64, p["lam2"][L - 1:L, 0:1], p["lam2"][L - 1:L, 64:65])
            dh_scr[gp] = dh_z[gp] + lam_rows * p["dhn"]
            dx_ref[:, p["sl"]] = p["dt2"] * dxd[gp]
        for g, gr in enumerate(G):
            lanes = slice(128 * g, 128 * g + 128)
            dcm = (dcm_t[4 * g] + dcm_t[4 * g + 1]) + (dcm_t[4 * g + 2] + dcm_t[4 * g + 3])
            db_ref[:, lanes] = gr["dbm"] + mm_tn(gr["dcb"], gr["cm"])
            dc_ref[:, lanes] = dcm + mm(gr["dcb"], gr["bm"])
            dgate_ref[:, lanes] = gr["comp"]

    rv = lambda c: (nc - 1 - c, 0)
    blk, blk256 = pl.BlockSpec((L, D), rv), pl.BlockSpec((L, 256), rv)
    rowspec = pl.BlockSpec((SSM_H, 1, L), lambda c: (0, 0, nc - 1 - c))
    return pl.pallas_call(
        body, name="ssd_bwd", grid=(nc,),
        in_specs=[blk, pl.BlockSpec((L, 512), rv), blk, blk, rowspec, blk,
                  pl.BlockSpec((2, None, 4, 128, 128), lambda c: (0, nc - 1 - c, 0, 0, 0))],
        out_specs=[blk, blk256, blk256, blk256, rowspec],
        out_shape=[jax.ShapeDtypeStruct((T, D), F32), jax.ShapeDtypeStruct((T, 256), F32),
                   jax.ShapeDtypeStruct((T, 256), F32), jax.ShapeDtypeStruct((T, 256), F32),
                   jax.ShapeDtypeStruct((SSM_H, 1, T), F32)],
        scratch_shapes=[pltpu.VMEM((8, 128, 128), F32)],
        compiler_params=_cparams(("arbitrary",)),
    )(xs, bc, dt_x, acs_x, acs_t, dy, hstates)


def _pos():
    return lax.axis_index("x"), lax.axis_index("y"), lax.axis_index("c")


def _other_chips(x, y):
    return [(1 - x, y), (x, 1 - y), (1 - x, 1 - y)]


def _rcopy(src, dst, ssem, rsem, dev):
    return pltpu.make_async_remote_copy(src_ref=src, dst_ref=dst, send_sem=ssem, recv_sem=rsem,
                                        device_id=dev, device_id_type=MESH)


def _rows_at(start, n):
    return pl.ds(pl.multiple_of(start, 8), n)


def _comm_call(body, name, out_shape, n_in, scratch):
    return pl.pallas_call(
        body, name=name, out_shape=out_shape, in_specs=[ANY] * n_in,
        out_specs=[ANY] * len(out_shape) if isinstance(out_shape, (list, tuple)) else ANY,
        scratch_shapes=scratch,
        compiler_params=pltpu.CompilerParams(has_side_effects=True),
    )


def _dma_sems(n):
    return pltpu.SemaphoreType.DMA((n,))


def ag_chips(name, shard):
    rr, cc = shard.shape
    h, nq = rr // 2, ICI_CHUNKS
    hq = h // nq

    def body(x_ref, out_ref, ssem, rsem):
        x, y, c = _pos()
        me_s = 2 * x + y
        chips = _other_chips(x, y)
        started = []
        for q in range(nq):
            rows = _rows_at(c * h + q * hq, hq)
            for j, (cx, cy) in enumerate(chips):
                cp = _rcopy(x_ref.at[rows], out_ref.at[me_s, rows], ssem.at[j * nq + q], rsem.at[j * nq + q], (cx, cy, c))
                cp.start()
                started.append(cp)
        for q in range(nq):
            rows = _rows_at(c * h + q * hq, hq)
            for j, (cx, cy) in enumerate(chips):
                blk = out_ref.at[2 * cx + cy, rows]
                _rcopy(blk, blk, ssem.at[j * nq + q], rsem.at[j * nq + q], (cx, cy, c)).wait_recv()
                k = 3 * nq + j * nq + q
                cp = _rcopy(blk, blk, ssem.at[k], rsem.at[k], (x, y, 1 - c))
                cp.start()
                started.append(cp)
        for q in range(nq):
            rows = _rows_at((1 - c) * h + q * hq, hq)
            for j, (cx, cy) in enumerate(chips):
                blk = out_ref.at[2 * cx + cy, rows]
                k = 3 * nq + j * nq + q
                _rcopy(blk, blk, ssem.at[k], rsem.at[k], (x, y, 1 - c)).wait_recv()
        for cp in started:
            cp.wait_send()

    return _comm_call(body, name, jax.ShapeDtypeStruct((4, rr, cc), shard.dtype), 1,
                      [_dma_sems(6 * nq), _dma_sems(6 * nq)])(shard)


def _with_own(shard, got, s_me):
    return lax.dynamic_update_index_in_dim(got, shard, s_me, 0)


def all_gather_chips(name, shard, s_me):
    return _with_own(shard, ag_chips(name, shard), s_me)


HBM_SPEC = pl.BlockSpec(memory_space=pltpu.HBM)
SEM_SPEC = pl.BlockSpec(memory_space=pltpu.SEMAPHORE)
SPLIT_EFFECT = pltpu.SideEffectType.DATAFLOW_SIDE_EFFECTING


def _split_copies(pieces, x_ref, land_ref, sems, arriving):
    x, y, c = _pos()
    return [_rcopy(s, d_in if arriving else d_out, sems[j], sems[3 + j], dev)
            for j, (s, d_out, d_in, dev) in enumerate(pieces(x_ref, land_ref, x, y, c))]


def split_copy_start(name, src, land_shape, pieces, after):
    def body(x_ref, land_ref, after_ref, *outs):
        for cp in _split_copies(pieces, x_ref, land_ref, outs[:6], False):
            cp.start()
        outs[8][...] = jnp.zeros_like(outs[8])

    dma = pltpu.SemaphoreType.DMA(())
    res = pl.pallas_call(
        body, name=name,
        out_shape=(dma,) * 6 + (pltpu.HBM(src.shape, src.dtype), pltpu.HBM(land_shape, src.dtype),
                                jax.ShapeDtypeStruct((8, 128), F32)),
        in_specs=(HBM_SPEC, HBM_SPEC, ANY),
        out_specs=(SEM_SPEC,) * 6 + (HBM_SPEC, HBM_SPEC, pl.BlockSpec(memory_space=pltpu.VMEM)),
        input_output_aliases={0: 6, 1: 7},
        compiler_params=pltpu.CompilerParams(has_side_effects=SPLIT_EFFECT),
    )(pltpu.with_memory_space_constraint(src, pltpu.HBM),
      pltpu.with_memory_space_constraint(lax.empty(land_shape, src.dtype), pltpu.HBM), after)
    return res[:6], res[6], res[7], res[8]


def split_copy_wait(name, sems, src_thru, land_thru, after, pieces):
    def body(x_ref, land_ref, *rest):
        for cp in _split_copies(pieces, x_ref, land_ref, rest[:6], False):
            cp.wait_send()
        for cp in _split_copies(pieces, x_ref, land_ref, rest[:6], True):
            cp.wait_recv()

    return pl.pallas_call(
        body, name=name,
        out_shape=(pltpu.HBM(src_thru.shape, src_thru.dtype), pltpu.HBM(land_thru.shape, land_thru.dtype)),
        in_specs=(HBM_SPEC, HBM_SPEC) + (SEM_SPEC,) * 6 + (ANY,), out_specs=(HBM_SPEC, HBM_SPEC),
        input_output_aliases={0: 0, 1: 1},
        compiler_params=pltpu.CompilerParams(has_side_effects=SPLIT_EFFECT),
    )(src_thru, land_thru, *sems, after)


def ag_pieces(h):
    def pieces(x_ref, land_ref, x, y, c):
        rows = _rows_at(c * h, h)
        return [(x_ref.at[rows], land_ref.at[2 * x + y, rows], land_ref.at[2 * cx + cy, rows], (cx, cy, c))
                for cx, cy in _other_chips(x, y)]
    return pieces


def rs_pieces(x_ref, land_ref, x, y, c):
    return [(x_ref.at[2 * cx + cy], land_ref.at[j], land_ref.at[j], (cx, cy, c))
            for j, (cx, cy) in enumerate(_other_chips(x, y))]


def ag_forward(name, got):
    _, rr, cc = got.shape
    h, nq = rr // 2, D2D_CHUNKS
    hq = h // nq

    def body(g_ref, out_ref, ssem, rsem):
        x, y, c = _pos()
        slots = [2 * cx + cy for cx, cy in _other_chips(x, y)]
        cps = []
        for j, s in enumerate(slots):
            for q in range(nq):
                blk = out_ref.at[s, _rows_at(c * h + q * hq, hq)]
                cp = _rcopy(blk, blk, ssem.at[j * nq + q], rsem.at[j * nq + q], (x, y, 1 - c))
                cp.start()
                cps.append(cp)
        for cp in cps:
            cp.wait_send()
        for j, s in enumerate(slots):
            for q in range(nq):
                blk = out_ref.at[s, _rows_at((1 - c) * h + q * hq, hq)]
                _rcopy(blk, blk, ssem.at[j * nq + q], rsem.at[j * nq + q], (x, y, 1 - c)).wait_recv()

    return pl.pallas_call(
        body, name=name, out_shape=jax.ShapeDtypeStruct(got.shape, got.dtype), in_specs=[ANY], out_specs=ANY,
        scratch_shapes=[_dma_sems(3 * nq), _dma_sems(3 * nq)], input_output_aliases={0: 0},
        compiler_params=pltpu.CompilerParams(has_side_effects=True),
    )(got)


def rs_pair(name, g):
    _, rr, cc = g.shape
    h, nq = rr // 2, D2D_CHUNKS
    hq = h // nq

    def body(g_ref, recv_ref, ssem, rsem):
        x, y, c = _pos()
        cps = []
        for q in range(nq):
            cp = _rcopy(g_ref.at[:, _rows_at((1 - c) * h + q * hq, hq), :], recv_ref.at[:, pl.ds(q * hq, hq), :],
                        ssem.at[q], rsem.at[q], (x, y, 1 - c))
            cp.start()
            cps.append(cp)
        for cp in cps:
            cp.wait()

    return _comm_call(body, name, jax.ShapeDtypeStruct((4, h, cc), g.dtype), 1, [_dma_sems(nq), _dma_sems(nq)])(g)


def rs_chips(name, p):
    _, h, cc = p.shape
    nq = ICI_CHUNKS
    hq = h // nq

    def body(p_ref, buf_ref, ssem, rsem):
        x, y, c = _pos()
        sends = []
        for q in range(nq):
            rows = pl.ds(q * hq, hq)
            for j, (cx, cy) in enumerate(_other_chips(x, y)):
                cp = _rcopy(p_ref.at[2 * cx + cy, rows], buf_ref.at[j, rows], ssem.at[j * nq + q],
                            rsem.at[j * nq + q], (cx, cy, c))
                cp.start()
                sends.append(cp)
        for cp in sends:
            cp.wait()

    return _comm_call(body, name, jax.ShapeDtypeStruct((3, h, cc), p.dtype), 1,
                      [_dma_sems(3 * nq), _dma_sems(3 * nq)])(p)


def rs_join(name, half):
    h, cc = half.shape
    nq = D2D_CHUNKS
    hq = h // nq

    def body(h_ref, out_ref, ssem, rsem):
        x, y, c = _pos()
        cps = []
        for q in range(nq):
            rows = pl.ds(q * hq, hq)
            cp = _rcopy(h_ref.at[rows], out_ref.at[rows], ssem.at[q], rsem.at[q], (x, y, 1 - c))
            cp.start()
            cps.append(cp)
        for cp in cps:
            cp.wait()

    return _comm_call(body, name, jax.ShapeDtypeStruct((h, cc), half.dtype), 1, [_dma_sems(nq), _dma_sems(nq)])(half)


def reduce_scatter(tag, g, tb, sp):
    return rs_end(rs_begin(tag, g, tb, sp, False), None)


def rs_begin(tag, g, tb, sp, split, after=None):
    _, rr, cc = g.shape
    h = rr // 2
    nbh = h // tb
    recv = rs_pair(tag + "_pair", g)
    mine_rows = lambda i, s: (i // nbh) * (2 * nbh) + s[0] * nbh + i % nbh
    part = rowwise(add2_fn, tag + "_add", 4 * h, tb, [R(g.reshape(4 * rr, cc), off=mine_rows), R(recv.reshape(4 * h, cc))],
                   [], [(cc, BF16)], sp=sp)[0].reshape(4, h, cc)
    st = dict(tag=tag, tb=tb, sp=sp, split=split, part=part)
    if split:
        st["sems"], st["part"], st["land"], st["token"] = split_copy_start(tag + "_start", part, (3, h, cc), rs_pieces,
                                                                           sp if after is None else after)
    return st


def rs_end(st, after):
    tag, tb, sp, part = st["tag"], st["tb"], st["sp"], st["part"]
    _, h, cc = part.shape
    nbh = h // tb
    if st["split"]:
        part, buf = split_copy_wait(tag + "_wait", st["sems"], part, st["land"], after, rs_pieces)
    else:
        buf = rs_chips(tag + "_chips", part)
    red = rowwise(sum4_fn, tag + "_sum", h, tb,
                  [R(part.reshape(4 * h, cc), off=lambda i, s: s[1] * nbh + i)]
                  + [R(buf.reshape(3 * h, cc), off=k * nbh) for k in range(3)],
                  [], [(cc, F32)], sp=sp)[0]
    return red, rs_join(tag + "_join", red)


def adam_halves(name, w, m, v, red, other, tb, blk0, sp):
    nbh = red.shape[0] // tb

    def fn(i, n, s, w_, m_, v_, r_, o_):
        g = jnp.where((blk0 + i) // nbh == s[0], r_, o_)
        return (g,) + _adamw(w_, g, m_, v_)

    half_rows = lambda i, s: (blk0 + i) % nbh
    return rowwise(fn, name, w.shape[0], tb, [R(w), R(m), R(v), R(red, off=half_rows), R(other, off=half_rows)],
                   [], [(w.shape[1], F32)] * 4, sp=sp)


SMALL_LANES = D


def all_reduce_items(name, items, after=None):
    flat = [a for it in items for a in it]
    shapes = [(sum(a.shape[0] for a in it), it[0].shape[1]) for it in items]
    extra = [] if after is None else [after]
    plan, a_idx, brow = [], 0, 0
    for o_idx, it in enumerate(items):
        orow = 0
        for a in it:
            ra, n = a.shape
            for lane0 in range(0, n, SMALL_LANES):
                plan.append((a_idx, o_idx, orow, ra, lane0, min(SMALL_LANES, n - lane0), brow))
                brow += ra
            orow += ra
            a_idx += 1
    nrows = -(-brow // 8) * 8

    def body(*refs):
        ins, refs = refs[:len(flat)], refs[len(flat) + len(extra):]
        outs = refs[:len(items)]
        mine, buf, ssem, rsem = refs[len(items):]
        x, y, c = _pos()
        me = 4 * x + 2 * y + c
        mine[...] = jnp.zeros_like(mine)
        for ai, _, _, ra, lane0, w, br in plan:
            mine[br:br + ra, 0:w] = ins[ai][:, lane0:lane0 + w]
        buf[me] = mine[...]
        cps = []
        for k in range(1, 8):
            dev = (x ^ (k >> 2), y ^ ((k >> 1) & 1), c ^ (k & 1))
            cp = _rcopy(mine, buf.at[me], ssem.at[k - 1], rsem.at[k - 1], dev)
            cp.start()
            cps.append(cp)
        for cp in cps:
            cp.wait()
        for _, oi, orow, ra, lane0, w, br in plan:
            acc = buf[0, br:br + ra, 0:w]
            for d in range(1, 8):
                acc = acc + buf[d, br:br + ra, 0:w]
            outs[oi][orow:orow + ra, lane0:lane0 + w] = acc

    vm = pl.BlockSpec(memory_space=pltpu.VMEM)
    return pl.pallas_call(
        body, name=name, out_shape=[jax.ShapeDtypeStruct(s, F32) for s in shapes],
        in_specs=[vm] * len(flat) + [ANY] * len(extra), out_specs=[vm] * len(items),
        scratch_shapes=[pltpu.VMEM((nrows, SMALL_LANES), F32), pltpu.VMEM((8, nrows, SMALL_LANES), F32),
                        _dma_sems(7), _dma_sems(7)],
        compiler_params=pltpu.CompilerParams(has_side_effects=True),
    )(*flat, *extra)


def adam_small(ws, gs, ms, vs):
    n = len(ws)

    def body(*refs):
        for k in range(n):
            w, g, m, v = (refs[j * n + k][...] for j in range(4))
            for j, val in enumerate(_adamw(w, g, m, v)):
                refs[(4 + j) * n + k][...] = val

    vm = pl.BlockSpec(memory_space=pltpu.VMEM)
    res = pl.pallas_call(
        body, name="adam_small", out_shape=[jax.ShapeDtypeStruct(w.shape, F32) for w in ws] * 3,
        in_specs=[vm] * (4 * n), out_specs=[vm] * (3 * n),
    )(*ws, *gs, *ms, *vs)
    return res[:n], res[n:2 * n], res[2 * n:]


def _sel(rows, cols, pairs):
    m = np.zeros((rows, cols), np.float32)
    for r, c in pairs:
        m[r, c] = 1.0
    return jnp.asarray(m)


def _pad_win(w):
    z = jnp.zeros((w.shape[0], 112), w.dtype)
    return jnp.concatenate([w[:, :4096], w[:, 4112:6672], w[:, 4096:4112], z, w[:, 6672:6688], z], axis=1)


def _unpad_win(wp):
    return jnp.concatenate([wp[:, :4096], wp[:, 6656:6672], wp[:, 4096:6656], wp[:, 6784:6800]], axis=1)


def kernel(x, mem, norm1_w, w_in, gdn_conv_w, gdn_a_log, gdn_dt_bias, gdn_norm_w, ssm_conv_w, ssm_conv_b, ssm_a_log, ssm_dt_bias, ssm_d, ssm_norm_w, w_out, norm2_w, mem_norm_w, wq_mem, wk_mem, wv_mem, wo_mem, norm3_w, w_up, w_down, final_norm_w, loss_target, m_norm1_w, m_w_in, m_gdn_conv_w, m_gdn_a_log, m_gdn_dt_bias, m_gdn_norm_w, m_ssm_conv_w, m_ssm_conv_b, m_ssm_a_log, m_ssm_dt_bias, m_ssm_d, m_ssm_norm_w, m_w_out, m_norm2_w, m_mem_norm_w, m_wq_mem, m_wk_mem, m_wv_mem, m_wo_mem, m_norm3_w, m_w_up, m_w_down, m_final_norm_w, v_norm1_w, v_w_in, v_gdn_conv_w, v_gdn_a_log, v_gdn_dt_bias, v_gdn_norm_w, v_ssm_conv_w, v_ssm_conv_b, v_ssm_a_log, v_ssm_dt_bias, v_ssm_d, v_ssm_norm_w, v_w_out, v_norm2_w, v_mem_norm_w, v_wq_mem, v_wk_mem, v_wv_mem, v_wo_mem, v_norm3_w, v_w_up, v_w_down, v_final_norm_w):
    T, M = x.shape[1], mem.shape[1]
    xi, yi, ci = _pos()
    s_me = 2 * xi + yi
    x0, mem0, tgt = x[0], mem[0], loss_target[0]
    tb = min(512, T)
    tbl = min(1024, T)
    tbp = min(512, T)
    row = lambda v: v.reshape(1, -1)

    keep = (ci == 0).astype(F32)
    gcw_z = lax.dynamic_update_slice(jnp.zeros((4, 3 * D), F32), gdn_conv_w * keep, (0, s_me * 768))
    scw_z = lax.dynamic_update_slice(jnp.zeros((4, 1536), F32), ssm_conv_w * keep, (0, s_me * 384))
    gcw, scw = all_reduce_items("ar_convw", [[gcw_z], [scw_z]])
    scw_x, scw_bc = scw[:, :D], scw[:, D:]
    win_shard = w_in.astype(BF16)
    agw_sems, win_thru, win_land, agw_token = split_copy_start("ag_win_start", win_shard, (4,) + win_shard.shape,
                                                               ag_pieces(win_shard.shape[0] // 2), gcw)
    h1 = rowwise(rms_fwd_fn, "rms1", T, tbl, [R(x0)], [row(norm1_w) + agw_token[0:1, 0:1]], [(D, BF16)])[0]
    win_thru, win_land = split_copy_wait("ag_win_wait", agw_sems, win_thru, win_land, h1,
                                         ag_pieces(win_shard.shape[0] // 2))
    win_g = _with_own(win_thru, ag_forward("ag_win_fwd", win_land), s_me)
    w_in_p = _pad_win(win_g.transpose(1, 0, 2).reshape(D, IN_COLS))
    rest_shard = jnp.concatenate([w_up, w_down, w_out, wq_mem, wk_mem, wv_mem, wo_mem], axis=0).astype(BF16)
    ag_sems, rest_thru, rest_land, ag_token = split_copy_start("ag_rest_start", rest_shard, (4,) + rest_shard.shape,
                                                               ag_pieces(rest_shard.shape[0] // 2), win_g)
    sp = jnp.stack([ci, s_me]).astype(jnp.int32)
    scb_x, scb_bc = row(ssm_conv_b[:D]), row(ssm_conv_b[D:])

    galog_c, gdtb_c = row(jnp.pad(gdn_a_log, (8, 112))), row(jnp.pad(gdn_dt_bias, (8, 112)))
    salog_c, sdtb_c = row(jnp.pad(ssm_a_log, (0, 112))), row(jnp.pad(ssm_dt_bias, (0, 112)))
    sd_x = row(jnp.repeat(ssm_d, 64))
    eb = _sel(128, D, [(h, 128 * h + l) for h in range(8) for l in range(128)])
    ea = _sel(128, D, [(8 + h, 128 * h + l) for h in range(8) for l in range(128)])
    e16 = _sel(128, D, [(h, 64 * h + l) for h in range(16) for l in range(64)])

    p = matmul("mm_in", h1, w_in_p, "nn", 2048, 768, 1024, [F32])[0]
    gp_ins = [R(p, 3 * D, CB_QKV, "prev"), R(p, 128, CB_BA)]
    qn, kn, vv, gcs_x, beta_x, ggate, gcs_t = rowwise(gdn_prep_fn, "gdn_prep", T, tbp, gp_ins,
                                                      [gcw, galog_c, gdtb_c, eb, ea],
                                                      [(D, F32)] * 5 + [(128, F32), (-8, F32)])
    gcs_t = gcs_t.reshape(GDN_H, 1, T)
    gtb, ggh = min(128, T), 8
    o_gdn, s_states, tinv = gdn_fwd(qn, kn, vv, gcs_x, beta_x, gcs_t, min(256, T), ggh)
    gnw = row(gdn_norm_w)
    oa = rowwise(gdn_post_fn, "gdn_post", T, tbl, [R(o_gdn), R(p, D, CB_Z)], [gnw], [(D, BF16)])[0]
    sp_ins = [R(p, D, CB_XS, "prev"), R(p, 512, CB_BC, "prev"), R(p, 128, CB_DT)]
    sp_full = [scw_x, scw_bc, scb_x, scb_bc, salog_c, sdtb_c]
    xs, bc, dt_x, acs_x, acs_t = rowwise(ssd_prep_fn, "ssd_prep", T, tbp, sp_ins, sp_full + [e16],
                                         [(D, F32), (512, F32), (D, F32), (D, F32), (-SSM_H, F32)])
    acs_t = acs_t.reshape(SSM_H, 1, T)
    y_ssd, h_states = ssd_fwd(xs, bc, dt_x, acs_x, acs_t)
    snw = row(ssm_norm_w)
    ob = rowwise(ssd_post_fn, "ssd_post", T, tbl, [R(y_ssd), R(xs), R(p, D, CB_ZS)], [sd_x, snw], [(D, BF16)])[0]
    rest_thru, rest_land = split_copy_wait("ag_rest_wait", ag_sems, rest_thru, rest_land, ob,
                                           ag_pieces(rest_shard.shape[0] // 2))
    rest_g = _with_own(rest_thru, ag_forward("ag_rest_fwd", rest_land), s_me)
    assert D == 1024
    view = lambda shape, blk, at: dict(b_sel=(shape, blk, at))
    wup_n = view((D, D_FF), (None, D, D), lambda i, j, k: (j, 0, 0))
    wup_t = view((D, D_FF), (None, D, D), lambda i, j, k: (k, 0, 0))
    wdown_n = view((D_FF, D), (None, D, D), lambda i, j, k: (k, 1, 0))
    wdown_t = view((D_FF, D), (None, D, D), lambda i, j, k: (j, 1, 0))
    wout_a = view((D, D), (2, 512, D), lambda i, j, k: (0, 4, 0))
    wout_b = view((D, D), (2, 512, D), lambda i, j, k: (1, 4, 0))
    wq_v, wk_v, wv_v, wo_v = (view((D, D), (4, 256, D), lambda i, j, k, r=r: (0, r, 0)) for r in (10, 11, 12, 13))
    x1a = matmul("mm_out_a", oa, rest_g, "nn", 1024, 1024, 1024, [F32], _epi_res, [x0], **wout_a)[0]
    x1, h2 = matmul("mm_out_b", ob, rest_g, "nn", 1024, 1024, 1024, [F32, BF16], _epi_res_rms, [x1a],
                    [row(norm2_w)], **wout_b)

    mn = rowwise(rms_fwd_fn, "rms_mem", M, M, [R(mem0)], [row(mem_norm_w)], [(D, BF16)])[0]
    km = matmul("mm_k", mn, rest_g, "nn", 256, 1024, 1024, [BF16], **wk_v)[0]
    vm = matmul("mm_v", mn, rest_g, "nn", 256, 1024, 1024, [BF16], **wv_v)[0]
    qm = matmul("mm_q", h2, rest_g, "nn", 1024, 1024, 1024, [BF16], **wq_v)[0]
    ao = rowwise(attn_fn, "attn", T, tbl, [R(qm)], [km, vm], [(D, BF16)])[0]
    x2, h3 = matmul("mm_o", ao, rest_g, "nn", 1024, 1024, 1024, [F32, BF16], _epi_res_rms, [x1], [row(norm3_w)], **wo_v)
    u, act = matmul("mm_up", h3, rest_g, "nn", 2048, 1024, 1024, [BF16, BF16], _epi_relu2, **wup_n)
    wdown_n2 = view((D_FF, D), (2, D, D), lambda i, j, k: (k, 1, 0))
    x3 = matmul("mm_down", act, rest_g, "nn", 1024, 1024, 2048, [F32], _epi_res, [x2], **wdown_n2)[0]
    dx3, dx3b, loss_lane, g_final = rowwise(final_fn, "final", T, tbl, [R(x3), R(tgt)], [row(final_norm_w)],
                                            [(D, F32), (D, BF16)], [(1, D), (1, D)])
    loss = lax.psum(0.5 / D * jnp.sum(loss_lane), ("x", "y", "c"))

    dup = matmul("mm_dact", dx3b, rest_g, "nt", 2048, 1024, 1024, [BF16], _epi_dup, [u], **wdown_t)[0]
    def g_into(buf, blk, at):
        return dict(into=(buf, blk, lambda i, j, k, at=at: at(i, j)))

    grest = jax.ShapeDtypeStruct((4, 3584, D), F32)
    grest = matmul("mm_gdown", act, dx3b, "tn", 1024, 1024, 4096, [F32],
                   **g_into(grest, (None, 1024, D), lambda i, j: (i, 1, 0)))
    wup_t4 = dict(b_sel=((D, D_FF), (4, D, D), lambda i, j, k: (0, 0, 0), "side by side"))
    dh3 = matmul("mm_dh3", dup, rest_g, "nt", 1024, 1024, 4096, [F32], **wup_t4)[0]
    dx2, dx2b, g_n3 = rowwise(rms_bwd_fn, "rms3_bwd", T, tbl, [R(x2), R(dh3), R(dx3)], [row(norm3_w)],
                              [(D, F32), (D, BF16)], [(1, D)])
    grest = matmul("mm_gup", h3, dup, "tn", 1024, 1024, 4096, [F32],
                   **g_into(grest, (None, 1024, D), lambda i, j: (j, 0, 0)))
    dao = matmul("mm_dao", dx2b, rest_g, "nt", 1024, 1024, 1024, [F32], **wo_v)[0]
    grest = matmul("mm_gwo", ao, dx2b, "tn", 1024, 1024, 2048, [F32],
                   **g_into(grest, (4, 256, D), lambda i, j: (0, 13, 0)))
    dqm, dkm, dvm = rowwise(attn_bwd_fn, "attn_bwd", T, tb, [R(qm), R(dao)], [km, vm], [(D, BF16)],
                            [(M, D), (M, D)])
    dx1, dx1b, g_n2 = matmul("mm_dh2", dqm, rest_g, "nt", 512, 1024, 1024, [F32, BF16], _epi_rms_bwd, [x1, dx2],
                             [row(norm2_w)], n_acc=1, **wq_v)
    grest = matmul("mm_gwq", h2, dqm, "tn", 1024, 1024, 2048, [F32],
                   **g_into(grest, (4, 256, D), lambda i, j: (0, 10, 0)))
    grest = matmul("mm_gwk", mn, dkm, "tn", 1024, 1024, 256, [F32],
                   **g_into(grest, (4, 256, D), lambda i, j: (0, 11, 0)))
    grest = matmul("mm_gwv", mn, dvm, "tn", 1024, 1024, 256, [F32],
                   **g_into(grest, (4, 256, D), lambda i, j: (0, 12, 0)))
    dmn_k = matmul("mm_dmk", dkm, rest_g, "nt", 256, 1024, 1024, [F32], **wk_v)[0]
    dmn = matmul("mm_dmv", dvm, rest_g, "nt", 256, 1024, 1024, [F32], _epi_res, [dmn_k], **wv_v)[0]
    g_nmem = rowwise(rms_bwd_w_fn, "rmsmem_bwd", M, M, [R(mem0), R(dmn)], [row(mem_norm_w)], [], [(1, D)])[0]
    doa = matmul("mm_doa", dx1b, rest_g, "nt", 2048, 1024, 1024, [F32], **wout_a)[0]
    dob = matmul("mm_dob", dx1b, rest_g, "nt", 2048, 1024, 1024, [F32], **wout_b)[0]
    grest = matmul("mm_gwout_a", oa, dx1b, "tn", 1024, 1024, 2048, [F32],
                   **g_into(grest, (2, 512, D), lambda i, j: (0, 4, 0)))
    grest = matmul("mm_gwout_b", ob, dx1b, "tn", 1024, 1024, 2048, [F32],
                   **g_into(grest, (2, 512, D), lambda i, j: (1, 4, 0)))

    rs_rest = rs_begin("rs_rest", grest, 256, sp, True)

    dp = jax.ShapeDtypeStruct((T, p.shape[1]), BF16)
    dy_ssd, dxs_dir, dp, g_snw, g_sd_lane = rowwise(
        ssd_post_bwd_fn, "ssd_post_bwd", T, tb, [R(y_ssd), R(xs), R(p, D, CB_ZS), R(dob)],
        [sd_x + rs_rest["token"][0:1, 0:1], snw],
        [(D, F32), (D, F32), (D, BF16, dp, CB_ZS)], [(1, D), (1, D)])
    dxs_scan, db_s, dc_s, dgate, dacs_t = ssd_bwd(xs, bc, dt_x, acs_x, acs_t, dy_ssd, h_states)
    spb = rowwise(ssd_prep_bwd_fn, "ssd_prep_bwd", T, tbp,
                  sp_ins + [R(dxs_scan), R(dxs_dir), R(db_s), R(dc_s), R(dgate), RC(dacs_t.reshape(SSM_H, T))], sp_full,
                  [(D, F32), (512, F32), (128, BF16, dp, CB_DT)],
                  [(1, D)] * 4 + [(1, 512)] * 4 + [(1, D), (1, 512), (1, 128), (1, 128)])
    dyc_x, dyc_bc, dp = spb[:3]
    dp = rowwise(conv_bwd_fn, "conv_bwd_x", T, tbp, [R(dyc_x, halo="next")], [scw_x], [(D, BF16, dp, CB_XS)])[0]
    dp = rowwise(conv_bwd_fn, "conv_bwd_bc", T, tbp, [R(dyc_bc, halo="next")], [scw_bc], [(512, BF16, dp, CB_BC)])[0]

    do_gdn, dp, g_gnw = rowwise(gdn_post_bwd_fn, "gdn_post_bwd", T, tb, [R(o_gdn), R(p, D, CB_Z), R(doa)], [gnw],
                                [(D, F32), (D, BF16, dp, CB_Z)], [(1, 128)])
    dqn, dkn, dvv, dggate, dgcs_t = gdn_bwd(qn, kn, vv, ggate, gcs_t, do_gdn, s_states, tinv, min(256, T), ggh)
    gpb = rowwise(gdn_prep_bwd_fn, "gdn_prep_bwd", T, tbp,
                  gp_ins + [R(dqn), R(dkn), R(dvv), R(dggate), RC(dgcs_t.reshape(GDN_H, T))],
                  [gcw, galog_c, gdtb_c],
                  [(3 * D, F32), (128, BF16, dp, CB_BA)], [(1, 3 * D)] * 4 + [(1, 128), (1, 128)])
    dyc_qkv, dp = gpb[:2]
    dp = rowwise(conv_bwd_fn, "conv_bwd_qkv", T, tbp, [R(dyc_qkv, halo="next")], [gcw], [(3 * D, BF16, dp, CB_QKV)])[0]
    dh1 = matmul("mm_dh1", dp, w_in_p, "nt", 1024, 1024, 2304, [F32])[0]
    grad_x, g_n1 = rowwise(rms_bwd1_fn, "rms1_bwd", T, tbl, [R(x0), R(dh1), R(dx1)], [row(norm1_w)], [(D, F32)], [(1, D)])
    g_win_p = matmul("mm_gwin", h1, dp, "tn", 1024, 768, 4096, [F32])[0]

    items = [[g_n1], [gpb[6]], [gpb[7]], [g_gnw], [spb[11]], [spb[12]], [spb[13]], [spb[14]], [g_sd_lane], [g_snw],
             [g_n2], [g_nmem], [g_n3], [g_final], list(gpb[2:6]), list(spb[3:7]), list(spb[7:11])]
    (gr_n1, r_galog, r_gdtb, gr_gnw, r_scb_x, r_scb_bc, r_salog, r_sdtb, r_sd, gr_snw, gr_n2, gr_nmem, gr_n3,
     gr_final, r_gcw, r_scw_x, r_scw_bc) = all_reduce_items("ar_grads", items)
    gr_galog, gr_gdtb = r_galog[:, 8:16], r_gdtb[:, 8:16]
    gr_salog, gr_sdtb = r_salog[:, :SSM_H], r_sdtb[:, :SSM_H]
    gr_sd = r_sd.reshape(SSM_H, SSM_P).sum(axis=1).reshape(1, SSM_H)
    gr_scb = jnp.concatenate([r_scb_x, r_scb_bc], axis=1)
    gr_gcw = lax.dynamic_slice(r_gcw, (0, s_me * 768), (4, 768))
    gr_scw = lax.dynamic_slice(jnp.concatenate([r_scw_x, r_scw_bc], axis=1), (0, s_me * 384), (4, 384))

    g_win = _unpad_win(g_win_p).reshape(D, 4, IN_COLS // 4).transpose(1, 0, 2)
    rs_win = rs_begin("rs_win", g_win, 256, sp, True, gr_n1)
    red_r, oth_r = rs_end(rs_rest, rs_win["token"])

    big = {}
    for n, w, m, v, blk0 in (("w_up", w_up, m_w_up, v_w_up, 0), ("w_down", w_down, m_w_down, v_w_down, 4),
                             ("w_out", w_out, m_w_out, v_w_out, 8), ("wq_mem", wq_mem, m_wq_mem, v_wq_mem, 10),
                             ("wk_mem", wk_mem, m_wk_mem, v_wk_mem, 11), ("wv_mem", wv_mem, m_wv_mem, v_wv_mem, 12),
                             ("wo_mem", wo_mem, m_wo_mem, v_wo_mem, 13)):
        big[n] = adam_halves("adam_" + n, w, m, v, red_r, oth_r, 256, blk0, sp)
    red_w, oth_w = rs_end(rs_win, big["wo_mem"][1])
    g_win_t = jnp.where(ci == 0, jnp.concatenate([red_w, oth_w], axis=0), jnp.concatenate([oth_w, red_w], axis=0)).T
    win_rows = g_win_t.shape[0]
    tbw = 152 if win_rows % 152 == 0 else win_rows
    d_t, m_t, v_t = rowwise(lambda i, n, w, g, m, v: _adamw(w, g, m, v), "adam_win", win_rows, tbw,
                            [R(w_in.T), R(g_win_t), R(m_w_in.T), R(v_w_in.T)], [], [(D, F32)] * 3)
    big["w_in"] = (g_win_t.T, d_t.T, m_t.T, v_t.T)
    names_s =["norm1_w", "gdn_conv_w", "gdn_a_log", "gdn_dt_bias", "gdn_norm_w", "ssm_conv_w", "ssm_conv_b",
               "ssm_a_log", "ssm_dt_bias", "ssm_d", "ssm_norm_w", "norm2_w", "mem_norm_w", "norm3_w", "final_norm_w"]
    w_s = [norm1_w, gdn_conv_w, gdn_a_log, gdn_dt_bias, gdn_norm_w, ssm_conv_w, ssm_conv_b, ssm_a_log, ssm_dt_bias,
           ssm_d, ssm_norm_w, norm2_w, mem_norm_w, norm3_w, final_norm_w]
    g_s = [gr_n1, gr_gcw, gr_galog, gr_gdtb, gr_gnw, gr_scw, gr_scb, gr_salog, gr_sdtb, gr_sd, gr_snw, gr_n2,
           gr_nmem, gr_n3, gr_final]
    m_s = [m_norm1_w, m_gdn_conv_w, m_gdn_a_log, m_gdn_dt_bias, m_gdn_norm_w, m_ssm_conv_w, m_ssm_conv_b, m_ssm_a_log,
           m_ssm_dt_bias, m_ssm_d, m_ssm_norm_w, m_norm2_w, m_mem_norm_w, m_norm3_w, m_final_norm_w]
    v_s = [v_norm1_w, v_gdn_conv_w, v_gdn_a_log, v_gdn_dt_bias, v_gdn_norm_w, v_ssm_conv_w, v_ssm_conv_b, v_ssm_a_log,
           v_ssm_dt_bias, v_ssm_d, v_ssm_norm_w, v_norm2_w, v_mem_norm_w, v_norm3_w, v_final_norm_w]
    shp_s = [w.shape for w in w_s]
    as2d = lambda a: a if a.ndim == 2 else a.reshape(1, -1)
    d_l, m_l, v_l = adam_small([as2d(a) for a in w_s], [as2d(a) for a in g_s], [as2d(a) for a in m_s],
                               [as2d(a) for a in v_s])

    grads, deltas, new_m, new_v = {}, {}, {}, {}
    for n, (gg, dd, mm_, vv_) in big.items():
        grads[n], deltas[n], new_m[n], new_v[n] = gg, dd, mm_, vv_
    for k, n in enumerate(names_s):
        grads[n] = g_s[k].reshape(shp_s[k])
        deltas[n], new_m[n], new_v[n] = (a[k].reshape(shp_s[k]) for a in (d_l, m_l, v_l))
    order = ["norm1_w", "w_in", "gdn_conv_w", "gdn_a_log", "gdn_dt_bias", "gdn_norm_w", "ssm_conv_w", "ssm_conv_b",
             "ssm_a_log", "ssm_dt_bias", "ssm_d", "ssm_norm_w", "w_out", "norm2_w", "mem_norm_w", "wq_mem", "wk_mem",
             "wv_mem", "wo_mem", "norm3_w", "w_up", "w_down", "final_norm_w"]
    return (loss, grad_x[None], *[grads[n] for n in order], *[deltas[n] for n in order],
            *[new_m[n] for n in order], *[new_v[n] for n in order])
```
